```python
import math
import jax
import jax.numpy as jnp
from jax import lax
import numpy as np

D_MODEL = 1024
BATCH = 8
SEQ = 4096
DEPTH = 2

N_MIXERS = 2
MEM_LEN = 256
EPS = 1e-6

A_HEADS = 12
A_KV_HEADS = 2
A_HEAD_DIM = 64
WINDOW = 128
BLOCK = 128
N_BUCKETS = 32
MAX_DISTANCE = 128

B_QK_HEADS = 3
B_V_HEADS = 6
B_HEAD_DIM = 128
B_CONV = 4
CHUNK = 64

X_HEADS = 4
X_HEAD_DIM = 64

D_FF = 2816
FFN_CONV = 3

A_Q = A_HEADS * A_HEAD_DIM
A_KV = A_KV_HEADS * A_HEAD_DIM
X_Q = X_HEADS * X_HEAD_DIM
B_QK = B_QK_HEADS * B_HEAD_DIM
B_V = B_V_HEADS * B_HEAD_DIM
B_QKV = 2 * B_QK + B_V
IN_A = A_Q + 2 * A_KV + X_Q
IN_B = B_QKV + B_V + 2 * B_V_HEADS + X_Q
MIX_WIDTH = A_Q + X_Q
N_A_LAYERS = (DEPTH + 1) // 2
N_B_LAYERS = DEPTH // 2

kernel_name = "hybrid_swa_sink_deltanet_memxattn_convglu"


def rms_norm(x, g):
    xf = x.astype(jnp.float32)
    y = xf * lax.rsqrt(jnp.mean(xf * xf, axis=-1, keepdims=True) + EPS)
    return (y * g.astype(jnp.float32)).astype(x.dtype)


def l2_normalize(x):
    xf = x.astype(jnp.float32)
    return xf * lax.rsqrt(jnp.sum(xf * xf, axis=-1, keepdims=True) + EPS)


def causal_depthwise_conv(x, w):
    k = w.shape[0]
    return lax.conv_general_dilated(
        x, w[:, None, :].astype(x.dtype), window_strides=(1,), padding=[(k - 1, 0)],
        dimension_numbers=('NWC', 'WIO', 'NWC'), feature_group_count=x.shape[-1])


def t5_causal_bucket(dist):
    n = jnp.maximum(dist, 0)
    max_exact = N_BUCKETS // 2
    nf = jnp.maximum(n, 1).astype(jnp.float32)
    large = max_exact + (jnp.log(nf / max_exact) / math.log(MAX_DISTANCE / max_exact)
                         * (N_BUCKETS - max_exact)).astype(jnp.int32)
    large = jnp.minimum(large, N_BUCKETS - 1)
    return jnp.where(n < max_exact, n, large)


def swa_sink_attention(q, k, v, sinks, rel_bias):
    b, s, _, dh = q.shape
    nb = s // BLOCK
    grp = A_HEADS // A_KV_HEADS
    qb = q.reshape(b, nb, BLOCK, A_KV_HEADS, grp, dh)

    def band(t):
        tb = t.reshape(b, nb, BLOCK, A_KV_HEADS, dh)
        prev = jnp.concatenate([jnp.zeros_like(tb[:, :1]), tb[:, :-1]], axis=1)
        return jnp.concatenate([prev, tb], axis=2)

    kb, vb = band(k), band(v)
    qi = jnp.arange(BLOCK)[:, None]
    kj = jnp.arange(2 * BLOCK)[None, :]
    dist = BLOCK + qi - kj
    band_ok = (dist >= 0) & (dist < WINDOW)
    blk_ok = (jnp.arange(nb)[:, None, None] > 0) | (kj[None] >= BLOCK)
    mask = band_ok[None] & blk_ok
    bias = rel_bias[t5_causal_bucket(dist)]
    bias = jnp.transpose(bias, (2, 0, 1)).reshape(A_KV_HEADS, grp, BLOCK, 2 * BLOCK)

    scores = jnp.einsum('bnqhgd,bnkhd->bnhgqk', qb, kb,
                        preferred_element_type=jnp.float32) * (dh ** -0.5)
    scores = scores + bias.astype(jnp.float32)
    scores = jnp.where(mask[None, :, None, None], scores, -jnp.inf)
    sink = sinks.astype(jnp.float32).reshape(1, 1, A_KV_HEADS, grp, 1, 1)
    m = jnp.maximum(jnp.max(scores, axis=-1, keepdims=True), sink)
    p = jnp.exp(scores - m)
    probs = p / (jnp.sum(p, axis=-1, keepdims=True) + jnp.exp(sink - m))
    out = jnp.einsum('bnhgqk,bnkhd->bnqhgd', probs.astype(v.dtype), vb)
    return out.reshape(b, s, A_HEADS * dh)


def memory_cross_attention(q, mem_k, mem_v):
    b, s = q.shape[0], q.shape[1]
    scores = jnp.einsum('bshd,bmhd->bhsm', q, mem_k,
                        preferred_element_type=jnp.float32) * (X_HEAD_DIM ** -0.5)
    probs = jax.nn.softmax(scores, axis=-1).astype(mem_v.dtype)
    out = jnp.einsum('bhsm,bmhd->bshd', probs, mem_v)
    return out.reshape(b, s, X_Q)


def gated_delta_rule(q, k, v, g, beta):
    b, s, h, dk = q.shape
    dv = v.shape[-1]
    nc = s // CHUNK
    f32 = jnp.float32

    def chunks(t):
        t = t.astype(f32).reshape((b, nc, CHUNK, h) + t.shape[3:])
        return jnp.moveaxis(t, 3, 1)

    qc = chunks(q) * (dk ** -0.5)
    kc = chunks(k)
    vc = chunks(v)
    bc = chunks(beta)
    gc = jnp.cumsum(chunks(g), axis=-1)
    idx = jnp.arange(CHUNK)
    strict = idx[:, None] > idx[None, :]
    incl = idx[:, None] >= idx[None, :]
    gdiff = gc[..., :, None] - gc[..., None, :]
    decay_incl = jnp.exp(jnp.where(incl, gdiff, -jnp.inf))
    kk = jnp.einsum('bhntd,bhnsd->bhnts', kc, kc)
    a_mat = bc[..., :, None] * kk * jnp.where(strict, decay_incl, 0.0)
    eye = jnp.eye(CHUNK, dtype=f32)
    rhs = jnp.concatenate([bc[..., None] * vc, (bc * jnp.exp(gc))[..., None] * kc], axis=-1)
    sol = lax.linalg.triangular_solve(eye + a_mat, rhs, left_side=True, lower=True,
                                      unit_diagonal=True)
    u, w = sol[..., :dv], sol[..., dv:]
    attn_qk = jnp.einsum('bhntd,bhnsd->bhnts', qc, kc) * decay_incl
    q_decay = qc * jnp.exp(gc)[..., None]
    k_tail = kc * jnp.exp(gc[..., -1:] - gc)[..., None]
    decay_chunk = jnp.exp(gc[..., -1])

    def step(state, xs):
        u_n, w_n, a_n, qd_n, kt_n, dc_n = xs
        delta = u_n - jnp.einsum('bhtk,bhkv->bhtv', w_n, state)
        out = (jnp.einsum('bhtk,bhkv->bhtv', qd_n, state)
               + jnp.einsum('bhts,bhsv->bhtv', a_n, delta))
        state = dc_n[..., None, None] * state + jnp.einsum('bhsk,bhsv->bhkv', kt_n, delta)
        return state, out

    xs = tuple(jnp.moveaxis(t, 2, 0) for t in (u, w, attn_qk, q_decay, k_tail, decay_chunk))
    state0 = jnp.zeros((b, h, dk, dv), f32)
    _, out = lax.scan(step, state0, xs)
    return jnp.transpose(out, (1, 0, 3, 2, 4)).reshape(b, s, h, dv)


def swa_group(proj, sinks, rel_bias):
    b, s, _ = proj.shape
    q, k, v = jnp.split(proj, [A_Q, A_Q + A_KV], axis=-1)
    q = q.reshape(b, s, A_HEADS, A_HEAD_DIM)
    k = k.reshape(b, s, A_KV_HEADS, A_HEAD_DIM)
    v = v.reshape(b, s, A_KV_HEADS, A_HEAD_DIM)
    return swa_sink_attention(q, k, v, sinks, rel_bias)


def deltanet_group(proj, conv_w, a_log, dt_bias, norm_g):
    b, s, _ = proj.shape
    f32 = jnp.float32
    qkv, z, beta_logit, a_logit = jnp.split(
        proj, [B_QKV, B_QKV + B_V, B_QKV + B_V + B_V_HEADS], axis=-1)
    qkv = jax.nn.silu(causal_depthwise_conv(qkv, conv_w))
    q, k, v = jnp.split(qkv, [B_QK, 2 * B_QK], axis=-1)
    rep = B_V_HEADS // B_QK_HEADS
    q = jnp.repeat(l2_normalize(q.reshape(b, s, B_QK_HEADS, B_HEAD_DIM)), rep, axis=2)
    k = jnp.repeat(l2_normalize(k.reshape(b, s, B_QK_HEADS, B_HEAD_DIM)), rep, axis=2)
    v = v.reshape(b, s, B_V_HEADS, B_HEAD_DIM)
    beta = jax.nn.sigmoid(beta_logit.astype(f32))
    g = -jnp.exp(a_log.astype(f32)) * jax.nn.softplus(a_logit.astype(f32) + dt_bias.astype(f32))
    o = gated_delta_rule(q, k, v, g, beta)
    o = o * lax.rsqrt(jnp.mean(o * o, axis=-1, keepdims=True) + EPS) * norm_g.astype(f32)
    o = o * jax.nn.silu(z.reshape(b, s, B_V_HEADS, B_HEAD_DIM).astype(f32))
    return o.reshape(b, s, B_V).astype(proj.dtype)


def conv_glu(hn, w_gate_up, conv_w, conv_b, w_down):
    gate, up = jnp.split(hn @ w_gate_up, [D_FF], axis=-1)
    gate = causal_depthwise_conv(gate, conv_w) + conv_b
    return (jax.nn.silu(gate) * up) @ w_down


def _fwd_setup_inputs(seed: int = 0) -> dict:
    key = jax.random.key(seed)
    ks = jax.random.split(key, 24)
    f32 = jnp.float32

    def dense(k, shape, fan_in):
        return jax.random.normal(k, shape, f32) * (fan_in ** -0.5)

    def gain(k, shape):
        return 1.0 + 0.02 * jax.random.normal(k, shape, f32)

    dt = jnp.exp(jax.random.uniform(ks[13], (N_B_LAYERS, B_V_HEADS), f32,
                                    math.log(1e-3), math.log(1e-1)))
    return {
        'x': jax.random.normal(ks[0], (BATCH, SEQ, D_MODEL), f32),
        'mem': jax.random.normal(ks[1], (BATCH, MEM_LEN, D_MODEL), f32),
        'rel_bias': 0.5 * jax.random.normal(ks[2], (N_BUCKETS, A_HEADS), f32),
        'norm_mix_g': gain(ks[3], (DEPTH, D_MODEL)),
        'norm_mem_g': gain(ks[4], (DEPTH, D_MODEL)),
        'w_mem_kv': dense(ks[5], (DEPTH, D_MODEL, 2 * X_Q), D_MODEL),
        'w_out': dense(ks[6], (DEPTH, MIX_WIDTH, D_MODEL), MIX_WIDTH),
        'w_in_a': dense(ks[7], (N_A_LAYERS, D_MODEL, IN_A), D_MODEL),
        'sinks_a': 0.5 * jax.random.normal(ks[8], (N_A_LAYERS, A_HEADS), f32),
        'w_in_b': dense(ks[9], (N_B_LAYERS, D_MODEL, IN_B), D_MODEL),
        'conv_qkv_b': dense(ks[10], (N_B_LAYERS, B_CONV, B_QKV), B_CONV),
        'a_log_b': jnp.log(jax.random.uniform(ks[11], (N_B_LAYERS, B_V_HEADS), f32, 1.0, 16.0)),
        'dt_bias_b': dt + jnp.log(-jnp.expm1(-dt)),
        'out_norm_g_b': gain(ks[12], (N_B_LAYERS, B_HEAD_DIM)),
        'norm_ffn_g': gain(ks[14], (DEPTH, D_MODEL)),
        'w_gate_up': dense(ks[15], (DEPTH, D_MODEL, 2 * D_FF), D_MODEL),
        'ffn_conv_w': dense(ks[16], (DEPTH, FFN_CONV, D_FF), FFN_CONV),
        'ffn_conv_b': 0.02 * jax.random.normal(ks[17], (DEPTH, D_FF), f32),
        'w_down': dense(ks[18], (DEPTH, D_FF, D_MODEL), D_FF),
        'final_norm_g': gain(ks[19], (D_MODEL,)),
    }


def _fwd_reference(x, mem, rel_bias, norm_mix_g, norm_mem_g, w_mem_kv, w_out, w_in_a, sinks_a,
              w_in_b, conv_qkv_b, a_log_b, dt_bias_b, out_norm_g_b, norm_ffn_g, w_gate_up,
              ffn_conv_w, ffn_conv_b, w_down, final_norm_g):
    b, s, _ = x.shape
    h = x
    for i in range(DEPTH):
        j = i // N_MIXERS
        hn = rms_norm(h, norm_mix_g[i])
        mem_kv = rms_norm(mem, norm_mem_g[i]) @ w_mem_kv[i]
        mem_k = mem_kv[..., :X_Q].reshape(b, MEM_LEN, X_HEADS, X_HEAD_DIM)
        mem_v = mem_kv[..., X_Q:].reshape(b, MEM_LEN, X_HEADS, X_HEAD_DIM)
        if i % N_MIXERS == 0:
            proj = hn @ w_in_a[j]
            self_out = swa_group(proj[..., :IN_A - X_Q], sinks_a[j], rel_bias)
        else:
            proj = hn @ w_in_b[j]
            self_out = deltanet_group(proj[..., :IN_B - X_Q], conv_qkv_b[j], a_log_b[j],
                                      dt_bias_b[j], out_norm_g_b[j])
        xq = proj[..., -X_Q:].reshape(b, s, X_HEADS, X_HEAD_DIM)
        cross_out = memory_cross_attention(xq, mem_k, mem_v)
        h = h + jnp.concatenate([self_out, cross_out], axis=-1) @ w_out[i]
        h = h + conv_glu(rms_norm(h, norm_ffn_g[i]), w_gate_up[i], ffn_conv_w[i],
                         ffn_conv_b[i], w_down[i])
    return rms_norm(h, final_norm_g)


import jax as _jax
import jax.numpy as _jnp

TWIN_FORMAT = 'train_step'
FWD_PARAMS = ['x', 'mem', 'rel_bias', 'norm_mix_g', 'norm_mem_g', 'w_mem_kv', 'w_out', 'w_in_a', 'sinks_a', 'w_in_b', 'conv_qkv_b', 'a_log_b', 'dt_bias_b', 'out_norm_g_b', 'norm_ffn_g', 'w_gate_up', 'ffn_conv_w', 'ffn_conv_b', 'w_down', 'final_norm_g']
TWIN_WEIGHTS = ['rel_bias', 'norm_mix_g', 'norm_mem_g', 'w_mem_kv', 'w_out', 'w_in_a', 'sinks_a', 'w_in_b', 'conv_qkv_b', 'a_log_b', 'dt_bias_b', 'out_norm_g_b', 'norm_ffn_g', 'w_gate_up', 'ffn_conv_w', 'ffn_conv_b', 'w_down', 'final_norm_g']
TWIN_DIFF_INPUT = 'x'
TWIN_INPUTS = ['x', 'mem', 'rel_bias', 'norm_mix_g', 'norm_mem_g', 'w_mem_kv', 'w_out', 'w_in_a', 'sinks_a', 'w_in_b', 'conv_qkv_b', 'a_log_b', 'dt_bias_b', 'out_norm_g_b', 'norm_ffn_g', 'w_gate_up', 'ffn_conv_w', 'ffn_conv_b', 'w_down', 'final_norm_g', 'loss_target', 'm_rel_bias', 'm_norm_mix_g', 'm_norm_mem_g', 'm_w_mem_kv', 'm_w_out', 'm_w_in_a', 'm_sinks_a', 'm_w_in_b', 'm_conv_qkv_b', 'm_a_log_b', 'm_dt_bias_b', 'm_out_norm_g_b', 'm_norm_ffn_g', 'm_w_gate_up', 'm_ffn_conv_w', 'm_ffn_conv_b', 'm_w_down', 'm_final_norm_g', 'v_rel_bias', 'v_norm_mix_g', 'v_norm_mem_g', 'v_w_mem_kv', 'v_w_out', 'v_w_in_a', 'v_sinks_a', 'v_w_in_b', 'v_conv_qkv_b', 'v_a_log_b', 'v_dt_bias_b', 'v_out_norm_g_b', 'v_norm_ffn_g', 'v_w_gate_up', 'v_ffn_conv_w', 'v_ffn_conv_b', 'v_w_down', 'v_final_norm_g']
TWIN_OUTPUTS = ['loss', 'grad_x', 'grad_rel_bias', 'grad_norm_mix_g', 'grad_norm_mem_g', 'grad_w_mem_kv', 'grad_w_out', 'grad_w_in_a', 'grad_sinks_a', 'grad_w_in_b', 'grad_conv_qkv_b', 'grad_a_log_b', 'grad_dt_bias_b', 'grad_out_norm_g_b', 'grad_norm_ffn_g', 'grad_w_gate_up', 'grad_ffn_conv_w', 'grad_ffn_conv_b', 'grad_w_down', 'grad_final_norm_g', 'delta_rel_bias', 'delta_norm_mix_g', 'delta_norm_mem_g', 'delta_w_mem_kv', 'delta_w_out', 'delta_w_in_a', 'delta_sinks_a', 'delta_w_in_b', 'delta_conv_qkv_b', 'delta_a_log_b', 'delta_dt_bias_b', 'delta_out_norm_g_b', 'delta_norm_ffn_g', 'delta_w_gate_up', 'delta_ffn_conv_w', 'delta_ffn_conv_b', 'delta_w_down', 'delta_final_norm_g', 'new_m_rel_bias', 'new_m_norm_mix_g', 'new_m_norm_mem_g', 'new_m_w_mem_kv', 'new_m_w_out', 'new_m_w_in_a', 'new_m_sinks_a', 'new_m_w_in_b', 'new_m_conv_qkv_b', 'new_m_a_log_b', 'new_m_dt_bias_b', 'new_m_out_norm_g_b', 'new_m_norm_ffn_g', 'new_m_w_gate_up', 'new_m_ffn_conv_w', 'new_m_ffn_conv_b', 'new_m_w_down', 'new_m_final_norm_g', 'new_v_rel_bias', 'new_v_norm_mix_g', 'new_v_norm_mem_g', 'new_v_w_mem_kv', 'new_v_w_out', 'new_v_w_in_a', 'new_v_sinks_a', 'new_v_w_in_b', 'new_v_conv_qkv_b', 'new_v_a_log_b', 'new_v_dt_bias_b', 'new_v_out_norm_g_b', 'new_v_norm_ffn_g', 'new_v_w_gate_up', 'new_v_ffn_conv_w', 'new_v_ffn_conv_b', 'new_v_w_down', 'new_v_final_norm_g']
TWIN_LEAF_KINDS = {'loss': 'loss', 'grad_x': 'grad_x', 'grad_rel_bias': 'grad_w', 'grad_norm_mix_g': 'grad_w', 'grad_norm_mem_g': 'grad_w', 'grad_w_mem_kv': 'grad_w', 'grad_w_out': 'grad_w', 'grad_w_in_a': 'grad_w', 'grad_sinks_a': 'grad_w', 'grad_w_in_b': 'grad_w', 'grad_conv_qkv_b': 'grad_w', 'grad_a_log_b': 'grad_w', 'grad_dt_bias_b': 'grad_w', 'grad_out_norm_g_b': 'grad_w', 'grad_norm_ffn_g': 'grad_w', 'grad_w_gate_up': 'grad_w', 'grad_ffn_conv_w': 'grad_w', 'grad_ffn_conv_b': 'grad_w', 'grad_w_down': 'grad_w', 'grad_final_norm_g': 'grad_w', 'delta_rel_bias': 'delta_w', 'delta_norm_mix_g': 'delta_w', 'delta_norm_mem_g': 'delta_w', 'delta_w_mem_kv': 'delta_w', 'delta_w_out': 'delta_w', 'delta_w_in_a': 'delta_w', 'delta_sinks_a': 'delta_w', 'delta_w_in_b': 'delta_w', 'delta_conv_qkv_b': 'delta_w', 'delta_a_log_b': 'delta_w', 'delta_dt_bias_b': 'delta_w', 'delta_out_norm_g_b': 'delta_w', 'delta_norm_ffn_g': 'delta_w', 'delta_w_gate_up': 'delta_w', 'delta_ffn_conv_w': 'delta_w', 'delta_ffn_conv_b': 'delta_w', 'delta_w_down': 'delta_w', 'delta_final_norm_g': 'delta_w', 'new_m_rel_bias': 'new_m', 'new_m_norm_mix_g': 'new_m', 'new_m_norm_mem_g': 'new_m', 'new_m_w_mem_kv': 'new_m', 'new_m_w_out': 'new_m', 'new_m_w_in_a': 'new_m', 'new_m_sinks_a': 'new_m', 'new_m_w_in_b': 'new_m', 'new_m_conv_qkv_b': 'new_m', 'new_m_a_log_b': 'new_m', 'new_m_dt_bias_b': 'new_m', 'new_m_out_norm_g_b': 'new_m', 'new_m_norm_ffn_g': 'new_m', 'new_m_w_gate_up': 'new_m', 'new_m_ffn_conv_w': 'new_m', 'new_m_ffn_conv_b': 'new_m', 'new_m_w_down': 'new_m', 'new_m_final_norm_g': 'new_m', 'new_v_rel_bias': 'new_v', 'new_v_norm_mix_g': 'new_v', 'new_v_norm_mem_g': 'new_v', 'new_v_w_mem_kv': 'new_v', 'new_v_w_out': 'new_v', 'new_v_w_in_a': 'new_v', 'new_v_sinks_a': 'new_v', 'new_v_w_in_b': 'new_v', 'new_v_conv_qkv_b': 'new_v', 'new_v_a_log_b': 'new_v', 'new_v_dt_bias_b': 'new_v', 'new_v_out_norm_g_b': 'new_v', 'new_v_norm_ffn_g': 'new_v', 'new_v_w_gate_up': 'new_v', 'new_v_ffn_conv_w': 'new_v', 'new_v_ffn_conv_b': 'new_v', 'new_v_w_down': 'new_v', 'new_v_final_norm_g': 'new_v'}


def _forward(args):
    return _fwd_reference(*[args[k] for k in FWD_PARAMS])


def _output_shape():
    out = _jax.eval_shape(lambda: _forward(_fwd_setup_inputs(0)))
    return out.shape, out.dtype

N_MICROBATCH = 1
ADAM_LR = 0.001
ADAM_B1 = 0.9
ADAM_B2 = 0.999
ADAM_EPS = 1e-08
ADAM_WD = 0.01
ADAM_STEP = 10
PER_EXAMPLE_BATCH_AXIS = {'x': 0, 'mem': 0, 'loss_target': 0}
SHARED_INPUTS = []
_WEIGHT_DTYPES = {'rel_bias': _jnp.float32, 'norm_mix_g': _jnp.float32, 'norm_mem_g': _jnp.float32, 'w_mem_kv': _jnp.float32, 'w_out': _jnp.float32, 'w_in_a': _jnp.float32, 'sinks_a': _jnp.float32, 'w_in_b': _jnp.float32, 'conv_qkv_b': _jnp.float32, 'a_log_b': _jnp.float32, 'dt_bias_b': _jnp.float32, 'out_norm_g_b': _jnp.float32, 'norm_ffn_g': _jnp.float32, 'w_gate_up': _jnp.float32, 'ffn_conv_w': _jnp.float32, 'ffn_conv_b': _jnp.float32, 'w_down': _jnp.float32, 'final_norm_g': _jnp.float32}
MOMENT_SCALE = {'rel_bias': 6.143604e-02, 'norm_mix_g': 1.045121e-01, 'norm_mem_g': 1.722540e-02, 'w_mem_kv': 2.193998e-02, 'w_out': 6.313466e-02, 'w_in_a': 6.707576e-02, 'sinks_a': 3.106374e-02, 'w_in_b': 8.245359e-02, 'conv_qkv_b': 8.534375e-02, 'a_log_b': 3.673356e-01, 'dt_bias_b': 3.504897e-01, 'out_norm_g_b': 2.038794e-01, 'norm_ffn_g': 1.543603e-01, 'w_gate_up': 6.001873e-02, 'ffn_conv_w': 6.127365e-02, 'ffn_conv_b': 5.732835e-02, 'w_down': 9.810880e-02, 'final_norm_g': 3.193387e+01}


def _to_microbatches(a, axis):
    t = _jnp.moveaxis(a, axis, 0)
    t = t.reshape((N_MICROBATCH, t.shape[0] // N_MICROBATCH) + t.shape[1:])
    return _jnp.moveaxis(t, 1, axis + 1)


def setup_inputs(seed: int = 0) -> dict:
    inp = _fwd_setup_inputs(seed)
    key = _jax.random.fold_in(_jax.random.key(seed), 7919)
    shape, _ = _output_shape()
    out = dict(inp)
    out["loss_target"] = _jax.random.normal(_jax.random.fold_in(key, 0), shape, _jnp.float32)
    for i, name in enumerate(TWIN_WEIGHTS):
        w = inp[name].astype(_jnp.float32)
        if MOMENT_SCALE is None:
            s = _jnp.sqrt(_jnp.mean(_jnp.square(w)) + 1e-30)
        else:
            s = MOMENT_SCALE[name]
        km, kv = _jax.random.split(_jax.random.fold_in(key, i + 1))
        out[name] = w
        out["m_" + name] = s * _jax.random.normal(km, w.shape, _jnp.float32)
        out["v_" + name] = (s * s) * _jax.random.uniform(kv, w.shape, _jnp.float32, 0.5, 1.5)
    if N_MICROBATCH > 1:
        for name, axis in PER_EXAMPLE_BATCH_AXIS.items():
            out[name] = _to_microbatches(out[name], axis)
    return {'x': out['x'], 'mem': out['mem'], 'rel_bias': out['rel_bias'], 'norm_mix_g': out['norm_mix_g'], 'norm_mem_g': out['norm_mem_g'], 'w_mem_kv': out['w_mem_kv'], 'w_out': out['w_out'], 'w_in_a': out['w_in_a'], 'sinks_a': out['sinks_a'], 'w_in_b': out['w_in_b'], 'conv_qkv_b': out['conv_qkv_b'], 'a_log_b': out['a_log_b'], 'dt_bias_b': out['dt_bias_b'], 'out_norm_g_b': out['out_norm_g_b'], 'norm_ffn_g': out['norm_ffn_g'], 'w_gate_up': out['w_gate_up'], 'ffn_conv_w': out['ffn_conv_w'], 'ffn_conv_b': out['ffn_conv_b'], 'w_down': out['w_down'], 'final_norm_g': out['final_norm_g'], 'loss_target': out['loss_target'], 'm_rel_bias': out['m_rel_bias'], 'm_norm_mix_g': out['m_norm_mix_g'], 'm_norm_mem_g': out['m_norm_mem_g'], 'm_w_mem_kv': out['m_w_mem_kv'], 'm_w_out': out['m_w_out'], 'm_w_in_a': out['m_w_in_a'], 'm_sinks_a': out['m_sinks_a'], 'm_w_in_b': out['m_w_in_b'], 'm_conv_qkv_b': out['m_conv_qkv_b'], 'm_a_log_b': out['m_a_log_b'], 'm_dt_bias_b': out['m_dt_bias_b'], 'm_out_norm_g_b': out['m_out_norm_g_b'], 'm_norm_ffn_g': out['m_norm_ffn_g'], 'm_w_gate_up': out['m_w_gate_up'], 'm_ffn_conv_w': out['m_ffn_conv_w'], 'm_ffn_conv_b': out['m_ffn_conv_b'], 'm_w_down': out['m_w_down'], 'm_final_norm_g': out['m_final_norm_g'], 'v_rel_bias': out['v_rel_bias'], 'v_norm_mix_g': out['v_norm_mix_g'], 'v_norm_mem_g': out['v_norm_mem_g'], 'v_w_mem_kv': out['v_w_mem_kv'], 'v_w_out': out['v_w_out'], 'v_w_in_a': out['v_w_in_a'], 'v_sinks_a': out['v_sinks_a'], 'v_w_in_b': out['v_w_in_b'], 'v_conv_qkv_b': out['v_conv_qkv_b'], 'v_a_log_b': out['v_a_log_b'], 'v_dt_bias_b': out['v_dt_bias_b'], 'v_out_norm_g_b': out['v_out_norm_g_b'], 'v_norm_ffn_g': out['v_norm_ffn_g'], 'v_w_gate_up': out['v_w_gate_up'], 'v_ffn_conv_w': out['v_ffn_conv_w'], 'v_ffn_conv_b': out['v_ffn_conv_b'], 'v_w_down': out['v_w_down'], 'v_final_norm_g': out['v_final_norm_g']}


def _loss(weights, diff, rest, loss_target):
    with _jax.named_scope("forward"):
        args = {**rest, TWIN_DIFF_INPUT: diff, **{k: w.astype(_WEIGHT_DTYPES[k]) for k, w in weights.items()}}
        y = _forward(args)
    with _jax.named_scope("loss_head"):
        err = _jnp.square(y.astype(_jnp.float32) - loss_target)
        return 0.5 * _jnp.sum(_jnp.mean(err, axis=-1)) if err.ndim else 0.5 * err


def _adamw(w, g, m, v):
    m = ADAM_B1 * m + (1.0 - ADAM_B1) * g
    v = ADAM_B2 * v + (1.0 - ADAM_B2) * _jnp.square(g)
    m_hat = m / (1.0 - ADAM_B1 ** ADAM_STEP)
    v_hat = v / (1.0 - ADAM_B2 ** ADAM_STEP)
    delta = -ADAM_LR * (m_hat / (_jnp.sqrt(v_hat) + ADAM_EPS) + ADAM_WD * w)
    return delta, m, v


def reference(x, mem, rel_bias, norm_mix_g, norm_mem_g, w_mem_kv, w_out, w_in_a, sinks_a, w_in_b, conv_qkv_b, a_log_b, dt_bias_b, out_norm_g_b, norm_ffn_g, w_gate_up, ffn_conv_w, ffn_conv_b, w_down, final_norm_g, loss_target, m_rel_bias, m_norm_mix_g, m_norm_mem_g, m_w_mem_kv, m_w_out, m_w_in_a, m_sinks_a, m_w_in_b, m_conv_qkv_b, m_a_log_b, m_dt_bias_b, m_out_norm_g_b, m_norm_ffn_g, m_w_gate_up, m_ffn_conv_w, m_ffn_conv_b, m_w_down, m_final_norm_g, v_rel_bias, v_norm_mix_g, v_norm_mem_g, v_w_mem_kv, v_w_out, v_w_in_a, v_sinks_a, v_w_in_b, v_conv_qkv_b, v_a_log_b, v_dt_bias_b, v_out_norm_g_b, v_norm_ffn_g, v_w_gate_up, v_ffn_conv_w, v_ffn_conv_b, v_w_down, v_final_norm_g):
    given = dict(x=x, mem=mem, rel_bias=rel_bias, norm_mix_g=norm_mix_g, norm_mem_g=norm_mem_g, w_mem_kv=w_mem_kv, w_out=w_out, w_in_a=w_in_a, sinks_a=sinks_a, w_in_b=w_in_b, conv_qkv_b=conv_qkv_b, a_log_b=a_log_b, dt_bias_b=dt_bias_b, out_norm_g_b=out_norm_g_b, norm_ffn_g=norm_ffn_g, w_gate_up=w_gate_up, ffn_conv_w=ffn_conv_w, ffn_conv_b=ffn_conv_b, w_down=w_down, final_norm_g=final_norm_g, loss_target=loss_target, m_rel_bias=m_rel_bias, m_norm_mix_g=m_norm_mix_g, m_norm_mem_g=m_norm_mem_g, m_w_mem_kv=m_w_mem_kv, m_w_out=m_w_out, m_w_in_a=m_w_in_a, m_sinks_a=m_sinks_a, m_w_in_b=m_w_in_b, m_conv_qkv_b=m_conv_qkv_b, m_a_log_b=m_a_log_b, m_dt_bias_b=m_dt_bias_b, m_out_norm_g_b=m_out_norm_g_b, m_norm_ffn_g=m_norm_ffn_g, m_w_gate_up=m_w_gate_up, m_ffn_conv_w=m_ffn_conv_w, m_ffn_conv_b=m_ffn_conv_b, m_w_down=m_w_down, m_final_norm_g=m_final_norm_g, v_rel_bias=v_rel_bias, v_norm_mix_g=v_norm_mix_g, v_norm_mem_g=v_norm_mem_g, v_w_mem_kv=v_w_mem_kv, v_w_out=v_w_out, v_w_in_a=v_w_in_a, v_sinks_a=v_sinks_a, v_w_in_b=v_w_in_b, v_conv_qkv_b=v_conv_qkv_b, v_a_log_b=v_a_log_b, v_dt_bias_b=v_dt_bias_b, v_out_norm_g_b=v_out_norm_g_b, v_norm_ffn_g=v_norm_ffn_g, v_w_gate_up=v_w_gate_up, v_ffn_conv_w=v_ffn_conv_w, v_ffn_conv_b=v_ffn_conv_b, v_w_down=v_w_down, v_final_norm_g=v_final_norm_g)
    weights = {n: given[n] for n in TWIN_WEIGHTS}
    shared = {n: given[n] for n in SHARED_INPUTS}
    per_example = {n: given[n] for n in ['x', 'mem']}
    grad_fn = _jax.value_and_grad(_loss, argnums=(0, 1))

    def one_microbatch(ex, loss_target):
        ex = dict(ex)
        diff = ex.pop(TWIN_DIFF_INPUT)
        return grad_fn(weights, diff, {**shared, **ex}, loss_target)

    if N_MICROBATCH == 1:
        loss, (grad_w, grad_x) = one_microbatch(per_example, given["loss_target"])
    else:
        def body(carry, xs):
            loss_sum, grad_sum = carry
            l_k, (gw_k, gx_k) = one_microbatch(xs[0], xs[1])
            with _jax.named_scope("update"):
                return (loss_sum + l_k, _jax.tree.map(_jnp.add, grad_sum, gw_k)), gx_k

        init = (_jnp.zeros((), _jnp.float32), _jax.tree.map(_jnp.zeros_like, weights))
        (loss, grad_w), grad_x = _jax.lax.scan(body, init, (per_example, given["loss_target"]))
    with _jax.named_scope("update"):
        delta_w, new_m, new_v = {}, {}, {}
        for n in TWIN_WEIGHTS:
            delta_w[n], new_m[n], new_v[n] = _adamw(weights[n], grad_w[n], given["m_" + n], given["v_" + n])
    return (loss, grad_x, *[grad_w[n] for n in TWIN_WEIGHTS], *[delta_w[n] for n in TWIN_WEIGHTS],
            *[new_m[n] for n in TWIN_WEIGHTS], *[new_v[n] for n in TWIN_WEIGHTS])
```

```python
import functools
import math

import numpy as np

import jax
import jax.numpy as jnp
from jax import lax
from jax.experimental import pallas as pl
from jax.experimental.pallas import tpu as pltpu

F32 = jnp.float32
_MXU = jnp.bfloat16
_ACT = jnp.bfloat16
_WIRE = jnp.bfloat16
_HI = lax.Precision.HIGHEST
_TM = 512
_TM_GLU = 256
_VMEM_LIMIT = 48 * 1024 * 1024
_SDS = jax.ShapeDtypeStruct

D = 1024
EPS = 1e-6
A_HEADS, A_KV_HEADS, A_HD, BLK = 12, 2, 64, 128
N_BUCKETS, MAX_DISTANCE = 32, 128
B_QK_HEADS, B_V_HEADS, B_HD, B_CONV, CHUNK = 3, 6, 128, 4, 64
X_HEADS, X_HD, MEM_LEN = 4, 64, 256
D_FF, FFN_CONV = 2816, 3
A_Q, A_KV, X_Q = 768, 128, 256
B_QK, B_V, B_QKV = 384, 768, 1536
IN_A, IN_B = 1280, 2572
IN_BP = 2688
BP_Z, BP_XQ, BP_GATE = 1536, 2304, 2560
HALO = 8

ADAM_LR, ADAM_B1, ADAM_B2, ADAM_EPS, ADAM_WD, ADAM_STEP = 0.001, 0.9, 0.999, 1e-08, 0.01, 10

N_DEV = 8
AXES = ("x", "y", "c")


def _cp(*sems):
    return pltpu.CompilerParams(dimension_semantics=sems, vmem_limit_bytes=_VMEM_LIMIT)


def _mm(a, b):
    return jnp.dot(a.astype(_MXU), b.astype(_MXU), preferred_element_type=F32)


def _mm_nt(a, b):
    return lax.dot_general(a.astype(_MXU), b.astype(_MXU), (((1,), (1,)), ((), ())), preferred_element_type=F32)


def _mm_tn(a, b):
    return lax.dot_general(a.astype(_MXU), b.astype(_MXU), (((0,), (0,)), ((), ())), preferred_element_type=F32)


def _mmf(a, b):
    return jnp.dot(a, b, preferred_element_type=F32, precision=_HI)


def _mmf_nt(a, b):
    return lax.dot_general(a, b, (((1,), (1,)), ((), ())), preferred_element_type=F32, precision=_HI)


def _mmf_tn(a, b):
    return lax.dot_general(a, b, (((0,), (0,)), ((), ())), preferred_element_type=F32, precision=_HI)


def _silu(x):
    return x * jax.nn.sigmoid(x)


def _norm_matmul(x, g, w, tn, name):
    m, k = x.shape
    n = w.shape[1]
    tm = min(m, _TM)

    def body(x_ref, g_ref, w_ref, y_ref, hn_ref):
        @pl.when(pl.program_id(1) == 0)
        def _():
            xv = x_ref[...]
            r = lax.rsqrt(jnp.mean(xv * xv, axis=-1, keepdims=True) + EPS)
            hn_ref[...] = (xv * r * g_ref[...]).astype(hn_ref.dtype)

        y_ref[...] = _mm(hn_ref[...], w_ref[...])

    return pl.pallas_call(
        body, grid=(m // tm, n // tn),
        in_specs=[pl.BlockSpec((tm, k), lambda i, j: (i, 0)), pl.BlockSpec((1, k), lambda i, j: (0, 0)),
                  pl.BlockSpec((k, tn), lambda i, j: (0, j))],
        out_specs=[pl.BlockSpec((tm, tn), lambda i, j: (i, j)), pl.BlockSpec((tm, k), lambda i, j: (i, 0))],
        out_shape=[_SDS((m, n), F32), _SDS((m, k), _ACT)],
        name=name, compiler_params=_cp("arbitrary", "arbitrary"))(x, g, w)


def _matmul_res(a, w, res, name):
    m, k = a.shape
    n = w.shape[1]
    tm = min(m, _TM)

    def body(a_ref, w_ref, r_ref, o_ref):
        o_ref[...] = r_ref[...] + _mm(a_ref[...], w_ref[...])

    return pl.pallas_call(
        body, grid=(m // tm,),
        in_specs=[pl.BlockSpec((tm, k), lambda i: (i, 0)), pl.BlockSpec((k, n), lambda i: (0, 0)),
                  pl.BlockSpec((tm, n), lambda i: (i, 0))],
        out_specs=pl.BlockSpec((tm, n), lambda i: (i, 0)),
        out_shape=_SDS((m, n), F32), name=name, compiler_params=_cp("arbitrary"))(a, w, res)


def _matmul_nt(dy, w, tk, out_dtype, name):
    m, n = dy.shape
    k = w.shape[0]
    tm = min(m, _TM)

    def body(dy_ref, w_ref, o_ref):
        o_ref[...] = _mm_nt(dy_ref[...], w_ref[...]).astype(o_ref.dtype)

    return pl.pallas_call(
        body, grid=(m // tm, k // tk),
        in_specs=[pl.BlockSpec((tm, n), lambda i, j: (i, 0)), pl.BlockSpec((tk, n), lambda i, j: (j, 0))],
        out_specs=pl.BlockSpec((tm, tk), lambda i, j: (i, j)),
        out_shape=_SDS((m, k), out_dtype), name=name, compiler_params=_cp("arbitrary", "arbitrary"))(dy, w)


def _matmul_nt_normbwd(dy, w, h, g, dh_in, tn, name):
    m, n = dy.shape
    k = w.shape[0]
    tm = min(m, _TM)
    nj = n // tn

    def body(dy_ref, w_ref, h_ref, g_ref, dhin_ref, dh_ref, dg_ref, acc_ref):
        i, j = pl.program_id(0), pl.program_id(1)

        @pl.when(j == 0)
        def _():
            acc_ref[...] = jnp.zeros_like(acc_ref)

        acc_ref[...] += _mm_nt(dy_ref[...], w_ref[...])

        @pl.when(j == nj - 1)
        def _():
            xv = h_ref[...]
            r = lax.rsqrt(jnp.mean(xv * xv, axis=-1, keepdims=True) + EPS)
            xh = xv * r
            dhn = acc_ref[...]
            part = jnp.sum(dhn * xh, axis=0, keepdims=True)

            @pl.when(i == 0)
            def _():
                dg_ref[...] = part

            @pl.when(i > 0)
            def _():
                dg_ref[...] += part

            t = dhn * g_ref[...]
            dh_ref[...] = dhin_ref[...] + r * (t - xh * jnp.mean(t * xh, axis=-1, keepdims=True))

    return pl.pallas_call(
        body, grid=(m // tm, nj),
        in_specs=[pl.BlockSpec((tm, tn), lambda i, j: (i, j)), pl.BlockSpec((k, tn), lambda i, j: (0, j)),
                  pl.BlockSpec((tm, k), lambda i, j: (i, 0)), pl.BlockSpec((1, k), lambda i, j: (0, 0)),
                  pl.BlockSpec((tm, k), lambda i, j: (i, 0))],
        out_specs=[pl.BlockSpec((tm, k), lambda i, j: (i, 0)), pl.BlockSpec((1, k), lambda i, j: (0, 0))],
        out_shape=[_SDS((m, k), F32), _SDS((1, k), F32)],
        scratch_shapes=[pltpu.VMEM((tm, k), F32)],
        name=name, compiler_params=_cp("arbitrary", "arbitrary"))(dy, w, h, g, dh_in)


def _matmul_tn(x, dy, tk, tn, name):
    m, k = x.shape
    n = dy.shape[1]
    tm = min(m, _TM)

    def body(x_ref, dy_ref, o_ref):
        @pl.when(pl.program_id(2) == 0)
        def _():
            o_ref[...] = jnp.zeros_like(o_ref)

        o_ref[...] += _mm_tn(x_ref[...], dy_ref[...])

    return pl.pallas_call(
        body, grid=(k // tk, n // tn, m // tm),
        in_specs=[pl.BlockSpec((tm, tk), lambda a, b, c: (c, a)), pl.BlockSpec((tm, tn), lambda a, b, c: (c, b))],
        out_specs=pl.BlockSpec((tk, tn), lambda a, b, c: (a, b)),
        out_shape=_SDS((k, n), F32), name=name,
        compiler_params=_cp("arbitrary", "arbitrary", "arbitrary"))(x, dy)


def _loss_head(h, g, tgt, name):
    m, k = h.shape
    tm = min(m, _TM)

    def body(h_ref, g_ref, t_ref, loss_ref, dh_ref, dg_ref):
        i = pl.program_id(0)
        xv = h_ref[...]
        r = lax.rsqrt(jnp.mean(xv * xv, axis=-1, keepdims=True) + EPS)
        xh = xv * r
        gv = g_ref[...]
        err = xh * gv - t_ref[...]
        lpart = jnp.zeros((1, 128), F32) + 0.5 * jnp.sum(jnp.mean(err * err, axis=-1, keepdims=True), axis=0, keepdims=True)
        dy = err * (1.0 / k)
        gpart = jnp.sum(dy * xh, axis=0, keepdims=True)

        @pl.when(i == 0)
        def _():
            loss_ref[...] = lpart
            dg_ref[...] = gpart

        @pl.when(i > 0)
        def _():
            loss_ref[...] += lpart
            dg_ref[...] += gpart

        t = dy * gv
        dh_ref[...] = r * (t - xh * jnp.mean(t * xh, axis=-1, keepdims=True))

    return pl.pallas_call(
        body, grid=(m // tm,),
        in_specs=[pl.BlockSpec((tm, k), lambda i: (i, 0)), pl.BlockSpec((1, k), lambda i: (0, 0)),
                  pl.BlockSpec((tm, k), lambda i: (i, 0))],
        out_specs=[pl.BlockSpec((1, 128), lambda i: (0, 0)), pl.BlockSpec((tm, k), lambda i: (i, 0)),
                   pl.BlockSpec((1, k), lambda i: (0, 0))],
        out_shape=[_SDS((1, 128), F32), _SDS((m, k), F32), _SDS((1, k), F32)],
        name=name, compiler_params=_cp("arbitrary"))(h, g, tgt)


def _glu_fwd(gu, conv_w, conv_b, name):
    s = gu.shape[0]
    tm = min(s, _TM_GLU)

    def body(gu_ref, prev_ref, w_ref, b_ref, act_ref):
        i = pl.program_id(0)
        gate = gu_ref[:, :D_FF]
        up = gu_ref[:, D_FF:]
        prev = jnp.where(i > 0, prev_ref[...], 0.0)
        ext = jnp.concatenate([prev, gate], axis=0)
        gc = b_ref[...] + w_ref[FFN_CONV - 1:FFN_CONV, :] * ext
        for j in range(FFN_CONV - 1):
            gc = gc + w_ref[j:j + 1, :] * pltpu.roll(ext, FFN_CONV - 1 - j, 0)
        act_ref[...] = (_silu(gc[HALO:]) * up).astype(act_ref.dtype)

    return pl.pallas_call(
        body, grid=(s // tm,),
        in_specs=[pl.BlockSpec((tm, 2 * D_FF), lambda i: (i, 0)),
                  pl.BlockSpec((HALO, D_FF), lambda i: (jnp.maximum(i * (tm // HALO) - 1, 0), 0)),
                  pl.BlockSpec((HALO, D_FF), lambda i: (0, 0)), pl.BlockSpec((1, D_FF), lambda i: (0, 0))],
        out_specs=pl.BlockSpec((tm, D_FF), lambda i: (i, 0)),
        out_shape=_SDS((s, D_FF), _ACT), name=name, compiler_params=_cp("arbitrary"))(gu, gu, conv_w, conv_b)


def _glu_bwd(gu, conv_w, conv_b, dact, name):
    s = gu.shape[0]
    tm = min(s, _TM_GLU)
    nt = s // tm
    ext_rows = tm + HALO

    def body(gu_ref, prev_ref, w_ref, b_ref, dact_ref, dgu_ref, dw_ref, db_ref, carry_ref):
        t = pl.program_id(0)
        i = nt - 1 - t

        @pl.when(t == 0)
        def _():
            carry_ref[...] = jnp.zeros_like(carry_ref)
            dw_ref[...] = jnp.zeros_like(dw_ref)
            db_ref[...] = jnp.zeros_like(db_ref)

        gate = gu_ref[:, :D_FF]
        up = gu_ref[:, D_FF:]
        prev = jnp.where(i > 0, prev_ref[...], 0.0)
        ext = jnp.concatenate([prev, gate], axis=0)
        shifted = [pltpu.roll(ext, FFN_CONV - 1 - j, 0) if j < FFN_CONV - 1 else ext for j in range(FFN_CONV)]
        gc = b_ref[...] + shifted[0] * w_ref[0:1, :]
        for j in range(1, FFN_CONV):
            gc = gc + shifted[j] * w_ref[j:j + 1, :]
        gc = gc[HALO:]
        sg = jax.nn.sigmoid(gc)
        da = dact_ref[...]
        dup = da * (gc * sg)
        dgc = da * up * (sg * (1.0 + gc * (1.0 - sg)))
        db_ref[...] += jnp.sum(dgc, axis=0, keepdims=True)
        dgc_ext = jnp.concatenate([jnp.zeros((HALO, D_FF), F32), dgc], axis=0)
        dext = dgc_ext * w_ref[FFN_CONV - 1:FFN_CONV, :]
        for j in range(FFN_CONV):
            dw_ref[j:j + 1, :] += jnp.sum(shifted[j] * dgc_ext, axis=0, keepdims=True)
            if j < FFN_CONV - 1:
                dext = dext + w_ref[j:j + 1, :] * pltpu.roll(dgc_ext, ext_rows - (FFN_CONV - 1 - j), 0)
        tail = jnp.concatenate([jnp.zeros((tm - HALO, D_FF), F32), carry_ref[...]], axis=0)
        dgate = dext[HALO:] + tail
        carry_ref[...] = dext[:HALO]
        dgu_ref[:, :D_FF] = dgate.astype(dgu_ref.dtype)
        dgu_ref[:, D_FF:] = dup.astype(dgu_ref.dtype)

    return pl.pallas_call(
        body, grid=(nt,),
        in_specs=[pl.BlockSpec((tm, 2 * D_FF), lambda t: (nt - 1 - t, 0)),
                  pl.BlockSpec((HALO, D_FF), lambda t: (jnp.maximum((nt - 1 - t) * (tm // HALO) - 1, 0), 0)),
                  pl.BlockSpec((HALO, D_FF), lambda t: (0, 0)), pl.BlockSpec((1, D_FF), lambda t: (0, 0)),
                  pl.BlockSpec((tm, D_FF), lambda t: (nt - 1 - t, 0))],
        out_specs=[pl.BlockSpec((tm, 2 * D_FF), lambda t: (nt - 1 - t, 0)),
                   pl.BlockSpec((HALO, D_FF), lambda t: (0, 0)), pl.BlockSpec((1, D_FF), lambda t: (0, 0))],
        out_shape=[_SDS((s, 2 * D_FF), _ACT), _SDS((HALO, D_FF), F32), _SDS((1, D_FF), F32)],
        scratch_shapes=[pltpu.VMEM((HALO, D_FF), F32)],
        name=name, compiler_params=_cp("arbitrary"))(gu, gu, conv_w, conv_b, dact)


def _bucket_table():
    qi = np.arange(BLK)[:, None]
    kj = np.arange(2 * BLK)[None, :]
    n = np.maximum(BLK + qi - kj, 0)
    max_exact = N_BUCKETS // 2
    nf = np.maximum(n, 1).astype(np.float32)
    large = max_exact + (np.log(nf / max_exact) / math.log(MAX_DISTANCE / max_exact)
                         * (N_BUCKETS - max_exact)).astype(np.int32)
    large = np.minimum(large, N_BUCKETS - 1)
    return np.where(n < max_exact, n, large).astype(np.int32)


def _swa_head(q, kb, vb, sink, bias, mask):
    s = _mm_nt(q, kb) * (A_HD ** -0.5) + bias
    s = jnp.where(mask, s, -jnp.inf)
    m = jnp.maximum(jnp.max(s, axis=-1, keepdims=True), sink)
    p = jnp.exp(s - m)
    probs = p / (jnp.sum(p, axis=-1, keepdims=True) + jnp.exp(sink - m))
    return _mm(probs, vb)


def _cross_head(q, mk, mv):
    s = _mm_nt(q, mk) * (X_HD ** -0.5)
    m = jnp.max(s, axis=-1, keepdims=True)
    p = jnp.exp(s - m)
    return _mm(p / jnp.sum(p, axis=-1, keepdims=True), mv)


def _swa_mask(i):
    qi = lax.broadcasted_iota(jnp.int32, (BLK, 2 * BLK), 0)
    kj = lax.broadcasted_iota(jnp.int32, (BLK, 2 * BLK), 1)
    dist = BLK + qi - kj
    return (dist >= 0) & (dist < BLK) & ((i > 0) | (kj >= BLK))


def _bias_build(rel_bias, bucket, name):
    def body(rb_ref, bucket_ref, o_ref):
        b = bucket_ref[...]
        for h in range(A_HEADS):
            acc = jnp.zeros((BLK, 2 * BLK), F32)
            for k in range(N_BUCKETS):
                acc = jnp.where(b == k, rb_ref[k, h], acc)
            o_ref[h] = acc

    return pl.pallas_call(
        body, in_specs=[pl.BlockSpec(memory_space=pltpu.SMEM), pl.BlockSpec(memory_space=pltpu.VMEM)],
        out_specs=pl.BlockSpec(memory_space=pltpu.VMEM),
        out_shape=_SDS((A_HEADS, BLK, 2 * BLK), F32), name=name)(rel_bias, bucket)


def _bias_reduce(dbias, bucket, name):
    def body(db_ref, bucket_ref, o_ref):
        b = bucket_ref[...]
        lane = lax.broadcasted_iota(jnp.int32, (1, 128), 1)
        o_ref[...] = jnp.zeros_like(o_ref)
        for h in range(A_HEADS):
            v = db_ref[h]
            row = jnp.zeros((1, 128), F32)
            for k in range(N_BUCKETS):
                sk = jnp.sum(jnp.sum(jnp.where(b == k, v, 0.0), axis=1, keepdims=True), axis=0, keepdims=True)
                row = row + jnp.where(lane == k, sk, 0.0)
            o_ref[h:h + 1, :] = row

    return pl.pallas_call(
        body, in_specs=[pl.BlockSpec(memory_space=pltpu.VMEM)] * 2,
        out_specs=pl.BlockSpec(memory_space=pltpu.VMEM),
        out_shape=_SDS((16, 128), F32), name=name)(dbias, bucket)


def _mix_a_fwd(proj, bias, sinks, memkv, name):
    s = proj.shape[0]
    nb = s // BLK

    def body(proj_ref, prev_ref, bias_ref, sink_ref, memkv_ref, o_ref):
        i = pl.program_id(0)
        mask = _swa_mask(i)
        prev = prev_ref[...]
        kb = jnp.concatenate([prev[:, :A_KV], proj_ref[:, A_Q:A_Q + A_KV]], axis=0)
        vb = jnp.concatenate([prev[:, A_KV:], proj_ref[:, A_Q + A_KV:A_Q + 2 * A_KV]], axis=0)
        outs = []
        for h in range(A_HEADS):
            g = h // (A_HEADS // A_KV_HEADS)
            outs.append(_swa_head(proj_ref[:, h * A_HD:(h + 1) * A_HD], kb[:, g * A_HD:(g + 1) * A_HD],
                                  vb[:, g * A_HD:(g + 1) * A_HD], sink_ref[:, h:h + 1], bias_ref[h], mask))
        for h in range(X_HEADS):
            outs.append(_cross_head(proj_ref[:, A_Q + 2 * A_KV + h * X_HD:A_Q + 2 * A_KV + (h + 1) * X_HD],
                                    memkv_ref[:, h * X_HD:(h + 1) * X_HD],
                                    memkv_ref[:, X_Q + h * X_HD:X_Q + (h + 1) * X_HD]))
        o_ref[...] = jnp.concatenate(outs, axis=1).astype(o_ref.dtype)

    return pl.pallas_call(
        body, grid=(nb,),
        in_specs=[pl.BlockSpec((BLK, IN_A), lambda i: (i, 0)),
                  pl.BlockSpec((BLK, 2 * A_KV), lambda i: (jnp.maximum(i - 1, 0), A_Q // (2 * A_KV))),
                  pl.BlockSpec((A_HEADS, BLK, 2 * BLK), lambda i: (0, 0, 0)),
                  pl.BlockSpec((1, 128), lambda i: (0, 0)),
                  pl.BlockSpec((MEM_LEN, 2 * X_Q), lambda i: (0, 0))],
        out_specs=pl.BlockSpec((BLK, D), lambda i: (i, 0)),
        out_shape=_SDS((s, D), _ACT), name=name, compiler_params=_cp("arbitrary"))(proj, proj, bias, sinks, memkv)


def _mix_a_bwd(proj, bias, sinks, memkv, dmix, name):
    s = proj.shape[0]
    nb = s // BLK
    grp = A_HEADS // A_KV_HEADS

    def body(proj_ref, prev_ref, bias_ref, sink_ref, memkv_ref, dmix_ref,
             dproj_ref, dbias_ref, dsink_ref, dmemkv_ref, carry_ref):
        t = pl.program_id(0)
        i = nb - 1 - t

        @pl.when(t == 0)
        def _():
            carry_ref[...] = jnp.zeros_like(carry_ref)
            dbias_ref[...] = jnp.zeros_like(dbias_ref)
            dsink_ref[...] = jnp.zeros_like(dsink_ref)
            dmemkv_ref[...] = jnp.zeros_like(dmemkv_ref)

        mask = _swa_mask(i)
        lane = lax.broadcasted_iota(jnp.int32, (1, 128), 1)
        prev = prev_ref[...]
        kb = jnp.concatenate([prev[:, :A_KV], proj_ref[:, A_Q:A_Q + A_KV]], axis=0)
        vb = jnp.concatenate([prev[:, A_KV:], proj_ref[:, A_Q + A_KV:A_Q + 2 * A_KV]], axis=0)
        dqs = []
        dkb = [None] * A_KV_HEADS
        dvb = [None] * A_KV_HEADS
        dsink = jnp.zeros((1, 128), F32)
        for h in range(A_HEADS):
            g = h // grp
            _, vjp = jax.vjp(
                functools.partial(_swa_head, mask=mask),
                proj_ref[:, h * A_HD:(h + 1) * A_HD], kb[:, g * A_HD:(g + 1) * A_HD],
                vb[:, g * A_HD:(g + 1) * A_HD], sink_ref[:, h:h + 1], bias_ref[h])
            dq, dk, dv, ds, db = vjp(dmix_ref[:, h * A_HD:(h + 1) * A_HD].astype(F32))
            dqs.append(dq)
            dkb[g] = dk if dkb[g] is None else dkb[g] + dk
            dvb[g] = dv if dvb[g] is None else dvb[g] + dv
            dsink = dsink + jnp.where(lane == h, ds, 0.0)
            dbias_ref[h] += db
        dsink_ref[...] += dsink
        dxq, dmk, dmv = [], [], []
        for h in range(X_HEADS):
            c0 = A_Q + 2 * A_KV + h * X_HD
            _, vjp = jax.vjp(_cross_head, proj_ref[:, c0:c0 + X_HD], memkv_ref[:, h * X_HD:(h + 1) * X_HD],
                             memkv_ref[:, X_Q + h * X_HD:X_Q + (h + 1) * X_HD])
            dq, dk, dv = vjp(dmix_ref[:, A_Q + h * X_HD:A_Q + (h + 1) * X_HD].astype(F32))
            dxq.append(dq)
            dmk.append(dk)
            dmv.append(dv)
        dmemkv_ref[...] += jnp.concatenate(dmk + dmv, axis=1)
        dkv_cur = jnp.concatenate([dkb[0][BLK:], dkb[1][BLK:], dvb[0][BLK:], dvb[1][BLK:]], axis=1) + carry_ref[...]
        carry_ref[...] = jnp.concatenate([dkb[0][:BLK], dkb[1][:BLK], dvb[0][:BLK], dvb[1][:BLK]], axis=1)
        dproj_ref[...] = jnp.concatenate(dqs + [dkv_cur] + dxq, axis=1)

    return pl.pallas_call(
        body, grid=(nb,),
        in_specs=[pl.BlockSpec((BLK, IN_A), lambda t: (nb - 1 - t, 0)),
                  pl.BlockSpec((BLK, 2 * A_KV), lambda t: (jnp.maximum(nb - 2 - t, 0), A_Q // (2 * A_KV))),
                  pl.BlockSpec((A_HEADS, BLK, 2 * BLK), lambda t: (0, 0, 0)),
                  pl.BlockSpec((1, 128), lambda t: (0, 0)),
                  pl.BlockSpec((MEM_LEN, 2 * X_Q), lambda t: (0, 0)),
                  pl.BlockSpec((BLK, D), lambda t: (nb - 1 - t, 0))],
        out_specs=[pl.BlockSpec((BLK, IN_A), lambda t: (nb - 1 - t, 0)),
                   pl.BlockSpec((A_HEADS, BLK, 2 * BLK), lambda t: (0, 0, 0)),
                   pl.BlockSpec((1, 128), lambda t: (0, 0)),
                   pl.BlockSpec((MEM_LEN, 2 * X_Q), lambda t: (0, 0))],
        out_shape=[_SDS((s, IN_A), F32), _SDS((A_HEADS, BLK, 2 * BLK), F32), _SDS((1, 128), F32),
                   _SDS((MEM_LEN, 2 * X_Q), F32)],
        scratch_shapes=[pltpu.VMEM((BLK, 2 * A_KV), F32)],
        name=name, compiler_params=_cp("arbitrary"))(proj, proj, bias, sinks, memkv, dmix)


def _dn_head(yq, yk, yv, z, bl, al, a_log, dtb, ng, s0):
    c = CHUNK
    q = _silu(yq)
    k = _silu(yk)
    v = _silu(yv)
    q = q * lax.rsqrt(jnp.sum(q * q, axis=-1, keepdims=True) + EPS)
    k = k * lax.rsqrt(jnp.sum(k * k, axis=-1, keepdims=True) + EPS)
    beta = jax.nn.sigmoid(bl)
    g = -jnp.exp(a_log) * jax.nn.softplus(al + dtb)
    qc = q * (B_HD ** -0.5)
    r = lax.broadcasted_iota(jnp.int32, (c, c), 0)
    cc = lax.broadcasted_iota(jnp.int32, (c, c), 1)
    gb = jnp.broadcast_to(g, (c, c))
    gc_col = jnp.sum(jnp.where(cc <= r, gb.T, 0.0), axis=1, keepdims=True)
    gc_row = jnp.sum(jnp.where(r <= cc, gb, 0.0), axis=0, keepdims=True)
    gc_last = jnp.sum(g, axis=0, keepdims=True)
    decay = jnp.exp(jnp.where(r >= cc, gc_col - gc_row, -jnp.inf))
    kk = _mmf_nt(k, k)
    a_mat = beta * kk * jnp.where(r > cc, decay, 0.0)
    pw = -a_mat
    t_inv = jnp.where(r == cc, 1.0, 0.0) + pw
    for _ in range(5):
        pw = _mmf(pw, pw)
        t_inv = t_inv + _mmf(t_inv, pw)
    egc = jnp.exp(gc_col)
    u = _mmf(t_inv, beta * v)
    w = _mmf(t_inv, (beta * egc) * k)
    attn = _mmf_nt(qc, k) * decay
    delta = u - _mmf(w, s0)
    out = _mmf(qc * egc, s0) + _mmf(attn, delta)
    s1 = jnp.exp(gc_last) * s0 + _mmf_tn(k * jnp.exp(gc_last - gc_col), delta)
    o = out * lax.rsqrt(jnp.mean(out * out, axis=-1, keepdims=True) + EPS) * ng
    return o * _silu(z), s1


def _dn_conv(ext, w_ref):
    y = ext * w_ref[B_CONV - 1:B_CONV, :]
    for j in range(B_CONV - 1):
        y = y + w_ref[j:j + 1, :] * pltpu.roll(ext, B_CONV - 1 - j, 0)
    return y


def _dn_head_args(y, cur_ref, par_ref, ng_ref, hv):
    hq = hv // (B_V_HEADS // B_QK_HEADS)
    return (y[:, hq * B_HD:(hq + 1) * B_HD], y[:, B_QK + hq * B_HD:B_QK + (hq + 1) * B_HD],
            y[:, 2 * B_QK + hv * B_HD:2 * B_QK + (hv + 1) * B_HD],
            cur_ref[:, BP_Z + hv * B_HD:BP_Z + (hv + 1) * B_HD],
            cur_ref[:, BP_GATE + hv:BP_GATE + hv + 1],
            cur_ref[:, BP_GATE + B_V_HEADS + hv:BP_GATE + B_V_HEADS + hv + 1],
            par_ref[:, hv:hv + 1], par_ref[:, B_V_HEADS + hv:B_V_HEADS + hv + 1], ng_ref[...])


def _mix_b_fwd(proj, conv_w, par, ng, memkv, name):
    s = proj.shape[0]
    nc = s // CHUNK

    def body(cur_ref, prev_ref, w_ref, par_ref, ng_ref, memkv_ref, o_ref, st_ref, state_ref):
        n = pl.program_id(0)

        @pl.when(n == 0)
        def _():
            state_ref[...] = jnp.zeros_like(state_ref)

        prev = jnp.where(n > 0, prev_ref[...], 0.0)
        ext = jnp.concatenate([prev, cur_ref[:, :B_QKV]], axis=0)
        y = _dn_conv(ext, w_ref)[HALO:]
        outs = []
        for hv in range(B_V_HEADS):
            s0 = state_ref[hv]
            st_ref[0, hv] = s0
            o, s1 = _dn_head(*_dn_head_args(y, cur_ref, par_ref, ng_ref, hv), s0)
            state_ref[hv] = s1
            outs.append(o)
        for h in range(X_HEADS):
            outs.append(_cross_head(cur_ref[:, BP_XQ + h * X_HD:BP_XQ + (h + 1) * X_HD],
                                    memkv_ref[:, h * X_HD:(h + 1) * X_HD],
                                    memkv_ref[:, X_Q + h * X_HD:X_Q + (h + 1) * X_HD]))
        o_ref[...] = jnp.concatenate(outs, axis=1).astype(o_ref.dtype)

    return pl.pallas_call(
        body, grid=(nc,),
        in_specs=[pl.BlockSpec((CHUNK, IN_BP), lambda n: (n, 0)),
                  pl.BlockSpec((HALO, B_QKV), lambda n: (jnp.maximum(n * (CHUNK // HALO) - 1, 0), 0)),
                  pl.BlockSpec((HALO, B_QKV), lambda n: (0, 0)),
                  pl.BlockSpec((1, 128), lambda n: (0, 0)), pl.BlockSpec((1, 128), lambda n: (0, 0)),
                  pl.BlockSpec((MEM_LEN, 2 * X_Q), lambda n: (0, 0))],
        out_specs=[pl.BlockSpec((CHUNK, D), lambda n: (n, 0)),
                   pl.BlockSpec((1, B_V_HEADS, B_HD, B_HD), lambda n: (n, 0, 0, 0))],
        out_shape=[_SDS((s, D), _ACT), _SDS((nc, B_V_HEADS, B_HD, B_HD), F32)],
        scratch_shapes=[pltpu.VMEM((B_V_HEADS, B_HD, B_HD), F32)],
        name=name, compiler_params=_cp("arbitrary"))(proj, proj, conv_w, par, ng, memkv)


def _mix_b_bwd(proj, conv_w, par, ng, memkv, states, dmix, name):
    s = proj.shape[0]
    nc = s // CHUNK
    rep = B_V_HEADS // B_QK_HEADS
    ext_rows = CHUNK + HALO

    def body(cur_ref, prev_ref, w_ref, par_ref, ng_ref, memkv_ref, st_ref, dmix_ref,
             dproj_ref, dw_ref, dpar_ref, dng_ref, dmemkv_ref, dstate_ref, carry_ref):
        t = pl.program_id(0)
        n = nc - 1 - t

        @pl.when(t == 0)
        def _():
            dstate_ref[...] = jnp.zeros_like(dstate_ref)
            carry_ref[...] = jnp.zeros_like(carry_ref)
            dw_ref[...] = jnp.zeros_like(dw_ref)
            dpar_ref[...] = jnp.zeros_like(dpar_ref)
            dng_ref[...] = jnp.zeros_like(dng_ref)
            dmemkv_ref[...] = jnp.zeros_like(dmemkv_ref)

        lane = lax.broadcasted_iota(jnp.int32, (1, 128), 1)
        prev = jnp.where(n > 0, prev_ref[...], 0.0)
        ext = jnp.concatenate([prev, cur_ref[:, :B_QKV]], axis=0)
        y = _dn_conv(ext, w_ref)[HALO:]
        dyq = [None] * B_QK_HEADS
        dyk = [None] * B_QK_HEADS
        dyv, dz = [], []
        dgate = jnp.zeros((CHUNK, 128), F32)
        dpar = jnp.zeros((1, 128), F32)
        dng = jnp.zeros((1, 128), F32)
        for hv in range(B_V_HEADS):
            hq = hv // rep
            _, vjp = jax.vjp(_dn_head, *_dn_head_args(y, cur_ref, par_ref, ng_ref, hv), st_ref[0, hv])
            gq, gk, gv, gz, gbl, gal, ga_log, gdtb, gng, gs0 = vjp(
                (dmix_ref[:, hv * B_HD:(hv + 1) * B_HD].astype(F32), dstate_ref[hv]))
            dstate_ref[hv] = gs0
            dyq[hq] = gq if dyq[hq] is None else dyq[hq] + gq
            dyk[hq] = gk if dyk[hq] is None else dyk[hq] + gk
            dyv.append(gv)
            dz.append(gz)
            dgate = dgate + jnp.where(lane == hv, gbl, 0.0) + jnp.where(lane == B_V_HEADS + hv, gal, 0.0)
            dpar = dpar + jnp.where(lane == hv, ga_log, 0.0) + jnp.where(lane == B_V_HEADS + hv, gdtb, 0.0)
            dng = dng + gng
        dpar_ref[...] += dpar
        dng_ref[...] += dng
        dxq, dmk, dmv = [], [], []
        for h in range(X_HEADS):
            _, vjp = jax.vjp(_cross_head, cur_ref[:, BP_XQ + h * X_HD:BP_XQ + (h + 1) * X_HD],
                             memkv_ref[:, h * X_HD:(h + 1) * X_HD],
                             memkv_ref[:, X_Q + h * X_HD:X_Q + (h + 1) * X_HD])
            gq, gk, gv = vjp(dmix_ref[:, B_V + h * X_HD:B_V + (h + 1) * X_HD].astype(F32))
            dxq.append(gq)
            dmk.append(gk)
            dmv.append(gv)
        dmemkv_ref[...] += jnp.concatenate(dmk + dmv, axis=1)
        dy = jnp.concatenate(dyq + dyk + dyv, axis=1)
        dy_ext = jnp.concatenate([jnp.zeros((HALO, B_QKV), F32), dy], axis=0)
        dext = dy_ext * w_ref[B_CONV - 1:B_CONV, :]
        dw_ref[B_CONV - 1:B_CONV, :] += jnp.sum(ext * dy_ext, axis=0, keepdims=True)
        for j in range(B_CONV - 1):
            sh = B_CONV - 1 - j
            dw_ref[j:j + 1, :] += jnp.sum(pltpu.roll(ext, sh, 0) * dy_ext, axis=0, keepdims=True)
            dext = dext + w_ref[j:j + 1, :] * pltpu.roll(dy_ext, ext_rows - sh, 0)
        tail = jnp.concatenate([jnp.zeros((CHUNK - HALO, B_QKV), F32), carry_ref[...]], axis=0)
        dqkv = dext[HALO:] + tail
        carry_ref[...] = dext[:HALO]
        dproj_ref[...] = jnp.concatenate([dqkv] + dz + dxq + [dgate], axis=1)

    return pl.pallas_call(
        body, grid=(nc,),
        in_specs=[pl.BlockSpec((CHUNK, IN_BP), lambda t: (nc - 1 - t, 0)),
                  pl.BlockSpec((HALO, B_QKV), lambda t: (jnp.maximum((nc - 1 - t) * (CHUNK // HALO) - 1, 0), 0)),
                  pl.BlockSpec((HALO, B_QKV), lambda t: (0, 0)),
                  pl.BlockSpec((1, 128), lambda t: (0, 0)), pl.BlockSpec((1, 128), lambda t: (0, 0)),
                  pl.BlockSpec((MEM_LEN, 2 * X_Q), lambda t: (0, 0)),
                  pl.BlockSpec((1, B_V_HEADS, B_HD, B_HD), lambda t: (nc - 1 - t, 0, 0, 0)),
                  pl.BlockSpec((CHUNK, D), lambda t: (nc - 1 - t, 0))],
        out_specs=[pl.BlockSpec((CHUNK, IN_BP), lambda t: (nc - 1 - t, 0)),
                   pl.BlockSpec((HALO, B_QKV), lambda t: (0, 0)),
                   pl.BlockSpec((1, 128), lambda t: (0, 0)), pl.BlockSpec((1, 128), lambda t: (0, 0)),
                   pl.BlockSpec((MEM_LEN, 2 * X_Q), lambda t: (0, 0))],
        out_shape=[_SDS((s, IN_BP), F32), _SDS((HALO, B_QKV), F32), _SDS((1, 128), F32), _SDS((1, 128), F32),
                   _SDS((MEM_LEN, 2 * X_Q), F32)],
        scratch_shapes=[pltpu.VMEM((B_V_HEADS, B_HD, B_HD), F32), pltpu.VMEM((HALO, B_QKV), F32)],
        name=name, compiler_params=_cp("arbitrary"))(proj, proj, conv_w, par, ng, memkv, states, dmix)


def _place():
    return lax.axis_index("x"), lax.axis_index("y"), lax.axis_index("c")


def _all_gather(shard, name):
    def body(x_ref, out_ref, send_sems, recv_sems, local_sem):
        x, y, c = _place()
        me, sibling = (x, y, c), (x, y, 1 - c)
        chips = [(1 - x, y), (x, 1 - y), (1 - x, 1 - y)]

        def rows(px, py, pc):
            return out_ref.at[4 * px + 2 * py + pc]

        def copy(k, block, to, src=None):
            return pltpu.make_async_remote_copy(
                src_ref=rows(*block) if src is None else src, dst_ref=rows(*block),
                send_sem=send_sems.at[k], recv_sem=recv_sems.at[k],
                device_id=to, device_id_type=pl.DeviceIdType.MESH)

        mine = pltpu.make_async_copy(x_ref, rows(*me), local_sem)
        mine.start()
        first = [copy(0, me, sibling, src=x_ref)]
        first += [copy(1 + j, me, (*chip, c), src=x_ref) for j, chip in enumerate(chips)]
        for cp in first:
            cp.start()
        passed = [copy(4 + j, (*chip, c), sibling) for j, chip in enumerate(chips)]
        for j, chip in enumerate(chips):
            copy(1 + j, (*chip, c), me).wait_recv()
            passed[j].start()
        copy(0, sibling, me).wait_recv()
        for j, chip in enumerate(chips):
            copy(4 + j, (*chip, 1 - c), me).wait_recv()
        for cp in first + passed:
            cp.wait_send()
        mine.wait()

    return pl.pallas_call(
        body, out_shape=_SDS((N_DEV,) + shard.shape, shard.dtype),
        in_specs=[pl.BlockSpec(memory_space=pl.ANY)], out_specs=pl.BlockSpec(memory_space=pl.ANY),
        scratch_shapes=[pltpu.SemaphoreType.DMA((7,)), pltpu.SemaphoreType.DMA((7,)), pltpu.SemaphoreType.DMA],
        name=name)(shard)


def _exchange(parts, name):
    def body(p_ref, out_ref, send_sems, recv_sems, local_sem):
        x, y, c = _place()
        my = 4 * x + 2 * y + c
        mine = pltpu.make_async_copy(p_ref.at[my], out_ref.at[my], local_sem)
        mine.start()

        def copy(k):
            px, py, pc = x ^ ((k + 1) >> 2 & 1), y ^ ((k + 1) >> 1 & 1), c ^ ((k + 1) & 1)
            peer = 4 * px + 2 * py + pc
            send = pltpu.make_async_remote_copy(
                src_ref=p_ref.at[peer], dst_ref=out_ref.at[my], send_sem=send_sems.at[k], recv_sem=recv_sems.at[k],
                device_id=(px, py, pc), device_id_type=pl.DeviceIdType.MESH)
            recv = pltpu.make_async_remote_copy(
                src_ref=p_ref.at[my], dst_ref=out_ref.at[peer], send_sem=send_sems.at[k], recv_sem=recv_sems.at[k],
                device_id=(px, py, pc), device_id_type=pl.DeviceIdType.MESH)
            return send, recv

        copies = [copy(k) for k in range(N_DEV - 1)]
        for send, _ in copies:
            send.start()
        for _, recv in copies:
            recv.wait_recv()
        for send, _ in copies:
            send.wait_send()
        mine.wait()

    return pl.pallas_call(
        body, out_shape=_SDS(parts.shape, parts.dtype),
        in_specs=[pl.BlockSpec(memory_space=pl.ANY)], out_specs=pl.BlockSpec(memory_space=pl.ANY),
        scratch_shapes=[pltpu.SemaphoreType.DMA((7,)), pltpu.SemaphoreType.DMA((7,)), pltpu.SemaphoreType.DMA],
        name=name)(parts)


def _adamw(parts, w, m, v, name):
    r = w.shape[0]
    tr = 1024
    c1 = 1.0 - ADAM_B1 ** ADAM_STEP
    c2 = 1.0 - ADAM_B2 ** ADAM_STEP

    def body(p_ref, w_ref, m_ref, v_ref, g_ref, d_ref, nm_ref, nv_ref):
        g = p_ref[0].astype(F32)
        for s in range(1, N_DEV):
            g = g + p_ref[s].astype(F32)
        mm = ADAM_B1 * m_ref[...] + (1.0 - ADAM_B1) * g
        vv = ADAM_B2 * v_ref[...] + (1.0 - ADAM_B2) * (g * g)
        g_ref[...] = g
        nm_ref[...] = mm
        nv_ref[...] = vv
        d_ref[...] = -ADAM_LR * ((mm / c1) / (jnp.sqrt(vv / c2) + ADAM_EPS) + ADAM_WD * w_ref[...])

    spec = pl.BlockSpec((tr, 128), lambda i: (i, 0))
    return pl.pallas_call(
        body, grid=(r // tr,),
        in_specs=[pl.BlockSpec((N_DEV, tr, 128), lambda i: (0, i, 0)), spec, spec, spec],
        out_specs=[spec] * 4, out_shape=[_SDS((r, 128), F32)] * 4,
        name=name, compiler_params=_cp("arbitrary"))(parts, w, m, v)


_BIG = [("w_mem_kv", 1), ("w_out", 1), ("w_in_a", 2), ("w_in_b", 1), ("w_gate_up", 2), ("w_down", 1)]
_SMALL_SHARDED = [("conv_qkv_b", 2), ("ffn_conv_w", 2)]
_REPLICATED = ["rel_bias", "norm_mix_g", "norm_mem_g", "sinks_a", "a_log_b", "dt_bias_b", "out_norm_g_b",
               "norm_ffn_g", "ffn_conv_b", "final_norm_g"]
_ORDER = ["rel_bias", "norm_mix_g", "norm_mem_g", "w_mem_kv", "w_out", "w_in_a", "sinks_a", "w_in_b", "conv_qkv_b",
          "a_log_b", "dt_bias_b", "out_norm_g_b", "norm_ffn_g", "w_gate_up", "ffn_conv_w", "ffn_conv_b", "w_down",
          "final_norm_g"]
_PACK_ROWS = 1024


def _pack(arrs, dtype):
    flat = jnp.concatenate([a.astype(dtype).reshape(-1) for a in arrs])
    n = flat.shape[0]
    rows = -(-n // (128 * _PACK_ROWS)) * _PACK_ROWS
    return jnp.pad(flat, (0, rows * 128 - n)).reshape(rows, 128)


def _unpack(buf, shapes, lead=()):
    flat = buf.reshape(lead + (-1,))
    out, off = [], 0
    for shp in shapes:
        n = int(np.prod(shp))
        out.append(flat[..., off:off + n].reshape(lead + tuple(shp)))
        off += n
    return out


def _assemble(gathered, axis):
    g = jnp.moveaxis(gathered, 0, axis)
    shp = list(g.shape)
    return g.reshape(shp[:axis] + [shp[axis] * shp[axis + 1]] + shp[axis + 2:])


def _split(full, axis):
    shp = list(full.shape)
    g = full.reshape(shp[:axis] + [N_DEV, shp[axis] // N_DEV] + shp[axis + 1:])
    return jnp.moveaxis(g, axis, 0)


def _pad_rows(a, rows):
    return jnp.pad(a, ((0, rows - a.shape[0]), (0, 0)))


def _pad_lanes(a, lanes=128):
    return jnp.pad(a, ((0, 0), (0, lanes - a.shape[1])))


def _reorder_b(w):
    qkv_z = w[..., :B_QKV + B_V]
    gates = w[..., B_QKV + B_V:B_QKV + B_V + 2 * B_V_HEADS]
    xq = w[..., IN_B - X_Q:]
    pad = jnp.zeros(w.shape[:-1] + (IN_BP - IN_B,), w.dtype)
    return jnp.concatenate([qkv_z, xq, gates, pad], axis=-1)


def _restore_b(w):
    return jnp.concatenate([w[..., :BP_XQ], w[..., BP_GATE:BP_GATE + 2 * B_V_HEADS], w[..., BP_XQ:BP_GATE]], axis=-1)


def kernel(x, mem, rel_bias, norm_mix_g, norm_mem_g, w_mem_kv, w_out, w_in_a, sinks_a, w_in_b, conv_qkv_b, a_log_b, dt_bias_b, out_norm_g_b, norm_ffn_g, w_gate_up, ffn_conv_w, ffn_conv_b, w_down, final_norm_g, loss_target, m_rel_bias, m_norm_mix_g, m_norm_mem_g, m_w_mem_kv, m_w_out, m_w_in_a, m_sinks_a, m_w_in_b, m_conv_qkv_b, m_a_log_b, m_dt_bias_b, m_out_norm_g_b, m_norm_ffn_g, m_w_gate_up, m_ffn_conv_w, m_ffn_conv_b, m_w_down, m_final_norm_g, v_rel_bias, v_norm_mix_g, v_norm_mem_g, v_w_mem_kv, v_w_out, v_w_in_a, v_sinks_a, v_w_in_b, v_conv_qkv_b, v_a_log_b, v_dt_bias_b, v_out_norm_g_b, v_norm_ffn_g, v_w_gate_up, v_ffn_conv_w, v_ffn_conv_b, v_w_down, v_final_norm_g):
    local = dict(locals())
    wts = {n: local[n] for n in _ORDER}
    moms = {n: local["m_" + n] for n in _ORDER}
    vars_ = {n: local["v_" + n] for n in _ORDER}
    h0 = x[0]
    memx = mem[0]
    tgt = loss_target[0]

    big_shapes = [wts[n].shape for n, _ in _BIG]
    big = _unpack(_all_gather(_pack([wts[n] for n, _ in _BIG], _MXU), "gather_weights"), big_shapes, (N_DEV,))
    full = {n: _assemble(a, ax) for (n, ax), a in zip(_BIG, big)}
    small_shapes = [wts[n].shape for n, _ in _SMALL_SHARDED]
    small = _unpack(_all_gather(_pack([wts[n] for n, _ in _SMALL_SHARDED], F32), "gather_conv_weights"),
                    small_shapes, (N_DEV,))
    full.update({n: _assemble(a, ax) for (n, ax), a in zip(_SMALL_SHARDED, small)})
    w_ib = _reorder_b(full["w_in_b"][0])
    conv_qkv = _pad_rows(full["conv_qkv_b"][0], HALO)
    ffn_cw = [_pad_rows(full["ffn_conv_w"][i], HALO) for i in range(2)]
    bucket = jnp.asarray(_bucket_table())
    bias = _bias_build(rel_bias, bucket, "bias_build")
    sinks = _pad_lanes(sinks_a)
    par_b = _pad_lanes(jnp.concatenate([a_log_b, dt_bias_b], axis=1))

    proj_a, _hn_a = _norm_matmul(h0, norm_mix_g[0:1], full["w_in_a"][0], 640, "in_proj_a")
    memkv0, memn0 = _norm_matmul(memx, norm_mem_g[0:1], full["w_mem_kv"][0], 512, "mem_proj_0")
    mix_a = _mix_a_fwd(proj_a, bias, sinks, memkv0, "mix_a_fwd")
    h1 = _matmul_res(mix_a, full["w_out"][0], h0, "out_proj_0")
    gu0, hn_f0 = _norm_matmul(h1, norm_ffn_g[0:1], full["w_gate_up"][0], 1408, "gate_up_0")
    act0 = _glu_fwd(gu0, ffn_cw[0], ffn_conv_b[0:1], "glu_fwd_0")
    h2 = _matmul_res(act0, full["w_down"][0], h1, "down_proj_0")
    proj_b, hn_b = _norm_matmul(h2, norm_mix_g[1:2], w_ib, 896, "in_proj_b")
    memkv1, memn1 = _norm_matmul(memx, norm_mem_g[1:2], full["w_mem_kv"][1], 512, "mem_proj_1")
    mix_b, states = _mix_b_fwd(proj_b, conv_qkv, par_b, out_norm_g_b, memkv1, "mix_b_fwd")
    h3 = _matmul_res(mix_b, full["w_out"][1], h2, "out_proj_1")
    gu1, hn_f1 = _norm_matmul(h3, norm_ffn_g[1:2], full["w_gate_up"][1], 1408, "gate_up_1")
    act1 = _glu_fwd(gu1, ffn_cw[1], ffn_conv_b[1:2], "glu_fwd_1")
    h4 = _matmul_res(act1, full["w_down"][1], h3, "down_proj_1")
    loss_row, dh, d_final_g = _loss_head(h4, final_norm_g[None, :], tgt, "loss_head")

    grads = {"final_norm_g": d_final_g[0]}
    zeros_mem = jnp.zeros_like(memx)

    def ffn_bwd(i, dh, h_in, gu, hn_f, act, cw):
        dact = _matmul_nt(dh, full["w_down"][i], 1408, F32, f"d_act_{i}")
        d_wdown = _matmul_tn(act, dh, 1408, 1024, f"d_w_down_{i}")
        dgu, d_cw, d_cb = _glu_bwd(gu, cw, ffn_conv_b[i:i + 1], dact, f"glu_bwd_{i}")
        dh_new, d_g = _matmul_nt_normbwd(dgu, full["w_gate_up"][i], h_in, norm_ffn_g[i:i + 1], dh, 1408, f"d_ffn_in_{i}")
        d_wgu = _matmul_tn(hn_f, dgu, 1024, 1408, f"d_w_gate_up_{i}")
        return dh_new, d_wdown, d_wgu, d_cw[:FFN_CONV], d_cb[0], d_g[0]

    def mem_bwd(i, dmemkv, memn):
        _, d_g = _matmul_nt_normbwd(dmemkv, full["w_mem_kv"][i], memx, norm_mem_g[i:i + 1], zeros_mem, 512, f"d_mem_in_{i}")
        return _matmul_tn(memn, dmemkv, 1024, 512, f"d_w_mem_kv_{i}"), d_g[0]

    dh, d_wdown1, d_wgu1, d_cw1, d_cb1, d_gf1 = ffn_bwd(1, dh, h3, gu1, hn_f1, act1, ffn_cw[1])
    dmix = _matmul_nt(dh, full["w_out"][1], 1024, F32, "d_mix_1")
    d_wout1 = _matmul_tn(mix_b, dh, 1024, 1024, "d_w_out_1")
    dproj_b, d_convw, d_par, d_ng, dmemkv1 = _mix_b_bwd(proj_b, conv_qkv, par_b, out_norm_g_b, memkv1, states, dmix, "mix_b_bwd")
    dh, d_gm1 = _matmul_nt_normbwd(dproj_b, w_ib, h2, norm_mix_g[1:2], dh, 896, "d_in_b")
    d_wib = _restore_b(_matmul_tn(hn_b, dproj_b, 1024, 896, "d_w_in_b"))
    d_wmk1, d_gmem1 = mem_bwd(1, dmemkv1, memn1)
    dh, d_wdown0, d_wgu0, d_cw0, d_cb0, d_gf0 = ffn_bwd(0, dh, h1, gu0, hn_f0, act0, ffn_cw[0])
    dmix = _matmul_nt(dh, full["w_out"][0], 1024, F32, "d_mix_0")
    d_wout0 = _matmul_tn(mix_a, dh, 1024, 1024, "d_w_out_0")
    dproj_a, dbias, dsinks, dmemkv0 = _mix_a_bwd(proj_a, bias, sinks, memkv0, dmix, "mix_a_bwd")
    dh, d_gm0 = _matmul_nt_normbwd(dproj_a, full["w_in_a"][0], h0, norm_mix_g[0:1], dh, 640, "d_in_a")
    d_wia = _matmul_tn(_hn_a, dproj_a, 1024, 640, "d_w_in_a")
    d_wmk0, d_gmem0 = mem_bwd(0, dmemkv0, memn0)
    d_rel = _bias_reduce(dbias, bucket, "bias_reduce")

    grads.update({
        "rel_bias": d_rel[:A_HEADS, :N_BUCKETS].T,
        "norm_mix_g": jnp.stack([d_gm0, d_gm1]), "norm_mem_g": jnp.stack([d_gmem0, d_gmem1]),
        "w_mem_kv": jnp.stack([d_wmk0, d_wmk1]), "w_out": jnp.stack([d_wout0, d_wout1]),
        "w_in_a": d_wia[None], "sinks_a": dsinks[:, :A_HEADS], "w_in_b": d_wib[None],
        "conv_qkv_b": d_convw[None, :B_CONV], "a_log_b": d_par[:, :B_V_HEADS],
        "dt_bias_b": d_par[:, B_V_HEADS:2 * B_V_HEADS], "out_norm_g_b": d_ng,
        "norm_ffn_g": jnp.stack([d_gf0, d_gf1]), "w_gate_up": jnp.stack([d_wgu0, d_wgu1]),
        "ffn_conv_w": jnp.stack([d_cw0, d_cw1]), "ffn_conv_b": jnp.stack([d_cb0, d_cb1]),
        "w_down": jnp.stack([d_wdown0, d_wdown1]),
    })

    sharded = _BIG + _SMALL_SHARDED
    names = [n for n, _ in sharded] + _REPLICATED
    split = [_split(grads[n], ax) for n, ax in sharded]
    rep = [jnp.broadcast_to(grads[n][None], (N_DEV,) + grads[n].shape) for n in _REPLICATED]
    per_dest = jnp.concatenate([a.astype(_WIRE).reshape(N_DEV, -1) for a in split + rep], axis=1)
    n_loc = per_dest.shape[1]
    rows = -(-n_loc // (128 * _PACK_ROWS)) * _PACK_ROWS
    parts = jnp.pad(per_dest, ((0, 0), (0, rows * 128 - n_loc))).reshape(N_DEV, rows, 128)
    recv = _exchange(parts, "exchange_grads")
    g_p, d_p, m_p, v_p = _adamw(recv, _pack([wts[n] for n in names], F32), _pack([moms[n] for n in names], F32),
                                _pack([vars_[n] for n in names], F32), "adamw")
    shapes = [wts[n].shape for n in names]
    outs = {}
    for tag, buf in (("grad", g_p), ("delta", d_p), ("new_m", m_p), ("new_v", v_p)):
        outs[tag] = dict(zip(names, _unpack(buf, shapes)))

    loss = lax.psum(loss_row[0, 0], AXES)
    return (loss, dh[None], *[outs["grad"][n] for n in _ORDER], *[outs["delta"][n] for n in _ORDER],
            *[outs["new_m"][n] for n in _ORDER], *[outs["new_v"][n] for n in _ORDER])
```

```python
import functools
import math

import numpy as np

import jax
import jax.numpy as jnp
from jax import lax
from jax.experimental import pallas as pl
from jax.experimental.pallas import tpu as pltpu

F32 = jnp.float32
_MXU = jnp.bfloat16
_ACT = jnp.bfloat16
_WIRE = jnp.bfloat16
_HI = lax.Precision.HIGHEST
_TM = 512
_VMEM_LIMIT = 48 * 1024 * 1024
_SDS = jax.ShapeDtypeStruct

D = 1024
EPS = 1e-6
A_HEADS, A_KV_HEADS, A_HD, BLK = 12, 2, 64, 128
N_BUCKETS, MAX_DISTANCE = 32, 128
B_QK_HEADS, B_V_HEADS, B_HD, B_CONV, CHUNK = 3, 6, 128, 4, 64
X_HEADS, X_HD, MEM_LEN = 4, 64, 256
D_FF, FFN_CONV = 2816, 3
A_Q, A_KV, X_Q = 768, 128, 256
B_QK, B_V, B_QKV = 384, 768, 1536
IN_A, IN_B = 1280, 2572
IN_BP = 2688
BP_Z, BP_XQ, BP_GATE = 1536, 2304, 2560
HALO = 8

N_DEV = 8
AXES = ("x", "y", "c")
GU_SHARD = 2 * D_FF // N_DEV
FF_BLOCKS = D_FF // GU_SHARD
DN_SHARD = D_FF // N_DEV
IA_SHARD = IN_A // N_DEV

ADAM_LR, ADAM_B1, ADAM_B2, ADAM_EPS, ADAM_WD, ADAM_STEP = 0.001, 0.9, 0.999, 1e-08, 0.01, 10

SP_REL, SP_CB, SP_CW, SP_QKV, SP_MIX, SP_MEM, SP_FFN, SP_FINAL, SP_MISC, SMALL_ROWS = 0, 32, 34, 40, 44, 46, 48, 50, 51, 56


def _cp(*sems):
    return pltpu.CompilerParams(dimension_semantics=sems, vmem_limit_bytes=_VMEM_LIMIT)


def _mm(a, b):
    return jnp.dot(a.astype(_MXU), b.astype(_MXU), preferred_element_type=F32)


def _mm_nt(a, b):
    return lax.dot_general(a.astype(_MXU), b.astype(_MXU), (((1,), (1,)), ((), ())), preferred_element_type=F32)


def _mm_tn(a, b):
    return lax.dot_general(a.astype(_MXU), b.astype(_MXU), (((0,), (0,)), ((), ())), preferred_element_type=F32)


def _mmf(a, b):
    return jnp.dot(a, b, preferred_element_type=F32, precision=_HI)


def _mmf_nt(a, b):
    return lax.dot_general(a, b, (((1,), (1,)), ((), ())), preferred_element_type=F32, precision=_HI)


def _mmf_tn(a, b):
    return lax.dot_general(a, b, (((0,), (0,)), ((), ())), preferred_element_type=F32, precision=_HI)


def _silu(x):
    return x * jax.nn.sigmoid(x)


def _w2d(ref):
    v = ref[...]
    return v.reshape(-1, v.shape[-1])


def _rows(m):
    return min(m, _TM)


def _spec_rowsharded(layer, rows, cols, col_block=None):
    if col_block is None:
        return pl.BlockSpec((N_DEV, None, rows, cols), lambda *_: (0, layer, 0, 0))
    return pl.BlockSpec((N_DEV, None, rows, cols), lambda *ids: (0, layer, 0, ids[col_block]))


def _spec_gate_up(layer, axis):
    return pl.BlockSpec((None, None, D, GU_SHARD), lambda *ids: (ids[axis], layer, 0, 0))


def _spec_down(layer, axis):
    return pl.BlockSpec((2, None, DN_SHARD, D), lambda *ids: (ids[axis], layer, 0, 0))


def _spec_gu_act(row_axis, axis, tm):
    return pl.BlockSpec((None, None, tm, GU_SHARD), lambda *ids: (ids[axis] // FF_BLOCKS, ids[axis] % FF_BLOCKS, ids[row_axis], 0))


def _norm_matmul(x, g, w, w_spec, n_blocks, out_shape, out_spec, name):
    m, k = x.shape
    tm = _rows(m)

    def body(x_ref, g_ref, w_ref, y_ref, hn_ref):
        @pl.when(pl.program_id(1) == 0)
        def _():
            xv = x_ref[...]
            r = lax.rsqrt(jnp.mean(xv * xv, axis=-1, keepdims=True) + EPS)
            hn_ref[...] = (xv * r * g_ref[...]).astype(hn_ref.dtype)

        y_ref[...] = _mm(hn_ref[...], _w2d(w_ref))

    return pl.pallas_call(
        body, grid=(m // tm, n_blocks),
        in_specs=[pl.BlockSpec((tm, k), lambda i, j: (i, 0)), pl.BlockSpec((1, k), lambda i, j: (0, 0)), w_spec],
        out_specs=[out_spec, pl.BlockSpec((tm, k), lambda i, j: (i, 0))],
        out_shape=[_SDS(out_shape, F32), _SDS((m, k), _ACT)],
        name=name, compiler_params=_cp("arbitrary", "arbitrary"))(x, g, w)


def _matmul_res(a, a_spec, w, w_spec, n_k, res, name):
    m, n = res.shape
    tm = _rows(m)

    def body(a_ref, w_ref, r_ref, o_ref):
        part = _mm(a_ref[...], _w2d(w_ref))

        @pl.when(pl.program_id(1) == 0)
        def _():
            o_ref[...] = r_ref[...] + part

        @pl.when(pl.program_id(1) > 0)
        def _():
            o_ref[...] += part

    return pl.pallas_call(
        body, grid=(m // tm, n_k),
        in_specs=[a_spec, w_spec, pl.BlockSpec((tm, n), lambda i, j: (i, 0))],
        out_specs=pl.BlockSpec((tm, n), lambda i, j: (i, 0)),
        out_shape=_SDS((m, n), F32), name=name, compiler_params=_cp("arbitrary", "arbitrary"))(a, w, res)


def _matmul_nt(dy, w, w_spec, n_blocks, out_shape, out_spec, name):
    m, n = dy.shape
    tm = _rows(m)

    def body(dy_ref, w_ref, o_ref):
        o_ref[...] = _mm_nt(dy_ref[...], _w2d(w_ref)).astype(o_ref.dtype)

    return pl.pallas_call(
        body, grid=(m // tm, n_blocks),
        in_specs=[pl.BlockSpec((tm, n), lambda i, j: (i, 0)), w_spec],
        out_specs=out_spec, out_shape=_SDS(out_shape, F32),
        name=name, compiler_params=_cp("arbitrary", "arbitrary"))(dy, w)


def _matmul_nt_normbwd(dy, dy_spec, w, w_spec, nj, h, g, dh_in, name):
    m, k = h.shape
    tm = _rows(m)

    def body(dy_ref, w_ref, h_ref, g_ref, dhin_ref, dh_ref, dg_ref, acc_ref):
        i, j = pl.program_id(0), pl.program_id(1)

        @pl.when(j == 0)
        def _():
            acc_ref[...] = jnp.zeros_like(acc_ref)

        acc_ref[...] += _mm_nt(dy_ref[...], _w2d(w_ref))

        @pl.when(j == nj - 1)
        def _():
            xv = h_ref[...]
            r = lax.rsqrt(jnp.mean(xv * xv, axis=-1, keepdims=True) + EPS)
            xh = xv * r
            dhn = acc_ref[...]
            part = jnp.sum(dhn * xh, axis=0, keepdims=True)

            @pl.when(i == 0)
            def _():
                dg_ref[...] = part

            @pl.when(i > 0)
            def _():
                dg_ref[...] += part

            t = dhn * g_ref[...]
            dh_ref[...] = dhin_ref[...] + r * (t - xh * jnp.mean(t * xh, axis=-1, keepdims=True))

    return pl.pallas_call(
        body, grid=(m // tm, nj),
        in_specs=[dy_spec, w_spec, pl.BlockSpec((tm, k), lambda i, j: (i, 0)), pl.BlockSpec((1, k), lambda i, j: (0, 0)),
                  pl.BlockSpec((tm, k), lambda i, j: (i, 0))],
        out_specs=[pl.BlockSpec((tm, k), lambda i, j: (i, 0)), pl.BlockSpec((1, k), lambda i, j: (0, 0))],
        out_shape=[_SDS((m, k), F32), _SDS((1, k), F32)],
        scratch_shapes=[pltpu.VMEM((tm, k), F32)],
        name=name, compiler_params=_cp("arbitrary", "arbitrary"))(dy, w, h, g, dh_in)


def _matmul_tn(x, x_spec, dy, dy_spec, m, n_blocks, acc_shape, out_shape, out_spec, name, split=None):
    tm = _rows(m)
    nm = m // tm

    def body(x_ref, dy_ref, o_ref, acc_ref):
        @pl.when(pl.program_id(1) == 0)
        def _():
            acc_ref[...] = jnp.zeros_like(acc_ref)

        acc_ref[...] += _mm_tn(x_ref[...], dy_ref[...])

        @pl.when(pl.program_id(1) == nm - 1)
        def _():
            if split is None:
                o_ref[...] = acc_ref[...].reshape(o_ref.shape).astype(o_ref.dtype)
            else:
                for d in range(N_DEV):
                    o_ref[d] = acc_ref[:, d * split:(d + 1) * split].astype(o_ref.dtype)

    return pl.pallas_call(
        body, grid=(n_blocks, nm), in_specs=[x_spec, dy_spec], out_specs=out_spec,
        out_shape=_SDS(out_shape, _WIRE), scratch_shapes=[pltpu.VMEM(acc_shape, F32)],
        name=name, compiler_params=_cp("arbitrary", "arbitrary"))(x, dy)


def _loss_head(h, g, tgt, name):
    m, k = h.shape
    tm = _rows(m)

    def body(h_ref, g_ref, t_ref, loss_ref, dh_ref, dg_ref):
        i = pl.program_id(0)
        xv = h_ref[...]
        r = lax.rsqrt(jnp.mean(xv * xv, axis=-1, keepdims=True) + EPS)
        xh = xv * r
        gv = g_ref[...]
        err = xh * gv - t_ref[...]
        lpart = jnp.zeros((1, 128), F32) + 0.5 * jnp.sum(jnp.mean(err * err, axis=-1, keepdims=True), axis=0, keepdims=True)
        dy = err * (1.0 / k)
        gpart = jnp.sum(dy * xh, axis=0, keepdims=True)

        @pl.when(i == 0)
        def _():
            loss_ref[...] = lpart
            dg_ref[...] = gpart

        @pl.when(i > 0)
        def _():
            loss_ref[...] += lpart
            dg_ref[...] += gpart

        t = dy * gv
        dh_ref[...] = r * (t - xh * jnp.mean(t * xh, axis=-1, keepdims=True))

    return pl.pallas_call(
        body, grid=(m // tm,),
        in_specs=[pl.BlockSpec((tm, k), lambda i: (i, 0)), pl.BlockSpec((1, k), lambda i: (0, 0)),
                  pl.BlockSpec((tm, k), lambda i: (i, 0))],
        out_specs=[pl.BlockSpec((1, 128), lambda i: (0, 0)), pl.BlockSpec((tm, k), lambda i: (i, 0)),
                   pl.BlockSpec((1, k), lambda i: (0, 0))],
        out_shape=[_SDS((1, 128), F32), _SDS((m, k), F32), _SDS((1, k), F32)],
        name=name, compiler_params=_cp("arbitrary"))(h, g, tgt)


def _glu_fwd(gu, conv_w, conv_b, name):
    s = gu.shape[2]
    tm = _rows(s)

    def body(gu_ref, prev_ref, w_ref, b_ref, act_ref):
        i = pl.program_id(0)
        prev = jnp.where(i > 0, prev_ref[...], 0.0)
        ext = jnp.concatenate([prev, gu_ref[0]], axis=0)
        gc = b_ref[...] + w_ref[FFN_CONV - 1:FFN_CONV, :] * ext
        for j in range(FFN_CONV - 1):
            gc = gc + w_ref[j:j + 1, :] * pltpu.roll(ext, FFN_CONV - 1 - j, 0)
        act_ref[...] = (_silu(gc[HALO:]) * gu_ref[1]).astype(act_ref.dtype)

    return pl.pallas_call(
        body, grid=(s // tm, FF_BLOCKS),
        in_specs=[pl.BlockSpec((2, None, tm, GU_SHARD), lambda i, j: (0, j, i, 0)),
                  pl.BlockSpec((None, None, HALO, GU_SHARD), lambda i, j: (0, j, jnp.maximum(i * (tm // HALO) - 1, 0), 0)),
                  pl.BlockSpec((None, HALO, GU_SHARD), lambda i, j: (j, 0, 0)),
                  pl.BlockSpec((None, 1, GU_SHARD), lambda i, j: (j, 0, 0))],
        out_specs=pl.BlockSpec((None, tm, GU_SHARD), lambda i, j: (j, i, 0)),
        out_shape=_SDS((FF_BLOCKS, s, GU_SHARD), _ACT), name=name,
        compiler_params=_cp("arbitrary", "arbitrary"))(gu, gu, conv_w, conv_b)


def _glu_bwd(gu, conv_w, conv_b, dact, name):
    s = gu.shape[2]
    tm = _rows(s)
    nt = s // tm
    ext_rows = tm + HALO

    def body(gu_ref, prev_ref, w_ref, b_ref, dact_ref, dgu_ref, dw_ref, db_ref, carry_ref):
        t = pl.program_id(1)
        i = nt - 1 - t

        @pl.when(t == 0)
        def _():
            carry_ref[...] = jnp.zeros_like(carry_ref)
            dw_ref[...] = jnp.zeros_like(dw_ref)
            db_ref[...] = jnp.zeros_like(db_ref)

        up = gu_ref[1]
        prev = jnp.where(i > 0, prev_ref[...], 0.0)
        ext = jnp.concatenate([prev, gu_ref[0]], axis=0)
        shifted = [pltpu.roll(ext, FFN_CONV - 1 - j, 0) if j < FFN_CONV - 1 else ext for j in range(FFN_CONV)]
        gc = b_ref[...] + shifted[0] * w_ref[0:1, :]
        for j in range(1, FFN_CONV):
            gc = gc + shifted[j] * w_ref[j:j + 1, :]
        gc = gc[HALO:]
        sg = jax.nn.sigmoid(gc)
        da = dact_ref[...]
        dup = da * (gc * sg)
        dgc = da * up * (sg * (1.0 + gc * (1.0 - sg)))
        db_ref[...] += jnp.sum(dgc, axis=0, keepdims=True)
        dgc_ext = jnp.concatenate([jnp.zeros((HALO, GU_SHARD), F32), dgc], axis=0)
        dext = dgc_ext * w_ref[FFN_CONV - 1:FFN_CONV, :]
        for j in range(FFN_CONV):
            dw_ref[j:j + 1, :] += jnp.sum(shifted[j] * dgc_ext, axis=0, keepdims=True)
            if j < FFN_CONV - 1:
                dext = dext + w_ref[j:j + 1, :] * pltpu.roll(dgc_ext, ext_rows - (FFN_CONV - 1 - j), 0)
        tail = jnp.concatenate([jnp.zeros((tm - HALO, GU_SHARD), F32), carry_ref[...]], axis=0)
        dgate = dext[HALO:] + tail
        carry_ref[...] = dext[:HALO]
        dgu_ref[0] = dgate.astype(dgu_ref.dtype)
        dgu_ref[1] = dup.astype(dgu_ref.dtype)

    return pl.pallas_call(
        body, grid=(FF_BLOCKS, nt),
        in_specs=[pl.BlockSpec((2, None, tm, GU_SHARD), lambda j, t: (0, j, nt - 1 - t, 0)),
                  pl.BlockSpec((None, None, HALO, GU_SHARD),
                               lambda j, t: (0, j, jnp.maximum((nt - 1 - t) * (tm // HALO) - 1, 0), 0)),
                  pl.BlockSpec((None, HALO, GU_SHARD), lambda j, t: (j, 0, 0)),
                  pl.BlockSpec((None, 1, GU_SHARD), lambda j, t: (j, 0, 0)),
                  pl.BlockSpec((None, tm, GU_SHARD), lambda j, t: (j, nt - 1 - t, 0))],
        out_specs=[pl.BlockSpec((2, None, tm, GU_SHARD), lambda j, t: (0, j, nt - 1 - t, 0)),
                   pl.BlockSpec((None, HALO, GU_SHARD), lambda j, t: (j, 0, 0)),
                   pl.BlockSpec((None, 1, GU_SHARD), lambda j, t: (j, 0, 0))],
        out_shape=[_SDS(gu.shape, _ACT), _SDS((FF_BLOCKS, HALO, GU_SHARD), F32), _SDS((FF_BLOCKS, 1, GU_SHARD), F32)],
        scratch_shapes=[pltpu.VMEM((HALO, GU_SHARD), F32)],
        name=name, compiler_params=_cp("arbitrary", "arbitrary"))(gu, gu, conv_w, conv_b, dact)


def _bucket_table():
    qi = np.arange(BLK)[:, None]
    kj = np.arange(2 * BLK)[None, :]
    n = np.maximum(BLK + qi - kj, 0)
    max_exact = N_BUCKETS // 2
    nf = np.maximum(n, 1).astype(np.float32)
    large = max_exact + (np.log(nf / max_exact) / math.log(MAX_DISTANCE / max_exact)
                         * (N_BUCKETS - max_exact)).astype(np.int32)
    large = np.minimum(large, N_BUCKETS - 1)
    return np.where(n < max_exact, n, large).astype(np.int32)


def _swa_head(q, kb, vb, sink, bias, mask):
    s = _mm_nt(q, kb) * (A_HD ** -0.5) + bias
    s = jnp.where(mask, s, -jnp.inf)
    m = jnp.maximum(jnp.max(s, axis=-1, keepdims=True), sink)
    p = jnp.exp(s - m)
    probs = p / (jnp.sum(p, axis=-1, keepdims=True) + jnp.exp(sink - m))
    return _mm(probs, vb)


def _cross_head(q, mk, mv):
    s = _mm_nt(q, mk) * (X_HD ** -0.5)
    m = jnp.max(s, axis=-1, keepdims=True)
    p = jnp.exp(s - m)
    return _mm(p / jnp.sum(p, axis=-1, keepdims=True), mv)


def _swa_mask(i):
    qi = lax.broadcasted_iota(jnp.int32, (BLK, 2 * BLK), 0)
    kj = lax.broadcasted_iota(jnp.int32, (BLK, 2 * BLK), 1)
    dist = BLK + qi - kj
    return (dist >= 0) & (dist < BLK) & ((i > 0) | (kj >= BLK))


def _bias_build(rel_bias, bucket, name):
    def body(rb_ref, bucket_ref, o_ref):
        b = bucket_ref[...]
        for h in range(A_HEADS):
            acc = jnp.zeros((BLK, 2 * BLK), F32)
            for k in range(N_BUCKETS):
                acc = jnp.where(b == k, rb_ref[k, h], acc)
            o_ref[h] = acc

    return pl.pallas_call(
        body, in_specs=[pl.BlockSpec(memory_space=pltpu.SMEM), pl.BlockSpec(memory_space=pltpu.VMEM)],
        out_specs=pl.BlockSpec(memory_space=pltpu.VMEM),
        out_shape=_SDS((A_HEADS, BLK, 2 * BLK), F32), name=name)(rel_bias, bucket)


def _bias_reduce(dbias, bucket, name):
    def body(db_ref, bucket_ref, o_ref):
        b = bucket_ref[...]
        row = lax.broadcasted_iota(jnp.int32, (N_BUCKETS, 128), 0)
        lane = lax.broadcasted_iota(jnp.int32, (N_BUCKETS, 128), 1)
        acc = jnp.zeros((N_BUCKETS, 128), F32)
        for h in range(A_HEADS):
            v = db_ref[h]
            for k in range(N_BUCKETS):
                sk = jnp.sum(jnp.sum(jnp.where(b == k, v, 0.0), axis=1, keepdims=True), axis=0, keepdims=True)
                acc = acc + jnp.where((row == k) & (lane == h), sk, 0.0)
        o_ref[...] = acc

    return pl.pallas_call(
        body, in_specs=[pl.BlockSpec(memory_space=pltpu.VMEM)] * 2,
        out_specs=pl.BlockSpec(memory_space=pltpu.VMEM),
        out_shape=_SDS((N_BUCKETS, 128), F32), name=name)(dbias, bucket)


def _mix_a_fwd(proj, bias, sinks, memkv, name):
    s = proj.shape[0]
    nb = s // BLK

    def body(proj_ref, prev_ref, bias_ref, sink_ref, memkv_ref, o_ref):
        i = pl.program_id(0)
        mask = _swa_mask(i)
        prev = prev_ref[...]
        kb = jnp.concatenate([prev[:, :A_KV], proj_ref[:, A_Q:A_Q + A_KV]], axis=0)
        vb = jnp.concatenate([prev[:, A_KV:], proj_ref[:, A_Q + A_KV:A_Q + 2 * A_KV]], axis=0)
        outs = []
        for h in range(A_HEADS):
            g = h // (A_HEADS // A_KV_HEADS)
            outs.append(_swa_head(proj_ref[:, h * A_HD:(h + 1) * A_HD], kb[:, g * A_HD:(g + 1) * A_HD],
                                  vb[:, g * A_HD:(g + 1) * A_HD], sink_ref[:, h:h + 1], bias_ref[h], mask))
        for h in range(X_HEADS):
            outs.append(_cross_head(proj_ref[:, A_Q + 2 * A_KV + h * X_HD:A_Q + 2 * A_KV + (h + 1) * X_HD],
                                    memkv_ref[:, h * X_HD:(h + 1) * X_HD],
                                    memkv_ref[:, X_Q + h * X_HD:X_Q + (h + 1) * X_HD]))
        o_ref[...] = jnp.concatenate(outs, axis=1).astype(o_ref.dtype)

    return pl.pallas_call(
        body, grid=(nb,),
        in_specs=[pl.BlockSpec((BLK, IN_A), lambda i: (i, 0)),
                  pl.BlockSpec((BLK, 2 * A_KV), lambda i: (jnp.maximum(i - 1, 0), A_Q // (2 * A_KV))),
                  pl.BlockSpec((A_HEADS, BLK, 2 * BLK), lambda i: (0, 0, 0)),
                  pl.BlockSpec((1, 128), lambda i: (0, 0)),
                  pl.BlockSpec((MEM_LEN, 2 * X_Q), lambda i: (0, 0))],
        out_specs=pl.BlockSpec((BLK, D), lambda i: (i, 0)),
        out_shape=_SDS((s, D), _ACT), name=name, compiler_params=_cp("arbitrary"))(proj, proj, bias, sinks, memkv)


def _mix_a_bwd(proj, bias, sinks, memkv, dmix, name):
    s = proj.shape[0]
    nb = s // BLK
    grp = A_HEADS // A_KV_HEADS

    def body(proj_ref, prev_ref, bias_ref, sink_ref, memkv_ref, dmix_ref,
             dproj_ref, dbias_ref, dsink_ref, dmemkv_ref, carry_ref):
        t = pl.program_id(0)
        i = nb - 1 - t

        @pl.when(t == 0)
        def _():
            carry_ref[...] = jnp.zeros_like(carry_ref)
            dbias_ref[...] = jnp.zeros_like(dbias_ref)
            dsink_ref[...] = jnp.zeros_like(dsink_ref)
            dmemkv_ref[...] = jnp.zeros_like(dmemkv_ref)

        mask = _swa_mask(i)
        lane = lax.broadcasted_iota(jnp.int32, (1, 128), 1)
        prev = prev_ref[...]
        kb = jnp.concatenate([prev[:, :A_KV], proj_ref[:, A_Q:A_Q + A_KV]], axis=0)
        vb = jnp.concatenate([prev[:, A_KV:], proj_ref[:, A_Q + A_KV:A_Q + 2 * A_KV]], axis=0)
        dqs = []
        dkb = [None] * A_KV_HEADS
        dvb = [None] * A_KV_HEADS
        dsink = jnp.zeros((1, 128), F32)
        for h in range(A_HEADS):
            g = h // grp
            _, vjp = jax.vjp(
                functools.partial(_swa_head, mask=mask),
                proj_ref[:, h * A_HD:(h + 1) * A_HD], kb[:, g * A_HD:(g + 1) * A_HD],
                vb[:, g * A_HD:(g + 1) * A_HD], sink_ref[:, h:h + 1], bias_ref[h])
            dq, dk, dv, ds, db = vjp(dmix_ref[:, h * A_HD:(h + 1) * A_HD].astype(F32))
            dqs.append(dq)
            dkb[g] = dk if dkb[g] is None else dkb[g] + dk
            dvb[g] = dv if dvb[g] is None else dvb[g] + dv
            dsink = dsink + jnp.where(lane == h, ds, 0.0)
            dbias_ref[h] += db
        dsink_ref[...] += dsink
        dxq, dmk, dmv = [], [], []
        for h in range(X_HEADS):
            c0 = A_Q + 2 * A_KV + h * X_HD
            _, vjp = jax.vjp(_cross_head, proj_ref[:, c0:c0 + X_HD], memkv_ref[:, h * X_HD:(h + 1) * X_HD],
                             memkv_ref[:, X_Q + h * X_HD:X_Q + (h + 1) * X_HD])
            dq, dk, dv = vjp(dmix_ref[:, A_Q + h * X_HD:A_Q + (h + 1) * X_HD].astype(F32))
            dxq.append(dq)
            dmk.append(dk)
            dmv.append(dv)
        dmemkv_ref[...] += jnp.concatenate(dmk + dmv, axis=1)
        dkv_cur = jnp.concatenate([dkb[0][BLK:], dkb[1][BLK:], dvb[0][BLK:], dvb[1][BLK:]], axis=1) + carry_ref[...]
        carry_ref[...] = jnp.concatenate([dkb[0][:BLK], dkb[1][:BLK], dvb[0][:BLK], dvb[1][:BLK]], axis=1)
        dproj_ref[...] = jnp.concatenate(dqs + [dkv_cur] + dxq, axis=1)

    return pl.pallas_call(
        body, grid=(nb,),
        in_specs=[pl.BlockSpec((BLK, IN_A), lambda t: (nb - 1 - t, 0)),
                  pl.BlockSpec((BLK, 2 * A_KV), lambda t: (jnp.maximum(nb - 2 - t, 0), A_Q // (2 * A_KV))),
                  pl.BlockSpec((A_HEADS, BLK, 2 * BLK), lambda t: (0, 0, 0)),
                  pl.BlockSpec((1, 128), lambda t: (0, 0)),
                  pl.BlockSpec((MEM_LEN, 2 * X_Q), lambda t: (0, 0)),
                  pl.BlockSpec((BLK, D), lambda t: (nb - 1 - t, 0))],
        out_specs=[pl.BlockSpec((BLK, IN_A), lambda t: (nb - 1 - t, 0)),
                   pl.BlockSpec((A_HEADS, BLK, 2 * BLK), lambda t: (0, 0, 0)),
                   pl.BlockSpec((1, 128), lambda t: (0, 0)),
                   pl.BlockSpec((MEM_LEN, 2 * X_Q), lambda t: (0, 0))],
        out_shape=[_SDS((s, IN_A), F32), _SDS((A_HEADS, BLK, 2 * BLK), F32), _SDS((1, 128), F32),
                   _SDS((MEM_LEN, 2 * X_Q), F32)],
        scratch_shapes=[pltpu.VMEM((BLK, 2 * A_KV), F32)],
        name=name, compiler_params=_cp("arbitrary"))(proj, proj, bias, sinks, memkv, dmix)


def _dn_head(yq, yk, yv, z, bl, al, a_log, dtb, ng, s0):
    c = CHUNK
    q = _silu(yq)
    k = _silu(yk)
    v = _silu(yv)
    q = q * lax.rsqrt(jnp.sum(q * q, axis=-1, keepdims=True) + EPS)
    k = k * lax.rsqrt(jnp.sum(k * k, axis=-1, keepdims=True) + EPS)
    beta = jax.nn.sigmoid(bl)
    g = -jnp.exp(a_log) * jax.nn.softplus(al + dtb)
    qc = q * (B_HD ** -0.5)
    r = lax.broadcasted_iota(jnp.int32, (c, c), 0)
    cc = lax.broadcasted_iota(jnp.int32, (c, c), 1)
    gb = jnp.broadcast_to(g, (c, c))
    gc_col = jnp.sum(jnp.where(cc <= r, gb.T, 0.0), axis=1, keepdims=True)
    gc_row = jnp.sum(jnp.where(r <= cc, gb, 0.0), axis=0, keepdims=True)
    gc_last = jnp.sum(g, axis=0, keepdims=True)
    decay = jnp.exp(jnp.where(r >= cc, gc_col - gc_row, -jnp.inf))
    kk = _mmf_nt(k, k)
    a_mat = beta * kk * jnp.where(r > cc, decay, 0.0)
    pw = -a_mat
    t_inv = jnp.where(r == cc, 1.0, 0.0) + pw
    for _ in range(5):
        pw = _mmf(pw, pw)
        t_inv = t_inv + _mmf(t_inv, pw)
    egc = jnp.exp(gc_col)
    u = _mmf(t_inv, beta * v)
    w = _mmf(t_inv, (beta * egc) * k)
    attn = _mmf_nt(qc, k) * decay
    delta = u - _mmf(w, s0)
    out = _mmf(qc * egc, s0) + _mmf(attn, delta)
    s1 = jnp.exp(gc_last) * s0 + _mmf_tn(k * jnp.exp(gc_last - gc_col), delta)
    o = out * lax.rsqrt(jnp.mean(out * out, axis=-1, keepdims=True) + EPS) * ng
    return o * _silu(z), s1


def _dn_conv(ext, w_ref):
    y = ext * w_ref[B_CONV - 1:B_CONV, :]
    for j in range(B_CONV - 1):
        y = y + w_ref[j:j + 1, :] * pltpu.roll(ext, B_CONV - 1 - j, 0)
    return y


def _dn_head_args(y, cur_ref, par_ref, ng_ref, hv):
    hq = hv // (B_V_HEADS // B_QK_HEADS)
    return (y[:, hq * B_HD:(hq + 1) * B_HD], y[:, B_QK + hq * B_HD:B_QK + (hq + 1) * B_HD],
            y[:, 2 * B_QK + hv * B_HD:2 * B_QK + (hv + 1) * B_HD],
            cur_ref[:, BP_Z + hv * B_HD:BP_Z + (hv + 1) * B_HD],
            cur_ref[:, BP_GATE + hv:BP_GATE + hv + 1],
            cur_ref[:, BP_GATE + B_V_HEADS + hv:BP_GATE + B_V_HEADS + hv + 1],
            par_ref[:, hv:hv + 1], par_ref[:, B_V_HEADS + hv:B_V_HEADS + hv + 1], ng_ref[...])


def _mix_b_fwd(proj, conv_w, par, ng, memkv, name):
    s = proj.shape[0]
    nc = s // CHUNK

    def body(cur_ref, prev_ref, w_ref, par_ref, ng_ref, memkv_ref, o_ref, st_ref, state_ref):
        n = pl.program_id(0)

        @pl.when(n == 0)
        def _():
            state_ref[...] = jnp.zeros_like(state_ref)

        prev = jnp.where(n > 0, prev_ref[...], 0.0)
        ext = jnp.concatenate([prev, cur_ref[:, :B_QKV]], axis=0)
        y = _dn_conv(ext, w_ref)[HALO:]
        outs = []
        for hv in range(B_V_HEADS):
            s0 = state_ref[hv]
            st_ref[0, hv] = s0
            o, s1 = _dn_head(*_dn_head_args(y, cur_ref, par_ref, ng_ref, hv), s0)
            state_ref[hv] = s1
            outs.append(o)
        for h in range(X_HEADS):
            outs.append(_cross_head(cur_ref[:, BP_XQ + h * X_HD:BP_XQ + (h + 1) * X_HD],
                                    memkv_ref[:, h * X_HD:(h + 1) * X_HD],
                                    memkv_ref[:, X_Q + h * X_HD:X_Q + (h + 1) * X_HD]))
        o_ref[...] = jnp.concatenate(outs, axis=1).astype(o_ref.dtype)

    return pl.pallas_call(
        body, grid=(nc,),
        in_specs=[pl.BlockSpec((CHUNK, IN_BP), lambda n: (n, 0)),
                  pl.BlockSpec((HALO, B_QKV), lambda n: (jnp.maximum(n * (CHUNK // HALO) - 1, 0), 0)),
                  pl.BlockSpec((HALO, B_QKV), lambda n: (0, 0)),
                  pl.BlockSpec((1, 128), lambda n: (0, 0)), pl.BlockSpec((1, 128), lambda n: (0, 0)),
                  pl.BlockSpec((MEM_LEN, 2 * X_Q), lambda n: (0, 0))],
        out_specs=[pl.BlockSpec((CHUNK, D), lambda n: (n, 0)),
                   pl.BlockSpec((1, B_V_HEADS, B_HD, B_HD), lambda n: (n, 0, 0, 0))],
        out_shape=[_SDS((s, D), _ACT), _SDS((nc, B_V_HEADS, B_HD, B_HD), F32)],
        scratch_shapes=[pltpu.VMEM((B_V_HEADS, B_HD, B_HD), F32)],
        name=name, compiler_params=_cp("arbitrary"))(proj, proj, conv_w, par, ng, memkv)


def _mix_b_bwd(proj, conv_w, par, ng, memkv, states, dmix, name):
    s = proj.shape[0]
    nc = s // CHUNK
    rep = B_V_HEADS // B_QK_HEADS
    ext_rows = CHUNK + HALO

    def body(cur_ref, prev_ref, w_ref, par_ref, ng_ref, memkv_ref, st_ref, dmix_ref,
             dproj_ref, dw_ref, dpar_ref, dng_ref, dmemkv_ref, dstate_ref, carry_ref):
        t = pl.program_id(0)
        n = nc - 1 - t

        @pl.when(t == 0)
        def _():
            dstate_ref[...] = jnp.zeros_like(dstate_ref)
            carry_ref[...] = jnp.zeros_like(carry_ref)
            dw_ref[...] = jnp.zeros_like(dw_ref)
            dpar_ref[...] = jnp.zeros_like(dpar_ref)
            dng_ref[...] = jnp.zeros_like(dng_ref)
            dmemkv_ref[...] = jnp.zeros_like(dmemkv_ref)

        lane = lax.broadcasted_iota(jnp.int32, (1, 128), 1)
        prev = jnp.where(n > 0, prev_ref[...], 0.0)
        ext = jnp.concatenate([prev, cur_ref[:, :B_QKV]], axis=0)
        y = _dn_conv(ext, w_ref)[HALO:]
        dyq = [None] * B_QK_HEADS
        dyk = [None] * B_QK_HEADS
        dyv, dz = [], []
        dgate = jnp.zeros((CHUNK, 128), F32)
        dpar = jnp.zeros((1, 128), F32)
        dng = jnp.zeros((1, 128), F32)
        for hv in range(B_V_HEADS):
            hq = hv // rep
            _, vjp = jax.vjp(_dn_head, *_dn_head_args(y, cur_ref, par_ref, ng_ref, hv), st_ref[0, hv])
            gq, gk, gv, gz, gbl, gal, ga_log, gdtb, gng, gs0 = vjp(
                (dmix_ref[:, hv * B_HD:(hv + 1) * B_HD].astype(F32), dstate_ref[hv]))
            dstate_ref[hv] = gs0
            dyq[hq] = gq if dyq[hq] is None else dyq[hq] + gq
            dyk[hq] = gk if dyk[hq] is None else dyk[hq] + gk
            dyv.append(gv)
            dz.append(gz)
            dgate = dgate + jnp.where(lane == hv, gbl, 0.0) + jnp.where(lane == B_V_HEADS + hv, gal, 0.0)
            dpar = dpar + jnp.where(lane == hv, ga_log, 0.0) + jnp.where(lane == B_V_HEADS + hv, gdtb, 0.0)
            dng = dng + gng
        dpar_ref[...] += dpar
        dng_ref[...] += dng
        dxq, dmk, dmv = [], [], []
        for h in range(X_HEADS):
            _, vjp = jax.vjp(_cross_head, cur_ref[:, BP_XQ + h * X_HD:BP_XQ + (h + 1) * X_HD],
                             memkv_ref[:, h * X_HD:(h + 1) * X_HD],
                             memkv_ref[:, X_Q + h * X_HD:X_Q + (h + 1) * X_HD])
            gq, gk, gv = vjp(dmix_ref[:, B_V + h * X_HD:B_V + (h + 1) * X_HD].astype(F32))
            dxq.append(gq)
            dmk.append(gk)
            dmv.append(gv)
        dmemkv_ref[...] += jnp.concatenate(dmk + dmv, axis=1)
        dy = jnp.concatenate(dyq + dyk + dyv, axis=1)
        dy_ext = jnp.concatenate([jnp.zeros((HALO, B_QKV), F32), dy], axis=0)
        dext = dy_ext * w_ref[B_CONV - 1:B_CONV, :]
        dw_ref[B_CONV - 1:B_CONV, :] += jnp.sum(ext * dy_ext, axis=0, keepdims=True)
        for j in range(B_CONV - 1):
            sh = B_CONV - 1 - j
            dw_ref[j:j + 1, :] += jnp.sum(pltpu.roll(ext, sh, 0) * dy_ext, axis=0, keepdims=True)
            dext = dext + w_ref[j:j + 1, :] * pltpu.roll(dy_ext, ext_rows - sh, 0)
        tail = jnp.concatenate([jnp.zeros((CHUNK - HALO, B_QKV), F32), carry_ref[...]], axis=0)
        dqkv = dext[HALO:] + tail
        carry_ref[...] = dext[:HALO]
        dproj_ref[...] = jnp.concatenate([dqkv] + dz + dxq + [dgate], axis=1)

    return pl.pallas_call(
        body, grid=(nc,),
        in_specs=[pl.BlockSpec((CHUNK, IN_BP), lambda t: (nc - 1 - t, 0)),
                  pl.BlockSpec((HALO, B_QKV), lambda t: (jnp.maximum((nc - 1 - t) * (CHUNK // HALO) - 1, 0), 0)),
                  pl.BlockSpec((HALO, B_QKV), lambda t: (0, 0)),
                  pl.BlockSpec((1, 128), lambda t: (0, 0)), pl.BlockSpec((1, 128), lambda t: (0, 0)),
                  pl.BlockSpec((MEM_LEN, 2 * X_Q), lambda t: (0, 0)),
                  pl.BlockSpec((1, B_V_HEADS, B_HD, B_HD), lambda t: (nc - 1 - t, 0, 0, 0)),
                  pl.BlockSpec((CHUNK, D), lambda t: (nc - 1 - t, 0))],
        out_specs=[pl.BlockSpec((CHUNK, IN_BP), lambda t: (nc - 1 - t, 0)),
                   pl.BlockSpec((HALO, B_QKV), lambda t: (0, 0)),
                   pl.BlockSpec((1, 128), lambda t: (0, 0)), pl.BlockSpec((1, 128), lambda t: (0, 0)),
                   pl.BlockSpec((MEM_LEN, 2 * X_Q), lambda t: (0, 0))],
        out_shape=[_SDS((s, IN_BP), F32), _SDS((HALO, B_QKV), F32), _SDS((1, 128), F32), _SDS((1, 128), F32),
                   _SDS((MEM_LEN, 2 * X_Q), F32)],
        scratch_shapes=[pltpu.VMEM((B_V_HEADS, B_HD, B_HD), F32), pltpu.VMEM((HALO, B_QKV), F32)],
        name=name, compiler_params=_cp("arbitrary"))(proj, proj, conv_w, par, ng, memkv, states, dmix)


def _place():
    return lax.axis_index("x"), lax.axis_index("y"), lax.axis_index("c")


def _all_gather(shards, name):
    n = len(shards)

    def body(*refs):
        ins, outs = refs[:n], refs[n:2 * n]
        send_sems, recv_sems, local_sems = refs[2 * n:]
        x, y, c = _place()
        me, sibling = (x, y, c), (x, y, 1 - c)
        chips = [(1 - x, y), (x, 1 - y), (1 - x, 1 - y)]

        def rows(a, px, py, pc):
            return outs[a].at[4 * px + 2 * py + pc]

        def copy(a, k, block, to, src=None):
            return pltpu.make_async_remote_copy(
                src_ref=rows(a, *block) if src is None else src, dst_ref=rows(a, *block),
                send_sem=send_sems.at[a, k], recv_sem=recv_sems.at[a, k],
                device_id=to, device_id_type=pl.DeviceIdType.MESH)

        mine = [pltpu.make_async_copy(ins[a], rows(a, *me), local_sems.at[a]) for a in range(n)]
        for cp in mine:
            cp.start()
        first = []
        for a in range(n):
            first.append(copy(a, 0, me, sibling, src=ins[a]))
            first += [copy(a, 1 + j, me, (*chip, c), src=ins[a]) for j, chip in enumerate(chips)]
        for cp in first:
            cp.start()
        passed = []
        for j, chip in enumerate(chips):
            for a in range(n):
                copy(a, 1 + j, (*chip, c), me).wait_recv()
                fwd = copy(a, 4 + j, (*chip, c), sibling)
                fwd.start()
                passed.append(fwd)
        for a in range(n):
            copy(a, 0, sibling, me).wait_recv()
            for j, chip in enumerate(chips):
                copy(a, 4 + j, (*chip, 1 - c), me).wait_recv()
        for cp in first + passed:
            cp.wait_send()
        for cp in mine:
            cp.wait()

    hbm = pl.BlockSpec(memory_space=pl.ANY)
    return pl.pallas_call(
        body, out_shape=[_SDS((N_DEV,) + s.shape, s.dtype) for s in shards],
        in_specs=[hbm] * n, out_specs=[hbm] * n,
        scratch_shapes=[pltpu.SemaphoreType.DMA((n, 7)), pltpu.SemaphoreType.DMA((n, 7)), pltpu.SemaphoreType.DMA((n,))],
        name=name)(*shards)


def _exchange(items, out_shapes, name):
    n = len(items)
    n_out = len(out_shapes)

    def body(*refs):
        srcs, outs = refs[:n], refs[n:n + n_out]
        send_sems, recv_sems, local_sems = refs[n + n_out:]
        x, y, c = _place()
        my = 4 * x + 2 * y + c

        def src_for(a, dest):
            return srcs[a].at[dest] if items[a][3] else srcs[a]

        def slot(a, source):
            o, layer = items[a][1], items[a][2]
            return outs[o].at[source] if layer is None else outs[o].at[source, layer]

        mine = [pltpu.make_async_copy(src_for(a, my), slot(a, my), local_sems.at[a]) for a in range(n)]
        for cp in mine:
            cp.start()
        sends, recvs = [], []
        for k in range(N_DEV - 1):
            px, py, pc = x ^ ((k + 1) >> 2 & 1), y ^ ((k + 1) >> 1 & 1), c ^ ((k + 1) & 1)
            peer = 4 * px + 2 * py + pc
            for a in range(n):
                kw = dict(send_sem=send_sems.at[a, k], recv_sem=recv_sems.at[a, k],
                          device_id=(px, py, pc), device_id_type=pl.DeviceIdType.MESH)
                sends.append(pltpu.make_async_remote_copy(src_ref=src_for(a, peer), dst_ref=slot(a, my), **kw))
                recvs.append(pltpu.make_async_remote_copy(src_ref=src_for(a, my), dst_ref=slot(a, peer), **kw))
        for cp in sends:
            cp.start()
        for cp in recvs:
            cp.wait_recv()
        for cp in sends:
            cp.wait_send()
        for cp in mine:
            cp.wait()

    hbm = pl.BlockSpec(memory_space=pl.ANY)
    return pl.pallas_call(
        body, out_shape=[_SDS(s, d) for s, d in out_shapes],
        in_specs=[hbm] * n, out_specs=[hbm] * n_out,
        scratch_shapes=[pltpu.SemaphoreType.DMA((n, 7)), pltpu.SemaphoreType.DMA((n, 7)), pltpu.SemaphoreType.DMA((n,))],
        name=name)(*[it[0] for it in items])


def _adam_update(g, w, m, v):
    c1 = 1.0 - ADAM_B1 ** ADAM_STEP
    c2 = 1.0 - ADAM_B2 ** ADAM_STEP
    mm = ADAM_B1 * m + (1.0 - ADAM_B1) * g
    vv = ADAM_B2 * v + (1.0 - ADAM_B2) * (g * g)
    delta = -ADAM_LR * ((mm / c1) / (jnp.sqrt(vv / c2) + ADAM_EPS) + ADAM_WD * w)
    return delta, mm, vv


def _sum_sources(p_ref):
    g = p_ref[0].astype(F32)
    for s in range(1, N_DEV):
        g = g + p_ref[s].astype(F32)
    return g


def _adamw(parts, w, m, v, tr, name, restore_b=False):
    nl, r, c = w.shape
    cp = parts.shape[-1]

    def body(p_ref, w_ref, m_ref, v_ref, g_ref, d_ref, nm_ref, nv_ref):
        g = _sum_sources(p_ref)
        if restore_b:
            g = jnp.concatenate([g[:, :BP_XQ], g[:, BP_GATE:BP_GATE + 2 * B_V_HEADS], g[:, BP_XQ:BP_GATE]], axis=1)
        delta, mm, vv = _adam_update(g, w_ref[...], m_ref[...], v_ref[...])
        g_ref[...] = g
        d_ref[...] = delta
        nm_ref[...] = mm
        nv_ref[...] = vv

    spec = pl.BlockSpec((None, tr, c), lambda l, i: (l, i, 0))
    return pl.pallas_call(
        body, grid=(nl, r // tr),
        in_specs=[pl.BlockSpec((N_DEV, None, tr, cp), lambda l, i: (0, l, i, 0)), spec, spec, spec],
        out_specs=[spec] * 4, out_shape=[_SDS(w.shape, F32)] * 4,
        name=name, compiler_params=_cp("arbitrary", "arbitrary"))(parts, w, m, v)


def _pack_small(d_rel, d_cb, d_cw, d_qkv, d_mix, d_mem, d_ffn, d_final, d_sinks, d_par, d_ng, name):
    flat = [d_rel, *d_cb, *d_cw, d_qkv, *d_mix, *d_mem, *d_ffn, d_final, d_sinks, d_par, d_ng]
    n = len(flat)

    def body(*refs):
        ins, o_ref = refs[:n], refs[n]
        rel, cb0, cb1, cw0, cw1, qkv, mx0, mx1, me0, me1, ff0, ff1, fin, snk, par, ng = ins
        o_ref[...] = jnp.zeros_like(o_ref)
        o_ref[SP_REL:SP_REL + N_BUCKETS, 0:128] = rel[...]
        for l, (cb, cw) in enumerate(((cb0, cw0), (cb1, cw1))):
            o_ref[SP_CB + l:SP_CB + l + 1, :] = jnp.concatenate([cb[j] for j in range(FF_BLOCKS)], axis=1)
            full = jnp.concatenate([cw[j] for j in range(FF_BLOCKS)], axis=1)
            o_ref[SP_CW + FFN_CONV * l:SP_CW + FFN_CONV * (l + 1), :] = full[:FFN_CONV]
        o_ref[SP_QKV:SP_QKV + B_CONV, 0:B_QKV] = qkv[0:B_CONV, :]
        for base, pair in ((SP_MIX, (mx0, mx1)), (SP_MEM, (me0, me1)), (SP_FFN, (ff0, ff1))):
            for l in range(2):
                o_ref[base + l:base + l + 1, 0:D] = pair[l][...]
        o_ref[SP_FINAL:SP_FINAL + 1, 0:D] = fin[...]
        o_ref[SP_MISC:SP_MISC + 1, 0:128] = snk[...]
        o_ref[SP_MISC:SP_MISC + 1, 128:256] = par[...]
        o_ref[SP_MISC:SP_MISC + 1, 256:384] = ng[...]

    vm = pl.BlockSpec(memory_space=pltpu.VMEM)
    return pl.pallas_call(body, in_specs=[vm] * n, out_specs=vm, out_shape=_SDS((SMALL_ROWS, D_FF), F32), name=name)(*flat)


_SMALL = ["rel_bias", "norm_mix_g", "norm_mem_g", "sinks_a", "a_log_b", "dt_bias_b", "out_norm_g_b", "norm_ffn_g",
          "ffn_conv_b", "final_norm_g", "conv_qkv_b", "ffn_conv_w"]


def _adamw_small(recv, rc_qkv, rc_ffn, ws, ms, vs, name):
    n = len(_SMALL)

    def body(*refs):
        recv_ref, qkv_ref, ffn_ref = refs[:3]
        w_refs, m_refs, v_refs = refs[3:3 + n], refs[3 + n:3 + 2 * n], refs[3 + 2 * n:3 + 3 * n]
        outs = refs[3 + 3 * n:]
        gs = _sum_sources(recv_ref)
        grads = {
            "rel_bias": gs[SP_REL:SP_REL + N_BUCKETS, 0:A_HEADS],
            "norm_mix_g": gs[SP_MIX:SP_MIX + 2, 0:D], "norm_mem_g": gs[SP_MEM:SP_MEM + 2, 0:D],
            "sinks_a": gs[SP_MISC:SP_MISC + 1, 0:A_HEADS],
            "a_log_b": gs[SP_MISC:SP_MISC + 1, 128:128 + B_V_HEADS],
            "dt_bias_b": gs[SP_MISC:SP_MISC + 1, 128 + B_V_HEADS:128 + 2 * B_V_HEADS],
            "out_norm_g_b": gs[SP_MISC:SP_MISC + 1, 256:256 + B_HD],
            "norm_ffn_g": gs[SP_FFN:SP_FFN + 2, 0:D], "ffn_conv_b": gs[SP_CB:SP_CB + 2, :],
            "final_norm_g": gs[SP_FINAL:SP_FINAL + 1, 0:D],
            "conv_qkv_b": _sum_sources(qkv_ref), "ffn_conv_w": _sum_sources(ffn_ref),
        }
        for i, nm in enumerate(_SMALL):
            g = grads[nm]
            delta, mm, vv = _adam_update(g, w_refs[i][...], m_refs[i][...], v_refs[i][...])
            outs[i][...] = g
            outs[n + i][...] = delta
            outs[2 * n + i][...] = mm
            outs[3 * n + i][...] = vv

    vm = pl.BlockSpec(memory_space=pltpu.VMEM)
    shapes = [_SDS(w.shape, F32) for w in ws]
    return pl.pallas_call(
        body, in_specs=[vm] * (3 + 3 * n), out_specs=[vm] * (4 * n), out_shape=shapes * 4,
        name=name)(recv, rc_qkv, rc_ffn, *ws, *ms, *vs)


def _assemble(gathered, axis):
    g = jnp.moveaxis(gathered, 0, axis)
    shp = list(g.shape)
    return g.reshape(shp[:axis] + [shp[axis] * shp[axis + 1]] + shp[axis + 2:])


def _pad_rows(a, rows):
    return jnp.pad(a, ((0, rows - a.shape[0]), (0, 0)))


def _pad_lanes(a, lanes=128):
    return jnp.pad(a, ((0, 0), (0, lanes - a.shape[1])))


def _ff_blocks(a):
    return jnp.moveaxis(a.reshape(a.shape[0], FF_BLOCKS, GU_SHARD), 1, 0)


def _reorder_b(w):
    qkv_z = w[..., :B_QKV + B_V]
    gates = w[..., B_QKV + B_V:B_QKV + B_V + 2 * B_V_HEADS]
    xq = w[..., IN_B - X_Q:]
    pad = jnp.zeros(w.shape[:-1] + (IN_BP - IN_B,), w.dtype)
    return jnp.concatenate([qkv_z, xq, gates, pad], axis=-1)


def kernel(x, mem, rel_bias, norm_mix_g, norm_mem_g, w_mem_kv, w_out, w_in_a, sinks_a, w_in_b, conv_qkv_b, a_log_b, dt_bias_b, out_norm_g_b, norm_ffn_g, w_gate_up, ffn_conv_w, ffn_conv_b, w_down, final_norm_g, loss_target, m_rel_bias, m_norm_mix_g, m_norm_mem_g, m_w_mem_kv, m_w_out, m_w_in_a, m_sinks_a, m_w_in_b, m_conv_qkv_b, m_a_log_b, m_dt_bias_b, m_out_norm_g_b, m_norm_ffn_g, m_w_gate_up, m_ffn_conv_w, m_ffn_conv_b, m_w_down, m_final_norm_g, v_rel_bias, v_norm_mix_g, v_norm_mem_g, v_w_mem_kv, v_w_out, v_w_in_a, v_sinks_a, v_w_in_b, v_conv_qkv_b, v_a_log_b, v_dt_bias_b, v_out_norm_g_b, v_norm_ffn_g, v_w_gate_up, v_ffn_conv_w, v_ffn_conv_b, v_w_down, v_final_norm_g):
    local = dict(locals())
    order = ["rel_bias", "norm_mix_g", "norm_mem_g", "w_mem_kv", "w_out", "w_in_a", "sinks_a", "w_in_b", "conv_qkv_b",
             "a_log_b", "dt_bias_b", "out_norm_g_b", "norm_ffn_g", "w_gate_up", "ffn_conv_w", "ffn_conv_b", "w_down",
             "final_norm_g"]
    wts = {n: local[n] for n in order}
    moms = {n: local["m_" + n] for n in order}
    vars_ = {n: local["v_" + n] for n in order}
    h0 = x[0]
    memx = mem[0]
    tgt = loss_target[0]
    s = h0.shape[0]
    tm = _rows(s)

    g_mk, g_out, g_ia, g_ib, g_gu, g_dn, g_cq, g_cw = _all_gather(
        [w_mem_kv.astype(_MXU), w_out.astype(_MXU), w_in_a.astype(_MXU), _reorder_b(w_in_b).astype(_MXU),
         w_gate_up.astype(_MXU), w_down.astype(_MXU), conv_qkv_b, ffn_conv_w], "gather_weights")
    w_ia = _assemble(g_ia, 2)[0]
    conv_qkv = _pad_rows(_assemble(g_cq, 2)[0], HALO)
    ffn_cw_full = _assemble(g_cw, 2)
    ffn_cw = [_ff_blocks(_pad_rows(ffn_cw_full[i], HALO)) for i in range(2)]
    ffn_cb = [_ff_blocks(ffn_conv_b[i:i + 1]) for i in range(2)]
    bucket = jnp.asarray(_bucket_table())
    bias = _bias_build(rel_bias, bucket, "bias_build")
    sinks = _pad_lanes(sinks_a)
    par_b = _pad_lanes(jnp.concatenate([a_log_b, dt_bias_b], axis=1))

    row_x = pl.BlockSpec((tm, D), lambda i, j: (i, 0))
    gu_shape = (2, FF_BLOCKS, s, GU_SHARD)

    def in_proj(h, g, w, w_spec, n_cols, tn, name):
        return _norm_matmul(h, g, w, w_spec, n_cols // tn, (h.shape[0], n_cols),
                            pl.BlockSpec((_rows(h.shape[0]), tn), lambda i, j: (i, j)), name)

    def ffn_fwd(i, h):
        gu, hn = _norm_matmul(h, norm_ffn_g[i:i + 1], g_gu, _spec_gate_up(i, 1), N_DEV, gu_shape,
                              _spec_gu_act(0, 1, tm), f"gate_up_{i}")
        act = _glu_fwd(gu, ffn_cw[i], ffn_cb[i], f"glu_fwd_{i}")
        h_new = _matmul_res(act, pl.BlockSpec((None, tm, GU_SHARD), lambda r, j: (j, r, 0)), g_dn, _spec_down(i, 1),
                            FF_BLOCKS, h, f"down_proj_{i}")
        return h_new, gu, hn, act

    def out_proj(i, mix, h):
        return _matmul_res(mix, row_x, g_out, _spec_rowsharded(i, D // N_DEV, D), 1, h, f"out_proj_{i}")

    proj_a, hn_a = in_proj(h0, norm_mix_g[0:1], w_ia, pl.BlockSpec((D, 640), lambda i, j: (0, j)), IN_A, 640, "in_proj_a")
    memkv0, memn0 = in_proj(memx, norm_mem_g[0:1], g_mk, _spec_rowsharded(0, D // N_DEV, 2 * X_Q), 2 * X_Q, 2 * X_Q, "mem_proj_0")
    mix_a = _mix_a_fwd(proj_a, bias, sinks, memkv0, "mix_a_fwd")
    h1 = out_proj(0, mix_a, h0)
    h2, gu0, hn_f0, act0 = ffn_fwd(0, h1)
    proj_b, hn_b = in_proj(h2, norm_mix_g[1:2], g_ib, _spec_rowsharded(0, D // N_DEV, 896, col_block=1), IN_BP, 896, "in_proj_b")
    memkv1, memn1 = in_proj(memx, norm_mem_g[1:2], g_mk, _spec_rowsharded(1, D // N_DEV, 2 * X_Q), 2 * X_Q, 2 * X_Q, "mem_proj_1")
    mix_b, states = _mix_b_fwd(proj_b, conv_qkv, par_b, out_norm_g_b, memkv1, "mix_b_fwd")
    h3 = out_proj(1, mix_b, h2)
    h4, gu1, hn_f1, act1 = ffn_fwd(1, h3)
    loss_row, dh, d_final_g = _loss_head(h4, final_norm_g[None, :], tgt, "loss_head")

    zeros_mem = jnp.zeros_like(memx)

    def ffn_bwd(i, dh, h_in, gu, hn_f, act):
        dact = _matmul_nt(dh, g_dn, _spec_down(i, 1), FF_BLOCKS, (FF_BLOCKS, s, GU_SHARD),
                          pl.BlockSpec((None, tm, GU_SHARD), lambda r, j: (j, r, 0)), f"d_act_{i}")
        d_wdown = _matmul_tn(act, pl.BlockSpec((None, tm, GU_SHARD), lambda j, r: (j, r, 0)),
                             dh, pl.BlockSpec((tm, D), lambda j, r: (r, 0)), s, FF_BLOCKS, (GU_SHARD, D),
                             (N_DEV, DN_SHARD, D), pl.BlockSpec((2, DN_SHARD, D), lambda j, r: (j, 0, 0)), f"d_w_down_{i}")
        dgu, d_cw, d_cb = _glu_bwd(gu, ffn_cw[i], ffn_cb[i], dact, f"glu_bwd_{i}")
        dh_new, d_g = _matmul_nt_normbwd(dgu, _spec_gu_act(0, 1, tm), g_gu, _spec_gate_up(i, 1), N_DEV, h_in,
                                         norm_ffn_g[i:i + 1], dh, f"d_ffn_in_{i}")
        d_wgu = _matmul_tn(hn_f, pl.BlockSpec((tm, D), lambda j, r: (r, 0)), dgu, _spec_gu_act(1, 0, tm), s, N_DEV,
                           (D, GU_SHARD), (N_DEV, D, GU_SHARD), pl.BlockSpec((None, D, GU_SHARD), lambda j, r: (j, 0, 0)),
                           f"d_w_gate_up_{i}")
        return dh_new, d_wdown, d_wgu, d_cw, d_cb, d_g

    def out_bwd(i, dh, mix):
        dmix = _matmul_nt(dh, g_out, _spec_rowsharded(i, D // N_DEV, D), 1, (s, D), row_x, f"d_mix_{i}")
        d_wout = _matmul_tn(mix, pl.BlockSpec((tm, D), lambda j, r: (r, 0)), dh, pl.BlockSpec((tm, D), lambda j, r: (r, 0)),
                            s, 1, (D, D), (N_DEV, D // N_DEV, D), pl.BlockSpec((N_DEV, D // N_DEV, D), lambda j, r: (0, 0, 0)),
                            f"d_w_out_{i}")
        return dmix, d_wout

    def mem_bwd(i, dmemkv, memn):
        tmm = _rows(MEM_LEN)
        _, d_g = _matmul_nt_normbwd(dmemkv, pl.BlockSpec((tmm, 2 * X_Q), lambda r, j: (r, 0)), g_mk,
                                    _spec_rowsharded(i, D // N_DEV, 2 * X_Q), 1, memx, norm_mem_g[i:i + 1], zeros_mem,
                                    f"d_mem_in_{i}")
        by_row = lambda j, r: (r, 0)
        d_w = _matmul_tn(memn, pl.BlockSpec((tmm, D), by_row), dmemkv, pl.BlockSpec((tmm, 2 * X_Q), by_row), MEM_LEN, 1,
                         (D, 2 * X_Q), (N_DEV, D // N_DEV, 2 * X_Q),
                         pl.BlockSpec((N_DEV, D // N_DEV, 2 * X_Q), lambda j, r: (0, 0, 0)), f"d_w_mem_kv_{i}")
        return d_w, d_g

    dh, d_wdown1, d_wgu1, d_cw1, d_cb1, d_gf1 = ffn_bwd(1, dh, h3, gu1, hn_f1, act1)
    dmix, d_wout1 = out_bwd(1, dh, mix_b)
    dproj_b, d_convw, d_par, d_ng, dmemkv1 = _mix_b_bwd(proj_b, conv_qkv, par_b, out_norm_g_b, memkv1, states, dmix, "mix_b_bwd")
    dh, d_gm1 = _matmul_nt_normbwd(dproj_b, pl.BlockSpec((tm, 896), lambda i, j: (i, j)), g_ib,
                                   _spec_rowsharded(0, D // N_DEV, 896, col_block=1), IN_BP // 896, h2, norm_mix_g[1:2], dh, "d_in_b")
    d_wib = _matmul_tn(hn_b, pl.BlockSpec((tm, D), lambda j, r: (r, 0)), dproj_b, pl.BlockSpec((tm, 896), lambda j, r: (r, j)),
                       s, IN_BP // 896, (D, 896), (N_DEV, D // N_DEV, IN_BP),
                       pl.BlockSpec((N_DEV, D // N_DEV, 896), lambda j, r: (0, 0, j)), "d_w_in_b")
    d_wmk1, d_gmem1 = mem_bwd(1, dmemkv1, memn1)
    dh, d_wdown0, d_wgu0, d_cw0, d_cb0, d_gf0 = ffn_bwd(0, dh, h1, gu0, hn_f0, act0)
    dmix, d_wout0 = out_bwd(0, dh, mix_a)
    dproj_a, dbias, dsinks, dmemkv0 = _mix_a_bwd(proj_a, bias, sinks, memkv0, dmix, "mix_a_bwd")
    dh, d_gm0 = _matmul_nt_normbwd(dproj_a, pl.BlockSpec((tm, 640), lambda i, j: (i, j)), w_ia,
                                   pl.BlockSpec((D, 640), lambda i, j: (0, j)), IN_A // 640, h0, norm_mix_g[0:1], dh, "d_in_a")
    d_wia = _matmul_tn(hn_a, pl.BlockSpec((tm, D), lambda j, r: (r, 0)), dproj_a, pl.BlockSpec((tm, IN_A), lambda j, r: (r, 0)),
                       s, 1, (D, IN_A), (N_DEV, D, IA_SHARD), pl.BlockSpec((N_DEV, D, IA_SHARD), lambda j, r: (0, 0, 0)),
                       "d_w_in_a", split=IA_SHARD)
    d_wmk0, d_gmem0 = mem_bwd(0, dmemkv0, memn0)
    d_rel = _bias_reduce(dbias, bucket, "bias_reduce")

    small = _pack_small(d_rel, (d_cb0, d_cb1), (d_cw0, d_cw1), d_convw, (d_gm0, d_gm1), (d_gmem0, d_gmem1),
                        (d_gf0, d_gf1), d_final_g, dsinks, d_par, d_ng, "pack_small")
    items = [(d_wmk0, 0, 0, True), (d_wmk1, 0, 1, True), (d_wout0, 1, 0, True), (d_wout1, 1, 1, True),
             (d_wia, 2, 0, True), (d_wib, 3, 0, True), (d_wgu0, 4, 0, True), (d_wgu1, 4, 1, True),
             (d_wdown0, 5, 0, True), (d_wdown1, 5, 1, True), (small, 6, None, False)]
    out_shapes = [((N_DEV, 2, D // N_DEV, 2 * X_Q), _WIRE), ((N_DEV, 2, D // N_DEV, D), _WIRE),
                  ((N_DEV, 1, D, IA_SHARD), _WIRE), ((N_DEV, 1, D // N_DEV, IN_BP), _WIRE),
                  ((N_DEV, 2, D, GU_SHARD), _WIRE), ((N_DEV, 2, DN_SHARD, D), _WIRE), ((N_DEV, SMALL_ROWS, D_FF), F32)]
    r_mk, r_out, r_ia, r_ib, r_gu, r_dn, r_small = _exchange(items, out_shapes, "exchange_grads")

    res = {}
    for nm, parts, tr, restore in (("w_mem_kv", r_mk, 128, False), ("w_out", r_out, 128, False), ("w_in_a", r_ia, 512, False),
                                  ("w_in_b", r_ib, 32, True), ("w_gate_up", r_gu, 128, False), ("w_down", r_dn, 176, False)):
        res[nm] = _adamw(parts, wts[nm], moms[nm], vars_[nm], tr, "adamw_" + nm, restore_b=restore)

    my = 4 * lax.axis_index("x") + 2 * lax.axis_index("y") + lax.axis_index("c")
    cq = conv_qkv_b.shape[-1]
    cf = ffn_conv_w.shape[-1]
    rc_qkv = lax.dynamic_slice_in_dim(r_small[:, SP_QKV:SP_QKV + B_CONV, :B_QKV], my * cq, cq, axis=2)[:, None]
    rc_ffn = lax.dynamic_slice_in_dim(r_small[:, SP_CW:SP_CW + 2 * FFN_CONV, :], my * cf, cf, axis=2).reshape(N_DEV, 2, FFN_CONV, cf)
    as2d = lambda a: a[None, :] if a.ndim == 1 else a
    small_out = _adamw_small(r_small, rc_qkv, rc_ffn, [as2d(wts[n]) for n in _SMALL], [as2d(moms[n]) for n in _SMALL],
                             [as2d(vars_[n]) for n in _SMALL], "adamw_small")
    ns = len(_SMALL)
    for i, nm in enumerate(_SMALL):
        res[nm] = [small_out[k * ns + i].reshape(wts[nm].shape) for k in range(4)]

    loss = lax.psum(loss_row[0, 0], AXES)
    return (loss, dh[None], *[res[n][0] for n in order], *[res[n][1] for n in order],
            *[res[n][2] for n in order], *[res[n][3] for n in order])
```

```python
import functools
import math

import numpy as np

import jax
import jax.numpy as jnp
from jax import lax
from jax.experimental import pallas as pl
from jax.experimental.pallas import tpu as pltpu

F32 = jnp.float32
_MXU = jnp.bfloat16
_ACT = jnp.bfloat16
_WIRE = jnp.bfloat16
_HI = lax.Precision.HIGH
_TM = 512
_VMEM_LIMIT = 48 * 1024 * 1024
_SDS = jax.ShapeDtypeStruct

D = 1024
EPS = 1e-6
A_HEADS, A_KV_HEADS, A_HD, BLK = 12, 2, 64, 128
N_BUCKETS, MAX_DISTANCE = 32, 128
B_QK_HEADS, B_V_HEADS, B_HD, B_CONV, CHUNK = 3, 6, 128, 4, 64
X_HEADS, X_HD, MEM_LEN = 4, 64, 256
D_FF, FFN_CONV = 2816, 3
A_Q, A_KV, X_Q = 768, 128, 256
B_QK, B_V, B_QKV = 384, 768, 1536
IN_A, IN_B = 1280, 2572
IN_BP = 2688
BP_Z, BP_XQ, BP_GATE = 1536, 2304, 2560
HALO = 8

N_DEV = 8
AXES = ("x", "y", "c")
GU_SHARD = 2 * D_FF // N_DEV
FF_BLOCKS = D_FF // GU_SHARD
DN_SHARD = D_FF // N_DEV
IA_SHARD = IN_A // N_DEV

ADAM_LR, ADAM_B1, ADAM_B2, ADAM_EPS, ADAM_WD, ADAM_STEP = 0.001, 0.9, 0.999, 1e-08, 0.01, 10

SP_REL, SP_CB, SP_CW, SP_QKV, SP_MIX, SP_MEM, SP_FFN, SP_FINAL, SP_MISC, SMALL_ROWS = 0, 32, 34, 40, 44, 46, 48, 50, 51, 56


def _cp(*sems):
    return pltpu.CompilerParams(dimension_semantics=sems, vmem_limit_bytes=_VMEM_LIMIT)


def _mm(a, b):
    return jnp.dot(a.astype(_MXU), b.astype(_MXU), preferred_element_type=F32)


def _mm_nt(a, b):
    return lax.dot_general(a.astype(_MXU), b.astype(_MXU), (((1,), (1,)), ((), ())), preferred_element_type=F32)


def _mm_tn(a, b):
    return lax.dot_general(a.astype(_MXU), b.astype(_MXU), (((0,), (0,)), ((), ())), preferred_element_type=F32)


def _mmf(a, b):
    return jnp.dot(a, b, preferred_element_type=F32, precision=_HI)


def _mmf_nt(a, b):
    return lax.dot_general(a, b, (((1,), (1,)), ((), ())), preferred_element_type=F32, precision=_HI)


def _mmf_tn(a, b):
    return lax.dot_general(a, b, (((0,), (0,)), ((), ())), preferred_element_type=F32, precision=_HI)


def _silu(x):
    return x * jax.nn.sigmoid(x)


def _w2d(ref):
    v = ref[...]
    return v.reshape(-1, v.shape[-1])


def _rows(m):
    return min(m, _TM)


def _spec_rowsharded(layer, rows, cols, col_block=None):
    if col_block is None:
        return pl.BlockSpec((N_DEV, None, rows, cols), lambda *_: (0, layer, 0, 0))
    return pl.BlockSpec((N_DEV, None, rows, cols), lambda *ids: (0, layer, 0, ids[col_block]))


def _spec_gate_up(layer, axis):
    return pl.BlockSpec((None, None, D, GU_SHARD), lambda *ids: (ids[axis], layer, 0, 0))


def _spec_down(layer, axis):
    return pl.BlockSpec((2, None, DN_SHARD, D), lambda *ids: (ids[axis], layer, 0, 0))


def _spec_gu_act(row_axis, axis, tm):
    return pl.BlockSpec((None, None, tm, GU_SHARD), lambda *ids: (ids[axis] // FF_BLOCKS, ids[axis] % FF_BLOCKS, ids[row_axis], 0))


def _norm_matmul(x, g, w, w_spec, n_blocks, out_shape, out_spec, name):
    m, k = x.shape
    tm = _rows(m)

    def body(x_ref, g_ref, w_ref, y_ref, hn_ref):
        @pl.when(pl.program_id(1) == 0)
        def _():
            xv = x_ref[...]
            r = lax.rsqrt(jnp.mean(xv * xv, axis=-1, keepdims=True) + EPS)
            hn_ref[...] = (xv * r * g_ref[...]).astype(hn_ref.dtype)

        y_ref[...] = _mm(hn_ref[...], _w2d(w_ref))

    return pl.pallas_call(
        body, grid=(m // tm, n_blocks),
        in_specs=[pl.BlockSpec((tm, k), lambda i, j: (i, 0)), pl.BlockSpec((1, k), lambda i, j: (0, 0)), w_spec],
        out_specs=[out_spec, pl.BlockSpec((tm, k), lambda i, j: (i, 0))],
        out_shape=[_SDS(out_shape, F32), _SDS((m, k), _ACT)],
        name=name, compiler_params=_cp("arbitrary", "arbitrary"))(x, g, w)


def _matmul_res(a, a_spec, w, w_spec, n_k, res, name):
    m, n = res.shape
    tm = _rows(m)

    def body(a_ref, w_ref, r_ref, o_ref):
        part = _mm(a_ref[...], _w2d(w_ref))

        @pl.when(pl.program_id(1) == 0)
        def _():
            o_ref[...] = r_ref[...] + part

        @pl.when(pl.program_id(1) > 0)
        def _():
            o_ref[...] += part

    return pl.pallas_call(
        body, grid=(m // tm, n_k),
        in_specs=[a_spec, w_spec, pl.BlockSpec((tm, n), lambda i, j: (i, 0))],
        out_specs=pl.BlockSpec((tm, n), lambda i, j: (i, 0)),
        out_shape=_SDS((m, n), F32), name=name, compiler_params=_cp("arbitrary", "arbitrary"))(a, w, res)


def _matmul_nt(dy, w, w_spec, n_blocks, out_shape, out_spec, name):
    m, n = dy.shape
    tm = _rows(m)

    def body(dy_ref, w_ref, o_ref):
        o_ref[...] = _mm_nt(dy_ref[...], _w2d(w_ref)).astype(o_ref.dtype)

    return pl.pallas_call(
        body, grid=(m // tm, n_blocks),
        in_specs=[pl.BlockSpec((tm, n), lambda i, j: (i, 0)), w_spec],
        out_specs=out_spec, out_shape=_SDS(out_shape, F32),
        name=name, compiler_params=_cp("arbitrary", "arbitrary"))(dy, w)


def _matmul_nt_normbwd(dy, dy_spec, w, w_spec, nj, h, g, dh_in, name):
    m, k = h.shape
    tm = _rows(m)

    def body(dy_ref, w_ref, h_ref, g_ref, dhin_ref, dh_ref, dg_ref, acc_ref):
        i, j = pl.program_id(0), pl.program_id(1)

        @pl.when(j == 0)
        def _():
            acc_ref[...] = jnp.zeros_like(acc_ref)

        acc_ref[...] += _mm_nt(dy_ref[...], _w2d(w_ref))

        @pl.when(j == nj - 1)
        def _():
            xv = h_ref[...]
            r = lax.rsqrt(jnp.mean(xv * xv, axis=-1, keepdims=True) + EPS)
            xh = xv * r
            dhn = acc_ref[...]
            part = jnp.sum(dhn * xh, axis=0, keepdims=True)

            @pl.when(i == 0)
            def _():
                dg_ref[...] = part

            @pl.when(i > 0)
            def _():
                dg_ref[...] += part

            t = dhn * g_ref[...]
            dh_ref[...] = dhin_ref[...] + r * (t - xh * jnp.mean(t * xh, axis=-1, keepdims=True))

    return pl.pallas_call(
        body, grid=(m // tm, nj),
        in_specs=[dy_spec, w_spec, pl.BlockSpec((tm, k), lambda i, j: (i, 0)), pl.BlockSpec((1, k), lambda i, j: (0, 0)),
                  pl.BlockSpec((tm, k), lambda i, j: (i, 0))],
        out_specs=[pl.BlockSpec((tm, k), lambda i, j: (i, 0)), pl.BlockSpec((1, k), lambda i, j: (0, 0))],
        out_shape=[_SDS((m, k), F32), _SDS((1, k), F32)],
        scratch_shapes=[pltpu.VMEM((tm, k), F32)],
        name=name, compiler_params=_cp("arbitrary", "arbitrary"))(dy, w, h, g, dh_in)


def _matmul_tn(x, x_spec, dy, dy_spec, m, n_blocks, acc_shape, out_shape, out_spec, name, split=None):
    tm = _rows(m)
    nm = m // tm

    def body(x_ref, dy_ref, o_ref, acc_ref):
        @pl.when(pl.program_id(1) == 0)
        def _():
            acc_ref[...] = jnp.zeros_like(acc_ref)

        acc_ref[...] += _mm_tn(x_ref[...], dy_ref[...])

        @pl.when(pl.program_id(1) == nm - 1)
        def _():
            if split is None:
                o_ref[...] = acc_ref[...].reshape(o_ref.shape).astype(o_ref.dtype)
            else:
                for d in range(N_DEV):
                    o_ref[d] = acc_ref[:, d * split:(d + 1) * split].astype(o_ref.dtype)

    return pl.pallas_call(
        body, grid=(n_blocks, nm), in_specs=[x_spec, dy_spec], out_specs=out_spec,
        out_shape=_SDS(out_shape, _WIRE), scratch_shapes=[pltpu.VMEM(acc_shape, F32)],
        name=name, compiler_params=_cp("arbitrary", "arbitrary"))(x, dy)


def _loss_head(h, g, tgt, name):
    m, k = h.shape
    tm = _rows(m)

    def body(h_ref, g_ref, t_ref, loss_ref, dh_ref, dg_ref):
        i = pl.program_id(0)
        xv = h_ref[...]
        r = lax.rsqrt(jnp.mean(xv * xv, axis=-1, keepdims=True) + EPS)
        xh = xv * r
        gv = g_ref[...]
        err = xh * gv - t_ref[...]
        lpart = jnp.zeros((1, 128), F32) + 0.5 * jnp.sum(jnp.mean(err * err, axis=-1, keepdims=True), axis=0, keepdims=True)
        dy = err * (1.0 / k)
        gpart = jnp.sum(dy * xh, axis=0, keepdims=True)

        @pl.when(i == 0)
        def _():
            loss_ref[...] = lpart
            dg_ref[...] = gpart

        @pl.when(i > 0)
        def _():
            loss_ref[...] += lpart
            dg_ref[...] += gpart

        t = dy * gv
        dh_ref[...] = r * (t - xh * jnp.mean(t * xh, axis=-1, keepdims=True))

    return pl.pallas_call(
        body, grid=(m // tm,),
        in_specs=[pl.BlockSpec((tm, k), lambda i: (i, 0)), pl.BlockSpec((1, k), lambda i: (0, 0)),
                  pl.BlockSpec((tm, k), lambda i: (i, 0))],
        out_specs=[pl.BlockSpec((1, 128), lambda i: (0, 0)), pl.BlockSpec((tm, k), lambda i: (i, 0)),
                   pl.BlockSpec((1, k), lambda i: (0, 0))],
        out_shape=[_SDS((1, 128), F32), _SDS((m, k), F32), _SDS((1, k), F32)],
        name=name, compiler_params=_cp("arbitrary"))(h, g, tgt)


def _glu_fwd(gu, conv_w, conv_b, name):
    s = gu.shape[2]
    tm = _rows(s)

    def body(gu_ref, prev_ref, w_ref, b_ref, act_ref):
        i = pl.program_id(0)
        prev = jnp.where(i > 0, prev_ref[...], 0.0)
        ext = jnp.concatenate([prev, gu_ref[0]], axis=0)
        gc = b_ref[...] + w_ref[FFN_CONV - 1:FFN_CONV, :] * ext
        for j in range(FFN_CONV - 1):
            gc = gc + w_ref[j:j + 1, :] * pltpu.roll(ext, FFN_CONV - 1 - j, 0)
        act_ref[...] = (_silu(gc[HALO:]) * gu_ref[1]).astype(act_ref.dtype)

    return pl.pallas_call(
        body, grid=(s // tm, FF_BLOCKS),
        in_specs=[pl.BlockSpec((2, None, tm, GU_SHARD), lambda i, j: (0, j, i, 0)),
                  pl.BlockSpec((None, None, HALO, GU_SHARD), lambda i, j: (0, j, jnp.maximum(i * (tm // HALO) - 1, 0), 0)),
                  pl.BlockSpec((None, HALO, GU_SHARD), lambda i, j: (j, 0, 0)),
                  pl.BlockSpec((None, 1, GU_SHARD), lambda i, j: (j, 0, 0))],
        out_specs=pl.BlockSpec((None, tm, GU_SHARD), lambda i, j: (j, i, 0)),
        out_shape=_SDS((FF_BLOCKS, s, GU_SHARD), _ACT), name=name,
        compiler_params=_cp("arbitrary", "arbitrary"))(gu, gu, conv_w, conv_b)


def _glu_bwd(gu, conv_w, conv_b, dact, name):
    s = gu.shape[2]
    tm = _rows(s)
    nt = s // tm
    ext_rows = tm + HALO

    def body(gu_ref, prev_ref, w_ref, b_ref, dact_ref, dgu_ref, dw_ref, db_ref, carry_ref):
        t = pl.program_id(1)
        i = nt - 1 - t

        @pl.when(t == 0)
        def _():
            carry_ref[...] = jnp.zeros_like(carry_ref)
            dw_ref[...] = jnp.zeros_like(dw_ref)
            db_ref[...] = jnp.zeros_like(db_ref)

        up = gu_ref[1]
        prev = jnp.where(i > 0, prev_ref[...], 0.0)
        ext = jnp.concatenate([prev, gu_ref[0]], axis=0)
        shifted = [pltpu.roll(ext, FFN_CONV - 1 - j, 0) if j < FFN_CONV - 1 else ext for j in range(FFN_CONV)]
        gc = b_ref[...] + shifted[0] * w_ref[0:1, :]
        for j in range(1, FFN_CONV):
            gc = gc + shifted[j] * w_ref[j:j + 1, :]
        gc = gc[HALO:]
        sg = jax.nn.sigmoid(gc)
        da = dact_ref[...]
        dup = da * (gc * sg)
        dgc = da * up * (sg * (1.0 + gc * (1.0 - sg)))
        db_ref[...] += jnp.sum(dgc, axis=0, keepdims=True)
        dgc_ext = jnp.concatenate([jnp.zeros((HALO, GU_SHARD), F32), dgc], axis=0)
        dext = dgc_ext * w_ref[FFN_CONV - 1:FFN_CONV, :]
        for j in range(FFN_CONV):
            dw_ref[j:j + 1, :] += jnp.sum(shifted[j] * dgc_ext, axis=0, keepdims=True)
            if j < FFN_CONV - 1:
                dext = dext + w_ref[j:j + 1, :] * pltpu.roll(dgc_ext, ext_rows - (FFN_CONV - 1 - j), 0)
        tail = jnp.concatenate([jnp.zeros((tm - HALO, GU_SHARD), F32), carry_ref[...]], axis=0)
        dgate = dext[HALO:] + tail
        carry_ref[...] = dext[:HALO]
        dgu_ref[0] = dgate.astype(dgu_ref.dtype)
        dgu_ref[1] = dup.astype(dgu_ref.dtype)

    return pl.pallas_call(
        body, grid=(FF_BLOCKS, nt),
        in_specs=[pl.BlockSpec((2, None, tm, GU_SHARD), lambda j, t: (0, j, nt - 1 - t, 0)),
                  pl.BlockSpec((None, None, HALO, GU_SHARD),
                               lambda j, t: (0, j, jnp.maximum((nt - 1 - t) * (tm // HALO) - 1, 0), 0)),
                  pl.BlockSpec((None, HALO, GU_SHARD), lambda j, t: (j, 0, 0)),
                  pl.BlockSpec((None, 1, GU_SHARD), lambda j, t: (j, 0, 0)),
                  pl.BlockSpec((None, tm, GU_SHARD), lambda j, t: (j, nt - 1 - t, 0))],
        out_specs=[pl.BlockSpec((2, None, tm, GU_SHARD), lambda j, t: (0, j, nt - 1 - t, 0)),
                   pl.BlockSpec((None, HALO, GU_SHARD), lambda j, t: (j, 0, 0)),
                   pl.BlockSpec((None, 1, GU_SHARD), lambda j, t: (j, 0, 0))],
        out_shape=[_SDS(gu.shape, _ACT), _SDS((FF_BLOCKS, HALO, GU_SHARD), F32), _SDS((FF_BLOCKS, 1, GU_SHARD), F32)],
        scratch_shapes=[pltpu.VMEM((HALO, GU_SHARD), F32)],
        name=name, compiler_params=_cp("arbitrary", "arbitrary"))(gu, gu, conv_w, conv_b, dact)


def _bucket_table():
    qi = np.arange(BLK)[:, None]
    kj = np.arange(BLK)[None, :]
    n = np.where(kj > qi, BLK + qi - kj, qi - kj)
    max_exact = N_BUCKETS // 2
    nf = np.maximum(n, 1).astype(np.float32)
    large = max_exact + (np.log(nf / max_exact) / math.log(MAX_DISTANCE / max_exact)
                         * (N_BUCKETS - max_exact)).astype(np.int32)
    large = np.minimum(large, N_BUCKETS - 1)
    return np.where(n < max_exact, n, large).astype(np.int32)


def _lane_low():
    return lax.broadcasted_iota(jnp.int32, (1, 128), 1) < A_HD


def _swa_group(q, kd, vd, sink, bias, upper, first):
    n = A_HEADS // A_KV_HEADS
    low = _lane_low()
    pairs = [q[:, p * 128:(p + 1) * 128] for p in range(n // 2)]
    qm = jnp.concatenate([jnp.where(low == (h % 2 == 0), pairs[h // 2], 0.0) for h in range(n)], axis=0)
    s2 = _mm_nt(qm, kd) * (A_HD ** -0.5)
    s = jnp.where(upper[None], s2[:, :BLK].reshape(n, BLK, BLK), s2[:, BLK:].reshape(n, BLK, BLK)) + bias
    s = jnp.where((upper & first)[None], -jnp.inf, s)
    m = jnp.maximum(jnp.max(s, axis=-1, keepdims=True), sink)
    p = jnp.exp(s - m)
    split = jnp.concatenate([jnp.where(upper[None], p, 0.0), jnp.where(upper[None], 0.0, p)], axis=-1)
    split = split.reshape(n * BLK, 2 * BLK)
    den = _mm(p.reshape(n * BLK, BLK), jnp.ones((BLK, 128), F32)) + jnp.exp(sink - m).reshape(n * BLK, 1)
    o = _mm(split, vd) / den
    return jnp.concatenate([jnp.where(low, o[2 * p * BLK:(2 * p + 1) * BLK], o[(2 * p + 1) * BLK:(2 * p + 2) * BLK])
                            for p in range(n // 2)], axis=1)


def _swa_sinks(sink_ref, g):
    n = A_HEADS // A_KV_HEADS
    return jnp.concatenate([sink_ref[:, h:h + 1] for h in range(g * n, (g + 1) * n)], axis=0).reshape(n, 1, 1)


def _both_halves(t, t_rolled, g):
    low = _lane_low()
    return jnp.where(low, t, t_rolled) if g == 0 else jnp.where(low, t_rolled, t)


def _cross_pairs(q, mk, mv):
    rows = q.shape[0]
    low = _lane_low()
    qm = [jnp.concatenate([jnp.where(low, q[:, p * 128:(p + 1) * 128], 0.0), jnp.where(low, 0.0, q[:, p * 128:(p + 1) * 128])], axis=0)
          for p in range(X_HEADS // 2)]
    s = [_mm_nt(qm[p], mk[:, p * 128:(p + 1) * 128]) * (X_HD ** -0.5) for p in range(X_HEADS // 2)]
    e = [jnp.exp(t - jnp.max(t, axis=-1, keepdims=True)) for t in s]
    pr = [t / jnp.sum(t, axis=-1, keepdims=True) for t in e]
    o = [_mm(pr[p], mv[:, p * 128:(p + 1) * 128]) for p in range(X_HEADS // 2)]
    return jnp.concatenate([jnp.where(low, t[:rows], t[rows:]) for t in o], axis=1)


def _swa_upper():
    qi = lax.broadcasted_iota(jnp.int32, (BLK, BLK), 0)
    kj = lax.broadcasted_iota(jnp.int32, (BLK, BLK), 1)
    return kj > qi


def _bias_build(rel_bias, bucket, name):
    def body(rb_ref, bucket_ref, o_ref):
        b = bucket_ref[...]
        for h in range(A_HEADS):
            acc = jnp.zeros((BLK, BLK), F32)
            for k in range(N_BUCKETS):
                acc = jnp.where(b == k, rb_ref[k, h], acc)
            o_ref[h] = acc

    return pl.pallas_call(
        body, in_specs=[pl.BlockSpec(memory_space=pltpu.SMEM), pl.BlockSpec(memory_space=pltpu.VMEM)],
        out_specs=pl.BlockSpec(memory_space=pltpu.VMEM),
        out_shape=_SDS((A_HEADS, BLK, BLK), F32), name=name)(rel_bias, bucket)


def _bias_reduce(dbias, bucket, name):
    def body(db_ref, bucket_ref, o_ref):
        b = bucket_ref[...]
        row = lax.broadcasted_iota(jnp.int32, (N_BUCKETS, 128), 0)
        lane = lax.broadcasted_iota(jnp.int32, (N_BUCKETS, 128), 1)
        acc = jnp.zeros((N_BUCKETS, 128), F32)
        for h in range(A_HEADS):
            v = db_ref[h]
            for k in range(N_BUCKETS):
                sk = jnp.sum(jnp.sum(jnp.where(b == k, v, 0.0), axis=1, keepdims=True), axis=0, keepdims=True)
                acc = acc + jnp.where((row == k) & (lane == h), sk, 0.0)
        o_ref[...] = acc

    return pl.pallas_call(
        body, in_specs=[pl.BlockSpec(memory_space=pltpu.VMEM)] * 2,
        out_specs=pl.BlockSpec(memory_space=pltpu.VMEM),
        out_shape=_SDS((N_BUCKETS, 128), F32), name=name)(dbias, bucket)


def _mix_a_fwd(proj, bias, sinks, memkv, name):
    s = proj.shape[0]
    nb = s // BLK
    grp = A_HEADS // A_KV_HEADS

    def body(proj_ref, prev_ref, bias_ref, sink_ref, memkv_ref, o_ref):
        i = pl.program_id(0)
        upper = _swa_upper()
        prev = prev_ref[...]
        kb = jnp.concatenate([prev[:, :A_KV], proj_ref[:, A_Q:A_Q + A_KV]], axis=0)
        vb = jnp.concatenate([prev[:, A_KV:], proj_ref[:, A_Q + A_KV:A_Q + 2 * A_KV]], axis=0)
        kb_r = pltpu.roll(kb, A_HD, 1)
        vb_r = pltpu.roll(vb, A_HD, 1)
        gw = A_Q // A_KV_HEADS
        outs = [_swa_group(proj_ref[:, g * gw:(g + 1) * gw], _both_halves(kb, kb_r, g), _both_halves(vb, vb_r, g),
                           _swa_sinks(sink_ref, g), bias_ref[g * grp:(g + 1) * grp], upper, i == 0) for g in range(A_KV_HEADS)]
        outs.append(_cross_pairs(proj_ref[:, A_Q + 2 * A_KV:], memkv_ref[:, :X_Q], memkv_ref[:, X_Q:]))
        o_ref[...] = jnp.concatenate(outs, axis=1).astype(o_ref.dtype)

    return pl.pallas_call(
        body, grid=(nb,),
        in_specs=[pl.BlockSpec((BLK, IN_A), lambda i: (i, 0)),
                  pl.BlockSpec((BLK, 2 * A_KV), lambda i: (jnp.maximum(i - 1, 0), A_Q // (2 * A_KV))),
                  pl.BlockSpec((A_HEADS, BLK, BLK), lambda i: (0, 0, 0)),
                  pl.BlockSpec((1, 128), lambda i: (0, 0)),
                  pl.BlockSpec((MEM_LEN, 2 * X_Q), lambda i: (0, 0))],
        out_specs=pl.BlockSpec((BLK, D), lambda i: (i, 0)),
        out_shape=_SDS((s, D), _ACT), name=name, compiler_params=_cp("arbitrary"))(proj, proj, bias, sinks, memkv)


def _mix_a_bwd(proj, bias, sinks, memkv, dmix, name):
    s = proj.shape[0]
    nb = s // BLK
    grp = A_HEADS // A_KV_HEADS

    def body(proj_ref, prev_ref, bias_ref, sink_ref, memkv_ref, dmix_ref,
             dproj_ref, dbias_ref, dsink_ref, dmemkv_ref, carry_ref):
        t = pl.program_id(0)
        i = nb - 1 - t

        @pl.when(t == 0)
        def _():
            carry_ref[...] = jnp.zeros_like(carry_ref)
            dbias_ref[...] = jnp.zeros_like(dbias_ref)
            dsink_ref[...] = jnp.zeros_like(dsink_ref)
            dmemkv_ref[...] = jnp.zeros_like(dmemkv_ref)

        upper = _swa_upper()
        lane = lax.broadcasted_iota(jnp.int32, (1, 128), 1)
        low = _lane_low()
        prev = prev_ref[...]
        kb = jnp.concatenate([prev[:, :A_KV], proj_ref[:, A_Q:A_Q + A_KV]], axis=0)
        vb = jnp.concatenate([prev[:, A_KV:], proj_ref[:, A_Q + A_KV:A_Q + 2 * A_KV]], axis=0)
        kb_r = pltpu.roll(kb, A_HD, 1)
        vb_r = pltpu.roll(vb, A_HD, 1)
        gw = A_Q // A_KV_HEADS
        dqs, dkd, dvd = [], [], []
        dsink = jnp.zeros((1, 128), F32)
        for g in range(A_KV_HEADS):
            _, vjp = jax.vjp(functools.partial(_swa_group, upper=upper, first=i == 0), proj_ref[:, g * gw:(g + 1) * gw],
                             _both_halves(kb, kb_r, g), _both_halves(vb, vb_r, g), _swa_sinks(sink_ref, g),
                             bias_ref[g * grp:(g + 1) * grp])
            dq, dk, dv, ds, db = vjp(dmix_ref[:, g * gw:(g + 1) * gw].astype(F32))
            dqs.append(dq)
            dkd.append(dk + pltpu.roll(dk, A_HD, 1))
            dvd.append(dv + pltpu.roll(dv, A_HD, 1))
            for h in range(grp):
                dsink = dsink + jnp.where(lane == g * grp + h, ds[h], 0.0)
            dbias_ref[g * grp:(g + 1) * grp] += db
        dsink_ref[...] += dsink
        dkb = jnp.where(low, dkd[0], dkd[1])
        dvb = jnp.where(low, dvd[0], dvd[1])
        _, vjp = jax.vjp(_cross_pairs, proj_ref[:, A_Q + 2 * A_KV:], memkv_ref[:, :X_Q], memkv_ref[:, X_Q:])
        dxq, dmk, dmv = vjp(dmix_ref[:, A_Q:].astype(F32))
        dmemkv_ref[...] += jnp.concatenate([dmk, dmv], axis=1)
        dkv_cur = jnp.concatenate([dkb[BLK:], dvb[BLK:]], axis=1) + carry_ref[...]
        carry_ref[...] = jnp.concatenate([dkb[:BLK], dvb[:BLK]], axis=1)
        dproj_ref[...] = jnp.concatenate(dqs + [dkv_cur, dxq], axis=1)

    return pl.pallas_call(
        body, grid=(nb,),
        in_specs=[pl.BlockSpec((BLK, IN_A), lambda t: (nb - 1 - t, 0)),
                  pl.BlockSpec((BLK, 2 * A_KV), lambda t: (jnp.maximum(nb - 2 - t, 0), A_Q // (2 * A_KV))),
                  pl.BlockSpec((A_HEADS, BLK, BLK), lambda t: (0, 0, 0)),
                  pl.BlockSpec((1, 128), lambda t: (0, 0)),
                  pl.BlockSpec((MEM_LEN, 2 * X_Q), lambda t: (0, 0)),
                  pl.BlockSpec((BLK, D), lambda t: (nb - 1 - t, 0))],
        out_specs=[pl.BlockSpec((BLK, IN_A), lambda t: (nb - 1 - t, 0)),
                   pl.BlockSpec((A_HEADS, BLK, BLK), lambda t: (0, 0, 0)),
                   pl.BlockSpec((1, 128), lambda t: (0, 0)),
                   pl.BlockSpec((MEM_LEN, 2 * X_Q), lambda t: (0, 0))],
        out_shape=[_SDS((s, IN_A), F32), _SDS((A_HEADS, BLK, BLK), F32), _SDS((1, 128), F32),
                   _SDS((MEM_LEN, 2 * X_Q), F32)],
        scratch_shapes=[pltpu.VMEM((BLK, 2 * A_KV), F32)],
        name=name, compiler_params=_cp("arbitrary"))(proj, proj, bias, sinks, memkv, dmix)


def _dn_heads(yq, yk, yv, z, bl, al, a_log, dtb, ng, s0):
    c = CHUNK
    nh = B_V_HEADS
    rep = B_V_HEADS // B_QK_HEADS
    r = lax.broadcasted_iota(jnp.int32, (c, c), 0)
    cc = lax.broadcasted_iota(jnp.int32, (c, c), 1)
    q = [_silu(t) for t in yq]
    k = [_silu(t) for t in yk]
    v = [_silu(t) for t in yv]
    q = [t * lax.rsqrt(jnp.sum(t * t, axis=-1, keepdims=True) + EPS) * (B_HD ** -0.5) for t in q]
    k = [t * lax.rsqrt(jnp.sum(t * t, axis=-1, keepdims=True) + EPS) for t in k]
    beta = [jax.nn.sigmoid(t) for t in bl]
    g = [-jnp.exp(a_log[h]) * jax.nn.softplus(al[h] + dtb[h]) for h in range(nh)]
    gb = [jnp.broadcast_to(t, (c, c)) for t in g]
    gc_col = [jnp.sum(jnp.where(cc <= r, t.T, 0.0), axis=1, keepdims=True) for t in gb]
    gc_row = [jnp.sum(jnp.where(r <= cc, t, 0.0), axis=0, keepdims=True) for t in gb]
    gc_last = [jnp.sum(t, axis=0, keepdims=True) for t in g]
    decay = [jnp.exp(jnp.where(r >= cc, gc_col[h] - gc_row[h], -jnp.inf)) for h in range(nh)]
    kq = [_mmf_nt(jnp.concatenate([k[h], q[h]], axis=0), k[h]) for h in range(B_QK_HEADS)]
    kk = [t[:c] for t in kq]
    qk = [t[c:] for t in kq]
    egc = [jnp.exp(t) for t in gc_col]
    both = [_mmf(jnp.concatenate([(beta[h] * egc[h]) * k[h // rep], q[h // rep] * egc[h]], axis=0), s0[h]) for h in range(nh)]
    rhs = [beta[h] * v[h] - both[h][:c] for h in range(nh)]
    qs0 = [t[c:] for t in both]
    pw = [-(beta[h] * kk[h // rep] * jnp.where(r > cc, decay[h], 0.0)) for h in range(nh)]
    x = rhs
    for lvl in range(6):
        if lvl < 5:
            prod = [_mmf(pw[h], jnp.concatenate([x[h], pw[h]], axis=1)) for h in range(nh)]
            x = [x[h] + prod[h][:, :B_HD] for h in range(nh)]
            pw = [t[:, B_HD:] for t in prod]
        else:
            x = [x[h] + _mmf(pw[h], x[h]) for h in range(nh)]
    delta = x
    last = [_mmf(jnp.concatenate([qk[h // rep] * decay[h], (k[h // rep] * jnp.exp(gc_last[h] - gc_col[h])).T], axis=0), delta[h])
            for h in range(nh)]
    out = [qs0[h] + last[h][:c] for h in range(nh)]
    s1 = [jnp.exp(gc_last[h]) * s0[h] + last[h][c:] for h in range(nh)]
    o = [t * lax.rsqrt(jnp.mean(t * t, axis=-1, keepdims=True) + EPS) * ng for t in out]
    return [o[h] * _silu(z[h]) for h in range(nh)], s1


def _dn_conv(ext, w_ref):
    y = ext * w_ref[B_CONV - 1:B_CONV, :]
    for j in range(B_CONV - 1):
        y = y + w_ref[j:j + 1, :] * pltpu.roll(ext, B_CONV - 1 - j, 0)
    return y


def _dn_args(y, cur_ref, par_ref, ng_ref):
    nh = B_V_HEADS
    return ([y[:, h * B_HD:(h + 1) * B_HD] for h in range(B_QK_HEADS)],
            [y[:, B_QK + h * B_HD:B_QK + (h + 1) * B_HD] for h in range(B_QK_HEADS)],
            [y[:, 2 * B_QK + h * B_HD:2 * B_QK + (h + 1) * B_HD] for h in range(nh)],
            [cur_ref[:, BP_Z + h * B_HD:BP_Z + (h + 1) * B_HD] for h in range(nh)],
            [cur_ref[:, BP_GATE + h:BP_GATE + h + 1] for h in range(nh)],
            [cur_ref[:, BP_GATE + nh + h:BP_GATE + nh + h + 1] for h in range(nh)],
            [par_ref[:, h:h + 1] for h in range(nh)], [par_ref[:, nh + h:nh + h + 1] for h in range(nh)], ng_ref[...])


def _mix_b_fwd(proj, conv_w, par, ng, memkv, name):
    s = proj.shape[0]
    nc = s // CHUNK

    def body(cur_ref, prev_ref, w_ref, par_ref, ng_ref, memkv_ref, o_ref, st_ref, state_ref):
        n = pl.program_id(0)

        @pl.when(n == 0)
        def _():
            state_ref[...] = jnp.zeros_like(state_ref)

        prev = jnp.where(n > 0, prev_ref[...], 0.0)
        ext = jnp.concatenate([prev, cur_ref[:, :B_QKV]], axis=0)
        y = _dn_conv(ext, w_ref)[HALO:]
        s0 = [state_ref[hv] for hv in range(B_V_HEADS)]
        st_ref[0] = state_ref[...]
        outs, s1 = _dn_heads(*_dn_args(y, cur_ref, par_ref, ng_ref), s0)
        for hv in range(B_V_HEADS):
            state_ref[hv] = s1[hv]
        outs = outs + [_cross_pairs(cur_ref[:, BP_XQ:BP_XQ + X_Q], memkv_ref[:, :X_Q], memkv_ref[:, X_Q:])]
        o_ref[...] = jnp.concatenate(outs, axis=1).astype(o_ref.dtype)

    return pl.pallas_call(
        body, grid=(nc,),
        in_specs=[pl.BlockSpec((CHUNK, IN_BP), lambda n: (n, 0)),
                  pl.BlockSpec((HALO, B_QKV), lambda n: (jnp.maximum(n * (CHUNK // HALO) - 1, 0), 0)),
                  pl.BlockSpec((HALO, B_QKV), lambda n: (0, 0)),
                  pl.BlockSpec((1, 128), lambda n: (0, 0)), pl.BlockSpec((1, 128), lambda n: (0, 0)),
                  pl.BlockSpec((MEM_LEN, 2 * X_Q), lambda n: (0, 0))],
        out_specs=[pl.BlockSpec((CHUNK, D), lambda n: (n, 0)),
                   pl.BlockSpec((1, B_V_HEADS, B_HD, B_HD), lambda n: (n, 0, 0, 0))],
        out_shape=[_SDS((s, D), _ACT), _SDS((nc, B_V_HEADS, B_HD, B_HD), F32)],
        scratch_shapes=[pltpu.VMEM((B_V_HEADS, B_HD, B_HD), F32)],
        name=name, compiler_params=_cp("arbitrary"))(proj, proj, conv_w, par, ng, memkv)


def _mix_b_bwd(proj, conv_w, par, ng, memkv, states, dmix, name):
    s = proj.shape[0]
    nc = s // CHUNK
    ext_rows = CHUNK + HALO

    def body(cur_ref, prev_ref, w_ref, par_ref, ng_ref, memkv_ref, st_ref, dmix_ref,
             dproj_ref, dw_ref, dpar_ref, dng_ref, dmemkv_ref, dstate_ref, carry_ref):
        t = pl.program_id(0)
        n = nc - 1 - t

        @pl.when(t == 0)
        def _():
            dstate_ref[...] = jnp.zeros_like(dstate_ref)
            carry_ref[...] = jnp.zeros_like(carry_ref)
            dw_ref[...] = jnp.zeros_like(dw_ref)
            dpar_ref[...] = jnp.zeros_like(dpar_ref)
            dng_ref[...] = jnp.zeros_like(dng_ref)
            dmemkv_ref[...] = jnp.zeros_like(dmemkv_ref)

        lane = lax.broadcasted_iota(jnp.int32, (1, 128), 1)
        prev = jnp.where(n > 0, prev_ref[...], 0.0)
        ext = jnp.concatenate([prev, cur_ref[:, :B_QKV]], axis=0)
        y = _dn_conv(ext, w_ref)[HALO:]
        _, vjp = jax.vjp(_dn_heads, *_dn_args(y, cur_ref, par_ref, ng_ref), [st_ref[0, hv] for hv in range(B_V_HEADS)])
        dyq, dyk, dyv, dz, gbl, gal, ga_log, gdtb, dng, gs0 = vjp(
            ([dmix_ref[:, hv * B_HD:(hv + 1) * B_HD].astype(F32) for hv in range(B_V_HEADS)],
             [dstate_ref[hv] for hv in range(B_V_HEADS)]))
        dgate = jnp.zeros((CHUNK, 128), F32)
        dpar = jnp.zeros((1, 128), F32)
        for hv in range(B_V_HEADS):
            dstate_ref[hv] = gs0[hv]
            dgate = dgate + jnp.where(lane == hv, gbl[hv], 0.0) + jnp.where(lane == B_V_HEADS + hv, gal[hv], 0.0)
            dpar = dpar + jnp.where(lane == hv, ga_log[hv], 0.0) + jnp.where(lane == B_V_HEADS + hv, gdtb[hv], 0.0)
        dpar_ref[...] += dpar
        dng_ref[...] += dng
        _, vjp = jax.vjp(_cross_pairs, cur_ref[:, BP_XQ:BP_XQ + X_Q], memkv_ref[:, :X_Q], memkv_ref[:, X_Q:])
        dxq, dmk, dmv = vjp(dmix_ref[:, B_V:].astype(F32))
        dmemkv_ref[...] += jnp.concatenate([dmk, dmv], axis=1)
        dy = jnp.concatenate(list(dyq) + list(dyk) + list(dyv), axis=1)
        dy_ext = jnp.concatenate([jnp.zeros((HALO, B_QKV), F32), dy], axis=0)
        dext = dy_ext * w_ref[B_CONV - 1:B_CONV, :]
        dw_ref[B_CONV - 1:B_CONV, :] += jnp.sum(ext * dy_ext, axis=0, keepdims=True)
        for j in range(B_CONV - 1):
            sh = B_CONV - 1 - j
            dw_ref[j:j + 1, :] += jnp.sum(pltpu.roll(ext, sh, 0) * dy_ext, axis=0, keepdims=True)
            dext = dext + w_ref[j:j + 1, :] * pltpu.roll(dy_ext, ext_rows - sh, 0)
        tail = jnp.concatenate([jnp.zeros((CHUNK - HALO, B_QKV), F32), carry_ref[...]], axis=0)
        dqkv = dext[HALO:] + tail
        carry_ref[...] = dext[:HALO]
        dproj_ref[...] = jnp.concatenate([dqkv] + list(dz) + [dxq, dgate], axis=1)

    return pl.pallas_call(
        body, grid=(nc,),
        in_specs=[pl.BlockSpec((CHUNK, IN_BP), lambda t: (nc - 1 - t, 0)),
                  pl.BlockSpec((HALO, B_QKV), lambda t: (jnp.maximum((nc - 1 - t) * (CHUNK // HALO) - 1, 0), 0)),
                  pl.BlockSpec((HALO, B_QKV), lambda t: (0, 0)),
                  pl.BlockSpec((1, 128), lambda t: (0, 0)), pl.BlockSpec((1, 128), lambda t: (0, 0)),
                  pl.BlockSpec((MEM_LEN, 2 * X_Q), lambda t: (0, 0)),
                  pl.BlockSpec((1, B_V_HEADS, B_HD, B_HD), lambda t: (nc - 1 - t, 0, 0, 0)),
                  pl.BlockSpec((CHUNK, D), lambda t: (nc - 1 - t, 0))],
        out_specs=[pl.BlockSpec((CHUNK, IN_BP), lambda t: (nc - 1 - t, 0)),
                   pl.BlockSpec((HALO, B_QKV), lambda t: (0, 0)),
                   pl.BlockSpec((1, 128), lambda t: (0, 0)), pl.BlockSpec((1, 128), lambda t: (0, 0)),
                   pl.BlockSpec((MEM_LEN, 2 * X_Q), lambda t: (0, 0))],
        out_shape=[_SDS((s, IN_BP), F32), _SDS((HALO, B_QKV), F32), _SDS((1, 128), F32), _SDS((1, 128), F32),
                   _SDS((MEM_LEN, 2 * X_Q), F32)],
        scratch_shapes=[pltpu.VMEM((B_V_HEADS, B_HD, B_HD), F32), pltpu.VMEM((HALO, B_QKV), F32)],
        name=name, compiler_params=_cp("arbitrary"))(proj, proj, conv_w, par, ng, memkv, states, dmix)


def _place():
    return lax.axis_index("x"), lax.axis_index("y"), lax.axis_index("c")


def _all_gather(shards, name):
    n = len(shards)

    def body(*refs):
        ins, outs = refs[:n], refs[n:2 * n]
        send_sems, recv_sems, local_sems = refs[2 * n:]
        x, y, c = _place()
        me, sibling = (x, y, c), (x, y, 1 - c)
        chips = [(1 - x, y), (x, 1 - y), (1 - x, 1 - y)]

        def rows(a, px, py, pc):
            return outs[a].at[4 * px + 2 * py + pc]

        def copy(a, k, block, to, src=None):
            return pltpu.make_async_remote_copy(
                src_ref=rows(a, *block) if src is None else src, dst_ref=rows(a, *block),
                send_sem=send_sems.at[a, k], recv_sem=recv_sems.at[a, k],
                device_id=to, device_id_type=pl.DeviceIdType.MESH)

        mine = [pltpu.make_async_copy(ins[a], rows(a, *me), local_sems.at[a]) for a in range(n)]
        for cp in mine:
            cp.start()
        first = []
        for a in range(n):
            first.append(copy(a, 0, me, sibling, src=ins[a]))
            first += [copy(a, 1 + j, me, (*chip, c), src=ins[a]) for j, chip in enumerate(chips)]
        for cp in first:
            cp.start()
        passed = []
        for j, chip in enumerate(chips):
            for a in range(n):
                copy(a, 1 + j, (*chip, c), me).wait_recv()
                fwd = copy(a, 4 + j, (*chip, c), sibling)
                fwd.start()
                passed.append(fwd)
        for a in range(n):
            copy(a, 0, sibling, me).wait_recv()
            for j, chip in enumerate(chips):
                copy(a, 4 + j, (*chip, 1 - c), me).wait_recv()
        for cp in first + passed:
            cp.wait_send()
        for cp in mine:
            cp.wait()

    hbm = pl.BlockSpec(memory_space=pl.ANY)
    return pl.pallas_call(
        body, out_shape=[_SDS((N_DEV,) + s.shape, s.dtype) for s in shards],
        in_specs=[hbm] * n, out_specs=[hbm] * n,
        scratch_shapes=[pltpu.SemaphoreType.DMA((n, 7)), pltpu.SemaphoreType.DMA((n, 7)), pltpu.SemaphoreType.DMA((n,))],
        name=name)(*shards)


def _exchange(items, out_shapes, name):
    n = len(items)
    n_out = len(out_shapes)

    def body(*refs):
        srcs, outs = refs[:n], refs[n:n + n_out]
        send_sems, recv_sems, local_sems = refs[n + n_out:]
        x, y, c = _place()
        my = 4 * x + 2 * y + c

        def src_for(a, dest):
            return srcs[a].at[dest] if items[a][3] else srcs[a]

        def slot(a, source):
            o, layer = items[a][1], items[a][2]
            return outs[o].at[source] if layer is None else outs[o].at[source, layer]

        mine = [pltpu.make_async_copy(src_for(a, my), slot(a, my), local_sems.at[a]) for a in range(n)]
        for cp in mine:
            cp.start()
        sends, recvs = [], []
        for k in range(N_DEV - 1):
            px, py, pc = x ^ ((k + 1) >> 2 & 1), y ^ ((k + 1) >> 1 & 1), c ^ ((k + 1) & 1)
            peer = 4 * px + 2 * py + pc
            for a in range(n):
                kw = dict(send_sem=send_sems.at[a, k], recv_sem=recv_sems.at[a, k],
                          device_id=(px, py, pc), device_id_type=pl.DeviceIdType.MESH)
                sends.append(pltpu.make_async_remote_copy(src_ref=src_for(a, peer), dst_ref=slot(a, my), **kw))
                recvs.append(pltpu.make_async_remote_copy(src_ref=src_for(a, my), dst_ref=slot(a, peer), **kw))
        for cp in sends:
            cp.start()
        for cp in recvs:
            cp.wait_recv()
        for cp in sends:
            cp.wait_send()
        for cp in mine:
            cp.wait()

    hbm = pl.BlockSpec(memory_space=pl.ANY)
    return pl.pallas_call(
        body, out_shape=[_SDS(s, d) for s, d in out_shapes],
        in_specs=[hbm] * n, out_specs=[hbm] * n_out,
        scratch_shapes=[pltpu.SemaphoreType.DMA((n, 7)), pltpu.SemaphoreType.DMA((n, 7)), pltpu.SemaphoreType.DMA((n,))],
        name=name)(*[it[0] for it in items])


def _adam_update(g, w, m, v):
    c1 = 1.0 - ADAM_B1 ** ADAM_STEP
    c2 = 1.0 - ADAM_B2 ** ADAM_STEP
    mm = ADAM_B1 * m + (1.0 - ADAM_B1) * g
    vv = ADAM_B2 * v + (1.0 - ADAM_B2) * (g * g)
    delta = -ADAM_LR * ((mm / c1) / (jnp.sqrt(vv / c2) + ADAM_EPS) + ADAM_WD * w)
    return delta, mm, vv


def _sum_sources(p_ref):
    g = p_ref[0].astype(F32)
    for s in range(1, N_DEV):
        g = g + p_ref[s].astype(F32)
    return g


def _adamw(parts, w, m, v, tr, name, restore_b=False):
    nl, r, c = w.shape
    cp = parts.shape[-1]

    def body(p_ref, w_ref, m_ref, v_ref, g_ref, d_ref, nm_ref, nv_ref):
        g = _sum_sources(p_ref)
        if restore_b:
            g = jnp.concatenate([g[:, :BP_XQ], g[:, BP_GATE:BP_GATE + 2 * B_V_HEADS], g[:, BP_XQ:BP_GATE]], axis=1)
        delta, mm, vv = _adam_update(g, w_ref[...], m_ref[...], v_ref[...])
        g_ref[...] = g
        d_ref[...] = delta
        nm_ref[...] = mm
        nv_ref[...] = vv

    spec = pl.BlockSpec((None, tr, c), lambda l, i: (l, i, 0))
    return pl.pallas_call(
        body, grid=(nl, r // tr),
        in_specs=[pl.BlockSpec((N_DEV, None, tr, cp), lambda l, i: (0, l, i, 0)), spec, spec, spec],
        out_specs=[spec] * 4, out_shape=[_SDS(w.shape, F32)] * 4,
        name=name, compiler_params=_cp("arbitrary", "arbitrary"))(parts, w, m, v)


def _pack_small(d_rel, d_cb, d_cw, d_qkv, d_mix, d_mem, d_ffn, d_final, d_sinks, d_par, d_ng, name):
    flat = [d_rel, *d_cb, *d_cw, d_qkv, *d_mix, *d_mem, *d_ffn, d_final, d_sinks, d_par, d_ng]
    n = len(flat)

    def body(*refs):
        ins, o_ref = refs[:n], refs[n]
        rel, cb0, cb1, cw0, cw1, qkv, mx0, mx1, me0, me1, ff0, ff1, fin, snk, par, ng = ins
        o_ref[...] = jnp.zeros_like(o_ref)
        o_ref[SP_REL:SP_REL + N_BUCKETS, 0:128] = rel[...]
        for l, (cb, cw) in enumerate(((cb0, cw0), (cb1, cw1))):
            o_ref[SP_CB + l:SP_CB + l + 1, :] = jnp.concatenate([cb[j] for j in range(FF_BLOCKS)], axis=1)
            full = jnp.concatenate([cw[j] for j in range(FF_BLOCKS)], axis=1)
            o_ref[SP_CW + FFN_CONV * l:SP_CW + FFN_CONV * (l + 1), :] = full[:FFN_CONV]
        o_ref[SP_QKV:SP_QKV + B_CONV, 0:B_QKV] = qkv[0:B_CONV, :]
        for base, pair in ((SP_MIX, (mx0, mx1)), (SP_MEM, (me0, me1)), (SP_FFN, (ff0, ff1))):
            for l in range(2):
                o_ref[base + l:base + l + 1, 0:D] = pair[l][...]
        o_ref[SP_FINAL:SP_FINAL + 1, 0:D] = fin[...]
        o_ref[SP_MISC:SP_MISC + 1, 0:128] = snk[...]
        o_ref[SP_MISC:SP_MISC + 1, 128:256] = par[...]
        o_ref[SP_MISC:SP_MISC + 1, 256:384] = ng[...]

    vm = pl.BlockSpec(memory_space=pltpu.VMEM)
    return pl.pallas_call(body, in_specs=[vm] * n, out_specs=vm, out_shape=_SDS((SMALL_ROWS, D_FF), F32), name=name)(*flat)


_SMALL = ["rel_bias", "norm_mix_g", "norm_mem_g", "sinks_a", "a_log_b", "dt_bias_b", "out_norm_g_b", "norm_ffn_g",
          "ffn_conv_b", "final_norm_g", "conv_qkv_b", "ffn_conv_w"]


def _adamw_small(recv, rc_qkv, rc_ffn, ws, ms, vs, name):
    n = len(_SMALL)

    def body(*refs):
        recv_ref, qkv_ref, ffn_ref = refs[:3]
        w_refs, m_refs, v_refs = refs[3:3 + n], refs[3 + n:3 + 2 * n], refs[3 + 2 * n:3 + 3 * n]
        outs = refs[3 + 3 * n:]
        gs = _sum_sources(recv_ref)
        grads = {
            "rel_bias": gs[SP_REL:SP_REL + N_BUCKETS, 0:A_HEADS],
            "norm_mix_g": gs[SP_MIX:SP_MIX + 2, 0:D], "norm_mem_g": gs[SP_MEM:SP_MEM + 2, 0:D],
            "sinks_a": gs[SP_MISC:SP_MISC + 1, 0:A_HEADS],
            "a_log_b": gs[SP_MISC:SP_MISC + 1, 128:128 + B_V_HEADS],
            "dt_bias_b": gs[SP_MISC:SP_MISC + 1, 128 + B_V_HEADS:128 + 2 * B_V_HEADS],
            "out_norm_g_b": gs[SP_MISC:SP_MISC + 1, 256:256 + B_HD],
            "norm_ffn_g": gs[SP_FFN:SP_FFN + 2, 0:D], "ffn_conv_b": gs[SP_CB:SP_CB + 2, :],
            "final_norm_g": gs[SP_FINAL:SP_FINAL + 1, 0:D],
            "conv_qkv_b": _sum_sources(qkv_ref), "ffn_conv_w": _sum_sources(ffn_ref),
        }
        for i, nm in enumerate(_SMALL):
            g = grads[nm]
            delta, mm, vv = _adam_update(g, w_refs[i][...], m_refs[i][...], v_refs[i][...])
            outs[i][...] = g
            outs[n + i][...] = delta
            outs[2 * n + i][...] = mm
            outs[3 * n + i][...] = vv

    vm = pl.BlockSpec(memory_space=pltpu.VMEM)
    shapes = [_SDS(w.shape, F32) for w in ws]
    return pl.pallas_call(
        body, in_specs=[vm] * (3 + 3 * n), out_specs=[vm] * (4 * n), out_shape=shapes * 4,
        name=name)(recv, rc_qkv, rc_ffn, *ws, *ms, *vs)


def _assemble(gathered, axis):
    g = jnp.moveaxis(gathered, 0, axis)
    shp = list(g.shape)
    return g.reshape(shp[:axis] + [shp[axis] * shp[axis + 1]] + shp[axis + 2:])


def _pad_rows(a, rows):
    return jnp.pad(a, ((0, rows - a.shape[0]), (0, 0)))


def _pad_lanes(a, lanes=128):
    return jnp.pad(a, ((0, 0), (0, lanes - a.shape[1])))


def _ff_blocks(a):
    return jnp.moveaxis(a.reshape(a.shape[0], FF_BLOCKS, GU_SHARD), 1, 0)


def _reorder_b(w):
    qkv_z = w[..., :B_QKV + B_V]
    gates = w[..., B_QKV + B_V:B_QKV + B_V + 2 * B_V_HEADS]
    xq = w[..., IN_B - X_Q:]
    pad = jnp.zeros(w.shape[:-1] + (IN_BP - IN_B,), w.dtype)
    return jnp.concatenate([qkv_z, xq, gates, pad], axis=-1)


def kernel(x, mem, rel_bias, norm_mix_g, norm_mem_g, w_mem_kv, w_out, w_in_a, sinks_a, w_in_b, conv_qkv_b, a_log_b, dt_bias_b, out_norm_g_b, norm_ffn_g, w_gate_up, ffn_conv_w, ffn_conv_b, w_down, final_norm_g, loss_target, m_rel_bias, m_norm_mix_g, m_norm_mem_g, m_w_mem_kv, m_w_out, m_w_in_a, m_sinks_a, m_w_in_b, m_conv_qkv_b, m_a_log_b, m_dt_bias_b, m_out_norm_g_b, m_norm_ffn_g, m_w_gate_up, m_ffn_conv_w, m_ffn_conv_b, m_w_down, m_final_norm_g, v_rel_bias, v_norm_mix_g, v_norm_mem_g, v_w_mem_kv, v_w_out, v_w_in_a, v_sinks_a, v_w_in_b, v_conv_qkv_b, v_a_log_b, v_dt_bias_b, v_out_norm_g_b, v_norm_ffn_g, v_w_gate_up, v_ffn_conv_w, v_ffn_conv_b, v_w_down, v_final_norm_g):
    local = dict(locals())
    order = ["rel_bias", "norm_mix_g", "norm_mem_g", "w_mem_kv", "w_out", "w_in_a", "sinks_a", "w_in_b", "conv_qkv_b",
             "a_log_b", "dt_bias_b", "out_norm_g_b", "norm_ffn_g", "w_gate_up", "ffn_conv_w", "ffn_conv_b", "w_down",
             "final_norm_g"]
    wts = {n: local[n] for n in order}
    moms = {n: local["m_" + n] for n in order}
    vars_ = {n: local["v_" + n] for n in order}
    h0 = x[0]
    memx = mem[0]
    tgt = loss_target[0]
    s = h0.shape[0]
    tm = _rows(s)

    g_mk, g_out, g_ia, g_ib, g_gu, g_dn, g_cq, g_cw = _all_gather(
        [w_mem_kv.astype(_MXU), w_out.astype(_MXU), w_in_a.astype(_MXU), _reorder_b(w_in_b).astype(_MXU),
         w_gate_up.astype(_MXU), w_down.astype(_MXU), conv_qkv_b, ffn_conv_w], "gather_weights")
    w_ia = _assemble(g_ia, 2)[0]
    conv_qkv = _pad_rows(_assemble(g_cq, 2)[0], HALO)
    ffn_cw_full = _assemble(g_cw, 2)
    ffn_cw = [_ff_blocks(_pad_rows(ffn_cw_full[i], HALO)) for i in range(2)]
    ffn_cb = [_ff_blocks(ffn_conv_b[i:i + 1]) for i in range(2)]
    bucket = jnp.asarray(_bucket_table())
    bias = _bias_build(rel_bias, bucket, "bias_build")
    sinks = _pad_lanes(sinks_a)
    par_b = _pad_lanes(jnp.concatenate([a_log_b, dt_bias_b], axis=1))

    row_x = pl.BlockSpec((tm, D), lambda i, j: (i, 0))
    gu_shape = (2, FF_BLOCKS, s, GU_SHARD)

    def in_proj(h, g, w, w_spec, n_cols, tn, name):
        return _norm_matmul(h, g, w, w_spec, n_cols // tn, (h.shape[0], n_cols),
                            pl.BlockSpec((_rows(h.shape[0]), tn), lambda i, j: (i, j)), name)

    def ffn_fwd(i, h):
        gu, hn = _norm_matmul(h, norm_ffn_g[i:i + 1], g_gu, _spec_gate_up(i, 1), N_DEV, gu_shape,
                              _spec_gu_act(0, 1, tm), f"gate_up_{i}")
        act = _glu_fwd(gu, ffn_cw[i], ffn_cb[i], f"glu_fwd_{i}")
        h_new = _matmul_res(act, pl.BlockSpec((None, tm, GU_SHARD), lambda r, j: (j, r, 0)), g_dn, _spec_down(i, 1),
                            FF_BLOCKS, h, f"down_proj_{i}")
        return h_new, gu, hn, act

    def out_proj(i, mix, h):
        return _matmul_res(mix, row_x, g_out, _spec_rowsharded(i, D // N_DEV, D), 1, h, f"out_proj_{i}")

    proj_a, hn_a = in_proj(h0, norm_mix_g[0:1], w_ia, pl.BlockSpec((D, 640), lambda i, j: (0, j)), IN_A, 640, "in_proj_a")
    memkv0, memn0 = in_proj(memx, norm_mem_g[0:1], g_mk, _spec_rowsharded(0, D // N_DEV, 2 * X_Q), 2 * X_Q, 2 * X_Q, "mem_proj_0")
    mix_a = _mix_a_fwd(proj_a, bias, sinks, memkv0, "mix_a_fwd")
    h1 = out_proj(0, mix_a, h0)
    h2, gu0, hn_f0, act0 = ffn_fwd(0, h1)
    proj_b, hn_b = in_proj(h2, norm_mix_g[1:2], g_ib, _spec_rowsharded(0, D // N_DEV, 896, col_block=1), IN_BP, 896, "in_proj_b")
    memkv1, memn1 = in_proj(memx, norm_mem_g[1:2], g_mk, _spec_rowsharded(1, D // N_DEV, 2 * X_Q), 2 * X_Q, 2 * X_Q, "mem_proj_1")
    mix_b, states = _mix_b_fwd(proj_b, conv_qkv, par_b, out_norm_g_b, memkv1, "mix_b_fwd")
    h3 = out_proj(1, mix_b, h2)
    h4, gu1, hn_f1, act1 = ffn_fwd(1, h3)
    loss_row, dh, d_final_g = _loss_head(h4, final_norm_g[None, :], tgt, "loss_head")

    zeros_mem = jnp.zeros_like(memx)

    def ffn_bwd(i, dh, h_in, gu, hn_f, act):
        dact = _matmul_nt(dh, g_dn, _spec_down(i, 1), FF_BLOCKS, (FF_BLOCKS, s, GU_SHARD),
                          pl.BlockSpec((None, tm, GU_SHARD), lambda r, j: (j, r, 0)), f"d_act_{i}")
        d_wdown = _matmul_tn(act, pl.BlockSpec((None, tm, GU_SHARD), lambda j, r: (j, r, 0)),
                             dh, pl.BlockSpec((tm, D), lambda j, r: (r, 0)), s, FF_BLOCKS, (GU_SHARD, D),
                             (N_DEV, DN_SHARD, D), pl.BlockSpec((2, DN_SHARD, D), lambda j, r: (j, 0, 0)), f"d_w_down_{i}")
        dgu, d_cw, d_cb = _glu_bwd(gu, ffn_cw[i], ffn_cb[i], dact, f"glu_bwd_{i}")
        dh_new, d_g = _matmul_nt_normbwd(dgu, _spec_gu_act(0, 1, tm), g_gu, _spec_gate_up(i, 1), N_DEV, h_in,
                                         norm_ffn_g[i:i + 1], dh, f"d_ffn_in_{i}")
        d_wgu = _matmul_tn(hn_f, pl.BlockSpec((tm, D), lambda j, r: (r, 0)), dgu, _spec_gu_act(1, 0, tm), s, N_DEV,
                           (D, GU_SHARD), (N_DEV, D, GU_SHARD), pl.BlockSpec((None, D, GU_SHARD), lambda j, r: (j, 0, 0)),
                           f"d_w_gate_up_{i}")
        return dh_new, d_wdown, d_wgu, d_cw, d_cb, d_g

    def out_bwd(i, dh, mix):
        dmix = _matmul_nt(dh, g_out, _spec_rowsharded(i, D // N_DEV, D), 1, (s, D), row_x, f"d_mix_{i}")
        d_wout = _matmul_tn(mix, pl.BlockSpec((tm, D), lambda j, r: (r, 0)), dh, pl.BlockSpec((tm, D), lambda j, r: (r, 0)),
                            s, 1, (D, D), (N_DEV, D // N_DEV, D), pl.BlockSpec((N_DEV, D // N_DEV, D), lambda j, r: (0, 0, 0)),
                            f"d_w_out_{i}")
        return dmix, d_wout

    def mem_bwd(i, dmemkv, memn):
        tmm = _rows(MEM_LEN)
        _, d_g = _matmul_nt_normbwd(dmemkv, pl.BlockSpec((tmm, 2 * X_Q), lambda r, j: (r, 0)), g_mk,
                                    _spec_rowsharded(i, D // N_DEV, 2 * X_Q), 1, memx, norm_mem_g[i:i + 1], zeros_mem,
                                    f"d_mem_in_{i}")
        by_row = lambda j, r: (r, 0)
        d_w = _matmul_tn(memn, pl.BlockSpec((tmm, D), by_row), dmemkv, pl.BlockSpec((tmm, 2 * X_Q), by_row), MEM_LEN, 1,
                         (D, 2 * X_Q), (N_DEV, D // N_DEV, 2 * X_Q),
                         pl.BlockSpec((N_DEV, D // N_DEV, 2 * X_Q), lambda j, r: (0, 0, 0)), f"d_w_mem_kv_{i}")
        return d_w, d_g

    dh, d_wdown1, d_wgu1, d_cw1, d_cb1, d_gf1 = ffn_bwd(1, dh, h3, gu1, hn_f1, act1)
    dmix, d_wout1 = out_bwd(1, dh, mix_b)
    dproj_b, d_convw, d_par, d_ng, dmemkv1 = _mix_b_bwd(proj_b, conv_qkv, par_b, out_norm_g_b, memkv1, states, dmix, "mix_b_bwd")
    dh, d_gm1 = _matmul_nt_normbwd(dproj_b, pl.BlockSpec((tm, 896), lambda i, j: (i, j)), g_ib,
                                   _spec_rowsharded(0, D // N_DEV, 896, col_block=1), IN_BP // 896, h2, norm_mix_g[1:2], dh, "d_in_b")
    d_wib = _matmul_tn(hn_b, pl.BlockSpec((tm, D), lambda j, r: (r, 0)), dproj_b, pl.BlockSpec((tm, 896), lambda j, r: (r, j)),
                       s, IN_BP // 896, (D, 896), (N_DEV, D // N_DEV, IN_BP),
                       pl.BlockSpec((N_DEV, D // N_DEV, 896), lambda j, r: (0, 0, j)), "d_w_in_b")
    d_wmk1, d_gmem1 = mem_bwd(1, dmemkv1, memn1)
    dh, d_wdown0, d_wgu0, d_cw0, d_cb0, d_gf0 = ffn_bwd(0, dh, h1, gu0, hn_f0, act0)
    dmix, d_wout0 = out_bwd(0, dh, mix_a)
    dproj_a, dbias, dsinks, dmemkv0 = _mix_a_bwd(proj_a, bias, sinks, memkv0, dmix, "mix_a_bwd")
    dh, d_gm0 = _matmul_nt_normbwd(dproj_a, pl.BlockSpec((tm, 640), lambda i, j: (i, j)), w_ia,
                                   pl.BlockSpec((D, 640), lambda i, j: (0, j)), IN_A // 640, h0, norm_mix_g[0:1], dh, "d_in_a")
    d_wia = _matmul_tn(hn_a, pl.BlockSpec((tm, D), lambda j, r: (r, 0)), dproj_a, pl.BlockSpec((tm, IN_A), lambda j, r: (r, 0)),
                       s, 1, (D, IN_A), (N_DEV, D, IA_SHARD), pl.BlockSpec((N_DEV, D, IA_SHARD), lambda j, r: (0, 0, 0)),
                       "d_w_in_a", split=IA_SHARD)
    d_wmk0, d_gmem0 = mem_bwd(0, dmemkv0, memn0)
    d_rel = _bias_reduce(dbias, bucket, "bias_reduce")

    small = _pack_small(d_rel, (d_cb0, d_cb1), (d_cw0, d_cw1), d_convw, (d_gm0, d_gm1), (d_gmem0, d_gmem1),
                        (d_gf0, d_gf1), d_final_g, dsinks, d_par, d_ng, "pack_small")
    items = [(d_wmk0, 0, 0, True), (d_wmk1, 0, 1, True), (d_wout0, 1, 0, True), (d_wout1, 1, 1, True),
             (d_wia, 2, 0, True), (d_wib, 3, 0, True), (d_wgu0, 4, 0, True), (d_wgu1, 4, 1, True),
             (d_wdown0, 5, 0, True), (d_wdown1, 5, 1, True), (small, 6, None, False)]
    out_shapes = [((N_DEV, 2, D // N_DEV, 2 * X_Q), _WIRE), ((N_DEV, 2, D // N_DEV, D), _WIRE),
                  ((N_DEV, 1, D, IA_SHARD), _WIRE), ((N_DEV, 1, D // N_DEV, IN_BP), _WIRE),
                  ((N_DEV, 2, D, GU_SHARD), _WIRE), ((N_DEV, 2, DN_SHARD, D), _WIRE), ((N_DEV, SMALL_ROWS, D_FF), F32)]
    r_mk, r_out, r_ia, r_ib, r_gu, r_dn, r_small = _exchange(items, out_shapes, "exchange_grads")

    res = {}
    for nm, parts, tr, restore in (("w_mem_kv", r_mk, 128, False), ("w_out", r_out, 128, False), ("w_in_a", r_ia, 512, False),
                                  ("w_in_b", r_ib, 32, True), ("w_gate_up", r_gu, 128, False), ("w_down", r_dn, 176, False)):
        res[nm] = _adamw(parts, wts[nm], moms[nm], vars_[nm], tr, "adamw_" + nm, restore_b=restore)

    my = 4 * lax.axis_index("x") + 2 * lax.axis_index("y") + lax.axis_index("c")
    cq = conv_qkv_b.shape[-1]
    cf = ffn_conv_w.shape[-1]
    rc_qkv = lax.dynamic_slice_in_dim(r_small[:, SP_QKV:SP_QKV + B_CONV, :B_QKV], my * cq, cq, axis=2)[:, None]
    rc_ffn = lax.dynamic_slice_in_dim(r_small[:, SP_CW:SP_CW + 2 * FFN_CONV, :], my * cf, cf, axis=2).reshape(N_DEV, 2, FFN_CONV, cf)
    as2d = lambda a: a[None, :] if a.ndim == 1 else a
    small_out = _adamw_small(r_small, rc_qkv, rc_ffn, [as2d(wts[n]) for n in _SMALL], [as2d(moms[n]) for n in _SMALL],
                             [as2d(vars_[n]) for n in _SMALL], "adamw_small")
    ns = len(_SMALL)
    for i, nm in enumerate(_SMALL):
        res[nm] = [small_out[k * ns + i].reshape(wts[nm].shape) for k in range(4)]

    loss = lax.psum(loss_row[0, 0], AXES)
    return (loss, dh[None], *[res[n][0] for n in order], *[res[n][1] for n in order],
            *[res[n][2] for n in order], *[res[n][3] for n in order])
```

```python
import functools
import math

import numpy as np

import jax
import jax.numpy as jnp
from jax import lax
from jax.experimental import pallas as pl
from jax.experimental.pallas import tpu as pltpu

F32 = jnp.float32
_MXU = jnp.bfloat16
_ACT = jnp.bfloat16
_WIRE = jnp.bfloat16
_HI = lax.Precision.HIGH
_TM = 512
_VMEM_LIMIT = 48 * 1024 * 1024
_SDS = jax.ShapeDtypeStruct

D = 1024
EPS = 1e-6
A_HEADS, A_KV_HEADS, A_HD, BLK = 12, 2, 64, 128
N_BUCKETS, MAX_DISTANCE = 32, 128
B_QK_HEADS, B_V_HEADS, B_HD, B_CONV, CHUNK = 3, 6, 128, 4, 64
X_HEADS, X_HD, MEM_LEN = 4, 64, 256
D_FF, FFN_CONV = 2816, 3
A_Q, A_KV, X_Q = 768, 128, 256
B_QK, B_V, B_QKV = 384, 768, 1536
IN_A, IN_B = 1280, 2572
IN_BP = 2688
BP_Z, BP_XQ, BP_GATE = 1536, 2304, 2560
HALO = 8

N_DEV = 8
AXES = ("x", "y", "c")
GU_SHARD = 2 * D_FF // N_DEV
FF_BLOCKS = D_FF // GU_SHARD
DN_SHARD = D_FF // N_DEV
IA_SHARD = IN_A // N_DEV

ADAM_LR, ADAM_B1, ADAM_B2, ADAM_EPS, ADAM_WD, ADAM_STEP = 0.001, 0.9, 0.999, 1e-08, 0.01, 10

SP_REL, SP_CB, SP_CW, SP_QKV, SP_MIX, SP_MEM, SP_FFN, SP_FINAL, SP_MISC, SMALL_ROWS = 0, 32, 34, 40, 44, 46, 48, 50, 51, 56


def _cp(*sems):
    return pltpu.CompilerParams(dimension_semantics=sems, vmem_limit_bytes=_VMEM_LIMIT)


def _mm(a, b):
    return jnp.dot(a.astype(_MXU), b.astype(_MXU), preferred_element_type=F32)


def _mm_nt(a, b):
    return lax.dot_general(a.astype(_MXU), b.astype(_MXU), (((1,), (1,)), ((), ())), preferred_element_type=F32)


def _mm_tn(a, b):
    return lax.dot_general(a.astype(_MXU), b.astype(_MXU), (((0,), (0,)), ((), ())), preferred_element_type=F32)


def _mmf(a, b):
    return jnp.dot(a, b, preferred_element_type=F32, precision=_HI)


def _mmf_nt(a, b):
    return lax.dot_general(a, b, (((1,), (1,)), ((), ())), preferred_element_type=F32, precision=_HI)


def _mmf_tn(a, b):
    return lax.dot_general(a, b, (((0,), (0,)), ((), ())), preferred_element_type=F32, precision=_HI)


def _silu(x):
    return x * jax.nn.sigmoid(x)


def _w2d(ref):
    v = ref[...]
    return v.reshape(-1, v.shape[-1])


def _rows(m):
    return min(m, _TM)


def _spec_rowsharded(layer, rows, cols, col_block=None):
    if col_block is None:
        return pl.BlockSpec((N_DEV, None, rows, cols), lambda *_: (0, layer, 0, 0))
    return pl.BlockSpec((N_DEV, None, rows, cols), lambda *ids: (0, layer, 0, ids[col_block]))


def _spec_gate_up(axis):
    return pl.BlockSpec((None, D, GU_SHARD), lambda *ids: (ids[axis], 0, 0))


def _spec_down(axis):
    return pl.BlockSpec((2, DN_SHARD, D), lambda *ids: (ids[axis], 0, 0))


def _dep_specs(deps):
    return [pl.BlockSpec(d.shape, lambda *_: (0,) * d.ndim) for d in deps]


def _spec_gu_act(row_axis, axis, tm):
    return pl.BlockSpec((None, None, tm, GU_SHARD), lambda *ids: (ids[axis] // FF_BLOCKS, ids[axis] % FF_BLOCKS, ids[row_axis], 0))


def _norm_matmul(x, g, w, w_spec, n_blocks, out_shape, out_spec, name, deps=()):
    m, k = x.shape
    tm = _rows(m)

    def body(x_ref, g_ref, w_ref, *rest):
        y_ref, hn_ref = rest[-2:]

        @pl.when(pl.program_id(1) == 0)
        def _():
            xv = x_ref[...]
            r = lax.rsqrt(jnp.mean(xv * xv, axis=-1, keepdims=True) + EPS)
            hn_ref[...] = (xv * r * g_ref[...]).astype(hn_ref.dtype)

        y_ref[...] = _mm(hn_ref[...], _w2d(w_ref))

    return pl.pallas_call(
        body, grid=(m // tm, n_blocks),
        in_specs=[pl.BlockSpec((tm, k), lambda i, j: (i, 0)), pl.BlockSpec((1, k), lambda i, j: (0, 0)), w_spec]
        + _dep_specs(deps),
        out_specs=[out_spec, pl.BlockSpec((tm, k), lambda i, j: (i, 0))],
        out_shape=[_SDS(out_shape, F32), _SDS((m, k), _ACT)],
        name=name, compiler_params=_cp("arbitrary", "arbitrary"))(x, g, w, *deps)


def _matmul_res(a, a_spec, w, w_spec, n_k, res, name):
    m, n = res.shape
    tm = _rows(m)

    def body(a_ref, w_ref, r_ref, o_ref):
        part = _mm(a_ref[...], _w2d(w_ref))

        @pl.when(pl.program_id(1) == 0)
        def _():
            o_ref[...] = r_ref[...] + part

        @pl.when(pl.program_id(1) > 0)
        def _():
            o_ref[...] += part

    return pl.pallas_call(
        body, grid=(m // tm, n_k),
        in_specs=[a_spec, w_spec, pl.BlockSpec((tm, n), lambda i, j: (i, 0))],
        out_specs=pl.BlockSpec((tm, n), lambda i, j: (i, 0)),
        out_shape=_SDS((m, n), F32), name=name, compiler_params=_cp("arbitrary", "arbitrary"))(a, w, res)


def _matmul_nt(dy, w, w_spec, n_blocks, out_shape, out_spec, name, deps=()):
    m, n = dy.shape
    tm = _rows(m)

    def body(dy_ref, w_ref, *rest):
        o_ref = rest[-1]
        o_ref[...] = _mm_nt(dy_ref[...], _w2d(w_ref)).astype(o_ref.dtype)

    return pl.pallas_call(
        body, grid=(m // tm, n_blocks),
        in_specs=[pl.BlockSpec((tm, n), lambda i, j: (i, 0)), w_spec] + _dep_specs(deps),
        out_specs=out_spec, out_shape=_SDS(out_shape, F32),
        name=name, compiler_params=_cp("arbitrary", "arbitrary"))(dy, w, *deps)


def _matmul_nt_normbwd(dy, dy_spec, w, w_spec, nj, h, g, dh_in, name):
    m, k = h.shape
    tm = _rows(m)

    def body(dy_ref, w_ref, h_ref, g_ref, dhin_ref, dh_ref, dg_ref, acc_ref):
        i, j = pl.program_id(0), pl.program_id(1)

        @pl.when(j == 0)
        def _():
            acc_ref[...] = jnp.zeros_like(acc_ref)

        acc_ref[...] += _mm_nt(dy_ref[...], _w2d(w_ref))

        @pl.when(j == nj - 1)
        def _():
            xv = h_ref[...]
            r = lax.rsqrt(jnp.mean(xv * xv, axis=-1, keepdims=True) + EPS)
            xh = xv * r
            dhn = acc_ref[...]
            part = jnp.sum(dhn * xh, axis=0, keepdims=True)

            @pl.when(i == 0)
            def _():
                dg_ref[...] = part

            @pl.when(i > 0)
            def _():
                dg_ref[...] += part

            t = dhn * g_ref[...]
            dh_ref[...] = dhin_ref[...] + r * (t - xh * jnp.mean(t * xh, axis=-1, keepdims=True))

    return pl.pallas_call(
        body, grid=(m // tm, nj),
        in_specs=[dy_spec, w_spec, pl.BlockSpec((tm, k), lambda i, j: (i, 0)), pl.BlockSpec((1, k), lambda i, j: (0, 0)),
                  pl.BlockSpec((tm, k), lambda i, j: (i, 0))],
        out_specs=[pl.BlockSpec((tm, k), lambda i, j: (i, 0)), pl.BlockSpec((1, k), lambda i, j: (0, 0))],
        out_shape=[_SDS((m, k), F32), _SDS((1, k), F32)],
        scratch_shapes=[pltpu.VMEM((tm, k), F32)],
        name=name, compiler_params=_cp("arbitrary", "arbitrary"))(dy, w, h, g, dh_in)


def _matmul_tn(x, x_spec, dy, dy_spec, m, n_blocks, acc_shape, out_shape, out_spec, name, split=None):
    tm = _rows(m)
    nm = m // tm

    def body(x_ref, dy_ref, o_ref, acc_ref):
        @pl.when(pl.program_id(1) == 0)
        def _():
            acc_ref[...] = jnp.zeros_like(acc_ref)

        acc_ref[...] += _mm_tn(x_ref[...], dy_ref[...])

        @pl.when(pl.program_id(1) == nm - 1)
        def _():
            if split is None:
                o_ref[...] = acc_ref[...].reshape(o_ref.shape).astype(o_ref.dtype)
            else:
                for d in range(N_DEV):
                    o_ref[d] = acc_ref[:, d * split:(d + 1) * split].astype(o_ref.dtype)

    return pl.pallas_call(
        body, grid=(n_blocks, nm), in_specs=[x_spec, dy_spec], out_specs=out_spec,
        out_shape=_SDS(out_shape, _WIRE), scratch_shapes=[pltpu.VMEM(acc_shape, F32)],
        name=name, compiler_params=_cp("arbitrary", "arbitrary"))(x, dy)


def _loss_head(h, g, tgt, name):
    m, k = h.shape
    tm = _rows(m)

    def body(h_ref, g_ref, t_ref, loss_ref, dh_ref, dg_ref):
        i = pl.program_id(0)
        xv = h_ref[...]
        r = lax.rsqrt(jnp.mean(xv * xv, axis=-1, keepdims=True) + EPS)
        xh = xv * r
        gv = g_ref[...]
        err = xh * gv - t_ref[...]
        lpart = jnp.zeros((1, 128), F32) + 0.5 * jnp.sum(jnp.mean(err * err, axis=-1, keepdims=True), axis=0, keepdims=True)
        dy = err * (1.0 / k)
        gpart = jnp.sum(dy * xh, axis=0, keepdims=True)

        @pl.when(i == 0)
        def _():
            loss_ref[...] = lpart
            dg_ref[...] = gpart

        @pl.when(i > 0)
        def _():
            loss_ref[...] += lpart
            dg_ref[...] += gpart

        t = dy * gv
        dh_ref[...] = r * (t - xh * jnp.mean(t * xh, axis=-1, keepdims=True))

    return pl.pallas_call(
        body, grid=(m // tm,),
        in_specs=[pl.BlockSpec((tm, k), lambda i: (i, 0)), pl.BlockSpec((1, k), lambda i: (0, 0)),
                  pl.BlockSpec((tm, k), lambda i: (i, 0))],
        out_specs=[pl.BlockSpec((1, 128), lambda i: (0, 0)), pl.BlockSpec((tm, k), lambda i: (i, 0)),
                   pl.BlockSpec((1, k), lambda i: (0, 0))],
        out_shape=[_SDS((1, 128), F32), _SDS((m, k), F32), _SDS((1, k), F32)],
        name=name, compiler_params=_cp("arbitrary"))(h, g, tgt)


def _glu_fwd(gu, conv_w, conv_b, name):
    s = gu.shape[2]
    tm = _rows(s)

    def body(gu_ref, prev_ref, w_ref, b_ref, act_ref):
        i = pl.program_id(0)
        prev = jnp.where(i > 0, prev_ref[...], 0.0)
        ext = jnp.concatenate([prev, gu_ref[0]], axis=0)
        gc = b_ref[...] + w_ref[FFN_CONV - 1:FFN_CONV, :] * ext
        for j in range(FFN_CONV - 1):
            gc = gc + w_ref[j:j + 1, :] * pltpu.roll(ext, FFN_CONV - 1 - j, 0)
        act_ref[...] = (_silu(gc[HALO:]) * gu_ref[1]).astype(act_ref.dtype)

    return pl.pallas_call(
        body, grid=(s // tm, FF_BLOCKS),
        in_specs=[pl.BlockSpec((2, None, tm, GU_SHARD), lambda i, j: (0, j, i, 0)),
                  pl.BlockSpec((None, None, HALO, GU_SHARD), lambda i, j: (0, j, jnp.maximum(i * (tm // HALO) - 1, 0), 0)),
                  pl.BlockSpec((None, HALO, GU_SHARD), lambda i, j: (j, 0, 0)),
                  pl.BlockSpec((None, 1, GU_SHARD), lambda i, j: (j, 0, 0))],
        out_specs=pl.BlockSpec((None, tm, GU_SHARD), lambda i, j: (j, i, 0)),
        out_shape=_SDS((FF_BLOCKS, s, GU_SHARD), _ACT), name=name,
        compiler_params=_cp("arbitrary", "arbitrary"))(gu, gu, conv_w, conv_b)


def _glu_bwd(gu, conv_w, conv_b, dact, name):
    s = gu.shape[2]
    tm = _rows(s)
    nt = s // tm
    ext_rows = tm + HALO

    def body(gu_ref, prev_ref, w_ref, b_ref, dact_ref, dgu_ref, dw_ref, db_ref, carry_ref):
        t = pl.program_id(1)
        i = nt - 1 - t

        @pl.when(t == 0)
        def _():
            carry_ref[...] = jnp.zeros_like(carry_ref)
            dw_ref[...] = jnp.zeros_like(dw_ref)
            db_ref[...] = jnp.zeros_like(db_ref)

        up = gu_ref[1]
        prev = jnp.where(i > 0, prev_ref[...], 0.0)
        ext = jnp.concatenate([prev, gu_ref[0]], axis=0)
        shifted = [pltpu.roll(ext, FFN_CONV - 1 - j, 0) if j < FFN_CONV - 1 else ext for j in range(FFN_CONV)]
        gc = b_ref[...] + shifted[0] * w_ref[0:1, :]
        for j in range(1, FFN_CONV):
            gc = gc + shifted[j] * w_ref[j:j + 1, :]
        gc = gc[HALO:]
        sg = jax.nn.sigmoid(gc)
        da = dact_ref[...]
        dup = da * (gc * sg)
        dgc = da * up * (sg * (1.0 + gc * (1.0 - sg)))
        db_ref[...] += jnp.sum(dgc, axis=0, keepdims=True)
        dgc_ext = jnp.concatenate([jnp.zeros((HALO, GU_SHARD), F32), dgc], axis=0)
        dext = dgc_ext * w_ref[FFN_CONV - 1:FFN_CONV, :]
        for j in range(FFN_CONV):
            dw_ref[j:j + 1, :] += jnp.sum(shifted[j] * dgc_ext, axis=0, keepdims=True)
            if j < FFN_CONV - 1:
                dext = dext + w_ref[j:j + 1, :] * pltpu.roll(dgc_ext, ext_rows - (FFN_CONV - 1 - j), 0)
        tail = jnp.concatenate([jnp.zeros((tm - HALO, GU_SHARD), F32), carry_ref[...]], axis=0)
        dgate = dext[HALO:] + tail
        carry_ref[...] = dext[:HALO]
        dgu_ref[0] = dgate.astype(dgu_ref.dtype)
        dgu_ref[1] = dup.astype(dgu_ref.dtype)

    return pl.pallas_call(
        body, grid=(FF_BLOCKS, nt),
        in_specs=[pl.BlockSpec((2, None, tm, GU_SHARD), lambda j, t: (0, j, nt - 1 - t, 0)),
                  pl.BlockSpec((None, None, HALO, GU_SHARD),
                               lambda j, t: (0, j, jnp.maximum((nt - 1 - t) * (tm // HALO) - 1, 0), 0)),
                  pl.BlockSpec((None, HALO, GU_SHARD), lambda j, t: (j, 0, 0)),
                  pl.BlockSpec((None, 1, GU_SHARD), lambda j, t: (j, 0, 0)),
                  pl.BlockSpec((None, tm, GU_SHARD), lambda j, t: (j, nt - 1 - t, 0))],
        out_specs=[pl.BlockSpec((2, None, tm, GU_SHARD), lambda j, t: (0, j, nt - 1 - t, 0)),
                   pl.BlockSpec((None, HALO, GU_SHARD), lambda j, t: (j, 0, 0)),
                   pl.BlockSpec((None, 1, GU_SHARD), lambda j, t: (j, 0, 0))],
        out_shape=[_SDS(gu.shape, _ACT), _SDS((FF_BLOCKS, HALO, GU_SHARD), F32), _SDS((FF_BLOCKS, 1, GU_SHARD), F32)],
        scratch_shapes=[pltpu.VMEM((HALO, GU_SHARD), F32)],
        name=name, compiler_params=_cp("arbitrary", "arbitrary"))(gu, gu, conv_w, conv_b, dact)


def _bucket_table():
    qi = np.arange(BLK)[:, None]
    kj = np.arange(BLK)[None, :]
    n = np.where(kj > qi, BLK + qi - kj, qi - kj)
    max_exact = N_BUCKETS // 2
    nf = np.maximum(n, 1).astype(np.float32)
    large = max_exact + (np.log(nf / max_exact) / math.log(MAX_DISTANCE / max_exact)
                         * (N_BUCKETS - max_exact)).astype(np.int32)
    large = np.minimum(large, N_BUCKETS - 1)
    return np.where(n < max_exact, n, large).astype(np.int32)


def _lane_low():
    return lax.broadcasted_iota(jnp.int32, (1, 128), 1) < A_HD


def _swa_group(q, kd, vd, sink, bias, upper, first):
    n = A_HEADS // A_KV_HEADS
    low = _lane_low()
    pairs = [q[:, p * 128:(p + 1) * 128] for p in range(n // 2)]
    qm = jnp.concatenate([jnp.where(low == (h % 2 == 0), pairs[h // 2], 0.0) for h in range(n)], axis=0)
    s2 = _mm_nt(qm, kd) * (A_HD ** -0.5)
    s = jnp.where(upper[None], s2[:, :BLK].reshape(n, BLK, BLK), s2[:, BLK:].reshape(n, BLK, BLK)) + bias
    s = jnp.where((upper & first)[None], -jnp.inf, s)
    m = jnp.maximum(jnp.max(s, axis=-1, keepdims=True), sink)
    p = jnp.exp(s - m)
    split = jnp.concatenate([jnp.where(upper[None], p, 0.0), jnp.where(upper[None], 0.0, p)], axis=-1)
    split = split.reshape(n * BLK, 2 * BLK)
    den = _mm(p.reshape(n * BLK, BLK), jnp.ones((BLK, 128), F32)) + jnp.exp(sink - m).reshape(n * BLK, 1)
    o = _mm(split, vd) / den
    return jnp.concatenate([jnp.where(low, o[2 * p * BLK:(2 * p + 1) * BLK], o[(2 * p + 1) * BLK:(2 * p + 2) * BLK])
                            for p in range(n // 2)], axis=1)


def _swa_sinks(sink_ref, g):
    n = A_HEADS // A_KV_HEADS
    return jnp.concatenate([sink_ref[:, h:h + 1] for h in range(g * n, (g + 1) * n)], axis=0).reshape(n, 1, 1)


def _both_halves(t, t_rolled, g):
    low = _lane_low()
    return jnp.where(low, t, t_rolled) if g == 0 else jnp.where(low, t_rolled, t)


def _cross_pairs(q, mk, mv):
    rows = q.shape[0]
    low = _lane_low()
    qm = [jnp.concatenate([jnp.where(low, q[:, p * 128:(p + 1) * 128], 0.0), jnp.where(low, 0.0, q[:, p * 128:(p + 1) * 128])], axis=0)
          for p in range(X_HEADS // 2)]
    s = [_mm_nt(qm[p], mk[:, p * 128:(p + 1) * 128]) * (X_HD ** -0.5) for p in range(X_HEADS // 2)]
    e = [jnp.exp(t - jnp.max(t, axis=-1, keepdims=True)) for t in s]
    pr = [t / jnp.sum(t, axis=-1, keepdims=True) for t in e]
    o = [_mm(pr[p], mv[:, p * 128:(p + 1) * 128]) for p in range(X_HEADS // 2)]
    return jnp.concatenate([jnp.where(low, t[:rows], t[rows:]) for t in o], axis=1)


def _swa_upper():
    qi = lax.broadcasted_iota(jnp.int32, (BLK, BLK), 0)
    kj = lax.broadcasted_iota(jnp.int32, (BLK, BLK), 1)
    return kj > qi


def _bias_build(rel_bias, bucket, name):
    def body(rb_ref, bucket_ref, o_ref):
        b = bucket_ref[...]
        for h in range(A_HEADS):
            acc = jnp.zeros((BLK, BLK), F32)
            for k in range(N_BUCKETS):
                acc = jnp.where(b == k, rb_ref[k, h], acc)
            o_ref[h] = acc

    return pl.pallas_call(
        body, in_specs=[pl.BlockSpec(memory_space=pltpu.SMEM), pl.BlockSpec(memory_space=pltpu.VMEM)],
        out_specs=pl.BlockSpec(memory_space=pltpu.VMEM),
        out_shape=_SDS((A_HEADS, BLK, BLK), F32), name=name)(rel_bias, bucket)


def _bias_reduce(dbias, bucket, name):
    def body(db_ref, bucket_ref, o_ref):
        b = bucket_ref[...]
        row = lax.broadcasted_iota(jnp.int32, (N_BUCKETS, 128), 0)
        lane = lax.broadcasted_iota(jnp.int32, (N_BUCKETS, 128), 1)
        acc = jnp.zeros((N_BUCKETS, 128), F32)
        for h in range(A_HEADS):
            v = db_ref[h]
            for k in range(N_BUCKETS):
                sk = jnp.sum(jnp.sum(jnp.where(b == k, v, 0.0), axis=1, keepdims=True), axis=0, keepdims=True)
                acc = acc + jnp.where((row == k) & (lane == h), sk, 0.0)
        o_ref[...] = acc

    return pl.pallas_call(
        body, in_specs=[pl.BlockSpec(memory_space=pltpu.VMEM)] * 2,
        out_specs=pl.BlockSpec(memory_space=pltpu.VMEM),
        out_shape=_SDS((N_BUCKETS, 128), F32), name=name)(dbias, bucket)


def _mix_a_fwd(proj, bias, sinks, memkv, name):
    s = proj.shape[0]
    nb = s // BLK
    grp = A_HEADS // A_KV_HEADS

    def body(proj_ref, prev_ref, bias_ref, sink_ref, memkv_ref, o_ref):
        i = pl.program_id(0)
        upper = _swa_upper()
        prev = prev_ref[...]
        kb = jnp.concatenate([prev[:, :A_KV], proj_ref[:, A_Q:A_Q + A_KV]], axis=0)
        vb = jnp.concatenate([prev[:, A_KV:], proj_ref[:, A_Q + A_KV:A_Q + 2 * A_KV]], axis=0)
        kb_r = pltpu.roll(kb, A_HD, 1)
        vb_r = pltpu.roll(vb, A_HD, 1)
        gw = A_Q // A_KV_HEADS
        outs = [_swa_group(proj_ref[:, g * gw:(g + 1) * gw], _both_halves(kb, kb_r, g), _both_halves(vb, vb_r, g),
                           _swa_sinks(sink_ref, g), bias_ref[g * grp:(g + 1) * grp], upper, i == 0) for g in range(A_KV_HEADS)]
        outs.append(_cross_pairs(proj_ref[:, A_Q + 2 * A_KV:], memkv_ref[:, :X_Q], memkv_ref[:, X_Q:]))
        o_ref[...] = jnp.concatenate(outs, axis=1).astype(o_ref.dtype)

    return pl.pallas_call(
        body, grid=(nb,),
        in_specs=[pl.BlockSpec((BLK, IN_A), lambda i: (i, 0)),
                  pl.BlockSpec((BLK, 2 * A_KV), lambda i: (jnp.maximum(i - 1, 0), A_Q // (2 * A_KV))),
                  pl.BlockSpec((A_HEADS, BLK, BLK), lambda i: (0, 0, 0)),
                  pl.BlockSpec((1, 128), lambda i: (0, 0)),
                  pl.BlockSpec((MEM_LEN, 2 * X_Q), lambda i: (0, 0))],
        out_specs=pl.BlockSpec((BLK, D), lambda i: (i, 0)),
        out_shape=_SDS((s, D), _ACT), name=name, compiler_params=_cp("arbitrary"))(proj, proj, bias, sinks, memkv)


def _mix_a_bwd(proj, bias, sinks, memkv, dmix, name):
    s = proj.shape[0]
    nb = s // BLK
    grp = A_HEADS // A_KV_HEADS

    def body(proj_ref, prev_ref, bias_ref, sink_ref, memkv_ref, dmix_ref,
             dproj_ref, dbias_ref, dsink_ref, dmemkv_ref, carry_ref):
        t = pl.program_id(0)
        i = nb - 1 - t

        @pl.when(t == 0)
        def _():
            carry_ref[...] = jnp.zeros_like(carry_ref)
            dbias_ref[...] = jnp.zeros_like(dbias_ref)
            dsink_ref[...] = jnp.zeros_like(dsink_ref)
            dmemkv_ref[...] = jnp.zeros_like(dmemkv_ref)

        upper = _swa_upper()
        lane = lax.broadcasted_iota(jnp.int32, (1, 128), 1)
        low = _lane_low()
        prev = prev_ref[...]
        kb = jnp.concatenate([prev[:, :A_KV], proj_ref[:, A_Q:A_Q + A_KV]], axis=0)
        vb = jnp.concatenate([prev[:, A_KV:], proj_ref[:, A_Q + A_KV:A_Q + 2 * A_KV]], axis=0)
        kb_r = pltpu.roll(kb, A_HD, 1)
        vb_r = pltpu.roll(vb, A_HD, 1)
        gw = A_Q // A_KV_HEADS
        dqs, dkd, dvd = [], [], []
        dsink = jnp.zeros((1, 128), F32)
        for g in range(A_KV_HEADS):
            _, vjp = jax.vjp(functools.partial(_swa_group, upper=upper, first=i == 0), proj_ref[:, g * gw:(g + 1) * gw],
                             _both_halves(kb, kb_r, g), _both_halves(vb, vb_r, g), _swa_sinks(sink_ref, g),
                             bias_ref[g * grp:(g + 1) * grp])
            dq, dk, dv, ds, db = vjp(dmix_ref[:, g * gw:(g + 1) * gw].astype(F32))
            dqs.append(dq)
            dkd.append(dk + pltpu.roll(dk, A_HD, 1))
            dvd.append(dv + pltpu.roll(dv, A_HD, 1))
            for h in range(grp):
                dsink = dsink + jnp.where(lane == g * grp + h, ds[h], 0.0)
            dbias_ref[g * grp:(g + 1) * grp] += db
        dsink_ref[...] += dsink
        dkb = jnp.where(low, dkd[0], dkd[1])
        dvb = jnp.where(low, dvd[0], dvd[1])
        _, vjp = jax.vjp(_cross_pairs, proj_ref[:, A_Q + 2 * A_KV:], memkv_ref[:, :X_Q], memkv_ref[:, X_Q:])
        dxq, dmk, dmv = vjp(dmix_ref[:, A_Q:].astype(F32))
        dmemkv_ref[...] += jnp.concatenate([dmk, dmv], axis=1)
        dkv_cur = jnp.concatenate([dkb[BLK:], dvb[BLK:]], axis=1) + carry_ref[...]
        carry_ref[...] = jnp.concatenate([dkb[:BLK], dvb[:BLK]], axis=1)
        dproj_ref[...] = jnp.concatenate(dqs + [dkv_cur, dxq], axis=1)

    return pl.pallas_call(
        body, grid=(nb,),
        in_specs=[pl.BlockSpec((BLK, IN_A), lambda t: (nb - 1 - t, 0)),
                  pl.BlockSpec((BLK, 2 * A_KV), lambda t: (jnp.maximum(nb - 2 - t, 0), A_Q // (2 * A_KV))),
                  pl.BlockSpec((A_HEADS, BLK, BLK), lambda t: (0, 0, 0)),
                  pl.BlockSpec((1, 128), lambda t: (0, 0)),
                  pl.BlockSpec((MEM_LEN, 2 * X_Q), lambda t: (0, 0)),
                  pl.BlockSpec((BLK, D), lambda t: (nb - 1 - t, 0))],
        out_specs=[pl.BlockSpec((BLK, IN_A), lambda t: (nb - 1 - t, 0)),
                   pl.BlockSpec((A_HEADS, BLK, BLK), lambda t: (0, 0, 0)),
                   pl.BlockSpec((1, 128), lambda t: (0, 0)),
                   pl.BlockSpec((MEM_LEN, 2 * X_Q), lambda t: (0, 0))],
        out_shape=[_SDS((s, IN_A), F32), _SDS((A_HEADS, BLK, BLK), F32), _SDS((1, 128), F32),
                   _SDS((MEM_LEN, 2 * X_Q), F32)],
        scratch_shapes=[pltpu.VMEM((BLK, 2 * A_KV), F32)],
        name=name, compiler_params=_cp("arbitrary"))(proj, proj, bias, sinks, memkv, dmix)


def _dn_heads(yq, yk, yv, z, bl, al, a_log, dtb, ng, s0):
    c = CHUNK
    nh = B_V_HEADS
    rep = B_V_HEADS // B_QK_HEADS
    r = lax.broadcasted_iota(jnp.int32, (c, c), 0)
    cc = lax.broadcasted_iota(jnp.int32, (c, c), 1)
    q = [_silu(t) for t in yq]
    k = [_silu(t) for t in yk]
    v = [_silu(t) for t in yv]
    q = [t * lax.rsqrt(jnp.sum(t * t, axis=-1, keepdims=True) + EPS) * (B_HD ** -0.5) for t in q]
    k = [t * lax.rsqrt(jnp.sum(t * t, axis=-1, keepdims=True) + EPS) for t in k]
    beta = [jax.nn.sigmoid(t) for t in bl]
    g = [-jnp.exp(a_log[h]) * jax.nn.softplus(al[h] + dtb[h]) for h in range(nh)]
    gb = [jnp.broadcast_to(t, (c, c)) for t in g]
    gc_col = [jnp.sum(jnp.where(cc <= r, t.T, 0.0), axis=1, keepdims=True) for t in gb]
    gc_row = [jnp.sum(jnp.where(r <= cc, t, 0.0), axis=0, keepdims=True) for t in gb]
    gc_last = [jnp.sum(t, axis=0, keepdims=True) for t in g]
    decay = [jnp.exp(jnp.where(r >= cc, gc_col[h] - gc_row[h], -jnp.inf)) for h in range(nh)]
    kq = [_mmf_nt(jnp.concatenate([k[h], q[h]], axis=0), k[h]) for h in range(B_QK_HEADS)]
    kk = [t[:c] for t in kq]
    qk = [t[c:] for t in kq]
    egc = [jnp.exp(t) for t in gc_col]
    both = [_mmf(jnp.concatenate([(beta[h] * egc[h]) * k[h // rep], q[h // rep] * egc[h]], axis=0), s0[h]) for h in range(nh)]
    rhs = [beta[h] * v[h] - both[h][:c] for h in range(nh)]
    qs0 = [t[c:] for t in both]
    pw = [-(beta[h] * kk[h // rep] * jnp.where(r > cc, decay[h], 0.0)) for h in range(nh)]
    x = rhs
    for lvl in range(6):
        if lvl < 5:
            prod = [_mmf(pw[h], jnp.concatenate([x[h], pw[h]], axis=1)) for h in range(nh)]
            x = [x[h] + prod[h][:, :B_HD] for h in range(nh)]
            pw = [t[:, B_HD:] for t in prod]
        else:
            x = [x[h] + _mmf(pw[h], x[h]) for h in range(nh)]
    delta = x
    last = [_mmf(jnp.concatenate([qk[h // rep] * decay[h], (k[h // rep] * jnp.exp(gc_last[h] - gc_col[h])).T], axis=0), delta[h])
            for h in range(nh)]
    out = [qs0[h] + last[h][:c] for h in range(nh)]
    s1 = [jnp.exp(gc_last[h]) * s0[h] + last[h][c:] for h in range(nh)]
    o = [t * lax.rsqrt(jnp.mean(t * t, axis=-1, keepdims=True) + EPS) * ng for t in out]
    return [o[h] * _silu(z[h]) for h in range(nh)], s1


def _dn_conv(ext, w_ref):
    y = ext * w_ref[B_CONV - 1:B_CONV, :]
    for j in range(B_CONV - 1):
        y = y + w_ref[j:j + 1, :] * pltpu.roll(ext, B_CONV - 1 - j, 0)
    return y


def _dn_args(y, cur_ref, par_ref, ng_ref):
    nh = B_V_HEADS
    return ([y[:, h * B_HD:(h + 1) * B_HD] for h in range(B_QK_HEADS)],
            [y[:, B_QK + h * B_HD:B_QK + (h + 1) * B_HD] for h in range(B_QK_HEADS)],
            [y[:, 2 * B_QK + h * B_HD:2 * B_QK + (h + 1) * B_HD] for h in range(nh)],
            [cur_ref[:, BP_Z + h * B_HD:BP_Z + (h + 1) * B_HD] for h in range(nh)],
            [cur_ref[:, BP_GATE + h:BP_GATE + h + 1] for h in range(nh)],
            [cur_ref[:, BP_GATE + nh + h:BP_GATE + nh + h + 1] for h in range(nh)],
            [par_ref[:, h:h + 1] for h in range(nh)], [par_ref[:, nh + h:nh + h + 1] for h in range(nh)], ng_ref[...])


def _mix_b_fwd(proj, conv_w, par, ng, memkv, name):
    s = proj.shape[0]
    nc = s // CHUNK

    def body(cur_ref, prev_ref, w_ref, par_ref, ng_ref, memkv_ref, o_ref, st_ref, state_ref):
        n = pl.program_id(0)

        @pl.when(n == 0)
        def _():
            state_ref[...] = jnp.zeros_like(state_ref)

        prev = jnp.where(n > 0, prev_ref[...], 0.0)
        ext = jnp.concatenate([prev, cur_ref[:, :B_QKV]], axis=0)
        y = _dn_conv(ext, w_ref)[HALO:]
        s0 = [state_ref[hv] for hv in range(B_V_HEADS)]
        st_ref[0] = state_ref[...]
        outs, s1 = _dn_heads(*_dn_args(y, cur_ref, par_ref, ng_ref), s0)
        for hv in range(B_V_HEADS):
            state_ref[hv] = s1[hv]
        outs = outs + [_cross_pairs(cur_ref[:, BP_XQ:BP_XQ + X_Q], memkv_ref[:, :X_Q], memkv_ref[:, X_Q:])]
        o_ref[...] = jnp.concatenate(outs, axis=1).astype(o_ref.dtype)

    return pl.pallas_call(
        body, grid=(nc,),
        in_specs=[pl.BlockSpec((CHUNK, IN_BP), lambda n: (n, 0)),
                  pl.BlockSpec((HALO, B_QKV), lambda n: (jnp.maximum(n * (CHUNK // HALO) - 1, 0), 0)),
                  pl.BlockSpec((HALO, B_QKV), lambda n: (0, 0)),
                  pl.BlockSpec((1, 128), lambda n: (0, 0)), pl.BlockSpec((1, 128), lambda n: (0, 0)),
                  pl.BlockSpec((MEM_LEN, 2 * X_Q), lambda n: (0, 0))],
        out_specs=[pl.BlockSpec((CHUNK, D), lambda n: (n, 0)),
                   pl.BlockSpec((1, B_V_HEADS, B_HD, B_HD), lambda n: (n, 0, 0, 0))],
        out_shape=[_SDS((s, D), _ACT), _SDS((nc, B_V_HEADS, B_HD, B_HD), F32)],
        scratch_shapes=[pltpu.VMEM((B_V_HEADS, B_HD, B_HD), F32)],
        name=name, compiler_params=_cp("arbitrary"))(proj, proj, conv_w, par, ng, memkv)


def _mix_b_bwd(proj, conv_w, par, ng, memkv, states, dmix, name):
    s = proj.shape[0]
    nc = s // CHUNK
    ext_rows = CHUNK + HALO

    def body(cur_ref, prev_ref, w_ref, par_ref, ng_ref, memkv_ref, st_ref, dmix_ref,
             dproj_ref, dw_ref, dpar_ref, dng_ref, dmemkv_ref, dstate_ref, carry_ref):
        t = pl.program_id(0)
        n = nc - 1 - t

        @pl.when(t == 0)
        def _():
            dstate_ref[...] = jnp.zeros_like(dstate_ref)
            carry_ref[...] = jnp.zeros_like(carry_ref)
            dw_ref[...] = jnp.zeros_like(dw_ref)
            dpar_ref[...] = jnp.zeros_like(dpar_ref)
            dng_ref[...] = jnp.zeros_like(dng_ref)
            dmemkv_ref[...] = jnp.zeros_like(dmemkv_ref)

        lane = lax.broadcasted_iota(jnp.int32, (1, 128), 1)
        prev = jnp.where(n > 0, prev_ref[...], 0.0)
        ext = jnp.concatenate([prev, cur_ref[:, :B_QKV]], axis=0)
        y = _dn_conv(ext, w_ref)[HALO:]
        _, vjp = jax.vjp(_dn_heads, *_dn_args(y, cur_ref, par_ref, ng_ref), [st_ref[0, hv] for hv in range(B_V_HEADS)])
        dyq, dyk, dyv, dz, gbl, gal, ga_log, gdtb, dng, gs0 = vjp(
            ([dmix_ref[:, hv * B_HD:(hv + 1) * B_HD].astype(F32) for hv in range(B_V_HEADS)],
             [dstate_ref[hv] for hv in range(B_V_HEADS)]))
        dgate = jnp.zeros((CHUNK, 128), F32)
        dpar = jnp.zeros((1, 128), F32)
        for hv in range(B_V_HEADS):
            dstate_ref[hv] = gs0[hv]
            dgate = dgate + jnp.where(lane == hv, gbl[hv], 0.0) + jnp.where(lane == B_V_HEADS + hv, gal[hv], 0.0)
            dpar = dpar + jnp.where(lane == hv, ga_log[hv], 0.0) + jnp.where(lane == B_V_HEADS + hv, gdtb[hv], 0.0)
        dpar_ref[...] += dpar
        dng_ref[...] += dng
        _, vjp = jax.vjp(_cross_pairs, cur_ref[:, BP_XQ:BP_XQ + X_Q], memkv_ref[:, :X_Q], memkv_ref[:, X_Q:])
        dxq, dmk, dmv = vjp(dmix_ref[:, B_V:].astype(F32))
        dmemkv_ref[...] += jnp.concatenate([dmk, dmv], axis=1)
        dy = jnp.concatenate(list(dyq) + list(dyk) + list(dyv), axis=1)
        dy_ext = jnp.concatenate([jnp.zeros((HALO, B_QKV), F32), dy], axis=0)
        dext = dy_ext * w_ref[B_CONV - 1:B_CONV, :]
        dw_ref[B_CONV - 1:B_CONV, :] += jnp.sum(ext * dy_ext, axis=0, keepdims=True)
        for j in range(B_CONV - 1):
            sh = B_CONV - 1 - j
            dw_ref[j:j + 1, :] += jnp.sum(pltpu.roll(ext, sh, 0) * dy_ext, axis=0, keepdims=True)
            dext = dext + w_ref[j:j + 1, :] * pltpu.roll(dy_ext, ext_rows - sh, 0)
        tail = jnp.concatenate([jnp.zeros((CHUNK - HALO, B_QKV), F32), carry_ref[...]], axis=0)
        dqkv = dext[HALO:] + tail
        carry_ref[...] = dext[:HALO]
        dproj_ref[...] = jnp.concatenate([dqkv] + list(dz) + [dxq, dgate], axis=1)

    return pl.pallas_call(
        body, grid=(nc,),
        in_specs=[pl.BlockSpec((CHUNK, IN_BP), lambda t: (nc - 1 - t, 0)),
                  pl.BlockSpec((HALO, B_QKV), lambda t: (jnp.maximum((nc - 1 - t) * (CHUNK // HALO) - 1, 0), 0)),
                  pl.BlockSpec((HALO, B_QKV), lambda t: (0, 0)),
                  pl.BlockSpec((1, 128), lambda t: (0, 0)), pl.BlockSpec((1, 128), lambda t: (0, 0)),
                  pl.BlockSpec((MEM_LEN, 2 * X_Q), lambda t: (0, 0)),
                  pl.BlockSpec((1, B_V_HEADS, B_HD, B_HD), lambda t: (nc - 1 - t, 0, 0, 0)),
                  pl.BlockSpec((CHUNK, D), lambda t: (nc - 1 - t, 0))],
        out_specs=[pl.BlockSpec((CHUNK, IN_BP), lambda t: (nc - 1 - t, 0)),
                   pl.BlockSpec((HALO, B_QKV), lambda t: (0, 0)),
                   pl.BlockSpec((1, 128), lambda t: (0, 0)), pl.BlockSpec((1, 128), lambda t: (0, 0)),
                   pl.BlockSpec((MEM_LEN, 2 * X_Q), lambda t: (0, 0))],
        out_shape=[_SDS((s, IN_BP), F32), _SDS((HALO, B_QKV), F32), _SDS((1, 128), F32), _SDS((1, 128), F32),
                   _SDS((MEM_LEN, 2 * X_Q), F32)],
        scratch_shapes=[pltpu.VMEM((B_V_HEADS, B_HD, B_HD), F32), pltpu.VMEM((HALO, B_QKV), F32)],
        name=name, compiler_params=_cp("arbitrary"))(proj, proj, conv_w, par, ng, memkv, states, dmix)


def _place():
    return lax.axis_index("x"), lax.axis_index("y"), lax.axis_index("c")


def _all_gather(shards, name):
    n = len(shards)

    def body(*refs):
        ins, outs = refs[:n], refs[n:2 * n]
        send_sems, recv_sems, local_sems = refs[2 * n:]
        x, y, c = _place()
        me, sibling = (x, y, c), (x, y, 1 - c)
        chips = [(1 - x, y), (x, 1 - y), (1 - x, 1 - y)]

        def rows(a, px, py, pc):
            return outs[a].at[4 * px + 2 * py + pc]

        def copy(a, k, block, to, src=None):
            return pltpu.make_async_remote_copy(
                src_ref=rows(a, *block) if src is None else src, dst_ref=rows(a, *block),
                send_sem=send_sems.at[a, k], recv_sem=recv_sems.at[a, k],
                device_id=to, device_id_type=pl.DeviceIdType.MESH)

        mine = [pltpu.make_async_copy(ins[a], rows(a, *me), local_sems.at[a]) for a in range(n)]
        for cp in mine:
            cp.start()
        first = []
        for a in range(n):
            first.append(copy(a, 0, me, sibling, src=ins[a]))
            first += [copy(a, 1 + j, me, (*chip, c), src=ins[a]) for j, chip in enumerate(chips)]
        for cp in first:
            cp.start()
        passed = []
        for j, chip in enumerate(chips):
            for a in range(n):
                copy(a, 1 + j, (*chip, c), me).wait_recv()
                fwd = copy(a, 4 + j, (*chip, c), sibling)
                fwd.start()
                passed.append(fwd)
        for a in range(n):
            copy(a, 0, sibling, me).wait_recv()
            for j, chip in enumerate(chips):
                copy(a, 4 + j, (*chip, 1 - c), me).wait_recv()
        for cp in first + passed:
            cp.wait_send()
        for cp in mine:
            cp.wait()

    hbm = pl.BlockSpec(memory_space=pl.ANY)
    return pl.pallas_call(
        body, out_shape=[_SDS((N_DEV,) + s.shape, s.dtype) for s in shards],
        in_specs=[hbm] * n, out_specs=[hbm] * n,
        scratch_shapes=[pltpu.SemaphoreType.DMA((n, 7)), pltpu.SemaphoreType.DMA((n, 7)), pltpu.SemaphoreType.DMA((n,))],
        name=name)(*shards)


class _Sends:
    def __init__(self, plan, sems, srcs, lands, token):
        self.plan, self.sems, self.srcs, self.lands, self.token = plan, sems, srcs, lands, token


def _send_refs(plan, src_refs, land_refs, x, y, c):
    my = 4 * x + 2 * y + c

    def src_for(a, dest):
        return src_refs[a].at[dest] if plan[a][1] else src_refs[a]

    def slot(a, source):
        return land_refs[plan[a][0]].at[source]

    return my, src_for, slot


def _send_start(srcs, land_shapes, plan, name):
    n, nl = len(srcs), len(land_shapes)

    def body(*refs):
        src_refs, land_refs = refs[:n], refs[n:n + nl]
        send_sems, recv_sems = refs[n + nl], refs[n + nl + 1]
        token, local_sems = refs[2 * (n + nl) + 2], refs[2 * (n + nl) + 3]
        x, y, c = _place()
        my, src_for, slot = _send_refs(plan, src_refs, land_refs, x, y, c)
        mine = [pltpu.make_async_copy(src_for(a, my), slot(a, my), local_sems.at[a]) for a in range(n)]
        for cp in mine:
            cp.start()
        for k in range(N_DEV - 1):
            px, py, pc = x ^ ((k + 1) >> 2 & 1), y ^ ((k + 1) >> 1 & 1), c ^ ((k + 1) & 1)
            for a in range(n):
                pltpu.make_async_remote_copy(
                    src_ref=src_for(a, 4 * px + 2 * py + pc), dst_ref=slot(a, my), send_sem=send_sems.at[a * (N_DEV - 1) + k],
                    recv_sem=recv_sems.at[a * (N_DEV - 1) + k], device_id=(px, py, pc), device_id_type=pl.DeviceIdType.MESH).start()
        for cp in mine:
            cp.wait()
        token[...] = jnp.zeros_like(token)

    hbm = pl.BlockSpec(memory_space=pltpu.HBM)
    sem = pl.BlockSpec(memory_space=pltpu.SEMAPHORE)
    lands = [pltpu.with_memory_space_constraint(lax.empty(s, d), pltpu.HBM) for s, d in land_shapes]
    srcs = [pltpu.with_memory_space_constraint(s, pltpu.HBM) for s in srcs]
    out = pl.pallas_call(
        body, name=name,
        out_shape=(pltpu.SemaphoreType.DMA((n * (N_DEV - 1),)), pltpu.SemaphoreType.DMA((n * (N_DEV - 1),)),
                   *[pltpu.HBM(s.shape, s.dtype) for s in srcs], *[pltpu.HBM(s, d) for s, d in land_shapes],
                   _SDS((8, 128), F32)),
        in_specs=[hbm] * (n + nl),
        out_specs=(sem, sem, *[hbm] * (n + nl), pl.BlockSpec(memory_space=pltpu.VMEM)),
        input_output_aliases={i: 2 + i for i in range(n + nl)},
        scratch_shapes=[pltpu.SemaphoreType.DMA((n,))],
        compiler_params=pltpu.CompilerParams(has_side_effects=pltpu.SideEffectType.DATAFLOW_SIDE_EFFECTING),
    )(*srcs, *lands)
    return _Sends(plan, out[:2], out[2:2 + n], out[2 + n:2 + n + nl], out[-1])


def _send_wait(sends, after, name):
    plan = sends.plan
    n, nl = len(sends.srcs), len(sends.lands)

    def body(*refs):
        src_refs, land_refs = refs[:n], refs[n:n + nl]
        send_sems, recv_sems = refs[n + nl], refs[n + nl + 1]
        x, y, c = _place()
        my, src_for, slot = _send_refs(plan, src_refs, land_refs, x, y, c)
        for k in range(N_DEV - 1):
            px, py, pc = x ^ ((k + 1) >> 2 & 1), y ^ ((k + 1) >> 1 & 1), c ^ ((k + 1) & 1)
            peer = 4 * px + 2 * py + pc
            for a in range(n):
                cp = pltpu.make_async_remote_copy(
                    src_ref=src_for(a, peer), dst_ref=slot(a, peer), send_sem=send_sems.at[a * (N_DEV - 1) + k],
                    recv_sem=recv_sems.at[a * (N_DEV - 1) + k], device_id=(px, py, pc), device_id_type=pl.DeviceIdType.MESH)
                cp.wait_send()
                cp.wait_recv()

    hbm = pl.BlockSpec(memory_space=pltpu.HBM)
    sem = pl.BlockSpec(memory_space=pltpu.SEMAPHORE)
    out = pl.pallas_call(
        body, name=name,
        out_shape=tuple(pltpu.HBM(s.shape, s.dtype) for s in (*sends.srcs, *sends.lands)),
        in_specs=[hbm] * (n + nl) + [sem, sem, pl.BlockSpec(memory_space=pl.ANY)],
        out_specs=tuple([hbm] * (n + nl)),
        input_output_aliases={i: i for i in range(n + nl)},
        compiler_params=pltpu.CompilerParams(has_side_effects=pltpu.SideEffectType.DATAFLOW_SIDE_EFFECTING),
    )(*sends.srcs, *sends.lands, *sends.sems, after)
    return list(out[n:])


def _adam_update(g, w, m, v):
    c1 = 1.0 - ADAM_B1 ** ADAM_STEP
    c2 = 1.0 - ADAM_B2 ** ADAM_STEP
    mm = ADAM_B1 * m + (1.0 - ADAM_B1) * g
    vv = ADAM_B2 * v + (1.0 - ADAM_B2) * (g * g)
    delta = -ADAM_LR * ((mm / c1) / (jnp.sqrt(vv / c2) + ADAM_EPS) + ADAM_WD * w)
    return delta, mm, vv


def _sum_sources(p_ref):
    g = p_ref[0].astype(F32)
    for s in range(1, N_DEV):
        g = g + p_ref[s].astype(F32)
    return g


def _adamw(parts, w, m, v, tr, name, restore_b=False):
    nl, r, c = w.shape
    cp = parts[0].shape[-1]

    def body(*refs):
        p_refs = refs[:nl]
        w_ref, m_ref, v_ref, g_ref, d_ref, nm_ref, nv_ref = refs[nl:]
        g = _sum_sources(p_refs[0])
        for l in range(1, nl):
            g = jnp.where(pl.program_id(0) == l, _sum_sources(p_refs[l]), g)
        if restore_b:
            g = jnp.concatenate([g[:, :BP_XQ], g[:, BP_GATE:BP_GATE + 2 * B_V_HEADS], g[:, BP_XQ:BP_GATE]], axis=1)
        delta, mm, vv = _adam_update(g, w_ref[...], m_ref[...], v_ref[...])
        g_ref[...] = g
        d_ref[...] = delta
        nm_ref[...] = mm
        nv_ref[...] = vv

    spec = pl.BlockSpec((None, tr, c), lambda l, i: (l, i, 0))
    part_specs = [pl.BlockSpec((N_DEV, tr, cp), functools.partial(lambda l, i, k: (0, jnp.where(l == k, i, 0), 0), k=k))
                  for k in range(nl)]
    return pl.pallas_call(
        body, grid=(nl, r // tr),
        in_specs=part_specs + [spec, spec, spec],
        out_specs=[spec] * 4, out_shape=[_SDS(w.shape, F32)] * 4,
        name=name, compiler_params=_cp("arbitrary", "arbitrary"))(*parts, w, m, v)


def _pack_small(d_rel, d_cb, d_cw, d_qkv, d_mix, d_mem, d_ffn, d_final, d_sinks, d_par, d_ng, name):
    flat = [d_rel, *d_cb, *d_cw, d_qkv, *d_mix, *d_mem, *d_ffn, d_final, d_sinks, d_par, d_ng]
    n = len(flat)

    def body(*refs):
        ins, o_ref = refs[:n], refs[n]
        rel, cb0, cb1, cw0, cw1, qkv, mx0, mx1, me0, me1, ff0, ff1, fin, snk, par, ng = ins
        o_ref[...] = jnp.zeros_like(o_ref)
        o_ref[SP_REL:SP_REL + N_BUCKETS, 0:128] = rel[...]
        for l, (cb, cw) in enumerate(((cb0, cw0), (cb1, cw1))):
            o_ref[SP_CB + l:SP_CB + l + 1, :] = jnp.concatenate([cb[j] for j in range(FF_BLOCKS)], axis=1)
            full = jnp.concatenate([cw[j] for j in range(FF_BLOCKS)], axis=1)
            o_ref[SP_CW + FFN_CONV * l:SP_CW + FFN_CONV * (l + 1), :] = full[:FFN_CONV]
        o_ref[SP_QKV:SP_QKV + B_CONV, 0:B_QKV] = qkv[0:B_CONV, :]
        for base, pair in ((SP_MIX, (mx0, mx1)), (SP_MEM, (me0, me1)), (SP_FFN, (ff0, ff1))):
            for l in range(2):
                o_ref[base + l:base + l + 1, 0:D] = pair[l][...]
        o_ref[SP_FINAL:SP_FINAL + 1, 0:D] = fin[...]
        o_ref[SP_MISC:SP_MISC + 1, 0:128] = snk[...]
        o_ref[SP_MISC:SP_MISC + 1, 128:256] = par[...]
        o_ref[SP_MISC:SP_MISC + 1, 256:384] = ng[...]

    vm = pl.BlockSpec(memory_space=pltpu.VMEM)
    return pl.pallas_call(body, in_specs=[vm] * n, out_specs=vm, out_shape=_SDS((SMALL_ROWS, D_FF), F32), name=name)(*flat)


_SMALL = ["rel_bias", "norm_mix_g", "norm_mem_g", "sinks_a", "a_log_b", "dt_bias_b", "out_norm_g_b", "norm_ffn_g",
          "ffn_conv_b", "final_norm_g", "conv_qkv_b", "ffn_conv_w"]


def _adamw_small(recv, rc_qkv, rc_ffn, ws, ms, vs, name):
    n = len(_SMALL)

    def body(*refs):
        recv_ref, qkv_ref, ffn_ref = refs[:3]
        w_refs, m_refs, v_refs = refs[3:3 + n], refs[3 + n:3 + 2 * n], refs[3 + 2 * n:3 + 3 * n]
        outs = refs[3 + 3 * n:]
        gs = _sum_sources(recv_ref)
        grads = {
            "rel_bias": gs[SP_REL:SP_REL + N_BUCKETS, 0:A_HEADS],
            "norm_mix_g": gs[SP_MIX:SP_MIX + 2, 0:D], "norm_mem_g": gs[SP_MEM:SP_MEM + 2, 0:D],
            "sinks_a": gs[SP_MISC:SP_MISC + 1, 0:A_HEADS],
            "a_log_b": gs[SP_MISC:SP_MISC + 1, 128:128 + B_V_HEADS],
            "dt_bias_b": gs[SP_MISC:SP_MISC + 1, 128 + B_V_HEADS:128 + 2 * B_V_HEADS],
            "out_norm_g_b": gs[SP_MISC:SP_MISC + 1, 256:256 + B_HD],
            "norm_ffn_g": gs[SP_FFN:SP_FFN + 2, 0:D], "ffn_conv_b": gs[SP_CB:SP_CB + 2, :],
            "final_norm_g": gs[SP_FINAL:SP_FINAL + 1, 0:D],
            "conv_qkv_b": _sum_sources(qkv_ref), "ffn_conv_w": _sum_sources(ffn_ref),
        }
        for i, nm in enumerate(_SMALL):
            g = grads[nm]
            delta, mm, vv = _adam_update(g, w_refs[i][...], m_refs[i][...], v_refs[i][...])
            outs[i][...] = g
            outs[n + i][...] = delta
            outs[2 * n + i][...] = mm
            outs[3 * n + i][...] = vv

    vm = pl.BlockSpec(memory_space=pltpu.VMEM)
    shapes = [_SDS(w.shape, F32) for w in ws]
    return pl.pallas_call(
        body, in_specs=[vm] * (3 + 3 * n), out_specs=[vm] * (4 * n), out_shape=shapes * 4,
        name=name)(recv, rc_qkv, rc_ffn, *ws, *ms, *vs)


def _assemble(gathered, axis):
    g = jnp.moveaxis(gathered, 0, axis)
    shp = list(g.shape)
    return g.reshape(shp[:axis] + [shp[axis] * shp[axis + 1]] + shp[axis + 2:])


def _pad_rows(a, rows):
    return jnp.pad(a, ((0, rows - a.shape[0]), (0, 0)))


def _pad_lanes(a, lanes=128):
    return jnp.pad(a, ((0, 0), (0, lanes - a.shape[1])))


def _ff_blocks(a):
    return jnp.moveaxis(a.reshape(a.shape[0], FF_BLOCKS, GU_SHARD), 1, 0)


def _reorder_b(w):
    qkv_z = w[..., :B_QKV + B_V]
    gates = w[..., B_QKV + B_V:B_QKV + B_V + 2 * B_V_HEADS]
    xq = w[..., IN_B - X_Q:]
    pad = jnp.zeros(w.shape[:-1] + (IN_BP - IN_B,), w.dtype)
    return jnp.concatenate([qkv_z, xq, gates, pad], axis=-1)


def kernel(x, mem, rel_bias, norm_mix_g, norm_mem_g, w_mem_kv, w_out, w_in_a, sinks_a, w_in_b, conv_qkv_b, a_log_b, dt_bias_b, out_norm_g_b, norm_ffn_g, w_gate_up, ffn_conv_w, ffn_conv_b, w_down, final_norm_g, loss_target, m_rel_bias, m_norm_mix_g, m_norm_mem_g, m_w_mem_kv, m_w_out, m_w_in_a, m_sinks_a, m_w_in_b, m_conv_qkv_b, m_a_log_b, m_dt_bias_b, m_out_norm_g_b, m_norm_ffn_g, m_w_gate_up, m_ffn_conv_w, m_ffn_conv_b, m_w_down, m_final_norm_g, v_rel_bias, v_norm_mix_g, v_norm_mem_g, v_w_mem_kv, v_w_out, v_w_in_a, v_sinks_a, v_w_in_b, v_conv_qkv_b, v_a_log_b, v_dt_bias_b, v_out_norm_g_b, v_norm_ffn_g, v_w_gate_up, v_ffn_conv_w, v_ffn_conv_b, v_w_down, v_final_norm_g):
    local = dict(locals())
    order = ["rel_bias", "norm_mix_g", "norm_mem_g", "w_mem_kv", "w_out", "w_in_a", "sinks_a", "w_in_b", "conv_qkv_b",
             "a_log_b", "dt_bias_b", "out_norm_g_b", "norm_ffn_g", "w_gate_up", "ffn_conv_w", "ffn_conv_b", "w_down",
             "final_norm_g"]
    wts = {n: local[n] for n in order}
    moms = {n: local["m_" + n] for n in order}
    vars_ = {n: local["v_" + n] for n in order}
    h0 = x[0]
    memx = mem[0]
    tgt = loss_target[0]
    s = h0.shape[0]
    tm = _rows(s)

    g_mk, g_out, g_ia, g_cq, g_cw = _all_gather(
        [w_mem_kv.astype(_MXU), w_out.astype(_MXU), w_in_a.astype(_MXU), conv_qkv_b, ffn_conv_w], "gather_first")
    gu_land = ((N_DEV, D, GU_SHARD), _MXU)
    dn_land = ((N_DEV, DN_SHARD, D), _MXU)
    whole = [(0, False), (1, False)]
    ffn0_w = _send_start([w_gate_up[0].astype(_MXU), w_down[0].astype(_MXU)], [gu_land, dn_land], whole, "gather_ffn0_start")
    w_ia = _assemble(g_ia, 2)[0]
    conv_qkv = _pad_rows(_assemble(g_cq, 2)[0], HALO)
    ffn_cw_full = _assemble(g_cw, 2)
    ffn_cw = [_ff_blocks(_pad_rows(ffn_cw_full[i], HALO)) for i in range(2)]
    ffn_cb = [_ff_blocks(ffn_conv_b[i:i + 1]) for i in range(2)]
    bucket = jnp.asarray(_bucket_table())
    bias = _bias_build(rel_bias, bucket, "bias_build")
    sinks = _pad_lanes(sinks_a)
    par_b = _pad_lanes(jnp.concatenate([a_log_b, dt_bias_b], axis=1))

    row_x = pl.BlockSpec((tm, D), lambda i, j: (i, 0))
    gu_shape = (2, FF_BLOCKS, s, GU_SHARD)

    def in_proj(h, g, w, w_spec, n_cols, tn, name, deps=()):
        return _norm_matmul(h, g, w, w_spec, n_cols // tn, (h.shape[0], n_cols),
                            pl.BlockSpec((_rows(h.shape[0]), tn), lambda i, j: (i, j)), name, deps=deps)

    def ffn_fwd(i, h, g_gu, g_dn, deps=()):
        gu, hn = _norm_matmul(h, norm_ffn_g[i:i + 1], g_gu, _spec_gate_up(1), N_DEV, gu_shape,
                              _spec_gu_act(0, 1, tm), f"gate_up_{i}", deps=deps)
        act = _glu_fwd(gu, ffn_cw[i], ffn_cb[i], f"glu_fwd_{i}")
        h_new = _matmul_res(act, pl.BlockSpec((None, tm, GU_SHARD), lambda r, j: (j, r, 0)), g_dn, _spec_down(1),
                            FF_BLOCKS, h, f"down_proj_{i}")
        return h_new, gu, hn, act

    def out_proj(i, mix, h):
        return _matmul_res(mix, row_x, g_out, _spec_rowsharded(i, D // N_DEV, D), 1, h, f"out_proj_{i}")

    proj_a, hn_a = in_proj(h0, norm_mix_g[0:1], w_ia, pl.BlockSpec((D, 640), lambda i, j: (0, j)), IN_A, 640, "in_proj_a",
                           deps=[ffn0_w.token])
    memkv0, memn0 = in_proj(memx, norm_mem_g[0:1], g_mk, _spec_rowsharded(0, D // N_DEV, 2 * X_Q), 2 * X_Q, 2 * X_Q, "mem_proj_0")
    mix_a = _mix_a_fwd(proj_a, bias, sinks, memkv0, "mix_a_fwd")
    h1 = out_proj(0, mix_a, h0)
    g_gu0, g_dn0 = _send_wait(ffn0_w, h1, "gather_ffn0_wait")
    in_b_w = _send_start([_reorder_b(w_in_b).astype(_MXU)], [((N_DEV, 1, D // N_DEV, IN_BP), _MXU)], [(0, False)], "gather_in_b_start")
    ffn1_w = _send_start([w_gate_up[1].astype(_MXU), w_down[1].astype(_MXU)], [gu_land, dn_land], whole, "gather_ffn1_start")
    h2, gu0, hn_f0, act0 = ffn_fwd(0, h1, g_gu0, g_dn0, deps=[in_b_w.token, ffn1_w.token])
    g_ib, = _send_wait(in_b_w, h2, "gather_in_b_wait")
    proj_b, hn_b = in_proj(h2, norm_mix_g[1:2], g_ib, _spec_rowsharded(0, D // N_DEV, 896, col_block=1), IN_BP, 896, "in_proj_b")
    memkv1, memn1 = in_proj(memx, norm_mem_g[1:2], g_mk, _spec_rowsharded(1, D // N_DEV, 2 * X_Q), 2 * X_Q, 2 * X_Q, "mem_proj_1")
    mix_b, states = _mix_b_fwd(proj_b, conv_qkv, par_b, out_norm_g_b, memkv1, "mix_b_fwd")
    h3 = out_proj(1, mix_b, h2)
    g_gu1, g_dn1 = _send_wait(ffn1_w, h3, "gather_ffn1_wait")
    h4, gu1, hn_f1, act1 = ffn_fwd(1, h3, g_gu1, g_dn1)
    loss_row, dh, d_final_g = _loss_head(h4, final_norm_g[None, :], tgt, "loss_head")

    zeros_mem = jnp.zeros_like(memx)
    per_dest2 = [(0, True), (1, True)]

    def ffn_bwd(i, dh, h_in, gu, hn_f, act, g_gu, g_dn, deps=()):
        dact = _matmul_nt(dh, g_dn, _spec_down(1), FF_BLOCKS, (FF_BLOCKS, s, GU_SHARD),
                          pl.BlockSpec((None, tm, GU_SHARD), lambda r, j: (j, r, 0)), f"d_act_{i}", deps=deps)
        d_wdown = _matmul_tn(act, pl.BlockSpec((None, tm, GU_SHARD), lambda j, r: (j, r, 0)),
                             dh, pl.BlockSpec((tm, D), lambda j, r: (r, 0)), s, FF_BLOCKS, (GU_SHARD, D),
                             (N_DEV, DN_SHARD, D), pl.BlockSpec((2, DN_SHARD, D), lambda j, r: (j, 0, 0)), f"d_w_down_{i}")
        dgu, d_cw, d_cb = _glu_bwd(gu, ffn_cw[i], ffn_cb[i], dact, f"glu_bwd_{i}")
        dh_new, d_g = _matmul_nt_normbwd(dgu, _spec_gu_act(0, 1, tm), g_gu, _spec_gate_up(1), N_DEV, h_in,
                                         norm_ffn_g[i:i + 1], dh, f"d_ffn_in_{i}")
        d_wgu = _matmul_tn(hn_f, pl.BlockSpec((tm, D), lambda j, r: (r, 0)), dgu, _spec_gu_act(1, 0, tm), s, N_DEV,
                           (D, GU_SHARD), (N_DEV, D, GU_SHARD), pl.BlockSpec((None, D, GU_SHARD), lambda j, r: (j, 0, 0)),
                           f"d_w_gate_up_{i}")
        sent = _send_start([d_wdown, d_wgu], [((N_DEV, DN_SHARD, D), _WIRE), ((N_DEV, D, GU_SHARD), _WIRE)], per_dest2,
                           f"send_ffn{i}_grads_start")
        return dh_new, sent, d_cw, d_cb, d_g

    def out_bwd(i, dh, mix, deps):
        dmix = _matmul_nt(dh, g_out, _spec_rowsharded(i, D // N_DEV, D), 1, (s, D), row_x, f"d_mix_{i}", deps=deps)
        d_wout = _matmul_tn(mix, pl.BlockSpec((tm, D), lambda j, r: (r, 0)), dh, pl.BlockSpec((tm, D), lambda j, r: (r, 0)),
                            s, 1, (D, D), (N_DEV, D // N_DEV, D), pl.BlockSpec((N_DEV, D // N_DEV, D), lambda j, r: (0, 0, 0)),
                            f"d_w_out_{i}")
        return dmix, d_wout

    def mem_bwd(i, dmemkv, memn):
        tmm = _rows(MEM_LEN)
        _, d_g = _matmul_nt_normbwd(dmemkv, pl.BlockSpec((tmm, 2 * X_Q), lambda r, j: (r, 0)), g_mk,
                                    _spec_rowsharded(i, D // N_DEV, 2 * X_Q), 1, memx, norm_mem_g[i:i + 1], zeros_mem,
                                    f"d_mem_in_{i}")
        by_row = lambda j, r: (r, 0)
        d_w = _matmul_tn(memn, pl.BlockSpec((tmm, D), by_row), dmemkv, pl.BlockSpec((tmm, 2 * X_Q), by_row), MEM_LEN, 1,
                         (D, 2 * X_Q), (N_DEV, D // N_DEV, 2 * X_Q),
                         pl.BlockSpec((N_DEV, D // N_DEV, 2 * X_Q), lambda j, r: (0, 0, 0)), f"d_w_mem_kv_{i}")
        return d_w, d_g

    out_land = ((N_DEV, D // N_DEV, D), _WIRE)
    mk_land = ((N_DEV, D // N_DEV, 2 * X_Q), _WIRE)
    dh, ffn1_g, d_cw1, d_cb1, d_gf1 = ffn_bwd(1, dh, h3, gu1, hn_f1, act1, g_gu1, g_dn1)
    dmix, d_wout1 = out_bwd(1, dh, mix_b, [ffn1_g.token])
    dproj_b, d_convw, d_par, d_ng, dmemkv1 = _mix_b_bwd(proj_b, conv_qkv, par_b, out_norm_g_b, memkv1, states, dmix, "mix_b_bwd")
    dh, d_gm1 = _matmul_nt_normbwd(dproj_b, pl.BlockSpec((tm, 896), lambda i, j: (i, j)), g_ib,
                                   _spec_rowsharded(0, D // N_DEV, 896, col_block=1), IN_BP // 896, h2, norm_mix_g[1:2], dh, "d_in_b")
    d_wib = _matmul_tn(hn_b, pl.BlockSpec((tm, D), lambda j, r: (r, 0)), dproj_b, pl.BlockSpec((tm, 896), lambda j, r: (r, j)),
                       s, IN_BP // 896, (D, 896), (N_DEV, D // N_DEV, IN_BP),
                       pl.BlockSpec((N_DEV, D // N_DEV, 896), lambda j, r: (0, 0, j)), "d_w_in_b")
    d_wmk1, d_gmem1 = mem_bwd(1, dmemkv1, memn1)
    mix1_g = _send_start([d_wout1, d_wib, d_wmk1], [out_land, ((N_DEV, D // N_DEV, IN_BP), _WIRE), mk_land],
                         [(0, True), (1, True), (2, True)], "send_mix1_grads_start")
    dh, ffn0_g, d_cw0, d_cb0, d_gf0 = ffn_bwd(0, dh, h1, gu0, hn_f0, act0, g_gu0, g_dn0, deps=[mix1_g.token])
    dmix, d_wout0 = out_bwd(0, dh, mix_a, [ffn0_g.token])
    dproj_a, dbias, dsinks, dmemkv0 = _mix_a_bwd(proj_a, bias, sinks, memkv0, dmix, "mix_a_bwd")
    dh, d_gm0 = _matmul_nt_normbwd(dproj_a, pl.BlockSpec((tm, 640), lambda i, j: (i, j)), w_ia,
                                   pl.BlockSpec((D, 640), lambda i, j: (0, j)), IN_A // 640, h0, norm_mix_g[0:1], dh, "d_in_a")
    d_wia = _matmul_tn(hn_a, pl.BlockSpec((tm, D), lambda j, r: (r, 0)), dproj_a, pl.BlockSpec((tm, IN_A), lambda j, r: (r, 0)),
                       s, 1, (D, IN_A), (N_DEV, D, IA_SHARD), pl.BlockSpec((N_DEV, D, IA_SHARD), lambda j, r: (0, 0, 0)),
                       "d_w_in_a", split=IA_SHARD)
    d_wmk0, d_gmem0 = mem_bwd(0, dmemkv0, memn0)
    d_rel = _bias_reduce(dbias, bucket, "bias_reduce")
    small = _pack_small(d_rel, (d_cb0, d_cb1), (d_cw0, d_cw1), d_convw, (d_gm0, d_gm1), (d_gmem0, d_gmem1),
                        (d_gf0, d_gf1), d_final_g, dsinks, d_par, d_ng, "pack_small")
    mix0_g = _send_start([d_wout0, d_wia, d_wmk0, small],
                         [out_land, ((N_DEV, D, IA_SHARD), _WIRE), mk_land, ((N_DEV, SMALL_ROWS, D_FF), F32)],
                         [(0, True), (1, True), (2, True), (3, False)], "send_mix0_grads_start")

    r_out0, r_ia, r_mk0, r_small = _send_wait(mix0_g, mix0_g.token, "send_mix0_grads_wait")
    r_dn1, r_gu1 = _send_wait(ffn1_g, mix0_g.token, "send_ffn1_grads_wait")
    r_out1, r_ib, r_mk1 = _send_wait(mix1_g, mix0_g.token, "send_mix1_grads_wait")
    r_dn0, r_gu0 = _send_wait(ffn0_g, mix0_g.token, "send_ffn0_grads_wait")

    res = {}
    for nm, parts, tr, restore in (("w_mem_kv", [r_mk0, r_mk1], 128, False), ("w_out", [r_out0, r_out1], 128, False),
                                  ("w_in_a", [r_ia], 512, False), ("w_in_b", [r_ib], 32, True),
                                  ("w_gate_up", [r_gu0, r_gu1], 128, False), ("w_down", [r_dn0, r_dn1], 176, False)):
        res[nm] = _adamw(parts, wts[nm], moms[nm], vars_[nm], tr, "adamw_" + nm, restore_b=restore)

    my = 4 * lax.axis_index("x") + 2 * lax.axis_index("y") + lax.axis_index("c")
    cq = conv_qkv_b.shape[-1]
    cf = ffn_conv_w.shape[-1]
    rc_qkv = lax.dynamic_slice_in_dim(r_small[:, SP_QKV:SP_QKV + B_CONV, :B_QKV], my * cq, cq, axis=2)[:, None]
    rc_ffn = lax.dynamic_slice_in_dim(r_small[:, SP_CW:SP_CW + 2 * FFN_CONV, :], my * cf, cf, axis=2).reshape(N_DEV, 2, FFN_CONV, cf)
    as2d = lambda a: a[None, :] if a.ndim == 1 else a
    small_out = _adamw_small(r_small, rc_qkv, rc_ffn, [as2d(wts[n]) for n in _SMALL], [as2d(moms[n]) for n in _SMALL],
                             [as2d(vars_[n]) for n in _SMALL], "adamw_small")
    ns = len(_SMALL)
    for i, nm in enumerate(_SMALL):
        res[nm] = [small_out[k * ns + i].reshape(wts[nm].shape) for k in range(4)]

    loss = lax.psum(loss_row[0, 0], AXES)
    return (loss, dh[None], *[res[n][0] for n in order], *[res[n][1] for n in order],
            *[res[n][2] for n in order], *[res[n][3] for n in order])
```

```python
import functools
import math

import numpy as np

import jax
import jax.numpy as jnp
from jax import lax
from jax.experimental import pallas as pl
from jax.experimental.pallas import tpu as pltpu

F32 = jnp.float32
_MXU = jnp.bfloat16
_ACT = jnp.bfloat16
_WIRE = jnp.bfloat16
_HI = lax.Precision.HIGH
_TM = 1024
_TM_GLU = 512
_VMEM_LIMIT = 48 * 1024 * 1024
_SDS = jax.ShapeDtypeStruct

D = 1024
EPS = 1e-6
A_HEADS, A_KV_HEADS, A_HD, BLK = 12, 2, 64, 128
N_BUCKETS, MAX_DISTANCE = 32, 128
B_QK_HEADS, B_V_HEADS, B_HD, B_CONV, CHUNK = 3, 6, 128, 4, 64
X_HEADS, X_HD, MEM_LEN = 4, 64, 256
D_FF, FFN_CONV = 2816, 3
A_Q, A_KV, X_Q = 768, 128, 256
B_QK, B_V, B_QKV = 384, 768, 1536
IN_A, IN_B = 1280, 2572
IN_BP = 2688
BP_Z, BP_XQ, BP_GATE = 1536, 2304, 2560
HALO = 8
GLU_HALO = 16

N_DEV = 8
AXES = ("x", "y", "c")
GU_SHARD = 2 * D_FF // N_DEV
FF_BLOCKS = D_FF // GU_SHARD
DN_SHARD = D_FF // N_DEV
IA_SHARD = IN_A // N_DEV

ADAM_LR, ADAM_B1, ADAM_B2, ADAM_EPS, ADAM_WD, ADAM_STEP = 0.001, 0.9, 0.999, 1e-08, 0.01, 10

SP_REL, SP_CB, SP_CW, SP_QKV, SP_MIX, SP_MEM, SP_FFN, SP_FINAL, SP_MISC, SMALL_ROWS = 0, 32, 34, 40, 44, 46, 48, 50, 51, 56


def _cp(*sems):
    return pltpu.CompilerParams(dimension_semantics=sems, vmem_limit_bytes=_VMEM_LIMIT)


def _mm(a, b):
    return jnp.dot(a.astype(_MXU), b.astype(_MXU), preferred_element_type=F32)


def _mm_nt(a, b):
    return lax.dot_general(a.astype(_MXU), b.astype(_MXU), (((1,), (1,)), ((), ())), preferred_element_type=F32)


def _mm_tn(a, b):
    return lax.dot_general(a.astype(_MXU), b.astype(_MXU), (((0,), (0,)), ((), ())), preferred_element_type=F32)


def _mmf(a, b):
    return jnp.dot(a, b, preferred_element_type=F32, precision=_HI)


def _mmf_nt(a, b):
    return lax.dot_general(a, b, (((1,), (1,)), ((), ())), preferred_element_type=F32, precision=_HI)


def _mmf_tn(a, b):
    return lax.dot_general(a, b, (((0,), (0,)), ((), ())), preferred_element_type=F32, precision=_HI)


def _silu(x):
    return x * jax.nn.sigmoid(x)


def _w2d(ref):
    v = ref[...]
    return v.reshape(-1, v.shape[-1])


def _rows(m):
    return min(m, _TM)


def _spec_rowsharded(layer, rows, cols, col_block=None):
    if col_block is None:
        return pl.BlockSpec((N_DEV, None, rows, cols), lambda *_: (0, layer, 0, 0))
    return pl.BlockSpec((N_DEV, None, rows, cols), lambda *ids: (0, layer, 0, ids[col_block]))


def _spec_gate_up(axis):
    return pl.BlockSpec((None, D, GU_SHARD), lambda *ids: (ids[axis], 0, 0))


def _spec_down(axis):
    return pl.BlockSpec((2, DN_SHARD, D), lambda *ids: (ids[axis], 0, 0))


def _dep_specs(deps):
    return [pl.BlockSpec(d.shape, lambda *_: (0,) * d.ndim) for d in deps]


def _spec_gu_act(row_axis, axis, tm):
    return pl.BlockSpec((None, None, tm, GU_SHARD), lambda *ids: (ids[axis] // FF_BLOCKS, ids[axis] % FF_BLOCKS, ids[row_axis], 0))


def _norm_matmul(x, g, w, w_spec, n_blocks, out_shape, out_spec, name, deps=(), out_dtype=F32):
    m, k = x.shape
    tm = _rows(m)

    def body(x_ref, g_ref, w_ref, *rest):
        y_ref, hn_ref = rest[-2:]

        @pl.when(pl.program_id(1) == 0)
        def _():
            xv = x_ref[...]
            r = lax.rsqrt(jnp.mean(xv * xv, axis=-1, keepdims=True) + EPS)
            hn_ref[...] = (xv * r * g_ref[...]).astype(hn_ref.dtype)

        y_ref[...] = _mm(hn_ref[...], _w2d(w_ref)).astype(y_ref.dtype)

    return pl.pallas_call(
        body, grid=(m // tm, n_blocks),
        in_specs=[pl.BlockSpec((tm, k), lambda i, j: (i, 0)), pl.BlockSpec((1, k), lambda i, j: (0, 0)), w_spec]
        + _dep_specs(deps),
        out_specs=[out_spec, pl.BlockSpec((tm, k), lambda i, j: (i, 0))],
        out_shape=[_SDS(out_shape, out_dtype), _SDS((m, k), _ACT)],
        name=name, compiler_params=_cp("arbitrary", "arbitrary"))(x, g, w, *deps)


def _matmul_res(a, a_spec, w, w_spec, n_k, res, name):
    m, n = res.shape
    tm = _rows(m)

    def body(a_ref, w_ref, r_ref, o_ref):
        part = _mm(a_ref[...], _w2d(w_ref))

        @pl.when(pl.program_id(1) == 0)
        def _():
            o_ref[...] = r_ref[...] + part

        @pl.when(pl.program_id(1) > 0)
        def _():
            o_ref[...] += part

    return pl.pallas_call(
        body, grid=(m // tm, n_k),
        in_specs=[a_spec, w_spec, pl.BlockSpec((tm, n), lambda i, j: (i, 0))],
        out_specs=pl.BlockSpec((tm, n), lambda i, j: (i, 0)),
        out_shape=_SDS((m, n), F32), name=name, compiler_params=_cp("arbitrary", "arbitrary"))(a, w, res)


def _matmul_nt(dy, w, w_spec, n_blocks, out_shape, out_spec, name, deps=(), out_dtype=F32):
    m, n = dy.shape
    tm = _rows(m)

    def body(dy_ref, w_ref, *rest):
        o_ref = rest[-1]
        o_ref[...] = _mm_nt(dy_ref[...], _w2d(w_ref)).astype(o_ref.dtype)

    return pl.pallas_call(
        body, grid=(m // tm, n_blocks),
        in_specs=[pl.BlockSpec((tm, n), lambda i, j: (i, 0)), w_spec] + _dep_specs(deps),
        out_specs=out_spec, out_shape=_SDS(out_shape, out_dtype),
        name=name, compiler_params=_cp("arbitrary", "arbitrary"))(dy, w, *deps)


def _matmul_nt_normbwd(dy, dy_spec, w, w_spec, nj, h, g, dh_in, name):
    m, k = h.shape
    tm = _rows(m)

    def body(dy_ref, w_ref, h_ref, g_ref, dhin_ref, dh_ref, dg_ref, acc_ref):
        i, j = pl.program_id(0), pl.program_id(1)

        @pl.when(j == 0)
        def _():
            acc_ref[...] = jnp.zeros_like(acc_ref)

        acc_ref[...] += _mm_nt(dy_ref[...], _w2d(w_ref))

        @pl.when(j == nj - 1)
        def _():
            xv = h_ref[...]
            r = lax.rsqrt(jnp.mean(xv * xv, axis=-1, keepdims=True) + EPS)
            xh = xv * r
            dhn = acc_ref[...]
            part = jnp.sum(dhn * xh, axis=0, keepdims=True)

            @pl.when(i == 0)
            def _():
                dg_ref[...] = part

            @pl.when(i > 0)
            def _():
                dg_ref[...] += part

            t = dhn * g_ref[...]
            dh_ref[...] = dhin_ref[...] + r * (t - xh * jnp.mean(t * xh, axis=-1, keepdims=True))

    return pl.pallas_call(
        body, grid=(m // tm, nj),
        in_specs=[dy_spec, w_spec, pl.BlockSpec((tm, k), lambda i, j: (i, 0)), pl.BlockSpec((1, k), lambda i, j: (0, 0)),
                  pl.BlockSpec((tm, k), lambda i, j: (i, 0))],
        out_specs=[pl.BlockSpec((tm, k), lambda i, j: (i, 0)), pl.BlockSpec((1, k), lambda i, j: (0, 0))],
        out_shape=[_SDS((m, k), F32), _SDS((1, k), F32)],
        scratch_shapes=[pltpu.VMEM((tm, k), F32)],
        name=name, compiler_params=_cp("arbitrary", "arbitrary"))(dy, w, h, g, dh_in)


def _matmul_tn(x, x_spec, dy, dy_spec, m, n_blocks, acc_shape, out_shape, out_spec, name, split=None):
    tm = _rows(m)
    nm = m // tm

    def body(x_ref, dy_ref, o_ref, acc_ref):
        @pl.when(pl.program_id(1) == 0)
        def _():
            acc_ref[...] = jnp.zeros_like(acc_ref)

        acc_ref[...] += _mm_tn(x_ref[...], dy_ref[...])

        @pl.when(pl.program_id(1) == nm - 1)
        def _():
            if split is None:
                o_ref[...] = acc_ref[...].reshape(o_ref.shape).astype(o_ref.dtype)
            else:
                for d in range(N_DEV):
                    o_ref[d] = acc_ref[:, d * split:(d + 1) * split].astype(o_ref.dtype)

    return pl.pallas_call(
        body, grid=(n_blocks, nm), in_specs=[x_spec, dy_spec], out_specs=out_spec,
        out_shape=_SDS(out_shape, _WIRE), scratch_shapes=[pltpu.VMEM(acc_shape, F32)],
        name=name, compiler_params=_cp("arbitrary", "arbitrary"))(x, dy)


def _loss_head(h, g, tgt, name):
    m, k = h.shape
    tm = _rows(m)

    def body(h_ref, g_ref, t_ref, loss_ref, dh_ref, dg_ref):
        i = pl.program_id(0)
        xv = h_ref[...]
        r = lax.rsqrt(jnp.mean(xv * xv, axis=-1, keepdims=True) + EPS)
        xh = xv * r
        gv = g_ref[...]
        err = xh * gv - t_ref[...]
        lpart = jnp.zeros((1, 128), F32) + 0.5 * jnp.sum(jnp.mean(err * err, axis=-1, keepdims=True), axis=0, keepdims=True)
        dy = err * (1.0 / k)
        gpart = jnp.sum(dy * xh, axis=0, keepdims=True)

        @pl.when(i == 0)
        def _():
            loss_ref[...] = lpart
            dg_ref[...] = gpart

        @pl.when(i > 0)
        def _():
            loss_ref[...] += lpart
            dg_ref[...] += gpart

        t = dy * gv
        dh_ref[...] = r * (t - xh * jnp.mean(t * xh, axis=-1, keepdims=True))

    return pl.pallas_call(
        body, grid=(m // tm,),
        in_specs=[pl.BlockSpec((tm, k), lambda i: (i, 0)), pl.BlockSpec((1, k), lambda i: (0, 0)),
                  pl.BlockSpec((tm, k), lambda i: (i, 0))],
        out_specs=[pl.BlockSpec((1, 128), lambda i: (0, 0)), pl.BlockSpec((tm, k), lambda i: (i, 0)),
                   pl.BlockSpec((1, k), lambda i: (0, 0))],
        out_shape=[_SDS((1, 128), F32), _SDS((m, k), F32), _SDS((1, k), F32)],
        name=name, compiler_params=_cp("arbitrary"))(h, g, tgt)


def _glu_fwd(gu, conv_w, conv_b, name):
    s = gu.shape[2]
    tm = min(s, _TM_GLU)

    def body(gu_ref, prev_ref, w_ref, b_ref, act_ref):
        i = pl.program_id(0)
        prev = jnp.where(i > 0, prev_ref[...].astype(F32), 0.0)
        ext = jnp.concatenate([prev, gu_ref[0].astype(F32)], axis=0)
        gc = b_ref[...] + w_ref[FFN_CONV - 1:FFN_CONV, :] * ext
        for j in range(FFN_CONV - 1):
            gc = gc + w_ref[j:j + 1, :] * pltpu.roll(ext, FFN_CONV - 1 - j, 0)
        act_ref[...] = (_silu(gc[GLU_HALO:]) * gu_ref[1].astype(F32)).astype(act_ref.dtype)

    return pl.pallas_call(
        body, grid=(s // tm, FF_BLOCKS),
        in_specs=[pl.BlockSpec((2, None, tm, GU_SHARD), lambda i, j: (0, j, i, 0)),
                  pl.BlockSpec((None, None, GLU_HALO, GU_SHARD),
                               lambda i, j: (0, j, jnp.maximum(i * (tm // GLU_HALO) - 1, 0), 0)),
                  pl.BlockSpec((None, HALO, GU_SHARD), lambda i, j: (j, 0, 0)),
                  pl.BlockSpec((None, 1, GU_SHARD), lambda i, j: (j, 0, 0))],
        out_specs=pl.BlockSpec((None, tm, GU_SHARD), lambda i, j: (j, i, 0)),
        out_shape=_SDS((FF_BLOCKS, s, GU_SHARD), _ACT), name=name,
        compiler_params=_cp("arbitrary", "arbitrary"))(gu, gu, conv_w, conv_b)


def _glu_bwd(gu, conv_w, conv_b, dact, name):
    s = gu.shape[2]
    tm = min(s, _TM_GLU)
    nt = s // tm
    ext_rows = tm + GLU_HALO

    def body(gu_ref, prev_ref, w_ref, b_ref, dact_ref, dgu_ref, dw_ref, db_ref, carry_ref):
        t = pl.program_id(1)
        i = nt - 1 - t

        @pl.when(t == 0)
        def _():
            carry_ref[...] = jnp.zeros_like(carry_ref)
            dw_ref[...] = jnp.zeros_like(dw_ref)
            db_ref[...] = jnp.zeros_like(db_ref)

        up = gu_ref[1].astype(F32)
        prev = jnp.where(i > 0, prev_ref[...].astype(F32), 0.0)
        ext = jnp.concatenate([prev, gu_ref[0].astype(F32)], axis=0)
        shifted = [pltpu.roll(ext, FFN_CONV - 1 - j, 0) if j < FFN_CONV - 1 else ext for j in range(FFN_CONV)]
        gc = b_ref[...] + shifted[0] * w_ref[0:1, :]
        for j in range(1, FFN_CONV):
            gc = gc + shifted[j] * w_ref[j:j + 1, :]
        gc = gc[GLU_HALO:]
        sg = jax.nn.sigmoid(gc)
        da = dact_ref[...].astype(F32)
        dup = da * (gc * sg)
        dgc = da * up * (sg * (1.0 + gc * (1.0 - sg)))
        db_ref[...] += jnp.sum(dgc, axis=0, keepdims=True)
        dgc_ext = jnp.concatenate([jnp.zeros((GLU_HALO, GU_SHARD), F32), dgc], axis=0)
        dext = dgc_ext * w_ref[FFN_CONV - 1:FFN_CONV, :]
        for j in range(FFN_CONV):
            dw_ref[j:j + 1, :] += jnp.sum(shifted[j] * dgc_ext, axis=0, keepdims=True)
            if j < FFN_CONV - 1:
                dext = dext + w_ref[j:j + 1, :] * pltpu.roll(dgc_ext, ext_rows - (FFN_CONV - 1 - j), 0)
        tail = jnp.concatenate([jnp.zeros((tm - GLU_HALO, GU_SHARD), F32), carry_ref[...]], axis=0)
        dgate = dext[GLU_HALO:] + tail
        carry_ref[...] = dext[:GLU_HALO]
        dgu_ref[0] = dgate.astype(dgu_ref.dtype)
        dgu_ref[1] = dup.astype(dgu_ref.dtype)

    return pl.pallas_call(
        body, grid=(FF_BLOCKS, nt),
        in_specs=[pl.BlockSpec((2, None, tm, GU_SHARD), lambda j, t: (0, j, nt - 1 - t, 0)),
                  pl.BlockSpec((None, None, GLU_HALO, GU_SHARD),
                               lambda j, t: (0, j, jnp.maximum((nt - 1 - t) * (tm // GLU_HALO) - 1, 0), 0)),
                  pl.BlockSpec((None, HALO, GU_SHARD), lambda j, t: (j, 0, 0)),
                  pl.BlockSpec((None, 1, GU_SHARD), lambda j, t: (j, 0, 0)),
                  pl.BlockSpec((None, tm, GU_SHARD), lambda j, t: (j, nt - 1 - t, 0))],
        out_specs=[pl.BlockSpec((2, None, tm, GU_SHARD), lambda j, t: (0, j, nt - 1 - t, 0)),
                   pl.BlockSpec((None, HALO, GU_SHARD), lambda j, t: (j, 0, 0)),
                   pl.BlockSpec((None, 1, GU_SHARD), lambda j, t: (j, 0, 0))],
        out_shape=[_SDS(gu.shape, _ACT), _SDS((FF_BLOCKS, HALO, GU_SHARD), F32), _SDS((FF_BLOCKS, 1, GU_SHARD), F32)],
        scratch_shapes=[pltpu.VMEM((GLU_HALO, GU_SHARD), F32)],
        name=name, compiler_params=_cp("arbitrary", "arbitrary"))(gu, gu, conv_w, conv_b, dact)


def _bucket_table():
    qi = np.arange(BLK)[:, None]
    kj = np.arange(BLK)[None, :]
    n = np.where(kj > qi, BLK + qi - kj, qi - kj)
    max_exact = N_BUCKETS // 2
    nf = np.maximum(n, 1).astype(np.float32)
    large = max_exact + (np.log(nf / max_exact) / math.log(MAX_DISTANCE / max_exact)
                         * (N_BUCKETS - max_exact)).astype(np.int32)
    large = np.minimum(large, N_BUCKETS - 1)
    return np.where(n < max_exact, n, large).astype(np.int32)


def _lane_low():
    return lax.broadcasted_iota(jnp.int32, (1, 128), 1) < A_HD


def _swa_group(q, kd, vd, sink, bias, upper, first):
    n = A_HEADS // A_KV_HEADS
    low = _lane_low()
    pairs = [q[:, p * 128:(p + 1) * 128] for p in range(n // 2)]
    qm = jnp.concatenate([jnp.where(low == (h % 2 == 0), pairs[h // 2], 0.0) for h in range(n)], axis=0)
    s2 = _mm_nt(qm, kd) * (A_HD ** -0.5)
    s = jnp.where(upper[None], s2[:, :BLK].reshape(n, BLK, BLK), s2[:, BLK:].reshape(n, BLK, BLK)) + bias
    s = jnp.where((upper & first)[None], -jnp.inf, s)
    m = jnp.maximum(jnp.max(s, axis=-1, keepdims=True), sink)
    p = jnp.exp(s - m)
    split = jnp.concatenate([jnp.where(upper[None], p, 0.0), jnp.where(upper[None], 0.0, p)], axis=-1)
    split = split.reshape(n * BLK, 2 * BLK)
    den = _mm(p.reshape(n * BLK, BLK), jnp.ones((BLK, 128), F32)) + jnp.exp(sink - m).reshape(n * BLK, 1)
    o = _mm(split, vd) / den
    return jnp.concatenate([jnp.where(low, o[2 * p * BLK:(2 * p + 1) * BLK], o[(2 * p + 1) * BLK:(2 * p + 2) * BLK])
                            for p in range(n // 2)], axis=1)


def _swa_sinks(sink_ref, g):
    n = A_HEADS // A_KV_HEADS
    return jnp.concatenate([sink_ref[:, h:h + 1] for h in range(g * n, (g + 1) * n)], axis=0).reshape(n, 1, 1)


def _both_halves(t, t_rolled, g):
    low = _lane_low()
    return jnp.where(low, t, t_rolled) if g == 0 else jnp.where(low, t_rolled, t)


def _cross_pairs(q, mk, mv):
    rows = q.shape[0]
    low = _lane_low()
    qm = [jnp.concatenate([jnp.where(low, q[:, p * 128:(p + 1) * 128], 0.0), jnp.where(low, 0.0, q[:, p * 128:(p + 1) * 128])], axis=0)
          for p in range(X_HEADS // 2)]
    s = [_mm_nt(qm[p], mk[:, p * 128:(p + 1) * 128]) * (X_HD ** -0.5) for p in range(X_HEADS // 2)]
    e = [jnp.exp(t - jnp.max(t, axis=-1, keepdims=True)) for t in s]
    pr = [t / jnp.sum(t, axis=-1, keepdims=True) for t in e]
    o = [_mm(pr[p], mv[:, p * 128:(p + 1) * 128]) for p in range(X_HEADS // 2)]
    return jnp.concatenate([jnp.where(low, t[:rows], t[rows:]) for t in o], axis=1)


def _swa_upper():
    qi = lax.broadcasted_iota(jnp.int32, (BLK, BLK), 0)
    kj = lax.broadcasted_iota(jnp.int32, (BLK, BLK), 1)
    return kj > qi


def _bias_build(rel_bias, bucket, name):
    def body(rb_ref, bucket_ref, o_ref):
        b = bucket_ref[...]
        for h in range(A_HEADS):
            acc = jnp.zeros((BLK, BLK), F32)
            for k in range(N_BUCKETS):
                acc = jnp.where(b == k, rb_ref[k, h], acc)
            o_ref[h] = acc

    return pl.pallas_call(
        body, in_specs=[pl.BlockSpec(memory_space=pltpu.SMEM), pl.BlockSpec(memory_space=pltpu.VMEM)],
        out_specs=pl.BlockSpec(memory_space=pltpu.VMEM),
        out_shape=_SDS((A_HEADS, BLK, BLK), F32), name=name)(rel_bias, bucket)


def _bias_reduce(dbias, bucket, name):
    def body(db_ref, bucket_ref, o_ref):
        b = bucket_ref[...]
        row = lax.broadcasted_iota(jnp.int32, (N_BUCKETS, 128), 0)
        lane = lax.broadcasted_iota(jnp.int32, (N_BUCKETS, 128), 1)
        acc = jnp.zeros((N_BUCKETS, 128), F32)
        for h in range(A_HEADS):
            v = db_ref[h]
            for k in range(N_BUCKETS):
                sk = jnp.sum(jnp.sum(jnp.where(b == k, v, 0.0), axis=1, keepdims=True), axis=0, keepdims=True)
                acc = acc + jnp.where((row == k) & (lane == h), sk, 0.0)
        o_ref[...] = acc

    return pl.pallas_call(
        body, in_specs=[pl.BlockSpec(memory_space=pltpu.VMEM)] * 2,
        out_specs=pl.BlockSpec(memory_space=pltpu.VMEM),
        out_shape=_SDS((N_BUCKETS, 128), F32), name=name)(dbias, bucket)


def _mix_a_fwd(proj, bias, sinks, memkv, name):
    s = proj.shape[0]
    nb = s // BLK
    grp = A_HEADS // A_KV_HEADS

    def body(proj_ref, prev_ref, bias_ref, sink_ref, memkv_ref, o_ref):
        i = pl.program_id(0)
        upper = _swa_upper()
        prev = prev_ref[...].astype(F32)
        proj = proj_ref[...].astype(F32)
        kb = jnp.concatenate([prev[:, :A_KV], proj[:, A_Q:A_Q + A_KV]], axis=0)
        vb = jnp.concatenate([prev[:, A_KV:], proj[:, A_Q + A_KV:A_Q + 2 * A_KV]], axis=0)
        kb_r = pltpu.roll(kb, A_HD, 1)
        vb_r = pltpu.roll(vb, A_HD, 1)
        gw = A_Q // A_KV_HEADS
        outs = [_swa_group(proj[:, g * gw:(g + 1) * gw], _both_halves(kb, kb_r, g), _both_halves(vb, vb_r, g),
                           _swa_sinks(sink_ref, g), bias_ref[g * grp:(g + 1) * grp], upper, i == 0) for g in range(A_KV_HEADS)]
        outs.append(_cross_pairs(proj[:, A_Q + 2 * A_KV:], memkv_ref[:, :X_Q], memkv_ref[:, X_Q:]))
        o_ref[...] = jnp.concatenate(outs, axis=1).astype(o_ref.dtype)

    return pl.pallas_call(
        body, grid=(nb,),
        in_specs=[pl.BlockSpec((BLK, IN_A), lambda i: (i, 0)),
                  pl.BlockSpec((BLK, 2 * A_KV), lambda i: (jnp.maximum(i - 1, 0), A_Q // (2 * A_KV))),
                  pl.BlockSpec((A_HEADS, BLK, BLK), lambda i: (0, 0, 0)),
                  pl.BlockSpec((1, 128), lambda i: (0, 0)),
                  pl.BlockSpec((MEM_LEN, 2 * X_Q), lambda i: (0, 0))],
        out_specs=pl.BlockSpec((BLK, D), lambda i: (i, 0)),
        out_shape=_SDS((s, D), _ACT), name=name, compiler_params=_cp("arbitrary"))(proj, proj, bias, sinks, memkv)


def _mix_a_bwd(proj, bias, sinks, memkv, dmix, name):
    s = proj.shape[0]
    nb = s // BLK
    grp = A_HEADS // A_KV_HEADS

    def body(proj_ref, prev_ref, bias_ref, sink_ref, memkv_ref, dmix_ref,
             dproj_ref, dbias_ref, dsink_ref, dmemkv_ref, carry_ref):
        t = pl.program_id(0)
        i = nb - 1 - t

        @pl.when(t == 0)
        def _():
            carry_ref[...] = jnp.zeros_like(carry_ref)
            dbias_ref[...] = jnp.zeros_like(dbias_ref)
            dsink_ref[...] = jnp.zeros_like(dsink_ref)
            dmemkv_ref[...] = jnp.zeros_like(dmemkv_ref)

        upper = _swa_upper()
        lane = lax.broadcasted_iota(jnp.int32, (1, 128), 1)
        low = _lane_low()
        prev = prev_ref[...].astype(F32)
        proj = proj_ref[...].astype(F32)
        kb = jnp.concatenate([prev[:, :A_KV], proj[:, A_Q:A_Q + A_KV]], axis=0)
        vb = jnp.concatenate([prev[:, A_KV:], proj[:, A_Q + A_KV:A_Q + 2 * A_KV]], axis=0)
        kb_r = pltpu.roll(kb, A_HD, 1)
        vb_r = pltpu.roll(vb, A_HD, 1)
        gw = A_Q // A_KV_HEADS
        dqs, dkd, dvd = [], [], []
        dsink = jnp.zeros((1, 128), F32)
        for g in range(A_KV_HEADS):
            _, vjp = jax.vjp(functools.partial(_swa_group, upper=upper, first=i == 0), proj[:, g * gw:(g + 1) * gw],
                             _both_halves(kb, kb_r, g), _both_halves(vb, vb_r, g), _swa_sinks(sink_ref, g),
                             bias_ref[g * grp:(g + 1) * grp])
            dq, dk, dv, ds, db = vjp(dmix_ref[:, g * gw:(g + 1) * gw].astype(F32))
            dqs.append(dq)
            dkd.append(dk + pltpu.roll(dk, A_HD, 1))
            dvd.append(dv + pltpu.roll(dv, A_HD, 1))
            for h in range(grp):
                dsink = dsink + jnp.where(lane == g * grp + h, ds[h], 0.0)
            dbias_ref[g * grp:(g + 1) * grp] += db
        dsink_ref[...] += dsink
        dkb = jnp.where(low, dkd[0], dkd[1])
        dvb = jnp.where(low, dvd[0], dvd[1])
        _, vjp = jax.vjp(_cross_pairs, proj[:, A_Q + 2 * A_KV:], memkv_ref[:, :X_Q], memkv_ref[:, X_Q:])
        dxq, dmk, dmv = vjp(dmix_ref[:, A_Q:].astype(F32))
        dmemkv_ref[...] += jnp.concatenate([dmk, dmv], axis=1)
        dkv_cur = jnp.concatenate([dkb[BLK:], dvb[BLK:]], axis=1) + carry_ref[...]
        carry_ref[...] = jnp.concatenate([dkb[:BLK], dvb[:BLK]], axis=1)
        dproj_ref[...] = jnp.concatenate(dqs + [dkv_cur, dxq], axis=1).astype(dproj_ref.dtype)

    return pl.pallas_call(
        body, grid=(nb,),
        in_specs=[pl.BlockSpec((BLK, IN_A), lambda t: (nb - 1 - t, 0)),
                  pl.BlockSpec((BLK, 2 * A_KV), lambda t: (jnp.maximum(nb - 2 - t, 0), A_Q // (2 * A_KV))),
                  pl.BlockSpec((A_HEADS, BLK, BLK), lambda t: (0, 0, 0)),
                  pl.BlockSpec((1, 128), lambda t: (0, 0)),
                  pl.BlockSpec((MEM_LEN, 2 * X_Q), lambda t: (0, 0)),
                  pl.BlockSpec((BLK, D), lambda t: (nb - 1 - t, 0))],
        out_specs=[pl.BlockSpec((BLK, IN_A), lambda t: (nb - 1 - t, 0)),
                   pl.BlockSpec((A_HEADS, BLK, BLK), lambda t: (0, 0, 0)),
                   pl.BlockSpec((1, 128), lambda t: (0, 0)),
                   pl.BlockSpec((MEM_LEN, 2 * X_Q), lambda t: (0, 0))],
        out_shape=[_SDS((s, IN_A), _ACT), _SDS((A_HEADS, BLK, BLK), F32), _SDS((1, 128), F32),
                   _SDS((MEM_LEN, 2 * X_Q), F32)],
        scratch_shapes=[pltpu.VMEM((BLK, 2 * A_KV), F32)],
        name=name, compiler_params=_cp("arbitrary"))(proj, proj, bias, sinks, memkv, dmix)


def _dn_heads(yq, yk, yv, z, bl, al, a_log, dtb, ng, s0):
    c = CHUNK
    nh = B_V_HEADS
    rep = B_V_HEADS // B_QK_HEADS
    r = lax.broadcasted_iota(jnp.int32, (c, c), 0)
    cc = lax.broadcasted_iota(jnp.int32, (c, c), 1)
    q = [_silu(t) for t in yq]
    k = [_silu(t) for t in yk]
    v = [_silu(t) for t in yv]
    q = [t * lax.rsqrt(jnp.sum(t * t, axis=-1, keepdims=True) + EPS) * (B_HD ** -0.5) for t in q]
    k = [t * lax.rsqrt(jnp.sum(t * t, axis=-1, keepdims=True) + EPS) for t in k]
    beta = [jax.nn.sigmoid(t) for t in bl]
    g = [-jnp.exp(a_log[h]) * jax.nn.softplus(al[h] + dtb[h]) for h in range(nh)]
    gb = [jnp.broadcast_to(t, (c, c)) for t in g]
    gc_col = [jnp.sum(jnp.where(cc <= r, t.T, 0.0), axis=1, keepdims=True) for t in gb]
    gc_row = [jnp.sum(jnp.where(r <= cc, t, 0.0), axis=0, keepdims=True) for t in gb]
    gc_last = [jnp.sum(t, axis=0, keepdims=True) for t in g]
    decay = [jnp.exp(jnp.where(r >= cc, gc_col[h] - gc_row[h], -jnp.inf)) for h in range(nh)]
    kq = [_mmf_nt(jnp.concatenate([k[h], q[h]], axis=0), k[h]) for h in range(B_QK_HEADS)]
    kk = [t[:c] for t in kq]
    qk = [t[c:] for t in kq]
    egc = [jnp.exp(t) for t in gc_col]
    both = [_mmf(jnp.concatenate([(beta[h] * egc[h]) * k[h // rep], q[h // rep] * egc[h]], axis=0), s0[h]) for h in range(nh)]
    rhs = [beta[h] * v[h] - both[h][:c] for h in range(nh)]
    qs0 = [t[c:] for t in both]
    pw = [-(beta[h] * kk[h // rep] * jnp.where(r > cc, decay[h], 0.0)) for h in range(nh)]
    x = rhs
    for lvl in range(6):
        if lvl < 5:
            prod = [_mmf(pw[h], jnp.concatenate([x[h], pw[h]], axis=1)) for h in range(nh)]
            x = [x[h] + prod[h][:, :B_HD] for h in range(nh)]
            pw = [t[:, B_HD:] for t in prod]
        else:
            x = [x[h] + _mmf(pw[h], x[h]) for h in range(nh)]
    delta = x
    last = [_mmf(jnp.concatenate([qk[h // rep] * decay[h], (k[h // rep] * jnp.exp(gc_last[h] - gc_col[h])).T], axis=0), delta[h])
            for h in range(nh)]
    out = [qs0[h] + last[h][:c] for h in range(nh)]
    s1 = [jnp.exp(gc_last[h]) * s0[h] + last[h][c:] for h in range(nh)]
    o = [t * lax.rsqrt(jnp.mean(t * t, axis=-1, keepdims=True) + EPS) * ng for t in out]
    return [o[h] * _silu(z[h]) for h in range(nh)], s1


def _dn_conv(ext, w_ref):
    y = ext * w_ref[B_CONV - 1:B_CONV, :]
    for j in range(B_CONV - 1):
        y = y + w_ref[j:j + 1, :] * pltpu.roll(ext, B_CONV - 1 - j, 0)
    return y


def _dn_args(y, cur_ref, par_ref, ng_ref):
    nh = B_V_HEADS
    return ([y[:, h * B_HD:(h + 1) * B_HD] for h in range(B_QK_HEADS)],
            [y[:, B_QK + h * B_HD:B_QK + (h + 1) * B_HD] for h in range(B_QK_HEADS)],
            [y[:, 2 * B_QK + h * B_HD:2 * B_QK + (h + 1) * B_HD] for h in range(nh)],
            [cur_ref[:, BP_Z + h * B_HD:BP_Z + (h + 1) * B_HD] for h in range(nh)],
            [cur_ref[:, BP_GATE + h:BP_GATE + h + 1] for h in range(nh)],
            [cur_ref[:, BP_GATE + nh + h:BP_GATE + nh + h + 1] for h in range(nh)],
            [par_ref[:, h:h + 1] for h in range(nh)], [par_ref[:, nh + h:nh + h + 1] for h in range(nh)], ng_ref[...])


def _mix_b_fwd(proj, conv_w, par, ng, memkv, name):
    s = proj.shape[0]
    nc = s // CHUNK

    def body(cur_ref, prev_ref, w_ref, par_ref, ng_ref, memkv_ref, o_ref, st_ref, state_ref):
        n = pl.program_id(0)

        @pl.when(n == 0)
        def _():
            state_ref[...] = jnp.zeros_like(state_ref)

        prev = jnp.where(n > 0, prev_ref[...], 0.0)
        ext = jnp.concatenate([prev, cur_ref[:, :B_QKV]], axis=0)
        y = _dn_conv(ext, w_ref)[HALO:]
        s0 = [state_ref[hv] for hv in range(B_V_HEADS)]
        st_ref[0] = state_ref[...]
        outs, s1 = _dn_heads(*_dn_args(y, cur_ref, par_ref, ng_ref), s0)
        for hv in range(B_V_HEADS):
            state_ref[hv] = s1[hv]
        outs = outs + [_cross_pairs(cur_ref[:, BP_XQ:BP_XQ + X_Q], memkv_ref[:, :X_Q], memkv_ref[:, X_Q:])]
        o_ref[...] = jnp.concatenate(outs, axis=1).astype(o_ref.dtype)

    return pl.pallas_call(
        body, grid=(nc,),
        in_specs=[pl.BlockSpec((CHUNK, IN_BP), lambda n: (n, 0)),
                  pl.BlockSpec((HALO, B_QKV), lambda n: (jnp.maximum(n * (CHUNK // HALO) - 1, 0), 0)),
                  pl.BlockSpec((HALO, B_QKV), lambda n: (0, 0)),
                  pl.BlockSpec((1, 128), lambda n: (0, 0)), pl.BlockSpec((1, 128), lambda n: (0, 0)),
                  pl.BlockSpec((MEM_LEN, 2 * X_Q), lambda n: (0, 0))],
        out_specs=[pl.BlockSpec((CHUNK, D), lambda n: (n, 0)),
                   pl.BlockSpec((1, B_V_HEADS, B_HD, B_HD), lambda n: (n, 0, 0, 0))],
        out_shape=[_SDS((s, D), _ACT), _SDS((nc, B_V_HEADS, B_HD, B_HD), F32)],
        scratch_shapes=[pltpu.VMEM((B_V_HEADS, B_HD, B_HD), F32)],
        name=name, compiler_params=_cp("arbitrary"))(proj, proj, conv_w, par, ng, memkv)


def _mix_b_bwd(proj, conv_w, par, ng, memkv, states, dmix, name):
    s = proj.shape[0]
    nc = s // CHUNK
    ext_rows = CHUNK + HALO

    def body(cur_ref, prev_ref, w_ref, par_ref, ng_ref, memkv_ref, st_ref, dmix_ref,
             dproj_ref, dw_ref, dpar_ref, dng_ref, dmemkv_ref, dstate_ref, carry_ref):
        t = pl.program_id(0)
        n = nc - 1 - t

        @pl.when(t == 0)
        def _():
            dstate_ref[...] = jnp.zeros_like(dstate_ref)
            carry_ref[...] = jnp.zeros_like(carry_ref)
            dw_ref[...] = jnp.zeros_like(dw_ref)
            dpar_ref[...] = jnp.zeros_like(dpar_ref)
            dng_ref[...] = jnp.zeros_like(dng_ref)
            dmemkv_ref[...] = jnp.zeros_like(dmemkv_ref)

        lane = lax.broadcasted_iota(jnp.int32, (1, 128), 1)
        prev = jnp.where(n > 0, prev_ref[...], 0.0)
        ext = jnp.concatenate([prev, cur_ref[:, :B_QKV]], axis=0)
        y = _dn_conv(ext, w_ref)[HALO:]
        _, vjp = jax.vjp(_dn_heads, *_dn_args(y, cur_ref, par_ref, ng_ref), [st_ref[0, hv] for hv in range(B_V_HEADS)])
        dyq, dyk, dyv, dz, gbl, gal, ga_log, gdtb, dng, gs0 = vjp(
            ([dmix_ref[:, hv * B_HD:(hv + 1) * B_HD].astype(F32) for hv in range(B_V_HEADS)],
             [dstate_ref[hv] for hv in range(B_V_HEADS)]))
        dgate = jnp.zeros((CHUNK, 128), F32)
        dpar = jnp.zeros((1, 128), F32)
        for hv in range(B_V_HEADS):
            dstate_ref[hv] = gs0[hv]
            dgate = dgate + jnp.where(lane == hv, gbl[hv], 0.0) + jnp.where(lane == B_V_HEADS + hv, gal[hv], 0.0)
            dpar = dpar + jnp.where(lane == hv, ga_log[hv], 0.0) + jnp.where(lane == B_V_HEADS + hv, gdtb[hv], 0.0)
        dpar_ref[...] += dpar
        dng_ref[...] += dng
        _, vjp = jax.vjp(_cross_pairs, cur_ref[:, BP_XQ:BP_XQ + X_Q], memkv_ref[:, :X_Q], memkv_ref[:, X_Q:])
        dxq, dmk, dmv = vjp(dmix_ref[:, B_V:].astype(F32))
        dmemkv_ref[...] += jnp.concatenate([dmk, dmv], axis=1)
        dy = jnp.concatenate(list(dyq) + list(dyk) + list(dyv), axis=1)
        dy_ext = jnp.concatenate([jnp.zeros((HALO, B_QKV), F32), dy], axis=0)
        dext = dy_ext * w_ref[B_CONV - 1:B_CONV, :]
        dw_ref[B_CONV - 1:B_CONV, :] += jnp.sum(ext * dy_ext, axis=0, keepdims=True)
        for j in range(B_CONV - 1):
            sh = B_CONV - 1 - j
            dw_ref[j:j + 1, :] += jnp.sum(pltpu.roll(ext, sh, 0) * dy_ext, axis=0, keepdims=True)
            dext = dext + w_ref[j:j + 1, :] * pltpu.roll(dy_ext, ext_rows - sh, 0)
        tail = jnp.concatenate([jnp.zeros((CHUNK - HALO, B_QKV), F32), carry_ref[...]], axis=0)
        dqkv = dext[HALO:] + tail
        carry_ref[...] = dext[:HALO]
        dproj_ref[...] = jnp.concatenate([dqkv] + list(dz) + [dxq, dgate], axis=1).astype(dproj_ref.dtype)

    return pl.pallas_call(
        body, grid=(nc,),
        in_specs=[pl.BlockSpec((CHUNK, IN_BP), lambda t: (nc - 1 - t, 0)),
                  pl.BlockSpec((HALO, B_QKV), lambda t: (jnp.maximum((nc - 1 - t) * (CHUNK // HALO) - 1, 0), 0)),
                  pl.BlockSpec((HALO, B_QKV), lambda t: (0, 0)),
                  pl.BlockSpec((1, 128), lambda t: (0, 0)), pl.BlockSpec((1, 128), lambda t: (0, 0)),
                  pl.BlockSpec((MEM_LEN, 2 * X_Q), lambda t: (0, 0)),
                  pl.BlockSpec((1, B_V_HEADS, B_HD, B_HD), lambda t: (nc - 1 - t, 0, 0, 0)),
                  pl.BlockSpec((CHUNK, D), lambda t: (nc - 1 - t, 0))],
        out_specs=[pl.BlockSpec((CHUNK, IN_BP), lambda t: (nc - 1 - t, 0)),
                   pl.BlockSpec((HALO, B_QKV), lambda t: (0, 0)),
                   pl.BlockSpec((1, 128), lambda t: (0, 0)), pl.BlockSpec((1, 128), lambda t: (0, 0)),
                   pl.BlockSpec((MEM_LEN, 2 * X_Q), lambda t: (0, 0))],
        out_shape=[_SDS((s, IN_BP), _ACT), _SDS((HALO, B_QKV), F32), _SDS((1, 128), F32), _SDS((1, 128), F32),
                   _SDS((MEM_LEN, 2 * X_Q), F32)],
        scratch_shapes=[pltpu.VMEM((B_V_HEADS, B_HD, B_HD), F32), pltpu.VMEM((HALO, B_QKV), F32)],
        name=name, compiler_params=_cp("arbitrary"))(proj, proj, conv_w, par, ng, memkv, states, dmix)


def _place():
    return lax.axis_index("x"), lax.axis_index("y"), lax.axis_index("c")


def _all_gather(shards, name):
    n = len(shards)

    def body(*refs):
        ins, outs = refs[:n], refs[n:2 * n]
        send_sems, recv_sems, local_sems = refs[2 * n:]
        x, y, c = _place()
        me, sibling = (x, y, c), (x, y, 1 - c)
        chips = [(1 - x, y), (x, 1 - y), (1 - x, 1 - y)]

        def rows(a, px, py, pc):
            return outs[a].at[4 * px + 2 * py + pc]

        def copy(a, k, block, to, src=None):
            return pltpu.make_async_remote_copy(
                src_ref=rows(a, *block) if src is None else src, dst_ref=rows(a, *block),
                send_sem=send_sems.at[a, k], recv_sem=recv_sems.at[a, k],
                device_id=to, device_id_type=pl.DeviceIdType.MESH)

        mine = [pltpu.make_async_copy(ins[a], rows(a, *me), local_sems.at[a]) for a in range(n)]
        for cp in mine:
            cp.start()
        first = []
        for a in range(n):
            first.append(copy(a, 0, me, sibling, src=ins[a]))
            first += [copy(a, 1 + j, me, (*chip, c), src=ins[a]) for j, chip in enumerate(chips)]
        for cp in first:
            cp.start()
        passed = []
        for j, chip in enumerate(chips):
            for a in range(n):
                copy(a, 1 + j, (*chip, c), me).wait_recv()
                fwd = copy(a, 4 + j, (*chip, c), sibling)
                fwd.start()
                passed.append(fwd)
        for a in range(n):
            copy(a, 0, sibling, me).wait_recv()
            for j, chip in enumerate(chips):
                copy(a, 4 + j, (*chip, 1 - c), me).wait_recv()
        for cp in first + passed:
            cp.wait_send()
        for cp in mine:
            cp.wait()

    hbm = pl.BlockSpec(memory_space=pl.ANY)
    return pl.pallas_call(
        body, out_shape=[_SDS((N_DEV,) + s.shape, s.dtype) for s in shards],
        in_specs=[hbm] * n, out_specs=[hbm] * n,
        scratch_shapes=[pltpu.SemaphoreType.DMA((n, 7)), pltpu.SemaphoreType.DMA((n, 7)), pltpu.SemaphoreType.DMA((n,))],
        name=name)(*shards)


class _Sends:
    def __init__(self, plan, sems, srcs, lands, token):
        self.plan, self.sems, self.srcs, self.lands, self.token = plan, sems, srcs, lands, token


def _send_refs(plan, src_refs, land_refs, x, y, c):
    my = 4 * x + 2 * y + c

    def src_for(a, dest):
        return src_refs[a].at[dest] if plan[a][1] else src_refs[a]

    def slot(a, source):
        return land_refs[plan[a][0]].at[source]

    return my, src_for, slot


def _send_start(srcs, land_shapes, plan, name):
    n, nl = len(srcs), len(land_shapes)

    def body(*refs):
        src_refs, land_refs = refs[:n], refs[n:n + nl]
        send_sems, recv_sems = refs[n + nl], refs[n + nl + 1]
        token, local_sems = refs[2 * (n + nl) + 2], refs[2 * (n + nl) + 3]
        x, y, c = _place()
        my, src_for, slot = _send_refs(plan, src_refs, land_refs, x, y, c)
        mine = [pltpu.make_async_copy(src_for(a, my), slot(a, my), local_sems.at[a]) for a in range(n)]
        for cp in mine:
            cp.start()
        for k in range(N_DEV - 1):
            px, py, pc = x ^ ((k + 1) >> 2 & 1), y ^ ((k + 1) >> 1 & 1), c ^ ((k + 1) & 1)
            for a in range(n):
                pltpu.make_async_remote_copy(
                    src_ref=src_for(a, 4 * px + 2 * py + pc), dst_ref=slot(a, my), send_sem=send_sems.at[a * (N_DEV - 1) + k],
                    recv_sem=recv_sems.at[a * (N_DEV - 1) + k], device_id=(px, py, pc), device_id_type=pl.DeviceIdType.MESH).start()
        for cp in mine:
            cp.wait()
        token[...] = jnp.zeros_like(token)

    hbm = pl.BlockSpec(memory_space=pltpu.HBM)
    sem = pl.BlockSpec(memory_space=pltpu.SEMAPHORE)
    lands = [pltpu.with_memory_space_constraint(lax.empty(s, d), pltpu.HBM) for s, d in land_shapes]
    srcs = [pltpu.with_memory_space_constraint(s, pltpu.HBM) for s in srcs]
    out = pl.pallas_call(
        body, name=name,
        out_shape=(pltpu.SemaphoreType.DMA((n * (N_DEV - 1),)), pltpu.SemaphoreType.DMA((n * (N_DEV - 1),)),
                   *[pltpu.HBM(s.shape, s.dtype) for s in srcs], *[pltpu.HBM(s, d) for s, d in land_shapes],
                   _SDS((8, 128), F32)),
        in_specs=[hbm] * (n + nl),
        out_specs=(sem, sem, *[hbm] * (n + nl), pl.BlockSpec(memory_space=pltpu.VMEM)),
        input_output_aliases={i: 2 + i for i in range(n + nl)},
        scratch_shapes=[pltpu.SemaphoreType.DMA((n,))],
        compiler_params=pltpu.CompilerParams(has_side_effects=pltpu.SideEffectType.DATAFLOW_SIDE_EFFECTING),
    )(*srcs, *lands)
    return _Sends(plan, out[:2], out[2:2 + n], out[2 + n:2 + n + nl], out[-1])


def _send_wait(sends, after, name):
    plan = sends.plan
    n, nl = len(sends.srcs), len(sends.lands)

    def body(*refs):
        src_refs, land_refs = refs[:n], refs[n:n + nl]
        send_sems, recv_sems = refs[n + nl], refs[n + nl + 1]
        x, y, c = _place()
        my, src_for, slot = _send_refs(plan, src_refs, land_refs, x, y, c)
        for k in range(N_DEV - 1):
            px, py, pc = x ^ ((k + 1) >> 2 & 1), y ^ ((k + 1) >> 1 & 1), c ^ ((k + 1) & 1)
            peer = 4 * px + 2 * py + pc
            for a in range(n):
                cp = pltpu.make_async_remote_copy(
                    src_ref=src_for(a, peer), dst_ref=slot(a, peer), send_sem=send_sems.at[a * (N_DEV - 1) + k],
                    recv_sem=recv_sems.at[a * (N_DEV - 1) + k], device_id=(px, py, pc), device_id_type=pl.DeviceIdType.MESH)
                cp.wait_send()
                cp.wait_recv()

    hbm = pl.BlockSpec(memory_space=pltpu.HBM)
    sem = pl.BlockSpec(memory_space=pltpu.SEMAPHORE)
    out = pl.pallas_call(
        body, name=name,
        out_shape=tuple(pltpu.HBM(s.shape, s.dtype) for s in (*sends.srcs, *sends.lands)),
        in_specs=[hbm] * (n + nl) + [sem, sem, pl.BlockSpec(memory_space=pl.ANY)],
        out_specs=tuple([hbm] * (n + nl)),
        input_output_aliases={i: i for i in range(n + nl)},
        compiler_params=pltpu.CompilerParams(has_side_effects=pltpu.SideEffectType.DATAFLOW_SIDE_EFFECTING),
    )(*sends.srcs, *sends.lands, *sends.sems, after)
    return list(out[n:])


def _adam_update(g, w, m, v):
    c1 = 1.0 - ADAM_B1 ** ADAM_STEP
    c2 = 1.0 - ADAM_B2 ** ADAM_STEP
    mm = ADAM_B1 * m + (1.0 - ADAM_B1) * g
    vv = ADAM_B2 * v + (1.0 - ADAM_B2) * (g * g)
    delta = -ADAM_LR * ((mm / c1) / (jnp.sqrt(vv / c2) + ADAM_EPS) + ADAM_WD * w)
    return delta, mm, vv


def _sum_sources(p_ref):
    g = p_ref[0].astype(F32)
    for s in range(1, N_DEV):
        g = g + p_ref[s].astype(F32)
    return g


def _adamw(parts, w, m, v, tr, name, restore_b=False):
    nl, r, c = w.shape
    cp = parts[0].shape[-1]

    def body(*refs):
        p_refs = refs[:nl]
        w_ref, m_ref, v_ref, g_ref, d_ref, nm_ref, nv_ref = refs[nl:]
        g = _sum_sources(p_refs[0])
        for l in range(1, nl):
            g = jnp.where(pl.program_id(0) == l, _sum_sources(p_refs[l]), g)
        if restore_b:
            g = jnp.concatenate([g[:, :BP_XQ], g[:, BP_GATE:BP_GATE + 2 * B_V_HEADS], g[:, BP_XQ:BP_GATE]], axis=1)
        delta, mm, vv = _adam_update(g, w_ref[...], m_ref[...], v_ref[...])
        g_ref[...] = g
        d_ref[...] = delta
        nm_ref[...] = mm
        nv_ref[...] = vv

    spec = pl.BlockSpec((None, tr, c), lambda l, i: (l, i, 0))
    part_specs = [pl.BlockSpec((N_DEV, tr, cp), functools.partial(lambda l, i, k: (0, jnp.where(l == k, i, 0), 0), k=k))
                  for k in range(nl)]
    return pl.pallas_call(
        body, grid=(nl, r // tr),
        in_specs=part_specs + [spec, spec, spec],
        out_specs=[spec] * 4, out_shape=[_SDS(w.shape, F32)] * 4,
        name=name, compiler_params=_cp("arbitrary", "arbitrary"))(*parts, w, m, v)


def _pack_small(d_rel, d_cb, d_cw, d_qkv, d_mix, d_mem, d_ffn, d_final, d_sinks, d_par, d_ng, name):
    flat = [d_rel, *d_cb, *d_cw, d_qkv, *d_mix, *d_mem, *d_ffn, d_final, d_sinks, d_par, d_ng]
    n = len(flat)

    def body(*refs):
        ins, o_ref = refs[:n], refs[n]
        rel, cb0, cb1, cw0, cw1, qkv, mx0, mx1, me0, me1, ff0, ff1, fin, snk, par, ng = ins
        o_ref[...] = jnp.zeros_like(o_ref)
        o_ref[SP_REL:SP_REL + N_BUCKETS, 0:128] = rel[...]
        for l, (cb, cw) in enumerate(((cb0, cw0), (cb1, cw1))):
            o_ref[SP_CB + l:SP_CB + l + 1, :] = jnp.concatenate([cb[j] for j in range(FF_BLOCKS)], axis=1)
            full = jnp.concatenate([cw[j] for j in range(FF_BLOCKS)], axis=1)
            o_ref[SP_CW + FFN_CONV * l:SP_CW + FFN_CONV * (l + 1), :] = full[:FFN_CONV]
        o_ref[SP_QKV:SP_QKV + B_CONV, 0:B_QKV] = qkv[0:B_CONV, :]
        for base, pair in ((SP_MIX, (mx0, mx1)), (SP_MEM, (me0, me1)), (SP_FFN, (ff0, ff1))):
            for l in range(2):
                o_ref[base + l:base + l + 1, 0:D] = pair[l][...]
        o_ref[SP_FINAL:SP_FINAL + 1, 0:D] = fin[...]
        o_ref[SP_MISC:SP_MISC + 1, 0:128] = snk[...]
        o_ref[SP_MISC:SP_MISC + 1, 128:256] = par[...]
        o_ref[SP_MISC:SP_MISC + 1, 256:384] = ng[...]

    vm = pl.BlockSpec(memory_space=pltpu.VMEM)
    return pl.pallas_call(body, in_specs=[vm] * n, out_specs=vm, out_shape=_SDS((SMALL_ROWS, D_FF), F32), name=name)(*flat)


_SMALL = ["rel_bias", "norm_mix_g", "norm_mem_g", "sinks_a", "a_log_b", "dt_bias_b", "out_norm_g_b", "norm_ffn_g",
          "ffn_conv_b", "final_norm_g", "conv_qkv_b", "ffn_conv_w"]


def _adamw_small(recv, rc_qkv, rc_ffn, ws, ms, vs, name):
    n = len(_SMALL)

    def body(*refs):
        recv_ref, qkv_ref, ffn_ref = refs[:3]
        w_refs, m_refs, v_refs = refs[3:3 + n], refs[3 + n:3 + 2 * n], refs[3 + 2 * n:3 + 3 * n]
        outs = refs[3 + 3 * n:]
        gs = _sum_sources(recv_ref)
        grads = {
            "rel_bias": gs[SP_REL:SP_REL + N_BUCKETS, 0:A_HEADS],
            "norm_mix_g": gs[SP_MIX:SP_MIX + 2, 0:D], "norm_mem_g": gs[SP_MEM:SP_MEM + 2, 0:D],
            "sinks_a": gs[SP_MISC:SP_MISC + 1, 0:A_HEADS],
            "a_log_b": gs[SP_MISC:SP_MISC + 1, 128:128 + B_V_HEADS],
            "dt_bias_b": gs[SP_MISC:SP_MISC + 1, 128 + B_V_HEADS:128 + 2 * B_V_HEADS],
            "out_norm_g_b": gs[SP_MISC:SP_MISC + 1, 256:256 + B_HD],
            "norm_ffn_g": gs[SP_FFN:SP_FFN + 2, 0:D], "ffn_conv_b": gs[SP_CB:SP_CB + 2, :],
            "final_norm_g": gs[SP_FINAL:SP_FINAL + 1, 0:D],
            "conv_qkv_b": _sum_sources(qkv_ref), "ffn_conv_w": _sum_sources(ffn_ref),
        }
        for i, nm in enumerate(_SMALL):
            g = grads[nm]
            delta, mm, vv = _adam_update(g, w_refs[i][...], m_refs[i][...], v_refs[i][...])
            outs[i][...] = g
            outs[n + i][...] = delta
            outs[2 * n + i][...] = mm
            outs[3 * n + i][...] = vv

    vm = pl.BlockSpec(memory_space=pltpu.VMEM)
    shapes = [_SDS(w.shape, F32) for w in ws]
    return pl.pallas_call(
        body, in_specs=[vm] * (3 + 3 * n), out_specs=[vm] * (4 * n), out_shape=shapes * 4,
        name=name)(recv, rc_qkv, rc_ffn, *ws, *ms, *vs)


def _assemble(gathered, axis):
    g = jnp.moveaxis(gathered, 0, axis)
    shp = list(g.shape)
    return g.reshape(shp[:axis] + [shp[axis] * shp[axis + 1]] + shp[axis + 2:])


def _pad_rows(a, rows):
    return jnp.pad(a, ((0, rows - a.shape[0]), (0, 0)))


def _pad_lanes(a, lanes=128):
    return jnp.pad(a, ((0, 0), (0, lanes - a.shape[1])))


def _ff_blocks(a):
    return jnp.moveaxis(a.reshape(a.shape[0], FF_BLOCKS, GU_SHARD), 1, 0)


def _reorder_b(w):
    qkv_z = w[..., :B_QKV + B_V]
    gates = w[..., B_QKV + B_V:B_QKV + B_V + 2 * B_V_HEADS]
    xq = w[..., IN_B - X_Q:]
    pad = jnp.zeros(w.shape[:-1] + (IN_BP - IN_B,), w.dtype)
    return jnp.concatenate([qkv_z, xq, gates, pad], axis=-1)


def kernel(x, mem, rel_bias, norm_mix_g, norm_mem_g, w_mem_kv, w_out, w_in_a, sinks_a, w_in_b, conv_qkv_b, a_log_b, dt_bias_b, out_norm_g_b, norm_ffn_g, w_gate_up, ffn_conv_w, ffn_conv_b, w_down, final_norm_g, loss_target, m_rel_bias, m_norm_mix_g, m_norm_mem_g, m_w_mem_kv, m_w_out, m_w_in_a, m_sinks_a, m_w_in_b, m_conv_qkv_b, m_a_log_b, m_dt_bias_b, m_out_norm_g_b, m_norm_ffn_g, m_w_gate_up, m_ffn_conv_w, m_ffn_conv_b, m_w_down, m_final_norm_g, v_rel_bias, v_norm_mix_g, v_norm_mem_g, v_w_mem_kv, v_w_out, v_w_in_a, v_sinks_a, v_w_in_b, v_conv_qkv_b, v_a_log_b, v_dt_bias_b, v_out_norm_g_b, v_norm_ffn_g, v_w_gate_up, v_ffn_conv_w, v_ffn_conv_b, v_w_down, v_final_norm_g):
    local = dict(locals())
    order = ["rel_bias", "norm_mix_g", "norm_mem_g", "w_mem_kv", "w_out", "w_in_a", "sinks_a", "w_in_b", "conv_qkv_b",
             "a_log_b", "dt_bias_b", "out_norm_g_b", "norm_ffn_g", "w_gate_up", "ffn_conv_w", "ffn_conv_b", "w_down",
             "final_norm_g"]
    wts = {n: local[n] for n in order}
    moms = {n: local["m_" + n] for n in order}
    vars_ = {n: local["v_" + n] for n in order}
    h0 = x[0]
    memx = mem[0]
    tgt = loss_target[0]
    s = h0.shape[0]
    tm = _rows(s)

    g_mk, g_out, g_ia, g_cq, g_cw = _all_gather(
        [w_mem_kv.astype(_MXU), w_out.astype(_MXU), w_in_a.astype(_MXU), conv_qkv_b, ffn_conv_w], "gather_first")
    gu_land = ((N_DEV, D, GU_SHARD), _MXU)
    dn_land = ((N_DEV, DN_SHARD, D), _MXU)
    whole = [(0, False), (1, False)]
    ffn0_w = _send_start([w_gate_up[0].astype(_MXU), w_down[0].astype(_MXU)], [gu_land, dn_land], whole, "gather_ffn0_start")
    w_ia = _assemble(g_ia, 2)[0]
    conv_qkv = _pad_rows(_assemble(g_cq, 2)[0], HALO)
    ffn_cw_full = _assemble(g_cw, 2)
    ffn_cw = [_ff_blocks(_pad_rows(ffn_cw_full[i], HALO)) for i in range(2)]
    ffn_cb = [_ff_blocks(ffn_conv_b[i:i + 1]) for i in range(2)]
    bucket = jnp.asarray(_bucket_table())
    bias = _bias_build(rel_bias, bucket, "bias_build")
    sinks = _pad_lanes(sinks_a)
    par_b = _pad_lanes(jnp.concatenate([a_log_b, dt_bias_b], axis=1))

    row_x = pl.BlockSpec((tm, D), lambda i, j: (i, 0))
    gu_shape = (2, FF_BLOCKS, s, GU_SHARD)

    def in_proj(h, g, w, w_spec, n_cols, tn, name, deps=(), out_dtype=F32):
        return _norm_matmul(h, g, w, w_spec, n_cols // tn, (h.shape[0], n_cols),
                            pl.BlockSpec((_rows(h.shape[0]), tn), lambda i, j: (i, j)), name, deps=deps, out_dtype=out_dtype)

    def ffn_fwd(i, h, g_gu, g_dn, deps=()):
        gu, hn = _norm_matmul(h, norm_ffn_g[i:i + 1], g_gu, _spec_gate_up(1), N_DEV, gu_shape,
                              _spec_gu_act(0, 1, tm), f"gate_up_{i}", deps=deps, out_dtype=_ACT)
        act = _glu_fwd(gu, ffn_cw[i], ffn_cb[i], f"glu_fwd_{i}")
        h_new = _matmul_res(act, pl.BlockSpec((None, tm, GU_SHARD), lambda r, j: (j, r, 0)), g_dn, _spec_down(1),
                            FF_BLOCKS, h, f"down_proj_{i}")
        return h_new, gu, hn, act

    def out_proj(i, mix, h):
        return _matmul_res(mix, row_x, g_out, _spec_rowsharded(i, D // N_DEV, D), 1, h, f"out_proj_{i}")

    proj_a, hn_a = in_proj(h0, norm_mix_g[0:1], w_ia, pl.BlockSpec((D, 640), lambda i, j: (0, j)), IN_A, 640, "in_proj_a",
                           deps=[ffn0_w.token], out_dtype=_ACT)
    memkv0, memn0 = in_proj(memx, norm_mem_g[0:1], g_mk, _spec_rowsharded(0, D // N_DEV, 2 * X_Q), 2 * X_Q, 2 * X_Q, "mem_proj_0")
    mix_a = _mix_a_fwd(proj_a, bias, sinks, memkv0, "mix_a_fwd")
    h1 = out_proj(0, mix_a, h0)
    g_gu0, g_dn0 = _send_wait(ffn0_w, h1, "gather_ffn0_wait")
    in_b_w = _send_start([_reorder_b(w_in_b).astype(_MXU)], [((N_DEV, 1, D // N_DEV, IN_BP), _MXU)], [(0, False)], "gather_in_b_start")
    ffn1_w = _send_start([w_gate_up[1].astype(_MXU), w_down[1].astype(_MXU)], [gu_land, dn_land], whole, "gather_ffn1_start")
    h2, gu0, hn_f0, act0 = ffn_fwd(0, h1, g_gu0, g_dn0, deps=[in_b_w.token, ffn1_w.token])
    g_ib, = _send_wait(in_b_w, h2, "gather_in_b_wait")
    proj_b, hn_b = in_proj(h2, norm_mix_g[1:2], g_ib, _spec_rowsharded(0, D // N_DEV, 896, col_block=1), IN_BP, 896, "in_proj_b")
    memkv1, memn1 = in_proj(memx, norm_mem_g[1:2], g_mk, _spec_rowsharded(1, D // N_DEV, 2 * X_Q), 2 * X_Q, 2 * X_Q, "mem_proj_1")
    mix_b, states = _mix_b_fwd(proj_b, conv_qkv, par_b, out_norm_g_b, memkv1, "mix_b_fwd")
    h3 = out_proj(1, mix_b, h2)
    g_gu1, g_dn1 = _send_wait(ffn1_w, h3, "gather_ffn1_wait")
    h4, gu1, hn_f1, act1 = ffn_fwd(1, h3, g_gu1, g_dn1)
    loss_row, dh, d_final_g = _loss_head(h4, final_norm_g[None, :], tgt, "loss_head")

    zeros_mem = jnp.zeros_like(memx)
    per_dest2 = [(0, True), (1, True)]

    def ffn_bwd(i, dh, h_in, gu, hn_f, act, g_gu, g_dn, deps=()):
        dact = _matmul_nt(dh, g_dn, _spec_down(1), FF_BLOCKS, (FF_BLOCKS, s, GU_SHARD),
                          pl.BlockSpec((None, tm, GU_SHARD), lambda r, j: (j, r, 0)), f"d_act_{i}", deps=deps, out_dtype=_ACT)
        d_wdown = _matmul_tn(act, pl.BlockSpec((None, tm, GU_SHARD), lambda j, r: (j, r, 0)),
                             dh, pl.BlockSpec((tm, D), lambda j, r: (r, 0)), s, FF_BLOCKS, (GU_SHARD, D),
                             (N_DEV, DN_SHARD, D), pl.BlockSpec((2, DN_SHARD, D), lambda j, r: (j, 0, 0)), f"d_w_down_{i}")
        dgu, d_cw, d_cb = _glu_bwd(gu, ffn_cw[i], ffn_cb[i], dact, f"glu_bwd_{i}")
        dh_new, d_g = _matmul_nt_normbwd(dgu, _spec_gu_act(0, 1, tm), g_gu, _spec_gate_up(1), N_DEV, h_in,
                                         norm_ffn_g[i:i + 1], dh, f"d_ffn_in_{i}")
        d_wgu = _matmul_tn(hn_f, pl.BlockSpec((tm, D), lambda j, r: (r, 0)), dgu, _spec_gu_act(1, 0, tm), s, N_DEV,
                           (D, GU_SHARD), (N_DEV, D, GU_SHARD), pl.BlockSpec((None, D, GU_SHARD), lambda j, r: (j, 0, 0)),
                           f"d_w_gate_up_{i}")
        sent = _send_start([d_wdown, d_wgu], [((N_DEV, DN_SHARD, D), _WIRE), ((N_DEV, D, GU_SHARD), _WIRE)], per_dest2,
                           f"send_ffn{i}_grads_start")
        return dh_new, sent, d_cw, d_cb, d_g

    def out_bwd(i, dh, mix, deps):
        dmix = _matmul_nt(dh, g_out, _spec_rowsharded(i, D // N_DEV, D), 1, (s, D), row_x, f"d_mix_{i}", deps=deps, out_dtype=_ACT)
        d_wout = _matmul_tn(mix, pl.BlockSpec((tm, D), lambda j, r: (r, 0)), dh, pl.BlockSpec((tm, D), lambda j, r: (r, 0)),
                            s, 1, (D, D), (N_DEV, D // N_DEV, D), pl.BlockSpec((N_DEV, D // N_DEV, D), lambda j, r: (0, 0, 0)),
                            f"d_w_out_{i}")
        return dmix, d_wout

    def mem_bwd(i, dmemkv, memn):
        tmm = _rows(MEM_LEN)
        _, d_g = _matmul_nt_normbwd(dmemkv, pl.BlockSpec((tmm, 2 * X_Q), lambda r, j: (r, 0)), g_mk,
                                    _spec_rowsharded(i, D // N_DEV, 2 * X_Q), 1, memx, norm_mem_g[i:i + 1], zeros_mem,
                                    f"d_mem_in_{i}")
        by_row = lambda j, r: (r, 0)
        d_w = _matmul_tn(memn, pl.BlockSpec((tmm, D), by_row), dmemkv, pl.BlockSpec((tmm, 2 * X_Q), by_row), MEM_LEN, 1,
                         (D, 2 * X_Q), (N_DEV, D // N_DEV, 2 * X_Q),
                         pl.BlockSpec((N_DEV, D // N_DEV, 2 * X_Q), lambda j, r: (0, 0, 0)), f"d_w_mem_kv_{i}")
        return d_w, d_g

    out_land = ((N_DEV, D // N_DEV, D), _WIRE)
    mk_land = ((N_DEV, D // N_DEV, 2 * X_Q), _WIRE)
    dh, ffn1_g, d_cw1, d_cb1, d_gf1 = ffn_bwd(1, dh, h3, gu1, hn_f1, act1, g_gu1, g_dn1)
    dmix, d_wout1 = out_bwd(1, dh, mix_b, [ffn1_g.token])
    dproj_b, d_convw, d_par, d_ng, dmemkv1 = _mix_b_bwd(proj_b, conv_qkv, par_b, out_norm_g_b, memkv1, states, dmix, "mix_b_bwd")
    dh, d_gm1 = _matmul_nt_normbwd(dproj_b, pl.BlockSpec((tm, 896), lambda i, j: (i, j)), g_ib,
                                   _spec_rowsharded(0, D // N_DEV, 896, col_block=1), IN_BP // 896, h2, norm_mix_g[1:2], dh, "d_in_b")
    d_wib = _matmul_tn(hn_b, pl.BlockSpec((tm, D), lambda j, r: (r, 0)), dproj_b, pl.BlockSpec((tm, 896), lambda j, r: (r, j)),
                       s, IN_BP // 896, (D, 896), (N_DEV, D // N_DEV, IN_BP),
                       pl.BlockSpec((N_DEV, D // N_DEV, 896), lambda j, r: (0, 0, j)), "d_w_in_b")
    d_wmk1, d_gmem1 = mem_bwd(1, dmemkv1, memn1)
    mix1_g = _send_start([d_wout1, d_wib, d_wmk1], [out_land, ((N_DEV, D // N_DEV, IN_BP), _WIRE), mk_land],
                         [(0, True), (1, True), (2, True)], "send_mix1_grads_start")
    dh, ffn0_g, d_cw0, d_cb0, d_gf0 = ffn_bwd(0, dh, h1, gu0, hn_f0, act0, g_gu0, g_dn0, deps=[mix1_g.token])
    dmix, d_wout0 = out_bwd(0, dh, mix_a, [ffn0_g.token])
    dproj_a, dbias, dsinks, dmemkv0 = _mix_a_bwd(proj_a, bias, sinks, memkv0, dmix, "mix_a_bwd")
    dh, d_gm0 = _matmul_nt_normbwd(dproj_a, pl.BlockSpec((tm, 640), lambda i, j: (i, j)), w_ia,
                                   pl.BlockSpec((D, 640), lambda i, j: (0, j)), IN_A // 640, h0, norm_mix_g[0:1], dh, "d_in_a")
    d_wia = _matmul_tn(hn_a, pl.BlockSpec((tm, D), lambda j, r: (r, 0)), dproj_a, pl.BlockSpec((tm, IN_A), lambda j, r: (r, 0)),
                       s, 1, (D, IN_A), (N_DEV, D, IA_SHARD), pl.BlockSpec((N_DEV, D, IA_SHARD), lambda j, r: (0, 0, 0)),
                       "d_w_in_a", split=IA_SHARD)
    d_wmk0, d_gmem0 = mem_bwd(0, dmemkv0, memn0)
    d_rel = _bias_reduce(dbias, bucket, "bias_reduce")
    small = _pack_small(d_rel, (d_cb0, d_cb1), (d_cw0, d_cw1), d_convw, (d_gm0, d_gm1), (d_gmem0, d_gmem1),
                        (d_gf0, d_gf1), d_final_g, dsinks, d_par, d_ng, "pack_small")
    mix0_g = _send_start([d_wout0, d_wia, d_wmk0, small],
                         [out_land, ((N_DEV, D, IA_SHARD), _WIRE), mk_land, ((N_DEV, SMALL_ROWS, D_FF), F32)],
                         [(0, True), (1, True), (2, True), (3, False)], "send_mix0_grads_start")

    r_out0, r_ia, r_mk0, r_small = _send_wait(mix0_g, mix0_g.token, "send_mix0_grads_wait")
    r_dn1, r_gu1 = _send_wait(ffn1_g, mix0_g.token, "send_ffn1_grads_wait")
    r_out1, r_ib, r_mk1 = _send_wait(mix1_g, mix0_g.token, "send_mix1_grads_wait")
    r_dn0, r_gu0 = _send_wait(ffn0_g, mix0_g.token, "send_ffn0_grads_wait")

    res = {}
    for nm, parts, tr, restore in (("w_mem_kv", [r_mk0, r_mk1], 128, False), ("w_out", [r_out0, r_out1], 128, False),
                                  ("w_in_a", [r_ia], 512, False), ("w_in_b", [r_ib], 32, True),
                                  ("w_gate_up", [r_gu0, r_gu1], 128, False), ("w_down", [r_dn0, r_dn1], 176, False)):
        res[nm] = _adamw(parts, wts[nm], moms[nm], vars_[nm], tr, "adamw_" + nm, restore_b=restore)

    my = 4 * lax.axis_index("x") + 2 * lax.axis_index("y") + lax.axis_index("c")
    cq = conv_qkv_b.shape[-1]
    cf = ffn_conv_w.shape[-1]
    rc_qkv = lax.dynamic_slice_in_dim(r_small[:, SP_QKV:SP_QKV + B_CONV, :B_QKV], my * cq, cq, axis=2)[:, None]
    rc_ffn = lax.dynamic_slice_in_dim(r_small[:, SP_CW:SP_CW + 2 * FFN_CONV, :], my * cf, cf, axis=2).reshape(N_DEV, 2, FFN_CONV, cf)
    as2d = lambda a: a[None, :] if a.ndim == 1 else a
    small_out = _adamw_small(r_small, rc_qkv, rc_ffn, [as2d(wts[n]) for n in _SMALL], [as2d(moms[n]) for n in _SMALL],
                             [as2d(vars_[n]) for n in _SMALL], "adamw_small")
    ns = len(_SMALL)
    for i, nm in enumerate(_SMALL):
        res[nm] = [small_out[k * ns + i].reshape(wts[nm].shape) for k in range(4)]

    loss = lax.psum(loss_row[0, 0], AXES)
    return (loss, dh[None], *[res[n][0] for n in order], *[res[n][1] for n in order],
            *[res[n][2] for n in order], *[res[n][3] for n in order])
```

```python
import functools
import math

import numpy as np

import jax
import jax.numpy as jnp
from jax import lax
from jax.experimental import pallas as pl
from jax.experimental.pallas import tpu as pltpu

F32 = jnp.float32
_MXU = jnp.bfloat16
_ACT = jnp.bfloat16
_WIRE = jnp.bfloat16
_HI = lax.Precision.HIGH
_TM = 1024
_TM_GLU = 512
_VMEM_LIMIT = 48 * 1024 * 1024
_SDS = jax.ShapeDtypeStruct

D = 1024
EPS = 1e-6
A_HEADS, A_KV_HEADS, A_HD, BLK = 12, 2, 64, 128
N_BUCKETS, MAX_DISTANCE = 32, 128
B_QK_HEADS, B_V_HEADS, B_HD, B_CONV, CHUNK = 3, 6, 128, 4, 64
X_HEADS, X_HD, MEM_LEN = 4, 64, 256
D_FF, FFN_CONV = 2816, 3
A_Q, A_KV, X_Q = 768, 128, 256
B_QK, B_V, B_QKV = 384, 768, 1536
IN_A, IN_B = 1280, 2572
IN_BP = 2688
BP_Z, BP_XQ, BP_GATE = 1536, 2304, 2560
HALO = 8
GLU_HALO = 16

N_DEV = 8
AXES = ("x", "y", "c")
GU_SHARD = 2 * D_FF // N_DEV
FF_BLOCKS = D_FF // GU_SHARD
DN_SHARD = D_FF // N_DEV
IA_SHARD = IN_A // N_DEV

ADAM_LR, ADAM_B1, ADAM_B2, ADAM_EPS, ADAM_WD, ADAM_STEP = 0.001, 0.9, 0.999, 1e-08, 0.01, 10

SP_REL, SP_CB, SP_CW, SP_QKV, SP_MIX, SP_MEM, SP_FFN, SP_FINAL, SP_MISC, SMALL_ROWS = 0, 32, 34, 40, 44, 46, 48, 50, 51, 56


def _cp(*sems):
    return pltpu.CompilerParams(dimension_semantics=sems, vmem_limit_bytes=_VMEM_LIMIT)


def _mm(a, b):
    return jnp.dot(a.astype(_MXU), b.astype(_MXU), preferred_element_type=F32)


def _mm_nt(a, b):
    return lax.dot_general(a.astype(_MXU), b.astype(_MXU), (((1,), (1,)), ((), ())), preferred_element_type=F32)


def _mm_tn(a, b):
    return lax.dot_general(a.astype(_MXU), b.astype(_MXU), (((0,), (0,)), ((), ())), preferred_element_type=F32)


def _mmf(a, b):
    return jnp.dot(a, b, preferred_element_type=F32, precision=_HI)


def _mmf_nt(a, b):
    return lax.dot_general(a, b, (((1,), (1,)), ((), ())), preferred_element_type=F32, precision=_HI)


def _mmf_tn(a, b):
    return lax.dot_general(a, b, (((0,), (0,)), ((), ())), preferred_element_type=F32, precision=_HI)


def _silu(x):
    return x * jax.nn.sigmoid(x)


def _w2d(ref):
    v = ref[...]
    return v.reshape(-1, v.shape[-1])


def _rows(m):
    return min(m, _TM)


def _spec_rowsharded(layer, rows, cols, col_block=None):
    if col_block is None:
        return pl.BlockSpec((N_DEV, None, rows, cols), lambda *_: (0, layer, 0, 0))
    return pl.BlockSpec((N_DEV, None, rows, cols), lambda *ids: (0, layer, 0, ids[col_block]))


def _spec_gate_up(axis):
    return pl.BlockSpec((None, D, GU_SHARD), lambda *ids: (ids[axis], 0, 0))


def _spec_down(axis):
    return pl.BlockSpec((2, DN_SHARD, D), lambda *ids: (ids[axis], 0, 0))


def _dep_specs(deps):
    return [pl.BlockSpec(d.shape, lambda *_: (0,) * d.ndim) for d in deps]


def _spec_gu_act(row_axis, axis, tm):
    return pl.BlockSpec((None, None, tm, GU_SHARD), lambda *ids: (ids[axis] // FF_BLOCKS, ids[axis] % FF_BLOCKS, ids[row_axis], 0))


def _norm_matmul(x, g, w, w_spec, n_blocks, out_shape, out_spec, name, deps=(), out_dtype=F32):
    m, k = x.shape
    tm = _rows(m)

    def body(x_ref, g_ref, w_ref, *rest):
        y_ref, hn_ref = rest[-2:]

        @pl.when(pl.program_id(1) == 0)
        def _():
            xv = x_ref[...]
            r = lax.rsqrt(jnp.mean(xv * xv, axis=-1, keepdims=True) + EPS)
            hn_ref[...] = (xv * r * g_ref[...]).astype(hn_ref.dtype)

        y_ref[...] = _mm(hn_ref[...], _w2d(w_ref)).astype(y_ref.dtype)

    return pl.pallas_call(
        body, grid=(m // tm, n_blocks),
        in_specs=[pl.BlockSpec((tm, k), lambda i, j: (i, 0)), pl.BlockSpec((1, k), lambda i, j: (0, 0)), w_spec]
        + _dep_specs(deps),
        out_specs=[out_spec, pl.BlockSpec((tm, k), lambda i, j: (i, 0))],
        out_shape=[_SDS(out_shape, out_dtype), _SDS((m, k), _ACT)],
        name=name, compiler_params=_cp("arbitrary", "arbitrary"))(x, g, w, *deps)


def _matmul_res(a, a_spec, w, w_spec, n_k, res, name):
    m, n = res.shape
    tm = _rows(m)

    def body(a_ref, w_ref, r_ref, o_ref):
        part = _mm(a_ref[...], _w2d(w_ref))

        @pl.when(pl.program_id(1) == 0)
        def _():
            o_ref[...] = r_ref[...] + part

        @pl.when(pl.program_id(1) > 0)
        def _():
            o_ref[...] += part

    return pl.pallas_call(
        body, grid=(m // tm, n_k),
        in_specs=[a_spec, w_spec, pl.BlockSpec((tm, n), lambda i, j: (i, 0))],
        out_specs=pl.BlockSpec((tm, n), lambda i, j: (i, 0)),
        out_shape=_SDS((m, n), F32), name=name, compiler_params=_cp("arbitrary", "arbitrary"))(a, w, res)


def _matmul_nt(dy, w, w_spec, n_blocks, out_shape, out_spec, name, deps=(), out_dtype=F32):
    m, n = dy.shape
    tm = _rows(m)

    def body(dy_ref, w_ref, *rest):
        o_ref = rest[-1]
        o_ref[...] = _mm_nt(dy_ref[...], _w2d(w_ref)).astype(o_ref.dtype)

    return pl.pallas_call(
        body, grid=(m // tm, n_blocks),
        in_specs=[pl.BlockSpec((tm, n), lambda i, j: (i, 0)), w_spec] + _dep_specs(deps),
        out_specs=out_spec, out_shape=_SDS(out_shape, out_dtype),
        name=name, compiler_params=_cp("arbitrary", "arbitrary"))(dy, w, *deps)


def _matmul_nt_normbwd(dy, dy_spec, w, w_spec, nj, h, g, dh_in, name):
    m, k = h.shape
    tm = _rows(m)

    def body(dy_ref, w_ref, h_ref, g_ref, dhin_ref, dh_ref, dg_ref, acc_ref):
        i, j = pl.program_id(0), pl.program_id(1)

        @pl.when(j == 0)
        def _():
            acc_ref[...] = jnp.zeros_like(acc_ref)

        acc_ref[...] += _mm_nt(dy_ref[...], _w2d(w_ref))

        @pl.when(j == nj - 1)
        def _():
            xv = h_ref[...]
            r = lax.rsqrt(jnp.mean(xv * xv, axis=-1, keepdims=True) + EPS)
            xh = xv * r
            dhn = acc_ref[...]
            part = jnp.sum(dhn * xh, axis=0, keepdims=True)

            @pl.when(i == 0)
            def _():
                dg_ref[...] = part

            @pl.when(i > 0)
            def _():
                dg_ref[...] += part

            t = dhn * g_ref[...]
            dh_ref[...] = dhin_ref[...] + r * (t - xh * jnp.mean(t * xh, axis=-1, keepdims=True))

    return pl.pallas_call(
        body, grid=(m // tm, nj),
        in_specs=[dy_spec, w_spec, pl.BlockSpec((tm, k), lambda i, j: (i, 0)), pl.BlockSpec((1, k), lambda i, j: (0, 0)),
                  pl.BlockSpec((tm, k), lambda i, j: (i, 0))],
        out_specs=[pl.BlockSpec((tm, k), lambda i, j: (i, 0)), pl.BlockSpec((1, k), lambda i, j: (0, 0))],
        out_shape=[_SDS((m, k), F32), _SDS((1, k), F32)],
        scratch_shapes=[pltpu.VMEM((tm, k), F32)],
        name=name, compiler_params=_cp("arbitrary", "arbitrary"))(dy, w, h, g, dh_in)


def _matmul_tn(x, x_spec, dy, dy_spec, m, n_blocks, acc_shape, out_shape, out_spec, name, split=None):
    tm = _rows(m)
    nm = m // tm

    def body(x_ref, dy_ref, o_ref, acc_ref):
        @pl.when(pl.program_id(1) == 0)
        def _():
            acc_ref[...] = jnp.zeros_like(acc_ref)

        acc_ref[...] += _mm_tn(x_ref[...], dy_ref[...])

        @pl.when(pl.program_id(1) == nm - 1)
        def _():
            if split is None:
                o_ref[...] = acc_ref[...].reshape(o_ref.shape).astype(o_ref.dtype)
            else:
                for d in range(N_DEV):
                    o_ref[d] = acc_ref[:, d * split:(d + 1) * split].astype(o_ref.dtype)

    return pl.pallas_call(
        body, grid=(n_blocks, nm), in_specs=[x_spec, dy_spec], out_specs=out_spec,
        out_shape=_SDS(out_shape, _WIRE), scratch_shapes=[pltpu.VMEM(acc_shape, F32)],
        name=name, compiler_params=_cp("arbitrary", "arbitrary"))(x, dy)


def _loss_head(h, g, tgt, name):
    m, k = h.shape
    tm = _rows(m)

    def body(h_ref, g_ref, t_ref, loss_ref, dh_ref, dg_ref):
        i = pl.program_id(0)
        xv = h_ref[...]
        r = lax.rsqrt(jnp.mean(xv * xv, axis=-1, keepdims=True) + EPS)
        xh = xv * r
        gv = g_ref[...]
        err = xh * gv - t_ref[...]
        lpart = jnp.zeros((1, 128), F32) + 0.5 * jnp.sum(jnp.mean(err * err, axis=-1, keepdims=True), axis=0, keepdims=True)
        dy = err * (1.0 / k)
        gpart = jnp.sum(dy * xh, axis=0, keepdims=True)

        @pl.when(i == 0)
        def _():
            loss_ref[...] = lpart
            dg_ref[...] = gpart

        @pl.when(i > 0)
        def _():
            loss_ref[...] += lpart
            dg_ref[...] += gpart

        t = dy * gv
        dh_ref[...] = r * (t - xh * jnp.mean(t * xh, axis=-1, keepdims=True))

    return pl.pallas_call(
        body, grid=(m // tm,),
        in_specs=[pl.BlockSpec((tm, k), lambda i: (i, 0)), pl.BlockSpec((1, k), lambda i: (0, 0)),
                  pl.BlockSpec((tm, k), lambda i: (i, 0))],
        out_specs=[pl.BlockSpec((1, 128), lambda i: (0, 0)), pl.BlockSpec((tm, k), lambda i: (i, 0)),
                   pl.BlockSpec((1, k), lambda i: (0, 0))],
        out_shape=[_SDS((1, 128), F32), _SDS((m, k), F32), _SDS((1, k), F32)],
        name=name, compiler_params=_cp("arbitrary"))(h, g, tgt)


def _glu_down(gu, conv_w, conv_b, w_down, res, name):
    s = gu.shape[2]
    tm = min(s, _TM_GLU)

    def body(gu_ref, prev_ref, w_ref, b_ref, wdn_ref, r_ref, o_ref, act_ref):
        i, j = pl.program_id(0), pl.program_id(1)
        prev = jnp.where(i > 0, prev_ref[...].astype(F32), 0.0)
        ext = jnp.concatenate([prev, gu_ref[0].astype(F32)], axis=0)
        gc = b_ref[...] + w_ref[FFN_CONV - 1:FFN_CONV, :] * ext
        for k in range(FFN_CONV - 1):
            gc = gc + w_ref[k:k + 1, :] * pltpu.roll(ext, FFN_CONV - 1 - k, 0)
        act = (_silu(gc[GLU_HALO:]) * gu_ref[1].astype(F32)).astype(act_ref.dtype)
        act_ref[...] = act
        part = _mm(act, _w2d(wdn_ref))

        @pl.when(j == 0)
        def _():
            o_ref[...] = r_ref[...] + part

        @pl.when(j > 0)
        def _():
            o_ref[...] += part

    return pl.pallas_call(
        body, grid=(s // tm, FF_BLOCKS),
        in_specs=[pl.BlockSpec((2, None, tm, GU_SHARD), lambda i, j: (0, j, i, 0)),
                  pl.BlockSpec((None, None, GLU_HALO, GU_SHARD),
                               lambda i, j: (0, j, jnp.maximum(i * (tm // GLU_HALO) - 1, 0), 0)),
                  pl.BlockSpec((None, HALO, GU_SHARD), lambda i, j: (j, 0, 0)),
                  pl.BlockSpec((None, 1, GU_SHARD), lambda i, j: (j, 0, 0)),
                  _spec_down(1), pl.BlockSpec((tm, D), lambda i, j: (i, 0))],
        out_specs=[pl.BlockSpec((tm, D), lambda i, j: (i, 0)), pl.BlockSpec((None, tm, GU_SHARD), lambda i, j: (j, i, 0))],
        out_shape=[_SDS((s, D), F32), _SDS((FF_BLOCKS, s, GU_SHARD), _ACT)], name=name,
        compiler_params=_cp("arbitrary", "arbitrary"))(gu, gu, conv_w, conv_b, w_down, res)


def _glu_bwd(gu, conv_w, conv_b, dh, w_down, name, deps=()):
    s = gu.shape[2]
    tm = min(s, _TM_GLU)
    nt = s // tm
    ext_rows = tm + GLU_HALO

    def body(gu_ref, prev_ref, w_ref, b_ref, dh_ref, wdn_ref, *rest):
        dgu_ref, dw_ref, db_ref, carry_ref = rest[-4:]
        t = pl.program_id(1)
        i = nt - 1 - t

        @pl.when(t == 0)
        def _():
            carry_ref[...] = jnp.zeros_like(carry_ref)
            dw_ref[...] = jnp.zeros_like(dw_ref)
            db_ref[...] = jnp.zeros_like(db_ref)

        up = gu_ref[1].astype(F32)
        prev = jnp.where(i > 0, prev_ref[...].astype(F32), 0.0)
        ext = jnp.concatenate([prev, gu_ref[0].astype(F32)], axis=0)
        shifted = [pltpu.roll(ext, FFN_CONV - 1 - j, 0) if j < FFN_CONV - 1 else ext for j in range(FFN_CONV)]
        gc = b_ref[...] + shifted[0] * w_ref[0:1, :]
        for j in range(1, FFN_CONV):
            gc = gc + shifted[j] * w_ref[j:j + 1, :]
        gc = gc[GLU_HALO:]
        sg = jax.nn.sigmoid(gc)
        da = _mm_nt(dh_ref[...], _w2d(wdn_ref))
        dup = da * (gc * sg)
        dgc = da * up * (sg * (1.0 + gc * (1.0 - sg)))
        db_ref[...] += jnp.sum(dgc, axis=0, keepdims=True)
        dgc_ext = jnp.concatenate([jnp.zeros((GLU_HALO, GU_SHARD), F32), dgc], axis=0)
        dext = dgc_ext * w_ref[FFN_CONV - 1:FFN_CONV, :]
        for j in range(FFN_CONV):
            dw_ref[j:j + 1, :] += jnp.sum(shifted[j] * dgc_ext, axis=0, keepdims=True)
            if j < FFN_CONV - 1:
                dext = dext + w_ref[j:j + 1, :] * pltpu.roll(dgc_ext, ext_rows - (FFN_CONV - 1 - j), 0)
        tail = jnp.concatenate([jnp.zeros((tm - GLU_HALO, GU_SHARD), F32), carry_ref[...]], axis=0)
        dgate = dext[GLU_HALO:] + tail
        carry_ref[...] = dext[:GLU_HALO]
        dgu_ref[0] = dgate.astype(dgu_ref.dtype)
        dgu_ref[1] = dup.astype(dgu_ref.dtype)

    return pl.pallas_call(
        body, grid=(FF_BLOCKS, nt),
        in_specs=[pl.BlockSpec((2, None, tm, GU_SHARD), lambda j, t: (0, j, nt - 1 - t, 0)),
                  pl.BlockSpec((None, None, GLU_HALO, GU_SHARD),
                               lambda j, t: (0, j, jnp.maximum((nt - 1 - t) * (tm // GLU_HALO) - 1, 0), 0)),
                  pl.BlockSpec((None, HALO, GU_SHARD), lambda j, t: (j, 0, 0)),
                  pl.BlockSpec((None, 1, GU_SHARD), lambda j, t: (j, 0, 0)),
                  pl.BlockSpec((tm, D), lambda j, t: (nt - 1 - t, 0)), _spec_down(0)] + _dep_specs(deps),
        out_specs=[pl.BlockSpec((2, None, tm, GU_SHARD), lambda j, t: (0, j, nt - 1 - t, 0)),
                   pl.BlockSpec((None, HALO, GU_SHARD), lambda j, t: (j, 0, 0)),
                   pl.BlockSpec((None, 1, GU_SHARD), lambda j, t: (j, 0, 0))],
        out_shape=[_SDS(gu.shape, _ACT), _SDS((FF_BLOCKS, HALO, GU_SHARD), F32), _SDS((FF_BLOCKS, 1, GU_SHARD), F32)],
        scratch_shapes=[pltpu.VMEM((GLU_HALO, GU_SHARD), F32)],
        name=name, compiler_params=_cp("arbitrary", "arbitrary"))(gu, gu, conv_w, conv_b, dh, w_down, *deps)


def _bucket_table():
    qi = np.arange(BLK)[:, None]
    kj = np.arange(BLK)[None, :]
    n = np.where(kj > qi, BLK + qi - kj, qi - kj)
    max_exact = N_BUCKETS // 2
    nf = np.maximum(n, 1).astype(np.float32)
    large = max_exact + (np.log(nf / max_exact) / math.log(MAX_DISTANCE / max_exact)
                         * (N_BUCKETS - max_exact)).astype(np.int32)
    large = np.minimum(large, N_BUCKETS - 1)
    return np.where(n < max_exact, n, large).astype(np.int32)


def _lane_low():
    return lax.broadcasted_iota(jnp.int32, (1, 128), 1) < A_HD


def _swa_group(q, kd, vd, sink, bias, upper, first):
    n = A_HEADS // A_KV_HEADS
    low = _lane_low()
    pairs = [q[:, p * 128:(p + 1) * 128] for p in range(n // 2)]
    qm = jnp.concatenate([jnp.where(low == (h % 2 == 0), pairs[h // 2], 0.0) for h in range(n)], axis=0)
    s2 = _mm_nt(qm, kd) * (A_HD ** -0.5)
    s = jnp.where(upper[None], s2[:, :BLK].reshape(n, BLK, BLK), s2[:, BLK:].reshape(n, BLK, BLK)) + bias
    s = jnp.where((upper & first)[None], -jnp.inf, s)
    m = jnp.maximum(jnp.max(s, axis=-1, keepdims=True), sink)
    p = jnp.exp(s - m)
    split = jnp.concatenate([jnp.where(upper[None], p, 0.0), jnp.where(upper[None], 0.0, p)], axis=-1)
    split = split.reshape(n * BLK, 2 * BLK)
    den = _mm(p.reshape(n * BLK, BLK), jnp.ones((BLK, 128), F32)) + jnp.exp(sink - m).reshape(n * BLK, 1)
    o = _mm(split, vd) / den
    return jnp.concatenate([jnp.where(low, o[2 * p * BLK:(2 * p + 1) * BLK], o[(2 * p + 1) * BLK:(2 * p + 2) * BLK])
                            for p in range(n // 2)], axis=1)


def _swa_sinks(sink_ref, g):
    n = A_HEADS // A_KV_HEADS
    return jnp.concatenate([sink_ref[:, h:h + 1] for h in range(g * n, (g + 1) * n)], axis=0).reshape(n, 1, 1)


def _both_halves(t, t_rolled, g):
    low = _lane_low()
    return jnp.where(low, t, t_rolled) if g == 0 else jnp.where(low, t_rolled, t)


def _cross_pairs(q, mk, mv):
    rows = q.shape[0]
    low = _lane_low()
    qm = [jnp.concatenate([jnp.where(low, q[:, p * 128:(p + 1) * 128], 0.0), jnp.where(low, 0.0, q[:, p * 128:(p + 1) * 128])], axis=0)
          for p in range(X_HEADS // 2)]
    s = [_mm_nt(qm[p], mk[:, p * 128:(p + 1) * 128]) * (X_HD ** -0.5) for p in range(X_HEADS // 2)]
    e = [jnp.exp(t - jnp.max(t, axis=-1, keepdims=True)) for t in s]
    pr = [t / jnp.sum(t, axis=-1, keepdims=True) for t in e]
    o = [_mm(pr[p], mv[:, p * 128:(p + 1) * 128]) for p in range(X_HEADS // 2)]
    return jnp.concatenate([jnp.where(low, t[:rows], t[rows:]) for t in o], axis=1)


def _swa_upper():
    qi = lax.broadcasted_iota(jnp.int32, (BLK, BLK), 0)
    kj = lax.broadcasted_iota(jnp.int32, (BLK, BLK), 1)
    return kj > qi


def _bias_build(rel_bias, bucket, name):
    def body(rb_ref, bucket_ref, o_ref):
        b = bucket_ref[...]
        for h in range(A_HEADS):
            acc = jnp.zeros((BLK, BLK), F32)
            for k in range(N_BUCKETS):
                acc = jnp.where(b == k, rb_ref[k, h], acc)
            o_ref[h] = acc

    return pl.pallas_call(
        body, in_specs=[pl.BlockSpec(memory_space=pltpu.SMEM), pl.BlockSpec(memory_space=pltpu.VMEM)],
        out_specs=pl.BlockSpec(memory_space=pltpu.VMEM),
        out_shape=_SDS((A_HEADS, BLK, BLK), F32), name=name)(rel_bias, bucket)


def _bias_reduce(dbias, bucket, name):
    def body(db_ref, bucket_ref, o_ref):
        b = bucket_ref[...]
        row = lax.broadcasted_iota(jnp.int32, (N_BUCKETS, 128), 0)
        lane = lax.broadcasted_iota(jnp.int32, (N_BUCKETS, 128), 1)
        acc = jnp.zeros((N_BUCKETS, 128), F32)
        for h in range(A_HEADS):
            v = db_ref[h]
            for k in range(N_BUCKETS):
                sk = jnp.sum(jnp.sum(jnp.where(b == k, v, 0.0), axis=1, keepdims=True), axis=0, keepdims=True)
                acc = acc + jnp.where((row == k) & (lane == h), sk, 0.0)
        o_ref[...] = acc

    return pl.pallas_call(
        body, in_specs=[pl.BlockSpec(memory_space=pltpu.VMEM)] * 2,
        out_specs=pl.BlockSpec(memory_space=pltpu.VMEM),
        out_shape=_SDS((N_BUCKETS, 128), F32), name=name)(dbias, bucket)


def _mix_a_fwd(proj, bias, sinks, memkv, name):
    s = proj.shape[0]
    nb = s // BLK
    grp = A_HEADS // A_KV_HEADS

    def body(proj_ref, prev_ref, bias_ref, sink_ref, memkv_ref, o_ref):
        i = pl.program_id(0)
        upper = _swa_upper()
        prev = prev_ref[...].astype(F32)
        proj = proj_ref[...].astype(F32)
        kb = jnp.concatenate([prev[:, :A_KV], proj[:, A_Q:A_Q + A_KV]], axis=0)
        vb = jnp.concatenate([prev[:, A_KV:], proj[:, A_Q + A_KV:A_Q + 2 * A_KV]], axis=0)
        kb_r = pltpu.roll(kb, A_HD, 1)
        vb_r = pltpu.roll(vb, A_HD, 1)
        gw = A_Q // A_KV_HEADS
        outs = [_swa_group(proj[:, g * gw:(g + 1) * gw], _both_halves(kb, kb_r, g), _both_halves(vb, vb_r, g),
                           _swa_sinks(sink_ref, g), bias_ref[g * grp:(g + 1) * grp], upper, i == 0) for g in range(A_KV_HEADS)]
        outs.append(_cross_pairs(proj[:, A_Q + 2 * A_KV:], memkv_ref[:, :X_Q], memkv_ref[:, X_Q:]))
        o_ref[...] = jnp.concatenate(outs, axis=1).astype(o_ref.dtype)

    return pl.pallas_call(
        body, grid=(nb,),
        in_specs=[pl.BlockSpec((BLK, IN_A), lambda i: (i, 0)),
                  pl.BlockSpec((BLK, 2 * A_KV), lambda i: (jnp.maximum(i - 1, 0), A_Q // (2 * A_KV))),
                  pl.BlockSpec((A_HEADS, BLK, BLK), lambda i: (0, 0, 0)),
                  pl.BlockSpec((1, 128), lambda i: (0, 0)),
                  pl.BlockSpec((MEM_LEN, 2 * X_Q), lambda i: (0, 0))],
        out_specs=pl.BlockSpec((BLK, D), lambda i: (i, 0)),
        out_shape=_SDS((s, D), _ACT), name=name, compiler_params=_cp("arbitrary"))(proj, proj, bias, sinks, memkv)


def _mix_a_bwd(proj, bias, sinks, memkv, dmix, name):
    s = proj.shape[0]
    nb = s // BLK
    grp = A_HEADS // A_KV_HEADS

    def body(proj_ref, prev_ref, bias_ref, sink_ref, memkv_ref, dmix_ref,
             dproj_ref, dbias_ref, dsink_ref, dmemkv_ref, carry_ref):
        t = pl.program_id(0)
        i = nb - 1 - t

        @pl.when(t == 0)
        def _():
            carry_ref[...] = jnp.zeros_like(carry_ref)
            dbias_ref[...] = jnp.zeros_like(dbias_ref)
            dsink_ref[...] = jnp.zeros_like(dsink_ref)
            dmemkv_ref[...] = jnp.zeros_like(dmemkv_ref)

        upper = _swa_upper()
        lane = lax.broadcasted_iota(jnp.int32, (1, 128), 1)
        low = _lane_low()
        prev = prev_ref[...].astype(F32)
        proj = proj_ref[...].astype(F32)
        kb = jnp.concatenate([prev[:, :A_KV], proj[:, A_Q:A_Q + A_KV]], axis=0)
        vb = jnp.concatenate([prev[:, A_KV:], proj[:, A_Q + A_KV:A_Q + 2 * A_KV]], axis=0)
        kb_r = pltpu.roll(kb, A_HD, 1)
        vb_r = pltpu.roll(vb, A_HD, 1)
        gw = A_Q // A_KV_HEADS
        dqs, dkd, dvd = [], [], []
        dsink = jnp.zeros((1, 128), F32)
        for g in range(A_KV_HEADS):
            _, vjp = jax.vjp(functools.partial(_swa_group, upper=upper, first=i == 0), proj[:, g * gw:(g + 1) * gw],
                             _both_halves(kb, kb_r, g), _both_halves(vb, vb_r, g), _swa_sinks(sink_ref, g),
                             bias_ref[g * grp:(g + 1) * grp])
            dq, dk, dv, ds, db = vjp(dmix_ref[:, g * gw:(g + 1) * gw].astype(F32))
            dqs.append(dq)
            dkd.append(dk + pltpu.roll(dk, A_HD, 1))
            dvd.append(dv + pltpu.roll(dv, A_HD, 1))
            for h in range(grp):
                dsink = dsink + jnp.where(lane == g * grp + h, ds[h], 0.0)
            dbias_ref[g * grp:(g + 1) * grp] += db
        dsink_ref[...] += dsink
        dkb = jnp.where(low, dkd[0], dkd[1])
        dvb = jnp.where(low, dvd[0], dvd[1])
        _, vjp = jax.vjp(_cross_pairs, proj[:, A_Q + 2 * A_KV:], memkv_ref[:, :X_Q], memkv_ref[:, X_Q:])
        dxq, dmk, dmv = vjp(dmix_ref[:, A_Q:].astype(F32))
        dmemkv_ref[...] += jnp.concatenate([dmk, dmv], axis=1)
        dkv_cur = jnp.concatenate([dkb[BLK:], dvb[BLK:]], axis=1) + carry_ref[...]
        carry_ref[...] = jnp.concatenate([dkb[:BLK], dvb[:BLK]], axis=1)
        dproj_ref[...] = jnp.concatenate(dqs + [dkv_cur, dxq], axis=1).astype(dproj_ref.dtype)

    return pl.pallas_call(
        body, grid=(nb,),
        in_specs=[pl.BlockSpec((BLK, IN_A), lambda t: (nb - 1 - t, 0)),
                  pl.BlockSpec((BLK, 2 * A_KV), lambda t: (jnp.maximum(nb - 2 - t, 0), A_Q // (2 * A_KV))),
                  pl.BlockSpec((A_HEADS, BLK, BLK), lambda t: (0, 0, 0)),
                  pl.BlockSpec((1, 128), lambda t: (0, 0)),
                  pl.BlockSpec((MEM_LEN, 2 * X_Q), lambda t: (0, 0)),
                  pl.BlockSpec((BLK, D), lambda t: (nb - 1 - t, 0))],
        out_specs=[pl.BlockSpec((BLK, IN_A), lambda t: (nb - 1 - t, 0)),
                   pl.BlockSpec((A_HEADS, BLK, BLK), lambda t: (0, 0, 0)),
                   pl.BlockSpec((1, 128), lambda t: (0, 0)),
                   pl.BlockSpec((MEM_LEN, 2 * X_Q), lambda t: (0, 0))],
        out_shape=[_SDS((s, IN_A), _ACT), _SDS((A_HEADS, BLK, BLK), F32), _SDS((1, 128), F32),
                   _SDS((MEM_LEN, 2 * X_Q), F32)],
        scratch_shapes=[pltpu.VMEM((BLK, 2 * A_KV), F32)],
        name=name, compiler_params=_cp("arbitrary"))(proj, proj, bias, sinks, memkv, dmix)


def _dn_heads(yq, yk, yv, z, bl, al, a_log, dtb, ng, s0):
    c = CHUNK
    nh = B_V_HEADS
    rep = B_V_HEADS // B_QK_HEADS
    r = lax.broadcasted_iota(jnp.int32, (c, c), 0)
    cc = lax.broadcasted_iota(jnp.int32, (c, c), 1)
    q = [_silu(t) for t in yq]
    k = [_silu(t) for t in yk]
    v = [_silu(t) for t in yv]
    q = [t * lax.rsqrt(jnp.sum(t * t, axis=-1, keepdims=True) + EPS) * (B_HD ** -0.5) for t in q]
    k = [t * lax.rsqrt(jnp.sum(t * t, axis=-1, keepdims=True) + EPS) for t in k]
    beta = [jax.nn.sigmoid(t) for t in bl]
    g = [-jnp.exp(a_log[h]) * jax.nn.softplus(al[h] + dtb[h]) for h in range(nh)]
    gb = [jnp.broadcast_to(t, (c, c)) for t in g]
    gc_col = [jnp.sum(jnp.where(cc <= r, t.T, 0.0), axis=1, keepdims=True) for t in gb]
    gc_row = [jnp.sum(jnp.where(r <= cc, t, 0.0), axis=0, keepdims=True) for t in gb]
    gc_last = [jnp.sum(t, axis=0, keepdims=True) for t in g]
    decay = [jnp.exp(jnp.where(r >= cc, gc_col[h] - gc_row[h], -jnp.inf)) for h in range(nh)]
    kq = [_mmf_nt(jnp.concatenate([k[h], q[h]], axis=0), k[h]) for h in range(B_QK_HEADS)]
    kk = [t[:c] for t in kq]
    qk = [t[c:] for t in kq]
    egc = [jnp.exp(t) for t in gc_col]
    both = [_mmf(jnp.concatenate([(beta[h] * egc[h]) * k[h // rep], q[h // rep] * egc[h]], axis=0), s0[h]) for h in range(nh)]
    rhs = [beta[h] * v[h] - both[h][:c] for h in range(nh)]
    qs0 = [t[c:] for t in both]
    pw = [-(beta[h] * kk[h // rep] * jnp.where(r > cc, decay[h], 0.0)) for h in range(nh)]
    x = rhs
    for lvl in range(6):
        if lvl < 5:
            prod = [_mmf(pw[h], jnp.concatenate([x[h], pw[h]], axis=1)) for h in range(nh)]
            x = [x[h] + prod[h][:, :B_HD] for h in range(nh)]
            pw = [t[:, B_HD:] for t in prod]
        else:
            x = [x[h] + _mmf(pw[h], x[h]) for h in range(nh)]
    delta = x
    last = [_mmf(jnp.concatenate([qk[h // rep] * decay[h], (k[h // rep] * jnp.exp(gc_last[h] - gc_col[h])).T], axis=0), delta[h])
            for h in range(nh)]
    out = [qs0[h] + last[h][:c] for h in range(nh)]
    s1 = [jnp.exp(gc_last[h]) * s0[h] + last[h][c:] for h in range(nh)]
    o = [t * lax.rsqrt(jnp.mean(t * t, axis=-1, keepdims=True) + EPS) * ng for t in out]
    return [o[h] * _silu(z[h]) for h in range(nh)], s1


def _dn_conv(ext, w_ref):
    y = ext * w_ref[B_CONV - 1:B_CONV, :]
    for j in range(B_CONV - 1):
        y = y + w_ref[j:j + 1, :] * pltpu.roll(ext, B_CONV - 1 - j, 0)
    return y


def _dn_args(y, cur_ref, par_ref, ng_ref):
    nh = B_V_HEADS
    return ([y[:, h * B_HD:(h + 1) * B_HD] for h in range(B_QK_HEADS)],
            [y[:, B_QK + h * B_HD:B_QK + (h + 1) * B_HD] for h in range(B_QK_HEADS)],
            [y[:, 2 * B_QK + h * B_HD:2 * B_QK + (h + 1) * B_HD] for h in range(nh)],
            [cur_ref[:, BP_Z + h * B_HD:BP_Z + (h + 1) * B_HD] for h in range(nh)],
            [cur_ref[:, BP_GATE + h:BP_GATE + h + 1] for h in range(nh)],
            [cur_ref[:, BP_GATE + nh + h:BP_GATE + nh + h + 1] for h in range(nh)],
            [par_ref[:, h:h + 1] for h in range(nh)], [par_ref[:, nh + h:nh + h + 1] for h in range(nh)], ng_ref[...])


def _mix_b_fwd(proj, conv_w, par, ng, memkv, name):
    s = proj.shape[0]
    nc = s // CHUNK

    def body(cur_ref, prev_ref, w_ref, par_ref, ng_ref, memkv_ref, o_ref, st_ref, state_ref):
        n = pl.program_id(0)

        @pl.when(n == 0)
        def _():
            state_ref[...] = jnp.zeros_like(state_ref)

        prev = jnp.where(n > 0, prev_ref[...], 0.0)
        ext = jnp.concatenate([prev, cur_ref[:, :B_QKV]], axis=0)
        y = _dn_conv(ext, w_ref)[HALO:]
        s0 = [state_ref[hv] for hv in range(B_V_HEADS)]
        st_ref[0] = state_ref[...]
        outs, s1 = _dn_heads(*_dn_args(y, cur_ref, par_ref, ng_ref), s0)
        for hv in range(B_V_HEADS):
            state_ref[hv] = s1[hv]
        outs = outs + [_cross_pairs(cur_ref[:, BP_XQ:BP_XQ + X_Q], memkv_ref[:, :X_Q], memkv_ref[:, X_Q:])]
        o_ref[...] = jnp.concatenate(outs, axis=1).astype(o_ref.dtype)

    return pl.pallas_call(
        body, grid=(nc,),
        in_specs=[pl.BlockSpec((CHUNK, IN_BP), lambda n: (n, 0)),
                  pl.BlockSpec((HALO, B_QKV), lambda n: (jnp.maximum(n * (CHUNK // HALO) - 1, 0), 0)),
                  pl.BlockSpec((HALO, B_QKV), lambda n: (0, 0)),
                  pl.BlockSpec((1, 128), lambda n: (0, 0)), pl.BlockSpec((1, 128), lambda n: (0, 0)),
                  pl.BlockSpec((MEM_LEN, 2 * X_Q), lambda n: (0, 0))],
        out_specs=[pl.BlockSpec((CHUNK, D), lambda n: (n, 0)),
                   pl.BlockSpec((1, B_V_HEADS, B_HD, B_HD), lambda n: (n, 0, 0, 0))],
        out_shape=[_SDS((s, D), _ACT), _SDS((nc, B_V_HEADS, B_HD, B_HD), F32)],
        scratch_shapes=[pltpu.VMEM((B_V_HEADS, B_HD, B_HD), F32)],
        name=name, compiler_params=_cp("arbitrary"))(proj, proj, conv_w, par, ng, memkv)


def _mix_b_bwd(proj, conv_w, par, ng, memkv, states, dmix, name):
    s = proj.shape[0]
    nc = s // CHUNK
    ext_rows = CHUNK + HALO

    def body(cur_ref, prev_ref, w_ref, par_ref, ng_ref, memkv_ref, st_ref, dmix_ref,
             dproj_ref, dw_ref, dpar_ref, dng_ref, dmemkv_ref, dstate_ref, carry_ref):
        t = pl.program_id(0)
        n = nc - 1 - t

        @pl.when(t == 0)
        def _():
            dstate_ref[...] = jnp.zeros_like(dstate_ref)
            carry_ref[...] = jnp.zeros_like(carry_ref)
            dw_ref[...] = jnp.zeros_like(dw_ref)
            dpar_ref[...] = jnp.zeros_like(dpar_ref)
            dng_ref[...] = jnp.zeros_like(dng_ref)
            dmemkv_ref[...] = jnp.zeros_like(dmemkv_ref)

        lane = lax.broadcasted_iota(jnp.int32, (1, 128), 1)
        prev = jnp.where(n > 0, prev_ref[...], 0.0)
        ext = jnp.concatenate([prev, cur_ref[:, :B_QKV]], axis=0)
        y = _dn_conv(ext, w_ref)[HALO:]
        _, vjp = jax.vjp(_dn_heads, *_dn_args(y, cur_ref, par_ref, ng_ref), [st_ref[0, hv] for hv in range(B_V_HEADS)])
        dyq, dyk, dyv, dz, gbl, gal, ga_log, gdtb, dng, gs0 = vjp(
            ([dmix_ref[:, hv * B_HD:(hv + 1) * B_HD].astype(F32) for hv in range(B_V_HEADS)],
             [dstate_ref[hv] for hv in range(B_V_HEADS)]))
        dgate = jnp.zeros((CHUNK, 128), F32)
        dpar = jnp.zeros((1, 128), F32)
        for hv in range(B_V_HEADS):
            dstate_ref[hv] = gs0[hv]
            dgate = dgate + jnp.where(lane == hv, gbl[hv], 0.0) + jnp.where(lane == B_V_HEADS + hv, gal[hv], 0.0)
            dpar = dpar + jnp.where(lane == hv, ga_log[hv], 0.0) + jnp.where(lane == B_V_HEADS + hv, gdtb[hv], 0.0)
        dpar_ref[...] += dpar
        dng_ref[...] += dng
        _, vjp = jax.vjp(_cross_pairs, cur_ref[:, BP_XQ:BP_XQ + X_Q], memkv_ref[:, :X_Q], memkv_ref[:, X_Q:])
        dxq, dmk, dmv = vjp(dmix_ref[:, B_V:].astype(F32))
        dmemkv_ref[...] += jnp.concatenate([dmk, dmv], axis=1)
        dy = jnp.concatenate(list(dyq) + list(dyk) + list(dyv), axis=1)
        dy_ext = jnp.concatenate([jnp.zeros((HALO, B_QKV), F32), dy], axis=0)
        dext = dy_ext * w_ref[B_CONV - 1:B_CONV, :]
        dw_ref[B_CONV - 1:B_CONV, :] += jnp.sum(ext * dy_ext, axis=0, keepdims=True)
        for j in range(B_CONV - 1):
            sh = B_CONV - 1 - j
            dw_ref[j:j + 1, :] += jnp.sum(pltpu.roll(ext, sh, 0) * dy_ext, axis=0, keepdims=True)
            dext = dext + w_ref[j:j + 1, :] * pltpu.roll(dy_ext, ext_rows - sh, 0)
        tail = jnp.concatenate([jnp.zeros((CHUNK - HALO, B_QKV), F32), carry_ref[...]], axis=0)
        dqkv = dext[HALO:] + tail
        carry_ref[...] = dext[:HALO]
        dproj_ref[...] = jnp.concatenate([dqkv] + list(dz) + [dxq, dgate], axis=1).astype(dproj_ref.dtype)

    return pl.pallas_call(
        body, grid=(nc,),
        in_specs=[pl.BlockSpec((CHUNK, IN_BP), lambda t: (nc - 1 - t, 0)),
                  pl.BlockSpec((HALO, B_QKV), lambda t: (jnp.maximum((nc - 1 - t) * (CHUNK // HALO) - 1, 0), 0)),
                  pl.BlockSpec((HALO, B_QKV), lambda t: (0, 0)),
                  pl.BlockSpec((1, 128), lambda t: (0, 0)), pl.BlockSpec((1, 128), lambda t: (0, 0)),
                  pl.BlockSpec((MEM_LEN, 2 * X_Q), lambda t: (0, 0)),
                  pl.BlockSpec((1, B_V_HEADS, B_HD, B_HD), lambda t: (nc - 1 - t, 0, 0, 0)),
                  pl.BlockSpec((CHUNK, D), lambda t: (nc - 1 - t, 0))],
        out_specs=[pl.BlockSpec((CHUNK, IN_BP), lambda t: (nc - 1 - t, 0)),
                   pl.BlockSpec((HALO, B_QKV), lambda t: (0, 0)),
                   pl.BlockSpec((1, 128), lambda t: (0, 0)), pl.BlockSpec((1, 128), lambda t: (0, 0)),
                   pl.BlockSpec((MEM_LEN, 2 * X_Q), lambda t: (0, 0))],
        out_shape=[_SDS((s, IN_BP), _ACT), _SDS((HALO, B_QKV), F32), _SDS((1, 128), F32), _SDS((1, 128), F32),
                   _SDS((MEM_LEN, 2 * X_Q), F32)],
        scratch_shapes=[pltpu.VMEM((B_V_HEADS, B_HD, B_HD), F32), pltpu.VMEM((HALO, B_QKV), F32)],
        name=name, compiler_params=_cp("arbitrary"))(proj, proj, conv_w, par, ng, memkv, states, dmix)


def _place():
    return lax.axis_index("x"), lax.axis_index("y"), lax.axis_index("c")


def _all_gather(shards, name):
    n = len(shards)

    def body(*refs):
        ins, outs = refs[:n], refs[n:2 * n]
        send_sems, recv_sems, local_sems = refs[2 * n:]
        x, y, c = _place()
        me, sibling = (x, y, c), (x, y, 1 - c)
        chips = [(1 - x, y), (x, 1 - y), (1 - x, 1 - y)]

        def rows(a, px, py, pc):
            return outs[a].at[4 * px + 2 * py + pc]

        def copy(a, k, block, to, src=None):
            return pltpu.make_async_remote_copy(
                src_ref=rows(a, *block) if src is None else src, dst_ref=rows(a, *block),
                send_sem=send_sems.at[a, k], recv_sem=recv_sems.at[a, k],
                device_id=to, device_id_type=pl.DeviceIdType.MESH)

        mine = [pltpu.make_async_copy(ins[a], rows(a, *me), local_sems.at[a]) for a in range(n)]
        for cp in mine:
            cp.start()
        first = []
        for a in range(n):
            first.append(copy(a, 0, me, sibling, src=ins[a]))
            first += [copy(a, 1 + j, me, (*chip, c), src=ins[a]) for j, chip in enumerate(chips)]
        for cp in first:
            cp.start()
        passed = []
        for j, chip in enumerate(chips):
            for a in range(n):
                copy(a, 1 + j, (*chip, c), me).wait_recv()
                fwd = copy(a, 4 + j, (*chip, c), sibling)
                fwd.start()
                passed.append(fwd)
        for a in range(n):
            copy(a, 0, sibling, me).wait_recv()
            for j, chip in enumerate(chips):
                copy(a, 4 + j, (*chip, 1 - c), me).wait_recv()
        for cp in first + passed:
            cp.wait_send()
        for cp in mine:
            cp.wait()

    hbm = pl.BlockSpec(memory_space=pl.ANY)
    return pl.pallas_call(
        body, out_shape=[_SDS((N_DEV,) + s.shape, s.dtype) for s in shards],
        in_specs=[hbm] * n, out_specs=[hbm] * n,
        scratch_shapes=[pltpu.SemaphoreType.DMA((n, 7)), pltpu.SemaphoreType.DMA((n, 7)), pltpu.SemaphoreType.DMA((n,))],
        name=name)(*shards)


class _Sends:
    def __init__(self, plan, sems, srcs, lands, token):
        self.plan, self.sems, self.srcs, self.lands, self.token = plan, sems, srcs, lands, token


def _send_refs(plan, src_refs, land_refs, x, y, c):
    my = 4 * x + 2 * y + c

    def src_for(a, dest):
        return src_refs[a].at[dest] if plan[a][1] else src_refs[a]

    def slot(a, source):
        return land_refs[plan[a][0]].at[source]

    return my, src_for, slot


def _send_start(srcs, land_shapes, plan, name):
    n, nl = len(srcs), len(land_shapes)

    def body(*refs):
        src_refs, land_refs = refs[:n], refs[n:n + nl]
        send_sems, recv_sems = refs[n + nl], refs[n + nl + 1]
        token, local_sems = refs[2 * (n + nl) + 2], refs[2 * (n + nl) + 3]
        x, y, c = _place()
        my, src_for, slot = _send_refs(plan, src_refs, land_refs, x, y, c)
        mine = [pltpu.make_async_copy(src_for(a, my), slot(a, my), local_sems.at[a]) for a in range(n)]
        for cp in mine:
            cp.start()
        for k in range(N_DEV - 1):
            px, py, pc = x ^ ((k + 1) >> 2 & 1), y ^ ((k + 1) >> 1 & 1), c ^ ((k + 1) & 1)
            for a in range(n):
                pltpu.make_async_remote_copy(
                    src_ref=src_for(a, 4 * px + 2 * py + pc), dst_ref=slot(a, my), send_sem=send_sems.at[a * (N_DEV - 1) + k],
                    recv_sem=recv_sems.at[a * (N_DEV - 1) + k], device_id=(px, py, pc), device_id_type=pl.DeviceIdType.MESH).start()
        for cp in mine:
            cp.wait()
        token[...] = jnp.zeros_like(token)

    hbm = pl.BlockSpec(memory_space=pltpu.HBM)
    sem = pl.BlockSpec(memory_space=pltpu.SEMAPHORE)
    lands = [pltpu.with_memory_space_constraint(lax.empty(s, d), pltpu.HBM) for s, d in land_shapes]
    srcs = [pltpu.with_memory_space_constraint(s, pltpu.HBM) for s in srcs]
    out = pl.pallas_call(
        body, name=name,
        out_shape=(pltpu.SemaphoreType.DMA((n * (N_DEV - 1),)), pltpu.SemaphoreType.DMA((n * (N_DEV - 1),)),
                   *[pltpu.HBM(s.shape, s.dtype) for s in srcs], *[pltpu.HBM(s, d) for s, d in land_shapes],
                   _SDS((8, 128), F32)),
        in_specs=[hbm] * (n + nl),
        out_specs=(sem, sem, *[hbm] * (n + nl), pl.BlockSpec(memory_space=pltpu.VMEM)),
        input_output_aliases={i: 2 + i for i in range(n + nl)},
        scratch_shapes=[pltpu.SemaphoreType.DMA((n,))],
        compiler_params=pltpu.CompilerParams(has_side_effects=pltpu.SideEffectType.DATAFLOW_SIDE_EFFECTING),
    )(*srcs, *lands)
    return _Sends(plan, out[:2], out[2:2 + n], out[2 + n:2 + n + nl], out[-1])


def _send_wait(sends, after, name):
    after = list(after) if isinstance(after, (list, tuple)) else [after]
    plan = sends.plan
    n, nl = len(sends.srcs), len(sends.lands)

    def body(*refs):
        src_refs, land_refs = refs[:n], refs[n:n + nl]
        send_sems, recv_sems = refs[n + nl], refs[n + nl + 1]
        x, y, c = _place()
        my, src_for, slot = _send_refs(plan, src_refs, land_refs, x, y, c)
        for k in range(N_DEV - 1):
            px, py, pc = x ^ ((k + 1) >> 2 & 1), y ^ ((k + 1) >> 1 & 1), c ^ ((k + 1) & 1)
            peer = 4 * px + 2 * py + pc
            for a in range(n):
                cp = pltpu.make_async_remote_copy(
                    src_ref=src_for(a, peer), dst_ref=slot(a, peer), send_sem=send_sems.at[a * (N_DEV - 1) + k],
                    recv_sem=recv_sems.at[a * (N_DEV - 1) + k], device_id=(px, py, pc), device_id_type=pl.DeviceIdType.MESH)
                cp.wait_send()
                cp.wait_recv()

    hbm = pl.BlockSpec(memory_space=pltpu.HBM)
    sem = pl.BlockSpec(memory_space=pltpu.SEMAPHORE)
    out = pl.pallas_call(
        body, name=name,
        out_shape=tuple(pltpu.HBM(s.shape, s.dtype) for s in (*sends.srcs, *sends.lands)),
        in_specs=[hbm] * (n + nl) + [sem, sem] + [pl.BlockSpec(memory_space=pl.ANY)] * len(after),
        out_specs=tuple([hbm] * (n + nl)),
        input_output_aliases={i: i for i in range(n + nl)},
        compiler_params=pltpu.CompilerParams(has_side_effects=pltpu.SideEffectType.DATAFLOW_SIDE_EFFECTING),
    )(*sends.srcs, *sends.lands, *sends.sems, *after)
    return list(out[n:])


def _adam_update(g, w, m, v):
    c1 = 1.0 - ADAM_B1 ** ADAM_STEP
    c2 = 1.0 - ADAM_B2 ** ADAM_STEP
    mm = ADAM_B1 * m + (1.0 - ADAM_B1) * g
    vv = ADAM_B2 * v + (1.0 - ADAM_B2) * (g * g)
    delta = -ADAM_LR * ((mm / c1) / (jnp.sqrt(vv / c2) + ADAM_EPS) + ADAM_WD * w)
    return delta, mm, vv


def _sum_sources(p_ref):
    g = p_ref[0].astype(F32)
    for s in range(1, N_DEV):
        g = g + p_ref[s].astype(F32)
    return g


def _adamw(parts, w, m, v, tr, name, restore_b=False):
    nl, r, c = w.shape
    cp = parts[0].shape[-1]

    def body(*refs):
        p_refs = refs[:nl]
        w_ref, m_ref, v_ref, g_ref, d_ref, nm_ref, nv_ref = refs[nl:]
        g = _sum_sources(p_refs[0])
        for l in range(1, nl):
            g = jnp.where(pl.program_id(0) == l, _sum_sources(p_refs[l]), g)
        if restore_b:
            g = jnp.concatenate([g[:, :BP_XQ], g[:, BP_GATE:BP_GATE + 2 * B_V_HEADS], g[:, BP_XQ:BP_GATE]], axis=1)
        delta, mm, vv = _adam_update(g, w_ref[...], m_ref[...], v_ref[...])
        g_ref[...] = g
        d_ref[...] = delta
        nm_ref[...] = mm
        nv_ref[...] = vv

    spec = pl.BlockSpec((None, tr, c), lambda l, i: (l, i, 0))
    part_specs = [pl.BlockSpec((N_DEV, tr, cp), functools.partial(lambda l, i, k: (0, jnp.where(l == k, i, 0), 0), k=k))
                  for k in range(nl)]
    return pl.pallas_call(
        body, grid=(nl, r // tr),
        in_specs=part_specs + [spec, spec, spec],
        out_specs=[spec] * 4, out_shape=[_SDS(w.shape, F32)] * 4,
        name=name, compiler_params=_cp("arbitrary", "arbitrary"))(*parts, w, m, v)


def _pack_small(d_rel, d_cb, d_cw, d_qkv, d_mix, d_mem, d_ffn, d_final, d_sinks, d_par, d_ng, name):
    flat = [d_rel, *d_cb, *d_cw, d_qkv, *d_mix, *d_mem, *d_ffn, d_final, d_sinks, d_par, d_ng]
    n = len(flat)

    def body(*refs):
        ins, o_ref = refs[:n], refs[n]
        rel, cb0, cb1, cw0, cw1, qkv, mx0, mx1, me0, me1, ff0, ff1, fin, snk, par, ng = ins
        o_ref[...] = jnp.zeros_like(o_ref)
        o_ref[SP_REL:SP_REL + N_BUCKETS, 0:128] = rel[...]
        for l, (cb, cw) in enumerate(((cb0, cw0), (cb1, cw1))):
            o_ref[SP_CB + l:SP_CB + l + 1, :] = jnp.concatenate([cb[j] for j in range(FF_BLOCKS)], axis=1)
            full = jnp.concatenate([cw[j] for j in range(FF_BLOCKS)], axis=1)
            o_ref[SP_CW + FFN_CONV * l:SP_CW + FFN_CONV * (l + 1), :] = full[:FFN_CONV]
        o_ref[SP_QKV:SP_QKV + B_CONV, 0:B_QKV] = qkv[0:B_CONV, :]
        for base, pair in ((SP_MIX, (mx0, mx1)), (SP_MEM, (me0, me1)), (SP_FFN, (ff0, ff1))):
            for l in range(2):
                o_ref[base + l:base + l + 1, 0:D] = pair[l][...]
        o_ref[SP_FINAL:SP_FINAL + 1, 0:D] = fin[...]
        o_ref[SP_MISC:SP_MISC + 1, 0:128] = snk[...]
        o_ref[SP_MISC:SP_MISC + 1, 128:256] = par[...]
        o_ref[SP_MISC:SP_MISC + 1, 256:384] = ng[...]

    vm = pl.BlockSpec(memory_space=pltpu.VMEM)
    return pl.pallas_call(body, in_specs=[vm] * n, out_specs=vm, out_shape=_SDS((SMALL_ROWS, D_FF), F32), name=name)(*flat)


_SMALL = ["rel_bias", "norm_mix_g", "norm_mem_g", "sinks_a", "a_log_b", "dt_bias_b", "out_norm_g_b", "norm_ffn_g",
          "ffn_conv_b", "final_norm_g", "conv_qkv_b", "ffn_conv_w"]


def _adamw_small(recv, rc_qkv, rc_ffn, ws, ms, vs, name):
    n = len(_SMALL)

    def body(*refs):
        recv_ref, qkv_ref, ffn_ref = refs[:3]
        w_refs, m_refs, v_refs = refs[3:3 + n], refs[3 + n:3 + 2 * n], refs[3 + 2 * n:3 + 3 * n]
        outs = refs[3 + 3 * n:]
        gs = _sum_sources(recv_ref)
        grads = {
            "rel_bias": gs[SP_REL:SP_REL + N_BUCKETS, 0:A_HEADS],
            "norm_mix_g": gs[SP_MIX:SP_MIX + 2, 0:D], "norm_mem_g": gs[SP_MEM:SP_MEM + 2, 0:D],
            "sinks_a": gs[SP_MISC:SP_MISC + 1, 0:A_HEADS],
            "a_log_b": gs[SP_MISC:SP_MISC + 1, 128:128 + B_V_HEADS],
            "dt_bias_b": gs[SP_MISC:SP_MISC + 1, 128 + B_V_HEADS:128 + 2 * B_V_HEADS],
            "out_norm_g_b": gs[SP_MISC:SP_MISC + 1, 256:256 + B_HD],
            "norm_ffn_g": gs[SP_FFN:SP_FFN + 2, 0:D], "ffn_conv_b": gs[SP_CB:SP_CB + 2, :],
            "final_norm_g": gs[SP_FINAL:SP_FINAL + 1, 0:D],
            "conv_qkv_b": _sum_sources(qkv_ref), "ffn_conv_w": _sum_sources(ffn_ref),
        }
        for i, nm in enumerate(_SMALL):
            g = grads[nm]
            delta, mm, vv = _adam_update(g, w_refs[i][...], m_refs[i][...], v_refs[i][...])
            outs[i][...] = g
            outs[n + i][...] = delta
            outs[2 * n + i][...] = mm
            outs[3 * n + i][...] = vv

    vm = pl.BlockSpec(memory_space=pltpu.VMEM)
    shapes = [_SDS(w.shape, F32) for w in ws]
    return pl.pallas_call(
        body, in_specs=[vm] * (3 + 3 * n), out_specs=[vm] * (4 * n), out_shape=shapes * 4,
        name=name)(recv, rc_qkv, rc_ffn, *ws, *ms, *vs)


def _assemble(gathered, axis):
    g = jnp.moveaxis(gathered, 0, axis)
    shp = list(g.shape)
    return g.reshape(shp[:axis] + [shp[axis] * shp[axis + 1]] + shp[axis + 2:])


def _pad_rows(a, rows):
    return jnp.pad(a, ((0, rows - a.shape[0]), (0, 0)))


def _pad_lanes(a, lanes=128):
    return jnp.pad(a, ((0, 0), (0, lanes - a.shape[1])))


def _ff_blocks(a):
    return jnp.moveaxis(a.reshape(a.shape[0], FF_BLOCKS, GU_SHARD), 1, 0)


def _reorder_b(w):
    qkv_z = w[..., :B_QKV + B_V]
    gates = w[..., B_QKV + B_V:B_QKV + B_V + 2 * B_V_HEADS]
    xq = w[..., IN_B - X_Q:]
    pad = jnp.zeros(w.shape[:-1] + (IN_BP - IN_B,), w.dtype)
    return jnp.concatenate([qkv_z, xq, gates, pad], axis=-1)


def kernel(x, mem, rel_bias, norm_mix_g, norm_mem_g, w_mem_kv, w_out, w_in_a, sinks_a, w_in_b, conv_qkv_b, a_log_b, dt_bias_b, out_norm_g_b, norm_ffn_g, w_gate_up, ffn_conv_w, ffn_conv_b, w_down, final_norm_g, loss_target, m_rel_bias, m_norm_mix_g, m_norm_mem_g, m_w_mem_kv, m_w_out, m_w_in_a, m_sinks_a, m_w_in_b, m_conv_qkv_b, m_a_log_b, m_dt_bias_b, m_out_norm_g_b, m_norm_ffn_g, m_w_gate_up, m_ffn_conv_w, m_ffn_conv_b, m_w_down, m_final_norm_g, v_rel_bias, v_norm_mix_g, v_norm_mem_g, v_w_mem_kv, v_w_out, v_w_in_a, v_sinks_a, v_w_in_b, v_conv_qkv_b, v_a_log_b, v_dt_bias_b, v_out_norm_g_b, v_norm_ffn_g, v_w_gate_up, v_ffn_conv_w, v_ffn_conv_b, v_w_down, v_final_norm_g):
    local = dict(locals())
    order = ["rel_bias", "norm_mix_g", "norm_mem_g", "w_mem_kv", "w_out", "w_in_a", "sinks_a", "w_in_b", "conv_qkv_b",
             "a_log_b", "dt_bias_b", "out_norm_g_b", "norm_ffn_g", "w_gate_up", "ffn_conv_w", "ffn_conv_b", "w_down",
             "final_norm_g"]
    wts = {n: local[n] for n in order}
    moms = {n: local["m_" + n] for n in order}
    vars_ = {n: local["v_" + n] for n in order}
    h0 = x[0]
    memx = mem[0]
    tgt = loss_target[0]
    s = h0.shape[0]
    tm = _rows(s)

    g_mk, g_out, g_ia, g_cq, g_cw = _all_gather(
        [w_mem_kv.astype(_MXU), w_out.astype(_MXU), w_in_a.astype(_MXU), conv_qkv_b, ffn_conv_w], "gather_first")
    gu_land = ((N_DEV, D, GU_SHARD), _MXU)
    dn_land = ((N_DEV, DN_SHARD, D), _MXU)
    whole = [(0, False), (1, False)]
    ffn0_w = _send_start([w_gate_up[0].astype(_MXU), w_down[0].astype(_MXU)], [gu_land, dn_land], whole, "gather_ffn0_start")
    w_ia = _assemble(g_ia, 2)[0]
    conv_qkv = _pad_rows(_assemble(g_cq, 2)[0], HALO)
    ffn_cw_full = _assemble(g_cw, 2)
    ffn_cw = [_ff_blocks(_pad_rows(ffn_cw_full[i], HALO)) for i in range(2)]
    ffn_cb = [_ff_blocks(ffn_conv_b[i:i + 1]) for i in range(2)]
    bucket = jnp.asarray(_bucket_table())
    bias = _bias_build(rel_bias, bucket, "bias_build")
    sinks = _pad_lanes(sinks_a)
    par_b = _pad_lanes(jnp.concatenate([a_log_b, dt_bias_b], axis=1))

    row_x = pl.BlockSpec((tm, D), lambda i, j: (i, 0))
    gu_shape = (2, FF_BLOCKS, s, GU_SHARD)

    def in_proj(h, g, w, w_spec, n_cols, tn, name, deps=(), out_dtype=F32):
        return _norm_matmul(h, g, w, w_spec, n_cols // tn, (h.shape[0], n_cols),
                            pl.BlockSpec((_rows(h.shape[0]), tn), lambda i, j: (i, j)), name, deps=deps, out_dtype=out_dtype)

    def ffn_fwd(i, h, g_gu, g_dn, deps=()):
        gu, hn = _norm_matmul(h, norm_ffn_g[i:i + 1], g_gu, _spec_gate_up(1), N_DEV, gu_shape,
                              _spec_gu_act(0, 1, tm), f"gate_up_{i}", deps=deps, out_dtype=_ACT)
        h_new, act = _glu_down(gu, ffn_cw[i], ffn_cb[i], g_dn, h, f"glu_down_{i}")
        return h_new, gu, hn, act

    def out_proj(i, mix, h):
        return _matmul_res(mix, row_x, g_out, _spec_rowsharded(i, D // N_DEV, D), 1, h, f"out_proj_{i}")

    proj_a, hn_a = in_proj(h0, norm_mix_g[0:1], w_ia, pl.BlockSpec((D, 640), lambda i, j: (0, j)), IN_A, 640, "in_proj_a",
                           deps=[ffn0_w.token], out_dtype=_ACT)
    memkv0, memn0 = in_proj(memx, norm_mem_g[0:1], g_mk, _spec_rowsharded(0, D // N_DEV, 2 * X_Q), 2 * X_Q, 2 * X_Q, "mem_proj_0")
    mix_a = _mix_a_fwd(proj_a, bias, sinks, memkv0, "mix_a_fwd")
    h1 = out_proj(0, mix_a, h0)
    g_gu0, g_dn0 = _send_wait(ffn0_w, h1, "gather_ffn0_wait")
    in_b_w = _send_start([_reorder_b(w_in_b).astype(_MXU)], [((N_DEV, 1, D // N_DEV, IN_BP), _MXU)], [(0, False)], "gather_in_b_start")
    ffn1_w = _send_start([w_gate_up[1].astype(_MXU), w_down[1].astype(_MXU)], [gu_land, dn_land], whole, "gather_ffn1_start")
    h2, gu0, hn_f0, act0 = ffn_fwd(0, h1, g_gu0, g_dn0, deps=[in_b_w.token, ffn1_w.token])
    g_ib, = _send_wait(in_b_w, h2, "gather_in_b_wait")
    proj_b, hn_b = in_proj(h2, norm_mix_g[1:2], g_ib, _spec_rowsharded(0, D // N_DEV, 896, col_block=1), IN_BP, 896, "in_proj_b")
    memkv1, memn1 = in_proj(memx, norm_mem_g[1:2], g_mk, _spec_rowsharded(1, D // N_DEV, 2 * X_Q), 2 * X_Q, 2 * X_Q, "mem_proj_1")
    mix_b, states = _mix_b_fwd(proj_b, conv_qkv, par_b, out_norm_g_b, memkv1, "mix_b_fwd")
    h3 = out_proj(1, mix_b, h2)
    g_gu1, g_dn1 = _send_wait(ffn1_w, h3, "gather_ffn1_wait")
    h4, gu1, hn_f1, act1 = ffn_fwd(1, h3, g_gu1, g_dn1)
    loss_row, dh, d_final_g = _loss_head(h4, final_norm_g[None, :], tgt, "loss_head")

    zeros_mem = jnp.zeros_like(memx)
    per_dest2 = [(0, True), (1, True)]

    def ffn_bwd(i, dh, h_in, gu, hn_f, act, g_gu, g_dn, deps=()):
        dgu, d_cw, d_cb = _glu_bwd(gu, ffn_cw[i], ffn_cb[i], dh, g_dn, f"glu_bwd_{i}", deps=deps)
        d_wdown = _matmul_tn(act, pl.BlockSpec((None, tm, GU_SHARD), lambda j, r: (j, r, 0)),
                             dh, pl.BlockSpec((tm, D), lambda j, r: (r, 0)), s, FF_BLOCKS, (GU_SHARD, D),
                             (N_DEV, DN_SHARD, D), pl.BlockSpec((2, DN_SHARD, D), lambda j, r: (j, 0, 0)), f"d_w_down_{i}")
        dh_new, d_g = _matmul_nt_normbwd(dgu, _spec_gu_act(0, 1, tm), g_gu, _spec_gate_up(1), N_DEV, h_in,
                                         norm_ffn_g[i:i + 1], dh, f"d_ffn_in_{i}")
        d_wgu = _matmul_tn(hn_f, pl.BlockSpec((tm, D), lambda j, r: (r, 0)), dgu, _spec_gu_act(1, 0, tm), s, N_DEV,
                           (D, GU_SHARD), (N_DEV, D, GU_SHARD), pl.BlockSpec((None, D, GU_SHARD), lambda j, r: (j, 0, 0)),
                           f"d_w_gate_up_{i}")
        sent = _send_start([d_wdown, d_wgu], [((N_DEV, DN_SHARD, D), _WIRE), ((N_DEV, D, GU_SHARD), _WIRE)], per_dest2,
                           f"send_ffn{i}_grads_start")
        return dh_new, sent, d_cw, d_cb, d_g

    def out_bwd(i, dh, mix, deps):
        dmix = _matmul_nt(dh, g_out, _spec_rowsharded(i, D // N_DEV, D), 1, (s, D), row_x, f"d_mix_{i}", deps=deps, out_dtype=_ACT)
        d_wout = _matmul_tn(mix, pl.BlockSpec((tm, D), lambda j, r: (r, 0)), dh, pl.BlockSpec((tm, D), lambda j, r: (r, 0)),
                            s, 1, (D, D), (N_DEV, D // N_DEV, D), pl.BlockSpec((N_DEV, D // N_DEV, D), lambda j, r: (0, 0, 0)),
                            f"d_w_out_{i}")
        return dmix, d_wout

    def mem_bwd(i, dmemkv, memn):
        tmm = _rows(MEM_LEN)
        _, d_g = _matmul_nt_normbwd(dmemkv, pl.BlockSpec((tmm, 2 * X_Q), lambda r, j: (r, 0)), g_mk,
                                    _spec_rowsharded(i, D // N_DEV, 2 * X_Q), 1, memx, norm_mem_g[i:i + 1], zeros_mem,
                                    f"d_mem_in_{i}")
        by_row = lambda j, r: (r, 0)
        d_w = _matmul_tn(memn, pl.BlockSpec((tmm, D), by_row), dmemkv, pl.BlockSpec((tmm, 2 * X_Q), by_row), MEM_LEN, 1,
                         (D, 2 * X_Q), (N_DEV, D // N_DEV, 2 * X_Q),
                         pl.BlockSpec((N_DEV, D // N_DEV, 2 * X_Q), lambda j, r: (0, 0, 0)), f"d_w_mem_kv_{i}")
        return d_w, d_g

    out_land = ((N_DEV, D // N_DEV, D), _WIRE)
    mk_land = ((N_DEV, D // N_DEV, 2 * X_Q), _WIRE)
    dh, ffn1_g, d_cw1, d_cb1, d_gf1 = ffn_bwd(1, dh, h3, gu1, hn_f1, act1, g_gu1, g_dn1)
    dmix, d_wout1 = out_bwd(1, dh, mix_b, [ffn1_g.token])
    dproj_b, d_convw, d_par, d_ng, dmemkv1 = _mix_b_bwd(proj_b, conv_qkv, par_b, out_norm_g_b, memkv1, states, dmix, "mix_b_bwd")
    dh, d_gm1 = _matmul_nt_normbwd(dproj_b, pl.BlockSpec((tm, 896), lambda i, j: (i, j)), g_ib,
                                   _spec_rowsharded(0, D // N_DEV, 896, col_block=1), IN_BP // 896, h2, norm_mix_g[1:2], dh, "d_in_b")
    d_wib = _matmul_tn(hn_b, pl.BlockSpec((tm, D), lambda j, r: (r, 0)), dproj_b, pl.BlockSpec((tm, 896), lambda j, r: (r, j)),
                       s, IN_BP // 896, (D, 896), (N_DEV, D // N_DEV, IN_BP),
                       pl.BlockSpec((N_DEV, D // N_DEV, 896), lambda j, r: (0, 0, j)), "d_w_in_b")
    d_wmk1, d_gmem1 = mem_bwd(1, dmemkv1, memn1)
    mix1_g = _send_start([d_wout1, d_wib, d_wmk1], [out_land, ((N_DEV, D // N_DEV, IN_BP), _WIRE), mk_land],
                         [(0, True), (1, True), (2, True)], "send_mix1_grads_start")
    dh, ffn0_g, d_cw0, d_cb0, d_gf0 = ffn_bwd(0, dh, h1, gu0, hn_f0, act0, g_gu0, g_dn0, deps=[mix1_g.token])
    dmix, d_wout0 = out_bwd(0, dh, mix_a, [ffn0_g.token])
    dproj_a, dbias, dsinks, dmemkv0 = _mix_a_bwd(proj_a, bias, sinks, memkv0, dmix, "mix_a_bwd")
    dh, d_gm0 = _matmul_nt_normbwd(dproj_a, pl.BlockSpec((tm, 640), lambda i, j: (i, j)), w_ia,
                                   pl.BlockSpec((D, 640), lambda i, j: (0, j)), IN_A // 640, h0, norm_mix_g[0:1], dh, "d_in_a")
    d_wia = _matmul_tn(hn_a, pl.BlockSpec((tm, D), lambda j, r: (r, 0)), dproj_a, pl.BlockSpec((tm, IN_A), lambda j, r: (r, 0)),
                       s, 1, (D, IN_A), (N_DEV, D, IA_SHARD), pl.BlockSpec((N_DEV, D, IA_SHARD), lambda j, r: (0, 0, 0)),
                       "d_w_in_a", split=IA_SHARD)
    d_wmk0, d_gmem0 = mem_bwd(0, dmemkv0, memn0)
    d_rel = _bias_reduce(dbias, bucket, "bias_reduce")
    small = _pack_small(d_rel, (d_cb0, d_cb1), (d_cw0, d_cw1), d_convw, (d_gm0, d_gm1), (d_gmem0, d_gmem1),
                        (d_gf0, d_gf1), d_final_g, dsinks, d_par, d_ng, "pack_small")
    mix0_g = _send_start([d_wout0, d_wia, d_wmk0, small],
                         [out_land, ((N_DEV, D, IA_SHARD), _WIRE), mk_land, ((N_DEV, SMALL_ROWS, D_FF), F32)],
                         [(0, True), (1, True), (2, True), (3, False)], "send_mix0_grads_start")

    res = {}

    def update(nm, parts, tr, restore=False):
        res[nm] = _adamw(parts, wts[nm], moms[nm], vars_[nm], tr, "adamw_" + nm, restore_b=restore)

    r_dn1, r_gu1 = _send_wait(ffn1_g, mix0_g.token, "send_ffn1_grads_wait")
    r_dn0, r_gu0 = _send_wait(ffn0_g, mix0_g.token, "send_ffn0_grads_wait")
    r_out1, r_ib, r_mk1 = _send_wait(mix1_g, mix0_g.token, "send_mix1_grads_wait")
    update("w_gate_up", [r_gu0, r_gu1], 128)
    update("w_down", [r_dn0, r_dn1], 176)
    update("w_in_b", [r_ib], 32, True)
    done = [res[nm][1] for nm in ("w_gate_up", "w_down", "w_in_b")]
    r_out0, r_ia, r_mk0, r_small = _send_wait(mix0_g, done, "send_mix0_grads_wait")
    update("w_mem_kv", [r_mk0, r_mk1], 128)
    update("w_out", [r_out0, r_out1], 128)
    update("w_in_a", [r_ia], 512)

    my = 4 * lax.axis_index("x") + 2 * lax.axis_index("y") + lax.axis_index("c")
    cq = conv_qkv_b.shape[-1]
    cf = ffn_conv_w.shape[-1]
    rc_qkv = lax.dynamic_slice_in_dim(r_small[:, SP_QKV:SP_QKV + B_CONV, :B_QKV], my * cq, cq, axis=2)[:, None]
    rc_ffn = lax.dynamic_slice_in_dim(r_small[:, SP_CW:SP_CW + 2 * FFN_CONV, :], my * cf, cf, axis=2).reshape(N_DEV, 2, FFN_CONV, cf)
    as2d = lambda a: a[None, :] if a.ndim == 1 else a
    small_out = _adamw_small(r_small, rc_qkv, rc_ffn, [as2d(wts[n]) for n in _SMALL], [as2d(moms[n]) for n in _SMALL],
                             [as2d(vars_[n]) for n in _SMALL], "adamw_small")
    ns = len(_SMALL)
    for i, nm in enumerate(_SMALL):
        res[nm] = [small_out[k * ns + i].reshape(wts[nm].shape) for k in range(4)]

    loss = lax.psum(loss_row[0, 0], AXES)
    return (loss, dh[None], *[res[n][0] for n in order], *[res[n][1] for n in order],
            *[res[n][2] for n in order], *[res[n][3] for n in order])
```

```python
import functools
import math

import numpy as np

import jax
import jax.numpy as jnp
from jax import lax
from jax.experimental import pallas as pl
from jax.experimental.pallas import tpu as pltpu
from jax.experimental.pallas import tpu_sc as plsc

F32 = jnp.float32
_MXU = jnp.bfloat16
_ACT = jnp.bfloat16
_WIRE = jnp.bfloat16
_HI = lax.Precision.HIGH
_TM = 1024
_TM_GLU = 512
_VMEM_LIMIT = 48 * 1024 * 1024
_SDS = jax.ShapeDtypeStruct

D = 1024
EPS = 1e-6
A_HEADS, A_KV_HEADS, A_HD, BLK = 12, 2, 64, 128
N_BUCKETS, MAX_DISTANCE = 32, 128
B_QK_HEADS, B_V_HEADS, B_HD, B_CONV, CHUNK = 3, 6, 128, 4, 64
X_HEADS, X_HD, MEM_LEN = 4, 64, 256
D_FF, FFN_CONV = 2816, 3
A_Q, A_KV, X_Q = 768, 128, 256
B_QK, B_V, B_QKV = 384, 768, 1536
IN_A, IN_B = 1280, 2572
IN_BP = 2688
BP_Z, BP_XQ, BP_GATE = 1536, 2304, 2560
HALO = 8
GLU_HALO = 16

N_DEV = 8
AXES = ("x", "y", "c")
GU_SHARD = 2 * D_FF // N_DEV
FF_BLOCKS = D_FF // GU_SHARD
DN_SHARD = D_FF // N_DEV
IA_SHARD = IN_A // N_DEV

ADAM_LR, ADAM_B1, ADAM_B2, ADAM_EPS, ADAM_WD, ADAM_STEP = 0.001, 0.9, 0.999, 1e-08, 0.01, 10

SP_REL, SP_CB, SP_CW, SP_QKV, SP_MIX, SP_MEM, SP_FFN, SP_FINAL, SP_MISC, SMALL_ROWS = 0, 32, 34, 40, 44, 46, 48, 50, 51, 56


def _cp(*sems):
    return pltpu.CompilerParams(dimension_semantics=sems, vmem_limit_bytes=_VMEM_LIMIT)


def _mm(a, b):
    return jnp.dot(a.astype(_MXU), b.astype(_MXU), preferred_element_type=F32)


def _mm_nt(a, b):
    return lax.dot_general(a.astype(_MXU), b.astype(_MXU), (((1,), (1,)), ((), ())), preferred_element_type=F32)


def _mm_tn(a, b):
    return lax.dot_general(a.astype(_MXU), b.astype(_MXU), (((0,), (0,)), ((), ())), preferred_element_type=F32)


def _mmf(a, b):
    return jnp.dot(a, b, preferred_element_type=F32, precision=_HI)


def _mmf_nt(a, b):
    return lax.dot_general(a, b, (((1,), (1,)), ((), ())), preferred_element_type=F32, precision=_HI)


def _mmf_tn(a, b):
    return lax.dot_general(a, b, (((0,), (0,)), ((), ())), preferred_element_type=F32, precision=_HI)


def _silu(x):
    return x * jax.nn.sigmoid(x)


def _w2d(ref):
    v = ref[...]
    return v.reshape(-1, v.shape[-1])


def _rows(m):
    return min(m, _TM)


def _spec_rowsharded(layer, rows, cols, col_block=None):
    if col_block is None:
        return pl.BlockSpec((N_DEV, None, rows, cols), lambda *_: (0, layer, 0, 0))
    return pl.BlockSpec((N_DEV, None, rows, cols), lambda *ids: (0, layer, 0, ids[col_block]))


def _spec_gate_up(axis):
    return pl.BlockSpec((None, D, GU_SHARD), lambda *ids: (ids[axis], 0, 0))


def _spec_down(axis):
    return pl.BlockSpec((2, DN_SHARD, D), lambda *ids: (ids[axis], 0, 0))


def _dep_specs(deps):
    return [pl.BlockSpec(memory_space=pl.ANY) for d in deps]


def _spec_gu_act(row_axis, axis, tm):
    return pl.BlockSpec((None, None, tm, GU_SHARD), lambda *ids: (ids[axis] // FF_BLOCKS, ids[axis] % FF_BLOCKS, ids[row_axis], 0))


def _norm_matmul(x, g, w, w_spec, n_blocks, out_shape, out_spec, name, deps=(), out_dtype=F32):
    m, k = x.shape
    tm = _rows(m)

    def body(x_ref, g_ref, w_ref, *rest):
        y_ref, hn_ref = rest[-2:]

        @pl.when(pl.program_id(1) == 0)
        def _():
            xv = x_ref[...]
            r = lax.rsqrt(jnp.mean(xv * xv, axis=-1, keepdims=True) + EPS)
            hn_ref[...] = (xv * r * g_ref[...]).astype(hn_ref.dtype)

        y_ref[...] = _mm(hn_ref[...], _w2d(w_ref)).astype(y_ref.dtype)

    return pl.pallas_call(
        body, grid=(m // tm, n_blocks),
        in_specs=[pl.BlockSpec((tm, k), lambda i, j: (i, 0)), pl.BlockSpec((1, k), lambda i, j: (0, 0)), w_spec]
        + _dep_specs(deps),
        out_specs=[out_spec, pl.BlockSpec((tm, k), lambda i, j: (i, 0))],
        out_shape=[_SDS(out_shape, out_dtype), _SDS((m, k), _ACT)],
        name=name, compiler_params=_cp("arbitrary", "arbitrary"))(x, g, w, *deps)


def _matmul_res(a, a_spec, w, w_spec, n_k, res, name):
    m, n = res.shape
    tm = _rows(m)

    def body(a_ref, w_ref, r_ref, o_ref):
        part = _mm(a_ref[...], _w2d(w_ref))

        @pl.when(pl.program_id(1) == 0)
        def _():
            o_ref[...] = r_ref[...] + part

        @pl.when(pl.program_id(1) > 0)
        def _():
            o_ref[...] += part

    return pl.pallas_call(
        body, grid=(m // tm, n_k),
        in_specs=[a_spec, w_spec, pl.BlockSpec((tm, n), lambda i, j: (i, 0))],
        out_specs=pl.BlockSpec((tm, n), lambda i, j: (i, 0)),
        out_shape=_SDS((m, n), F32), name=name, compiler_params=_cp("arbitrary", "arbitrary"))(a, w, res)


def _matmul_nt(dy, w, w_spec, n_blocks, out_shape, out_spec, name, deps=(), out_dtype=F32):
    m, n = dy.shape
    tm = _rows(m)

    def body(dy_ref, w_ref, *rest):
        o_ref = rest[-1]
        o_ref[...] = _mm_nt(dy_ref[...], _w2d(w_ref)).astype(o_ref.dtype)

    return pl.pallas_call(
        body, grid=(m // tm, n_blocks),
        in_specs=[pl.BlockSpec((tm, n), lambda i, j: (i, 0)), w_spec] + _dep_specs(deps),
        out_specs=out_spec, out_shape=_SDS(out_shape, out_dtype),
        name=name, compiler_params=_cp("arbitrary", "arbitrary"))(dy, w, *deps)


def _matmul_nt_normbwd(dy, dy_spec, w, w_spec, nj, h, g, dh_in, name):
    m, k = h.shape
    tm = _rows(m)

    def body(dy_ref, w_ref, h_ref, g_ref, dhin_ref, dh_ref, dg_ref, acc_ref):
        i, j = pl.program_id(0), pl.program_id(1)

        @pl.when(j == 0)
        def _():
            acc_ref[...] = jnp.zeros_like(acc_ref)

        acc_ref[...] += _mm_nt(dy_ref[...], _w2d(w_ref))

        @pl.when(j == nj - 1)
        def _():
            xv = h_ref[...]
            r = lax.rsqrt(jnp.mean(xv * xv, axis=-1, keepdims=True) + EPS)
            xh = xv * r
            dhn = acc_ref[...]
            part = jnp.sum(dhn * xh, axis=0, keepdims=True)

            @pl.when(i == 0)
            def _():
                dg_ref[...] = part

            @pl.when(i > 0)
            def _():
                dg_ref[...] += part

            t = dhn * g_ref[...]
            dh_ref[...] = dhin_ref[...] + r * (t - xh * jnp.mean(t * xh, axis=-1, keepdims=True))

    return pl.pallas_call(
        body, grid=(m // tm, nj),
        in_specs=[dy_spec, w_spec, pl.BlockSpec((tm, k), lambda i, j: (i, 0)), pl.BlockSpec((1, k), lambda i, j: (0, 0)),
                  pl.BlockSpec((tm, k), lambda i, j: (i, 0))],
        out_specs=[pl.BlockSpec((tm, k), lambda i, j: (i, 0)), pl.BlockSpec((1, k), lambda i, j: (0, 0))],
        out_shape=[_SDS((m, k), F32), _SDS((1, k), F32)],
        scratch_shapes=[pltpu.VMEM((tm, k), F32)],
        name=name, compiler_params=_cp("arbitrary", "arbitrary"))(dy, w, h, g, dh_in)


def _matmul_tn(x, x_spec, dy, dy_spec, m, n_blocks, acc_shape, out_shape, out_spec, name, split=None):
    tm = _rows(m)
    nm = m // tm

    def body(x_ref, dy_ref, o_ref, acc_ref):
        @pl.when(pl.program_id(1) == 0)
        def _():
            acc_ref[...] = jnp.zeros_like(acc_ref)

        acc_ref[...] += _mm_tn(x_ref[...], dy_ref[...])

        @pl.when(pl.program_id(1) == nm - 1)
        def _():
            if split is None:
                o_ref[...] = acc_ref[...].reshape(o_ref.shape).astype(o_ref.dtype)
            else:
                for d in range(N_DEV):
                    o_ref[d] = acc_ref[:, d * split:(d + 1) * split].astype(o_ref.dtype)

    return pl.pallas_call(
        body, grid=(n_blocks, nm), in_specs=[x_spec, dy_spec], out_specs=out_spec,
        out_shape=_SDS(out_shape, _WIRE), scratch_shapes=[pltpu.VMEM(acc_shape, F32)],
        name=name, compiler_params=_cp("arbitrary", "arbitrary"))(x, dy)


def _loss_head(h, g, tgt, name):
    m, k = h.shape
    tm = _rows(m)

    def body(h_ref, g_ref, t_ref, loss_ref, dh_ref, dg_ref):
        i = pl.program_id(0)
        xv = h_ref[...]
        r = lax.rsqrt(jnp.mean(xv * xv, axis=-1, keepdims=True) + EPS)
        xh = xv * r
        gv = g_ref[...]
        err = xh * gv - t_ref[...]
        lpart = jnp.zeros((1, 128), F32) + 0.5 * jnp.sum(jnp.mean(err * err, axis=-1, keepdims=True), axis=0, keepdims=True)
        dy = err * (1.0 / k)
        gpart = jnp.sum(dy * xh, axis=0, keepdims=True)

        @pl.when(i == 0)
        def _():
            loss_ref[...] = lpart
            dg_ref[...] = gpart

        @pl.when(i > 0)
        def _():
            loss_ref[...] += lpart
            dg_ref[...] += gpart

        t = dy * gv
        dh_ref[...] = r * (t - xh * jnp.mean(t * xh, axis=-1, keepdims=True))

    return pl.pallas_call(
        body, grid=(m // tm,),
        in_specs=[pl.BlockSpec((tm, k), lambda i: (i, 0)), pl.BlockSpec((1, k), lambda i: (0, 0)),
                  pl.BlockSpec((tm, k), lambda i: (i, 0))],
        out_specs=[pl.BlockSpec((1, 128), lambda i: (0, 0)), pl.BlockSpec((tm, k), lambda i: (i, 0)),
                   pl.BlockSpec((1, k), lambda i: (0, 0))],
        out_shape=[_SDS((1, 128), F32), _SDS((m, k), F32), _SDS((1, k), F32)],
        name=name, compiler_params=_cp("arbitrary"))(h, g, tgt)


def _glu_down(gu, conv_w, conv_b, w_down, res, name):
    s = gu.shape[2]
    tm = min(s, _TM_GLU)

    def body(gu_ref, prev_ref, w_ref, b_ref, wdn_ref, r_ref, o_ref, act_ref):
        i, j = pl.program_id(0), pl.program_id(1)
        prev = jnp.where(i > 0, prev_ref[...].astype(F32), 0.0)
        ext = jnp.concatenate([prev, gu_ref[0].astype(F32)], axis=0)
        gc = b_ref[...] + w_ref[FFN_CONV - 1:FFN_CONV, :] * ext
        for k in range(FFN_CONV - 1):
            gc = gc + w_ref[k:k + 1, :] * pltpu.roll(ext, FFN_CONV - 1 - k, 0)
        act = (_silu(gc[GLU_HALO:]) * gu_ref[1].astype(F32)).astype(act_ref.dtype)
        act_ref[...] = act
        part = _mm(act, _w2d(wdn_ref))

        @pl.when(j == 0)
        def _():
            o_ref[...] = r_ref[...] + part

        @pl.when(j > 0)
        def _():
            o_ref[...] += part

    return pl.pallas_call(
        body, grid=(s // tm, FF_BLOCKS),
        in_specs=[pl.BlockSpec((2, None, tm, GU_SHARD), lambda i, j: (0, j, i, 0)),
                  pl.BlockSpec((None, None, GLU_HALO, GU_SHARD),
                               lambda i, j: (0, j, jnp.maximum(i * (tm // GLU_HALO) - 1, 0), 0)),
                  pl.BlockSpec((None, HALO, GU_SHARD), lambda i, j: (j, 0, 0)),
                  pl.BlockSpec((None, 1, GU_SHARD), lambda i, j: (j, 0, 0)),
                  _spec_down(1), pl.BlockSpec((tm, D), lambda i, j: (i, 0))],
        out_specs=[pl.BlockSpec((tm, D), lambda i, j: (i, 0)), pl.BlockSpec((None, tm, GU_SHARD), lambda i, j: (j, i, 0))],
        out_shape=[_SDS((s, D), F32), _SDS((FF_BLOCKS, s, GU_SHARD), _ACT)], name=name,
        compiler_params=_cp("arbitrary", "arbitrary"))(gu, gu, conv_w, conv_b, w_down, res)


def _glu_bwd(gu, conv_w, conv_b, dh, w_down, name, deps=()):
    s = gu.shape[2]
    tm = min(s, _TM_GLU)
    nt = s // tm
    ext_rows = tm + GLU_HALO

    def body(gu_ref, prev_ref, w_ref, b_ref, dh_ref, wdn_ref, *rest):
        dgu_ref, dw_ref, db_ref, carry_ref = rest[-4:]
        t = pl.program_id(1)
        i = nt - 1 - t

        @pl.when(t == 0)
        def _():
            carry_ref[...] = jnp.zeros_like(carry_ref)
            dw_ref[...] = jnp.zeros_like(dw_ref)
            db_ref[...] = jnp.zeros_like(db_ref)

        up = gu_ref[1].astype(F32)
        prev = jnp.where(i > 0, prev_ref[...].astype(F32), 0.0)
        ext = jnp.concatenate([prev, gu_ref[0].astype(F32)], axis=0)
        shifted = [pltpu.roll(ext, FFN_CONV - 1 - j, 0) if j < FFN_CONV - 1 else ext for j in range(FFN_CONV)]
        gc = b_ref[...] + shifted[0] * w_ref[0:1, :]
        for j in range(1, FFN_CONV):
            gc = gc + shifted[j] * w_ref[j:j + 1, :]
        gc = gc[GLU_HALO:]
        sg = jax.nn.sigmoid(gc)
        da = _mm_nt(dh_ref[...], _w2d(wdn_ref))
        dup = da * (gc * sg)
        dgc = da * up * (sg * (1.0 + gc * (1.0 - sg)))
        db_ref[...] += jnp.sum(dgc, axis=0, keepdims=True)
        dgc_ext = jnp.concatenate([jnp.zeros((GLU_HALO, GU_SHARD), F32), dgc], axis=0)
        dext = dgc_ext * w_ref[FFN_CONV - 1:FFN_CONV, :]
        for j in range(FFN_CONV):
            dw_ref[j:j + 1, :] += jnp.sum(shifted[j] * dgc_ext, axis=0, keepdims=True)
            if j < FFN_CONV - 1:
                dext = dext + w_ref[j:j + 1, :] * pltpu.roll(dgc_ext, ext_rows - (FFN_CONV - 1 - j), 0)
        tail = jnp.concatenate([jnp.zeros((tm - GLU_HALO, GU_SHARD), F32), carry_ref[...]], axis=0)
        dgate = dext[GLU_HALO:] + tail
        carry_ref[...] = dext[:GLU_HALO]
        dgu_ref[0] = dgate.astype(dgu_ref.dtype)
        dgu_ref[1] = dup.astype(dgu_ref.dtype)

    return pl.pallas_call(
        body, grid=(FF_BLOCKS, nt),
        in_specs=[pl.BlockSpec((2, None, tm, GU_SHARD), lambda j, t: (0, j, nt - 1 - t, 0)),
                  pl.BlockSpec((None, None, GLU_HALO, GU_SHARD),
                               lambda j, t: (0, j, jnp.maximum((nt - 1 - t) * (tm // GLU_HALO) - 1, 0), 0)),
                  pl.BlockSpec((None, HALO, GU_SHARD), lambda j, t: (j, 0, 0)),
                  pl.BlockSpec((None, 1, GU_SHARD), lambda j, t: (j, 0, 0)),
                  pl.BlockSpec((tm, D), lambda j, t: (nt - 1 - t, 0)), _spec_down(0)] + _dep_specs(deps),
        out_specs=[pl.BlockSpec((2, None, tm, GU_SHARD), lambda j, t: (0, j, nt - 1 - t, 0)),
                   pl.BlockSpec((None, HALO, GU_SHARD), lambda j, t: (j, 0, 0)),
                   pl.BlockSpec((None, 1, GU_SHARD), lambda j, t: (j, 0, 0))],
        out_shape=[_SDS(gu.shape, _ACT), _SDS((FF_BLOCKS, HALO, GU_SHARD), F32), _SDS((FF_BLOCKS, 1, GU_SHARD), F32)],
        scratch_shapes=[pltpu.VMEM((GLU_HALO, GU_SHARD), F32)],
        name=name, compiler_params=_cp("arbitrary", "arbitrary"))(gu, gu, conv_w, conv_b, dh, w_down, *deps)


def _bucket_table():
    qi = np.arange(BLK)[:, None]
    kj = np.arange(BLK)[None, :]
    n = np.where(kj > qi, BLK + qi - kj, qi - kj)
    max_exact = N_BUCKETS // 2
    nf = np.maximum(n, 1).astype(np.float32)
    large = max_exact + (np.log(nf / max_exact) / math.log(MAX_DISTANCE / max_exact)
                         * (N_BUCKETS - max_exact)).astype(np.int32)
    large = np.minimum(large, N_BUCKETS - 1)
    return np.where(n < max_exact, n, large).astype(np.int32)


def _lane_low():
    return lax.broadcasted_iota(jnp.int32, (1, 128), 1) < A_HD


def _swa_group(q, kd, vd, sink, bias, upper, first):
    n = A_HEADS // A_KV_HEADS
    low = _lane_low()
    pairs = [q[:, p * 128:(p + 1) * 128] for p in range(n // 2)]
    qm = jnp.concatenate([jnp.where(low == (h % 2 == 0), pairs[h // 2], 0.0) for h in range(n)], axis=0)
    s2 = _mm_nt(qm, kd) * (A_HD ** -0.5)
    s = jnp.where(upper[None], s2[:, :BLK].reshape(n, BLK, BLK), s2[:, BLK:].reshape(n, BLK, BLK)) + bias
    s = jnp.where((upper & first)[None], -jnp.inf, s)
    m = jnp.maximum(jnp.max(s, axis=-1, keepdims=True), sink)
    p = jnp.exp(s - m)
    split = jnp.concatenate([jnp.where(upper[None], p, 0.0), jnp.where(upper[None], 0.0, p)], axis=-1)
    split = split.reshape(n * BLK, 2 * BLK)
    den = _mm(p.reshape(n * BLK, BLK), jnp.ones((BLK, 128), F32)) + jnp.exp(sink - m).reshape(n * BLK, 1)
    o = _mm(split, vd) / den
    return jnp.concatenate([jnp.where(low, o[2 * p * BLK:(2 * p + 1) * BLK], o[(2 * p + 1) * BLK:(2 * p + 2) * BLK])
                            for p in range(n // 2)], axis=1)


def _swa_sinks(sink_ref, g):
    n = A_HEADS // A_KV_HEADS
    return jnp.concatenate([sink_ref[:, h:h + 1] for h in range(g * n, (g + 1) * n)], axis=0).reshape(n, 1, 1)


def _both_halves(t, t_rolled, g):
    low = _lane_low()
    return jnp.where(low, t, t_rolled) if g == 0 else jnp.where(low, t_rolled, t)


def _cross_pairs(q, mk, mv):
    rows = q.shape[0]
    low = _lane_low()
    qm = [jnp.concatenate([jnp.where(low, q[:, p * 128:(p + 1) * 128], 0.0), jnp.where(low, 0.0, q[:, p * 128:(p + 1) * 128])], axis=0)
          for p in range(X_HEADS // 2)]
    s = [_mm_nt(qm[p], mk[:, p * 128:(p + 1) * 128]) * (X_HD ** -0.5) for p in range(X_HEADS // 2)]
    e = [jnp.exp(t - jnp.max(t, axis=-1, keepdims=True)) for t in s]
    pr = [t / jnp.sum(t, axis=-1, keepdims=True) for t in e]
    o = [_mm(pr[p], mv[:, p * 128:(p + 1) * 128]) for p in range(X_HEADS // 2)]
    return jnp.concatenate([jnp.where(low, t[:rows], t[rows:]) for t in o], axis=1)


def _swa_upper():
    qi = lax.broadcasted_iota(jnp.int32, (BLK, BLK), 0)
    kj = lax.broadcasted_iota(jnp.int32, (BLK, BLK), 1)
    return kj > qi


def _bias_build(rel_bias, bucket, name):
    def body(rb_ref, bucket_ref, o_ref):
        b = bucket_ref[...]
        for h in range(A_HEADS):
            acc = jnp.zeros((BLK, BLK), F32)
            for k in range(N_BUCKETS):
                acc = jnp.where(b == k, rb_ref[k, h], acc)
            o_ref[h] = acc

    return pl.pallas_call(
        body, in_specs=[pl.BlockSpec(memory_space=pltpu.SMEM), pl.BlockSpec(memory_space=pltpu.VMEM)],
        out_specs=pl.BlockSpec(memory_space=pltpu.VMEM),
        out_shape=_SDS((A_HEADS, BLK, BLK), F32), name=name)(rel_bias, bucket)


def _bias_reduce(dbias, bucket, name):
    def body(db_ref, bucket_ref, o_ref):
        b = bucket_ref[...]
        row = lax.broadcasted_iota(jnp.int32, (N_BUCKETS, 128), 0)
        lane = lax.broadcasted_iota(jnp.int32, (N_BUCKETS, 128), 1)
        acc = jnp.zeros((N_BUCKETS, 128), F32)
        for h in range(A_HEADS):
            v = db_ref[h]
            for k in range(N_BUCKETS):
                sk = jnp.sum(jnp.sum(jnp.where(b == k, v, 0.0), axis=1, keepdims=True), axis=0, keepdims=True)
                acc = acc + jnp.where((row == k) & (lane == h), sk, 0.0)
        o_ref[...] = acc

    return pl.pallas_call(
        body, in_specs=[pl.BlockSpec(memory_space=pltpu.VMEM)] * 2,
        out_specs=pl.BlockSpec(memory_space=pltpu.VMEM),
        out_shape=_SDS((N_BUCKETS, 128), F32), name=name)(dbias, bucket)


def _mix_a_fwd(proj, bias, sinks, memkv, name):
    s = proj.shape[0]
    nb = s // BLK
    grp = A_HEADS // A_KV_HEADS

    def body(proj_ref, prev_ref, bias_ref, sink_ref, memkv_ref, o_ref):
        i = pl.program_id(0)
        upper = _swa_upper()
        prev = prev_ref[...].astype(F32)
        proj = proj_ref[...].astype(F32)
        kb = jnp.concatenate([prev[:, :A_KV], proj[:, A_Q:A_Q + A_KV]], axis=0)
        vb = jnp.concatenate([prev[:, A_KV:], proj[:, A_Q + A_KV:A_Q + 2 * A_KV]], axis=0)
        kb_r = pltpu.roll(kb, A_HD, 1)
        vb_r = pltpu.roll(vb, A_HD, 1)
        gw = A_Q // A_KV_HEADS
        outs = [_swa_group(proj[:, g * gw:(g + 1) * gw], _both_halves(kb, kb_r, g), _both_halves(vb, vb_r, g),
                           _swa_sinks(sink_ref, g), bias_ref[g * grp:(g + 1) * grp], upper, i == 0) for g in range(A_KV_HEADS)]
        outs.append(_cross_pairs(proj[:, A_Q + 2 * A_KV:], memkv_ref[:, :X_Q], memkv_ref[:, X_Q:]))
        o_ref[...] = jnp.concatenate(outs, axis=1).astype(o_ref.dtype)

    return pl.pallas_call(
        body, grid=(nb,),
        in_specs=[pl.BlockSpec((BLK, IN_A), lambda i: (i, 0)),
                  pl.BlockSpec((BLK, 2 * A_KV), lambda i: (jnp.maximum(i - 1, 0), A_Q // (2 * A_KV))),
                  pl.BlockSpec((A_HEADS, BLK, BLK), lambda i: (0, 0, 0)),
                  pl.BlockSpec((1, 128), lambda i: (0, 0)),
                  pl.BlockSpec((MEM_LEN, 2 * X_Q), lambda i: (0, 0))],
        out_specs=pl.BlockSpec((BLK, D), lambda i: (i, 0)),
        out_shape=_SDS((s, D), _ACT), name=name, compiler_params=_cp("arbitrary"))(proj, proj, bias, sinks, memkv)


def _mix_a_bwd(proj, bias, sinks, memkv, dmix, name):
    s = proj.shape[0]
    nb = s // BLK
    grp = A_HEADS // A_KV_HEADS

    def body(proj_ref, prev_ref, bias_ref, sink_ref, memkv_ref, dmix_ref,
             dproj_ref, dbias_ref, dsink_ref, dmemkv_ref, carry_ref):
        t = pl.program_id(0)
        i = nb - 1 - t

        @pl.when(t == 0)
        def _():
            carry_ref[...] = jnp.zeros_like(carry_ref)
            dbias_ref[...] = jnp.zeros_like(dbias_ref)
            dsink_ref[...] = jnp.zeros_like(dsink_ref)
            dmemkv_ref[...] = jnp.zeros_like(dmemkv_ref)

        upper = _swa_upper()
        lane = lax.broadcasted_iota(jnp.int32, (1, 128), 1)
        low = _lane_low()
        prev = prev_ref[...].astype(F32)
        proj = proj_ref[...].astype(F32)
        kb = jnp.concatenate([prev[:, :A_KV], proj[:, A_Q:A_Q + A_KV]], axis=0)
        vb = jnp.concatenate([prev[:, A_KV:], proj[:, A_Q + A_KV:A_Q + 2 * A_KV]], axis=0)
        kb_r = pltpu.roll(kb, A_HD, 1)
        vb_r = pltpu.roll(vb, A_HD, 1)
        gw = A_Q // A_KV_HEADS
        dqs, dkd, dvd = [], [], []
        dsink = jnp.zeros((1, 128), F32)
        for g in range(A_KV_HEADS):
            _, vjp = jax.vjp(functools.partial(_swa_group, upper=upper, first=i == 0), proj[:, g * gw:(g + 1) * gw],
                             _both_halves(kb, kb_r, g), _both_halves(vb, vb_r, g), _swa_sinks(sink_ref, g),
                             bias_ref[g * grp:(g + 1) * grp])
            dq, dk, dv, ds, db = vjp(dmix_ref[:, g * gw:(g + 1) * gw].astype(F32))
            dqs.append(dq)
            dkd.append(dk + pltpu.roll(dk, A_HD, 1))
            dvd.append(dv + pltpu.roll(dv, A_HD, 1))
            for h in range(grp):
                dsink = dsink + jnp.where(lane == g * grp + h, ds[h], 0.0)
            dbias_ref[g * grp:(g + 1) * grp] += db
        dsink_ref[...] += dsink
        dkb = jnp.where(low, dkd[0], dkd[1])
        dvb = jnp.where(low, dvd[0], dvd[1])
        _, vjp = jax.vjp(_cross_pairs, proj[:, A_Q + 2 * A_KV:], memkv_ref[:, :X_Q], memkv_ref[:, X_Q:])
        dxq, dmk, dmv = vjp(dmix_ref[:, A_Q:].astype(F32))
        dmemkv_ref[...] += jnp.concatenate([dmk, dmv], axis=1)
        dkv_cur = jnp.concatenate([dkb[BLK:], dvb[BLK:]], axis=1) + carry_ref[...]
        carry_ref[...] = jnp.concatenate([dkb[:BLK], dvb[:BLK]], axis=1)
        dproj_ref[...] = jnp.concatenate(dqs + [dkv_cur, dxq], axis=1).astype(dproj_ref.dtype)

    return pl.pallas_call(
        body, grid=(nb,),
        in_specs=[pl.BlockSpec((BLK, IN_A), lambda t: (nb - 1 - t, 0)),
                  pl.BlockSpec((BLK, 2 * A_KV), lambda t: (jnp.maximum(nb - 2 - t, 0), A_Q // (2 * A_KV))),
                  pl.BlockSpec((A_HEADS, BLK, BLK), lambda t: (0, 0, 0)),
                  pl.BlockSpec((1, 128), lambda t: (0, 0)),
                  pl.BlockSpec((MEM_LEN, 2 * X_Q), lambda t: (0, 0)),
                  pl.BlockSpec((BLK, D), lambda t: (nb - 1 - t, 0))],
        out_specs=[pl.BlockSpec((BLK, IN_A), lambda t: (nb - 1 - t, 0)),
                   pl.BlockSpec((A_HEADS, BLK, BLK), lambda t: (0, 0, 0)),
                   pl.BlockSpec((1, 128), lambda t: (0, 0)),
                   pl.BlockSpec((MEM_LEN, 2 * X_Q), lambda t: (0, 0))],
        out_shape=[_SDS((s, IN_A), _ACT), _SDS((A_HEADS, BLK, BLK), F32), _SDS((1, 128), F32),
                   _SDS((MEM_LEN, 2 * X_Q), F32)],
        scratch_shapes=[pltpu.VMEM((BLK, 2 * A_KV), F32)],
        name=name, compiler_params=_cp("arbitrary"))(proj, proj, bias, sinks, memkv, dmix)


def _dn_heads(yq, yk, yv, z, bl, al, a_log, dtb, ng, s0):
    c = CHUNK
    nh = B_V_HEADS
    rep = B_V_HEADS // B_QK_HEADS
    r = lax.broadcasted_iota(jnp.int32, (c, c), 0)
    cc = lax.broadcasted_iota(jnp.int32, (c, c), 1)
    q = [_silu(t) for t in yq]
    k = [_silu(t) for t in yk]
    v = [_silu(t) for t in yv]
    q = [t * lax.rsqrt(jnp.sum(t * t, axis=-1, keepdims=True) + EPS) * (B_HD ** -0.5) for t in q]
    k = [t * lax.rsqrt(jnp.sum(t * t, axis=-1, keepdims=True) + EPS) for t in k]
    beta = [jax.nn.sigmoid(t) for t in bl]
    g = [-jnp.exp(a_log[h]) * jax.nn.softplus(al[h] + dtb[h]) for h in range(nh)]
    gb = [jnp.broadcast_to(t, (c, c)) for t in g]
    gc_col = [jnp.sum(jnp.where(cc <= r, t.T, 0.0), axis=1, keepdims=True) for t in gb]
    gc_row = [jnp.sum(jnp.where(r <= cc, t, 0.0), axis=0, keepdims=True) for t in gb]
    gc_last = [jnp.sum(t, axis=0, keepdims=True) for t in g]
    decay = [jnp.exp(jnp.where(r >= cc, gc_col[h] - gc_row[h], -jnp.inf)) for h in range(nh)]
    kq = [_mmf_nt(jnp.concatenate([k[h], q[h]], axis=0), k[h]) for h in range(B_QK_HEADS)]
    kk = [t[:c] for t in kq]
    qk = [t[c:] for t in kq]
    egc = [jnp.exp(t) for t in gc_col]
    both = [_mmf(jnp.concatenate([(beta[h] * egc[h]) * k[h // rep], q[h // rep] * egc[h]], axis=0), s0[h]) for h in range(nh)]
    rhs = [beta[h] * v[h] - both[h][:c] for h in range(nh)]
    qs0 = [t[c:] for t in both]
    pw = [-(beta[h] * kk[h // rep] * jnp.where(r > cc, decay[h], 0.0)) for h in range(nh)]
    x = rhs
    for lvl in range(6):
        if lvl < 5:
            prod = [_mmf(pw[h], jnp.concatenate([x[h], pw[h]], axis=1)) for h in range(nh)]
            x = [x[h] + prod[h][:, :B_HD] for h in range(nh)]
            pw = [t[:, B_HD:] for t in prod]
        else:
            x = [x[h] + _mmf(pw[h], x[h]) for h in range(nh)]
    delta = x
    last = [_mmf(jnp.concatenate([qk[h // rep] * decay[h], (k[h // rep] * jnp.exp(gc_last[h] - gc_col[h])).T], axis=0), delta[h])
            for h in range(nh)]
    out = [qs0[h] + last[h][:c] for h in range(nh)]
    s1 = [jnp.exp(gc_last[h]) * s0[h] + last[h][c:] for h in range(nh)]
    o = [t * lax.rsqrt(jnp.mean(t * t, axis=-1, keepdims=True) + EPS) * ng for t in out]
    return [o[h] * _silu(z[h]) for h in range(nh)], s1


def _dn_conv(ext, w_ref):
    y = ext * w_ref[B_CONV - 1:B_CONV, :]
    for j in range(B_CONV - 1):
        y = y + w_ref[j:j + 1, :] * pltpu.roll(ext, B_CONV - 1 - j, 0)
    return y


def _dn_args(y, cur_ref, par_ref, ng_ref):
    nh = B_V_HEADS
    return ([y[:, h * B_HD:(h + 1) * B_HD] for h in range(B_QK_HEADS)],
            [y[:, B_QK + h * B_HD:B_QK + (h + 1) * B_HD] for h in range(B_QK_HEADS)],
            [y[:, 2 * B_QK + h * B_HD:2 * B_QK + (h + 1) * B_HD] for h in range(nh)],
            [cur_ref[:, BP_Z + h * B_HD:BP_Z + (h + 1) * B_HD] for h in range(nh)],
            [cur_ref[:, BP_GATE + h:BP_GATE + h + 1] for h in range(nh)],
            [cur_ref[:, BP_GATE + nh + h:BP_GATE + nh + h + 1] for h in range(nh)],
            [par_ref[:, h:h + 1] for h in range(nh)], [par_ref[:, nh + h:nh + h + 1] for h in range(nh)], ng_ref[...])


def _mix_b_fwd(proj, conv_w, par, ng, memkv, name):
    s = proj.shape[0]
    nc = s // CHUNK

    def body(cur_ref, prev_ref, w_ref, par_ref, ng_ref, memkv_ref, o_ref, st_ref, state_ref):
        n = pl.program_id(0)

        @pl.when(n == 0)
        def _():
            state_ref[...] = jnp.zeros_like(state_ref)

        prev = jnp.where(n > 0, prev_ref[...], 0.0)
        ext = jnp.concatenate([prev, cur_ref[:, :B_QKV]], axis=0)
        y = _dn_conv(ext, w_ref)[HALO:]
        s0 = [state_ref[hv] for hv in range(B_V_HEADS)]
        st_ref[0] = state_ref[...]
        outs, s1 = _dn_heads(*_dn_args(y, cur_ref, par_ref, ng_ref), s0)
        for hv in range(B_V_HEADS):
            state_ref[hv] = s1[hv]
        outs = outs + [_cross_pairs(cur_ref[:, BP_XQ:BP_XQ + X_Q], memkv_ref[:, :X_Q], memkv_ref[:, X_Q:])]
        o_ref[...] = jnp.concatenate(outs, axis=1).astype(o_ref.dtype)

    return pl.pallas_call(
        body, grid=(nc,),
        in_specs=[pl.BlockSpec((CHUNK, IN_BP), lambda n: (n, 0)),
                  pl.BlockSpec((HALO, B_QKV), lambda n: (jnp.maximum(n * (CHUNK // HALO) - 1, 0), 0)),
                  pl.BlockSpec((HALO, B_QKV), lambda n: (0, 0)),
                  pl.BlockSpec((1, 128), lambda n: (0, 0)), pl.BlockSpec((1, 128), lambda n: (0, 0)),
                  pl.BlockSpec((MEM_LEN, 2 * X_Q), lambda n: (0, 0))],
        out_specs=[pl.BlockSpec((CHUNK, D), lambda n: (n, 0)),
                   pl.BlockSpec((1, B_V_HEADS, B_HD, B_HD), lambda n: (n, 0, 0, 0))],
        out_shape=[_SDS((s, D), _ACT), _SDS((nc, B_V_HEADS, B_HD, B_HD), F32)],
        scratch_shapes=[pltpu.VMEM((B_V_HEADS, B_HD, B_HD), F32)],
        name=name, compiler_params=_cp("arbitrary"))(proj, proj, conv_w, par, ng, memkv)


def _mix_b_bwd(proj, conv_w, par, ng, memkv, states, dmix, name):
    s = proj.shape[0]
    nc = s // CHUNK
    ext_rows = CHUNK + HALO

    def body(cur_ref, prev_ref, w_ref, par_ref, ng_ref, memkv_ref, st_ref, dmix_ref,
             dproj_ref, dw_ref, dpar_ref, dng_ref, dmemkv_ref, dstate_ref, carry_ref):
        t = pl.program_id(0)
        n = nc - 1 - t

        @pl.when(t == 0)
        def _():
            dstate_ref[...] = jnp.zeros_like(dstate_ref)
            carry_ref[...] = jnp.zeros_like(carry_ref)
            dw_ref[...] = jnp.zeros_like(dw_ref)
            dpar_ref[...] = jnp.zeros_like(dpar_ref)
            dng_ref[...] = jnp.zeros_like(dng_ref)
            dmemkv_ref[...] = jnp.zeros_like(dmemkv_ref)

        lane = lax.broadcasted_iota(jnp.int32, (1, 128), 1)
        prev = jnp.where(n > 0, prev_ref[...], 0.0)
        ext = jnp.concatenate([prev, cur_ref[:, :B_QKV]], axis=0)
        y = _dn_conv(ext, w_ref)[HALO:]
        _, vjp = jax.vjp(_dn_heads, *_dn_args(y, cur_ref, par_ref, ng_ref), [st_ref[0, hv] for hv in range(B_V_HEADS)])
        dyq, dyk, dyv, dz, gbl, gal, ga_log, gdtb, dng, gs0 = vjp(
            ([dmix_ref[:, hv * B_HD:(hv + 1) * B_HD].astype(F32) for hv in range(B_V_HEADS)],
             [dstate_ref[hv] for hv in range(B_V_HEADS)]))
        dgate = jnp.zeros((CHUNK, 128), F32)
        dpar = jnp.zeros((1, 128), F32)
        for hv in range(B_V_HEADS):
            dstate_ref[hv] = gs0[hv]
            dgate = dgate + jnp.where(lane == hv, gbl[hv], 0.0) + jnp.where(lane == B_V_HEADS + hv, gal[hv], 0.0)
            dpar = dpar + jnp.where(lane == hv, ga_log[hv], 0.0) + jnp.where(lane == B_V_HEADS + hv, gdtb[hv], 0.0)
        dpar_ref[...] += dpar
        dng_ref[...] += dng
        _, vjp = jax.vjp(_cross_pairs, cur_ref[:, BP_XQ:BP_XQ + X_Q], memkv_ref[:, :X_Q], memkv_ref[:, X_Q:])
        dxq, dmk, dmv = vjp(dmix_ref[:, B_V:].astype(F32))
        dmemkv_ref[...] += jnp.concatenate([dmk, dmv], axis=1)
        dy = jnp.concatenate(list(dyq) + list(dyk) + list(dyv), axis=1)
        dy_ext = jnp.concatenate([jnp.zeros((HALO, B_QKV), F32), dy], axis=0)
        dext = dy_ext * w_ref[B_CONV - 1:B_CONV, :]
        dw_ref[B_CONV - 1:B_CONV, :] += jnp.sum(ext * dy_ext, axis=0, keepdims=True)
        for j in range(B_CONV - 1):
            sh = B_CONV - 1 - j
            dw_ref[j:j + 1, :] += jnp.sum(pltpu.roll(ext, sh, 0) * dy_ext, axis=0, keepdims=True)
            dext = dext + w_ref[j:j + 1, :] * pltpu.roll(dy_ext, ext_rows - sh, 0)
        tail = jnp.concatenate([jnp.zeros((CHUNK - HALO, B_QKV), F32), carry_ref[...]], axis=0)
        dqkv = dext[HALO:] + tail
        carry_ref[...] = dext[:HALO]
        dproj_ref[...] = jnp.concatenate([dqkv] + list(dz) + [dxq, dgate], axis=1).astype(dproj_ref.dtype)

    return pl.pallas_call(
        body, grid=(nc,),
        in_specs=[pl.BlockSpec((CHUNK, IN_BP), lambda t: (nc - 1 - t, 0)),
                  pl.BlockSpec((HALO, B_QKV), lambda t: (jnp.maximum((nc - 1 - t) * (CHUNK // HALO) - 1, 0), 0)),
                  pl.BlockSpec((HALO, B_QKV), lambda t: (0, 0)),
                  pl.BlockSpec((1, 128), lambda t: (0, 0)), pl.BlockSpec((1, 128), lambda t: (0, 0)),
                  pl.BlockSpec((MEM_LEN, 2 * X_Q), lambda t: (0, 0)),
                  pl.BlockSpec((1, B_V_HEADS, B_HD, B_HD), lambda t: (nc - 1 - t, 0, 0, 0)),
                  pl.BlockSpec((CHUNK, D), lambda t: (nc - 1 - t, 0))],
        out_specs=[pl.BlockSpec((CHUNK, IN_BP), lambda t: (nc - 1 - t, 0)),
                   pl.BlockSpec((HALO, B_QKV), lambda t: (0, 0)),
                   pl.BlockSpec((1, 128), lambda t: (0, 0)), pl.BlockSpec((1, 128), lambda t: (0, 0)),
                   pl.BlockSpec((MEM_LEN, 2 * X_Q), lambda t: (0, 0))],
        out_shape=[_SDS((s, IN_BP), _ACT), _SDS((HALO, B_QKV), F32), _SDS((1, 128), F32), _SDS((1, 128), F32),
                   _SDS((MEM_LEN, 2 * X_Q), F32)],
        scratch_shapes=[pltpu.VMEM((B_V_HEADS, B_HD, B_HD), F32), pltpu.VMEM((HALO, B_QKV), F32)],
        name=name, compiler_params=_cp("arbitrary"))(proj, proj, conv_w, par, ng, memkv, states, dmix)


def _place():
    return lax.axis_index("x"), lax.axis_index("y"), lax.axis_index("c")


def _all_gather(shards, name):
    n = len(shards)

    def body(*refs):
        ins, outs = refs[:n], refs[n:2 * n]
        send_sems, recv_sems, local_sems = refs[2 * n:]
        x, y, c = _place()
        me, sibling = (x, y, c), (x, y, 1 - c)
        chips = [(1 - x, y), (x, 1 - y), (1 - x, 1 - y)]

        def rows(a, px, py, pc):
            return outs[a].at[4 * px + 2 * py + pc]

        def copy(a, k, block, to, src=None):
            return pltpu.make_async_remote_copy(
                src_ref=rows(a, *block) if src is None else src, dst_ref=rows(a, *block),
                send_sem=send_sems.at[a, k], recv_sem=recv_sems.at[a, k],
                device_id=to, device_id_type=pl.DeviceIdType.MESH)

        mine = [pltpu.make_async_copy(ins[a], rows(a, *me), local_sems.at[a]) for a in range(n)]
        for cp in mine:
            cp.start()
        first = []
        for a in range(n):
            first.append(copy(a, 0, me, sibling, src=ins[a]))
            first += [copy(a, 1 + j, me, (*chip, c), src=ins[a]) for j, chip in enumerate(chips)]
        for cp in first:
            cp.start()
        passed = []
        for j, chip in enumerate(chips):
            for a in range(n):
                copy(a, 1 + j, (*chip, c), me).wait_recv()
                fwd = copy(a, 4 + j, (*chip, c), sibling)
                fwd.start()
                passed.append(fwd)
        for a in range(n):
            copy(a, 0, sibling, me).wait_recv()
            for j, chip in enumerate(chips):
                copy(a, 4 + j, (*chip, 1 - c), me).wait_recv()
        for cp in first + passed:
            cp.wait_send()
        for cp in mine:
            cp.wait()

    hbm = pl.BlockSpec(memory_space=pl.ANY)
    return pl.pallas_call(
        body, out_shape=[_SDS((N_DEV,) + s.shape, s.dtype) for s in shards],
        in_specs=[hbm] * n, out_specs=[hbm] * n,
        scratch_shapes=[pltpu.SemaphoreType.DMA((n, 7)), pltpu.SemaphoreType.DMA((n, 7)), pltpu.SemaphoreType.DMA((n,))],
        name=name)(*shards)


class _Sends:
    def __init__(self, plan, sems, srcs, lands, token):
        self.plan, self.sems, self.srcs, self.lands, self.token = plan, sems, srcs, lands, token


def _send_refs(plan, src_refs, land_refs, x, y, c):
    my = 4 * x + 2 * y + c

    def src_for(a, dest):
        return src_refs[a].at[dest] if plan[a][1] else src_refs[a]

    def slot(a, source):
        return land_refs[plan[a][0]].at[source]

    return my, src_for, slot


def _send_start(srcs, land_shapes, plan, name):
    n, nl = len(srcs), len(land_shapes)

    def body(*refs):
        src_refs, land_refs = refs[:n], refs[n:n + nl]
        send_sems, recv_sems = refs[n + nl], refs[n + nl + 1]
        token, local_sems = refs[2 * (n + nl) + 2], refs[2 * (n + nl) + 3]
        x, y, c = _place()
        my, src_for, slot = _send_refs(plan, src_refs, land_refs, x, y, c)
        mine = [pltpu.make_async_copy(src_for(a, my), slot(a, my), local_sems.at[a]) for a in range(n)]
        for cp in mine:
            cp.start()
        for k in range(N_DEV - 1):
            px, py, pc = x ^ ((k + 1) >> 2 & 1), y ^ ((k + 1) >> 1 & 1), c ^ ((k + 1) & 1)
            for a in range(n):
                pltpu.make_async_remote_copy(
                    src_ref=src_for(a, 4 * px + 2 * py + pc), dst_ref=slot(a, my), send_sem=send_sems.at[a * (N_DEV - 1) + k],
                    recv_sem=recv_sems.at[a * (N_DEV - 1) + k], device_id=(px, py, pc), device_id_type=pl.DeviceIdType.MESH).start()
        for cp in mine:
            cp.wait()
        token[...] = jnp.zeros_like(token)

    hbm = pl.BlockSpec(memory_space=pltpu.HBM)
    sem = pl.BlockSpec(memory_space=pltpu.SEMAPHORE)
    lands = [pltpu.with_memory_space_constraint(lax.empty(s, d), pltpu.HBM) for s, d in land_shapes]
    srcs = [pltpu.with_memory_space_constraint(s, pltpu.HBM) for s in srcs]
    out = pl.pallas_call(
        body, name=name,
        out_shape=(pltpu.SemaphoreType.DMA((n * (N_DEV - 1),)), pltpu.SemaphoreType.DMA((n * (N_DEV - 1),)),
                   *[pltpu.HBM(s.shape, s.dtype) for s in srcs], *[pltpu.HBM(s, d) for s, d in land_shapes],
                   _SDS((8, 128), F32)),
        in_specs=[hbm] * (n + nl),
        out_specs=(sem, sem, *[hbm] * (n + nl), pl.BlockSpec(memory_space=pltpu.VMEM)),
        input_output_aliases={i: 2 + i for i in range(n + nl)},
        scratch_shapes=[pltpu.SemaphoreType.DMA((n,))],
        compiler_params=pltpu.CompilerParams(has_side_effects=pltpu.SideEffectType.DATAFLOW_SIDE_EFFECTING),
    )(*srcs, *lands)
    return _Sends(plan, out[:2], out[2:2 + n], out[2 + n:2 + n + nl], out[-1])


def _send_wait(sends, after, name):
    after = list(after) if isinstance(after, (list, tuple)) else [after]
    plan = sends.plan
    n, nl = len(sends.srcs), len(sends.lands)

    def body(*refs):
        src_refs, land_refs = refs[:n], refs[n:n + nl]
        send_sems, recv_sems = refs[n + nl], refs[n + nl + 1]
        x, y, c = _place()
        my, src_for, slot = _send_refs(plan, src_refs, land_refs, x, y, c)
        for k in range(N_DEV - 1):
            px, py, pc = x ^ ((k + 1) >> 2 & 1), y ^ ((k + 1) >> 1 & 1), c ^ ((k + 1) & 1)
            peer = 4 * px + 2 * py + pc
            for a in range(n):
                cp = pltpu.make_async_remote_copy(
                    src_ref=src_for(a, peer), dst_ref=slot(a, peer), send_sem=send_sems.at[a * (N_DEV - 1) + k],
                    recv_sem=recv_sems.at[a * (N_DEV - 1) + k], device_id=(px, py, pc), device_id_type=pl.DeviceIdType.MESH)
                cp.wait_send()
                cp.wait_recv()

    hbm = pl.BlockSpec(memory_space=pltpu.HBM)
    sem = pl.BlockSpec(memory_space=pltpu.SEMAPHORE)
    out = pl.pallas_call(
        body, name=name,
        out_shape=tuple(pltpu.HBM(s.shape, s.dtype) for s in (*sends.srcs, *sends.lands)),
        in_specs=[hbm] * (n + nl) + [sem, sem] + [pl.BlockSpec(memory_space=pl.ANY)] * len(after),
        out_specs=tuple([hbm] * (n + nl)),
        input_output_aliases={i: i for i in range(n + nl)},
        compiler_params=pltpu.CompilerParams(has_side_effects=pltpu.SideEffectType.DATAFLOW_SIDE_EFFECTING),
    )(*sends.srcs, *sends.lands, *sends.sems, *after)
    return list(out[n:])


class _Exchange:
    def __init__(self, lands, srcs):
        self.lands, self.srcs = lands, srcs


def _seq_exchange(srcs, land_shapes, plan, name, cid):
    n, nl = len(srcs), len(land_shapes)

    def launch(*refs):
        src_refs, land_refs = refs[:n], refs[n:n + nl]
        send_sems, recv_sems, local_sems = refs[n + nl:]
        x, y, c = _place()
        my = 4 * x + 2 * y + c
        peers = [(x ^ ((k + 1) >> 2 & 1), y ^ ((k + 1) >> 1 & 1), c ^ ((k + 1) & 1)) for k in range(N_DEV - 1)]
        barrier = pltpu.get_barrier_semaphore()
        for p in peers:
            pl.semaphore_signal(barrier, inc=1, device_id=p, device_id_type=pl.DeviceIdType.MESH)
        pl.semaphore_wait(barrier, N_DEV - 1)

        def src_for(a, dest):
            return src_refs[a].at[dest] if plan[a][1] else src_refs[a]

        def slot(a, source):
            return land_refs[plan[a][0]].at[source]

        mine = [pltpu.make_async_copy(src_for(a, my), slot(a, my), local_sems.at[a]) for a in range(n)]
        for cp in mine:
            cp.start()
        sends, recvs = [], []
        for k, (px, py, pc) in enumerate(peers):
            peer = 4 * px + 2 * py + pc
            for a in range(n):
                kw = dict(send_sem=send_sems.at[a * (N_DEV - 1) + k], recv_sem=recv_sems.at[a * (N_DEV - 1) + k],
                          device_id=(px, py, pc), device_id_type=pl.DeviceIdType.MESH)
                sends.append(pltpu.make_async_remote_copy(src_ref=src_for(a, peer), dst_ref=slot(a, my), **kw))
                recvs.append(pltpu.make_async_remote_copy(src_ref=src_for(a, my), dst_ref=slot(a, peer), **kw))
        for cp in sends:
            cp.start()
        for cp in recvs:
            cp.wait_recv()
        for cp in sends:
            cp.wait_send()
        for cp in mine:
            cp.wait()

    lands = pl.kernel(
        launch, out_type=[_SDS(s, d) for s, d in land_shapes],
        mesh=plsc.ScalarSubcoreMesh(axis_name="sequencer", num_cores=1), name=name,
        scratch_types=(pltpu.SemaphoreType.DMA((n * (N_DEV - 1),)), pltpu.SemaphoreType.DMA((n * (N_DEV - 1),)),
                       pltpu.SemaphoreType.DMA((n,))),
        compiler_params=pltpu.CompilerParams(collective_id=cid))(*srcs)
    return _Exchange(list(lands), list(srcs))


def _adam_update(g, w, m, v):
    c1 = 1.0 - ADAM_B1 ** ADAM_STEP
    c2 = 1.0 - ADAM_B2 ** ADAM_STEP
    mm = ADAM_B1 * m + (1.0 - ADAM_B1) * g
    vv = ADAM_B2 * v + (1.0 - ADAM_B2) * (g * g)
    delta = -ADAM_LR * ((mm / c1) / (jnp.sqrt(vv / c2) + ADAM_EPS) + ADAM_WD * w)
    return delta, mm, vv


def _sum_sources(p_ref):
    g = p_ref[0].astype(F32)
    for s in range(1, N_DEV):
        g = g + p_ref[s].astype(F32)
    return g


def _adamw(parts, w, m, v, tr, name, restore_b=False, deps=()):
    nl, r, c = w.shape
    cp = parts[0].shape[-1]

    def body(*refs):
        p_refs = refs[:nl]
        w_ref, m_ref, v_ref = refs[nl:nl + 3]
        g_ref, d_ref, nm_ref, nv_ref = refs[-4:]
        g = _sum_sources(p_refs[0])
        for l in range(1, nl):
            g = jnp.where(pl.program_id(0) == l, _sum_sources(p_refs[l]), g)
        if restore_b:
            g = jnp.concatenate([g[:, :BP_XQ], g[:, BP_GATE:BP_GATE + 2 * B_V_HEADS], g[:, BP_XQ:BP_GATE]], axis=1)
        delta, mm, vv = _adam_update(g, w_ref[...], m_ref[...], v_ref[...])
        g_ref[...] = g
        d_ref[...] = delta
        nm_ref[...] = mm
        nv_ref[...] = vv

    spec = pl.BlockSpec((None, tr, c), lambda l, i: (l, i, 0))
    part_specs = [pl.BlockSpec((N_DEV, tr, cp), functools.partial(lambda l, i, k: (0, jnp.where(l == k, i, 0), 0), k=k))
                  for k in range(nl)]
    return pl.pallas_call(
        body, grid=(nl, r // tr),
        in_specs=part_specs + [spec, spec, spec] + _dep_specs(deps),
        out_specs=[spec] * 4, out_shape=[_SDS(w.shape, F32)] * 4,
        name=name, compiler_params=_cp("arbitrary", "arbitrary"))(*parts, w, m, v, *deps)


def _pack_small(d_rel, d_cb, d_cw, d_qkv, d_mix, d_mem, d_ffn, d_final, d_sinks, d_par, d_ng, name):
    flat = [d_rel, *d_cb, *d_cw, d_qkv, *d_mix, *d_mem, *d_ffn, d_final, d_sinks, d_par, d_ng]
    n = len(flat)

    def body(*refs):
        ins, o_ref = refs[:n], refs[n]
        rel, cb0, cb1, cw0, cw1, qkv, mx0, mx1, me0, me1, ff0, ff1, fin, snk, par, ng = ins
        o_ref[...] = jnp.zeros_like(o_ref)
        o_ref[SP_REL:SP_REL + N_BUCKETS, 0:128] = rel[...]
        for l, (cb, cw) in enumerate(((cb0, cw0), (cb1, cw1))):
            o_ref[SP_CB + l:SP_CB + l + 1, :] = jnp.concatenate([cb[j] for j in range(FF_BLOCKS)], axis=1)
            full = jnp.concatenate([cw[j] for j in range(FF_BLOCKS)], axis=1)
            o_ref[SP_CW + FFN_CONV * l:SP_CW + FFN_CONV * (l + 1), :] = full[:FFN_CONV]
        o_ref[SP_QKV:SP_QKV + B_CONV, 0:B_QKV] = qkv[0:B_CONV, :]
        for base, pair in ((SP_MIX, (mx0, mx1)), (SP_MEM, (me0, me1)), (SP_FFN, (ff0, ff1))):
            for l in range(2):
                o_ref[base + l:base + l + 1, 0:D] = pair[l][...]
        o_ref[SP_FINAL:SP_FINAL + 1, 0:D] = fin[...]
        o_ref[SP_MISC:SP_MISC + 1, 0:128] = snk[...]
        o_ref[SP_MISC:SP_MISC + 1, 128:256] = par[...]
        o_ref[SP_MISC:SP_MISC + 1, 256:384] = ng[...]

    vm = pl.BlockSpec(memory_space=pltpu.VMEM)
    return pl.pallas_call(body, in_specs=[vm] * n, out_specs=vm, out_shape=_SDS((SMALL_ROWS, D_FF), F32), name=name)(*flat)


_SMALL = ["rel_bias", "norm_mix_g", "norm_mem_g", "sinks_a", "a_log_b", "dt_bias_b", "out_norm_g_b", "norm_ffn_g",
          "ffn_conv_b", "final_norm_g", "conv_qkv_b", "ffn_conv_w"]


def _adamw_small(recv, rc_qkv, rc_ffn, ws, ms, vs, name, deps=()):
    n = len(_SMALL)

    def body(*refs):
        recv_ref, qkv_ref, ffn_ref = refs[:3]
        w_refs, m_refs, v_refs = refs[3:3 + n], refs[3 + n:3 + 2 * n], refs[3 + 2 * n:3 + 3 * n]
        outs = refs[len(refs) - 4 * n:]
        gs = _sum_sources(recv_ref)
        grads = {
            "rel_bias": gs[SP_REL:SP_REL + N_BUCKETS, 0:A_HEADS],
            "norm_mix_g": gs[SP_MIX:SP_MIX + 2, 0:D], "norm_mem_g": gs[SP_MEM:SP_MEM + 2, 0:D],
            "sinks_a": gs[SP_MISC:SP_MISC + 1, 0:A_HEADS],
            "a_log_b": gs[SP_MISC:SP_MISC + 1, 128:128 + B_V_HEADS],
            "dt_bias_b": gs[SP_MISC:SP_MISC + 1, 128 + B_V_HEADS:128 + 2 * B_V_HEADS],
            "out_norm_g_b": gs[SP_MISC:SP_MISC + 1, 256:256 + B_HD],
            "norm_ffn_g": gs[SP_FFN:SP_FFN + 2, 0:D], "ffn_conv_b": gs[SP_CB:SP_CB + 2, :],
            "final_norm_g": gs[SP_FINAL:SP_FINAL + 1, 0:D],
            "conv_qkv_b": _sum_sources(qkv_ref), "ffn_conv_w": _sum_sources(ffn_ref),
        }
        for i, nm in enumerate(_SMALL):
            g = grads[nm]
            delta, mm, vv = _adam_update(g, w_refs[i][...], m_refs[i][...], v_refs[i][...])
            outs[i][...] = g
            outs[n + i][...] = delta
            outs[2 * n + i][...] = mm
            outs[3 * n + i][...] = vv

    vm = pl.BlockSpec(memory_space=pltpu.VMEM)
    shapes = [_SDS(w.shape, F32) for w in ws]
    return pl.pallas_call(
        body, in_specs=[vm] * (3 + 3 * n) + _dep_specs(deps), out_specs=[vm] * (4 * n), out_shape=shapes * 4,
        name=name)(recv, rc_qkv, rc_ffn, *ws, *ms, *vs, *deps)


def _assemble(gathered, axis):
    g = jnp.moveaxis(gathered, 0, axis)
    shp = list(g.shape)
    return g.reshape(shp[:axis] + [shp[axis] * shp[axis + 1]] + shp[axis + 2:])


def _pad_rows(a, rows):
    return jnp.pad(a, ((0, rows - a.shape[0]), (0, 0)))


def _pad_lanes(a, lanes=128):
    return jnp.pad(a, ((0, 0), (0, lanes - a.shape[1])))


def _ff_blocks(a):
    return jnp.moveaxis(a.reshape(a.shape[0], FF_BLOCKS, GU_SHARD), 1, 0)


def _reorder_b(w):
    qkv_z = w[..., :B_QKV + B_V]
    gates = w[..., B_QKV + B_V:B_QKV + B_V + 2 * B_V_HEADS]
    xq = w[..., IN_B - X_Q:]
    pad = jnp.zeros(w.shape[:-1] + (IN_BP - IN_B,), w.dtype)
    return jnp.concatenate([qkv_z, xq, gates, pad], axis=-1)


def kernel(x, mem, rel_bias, norm_mix_g, norm_mem_g, w_mem_kv, w_out, w_in_a, sinks_a, w_in_b, conv_qkv_b, a_log_b, dt_bias_b, out_norm_g_b, norm_ffn_g, w_gate_up, ffn_conv_w, ffn_conv_b, w_down, final_norm_g, loss_target, m_rel_bias, m_norm_mix_g, m_norm_mem_g, m_w_mem_kv, m_w_out, m_w_in_a, m_sinks_a, m_w_in_b, m_conv_qkv_b, m_a_log_b, m_dt_bias_b, m_out_norm_g_b, m_norm_ffn_g, m_w_gate_up, m_ffn_conv_w, m_ffn_conv_b, m_w_down, m_final_norm_g, v_rel_bias, v_norm_mix_g, v_norm_mem_g, v_w_mem_kv, v_w_out, v_w_in_a, v_sinks_a, v_w_in_b, v_conv_qkv_b, v_a_log_b, v_dt_bias_b, v_out_norm_g_b, v_norm_ffn_g, v_w_gate_up, v_ffn_conv_w, v_ffn_conv_b, v_w_down, v_final_norm_g):
    local = dict(locals())
    order = ["rel_bias", "norm_mix_g", "norm_mem_g", "w_mem_kv", "w_out", "w_in_a", "sinks_a", "w_in_b", "conv_qkv_b",
             "a_log_b", "dt_bias_b", "out_norm_g_b", "norm_ffn_g", "w_gate_up", "ffn_conv_w", "ffn_conv_b", "w_down",
             "final_norm_g"]
    wts = {n: local[n] for n in order}
    moms = {n: local["m_" + n] for n in order}
    vars_ = {n: local["v_" + n] for n in order}
    h0 = x[0]
    memx = mem[0]
    tgt = loss_target[0]
    s = h0.shape[0]
    tm = _rows(s)

    g_mk, g_out, g_ia, g_cq, g_cw = _all_gather(
        [w_mem_kv.astype(_MXU), w_out.astype(_MXU), w_in_a.astype(_MXU), conv_qkv_b, ffn_conv_w], "gather_first")
    gu_land = ((N_DEV, D, GU_SHARD), _MXU)
    dn_land = ((N_DEV, DN_SHARD, D), _MXU)
    whole = [(0, False), (1, False)]
    ffn0_w = _seq_exchange([w_gate_up[0].astype(_MXU), w_down[0].astype(_MXU)], [gu_land, dn_land], whole, "gather_ffn0", 1)
    w_ia = _assemble(g_ia, 2)[0]
    conv_qkv = _pad_rows(_assemble(g_cq, 2)[0], HALO)
    ffn_cw_full = _assemble(g_cw, 2)
    ffn_cw = [_ff_blocks(_pad_rows(ffn_cw_full[i], HALO)) for i in range(2)]
    ffn_cb = [_ff_blocks(ffn_conv_b[i:i + 1]) for i in range(2)]
    bucket = jnp.asarray(_bucket_table())
    bias = _bias_build(rel_bias, bucket, "bias_build")
    sinks = _pad_lanes(sinks_a)
    par_b = _pad_lanes(jnp.concatenate([a_log_b, dt_bias_b], axis=1))

    row_x = pl.BlockSpec((tm, D), lambda i, j: (i, 0))
    gu_shape = (2, FF_BLOCKS, s, GU_SHARD)

    def in_proj(h, g, w, w_spec, n_cols, tn, name, deps=(), out_dtype=F32):
        return _norm_matmul(h, g, w, w_spec, n_cols // tn, (h.shape[0], n_cols),
                            pl.BlockSpec((_rows(h.shape[0]), tn), lambda i, j: (i, j)), name, deps=deps, out_dtype=out_dtype)

    def ffn_fwd(i, h, g_gu, g_dn, deps=()):
        gu, hn = _norm_matmul(h, norm_ffn_g[i:i + 1], g_gu, _spec_gate_up(1), N_DEV, gu_shape,
                              _spec_gu_act(0, 1, tm), f"gate_up_{i}", deps=deps, out_dtype=_ACT)
        h_new, act = _glu_down(gu, ffn_cw[i], ffn_cb[i], g_dn, h, f"glu_down_{i}")
        return h_new, gu, hn, act

    def out_proj(i, mix, h):
        return _matmul_res(mix, row_x, g_out, _spec_rowsharded(i, D // N_DEV, D), 1, h, f"out_proj_{i}")

    proj_a, hn_a = in_proj(h0, norm_mix_g[0:1], w_ia, pl.BlockSpec((D, 640), lambda i, j: (0, j)), IN_A, 640, "in_proj_a",
                           deps=ffn0_w.srcs, out_dtype=_ACT)
    memkv0, memn0 = in_proj(memx, norm_mem_g[0:1], g_mk, _spec_rowsharded(0, D // N_DEV, 2 * X_Q), 2 * X_Q, 2 * X_Q, "mem_proj_0")
    mix_a = _mix_a_fwd(proj_a, bias, sinks, memkv0, "mix_a_fwd")
    h1 = out_proj(0, mix_a, h0)
    g_gu0, g_dn0 = ffn0_w.lands

    def after(a, b):
        return a + (b.reshape(-1)[0] * 0).astype(a.dtype)

    in_b_w = _seq_exchange([after(_reorder_b(w_in_b).astype(_MXU), h1)], [((N_DEV, 1, D // N_DEV, IN_BP), _MXU)], [(0, False)],
                           "gather_in_b", 2)
    ffn1_w = _seq_exchange([after(w_gate_up[1].astype(_MXU), h1), after(w_down[1].astype(_MXU), h1)], [gu_land, dn_land], whole,
                           "gather_ffn1", 3)
    h2, gu0, hn_f0, act0 = ffn_fwd(0, h1, g_gu0, g_dn0, deps=in_b_w.srcs + ffn1_w.srcs)
    g_ib, = in_b_w.lands
    proj_b, hn_b = in_proj(h2, norm_mix_g[1:2], g_ib, _spec_rowsharded(0, D // N_DEV, 896, col_block=1), IN_BP, 896, "in_proj_b")
    memkv1, memn1 = in_proj(memx, norm_mem_g[1:2], g_mk, _spec_rowsharded(1, D // N_DEV, 2 * X_Q), 2 * X_Q, 2 * X_Q, "mem_proj_1")
    mix_b, states = _mix_b_fwd(proj_b, conv_qkv, par_b, out_norm_g_b, memkv1, "mix_b_fwd")
    h3 = out_proj(1, mix_b, h2)
    g_gu1, g_dn1 = ffn1_w.lands
    h4, gu1, hn_f1, act1 = ffn_fwd(1, h3, g_gu1, g_dn1)
    loss_row, dh, d_final_g = _loss_head(h4, final_norm_g[None, :], tgt, "loss_head")

    zeros_mem = jnp.zeros_like(memx)
    per_dest2 = [(0, True), (1, True)]

    def ffn_bwd(i, dh, h_in, gu, hn_f, act, g_gu, g_dn, deps=()):
        dgu, d_cw, d_cb = _glu_bwd(gu, ffn_cw[i], ffn_cb[i], dh, g_dn, f"glu_bwd_{i}", deps=deps)
        d_wdown = _matmul_tn(act, pl.BlockSpec((None, tm, GU_SHARD), lambda j, r: (j, r, 0)),
                             dh, pl.BlockSpec((tm, D), lambda j, r: (r, 0)), s, FF_BLOCKS, (GU_SHARD, D),
                             (N_DEV, DN_SHARD, D), pl.BlockSpec((2, DN_SHARD, D), lambda j, r: (j, 0, 0)), f"d_w_down_{i}")
        dh_new, d_g = _matmul_nt_normbwd(dgu, _spec_gu_act(0, 1, tm), g_gu, _spec_gate_up(1), N_DEV, h_in,
                                         norm_ffn_g[i:i + 1], dh, f"d_ffn_in_{i}")
        d_wgu = _matmul_tn(hn_f, pl.BlockSpec((tm, D), lambda j, r: (r, 0)), dgu, _spec_gu_act(1, 0, tm), s, N_DEV,
                           (D, GU_SHARD), (N_DEV, D, GU_SHARD), pl.BlockSpec((None, D, GU_SHARD), lambda j, r: (j, 0, 0)),
                           f"d_w_gate_up_{i}")
        sent = _seq_exchange([d_wdown, d_wgu], [((N_DEV, DN_SHARD, D), _WIRE), ((N_DEV, D, GU_SHARD), _WIRE)], per_dest2,
                             f"send_ffn{i}_grads", 4 + i)
        return dh_new, sent, d_cw, d_cb, d_g

    def out_bwd(i, dh, mix, deps):
        dmix = _matmul_nt(dh, g_out, _spec_rowsharded(i, D // N_DEV, D), 1, (s, D), row_x, f"d_mix_{i}", deps=deps, out_dtype=_ACT)
        d_wout = _matmul_tn(mix, pl.BlockSpec((tm, D), lambda j, r: (r, 0)), dh, pl.BlockSpec((tm, D), lambda j, r: (r, 0)),
                            s, 1, (D, D), (N_DEV, D // N_DEV, D), pl.BlockSpec((N_DEV, D // N_DEV, D), lambda j, r: (0, 0, 0)),
                            f"d_w_out_{i}")
        return dmix, d_wout

    def mem_bwd(i, dmemkv, memn):
        tmm = _rows(MEM_LEN)
        _, d_g = _matmul_nt_normbwd(dmemkv, pl.BlockSpec((tmm, 2 * X_Q), lambda r, j: (r, 0)), g_mk,
                                    _spec_rowsharded(i, D // N_DEV, 2 * X_Q), 1, memx, norm_mem_g[i:i + 1], zeros_mem,
                                    f"d_mem_in_{i}")
        by_row = lambda j, r: (r, 0)
        d_w = _matmul_tn(memn, pl.BlockSpec((tmm, D), by_row), dmemkv, pl.BlockSpec((tmm, 2 * X_Q), by_row), MEM_LEN, 1,
                         (D, 2 * X_Q), (N_DEV, D // N_DEV, 2 * X_Q),
                         pl.BlockSpec((N_DEV, D // N_DEV, 2 * X_Q), lambda j, r: (0, 0, 0)), f"d_w_mem_kv_{i}")
        return d_w, d_g

    out_land = ((N_DEV, D // N_DEV, D), _WIRE)
    mk_land = ((N_DEV, D // N_DEV, 2 * X_Q), _WIRE)
    dh, ffn1_g, d_cw1, d_cb1, d_gf1 = ffn_bwd(1, dh, h3, gu1, hn_f1, act1, g_gu1, g_dn1)
    dmix, d_wout1 = out_bwd(1, dh, mix_b, ffn1_g.srcs)
    dproj_b, d_convw, d_par, d_ng, dmemkv1 = _mix_b_bwd(proj_b, conv_qkv, par_b, out_norm_g_b, memkv1, states, dmix, "mix_b_bwd")
    dh, d_gm1 = _matmul_nt_normbwd(dproj_b, pl.BlockSpec((tm, 896), lambda i, j: (i, j)), g_ib,
                                   _spec_rowsharded(0, D // N_DEV, 896, col_block=1), IN_BP // 896, h2, norm_mix_g[1:2], dh, "d_in_b")
    d_wib = _matmul_tn(hn_b, pl.BlockSpec((tm, D), lambda j, r: (r, 0)), dproj_b, pl.BlockSpec((tm, 896), lambda j, r: (r, j)),
                       s, IN_BP // 896, (D, 896), (N_DEV, D // N_DEV, IN_BP),
                       pl.BlockSpec((N_DEV, D // N_DEV, 896), lambda j, r: (0, 0, j)), "d_w_in_b")
    d_wmk1, d_gmem1 = mem_bwd(1, dmemkv1, memn1)
    mix1_g = _seq_exchange([d_wout1, d_wib, d_wmk1], [out_land, ((N_DEV, D // N_DEV, IN_BP), _WIRE), mk_land],
                           [(0, True), (1, True), (2, True)], "send_mix1_grads", 6)
    dh, ffn0_g, d_cw0, d_cb0, d_gf0 = ffn_bwd(0, dh, h1, gu0, hn_f0, act0, g_gu0, g_dn0, deps=mix1_g.srcs)
    dmix, d_wout0 = out_bwd(0, dh, mix_a, ffn0_g.srcs + ffn1_g.lands[:1])
    dproj_a, dbias, dsinks, dmemkv0 = _mix_a_bwd(proj_a, bias, sinks, memkv0, dmix, "mix_a_bwd")
    dh, d_gm0 = _matmul_nt_normbwd(dproj_a, pl.BlockSpec((tm, 640), lambda i, j: (i, j)), w_ia,
                                   pl.BlockSpec((D, 640), lambda i, j: (0, j)), IN_A // 640, h0, norm_mix_g[0:1], dh, "d_in_a")
    d_wia = _matmul_tn(hn_a, pl.BlockSpec((tm, D), lambda j, r: (r, 0)), dproj_a, pl.BlockSpec((tm, IN_A), lambda j, r: (r, 0)),
                       s, 1, (D, IN_A), (N_DEV, D, IA_SHARD), pl.BlockSpec((N_DEV, D, IA_SHARD), lambda j, r: (0, 0, 0)),
                       "d_w_in_a", split=IA_SHARD)
    d_wmk0, d_gmem0 = mem_bwd(0, dmemkv0, memn0)
    d_rel = _bias_reduce(dbias, bucket, "bias_reduce")
    small = _pack_small(d_rel, (d_cb0, d_cb1), (d_cw0, d_cw1), d_convw, (d_gm0, d_gm1), (d_gmem0, d_gmem1),
                        (d_gf0, d_gf1), d_final_g, dsinks, d_par, d_ng, "pack_small")
    mix0_g = _seq_exchange([d_wout0, d_wia, d_wmk0, small],
                           [out_land, ((N_DEV, D, IA_SHARD), _WIRE), mk_land, ((N_DEV, SMALL_ROWS, D_FF), F32)],
                           [(0, True), (1, True), (2, True), (3, False)], "send_mix0_grads", 7)

    res = {}
    last = []

    def update(nm, parts, tr, restore=False):
        res[nm] = _adamw(parts, wts[nm], moms[nm], vars_[nm], tr, "adamw_" + nm, restore_b=restore, deps=last[-1:])
        last.append(res[nm][1])

    r_dn1, r_gu1 = ffn1_g.lands
    r_dn0, r_gu0 = ffn0_g.lands
    r_out1, r_ib, r_mk1 = mix1_g.lands
    update("w_gate_up", [r_gu0, r_gu1], 128)
    update("w_down", [r_dn0, r_dn1], 176)
    update("w_in_b", [r_ib], 32, True)
    r_out0, r_ia, r_mk0, r_small = mix0_g.lands
    update("w_mem_kv", [r_mk0, r_mk1], 128)
    update("w_out", [r_out0, r_out1], 128)
    update("w_in_a", [r_ia], 512)

    my = 4 * lax.axis_index("x") + 2 * lax.axis_index("y") + lax.axis_index("c")
    cq = conv_qkv_b.shape[-1]
    cf = ffn_conv_w.shape[-1]
    rc_qkv = lax.dynamic_slice_in_dim(r_small[:, SP_QKV:SP_QKV + B_CONV, :B_QKV], my * cq, cq, axis=2)[:, None]
    rc_ffn = lax.dynamic_slice_in_dim(r_small[:, SP_CW:SP_CW + 2 * FFN_CONV, :], my * cf, cf, axis=2).reshape(N_DEV, 2, FFN_CONV, cf)
    as2d = lambda a: a[None, :] if a.ndim == 1 else a
    small_out = _adamw_small(r_small, rc_qkv, rc_ffn, [as2d(wts[n]) for n in _SMALL], [as2d(moms[n]) for n in _SMALL],
                             [as2d(vars_[n]) for n in _SMALL], "adamw_small", deps=last[-1:])
    ns = len(_SMALL)
    for i, nm in enumerate(_SMALL):
        res[nm] = [small_out[k * ns + i].reshape(wts[nm].shape) for k in range(4)]

    loss = lax.psum(loss_row[0, 0], AXES)
    return (loss, dh[None], *[res[n][0] for n in order], *[res[n][1] for n in order],
            *[res[n][2] for n in order], *[res[n][3] for n in order])
```

```python
import functools
import math

import numpy as np

import jax
import jax.numpy as jnp
from jax import lax
from jax.experimental import pallas as pl
from jax.experimental.pallas import tpu as pltpu
from jax.experimental.pallas import tpu_sc as plsc

F32 = jnp.float32
_MXU = jnp.bfloat16
_ACT = jnp.bfloat16
_WIRE = jnp.bfloat16
_HI = lax.Precision.HIGH
_TM = 1024
_TM_GLU = 512
_VMEM_LIMIT = 48 * 1024 * 1024
_SDS = jax.ShapeDtypeStruct

D = 1024
EPS = 1e-6
A_HEADS, A_KV_HEADS, A_HD, BLK = 12, 2, 64, 128
N_BUCKETS, MAX_DISTANCE = 32, 128
B_QK_HEADS, B_V_HEADS, B_HD, B_CONV, CHUNK = 3, 6, 128, 4, 64
X_HEADS, X_HD, MEM_LEN = 4, 64, 256
D_FF, FFN_CONV = 2816, 3
A_Q, A_KV, X_Q = 768, 128, 256
B_QK, B_V, B_QKV = 384, 768, 1536
IN_A, IN_B = 1280, 2572
IN_BP = 2688
BP_Z, BP_XQ, BP_GATE = 1536, 2304, 2560
HALO = 8
GLU_HALO = 16

N_DEV = 8
AXES = ("x", "y", "c")
GU_SHARD = 2 * D_FF // N_DEV
FF_BLOCKS = D_FF // GU_SHARD
DN_SHARD = D_FF // N_DEV
IA_SHARD = IN_A // N_DEV

ADAM_LR, ADAM_B1, ADAM_B2, ADAM_EPS, ADAM_WD, ADAM_STEP = 0.001, 0.9, 0.999, 1e-08, 0.01, 10

SP_REL, SP_CB, SP_CW, SP_QKV, SP_MIX, SP_MEM, SP_FFN, SP_FINAL, SP_MISC, SMALL_ROWS = 0, 32, 34, 40, 44, 46, 48, 50, 51, 56


def _cp(*sems):
    return pltpu.CompilerParams(dimension_semantics=sems, vmem_limit_bytes=_VMEM_LIMIT)


def _mm(a, b):
    return jnp.dot(a.astype(_MXU), b.astype(_MXU), preferred_element_type=F32)


def _mm_nt(a, b):
    return lax.dot_general(a.astype(_MXU), b.astype(_MXU), (((1,), (1,)), ((), ())), preferred_element_type=F32)


def _mm_tn(a, b):
    return lax.dot_general(a.astype(_MXU), b.astype(_MXU), (((0,), (0,)), ((), ())), preferred_element_type=F32)


def _mmf(a, b):
    return jnp.dot(a, b, preferred_element_type=F32, precision=_HI)


def _mmf_nt(a, b):
    return lax.dot_general(a, b, (((1,), (1,)), ((), ())), preferred_element_type=F32, precision=_HI)


def _mmf_tn(a, b):
    return lax.dot_general(a, b, (((0,), (0,)), ((), ())), preferred_element_type=F32, precision=_HI)


def _silu(x):
    return x * jax.nn.sigmoid(x)


def _w2d(ref):
    v = ref[...]
    return v.reshape(-1, v.shape[-1])


def _rows(m):
    return min(m, _TM)


def _spec_rowsharded(layer, rows, cols, col_block=None):
    if col_block is None:
        return pl.BlockSpec((N_DEV, None, rows, cols), lambda *_: (0, layer, 0, 0))
    return pl.BlockSpec((N_DEV, None, rows, cols), lambda *ids: (0, layer, 0, ids[col_block]))


def _spec_gate_up(axis):
    return pl.BlockSpec((None, D, GU_SHARD), lambda *ids: (ids[axis], 0, 0))


def _spec_down(axis):
    return pl.BlockSpec((2, DN_SHARD, D), lambda *ids: (ids[axis], 0, 0))


def _dep_specs(deps):
    return [pl.BlockSpec(memory_space=pl.ANY) for d in deps]


def _spec_gu_act(row_axis, axis, tm):
    return pl.BlockSpec((None, None, tm, GU_SHARD), lambda *ids: (ids[axis] // FF_BLOCKS, ids[axis] % FF_BLOCKS, ids[row_axis], 0))


def _norm_matmul(x, g, w, w_spec, n_blocks, out_shape, out_spec, name, deps=(), out_dtype=F32):
    m, k = x.shape
    tm = _rows(m)

    def body(x_ref, g_ref, w_ref, *rest):
        y_ref, hn_ref = rest[-2:]

        @pl.when(pl.program_id(1) == 0)
        def _():
            xv = x_ref[...]
            r = lax.rsqrt(jnp.mean(xv * xv, axis=-1, keepdims=True) + EPS)
            hn_ref[...] = (xv * r * g_ref[...]).astype(hn_ref.dtype)

        y_ref[...] = _mm(hn_ref[...], _w2d(w_ref)).astype(y_ref.dtype)

    return pl.pallas_call(
        body, grid=(m // tm, n_blocks),
        in_specs=[pl.BlockSpec((tm, k), lambda i, j: (i, 0)), pl.BlockSpec((1, k), lambda i, j: (0, 0)), w_spec]
        + _dep_specs(deps),
        out_specs=[out_spec, pl.BlockSpec((tm, k), lambda i, j: (i, 0))],
        out_shape=[_SDS(out_shape, out_dtype), _SDS((m, k), _ACT)],
        name=name, compiler_params=_cp("arbitrary", "arbitrary"))(x, g, w, *deps)


def _matmul_res(a, a_spec, w, w_spec, n_k, res, name):
    m, n = res.shape
    tm = _rows(m)

    def body(a_ref, w_ref, r_ref, o_ref):
        part = _mm(a_ref[...], _w2d(w_ref))

        @pl.when(pl.program_id(1) == 0)
        def _():
            o_ref[...] = r_ref[...] + part

        @pl.when(pl.program_id(1) > 0)
        def _():
            o_ref[...] += part

    return pl.pallas_call(
        body, grid=(m // tm, n_k),
        in_specs=[a_spec, w_spec, pl.BlockSpec((tm, n), lambda i, j: (i, 0))],
        out_specs=pl.BlockSpec((tm, n), lambda i, j: (i, 0)),
        out_shape=_SDS((m, n), F32), name=name, compiler_params=_cp("arbitrary", "arbitrary"))(a, w, res)


def _matmul_nt(dy, w, w_spec, n_blocks, out_shape, out_spec, name, deps=(), out_dtype=F32):
    m, n = dy.shape
    tm = _rows(m)

    def body(dy_ref, w_ref, *rest):
        o_ref = rest[-1]
        o_ref[...] = _mm_nt(dy_ref[...], _w2d(w_ref)).astype(o_ref.dtype)

    return pl.pallas_call(
        body, grid=(m // tm, n_blocks),
        in_specs=[pl.BlockSpec((tm, n), lambda i, j: (i, 0)), w_spec] + _dep_specs(deps),
        out_specs=out_spec, out_shape=_SDS(out_shape, out_dtype),
        name=name, compiler_params=_cp("arbitrary", "arbitrary"))(dy, w, *deps)


def _matmul_nt_normbwd(dy, dy_spec, w, w_spec, nj, h, g, dh_in, name):
    m, k = h.shape
    tm = _rows(m)

    def body(dy_ref, w_ref, h_ref, g_ref, dhin_ref, dh_ref, dg_ref, acc_ref):
        i, j = pl.program_id(0), pl.program_id(1)

        @pl.when(j == 0)
        def _():
            acc_ref[...] = jnp.zeros_like(acc_ref)

        acc_ref[...] += _mm_nt(dy_ref[...], _w2d(w_ref))

        @pl.when(j == nj - 1)
        def _():
            xv = h_ref[...]
            r = lax.rsqrt(jnp.mean(xv * xv, axis=-1, keepdims=True) + EPS)
            xh = xv * r
            dhn = acc_ref[...]
            part = jnp.sum(dhn * xh, axis=0, keepdims=True)

            @pl.when(i == 0)
            def _():
                dg_ref[...] = part

            @pl.when(i > 0)
            def _():
                dg_ref[...] += part

            t = dhn * g_ref[...]
            dh_ref[...] = dhin_ref[...] + r * (t - xh * jnp.mean(t * xh, axis=-1, keepdims=True))

    return pl.pallas_call(
        body, grid=(m // tm, nj),
        in_specs=[dy_spec, w_spec, pl.BlockSpec((tm, k), lambda i, j: (i, 0)), pl.BlockSpec((1, k), lambda i, j: (0, 0)),
                  pl.BlockSpec((tm, k), lambda i, j: (i, 0))],
        out_specs=[pl.BlockSpec((tm, k), lambda i, j: (i, 0)), pl.BlockSpec((1, k), lambda i, j: (0, 0))],
        out_shape=[_SDS((m, k), F32), _SDS((1, k), F32)],
        scratch_shapes=[pltpu.VMEM((tm, k), F32)],
        name=name, compiler_params=_cp("arbitrary", "arbitrary"))(dy, w, h, g, dh_in)


def _matmul_tn(x, x_spec, dy, dy_spec, m, n_blocks, acc_shape, out_shape, out_spec, name, split=None):
    tm = _rows(m)
    nm = m // tm

    def body(x_ref, dy_ref, o_ref, acc_ref):
        @pl.when(pl.program_id(1) == 0)
        def _():
            acc_ref[...] = jnp.zeros_like(acc_ref)

        acc_ref[...] += _mm_tn(x_ref[...], dy_ref[...])

        @pl.when(pl.program_id(1) == nm - 1)
        def _():
            if split is None:
                o_ref[...] = acc_ref[...].reshape(o_ref.shape).astype(o_ref.dtype)
            else:
                for d in range(N_DEV):
                    o_ref[d] = acc_ref[:, d * split:(d + 1) * split].astype(o_ref.dtype)

    return pl.pallas_call(
        body, grid=(n_blocks, nm), in_specs=[x_spec, dy_spec], out_specs=out_spec,
        out_shape=_SDS(out_shape, _WIRE), scratch_shapes=[pltpu.VMEM(acc_shape, F32)],
        name=name, compiler_params=_cp("arbitrary", "arbitrary"))(x, dy)


def _loss_head(h, g, tgt, name):
    m, k = h.shape
    tm = _rows(m)

    def body(h_ref, g_ref, t_ref, loss_ref, dh_ref, dg_ref):
        i = pl.program_id(0)
        xv = h_ref[...]
        r = lax.rsqrt(jnp.mean(xv * xv, axis=-1, keepdims=True) + EPS)
        xh = xv * r
        gv = g_ref[...]
        err = xh * gv - t_ref[...]
        lpart = jnp.zeros((1, 128), F32) + 0.5 * jnp.sum(jnp.mean(err * err, axis=-1, keepdims=True), axis=0, keepdims=True)
        dy = err * (1.0 / k)
        gpart = jnp.sum(dy * xh, axis=0, keepdims=True)

        @pl.when(i == 0)
        def _():
            loss_ref[...] = lpart
            dg_ref[...] = gpart

        @pl.when(i > 0)
        def _():
            loss_ref[...] += lpart
            dg_ref[...] += gpart

        t = dy * gv
        dh_ref[...] = r * (t - xh * jnp.mean(t * xh, axis=-1, keepdims=True))

    return pl.pallas_call(
        body, grid=(m // tm,),
        in_specs=[pl.BlockSpec((tm, k), lambda i: (i, 0)), pl.BlockSpec((1, k), lambda i: (0, 0)),
                  pl.BlockSpec((tm, k), lambda i: (i, 0))],
        out_specs=[pl.BlockSpec((1, 128), lambda i: (0, 0)), pl.BlockSpec((tm, k), lambda i: (i, 0)),
                   pl.BlockSpec((1, k), lambda i: (0, 0))],
        out_shape=[_SDS((1, 128), F32), _SDS((m, k), F32), _SDS((1, k), F32)],
        name=name, compiler_params=_cp("arbitrary"))(h, g, tgt)


def _glu_down(gu, conv_w, conv_b, w_down, res, name):
    s = gu.shape[2]
    tm = min(s, _TM_GLU)

    def body(gu_ref, prev_ref, w_ref, b_ref, wdn_ref, r_ref, o_ref, act_ref):
        i, j = pl.program_id(0), pl.program_id(1)
        prev = jnp.where(i > 0, prev_ref[...].astype(F32), 0.0)
        ext = jnp.concatenate([prev, gu_ref[0].astype(F32)], axis=0)
        gc = b_ref[...] + w_ref[FFN_CONV - 1:FFN_CONV, :] * ext
        for k in range(FFN_CONV - 1):
            gc = gc + w_ref[k:k + 1, :] * pltpu.roll(ext, FFN_CONV - 1 - k, 0)
        act = (_silu(gc[GLU_HALO:]) * gu_ref[1].astype(F32)).astype(act_ref.dtype)
        act_ref[...] = act
        part = _mm(act, _w2d(wdn_ref))

        @pl.when(j == 0)
        def _():
            o_ref[...] = r_ref[...] + part

        @pl.when(j > 0)
        def _():
            o_ref[...] += part

    return pl.pallas_call(
        body, grid=(s // tm, FF_BLOCKS),
        in_specs=[pl.BlockSpec((2, None, tm, GU_SHARD), lambda i, j: (0, j, i, 0)),
                  pl.BlockSpec((None, None, GLU_HALO, GU_SHARD),
                               lambda i, j: (0, j, jnp.maximum(i * (tm // GLU_HALO) - 1, 0), 0)),
                  pl.BlockSpec((None, HALO, GU_SHARD), lambda i, j: (j, 0, 0)),
                  pl.BlockSpec((None, 1, GU_SHARD), lambda i, j: (j, 0, 0)),
                  _spec_down(1), pl.BlockSpec((tm, D), lambda i, j: (i, 0))],
        out_specs=[pl.BlockSpec((tm, D), lambda i, j: (i, 0)), pl.BlockSpec((None, tm, GU_SHARD), lambda i, j: (j, i, 0))],
        out_shape=[_SDS((s, D), F32), _SDS((FF_BLOCKS, s, GU_SHARD), _ACT)], name=name,
        compiler_params=_cp("arbitrary", "arbitrary"))(gu, gu, conv_w, conv_b, w_down, res)


def _glu_bwd(gu, conv_w, conv_b, dh, w_down, name, deps=()):
    s = gu.shape[2]
    tm = min(s, _TM_GLU)
    nt = s // tm
    ext_rows = tm + GLU_HALO

    def body(gu_ref, prev_ref, w_ref, b_ref, dh_ref, wdn_ref, *rest):
        dgu_ref, dw_ref, db_ref, carry_ref = rest[-4:]
        t = pl.program_id(1)
        i = nt - 1 - t

        @pl.when(t == 0)
        def _():
            carry_ref[...] = jnp.zeros_like(carry_ref)
            dw_ref[...] = jnp.zeros_like(dw_ref)
            db_ref[...] = jnp.zeros_like(db_ref)

        up = gu_ref[1].astype(F32)
        prev = jnp.where(i > 0, prev_ref[...].astype(F32), 0.0)
        ext = jnp.concatenate([prev, gu_ref[0].astype(F32)], axis=0)
        shifted = [pltpu.roll(ext, FFN_CONV - 1 - j, 0) if j < FFN_CONV - 1 else ext for j in range(FFN_CONV)]
        gc = b_ref[...] + shifted[0] * w_ref[0:1, :]
        for j in range(1, FFN_CONV):
            gc = gc + shifted[j] * w_ref[j:j + 1, :]
        gc = gc[GLU_HALO:]
        sg = jax.nn.sigmoid(gc)
        da = _mm_nt(dh_ref[...], _w2d(wdn_ref))
        dup = da * (gc * sg)
        dgc = da * up * (sg * (1.0 + gc * (1.0 - sg)))
        db_ref[...] += jnp.sum(dgc, axis=0, keepdims=True)
        dgc_ext = jnp.concatenate([jnp.zeros((GLU_HALO, GU_SHARD), F32), dgc], axis=0)
        dext = dgc_ext * w_ref[FFN_CONV - 1:FFN_CONV, :]
        for j in range(FFN_CONV):
            dw_ref[j:j + 1, :] += jnp.sum(shifted[j] * dgc_ext, axis=0, keepdims=True)
            if j < FFN_CONV - 1:
                dext = dext + w_ref[j:j + 1, :] * pltpu.roll(dgc_ext, ext_rows - (FFN_CONV - 1 - j), 0)
        tail = jnp.concatenate([jnp.zeros((tm - GLU_HALO, GU_SHARD), F32), carry_ref[...]], axis=0)
        dgate = dext[GLU_HALO:] + tail
        carry_ref[...] = dext[:GLU_HALO]
        dgu_ref[0] = dgate.astype(dgu_ref.dtype)
        dgu_ref[1] = dup.astype(dgu_ref.dtype)

    return pl.pallas_call(
        body, grid=(FF_BLOCKS, nt),
        in_specs=[pl.BlockSpec((2, None, tm, GU_SHARD), lambda j, t: (0, j, nt - 1 - t, 0)),
                  pl.BlockSpec((None, None, GLU_HALO, GU_SHARD),
                               lambda j, t: (0, j, jnp.maximum((nt - 1 - t) * (tm // GLU_HALO) - 1, 0), 0)),
                  pl.BlockSpec((None, HALO, GU_SHARD), lambda j, t: (j, 0, 0)),
                  pl.BlockSpec((None, 1, GU_SHARD), lambda j, t: (j, 0, 0)),
                  pl.BlockSpec((tm, D), lambda j, t: (nt - 1 - t, 0)), _spec_down(0)] + _dep_specs(deps),
        out_specs=[pl.BlockSpec((2, None, tm, GU_SHARD), lambda j, t: (0, j, nt - 1 - t, 0)),
                   pl.BlockSpec((None, HALO, GU_SHARD), lambda j, t: (j, 0, 0)),
                   pl.BlockSpec((None, 1, GU_SHARD), lambda j, t: (j, 0, 0))],
        out_shape=[_SDS(gu.shape, _ACT), _SDS((FF_BLOCKS, HALO, GU_SHARD), F32), _SDS((FF_BLOCKS, 1, GU_SHARD), F32)],
        scratch_shapes=[pltpu.VMEM((GLU_HALO, GU_SHARD), F32)],
        name=name, compiler_params=_cp("arbitrary", "arbitrary"))(gu, gu, conv_w, conv_b, dh, w_down, *deps)


def _bucket_table():
    qi = np.arange(BLK)[:, None]
    kj = np.arange(BLK)[None, :]
    n = np.where(kj > qi, BLK + qi - kj, qi - kj)
    max_exact = N_BUCKETS // 2
    nf = np.maximum(n, 1).astype(np.float32)
    large = max_exact + (np.log(nf / max_exact) / math.log(MAX_DISTANCE / max_exact)
                         * (N_BUCKETS - max_exact)).astype(np.int32)
    large = np.minimum(large, N_BUCKETS - 1)
    return np.where(n < max_exact, n, large).astype(np.int32)


def _lane_low():
    return lax.broadcasted_iota(jnp.int32, (1, 128), 1) < A_HD


def _swa_group(q, kd, vd, sink, bias, upper, first):
    n = A_HEADS // A_KV_HEADS
    low = _lane_low()
    pairs = [q[:, p * 128:(p + 1) * 128] for p in range(n // 2)]
    qm = jnp.concatenate([jnp.where(low == (h % 2 == 0), pairs[h // 2], 0.0) for h in range(n)], axis=0)
    s2 = _mm_nt(qm, kd) * (A_HD ** -0.5)
    s = jnp.where(upper[None], s2[:, :BLK].reshape(n, BLK, BLK), s2[:, BLK:].reshape(n, BLK, BLK)) + bias
    s = jnp.where((upper & first)[None], -jnp.inf, s)
    m = jnp.maximum(jnp.max(s, axis=-1, keepdims=True), sink)
    p = jnp.exp(s - m)
    split = jnp.concatenate([jnp.where(upper[None], p, 0.0), jnp.where(upper[None], 0.0, p)], axis=-1)
    split = split.reshape(n * BLK, 2 * BLK)
    den = _mm(p.reshape(n * BLK, BLK), jnp.ones((BLK, 128), F32)) + jnp.exp(sink - m).reshape(n * BLK, 1)
    o = _mm(split, vd) / den
    return jnp.concatenate([jnp.where(low, o[2 * p * BLK:(2 * p + 1) * BLK], o[(2 * p + 1) * BLK:(2 * p + 2) * BLK])
                            for p in range(n // 2)], axis=1)


def _swa_sinks(sink_ref, g):
    n = A_HEADS // A_KV_HEADS
    return jnp.concatenate([sink_ref[:, h:h + 1] for h in range(g * n, (g + 1) * n)], axis=0).reshape(n, 1, 1)


def _both_halves(t, t_rolled, g):
    low = _lane_low()
    return jnp.where(low, t, t_rolled) if g == 0 else jnp.where(low, t_rolled, t)


def _cross_pairs(q, mk, mv):
    rows = q.shape[0]
    low = _lane_low()
    qm = [jnp.concatenate([jnp.where(low, q[:, p * 128:(p + 1) * 128], 0.0), jnp.where(low, 0.0, q[:, p * 128:(p + 1) * 128])], axis=0)
          for p in range(X_HEADS // 2)]
    s = [_mm_nt(qm[p], mk[:, p * 128:(p + 1) * 128]) * (X_HD ** -0.5) for p in range(X_HEADS // 2)]
    e = [jnp.exp(t - jnp.max(t, axis=-1, keepdims=True)) for t in s]
    pr = [t / jnp.sum(t, axis=-1, keepdims=True) for t in e]
    o = [_mm(pr[p], mv[:, p * 128:(p + 1) * 128]) for p in range(X_HEADS // 2)]
    return jnp.concatenate([jnp.where(low, t[:rows], t[rows:]) for t in o], axis=1)


def _swa_upper():
    qi = lax.broadcasted_iota(jnp.int32, (BLK, BLK), 0)
    kj = lax.broadcasted_iota(jnp.int32, (BLK, BLK), 1)
    return kj > qi


def _bias_build(rel_bias, bucket, name):
    def body(rb_ref, bucket_ref, o_ref):
        b = bucket_ref[...]
        for h in range(A_HEADS):
            acc = jnp.zeros((BLK, BLK), F32)
            for k in range(N_BUCKETS):
                acc = jnp.where(b == k, rb_ref[k, h], acc)
            o_ref[h] = acc

    return pl.pallas_call(
        body, in_specs=[pl.BlockSpec(memory_space=pltpu.SMEM), pl.BlockSpec(memory_space=pltpu.VMEM)],
        out_specs=pl.BlockSpec(memory_space=pltpu.VMEM),
        out_shape=_SDS((A_HEADS, BLK, BLK), F32), name=name)(rel_bias, bucket)


def _bias_reduce(dbias, bucket, name):
    def body(db_ref, bucket_ref, o_ref):
        b = bucket_ref[...]
        row = lax.broadcasted_iota(jnp.int32, (N_BUCKETS, 128), 0)
        lane = lax.broadcasted_iota(jnp.int32, (N_BUCKETS, 128), 1)
        acc = jnp.zeros((N_BUCKETS, 128), F32)
        for h in range(A_HEADS):
            v = db_ref[h]
            for k in range(N_BUCKETS):
                sk = jnp.sum(jnp.sum(jnp.where(b == k, v, 0.0), axis=1, keepdims=True), axis=0, keepdims=True)
                acc = acc + jnp.where((row == k) & (lane == h), sk, 0.0)
        o_ref[...] = acc

    return pl.pallas_call(
        body, in_specs=[pl.BlockSpec(memory_space=pltpu.VMEM)] * 2,
        out_specs=pl.BlockSpec(memory_space=pltpu.VMEM),
        out_shape=_SDS((N_BUCKETS, 128), F32), name=name)(dbias, bucket)


def _mix_a_fwd(proj, bias, sinks, memkv, name):
    s = proj.shape[0]
    nb = s // BLK
    grp = A_HEADS // A_KV_HEADS

    def body(proj_ref, prev_ref, bias_ref, sink_ref, memkv_ref, o_ref):
        i = pl.program_id(0)
        upper = _swa_upper()
        prev = prev_ref[...].astype(F32)
        proj = proj_ref[...].astype(F32)
        kb = jnp.concatenate([prev[:, :A_KV], proj[:, A_Q:A_Q + A_KV]], axis=0)
        vb = jnp.concatenate([prev[:, A_KV:], proj[:, A_Q + A_KV:A_Q + 2 * A_KV]], axis=0)
        kb_r = pltpu.roll(kb, A_HD, 1)
        vb_r = pltpu.roll(vb, A_HD, 1)
        gw = A_Q // A_KV_HEADS
        outs = [_swa_group(proj[:, g * gw:(g + 1) * gw], _both_halves(kb, kb_r, g), _both_halves(vb, vb_r, g),
                           _swa_sinks(sink_ref, g), bias_ref[g * grp:(g + 1) * grp], upper, i == 0) for g in range(A_KV_HEADS)]
        outs.append(_cross_pairs(proj[:, A_Q + 2 * A_KV:], memkv_ref[:, :X_Q], memkv_ref[:, X_Q:]))
        o_ref[...] = jnp.concatenate(outs, axis=1).astype(o_ref.dtype)

    return pl.pallas_call(
        body, grid=(nb,),
        in_specs=[pl.BlockSpec((BLK, IN_A), lambda i: (i, 0)),
                  pl.BlockSpec((BLK, 2 * A_KV), lambda i: (jnp.maximum(i - 1, 0), A_Q // (2 * A_KV))),
                  pl.BlockSpec((A_HEADS, BLK, BLK), lambda i: (0, 0, 0)),
                  pl.BlockSpec((1, 128), lambda i: (0, 0)),
                  pl.BlockSpec((MEM_LEN, 2 * X_Q), lambda i: (0, 0))],
        out_specs=pl.BlockSpec((BLK, D), lambda i: (i, 0)),
        out_shape=_SDS((s, D), _ACT), name=name, compiler_params=_cp("arbitrary"))(proj, proj, bias, sinks, memkv)


def _mix_a_bwd(proj, bias, sinks, memkv, dmix, name):
    s = proj.shape[0]
    nb = s // BLK
    grp = A_HEADS // A_KV_HEADS

    def body(proj_ref, prev_ref, bias_ref, sink_ref, memkv_ref, dmix_ref,
             dproj_ref, dbias_ref, dsink_ref, dmemkv_ref, carry_ref):
        t = pl.program_id(0)
        i = nb - 1 - t

        @pl.when(t == 0)
        def _():
            carry_ref[...] = jnp.zeros_like(carry_ref)
            dbias_ref[...] = jnp.zeros_like(dbias_ref)
            dsink_ref[...] = jnp.zeros_like(dsink_ref)
            dmemkv_ref[...] = jnp.zeros_like(dmemkv_ref)

        upper = _swa_upper()
        lane = lax.broadcasted_iota(jnp.int32, (1, 128), 1)
        low = _lane_low()
        prev = prev_ref[...].astype(F32)
        proj = proj_ref[...].astype(F32)
        kb = jnp.concatenate([prev[:, :A_KV], proj[:, A_Q:A_Q + A_KV]], axis=0)
        vb = jnp.concatenate([prev[:, A_KV:], proj[:, A_Q + A_KV:A_Q + 2 * A_KV]], axis=0)
        kb_r = pltpu.roll(kb, A_HD, 1)
        vb_r = pltpu.roll(vb, A_HD, 1)
        gw = A_Q // A_KV_HEADS
        dqs, dkd, dvd = [], [], []
        dsink = jnp.zeros((1, 128), F32)
        for g in range(A_KV_HEADS):
            _, vjp = jax.vjp(functools.partial(_swa_group, upper=upper, first=i == 0), proj[:, g * gw:(g + 1) * gw],
                             _both_halves(kb, kb_r, g), _both_halves(vb, vb_r, g), _swa_sinks(sink_ref, g),
                             bias_ref[g * grp:(g + 1) * grp])
            dq, dk, dv, ds, db = vjp(dmix_ref[:, g * gw:(g + 1) * gw].astype(F32))
            dqs.append(dq)
            dkd.append(dk + pltpu.roll(dk, A_HD, 1))
            dvd.append(dv + pltpu.roll(dv, A_HD, 1))
            for h in range(grp):
                dsink = dsink + jnp.where(lane == g * grp + h, ds[h], 0.0)
            dbias_ref[g * grp:(g + 1) * grp] += db
        dsink_ref[...] += dsink
        dkb = jnp.where(low, dkd[0], dkd[1])
        dvb = jnp.where(low, dvd[0], dvd[1])
        _, vjp = jax.vjp(_cross_pairs, proj[:, A_Q + 2 * A_KV:], memkv_ref[:, :X_Q], memkv_ref[:, X_Q:])
        dxq, dmk, dmv = vjp(dmix_ref[:, A_Q:].astype(F32))
        dmemkv_ref[...] += jnp.concatenate([dmk, dmv], axis=1)
        dkv_cur = jnp.concatenate([dkb[BLK:], dvb[BLK:]], axis=1) + carry_ref[...]
        carry_ref[...] = jnp.concatenate([dkb[:BLK], dvb[:BLK]], axis=1)
        dproj_ref[...] = jnp.concatenate(dqs + [dkv_cur, dxq], axis=1).astype(dproj_ref.dtype)

    return pl.pallas_call(
        body, grid=(nb,),
        in_specs=[pl.BlockSpec((BLK, IN_A), lambda t: (nb - 1 - t, 0)),
                  pl.BlockSpec((BLK, 2 * A_KV), lambda t: (jnp.maximum(nb - 2 - t, 0), A_Q // (2 * A_KV))),
                  pl.BlockSpec((A_HEADS, BLK, BLK), lambda t: (0, 0, 0)),
                  pl.BlockSpec((1, 128), lambda t: (0, 0)),
                  pl.BlockSpec((MEM_LEN, 2 * X_Q), lambda t: (0, 0)),
                  pl.BlockSpec((BLK, D), lambda t: (nb - 1 - t, 0))],
        out_specs=[pl.BlockSpec((BLK, IN_A), lambda t: (nb - 1 - t, 0)),
                   pl.BlockSpec((A_HEADS, BLK, BLK), lambda t: (0, 0, 0)),
                   pl.BlockSpec((1, 128), lambda t: (0, 0)),
                   pl.BlockSpec((MEM_LEN, 2 * X_Q), lambda t: (0, 0))],
        out_shape=[_SDS((s, IN_A), _ACT), _SDS((A_HEADS, BLK, BLK), F32), _SDS((1, 128), F32),
                   _SDS((MEM_LEN, 2 * X_Q), F32)],
        scratch_shapes=[pltpu.VMEM((BLK, 2 * A_KV), F32)],
        name=name, compiler_params=_cp("arbitrary"))(proj, proj, bias, sinks, memkv, dmix)


def _dn_heads(yq, yk, yv, z, bl, al, a_log, dtb, ng, s0):
    c = CHUNK
    nh = B_V_HEADS
    rep = B_V_HEADS // B_QK_HEADS
    r = lax.broadcasted_iota(jnp.int32, (c, c), 0)
    cc = lax.broadcasted_iota(jnp.int32, (c, c), 1)
    q = [_silu(t) for t in yq]
    k = [_silu(t) for t in yk]
    v = [_silu(t) for t in yv]
    q = [t * lax.rsqrt(jnp.sum(t * t, axis=-1, keepdims=True) + EPS) * (B_HD ** -0.5) for t in q]
    k = [t * lax.rsqrt(jnp.sum(t * t, axis=-1, keepdims=True) + EPS) for t in k]
    beta = [jax.nn.sigmoid(t) for t in bl]
    g = [-jnp.exp(a_log[h]) * jax.nn.softplus(al[h] + dtb[h]) for h in range(nh)]
    gb = [jnp.broadcast_to(t, (c, c)) for t in g]
    gc_col = [jnp.sum(jnp.where(cc <= r, t.T, 0.0), axis=1, keepdims=True) for t in gb]
    gc_row = [jnp.sum(jnp.where(r <= cc, t, 0.0), axis=0, keepdims=True) for t in gb]
    gc_last = [jnp.sum(t, axis=0, keepdims=True) for t in g]
    decay = [jnp.exp(jnp.where(r >= cc, gc_col[h] - gc_row[h], -jnp.inf)) for h in range(nh)]
    kq = [_mmf_nt(jnp.concatenate([k[h], q[h]], axis=0), k[h]) for h in range(B_QK_HEADS)]
    kk = [t[:c] for t in kq]
    qk = [t[c:] for t in kq]
    egc = [jnp.exp(t) for t in gc_col]
    both = [_mmf(jnp.concatenate([(beta[h] * egc[h]) * k[h // rep], q[h // rep] * egc[h]], axis=0), s0[h]) for h in range(nh)]
    rhs = [beta[h] * v[h] - both[h][:c] for h in range(nh)]
    qs0 = [t[c:] for t in both]
    pw = [-(beta[h] * kk[h // rep] * jnp.where(r > cc, decay[h], 0.0)) for h in range(nh)]
    x = rhs
    for lvl in range(6):
        if lvl < 5:
            prod = [_mmf(pw[h], jnp.concatenate([x[h], pw[h]], axis=1)) for h in range(nh)]
            x = [x[h] + prod[h][:, :B_HD] for h in range(nh)]
            pw = [t[:, B_HD:] for t in prod]
        else:
            x = [x[h] + _mmf(pw[h], x[h]) for h in range(nh)]
    delta = x
    last = [_mmf(jnp.concatenate([qk[h // rep] * decay[h], (k[h // rep] * jnp.exp(gc_last[h] - gc_col[h])).T], axis=0), delta[h])
            for h in range(nh)]
    out = [qs0[h] + last[h][:c] for h in range(nh)]
    s1 = [jnp.exp(gc_last[h]) * s0[h] + last[h][c:] for h in range(nh)]
    o = [t * lax.rsqrt(jnp.mean(t * t, axis=-1, keepdims=True) + EPS) * ng for t in out]
    return [o[h] * _silu(z[h]) for h in range(nh)], s1


def _dn_conv(ext, w_ref):
    y = ext * w_ref[B_CONV - 1:B_CONV, :]
    for j in range(B_CONV - 1):
        y = y + w_ref[j:j + 1, :] * pltpu.roll(ext, B_CONV - 1 - j, 0)
    return y


def _dn_args(y, cur_ref, par_ref, ng_ref):
    nh = B_V_HEADS
    return ([y[:, h * B_HD:(h + 1) * B_HD] for h in range(B_QK_HEADS)],
            [y[:, B_QK + h * B_HD:B_QK + (h + 1) * B_HD] for h in range(B_QK_HEADS)],
            [y[:, 2 * B_QK + h * B_HD:2 * B_QK + (h + 1) * B_HD] for h in range(nh)],
            [cur_ref[:, BP_Z + h * B_HD:BP_Z + (h + 1) * B_HD] for h in range(nh)],
            [cur_ref[:, BP_GATE + h:BP_GATE + h + 1] for h in range(nh)],
            [cur_ref[:, BP_GATE + nh + h:BP_GATE + nh + h + 1] for h in range(nh)],
            [par_ref[:, h:h + 1] for h in range(nh)], [par_ref[:, nh + h:nh + h + 1] for h in range(nh)], ng_ref[...])


def _mix_b_fwd(proj, conv_w, par, ng, memkv, name):
    s = proj.shape[0]
    nc = s // CHUNK

    def body(cur_ref, prev_ref, w_ref, par_ref, ng_ref, memkv_ref, o_ref, st_ref, state_ref):
        n = pl.program_id(0)

        @pl.when(n == 0)
        def _():
            state_ref[...] = jnp.zeros_like(state_ref)

        prev = jnp.where(n > 0, prev_ref[...], 0.0)
        ext = jnp.concatenate([prev, cur_ref[:, :B_QKV]], axis=0)
        y = _dn_conv(ext, w_ref)[HALO:]
        s0 = [state_ref[hv] for hv in range(B_V_HEADS)]
        st_ref[0] = state_ref[...]
        outs, s1 = _dn_heads(*_dn_args(y, cur_ref, par_ref, ng_ref), s0)
        for hv in range(B_V_HEADS):
            state_ref[hv] = s1[hv]
        outs = outs + [_cross_pairs(cur_ref[:, BP_XQ:BP_XQ + X_Q], memkv_ref[:, :X_Q], memkv_ref[:, X_Q:])]
        o_ref[...] = jnp.concatenate(outs, axis=1).astype(o_ref.dtype)

    return pl.pallas_call(
        body, grid=(nc,),
        in_specs=[pl.BlockSpec((CHUNK, IN_BP), lambda n: (n, 0)),
                  pl.BlockSpec((HALO, B_QKV), lambda n: (jnp.maximum(n * (CHUNK // HALO) - 1, 0), 0)),
                  pl.BlockSpec((HALO, B_QKV), lambda n: (0, 0)),
                  pl.BlockSpec((1, 128), lambda n: (0, 0)), pl.BlockSpec((1, 128), lambda n: (0, 0)),
                  pl.BlockSpec((MEM_LEN, 2 * X_Q), lambda n: (0, 0))],
        out_specs=[pl.BlockSpec((CHUNK, D), lambda n: (n, 0)),
                   pl.BlockSpec((1, B_V_HEADS, B_HD, B_HD), lambda n: (n, 0, 0, 0))],
        out_shape=[_SDS((s, D), _ACT), _SDS((nc, B_V_HEADS, B_HD, B_HD), F32)],
        scratch_shapes=[pltpu.VMEM((B_V_HEADS, B_HD, B_HD), F32)],
        name=name, compiler_params=_cp("arbitrary"))(proj, proj, conv_w, par, ng, memkv)


def _mix_b_bwd(proj, conv_w, par, ng, memkv, states, dmix, name):
    s = proj.shape[0]
    nc = s // CHUNK
    ext_rows = CHUNK + HALO

    def body(cur_ref, prev_ref, w_ref, par_ref, ng_ref, memkv_ref, st_ref, dmix_ref,
             dproj_ref, dw_ref, dpar_ref, dng_ref, dmemkv_ref, dstate_ref, carry_ref):
        t = pl.program_id(0)
        n = nc - 1 - t

        @pl.when(t == 0)
        def _():
            dstate_ref[...] = jnp.zeros_like(dstate_ref)
            carry_ref[...] = jnp.zeros_like(carry_ref)
            dw_ref[...] = jnp.zeros_like(dw_ref)
            dpar_ref[...] = jnp.zeros_like(dpar_ref)
            dng_ref[...] = jnp.zeros_like(dng_ref)
            dmemkv_ref[...] = jnp.zeros_like(dmemkv_ref)

        lane = lax.broadcasted_iota(jnp.int32, (1, 128), 1)
        prev = jnp.where(n > 0, prev_ref[...], 0.0)
        ext = jnp.concatenate([prev, cur_ref[:, :B_QKV]], axis=0)
        y = _dn_conv(ext, w_ref)[HALO:]
        _, vjp = jax.vjp(_dn_heads, *_dn_args(y, cur_ref, par_ref, ng_ref), [st_ref[0, hv] for hv in range(B_V_HEADS)])
        dyq, dyk, dyv, dz, gbl, gal, ga_log, gdtb, dng, gs0 = vjp(
            ([dmix_ref[:, hv * B_HD:(hv + 1) * B_HD].astype(F32) for hv in range(B_V_HEADS)],
             [dstate_ref[hv] for hv in range(B_V_HEADS)]))
        dgate = jnp.zeros((CHUNK, 128), F32)
        dpar = jnp.zeros((1, 128), F32)
        for hv in range(B_V_HEADS):
            dstate_ref[hv] = gs0[hv]
            dgate = dgate + jnp.where(lane == hv, gbl[hv], 0.0) + jnp.where(lane == B_V_HEADS + hv, gal[hv], 0.0)
            dpar = dpar + jnp.where(lane == hv, ga_log[hv], 0.0) + jnp.where(lane == B_V_HEADS + hv, gdtb[hv], 0.0)
        dpar_ref[...] += dpar
        dng_ref[...] += dng
        _, vjp = jax.vjp(_cross_pairs, cur_ref[:, BP_XQ:BP_XQ + X_Q], memkv_ref[:, :X_Q], memkv_ref[:, X_Q:])
        dxq, dmk, dmv = vjp(dmix_ref[:, B_V:].astype(F32))
        dmemkv_ref[...] += jnp.concatenate([dmk, dmv], axis=1)
        dy = jnp.concatenate(list(dyq) + list(dyk) + list(dyv), axis=1)
        dy_ext = jnp.concatenate([jnp.zeros((HALO, B_QKV), F32), dy], axis=0)
        dext = dy_ext * w_ref[B_CONV - 1:B_CONV, :]
        dw_ref[B_CONV - 1:B_CONV, :] += jnp.sum(ext * dy_ext, axis=0, keepdims=True)
        for j in range(B_CONV - 1):
            sh = B_CONV - 1 - j
            dw_ref[j:j + 1, :] += jnp.sum(pltpu.roll(ext, sh, 0) * dy_ext, axis=0, keepdims=True)
            dext = dext + w_ref[j:j + 1, :] * pltpu.roll(dy_ext, ext_rows - sh, 0)
        tail = jnp.concatenate([jnp.zeros((CHUNK - HALO, B_QKV), F32), carry_ref[...]], axis=0)
        dqkv = dext[HALO:] + tail
        carry_ref[...] = dext[:HALO]
        dproj_ref[...] = jnp.concatenate([dqkv] + list(dz) + [dxq, dgate], axis=1).astype(dproj_ref.dtype)

    return pl.pallas_call(
        body, grid=(nc,),
        in_specs=[pl.BlockSpec((CHUNK, IN_BP), lambda t: (nc - 1 - t, 0)),
                  pl.BlockSpec((HALO, B_QKV), lambda t: (jnp.maximum((nc - 1 - t) * (CHUNK // HALO) - 1, 0), 0)),
                  pl.BlockSpec((HALO, B_QKV), lambda t: (0, 0)),
                  pl.BlockSpec((1, 128), lambda t: (0, 0)), pl.BlockSpec((1, 128), lambda t: (0, 0)),
                  pl.BlockSpec((MEM_LEN, 2 * X_Q), lambda t: (0, 0)),
                  pl.BlockSpec((1, B_V_HEADS, B_HD, B_HD), lambda t: (nc - 1 - t, 0, 0, 0)),
                  pl.BlockSpec((CHUNK, D), lambda t: (nc - 1 - t, 0))],
        out_specs=[pl.BlockSpec((CHUNK, IN_BP), lambda t: (nc - 1 - t, 0)),
                   pl.BlockSpec((HALO, B_QKV), lambda t: (0, 0)),
                   pl.BlockSpec((1, 128), lambda t: (0, 0)), pl.BlockSpec((1, 128), lambda t: (0, 0)),
                   pl.BlockSpec((MEM_LEN, 2 * X_Q), lambda t: (0, 0))],
        out_shape=[_SDS((s, IN_BP), _ACT), _SDS((HALO, B_QKV), F32), _SDS((1, 128), F32), _SDS((1, 128), F32),
                   _SDS((MEM_LEN, 2 * X_Q), F32)],
        scratch_shapes=[pltpu.VMEM((B_V_HEADS, B_HD, B_HD), F32), pltpu.VMEM((HALO, B_QKV), F32)],
        name=name, compiler_params=_cp("arbitrary"))(proj, proj, conv_w, par, ng, memkv, states, dmix)


def _place():
    return lax.axis_index("x"), lax.axis_index("y"), lax.axis_index("c")


def _all_gather(shards, name):
    n = len(shards)

    def body(*refs):
        ins, outs = refs[:n], refs[n:2 * n]
        send_sems, recv_sems, local_sems = refs[2 * n:]
        x, y, c = _place()
        me, sibling = (x, y, c), (x, y, 1 - c)
        chips = [(1 - x, y), (x, 1 - y), (1 - x, 1 - y)]

        def rows(a, px, py, pc):
            return outs[a].at[4 * px + 2 * py + pc]

        def copy(a, k, block, to, src=None):
            return pltpu.make_async_remote_copy(
                src_ref=rows(a, *block) if src is None else src, dst_ref=rows(a, *block),
                send_sem=send_sems.at[a, k], recv_sem=recv_sems.at[a, k],
                device_id=to, device_id_type=pl.DeviceIdType.MESH)

        mine = [pltpu.make_async_copy(ins[a], rows(a, *me), local_sems.at[a]) for a in range(n)]
        for cp in mine:
            cp.start()
        first = []
        for a in range(n):
            first.append(copy(a, 0, me, sibling, src=ins[a]))
            first += [copy(a, 1 + j, me, (*chip, c), src=ins[a]) for j, chip in enumerate(chips)]
        for cp in first:
            cp.start()
        passed = []
        for j, chip in enumerate(chips):
            for a in range(n):
                copy(a, 1 + j, (*chip, c), me).wait_recv()
                fwd = copy(a, 4 + j, (*chip, c), sibling)
                fwd.start()
                passed.append(fwd)
        for a in range(n):
            copy(a, 0, sibling, me).wait_recv()
            for j, chip in enumerate(chips):
                copy(a, 4 + j, (*chip, 1 - c), me).wait_recv()
        for cp in first + passed:
            cp.wait_send()
        for cp in mine:
            cp.wait()

    hbm = pl.BlockSpec(memory_space=pl.ANY)
    return pl.pallas_call(
        body, out_shape=[_SDS((N_DEV,) + s.shape, s.dtype) for s in shards],
        in_specs=[hbm] * n, out_specs=[hbm] * n,
        scratch_shapes=[pltpu.SemaphoreType.DMA((n, 7)), pltpu.SemaphoreType.DMA((n, 7)), pltpu.SemaphoreType.DMA((n,))],
        name=name)(*shards)


class _Exchange:
    def __init__(self, lands, srcs):
        self.lands, self.srcs = lands, srcs


def _seq_exchange(srcs, land_shapes, plan, name, cid):
    n, nl = len(srcs), len(land_shapes)

    def launch(*refs):
        src_refs, land_refs = refs[:n], refs[n:n + nl]
        send_sems, recv_sems, local_sems = refs[n + nl:]
        x, y, c = _place()
        my = 4 * x + 2 * y + c
        peers = [(x ^ ((k + 1) >> 2 & 1), y ^ ((k + 1) >> 1 & 1), c ^ ((k + 1) & 1)) for k in range(N_DEV - 1)]
        barrier = pltpu.get_barrier_semaphore()
        for p in peers:
            pl.semaphore_signal(barrier, inc=1, device_id=p, device_id_type=pl.DeviceIdType.MESH)
        pl.semaphore_wait(barrier, N_DEV - 1)

        def src_for(a, dest):
            return src_refs[a].at[dest] if plan[a][1] else src_refs[a]

        def slot(a, source):
            return land_refs[plan[a][0]].at[source]

        mine = [pltpu.make_async_copy(src_for(a, my), slot(a, my), local_sems.at[a]) for a in range(n)]
        for cp in mine:
            cp.start()
        sends, recvs = [], []
        for k, (px, py, pc) in enumerate(peers):
            peer = 4 * px + 2 * py + pc
            for a in range(n):
                kw = dict(send_sem=send_sems.at[a * (N_DEV - 1) + k], recv_sem=recv_sems.at[a * (N_DEV - 1) + k],
                          device_id=(px, py, pc), device_id_type=pl.DeviceIdType.MESH)
                sends.append(pltpu.make_async_remote_copy(src_ref=src_for(a, peer), dst_ref=slot(a, my), **kw))
                recvs.append(pltpu.make_async_remote_copy(src_ref=src_for(a, my), dst_ref=slot(a, peer), **kw))
        for cp in sends:
            cp.start()
        for cp in recvs:
            cp.wait_recv()
        for cp in sends:
            cp.wait_send()
        for cp in mine:
            cp.wait()

    lands = pl.kernel(
        launch, out_type=[_SDS(s, d) for s, d in land_shapes],
        mesh=plsc.ScalarSubcoreMesh(axis_name="sequencer", num_cores=1), name=name,
        scratch_types=(pltpu.SemaphoreType.DMA((n * (N_DEV - 1),)), pltpu.SemaphoreType.DMA((n * (N_DEV - 1),)),
                       pltpu.SemaphoreType.DMA((n,))),
        compiler_params=pltpu.CompilerParams(collective_id=cid))(*srcs)
    return _Exchange(list(lands), list(srcs))


def _adam_update(g, w, m, v):
    c1 = 1.0 - ADAM_B1 ** ADAM_STEP
    c2 = 1.0 - ADAM_B2 ** ADAM_STEP
    mm = ADAM_B1 * m + (1.0 - ADAM_B1) * g
    vv = ADAM_B2 * v + (1.0 - ADAM_B2) * (g * g)
    delta = -ADAM_LR * ((mm / c1) / (jnp.sqrt(vv / c2) + ADAM_EPS) + ADAM_WD * w)
    return delta, mm, vv


def _sum_sources(p_ref):
    g = p_ref[0].astype(F32)
    for s in range(1, N_DEV):
        g = g + p_ref[s].astype(F32)
    return g


def _adamw(parts, w, m, v, tr, name, restore_b=False, deps=()):
    nl, r, c = w.shape
    cp = parts[0].shape[-1]

    def body(*refs):
        p_refs = refs[:nl]
        w_ref, m_ref, v_ref = refs[nl:nl + 3]
        g_ref, d_ref, nm_ref, nv_ref = refs[-4:]
        g = _sum_sources(p_refs[0])
        for l in range(1, nl):
            g = jnp.where(pl.program_id(0) == l, _sum_sources(p_refs[l]), g)
        if restore_b:
            g = jnp.concatenate([g[:, :BP_XQ], g[:, BP_GATE:BP_GATE + 2 * B_V_HEADS], g[:, BP_XQ:BP_GATE]], axis=1)
        delta, mm, vv = _adam_update(g, w_ref[...], m_ref[...], v_ref[...])
        g_ref[...] = g
        d_ref[...] = delta
        nm_ref[...] = mm
        nv_ref[...] = vv

    spec = pl.BlockSpec((None, tr, c), lambda l, i: (l, i, 0))
    part_specs = [pl.BlockSpec((N_DEV, tr, cp), functools.partial(lambda l, i, k: (0, jnp.where(l == k, i, 0), 0), k=k))
                  for k in range(nl)]
    return pl.pallas_call(
        body, grid=(nl, r // tr),
        in_specs=part_specs + [spec, spec, spec] + _dep_specs(deps),
        out_specs=[spec] * 4, out_shape=[_SDS(w.shape, F32)] * 4,
        name=name, compiler_params=_cp("arbitrary", "arbitrary"))(*parts, w, m, v, *deps)


def _pack_small(d_rel, d_cb, d_cw, d_qkv, d_mix, d_mem, d_ffn, d_final, d_sinks, d_par, d_ng, name):
    flat = [d_rel, *d_cb, *d_cw, d_qkv, *d_mix, *d_mem, *d_ffn, d_final, d_sinks, d_par, d_ng]
    n = len(flat)

    def body(*refs):
        ins, o_ref = refs[:n], refs[n]
        rel, cb0, cb1, cw0, cw1, qkv, mx0, mx1, me0, me1, ff0, ff1, fin, snk, par, ng = ins
        o_ref[...] = jnp.zeros_like(o_ref)
        o_ref[SP_REL:SP_REL + N_BUCKETS, 0:128] = rel[...]
        for l, (cb, cw) in enumerate(((cb0, cw0), (cb1, cw1))):
            o_ref[SP_CB + l:SP_CB + l + 1, :] = jnp.concatenate([cb[j] for j in range(FF_BLOCKS)], axis=1)
            full = jnp.concatenate([cw[j] for j in range(FF_BLOCKS)], axis=1)
            o_ref[SP_CW + FFN_CONV * l:SP_CW + FFN_CONV * (l + 1), :] = full[:FFN_CONV]
        o_ref[SP_QKV:SP_QKV + B_CONV, 0:B_QKV] = qkv[0:B_CONV, :]
        for base, pair in ((SP_MIX, (mx0, mx1)), (SP_MEM, (me0, me1)), (SP_FFN, (ff0, ff1))):
            for l in range(2):
                o_ref[base + l:base + l + 1, 0:D] = pair[l][...]
        o_ref[SP_FINAL:SP_FINAL + 1, 0:D] = fin[...]
        o_ref[SP_MISC:SP_MISC + 1, 0:128] = snk[...]
        o_ref[SP_MISC:SP_MISC + 1, 128:256] = par[...]
        o_ref[SP_MISC:SP_MISC + 1, 256:384] = ng[...]

    vm = pl.BlockSpec(memory_space=pltpu.VMEM)
    return pl.pallas_call(body, in_specs=[vm] * n, out_specs=vm, out_shape=_SDS((SMALL_ROWS, D_FF), F32), name=name)(*flat)


_SMALL = ["rel_bias", "norm_mix_g", "norm_mem_g", "sinks_a", "a_log_b", "dt_bias_b", "out_norm_g_b", "norm_ffn_g",
          "ffn_conv_b", "final_norm_g", "conv_qkv_b", "ffn_conv_w"]


def _adamw_small(recv, rc_qkv, rc_ffn, ws, ms, vs, name, deps=()):
    n = len(_SMALL)

    def body(*refs):
        recv_ref, qkv_ref, ffn_ref = refs[:3]
        w_refs, m_refs, v_refs = refs[3:3 + n], refs[3 + n:3 + 2 * n], refs[3 + 2 * n:3 + 3 * n]
        outs = refs[len(refs) - 4 * n:]
        gs = _sum_sources(recv_ref)
        grads = {
            "rel_bias": gs[SP_REL:SP_REL + N_BUCKETS, 0:A_HEADS],
            "norm_mix_g": gs[SP_MIX:SP_MIX + 2, 0:D], "norm_mem_g": gs[SP_MEM:SP_MEM + 2, 0:D],
            "sinks_a": gs[SP_MISC:SP_MISC + 1, 0:A_HEADS],
            "a_log_b": gs[SP_MISC:SP_MISC + 1, 128:128 + B_V_HEADS],
            "dt_bias_b": gs[SP_MISC:SP_MISC + 1, 128 + B_V_HEADS:128 + 2 * B_V_HEADS],
            "out_norm_g_b": gs[SP_MISC:SP_MISC + 1, 256:256 + B_HD],
            "norm_ffn_g": gs[SP_FFN:SP_FFN + 2, 0:D], "ffn_conv_b": gs[SP_CB:SP_CB + 2, :],
            "final_norm_g": gs[SP_FINAL:SP_FINAL + 1, 0:D],
            "conv_qkv_b": _sum_sources(qkv_ref), "ffn_conv_w": _sum_sources(ffn_ref),
        }
        for i, nm in enumerate(_SMALL):
            g = grads[nm]
            delta, mm, vv = _adam_update(g, w_refs[i][...], m_refs[i][...], v_refs[i][...])
            outs[i][...] = g
            outs[n + i][...] = delta
            outs[2 * n + i][...] = mm
            outs[3 * n + i][...] = vv

    vm = pl.BlockSpec(memory_space=pltpu.VMEM)
    shapes = [_SDS(w.shape, F32) for w in ws]
    return pl.pallas_call(
        body, in_specs=[vm] * (3 + 3 * n) + _dep_specs(deps), out_specs=[vm] * (4 * n), out_shape=shapes * 4,
        name=name)(recv, rc_qkv, rc_ffn, *ws, *ms, *vs, *deps)


def _assemble(gathered, axis):
    g = jnp.moveaxis(gathered, 0, axis)
    shp = list(g.shape)
    return g.reshape(shp[:axis] + [shp[axis] * shp[axis + 1]] + shp[axis + 2:])


def _pad_rows(a, rows):
    return jnp.pad(a, ((0, rows - a.shape[0]), (0, 0)))


def _pad_lanes(a, lanes=128):
    return jnp.pad(a, ((0, 0), (0, lanes - a.shape[1])))


def _ff_blocks(a):
    return jnp.moveaxis(a.reshape(a.shape[0], FF_BLOCKS, GU_SHARD), 1, 0)


def _reorder_b(w):
    qkv_z = w[..., :B_QKV + B_V]
    gates = w[..., B_QKV + B_V:B_QKV + B_V + 2 * B_V_HEADS]
    xq = w[..., IN_B - X_Q:]
    pad = jnp.zeros(w.shape[:-1] + (IN_BP - IN_B,), w.dtype)
    return jnp.concatenate([qkv_z, xq, gates, pad], axis=-1)


def kernel(x, mem, rel_bias, norm_mix_g, norm_mem_g, w_mem_kv, w_out, w_in_a, sinks_a, w_in_b, conv_qkv_b, a_log_b, dt_bias_b, out_norm_g_b, norm_ffn_g, w_gate_up, ffn_conv_w, ffn_conv_b, w_down, final_norm_g, loss_target, m_rel_bias, m_norm_mix_g, m_norm_mem_g, m_w_mem_kv, m_w_out, m_w_in_a, m_sinks_a, m_w_in_b, m_conv_qkv_b, m_a_log_b, m_dt_bias_b, m_out_norm_g_b, m_norm_ffn_g, m_w_gate_up, m_ffn_conv_w, m_ffn_conv_b, m_w_down, m_final_norm_g, v_rel_bias, v_norm_mix_g, v_norm_mem_g, v_w_mem_kv, v_w_out, v_w_in_a, v_sinks_a, v_w_in_b, v_conv_qkv_b, v_a_log_b, v_dt_bias_b, v_out_norm_g_b, v_norm_ffn_g, v_w_gate_up, v_ffn_conv_w, v_ffn_conv_b, v_w_down, v_final_norm_g):
    local = dict(locals())
    order = ["rel_bias", "norm_mix_g", "norm_mem_g", "w_mem_kv", "w_out", "w_in_a", "sinks_a", "w_in_b", "conv_qkv_b",
             "a_log_b", "dt_bias_b", "out_norm_g_b", "norm_ffn_g", "w_gate_up", "ffn_conv_w", "ffn_conv_b", "w_down",
             "final_norm_g"]
    wts = {n: local[n] for n in order}
    moms = {n: local["m_" + n] for n in order}
    vars_ = {n: local["v_" + n] for n in order}
    h0 = x[0]
    memx = mem[0]
    tgt = loss_target[0]
    s = h0.shape[0]
    tm = _rows(s)

    g_mk, g_out, g_ia, g_cq, g_cw = _all_gather(
        [w_mem_kv.astype(_MXU), w_out.astype(_MXU), w_in_a.astype(_MXU), conv_qkv_b, ffn_conv_w], "gather_first")
    gu_land = ((N_DEV, D, GU_SHARD), _MXU)
    dn_land = ((N_DEV, DN_SHARD, D), _MXU)
    whole = [(0, False), (1, False)]
    ffn0_w = _seq_exchange([w_gate_up[0].astype(_MXU), w_down[0].astype(_MXU)], [gu_land, dn_land], whole, "gather_ffn0", 1)
    w_ia = _assemble(g_ia, 2)[0]
    conv_qkv = _pad_rows(_assemble(g_cq, 2)[0], HALO)
    ffn_cw_full = _assemble(g_cw, 2)
    ffn_cw = [_ff_blocks(_pad_rows(ffn_cw_full[i], HALO)) for i in range(2)]
    ffn_cb = [_ff_blocks(ffn_conv_b[i:i + 1]) for i in range(2)]
    bucket = jnp.asarray(_bucket_table())
    bias = _bias_build(rel_bias, bucket, "bias_build")
    sinks = _pad_lanes(sinks_a)
    par_b = _pad_lanes(jnp.concatenate([a_log_b, dt_bias_b], axis=1))

    row_x = pl.BlockSpec((tm, D), lambda i, j: (i, 0))
    gu_shape = (2, FF_BLOCKS, s, GU_SHARD)

    def in_proj(h, g, w, w_spec, n_cols, tn, name, deps=(), out_dtype=F32):
        return _norm_matmul(h, g, w, w_spec, n_cols // tn, (h.shape[0], n_cols),
                            pl.BlockSpec((_rows(h.shape[0]), tn), lambda i, j: (i, j)), name, deps=deps, out_dtype=out_dtype)

    def ffn_fwd(i, h, g_gu, g_dn, deps=()):
        gu, hn = _norm_matmul(h, norm_ffn_g[i:i + 1], g_gu, _spec_gate_up(1), N_DEV, gu_shape,
                              _spec_gu_act(0, 1, tm), f"gate_up_{i}", deps=deps, out_dtype=_ACT)
        h_new, act = _glu_down(gu, ffn_cw[i], ffn_cb[i], g_dn, h, f"glu_down_{i}")
        return h_new, gu, hn, act

    def out_proj(i, mix, h):
        return _matmul_res(mix, row_x, g_out, _spec_rowsharded(i, D // N_DEV, D), 1, h, f"out_proj_{i}")

    proj_a, hn_a = in_proj(h0, norm_mix_g[0:1], w_ia, pl.BlockSpec((D, 640), lambda i, j: (0, j)), IN_A, 640, "in_proj_a",
                           deps=ffn0_w.srcs, out_dtype=_ACT)
    memkv0, memn0 = in_proj(memx, norm_mem_g[0:1], g_mk, _spec_rowsharded(0, D // N_DEV, 2 * X_Q), 2 * X_Q, 2 * X_Q, "mem_proj_0")
    mix_a = _mix_a_fwd(proj_a, bias, sinks, memkv0, "mix_a_fwd")
    h1 = out_proj(0, mix_a, h0)
    g_gu0, g_dn0 = ffn0_w.lands

    def after(a, b):
        return a + (b[(0,) * b.ndim] * 0).astype(a.dtype)

    in_b_w = _seq_exchange([after(_reorder_b(w_in_b).astype(_MXU), h1)], [((N_DEV, 1, D // N_DEV, IN_BP), _MXU)], [(0, False)],
                           "gather_in_b", 2)
    ffn1_w = _seq_exchange([after(w_gate_up[1].astype(_MXU), h1), after(w_down[1].astype(_MXU), h1)], [gu_land, dn_land], whole,
                           "gather_ffn1", 3)
    h2, gu0, hn_f0, act0 = ffn_fwd(0, h1, g_gu0, g_dn0, deps=in_b_w.srcs + ffn1_w.srcs)
    g_ib, = in_b_w.lands
    proj_b, hn_b = in_proj(h2, norm_mix_g[1:2], g_ib, _spec_rowsharded(0, D // N_DEV, 896, col_block=1), IN_BP, 896, "in_proj_b")
    memkv1, memn1 = in_proj(memx, norm_mem_g[1:2], g_mk, _spec_rowsharded(1, D // N_DEV, 2 * X_Q), 2 * X_Q, 2 * X_Q, "mem_proj_1")
    mix_b, states = _mix_b_fwd(proj_b, conv_qkv, par_b, out_norm_g_b, memkv1, "mix_b_fwd")
    h3 = out_proj(1, mix_b, h2)
    g_gu1, g_dn1 = ffn1_w.lands
    h4, gu1, hn_f1, act1 = ffn_fwd(1, h3, g_gu1, g_dn1)
    loss_row, dh, d_final_g = _loss_head(h4, final_norm_g[None, :], tgt, "loss_head")

    zeros_mem = jnp.zeros_like(memx)
    per_dest2 = [(0, True), (1, True)]

    def ffn_bwd(i, dh, h_in, gu, hn_f, act, g_gu, g_dn, deps=()):
        dgu, d_cw, d_cb = _glu_bwd(gu, ffn_cw[i], ffn_cb[i], dh, g_dn, f"glu_bwd_{i}", deps=deps)
        d_wdown = _matmul_tn(act, pl.BlockSpec((None, tm, GU_SHARD), lambda j, r: (j, r, 0)),
                             dh, pl.BlockSpec((tm, D), lambda j, r: (r, 0)), s, FF_BLOCKS, (GU_SHARD, D),
                             (N_DEV, DN_SHARD, D), pl.BlockSpec((2, DN_SHARD, D), lambda j, r: (j, 0, 0)), f"d_w_down_{i}")
        dh_new, d_g = _matmul_nt_normbwd(dgu, _spec_gu_act(0, 1, tm), g_gu, _spec_gate_up(1), N_DEV, h_in,
                                         norm_ffn_g[i:i + 1], dh, f"d_ffn_in_{i}")
        d_wgu = _matmul_tn(hn_f, pl.BlockSpec((tm, D), lambda j, r: (r, 0)), dgu, _spec_gu_act(1, 0, tm), s, N_DEV,
                           (D, GU_SHARD), (N_DEV, D, GU_SHARD), pl.BlockSpec((None, D, GU_SHARD), lambda j, r: (j, 0, 0)),
                           f"d_w_gate_up_{i}")
        sent = _seq_exchange([d_wdown, d_wgu], [((N_DEV, DN_SHARD, D), _WIRE), ((N_DEV, D, GU_SHARD), _WIRE)], per_dest2,
                             f"send_ffn{i}_grads", 4 + i)
        return dh_new, sent, d_cw, d_cb, d_g

    def out_bwd(i, dh, mix, deps):
        dmix = _matmul_nt(dh, g_out, _spec_rowsharded(i, D // N_DEV, D), 1, (s, D), row_x, f"d_mix_{i}", deps=deps, out_dtype=_ACT)
        d_wout = _matmul_tn(mix, pl.BlockSpec((tm, D), lambda j, r: (r, 0)), dh, pl.BlockSpec((tm, D), lambda j, r: (r, 0)),
                            s, 1, (D, D), (N_DEV, D // N_DEV, D), pl.BlockSpec((N_DEV, D // N_DEV, D), lambda j, r: (0, 0, 0)),
                            f"d_w_out_{i}")
        return dmix, d_wout

    def mem_bwd(i, dmemkv, memn):
        tmm = _rows(MEM_LEN)
        _, d_g = _matmul_nt_normbwd(dmemkv, pl.BlockSpec((tmm, 2 * X_Q), lambda r, j: (r, 0)), g_mk,
                                    _spec_rowsharded(i, D // N_DEV, 2 * X_Q), 1, memx, norm_mem_g[i:i + 1], zeros_mem,
                                    f"d_mem_in_{i}")
        by_row = lambda j, r: (r, 0)
        d_w = _matmul_tn(memn, pl.BlockSpec((tmm, D), by_row), dmemkv, pl.BlockSpec((tmm, 2 * X_Q), by_row), MEM_LEN, 1,
                         (D, 2 * X_Q), (N_DEV, D // N_DEV, 2 * X_Q),
                         pl.BlockSpec((N_DEV, D // N_DEV, 2 * X_Q), lambda j, r: (0, 0, 0)), f"d_w_mem_kv_{i}")
        return d_w, d_g

    out_land = ((N_DEV, D // N_DEV, D), _WIRE)
    mk_land = ((N_DEV, D // N_DEV, 2 * X_Q), _WIRE)
    dh, ffn1_g, d_cw1, d_cb1, d_gf1 = ffn_bwd(1, dh, h3, gu1, hn_f1, act1, g_gu1, g_dn1)
    dmix, d_wout1 = out_bwd(1, dh, mix_b, ffn1_g.srcs)
    dproj_b, d_convw, d_par, d_ng, dmemkv1 = _mix_b_bwd(proj_b, conv_qkv, par_b, out_norm_g_b, memkv1, states, dmix, "mix_b_bwd")
    dh, d_gm1 = _matmul_nt_normbwd(dproj_b, pl.BlockSpec((tm, 896), lambda i, j: (i, j)), g_ib,
                                   _spec_rowsharded(0, D // N_DEV, 896, col_block=1), IN_BP // 896, h2, norm_mix_g[1:2], dh, "d_in_b")
    d_wib = _matmul_tn(hn_b, pl.BlockSpec((tm, D), lambda j, r: (r, 0)), dproj_b, pl.BlockSpec((tm, 896), lambda j, r: (r, j)),
                       s, IN_BP // 896, (D, 896), (N_DEV, D // N_DEV, IN_BP),
                       pl.BlockSpec((N_DEV, D // N_DEV, 896), lambda j, r: (0, 0, j)), "d_w_in_b")
    d_wmk1, d_gmem1 = mem_bwd(1, dmemkv1, memn1)
    mix1_g = _seq_exchange([d_wout1, d_wib, d_wmk1], [out_land, ((N_DEV, D // N_DEV, IN_BP), _WIRE), mk_land],
                           [(0, True), (1, True), (2, True)], "send_mix1_grads", 6)
    dh, ffn0_g, d_cw0, d_cb0, d_gf0 = ffn_bwd(0, dh, h1, gu0, hn_f0, act0, g_gu0, g_dn0, deps=mix1_g.srcs)
    dmix, d_wout0 = out_bwd(0, dh, mix_a, ffn0_g.srcs + ffn1_g.lands[:1])
    dproj_a, dbias, dsinks, dmemkv0 = _mix_a_bwd(proj_a, bias, sinks, memkv0, dmix, "mix_a_bwd")
    dh, d_gm0 = _matmul_nt_normbwd(dproj_a, pl.BlockSpec((tm, 640), lambda i, j: (i, j)), w_ia,
                                   pl.BlockSpec((D, 640), lambda i, j: (0, j)), IN_A // 640, h0, norm_mix_g[0:1], dh, "d_in_a")
    d_wia = _matmul_tn(hn_a, pl.BlockSpec((tm, D), lambda j, r: (r, 0)), dproj_a, pl.BlockSpec((tm, IN_A), lambda j, r: (r, 0)),
                       s, 1, (D, IN_A), (N_DEV, D, IA_SHARD), pl.BlockSpec((N_DEV, D, IA_SHARD), lambda j, r: (0, 0, 0)),
                       "d_w_in_a", split=IA_SHARD)
    d_wmk0, d_gmem0 = mem_bwd(0, dmemkv0, memn0)
    d_rel = _bias_reduce(dbias, bucket, "bias_reduce")
    small = _pack_small(d_rel, (d_cb0, d_cb1), (d_cw0, d_cw1), d_convw, (d_gm0, d_gm1), (d_gmem0, d_gmem1),
                        (d_gf0, d_gf1), d_final_g, dsinks, d_par, d_ng, "pack_small")
    mix0_g = _seq_exchange([d_wout0, d_wia, d_wmk0, small],
                           [out_land, ((N_DEV, D, IA_SHARD), _WIRE), mk_land, ((N_DEV, SMALL_ROWS, D_FF), F32)],
                           [(0, True), (1, True), (2, True), (3, False)], "send_mix0_grads", 7)

    res = {}
    last = []

    def update(nm, parts, tr, restore=False):
        res[nm] = _adamw(parts, wts[nm], moms[nm], vars_[nm], tr, "adamw_" + nm, restore_b=restore, deps=last[-1:])
        last.append(res[nm][1])

    r_dn1, r_gu1 = ffn1_g.lands
    r_dn0, r_gu0 = ffn0_g.lands
    r_out1, r_ib, r_mk1 = mix1_g.lands
    update("w_gate_up", [r_gu0, r_gu1], 128)
    update("w_down", [r_dn0, r_dn1], 176)
    update("w_in_b", [r_ib], 32, True)
    r_out0, r_ia, r_mk0, r_small = mix0_g.lands
    update("w_mem_kv", [r_mk0, r_mk1], 128)
    update("w_out", [r_out0, r_out1], 128)
    update("w_in_a", [r_ia], 512)

    my = 4 * lax.axis_index("x") + 2 * lax.axis_index("y") + lax.axis_index("c")
    cq = conv_qkv_b.shape[-1]
    cf = ffn_conv_w.shape[-1]
    rc_qkv = lax.dynamic_slice_in_dim(r_small[:, SP_QKV:SP_QKV + B_CONV, :B_QKV], my * cq, cq, axis=2)[:, None]
    rc_ffn = lax.dynamic_slice_in_dim(r_small[:, SP_CW:SP_CW + 2 * FFN_CONV, :], my * cf, cf, axis=2).reshape(N_DEV, 2, FFN_CONV, cf)
    as2d = lambda a: a[None, :] if a.ndim == 1 else a
    small_out = _adamw_small(r_small, rc_qkv, rc_ffn, [as2d(wts[n]) for n in _SMALL], [as2d(moms[n]) for n in _SMALL],
                             [as2d(vars_[n]) for n in _SMALL], "adamw_small", deps=last[-1:])
    ns = len(_SMALL)
    for i, nm in enumerate(_SMALL):
        res[nm] = [small_out[k * ns + i].reshape(wts[nm].shape) for k in range(4)]

    loss = lax.psum(loss_row[0, 0], AXES)
    return (loss, dh[None], *[res[n][0] for n in order], *[res[n][1] for n in order],
            *[res[n][2] for n in order], *[res[n][3] for n in order])
```

```python
import functools
import math

import numpy as np

import jax
import jax.numpy as jnp
from jax import lax
from jax.experimental import pallas as pl
from jax.experimental.pallas import tpu as pltpu
from jax.experimental.pallas import tpu_sc as plsc

F32 = jnp.float32
_MXU = jnp.bfloat16
_ACT = jnp.bfloat16
_WIRE = jnp.bfloat16
_HI = lax.Precision.HIGH
_TM = 1024
_TM_GLU = 512
_VMEM_LIMIT = 48 * 1024 * 1024
_SDS = jax.ShapeDtypeStruct

D = 1024
EPS = 1e-6
A_HEADS, A_KV_HEADS, A_HD, BLK = 12, 2, 64, 128
N_BUCKETS, MAX_DISTANCE = 32, 128
B_QK_HEADS, B_V_HEADS, B_HD, B_CONV, CHUNK = 3, 6, 128, 4, 64
X_HEADS, X_HD, MEM_LEN = 4, 64, 256
D_FF, FFN_CONV = 2816, 3
A_Q, A_KV, X_Q = 768, 128, 256
B_QK, B_V, B_QKV = 384, 768, 1536
IN_A, IN_B = 1280, 2572
IN_BP = 2688
BP_Z, BP_XQ, BP_GATE = 1536, 2304, 2560
HALO = 8
GLU_HALO = 16

N_DEV = 8
AXES = ("x", "y", "c")
GU_SHARD = 2 * D_FF // N_DEV
FF_BLOCKS = D_FF // GU_SHARD
DN_SHARD = D_FF // N_DEV
IA_SHARD = IN_A // N_DEV

ADAM_LR, ADAM_B1, ADAM_B2, ADAM_EPS, ADAM_WD, ADAM_STEP = 0.001, 0.9, 0.999, 1e-08, 0.01, 10

SP_CB, SP_CW, SP_QKV, SP_MIX, SP_MEM, SP_FFN, SP_FINAL, SP_MISC, SMALL_ROWS = 0, 2, 8, 12, 14, 16, 18, 19, 24
SP_REL_LANE = B_QKV


def _cp(*sems):
    return pltpu.CompilerParams(dimension_semantics=sems, vmem_limit_bytes=_VMEM_LIMIT)


def _mm(a, b):
    return jnp.dot(a.astype(_MXU), b.astype(_MXU), preferred_element_type=F32)


def _mm_nt(a, b):
    return lax.dot_general(a.astype(_MXU), b.astype(_MXU), (((1,), (1,)), ((), ())), preferred_element_type=F32)


def _mm_tn(a, b):
    return lax.dot_general(a.astype(_MXU), b.astype(_MXU), (((0,), (0,)), ((), ())), preferred_element_type=F32)


def _mmf(a, b):
    return jnp.dot(a, b, preferred_element_type=F32, precision=_HI)


def _mmf_nt(a, b):
    return lax.dot_general(a, b, (((1,), (1,)), ((), ())), preferred_element_type=F32, precision=_HI)


def _mmf_tn(a, b):
    return lax.dot_general(a, b, (((0,), (0,)), ((), ())), preferred_element_type=F32, precision=_HI)


def _silu(x):
    return x * jax.nn.sigmoid(x)


def _w2d(ref):
    v = ref[...]
    return v.reshape(-1, v.shape[-1])


def _rows(m):
    return min(m, _TM)


def _spec_rowsharded(layer, rows, cols, col_block=None):
    if col_block is None:
        return pl.BlockSpec((N_DEV, None, rows, cols), lambda *_: (0, layer, 0, 0))
    return pl.BlockSpec((N_DEV, None, rows, cols), lambda *ids: (0, layer, 0, ids[col_block]))


def _spec_gate_up(axis):
    return pl.BlockSpec((None, D, GU_SHARD), lambda *ids: (ids[axis], 0, 0))


def _spec_down(axis):
    return pl.BlockSpec((2, DN_SHARD, D), lambda *ids: (ids[axis], 0, 0))


def _dep_specs(deps):
    return [pl.BlockSpec(memory_space=pl.ANY) for d in deps]


def _spec_gu_act(row_axis, axis, tm):
    return pl.BlockSpec((None, None, tm, GU_SHARD), lambda *ids: (ids[axis] // FF_BLOCKS, ids[axis] % FF_BLOCKS, ids[row_axis], 0))


def _norm_matmul(x, g, w, w_spec, n_blocks, out_shape, out_spec, name, deps=(), out_dtype=F32):
    m, k = x.shape
    tm = _rows(m)

    def body(x_ref, g_ref, w_ref, *rest):
        y_ref, hn_ref = rest[-2:]

        @pl.when(pl.program_id(1) == 0)
        def _():
            xv = x_ref[...]
            r = lax.rsqrt(jnp.mean(xv * xv, axis=-1, keepdims=True) + EPS)
            hn_ref[...] = (xv * r * g_ref[...]).astype(hn_ref.dtype)

        y_ref[...] = _mm(hn_ref[...], _w2d(w_ref)).astype(y_ref.dtype)

    return pl.pallas_call(
        body, grid=(m // tm, n_blocks),
        in_specs=[pl.BlockSpec((tm, k), lambda i, j: (i, 0)), pl.BlockSpec((1, k), lambda i, j: (0, 0)), w_spec]
        + _dep_specs(deps),
        out_specs=[out_spec, pl.BlockSpec((tm, k), lambda i, j: (i, 0))],
        out_shape=[_SDS(out_shape, out_dtype), _SDS((m, k), _ACT)],
        name=name, compiler_params=_cp("arbitrary", "arbitrary"))(x, g, w, *deps)


def _matmul_res(a, a_spec, w, w_spec, n_k, res, name):
    m, n = res.shape
    tm = _rows(m)

    def body(a_ref, w_ref, r_ref, o_ref):
        part = _mm(a_ref[...], _w2d(w_ref))

        @pl.when(pl.program_id(1) == 0)
        def _():
            o_ref[...] = r_ref[...] + part

        @pl.when(pl.program_id(1) > 0)
        def _():
            o_ref[...] += part

    return pl.pallas_call(
        body, grid=(m // tm, n_k),
        in_specs=[a_spec, w_spec, pl.BlockSpec((tm, n), lambda i, j: (i, 0))],
        out_specs=pl.BlockSpec((tm, n), lambda i, j: (i, 0)),
        out_shape=_SDS((m, n), F32), name=name, compiler_params=_cp("arbitrary", "arbitrary"))(a, w, res)


def _matmul_nt(dy, w, w_spec, n_blocks, out_shape, out_spec, name, deps=(), out_dtype=F32):
    m, n = dy.shape
    tm = _rows(m)

    def body(dy_ref, w_ref, *rest):
        o_ref = rest[-1]
        o_ref[...] = _mm_nt(dy_ref[...], _w2d(w_ref)).astype(o_ref.dtype)

    return pl.pallas_call(
        body, grid=(m // tm, n_blocks),
        in_specs=[pl.BlockSpec((tm, n), lambda i, j: (i, 0)), w_spec] + _dep_specs(deps),
        out_specs=out_spec, out_shape=_SDS(out_shape, out_dtype),
        name=name, compiler_params=_cp("arbitrary", "arbitrary"))(dy, w, *deps)


def _matmul_nt_normbwd(dy, dy_spec, w, w_spec, nj, h, g, dh_in, name):
    m, k = h.shape
    tm = _rows(m)

    def body(dy_ref, w_ref, h_ref, g_ref, dhin_ref, dh_ref, dg_ref, acc_ref):
        i, j = pl.program_id(0), pl.program_id(1)

        @pl.when(j == 0)
        def _():
            acc_ref[...] = jnp.zeros_like(acc_ref)

        acc_ref[...] += _mm_nt(dy_ref[...], _w2d(w_ref))

        @pl.when(j == nj - 1)
        def _():
            xv = h_ref[...]
            r = lax.rsqrt(jnp.mean(xv * xv, axis=-1, keepdims=True) + EPS)
            xh = xv * r
            dhn = acc_ref[...]
            part = jnp.sum(dhn * xh, axis=0, keepdims=True)

            @pl.when(i == 0)
            def _():
                dg_ref[...] = part

            @pl.when(i > 0)
            def _():
                dg_ref[...] += part

            t = dhn * g_ref[...]
            dh_ref[...] = dhin_ref[...] + r * (t - xh * jnp.mean(t * xh, axis=-1, keepdims=True))

    return pl.pallas_call(
        body, grid=(m // tm, nj),
        in_specs=[dy_spec, w_spec, pl.BlockSpec((tm, k), lambda i, j: (i, 0)), pl.BlockSpec((1, k), lambda i, j: (0, 0)),
                  pl.BlockSpec((tm, k), lambda i, j: (i, 0))],
        out_specs=[pl.BlockSpec((tm, k), lambda i, j: (i, 0)), pl.BlockSpec((1, k), lambda i, j: (0, 0))],
        out_shape=[_SDS((m, k), F32), _SDS((1, k), F32)],
        scratch_shapes=[pltpu.VMEM((tm, k), F32)],
        name=name, compiler_params=_cp("arbitrary", "arbitrary"))(dy, w, h, g, dh_in)


def _matmul_tn(x, x_spec, dy, dy_spec, m, n_blocks, acc_shape, out_shape, out_spec, name, split=None):
    tm = _rows(m)
    nm = m // tm

    def body(x_ref, dy_ref, o_ref, acc_ref):
        @pl.when(pl.program_id(1) == 0)
        def _():
            acc_ref[...] = jnp.zeros_like(acc_ref)

        acc_ref[...] += _mm_tn(x_ref[...], dy_ref[...])

        @pl.when(pl.program_id(1) == nm - 1)
        def _():
            if split is None:
                o_ref[...] = acc_ref[...].reshape(o_ref.shape).astype(o_ref.dtype)
            else:
                for d in range(N_DEV):
                    o_ref[d] = acc_ref[:, d * split:(d + 1) * split].astype(o_ref.dtype)

    return pl.pallas_call(
        body, grid=(n_blocks, nm), in_specs=[x_spec, dy_spec], out_specs=out_spec,
        out_shape=_SDS(out_shape, _WIRE), scratch_shapes=[pltpu.VMEM(acc_shape, F32)],
        name=name, compiler_params=_cp("arbitrary", "arbitrary"))(x, dy)


def _loss_head(h, g, tgt, name):
    m, k = h.shape
    tm = _rows(m)

    def body(h_ref, g_ref, t_ref, loss_ref, dh_ref, dg_ref):
        i = pl.program_id(0)
        xv = h_ref[...]
        r = lax.rsqrt(jnp.mean(xv * xv, axis=-1, keepdims=True) + EPS)
        xh = xv * r
        gv = g_ref[...]
        err = xh * gv - t_ref[...]
        lpart = jnp.zeros((1, 128), F32) + 0.5 * jnp.sum(jnp.mean(err * err, axis=-1, keepdims=True), axis=0, keepdims=True)
        dy = err * (1.0 / k)
        gpart = jnp.sum(dy * xh, axis=0, keepdims=True)

        @pl.when(i == 0)
        def _():
            loss_ref[...] = lpart
            dg_ref[...] = gpart

        @pl.when(i > 0)
        def _():
            loss_ref[...] += lpart
            dg_ref[...] += gpart

        t = dy * gv
        dh_ref[...] = r * (t - xh * jnp.mean(t * xh, axis=-1, keepdims=True))

    return pl.pallas_call(
        body, grid=(m // tm,),
        in_specs=[pl.BlockSpec((tm, k), lambda i: (i, 0)), pl.BlockSpec((1, k), lambda i: (0, 0)),
                  pl.BlockSpec((tm, k), lambda i: (i, 0))],
        out_specs=[pl.BlockSpec((1, 128), lambda i: (0, 0)), pl.BlockSpec((tm, k), lambda i: (i, 0)),
                   pl.BlockSpec((1, k), lambda i: (0, 0))],
        out_shape=[_SDS((1, 128), F32), _SDS((m, k), F32), _SDS((1, k), F32)],
        name=name, compiler_params=_cp("arbitrary"))(h, g, tgt)


def _glu_down(gu, conv_w, conv_b, w_down, res, name):
    s = gu.shape[2]
    tm = min(s, _TM_GLU)

    def body(gu_ref, prev_ref, w_ref, b_ref, wdn_ref, r_ref, o_ref, act_ref):
        i, j = pl.program_id(0), pl.program_id(1)
        prev = jnp.where(i > 0, prev_ref[...].astype(F32), 0.0)
        ext = jnp.concatenate([prev, gu_ref[0].astype(F32)], axis=0)
        gc = b_ref[...] + w_ref[FFN_CONV - 1:FFN_CONV, :] * ext
        for k in range(FFN_CONV - 1):
            gc = gc + w_ref[k:k + 1, :] * pltpu.roll(ext, FFN_CONV - 1 - k, 0)
        act = (_silu(gc[GLU_HALO:]) * gu_ref[1].astype(F32)).astype(act_ref.dtype)
        act_ref[...] = act
        part = _mm(act, _w2d(wdn_ref))

        @pl.when(j == 0)
        def _():
            o_ref[...] = r_ref[...] + part

        @pl.when(j > 0)
        def _():
            o_ref[...] += part

    return pl.pallas_call(
        body, grid=(s // tm, FF_BLOCKS),
        in_specs=[pl.BlockSpec((2, None, tm, GU_SHARD), lambda i, j: (0, j, i, 0)),
                  pl.BlockSpec((None, None, GLU_HALO, GU_SHARD),
                               lambda i, j: (0, j, jnp.maximum(i * (tm // GLU_HALO) - 1, 0), 0)),
                  pl.BlockSpec((None, HALO, GU_SHARD), lambda i, j: (j, 0, 0)),
                  pl.BlockSpec((None, 1, GU_SHARD), lambda i, j: (j, 0, 0)),
                  _spec_down(1), pl.BlockSpec((tm, D), lambda i, j: (i, 0))],
        out_specs=[pl.BlockSpec((tm, D), lambda i, j: (i, 0)), pl.BlockSpec((None, tm, GU_SHARD), lambda i, j: (j, i, 0))],
        out_shape=[_SDS((s, D), F32), _SDS((FF_BLOCKS, s, GU_SHARD), _ACT)], name=name,
        compiler_params=_cp("arbitrary", "arbitrary"))(gu, gu, conv_w, conv_b, w_down, res)


def _glu_bwd(gu, conv_w, conv_b, dh, w_down, name, deps=()):
    s = gu.shape[2]
    tm = min(s, _TM_GLU)
    nt = s // tm
    ext_rows = tm + GLU_HALO

    def body(gu_ref, prev_ref, w_ref, b_ref, dh_ref, wdn_ref, *rest):
        dgu_ref, dw_ref, db_ref, carry_ref = rest[-4:]
        t = pl.program_id(1)
        i = nt - 1 - t

        @pl.when(t == 0)
        def _():
            carry_ref[...] = jnp.zeros_like(carry_ref)
            dw_ref[...] = jnp.zeros_like(dw_ref)
            db_ref[...] = jnp.zeros_like(db_ref)

        up = gu_ref[1].astype(F32)
        prev = jnp.where(i > 0, prev_ref[...].astype(F32), 0.0)
        ext = jnp.concatenate([prev, gu_ref[0].astype(F32)], axis=0)
        shifted = [pltpu.roll(ext, FFN_CONV - 1 - j, 0) if j < FFN_CONV - 1 else ext for j in range(FFN_CONV)]
        gc = b_ref[...] + shifted[0] * w_ref[0:1, :]
        for j in range(1, FFN_CONV):
            gc = gc + shifted[j] * w_ref[j:j + 1, :]
        gc = gc[GLU_HALO:]
        sg = jax.nn.sigmoid(gc)
        da = _mm_nt(dh_ref[...], _w2d(wdn_ref))
        dup = da * (gc * sg)
        dgc = da * up * (sg * (1.0 + gc * (1.0 - sg)))
        db_ref[...] += jnp.sum(dgc, axis=0, keepdims=True)
        dgc_ext = jnp.concatenate([jnp.zeros((GLU_HALO, GU_SHARD), F32), dgc], axis=0)
        dext = dgc_ext * w_ref[FFN_CONV - 1:FFN_CONV, :]
        for j in range(FFN_CONV):
            dw_ref[j:j + 1, :] += jnp.sum(shifted[j] * dgc_ext, axis=0, keepdims=True)
            if j < FFN_CONV - 1:
                dext = dext + w_ref[j:j + 1, :] * pltpu.roll(dgc_ext, ext_rows - (FFN_CONV - 1 - j), 0)
        tail = jnp.concatenate([jnp.zeros((tm - GLU_HALO, GU_SHARD), F32), carry_ref[...]], axis=0)
        dgate = dext[GLU_HALO:] + tail
        carry_ref[...] = dext[:GLU_HALO]
        dgu_ref[0] = dgate.astype(dgu_ref.dtype)
        dgu_ref[1] = dup.astype(dgu_ref.dtype)

    return pl.pallas_call(
        body, grid=(FF_BLOCKS, nt),
        in_specs=[pl.BlockSpec((2, None, tm, GU_SHARD), lambda j, t: (0, j, nt - 1 - t, 0)),
                  pl.BlockSpec((None, None, GLU_HALO, GU_SHARD),
                               lambda j, t: (0, j, jnp.maximum((nt - 1 - t) * (tm // GLU_HALO) - 1, 0), 0)),
                  pl.BlockSpec((None, HALO, GU_SHARD), lambda j, t: (j, 0, 0)),
                  pl.BlockSpec((None, 1, GU_SHARD), lambda j, t: (j, 0, 0)),
                  pl.BlockSpec((tm, D), lambda j, t: (nt - 1 - t, 0)), _spec_down(0)] + _dep_specs(deps),
        out_specs=[pl.BlockSpec((2, None, tm, GU_SHARD), lambda j, t: (0, j, nt - 1 - t, 0)),
                   pl.BlockSpec((None, HALO, GU_SHARD), lambda j, t: (j, 0, 0)),
                   pl.BlockSpec((None, 1, GU_SHARD), lambda j, t: (j, 0, 0))],
        out_shape=[_SDS(gu.shape, _ACT), _SDS((FF_BLOCKS, HALO, GU_SHARD), F32), _SDS((FF_BLOCKS, 1, GU_SHARD), F32)],
        scratch_shapes=[pltpu.VMEM((GLU_HALO, GU_SHARD), F32)],
        name=name, compiler_params=_cp("arbitrary", "arbitrary"))(gu, gu, conv_w, conv_b, dh, w_down, *deps)


def _bucket_table():
    qi = np.arange(BLK)[:, None]
    kj = np.arange(BLK)[None, :]
    n = np.where(kj > qi, BLK + qi - kj, qi - kj)
    max_exact = N_BUCKETS // 2
    nf = np.maximum(n, 1).astype(np.float32)
    large = max_exact + (np.log(nf / max_exact) / math.log(MAX_DISTANCE / max_exact)
                         * (N_BUCKETS - max_exact)).astype(np.int32)
    large = np.minimum(large, N_BUCKETS - 1)
    return np.where(n < max_exact, n, large).astype(np.int32)


def _lane_low():
    return lax.broadcasted_iota(jnp.int32, (1, 128), 1) < A_HD


def _swa_group(q, kd, vd, sink, bias, upper, first):
    n = A_HEADS // A_KV_HEADS
    low = _lane_low()
    pairs = [q[:, p * 128:(p + 1) * 128] for p in range(n // 2)]
    qm = jnp.concatenate([jnp.where(low == (h % 2 == 0), pairs[h // 2], 0.0) for h in range(n)], axis=0)
    s2 = _mm_nt(qm, kd) * (A_HD ** -0.5)
    s = jnp.where(upper[None], s2[:, :BLK].reshape(n, BLK, BLK), s2[:, BLK:].reshape(n, BLK, BLK)) + bias
    s = jnp.where((upper & first)[None], -jnp.inf, s)
    m = jnp.maximum(jnp.max(s, axis=-1, keepdims=True), sink)
    p = jnp.exp(s - m)
    split = jnp.concatenate([jnp.where(upper[None], p, 0.0), jnp.where(upper[None], 0.0, p)], axis=-1)
    split = split.reshape(n * BLK, 2 * BLK)
    den = _mm(p.reshape(n * BLK, BLK), jnp.ones((BLK, 128), F32)) + jnp.exp(sink - m).reshape(n * BLK, 1)
    o = _mm(split, vd) / den
    return jnp.concatenate([jnp.where(low, o[2 * p * BLK:(2 * p + 1) * BLK], o[(2 * p + 1) * BLK:(2 * p + 2) * BLK])
                            for p in range(n // 2)], axis=1)


def _swa_sinks(sink_ref, g):
    n = A_HEADS // A_KV_HEADS
    return jnp.concatenate([sink_ref[:, h:h + 1] for h in range(g * n, (g + 1) * n)], axis=0).reshape(n, 1, 1)


def _both_halves(t, t_rolled, g):
    low = _lane_low()
    return jnp.where(low, t, t_rolled) if g == 0 else jnp.where(low, t_rolled, t)


def _cross_pairs(q, mk, mv):
    rows = q.shape[0]
    low = _lane_low()
    qm = [jnp.concatenate([jnp.where(low, q[:, p * 128:(p + 1) * 128], 0.0), jnp.where(low, 0.0, q[:, p * 128:(p + 1) * 128])], axis=0)
          for p in range(X_HEADS // 2)]
    s = [_mm_nt(qm[p], mk[:, p * 128:(p + 1) * 128]) * (X_HD ** -0.5) for p in range(X_HEADS // 2)]
    e = [jnp.exp(t - jnp.max(t, axis=-1, keepdims=True)) for t in s]
    pr = [t / jnp.sum(t, axis=-1, keepdims=True) for t in e]
    o = [_mm(pr[p], mv[:, p * 128:(p + 1) * 128]) for p in range(X_HEADS // 2)]
    return jnp.concatenate([jnp.where(low, t[:rows], t[rows:]) for t in o], axis=1)


def _swa_upper():
    qi = lax.broadcasted_iota(jnp.int32, (BLK, BLK), 0)
    kj = lax.broadcasted_iota(jnp.int32, (BLK, BLK), 1)
    return kj > qi


def _bias_build(rel_bias, bucket, name):
    def body(rb_ref, bucket_ref, o_ref):
        b = bucket_ref[...]
        for h in range(A_HEADS):
            acc = jnp.zeros((BLK, BLK), F32)
            for k in range(N_BUCKETS):
                acc = jnp.where(b == k, rb_ref[k, h], acc)
            o_ref[h] = acc

    return pl.pallas_call(
        body, in_specs=[pl.BlockSpec(memory_space=pltpu.SMEM), pl.BlockSpec(memory_space=pltpu.VMEM)],
        out_specs=pl.BlockSpec(memory_space=pltpu.VMEM),
        out_shape=_SDS((A_HEADS, BLK, BLK), F32), name=name)(rel_bias, bucket)


def _bias_reduce(dbias, bucket, name):
    def body(db_ref, bucket_ref, o_ref):
        b = bucket_ref[...]
        row = lax.broadcasted_iota(jnp.int32, (N_BUCKETS, 128), 0)
        lane = lax.broadcasted_iota(jnp.int32, (N_BUCKETS, 128), 1)
        acc = jnp.zeros((N_BUCKETS, 128), F32)
        for h in range(A_HEADS):
            v = db_ref[h]
            for k in range(N_BUCKETS):
                sk = jnp.sum(jnp.sum(jnp.where(b == k, v, 0.0), axis=1, keepdims=True), axis=0, keepdims=True)
                acc = acc + jnp.where((row == k) & (lane == h), sk, 0.0)
        o_ref[...] = acc

    return pl.pallas_call(
        body, in_specs=[pl.BlockSpec(memory_space=pltpu.VMEM)] * 2,
        out_specs=pl.BlockSpec(memory_space=pltpu.VMEM),
        out_shape=_SDS((N_BUCKETS, 128), F32), name=name)(dbias, bucket)


def _mix_a_fwd(proj, bias, sinks, memkv, name):
    s = proj.shape[0]
    nb = s // BLK
    grp = A_HEADS // A_KV_HEADS

    def body(proj_ref, prev_ref, bias_ref, sink_ref, memkv_ref, o_ref):
        i = pl.program_id(0)
        upper = _swa_upper()
        prev = prev_ref[...].astype(F32)
        proj = proj_ref[...].astype(F32)
        kb = jnp.concatenate([prev[:, :A_KV], proj[:, A_Q:A_Q + A_KV]], axis=0)
        vb = jnp.concatenate([prev[:, A_KV:], proj[:, A_Q + A_KV:A_Q + 2 * A_KV]], axis=0)
        kb_r = pltpu.roll(kb, A_HD, 1)
        vb_r = pltpu.roll(vb, A_HD, 1)
        gw = A_Q // A_KV_HEADS
        outs = [_swa_group(proj[:, g * gw:(g + 1) * gw], _both_halves(kb, kb_r, g), _both_halves(vb, vb_r, g),
                           _swa_sinks(sink_ref, g), bias_ref[g * grp:(g + 1) * grp], upper, i == 0) for g in range(A_KV_HEADS)]
        outs.append(_cross_pairs(proj[:, A_Q + 2 * A_KV:], memkv_ref[:, :X_Q], memkv_ref[:, X_Q:]))
        o_ref[...] = jnp.concatenate(outs, axis=1).astype(o_ref.dtype)

    return pl.pallas_call(
        body, grid=(nb,),
        in_specs=[pl.BlockSpec((BLK, IN_A), lambda i: (i, 0)),
                  pl.BlockSpec((BLK, 2 * A_KV), lambda i: (jnp.maximum(i - 1, 0), A_Q // (2 * A_KV))),
                  pl.BlockSpec((A_HEADS, BLK, BLK), lambda i: (0, 0, 0)),
                  pl.BlockSpec((1, 128), lambda i: (0, 0)),
                  pl.BlockSpec((MEM_LEN, 2 * X_Q), lambda i: (0, 0))],
        out_specs=pl.BlockSpec((BLK, D), lambda i: (i, 0)),
        out_shape=_SDS((s, D), _ACT), name=name, compiler_params=_cp("arbitrary"))(proj, proj, bias, sinks, memkv)


def _mix_a_bwd(proj, bias, sinks, memkv, dmix, name):
    s = proj.shape[0]
    nb = s // BLK
    grp = A_HEADS // A_KV_HEADS

    def body(proj_ref, prev_ref, bias_ref, sink_ref, memkv_ref, dmix_ref,
             dproj_ref, dbias_ref, dsink_ref, dmemkv_ref, carry_ref):
        t = pl.program_id(0)
        i = nb - 1 - t

        @pl.when(t == 0)
        def _():
            carry_ref[...] = jnp.zeros_like(carry_ref)
            dbias_ref[...] = jnp.zeros_like(dbias_ref)
            dsink_ref[...] = jnp.zeros_like(dsink_ref)
            dmemkv_ref[...] = jnp.zeros_like(dmemkv_ref)

        upper = _swa_upper()
        lane = lax.broadcasted_iota(jnp.int32, (1, 128), 1)
        low = _lane_low()
        prev = prev_ref[...].astype(F32)
        proj = proj_ref[...].astype(F32)
        kb = jnp.concatenate([prev[:, :A_KV], proj[:, A_Q:A_Q + A_KV]], axis=0)
        vb = jnp.concatenate([prev[:, A_KV:], proj[:, A_Q + A_KV:A_Q + 2 * A_KV]], axis=0)
        kb_r = pltpu.roll(kb, A_HD, 1)
        vb_r = pltpu.roll(vb, A_HD, 1)
        gw = A_Q // A_KV_HEADS
        dqs, dkd, dvd = [], [], []
        dsink = jnp.zeros((1, 128), F32)
        for g in range(A_KV_HEADS):
            _, vjp = jax.vjp(functools.partial(_swa_group, upper=upper, first=i == 0), proj[:, g * gw:(g + 1) * gw],
                             _both_halves(kb, kb_r, g), _both_halves(vb, vb_r, g), _swa_sinks(sink_ref, g),
                             bias_ref[g * grp:(g + 1) * grp])
            dq, dk, dv, ds, db = vjp(dmix_ref[:, g * gw:(g + 1) * gw].astype(F32))
            dqs.append(dq)
            dkd.append(dk + pltpu.roll(dk, A_HD, 1))
            dvd.append(dv + pltpu.roll(dv, A_HD, 1))
            for h in range(grp):
                dsink = dsink + jnp.where(lane == g * grp + h, ds[h], 0.0)
            dbias_ref[g * grp:(g + 1) * grp] += db
        dsink_ref[...] += dsink
        dkb = jnp.where(low, dkd[0], dkd[1])
        dvb = jnp.where(low, dvd[0], dvd[1])
        _, vjp = jax.vjp(_cross_pairs, proj[:, A_Q + 2 * A_KV:], memkv_ref[:, :X_Q], memkv_ref[:, X_Q:])
        dxq, dmk, dmv = vjp(dmix_ref[:, A_Q:].astype(F32))
        dmemkv_ref[...] += jnp.concatenate([dmk, dmv], axis=1)
        dkv_cur = jnp.concatenate([dkb[BLK:], dvb[BLK:]], axis=1) + carry_ref[...]
        carry_ref[...] = jnp.concatenate([dkb[:BLK], dvb[:BLK]], axis=1)
        dproj_ref[...] = jnp.concatenate(dqs + [dkv_cur, dxq], axis=1).astype(dproj_ref.dtype)

    return pl.pallas_call(
        body, grid=(nb,),
        in_specs=[pl.BlockSpec((BLK, IN_A), lambda t: (nb - 1 - t, 0)),
                  pl.BlockSpec((BLK, 2 * A_KV), lambda t: (jnp.maximum(nb - 2 - t, 0), A_Q // (2 * A_KV))),
                  pl.BlockSpec((A_HEADS, BLK, BLK), lambda t: (0, 0, 0)),
                  pl.BlockSpec((1, 128), lambda t: (0, 0)),
                  pl.BlockSpec((MEM_LEN, 2 * X_Q), lambda t: (0, 0)),
                  pl.BlockSpec((BLK, D), lambda t: (nb - 1 - t, 0))],
        out_specs=[pl.BlockSpec((BLK, IN_A), lambda t: (nb - 1 - t, 0)),
                   pl.BlockSpec((A_HEADS, BLK, BLK), lambda t: (0, 0, 0)),
                   pl.BlockSpec((1, 128), lambda t: (0, 0)),
                   pl.BlockSpec((MEM_LEN, 2 * X_Q), lambda t: (0, 0))],
        out_shape=[_SDS((s, IN_A), _ACT), _SDS((A_HEADS, BLK, BLK), F32), _SDS((1, 128), F32),
                   _SDS((MEM_LEN, 2 * X_Q), F32)],
        scratch_shapes=[pltpu.VMEM((BLK, 2 * A_KV), F32)],
        name=name, compiler_params=_cp("arbitrary"))(proj, proj, bias, sinks, memkv, dmix)


def _dn_heads(yq, yk, yv, z, bl, al, a_log, dtb, ng, s0):
    c = CHUNK
    nh = B_V_HEADS
    rep = B_V_HEADS // B_QK_HEADS
    r = lax.broadcasted_iota(jnp.int32, (c, c), 0)
    cc = lax.broadcasted_iota(jnp.int32, (c, c), 1)
    q = [_silu(t) for t in yq]
    k = [_silu(t) for t in yk]
    v = [_silu(t) for t in yv]
    q = [t * lax.rsqrt(jnp.sum(t * t, axis=-1, keepdims=True) + EPS) * (B_HD ** -0.5) for t in q]
    k = [t * lax.rsqrt(jnp.sum(t * t, axis=-1, keepdims=True) + EPS) for t in k]
    beta = [jax.nn.sigmoid(t) for t in bl]
    g = [-jnp.exp(a_log[h]) * jax.nn.softplus(al[h] + dtb[h]) for h in range(nh)]
    gb = [jnp.broadcast_to(t, (c, c)) for t in g]
    gc_col = [jnp.sum(jnp.where(cc <= r, t.T, 0.0), axis=1, keepdims=True) for t in gb]
    gc_row = [jnp.sum(jnp.where(r <= cc, t, 0.0), axis=0, keepdims=True) for t in gb]
    gc_last = [jnp.sum(t, axis=0, keepdims=True) for t in g]
    decay = [jnp.exp(jnp.where(r >= cc, gc_col[h] - gc_row[h], -jnp.inf)) for h in range(nh)]
    kq = [_mmf_nt(jnp.concatenate([k[h], q[h]], axis=0), k[h]) for h in range(B_QK_HEADS)]
    kk = [t[:c] for t in kq]
    qk = [t[c:] for t in kq]
    egc = [jnp.exp(t) for t in gc_col]
    both = [_mmf(jnp.concatenate([(beta[h] * egc[h]) * k[h // rep], q[h // rep] * egc[h]], axis=0), s0[h]) for h in range(nh)]
    rhs = [beta[h] * v[h] - both[h][:c] for h in range(nh)]
    qs0 = [t[c:] for t in both]
    pw = [-(beta[h] * kk[h // rep] * jnp.where(r > cc, decay[h], 0.0)) for h in range(nh)]
    x = rhs
    for lvl in range(6):
        if lvl < 5:
            prod = [_mmf(pw[h], jnp.concatenate([x[h], pw[h]], axis=1)) for h in range(nh)]
            x = [x[h] + prod[h][:, :B_HD] for h in range(nh)]
            pw = [t[:, B_HD:] for t in prod]
        else:
            x = [x[h] + _mmf(pw[h], x[h]) for h in range(nh)]
    delta = x
    last = [_mmf(jnp.concatenate([qk[h // rep] * decay[h], (k[h // rep] * jnp.exp(gc_last[h] - gc_col[h])).T], axis=0), delta[h])
            for h in range(nh)]
    out = [qs0[h] + last[h][:c] for h in range(nh)]
    s1 = [jnp.exp(gc_last[h]) * s0[h] + last[h][c:] for h in range(nh)]
    o = [t * lax.rsqrt(jnp.mean(t * t, axis=-1, keepdims=True) + EPS) * ng for t in out]
    return [o[h] * _silu(z[h]) for h in range(nh)], s1


def _dn_conv(ext, w_ref):
    y = ext * w_ref[B_CONV - 1:B_CONV, :]
    for j in range(B_CONV - 1):
        y = y + w_ref[j:j + 1, :] * pltpu.roll(ext, B_CONV - 1 - j, 0)
    return y


def _dn_args(y, cur_ref, par_ref, ng_ref):
    nh = B_V_HEADS
    return ([y[:, h * B_HD:(h + 1) * B_HD] for h in range(B_QK_HEADS)],
            [y[:, B_QK + h * B_HD:B_QK + (h + 1) * B_HD] for h in range(B_QK_HEADS)],
            [y[:, 2 * B_QK + h * B_HD:2 * B_QK + (h + 1) * B_HD] for h in range(nh)],
            [cur_ref[:, BP_Z + h * B_HD:BP_Z + (h + 1) * B_HD] for h in range(nh)],
            [cur_ref[:, BP_GATE + h:BP_GATE + h + 1] for h in range(nh)],
            [cur_ref[:, BP_GATE + nh + h:BP_GATE + nh + h + 1] for h in range(nh)],
            [par_ref[:, h:h + 1] for h in range(nh)], [par_ref[:, nh + h:nh + h + 1] for h in range(nh)], ng_ref[...])


def _mix_b_fwd(proj, conv_w, par, ng, memkv, name):
    s = proj.shape[0]
    nc = s // CHUNK

    def body(cur_ref, prev_ref, w_ref, par_ref, ng_ref, memkv_ref, o_ref, st_ref, state_ref):
        n = pl.program_id(0)

        @pl.when(n == 0)
        def _():
            state_ref[...] = jnp.zeros_like(state_ref)

        prev = jnp.where(n > 0, prev_ref[...], 0.0)
        ext = jnp.concatenate([prev, cur_ref[:, :B_QKV]], axis=0)
        y = _dn_conv(ext, w_ref)[HALO:]
        s0 = [state_ref[hv] for hv in range(B_V_HEADS)]
        st_ref[0] = state_ref[...]
        outs, s1 = _dn_heads(*_dn_args(y, cur_ref, par_ref, ng_ref), s0)
        for hv in range(B_V_HEADS):
            state_ref[hv] = s1[hv]
        outs = outs + [_cross_pairs(cur_ref[:, BP_XQ:BP_XQ + X_Q], memkv_ref[:, :X_Q], memkv_ref[:, X_Q:])]
        o_ref[...] = jnp.concatenate(outs, axis=1).astype(o_ref.dtype)

    return pl.pallas_call(
        body, grid=(nc,),
        in_specs=[pl.BlockSpec((CHUNK, IN_BP), lambda n: (n, 0)),
                  pl.BlockSpec((HALO, B_QKV), lambda n: (jnp.maximum(n * (CHUNK // HALO) - 1, 0), 0)),
                  pl.BlockSpec((HALO, B_QKV), lambda n: (0, 0)),
                  pl.BlockSpec((1, 128), lambda n: (0, 0)), pl.BlockSpec((1, 128), lambda n: (0, 0)),
                  pl.BlockSpec((MEM_LEN, 2 * X_Q), lambda n: (0, 0))],
        out_specs=[pl.BlockSpec((CHUNK, D), lambda n: (n, 0)),
                   pl.BlockSpec((1, B_V_HEADS, B_HD, B_HD), lambda n: (n, 0, 0, 0))],
        out_shape=[_SDS((s, D), _ACT), _SDS((nc, B_V_HEADS, B_HD, B_HD), F32)],
        scratch_shapes=[pltpu.VMEM((B_V_HEADS, B_HD, B_HD), F32)],
        name=name, compiler_params=_cp("arbitrary"))(proj, proj, conv_w, par, ng, memkv)


def _mix_b_bwd(proj, conv_w, par, ng, memkv, states, dmix, name):
    s = proj.shape[0]
    nc = s // CHUNK
    ext_rows = CHUNK + HALO

    def body(cur_ref, prev_ref, w_ref, par_ref, ng_ref, memkv_ref, st_ref, dmix_ref,
             dproj_ref, dw_ref, dpar_ref, dng_ref, dmemkv_ref, dstate_ref, carry_ref):
        t = pl.program_id(0)
        n = nc - 1 - t

        @pl.when(t == 0)
        def _():
            dstate_ref[...] = jnp.zeros_like(dstate_ref)
            carry_ref[...] = jnp.zeros_like(carry_ref)
            dw_ref[...] = jnp.zeros_like(dw_ref)
            dpar_ref[...] = jnp.zeros_like(dpar_ref)
            dng_ref[...] = jnp.zeros_like(dng_ref)
            dmemkv_ref[...] = jnp.zeros_like(dmemkv_ref)

        lane = lax.broadcasted_iota(jnp.int32, (1, 128), 1)
        prev = jnp.where(n > 0, prev_ref[...], 0.0)
        ext = jnp.concatenate([prev, cur_ref[:, :B_QKV]], axis=0)
        y = _dn_conv(ext, w_ref)[HALO:]
        _, vjp = jax.vjp(_dn_heads, *_dn_args(y, cur_ref, par_ref, ng_ref), [st_ref[0, hv] for hv in range(B_V_HEADS)])
        dyq, dyk, dyv, dz, gbl, gal, ga_log, gdtb, dng, gs0 = vjp(
            ([dmix_ref[:, hv * B_HD:(hv + 1) * B_HD].astype(F32) for hv in range(B_V_HEADS)],
             [dstate_ref[hv] for hv in range(B_V_HEADS)]))
        dgate = jnp.zeros((CHUNK, 128), F32)
        dpar = jnp.zeros((1, 128), F32)
        for hv in range(B_V_HEADS):
            dstate_ref[hv] = gs0[hv]
            dgate = dgate + jnp.where(lane == hv, gbl[hv], 0.0) + jnp.where(lane == B_V_HEADS + hv, gal[hv], 0.0)
            dpar = dpar + jnp.where(lane == hv, ga_log[hv], 0.0) + jnp.where(lane == B_V_HEADS + hv, gdtb[hv], 0.0)
        dpar_ref[...] += dpar
        dng_ref[...] += dng
        _, vjp = jax.vjp(_cross_pairs, cur_ref[:, BP_XQ:BP_XQ + X_Q], memkv_ref[:, :X_Q], memkv_ref[:, X_Q:])
        dxq, dmk, dmv = vjp(dmix_ref[:, B_V:].astype(F32))
        dmemkv_ref[...] += jnp.concatenate([dmk, dmv], axis=1)
        dy = jnp.concatenate(list(dyq) + list(dyk) + list(dyv), axis=1)
        dy_ext = jnp.concatenate([jnp.zeros((HALO, B_QKV), F32), dy], axis=0)
        dext = dy_ext * w_ref[B_CONV - 1:B_CONV, :]
        dw_ref[B_CONV - 1:B_CONV, :] += jnp.sum(ext * dy_ext, axis=0, keepdims=True)
        for j in range(B_CONV - 1):
            sh = B_CONV - 1 - j
            dw_ref[j:j + 1, :] += jnp.sum(pltpu.roll(ext, sh, 0) * dy_ext, axis=0, keepdims=True)
            dext = dext + w_ref[j:j + 1, :] * pltpu.roll(dy_ext, ext_rows - sh, 0)
        tail = jnp.concatenate([jnp.zeros((CHUNK - HALO, B_QKV), F32), carry_ref[...]], axis=0)
        dqkv = dext[HALO:] + tail
        carry_ref[...] = dext[:HALO]
        dproj_ref[...] = jnp.concatenate([dqkv] + list(dz) + [dxq, dgate], axis=1).astype(dproj_ref.dtype)

    return pl.pallas_call(
        body, grid=(nc,),
        in_specs=[pl.BlockSpec((CHUNK, IN_BP), lambda t: (nc - 1 - t, 0)),
                  pl.BlockSpec((HALO, B_QKV), lambda t: (jnp.maximum((nc - 1 - t) * (CHUNK // HALO) - 1, 0), 0)),
                  pl.BlockSpec((HALO, B_QKV), lambda t: (0, 0)),
                  pl.BlockSpec((1, 128), lambda t: (0, 0)), pl.BlockSpec((1, 128), lambda t: (0, 0)),
                  pl.BlockSpec((MEM_LEN, 2 * X_Q), lambda t: (0, 0)),
                  pl.BlockSpec((1, B_V_HEADS, B_HD, B_HD), lambda t: (nc - 1 - t, 0, 0, 0)),
                  pl.BlockSpec((CHUNK, D), lambda t: (nc - 1 - t, 0))],
        out_specs=[pl.BlockSpec((CHUNK, IN_BP), lambda t: (nc - 1 - t, 0)),
                   pl.BlockSpec((HALO, B_QKV), lambda t: (0, 0)),
                   pl.BlockSpec((1, 128), lambda t: (0, 0)), pl.BlockSpec((1, 128), lambda t: (0, 0)),
                   pl.BlockSpec((MEM_LEN, 2 * X_Q), lambda t: (0, 0))],
        out_shape=[_SDS((s, IN_BP), _ACT), _SDS((HALO, B_QKV), F32), _SDS((1, 128), F32), _SDS((1, 128), F32),
                   _SDS((MEM_LEN, 2 * X_Q), F32)],
        scratch_shapes=[pltpu.VMEM((B_V_HEADS, B_HD, B_HD), F32), pltpu.VMEM((HALO, B_QKV), F32)],
        name=name, compiler_params=_cp("arbitrary"))(proj, proj, conv_w, par, ng, memkv, states, dmix)


def _place():
    return lax.axis_index("x"), lax.axis_index("y"), lax.axis_index("c")


def _all_gather(shards, name):
    n = len(shards)

    def body(*refs):
        ins, outs = refs[:n], refs[n:2 * n]
        send_sems, recv_sems, local_sems = refs[2 * n:]
        x, y, c = _place()
        me, sibling = (x, y, c), (x, y, 1 - c)
        chips = [(1 - x, y), (x, 1 - y), (1 - x, 1 - y)]

        def rows(a, px, py, pc):
            return outs[a].at[4 * px + 2 * py + pc]

        def copy(a, k, block, to, src=None):
            return pltpu.make_async_remote_copy(
                src_ref=rows(a, *block) if src is None else src, dst_ref=rows(a, *block),
                send_sem=send_sems.at[a, k], recv_sem=recv_sems.at[a, k],
                device_id=to, device_id_type=pl.DeviceIdType.MESH)

        mine = [pltpu.make_async_copy(ins[a], rows(a, *me), local_sems.at[a]) for a in range(n)]
        for cp in mine:
            cp.start()
        first = []
        for a in range(n):
            first.append(copy(a, 0, me, sibling, src=ins[a]))
            first += [copy(a, 1 + j, me, (*chip, c), src=ins[a]) for j, chip in enumerate(chips)]
        for cp in first:
            cp.start()
        passed = []
        for j, chip in enumerate(chips):
            for a in range(n):
                copy(a, 1 + j, (*chip, c), me).wait_recv()
                fwd = copy(a, 4 + j, (*chip, c), sibling)
                fwd.start()
                passed.append(fwd)
        for a in range(n):
            copy(a, 0, sibling, me).wait_recv()
            for j, chip in enumerate(chips):
                copy(a, 4 + j, (*chip, 1 - c), me).wait_recv()
        for cp in first + passed:
            cp.wait_send()
        for cp in mine:
            cp.wait()

    hbm = pl.BlockSpec(memory_space=pl.ANY)
    return pl.pallas_call(
        body, out_shape=[_SDS((N_DEV,) + s.shape, s.dtype) for s in shards],
        in_specs=[hbm] * n, out_specs=[hbm] * n,
        scratch_shapes=[pltpu.SemaphoreType.DMA((n, 7)), pltpu.SemaphoreType.DMA((n, 7)), pltpu.SemaphoreType.DMA((n,))],
        name=name)(*shards)


class _Exchange:
    def __init__(self, lands, srcs):
        self.lands, self.srcs = lands, srcs


def _seq_exchange(srcs, land_shapes, plan, name, cid):
    n, nl = len(srcs), len(land_shapes)

    def launch(*refs):
        src_refs, land_refs = refs[:n], refs[n:n + nl]
        send_sems, recv_sems, local_sems = refs[n + nl:]
        x, y, c = _place()
        my = 4 * x + 2 * y + c
        peers = [(x ^ ((k + 1) >> 2 & 1), y ^ ((k + 1) >> 1 & 1), c ^ ((k + 1) & 1)) for k in range(N_DEV - 1)]
        barrier = pltpu.get_barrier_semaphore()
        for p in peers:
            pl.semaphore_signal(barrier, inc=1, device_id=p, device_id_type=pl.DeviceIdType.MESH)
        pl.semaphore_wait(barrier, N_DEV - 1)

        def src_for(a, dest):
            return src_refs[a].at[dest] if plan[a][1] else src_refs[a]

        def slot(a, source):
            return land_refs[plan[a][0]].at[source]

        mine = [pltpu.make_async_copy(src_for(a, my), slot(a, my), local_sems.at[a]) for a in range(n)]
        for cp in mine:
            cp.start()
        sends, recvs = [], []
        for k, (px, py, pc) in enumerate(peers):
            peer = 4 * px + 2 * py + pc
            for a in range(n):
                kw = dict(send_sem=send_sems.at[a * (N_DEV - 1) + k], recv_sem=recv_sems.at[a * (N_DEV - 1) + k],
                          device_id=(px, py, pc), device_id_type=pl.DeviceIdType.MESH)
                sends.append(pltpu.make_async_remote_copy(src_ref=src_for(a, peer), dst_ref=slot(a, my), **kw))
                recvs.append(pltpu.make_async_remote_copy(src_ref=src_for(a, my), dst_ref=slot(a, peer), **kw))
        for cp in sends:
            cp.start()
        for cp in recvs:
            cp.wait_recv()
        for cp in sends:
            cp.wait_send()
        for cp in mine:
            cp.wait()

    lands = pl.kernel(
        launch, out_type=[_SDS(s, d) for s, d in land_shapes],
        mesh=plsc.ScalarSubcoreMesh(axis_name="sequencer", num_cores=1), name=name,
        scratch_types=(pltpu.SemaphoreType.DMA((n * (N_DEV - 1),)), pltpu.SemaphoreType.DMA((n * (N_DEV - 1),)),
                       pltpu.SemaphoreType.DMA((n,))),
        compiler_params=pltpu.CompilerParams(collective_id=cid))(*srcs)
    return _Exchange(list(lands), list(srcs))


def _adam_update(g, w, m, v):
    c1 = 1.0 - ADAM_B1 ** ADAM_STEP
    c2 = 1.0 - ADAM_B2 ** ADAM_STEP
    mm = ADAM_B1 * m + (1.0 - ADAM_B1) * g
    vv = ADAM_B2 * v + (1.0 - ADAM_B2) * (g * g)
    delta = -ADAM_LR * ((mm / c1) / (jnp.sqrt(vv / c2) + ADAM_EPS) + ADAM_WD * w)
    return delta, mm, vv


def _sum_sources(p_ref):
    g = p_ref[0].astype(F32)
    for s in range(1, N_DEV):
        g = g + p_ref[s].astype(F32)
    return g


def _adamw(parts, w, m, v, tr, name, restore_b=False, deps=()):
    nl, r, c = w.shape
    cp = parts[0].shape[-1]

    def body(*refs):
        p_refs = refs[:nl]
        w_ref, m_ref, v_ref = refs[nl:nl + 3]
        g_ref, d_ref, nm_ref, nv_ref = refs[-4:]
        g = _sum_sources(p_refs[0])
        for l in range(1, nl):
            g = jnp.where(pl.program_id(0) == l, _sum_sources(p_refs[l]), g)
        if restore_b:
            g = jnp.concatenate([g[:, :BP_XQ], g[:, BP_GATE:BP_GATE + 2 * B_V_HEADS], g[:, BP_XQ:BP_GATE]], axis=1)
        delta, mm, vv = _adam_update(g, w_ref[...], m_ref[...], v_ref[...])
        g_ref[...] = g
        d_ref[...] = delta
        nm_ref[...] = mm
        nv_ref[...] = vv

    spec = pl.BlockSpec((None, tr, c), lambda l, i: (l, i, 0))
    part_specs = [pl.BlockSpec((N_DEV, tr, cp), functools.partial(lambda l, i, k: (0, jnp.where(l == k, i, 0), 0), k=k))
                  for k in range(nl)]
    return pl.pallas_call(
        body, grid=(nl, r // tr),
        in_specs=part_specs + [spec, spec, spec] + _dep_specs(deps),
        out_specs=[spec] * 4, out_shape=[_SDS(w.shape, F32)] * 4,
        name=name, compiler_params=_cp("arbitrary", "arbitrary"))(*parts, w, m, v, *deps)


def _pack_small(d_rel, d_cb, d_cw, d_qkv, d_mix, d_mem, d_ffn, d_final, d_sinks, d_par, d_ng, name):
    flat = [d_rel, *d_cb, *d_cw, d_qkv, *d_mix, *d_mem, *d_ffn, d_final, d_sinks, d_par, d_ng]
    n = len(flat)

    def body(*refs):
        ins, o_ref = refs[:n], refs[n]
        rel, cb0, cb1, cw0, cw1, qkv, mx0, mx1, me0, me1, ff0, ff1, fin, snk, par, ng = ins
        o_ref[...] = jnp.zeros_like(o_ref)
        for k in range(N_BUCKETS):
            lane = SP_REL_LANE + 128 * (k % 8)
            o_ref[SP_QKV + k // 8:SP_QKV + k // 8 + 1, lane:lane + 128] = rel[k:k + 1, :]
        for l, (cb, cw) in enumerate(((cb0, cw0), (cb1, cw1))):
            o_ref[SP_CB + l:SP_CB + l + 1, :] = jnp.concatenate([cb[j] for j in range(FF_BLOCKS)], axis=1)
            full = jnp.concatenate([cw[j] for j in range(FF_BLOCKS)], axis=1)
            o_ref[SP_CW + FFN_CONV * l:SP_CW + FFN_CONV * (l + 1), :] = full[:FFN_CONV]
        o_ref[SP_QKV:SP_QKV + B_CONV, 0:B_QKV] = qkv[0:B_CONV, :]
        for base, pair in ((SP_MIX, (mx0, mx1)), (SP_MEM, (me0, me1)), (SP_FFN, (ff0, ff1))):
            for l in range(2):
                o_ref[base + l:base + l + 1, 0:D] = pair[l][...]
        o_ref[SP_FINAL:SP_FINAL + 1, 0:D] = fin[...]
        o_ref[SP_MISC:SP_MISC + 1, 0:128] = snk[...]
        o_ref[SP_MISC:SP_MISC + 1, 128:256] = par[...]
        o_ref[SP_MISC:SP_MISC + 1, 256:384] = ng[...]

    vm = pl.BlockSpec(memory_space=pltpu.VMEM)
    return pl.pallas_call(body, in_specs=[vm] * n, out_specs=vm, out_shape=_SDS((SMALL_ROWS, D_FF), F32), name=name)(*flat)


_SMALL = ["rel_bias", "norm_mix_g", "norm_mem_g", "sinks_a", "a_log_b", "dt_bias_b", "out_norm_g_b", "norm_ffn_g",
          "ffn_conv_b", "final_norm_g", "conv_qkv_b", "ffn_conv_w"]


def _adamw_small(recv, rc_qkv, rc_ffn, ws, ms, vs, name, deps=()):
    n = len(_SMALL)

    def body(*refs):
        recv_ref, qkv_ref, ffn_ref = refs[:3]
        w_refs, m_refs, v_refs = refs[3:3 + n], refs[3 + n:3 + 2 * n], refs[3 + 2 * n:3 + 3 * n]
        outs = refs[len(refs) - 4 * n:]
        gs = _sum_sources(recv_ref)
        grads = {
            "rel_bias": jnp.concatenate(
                [gs[SP_QKV + k // 8:SP_QKV + k // 8 + 1, SP_REL_LANE + 128 * (k % 8):SP_REL_LANE + 128 * (k % 8) + A_HEADS]
                 for k in range(N_BUCKETS)], axis=0),
            "norm_mix_g": gs[SP_MIX:SP_MIX + 2, 0:D], "norm_mem_g": gs[SP_MEM:SP_MEM + 2, 0:D],
            "sinks_a": gs[SP_MISC:SP_MISC + 1, 0:A_HEADS],
            "a_log_b": gs[SP_MISC:SP_MISC + 1, 128:128 + B_V_HEADS],
            "dt_bias_b": gs[SP_MISC:SP_MISC + 1, 128 + B_V_HEADS:128 + 2 * B_V_HEADS],
            "out_norm_g_b": gs[SP_MISC:SP_MISC + 1, 256:256 + B_HD],
            "norm_ffn_g": gs[SP_FFN:SP_FFN + 2, 0:D], "ffn_conv_b": gs[SP_CB:SP_CB + 2, :],
            "final_norm_g": gs[SP_FINAL:SP_FINAL + 1, 0:D],
            "conv_qkv_b": _sum_sources(qkv_ref), "ffn_conv_w": _sum_sources(ffn_ref),
        }
        for i, nm in enumerate(_SMALL):
            g = grads[nm]
            delta, mm, vv = _adam_update(g, w_refs[i][...], m_refs[i][...], v_refs[i][...])
            outs[i][...] = g
            outs[n + i][...] = delta
            outs[2 * n + i][...] = mm
            outs[3 * n + i][...] = vv

    vm = pl.BlockSpec(memory_space=pltpu.VMEM)
    shapes = [_SDS(w.shape, F32) for w in ws]
    return pl.pallas_call(
        body, in_specs=[vm] * (3 + 3 * n) + _dep_specs(deps), out_specs=[vm] * (4 * n), out_shape=shapes * 4,
        name=name)(recv, rc_qkv, rc_ffn, *ws, *ms, *vs, *deps)


def _assemble(gathered, axis):
    g = jnp.moveaxis(gathered, 0, axis)
    shp = list(g.shape)
    return g.reshape(shp[:axis] + [shp[axis] * shp[axis + 1]] + shp[axis + 2:])


def _pad_rows(a, rows):
    return jnp.pad(a, ((0, rows - a.shape[0]), (0, 0)))


def _pad_lanes(a, lanes=128):
    return jnp.pad(a, ((0, 0), (0, lanes - a.shape[1])))


def _ff_blocks(a):
    return jnp.moveaxis(a.reshape(a.shape[0], FF_BLOCKS, GU_SHARD), 1, 0)


def _reorder_b(w):
    qkv_z = w[..., :B_QKV + B_V]
    gates = w[..., B_QKV + B_V:B_QKV + B_V + 2 * B_V_HEADS]
    xq = w[..., IN_B - X_Q:]
    pad = jnp.zeros(w.shape[:-1] + (IN_BP - IN_B,), w.dtype)
    return jnp.concatenate([qkv_z, xq, gates, pad], axis=-1)


def kernel(x, mem, rel_bias, norm_mix_g, norm_mem_g, w_mem_kv, w_out, w_in_a, sinks_a, w_in_b, conv_qkv_b, a_log_b, dt_bias_b, out_norm_g_b, norm_ffn_g, w_gate_up, ffn_conv_w, ffn_conv_b, w_down, final_norm_g, loss_target, m_rel_bias, m_norm_mix_g, m_norm_mem_g, m_w_mem_kv, m_w_out, m_w_in_a, m_sinks_a, m_w_in_b, m_conv_qkv_b, m_a_log_b, m_dt_bias_b, m_out_norm_g_b, m_norm_ffn_g, m_w_gate_up, m_ffn_conv_w, m_ffn_conv_b, m_w_down, m_final_norm_g, v_rel_bias, v_norm_mix_g, v_norm_mem_g, v_w_mem_kv, v_w_out, v_w_in_a, v_sinks_a, v_w_in_b, v_conv_qkv_b, v_a_log_b, v_dt_bias_b, v_out_norm_g_b, v_norm_ffn_g, v_w_gate_up, v_ffn_conv_w, v_ffn_conv_b, v_w_down, v_final_norm_g):
    local = dict(locals())
    order = ["rel_bias", "norm_mix_g", "norm_mem_g", "w_mem_kv", "w_out", "w_in_a", "sinks_a", "w_in_b", "conv_qkv_b",
             "a_log_b", "dt_bias_b", "out_norm_g_b", "norm_ffn_g", "w_gate_up", "ffn_conv_w", "ffn_conv_b", "w_down",
             "final_norm_g"]
    wts = {n: local[n] for n in order}
    moms = {n: local["m_" + n] for n in order}
    vars_ = {n: local["v_" + n] for n in order}
    h0 = x[0]
    memx = mem[0]
    tgt = loss_target[0]
    s = h0.shape[0]
    tm = _rows(s)

    g_mk, g_out, g_ia, g_cq, g_cw = _all_gather(
        [w_mem_kv.astype(_MXU), w_out.astype(_MXU), w_in_a.astype(_MXU), conv_qkv_b, ffn_conv_w], "gather_first")
    gu_land = ((N_DEV, D, GU_SHARD), _MXU)
    dn_land = ((N_DEV, DN_SHARD, D), _MXU)
    whole = [(0, False), (1, False)]
    def after(a, b):
        return a + (b[(0,) * b.ndim] * 0).astype(a.dtype)

    ffn0_w = _seq_exchange([after(w_gate_up[0].astype(_MXU), g_ia), after(w_down[0].astype(_MXU), g_ia)], [gu_land, dn_land],
                           whole, "gather_ffn0", 1)
    w_ia = _assemble(g_ia, 2)[0]
    conv_qkv = _pad_rows(_assemble(g_cq, 2)[0], HALO)
    ffn_cw_full = _assemble(g_cw, 2)
    ffn_cw = [_ff_blocks(_pad_rows(ffn_cw_full[i], HALO)) for i in range(2)]
    ffn_cb = [_ff_blocks(ffn_conv_b[i:i + 1]) for i in range(2)]
    bucket = jnp.asarray(_bucket_table())
    bias = _bias_build(rel_bias, bucket, "bias_build")
    sinks = _pad_lanes(sinks_a)
    par_b = _pad_lanes(jnp.concatenate([a_log_b, dt_bias_b], axis=1))

    row_x = pl.BlockSpec((tm, D), lambda i, j: (i, 0))
    gu_shape = (2, FF_BLOCKS, s, GU_SHARD)

    def in_proj(h, g, w, w_spec, n_cols, tn, name, deps=(), out_dtype=F32):
        return _norm_matmul(h, g, w, w_spec, n_cols // tn, (h.shape[0], n_cols),
                            pl.BlockSpec((_rows(h.shape[0]), tn), lambda i, j: (i, j)), name, deps=deps, out_dtype=out_dtype)

    def ffn_fwd(i, h, g_gu, g_dn, deps=()):
        gu, hn = _norm_matmul(h, norm_ffn_g[i:i + 1], g_gu, _spec_gate_up(1), N_DEV, gu_shape,
                              _spec_gu_act(0, 1, tm), f"gate_up_{i}", deps=deps, out_dtype=_ACT)
        h_new, act = _glu_down(gu, ffn_cw[i], ffn_cb[i], g_dn, h, f"glu_down_{i}")
        return h_new, gu, hn, act

    def out_proj(i, mix, h):
        return _matmul_res(mix, row_x, g_out, _spec_rowsharded(i, D // N_DEV, D), 1, h, f"out_proj_{i}")

    proj_a, hn_a = in_proj(h0, norm_mix_g[0:1], w_ia, pl.BlockSpec((D, 640), lambda i, j: (0, j)), IN_A, 640, "in_proj_a",
                           deps=ffn0_w.srcs, out_dtype=_ACT)
    memkv0, memn0 = in_proj(memx, norm_mem_g[0:1], g_mk, _spec_rowsharded(0, D // N_DEV, 2 * X_Q), 2 * X_Q, 2 * X_Q, "mem_proj_0")
    mix_a = _mix_a_fwd(proj_a, bias, sinks, memkv0, "mix_a_fwd")
    h1 = out_proj(0, mix_a, h0)
    g_gu0, g_dn0 = ffn0_w.lands
    in_b_w = _seq_exchange([after(_reorder_b(w_in_b).astype(_MXU), h1)], [((N_DEV, 1, D // N_DEV, IN_BP), _MXU)], [(0, False)],
                           "gather_in_b", 2)
    ffn1_w = _seq_exchange([after(w_gate_up[1].astype(_MXU), h1), after(w_down[1].astype(_MXU), h1)], [gu_land, dn_land], whole,
                           "gather_ffn1", 3)
    h2, gu0, hn_f0, act0 = ffn_fwd(0, h1, g_gu0, g_dn0, deps=in_b_w.srcs + ffn1_w.srcs)
    g_ib, = in_b_w.lands
    proj_b, hn_b = in_proj(h2, norm_mix_g[1:2], g_ib, _spec_rowsharded(0, D // N_DEV, 896, col_block=1), IN_BP, 896, "in_proj_b")
    memkv1, memn1 = in_proj(memx, norm_mem_g[1:2], g_mk, _spec_rowsharded(1, D // N_DEV, 2 * X_Q), 2 * X_Q, 2 * X_Q, "mem_proj_1")
    mix_b, states = _mix_b_fwd(proj_b, conv_qkv, par_b, out_norm_g_b, memkv1, "mix_b_fwd")
    h3 = out_proj(1, mix_b, h2)
    g_gu1, g_dn1 = ffn1_w.lands
    h4, gu1, hn_f1, act1 = ffn_fwd(1, h3, g_gu1, g_dn1)
    loss_row, dh, d_final_g = _loss_head(h4, final_norm_g[None, :], tgt, "loss_head")

    zeros_mem = jnp.zeros_like(memx)
    per_dest2 = [(0, True), (1, True)]

    def ffn_bwd(i, dh, h_in, gu, hn_f, act, g_gu, g_dn, deps=()):
        dgu, d_cw, d_cb = _glu_bwd(gu, ffn_cw[i], ffn_cb[i], dh, g_dn, f"glu_bwd_{i}", deps=deps)
        d_wdown = _matmul_tn(act, pl.BlockSpec((None, tm, GU_SHARD), lambda j, r: (j, r, 0)),
                             dh, pl.BlockSpec((tm, D), lambda j, r: (r, 0)), s, FF_BLOCKS, (GU_SHARD, D),
                             (N_DEV, DN_SHARD, D), pl.BlockSpec((2, DN_SHARD, D), lambda j, r: (j, 0, 0)), f"d_w_down_{i}")
        dh_new, d_g = _matmul_nt_normbwd(dgu, _spec_gu_act(0, 1, tm), g_gu, _spec_gate_up(1), N_DEV, h_in,
                                         norm_ffn_g[i:i + 1], dh, f"d_ffn_in_{i}")
        d_wgu = _matmul_tn(hn_f, pl.BlockSpec((tm, D), lambda j, r: (r, 0)), dgu, _spec_gu_act(1, 0, tm), s, N_DEV,
                           (D, GU_SHARD), (N_DEV, D, GU_SHARD), pl.BlockSpec((None, D, GU_SHARD), lambda j, r: (j, 0, 0)),
                           f"d_w_gate_up_{i}")
        sent = _seq_exchange([d_wdown, d_wgu], [((N_DEV, DN_SHARD, D), _WIRE), ((N_DEV, D, GU_SHARD), _WIRE)], per_dest2,
                             f"send_ffn{i}_grads", 4 + i)
        return dh_new, sent, d_cw, d_cb, d_g

    def out_bwd(i, dh, mix, deps):
        dmix = _matmul_nt(dh, g_out, _spec_rowsharded(i, D // N_DEV, D), 1, (s, D), row_x, f"d_mix_{i}", deps=deps, out_dtype=_ACT)
        d_wout = _matmul_tn(mix, pl.BlockSpec((tm, D), lambda j, r: (r, 0)), dh, pl.BlockSpec((tm, D), lambda j, r: (r, 0)),
                            s, 1, (D, D), (N_DEV, D // N_DEV, D), pl.BlockSpec((N_DEV, D // N_DEV, D), lambda j, r: (0, 0, 0)),
                            f"d_w_out_{i}")
        return dmix, d_wout

    def mem_bwd(i, dmemkv, memn):
        tmm = _rows(MEM_LEN)
        _, d_g = _matmul_nt_normbwd(dmemkv, pl.BlockSpec((tmm, 2 * X_Q), lambda r, j: (r, 0)), g_mk,
                                    _spec_rowsharded(i, D // N_DEV, 2 * X_Q), 1, memx, norm_mem_g[i:i + 1], zeros_mem,
                                    f"d_mem_in_{i}")
        by_row = lambda j, r: (r, 0)
        d_w = _matmul_tn(memn, pl.BlockSpec((tmm, D), by_row), dmemkv, pl.BlockSpec((tmm, 2 * X_Q), by_row), MEM_LEN, 1,
                         (D, 2 * X_Q), (N_DEV, D // N_DEV, 2 * X_Q),
                         pl.BlockSpec((N_DEV, D // N_DEV, 2 * X_Q), lambda j, r: (0, 0, 0)), f"d_w_mem_kv_{i}")
        return d_w, d_g

    out_land = ((N_DEV, D // N_DEV, D), _WIRE)
    mk_land = ((N_DEV, D // N_DEV, 2 * X_Q), _WIRE)
    dh, ffn1_g, d_cw1, d_cb1, d_gf1 = ffn_bwd(1, dh, h3, gu1, hn_f1, act1, g_gu1, g_dn1)
    dmix, d_wout1 = out_bwd(1, dh, mix_b, ffn1_g.srcs)
    dproj_b, d_convw, d_par, d_ng, dmemkv1 = _mix_b_bwd(proj_b, conv_qkv, par_b, out_norm_g_b, memkv1, states, dmix, "mix_b_bwd")
    dh, d_gm1 = _matmul_nt_normbwd(dproj_b, pl.BlockSpec((tm, 896), lambda i, j: (i, j)), g_ib,
                                   _spec_rowsharded(0, D // N_DEV, 896, col_block=1), IN_BP // 896, h2, norm_mix_g[1:2], dh, "d_in_b")
    d_wib = _matmul_tn(hn_b, pl.BlockSpec((tm, D), lambda j, r: (r, 0)), dproj_b, pl.BlockSpec((tm, 896), lambda j, r: (r, j)),
                       s, IN_BP // 896, (D, 896), (N_DEV, D // N_DEV, IN_BP),
                       pl.BlockSpec((N_DEV, D // N_DEV, 896), lambda j, r: (0, 0, j)), "d_w_in_b")
    d_wmk1, d_gmem1 = mem_bwd(1, dmemkv1, memn1)
    mix1_g = _seq_exchange([d_wout1, d_wib, d_wmk1], [out_land, ((N_DEV, D // N_DEV, IN_BP), _WIRE), mk_land],
                           [(0, True), (1, True), (2, True)], "send_mix1_grads", 6)
    dh, ffn0_g, d_cw0, d_cb0, d_gf0 = ffn_bwd(0, dh, h1, gu0, hn_f0, act0, g_gu0, g_dn0, deps=mix1_g.srcs)
    dmix, d_wout0 = out_bwd(0, dh, mix_a, ffn0_g.srcs + ffn1_g.lands[:1])
    dproj_a, dbias, dsinks, dmemkv0 = _mix_a_bwd(proj_a, bias, sinks, memkv0, dmix, "mix_a_bwd")
    dh, d_gm0 = _matmul_nt_normbwd(dproj_a, pl.BlockSpec((tm, 640), lambda i, j: (i, j)), w_ia,
                                   pl.BlockSpec((D, 640), lambda i, j: (0, j)), IN_A // 640, h0, norm_mix_g[0:1], dh, "d_in_a")
    d_wia = _matmul_tn(hn_a, pl.BlockSpec((tm, D), lambda j, r: (r, 0)), dproj_a, pl.BlockSpec((tm, IN_A), lambda j, r: (r, 0)),
                       s, 1, (D, IN_A), (N_DEV, D, IA_SHARD), pl.BlockSpec((N_DEV, D, IA_SHARD), lambda j, r: (0, 0, 0)),
                       "d_w_in_a", split=IA_SHARD)
    d_wmk0, d_gmem0 = mem_bwd(0, dmemkv0, memn0)
    d_rel = _bias_reduce(dbias, bucket, "bias_reduce")
    small = _pack_small(d_rel, (d_cb0, d_cb1), (d_cw0, d_cw1), d_convw, (d_gm0, d_gm1), (d_gmem0, d_gmem1),
                        (d_gf0, d_gf1), d_final_g, dsinks, d_par, d_ng, "pack_small")
    mix0_g = _seq_exchange([d_wout0, d_wia, d_wmk0, small],
                           [out_land, ((N_DEV, D, IA_SHARD), _WIRE), mk_land, ((N_DEV, SMALL_ROWS, D_FF), F32)],
                           [(0, True), (1, True), (2, True), (3, False)], "send_mix0_grads", 7)

    res = {}
    last = []

    def update(nm, parts, tr, restore=False):
        res[nm] = _adamw(parts, wts[nm], moms[nm], vars_[nm], tr, "adamw_" + nm, restore_b=restore, deps=last[-1:])
        last.append(res[nm][1])

    r_dn1, r_gu1 = ffn1_g.lands
    r_dn0, r_gu0 = ffn0_g.lands
    r_out1, r_ib, r_mk1 = mix1_g.lands
    update("w_gate_up", [r_gu0, r_gu1], 128)
    update("w_down", [r_dn0, r_dn1], 176)
    update("w_in_b", [r_ib], 32, True)
    r_out0, r_ia, r_mk0, r_small = mix0_g.lands
    update("w_mem_kv", [r_mk0, r_mk1], 128)
    update("w_out", [r_out0, r_out1], 128)
    update("w_in_a", [r_ia], 512)

    my = 4 * lax.axis_index("x") + 2 * lax.axis_index("y") + lax.axis_index("c")
    cq = conv_qkv_b.shape[-1]
    cf = ffn_conv_w.shape[-1]
    rc_qkv = lax.dynamic_slice_in_dim(r_small[:, SP_QKV:SP_QKV + B_CONV, :B_QKV], my * cq, cq, axis=2)[:, None]
    rc_ffn = lax.dynamic_slice_in_dim(r_small[:, SP_CW:SP_CW + 2 * FFN_CONV, :], my * cf, cf, axis=2).reshape(N_DEV, 2, FFN_CONV, cf)
    as2d = lambda a: a[None, :] if a.ndim == 1 else a
    small_out = _adamw_small(r_small, rc_qkv, rc_ffn, [as2d(wts[n]) for n in _SMALL], [as2d(moms[n]) for n in _SMALL],
                             [as2d(vars_[n]) for n in _SMALL], "adamw_small", deps=last[-1:])
    ns = len(_SMALL)
    for i, nm in enumerate(_SMALL):
        res[nm] = [small_out[k * ns + i].reshape(wts[nm].shape) for k in range(4)]

    loss = lax.psum(loss_row[0, 0], AXES)
    return (loss, dh[None], *[res[n][0] for n in order], *[res[n][1] for n in order],
            *[res[n][2] for n in order], *[res[n][3] for n in order])
```

```python
import functools
import math

import numpy as np

import jax
import jax.numpy as jnp
from jax import lax
from jax.experimental import pallas as pl
from jax.experimental.pallas import tpu as pltpu
from jax.experimental.pallas import tpu_sc as plsc

F32 = jnp.float32
_MXU = jnp.bfloat16
_ACT = jnp.bfloat16
_WIRE = jnp.bfloat16
_HI = lax.Precision.HIGH
_TM = 1024
_TM_GLU = 512
_VMEM_LIMIT = 48 * 1024 * 1024
_SDS = jax.ShapeDtypeStruct

D = 1024
EPS = 1e-6
A_HEADS, A_KV_HEADS, A_HD, BLK = 12, 2, 64, 128
N_BUCKETS, MAX_DISTANCE = 32, 128
B_QK_HEADS, B_V_HEADS, B_HD, B_CONV, CHUNK = 3, 6, 128, 4, 64
X_HEADS, X_HD, MEM_LEN = 4, 64, 256
D_FF, FFN_CONV = 2816, 3
A_Q, A_KV, X_Q = 768, 128, 256
B_QK, B_V, B_QKV = 384, 768, 1536
IN_A, IN_B = 1280, 2572
IN_BP = 2688
BP_Z, BP_XQ, BP_GATE = 1536, 2304, 2560
HALO = 8
GLU_HALO = 16

N_DEV = 8
AXES = ("x", "y", "c")
GU_SHARD = 2 * D_FF // N_DEV
FF_BLOCKS = D_FF // GU_SHARD
DN_SHARD = D_FF // N_DEV
IA_SHARD = IN_A // N_DEV

ADAM_LR, ADAM_B1, ADAM_B2, ADAM_EPS, ADAM_WD, ADAM_STEP = 0.001, 0.9, 0.999, 1e-08, 0.01, 10

SP_CB, SP_CW, SP_QKV, SP_MIX, SP_MEM, SP_FFN, SP_FINAL, SP_MISC, SMALL_ROWS = 0, 2, 8, 12, 14, 16, 18, 19, 24
SP_REL_LANE = B_QKV


def _cp(*sems):
    return pltpu.CompilerParams(dimension_semantics=sems, vmem_limit_bytes=_VMEM_LIMIT)


def _mm(a, b):
    return jnp.dot(a.astype(_MXU), b.astype(_MXU), preferred_element_type=F32)


def _mm_nt(a, b):
    return lax.dot_general(a.astype(_MXU), b.astype(_MXU), (((1,), (1,)), ((), ())), preferred_element_type=F32)


def _mm_tn(a, b):
    return lax.dot_general(a.astype(_MXU), b.astype(_MXU), (((0,), (0,)), ((), ())), preferred_element_type=F32)


def _mmf(a, b):
    return jnp.dot(a, b, preferred_element_type=F32, precision=_HI)


def _mmf_nt(a, b):
    return lax.dot_general(a, b, (((1,), (1,)), ((), ())), preferred_element_type=F32, precision=_HI)


def _mmf_tn(a, b):
    return lax.dot_general(a, b, (((0,), (0,)), ((), ())), preferred_element_type=F32, precision=_HI)


def _silu(x):
    return x * jax.nn.sigmoid(x)


def _w2d(ref):
    v = ref[...]
    return v.reshape(-1, v.shape[-1])


def _rows(m):
    return min(m, _TM)


def _spec_rowsharded(layer, rows, cols, col_block=None):
    if col_block is None:
        return pl.BlockSpec((N_DEV, None, rows, cols), lambda *_: (0, layer, 0, 0))
    return pl.BlockSpec((N_DEV, None, rows, cols), lambda *ids: (0, layer, 0, ids[col_block]))


def _spec_gate_up(axis):
    return pl.BlockSpec((None, D, GU_SHARD), lambda *ids: (ids[axis], 0, 0))


def _spec_down(axis):
    return pl.BlockSpec((2, DN_SHARD, D), lambda *ids: (ids[axis], 0, 0))


def _dep_specs(deps):
    return [pl.BlockSpec(memory_space=pl.ANY) for d in deps]


def _spec_gu_act(row_axis, axis, tm):
    return pl.BlockSpec((None, None, tm, GU_SHARD), lambda *ids: (ids[axis] // FF_BLOCKS, ids[axis] % FF_BLOCKS, ids[row_axis], 0))


def _norm_matmul(x, g, w, w_spec, n_blocks, out_shape, out_spec, name, deps=(), out_dtype=F32):
    m, k = x.shape
    tm = _rows(m)

    def body(x_ref, g_ref, w_ref, *rest):
        y_ref, hn_ref = rest[-2:]

        @pl.when(pl.program_id(1) == 0)
        def _():
            xv = x_ref[...]
            r = lax.rsqrt(jnp.mean(xv * xv, axis=-1, keepdims=True) + EPS)
            hn_ref[...] = (xv * r * g_ref[...]).astype(hn_ref.dtype)

        y_ref[...] = _mm(hn_ref[...], _w2d(w_ref)).astype(y_ref.dtype)

    return pl.pallas_call(
        body, grid=(m // tm, n_blocks),
        in_specs=[pl.BlockSpec((tm, k), lambda i, j: (i, 0)), pl.BlockSpec((1, k), lambda i, j: (0, 0)), w_spec]
        + _dep_specs(deps),
        out_specs=[out_spec, pl.BlockSpec((tm, k), lambda i, j: (i, 0))],
        out_shape=[_SDS(out_shape, out_dtype), _SDS((m, k), _ACT)],
        name=name, compiler_params=_cp("arbitrary", "arbitrary"))(x, g, w, *deps)


def _matmul_res(a, a_spec, w, w_spec, n_k, res, name):
    m, n = res.shape
    tm = _rows(m)

    def body(a_ref, w_ref, r_ref, o_ref):
        part = _mm(a_ref[...], _w2d(w_ref))

        @pl.when(pl.program_id(1) == 0)
        def _():
            o_ref[...] = r_ref[...] + part

        @pl.when(pl.program_id(1) > 0)
        def _():
            o_ref[...] += part

    return pl.pallas_call(
        body, grid=(m // tm, n_k),
        in_specs=[a_spec, w_spec, pl.BlockSpec((tm, n), lambda i, j: (i, 0))],
        out_specs=pl.BlockSpec((tm, n), lambda i, j: (i, 0)),
        out_shape=_SDS((m, n), F32), name=name, compiler_params=_cp("arbitrary", "arbitrary"))(a, w, res)


def _matmul_nt(dy, w, w_spec, n_blocks, out_shape, out_spec, name, deps=(), out_dtype=F32):
    m, n = dy.shape
    tm = _rows(m)

    def body(dy_ref, w_ref, *rest):
        o_ref = rest[-1]
        o_ref[...] = _mm_nt(dy_ref[...], _w2d(w_ref)).astype(o_ref.dtype)

    return pl.pallas_call(
        body, grid=(m // tm, n_blocks),
        in_specs=[pl.BlockSpec((tm, n), lambda i, j: (i, 0)), w_spec] + _dep_specs(deps),
        out_specs=out_spec, out_shape=_SDS(out_shape, out_dtype),
        name=name, compiler_params=_cp("arbitrary", "arbitrary"))(dy, w, *deps)


def _matmul_nt_normbwd(dy, dy_spec, w, w_spec, nj, h, g, dh_in, name):
    m, k = h.shape
    tm = _rows(m)

    def body(dy_ref, w_ref, h_ref, g_ref, dhin_ref, dh_ref, dg_ref, acc_ref):
        i, j = pl.program_id(0), pl.program_id(1)

        @pl.when(j == 0)
        def _():
            acc_ref[...] = jnp.zeros_like(acc_ref)

        acc_ref[...] += _mm_nt(dy_ref[...], _w2d(w_ref))

        @pl.when(j == nj - 1)
        def _():
            xv = h_ref[...]
            r = lax.rsqrt(jnp.mean(xv * xv, axis=-1, keepdims=True) + EPS)
            xh = xv * r
            dhn = acc_ref[...]
            part = jnp.sum(dhn * xh, axis=0, keepdims=True)

            @pl.when(i == 0)
            def _():
                dg_ref[...] = part

            @pl.when(i > 0)
            def _():
                dg_ref[...] += part

            t = dhn * g_ref[...]
            dh_ref[...] = dhin_ref[...] + r * (t - xh * jnp.mean(t * xh, axis=-1, keepdims=True))

    return pl.pallas_call(
        body, grid=(m // tm, nj),
        in_specs=[dy_spec, w_spec, pl.BlockSpec((tm, k), lambda i, j: (i, 0)), pl.BlockSpec((1, k), lambda i, j: (0, 0)),
                  pl.BlockSpec((tm, k), lambda i, j: (i, 0))],
        out_specs=[pl.BlockSpec((tm, k), lambda i, j: (i, 0)), pl.BlockSpec((1, k), lambda i, j: (0, 0))],
        out_shape=[_SDS((m, k), F32), _SDS((1, k), F32)],
        scratch_shapes=[pltpu.VMEM((tm, k), F32)],
        name=name, compiler_params=_cp("arbitrary", "arbitrary"))(dy, w, h, g, dh_in)


def _matmul_tn(x, x_spec, dy, dy_spec, m, n_blocks, acc_shape, out_shape, out_spec, name, split=None):
    tm = _rows(m)
    nm = m // tm

    def body(x_ref, dy_ref, o_ref, acc_ref):
        @pl.when(pl.program_id(1) == 0)
        def _():
            acc_ref[...] = jnp.zeros_like(acc_ref)

        acc_ref[...] += _mm_tn(x_ref[...], dy_ref[...])

        @pl.when(pl.program_id(1) == nm - 1)
        def _():
            if split is None:
                o_ref[...] = acc_ref[...].reshape(o_ref.shape).astype(o_ref.dtype)
            else:
                for d in range(N_DEV):
                    o_ref[d] = acc_ref[:, d * split:(d + 1) * split].astype(o_ref.dtype)

    return pl.pallas_call(
        body, grid=(n_blocks, nm), in_specs=[x_spec, dy_spec], out_specs=out_spec,
        out_shape=_SDS(out_shape, _WIRE), scratch_shapes=[pltpu.VMEM(acc_shape, F32)],
        name=name, compiler_params=_cp("arbitrary", "arbitrary"))(x, dy)


def _loss_head(h, g, tgt, name):
    m, k = h.shape
    tm = _rows(m)

    def body(h_ref, g_ref, t_ref, loss_ref, dh_ref, dg_ref):
        i = pl.program_id(0)
        xv = h_ref[...]
        r = lax.rsqrt(jnp.mean(xv * xv, axis=-1, keepdims=True) + EPS)
        xh = xv * r
        gv = g_ref[...]
        err = xh * gv - t_ref[...]
        lpart = jnp.zeros((1, 128), F32) + 0.5 * jnp.sum(jnp.mean(err * err, axis=-1, keepdims=True), axis=0, keepdims=True)
        dy = err * (1.0 / k)
        gpart = jnp.sum(dy * xh, axis=0, keepdims=True)

        @pl.when(i == 0)
        def _():
            loss_ref[...] = lpart
            dg_ref[...] = gpart

        @pl.when(i > 0)
        def _():
            loss_ref[...] += lpart
            dg_ref[...] += gpart

        t = dy * gv
        dh_ref[...] = r * (t - xh * jnp.mean(t * xh, axis=-1, keepdims=True))

    return pl.pallas_call(
        body, grid=(m // tm,),
        in_specs=[pl.BlockSpec((tm, k), lambda i: (i, 0)), pl.BlockSpec((1, k), lambda i: (0, 0)),
                  pl.BlockSpec((tm, k), lambda i: (i, 0))],
        out_specs=[pl.BlockSpec((1, 128), lambda i: (0, 0)), pl.BlockSpec((tm, k), lambda i: (i, 0)),
                   pl.BlockSpec((1, k), lambda i: (0, 0))],
        out_shape=[_SDS((1, 128), F32), _SDS((m, k), F32), _SDS((1, k), F32)],
        name=name, compiler_params=_cp("arbitrary"))(h, g, tgt)


def _glu_down(gu, conv_w, conv_b, w_down, res, name):
    s = gu.shape[2]
    tm = min(s, _TM_GLU)

    def body(gu_ref, prev_ref, w_ref, b_ref, wdn_ref, r_ref, o_ref, act_ref):
        i, j = pl.program_id(0), pl.program_id(1)
        prev = jnp.where(i > 0, prev_ref[...].astype(F32), 0.0)
        ext = jnp.concatenate([prev, gu_ref[0].astype(F32)], axis=0)
        gc = b_ref[...] + w_ref[FFN_CONV - 1:FFN_CONV, :] * ext
        for k in range(FFN_CONV - 1):
            gc = gc + w_ref[k:k + 1, :] * pltpu.roll(ext, FFN_CONV - 1 - k, 0)
        act = (_silu(gc[GLU_HALO:]) * gu_ref[1].astype(F32)).astype(act_ref.dtype)
        act_ref[...] = act
        part = _mm(act, _w2d(wdn_ref))

        @pl.when(j == 0)
        def _():
            o_ref[...] = r_ref[...] + part

        @pl.when(j > 0)
        def _():
            o_ref[...] += part

    return pl.pallas_call(
        body, grid=(s // tm, FF_BLOCKS),
        in_specs=[pl.BlockSpec((2, None, tm, GU_SHARD), lambda i, j: (0, j, i, 0)),
                  pl.BlockSpec((None, None, GLU_HALO, GU_SHARD),
                               lambda i, j: (0, j, jnp.maximum(i * (tm // GLU_HALO) - 1, 0), 0)),
                  pl.BlockSpec((None, HALO, GU_SHARD), lambda i, j: (j, 0, 0)),
                  pl.BlockSpec((None, 1, GU_SHARD), lambda i, j: (j, 0, 0)),
                  _spec_down(1), pl.BlockSpec((tm, D), lambda i, j: (i, 0))],
        out_specs=[pl.BlockSpec((tm, D), lambda i, j: (i, 0)), pl.BlockSpec((None, tm, GU_SHARD), lambda i, j: (j, i, 0))],
        out_shape=[_SDS((s, D), F32), _SDS((FF_BLOCKS, s, GU_SHARD), _ACT)], name=name,
        compiler_params=_cp("arbitrary", "arbitrary"))(gu, gu, conv_w, conv_b, w_down, res)


def _glu_bwd(gu, conv_w, conv_b, dh, w_down, name, deps=()):
    s = gu.shape[2]
    tm = min(s, _TM_GLU)
    nt = s // tm
    ext_rows = tm + GLU_HALO

    def body(gu_ref, prev_ref, w_ref, b_ref, dh_ref, wdn_ref, *rest):
        dgu_ref, dw_ref, db_ref, carry_ref = rest[-4:]
        t = pl.program_id(1)
        i = nt - 1 - t

        @pl.when(t == 0)
        def _():
            carry_ref[...] = jnp.zeros_like(carry_ref)
            dw_ref[...] = jnp.zeros_like(dw_ref)
            db_ref[...] = jnp.zeros_like(db_ref)

        up = gu_ref[1].astype(F32)
        prev = jnp.where(i > 0, prev_ref[...].astype(F32), 0.0)
        ext = jnp.concatenate([prev, gu_ref[0].astype(F32)], axis=0)
        shifted = [pltpu.roll(ext, FFN_CONV - 1 - j, 0) if j < FFN_CONV - 1 else ext for j in range(FFN_CONV)]
        gc = b_ref[...] + shifted[0] * w_ref[0:1, :]
        for j in range(1, FFN_CONV):
            gc = gc + shifted[j] * w_ref[j:j + 1, :]
        gc = gc[GLU_HALO:]
        sg = jax.nn.sigmoid(gc)
        da = _mm_nt(dh_ref[...], _w2d(wdn_ref))
        dup = da * (gc * sg)
        dgc = da * up * (sg * (1.0 + gc * (1.0 - sg)))
        db_ref[...] += jnp.sum(dgc, axis=0, keepdims=True)
        dgc_ext = jnp.concatenate([jnp.zeros((GLU_HALO, GU_SHARD), F32), dgc], axis=0)
        dext = dgc_ext * w_ref[FFN_CONV - 1:FFN_CONV, :]
        for j in range(FFN_CONV):
            dw_ref[j:j + 1, :] += jnp.sum(shifted[j] * dgc_ext, axis=0, keepdims=True)
            if j < FFN_CONV - 1:
                dext = dext + w_ref[j:j + 1, :] * pltpu.roll(dgc_ext, ext_rows - (FFN_CONV - 1 - j), 0)
        tail = jnp.concatenate([jnp.zeros((tm - GLU_HALO, GU_SHARD), F32), carry_ref[...]], axis=0)
        dgate = dext[GLU_HALO:] + tail
        carry_ref[...] = dext[:GLU_HALO]
        dgu_ref[0] = dgate.astype(dgu_ref.dtype)
        dgu_ref[1] = dup.astype(dgu_ref.dtype)

    return pl.pallas_call(
        body, grid=(FF_BLOCKS, nt),
        in_specs=[pl.BlockSpec((2, None, tm, GU_SHARD), lambda j, t: (0, j, nt - 1 - t, 0)),
                  pl.BlockSpec((None, None, GLU_HALO, GU_SHARD),
                               lambda j, t: (0, j, jnp.maximum((nt - 1 - t) * (tm // GLU_HALO) - 1, 0), 0)),
                  pl.BlockSpec((None, HALO, GU_SHARD), lambda j, t: (j, 0, 0)),
                  pl.BlockSpec((None, 1, GU_SHARD), lambda j, t: (j, 0, 0)),
                  pl.BlockSpec((tm, D), lambda j, t: (nt - 1 - t, 0)), _spec_down(0)] + _dep_specs(deps),
        out_specs=[pl.BlockSpec((2, None, tm, GU_SHARD), lambda j, t: (0, j, nt - 1 - t, 0)),
                   pl.BlockSpec((None, HALO, GU_SHARD), lambda j, t: (j, 0, 0)),
                   pl.BlockSpec((None, 1, GU_SHARD), lambda j, t: (j, 0, 0))],
        out_shape=[_SDS(gu.shape, _ACT), _SDS((FF_BLOCKS, HALO, GU_SHARD), F32), _SDS((FF_BLOCKS, 1, GU_SHARD), F32)],
        scratch_shapes=[pltpu.VMEM((GLU_HALO, GU_SHARD), F32)],
        name=name, compiler_params=_cp("arbitrary", "arbitrary"))(gu, gu, conv_w, conv_b, dh, w_down, *deps)


def _bucket_table():
    qi = np.arange(BLK)[:, None]
    kj = np.arange(BLK)[None, :]
    n = np.where(kj > qi, BLK + qi - kj, qi - kj)
    max_exact = N_BUCKETS // 2
    nf = np.maximum(n, 1).astype(np.float32)
    large = max_exact + (np.log(nf / max_exact) / math.log(MAX_DISTANCE / max_exact)
                         * (N_BUCKETS - max_exact)).astype(np.int32)
    large = np.minimum(large, N_BUCKETS - 1)
    return np.where(n < max_exact, n, large).astype(np.int32)


def _lane_low():
    return lax.broadcasted_iota(jnp.int32, (1, 128), 1) < A_HD


def _swa_groups(q, kd, vd, sink, bias, upper, first):
    n = A_HEADS // A_KV_HEADS
    ng = A_KV_HEADS
    low = _lane_low()
    qm = [jnp.concatenate([jnp.where(low == (h % 2 == 0), q[g][:, (h // 2) * 128:(h // 2 + 1) * 128], 0.0) for h in range(n)], axis=0)
          for g in range(ng)]
    s2 = [_mm_nt(qm[g], kd[g]) * (A_HD ** -0.5) for g in range(ng)]
    s = [jnp.where(upper[None], s2[g][:, :BLK].reshape(n, BLK, BLK), s2[g][:, BLK:].reshape(n, BLK, BLK)) + bias[g] for g in range(ng)]
    s = [jnp.where((upper & first)[None], -jnp.inf, t) for t in s]
    m = [jnp.maximum(jnp.max(s[g], axis=-1, keepdims=True), sink[g]) for g in range(ng)]
    p = [jnp.exp(s[g] - m[g]) for g in range(ng)]
    split = [jnp.concatenate([jnp.where(upper[None], t, 0.0), jnp.where(upper[None], 0.0, t)], axis=-1).reshape(n * BLK, 2 * BLK)
             for t in p]
    ones = jnp.ones((BLK, 128), F32)
    den = [_mm(p[g].reshape(n * BLK, BLK), ones) + jnp.exp(sink[g] - m[g]).reshape(n * BLK, 1) for g in range(ng)]
    o = [_mm(split[g], vd[g]) / den[g] for g in range(ng)]
    return [jnp.concatenate([jnp.where(low, t[2 * k * BLK:(2 * k + 1) * BLK], t[(2 * k + 1) * BLK:(2 * k + 2) * BLK])
                             for k in range(n // 2)], axis=1) for t in o]


def _mix_a_core(q, kd, vd, sink, bias, xq, mk, mv, upper, first):
    return _swa_groups(q, kd, vd, sink, bias, upper, first), _cross_pairs(xq, mk, mv)


def _swa_sinks(sink_ref, g):
    n = A_HEADS // A_KV_HEADS
    return jnp.concatenate([sink_ref[:, h:h + 1] for h in range(g * n, (g + 1) * n)], axis=0).reshape(n, 1, 1)


def _both_halves(t, t_rolled, g):
    low = _lane_low()
    return jnp.where(low, t, t_rolled) if g == 0 else jnp.where(low, t_rolled, t)


def _cross_pairs(q, mk, mv):
    rows = q.shape[0]
    low = _lane_low()
    qm = [jnp.concatenate([jnp.where(low, q[:, p * 128:(p + 1) * 128], 0.0), jnp.where(low, 0.0, q[:, p * 128:(p + 1) * 128])], axis=0)
          for p in range(X_HEADS // 2)]
    s = [_mm_nt(qm[p], mk[:, p * 128:(p + 1) * 128]) * (X_HD ** -0.5) for p in range(X_HEADS // 2)]
    e = [jnp.exp(t - jnp.max(t, axis=-1, keepdims=True)) for t in s]
    pr = [t / jnp.sum(t, axis=-1, keepdims=True) for t in e]
    o = [_mm(pr[p], mv[:, p * 128:(p + 1) * 128]) for p in range(X_HEADS // 2)]
    return jnp.concatenate([jnp.where(low, t[:rows], t[rows:]) for t in o], axis=1)


def _swa_upper():
    qi = lax.broadcasted_iota(jnp.int32, (BLK, BLK), 0)
    kj = lax.broadcasted_iota(jnp.int32, (BLK, BLK), 1)
    return kj > qi


def _bias_build(rel_bias, bucket, name):
    def body(rb_ref, bucket_ref, o_ref):
        b = bucket_ref[...]
        for h in range(A_HEADS):
            acc = jnp.zeros((BLK, BLK), F32)
            for k in range(N_BUCKETS):
                acc = jnp.where(b == k, rb_ref[k, h], acc)
            o_ref[h] = acc

    return pl.pallas_call(
        body, in_specs=[pl.BlockSpec(memory_space=pltpu.SMEM), pl.BlockSpec(memory_space=pltpu.VMEM)],
        out_specs=pl.BlockSpec(memory_space=pltpu.VMEM),
        out_shape=_SDS((A_HEADS, BLK, BLK), F32), name=name)(rel_bias, bucket)


def _bias_reduce(dbias, bucket, name):
    def body(db_ref, bucket_ref, o_ref):
        b = bucket_ref[...]
        row = lax.broadcasted_iota(jnp.int32, (N_BUCKETS, 128), 0)
        lane = lax.broadcasted_iota(jnp.int32, (N_BUCKETS, 128), 1)
        acc = jnp.zeros((N_BUCKETS, 128), F32)
        for h in range(A_HEADS):
            v = db_ref[h]
            for k in range(N_BUCKETS):
                sk = jnp.sum(jnp.sum(jnp.where(b == k, v, 0.0), axis=1, keepdims=True), axis=0, keepdims=True)
                acc = acc + jnp.where((row == k) & (lane == h), sk, 0.0)
        o_ref[...] = acc

    return pl.pallas_call(
        body, in_specs=[pl.BlockSpec(memory_space=pltpu.VMEM)] * 2,
        out_specs=pl.BlockSpec(memory_space=pltpu.VMEM),
        out_shape=_SDS((N_BUCKETS, 128), F32), name=name)(dbias, bucket)


def _mix_a_fwd(proj, bias, sinks, memkv, name):
    s = proj.shape[0]
    nb = s // BLK
    grp = A_HEADS // A_KV_HEADS

    def body(proj_ref, prev_ref, bias_ref, sink_ref, memkv_ref, o_ref):
        i = pl.program_id(0)
        upper = _swa_upper()
        prev = prev_ref[...].astype(F32)
        proj = proj_ref[...].astype(F32)
        kb = jnp.concatenate([prev[:, :A_KV], proj[:, A_Q:A_Q + A_KV]], axis=0)
        vb = jnp.concatenate([prev[:, A_KV:], proj[:, A_Q + A_KV:A_Q + 2 * A_KV]], axis=0)
        kb_r = pltpu.roll(kb, A_HD, 1)
        vb_r = pltpu.roll(vb, A_HD, 1)
        gw = A_Q // A_KV_HEADS
        groups = range(A_KV_HEADS)
        swa, cross = _mix_a_core([proj[:, g * gw:(g + 1) * gw] for g in groups], [_both_halves(kb, kb_r, g) for g in groups],
                                 [_both_halves(vb, vb_r, g) for g in groups], [_swa_sinks(sink_ref, g) for g in groups],
                                 [bias_ref[g * grp:(g + 1) * grp] for g in groups], proj[:, A_Q + 2 * A_KV:],
                                 memkv_ref[:, :X_Q], memkv_ref[:, X_Q:], upper, i == 0)
        o_ref[...] = jnp.concatenate(swa + [cross], axis=1).astype(o_ref.dtype)

    return pl.pallas_call(
        body, grid=(nb,),
        in_specs=[pl.BlockSpec((BLK, IN_A), lambda i: (i, 0)),
                  pl.BlockSpec((BLK, 2 * A_KV), lambda i: (jnp.maximum(i - 1, 0), A_Q // (2 * A_KV))),
                  pl.BlockSpec((A_HEADS, BLK, BLK), lambda i: (0, 0, 0)),
                  pl.BlockSpec((1, 128), lambda i: (0, 0)),
                  pl.BlockSpec((MEM_LEN, 2 * X_Q), lambda i: (0, 0))],
        out_specs=pl.BlockSpec((BLK, D), lambda i: (i, 0)),
        out_shape=_SDS((s, D), _ACT), name=name, compiler_params=_cp("arbitrary"))(proj, proj, bias, sinks, memkv)


def _mix_a_bwd(proj, bias, sinks, memkv, dmix, name):
    s = proj.shape[0]
    nb = s // BLK
    grp = A_HEADS // A_KV_HEADS

    def body(proj_ref, prev_ref, bias_ref, sink_ref, memkv_ref, dmix_ref,
             dproj_ref, dbias_ref, dsink_ref, dmemkv_ref, carry_ref):
        t = pl.program_id(0)
        i = nb - 1 - t

        @pl.when(t == 0)
        def _():
            carry_ref[...] = jnp.zeros_like(carry_ref)
            dbias_ref[...] = jnp.zeros_like(dbias_ref)
            dsink_ref[...] = jnp.zeros_like(dsink_ref)
            dmemkv_ref[...] = jnp.zeros_like(dmemkv_ref)

        upper = _swa_upper()
        lane = lax.broadcasted_iota(jnp.int32, (1, 128), 1)
        low = _lane_low()
        prev = prev_ref[...].astype(F32)
        proj = proj_ref[...].astype(F32)
        kb = jnp.concatenate([prev[:, :A_KV], proj[:, A_Q:A_Q + A_KV]], axis=0)
        vb = jnp.concatenate([prev[:, A_KV:], proj[:, A_Q + A_KV:A_Q + 2 * A_KV]], axis=0)
        kb_r = pltpu.roll(kb, A_HD, 1)
        vb_r = pltpu.roll(vb, A_HD, 1)
        gw = A_Q // A_KV_HEADS
        groups = range(A_KV_HEADS)
        _, vjp = jax.vjp(
            functools.partial(_mix_a_core, upper=upper, first=i == 0),
            [proj[:, g * gw:(g + 1) * gw] for g in groups], [_both_halves(kb, kb_r, g) for g in groups],
            [_both_halves(vb, vb_r, g) for g in groups], [_swa_sinks(sink_ref, g) for g in groups],
            [bias_ref[g * grp:(g + 1) * grp] for g in groups], proj[:, A_Q + 2 * A_KV:], memkv_ref[:, :X_Q], memkv_ref[:, X_Q:])
        dqs, dk, dv, ds, db, dxq, dmk, dmv = vjp(
            ([dmix_ref[:, g * gw:(g + 1) * gw].astype(F32) for g in groups], dmix_ref[:, A_Q:].astype(F32)))
        dkd = [t + pltpu.roll(t, A_HD, 1) for t in dk]
        dvd = [t + pltpu.roll(t, A_HD, 1) for t in dv]
        dsink = jnp.zeros((1, 128), F32)
        for g in groups:
            for h in range(grp):
                dsink = dsink + jnp.where(lane == g * grp + h, ds[g][h], 0.0)
            dbias_ref[g * grp:(g + 1) * grp] += db[g]
        dsink_ref[...] += dsink
        dkb = jnp.where(low, dkd[0], dkd[1])
        dvb = jnp.where(low, dvd[0], dvd[1])
        dmemkv_ref[...] += jnp.concatenate([dmk, dmv], axis=1)
        dkv_cur = jnp.concatenate([dkb[BLK:], dvb[BLK:]], axis=1) + carry_ref[...]
        carry_ref[...] = jnp.concatenate([dkb[:BLK], dvb[:BLK]], axis=1)
        dproj_ref[...] = jnp.concatenate(list(dqs) + [dkv_cur, dxq], axis=1).astype(dproj_ref.dtype)

    return pl.pallas_call(
        body, grid=(nb,),
        in_specs=[pl.BlockSpec((BLK, IN_A), lambda t: (nb - 1 - t, 0)),
                  pl.BlockSpec((BLK, 2 * A_KV), lambda t: (jnp.maximum(nb - 2 - t, 0), A_Q // (2 * A_KV))),
                  pl.BlockSpec((A_HEADS, BLK, BLK), lambda t: (0, 0, 0)),
                  pl.BlockSpec((1, 128), lambda t: (0, 0)),
                  pl.BlockSpec((MEM_LEN, 2 * X_Q), lambda t: (0, 0)),
                  pl.BlockSpec((BLK, D), lambda t: (nb - 1 - t, 0))],
        out_specs=[pl.BlockSpec((BLK, IN_A), lambda t: (nb - 1 - t, 0)),
                   pl.BlockSpec((A_HEADS, BLK, BLK), lambda t: (0, 0, 0)),
                   pl.BlockSpec((1, 128), lambda t: (0, 0)),
                   pl.BlockSpec((MEM_LEN, 2 * X_Q), lambda t: (0, 0))],
        out_shape=[_SDS((s, IN_A), _ACT), _SDS((A_HEADS, BLK, BLK), F32), _SDS((1, 128), F32),
                   _SDS((MEM_LEN, 2 * X_Q), F32)],
        scratch_shapes=[pltpu.VMEM((BLK, 2 * A_KV), F32)],
        name=name, compiler_params=_cp("arbitrary"))(proj, proj, bias, sinks, memkv, dmix)


def _dn_heads(yq, yk, yv, z, bl, al, a_log, dtb, ng, s0):
    c = CHUNK
    nh = B_V_HEADS
    rep = B_V_HEADS // B_QK_HEADS
    r = lax.broadcasted_iota(jnp.int32, (c, c), 0)
    cc = lax.broadcasted_iota(jnp.int32, (c, c), 1)
    q = [_silu(t) for t in yq]
    k = [_silu(t) for t in yk]
    v = [_silu(t) for t in yv]
    q = [t * lax.rsqrt(jnp.sum(t * t, axis=-1, keepdims=True) + EPS) * (B_HD ** -0.5) for t in q]
    k = [t * lax.rsqrt(jnp.sum(t * t, axis=-1, keepdims=True) + EPS) for t in k]
    beta = [jax.nn.sigmoid(t) for t in bl]
    g = [-jnp.exp(a_log[h]) * jax.nn.softplus(al[h] + dtb[h]) for h in range(nh)]
    gb = [jnp.broadcast_to(t, (c, c)) for t in g]
    gc_col = [jnp.sum(jnp.where(cc <= r, t.T, 0.0), axis=1, keepdims=True) for t in gb]
    gc_row = [jnp.sum(jnp.where(r <= cc, t, 0.0), axis=0, keepdims=True) for t in gb]
    gc_last = [jnp.sum(t, axis=0, keepdims=True) for t in g]
    decay = [jnp.exp(jnp.where(r >= cc, gc_col[h] - gc_row[h], -jnp.inf)) for h in range(nh)]
    kq = [_mmf_nt(jnp.concatenate([k[h], q[h]], axis=0), k[h]) for h in range(B_QK_HEADS)]
    kk = [t[:c] for t in kq]
    qk = [t[c:] for t in kq]
    egc = [jnp.exp(t) for t in gc_col]
    both = [_mmf(jnp.concatenate([(beta[h] * egc[h]) * k[h // rep], q[h // rep] * egc[h]], axis=0), s0[h]) for h in range(nh)]
    rhs = [beta[h] * v[h] - both[h][:c] for h in range(nh)]
    qs0 = [t[c:] for t in both]
    pw = [-(beta[h] * kk[h // rep] * jnp.where(r > cc, decay[h], 0.0)) for h in range(nh)]
    x = rhs
    for lvl in range(6):
        if lvl < 5:
            prod = [_mmf(pw[h], jnp.concatenate([x[h], pw[h]], axis=1)) for h in range(nh)]
            x = [x[h] + prod[h][:, :B_HD] for h in range(nh)]
            pw = [t[:, B_HD:] for t in prod]
        else:
            x = [x[h] + _mmf(pw[h], x[h]) for h in range(nh)]
    delta = x
    last = [_mmf(jnp.concatenate([qk[h // rep] * decay[h], (k[h // rep] * jnp.exp(gc_last[h] - gc_col[h])).T], axis=0), delta[h])
            for h in range(nh)]
    out = [qs0[h] + last[h][:c] for h in range(nh)]
    s1 = [jnp.exp(gc_last[h]) * s0[h] + last[h][c:] for h in range(nh)]
    o = [t * lax.rsqrt(jnp.mean(t * t, axis=-1, keepdims=True) + EPS) * ng for t in out]
    return [o[h] * _silu(z[h]) for h in range(nh)], s1


def _dn_conv(ext, w_ref):
    y = ext * w_ref[B_CONV - 1:B_CONV, :]
    for j in range(B_CONV - 1):
        y = y + w_ref[j:j + 1, :] * pltpu.roll(ext, B_CONV - 1 - j, 0)
    return y


def _dn_args(y, cur_ref, par_ref, ng_ref):
    nh = B_V_HEADS
    return ([y[:, h * B_HD:(h + 1) * B_HD] for h in range(B_QK_HEADS)],
            [y[:, B_QK + h * B_HD:B_QK + (h + 1) * B_HD] for h in range(B_QK_HEADS)],
            [y[:, 2 * B_QK + h * B_HD:2 * B_QK + (h + 1) * B_HD] for h in range(nh)],
            [cur_ref[:, BP_Z + h * B_HD:BP_Z + (h + 1) * B_HD] for h in range(nh)],
            [cur_ref[:, BP_GATE + h:BP_GATE + h + 1] for h in range(nh)],
            [cur_ref[:, BP_GATE + nh + h:BP_GATE + nh + h + 1] for h in range(nh)],
            [par_ref[:, h:h + 1] for h in range(nh)], [par_ref[:, nh + h:nh + h + 1] for h in range(nh)], ng_ref[...])


def _mix_b_fwd(proj, conv_w, par, ng, memkv, name):
    s = proj.shape[0]
    nc = s // CHUNK

    def body(cur_ref, prev_ref, w_ref, par_ref, ng_ref, memkv_ref, o_ref, st_ref, state_ref):
        n = pl.program_id(0)

        @pl.when(n == 0)
        def _():
            state_ref[...] = jnp.zeros_like(state_ref)

        prev = jnp.where(n > 0, prev_ref[...], 0.0)
        ext = jnp.concatenate([prev, cur_ref[:, :B_QKV]], axis=0)
        y = _dn_conv(ext, w_ref)[HALO:]
        s0 = [state_ref[hv] for hv in range(B_V_HEADS)]
        st_ref[0] = state_ref[...]
        outs, s1 = _dn_heads(*_dn_args(y, cur_ref, par_ref, ng_ref), s0)
        for hv in range(B_V_HEADS):
            state_ref[hv] = s1[hv]
        outs = outs + [_cross_pairs(cur_ref[:, BP_XQ:BP_XQ + X_Q], memkv_ref[:, :X_Q], memkv_ref[:, X_Q:])]
        o_ref[...] = jnp.concatenate(outs, axis=1).astype(o_ref.dtype)

    return pl.pallas_call(
        body, grid=(nc,),
        in_specs=[pl.BlockSpec((CHUNK, IN_BP), lambda n: (n, 0)),
                  pl.BlockSpec((HALO, B_QKV), lambda n: (jnp.maximum(n * (CHUNK // HALO) - 1, 0), 0)),
                  pl.BlockSpec((HALO, B_QKV), lambda n: (0, 0)),
                  pl.BlockSpec((1, 128), lambda n: (0, 0)), pl.BlockSpec((1, 128), lambda n: (0, 0)),
                  pl.BlockSpec((MEM_LEN, 2 * X_Q), lambda n: (0, 0))],
        out_specs=[pl.BlockSpec((CHUNK, D), lambda n: (n, 0)),
                   pl.BlockSpec((1, B_V_HEADS, B_HD, B_HD), lambda n: (n, 0, 0, 0))],
        out_shape=[_SDS((s, D), _ACT), _SDS((nc, B_V_HEADS, B_HD, B_HD), F32)],
        scratch_shapes=[pltpu.VMEM((B_V_HEADS, B_HD, B_HD), F32)],
        name=name, compiler_params=_cp("arbitrary"))(proj, proj, conv_w, par, ng, memkv)


def _mix_b_bwd(proj, conv_w, par, ng, memkv, states, dmix, name):
    s = proj.shape[0]
    nc = s // CHUNK
    ext_rows = CHUNK + HALO

    def body(cur_ref, prev_ref, w_ref, par_ref, ng_ref, memkv_ref, st_ref, dmix_ref,
             dproj_ref, dw_ref, dpar_ref, dng_ref, dmemkv_ref, dstate_ref, carry_ref):
        t = pl.program_id(0)
        n = nc - 1 - t

        @pl.when(t == 0)
        def _():
            dstate_ref[...] = jnp.zeros_like(dstate_ref)
            carry_ref[...] = jnp.zeros_like(carry_ref)
            dw_ref[...] = jnp.zeros_like(dw_ref)
            dpar_ref[...] = jnp.zeros_like(dpar_ref)
            dng_ref[...] = jnp.zeros_like(dng_ref)
            dmemkv_ref[...] = jnp.zeros_like(dmemkv_ref)

        lane = lax.broadcasted_iota(jnp.int32, (1, 128), 1)
        prev = jnp.where(n > 0, prev_ref[...], 0.0)
        ext = jnp.concatenate([prev, cur_ref[:, :B_QKV]], axis=0)
        y = _dn_conv(ext, w_ref)[HALO:]
        _, vjp = jax.vjp(_dn_heads, *_dn_args(y, cur_ref, par_ref, ng_ref), [st_ref[0, hv] for hv in range(B_V_HEADS)])
        dyq, dyk, dyv, dz, gbl, gal, ga_log, gdtb, dng, gs0 = vjp(
            ([dmix_ref[:, hv * B_HD:(hv + 1) * B_HD].astype(F32) for hv in range(B_V_HEADS)],
             [dstate_ref[hv] for hv in range(B_V_HEADS)]))
        dgate = jnp.zeros((CHUNK, 128), F32)
        dpar = jnp.zeros((1, 128), F32)
        for hv in range(B_V_HEADS):
            dstate_ref[hv] = gs0[hv]
            dgate = dgate + jnp.where(lane == hv, gbl[hv], 0.0) + jnp.where(lane == B_V_HEADS + hv, gal[hv], 0.0)
            dpar = dpar + jnp.where(lane == hv, ga_log[hv], 0.0) + jnp.where(lane == B_V_HEADS + hv, gdtb[hv], 0.0)
        dpar_ref[...] += dpar
        dng_ref[...] += dng
        _, vjp = jax.vjp(_cross_pairs, cur_ref[:, BP_XQ:BP_XQ + X_Q], memkv_ref[:, :X_Q], memkv_ref[:, X_Q:])
        dxq, dmk, dmv = vjp(dmix_ref[:, B_V:].astype(F32))
        dmemkv_ref[...] += jnp.concatenate([dmk, dmv], axis=1)
        dy = jnp.concatenate(list(dyq) + list(dyk) + list(dyv), axis=1)
        dy_ext = jnp.concatenate([jnp.zeros((HALO, B_QKV), F32), dy], axis=0)
        dext = dy_ext * w_ref[B_CONV - 1:B_CONV, :]
        dw_ref[B_CONV - 1:B_CONV, :] += jnp.sum(ext * dy_ext, axis=0, keepdims=True)
        for j in range(B_CONV - 1):
            sh = B_CONV - 1 - j
            dw_ref[j:j + 1, :] += jnp.sum(pltpu.roll(ext, sh, 0) * dy_ext, axis=0, keepdims=True)
            dext = dext + w_ref[j:j + 1, :] * pltpu.roll(dy_ext, ext_rows - sh, 0)
        tail = jnp.concatenate([jnp.zeros((CHUNK - HALO, B_QKV), F32), carry_ref[...]], axis=0)
        dqkv = dext[HALO:] + tail
        carry_ref[...] = dext[:HALO]
        dproj_ref[...] = jnp.concatenate([dqkv] + list(dz) + [dxq, dgate], axis=1).astype(dproj_ref.dtype)

    return pl.pallas_call(
        body, grid=(nc,),
        in_specs=[pl.BlockSpec((CHUNK, IN_BP), lambda t: (nc - 1 - t, 0)),
                  pl.BlockSpec((HALO, B_QKV), lambda t: (jnp.maximum((nc - 1 - t) * (CHUNK // HALO) - 1, 0), 0)),
                  pl.BlockSpec((HALO, B_QKV), lambda t: (0, 0)),
                  pl.BlockSpec((1, 128), lambda t: (0, 0)), pl.BlockSpec((1, 128), lambda t: (0, 0)),
                  pl.BlockSpec((MEM_LEN, 2 * X_Q), lambda t: (0, 0)),
                  pl.BlockSpec((1, B_V_HEADS, B_HD, B_HD), lambda t: (nc - 1 - t, 0, 0, 0)),
                  pl.BlockSpec((CHUNK, D), lambda t: (nc - 1 - t, 0))],
        out_specs=[pl.BlockSpec((CHUNK, IN_BP), lambda t: (nc - 1 - t, 0)),
                   pl.BlockSpec((HALO, B_QKV), lambda t: (0, 0)),
                   pl.BlockSpec((1, 128), lambda t: (0, 0)), pl.BlockSpec((1, 128), lambda t: (0, 0)),
                   pl.BlockSpec((MEM_LEN, 2 * X_Q), lambda t: (0, 0))],
        out_shape=[_SDS((s, IN_BP), _ACT), _SDS((HALO, B_QKV), F32), _SDS((1, 128), F32), _SDS((1, 128), F32),
                   _SDS((MEM_LEN, 2 * X_Q), F32)],
        scratch_shapes=[pltpu.VMEM((B_V_HEADS, B_HD, B_HD), F32), pltpu.VMEM((HALO, B_QKV), F32)],
        name=name, compiler_params=_cp("arbitrary"))(proj, proj, conv_w, par, ng, memkv, states, dmix)


def _place():
    return lax.axis_index("x"), lax.axis_index("y"), lax.axis_index("c")


def _all_gather(shards, name):
    n = len(shards)

    def body(*refs):
        ins, outs = refs[:n], refs[n:2 * n]
        send_sems, recv_sems, local_sems = refs[2 * n:]
        x, y, c = _place()
        me, sibling = (x, y, c), (x, y, 1 - c)
        chips = [(1 - x, y), (x, 1 - y), (1 - x, 1 - y)]

        def rows(a, px, py, pc):
            return outs[a].at[4 * px + 2 * py + pc]

        def copy(a, k, block, to, src=None):
            return pltpu.make_async_remote_copy(
                src_ref=rows(a, *block) if src is None else src, dst_ref=rows(a, *block),
                send_sem=send_sems.at[a, k], recv_sem=recv_sems.at[a, k],
                device_id=to, device_id_type=pl.DeviceIdType.MESH)

        mine = [pltpu.make_async_copy(ins[a], rows(a, *me), local_sems.at[a]) for a in range(n)]
        for cp in mine:
            cp.start()
        first = []
        for a in range(n):
            first.append(copy(a, 0, me, sibling, src=ins[a]))
            first += [copy(a, 1 + j, me, (*chip, c), src=ins[a]) for j, chip in enumerate(chips)]
        for cp in first:
            cp.start()
        passed = []
        for j, chip in enumerate(chips):
            for a in range(n):
                copy(a, 1 + j, (*chip, c), me).wait_recv()
                fwd = copy(a, 4 + j, (*chip, c), sibling)
                fwd.start()
                passed.append(fwd)
        for a in range(n):
            copy(a, 0, sibling, me).wait_recv()
            for j, chip in enumerate(chips):
                copy(a, 4 + j, (*chip, 1 - c), me).wait_recv()
        for cp in first + passed:
            cp.wait_send()
        for cp in mine:
            cp.wait()

    hbm = pl.BlockSpec(memory_space=pl.ANY)
    return pl.pallas_call(
        body, out_shape=[_SDS((N_DEV,) + s.shape, s.dtype) for s in shards],
        in_specs=[hbm] * n, out_specs=[hbm] * n,
        scratch_shapes=[pltpu.SemaphoreType.DMA((n, 7)), pltpu.SemaphoreType.DMA((n, 7)), pltpu.SemaphoreType.DMA((n,))],
        name=name)(*shards)


class _Exchange:
    def __init__(self, lands, srcs):
        self.lands, self.srcs = lands, srcs


def _seq_exchange(srcs, land_shapes, plan, name, cid):
    n, nl = len(srcs), len(land_shapes)

    def launch(*refs):
        src_refs, land_refs = refs[:n], refs[n:n + nl]
        send_sems, recv_sems, local_sems = refs[n + nl:]
        x, y, c = _place()
        my = 4 * x + 2 * y + c
        peers = [(x ^ ((k + 1) >> 2 & 1), y ^ ((k + 1) >> 1 & 1), c ^ ((k + 1) & 1)) for k in range(N_DEV - 1)]
        barrier = pltpu.get_barrier_semaphore()
        for p in peers:
            pl.semaphore_signal(barrier, inc=1, device_id=p, device_id_type=pl.DeviceIdType.MESH)
        pl.semaphore_wait(barrier, N_DEV - 1)

        def src_for(a, dest):
            return src_refs[a].at[dest] if plan[a][1] else src_refs[a]

        def slot(a, source):
            return land_refs[plan[a][0]].at[source]

        mine = [pltpu.make_async_copy(src_for(a, my), slot(a, my), local_sems.at[a]) for a in range(n)]
        for cp in mine:
            cp.start()
        sends, recvs = [], []
        for k, (px, py, pc) in enumerate(peers):
            peer = 4 * px + 2 * py + pc
            for a in range(n):
                kw = dict(send_sem=send_sems.at[a * (N_DEV - 1) + k], recv_sem=recv_sems.at[a * (N_DEV - 1) + k],
                          device_id=(px, py, pc), device_id_type=pl.DeviceIdType.MESH)
                sends.append(pltpu.make_async_remote_copy(src_ref=src_for(a, peer), dst_ref=slot(a, my), **kw))
                recvs.append(pltpu.make_async_remote_copy(src_ref=src_for(a, my), dst_ref=slot(a, peer), **kw))
        for cp in sends:
            cp.start()
        for cp in recvs:
            cp.wait_recv()
        for cp in sends:
            cp.wait_send()
        for cp in mine:
            cp.wait()

    lands = pl.kernel(
        launch, out_type=[_SDS(s, d) for s, d in land_shapes],
        mesh=plsc.ScalarSubcoreMesh(axis_name="sequencer", num_cores=1), name=name,
        scratch_types=(pltpu.SemaphoreType.DMA((n * (N_DEV - 1),)), pltpu.SemaphoreType.DMA((n * (N_DEV - 1),)),
                       pltpu.SemaphoreType.DMA((n,))),
        compiler_params=pltpu.CompilerParams(collective_id=cid))(*srcs)
    return _Exchange(list(lands), list(srcs))


def _adam_update(g, w, m, v):
    c1 = 1.0 - ADAM_B1 ** ADAM_STEP
    c2 = 1.0 - ADAM_B2 ** ADAM_STEP
    mm = ADAM_B1 * m + (1.0 - ADAM_B1) * g
    vv = ADAM_B2 * v + (1.0 - ADAM_B2) * (g * g)
    delta = -ADAM_LR * ((mm / c1) / (jnp.sqrt(vv / c2) + ADAM_EPS) + ADAM_WD * w)
    return delta, mm, vv


def _sum_sources(p_ref):
    g = p_ref[0].astype(F32)
    for s in range(1, N_DEV):
        g = g + p_ref[s].astype(F32)
    return g


def _adamw(parts, w, m, v, tr, name, restore_b=False, deps=()):
    nl, r, c = w.shape
    cp = parts[0].shape[-1]

    def body(*refs):
        p_refs = refs[:nl]
        w_ref, m_ref, v_ref = refs[nl:nl + 3]
        g_ref, d_ref, nm_ref, nv_ref = refs[-4:]
        g = _sum_sources(p_refs[0])
        for l in range(1, nl):
            g = jnp.where(pl.program_id(0) == l, _sum_sources(p_refs[l]), g)
        if restore_b:
            g = jnp.concatenate([g[:, :BP_XQ], g[:, BP_GATE:BP_GATE + 2 * B_V_HEADS], g[:, BP_XQ:BP_GATE]], axis=1)
        delta, mm, vv = _adam_update(g, w_ref[...], m_ref[...], v_ref[...])
        g_ref[...] = g
        d_ref[...] = delta
        nm_ref[...] = mm
        nv_ref[...] = vv

    spec = pl.BlockSpec((None, tr, c), lambda l, i: (l, i, 0))
    part_specs = [pl.BlockSpec((N_DEV, tr, cp), functools.partial(lambda l, i, k: (0, jnp.where(l == k, i, 0), 0), k=k))
                  for k in range(nl)]
    return pl.pallas_call(
        body, grid=(nl, r // tr),
        in_specs=part_specs + [spec, spec, spec] + _dep_specs(deps),
        out_specs=[spec] * 4, out_shape=[_SDS(w.shape, F32)] * 4,
        name=name, compiler_params=_cp("arbitrary", "arbitrary"))(*parts, w, m, v, *deps)


def _pack_small(d_rel, d_cb, d_cw, d_qkv, d_mix, d_mem, d_ffn, d_final, d_sinks, d_par, d_ng, name):
    flat = [d_rel, *d_cb, *d_cw, d_qkv, *d_mix, *d_mem, *d_ffn, d_final, d_sinks, d_par, d_ng]
    n = len(flat)

    def body(*refs):
        ins, o_ref = refs[:n], refs[n]
        rel, cb0, cb1, cw0, cw1, qkv, mx0, mx1, me0, me1, ff0, ff1, fin, snk, par, ng = ins
        o_ref[...] = jnp.zeros_like(o_ref)
        for k in range(N_BUCKETS):
            lane = SP_REL_LANE + 128 * (k % 8)
            o_ref[SP_QKV + k // 8:SP_QKV + k // 8 + 1, lane:lane + 128] = rel[k:k + 1, :]
        for l, (cb, cw) in enumerate(((cb0, cw0), (cb1, cw1))):
            o_ref[SP_CB + l:SP_CB + l + 1, :] = jnp.concatenate([cb[j] for j in range(FF_BLOCKS)], axis=1)
            full = jnp.concatenate([cw[j] for j in range(FF_BLOCKS)], axis=1)
            o_ref[SP_CW + FFN_CONV * l:SP_CW + FFN_CONV * (l + 1), :] = full[:FFN_CONV]
        o_ref[SP_QKV:SP_QKV + B_CONV, 0:B_QKV] = qkv[0:B_CONV, :]
        for base, pair in ((SP_MIX, (mx0, mx1)), (SP_MEM, (me0, me1)), (SP_FFN, (ff0, ff1))):
            for l in range(2):
                o_ref[base + l:base + l + 1, 0:D] = pair[l][...]
        o_ref[SP_FINAL:SP_FINAL + 1, 0:D] = fin[...]
        o_ref[SP_MISC:SP_MISC + 1, 0:128] = snk[...]
        o_ref[SP_MISC:SP_MISC + 1, 128:256] = par[...]
        o_ref[SP_MISC:SP_MISC + 1, 256:384] = ng[...]

    vm = pl.BlockSpec(memory_space=pltpu.VMEM)
    return pl.pallas_call(body, in_specs=[vm] * n, out_specs=vm, out_shape=_SDS((SMALL_ROWS, D_FF), F32), name=name)(*flat)


_SMALL = ["rel_bias", "norm_mix_g", "norm_mem_g", "sinks_a", "a_log_b", "dt_bias_b", "out_norm_g_b", "norm_ffn_g",
          "ffn_conv_b", "final_norm_g", "conv_qkv_b", "ffn_conv_w"]


def _adamw_small(recv, rc_qkv, rc_ffn, ws, ms, vs, name, deps=()):
    n = len(_SMALL)

    def body(*refs):
        recv_ref, qkv_ref, ffn_ref = refs[:3]
        w_refs, m_refs, v_refs = refs[3:3 + n], refs[3 + n:3 + 2 * n], refs[3 + 2 * n:3 + 3 * n]
        outs = refs[len(refs) - 4 * n:]
        gs = _sum_sources(recv_ref)
        grads = {
            "rel_bias": jnp.concatenate(
                [gs[SP_QKV + k // 8:SP_QKV + k // 8 + 1, SP_REL_LANE + 128 * (k % 8):SP_REL_LANE + 128 * (k % 8) + A_HEADS]
                 for k in range(N_BUCKETS)], axis=0),
            "norm_mix_g": gs[SP_MIX:SP_MIX + 2, 0:D], "norm_mem_g": gs[SP_MEM:SP_MEM + 2, 0:D],
            "sinks_a": gs[SP_MISC:SP_MISC + 1, 0:A_HEADS],
            "a_log_b": gs[SP_MISC:SP_MISC + 1, 128:128 + B_V_HEADS],
            "dt_bias_b": gs[SP_MISC:SP_MISC + 1, 128 + B_V_HEADS:128 + 2 * B_V_HEADS],
            "out_norm_g_b": gs[SP_MISC:SP_MISC + 1, 256:256 + B_HD],
            "norm_ffn_g": gs[SP_FFN:SP_FFN + 2, 0:D], "ffn_conv_b": gs[SP_CB:SP_CB + 2, :],
            "final_norm_g": gs[SP_FINAL:SP_FINAL + 1, 0:D],
            "conv_qkv_b": _sum_sources(qkv_ref), "ffn_conv_w": _sum_sources(ffn_ref),
        }
        for i, nm in enumerate(_SMALL):
            g = grads[nm]
            delta, mm, vv = _adam_update(g, w_refs[i][...], m_refs[i][...], v_refs[i][...])
            outs[i][...] = g
            outs[n + i][...] = delta
            outs[2 * n + i][...] = mm
            outs[3 * n + i][...] = vv

    vm = pl.BlockSpec(memory_space=pltpu.VMEM)
    shapes = [_SDS(w.shape, F32) for w in ws]
    return pl.pallas_call(
        body, in_specs=[vm] * (3 + 3 * n) + _dep_specs(deps), out_specs=[vm] * (4 * n), out_shape=shapes * 4,
        name=name)(recv, rc_qkv, rc_ffn, *ws, *ms, *vs, *deps)


def _assemble(gathered, axis):
    g = jnp.moveaxis(gathered, 0, axis)
    shp = list(g.shape)
    return g.reshape(shp[:axis] + [shp[axis] * shp[axis + 1]] + shp[axis + 2:])


def _pad_rows(a, rows):
    return jnp.pad(a, ((0, rows - a.shape[0]), (0, 0)))


def _pad_lanes(a, lanes=128):
    return jnp.pad(a, ((0, 0), (0, lanes - a.shape[1])))


def _ff_blocks(a):
    return jnp.moveaxis(a.reshape(a.shape[0], FF_BLOCKS, GU_SHARD), 1, 0)


def _reorder_b(w):
    qkv_z = w[..., :B_QKV + B_V]
    gates = w[..., B_QKV + B_V:B_QKV + B_V + 2 * B_V_HEADS]
    xq = w[..., IN_B - X_Q:]
    pad = jnp.zeros(w.shape[:-1] + (IN_BP - IN_B,), w.dtype)
    return jnp.concatenate([qkv_z, xq, gates, pad], axis=-1)


def kernel(x, mem, rel_bias, norm_mix_g, norm_mem_g, w_mem_kv, w_out, w_in_a, sinks_a, w_in_b, conv_qkv_b, a_log_b, dt_bias_b, out_norm_g_b, norm_ffn_g, w_gate_up, ffn_conv_w, ffn_conv_b, w_down, final_norm_g, loss_target, m_rel_bias, m_norm_mix_g, m_norm_mem_g, m_w_mem_kv, m_w_out, m_w_in_a, m_sinks_a, m_w_in_b, m_conv_qkv_b, m_a_log_b, m_dt_bias_b, m_out_norm_g_b, m_norm_ffn_g, m_w_gate_up, m_ffn_conv_w, m_ffn_conv_b, m_w_down, m_final_norm_g, v_rel_bias, v_norm_mix_g, v_norm_mem_g, v_w_mem_kv, v_w_out, v_w_in_a, v_sinks_a, v_w_in_b, v_conv_qkv_b, v_a_log_b, v_dt_bias_b, v_out_norm_g_b, v_norm_ffn_g, v_w_gate_up, v_ffn_conv_w, v_ffn_conv_b, v_w_down, v_final_norm_g):
    local = dict(locals())
    order = ["rel_bias", "norm_mix_g", "norm_mem_g", "w_mem_kv", "w_out", "w_in_a", "sinks_a", "w_in_b", "conv_qkv_b",
             "a_log_b", "dt_bias_b", "out_norm_g_b", "norm_ffn_g", "w_gate_up", "ffn_conv_w", "ffn_conv_b", "w_down",
             "final_norm_g"]
    wts = {n: local[n] for n in order}
    moms = {n: local["m_" + n] for n in order}
    vars_ = {n: local["v_" + n] for n in order}
    h0 = x[0]
    memx = mem[0]
    tgt = loss_target[0]
    s = h0.shape[0]
    tm = _rows(s)

    g_mk, g_out, g_ia, g_cq, g_cw = _all_gather(
        [w_mem_kv.astype(_MXU), w_out.astype(_MXU), w_in_a.astype(_MXU), conv_qkv_b, ffn_conv_w], "gather_first")
    gu_land = ((N_DEV, D, GU_SHARD), _MXU)
    dn_land = ((N_DEV, DN_SHARD, D), _MXU)
    whole = [(0, False), (1, False)]
    def after(a, b):
        return a + (b[(0,) * b.ndim] * 0).astype(a.dtype)

    ffn0_w = _seq_exchange([after(w_gate_up[0].astype(_MXU), g_ia), after(w_down[0].astype(_MXU), g_ia)], [gu_land, dn_land],
                           whole, "gather_ffn0", 1)
    w_ia = _assemble(g_ia, 2)[0]
    conv_qkv = _pad_rows(_assemble(g_cq, 2)[0], HALO)
    ffn_cw_full = _assemble(g_cw, 2)
    ffn_cw = [_ff_blocks(_pad_rows(ffn_cw_full[i], HALO)) for i in range(2)]
    ffn_cb = [_ff_blocks(ffn_conv_b[i:i + 1]) for i in range(2)]
    bucket = jnp.asarray(_bucket_table())
    bias = _bias_build(rel_bias, bucket, "bias_build")
    sinks = _pad_lanes(sinks_a)
    par_b = _pad_lanes(jnp.concatenate([a_log_b, dt_bias_b], axis=1))

    row_x = pl.BlockSpec((tm, D), lambda i, j: (i, 0))
    gu_shape = (2, FF_BLOCKS, s, GU_SHARD)

    def in_proj(h, g, w, w_spec, n_cols, tn, name, deps=(), out_dtype=F32):
        return _norm_matmul(h, g, w, w_spec, n_cols // tn, (h.shape[0], n_cols),
                            pl.BlockSpec((_rows(h.shape[0]), tn), lambda i, j: (i, j)), name, deps=deps, out_dtype=out_dtype)

    def ffn_fwd(i, h, g_gu, g_dn, deps=()):
        gu, hn = _norm_matmul(h, norm_ffn_g[i:i + 1], g_gu, _spec_gate_up(1), N_DEV, gu_shape,
                              _spec_gu_act(0, 1, tm), f"gate_up_{i}", deps=deps, out_dtype=_ACT)
        h_new, act = _glu_down(gu, ffn_cw[i], ffn_cb[i], g_dn, h, f"glu_down_{i}")
        return h_new, gu, hn, act

    def out_proj(i, mix, h):
        return _matmul_res(mix, row_x, g_out, _spec_rowsharded(i, D // N_DEV, D), 1, h, f"out_proj_{i}")

    proj_a, hn_a = in_proj(h0, norm_mix_g[0:1], w_ia, pl.BlockSpec((D, 640), lambda i, j: (0, j)), IN_A, 640, "in_proj_a",
                           deps=ffn0_w.srcs, out_dtype=_ACT)
    memkv0, memn0 = in_proj(memx, norm_mem_g[0:1], g_mk, _spec_rowsharded(0, D // N_DEV, 2 * X_Q), 2 * X_Q, 2 * X_Q, "mem_proj_0")
    mix_a = _mix_a_fwd(proj_a, bias, sinks, memkv0, "mix_a_fwd")
    h1 = out_proj(0, mix_a, h0)
    g_gu0, g_dn0 = ffn0_w.lands
    in_b_w = _seq_exchange([after(_reorder_b(w_in_b).astype(_MXU), h1)], [((N_DEV, 1, D // N_DEV, IN_BP), _MXU)], [(0, False)],
                           "gather_in_b", 2)
    ffn1_w = _seq_exchange([after(w_gate_up[1].astype(_MXU), h1), after(w_down[1].astype(_MXU), h1)], [gu_land, dn_land], whole,
                           "gather_ffn1", 3)
    h2, gu0, hn_f0, act0 = ffn_fwd(0, h1, g_gu0, g_dn0, deps=in_b_w.srcs + ffn1_w.srcs)
    g_ib, = in_b_w.lands
    proj_b, hn_b = in_proj(h2, norm_mix_g[1:2], g_ib, _spec_rowsharded(0, D // N_DEV, 896, col_block=1), IN_BP, 896, "in_proj_b")
    memkv1, memn1 = in_proj(memx, norm_mem_g[1:2], g_mk, _spec_rowsharded(1, D // N_DEV, 2 * X_Q), 2 * X_Q, 2 * X_Q, "mem_proj_1")
    mix_b, states = _mix_b_fwd(proj_b, conv_qkv, par_b, out_norm_g_b, memkv1, "mix_b_fwd")
    h3 = out_proj(1, mix_b, h2)
    g_gu1, g_dn1 = ffn1_w.lands
    h4, gu1, hn_f1, act1 = ffn_fwd(1, h3, g_gu1, g_dn1)
    loss_row, dh, d_final_g = _loss_head(h4, final_norm_g[None, :], tgt, "loss_head")

    zeros_mem = jnp.zeros_like(memx)
    per_dest2 = [(0, True), (1, True)]

    def ffn_bwd(i, dh, h_in, gu, hn_f, act, g_gu, g_dn, deps=()):
        dgu, d_cw, d_cb = _glu_bwd(gu, ffn_cw[i], ffn_cb[i], dh, g_dn, f"glu_bwd_{i}", deps=deps)
        d_wdown = _matmul_tn(act, pl.BlockSpec((None, tm, GU_SHARD), lambda j, r: (j, r, 0)),
                             dh, pl.BlockSpec((tm, D), lambda j, r: (r, 0)), s, FF_BLOCKS, (GU_SHARD, D),
                             (N_DEV, DN_SHARD, D), pl.BlockSpec((2, DN_SHARD, D), lambda j, r: (j, 0, 0)), f"d_w_down_{i}")
        dh_new, d_g = _matmul_nt_normbwd(dgu, _spec_gu_act(0, 1, tm), g_gu, _spec_gate_up(1), N_DEV, h_in,
                                         norm_ffn_g[i:i + 1], dh, f"d_ffn_in_{i}")
        d_wgu = _matmul_tn(hn_f, pl.BlockSpec((tm, D), lambda j, r: (r, 0)), dgu, _spec_gu_act(1, 0, tm), s, N_DEV,
                           (D, GU_SHARD), (N_DEV, D, GU_SHARD), pl.BlockSpec((None, D, GU_SHARD), lambda j, r: (j, 0, 0)),
                           f"d_w_gate_up_{i}")
        sent = _seq_exchange([d_wdown, d_wgu], [((N_DEV, DN_SHARD, D), _WIRE), ((N_DEV, D, GU_SHARD), _WIRE)], per_dest2,
                             f"send_ffn{i}_grads", 4 + i)
        return dh_new, sent, d_cw, d_cb, d_g

    def out_bwd(i, dh, mix, deps):
        dmix = _matmul_nt(dh, g_out, _spec_rowsharded(i, D // N_DEV, D), 1, (s, D), row_x, f"d_mix_{i}", deps=deps, out_dtype=_ACT)
        d_wout = _matmul_tn(mix, pl.BlockSpec((tm, D), lambda j, r: (r, 0)), dh, pl.BlockSpec((tm, D), lambda j, r: (r, 0)),
                            s, 1, (D, D), (N_DEV, D // N_DEV, D), pl.BlockSpec((N_DEV, D // N_DEV, D), lambda j, r: (0, 0, 0)),
                            f"d_w_out_{i}")
        return dmix, d_wout

    def mem_bwd(i, dmemkv, memn):
        tmm = _rows(MEM_LEN)
        _, d_g = _matmul_nt_normbwd(dmemkv, pl.BlockSpec((tmm, 2 * X_Q), lambda r, j: (r, 0)), g_mk,
                                    _spec_rowsharded(i, D // N_DEV, 2 * X_Q), 1, memx, norm_mem_g[i:i + 1], zeros_mem,
                                    f"d_mem_in_{i}")
        by_row = lambda j, r: (r, 0)
        d_w = _matmul_tn(memn, pl.BlockSpec((tmm, D), by_row), dmemkv, pl.BlockSpec((tmm, 2 * X_Q), by_row), MEM_LEN, 1,
                         (D, 2 * X_Q), (N_DEV, D // N_DEV, 2 * X_Q),
                         pl.BlockSpec((N_DEV, D // N_DEV, 2 * X_Q), lambda j, r: (0, 0, 0)), f"d_w_mem_kv_{i}")
        return d_w, d_g

    out_land = ((N_DEV, D // N_DEV, D), _WIRE)
    mk_land = ((N_DEV, D // N_DEV, 2 * X_Q), _WIRE)
    dh, ffn1_g, d_cw1, d_cb1, d_gf1 = ffn_bwd(1, dh, h3, gu1, hn_f1, act1, g_gu1, g_dn1)
    dmix, d_wout1 = out_bwd(1, dh, mix_b, ffn1_g.srcs)
    dproj_b, d_convw, d_par, d_ng, dmemkv1 = _mix_b_bwd(proj_b, conv_qkv, par_b, out_norm_g_b, memkv1, states, dmix, "mix_b_bwd")
    dh, d_gm1 = _matmul_nt_normbwd(dproj_b, pl.BlockSpec((tm, 896), lambda i, j: (i, j)), g_ib,
                                   _spec_rowsharded(0, D // N_DEV, 896, col_block=1), IN_BP // 896, h2, norm_mix_g[1:2], dh, "d_in_b")
    d_wib = _matmul_tn(hn_b, pl.BlockSpec((tm, D), lambda j, r: (r, 0)), dproj_b, pl.BlockSpec((tm, 896), lambda j, r: (r, j)),
                       s, IN_BP // 896, (D, 896), (N_DEV, D // N_DEV, IN_BP),
                       pl.BlockSpec((N_DEV, D // N_DEV, 896), lambda j, r: (0, 0, j)), "d_w_in_b")
    d_wmk1, d_gmem1 = mem_bwd(1, dmemkv1, memn1)
    mix1_g = _seq_exchange([d_wout1, d_wib, d_wmk1], [out_land, ((N_DEV, D // N_DEV, IN_BP), _WIRE), mk_land],
                           [(0, True), (1, True), (2, True)], "send_mix1_grads", 6)
    dh, ffn0_g, d_cw0, d_cb0, d_gf0 = ffn_bwd(0, dh, h1, gu0, hn_f0, act0, g_gu0, g_dn0, deps=mix1_g.srcs)
    dmix, d_wout0 = out_bwd(0, dh, mix_a, ffn0_g.srcs + ffn1_g.lands[:1])
    dproj_a, dbias, dsinks, dmemkv0 = _mix_a_bwd(proj_a, bias, sinks, memkv0, dmix, "mix_a_bwd")
    dh, d_gm0 = _matmul_nt_normbwd(dproj_a, pl.BlockSpec((tm, 640), lambda i, j: (i, j)), w_ia,
                                   pl.BlockSpec((D, 640), lambda i, j: (0, j)), IN_A // 640, h0, norm_mix_g[0:1], dh, "d_in_a")
    d_wia = _matmul_tn(hn_a, pl.BlockSpec((tm, D), lambda j, r: (r, 0)), dproj_a, pl.BlockSpec((tm, IN_A), lambda j, r: (r, 0)),
                       s, 1, (D, IN_A), (N_DEV, D, IA_SHARD), pl.BlockSpec((N_DEV, D, IA_SHARD), lambda j, r: (0, 0, 0)),
                       "d_w_in_a", split=IA_SHARD)
    d_wmk0, d_gmem0 = mem_bwd(0, dmemkv0, memn0)
    d_rel = _bias_reduce(dbias, bucket, "bias_reduce")
    small = _pack_small(d_rel, (d_cb0, d_cb1), (d_cw0, d_cw1), d_convw, (d_gm0, d_gm1), (d_gmem0, d_gmem1),
                        (d_gf0, d_gf1), d_final_g, dsinks, d_par, d_ng, "pack_small")
    mix0_g = _seq_exchange([d_wout0, d_wia, d_wmk0, small],
                           [out_land, ((N_DEV, D, IA_SHARD), _WIRE), mk_land, ((N_DEV, SMALL_ROWS, D_FF), F32)],
                           [(0, True), (1, True), (2, True), (3, False)], "send_mix0_grads", 7)

    res = {}
    last = []

    def update(nm, parts, tr, restore=False):
        res[nm] = _adamw(parts, wts[nm], moms[nm], vars_[nm], tr, "adamw_" + nm, restore_b=restore, deps=last[-1:])
        last.append(res[nm][1])

    r_dn1, r_gu1 = ffn1_g.lands
    r_dn0, r_gu0 = ffn0_g.lands
    r_out1, r_ib, r_mk1 = mix1_g.lands
    update("w_gate_up", [r_gu0, r_gu1], 128)
    update("w_down", [r_dn0, r_dn1], 176)
    update("w_in_b", [r_ib], 32, True)
    r_out0, r_ia, r_mk0, r_small = mix0_g.lands
    update("w_mem_kv", [r_mk0, r_mk1], 128)
    update("w_out", [r_out0, r_out1], 128)
    update("w_in_a", [r_ia], 512)

    my = 4 * lax.axis_index("x") + 2 * lax.axis_index("y") + lax.axis_index("c")
    cq = conv_qkv_b.shape[-1]
    cf = ffn_conv_w.shape[-1]
    rc_qkv = lax.dynamic_slice_in_dim(r_small[:, SP_QKV:SP_QKV + B_CONV, :B_QKV], my * cq, cq, axis=2)[:, None]
    rc_ffn = lax.dynamic_slice_in_dim(r_small[:, SP_CW:SP_CW + 2 * FFN_CONV, :], my * cf, cf, axis=2).reshape(N_DEV, 2, FFN_CONV, cf)
    as2d = lambda a: a[None, :] if a.ndim == 1 else a
    small_out = _adamw_small(r_small, rc_qkv, rc_ffn, [as2d(wts[n]) for n in _SMALL], [as2d(moms[n]) for n in _SMALL],
                             [as2d(vars_[n]) for n in _SMALL], "adamw_small", deps=last[-1:])
    ns = len(_SMALL)
    for i, nm in enumerate(_SMALL):
        res[nm] = [small_out[k * ns + i].reshape(wts[nm].shape) for k in range(4)]

    loss = lax.psum(loss_row[0, 0], AXES)
    return (loss, dh[None], *[res[n][0] for n in order], *[res[n][1] for n in order],
            *[res[n][2] for n in order], *[res[n][3] for n in order])
```

```python
import functools
import math

import numpy as np

import jax
import jax.numpy as jnp
from jax import lax
from jax.experimental import pallas as pl
from jax.experimental.pallas import tpu as pltpu
from jax.experimental.pallas import tpu_sc as plsc

F32 = jnp.float32
_MXU = jnp.bfloat16
_ACT = jnp.bfloat16
_WIRE = jnp.bfloat16
_HI = lax.Precision.HIGH
_TM = 1024
_TM_GLU = 512
_VMEM_LIMIT = 48 * 1024 * 1024
_SDS = jax.ShapeDtypeStruct

D = 1024
EPS = 1e-6
A_HEADS, A_KV_HEADS, A_HD, BLK = 12, 2, 64, 128
N_BUCKETS, MAX_DISTANCE = 32, 128
B_QK_HEADS, B_V_HEADS, B_HD, B_CONV, CHUNK = 3, 6, 128, 4, 64
X_HEADS, X_HD, MEM_LEN = 4, 64, 256
D_FF, FFN_CONV = 2816, 3
A_Q, A_KV, X_Q = 768, 128, 256
B_QK, B_V, B_QKV = 384, 768, 1536
IN_A, IN_B = 1280, 2572
IN_BP = 2688
BP_Z, BP_XQ, BP_GATE = 1536, 2304, 2560
HALO = 8
GLU_HALO = 16

N_DEV = 8
AXES = ("x", "y", "c")
GU_SHARD = 2 * D_FF // N_DEV
FF_BLOCKS = D_FF // GU_SHARD
DN_SHARD = D_FF // N_DEV
IA_SHARD = IN_A // N_DEV

ADAM_LR, ADAM_B1, ADAM_B2, ADAM_EPS, ADAM_WD, ADAM_STEP = 0.001, 0.9, 0.999, 1e-08, 0.01, 10

SP_CB, SP_CW, SP_QKV, SP_MIX, SP_MEM, SP_FFN, SP_FINAL, SP_MISC, SMALL_ROWS = 0, 2, 8, 12, 14, 16, 18, 19, 24
SP_REL_LANE = B_QKV


def _cp(*sems):
    return pltpu.CompilerParams(dimension_semantics=sems, vmem_limit_bytes=_VMEM_LIMIT)


def _mm(a, b):
    return jnp.dot(a.astype(_MXU), b.astype(_MXU), preferred_element_type=F32)


def _mm_nt(a, b):
    return lax.dot_general(a.astype(_MXU), b.astype(_MXU), (((1,), (1,)), ((), ())), preferred_element_type=F32)


def _mm_tn(a, b):
    return lax.dot_general(a.astype(_MXU), b.astype(_MXU), (((0,), (0,)), ((), ())), preferred_element_type=F32)


def _mmf(a, b):
    return jnp.dot(a, b, preferred_element_type=F32, precision=_HI)


def _mmf_nt(a, b):
    return lax.dot_general(a, b, (((1,), (1,)), ((), ())), preferred_element_type=F32, precision=_HI)


def _mmf_tn(a, b):
    return lax.dot_general(a, b, (((0,), (0,)), ((), ())), preferred_element_type=F32, precision=_HI)


def _silu(x):
    return x * jax.nn.sigmoid(x)


def _w2d(ref):
    v = ref[...]
    return v.reshape(-1, v.shape[-1])


def _rows(m):
    return min(m, _TM)


def _spec_rowsharded(layer, rows, cols, col_block=None):
    if col_block is None:
        return pl.BlockSpec((N_DEV, None, rows, cols), lambda *_: (0, layer, 0, 0))
    return pl.BlockSpec((N_DEV, None, rows, cols), lambda *ids: (0, layer, 0, ids[col_block]))


def _spec_gate_up(axis):
    return pl.BlockSpec((None, GU_SHARD, D), lambda *ids: (ids[axis], 0, 0))


def _spec_down(axis):
    return pl.BlockSpec((2, DN_SHARD, D), lambda *ids: (ids[axis], 0, 0))


def _dep_specs(deps):
    return [pl.BlockSpec(memory_space=pl.ANY) for d in deps]


def _spec_gu_act(row_axis, axis, tm):
    return pl.BlockSpec((None, None, tm, GU_SHARD), lambda *ids: (ids[axis] // FF_BLOCKS, ids[axis] % FF_BLOCKS, ids[row_axis], 0))


def _norm_matmul(x, g, w, w_spec, n_blocks, out_shape, out_spec, name, deps=(), out_dtype=F32, w_t=False):
    m, k = x.shape
    tm = _rows(m)

    def body(x_ref, g_ref, w_ref, *rest):
        y_ref, hn_ref = rest[-2:]

        @pl.when(pl.program_id(1) == 0)
        def _():
            xv = x_ref[...]
            r = lax.rsqrt(jnp.mean(xv * xv, axis=-1, keepdims=True) + EPS)
            hn_ref[...] = (xv * r * g_ref[...]).astype(hn_ref.dtype)

        y_ref[...] = (_mm_nt if w_t else _mm)(hn_ref[...], _w2d(w_ref)).astype(y_ref.dtype)

    return pl.pallas_call(
        body, grid=(m // tm, n_blocks),
        in_specs=[pl.BlockSpec((tm, k), lambda i, j: (i, 0)), pl.BlockSpec((1, k), lambda i, j: (0, 0)), w_spec]
        + _dep_specs(deps),
        out_specs=[out_spec, pl.BlockSpec((tm, k), lambda i, j: (i, 0))],
        out_shape=[_SDS(out_shape, out_dtype), _SDS((m, k), _ACT)],
        name=name, compiler_params=_cp("arbitrary", "arbitrary"))(x, g, w, *deps)


def _matmul_res(a, a_spec, w, w_spec, n_k, res, name):
    m, n = res.shape
    tm = _rows(m)

    def body(a_ref, w_ref, r_ref, o_ref):
        part = _mm(a_ref[...], _w2d(w_ref))

        @pl.when(pl.program_id(1) == 0)
        def _():
            o_ref[...] = r_ref[...] + part

        @pl.when(pl.program_id(1) > 0)
        def _():
            o_ref[...] += part

    return pl.pallas_call(
        body, grid=(m // tm, n_k),
        in_specs=[a_spec, w_spec, pl.BlockSpec((tm, n), lambda i, j: (i, 0))],
        out_specs=pl.BlockSpec((tm, n), lambda i, j: (i, 0)),
        out_shape=_SDS((m, n), F32), name=name, compiler_params=_cp("arbitrary", "arbitrary"))(a, w, res)


def _matmul_nt(dy, w, w_spec, n_blocks, out_shape, out_spec, name, deps=(), out_dtype=F32):
    m, n = dy.shape
    tm = _rows(m)

    def body(dy_ref, w_ref, *rest):
        o_ref = rest[-1]
        o_ref[...] = _mm_nt(dy_ref[...], _w2d(w_ref)).astype(o_ref.dtype)

    return pl.pallas_call(
        body, grid=(m // tm, n_blocks),
        in_specs=[pl.BlockSpec((tm, n), lambda i, j: (i, 0)), w_spec] + _dep_specs(deps),
        out_specs=out_spec, out_shape=_SDS(out_shape, out_dtype),
        name=name, compiler_params=_cp("arbitrary", "arbitrary"))(dy, w, *deps)


def _matmul_nt_normbwd(dy, dy_spec, w, w_spec, nj, h, g, dh_in, name, w_t=False):
    m, k = h.shape
    tm = _rows(m)

    def body(dy_ref, w_ref, h_ref, g_ref, dhin_ref, dh_ref, dg_ref, acc_ref):
        i, j = pl.program_id(0), pl.program_id(1)

        @pl.when(j == 0)
        def _():
            acc_ref[...] = jnp.zeros_like(acc_ref)

        acc_ref[...] += (_mm if w_t else _mm_nt)(dy_ref[...], _w2d(w_ref))

        @pl.when(j == nj - 1)
        def _():
            xv = h_ref[...]
            r = lax.rsqrt(jnp.mean(xv * xv, axis=-1, keepdims=True) + EPS)
            xh = xv * r
            dhn = acc_ref[...]
            part = jnp.sum(dhn * xh, axis=0, keepdims=True)

            @pl.when(i == 0)
            def _():
                dg_ref[...] = part

            @pl.when(i > 0)
            def _():
                dg_ref[...] += part

            t = dhn * g_ref[...]
            dh_ref[...] = dhin_ref[...] + r * (t - xh * jnp.mean(t * xh, axis=-1, keepdims=True))

    return pl.pallas_call(
        body, grid=(m // tm, nj),
        in_specs=[dy_spec, w_spec, pl.BlockSpec((tm, k), lambda i, j: (i, 0)), pl.BlockSpec((1, k), lambda i, j: (0, 0)),
                  pl.BlockSpec((tm, k), lambda i, j: (i, 0))],
        out_specs=[pl.BlockSpec((tm, k), lambda i, j: (i, 0)), pl.BlockSpec((1, k), lambda i, j: (0, 0))],
        out_shape=[_SDS((m, k), F32), _SDS((1, k), F32)],
        scratch_shapes=[pltpu.VMEM((tm, k), F32)],
        name=name, compiler_params=_cp("arbitrary", "arbitrary"))(dy, w, h, g, dh_in)


def _matmul_tn(x, x_spec, dy, dy_spec, m, n_blocks, acc_shape, out_shape, out_spec, name):
    tm = _rows(m)
    nm = m // tm

    def body(x_ref, dy_ref, o_ref, acc_ref):
        @pl.when(pl.program_id(1) == 0)
        def _():
            acc_ref[...] = jnp.zeros_like(acc_ref)

        acc_ref[...] += _mm_tn(x_ref[...], dy_ref[...])

        @pl.when(pl.program_id(1) == nm - 1)
        def _():
            o_ref[...] = acc_ref[...].reshape(o_ref.shape).astype(o_ref.dtype)

    return pl.pallas_call(
        body, grid=(n_blocks, nm), in_specs=[x_spec, dy_spec], out_specs=out_spec,
        out_shape=_SDS(out_shape, _WIRE), scratch_shapes=[pltpu.VMEM(acc_shape, F32)],
        name=name, compiler_params=_cp("arbitrary", "arbitrary"))(x, dy)


def _loss_head(h, g, tgt, name):
    m, k = h.shape
    tm = _rows(m)

    def body(h_ref, g_ref, t_ref, loss_ref, dh_ref, dg_ref):
        i = pl.program_id(0)
        xv = h_ref[...]
        r = lax.rsqrt(jnp.mean(xv * xv, axis=-1, keepdims=True) + EPS)
        xh = xv * r
        gv = g_ref[...]
        err = xh * gv - t_ref[...]
        lpart = jnp.zeros((1, 128), F32) + 0.5 * jnp.sum(jnp.mean(err * err, axis=-1, keepdims=True), axis=0, keepdims=True)
        dy = err * (1.0 / k)
        gpart = jnp.sum(dy * xh, axis=0, keepdims=True)

        @pl.when(i == 0)
        def _():
            loss_ref[...] = lpart
            dg_ref[...] = gpart

        @pl.when(i > 0)
        def _():
            loss_ref[...] += lpart
            dg_ref[...] += gpart

        t = dy * gv
        dh_ref[...] = r * (t - xh * jnp.mean(t * xh, axis=-1, keepdims=True))

    return pl.pallas_call(
        body, grid=(m // tm,),
        in_specs=[pl.BlockSpec((tm, k), lambda i: (i, 0)), pl.BlockSpec((1, k), lambda i: (0, 0)),
                  pl.BlockSpec((tm, k), lambda i: (i, 0))],
        out_specs=[pl.BlockSpec((1, 128), lambda i: (0, 0)), pl.BlockSpec((tm, k), lambda i: (i, 0)),
                   pl.BlockSpec((1, k), lambda i: (0, 0))],
        out_shape=[_SDS((1, 128), F32), _SDS((m, k), F32), _SDS((1, k), F32)],
        name=name, compiler_params=_cp("arbitrary"))(h, g, tgt)


def _glu_down(gu, conv_w, conv_b, w_down, res, name):
    s = gu.shape[2]
    tm = min(s, _TM_GLU)

    def body(gu_ref, prev_ref, w_ref, b_ref, wdn_ref, r_ref, o_ref, act_ref):
        i, j = pl.program_id(0), pl.program_id(1)
        prev = jnp.where(i > 0, prev_ref[...].astype(F32), 0.0)
        ext = jnp.concatenate([prev, gu_ref[0].astype(F32)], axis=0)
        gc = b_ref[...] + w_ref[FFN_CONV - 1:FFN_CONV, :] * ext
        for k in range(FFN_CONV - 1):
            gc = gc + w_ref[k:k + 1, :] * pltpu.roll(ext, FFN_CONV - 1 - k, 0)
        act = (_silu(gc[GLU_HALO:]) * gu_ref[1].astype(F32)).astype(act_ref.dtype)
        act_ref[...] = act
        part = _mm(act, _w2d(wdn_ref))

        @pl.when(j == 0)
        def _():
            o_ref[...] = r_ref[...] + part

        @pl.when(j > 0)
        def _():
            o_ref[...] += part

    return pl.pallas_call(
        body, grid=(s // tm, FF_BLOCKS),
        in_specs=[pl.BlockSpec((2, None, tm, GU_SHARD), lambda i, j: (0, j, i, 0)),
                  pl.BlockSpec((None, None, GLU_HALO, GU_SHARD),
                               lambda i, j: (0, j, jnp.maximum(i * (tm // GLU_HALO) - 1, 0), 0)),
                  pl.BlockSpec((None, HALO, GU_SHARD), lambda i, j: (j, 0, 0)),
                  pl.BlockSpec((None, 1, GU_SHARD), lambda i, j: (j, 0, 0)),
                  _spec_down(1), pl.BlockSpec((tm, D), lambda i, j: (i, 0))],
        out_specs=[pl.BlockSpec((tm, D), lambda i, j: (i, 0)), pl.BlockSpec((None, tm, GU_SHARD), lambda i, j: (j, i, 0))],
        out_shape=[_SDS((s, D), F32), _SDS((FF_BLOCKS, s, GU_SHARD), _ACT)], name=name,
        compiler_params=_cp("arbitrary", "arbitrary"))(gu, gu, conv_w, conv_b, w_down, res)


def _glu_bwd(gu, conv_w, conv_b, dh, w_down, name, deps=()):
    s = gu.shape[2]
    tm = min(s, _TM_GLU)
    nt = s // tm
    ext_rows = tm + GLU_HALO

    def body(gu_ref, prev_ref, w_ref, b_ref, dh_ref, wdn_ref, *rest):
        dgu_ref, dw_ref, db_ref, carry_ref = rest[-4:]
        t = pl.program_id(1)
        i = nt - 1 - t

        @pl.when(t == 0)
        def _():
            carry_ref[...] = jnp.zeros_like(carry_ref)
            dw_ref[...] = jnp.zeros_like(dw_ref)
            db_ref[...] = jnp.zeros_like(db_ref)

        up = gu_ref[1].astype(F32)
        prev = jnp.where(i > 0, prev_ref[...].astype(F32), 0.0)
        ext = jnp.concatenate([prev, gu_ref[0].astype(F32)], axis=0)
        shifted = [pltpu.roll(ext, FFN_CONV - 1 - j, 0) if j < FFN_CONV - 1 else ext for j in range(FFN_CONV)]
        gc = b_ref[...] + shifted[0] * w_ref[0:1, :]
        for j in range(1, FFN_CONV):
            gc = gc + shifted[j] * w_ref[j:j + 1, :]
        gc = gc[GLU_HALO:]
        sg = jax.nn.sigmoid(gc)
        da = _mm_nt(dh_ref[...], _w2d(wdn_ref))
        dup = da * (gc * sg)
        dgc = da * up * (sg * (1.0 + gc * (1.0 - sg)))
        db_ref[...] += jnp.sum(dgc, axis=0, keepdims=True)
        dgc_ext = jnp.concatenate([jnp.zeros((GLU_HALO, GU_SHARD), F32), dgc], axis=0)
        dext = dgc_ext * w_ref[FFN_CONV - 1:FFN_CONV, :]
        for j in range(FFN_CONV):
            dw_ref[j:j + 1, :] += jnp.sum(shifted[j] * dgc_ext, axis=0, keepdims=True)
            if j < FFN_CONV - 1:
                dext = dext + w_ref[j:j + 1, :] * pltpu.roll(dgc_ext, ext_rows - (FFN_CONV - 1 - j), 0)
        tail = jnp.concatenate([jnp.zeros((tm - GLU_HALO, GU_SHARD), F32), carry_ref[...]], axis=0)
        dgate = dext[GLU_HALO:] + tail
        carry_ref[...] = dext[:GLU_HALO]
        dgu_ref[0] = dgate.astype(dgu_ref.dtype)
        dgu_ref[1] = dup.astype(dgu_ref.dtype)

    return pl.pallas_call(
        body, grid=(FF_BLOCKS, nt),
        in_specs=[pl.BlockSpec((2, None, tm, GU_SHARD), lambda j, t: (0, j, nt - 1 - t, 0)),
                  pl.BlockSpec((None, None, GLU_HALO, GU_SHARD),
                               lambda j, t: (0, j, jnp.maximum((nt - 1 - t) * (tm // GLU_HALO) - 1, 0), 0)),
                  pl.BlockSpec((None, HALO, GU_SHARD), lambda j, t: (j, 0, 0)),
                  pl.BlockSpec((None, 1, GU_SHARD), lambda j, t: (j, 0, 0)),
                  pl.BlockSpec((tm, D), lambda j, t: (nt - 1 - t, 0)), _spec_down(0)] + _dep_specs(deps),
        out_specs=[pl.BlockSpec((2, None, tm, GU_SHARD), lambda j, t: (0, j, nt - 1 - t, 0)),
                   pl.BlockSpec((None, HALO, GU_SHARD), lambda j, t: (j, 0, 0)),
                   pl.BlockSpec((None, 1, GU_SHARD), lambda j, t: (j, 0, 0))],
        out_shape=[_SDS(gu.shape, _ACT), _SDS((FF_BLOCKS, HALO, GU_SHARD), F32), _SDS((FF_BLOCKS, 1, GU_SHARD), F32)],
        scratch_shapes=[pltpu.VMEM((GLU_HALO, GU_SHARD), F32)],
        name=name, compiler_params=_cp("arbitrary", "arbitrary"))(gu, gu, conv_w, conv_b, dh, w_down, *deps)


def _bucket_table():
    qi = np.arange(BLK)[:, None]
    kj = np.arange(BLK)[None, :]
    n = np.where(kj > qi, BLK + qi - kj, qi - kj)
    max_exact = N_BUCKETS // 2
    nf = np.maximum(n, 1).astype(np.float32)
    large = max_exact + (np.log(nf / max_exact) / math.log(MAX_DISTANCE / max_exact)
                         * (N_BUCKETS - max_exact)).astype(np.int32)
    large = np.minimum(large, N_BUCKETS - 1)
    return np.where(n < max_exact, n, large).astype(np.int32)


def _lane_low():
    return lax.broadcasted_iota(jnp.int32, (1, 128), 1) < A_HD


def _swa_groups(q, kd, vd, sink, bias, upper, first):
    n = A_HEADS // A_KV_HEADS
    ng = A_KV_HEADS
    low = _lane_low()
    qm = [jnp.concatenate([jnp.where(low == (h % 2 == 0), q[g][:, (h // 2) * 128:(h // 2 + 1) * 128], 0.0) for h in range(n)], axis=0)
          for g in range(ng)]
    s2 = [_mm_nt(qm[g], kd[g]) * (A_HD ** -0.5) for g in range(ng)]
    s = [jnp.where(upper[None], s2[g][:, :BLK].reshape(n, BLK, BLK), s2[g][:, BLK:].reshape(n, BLK, BLK)) + bias[g] for g in range(ng)]
    s = [jnp.where((upper & first)[None], -jnp.inf, t) for t in s]
    m = [jnp.maximum(jnp.max(s[g], axis=-1, keepdims=True), sink[g]) for g in range(ng)]
    p = [jnp.exp(s[g] - m[g]) for g in range(ng)]
    split = [jnp.concatenate([jnp.where(upper[None], t, 0.0), jnp.where(upper[None], 0.0, t)], axis=-1).reshape(n * BLK, 2 * BLK)
             for t in p]
    ones = jnp.ones((BLK, 128), F32)
    den = [_mm(p[g].reshape(n * BLK, BLK), ones) + jnp.exp(sink[g] - m[g]).reshape(n * BLK, 1) for g in range(ng)]
    o = [_mm(split[g], vd[g]) / den[g] for g in range(ng)]
    return [jnp.concatenate([jnp.where(low, t[2 * k * BLK:(2 * k + 1) * BLK], t[(2 * k + 1) * BLK:(2 * k + 2) * BLK])
                             for k in range(n // 2)], axis=1) for t in o]


def _mix_a_core(q, kd, vd, sink, bias, xq, mk, mv, upper, first):
    return _swa_groups(q, kd, vd, sink, bias, upper, first), _cross_pairs(xq, mk, mv)


def _swa_sinks(sink_ref, g):
    n = A_HEADS // A_KV_HEADS
    return jnp.concatenate([sink_ref[:, h:h + 1] for h in range(g * n, (g + 1) * n)], axis=0).reshape(n, 1, 1)


def _both_halves(t, t_rolled, g):
    low = _lane_low()
    return jnp.where(low, t, t_rolled) if g == 0 else jnp.where(low, t_rolled, t)


def _cross_pairs(q, mk, mv):
    rows = q.shape[0]
    low = _lane_low()
    qm = [jnp.concatenate([jnp.where(low, q[:, p * 128:(p + 1) * 128], 0.0), jnp.where(low, 0.0, q[:, p * 128:(p + 1) * 128])], axis=0)
          for p in range(X_HEADS // 2)]
    s = [_mm_nt(qm[p], mk[:, p * 128:(p + 1) * 128]) * (X_HD ** -0.5) for p in range(X_HEADS // 2)]
    e = [jnp.exp(t - jnp.max(t, axis=-1, keepdims=True)) for t in s]
    pr = [t / jnp.sum(t, axis=-1, keepdims=True) for t in e]
    o = [_mm(pr[p], mv[:, p * 128:(p + 1) * 128]) for p in range(X_HEADS // 2)]
    return jnp.concatenate([jnp.where(low, t[:rows], t[rows:]) for t in o], axis=1)


def _swa_upper():
    qi = lax.broadcasted_iota(jnp.int32, (BLK, BLK), 0)
    kj = lax.broadcasted_iota(jnp.int32, (BLK, BLK), 1)
    return kj > qi


def _bias_build(rel_bias, bucket, name):
    def body(rb_ref, bucket_ref, o_ref):
        b = bucket_ref[...]
        for h in range(A_HEADS):
            acc = jnp.zeros((BLK, BLK), F32)
            for k in range(N_BUCKETS):
                acc = jnp.where(b == k, rb_ref[k, h], acc)
            o_ref[h] = acc

    return pl.pallas_call(
        body, in_specs=[pl.BlockSpec(memory_space=pltpu.SMEM), pl.BlockSpec(memory_space=pltpu.VMEM)],
        out_specs=pl.BlockSpec(memory_space=pltpu.VMEM),
        out_shape=_SDS((A_HEADS, BLK, BLK), F32), name=name)(rel_bias, bucket)


def _bias_reduce(dbias, bucket, name):
    def body(db_ref, bucket_ref, o_ref):
        b = bucket_ref[...]
        row = lax.broadcasted_iota(jnp.int32, (N_BUCKETS, 128), 0)
        lane = lax.broadcasted_iota(jnp.int32, (N_BUCKETS, 128), 1)
        acc = jnp.zeros((N_BUCKETS, 128), F32)
        for h in range(A_HEADS):
            v = db_ref[h]
            for k in range(N_BUCKETS):
                sk = jnp.sum(jnp.sum(jnp.where(b == k, v, 0.0), axis=1, keepdims=True), axis=0, keepdims=True)
                acc = acc + jnp.where((row == k) & (lane == h), sk, 0.0)
        o_ref[...] = acc

    return pl.pallas_call(
        body, in_specs=[pl.BlockSpec(memory_space=pltpu.VMEM)] * 2,
        out_specs=pl.BlockSpec(memory_space=pltpu.VMEM),
        out_shape=_SDS((N_BUCKETS, 128), F32), name=name)(dbias, bucket)


def _mix_a_fwd(proj, bias, sinks, memkv, name):
    s = proj.shape[0]
    nb = s // BLK
    grp = A_HEADS // A_KV_HEADS

    def body(proj_ref, prev_ref, bias_ref, sink_ref, memkv_ref, o_ref):
        i = pl.program_id(0)
        upper = _swa_upper()
        prev = prev_ref[...].astype(F32)
        proj = proj_ref[...].astype(F32)
        kb = jnp.concatenate([prev[:, :A_KV], proj[:, A_Q:A_Q + A_KV]], axis=0)
        vb = jnp.concatenate([prev[:, A_KV:], proj[:, A_Q + A_KV:A_Q + 2 * A_KV]], axis=0)
        kb_r = pltpu.roll(kb, A_HD, 1)
        vb_r = pltpu.roll(vb, A_HD, 1)
        gw = A_Q // A_KV_HEADS
        groups = range(A_KV_HEADS)
        swa, cross = _mix_a_core([proj[:, g * gw:(g + 1) * gw] for g in groups], [_both_halves(kb, kb_r, g) for g in groups],
                                 [_both_halves(vb, vb_r, g) for g in groups], [_swa_sinks(sink_ref, g) for g in groups],
                                 [bias_ref[g * grp:(g + 1) * grp] for g in groups], proj[:, A_Q + 2 * A_KV:],
                                 memkv_ref[:, :X_Q], memkv_ref[:, X_Q:], upper, i == 0)
        o_ref[...] = jnp.concatenate(swa + [cross], axis=1).astype(o_ref.dtype)

    return pl.pallas_call(
        body, grid=(nb,),
        in_specs=[pl.BlockSpec((BLK, IN_A), lambda i: (i, 0)),
                  pl.BlockSpec((BLK, 2 * A_KV), lambda i: (jnp.maximum(i - 1, 0), A_Q // (2 * A_KV))),
                  pl.BlockSpec((A_HEADS, BLK, BLK), lambda i: (0, 0, 0)),
                  pl.BlockSpec((1, 128), lambda i: (0, 0)),
                  pl.BlockSpec((MEM_LEN, 2 * X_Q), lambda i: (0, 0))],
        out_specs=pl.BlockSpec((BLK, D), lambda i: (i, 0)),
        out_shape=_SDS((s, D), _ACT), name=name, compiler_params=_cp("arbitrary"))(proj, proj, bias, sinks, memkv)


def _mix_a_bwd(proj, bias, sinks, memkv, dmix, name):
    s = proj.shape[0]
    nb = s // BLK
    grp = A_HEADS // A_KV_HEADS

    def body(proj_ref, prev_ref, bias_ref, sink_ref, memkv_ref, dmix_ref,
             dproj_ref, dbias_ref, dsink_ref, dmemkv_ref, carry_ref):
        t = pl.program_id(0)
        i = nb - 1 - t

        @pl.when(t == 0)
        def _():
            carry_ref[...] = jnp.zeros_like(carry_ref)
            dbias_ref[...] = jnp.zeros_like(dbias_ref)
            dsink_ref[...] = jnp.zeros_like(dsink_ref)
            dmemkv_ref[...] = jnp.zeros_like(dmemkv_ref)

        upper = _swa_upper()
        lane = lax.broadcasted_iota(jnp.int32, (1, 128), 1)
        low = _lane_low()
        prev = prev_ref[...].astype(F32)
        proj = proj_ref[...].astype(F32)
        kb = jnp.concatenate([prev[:, :A_KV], proj[:, A_Q:A_Q + A_KV]], axis=0)
        vb = jnp.concatenate([prev[:, A_KV:], proj[:, A_Q + A_KV:A_Q + 2 * A_KV]], axis=0)
        kb_r = pltpu.roll(kb, A_HD, 1)
        vb_r = pltpu.roll(vb, A_HD, 1)
        gw = A_Q // A_KV_HEADS
        groups = range(A_KV_HEADS)
        _, vjp = jax.vjp(
            functools.partial(_mix_a_core, upper=upper, first=i == 0),
            [proj[:, g * gw:(g + 1) * gw] for g in groups], [_both_halves(kb, kb_r, g) for g in groups],
            [_both_halves(vb, vb_r, g) for g in groups], [_swa_sinks(sink_ref, g) for g in groups],
            [bias_ref[g * grp:(g + 1) * grp] for g in groups], proj[:, A_Q + 2 * A_KV:], memkv_ref[:, :X_Q], memkv_ref[:, X_Q:])
        dqs, dk, dv, ds, db, dxq, dmk, dmv = vjp(
            ([dmix_ref[:, g * gw:(g + 1) * gw].astype(F32) for g in groups], dmix_ref[:, A_Q:].astype(F32)))
        dkd = [t + pltpu.roll(t, A_HD, 1) for t in dk]
        dvd = [t + pltpu.roll(t, A_HD, 1) for t in dv]
        dsink = jnp.zeros((1, 128), F32)
        for g in groups:
            for h in range(grp):
                dsink = dsink + jnp.where(lane == g * grp + h, ds[g][h], 0.0)
            dbias_ref[g * grp:(g + 1) * grp] += db[g]
        dsink_ref[...] += dsink
        dkb = jnp.where(low, dkd[0], dkd[1])
        dvb = jnp.where(low, dvd[0], dvd[1])
        dmemkv_ref[...] += jnp.concatenate([dmk, dmv], axis=1)
        dkv_cur = jnp.concatenate([dkb[BLK:], dvb[BLK:]], axis=1) + carry_ref[...]
        carry_ref[...] = jnp.concatenate([dkb[:BLK], dvb[:BLK]], axis=1)
        dproj_ref[...] = jnp.concatenate(list(dqs) + [dkv_cur, dxq], axis=1).astype(dproj_ref.dtype)

    return pl.pallas_call(
        body, grid=(nb,),
        in_specs=[pl.BlockSpec((BLK, IN_A), lambda t: (nb - 1 - t, 0)),
                  pl.BlockSpec((BLK, 2 * A_KV), lambda t: (jnp.maximum(nb - 2 - t, 0), A_Q // (2 * A_KV))),
                  pl.BlockSpec((A_HEADS, BLK, BLK), lambda t: (0, 0, 0)),
                  pl.BlockSpec((1, 128), lambda t: (0, 0)),
                  pl.BlockSpec((MEM_LEN, 2 * X_Q), lambda t: (0, 0)),
                  pl.BlockSpec((BLK, D), lambda t: (nb - 1 - t, 0))],
        out_specs=[pl.BlockSpec((BLK, IN_A), lambda t: (nb - 1 - t, 0)),
                   pl.BlockSpec((A_HEADS, BLK, BLK), lambda t: (0, 0, 0)),
                   pl.BlockSpec((1, 128), lambda t: (0, 0)),
                   pl.BlockSpec((MEM_LEN, 2 * X_Q), lambda t: (0, 0))],
        out_shape=[_SDS((s, IN_A), _ACT), _SDS((A_HEADS, BLK, BLK), F32), _SDS((1, 128), F32),
                   _SDS((MEM_LEN, 2 * X_Q), F32)],
        scratch_shapes=[pltpu.VMEM((BLK, 2 * A_KV), F32)],
        name=name, compiler_params=_cp("arbitrary"))(proj, proj, bias, sinks, memkv, dmix)


def _dn_heads(yq, yk, yv, z, bl, al, a_log, dtb, ng, s0):
    c = CHUNK
    nh = B_V_HEADS
    rep = B_V_HEADS // B_QK_HEADS
    r = lax.broadcasted_iota(jnp.int32, (c, c), 0)
    cc = lax.broadcasted_iota(jnp.int32, (c, c), 1)
    q = [_silu(t) for t in yq]
    k = [_silu(t) for t in yk]
    v = [_silu(t) for t in yv]
    q = [t * lax.rsqrt(jnp.sum(t * t, axis=-1, keepdims=True) + EPS) * (B_HD ** -0.5) for t in q]
    k = [t * lax.rsqrt(jnp.sum(t * t, axis=-1, keepdims=True) + EPS) for t in k]
    beta = [jax.nn.sigmoid(t) for t in bl]
    g = [-jnp.exp(a_log[h]) * jax.nn.softplus(al[h] + dtb[h]) for h in range(nh)]
    gb = [jnp.broadcast_to(t, (c, c)) for t in g]
    gc_col = [jnp.sum(jnp.where(cc <= r, t.T, 0.0), axis=1, keepdims=True) for t in gb]
    gc_row = [jnp.sum(jnp.where(r <= cc, t, 0.0), axis=0, keepdims=True) for t in gb]
    gc_last = [jnp.sum(t, axis=0, keepdims=True) for t in g]
    decay = [jnp.exp(jnp.where(r >= cc, gc_col[h] - gc_row[h], -jnp.inf)) for h in range(nh)]
    kq = [_mmf_nt(jnp.concatenate([k[h], q[h]], axis=0), k[h]) for h in range(B_QK_HEADS)]
    kk = [t[:c] for t in kq]
    qk = [t[c:] for t in kq]
    egc = [jnp.exp(t) for t in gc_col]
    both = [_mmf(jnp.concatenate([(beta[h] * egc[h]) * k[h // rep], q[h // rep] * egc[h]], axis=0), s0[h]) for h in range(nh)]
    rhs = [beta[h] * v[h] - both[h][:c] for h in range(nh)]
    qs0 = [t[c:] for t in both]
    pw = [-(beta[h] * kk[h // rep] * jnp.where(r > cc, decay[h], 0.0)) for h in range(nh)]
    x = rhs
    for lvl in range(6):
        if lvl < 5:
            prod = [_mmf(pw[h], jnp.concatenate([x[h], pw[h]], axis=1)) for h in range(nh)]
            x = [x[h] + prod[h][:, :B_HD] for h in range(nh)]
            pw = [t[:, B_HD:] for t in prod]
        else:
            x = [x[h] + _mmf(pw[h], x[h]) for h in range(nh)]
    delta = x
    last = [_mmf(jnp.concatenate([qk[h // rep] * decay[h], (k[h // rep] * jnp.exp(gc_last[h] - gc_col[h])).T], axis=0), delta[h])
            for h in range(nh)]
    out = [qs0[h] + last[h][:c] for h in range(nh)]
    s1 = [jnp.exp(gc_last[h]) * s0[h] + last[h][c:] for h in range(nh)]
    o = [t * lax.rsqrt(jnp.mean(t * t, axis=-1, keepdims=True) + EPS) * ng for t in out]
    return [o[h] * _silu(z[h]) for h in range(nh)], s1


def _dn_conv(ext, w_ref):
    y = ext * w_ref[B_CONV - 1:B_CONV, :]
    for j in range(B_CONV - 1):
        y = y + w_ref[j:j + 1, :] * pltpu.roll(ext, B_CONV - 1 - j, 0)
    return y


def _dn_args(y, cur_ref, par_ref, ng_ref):
    nh = B_V_HEADS
    return ([y[:, h * B_HD:(h + 1) * B_HD] for h in range(B_QK_HEADS)],
            [y[:, B_QK + h * B_HD:B_QK + (h + 1) * B_HD] for h in range(B_QK_HEADS)],
            [y[:, 2 * B_QK + h * B_HD:2 * B_QK + (h + 1) * B_HD] for h in range(nh)],
            [cur_ref[:, BP_Z + h * B_HD:BP_Z + (h + 1) * B_HD] for h in range(nh)],
            [cur_ref[:, BP_GATE + h:BP_GATE + h + 1] for h in range(nh)],
            [cur_ref[:, BP_GATE + nh + h:BP_GATE + nh + h + 1] for h in range(nh)],
            [par_ref[:, h:h + 1] for h in range(nh)], [par_ref[:, nh + h:nh + h + 1] for h in range(nh)], ng_ref[...])


def _mix_b_fwd(proj, conv_w, par, ng, memkv, name):
    s = proj.shape[0]
    nc = s // CHUNK

    def body(cur_ref, prev_ref, w_ref, par_ref, ng_ref, memkv_ref, o_ref, st_ref, state_ref):
        n = pl.program_id(0)

        @pl.when(n == 0)
        def _():
            state_ref[...] = jnp.zeros_like(state_ref)

        prev = jnp.where(n > 0, prev_ref[...], 0.0)
        ext = jnp.concatenate([prev, cur_ref[:, :B_QKV]], axis=0)
        y = _dn_conv(ext, w_ref)[HALO:]
        s0 = [state_ref[hv] for hv in range(B_V_HEADS)]
        st_ref[0] = state_ref[...]
        outs, s1 = _dn_heads(*_dn_args(y, cur_ref, par_ref, ng_ref), s0)
        for hv in range(B_V_HEADS):
            state_ref[hv] = s1[hv]
        outs = outs + [_cross_pairs(cur_ref[:, BP_XQ:BP_XQ + X_Q], memkv_ref[:, :X_Q], memkv_ref[:, X_Q:])]
        o_ref[...] = jnp.concatenate(outs, axis=1).astype(o_ref.dtype)

    return pl.pallas_call(
        body, grid=(nc,),
        in_specs=[pl.BlockSpec((CHUNK, IN_BP), lambda n: (n, 0)),
                  pl.BlockSpec((HALO, B_QKV), lambda n: (jnp.maximum(n * (CHUNK // HALO) - 1, 0), 0)),
                  pl.BlockSpec((HALO, B_QKV), lambda n: (0, 0)),
                  pl.BlockSpec((1, 128), lambda n: (0, 0)), pl.BlockSpec((1, 128), lambda n: (0, 0)),
                  pl.BlockSpec((MEM_LEN, 2 * X_Q), lambda n: (0, 0))],
        out_specs=[pl.BlockSpec((CHUNK, D), lambda n: (n, 0)),
                   pl.BlockSpec((1, B_V_HEADS, B_HD, B_HD), lambda n: (n, 0, 0, 0))],
        out_shape=[_SDS((s, D), _ACT), _SDS((nc, B_V_HEADS, B_HD, B_HD), F32)],
        scratch_shapes=[pltpu.VMEM((B_V_HEADS, B_HD, B_HD), F32)],
        name=name, compiler_params=_cp("arbitrary"))(proj, proj, conv_w, par, ng, memkv)


def _mix_b_bwd(proj, conv_w, par, ng, memkv, states, dmix, name):
    s = proj.shape[0]
    nc = s // CHUNK
    ext_rows = CHUNK + HALO

    def body(cur_ref, prev_ref, w_ref, par_ref, ng_ref, memkv_ref, st_ref, dmix_ref,
             dproj_ref, dw_ref, dpar_ref, dng_ref, dmemkv_ref, dstate_ref, carry_ref):
        t = pl.program_id(0)
        n = nc - 1 - t

        @pl.when(t == 0)
        def _():
            dstate_ref[...] = jnp.zeros_like(dstate_ref)
            carry_ref[...] = jnp.zeros_like(carry_ref)
            dw_ref[...] = jnp.zeros_like(dw_ref)
            dpar_ref[...] = jnp.zeros_like(dpar_ref)
            dng_ref[...] = jnp.zeros_like(dng_ref)
            dmemkv_ref[...] = jnp.zeros_like(dmemkv_ref)

        lane = lax.broadcasted_iota(jnp.int32, (1, 128), 1)
        prev = jnp.where(n > 0, prev_ref[...], 0.0)
        ext = jnp.concatenate([prev, cur_ref[:, :B_QKV]], axis=0)
        y = _dn_conv(ext, w_ref)[HALO:]
        _, vjp = jax.vjp(_dn_heads, *_dn_args(y, cur_ref, par_ref, ng_ref), [st_ref[0, hv] for hv in range(B_V_HEADS)])
        dyq, dyk, dyv, dz, gbl, gal, ga_log, gdtb, dng, gs0 = vjp(
            ([dmix_ref[:, hv * B_HD:(hv + 1) * B_HD].astype(F32) for hv in range(B_V_HEADS)],
             [dstate_ref[hv] for hv in range(B_V_HEADS)]))
        dgate = jnp.zeros((CHUNK, 128), F32)
        dpar = jnp.zeros((1, 128), F32)
        for hv in range(B_V_HEADS):
            dstate_ref[hv] = gs0[hv]
            dgate = dgate + jnp.where(lane == hv, gbl[hv], 0.0) + jnp.where(lane == B_V_HEADS + hv, gal[hv], 0.0)
            dpar = dpar + jnp.where(lane == hv, ga_log[hv], 0.0) + jnp.where(lane == B_V_HEADS + hv, gdtb[hv], 0.0)
        dpar_ref[...] += dpar
        dng_ref[...] += dng
        _, vjp = jax.vjp(_cross_pairs, cur_ref[:, BP_XQ:BP_XQ + X_Q], memkv_ref[:, :X_Q], memkv_ref[:, X_Q:])
        dxq, dmk, dmv = vjp(dmix_ref[:, B_V:].astype(F32))
        dmemkv_ref[...] += jnp.concatenate([dmk, dmv], axis=1)
        dy = jnp.concatenate(list(dyq) + list(dyk) + list(dyv), axis=1)
        dy_ext = jnp.concatenate([jnp.zeros((HALO, B_QKV), F32), dy], axis=0)
        dext = dy_ext * w_ref[B_CONV - 1:B_CONV, :]
        dw_ref[B_CONV - 1:B_CONV, :] += jnp.sum(ext * dy_ext, axis=0, keepdims=True)
        for j in range(B_CONV - 1):
            sh = B_CONV - 1 - j
            dw_ref[j:j + 1, :] += jnp.sum(pltpu.roll(ext, sh, 0) * dy_ext, axis=0, keepdims=True)
            dext = dext + w_ref[j:j + 1, :] * pltpu.roll(dy_ext, ext_rows - sh, 0)
        tail = jnp.concatenate([jnp.zeros((CHUNK - HALO, B_QKV), F32), carry_ref[...]], axis=0)
        dqkv = dext[HALO:] + tail
        carry_ref[...] = dext[:HALO]
        dproj_ref[...] = jnp.concatenate([dqkv] + list(dz) + [dxq, dgate], axis=1).astype(dproj_ref.dtype)

    return pl.pallas_call(
        body, grid=(nc,),
        in_specs=[pl.BlockSpec((CHUNK, IN_BP), lambda t: (nc - 1 - t, 0)),
                  pl.BlockSpec((HALO, B_QKV), lambda t: (jnp.maximum((nc - 1 - t) * (CHUNK // HALO) - 1, 0), 0)),
                  pl.BlockSpec((HALO, B_QKV), lambda t: (0, 0)),
                  pl.BlockSpec((1, 128), lambda t: (0, 0)), pl.BlockSpec((1, 128), lambda t: (0, 0)),
                  pl.BlockSpec((MEM_LEN, 2 * X_Q), lambda t: (0, 0)),
                  pl.BlockSpec((1, B_V_HEADS, B_HD, B_HD), lambda t: (nc - 1 - t, 0, 0, 0)),
                  pl.BlockSpec((CHUNK, D), lambda t: (nc - 1 - t, 0))],
        out_specs=[pl.BlockSpec((CHUNK, IN_BP), lambda t: (nc - 1 - t, 0)),
                   pl.BlockSpec((HALO, B_QKV), lambda t: (0, 0)),
                   pl.BlockSpec((1, 128), lambda t: (0, 0)), pl.BlockSpec((1, 128), lambda t: (0, 0)),
                   pl.BlockSpec((MEM_LEN, 2 * X_Q), lambda t: (0, 0))],
        out_shape=[_SDS((s, IN_BP), _ACT), _SDS((HALO, B_QKV), F32), _SDS((1, 128), F32), _SDS((1, 128), F32),
                   _SDS((MEM_LEN, 2 * X_Q), F32)],
        scratch_shapes=[pltpu.VMEM((B_V_HEADS, B_HD, B_HD), F32), pltpu.VMEM((HALO, B_QKV), F32)],
        name=name, compiler_params=_cp("arbitrary"))(proj, proj, conv_w, par, ng, memkv, states, dmix)


def _place():
    return lax.axis_index("x"), lax.axis_index("y"), lax.axis_index("c")


def _all_gather(shards, name):
    n = len(shards)

    def body(*refs):
        ins, outs = refs[:n], refs[n:2 * n]
        send_sems, recv_sems, local_sems = refs[2 * n:]
        x, y, c = _place()
        me, sibling = (x, y, c), (x, y, 1 - c)
        chips = [(1 - x, y), (x, 1 - y), (1 - x, 1 - y)]

        def rows(a, px, py, pc):
            return outs[a].at[4 * px + 2 * py + pc]

        def copy(a, k, block, to, src=None):
            return pltpu.make_async_remote_copy(
                src_ref=rows(a, *block) if src is None else src, dst_ref=rows(a, *block),
                send_sem=send_sems.at[a, k], recv_sem=recv_sems.at[a, k],
                device_id=to, device_id_type=pl.DeviceIdType.MESH)

        mine = [pltpu.make_async_copy(ins[a], rows(a, *me), local_sems.at[a]) for a in range(n)]
        for cp in mine:
            cp.start()
        first = []
        for a in range(n):
            first.append(copy(a, 0, me, sibling, src=ins[a]))
            first += [copy(a, 1 + j, me, (*chip, c), src=ins[a]) for j, chip in enumerate(chips)]
        for cp in first:
            cp.start()
        passed = []
        for j, chip in enumerate(chips):
            for a in range(n):
                copy(a, 1 + j, (*chip, c), me).wait_recv()
                fwd = copy(a, 4 + j, (*chip, c), sibling)
                fwd.start()
                passed.append(fwd)
        for a in range(n):
            copy(a, 0, sibling, me).wait_recv()
            for j, chip in enumerate(chips):
                copy(a, 4 + j, (*chip, 1 - c), me).wait_recv()
        for cp in first + passed:
            cp.wait_send()
        for cp in mine:
            cp.wait()

    hbm = pl.BlockSpec(memory_space=pl.ANY)
    return pl.pallas_call(
        body, out_shape=[_SDS((N_DEV,) + s.shape, s.dtype) for s in shards],
        in_specs=[hbm] * n, out_specs=[hbm] * n,
        scratch_shapes=[pltpu.SemaphoreType.DMA((n, 7)), pltpu.SemaphoreType.DMA((n, 7)), pltpu.SemaphoreType.DMA((n,))],
        name=name)(*shards)


class _Exchange:
    def __init__(self, lands, srcs):
        self.lands, self.srcs = lands, srcs


def _seq_exchange(srcs, land_shapes, plan, name, cid):
    n, nl = len(srcs), len(land_shapes)

    def launch(*refs):
        src_refs, land_refs = refs[:n], refs[n:n + nl]
        send_sems, recv_sems, local_sems = refs[n + nl:]
        x, y, c = _place()
        my = 4 * x + 2 * y + c
        peers = [(x ^ ((k + 1) >> 2 & 1), y ^ ((k + 1) >> 1 & 1), c ^ ((k + 1) & 1)) for k in range(N_DEV - 1)]
        barrier = pltpu.get_barrier_semaphore()
        for p in peers:
            pl.semaphore_signal(barrier, inc=1, device_id=p, device_id_type=pl.DeviceIdType.MESH)
        pl.semaphore_wait(barrier, N_DEV - 1)

        def src_for(a, dest):
            return src_refs[a].at[dest] if plan[a][1] else src_refs[a]

        def slot(a, source):
            return land_refs[plan[a][0]].at[source]

        mine = [pltpu.make_async_copy(src_for(a, my), slot(a, my), local_sems.at[a]) for a in range(n)]
        for cp in mine:
            cp.start()
        sends, recvs = [], []
        for k, (px, py, pc) in enumerate(peers):
            peer = 4 * px + 2 * py + pc
            for a in range(n):
                kw = dict(send_sem=send_sems.at[a * (N_DEV - 1) + k], recv_sem=recv_sems.at[a * (N_DEV - 1) + k],
                          device_id=(px, py, pc), device_id_type=pl.DeviceIdType.MESH)
                sends.append(pltpu.make_async_remote_copy(src_ref=src_for(a, peer), dst_ref=slot(a, my), **kw))
                recvs.append(pltpu.make_async_remote_copy(src_ref=src_for(a, my), dst_ref=slot(a, peer), **kw))
        for cp in sends:
            cp.start()
        for cp in recvs:
            cp.wait_recv()
        for cp in sends:
            cp.wait_send()
        for cp in mine:
            cp.wait()

    lands = pl.kernel(
        launch, out_type=[_SDS(s, d) for s, d in land_shapes],
        mesh=plsc.ScalarSubcoreMesh(axis_name="sequencer", num_cores=1), name=name,
        scratch_types=(pltpu.SemaphoreType.DMA((n * (N_DEV - 1),)), pltpu.SemaphoreType.DMA((n * (N_DEV - 1),)),
                       pltpu.SemaphoreType.DMA((n,))),
        compiler_params=pltpu.CompilerParams(collective_id=cid))(*srcs)
    return _Exchange(list(lands), list(srcs))


def _adam_update(g, w, m, v):
    c1 = 1.0 - ADAM_B1 ** ADAM_STEP
    c2 = 1.0 - ADAM_B2 ** ADAM_STEP
    mm = ADAM_B1 * m + (1.0 - ADAM_B1) * g
    vv = ADAM_B2 * v + (1.0 - ADAM_B2) * (g * g)
    delta = -ADAM_LR * ((mm / c1) / (jnp.sqrt(vv / c2) + ADAM_EPS) + ADAM_WD * w)
    return delta, mm, vv


def _sum_sources(p_ref):
    g = p_ref[0].astype(F32)
    for s in range(1, N_DEV):
        g = g + p_ref[s].astype(F32)
    return g


def _adamw(parts, w, m, v, tr, name, restore_b=False, deps=()):
    nl, r, c = w.shape
    cp = parts[0].shape[-1]

    def body(*refs):
        p_refs = refs[:nl]
        w_ref, m_ref, v_ref = refs[nl:nl + 3]
        g_ref, d_ref, nm_ref, nv_ref = refs[-4:]
        g = _sum_sources(p_refs[0])
        for l in range(1, nl):
            g = jnp.where(pl.program_id(0) == l, _sum_sources(p_refs[l]), g)
        if restore_b:
            g = jnp.concatenate([g[:, :BP_XQ], g[:, BP_GATE:BP_GATE + 2 * B_V_HEADS], g[:, BP_XQ:BP_GATE]], axis=1)
        delta, mm, vv = _adam_update(g, w_ref[...], m_ref[...], v_ref[...])
        g_ref[...] = g
        d_ref[...] = delta
        nm_ref[...] = mm
        nv_ref[...] = vv

    spec = pl.BlockSpec((None, tr, c), lambda l, i: (l, i, 0))
    part_specs = [pl.BlockSpec((N_DEV, tr, cp), functools.partial(lambda l, i, k: (0, jnp.where(l == k, i, 0), 0), k=k))
                  for k in range(nl)]
    return pl.pallas_call(
        body, grid=(nl, r // tr),
        in_specs=part_specs + [spec, spec, spec] + _dep_specs(deps),
        out_specs=[spec] * 4, out_shape=[_SDS(w.shape, F32)] * 4,
        name=name, compiler_params=_cp("arbitrary", "arbitrary"))(*parts, w, m, v, *deps)


def _pack_small(d_rel, d_cb, d_cw, d_qkv, d_mix, d_mem, d_ffn, d_final, d_sinks, d_par, d_ng, name):
    flat = [d_rel, *d_cb, *d_cw, d_qkv, *d_mix, *d_mem, *d_ffn, d_final, d_sinks, d_par, d_ng]
    n = len(flat)

    def body(*refs):
        ins, o_ref = refs[:n], refs[n]
        rel, cb0, cb1, cw0, cw1, qkv, mx0, mx1, me0, me1, ff0, ff1, fin, snk, par, ng = ins
        o_ref[...] = jnp.zeros_like(o_ref)
        for k in range(N_BUCKETS):
            lane = SP_REL_LANE + 128 * (k % 8)
            o_ref[SP_QKV + k // 8:SP_QKV + k // 8 + 1, lane:lane + 128] = rel[k:k + 1, :]
        for l, (cb, cw) in enumerate(((cb0, cw0), (cb1, cw1))):
            o_ref[SP_CB + l:SP_CB + l + 1, :] = jnp.concatenate([cb[j] for j in range(FF_BLOCKS)], axis=1)
            full = jnp.concatenate([cw[j] for j in range(FF_BLOCKS)], axis=1)
            o_ref[SP_CW + FFN_CONV * l:SP_CW + FFN_CONV * (l + 1), :] = full[:FFN_CONV]
        o_ref[SP_QKV:SP_QKV + B_CONV, 0:B_QKV] = qkv[0:B_CONV, :]
        for base, pair in ((SP_MIX, (mx0, mx1)), (SP_MEM, (me0, me1)), (SP_FFN, (ff0, ff1))):
            for l in range(2):
                o_ref[base + l:base + l + 1, 0:D] = pair[l][...]
        o_ref[SP_FINAL:SP_FINAL + 1, 0:D] = fin[...]
        o_ref[SP_MISC:SP_MISC + 1, 0:128] = snk[...]
        o_ref[SP_MISC:SP_MISC + 1, 128:256] = par[...]
        o_ref[SP_MISC:SP_MISC + 1, 256:384] = ng[...]

    vm = pl.BlockSpec(memory_space=pltpu.VMEM)
    return pl.pallas_call(body, in_specs=[vm] * n, out_specs=vm, out_shape=_SDS((SMALL_ROWS, D_FF), F32), name=name)(*flat)


_SMALL = ["rel_bias", "norm_mix_g", "norm_mem_g", "sinks_a", "a_log_b", "dt_bias_b", "out_norm_g_b", "norm_ffn_g",
          "ffn_conv_b", "final_norm_g", "conv_qkv_b", "ffn_conv_w"]


def _adamw_small(recv, rc_qkv, rc_ffn, ws, ms, vs, name, deps=()):
    n = len(_SMALL)

    def body(*refs):
        recv_ref, qkv_ref, ffn_ref = refs[:3]
        w_refs, m_refs, v_refs = refs[3:3 + n], refs[3 + n:3 + 2 * n], refs[3 + 2 * n:3 + 3 * n]
        outs = refs[len(refs) - 4 * n:]
        gs = _sum_sources(recv_ref)
        grads = {
            "rel_bias": jnp.concatenate(
                [gs[SP_QKV + k // 8:SP_QKV + k // 8 + 1, SP_REL_LANE + 128 * (k % 8):SP_REL_LANE + 128 * (k % 8) + A_HEADS]
                 for k in range(N_BUCKETS)], axis=0),
            "norm_mix_g": gs[SP_MIX:SP_MIX + 2, 0:D], "norm_mem_g": gs[SP_MEM:SP_MEM + 2, 0:D],
            "sinks_a": gs[SP_MISC:SP_MISC + 1, 0:A_HEADS],
            "a_log_b": gs[SP_MISC:SP_MISC + 1, 128:128 + B_V_HEADS],
            "dt_bias_b": gs[SP_MISC:SP_MISC + 1, 128 + B_V_HEADS:128 + 2 * B_V_HEADS],
            "out_norm_g_b": gs[SP_MISC:SP_MISC + 1, 256:256 + B_HD],
            "norm_ffn_g": gs[SP_FFN:SP_FFN + 2, 0:D], "ffn_conv_b": gs[SP_CB:SP_CB + 2, :],
            "final_norm_g": gs[SP_FINAL:SP_FINAL + 1, 0:D],
            "conv_qkv_b": _sum_sources(qkv_ref), "ffn_conv_w": _sum_sources(ffn_ref),
        }
        for i, nm in enumerate(_SMALL):
            g = grads[nm]
            delta, mm, vv = _adam_update(g, w_refs[i][...], m_refs[i][...], v_refs[i][...])
            outs[i][...] = g
            outs[n + i][...] = delta
            outs[2 * n + i][...] = mm
            outs[3 * n + i][...] = vv

    vm = pl.BlockSpec(memory_space=pltpu.VMEM)
    shapes = [_SDS(w.shape, F32) for w in ws]
    return pl.pallas_call(
        body, in_specs=[vm] * (3 + 3 * n) + _dep_specs(deps), out_specs=[vm] * (4 * n), out_shape=shapes * 4,
        name=name)(recv, rc_qkv, rc_ffn, *ws, *ms, *vs, *deps)


def _assemble(gathered, axis):
    g = jnp.moveaxis(gathered, 0, axis)
    shp = list(g.shape)
    return g.reshape(shp[:axis] + [shp[axis] * shp[axis + 1]] + shp[axis + 2:])


def _pad_rows(a, rows):
    return jnp.pad(a, ((0, rows - a.shape[0]), (0, 0)))


def _pad_lanes(a, lanes=128):
    return jnp.pad(a, ((0, 0), (0, lanes - a.shape[1])))


def _ff_blocks(a):
    return jnp.moveaxis(a.reshape(a.shape[0], FF_BLOCKS, GU_SHARD), 1, 0)


def _reorder_b(w):
    qkv_z = w[..., :B_QKV + B_V]
    gates = w[..., B_QKV + B_V:B_QKV + B_V + 2 * B_V_HEADS]
    xq = w[..., IN_B - X_Q:]
    pad = jnp.zeros(w.shape[:-1] + (IN_BP - IN_B,), w.dtype)
    return jnp.concatenate([qkv_z, xq, gates, pad], axis=-1)


def kernel(x, mem, rel_bias, norm_mix_g, norm_mem_g, w_mem_kv, w_out, w_in_a, sinks_a, w_in_b, conv_qkv_b, a_log_b, dt_bias_b, out_norm_g_b, norm_ffn_g, w_gate_up, ffn_conv_w, ffn_conv_b, w_down, final_norm_g, loss_target, m_rel_bias, m_norm_mix_g, m_norm_mem_g, m_w_mem_kv, m_w_out, m_w_in_a, m_sinks_a, m_w_in_b, m_conv_qkv_b, m_a_log_b, m_dt_bias_b, m_out_norm_g_b, m_norm_ffn_g, m_w_gate_up, m_ffn_conv_w, m_ffn_conv_b, m_w_down, m_final_norm_g, v_rel_bias, v_norm_mix_g, v_norm_mem_g, v_w_mem_kv, v_w_out, v_w_in_a, v_sinks_a, v_w_in_b, v_conv_qkv_b, v_a_log_b, v_dt_bias_b, v_out_norm_g_b, v_norm_ffn_g, v_w_gate_up, v_ffn_conv_w, v_ffn_conv_b, v_w_down, v_final_norm_g):
    local = dict(locals())
    order = ["rel_bias", "norm_mix_g", "norm_mem_g", "w_mem_kv", "w_out", "w_in_a", "sinks_a", "w_in_b", "conv_qkv_b",
             "a_log_b", "dt_bias_b", "out_norm_g_b", "norm_ffn_g", "w_gate_up", "ffn_conv_w", "ffn_conv_b", "w_down",
             "final_norm_g"]
    wts = {n: local[n] for n in order}
    moms = {n: local["m_" + n] for n in order}
    vars_ = {n: local["v_" + n] for n in order}
    h0 = x[0]
    memx = mem[0]
    tgt = loss_target[0]
    s = h0.shape[0]
    tm = _rows(s)

    t_ = lambda a: jnp.swapaxes(a, 1, 2)
    g_mk, g_out, g_ia, g_cq, g_cw = _all_gather(
        [w_mem_kv.astype(_MXU), w_out.astype(_MXU), t_(w_in_a).astype(_MXU), conv_qkv_b, ffn_conv_w], "gather_first")
    gu_land = ((N_DEV, GU_SHARD, D), _MXU)
    dn_land = ((N_DEV, DN_SHARD, D), _MXU)
    whole = [(0, False), (1, False)]
    def after(a, b):
        return a + (b[(0,) * b.ndim] * 0).astype(a.dtype)

    ffn0_w = _seq_exchange([after(t_(w_gate_up)[0].astype(_MXU), g_ia), after(w_down[0].astype(_MXU), g_ia)], [gu_land, dn_land],
                           whole, "gather_ffn0", 1)
    w_ia = g_ia.reshape(IN_A, D)
    conv_qkv = _pad_rows(_assemble(g_cq, 2)[0], HALO)
    ffn_cw_full = _assemble(g_cw, 2)
    ffn_cw = [_ff_blocks(_pad_rows(ffn_cw_full[i], HALO)) for i in range(2)]
    ffn_cb = [_ff_blocks(ffn_conv_b[i:i + 1]) for i in range(2)]
    bucket = jnp.asarray(_bucket_table())
    bias = _bias_build(rel_bias, bucket, "bias_build")
    sinks = _pad_lanes(sinks_a)
    par_b = _pad_lanes(jnp.concatenate([a_log_b, dt_bias_b], axis=1))

    row_x = pl.BlockSpec((tm, D), lambda i, j: (i, 0))
    gu_shape = (2, FF_BLOCKS, s, GU_SHARD)

    def in_proj(h, g, w, w_spec, n_cols, tn, name, deps=(), out_dtype=F32, w_t=False):
        return _norm_matmul(h, g, w, w_spec, n_cols // tn, (h.shape[0], n_cols),
                            pl.BlockSpec((_rows(h.shape[0]), tn), lambda i, j: (i, j)), name, deps=deps, out_dtype=out_dtype,
                            w_t=w_t)

    def ffn_fwd(i, h, g_gu, g_dn, deps=()):
        gu, hn = _norm_matmul(h, norm_ffn_g[i:i + 1], g_gu, _spec_gate_up(1), N_DEV, gu_shape,
                              _spec_gu_act(0, 1, tm), f"gate_up_{i}", deps=deps, out_dtype=_ACT, w_t=True)
        h_new, act = _glu_down(gu, ffn_cw[i], ffn_cb[i], g_dn, h, f"glu_down_{i}")
        return h_new, gu, hn, act

    def out_proj(i, mix, h):
        return _matmul_res(mix, row_x, g_out, _spec_rowsharded(i, D // N_DEV, D), 1, h, f"out_proj_{i}")

    proj_a, hn_a = in_proj(h0, norm_mix_g[0:1], w_ia, pl.BlockSpec((640, D), lambda i, j: (j, 0)), IN_A, 640, "in_proj_a",
                           deps=ffn0_w.srcs, out_dtype=_ACT, w_t=True)
    memkv0, memn0 = in_proj(memx, norm_mem_g[0:1], g_mk, _spec_rowsharded(0, D // N_DEV, 2 * X_Q), 2 * X_Q, 2 * X_Q, "mem_proj_0")
    mix_a = _mix_a_fwd(proj_a, bias, sinks, memkv0, "mix_a_fwd")
    h1 = out_proj(0, mix_a, h0)
    g_gu0, g_dn0 = ffn0_w.lands
    in_b_w = _seq_exchange([after(_reorder_b(w_in_b).astype(_MXU), h1)], [((N_DEV, 1, D // N_DEV, IN_BP), _MXU)], [(0, False)],
                           "gather_in_b", 2)
    ffn1_w = _seq_exchange([after(t_(w_gate_up)[1].astype(_MXU), h1), after(w_down[1].astype(_MXU), h1)], [gu_land, dn_land], whole,
                           "gather_ffn1", 3)
    h2, gu0, hn_f0, act0 = ffn_fwd(0, h1, g_gu0, g_dn0, deps=in_b_w.srcs + ffn1_w.srcs)
    g_ib, = in_b_w.lands
    proj_b, hn_b = in_proj(h2, norm_mix_g[1:2], g_ib, _spec_rowsharded(0, D // N_DEV, 896, col_block=1), IN_BP, 896, "in_proj_b")
    memkv1, memn1 = in_proj(memx, norm_mem_g[1:2], g_mk, _spec_rowsharded(1, D // N_DEV, 2 * X_Q), 2 * X_Q, 2 * X_Q, "mem_proj_1")
    mix_b, states = _mix_b_fwd(proj_b, conv_qkv, par_b, out_norm_g_b, memkv1, "mix_b_fwd")
    h3 = out_proj(1, mix_b, h2)
    g_gu1, g_dn1 = ffn1_w.lands
    h4, gu1, hn_f1, act1 = ffn_fwd(1, h3, g_gu1, g_dn1)
    loss_row, dh, d_final_g = _loss_head(h4, final_norm_g[None, :], tgt, "loss_head")

    zeros_mem = jnp.zeros_like(memx)
    per_dest2 = [(0, True), (1, True)]

    def ffn_bwd(i, dh, h_in, gu, hn_f, act, g_gu, g_dn, deps=()):
        dgu, d_cw, d_cb = _glu_bwd(gu, ffn_cw[i], ffn_cb[i], dh, g_dn, f"glu_bwd_{i}", deps=deps)
        d_wdown = _matmul_tn(act, pl.BlockSpec((None, tm, GU_SHARD), lambda j, r: (j, r, 0)),
                             dh, pl.BlockSpec((tm, D), lambda j, r: (r, 0)), s, FF_BLOCKS, (GU_SHARD, D),
                             (N_DEV, DN_SHARD, D), pl.BlockSpec((2, DN_SHARD, D), lambda j, r: (j, 0, 0)), f"d_w_down_{i}")
        dh_new, d_g = _matmul_nt_normbwd(dgu, _spec_gu_act(0, 1, tm), g_gu, _spec_gate_up(1), N_DEV, h_in,
                                         norm_ffn_g[i:i + 1], dh, f"d_ffn_in_{i}", w_t=True)
        d_wgu = _matmul_tn(dgu, _spec_gu_act(1, 0, tm), hn_f, pl.BlockSpec((tm, D), lambda j, r: (r, 0)), s, N_DEV,
                           (GU_SHARD, D), (N_DEV, GU_SHARD, D), pl.BlockSpec((None, GU_SHARD, D), lambda j, r: (j, 0, 0)),
                           f"d_w_gate_up_{i}")
        sent = _seq_exchange([d_wdown, d_wgu], [((N_DEV, DN_SHARD, D), _WIRE), ((N_DEV, GU_SHARD, D), _WIRE)], per_dest2,
                             f"send_ffn{i}_grads", 4 + i)
        return dh_new, sent, d_cw, d_cb, d_g

    def out_bwd(i, dh, mix, deps):
        dmix = _matmul_nt(dh, g_out, _spec_rowsharded(i, D // N_DEV, D), 1, (s, D), row_x, f"d_mix_{i}", deps=deps, out_dtype=_ACT)
        d_wout = _matmul_tn(mix, pl.BlockSpec((tm, D), lambda j, r: (r, 0)), dh, pl.BlockSpec((tm, D), lambda j, r: (r, 0)),
                            s, 1, (D, D), (N_DEV, D // N_DEV, D), pl.BlockSpec((N_DEV, D // N_DEV, D), lambda j, r: (0, 0, 0)),
                            f"d_w_out_{i}")
        return dmix, d_wout

    def mem_bwd(i, dmemkv, memn):
        tmm = _rows(MEM_LEN)
        _, d_g = _matmul_nt_normbwd(dmemkv, pl.BlockSpec((tmm, 2 * X_Q), lambda r, j: (r, 0)), g_mk,
                                    _spec_rowsharded(i, D // N_DEV, 2 * X_Q), 1, memx, norm_mem_g[i:i + 1], zeros_mem,
                                    f"d_mem_in_{i}")
        by_row = lambda j, r: (r, 0)
        d_w = _matmul_tn(memn, pl.BlockSpec((tmm, D), by_row), dmemkv, pl.BlockSpec((tmm, 2 * X_Q), by_row), MEM_LEN, 1,
                         (D, 2 * X_Q), (N_DEV, D // N_DEV, 2 * X_Q),
                         pl.BlockSpec((N_DEV, D // N_DEV, 2 * X_Q), lambda j, r: (0, 0, 0)), f"d_w_mem_kv_{i}")
        return d_w, d_g

    out_land = ((N_DEV, D // N_DEV, D), _WIRE)
    mk_land = ((N_DEV, D // N_DEV, 2 * X_Q), _WIRE)
    dh, ffn1_g, d_cw1, d_cb1, d_gf1 = ffn_bwd(1, dh, h3, gu1, hn_f1, act1, g_gu1, g_dn1)
    dmix, d_wout1 = out_bwd(1, dh, mix_b, ffn1_g.srcs)
    dproj_b, d_convw, d_par, d_ng, dmemkv1 = _mix_b_bwd(proj_b, conv_qkv, par_b, out_norm_g_b, memkv1, states, dmix, "mix_b_bwd")
    dh, d_gm1 = _matmul_nt_normbwd(dproj_b, pl.BlockSpec((tm, 896), lambda i, j: (i, j)), g_ib,
                                   _spec_rowsharded(0, D // N_DEV, 896, col_block=1), IN_BP // 896, h2, norm_mix_g[1:2], dh, "d_in_b")
    d_wib = _matmul_tn(hn_b, pl.BlockSpec((tm, D), lambda j, r: (r, 0)), dproj_b, pl.BlockSpec((tm, 896), lambda j, r: (r, j)),
                       s, IN_BP // 896, (D, 896), (N_DEV, D // N_DEV, IN_BP),
                       pl.BlockSpec((N_DEV, D // N_DEV, 896), lambda j, r: (0, 0, j)), "d_w_in_b")
    d_wmk1, d_gmem1 = mem_bwd(1, dmemkv1, memn1)
    mix1_g = _seq_exchange([d_wout1, d_wib, d_wmk1], [out_land, ((N_DEV, D // N_DEV, IN_BP), _WIRE), mk_land],
                           [(0, True), (1, True), (2, True)], "send_mix1_grads", 6)
    dh, ffn0_g, d_cw0, d_cb0, d_gf0 = ffn_bwd(0, dh, h1, gu0, hn_f0, act0, g_gu0, g_dn0, deps=mix1_g.srcs)
    dmix, d_wout0 = out_bwd(0, dh, mix_a, ffn0_g.srcs + ffn1_g.lands[:1])
    dproj_a, dbias, dsinks, dmemkv0 = _mix_a_bwd(proj_a, bias, sinks, memkv0, dmix, "mix_a_bwd")
    dh, d_gm0 = _matmul_nt_normbwd(dproj_a, pl.BlockSpec((tm, 640), lambda i, j: (i, j)), w_ia,
                                   pl.BlockSpec((640, D), lambda i, j: (j, 0)), IN_A // 640, h0, norm_mix_g[0:1], dh, "d_in_a",
                                   w_t=True)
    d_wia = _matmul_tn(dproj_a, pl.BlockSpec((tm, IN_A), lambda j, r: (r, 0)), hn_a, pl.BlockSpec((tm, D), lambda j, r: (r, 0)),
                       s, 1, (IN_A, D), (N_DEV, IA_SHARD, D), pl.BlockSpec((N_DEV, IA_SHARD, D), lambda j, r: (0, 0, 0)),
                       "d_w_in_a")
    d_wmk0, d_gmem0 = mem_bwd(0, dmemkv0, memn0)
    d_rel = _bias_reduce(dbias, bucket, "bias_reduce")
    small = _pack_small(d_rel, (d_cb0, d_cb1), (d_cw0, d_cw1), d_convw, (d_gm0, d_gm1), (d_gmem0, d_gmem1),
                        (d_gf0, d_gf1), d_final_g, dsinks, d_par, d_ng, "pack_small")
    mix0_g = _seq_exchange([d_wout0, d_wia, d_wmk0, small],
                           [out_land, ((N_DEV, IA_SHARD, D), _WIRE), mk_land, ((N_DEV, SMALL_ROWS, D_FF), F32)],
                           [(0, True), (1, True), (2, True), (3, False)], "send_mix0_grads", 7)

    res = {}
    last = []

    def update(nm, parts, tr, restore=False, transposed=False):
        view = t_ if transposed else (lambda a: a)
        out = _adamw(parts, view(wts[nm]), view(moms[nm]), view(vars_[nm]), tr, "adamw_" + nm, restore_b=restore, deps=last[-1:])
        res[nm] = [view(o) for o in out]
        last.append(out[1])

    r_dn1, r_gu1 = ffn1_g.lands
    r_dn0, r_gu0 = ffn0_g.lands
    r_out1, r_ib, r_mk1 = mix1_g.lands
    update("w_gate_up", [r_gu0, r_gu1], 176, transposed=True)
    update("w_down", [r_dn0, r_dn1], 176)
    update("w_in_b", [r_ib], 32, True)
    r_out0, r_ia, r_mk0, r_small = mix0_g.lands
    update("w_mem_kv", [r_mk0, r_mk1], 128)
    update("w_out", [r_out0, r_out1], 128)
    update("w_in_a", [r_ia], IA_SHARD, transposed=True)

    my = 4 * lax.axis_index("x") + 2 * lax.axis_index("y") + lax.axis_index("c")
    cq = conv_qkv_b.shape[-1]
    cf = ffn_conv_w.shape[-1]
    rc_qkv = lax.dynamic_slice_in_dim(r_small[:, SP_QKV:SP_QKV + B_CONV, :B_QKV], my * cq, cq, axis=2)[:, None]
    rc_ffn = lax.dynamic_slice_in_dim(r_small[:, SP_CW:SP_CW + 2 * FFN_CONV, :], my * cf, cf, axis=2).reshape(N_DEV, 2, FFN_CONV, cf)
    as2d = lambda a: a[None, :] if a.ndim == 1 else a
    small_out = _adamw_small(r_small, rc_qkv, rc_ffn, [as2d(wts[n]) for n in _SMALL], [as2d(moms[n]) for n in _SMALL],
                             [as2d(vars_[n]) for n in _SMALL], "adamw_small", deps=last[-1:])
    ns = len(_SMALL)
    for i, nm in enumerate(_SMALL):
        res[nm] = [small_out[k * ns + i].reshape(wts[nm].shape) for k in range(4)]

    loss = lax.psum(loss_row[0, 0], AXES)
    return (loss, dh[None], *[res[n][0] for n in order], *[res[n][1] for n in order],
            *[res[n][2] for n in order], *[res[n][3] for n in order])
```

```python
import functools
import math

import numpy as np

import jax
import jax.numpy as jnp
from jax import lax
from jax.experimental import pallas as pl
from jax.experimental.pallas import tpu as pltpu
from jax.experimental.pallas import tpu_sc as plsc

F32 = jnp.float32
_MXU = jnp.bfloat16
_ACT = jnp.bfloat16
_WIRE = jnp.bfloat16
_HI = lax.Precision.HIGH
_TM = 1024
_TM_GLU = 512
_TM_BIG = 2048
_VMEM_LIMIT = 48 * 1024 * 1024
_SDS = jax.ShapeDtypeStruct

D = 1024
EPS = 1e-6
A_HEADS, A_KV_HEADS, A_HD, BLK = 12, 2, 64, 128
N_BUCKETS, MAX_DISTANCE = 32, 128
B_QK_HEADS, B_V_HEADS, B_HD, B_CONV, CHUNK = 3, 6, 128, 4, 64
X_HEADS, X_HD, MEM_LEN = 4, 64, 256
D_FF, FFN_CONV = 2816, 3
A_Q, A_KV, X_Q = 768, 128, 256
B_QK, B_V, B_QKV = 384, 768, 1536
IN_A, IN_B = 1280, 2572
IN_BP = 2688
BP_Z, BP_XQ, BP_GATE = 1536, 2304, 2560
HALO = 8
GLU_HALO = 16

N_DEV = 8
GU_SHARD = 2 * D_FF // N_DEV
FF_BLOCKS = D_FF // GU_SHARD
DN_SHARD = D_FF // N_DEV
IA_SHARD = IN_A // N_DEV

ADAM_LR, ADAM_B1, ADAM_B2, ADAM_EPS, ADAM_WD, ADAM_STEP = 0.001, 0.9, 0.999, 1e-08, 0.01, 10

SP_CB, SP_CW, SP_QKV, SP_MIX, SP_MEM, SP_FFN, SP_FINAL, SP_MISC, SMALL_ROWS = 0, 2, 8, 12, 14, 16, 18, 19, 24
SP_REL_LANE = B_QKV


def _cp(*sems):
    return pltpu.CompilerParams(dimension_semantics=sems, vmem_limit_bytes=_VMEM_LIMIT)


def _mm(a, b):
    return jnp.dot(a.astype(_MXU), b.astype(_MXU), preferred_element_type=F32)


def _mm_nt(a, b):
    return lax.dot_general(a.astype(_MXU), b.astype(_MXU), (((1,), (1,)), ((), ())), preferred_element_type=F32)


def _mm_tn(a, b):
    return lax.dot_general(a.astype(_MXU), b.astype(_MXU), (((0,), (0,)), ((), ())), preferred_element_type=F32)


def _mmf(a, b):
    return jnp.dot(a, b, preferred_element_type=F32, precision=_HI)


def _mmf_nt(a, b):
    return lax.dot_general(a, b, (((1,), (1,)), ((), ())), preferred_element_type=F32, precision=_HI)


def _mmf_tn(a, b):
    return lax.dot_general(a, b, (((0,), (0,)), ((), ())), preferred_element_type=F32, precision=_HI)


def _silu(x):
    return x * jax.nn.sigmoid(x)


def _w2d(ref):
    v = ref[...]
    return v.reshape(-1, v.shape[-1])


def _rows(m):
    return min(m, _TM)


def _spec_rowsharded(layer, rows, cols, col_block=None):
    if col_block is None:
        return pl.BlockSpec((N_DEV, None, rows, cols), lambda *_: (0, layer, 0, 0))
    return pl.BlockSpec((N_DEV, None, rows, cols), lambda *ids: (0, layer, 0, ids[col_block]))


def _spec_gate_up(axis):
    return pl.BlockSpec((None, GU_SHARD, D), lambda *ids: (ids[axis], 0, 0))


def _spec_down(axis):
    return pl.BlockSpec((2, DN_SHARD, D), lambda *ids: (ids[axis], 0, 0))


def _dep_specs(deps):
    return [pl.BlockSpec(memory_space=pl.ANY) for d in deps]


def _spec_gu_act(row_axis, axis, tm):
    return pl.BlockSpec((None, None, tm, GU_SHARD), lambda *ids: (ids[axis] // FF_BLOCKS, ids[axis] % FF_BLOCKS, ids[row_axis], 0))


def _norm_matmul(x, g, w, w_spec, n_blocks, out_shape, out_spec, name, deps=(), out_dtype=F32, w_t=False, tm=None):
    m, k = x.shape
    tm = tm or _rows(m)

    def body(x_ref, g_ref, w_ref, *rest):
        y_ref, hn_ref = rest[-2:]

        @pl.when(pl.program_id(1) == 0)
        def _():
            xv = x_ref[...]
            r = lax.rsqrt(jnp.mean(xv * xv, axis=-1, keepdims=True) + EPS)
            hn_ref[...] = (xv * r * g_ref[...]).astype(hn_ref.dtype)

        y_ref[...] = (_mm_nt if w_t else _mm)(hn_ref[...], _w2d(w_ref)).astype(y_ref.dtype)

    return pl.pallas_call(
        body, grid=(m // tm, n_blocks),
        in_specs=[pl.BlockSpec((tm, k), lambda i, j: (i, 0)), pl.BlockSpec((1, k), lambda i, j: (0, 0)), w_spec]
        + _dep_specs(deps),
        out_specs=[out_spec, pl.BlockSpec((tm, k), lambda i, j: (i, 0))],
        out_shape=[_SDS(out_shape, out_dtype), _SDS((m, k), _ACT)],
        name=name, compiler_params=_cp("arbitrary", "arbitrary"))(x, g, w, *deps)


def _matmul_res(a, a_spec, w, w_spec, n_k, res, name):
    m, n = res.shape
    tm = _rows(m)

    def body(a_ref, w_ref, r_ref, o_ref):
        part = _mm(a_ref[...], _w2d(w_ref))

        @pl.when(pl.program_id(1) == 0)
        def _():
            o_ref[...] = r_ref[...] + part

        @pl.when(pl.program_id(1) > 0)
        def _():
            o_ref[...] += part

    return pl.pallas_call(
        body, grid=(m // tm, n_k),
        in_specs=[a_spec, w_spec, pl.BlockSpec((tm, n), lambda i, j: (i, 0))],
        out_specs=pl.BlockSpec((tm, n), lambda i, j: (i, 0)),
        out_shape=_SDS((m, n), F32), name=name, compiler_params=_cp("arbitrary", "arbitrary"))(a, w, res)


def _matmul_nt(dy, w, w_spec, n_blocks, out_shape, out_spec, name, deps=(), out_dtype=F32):
    m, n = dy.shape
    tm = _rows(m)

    def body(dy_ref, w_ref, *rest):
        o_ref = rest[-1]
        o_ref[...] = _mm_nt(dy_ref[...], _w2d(w_ref)).astype(o_ref.dtype)

    return pl.pallas_call(
        body, grid=(m // tm, n_blocks),
        in_specs=[pl.BlockSpec((tm, n), lambda i, j: (i, 0)), w_spec] + _dep_specs(deps),
        out_specs=out_spec, out_shape=_SDS(out_shape, out_dtype),
        name=name, compiler_params=_cp("arbitrary", "arbitrary"))(dy, w, *deps)


def _matmul_nt_normbwd(dy, dy_spec, w, w_spec, nj, h, g, dh_in, name, w_t=False):
    m, k = h.shape
    tm = _rows(m)

    def body(dy_ref, w_ref, h_ref, g_ref, dhin_ref, dh_ref, dg_ref, acc_ref):
        i, j = pl.program_id(0), pl.program_id(1)

        @pl.when(j == 0)
        def _():
            acc_ref[...] = jnp.zeros_like(acc_ref)

        acc_ref[...] += (_mm if w_t else _mm_nt)(dy_ref[...], _w2d(w_ref))

        @pl.when(j == nj - 1)
        def _():
            xv = h_ref[...]
            r = lax.rsqrt(jnp.mean(xv * xv, axis=-1, keepdims=True) + EPS)
            xh = xv * r
            dhn = acc_ref[...]
            part = jnp.sum(dhn * xh, axis=0, keepdims=True)

            @pl.when(i == 0)
            def _():
                dg_ref[...] = part

            @pl.when(i > 0)
            def _():
                dg_ref[...] += part

            t = dhn * g_ref[...]
            dh_ref[...] = dhin_ref[...] + r * (t - xh * jnp.mean(t * xh, axis=-1, keepdims=True))

    return pl.pallas_call(
        body, grid=(m // tm, nj),
        in_specs=[dy_spec, w_spec, pl.BlockSpec((tm, k), lambda i, j: (i, 0)), pl.BlockSpec((1, k), lambda i, j: (0, 0)),
                  pl.BlockSpec((tm, k), lambda i, j: (i, 0))],
        out_specs=[pl.BlockSpec((tm, k), lambda i, j: (i, 0)), pl.BlockSpec((1, k), lambda i, j: (0, 0))],
        out_shape=[_SDS((m, k), F32), _SDS((1, k), F32)],
        scratch_shapes=[pltpu.VMEM((tm, k), F32)],
        name=name, compiler_params=_cp("arbitrary", "arbitrary"))(dy, w, h, g, dh_in)


def _matmul_tn(x, x_spec, dy, dy_spec, m, n_blocks, acc_shape, out_shape, out_spec, name, tm=None):
    tm = tm or _rows(m)
    nm = m // tm

    def body(x_ref, dy_ref, o_ref, acc_ref):
        @pl.when(pl.program_id(1) == 0)
        def _():
            acc_ref[...] = jnp.zeros_like(acc_ref)

        acc_ref[...] += _mm_tn(x_ref[...], dy_ref[...])

        @pl.when(pl.program_id(1) == nm - 1)
        def _():
            o_ref[...] = acc_ref[...].reshape(o_ref.shape).astype(o_ref.dtype)

    return pl.pallas_call(
        body, grid=(n_blocks, nm), in_specs=[x_spec, dy_spec], out_specs=out_spec,
        out_shape=_SDS(out_shape, _WIRE), scratch_shapes=[pltpu.VMEM(acc_shape, F32)],
        name=name, compiler_params=_cp("arbitrary", "arbitrary"))(x, dy)


def _loss_head(h, g, tgt, name):
    m, k = h.shape
    tm = _rows(m)

    def body(h_ref, g_ref, t_ref, loss_ref, dh_ref, dg_ref):
        i = pl.program_id(0)
        xv = h_ref[...]
        r = lax.rsqrt(jnp.mean(xv * xv, axis=-1, keepdims=True) + EPS)
        xh = xv * r
        gv = g_ref[...]
        err = xh * gv - t_ref[...]
        lpart = jnp.zeros((1, 128), F32) + 0.5 * jnp.sum(jnp.mean(err * err, axis=-1, keepdims=True), axis=0, keepdims=True)
        dy = err * (1.0 / k)
        gpart = jnp.sum(dy * xh, axis=0, keepdims=True)

        @pl.when(i == 0)
        def _():
            loss_ref[...] = lpart
            dg_ref[...] = gpart

        @pl.when(i > 0)
        def _():
            loss_ref[...] += lpart
            dg_ref[...] += gpart

        t = dy * gv
        dh_ref[...] = r * (t - xh * jnp.mean(t * xh, axis=-1, keepdims=True))

    return pl.pallas_call(
        body, grid=(m // tm,),
        in_specs=[pl.BlockSpec((tm, k), lambda i: (i, 0)), pl.BlockSpec((1, k), lambda i: (0, 0)),
                  pl.BlockSpec((tm, k), lambda i: (i, 0))],
        out_specs=[pl.BlockSpec((1, 128), lambda i: (0, 0)), pl.BlockSpec((tm, k), lambda i: (i, 0)),
                   pl.BlockSpec((1, k), lambda i: (0, 0))],
        out_shape=[_SDS((1, 128), F32), _SDS((m, k), F32), _SDS((1, k), F32)],
        name=name, compiler_params=_cp("arbitrary"))(h, g, tgt)


def _glu_down(gu, conv_w, conv_b, w_down, res, name):
    s = gu.shape[2]
    tm = min(s, _TM_GLU)

    def body(gu_ref, prev_ref, w_ref, b_ref, wdn_ref, r_ref, o_ref, act_ref):
        i, j = pl.program_id(0), pl.program_id(1)
        prev = jnp.where(i > 0, prev_ref[...].astype(F32), 0.0)
        ext = jnp.concatenate([prev, gu_ref[0].astype(F32)], axis=0)
        gc = b_ref[...] + w_ref[FFN_CONV - 1:FFN_CONV, :] * ext
        for k in range(FFN_CONV - 1):
            gc = gc + w_ref[k:k + 1, :] * pltpu.roll(ext, FFN_CONV - 1 - k, 0)
        act = (_silu(gc[GLU_HALO:]) * gu_ref[1].astype(F32)).astype(act_ref.dtype)
        act_ref[...] = act
        part = _mm(act, _w2d(wdn_ref))

        @pl.when(j == 0)
        def _():
            o_ref[...] = r_ref[...] + part

        @pl.when(j > 0)
        def _():
            o_ref[...] += part

    return pl.pallas_call(
        body, grid=(s // tm, FF_BLOCKS),
        in_specs=[pl.BlockSpec((2, None, tm, GU_SHARD), lambda i, j: (0, j, i, 0)),
                  pl.BlockSpec((None, None, GLU_HALO, GU_SHARD),
                               lambda i, j: (0, j, jnp.maximum(i * (tm // GLU_HALO) - 1, 0), 0)),
                  pl.BlockSpec((None, HALO, GU_SHARD), lambda i, j: (j, 0, 0)),
                  pl.BlockSpec((None, 1, GU_SHARD), lambda i, j: (j, 0, 0)),
                  _spec_down(1), pl.BlockSpec((tm, D), lambda i, j: (i, 0))],
        out_specs=[pl.BlockSpec((tm, D), lambda i, j: (i, 0)), pl.BlockSpec((None, tm, GU_SHARD), lambda i, j: (j, i, 0))],
        out_shape=[_SDS((s, D), F32), _SDS((FF_BLOCKS, s, GU_SHARD), _ACT)], name=name,
        compiler_params=_cp("arbitrary", "arbitrary"))(gu, gu, conv_w, conv_b, w_down, res)


def _glu_bwd(gu, conv_w, conv_b, dh, w_down, name, deps=()):
    s = gu.shape[2]
    tm = min(s, _TM_GLU)
    nt = s // tm
    ext_rows = tm + GLU_HALO

    def body(gu_ref, prev_ref, w_ref, b_ref, dh_ref, wdn_ref, *rest):
        dgu_ref, dw_ref, db_ref, carry_ref = rest[-4:]
        t = pl.program_id(1)
        i = nt - 1 - t

        @pl.when(t == 0)
        def _():
            carry_ref[...] = jnp.zeros_like(carry_ref)
            dw_ref[...] = jnp.zeros_like(dw_ref)
            db_ref[...] = jnp.zeros_like(db_ref)

        up = gu_ref[1].astype(F32)
        prev = jnp.where(i > 0, prev_ref[...].astype(F32), 0.0)
        ext = jnp.concatenate([prev, gu_ref[0].astype(F32)], axis=0)
        shifted = [pltpu.roll(ext, FFN_CONV - 1 - j, 0) if j < FFN_CONV - 1 else ext for j in range(FFN_CONV)]
        gc = b_ref[...] + shifted[0] * w_ref[0:1, :]
        for j in range(1, FFN_CONV):
            gc = gc + shifted[j] * w_ref[j:j + 1, :]
        gc = gc[GLU_HALO:]
        sg = jax.nn.sigmoid(gc)
        da = _mm_nt(dh_ref[...], _w2d(wdn_ref))
        dup = da * (gc * sg)
        dgc = da * up * (sg * (1.0 + gc * (1.0 - sg)))
        db_ref[...] += jnp.sum(dgc, axis=0, keepdims=True)
        dgc_ext = jnp.concatenate([jnp.zeros((GLU_HALO, GU_SHARD), F32), dgc], axis=0)
        dext = dgc_ext * w_ref[FFN_CONV - 1:FFN_CONV, :]
        for j in range(FFN_CONV):
            dw_ref[j:j + 1, :] += jnp.sum(shifted[j] * dgc_ext, axis=0, keepdims=True)
            if j < FFN_CONV - 1:
                dext = dext + w_ref[j:j + 1, :] * pltpu.roll(dgc_ext, ext_rows - (FFN_CONV - 1 - j), 0)
        tail = jnp.concatenate([jnp.zeros((tm - GLU_HALO, GU_SHARD), F32), carry_ref[...]], axis=0)
        dgate = dext[GLU_HALO:] + tail
        carry_ref[...] = dext[:GLU_HALO]
        dgu_ref[0] = dgate.astype(dgu_ref.dtype)
        dgu_ref[1] = dup.astype(dgu_ref.dtype)

    return pl.pallas_call(
        body, grid=(FF_BLOCKS, nt),
        in_specs=[pl.BlockSpec((2, None, tm, GU_SHARD), lambda j, t: (0, j, nt - 1 - t, 0)),
                  pl.BlockSpec((None, None, GLU_HALO, GU_SHARD),
                               lambda j, t: (0, j, jnp.maximum((nt - 1 - t) * (tm // GLU_HALO) - 1, 0), 0)),
                  pl.BlockSpec((None, HALO, GU_SHARD), lambda j, t: (j, 0, 0)),
                  pl.BlockSpec((None, 1, GU_SHARD), lambda j, t: (j, 0, 0)),
                  pl.BlockSpec((tm, D), lambda j, t: (nt - 1 - t, 0)), _spec_down(0)] + _dep_specs(deps),
        out_specs=[pl.BlockSpec((2, None, tm, GU_SHARD), lambda j, t: (0, j, nt - 1 - t, 0)),
                   pl.BlockSpec((None, HALO, GU_SHARD), lambda j, t: (j, 0, 0)),
                   pl.BlockSpec((None, 1, GU_SHARD), lambda j, t: (j, 0, 0))],
        out_shape=[_SDS(gu.shape, _ACT), _SDS((FF_BLOCKS, HALO, GU_SHARD), F32), _SDS((FF_BLOCKS, 1, GU_SHARD), F32)],
        scratch_shapes=[pltpu.VMEM((GLU_HALO, GU_SHARD), F32)],
        name=name, compiler_params=_cp("arbitrary", "arbitrary"))(gu, gu, conv_w, conv_b, dh, w_down, *deps)


def _bucket_table():
    qi = np.arange(BLK)[:, None]
    kj = np.arange(BLK)[None, :]
    n = np.where(kj > qi, BLK + qi - kj, qi - kj)
    max_exact = N_BUCKETS // 2
    nf = np.maximum(n, 1).astype(np.float32)
    large = max_exact + (np.log(nf / max_exact) / math.log(MAX_DISTANCE / max_exact)
                         * (N_BUCKETS - max_exact)).astype(np.int32)
    large = np.minimum(large, N_BUCKETS - 1)
    return np.where(n < max_exact, n, large).astype(np.int32)


def _lane_low():
    return lax.broadcasted_iota(jnp.int32, (1, 128), 1) < A_HD


def _swa_groups(q, kd, vd, sink, bias, upper, first):
    n = A_HEADS // A_KV_HEADS
    ng = A_KV_HEADS
    low = _lane_low()
    qm = [jnp.concatenate([jnp.where(low == (h % 2 == 0), q[g][:, (h // 2) * 128:(h // 2 + 1) * 128], 0.0) for h in range(n)], axis=0)
          for g in range(ng)]
    s2 = [_mm_nt(qm[g], kd[g]) * (A_HD ** -0.5) for g in range(ng)]
    s = [jnp.where(upper[None], s2[g][:, :BLK].reshape(n, BLK, BLK), s2[g][:, BLK:].reshape(n, BLK, BLK)) + bias[g] for g in range(ng)]
    s = [jnp.where((upper & first)[None], -jnp.inf, t) for t in s]
    m = [jnp.maximum(jnp.max(s[g], axis=-1, keepdims=True), sink[g]) for g in range(ng)]
    p = [jnp.exp(s[g] - m[g]) for g in range(ng)]
    split = [jnp.concatenate([jnp.where(upper[None], t, 0.0), jnp.where(upper[None], 0.0, t)], axis=-1).reshape(n * BLK, 2 * BLK)
             for t in p]
    ones = jnp.ones((BLK, 128), F32)
    den = [_mm(p[g].reshape(n * BLK, BLK), ones) + jnp.exp(sink[g] - m[g]).reshape(n * BLK, 1) for g in range(ng)]
    o = [_mm(split[g], vd[g]) / den[g] for g in range(ng)]
    return [jnp.concatenate([jnp.where(low, t[2 * k * BLK:(2 * k + 1) * BLK], t[(2 * k + 1) * BLK:(2 * k + 2) * BLK])
                             for k in range(n // 2)], axis=1) for t in o]


def _mix_a_core(q, kd, vd, sink, bias, xq, mk, mv, upper, first):
    return _swa_groups(q, kd, vd, sink, bias, upper, first), _cross_pairs(xq, mk, mv)


def _swa_sinks(sink_ref, g):
    n = A_HEADS // A_KV_HEADS
    return jnp.concatenate([sink_ref[:, h:h + 1] for h in range(g * n, (g + 1) * n)], axis=0).reshape(n, 1, 1)


def _both_halves(t, t_rolled, g):
    low = _lane_low()
    return jnp.where(low, t, t_rolled) if g == 0 else jnp.where(low, t_rolled, t)


def _cross_pairs(q, mk, mv):
    rows = q.shape[0]
    low = _lane_low()
    qm = [jnp.concatenate([jnp.where(low, q[:, p * 128:(p + 1) * 128], 0.0), jnp.where(low, 0.0, q[:, p * 128:(p + 1) * 128])], axis=0)
          for p in range(X_HEADS // 2)]
    s = [_mm_nt(qm[p], mk[:, p * 128:(p + 1) * 128]) * (X_HD ** -0.5) for p in range(X_HEADS // 2)]
    e = [jnp.exp(t - jnp.max(t, axis=-1, keepdims=True)) for t in s]
    pr = [t / jnp.sum(t, axis=-1, keepdims=True) for t in e]
    o = [_mm(pr[p], mv[:, p * 128:(p + 1) * 128]) for p in range(X_HEADS // 2)]
    return jnp.concatenate([jnp.where(low, t[:rows], t[rows:]) for t in o], axis=1)


def _swa_upper():
    qi = lax.broadcasted_iota(jnp.int32, (BLK, BLK), 0)
    kj = lax.broadcasted_iota(jnp.int32, (BLK, BLK), 1)
    return kj > qi


def _bias_build(rel_bias, bucket, name):
    def body(rb_ref, bucket_ref, o_ref):
        b = bucket_ref[...]
        for h in range(A_HEADS):
            acc = jnp.zeros((BLK, BLK), F32)
            for k in range(N_BUCKETS):
                acc = jnp.where(b == k, rb_ref[k, h], acc)
            o_ref[h] = acc

    return pl.pallas_call(
        body, in_specs=[pl.BlockSpec(memory_space=pltpu.SMEM), pl.BlockSpec(memory_space=pltpu.VMEM)],
        out_specs=pl.BlockSpec(memory_space=pltpu.VMEM),
        out_shape=_SDS((A_HEADS, BLK, BLK), F32), name=name)(rel_bias, bucket)


def _bias_reduce(dbias, bucket, name):
    def body(db_ref, bucket_ref, o_ref):
        b = bucket_ref[...]
        row = lax.broadcasted_iota(jnp.int32, (N_BUCKETS, 128), 0)
        lane = lax.broadcasted_iota(jnp.int32, (N_BUCKETS, 128), 1)
        acc = jnp.zeros((N_BUCKETS, 128), F32)
        for h in range(A_HEADS):
            v = db_ref[h]
            for k in range(N_BUCKETS):
                sk = jnp.sum(jnp.sum(jnp.where(b == k, v, 0.0), axis=1, keepdims=True), axis=0, keepdims=True)
                acc = acc + jnp.where((row == k) & (lane == h), sk, 0.0)
        o_ref[...] = acc

    return pl.pallas_call(
        body, in_specs=[pl.BlockSpec(memory_space=pltpu.VMEM)] * 2,
        out_specs=pl.BlockSpec(memory_space=pltpu.VMEM),
        out_shape=_SDS((N_BUCKETS, 128), F32), name=name)(dbias, bucket)


def _mix_a_fwd(proj, bias, sinks, memkv, name):
    s = proj.shape[0]
    nb = s // BLK
    grp = A_HEADS // A_KV_HEADS

    def body(proj_ref, prev_ref, bias_ref, sink_ref, memkv_ref, o_ref):
        i = pl.program_id(0)
        upper = _swa_upper()
        prev = prev_ref[...].astype(F32)
        proj = proj_ref[...].astype(F32)
        kb = jnp.concatenate([prev[:, :A_KV], proj[:, A_Q:A_Q + A_KV]], axis=0)
        vb = jnp.concatenate([prev[:, A_KV:], proj[:, A_Q + A_KV:A_Q + 2 * A_KV]], axis=0)
        kb_r = pltpu.roll(kb, A_HD, 1)
        vb_r = pltpu.roll(vb, A_HD, 1)
        gw = A_Q // A_KV_HEADS
        groups = range(A_KV_HEADS)
        swa, cross = _mix_a_core([proj[:, g * gw:(g + 1) * gw] for g in groups], [_both_halves(kb, kb_r, g) for g in groups],
                                 [_both_halves(vb, vb_r, g) for g in groups], [_swa_sinks(sink_ref, g) for g in groups],
                                 [bias_ref[g * grp:(g + 1) * grp] for g in groups], proj[:, A_Q + 2 * A_KV:],
                                 memkv_ref[:, :X_Q], memkv_ref[:, X_Q:], upper, i == 0)
        o_ref[...] = jnp.concatenate(swa + [cross], axis=1).astype(o_ref.dtype)

    return pl.pallas_call(
        body, grid=(nb,),
        in_specs=[pl.BlockSpec((BLK, IN_A), lambda i: (i, 0)),
                  pl.BlockSpec((BLK, 2 * A_KV), lambda i: (jnp.maximum(i - 1, 0), A_Q // (2 * A_KV))),
                  pl.BlockSpec((A_HEADS, BLK, BLK), lambda i: (0, 0, 0)),
                  pl.BlockSpec((1, 128), lambda i: (0, 0)),
                  pl.BlockSpec((MEM_LEN, 2 * X_Q), lambda i: (0, 0))],
        out_specs=pl.BlockSpec((BLK, D), lambda i: (i, 0)),
        out_shape=_SDS((s, D), _ACT), name=name, compiler_params=_cp("arbitrary"))(proj, proj, bias, sinks, memkv)


def _mix_a_bwd(proj, bias, sinks, memkv, dmix, name):
    s = proj.shape[0]
    nb = s // BLK
    grp = A_HEADS // A_KV_HEADS

    def body(proj_ref, prev_ref, bias_ref, sink_ref, memkv_ref, dmix_ref,
             dproj_ref, dbias_ref, dsink_ref, dmemkv_ref, carry_ref):
        t = pl.program_id(0)
        i = nb - 1 - t

        @pl.when(t == 0)
        def _():
            carry_ref[...] = jnp.zeros_like(carry_ref)
            dbias_ref[...] = jnp.zeros_like(dbias_ref)
            dsink_ref[...] = jnp.zeros_like(dsink_ref)
            dmemkv_ref[...] = jnp.zeros_like(dmemkv_ref)

        upper = _swa_upper()
        lane = lax.broadcasted_iota(jnp.int32, (1, 128), 1)
        low = _lane_low()
        prev = prev_ref[...].astype(F32)
        proj = proj_ref[...].astype(F32)
        kb = jnp.concatenate([prev[:, :A_KV], proj[:, A_Q:A_Q + A_KV]], axis=0)
        vb = jnp.concatenate([prev[:, A_KV:], proj[:, A_Q + A_KV:A_Q + 2 * A_KV]], axis=0)
        kb_r = pltpu.roll(kb, A_HD, 1)
        vb_r = pltpu.roll(vb, A_HD, 1)
        gw = A_Q // A_KV_HEADS
        groups = range(A_KV_HEADS)
        _, vjp = jax.vjp(
            functools.partial(_mix_a_core, upper=upper, first=i == 0),
            [proj[:, g * gw:(g + 1) * gw] for g in groups], [_both_halves(kb, kb_r, g) for g in groups],
            [_both_halves(vb, vb_r, g) for g in groups], [_swa_sinks(sink_ref, g) for g in groups],
            [bias_ref[g * grp:(g + 1) * grp] for g in groups], proj[:, A_Q + 2 * A_KV:], memkv_ref[:, :X_Q], memkv_ref[:, X_Q:])
        dqs, dk, dv, ds, db, dxq, dmk, dmv = vjp(
            ([dmix_ref[:, g * gw:(g + 1) * gw].astype(F32) for g in groups], dmix_ref[:, A_Q:].astype(F32)))
        dkd = [t + pltpu.roll(t, A_HD, 1) for t in dk]
        dvd = [t + pltpu.roll(t, A_HD, 1) for t in dv]
        dsink = jnp.zeros((1, 128), F32)
        for g in groups:
            for h in range(grp):
                dsink = dsink + jnp.where(lane == g * grp + h, ds[g][h], 0.0)
            dbias_ref[g * grp:(g + 1) * grp] += db[g]
        dsink_ref[...] += dsink
        dkb = jnp.where(low, dkd[0], dkd[1])
        dvb = jnp.where(low, dvd[0], dvd[1])
        dmemkv_ref[...] += jnp.concatenate([dmk, dmv], axis=1)
        dkv_cur = jnp.concatenate([dkb[BLK:], dvb[BLK:]], axis=1) + carry_ref[...]
        carry_ref[...] = jnp.concatenate([dkb[:BLK], dvb[:BLK]], axis=1)
        dproj_ref[...] = jnp.concatenate(list(dqs) + [dkv_cur, dxq], axis=1).astype(dproj_ref.dtype)

    return pl.pallas_call(
        body, grid=(nb,),
        in_specs=[pl.BlockSpec((BLK, IN_A), lambda t: (nb - 1 - t, 0)),
                  pl.BlockSpec((BLK, 2 * A_KV), lambda t: (jnp.maximum(nb - 2 - t, 0), A_Q // (2 * A_KV))),
                  pl.BlockSpec((A_HEADS, BLK, BLK), lambda t: (0, 0, 0)),
                  pl.BlockSpec((1, 128), lambda t: (0, 0)),
                  pl.BlockSpec((MEM_LEN, 2 * X_Q), lambda t: (0, 0)),
                  pl.BlockSpec((BLK, D), lambda t: (nb - 1 - t, 0))],
        out_specs=[pl.BlockSpec((BLK, IN_A), lambda t: (nb - 1 - t, 0)),
                   pl.BlockSpec((A_HEADS, BLK, BLK), lambda t: (0, 0, 0)),
                   pl.BlockSpec((1, 128), lambda t: (0, 0)),
                   pl.BlockSpec((MEM_LEN, 2 * X_Q), lambda t: (0, 0))],
        out_shape=[_SDS((s, IN_A), _ACT), _SDS((A_HEADS, BLK, BLK), F32), _SDS((1, 128), F32),
                   _SDS((MEM_LEN, 2 * X_Q), F32)],
        scratch_shapes=[pltpu.VMEM((BLK, 2 * A_KV), F32)],
        name=name, compiler_params=_cp("arbitrary"))(proj, proj, bias, sinks, memkv, dmix)


def _dn_heads(yq, yk, yv, z, bl, al, a_log, dtb, ng, s0):
    c = CHUNK
    nh = B_V_HEADS
    rep = B_V_HEADS // B_QK_HEADS
    r = lax.broadcasted_iota(jnp.int32, (c, c), 0)
    cc = lax.broadcasted_iota(jnp.int32, (c, c), 1)
    q = [_silu(t) for t in yq]
    k = [_silu(t) for t in yk]
    v = [_silu(t) for t in yv]
    q = [t * lax.rsqrt(jnp.sum(t * t, axis=-1, keepdims=True) + EPS) * (B_HD ** -0.5) for t in q]
    k = [t * lax.rsqrt(jnp.sum(t * t, axis=-1, keepdims=True) + EPS) for t in k]
    beta = [jax.nn.sigmoid(t) for t in bl]
    g = [-jnp.exp(a_log[h]) * jax.nn.softplus(al[h] + dtb[h]) for h in range(nh)]
    gb = [jnp.broadcast_to(t, (c, c)) for t in g]
    gc_col = [jnp.sum(jnp.where(cc <= r, t.T, 0.0), axis=1, keepdims=True) for t in gb]
    gc_row = [jnp.sum(jnp.where(r <= cc, t, 0.0), axis=0, keepdims=True) for t in gb]
    gc_last = [jnp.sum(t, axis=0, keepdims=True) for t in g]
    decay = [jnp.exp(jnp.where(r >= cc, gc_col[h] - gc_row[h], -jnp.inf)) for h in range(nh)]
    kq = [_mmf_nt(jnp.concatenate([k[h], q[h]], axis=0), k[h]) for h in range(B_QK_HEADS)]
    kk = [t[:c] for t in kq]
    qk = [t[c:] for t in kq]
    egc = [jnp.exp(t) for t in gc_col]
    both = [_mmf(jnp.concatenate([(beta[h] * egc[h]) * k[h // rep], q[h // rep] * egc[h]], axis=0), s0[h]) for h in range(nh)]
    rhs = [beta[h] * v[h] - both[h][:c] for h in range(nh)]
    qs0 = [t[c:] for t in both]
    pw = [-(beta[h] * kk[h // rep] * jnp.where(r > cc, decay[h], 0.0)) for h in range(nh)]
    x = rhs
    for lvl in range(6):
        if lvl < 5:
            prod = [_mmf(pw[h], jnp.concatenate([x[h], pw[h]], axis=1)) for h in range(nh)]
            x = [x[h] + prod[h][:, :B_HD] for h in range(nh)]
            pw = [t[:, B_HD:] for t in prod]
        else:
            x = [x[h] + _mmf(pw[h], x[h]) for h in range(nh)]
    delta = x
    last = [_mmf(jnp.concatenate([qk[h // rep] * decay[h], (k[h // rep] * jnp.exp(gc_last[h] - gc_col[h])).T], axis=0), delta[h])
            for h in range(nh)]
    out = [qs0[h] + last[h][:c] for h in range(nh)]
    s1 = [jnp.exp(gc_last[h]) * s0[h] + last[h][c:] for h in range(nh)]
    o = [t * lax.rsqrt(jnp.mean(t * t, axis=-1, keepdims=True) + EPS) * ng for t in out]
    return [o[h] * _silu(z[h]) for h in range(nh)], s1


def _dn_conv(ext, w_ref):
    y = ext * w_ref[B_CONV - 1:B_CONV, :]
    for j in range(B_CONV - 1):
        y = y + w_ref[j:j + 1, :] * pltpu.roll(ext, B_CONV - 1 - j, 0)
    return y


def _dn_args(y, cur_ref, par_ref, ng_ref):
    nh = B_V_HEADS
    return ([y[:, h * B_HD:(h + 1) * B_HD] for h in range(B_QK_HEADS)],
            [y[:, B_QK + h * B_HD:B_QK + (h + 1) * B_HD] for h in range(B_QK_HEADS)],
            [y[:, 2 * B_QK + h * B_HD:2 * B_QK + (h + 1) * B_HD] for h in range(nh)],
            [cur_ref[:, BP_Z + h * B_HD:BP_Z + (h + 1) * B_HD] for h in range(nh)],
            [cur_ref[:, BP_GATE + h:BP_GATE + h + 1] for h in range(nh)],
            [cur_ref[:, BP_GATE + nh + h:BP_GATE + nh + h + 1] for h in range(nh)],
            [par_ref[:, h:h + 1] for h in range(nh)], [par_ref[:, nh + h:nh + h + 1] for h in range(nh)], ng_ref[...])


def _mix_b_fwd(proj, conv_w, par, ng, memkv, name):
    s = proj.shape[0]
    nc = s // CHUNK

    def body(cur_ref, prev_ref, w_ref, par_ref, ng_ref, memkv_ref, o_ref, st_ref, state_ref):
        n = pl.program_id(0)

        @pl.when(n == 0)
        def _():
            state_ref[...] = jnp.zeros_like(state_ref)

        prev = jnp.where(n > 0, prev_ref[...], 0.0)
        ext = jnp.concatenate([prev, cur_ref[:, :B_QKV]], axis=0)
        y = _dn_conv(ext, w_ref)[HALO:]
        s0 = [state_ref[hv] for hv in range(B_V_HEADS)]
        st_ref[0] = state_ref[...]
        outs, s1 = _dn_heads(*_dn_args(y, cur_ref, par_ref, ng_ref), s0)
        for hv in range(B_V_HEADS):
            state_ref[hv] = s1[hv]
        outs = outs + [_cross_pairs(cur_ref[:, BP_XQ:BP_XQ + X_Q], memkv_ref[:, :X_Q], memkv_ref[:, X_Q:])]
        o_ref[...] = jnp.concatenate(outs, axis=1).astype(o_ref.dtype)

    return pl.pallas_call(
        body, grid=(nc,),
        in_specs=[pl.BlockSpec((CHUNK, IN_BP), lambda n: (n, 0)),
                  pl.BlockSpec((HALO, B_QKV), lambda n: (jnp.maximum(n * (CHUNK // HALO) - 1, 0), 0)),
                  pl.BlockSpec((HALO, B_QKV), lambda n: (0, 0)),
                  pl.BlockSpec((1, 128), lambda n: (0, 0)), pl.BlockSpec((1, 128), lambda n: (0, 0)),
                  pl.BlockSpec((MEM_LEN, 2 * X_Q), lambda n: (0, 0))],
        out_specs=[pl.BlockSpec((CHUNK, D), lambda n: (n, 0)),
                   pl.BlockSpec((1, B_V_HEADS, B_HD, B_HD), lambda n: (n, 0, 0, 0))],
        out_shape=[_SDS((s, D), _ACT), _SDS((nc, B_V_HEADS, B_HD, B_HD), F32)],
        scratch_shapes=[pltpu.VMEM((B_V_HEADS, B_HD, B_HD), F32)],
        name=name, compiler_params=_cp("arbitrary"))(proj, proj, conv_w, par, ng, memkv)


def _mix_b_bwd(proj, conv_w, par, ng, memkv, states, dmix, name):
    s = proj.shape[0]
    nc = s // CHUNK
    ext_rows = CHUNK + HALO

    def body(cur_ref, prev_ref, w_ref, par_ref, ng_ref, memkv_ref, st_ref, dmix_ref,
             dproj_ref, dw_ref, dpar_ref, dng_ref, dmemkv_ref, dstate_ref, carry_ref):
        t = pl.program_id(0)
        n = nc - 1 - t

        @pl.when(t == 0)
        def _():
            dstate_ref[...] = jnp.zeros_like(dstate_ref)
            carry_ref[...] = jnp.zeros_like(carry_ref)
            dw_ref[...] = jnp.zeros_like(dw_ref)
            dpar_ref[...] = jnp.zeros_like(dpar_ref)
            dng_ref[...] = jnp.zeros_like(dng_ref)
            dmemkv_ref[...] = jnp.zeros_like(dmemkv_ref)

        lane = lax.broadcasted_iota(jnp.int32, (1, 128), 1)
        prev = jnp.where(n > 0, prev_ref[...], 0.0)
        ext = jnp.concatenate([prev, cur_ref[:, :B_QKV]], axis=0)
        y = _dn_conv(ext, w_ref)[HALO:]
        _, vjp = jax.vjp(_dn_heads, *_dn_args(y, cur_ref, par_ref, ng_ref), [st_ref[0, hv] for hv in range(B_V_HEADS)])
        dyq, dyk, dyv, dz, gbl, gal, ga_log, gdtb, dng, gs0 = vjp(
            ([dmix_ref[:, hv * B_HD:(hv + 1) * B_HD].astype(F32) for hv in range(B_V_HEADS)],
             [dstate_ref[hv] for hv in range(B_V_HEADS)]))
        dgate = jnp.zeros((CHUNK, 128), F32)
        dpar = jnp.zeros((1, 128), F32)
        for hv in range(B_V_HEADS):
            dstate_ref[hv] = gs0[hv]
            dgate = dgate + jnp.where(lane == hv, gbl[hv], 0.0) + jnp.where(lane == B_V_HEADS + hv, gal[hv], 0.0)
            dpar = dpar + jnp.where(lane == hv, ga_log[hv], 0.0) + jnp.where(lane == B_V_HEADS + hv, gdtb[hv], 0.0)
        dpar_ref[...] += dpar
        dng_ref[...] += dng
        _, vjp = jax.vjp(_cross_pairs, cur_ref[:, BP_XQ:BP_XQ + X_Q], memkv_ref[:, :X_Q], memkv_ref[:, X_Q:])
        dxq, dmk, dmv = vjp(dmix_ref[:, B_V:].astype(F32))
        dmemkv_ref[...] += jnp.concatenate([dmk, dmv], axis=1)
        dy = jnp.concatenate(list(dyq) + list(dyk) + list(dyv), axis=1)
        dy_ext = jnp.concatenate([jnp.zeros((HALO, B_QKV), F32), dy], axis=0)
        dext = dy_ext * w_ref[B_CONV - 1:B_CONV, :]
        dw_ref[B_CONV - 1:B_CONV, :] += jnp.sum(ext * dy_ext, axis=0, keepdims=True)
        for j in range(B_CONV - 1):
            sh = B_CONV - 1 - j
            dw_ref[j:j + 1, :] += jnp.sum(pltpu.roll(ext, sh, 0) * dy_ext, axis=0, keepdims=True)
            dext = dext + w_ref[j:j + 1, :] * pltpu.roll(dy_ext, ext_rows - sh, 0)
        tail = jnp.concatenate([jnp.zeros((CHUNK - HALO, B_QKV), F32), carry_ref[...]], axis=0)
        dqkv = dext[HALO:] + tail
        carry_ref[...] = dext[:HALO]
        dproj_ref[...] = jnp.concatenate([dqkv] + list(dz) + [dxq, dgate], axis=1).astype(dproj_ref.dtype)

    return pl.pallas_call(
        body, grid=(nc,),
        in_specs=[pl.BlockSpec((CHUNK, IN_BP), lambda t: (nc - 1 - t, 0)),
                  pl.BlockSpec((HALO, B_QKV), lambda t: (jnp.maximum((nc - 1 - t) * (CHUNK // HALO) - 1, 0), 0)),
                  pl.BlockSpec((HALO, B_QKV), lambda t: (0, 0)),
                  pl.BlockSpec((1, 128), lambda t: (0, 0)), pl.BlockSpec((1, 128), lambda t: (0, 0)),
                  pl.BlockSpec((MEM_LEN, 2 * X_Q), lambda t: (0, 0)),
                  pl.BlockSpec((1, B_V_HEADS, B_HD, B_HD), lambda t: (nc - 1 - t, 0, 0, 0)),
                  pl.BlockSpec((CHUNK, D), lambda t: (nc - 1 - t, 0))],
        out_specs=[pl.BlockSpec((CHUNK, IN_BP), lambda t: (nc - 1 - t, 0)),
                   pl.BlockSpec((HALO, B_QKV), lambda t: (0, 0)),
                   pl.BlockSpec((1, 128), lambda t: (0, 0)), pl.BlockSpec((1, 128), lambda t: (0, 0)),
                   pl.BlockSpec((MEM_LEN, 2 * X_Q), lambda t: (0, 0))],
        out_shape=[_SDS((s, IN_BP), _ACT), _SDS((HALO, B_QKV), F32), _SDS((1, 128), F32), _SDS((1, 128), F32),
                   _SDS((MEM_LEN, 2 * X_Q), F32)],
        scratch_shapes=[pltpu.VMEM((B_V_HEADS, B_HD, B_HD), F32), pltpu.VMEM((HALO, B_QKV), F32)],
        name=name, compiler_params=_cp("arbitrary"))(proj, proj, conv_w, par, ng, memkv, states, dmix)


def _place():
    return lax.axis_index("x"), lax.axis_index("y"), lax.axis_index("c")


def _all_gather(shards, name):
    n = len(shards)

    def body(*refs):
        ins, outs = refs[:n], refs[n:2 * n]
        send_sems, recv_sems, local_sems = refs[2 * n:]
        x, y, c = _place()
        me, sibling = (x, y, c), (x, y, 1 - c)
        chips = [(1 - x, y), (x, 1 - y), (1 - x, 1 - y)]

        def rows(a, px, py, pc):
            return outs[a].at[4 * px + 2 * py + pc]

        def copy(a, k, block, to, src=None):
            return pltpu.make_async_remote_copy(
                src_ref=rows(a, *block) if src is None else src, dst_ref=rows(a, *block),
                send_sem=send_sems.at[a, k], recv_sem=recv_sems.at[a, k],
                device_id=to, device_id_type=pl.DeviceIdType.MESH)

        mine = [pltpu.make_async_copy(ins[a], rows(a, *me), local_sems.at[a]) for a in range(n)]
        for cp in mine:
            cp.start()
        first = []
        for a in range(n):
            first.append(copy(a, 0, me, sibling, src=ins[a]))
            first += [copy(a, 1 + j, me, (*chip, c), src=ins[a]) for j, chip in enumerate(chips)]
        for cp in first:
            cp.start()
        passed = []
        for j, chip in enumerate(chips):
            for a in range(n):
                copy(a, 1 + j, (*chip, c), me).wait_recv()
                fwd = copy(a, 4 + j, (*chip, c), sibling)
                fwd.start()
                passed.append(fwd)
        for a in range(n):
            copy(a, 0, sibling, me).wait_recv()
            for j, chip in enumerate(chips):
                copy(a, 4 + j, (*chip, 1 - c), me).wait_recv()
        for cp in first + passed:
            cp.wait_send()
        for cp in mine:
            cp.wait()

    hbm = pl.BlockSpec(memory_space=pl.ANY)
    return pl.pallas_call(
        body, out_shape=[_SDS((N_DEV,) + s.shape, s.dtype) for s in shards],
        in_specs=[hbm] * n, out_specs=[hbm] * n,
        scratch_shapes=[pltpu.SemaphoreType.DMA((n, 7)), pltpu.SemaphoreType.DMA((n, 7)), pltpu.SemaphoreType.DMA((n,))],
        name=name)(*shards)


class _Exchange:
    def __init__(self, lands, srcs):
        self.lands, self.srcs = lands, srcs


def _seq_exchange(srcs, land_shapes, plan, name, cid):
    n, nl = len(srcs), len(land_shapes)

    def launch(*refs):
        src_refs, land_refs = refs[:n], refs[n:n + nl]
        send_sems, recv_sems, local_sems = refs[n + nl:]
        x, y, c = _place()
        my = 4 * x + 2 * y + c
        peers = [(x ^ ((k + 1) >> 2 & 1), y ^ ((k + 1) >> 1 & 1), c ^ ((k + 1) & 1)) for k in range(N_DEV - 1)]
        barrier = pltpu.get_barrier_semaphore()
        for p in peers:
            pl.semaphore_signal(barrier, inc=1, device_id=p, device_id_type=pl.DeviceIdType.MESH)
        pl.semaphore_wait(barrier, N_DEV - 1)

        def src_for(a, dest):
            return src_refs[a].at[dest] if plan[a][1] else src_refs[a]

        def slot(a, source):
            return land_refs[plan[a][0]].at[source]

        mine = [pltpu.make_async_copy(src_for(a, my), slot(a, my), local_sems.at[a]) for a in range(n)]
        for cp in mine:
            cp.start()
        sends, recvs = [], []
        for k, (px, py, pc) in enumerate(peers):
            peer = 4 * px + 2 * py + pc
            for a in range(n):
                kw = dict(send_sem=send_sems.at[a * (N_DEV - 1) + k], recv_sem=recv_sems.at[a * (N_DEV - 1) + k],
                          device_id=(px, py, pc), device_id_type=pl.DeviceIdType.MESH)
                sends.append(pltpu.make_async_remote_copy(src_ref=src_for(a, peer), dst_ref=slot(a, my), **kw))
                recvs.append(pltpu.make_async_remote_copy(src_ref=src_for(a, my), dst_ref=slot(a, peer), **kw))
        for cp in sends:
            cp.start()
        for cp in recvs:
            cp.wait_recv()
        for cp in sends:
            cp.wait_send()
        for cp in mine:
            cp.wait()

    lands = pl.kernel(
        launch, out_type=[_SDS(s, d) for s, d in land_shapes],
        mesh=plsc.ScalarSubcoreMesh(axis_name="sequencer", num_cores=1), name=name,
        scratch_types=(pltpu.SemaphoreType.DMA((n * (N_DEV - 1),)), pltpu.SemaphoreType.DMA((n * (N_DEV - 1),)),
                       pltpu.SemaphoreType.DMA((n,))),
        compiler_params=pltpu.CompilerParams(collective_id=cid))(*srcs)
    return _Exchange(list(lands), list(srcs))


def _adam_update(g, w, m, v):
    c1 = 1.0 - ADAM_B1 ** ADAM_STEP
    c2 = 1.0 - ADAM_B2 ** ADAM_STEP
    mm = ADAM_B1 * m + (1.0 - ADAM_B1) * g
    vv = ADAM_B2 * v + (1.0 - ADAM_B2) * (g * g)
    delta = -ADAM_LR * ((mm / c1) / (jnp.sqrt(vv / c2) + ADAM_EPS) + ADAM_WD * w)
    return delta, mm, vv


def _sum_sources(p_ref):
    g = p_ref[0].astype(F32)
    for s in range(1, N_DEV):
        g = g + p_ref[s].astype(F32)
    return g


def _adamw(parts, w, m, v, tr, name, restore_b=False, deps=()):
    nl, r, c = w.shape
    cp = parts[0].shape[-1]

    def body(*refs):
        p_refs = refs[:nl]
        w_ref, m_ref, v_ref = refs[nl:nl + 3]
        g_ref, d_ref, nm_ref, nv_ref = refs[-4:]
        g = _sum_sources(p_refs[0])
        for l in range(1, nl):
            g = jnp.where(pl.program_id(0) == l, _sum_sources(p_refs[l]), g)
        if restore_b:
            g = jnp.concatenate([g[:, :BP_XQ], g[:, BP_GATE:BP_GATE + 2 * B_V_HEADS], g[:, BP_XQ:BP_GATE]], axis=1)
        delta, mm, vv = _adam_update(g, w_ref[...], m_ref[...], v_ref[...])
        g_ref[...] = g
        d_ref[...] = delta
        nm_ref[...] = mm
        nv_ref[...] = vv

    spec = pl.BlockSpec((None, tr, c), lambda l, i: (l, i, 0))
    part_specs = [pl.BlockSpec((N_DEV, tr, cp), functools.partial(lambda l, i, k: (0, jnp.where(l == k, i, 0), 0), k=k))
                  for k in range(nl)]
    return pl.pallas_call(
        body, grid=(nl, r // tr),
        in_specs=part_specs + [spec, spec, spec] + _dep_specs(deps),
        out_specs=[spec] * 4, out_shape=[_SDS(w.shape, F32)] * 4,
        name=name, compiler_params=_cp("arbitrary", "arbitrary"))(*parts, w, m, v, *deps)


def _pack_small(d_rel, d_cb, d_cw, d_qkv, d_mix, d_mem, d_ffn, d_final, d_sinks, d_par, d_ng, loss_row, name):
    flat = [d_rel, *d_cb, *d_cw, d_qkv, *d_mix, *d_mem, *d_ffn, d_final, d_sinks, d_par, d_ng, loss_row]
    n = len(flat)

    def body(*refs):
        ins, o_ref = refs[:n], refs[n]
        rel, cb0, cb1, cw0, cw1, qkv, mx0, mx1, me0, me1, ff0, ff1, fin, snk, par, ng, lss = ins
        o_ref[...] = jnp.zeros_like(o_ref)
        for k in range(N_BUCKETS):
            lane = SP_REL_LANE + 128 * (k % 8)
            o_ref[SP_QKV + k // 8:SP_QKV + k // 8 + 1, lane:lane + 128] = rel[k:k + 1, :]
        for l, (cb, cw) in enumerate(((cb0, cw0), (cb1, cw1))):
            o_ref[SP_CB + l:SP_CB + l + 1, :] = jnp.concatenate([cb[j] for j in range(FF_BLOCKS)], axis=1)
            full = jnp.concatenate([cw[j] for j in range(FF_BLOCKS)], axis=1)
            o_ref[SP_CW + FFN_CONV * l:SP_CW + FFN_CONV * (l + 1), :] = full[:FFN_CONV]
        o_ref[SP_QKV:SP_QKV + B_CONV, 0:B_QKV] = qkv[0:B_CONV, :]
        for base, pair in ((SP_MIX, (mx0, mx1)), (SP_MEM, (me0, me1)), (SP_FFN, (ff0, ff1))):
            for l in range(2):
                o_ref[base + l:base + l + 1, 0:D] = pair[l][...]
        o_ref[SP_FINAL:SP_FINAL + 1, 0:D] = fin[...]
        o_ref[SP_MISC:SP_MISC + 1, 0:128] = snk[...]
        o_ref[SP_MISC:SP_MISC + 1, 128:256] = par[...]
        o_ref[SP_MISC:SP_MISC + 1, 256:384] = ng[...]
        o_ref[SP_MISC:SP_MISC + 1, 384:512] = lss[...]

    vm = pl.BlockSpec(memory_space=pltpu.VMEM)
    return pl.pallas_call(body, in_specs=[vm] * n, out_specs=vm, out_shape=_SDS((SMALL_ROWS, D_FF), F32), name=name)(*flat)


_SMALL = ["rel_bias", "norm_mix_g", "norm_mem_g", "sinks_a", "a_log_b", "dt_bias_b", "out_norm_g_b", "norm_ffn_g",
          "ffn_conv_b", "final_norm_g", "conv_qkv_b", "ffn_conv_w"]


def _adamw_small(recv, rc_qkv, rc_ffn, ws, ms, vs, name, deps=()):
    n = len(_SMALL)

    def body(*refs):
        recv_ref, qkv_ref, ffn_ref = refs[:3]
        w_refs, m_refs, v_refs = refs[3:3 + n], refs[3 + n:3 + 2 * n], refs[3 + 2 * n:3 + 3 * n]
        outs, loss_ref = refs[len(refs) - 4 * n - 1:len(refs) - 1], refs[-1]
        gs = _sum_sources(recv_ref)
        loss_ref[...] = gs[SP_MISC:SP_MISC + 1, 384:512]
        grads = {
            "rel_bias": jnp.concatenate(
                [gs[SP_QKV + k // 8:SP_QKV + k // 8 + 1, SP_REL_LANE + 128 * (k % 8):SP_REL_LANE + 128 * (k % 8) + A_HEADS]
                 for k in range(N_BUCKETS)], axis=0),
            "norm_mix_g": gs[SP_MIX:SP_MIX + 2, 0:D], "norm_mem_g": gs[SP_MEM:SP_MEM + 2, 0:D],
            "sinks_a": gs[SP_MISC:SP_MISC + 1, 0:A_HEADS],
            "a_log_b": gs[SP_MISC:SP_MISC + 1, 128:128 + B_V_HEADS],
            "dt_bias_b": gs[SP_MISC:SP_MISC + 1, 128 + B_V_HEADS:128 + 2 * B_V_HEADS],
            "out_norm_g_b": gs[SP_MISC:SP_MISC + 1, 256:256 + B_HD],
            "norm_ffn_g": gs[SP_FFN:SP_FFN + 2, 0:D], "ffn_conv_b": gs[SP_CB:SP_CB + 2, :],
            "final_norm_g": gs[SP_FINAL:SP_FINAL + 1, 0:D],
            "conv_qkv_b": _sum_sources(qkv_ref), "ffn_conv_w": _sum_sources(ffn_ref),
        }
        for i, nm in enumerate(_SMALL):
            g = grads[nm]
            delta, mm, vv = _adam_update(g, w_refs[i][...], m_refs[i][...], v_refs[i][...])
            outs[i][...] = g
            outs[n + i][...] = delta
            outs[2 * n + i][...] = mm
            outs[3 * n + i][...] = vv

    vm = pl.BlockSpec(memory_space=pltpu.VMEM)
    shapes = [_SDS(w.shape, F32) for w in ws]
    return pl.pallas_call(
        body, in_specs=[vm] * (3 + 3 * n) + _dep_specs(deps), out_specs=[vm] * (4 * n + 1),
        out_shape=shapes * 4 + [_SDS((1, 128), F32)],
        name=name)(recv, rc_qkv, rc_ffn, *ws, *ms, *vs, *deps)


def _assemble(gathered, axis):
    g = jnp.moveaxis(gathered, 0, axis)
    shp = list(g.shape)
    return g.reshape(shp[:axis] + [shp[axis] * shp[axis + 1]] + shp[axis + 2:])


def _pad_rows(a, rows):
    return jnp.pad(a, ((0, rows - a.shape[0]), (0, 0)))


def _pad_lanes(a, lanes=128):
    return jnp.pad(a, ((0, 0), (0, lanes - a.shape[1])))


def _ff_blocks(a):
    return jnp.moveaxis(a.reshape(a.shape[0], FF_BLOCKS, GU_SHARD), 1, 0)


def _reorder_b(w):
    qkv_z = w[..., :B_QKV + B_V]
    gates = w[..., B_QKV + B_V:B_QKV + B_V + 2 * B_V_HEADS]
    xq = w[..., IN_B - X_Q:]
    pad = jnp.zeros(w.shape[:-1] + (IN_BP - IN_B,), w.dtype)
    return jnp.concatenate([qkv_z, xq, gates, pad], axis=-1)


def kernel(x, mem, rel_bias, norm_mix_g, norm_mem_g, w_mem_kv, w_out, w_in_a, sinks_a, w_in_b, conv_qkv_b, a_log_b, dt_bias_b, out_norm_g_b, norm_ffn_g, w_gate_up, ffn_conv_w, ffn_conv_b, w_down, final_norm_g, loss_target, m_rel_bias, m_norm_mix_g, m_norm_mem_g, m_w_mem_kv, m_w_out, m_w_in_a, m_sinks_a, m_w_in_b, m_conv_qkv_b, m_a_log_b, m_dt_bias_b, m_out_norm_g_b, m_norm_ffn_g, m_w_gate_up, m_ffn_conv_w, m_ffn_conv_b, m_w_down, m_final_norm_g, v_rel_bias, v_norm_mix_g, v_norm_mem_g, v_w_mem_kv, v_w_out, v_w_in_a, v_sinks_a, v_w_in_b, v_conv_qkv_b, v_a_log_b, v_dt_bias_b, v_out_norm_g_b, v_norm_ffn_g, v_w_gate_up, v_ffn_conv_w, v_ffn_conv_b, v_w_down, v_final_norm_g):
    local = dict(locals())
    order = ["rel_bias", "norm_mix_g", "norm_mem_g", "w_mem_kv", "w_out", "w_in_a", "sinks_a", "w_in_b", "conv_qkv_b",
             "a_log_b", "dt_bias_b", "out_norm_g_b", "norm_ffn_g", "w_gate_up", "ffn_conv_w", "ffn_conv_b", "w_down",
             "final_norm_g"]
    wts = {n: local[n] for n in order}
    moms = {n: local["m_" + n] for n in order}
    vars_ = {n: local["v_" + n] for n in order}
    h0 = x[0]
    memx = mem[0]
    tgt = loss_target[0]
    s = h0.shape[0]
    tm = _rows(s)
    tb = min(s, _TM_BIG)

    t_ = lambda a: jnp.swapaxes(a, 1, 2)
    g_mk, g_out, g_ia, g_cq, g_cw = _all_gather(
        [w_mem_kv.astype(_MXU), w_out.astype(_MXU), t_(w_in_a).astype(_MXU), conv_qkv_b, ffn_conv_w], "gather_first")
    gu_land = ((N_DEV, GU_SHARD, D), _MXU)
    dn_land = ((N_DEV, DN_SHARD, D), _MXU)
    whole = [(0, False), (1, False)]
    def after(a, b):
        return a + (b[(0,) * b.ndim] * 0).astype(a.dtype)

    ffn0_w = _seq_exchange([after(t_(w_gate_up)[0].astype(_MXU), g_ia), after(w_down[0].astype(_MXU), g_ia)], [gu_land, dn_land],
                           whole, "gather_ffn0", 1)
    w_ia = g_ia.reshape(IN_A, D)
    conv_qkv = _pad_rows(_assemble(g_cq, 2)[0], HALO)
    ffn_cw_full = _assemble(g_cw, 2)
    ffn_cw = [_ff_blocks(_pad_rows(ffn_cw_full[i], HALO)) for i in range(2)]
    ffn_cb = [_ff_blocks(ffn_conv_b[i:i + 1]) for i in range(2)]
    bucket = jnp.asarray(_bucket_table())
    bias = _bias_build(rel_bias, bucket, "bias_build")
    sinks = _pad_lanes(sinks_a)
    par_b = _pad_lanes(jnp.concatenate([a_log_b, dt_bias_b], axis=1))

    row_x = pl.BlockSpec((tm, D), lambda i, j: (i, 0))
    gu_shape = (2, FF_BLOCKS, s, GU_SHARD)

    def in_proj(h, g, w, w_spec, n_cols, tn, name, deps=(), out_dtype=F32, w_t=False):
        return _norm_matmul(h, g, w, w_spec, n_cols // tn, (h.shape[0], n_cols),
                            pl.BlockSpec((_rows(h.shape[0]), tn), lambda i, j: (i, j)), name, deps=deps, out_dtype=out_dtype,
                            w_t=w_t)

    def ffn_fwd(i, h, g_gu, g_dn, deps=()):
        gu, hn = _norm_matmul(h, norm_ffn_g[i:i + 1], g_gu, _spec_gate_up(1), N_DEV, gu_shape,
                              _spec_gu_act(0, 1, tb), f"gate_up_{i}", deps=deps, out_dtype=_ACT, w_t=True, tm=tb)
        h_new, act = _glu_down(gu, ffn_cw[i], ffn_cb[i], g_dn, h, f"glu_down_{i}")
        return h_new, gu, hn, act

    def out_proj(i, mix, h):
        return _matmul_res(mix, row_x, g_out, _spec_rowsharded(i, D // N_DEV, D), 1, h, f"out_proj_{i}")

    proj_a, hn_a = in_proj(h0, norm_mix_g[0:1], w_ia, pl.BlockSpec((640, D), lambda i, j: (j, 0)), IN_A, 640, "in_proj_a",
                           deps=ffn0_w.srcs, out_dtype=_ACT, w_t=True)
    memkv0, memn0 = in_proj(memx, norm_mem_g[0:1], g_mk, _spec_rowsharded(0, D // N_DEV, 2 * X_Q), 2 * X_Q, 2 * X_Q, "mem_proj_0")
    mix_a = _mix_a_fwd(proj_a, bias, sinks, memkv0, "mix_a_fwd")
    h1 = out_proj(0, mix_a, h0)
    g_gu0, g_dn0 = ffn0_w.lands
    in_b_w = _seq_exchange([after(_reorder_b(w_in_b).astype(_MXU), h1)], [((N_DEV, 1, D // N_DEV, IN_BP), _MXU)], [(0, False)],
                           "gather_in_b", 2)
    ffn1_w = _seq_exchange([after(t_(w_gate_up)[1].astype(_MXU), h1), after(w_down[1].astype(_MXU), h1)], [gu_land, dn_land], whole,
                           "gather_ffn1", 3)
    h2, gu0, hn_f0, act0 = ffn_fwd(0, h1, g_gu0, g_dn0, deps=in_b_w.srcs + ffn1_w.srcs)
    g_ib, = in_b_w.lands
    proj_b, hn_b = in_proj(h2, norm_mix_g[1:2], g_ib, _spec_rowsharded(0, D // N_DEV, 896, col_block=1), IN_BP, 896, "in_proj_b")
    memkv1, memn1 = in_proj(memx, norm_mem_g[1:2], g_mk, _spec_rowsharded(1, D // N_DEV, 2 * X_Q), 2 * X_Q, 2 * X_Q, "mem_proj_1")
    mix_b, states = _mix_b_fwd(proj_b, conv_qkv, par_b, out_norm_g_b, memkv1, "mix_b_fwd")
    h3 = out_proj(1, mix_b, h2)
    g_gu1, g_dn1 = ffn1_w.lands
    h4, gu1, hn_f1, act1 = ffn_fwd(1, h3, g_gu1, g_dn1)
    loss_row, dh, d_final_g = _loss_head(h4, final_norm_g[None, :], tgt, "loss_head")

    zeros_mem = jnp.zeros_like(memx)
    per_dest2 = [(0, True), (1, True)]

    def ffn_bwd(i, dh, h_in, gu, hn_f, act, g_gu, g_dn, deps=()):
        dgu, d_cw, d_cb = _glu_bwd(gu, ffn_cw[i], ffn_cb[i], dh, g_dn, f"glu_bwd_{i}", deps=deps)
        d_wdown = _matmul_tn(act, pl.BlockSpec((None, tm, GU_SHARD), lambda j, r: (j, r, 0)),
                             dh, pl.BlockSpec((tm, D), lambda j, r: (r, 0)), s, FF_BLOCKS, (GU_SHARD, D),
                             (N_DEV, DN_SHARD, D), pl.BlockSpec((2, DN_SHARD, D), lambda j, r: (j, 0, 0)), f"d_w_down_{i}")
        dh_new, d_g = _matmul_nt_normbwd(dgu, _spec_gu_act(0, 1, tm), g_gu, _spec_gate_up(1), N_DEV, h_in,
                                         norm_ffn_g[i:i + 1], dh, f"d_ffn_in_{i}", w_t=True)
        d_wgu = _matmul_tn(dgu, _spec_gu_act(1, 0, tb), hn_f, pl.BlockSpec((tb, D), lambda j, r: (r, 0)), s, N_DEV,
                           (GU_SHARD, D), (N_DEV, GU_SHARD, D), pl.BlockSpec((None, GU_SHARD, D), lambda j, r: (j, 0, 0)),
                           f"d_w_gate_up_{i}", tm=tb)
        sent = _seq_exchange([d_wdown, d_wgu], [((N_DEV, DN_SHARD, D), _WIRE), ((N_DEV, GU_SHARD, D), _WIRE)], per_dest2,
                             f"send_ffn{i}_grads", 4 + i)
        return dh_new, sent, d_cw, d_cb, d_g

    def out_bwd(i, dh, mix, deps):
        dmix = _matmul_nt(dh, g_out, _spec_rowsharded(i, D // N_DEV, D), 1, (s, D), row_x, f"d_mix_{i}", deps=deps, out_dtype=_ACT)
        d_wout = _matmul_tn(mix, pl.BlockSpec((tm, D), lambda j, r: (r, 0)), dh, pl.BlockSpec((tm, D), lambda j, r: (r, 0)),
                            s, 1, (D, D), (N_DEV, D // N_DEV, D), pl.BlockSpec((N_DEV, D // N_DEV, D), lambda j, r: (0, 0, 0)),
                            f"d_w_out_{i}")
        return dmix, d_wout

    def mem_bwd(i, dmemkv, memn):
        tmm = _rows(MEM_LEN)
        _, d_g = _matmul_nt_normbwd(dmemkv, pl.BlockSpec((tmm, 2 * X_Q), lambda r, j: (r, 0)), g_mk,
                                    _spec_rowsharded(i, D // N_DEV, 2 * X_Q), 1, memx, norm_mem_g[i:i + 1], zeros_mem,
                                    f"d_mem_in_{i}")
        by_row = lambda j, r: (r, 0)
        d_w = _matmul_tn(memn, pl.BlockSpec((tmm, D), by_row), dmemkv, pl.BlockSpec((tmm, 2 * X_Q), by_row), MEM_LEN, 1,
                         (D, 2 * X_Q), (N_DEV, D // N_DEV, 2 * X_Q),
                         pl.BlockSpec((N_DEV, D // N_DEV, 2 * X_Q), lambda j, r: (0, 0, 0)), f"d_w_mem_kv_{i}")
        return d_w, d_g

    out_land = ((N_DEV, D // N_DEV, D), _WIRE)
    mk_land = ((N_DEV, D // N_DEV, 2 * X_Q), _WIRE)
    dh, ffn1_g, d_cw1, d_cb1, d_gf1 = ffn_bwd(1, dh, h3, gu1, hn_f1, act1, g_gu1, g_dn1)
    dmix, d_wout1 = out_bwd(1, dh, mix_b, ffn1_g.srcs)
    dproj_b, d_convw, d_par, d_ng, dmemkv1 = _mix_b_bwd(proj_b, conv_qkv, par_b, out_norm_g_b, memkv1, states, dmix, "mix_b_bwd")
    dh, d_gm1 = _matmul_nt_normbwd(dproj_b, pl.BlockSpec((tm, 896), lambda i, j: (i, j)), g_ib,
                                   _spec_rowsharded(0, D // N_DEV, 896, col_block=1), IN_BP // 896, h2, norm_mix_g[1:2], dh, "d_in_b")
    d_wib = _matmul_tn(hn_b, pl.BlockSpec((tm, D), lambda j, r: (r, 0)), dproj_b, pl.BlockSpec((tm, 896), lambda j, r: (r, j)),
                       s, IN_BP // 896, (D, 896), (N_DEV, D // N_DEV, IN_BP),
                       pl.BlockSpec((N_DEV, D // N_DEV, 896), lambda j, r: (0, 0, j)), "d_w_in_b")
    d_wmk1, d_gmem1 = mem_bwd(1, dmemkv1, memn1)
    mix1_g = _seq_exchange([d_wout1, d_wib, d_wmk1], [out_land, ((N_DEV, D // N_DEV, IN_BP), _WIRE), mk_land],
                           [(0, True), (1, True), (2, True)], "send_mix1_grads", 6)
    dh, ffn0_g, d_cw0, d_cb0, d_gf0 = ffn_bwd(0, dh, h1, gu0, hn_f0, act0, g_gu0, g_dn0, deps=mix1_g.srcs)
    dmix, d_wout0 = out_bwd(0, dh, mix_a, ffn0_g.srcs + ffn1_g.lands[:1])
    dproj_a, dbias, dsinks, dmemkv0 = _mix_a_bwd(proj_a, bias, sinks, memkv0, dmix, "mix_a_bwd")
    dh, d_gm0 = _matmul_nt_normbwd(dproj_a, pl.BlockSpec((tm, 640), lambda i, j: (i, j)), w_ia,
                                   pl.BlockSpec((640, D), lambda i, j: (j, 0)), IN_A // 640, h0, norm_mix_g[0:1], dh, "d_in_a",
                                   w_t=True)
    d_wia = _matmul_tn(dproj_a, pl.BlockSpec((tm, IN_A), lambda j, r: (r, 0)), hn_a, pl.BlockSpec((tm, D), lambda j, r: (r, 0)),
                       s, 1, (IN_A, D), (N_DEV, IA_SHARD, D), pl.BlockSpec((N_DEV, IA_SHARD, D), lambda j, r: (0, 0, 0)),
                       "d_w_in_a")
    d_wmk0, d_gmem0 = mem_bwd(0, dmemkv0, memn0)
    d_rel = _bias_reduce(dbias, bucket, "bias_reduce")
    small = _pack_small(d_rel, (d_cb0, d_cb1), (d_cw0, d_cw1), d_convw, (d_gm0, d_gm1), (d_gmem0, d_gmem1),
                        (d_gf0, d_gf1), d_final_g, dsinks, d_par, d_ng, loss_row, "pack_small")
    mix0_g = _seq_exchange([d_wout0, d_wia, d_wmk0, small],
                           [out_land, ((N_DEV, IA_SHARD, D), _WIRE), mk_land, ((N_DEV, SMALL_ROWS, D_FF), F32)],
                           [(0, True), (1, True), (2, True), (3, False)], "send_mix0_grads", 7)

    res = {}
    last = []

    def update(nm, parts, tr, restore=False, transposed=False):
        view = t_ if transposed else (lambda a: a)
        out = _adamw(parts, view(wts[nm]), view(moms[nm]), view(vars_[nm]), tr, "adamw_" + nm, restore_b=restore, deps=last[-1:])
        res[nm] = [view(o) for o in out]
        last.append(out[1])

    r_dn1, r_gu1 = ffn1_g.lands
    r_dn0, r_gu0 = ffn0_g.lands
    r_out1, r_ib, r_mk1 = mix1_g.lands
    update("w_gate_up", [r_gu0, r_gu1], 176, transposed=True)
    update("w_down", [r_dn0, r_dn1], 176)
    update("w_in_b", [r_ib], 32, True)
    r_out0, r_ia, r_mk0, r_small = mix0_g.lands
    update("w_mem_kv", [r_mk0, r_mk1], 128)
    update("w_out", [r_out0, r_out1], 128)
    update("w_in_a", [r_ia], IA_SHARD, transposed=True)

    my = 4 * lax.axis_index("x") + 2 * lax.axis_index("y") + lax.axis_index("c")
    cq = conv_qkv_b.shape[-1]
    cf = ffn_conv_w.shape[-1]
    rc_qkv = lax.dynamic_slice_in_dim(r_small[:, SP_QKV:SP_QKV + B_CONV, :B_QKV], my * cq, cq, axis=2)[:, None]
    rc_ffn = lax.dynamic_slice_in_dim(r_small[:, SP_CW:SP_CW + 2 * FFN_CONV, :], my * cf, cf, axis=2).reshape(N_DEV, 2, FFN_CONV, cf)
    as2d = lambda a: a[None, :] if a.ndim == 1 else a
    small_out = _adamw_small(r_small, rc_qkv, rc_ffn, [as2d(wts[n]) for n in _SMALL], [as2d(moms[n]) for n in _SMALL],
                             [as2d(vars_[n]) for n in _SMALL], "adamw_small", deps=last[-1:])
    ns = len(_SMALL)
    for i, nm in enumerate(_SMALL):
        res[nm] = [small_out[k * ns + i].reshape(wts[nm].shape) for k in range(4)]

    return (small_out[-1][0, 0], dh[None], *[res[n][0] for n in order], *[res[n][1] for n in order],
            *[res[n][2] for n in order], *[res[n][3] for n in order])
```

```python
import functools
import math

import numpy as np

import jax
import jax.numpy as jnp
from jax import lax
from jax.experimental import pallas as pl
from jax.experimental.pallas import tpu as pltpu
from jax.experimental.pallas import tpu_sc as plsc

F32 = jnp.float32
_MXU = jnp.bfloat16
_ACT = jnp.bfloat16
_WIRE = jnp.bfloat16
_HI = lax.Precision.HIGH
_TM = 1024
_TM_GLU = 512
_TM_BIG = 2048
_VMEM_LIMIT = 48 * 1024 * 1024
_SDS = jax.ShapeDtypeStruct

D = 1024
EPS = 1e-6
A_HEADS, A_KV_HEADS, A_HD, BLK = 12, 2, 64, 128
N_BUCKETS, MAX_DISTANCE = 32, 128
B_QK_HEADS, B_V_HEADS, B_HD, B_CONV, CHUNK = 3, 6, 128, 4, 64
X_HEADS, X_HD, MEM_LEN = 4, 64, 256
D_FF, FFN_CONV = 2816, 3
A_Q, A_KV, X_Q = 768, 128, 256
B_QK, B_V, B_QKV = 384, 768, 1536
IN_A, IN_B = 1280, 2572
IN_BP = 2688
BP_Z, BP_XQ, BP_GATE = 1536, 2304, 2560
HALO = 8
GLU_HALO = 16

N_DEV = 8
GU_SHARD = 2 * D_FF // N_DEV
FF_BLOCKS = D_FF // GU_SHARD
DN_SHARD = D_FF // N_DEV
IA_SHARD = IN_A // N_DEV

ADAM_LR, ADAM_B1, ADAM_B2, ADAM_EPS, ADAM_WD, ADAM_STEP = 0.001, 0.9, 0.999, 1e-08, 0.01, 10

SP_CB, SP_CW, SP_QKV, SP_MIX, SP_MEM, SP_FFN, SP_FINAL, SP_MISC, SMALL_ROWS = 0, 2, 8, 12, 14, 16, 18, 19, 24
SP_REL_LANE = B_QKV


def _cp(*sems):
    return pltpu.CompilerParams(dimension_semantics=sems, vmem_limit_bytes=_VMEM_LIMIT)


def _mm(a, b):
    return jnp.dot(a.astype(_MXU), b.astype(_MXU), preferred_element_type=F32)


def _mm_nt(a, b):
    return lax.dot_general(a.astype(_MXU), b.astype(_MXU), (((1,), (1,)), ((), ())), preferred_element_type=F32)


def _mm_tn(a, b):
    return lax.dot_general(a.astype(_MXU), b.astype(_MXU), (((0,), (0,)), ((), ())), preferred_element_type=F32)


def _mmf(a, b):
    return jnp.dot(a, b, preferred_element_type=F32, precision=_HI)


def _mmf_nt(a, b):
    return lax.dot_general(a, b, (((1,), (1,)), ((), ())), preferred_element_type=F32, precision=_HI)


def _mmf_tn(a, b):
    return lax.dot_general(a, b, (((0,), (0,)), ((), ())), preferred_element_type=F32, precision=_HI)


def _silu(x):
    return x * jax.nn.sigmoid(x)


def _w2d(ref):
    v = ref[...]
    return v.reshape(-1, v.shape[-1])


def _rows(m):
    return min(m, _TM)


def _spec_rowsharded(layer, rows, cols, col_block=None):
    if col_block is None:
        return pl.BlockSpec((N_DEV, None, rows, cols), lambda *_: (0, layer, 0, 0))
    return pl.BlockSpec((N_DEV, None, rows, cols), lambda *ids: (0, layer, 0, ids[col_block]))


def _spec_gate_up(axis):
    return pl.BlockSpec((None, GU_SHARD, D), lambda *ids: (ids[axis], 0, 0))


def _spec_down(axis):
    return pl.BlockSpec((2, DN_SHARD, D), lambda *ids: (ids[axis], 0, 0))


def _dep_specs(deps):
    return [pl.BlockSpec(memory_space=pl.ANY) for d in deps]


def _spec_gu_act(row_axis, axis, tm):
    return pl.BlockSpec((None, None, tm, GU_SHARD), lambda *ids: (ids[axis] // FF_BLOCKS, ids[axis] % FF_BLOCKS, ids[row_axis], 0))


def _norm_matmul(x, g, w, w_spec, n_blocks, out_shape, out_spec, name, deps=(), out_dtype=F32, w_t=False, tm=None):
    m, k = x.shape
    tm = tm or _rows(m)

    def body(x_ref, g_ref, w_ref, *rest):
        y_ref, hn_ref = rest[-2:]

        @pl.when(pl.program_id(1) == 0)
        def _():
            xv = x_ref[...]
            r = lax.rsqrt(jnp.mean(xv * xv, axis=-1, keepdims=True) + EPS)
            hn_ref[...] = (xv * r * g_ref[...]).astype(hn_ref.dtype)

        y_ref[...] = (_mm_nt if w_t else _mm)(hn_ref[...], _w2d(w_ref)).astype(y_ref.dtype)

    return pl.pallas_call(
        body, grid=(m // tm, n_blocks),
        in_specs=[pl.BlockSpec((tm, k), lambda i, j: (i, 0)), pl.BlockSpec((1, k), lambda i, j: (0, 0)), w_spec]
        + _dep_specs(deps),
        out_specs=[out_spec, pl.BlockSpec((tm, k), lambda i, j: (i, 0))],
        out_shape=[_SDS(out_shape, out_dtype), _SDS((m, k), _ACT)],
        name=name, compiler_params=_cp("arbitrary", "arbitrary"))(x, g, w, *deps)


def _matmul_res(a, a_spec, w, w_spec, n_k, res, name):
    m, n = res.shape
    tm = _rows(m)

    def body(a_ref, w_ref, r_ref, o_ref):
        part = _mm(a_ref[...], _w2d(w_ref))

        @pl.when(pl.program_id(1) == 0)
        def _():
            o_ref[...] = r_ref[...] + part

        @pl.when(pl.program_id(1) > 0)
        def _():
            o_ref[...] += part

    return pl.pallas_call(
        body, grid=(m // tm, n_k),
        in_specs=[a_spec, w_spec, pl.BlockSpec((tm, n), lambda i, j: (i, 0))],
        out_specs=pl.BlockSpec((tm, n), lambda i, j: (i, 0)),
        out_shape=_SDS((m, n), F32), name=name, compiler_params=_cp("arbitrary", "arbitrary"))(a, w, res)


def _matmul_nt(dy, w, w_spec, n_blocks, out_shape, out_spec, name, deps=(), out_dtype=F32):
    m, n = dy.shape
    tm = _rows(m)

    def body(dy_ref, w_ref, *rest):
        o_ref = rest[-1]
        o_ref[...] = _mm_nt(dy_ref[...], _w2d(w_ref)).astype(o_ref.dtype)

    return pl.pallas_call(
        body, grid=(m // tm, n_blocks),
        in_specs=[pl.BlockSpec((tm, n), lambda i, j: (i, 0)), w_spec] + _dep_specs(deps),
        out_specs=out_spec, out_shape=_SDS(out_shape, out_dtype),
        name=name, compiler_params=_cp("arbitrary", "arbitrary"))(dy, w, *deps)


def _matmul_nt_normbwd(dy, dy_spec, w, w_spec, nj, h, g, dh_in, name, w_t=False):
    m, k = h.shape
    tm = _rows(m)

    def body(dy_ref, w_ref, h_ref, g_ref, dhin_ref, dh_ref, dg_ref, acc_ref):
        i, j = pl.program_id(0), pl.program_id(1)

        @pl.when(j == 0)
        def _():
            acc_ref[...] = jnp.zeros_like(acc_ref)

        acc_ref[...] += (_mm if w_t else _mm_nt)(dy_ref[...], _w2d(w_ref))

        @pl.when(j == nj - 1)
        def _():
            xv = h_ref[...]
            r = lax.rsqrt(jnp.mean(xv * xv, axis=-1, keepdims=True) + EPS)
            xh = xv * r
            dhn = acc_ref[...]
            part = jnp.sum(dhn * xh, axis=0, keepdims=True)

            @pl.when(i == 0)
            def _():
                dg_ref[...] = part

            @pl.when(i > 0)
            def _():
                dg_ref[...] += part

            t = dhn * g_ref[...]
            dh_ref[...] = dhin_ref[...] + r * (t - xh * jnp.mean(t * xh, axis=-1, keepdims=True))

    return pl.pallas_call(
        body, grid=(m // tm, nj),
        in_specs=[dy_spec, w_spec, pl.BlockSpec((tm, k), lambda i, j: (i, 0)), pl.BlockSpec((1, k), lambda i, j: (0, 0)),
                  pl.BlockSpec((tm, k), lambda i, j: (i, 0))],
        out_specs=[pl.BlockSpec((tm, k), lambda i, j: (i, 0)), pl.BlockSpec((1, k), lambda i, j: (0, 0))],
        out_shape=[_SDS((m, k), F32), _SDS((1, k), F32)],
        scratch_shapes=[pltpu.VMEM((tm, k), F32)],
        name=name, compiler_params=_cp("arbitrary", "arbitrary"))(dy, w, h, g, dh_in)


def _matmul_tn(x, x_spec, dy, dy_spec, m, n_blocks, acc_shape, out_shape, out_spec, name, tm=None):
    tm = tm or _rows(m)
    nm = m // tm

    def body(x_ref, dy_ref, o_ref, acc_ref):
        @pl.when(pl.program_id(1) == 0)
        def _():
            acc_ref[...] = jnp.zeros_like(acc_ref)

        acc_ref[...] += _mm_tn(x_ref[...], dy_ref[...])

        @pl.when(pl.program_id(1) == nm - 1)
        def _():
            o_ref[...] = acc_ref[...].reshape(o_ref.shape).astype(o_ref.dtype)

    return pl.pallas_call(
        body, grid=(n_blocks, nm), in_specs=[x_spec, dy_spec], out_specs=out_spec,
        out_shape=_SDS(out_shape, _WIRE), scratch_shapes=[pltpu.VMEM(acc_shape, F32)],
        name=name, compiler_params=_cp("arbitrary", "arbitrary"))(x, dy)


def _loss_head(h, g, tgt, name):
    m, k = h.shape
    tm = _rows(m)

    def body(h_ref, g_ref, t_ref, loss_ref, dh_ref, dg_ref):
        i = pl.program_id(0)
        xv = h_ref[...]
        r = lax.rsqrt(jnp.mean(xv * xv, axis=-1, keepdims=True) + EPS)
        xh = xv * r
        gv = g_ref[...]
        err = xh * gv - t_ref[...]
        lpart = jnp.zeros((1, 128), F32) + 0.5 * jnp.sum(jnp.mean(err * err, axis=-1, keepdims=True), axis=0, keepdims=True)
        dy = err * (1.0 / k)
        gpart = jnp.sum(dy * xh, axis=0, keepdims=True)

        @pl.when(i == 0)
        def _():
            loss_ref[...] = lpart
            dg_ref[...] = gpart

        @pl.when(i > 0)
        def _():
            loss_ref[...] += lpart
            dg_ref[...] += gpart

        t = dy * gv
        dh_ref[...] = r * (t - xh * jnp.mean(t * xh, axis=-1, keepdims=True))

    return pl.pallas_call(
        body, grid=(m // tm,),
        in_specs=[pl.BlockSpec((tm, k), lambda i: (i, 0)), pl.BlockSpec((1, k), lambda i: (0, 0)),
                  pl.BlockSpec((tm, k), lambda i: (i, 0))],
        out_specs=[pl.BlockSpec((1, 128), lambda i: (0, 0)), pl.BlockSpec((tm, k), lambda i: (i, 0)),
                   pl.BlockSpec((1, k), lambda i: (0, 0))],
        out_shape=[_SDS((1, 128), F32), _SDS((m, k), F32), _SDS((1, k), F32)],
        name=name, compiler_params=_cp("arbitrary"))(h, g, tgt)


def _glu_down(gu, conv_w, conv_b, w_down, res, name):
    s = gu.shape[2]
    tm = min(s, _TM_GLU)

    def body(gu_ref, prev_ref, w_ref, b_ref, wdn_ref, r_ref, o_ref, act_ref):
        i, j = pl.program_id(0), pl.program_id(1)
        prev = jnp.where(i > 0, prev_ref[...].astype(F32), 0.0)
        ext = jnp.concatenate([prev, gu_ref[0].astype(F32)], axis=0)
        gc = b_ref[...] + w_ref[FFN_CONV - 1:FFN_CONV, :] * ext
        for k in range(FFN_CONV - 1):
            gc = gc + w_ref[k:k + 1, :] * pltpu.roll(ext, FFN_CONV - 1 - k, 0)
        act = (_silu(gc[GLU_HALO:]) * gu_ref[1].astype(F32)).astype(act_ref.dtype)
        act_ref[...] = act
        part = _mm(act, _w2d(wdn_ref))

        @pl.when(j == 0)
        def _():
            o_ref[...] = r_ref[...] + part

        @pl.when(j > 0)
        def _():
            o_ref[...] += part

    return pl.pallas_call(
        body, grid=(s // tm, FF_BLOCKS),
        in_specs=[pl.BlockSpec((2, None, tm, GU_SHARD), lambda i, j: (0, j, i, 0)),
                  pl.BlockSpec((None, None, GLU_HALO, GU_SHARD),
                               lambda i, j: (0, j, jnp.maximum(i * (tm // GLU_HALO) - 1, 0), 0)),
                  pl.BlockSpec((None, HALO, GU_SHARD), lambda i, j: (j, 0, 0)),
                  pl.BlockSpec((None, 1, GU_SHARD), lambda i, j: (j, 0, 0)),
                  _spec_down(1), pl.BlockSpec((tm, D), lambda i, j: (i, 0))],
        out_specs=[pl.BlockSpec((tm, D), lambda i, j: (i, 0)), pl.BlockSpec((None, tm, GU_SHARD), lambda i, j: (j, i, 0))],
        out_shape=[_SDS((s, D), F32), _SDS((FF_BLOCKS, s, GU_SHARD), _ACT)], name=name,
        compiler_params=_cp("arbitrary", "arbitrary"))(gu, gu, conv_w, conv_b, w_down, res)


def _glu_bwd(gu, conv_w, conv_b, dh, w_down, name, deps=()):
    s = gu.shape[2]
    tm = min(s, _TM_GLU)
    nt = s // tm
    ext_rows = tm + GLU_HALO

    def body(gu_ref, prev_ref, w_ref, b_ref, dh_ref, wdn_ref, *rest):
        dgu_ref, dw_ref, db_ref, carry_ref = rest[-4:]
        t = pl.program_id(1)
        i = nt - 1 - t

        @pl.when(t == 0)
        def _():
            carry_ref[...] = jnp.zeros_like(carry_ref)
            dw_ref[...] = jnp.zeros_like(dw_ref)
            db_ref[...] = jnp.zeros_like(db_ref)

        up = gu_ref[1].astype(F32)
        prev = jnp.where(i > 0, prev_ref[...].astype(F32), 0.0)
        ext = jnp.concatenate([prev, gu_ref[0].astype(F32)], axis=0)
        shifted = [pltpu.roll(ext, FFN_CONV - 1 - j, 0) if j < FFN_CONV - 1 else ext for j in range(FFN_CONV)]
        gc = b_ref[...] + shifted[0] * w_ref[0:1, :]
        for j in range(1, FFN_CONV):
            gc = gc + shifted[j] * w_ref[j:j + 1, :]
        gc = gc[GLU_HALO:]
        sg = jax.nn.sigmoid(gc)
        da = _mm_nt(dh_ref[...], _w2d(wdn_ref))
        dup = da * (gc * sg)
        dgc = da * up * (sg * (1.0 + gc * (1.0 - sg)))
        db_ref[...] += jnp.sum(dgc, axis=0, keepdims=True)
        dgc_ext = jnp.concatenate([jnp.zeros((GLU_HALO, GU_SHARD), F32), dgc], axis=0)
        dext = dgc_ext * w_ref[FFN_CONV - 1:FFN_CONV, :]
        for j in range(FFN_CONV):
            dw_ref[j:j + 1, :] += jnp.sum(shifted[j] * dgc_ext, axis=0, keepdims=True)
            if j < FFN_CONV - 1:
                dext = dext + w_ref[j:j + 1, :] * pltpu.roll(dgc_ext, ext_rows - (FFN_CONV - 1 - j), 0)
        tail = jnp.concatenate([jnp.zeros((tm - GLU_HALO, GU_SHARD), F32), carry_ref[...]], axis=0)
        dgate = dext[GLU_HALO:] + tail
        carry_ref[...] = dext[:GLU_HALO]
        dgu_ref[0] = dgate.astype(dgu_ref.dtype)
        dgu_ref[1] = dup.astype(dgu_ref.dtype)

    return pl.pallas_call(
        body, grid=(FF_BLOCKS, nt),
        in_specs=[pl.BlockSpec((2, None, tm, GU_SHARD), lambda j, t: (0, j, nt - 1 - t, 0)),
                  pl.BlockSpec((None, None, GLU_HALO, GU_SHARD),
                               lambda j, t: (0, j, jnp.maximum((nt - 1 - t) * (tm // GLU_HALO) - 1, 0), 0)),
                  pl.BlockSpec((None, HALO, GU_SHARD), lambda j, t: (j, 0, 0)),
                  pl.BlockSpec((None, 1, GU_SHARD), lambda j, t: (j, 0, 0)),
                  pl.BlockSpec((tm, D), lambda j, t: (nt - 1 - t, 0)), _spec_down(0)] + _dep_specs(deps),
        out_specs=[pl.BlockSpec((2, None, tm, GU_SHARD), lambda j, t: (0, j, nt - 1 - t, 0)),
                   pl.BlockSpec((None, HALO, GU_SHARD), lambda j, t: (j, 0, 0)),
                   pl.BlockSpec((None, 1, GU_SHARD), lambda j, t: (j, 0, 0))],
        out_shape=[_SDS(gu.shape, _ACT), _SDS((FF_BLOCKS, HALO, GU_SHARD), F32), _SDS((FF_BLOCKS, 1, GU_SHARD), F32)],
        scratch_shapes=[pltpu.VMEM((GLU_HALO, GU_SHARD), F32)],
        name=name, compiler_params=_cp("arbitrary", "arbitrary"))(gu, gu, conv_w, conv_b, dh, w_down, *deps)


def _bucket_table():
    qi = np.arange(BLK)[:, None]
    kj = np.arange(BLK)[None, :]
    n = np.where(kj > qi, BLK + qi - kj, qi - kj)
    max_exact = N_BUCKETS // 2
    nf = np.maximum(n, 1).astype(np.float32)
    large = max_exact + (np.log(nf / max_exact) / math.log(MAX_DISTANCE / max_exact)
                         * (N_BUCKETS - max_exact)).astype(np.int32)
    large = np.minimum(large, N_BUCKETS - 1)
    return np.where(n < max_exact, n, large).astype(np.int32)


def _lane_low():
    return lax.broadcasted_iota(jnp.int32, (1, 128), 1) < A_HD


def _swa_groups(q, kd, vd, sink, bias, upper, first):
    n = A_HEADS // A_KV_HEADS
    ng = A_KV_HEADS
    low = _lane_low()
    qm = [jnp.concatenate([jnp.where(low == (h % 2 == 0), q[g][:, (h // 2) * 128:(h // 2 + 1) * 128], 0.0) for h in range(n)], axis=0)
          for g in range(ng)]
    s2 = [_mm_nt(qm[g], kd[g]) * (A_HD ** -0.5) for g in range(ng)]
    s = [jnp.where(upper[None], s2[g][:, :BLK].reshape(n, BLK, BLK), s2[g][:, BLK:].reshape(n, BLK, BLK)) + bias[g] for g in range(ng)]
    s = [jnp.where((upper & first)[None], -jnp.inf, t) for t in s]
    m = [jnp.maximum(jnp.max(s[g], axis=-1, keepdims=True), sink[g]) for g in range(ng)]
    p = [jnp.exp(s[g] - m[g]) for g in range(ng)]
    split = [jnp.concatenate([jnp.where(upper[None], t, 0.0), jnp.where(upper[None], 0.0, t)], axis=-1).reshape(n * BLK, 2 * BLK)
             for t in p]
    ones = jnp.ones((BLK, 128), F32)
    den = [_mm(p[g].reshape(n * BLK, BLK), ones) + jnp.exp(sink[g] - m[g]).reshape(n * BLK, 1) for g in range(ng)]
    o = [_mm(split[g], vd[g]) / den[g] for g in range(ng)]
    return [jnp.concatenate([jnp.where(low, t[2 * k * BLK:(2 * k + 1) * BLK], t[(2 * k + 1) * BLK:(2 * k + 2) * BLK])
                             for k in range(n // 2)], axis=1) for t in o]


def _mix_a_core(q, kd, vd, sink, bias, xq, mk, mv, upper, first):
    return _swa_groups(q, kd, vd, sink, bias, upper, first), _cross_pairs(xq, mk, mv)


def _swa_sinks(sink_ref, g):
    n = A_HEADS // A_KV_HEADS
    return jnp.concatenate([sink_ref[:, h:h + 1] for h in range(g * n, (g + 1) * n)], axis=0).reshape(n, 1, 1)


def _both_halves(t, t_rolled, g):
    low = _lane_low()
    return jnp.where(low, t, t_rolled) if g == 0 else jnp.where(low, t_rolled, t)


def _cross_pairs(q, mk, mv):
    rows = q.shape[0]
    low = _lane_low()
    qm = [jnp.concatenate([jnp.where(low, q[:, p * 128:(p + 1) * 128], 0.0), jnp.where(low, 0.0, q[:, p * 128:(p + 1) * 128])], axis=0)
          for p in range(X_HEADS // 2)]
    s = [_mm_nt(qm[p], mk[:, p * 128:(p + 1) * 128]) * (X_HD ** -0.5) for p in range(X_HEADS // 2)]
    e = [jnp.exp(t - jnp.max(t, axis=-1, keepdims=True)) for t in s]
    pr = [t / jnp.sum(t, axis=-1, keepdims=True) for t in e]
    o = [_mm(pr[p], mv[:, p * 128:(p + 1) * 128]) for p in range(X_HEADS // 2)]
    return jnp.concatenate([jnp.where(low, t[:rows], t[rows:]) for t in o], axis=1)


def _swa_upper():
    qi = lax.broadcasted_iota(jnp.int32, (BLK, BLK), 0)
    kj = lax.broadcasted_iota(jnp.int32, (BLK, BLK), 1)
    return kj > qi


def _bias_build(rel_bias, bucket, name):
    def body(rb_ref, bucket_ref, o_ref):
        b = bucket_ref[...]
        for h in range(A_HEADS):
            acc = jnp.zeros((BLK, BLK), F32)
            for k in range(N_BUCKETS):
                acc = jnp.where(b == k, rb_ref[k, h], acc)
            o_ref[h] = acc

    return pl.pallas_call(
        body, in_specs=[pl.BlockSpec(memory_space=pltpu.SMEM), pl.BlockSpec(memory_space=pltpu.VMEM)],
        out_specs=pl.BlockSpec(memory_space=pltpu.VMEM),
        out_shape=_SDS((A_HEADS, BLK, BLK), F32), name=name)(rel_bias, bucket)


def _bias_reduce(dbias, bucket, name):
    def body(db_ref, bucket_ref, o_ref):
        b = bucket_ref[...]
        row = lax.broadcasted_iota(jnp.int32, (N_BUCKETS, 128), 0)
        lane = lax.broadcasted_iota(jnp.int32, (N_BUCKETS, 128), 1)
        acc = jnp.zeros((N_BUCKETS, 128), F32)
        for h in range(A_HEADS):
            v = db_ref[h]
            for k in range(N_BUCKETS):
                sk = jnp.sum(jnp.sum(jnp.where(b == k, v, 0.0), axis=1, keepdims=True), axis=0, keepdims=True)
                acc = acc + jnp.where((row == k) & (lane == h), sk, 0.0)
        o_ref[...] = acc

    return pl.pallas_call(
        body, in_specs=[pl.BlockSpec(memory_space=pltpu.VMEM)] * 2,
        out_specs=pl.BlockSpec(memory_space=pltpu.VMEM),
        out_shape=_SDS((N_BUCKETS, 128), F32), name=name)(dbias, bucket)


def _mix_a_fwd(proj, bias, sinks, memkv, name):
    s = proj.shape[0]
    nb = s // BLK
    grp = A_HEADS // A_KV_HEADS

    def body(proj_ref, prev_ref, bias_ref, sink_ref, memkv_ref, o_ref):
        i = pl.program_id(0)
        upper = _swa_upper()
        prev = prev_ref[...].astype(F32)
        proj = proj_ref[...].astype(F32)
        kb = jnp.concatenate([prev[:, :A_KV], proj[:, A_Q:A_Q + A_KV]], axis=0)
        vb = jnp.concatenate([prev[:, A_KV:], proj[:, A_Q + A_KV:A_Q + 2 * A_KV]], axis=0)
        kb_r = pltpu.roll(kb, A_HD, 1)
        vb_r = pltpu.roll(vb, A_HD, 1)
        gw = A_Q // A_KV_HEADS
        groups = range(A_KV_HEADS)
        swa, cross = _mix_a_core([proj[:, g * gw:(g + 1) * gw] for g in groups], [_both_halves(kb, kb_r, g) for g in groups],
                                 [_both_halves(vb, vb_r, g) for g in groups], [_swa_sinks(sink_ref, g) for g in groups],
                                 [bias_ref[g * grp:(g + 1) * grp] for g in groups], proj[:, A_Q + 2 * A_KV:],
                                 memkv_ref[:, :X_Q], memkv_ref[:, X_Q:], upper, i == 0)
        o_ref[...] = jnp.concatenate(swa + [cross], axis=1).astype(o_ref.dtype)

    return pl.pallas_call(
        body, grid=(nb,),
        in_specs=[pl.BlockSpec((BLK, IN_A), lambda i: (i, 0)),
                  pl.BlockSpec((BLK, 2 * A_KV), lambda i: (jnp.maximum(i - 1, 0), A_Q // (2 * A_KV))),
                  pl.BlockSpec((A_HEADS, BLK, BLK), lambda i: (0, 0, 0)),
                  pl.BlockSpec((1, 128), lambda i: (0, 0)),
                  pl.BlockSpec((MEM_LEN, 2 * X_Q), lambda i: (0, 0))],
        out_specs=pl.BlockSpec((BLK, D), lambda i: (i, 0)),
        out_shape=_SDS((s, D), _ACT), name=name, compiler_params=_cp("arbitrary"))(proj, proj, bias, sinks, memkv)


def _mix_a_bwd(proj, bias, sinks, memkv, dmix, name, deps=()):
    s = proj.shape[0]
    nb = s // BLK
    grp = A_HEADS // A_KV_HEADS

    def body(proj_ref, prev_ref, bias_ref, sink_ref, memkv_ref, dmix_ref, *rest):
        dproj_ref, dbias_ref, dsink_ref, dmemkv_ref, carry_ref = rest[-5:]
        t = pl.program_id(0)
        i = nb - 1 - t

        @pl.when(t == 0)
        def _():
            carry_ref[...] = jnp.zeros_like(carry_ref)
            dbias_ref[...] = jnp.zeros_like(dbias_ref)
            dsink_ref[...] = jnp.zeros_like(dsink_ref)
            dmemkv_ref[...] = jnp.zeros_like(dmemkv_ref)

        upper = _swa_upper()
        lane = lax.broadcasted_iota(jnp.int32, (1, 128), 1)
        low = _lane_low()
        prev = prev_ref[...].astype(F32)
        proj = proj_ref[...].astype(F32)
        kb = jnp.concatenate([prev[:, :A_KV], proj[:, A_Q:A_Q + A_KV]], axis=0)
        vb = jnp.concatenate([prev[:, A_KV:], proj[:, A_Q + A_KV:A_Q + 2 * A_KV]], axis=0)
        kb_r = pltpu.roll(kb, A_HD, 1)
        vb_r = pltpu.roll(vb, A_HD, 1)
        gw = A_Q // A_KV_HEADS
        groups = range(A_KV_HEADS)
        _, vjp = jax.vjp(
            functools.partial(_mix_a_core, upper=upper, first=i == 0),
            [proj[:, g * gw:(g + 1) * gw] for g in groups], [_both_halves(kb, kb_r, g) for g in groups],
            [_both_halves(vb, vb_r, g) for g in groups], [_swa_sinks(sink_ref, g) for g in groups],
            [bias_ref[g * grp:(g + 1) * grp] for g in groups], proj[:, A_Q + 2 * A_KV:], memkv_ref[:, :X_Q], memkv_ref[:, X_Q:])
        dqs, dk, dv, ds, db, dxq, dmk, dmv = vjp(
            ([dmix_ref[:, g * gw:(g + 1) * gw].astype(F32) for g in groups], dmix_ref[:, A_Q:].astype(F32)))
        dkd = [t + pltpu.roll(t, A_HD, 1) for t in dk]
        dvd = [t + pltpu.roll(t, A_HD, 1) for t in dv]
        dsink = jnp.zeros((1, 128), F32)
        for g in groups:
            for h in range(grp):
                dsink = dsink + jnp.where(lane == g * grp + h, ds[g][h], 0.0)
            dbias_ref[g * grp:(g + 1) * grp] += db[g]
        dsink_ref[...] += dsink
        dkb = jnp.where(low, dkd[0], dkd[1])
        dvb = jnp.where(low, dvd[0], dvd[1])
        dmemkv_ref[...] += jnp.concatenate([dmk, dmv], axis=1)
        dkv_cur = jnp.concatenate([dkb[BLK:], dvb[BLK:]], axis=1) + carry_ref[...]
        carry_ref[...] = jnp.concatenate([dkb[:BLK], dvb[:BLK]], axis=1)
        dproj_ref[...] = jnp.concatenate(list(dqs) + [dkv_cur, dxq], axis=1).astype(dproj_ref.dtype)

    return pl.pallas_call(
        body, grid=(nb,),
        in_specs=[pl.BlockSpec((BLK, IN_A), lambda t: (nb - 1 - t, 0)),
                  pl.BlockSpec((BLK, 2 * A_KV), lambda t: (jnp.maximum(nb - 2 - t, 0), A_Q // (2 * A_KV))),
                  pl.BlockSpec((A_HEADS, BLK, BLK), lambda t: (0, 0, 0)),
                  pl.BlockSpec((1, 128), lambda t: (0, 0)),
                  pl.BlockSpec((MEM_LEN, 2 * X_Q), lambda t: (0, 0)),
                  pl.BlockSpec((BLK, D), lambda t: (nb - 1 - t, 0))] + _dep_specs(deps),
        out_specs=[pl.BlockSpec((BLK, IN_A), lambda t: (nb - 1 - t, 0)),
                   pl.BlockSpec((A_HEADS, BLK, BLK), lambda t: (0, 0, 0)),
                   pl.BlockSpec((1, 128), lambda t: (0, 0)),
                   pl.BlockSpec((MEM_LEN, 2 * X_Q), lambda t: (0, 0))],
        out_shape=[_SDS((s, IN_A), _ACT), _SDS((A_HEADS, BLK, BLK), F32), _SDS((1, 128), F32),
                   _SDS((MEM_LEN, 2 * X_Q), F32)],
        scratch_shapes=[pltpu.VMEM((BLK, 2 * A_KV), F32)],
        name=name, compiler_params=_cp("arbitrary"))(proj, proj, bias, sinks, memkv, dmix, *deps)


def _neumann(pw, rhs):
    nh = len(pw)
    x = rhs
    for lvl in range(6):
        if lvl < 5:
            prod = [_mmf(pw[h], jnp.concatenate([x[h], pw[h]], axis=1)) for h in range(nh)]
            x = [x[h] + prod[h][:, :B_HD] for h in range(nh)]
            pw = [t[:, B_HD:] for t in prod]
        else:
            x = [x[h] + _mmf(pw[h], x[h]) for h in range(nh)]
    return x


@jax.custom_vjp
def _tri_solve(pw, rhs):
    return _neumann(pw, rhs)


def _tri_solve_fwd(pw, rhs):
    x = _neumann(pw, rhs)
    return x, (pw, x)


def _tri_solve_bwd(res, dx):
    pw, x = res
    d_rhs = _neumann([t.T for t in pw], list(dx))
    return [_mmf_nt(d_rhs[h], x[h]) for h in range(len(pw))], d_rhs


_tri_solve.defvjp(_tri_solve_fwd, _tri_solve_bwd)


@jax.custom_vjp
def _tri_solved(pw, rhs, x):
    return x


def _tri_solved_fwd(pw, rhs, x):
    return x, (pw, x)


def _tri_solved_bwd(res, dx):
    d_pw, d_rhs = _tri_solve_bwd(res, dx)
    return d_pw, d_rhs, [jnp.zeros_like(t) for t in res[1]]


_tri_solved.defvjp(_tri_solved_fwd, _tri_solved_bwd)


def _dn_heads(yq, yk, yv, z, bl, al, a_log, dtb, ng, s0, solved=None):
    c = CHUNK
    nh = B_V_HEADS
    rep = B_V_HEADS // B_QK_HEADS
    r = lax.broadcasted_iota(jnp.int32, (c, c), 0)
    cc = lax.broadcasted_iota(jnp.int32, (c, c), 1)
    q = [_silu(t) for t in yq]
    k = [_silu(t) for t in yk]
    v = [_silu(t) for t in yv]
    q = [t * lax.rsqrt(jnp.sum(t * t, axis=-1, keepdims=True) + EPS) * (B_HD ** -0.5) for t in q]
    k = [t * lax.rsqrt(jnp.sum(t * t, axis=-1, keepdims=True) + EPS) for t in k]
    beta = [jax.nn.sigmoid(t) for t in bl]
    g = [-jnp.exp(a_log[h]) * jax.nn.softplus(al[h] + dtb[h]) for h in range(nh)]
    gb = [jnp.broadcast_to(t, (c, c)) for t in g]
    gc_col = [jnp.sum(jnp.where(cc <= r, t.T, 0.0), axis=1, keepdims=True) for t in gb]
    gc_row = [jnp.sum(jnp.where(r <= cc, t, 0.0), axis=0, keepdims=True) for t in gb]
    gc_last = [jnp.sum(t, axis=0, keepdims=True) for t in g]
    decay = [jnp.exp(jnp.where(r >= cc, gc_col[h] - gc_row[h], -jnp.inf)) for h in range(nh)]
    kq = [_mmf_nt(jnp.concatenate([k[h], q[h]], axis=0), k[h]) for h in range(B_QK_HEADS)]
    kk = [t[:c] for t in kq]
    qk = [t[c:] for t in kq]
    egc = [jnp.exp(t) for t in gc_col]
    both = [_mmf(jnp.concatenate([(beta[h] * egc[h]) * k[h // rep], q[h // rep] * egc[h]], axis=0), s0[h]) for h in range(nh)]
    rhs = [beta[h] * v[h] - both[h][:c] for h in range(nh)]
    qs0 = [t[c:] for t in both]
    pw = [-(beta[h] * kk[h // rep] * jnp.where(r > cc, decay[h], 0.0)) for h in range(nh)]
    delta = _tri_solve(pw, rhs) if solved is None else _tri_solved(pw, rhs, solved)
    last = [_mmf(jnp.concatenate([qk[h // rep] * decay[h], (k[h // rep] * jnp.exp(gc_last[h] - gc_col[h])).T], axis=0), delta[h])
            for h in range(nh)]
    out = [qs0[h] + last[h][:c] for h in range(nh)]
    s1 = [jnp.exp(gc_last[h]) * s0[h] + last[h][c:] for h in range(nh)]
    o = [t * lax.rsqrt(jnp.mean(t * t, axis=-1, keepdims=True) + EPS) * ng for t in out]
    return [o[h] * _silu(z[h]) for h in range(nh)], s1, delta


def _dn_conv(ext, w_ref):
    y = ext * w_ref[B_CONV - 1:B_CONV, :]
    for j in range(B_CONV - 1):
        y = y + w_ref[j:j + 1, :] * pltpu.roll(ext, B_CONV - 1 - j, 0)
    return y


def _dn_args(y, cur_ref, par_ref, ng_ref):
    nh = B_V_HEADS
    return ([y[:, h * B_HD:(h + 1) * B_HD] for h in range(B_QK_HEADS)],
            [y[:, B_QK + h * B_HD:B_QK + (h + 1) * B_HD] for h in range(B_QK_HEADS)],
            [y[:, 2 * B_QK + h * B_HD:2 * B_QK + (h + 1) * B_HD] for h in range(nh)],
            [cur_ref[:, BP_Z + h * B_HD:BP_Z + (h + 1) * B_HD] for h in range(nh)],
            [cur_ref[:, BP_GATE + h:BP_GATE + h + 1] for h in range(nh)],
            [cur_ref[:, BP_GATE + nh + h:BP_GATE + nh + h + 1] for h in range(nh)],
            [par_ref[:, h:h + 1] for h in range(nh)], [par_ref[:, nh + h:nh + h + 1] for h in range(nh)], ng_ref[...])


def _mix_b_fwd(proj, conv_w, par, ng, memkv, name):
    s = proj.shape[0]
    nc = s // CHUNK

    def body(cur_ref, prev_ref, w_ref, par_ref, ng_ref, memkv_ref, o_ref, st_ref, dl_ref, state_ref):
        n = pl.program_id(0)

        @pl.when(n == 0)
        def _():
            state_ref[...] = jnp.zeros_like(state_ref)

        prev = jnp.where(n > 0, prev_ref[...], 0.0)
        ext = jnp.concatenate([prev, cur_ref[:, :B_QKV]], axis=0)
        y = _dn_conv(ext, w_ref)[HALO:]
        s0 = [state_ref[hv] for hv in range(B_V_HEADS)]
        st_ref[0] = state_ref[...]
        outs, s1, delta = _dn_heads(*_dn_args(y, cur_ref, par_ref, ng_ref), s0)
        for hv in range(B_V_HEADS):
            state_ref[hv] = s1[hv]
            dl_ref[0, hv] = delta[hv]
        outs = outs + [_cross_pairs(cur_ref[:, BP_XQ:BP_XQ + X_Q], memkv_ref[:, :X_Q], memkv_ref[:, X_Q:])]
        o_ref[...] = jnp.concatenate(outs, axis=1).astype(o_ref.dtype)

    return pl.pallas_call(
        body, grid=(nc,),
        in_specs=[pl.BlockSpec((CHUNK, IN_BP), lambda n: (n, 0)),
                  pl.BlockSpec((HALO, B_QKV), lambda n: (jnp.maximum(n * (CHUNK // HALO) - 1, 0), 0)),
                  pl.BlockSpec((HALO, B_QKV), lambda n: (0, 0)),
                  pl.BlockSpec((1, 128), lambda n: (0, 0)), pl.BlockSpec((1, 128), lambda n: (0, 0)),
                  pl.BlockSpec((MEM_LEN, 2 * X_Q), lambda n: (0, 0))],
        out_specs=[pl.BlockSpec((CHUNK, D), lambda n: (n, 0)),
                   pl.BlockSpec((1, B_V_HEADS, B_HD, B_HD), lambda n: (n, 0, 0, 0)),
                   pl.BlockSpec((1, B_V_HEADS, CHUNK, B_HD), lambda n: (n, 0, 0, 0))],
        out_shape=[_SDS((s, D), _ACT), _SDS((nc, B_V_HEADS, B_HD, B_HD), F32), _SDS((nc, B_V_HEADS, CHUNK, B_HD), F32)],
        scratch_shapes=[pltpu.VMEM((B_V_HEADS, B_HD, B_HD), F32)],
        name=name, compiler_params=_cp("arbitrary"))(proj, proj, conv_w, par, ng, memkv)


def _mix_b_bwd(proj, conv_w, par, ng, memkv, states, deltas, dmix, name):
    s = proj.shape[0]
    nc = s // CHUNK
    ext_rows = CHUNK + HALO

    def body(cur_ref, prev_ref, w_ref, par_ref, ng_ref, memkv_ref, st_ref, dl_ref, dmix_ref,
             dproj_ref, dw_ref, dpar_ref, dng_ref, dmemkv_ref, dstate_ref, carry_ref):
        t = pl.program_id(0)
        n = nc - 1 - t

        @pl.when(t == 0)
        def _():
            dstate_ref[...] = jnp.zeros_like(dstate_ref)
            carry_ref[...] = jnp.zeros_like(carry_ref)
            dw_ref[...] = jnp.zeros_like(dw_ref)
            dpar_ref[...] = jnp.zeros_like(dpar_ref)
            dng_ref[...] = jnp.zeros_like(dng_ref)
            dmemkv_ref[...] = jnp.zeros_like(dmemkv_ref)

        lane = lax.broadcasted_iota(jnp.int32, (1, 128), 1)
        prev = jnp.where(n > 0, prev_ref[...], 0.0)
        ext = jnp.concatenate([prev, cur_ref[:, :B_QKV]], axis=0)
        y = _dn_conv(ext, w_ref)[HALO:]
        solved = [dl_ref[0, hv] for hv in range(B_V_HEADS)]
        _, vjp = jax.vjp(functools.partial(_dn_heads, solved=solved), *_dn_args(y, cur_ref, par_ref, ng_ref),
                         [st_ref[0, hv] for hv in range(B_V_HEADS)])
        dyq, dyk, dyv, dz, gbl, gal, ga_log, gdtb, dng, gs0 = vjp(
            ([dmix_ref[:, hv * B_HD:(hv + 1) * B_HD].astype(F32) for hv in range(B_V_HEADS)],
             [dstate_ref[hv] for hv in range(B_V_HEADS)], [jnp.zeros((CHUNK, B_HD), F32)] * B_V_HEADS))
        dgate = jnp.zeros((CHUNK, 128), F32)
        dpar = jnp.zeros((1, 128), F32)
        for hv in range(B_V_HEADS):
            dstate_ref[hv] = gs0[hv]
            dgate = dgate + jnp.where(lane == hv, gbl[hv], 0.0) + jnp.where(lane == B_V_HEADS + hv, gal[hv], 0.0)
            dpar = dpar + jnp.where(lane == hv, ga_log[hv], 0.0) + jnp.where(lane == B_V_HEADS + hv, gdtb[hv], 0.0)
        dpar_ref[...] += dpar
        dng_ref[...] += dng
        _, vjp = jax.vjp(_cross_pairs, cur_ref[:, BP_XQ:BP_XQ + X_Q], memkv_ref[:, :X_Q], memkv_ref[:, X_Q:])
        dxq, dmk, dmv = vjp(dmix_ref[:, B_V:].astype(F32))
        dmemkv_ref[...] += jnp.concatenate([dmk, dmv], axis=1)
        dy = jnp.concatenate(list(dyq) + list(dyk) + list(dyv), axis=1)
        dy_ext = jnp.concatenate([jnp.zeros((HALO, B_QKV), F32), dy], axis=0)
        dext = dy_ext * w_ref[B_CONV - 1:B_CONV, :]
        dw_ref[B_CONV - 1:B_CONV, :] += jnp.sum(ext * dy_ext, axis=0, keepdims=True)
        for j in range(B_CONV - 1):
            sh = B_CONV - 1 - j
            dw_ref[j:j + 1, :] += jnp.sum(pltpu.roll(ext, sh, 0) * dy_ext, axis=0, keepdims=True)
            dext = dext + w_ref[j:j + 1, :] * pltpu.roll(dy_ext, ext_rows - sh, 0)
        tail = jnp.concatenate([jnp.zeros((CHUNK - HALO, B_QKV), F32), carry_ref[...]], axis=0)
        dqkv = dext[HALO:] + tail
        carry_ref[...] = dext[:HALO]
        dproj_ref[...] = jnp.concatenate([dqkv] + list(dz) + [dxq, dgate], axis=1).astype(dproj_ref.dtype)

    return pl.pallas_call(
        body, grid=(nc,),
        in_specs=[pl.BlockSpec((CHUNK, IN_BP), lambda t: (nc - 1 - t, 0)),
                  pl.BlockSpec((HALO, B_QKV), lambda t: (jnp.maximum((nc - 1 - t) * (CHUNK // HALO) - 1, 0), 0)),
                  pl.BlockSpec((HALO, B_QKV), lambda t: (0, 0)),
                  pl.BlockSpec((1, 128), lambda t: (0, 0)), pl.BlockSpec((1, 128), lambda t: (0, 0)),
                  pl.BlockSpec((MEM_LEN, 2 * X_Q), lambda t: (0, 0)),
                  pl.BlockSpec((1, B_V_HEADS, B_HD, B_HD), lambda t: (nc - 1 - t, 0, 0, 0)),
                  pl.BlockSpec((1, B_V_HEADS, CHUNK, B_HD), lambda t: (nc - 1 - t, 0, 0, 0)),
                  pl.BlockSpec((CHUNK, D), lambda t: (nc - 1 - t, 0))],
        out_specs=[pl.BlockSpec((CHUNK, IN_BP), lambda t: (nc - 1 - t, 0)),
                   pl.BlockSpec((HALO, B_QKV), lambda t: (0, 0)),
                   pl.BlockSpec((1, 128), lambda t: (0, 0)), pl.BlockSpec((1, 128), lambda t: (0, 0)),
                   pl.BlockSpec((MEM_LEN, 2 * X_Q), lambda t: (0, 0))],
        out_shape=[_SDS((s, IN_BP), _ACT), _SDS((HALO, B_QKV), F32), _SDS((1, 128), F32), _SDS((1, 128), F32),
                   _SDS((MEM_LEN, 2 * X_Q), F32)],
        scratch_shapes=[pltpu.VMEM((B_V_HEADS, B_HD, B_HD), F32), pltpu.VMEM((HALO, B_QKV), F32)],
        name=name, compiler_params=_cp("arbitrary"))(proj, proj, conv_w, par, ng, memkv, states, deltas, dmix)


def _place():
    return lax.axis_index("x"), lax.axis_index("y"), lax.axis_index("c")


def _all_gather(shards, name):
    n = len(shards)

    def body(*refs):
        ins, outs = refs[:n], refs[n:2 * n]
        send_sems, recv_sems, local_sems = refs[2 * n:]
        x, y, c = _place()
        me, sibling = (x, y, c), (x, y, 1 - c)
        chips = [(1 - x, y), (x, 1 - y), (1 - x, 1 - y)]

        def rows(a, px, py, pc):
            return outs[a].at[4 * px + 2 * py + pc]

        def copy(a, k, block, to, src=None):
            return pltpu.make_async_remote_copy(
                src_ref=rows(a, *block) if src is None else src, dst_ref=rows(a, *block),
                send_sem=send_sems.at[a, k], recv_sem=recv_sems.at[a, k],
                device_id=to, device_id_type=pl.DeviceIdType.MESH)

        mine = [pltpu.make_async_copy(ins[a], rows(a, *me), local_sems.at[a]) for a in range(n)]
        for cp in mine:
            cp.start()
        first = []
        for a in range(n):
            first.append(copy(a, 0, me, sibling, src=ins[a]))
            first += [copy(a, 1 + j, me, (*chip, c), src=ins[a]) for j, chip in enumerate(chips)]
        for cp in first:
            cp.start()
        passed = []
        for j, chip in enumerate(chips):
            for a in range(n):
                copy(a, 1 + j, (*chip, c), me).wait_recv()
                fwd = copy(a, 4 + j, (*chip, c), sibling)
                fwd.start()
                passed.append(fwd)
        for a in range(n):
            copy(a, 0, sibling, me).wait_recv()
            for j, chip in enumerate(chips):
                copy(a, 4 + j, (*chip, 1 - c), me).wait_recv()
        for cp in first + passed:
            cp.wait_send()
        for cp in mine:
            cp.wait()

    hbm = pl.BlockSpec(memory_space=pl.ANY)
    return pl.pallas_call(
        body, out_shape=[_SDS((N_DEV,) + s.shape, s.dtype) for s in shards],
        in_specs=[hbm] * n, out_specs=[hbm] * n,
        scratch_shapes=[pltpu.SemaphoreType.DMA((n, 7)), pltpu.SemaphoreType.DMA((n, 7)), pltpu.SemaphoreType.DMA((n,))],
        name=name)(*shards)


class _Exchange:
    def __init__(self, lands, srcs):
        self.lands, self.srcs = lands, srcs


def _seq_exchange(srcs, land_shapes, plan, name, cid):
    n, nl = len(srcs), len(land_shapes)

    def launch(*refs):
        src_refs, land_refs = refs[:n], refs[n:n + nl]
        send_sems, recv_sems, local_sems = refs[n + nl:]
        x, y, c = _place()
        my = 4 * x + 2 * y + c
        peers = [(x ^ ((k + 1) >> 2 & 1), y ^ ((k + 1) >> 1 & 1), c ^ ((k + 1) & 1)) for k in range(N_DEV - 1)]
        barrier = pltpu.get_barrier_semaphore()
        for p in peers:
            pl.semaphore_signal(barrier, inc=1, device_id=p, device_id_type=pl.DeviceIdType.MESH)
        pl.semaphore_wait(barrier, N_DEV - 1)

        def src_for(a, dest):
            return src_refs[a].at[dest] if plan[a][1] else src_refs[a]

        def slot(a, source):
            return land_refs[plan[a][0]].at[source]

        mine = [pltpu.make_async_copy(src_for(a, my), slot(a, my), local_sems.at[a]) for a in range(n)]
        for cp in mine:
            cp.start()
        sends, recvs = [], []
        for k, (px, py, pc) in enumerate(peers):
            peer = 4 * px + 2 * py + pc
            for a in range(n):
                kw = dict(send_sem=send_sems.at[a * (N_DEV - 1) + k], recv_sem=recv_sems.at[a * (N_DEV - 1) + k],
                          device_id=(px, py, pc), device_id_type=pl.DeviceIdType.MESH)
                sends.append(pltpu.make_async_remote_copy(src_ref=src_for(a, peer), dst_ref=slot(a, my), **kw))
                recvs.append(pltpu.make_async_remote_copy(src_ref=src_for(a, my), dst_ref=slot(a, peer), **kw))
        for cp in sends:
            cp.start()
        for cp in recvs:
            cp.wait_recv()
        for cp in sends:
            cp.wait_send()
        for cp in mine:
            cp.wait()

    lands = pl.kernel(
        launch, out_type=[_SDS(s, d) for s, d in land_shapes],
        mesh=plsc.ScalarSubcoreMesh(axis_name="sequencer", num_cores=1), name=name,
        scratch_types=(pltpu.SemaphoreType.DMA((n * (N_DEV - 1),)), pltpu.SemaphoreType.DMA((n * (N_DEV - 1),)),
                       pltpu.SemaphoreType.DMA((n,))),
        compiler_params=pltpu.CompilerParams(collective_id=cid))(*srcs)
    return _Exchange(list(lands), list(srcs))


def _adam_update(g, w, m, v):
    c1 = 1.0 - ADAM_B1 ** ADAM_STEP
    c2 = 1.0 - ADAM_B2 ** ADAM_STEP
    mm = ADAM_B1 * m + (1.0 - ADAM_B1) * g
    vv = ADAM_B2 * v + (1.0 - ADAM_B2) * (g * g)
    delta = -ADAM_LR * ((mm / c1) / (jnp.sqrt(vv / c2) + ADAM_EPS) + ADAM_WD * w)
    return delta, mm, vv


def _sum_sources(p_ref):
    g = p_ref[0].astype(F32)
    for s in range(1, N_DEV):
        g = g + p_ref[s].astype(F32)
    return g


def _adamw(parts, w, m, v, tr, name, restore_b=False, deps=()):
    nl, r, c = w.shape
    cp = parts[0].shape[-1]

    def body(*refs):
        p_refs = refs[:nl]
        w_ref, m_ref, v_ref = refs[nl:nl + 3]
        g_ref, d_ref, nm_ref, nv_ref = refs[-4:]
        g = _sum_sources(p_refs[0])
        for l in range(1, nl):
            g = jnp.where(pl.program_id(0) == l, _sum_sources(p_refs[l]), g)
        if restore_b:
            g = jnp.concatenate([g[:, :BP_XQ], g[:, BP_GATE:BP_GATE + 2 * B_V_HEADS], g[:, BP_XQ:BP_GATE]], axis=1)
        delta, mm, vv = _adam_update(g, w_ref[...], m_ref[...], v_ref[...])
        g_ref[...] = g
        d_ref[...] = delta
        nm_ref[...] = mm
        nv_ref[...] = vv

    spec = pl.BlockSpec((None, tr, c), lambda l, i: (l, i, 0))
    part_specs = [pl.BlockSpec((N_DEV, tr, cp), functools.partial(lambda l, i, k: (0, jnp.where(l == k, i, 0), 0), k=k))
                  for k in range(nl)]
    return pl.pallas_call(
        body, grid=(nl, r // tr),
        in_specs=part_specs + [spec, spec, spec] + _dep_specs(deps),
        out_specs=[spec] * 4, out_shape=[_SDS(w.shape, F32)] * 4,
        name=name, compiler_params=_cp("arbitrary", "arbitrary"))(*parts, w, m, v, *deps)


def _pack_small(d_rel, d_cb, d_cw, d_qkv, d_mix, d_mem, d_ffn, d_final, d_sinks, d_par, d_ng, loss_row, name):
    flat = [d_rel, *d_cb, *d_cw, d_qkv, *d_mix, *d_mem, *d_ffn, d_final, d_sinks, d_par, d_ng, loss_row]
    n = len(flat)

    def body(*refs):
        ins, o_ref = refs[:n], refs[n]
        rel, cb0, cb1, cw0, cw1, qkv, mx0, mx1, me0, me1, ff0, ff1, fin, snk, par, ng, lss = ins
        o_ref[...] = jnp.zeros_like(o_ref)
        for k in range(N_BUCKETS):
            lane = SP_REL_LANE + 128 * (k % 8)
            o_ref[SP_QKV + k // 8:SP_QKV + k // 8 + 1, lane:lane + 128] = rel[k:k + 1, :]
        for l, (cb, cw) in enumerate(((cb0, cw0), (cb1, cw1))):
            o_ref[SP_CB + l:SP_CB + l + 1, :] = jnp.concatenate([cb[j] for j in range(FF_BLOCKS)], axis=1)
            full = jnp.concatenate([cw[j] for j in range(FF_BLOCKS)], axis=1)
            o_ref[SP_CW + FFN_CONV * l:SP_CW + FFN_CONV * (l + 1), :] = full[:FFN_CONV]
        o_ref[SP_QKV:SP_QKV + B_CONV, 0:B_QKV] = qkv[0:B_CONV, :]
        for base, pair in ((SP_MIX, (mx0, mx1)), (SP_MEM, (me0, me1)), (SP_FFN, (ff0, ff1))):
            for l in range(2):
                o_ref[base + l:base + l + 1, 0:D] = pair[l][...]
        o_ref[SP_FINAL:SP_FINAL + 1, 0:D] = fin[...]
        o_ref[SP_MISC:SP_MISC + 1, 0:128] = snk[...]
        o_ref[SP_MISC:SP_MISC + 1, 128:256] = par[...]
        o_ref[SP_MISC:SP_MISC + 1, 256:384] = ng[...]
        o_ref[SP_MISC:SP_MISC + 1, 384:512] = lss[...]

    vm = pl.BlockSpec(memory_space=pltpu.VMEM)
    return pl.pallas_call(body, in_specs=[vm] * n, out_specs=vm, out_shape=_SDS((SMALL_ROWS, D_FF), F32), name=name)(*flat)


_SMALL = ["rel_bias", "norm_mix_g", "norm_mem_g", "sinks_a", "a_log_b", "dt_bias_b", "out_norm_g_b", "norm_ffn_g",
          "ffn_conv_b", "final_norm_g", "conv_qkv_b", "ffn_conv_w"]


def _adamw_small(recv, rc_qkv, rc_ffn, ws, ms, vs, name, deps=()):
    n = len(_SMALL)

    def body(*refs):
        recv_ref, qkv_ref, ffn_ref = refs[:3]
        w_refs, m_refs, v_refs = refs[3:3 + n], refs[3 + n:3 + 2 * n], refs[3 + 2 * n:3 + 3 * n]
        outs, loss_ref = refs[len(refs) - 4 * n - 1:len(refs) - 1], refs[-1]
        gs = _sum_sources(recv_ref)
        loss_ref[...] = gs[SP_MISC:SP_MISC + 1, 384:512]
        grads = {
            "rel_bias": jnp.concatenate(
                [gs[SP_QKV + k // 8:SP_QKV + k // 8 + 1, SP_REL_LANE + 128 * (k % 8):SP_REL_LANE + 128 * (k % 8) + A_HEADS]
                 for k in range(N_BUCKETS)], axis=0),
            "norm_mix_g": gs[SP_MIX:SP_MIX + 2, 0:D], "norm_mem_g": gs[SP_MEM:SP_MEM + 2, 0:D],
            "sinks_a": gs[SP_MISC:SP_MISC + 1, 0:A_HEADS],
            "a_log_b": gs[SP_MISC:SP_MISC + 1, 128:128 + B_V_HEADS],
            "dt_bias_b": gs[SP_MISC:SP_MISC + 1, 128 + B_V_HEADS:128 + 2 * B_V_HEADS],
            "out_norm_g_b": gs[SP_MISC:SP_MISC + 1, 256:256 + B_HD],
            "norm_ffn_g": gs[SP_FFN:SP_FFN + 2, 0:D], "ffn_conv_b": gs[SP_CB:SP_CB + 2, :],
            "final_norm_g": gs[SP_FINAL:SP_FINAL + 1, 0:D],
            "conv_qkv_b": _sum_sources(qkv_ref), "ffn_conv_w": _sum_sources(ffn_ref),
        }
        for i, nm in enumerate(_SMALL):
            g = grads[nm]
            delta, mm, vv = _adam_update(g, w_refs[i][...], m_refs[i][...], v_refs[i][...])
            outs[i][...] = g
            outs[n + i][...] = delta
            outs[2 * n + i][...] = mm
            outs[3 * n + i][...] = vv

    vm = pl.BlockSpec(memory_space=pltpu.VMEM)
    shapes = [_SDS(w.shape, F32) for w in ws]
    return pl.pallas_call(
        body, in_specs=[vm] * (3 + 3 * n) + _dep_specs(deps), out_specs=[vm] * (4 * n + 1),
        out_shape=shapes * 4 + [_SDS((1, 128), F32)],
        name=name)(recv, rc_qkv, rc_ffn, *ws, *ms, *vs, *deps)


def _assemble(gathered, axis):
    g = jnp.moveaxis(gathered, 0, axis)
    shp = list(g.shape)
    return g.reshape(shp[:axis] + [shp[axis] * shp[axis + 1]] + shp[axis + 2:])


def _pad_rows(a, rows):
    return jnp.pad(a, ((0, rows - a.shape[0]), (0, 0)))


def _pad_lanes(a, lanes=128):
    return jnp.pad(a, ((0, 0), (0, lanes - a.shape[1])))


def _ff_blocks(a):
    return jnp.moveaxis(a.reshape(a.shape[0], FF_BLOCKS, GU_SHARD), 1, 0)


def _reorder_b(w):
    qkv_z = w[..., :B_QKV + B_V]
    gates = w[..., B_QKV + B_V:B_QKV + B_V + 2 * B_V_HEADS]
    xq = w[..., IN_B - X_Q:]
    pad = jnp.zeros(w.shape[:-1] + (IN_BP - IN_B,), w.dtype)
    return jnp.concatenate([qkv_z, xq, gates, pad], axis=-1)


def kernel(x, mem, rel_bias, norm_mix_g, norm_mem_g, w_mem_kv, w_out, w_in_a, sinks_a, w_in_b, conv_qkv_b, a_log_b, dt_bias_b, out_norm_g_b, norm_ffn_g, w_gate_up, ffn_conv_w, ffn_conv_b, w_down, final_norm_g, loss_target, m_rel_bias, m_norm_mix_g, m_norm_mem_g, m_w_mem_kv, m_w_out, m_w_in_a, m_sinks_a, m_w_in_b, m_conv_qkv_b, m_a_log_b, m_dt_bias_b, m_out_norm_g_b, m_norm_ffn_g, m_w_gate_up, m_ffn_conv_w, m_ffn_conv_b, m_w_down, m_final_norm_g, v_rel_bias, v_norm_mix_g, v_norm_mem_g, v_w_mem_kv, v_w_out, v_w_in_a, v_sinks_a, v_w_in_b, v_conv_qkv_b, v_a_log_b, v_dt_bias_b, v_out_norm_g_b, v_norm_ffn_g, v_w_gate_up, v_ffn_conv_w, v_ffn_conv_b, v_w_down, v_final_norm_g):
    local = dict(locals())
    order = ["rel_bias", "norm_mix_g", "norm_mem_g", "w_mem_kv", "w_out", "w_in_a", "sinks_a", "w_in_b", "conv_qkv_b",
             "a_log_b", "dt_bias_b", "out_norm_g_b", "norm_ffn_g", "w_gate_up", "ffn_conv_w", "ffn_conv_b", "w_down",
             "final_norm_g"]
    wts = {n: local[n] for n in order}
    moms = {n: local["m_" + n] for n in order}
    vars_ = {n: local["v_" + n] for n in order}
    h0 = x[0]
    memx = mem[0]
    tgt = loss_target[0]
    s = h0.shape[0]
    tm = _rows(s)
    tb = min(s, _TM_BIG)

    t_ = lambda a: jnp.swapaxes(a, 1, 2)
    g_mk, g_out, g_ia, g_cq, g_cw = _all_gather(
        [w_mem_kv.astype(_MXU), w_out.astype(_MXU), t_(w_in_a).astype(_MXU), conv_qkv_b, ffn_conv_w], "gather_first")
    gu_land = ((N_DEV, GU_SHARD, D), _MXU)
    dn_land = ((N_DEV, DN_SHARD, D), _MXU)
    whole = [(0, False), (1, False)]
    def after(a, b):
        return a + (b[(0,) * b.ndim] * 0).astype(a.dtype)

    ffn0_w = _seq_exchange([after(t_(w_gate_up)[0].astype(_MXU), g_ia), after(w_down[0].astype(_MXU), g_ia)], [gu_land, dn_land],
                           whole, "gather_ffn0", 1)
    w_ia = g_ia.reshape(IN_A, D)
    conv_qkv = _pad_rows(_assemble(g_cq, 2)[0], HALO)
    ffn_cw_full = _assemble(g_cw, 2)
    ffn_cw = [_ff_blocks(_pad_rows(ffn_cw_full[i], HALO)) for i in range(2)]
    ffn_cb = [_ff_blocks(ffn_conv_b[i:i + 1]) for i in range(2)]
    bucket = jnp.asarray(_bucket_table())
    bias = _bias_build(rel_bias, bucket, "bias_build")
    sinks = _pad_lanes(sinks_a)
    par_b = _pad_lanes(jnp.concatenate([a_log_b, dt_bias_b], axis=1))

    row_x = pl.BlockSpec((tm, D), lambda i, j: (i, 0))
    gu_shape = (2, FF_BLOCKS, s, GU_SHARD)

    def in_proj(h, g, w, w_spec, n_cols, tn, name, deps=(), out_dtype=F32, w_t=False):
        return _norm_matmul(h, g, w, w_spec, n_cols // tn, (h.shape[0], n_cols),
                            pl.BlockSpec((_rows(h.shape[0]), tn), lambda i, j: (i, j)), name, deps=deps, out_dtype=out_dtype,
                            w_t=w_t)

    def ffn_fwd(i, h, g_gu, g_dn, deps=()):
        gu, hn = _norm_matmul(h, norm_ffn_g[i:i + 1], g_gu, _spec_gate_up(1), N_DEV, gu_shape,
                              _spec_gu_act(0, 1, tb), f"gate_up_{i}", deps=deps, out_dtype=_ACT, w_t=True, tm=tb)
        h_new, act = _glu_down(gu, ffn_cw[i], ffn_cb[i], g_dn, h, f"glu_down_{i}")
        return h_new, gu, hn, act

    def out_proj(i, mix, h):
        return _matmul_res(mix, row_x, g_out, _spec_rowsharded(i, D // N_DEV, D), 1, h, f"out_proj_{i}")

    proj_a, hn_a = in_proj(h0, norm_mix_g[0:1], w_ia, pl.BlockSpec((640, D), lambda i, j: (j, 0)), IN_A, 640, "in_proj_a",
                           deps=ffn0_w.srcs, out_dtype=_ACT, w_t=True)
    memkv0, memn0 = in_proj(memx, norm_mem_g[0:1], g_mk, _spec_rowsharded(0, D // N_DEV, 2 * X_Q), 2 * X_Q, 2 * X_Q, "mem_proj_0")
    mix_a = _mix_a_fwd(proj_a, bias, sinks, memkv0, "mix_a_fwd")
    h1 = out_proj(0, mix_a, h0)
    g_gu0, g_dn0 = ffn0_w.lands
    in_b_w = _seq_exchange([after(_reorder_b(w_in_b).astype(_MXU), h1)], [((N_DEV, 1, D // N_DEV, IN_BP), _MXU)], [(0, False)],
                           "gather_in_b", 2)
    ffn1_w = _seq_exchange([after(t_(w_gate_up)[1].astype(_MXU), h1), after(w_down[1].astype(_MXU), h1)], [gu_land, dn_land], whole,
                           "gather_ffn1", 3)
    h2, gu0, hn_f0, act0 = ffn_fwd(0, h1, g_gu0, g_dn0, deps=in_b_w.srcs + ffn1_w.srcs)
    g_ib, = in_b_w.lands
    proj_b, hn_b = in_proj(h2, norm_mix_g[1:2], g_ib, _spec_rowsharded(0, D // N_DEV, 896, col_block=1), IN_BP, 896, "in_proj_b")
    memkv1, memn1 = in_proj(memx, norm_mem_g[1:2], g_mk, _spec_rowsharded(1, D // N_DEV, 2 * X_Q), 2 * X_Q, 2 * X_Q, "mem_proj_1")
    mix_b, states, deltas = _mix_b_fwd(proj_b, conv_qkv, par_b, out_norm_g_b, memkv1, "mix_b_fwd")
    h3 = out_proj(1, mix_b, h2)
    g_gu1, g_dn1 = ffn1_w.lands
    h4, gu1, hn_f1, act1 = ffn_fwd(1, h3, g_gu1, g_dn1)
    loss_row, dh, d_final_g = _loss_head(h4, final_norm_g[None, :], tgt, "loss_head")

    zeros_mem = jnp.zeros_like(memx)
    per_dest2 = [(0, True), (1, True)]

    def ffn_bwd(i, dh, h_in, gu, hn_f, act, g_gu, g_dn, deps=()):
        dgu, d_cw, d_cb = _glu_bwd(gu, ffn_cw[i], ffn_cb[i], dh, g_dn, f"glu_bwd_{i}", deps=deps)
        d_wdown = _matmul_tn(act, pl.BlockSpec((None, tm, GU_SHARD), lambda j, r: (j, r, 0)),
                             dh, pl.BlockSpec((tm, D), lambda j, r: (r, 0)), s, FF_BLOCKS, (GU_SHARD, D),
                             (N_DEV, DN_SHARD, D), pl.BlockSpec((2, DN_SHARD, D), lambda j, r: (j, 0, 0)), f"d_w_down_{i}")
        dh_new, d_g = _matmul_nt_normbwd(dgu, _spec_gu_act(0, 1, tm), g_gu, _spec_gate_up(1), N_DEV, h_in,
                                         norm_ffn_g[i:i + 1], dh, f"d_ffn_in_{i}", w_t=True)
        d_wgu = _matmul_tn(dgu, _spec_gu_act(1, 0, tb), hn_f, pl.BlockSpec((tb, D), lambda j, r: (r, 0)), s, N_DEV,
                           (GU_SHARD, D), (N_DEV, GU_SHARD, D), pl.BlockSpec((None, GU_SHARD, D), lambda j, r: (j, 0, 0)),
                           f"d_w_gate_up_{i}", tm=tb)
        return dh_new, [d_wdown, d_wgu], d_cw, d_cb, d_g

    def out_bwd(i, dh, mix, deps):
        dmix = _matmul_nt(dh, g_out, _spec_rowsharded(i, D // N_DEV, D), 1, (s, D), row_x, f"d_mix_{i}", deps=deps, out_dtype=_ACT)
        d_wout = _matmul_tn(mix, pl.BlockSpec((tm, D), lambda j, r: (r, 0)), dh, pl.BlockSpec((tm, D), lambda j, r: (r, 0)),
                            s, 1, (D, D), (N_DEV, D // N_DEV, D), pl.BlockSpec((N_DEV, D // N_DEV, D), lambda j, r: (0, 0, 0)),
                            f"d_w_out_{i}")
        return dmix, d_wout

    def mem_bwd(i, dmemkv, memn):
        tmm = _rows(MEM_LEN)
        _, d_g = _matmul_nt_normbwd(dmemkv, pl.BlockSpec((tmm, 2 * X_Q), lambda r, j: (r, 0)), g_mk,
                                    _spec_rowsharded(i, D // N_DEV, 2 * X_Q), 1, memx, norm_mem_g[i:i + 1], zeros_mem,
                                    f"d_mem_in_{i}")
        by_row = lambda j, r: (r, 0)
        d_w = _matmul_tn(memn, pl.BlockSpec((tmm, D), by_row), dmemkv, pl.BlockSpec((tmm, 2 * X_Q), by_row), MEM_LEN, 1,
                         (D, 2 * X_Q), (N_DEV, D // N_DEV, 2 * X_Q),
                         pl.BlockSpec((N_DEV, D // N_DEV, 2 * X_Q), lambda j, r: (0, 0, 0)), f"d_w_mem_kv_{i}")
        return d_w, d_g

    out_land = ((N_DEV, D // N_DEV, D), _WIRE)
    mk_land = ((N_DEV, D // N_DEV, 2 * X_Q), _WIRE)
    ffn_lands = [((N_DEV, DN_SHARD, D), _WIRE), ((N_DEV, GU_SHARD, D), _WIRE)]
    dh, d_ffn1, d_cw1, d_cb1, d_gf1 = ffn_bwd(1, dh, h3, gu1, hn_f1, act1, g_gu1, g_dn1)
    ffn1_g = _seq_exchange(d_ffn1, ffn_lands, per_dest2, "send_ffn1_grads", 5)
    dmix, d_wout1 = out_bwd(1, dh, mix_b, ffn1_g.srcs)
    dproj_b, d_convw, d_par, d_ng, dmemkv1 = _mix_b_bwd(proj_b, conv_qkv, par_b, out_norm_g_b, memkv1, states, deltas, dmix, "mix_b_bwd")
    dh, d_gm1 = _matmul_nt_normbwd(dproj_b, pl.BlockSpec((tm, 896), lambda i, j: (i, j)), g_ib,
                                   _spec_rowsharded(0, D // N_DEV, 896, col_block=1), IN_BP // 896, h2, norm_mix_g[1:2], dh, "d_in_b")
    d_wib = _matmul_tn(hn_b, pl.BlockSpec((tm, D), lambda j, r: (r, 0)), dproj_b, pl.BlockSpec((tm, 896), lambda j, r: (r, j)),
                       s, IN_BP // 896, (D, 896), (N_DEV, D // N_DEV, IN_BP),
                       pl.BlockSpec((N_DEV, D // N_DEV, 896), lambda j, r: (0, 0, j)), "d_w_in_b")
    d_wmk1, d_gmem1 = mem_bwd(1, dmemkv1, memn1)
    mix1_g = _seq_exchange([d_wout1, d_wib, d_wmk1], [out_land, ((N_DEV, D // N_DEV, IN_BP), _WIRE), mk_land],
                           [(0, True), (1, True), (2, True)], "send_mix1_grads", 6)
    dh, d_ffn0, d_cw0, d_cb0, d_gf0 = ffn_bwd(0, dh, h1, gu0, hn_f0, act0, g_gu0, g_dn0, deps=mix1_g.srcs)
    dmix, d_wout0 = out_bwd(0, dh, mix_a, d_ffn0 + ffn1_g.lands[:1])
    ffn0_g = _seq_exchange(d_ffn0 + [d_wout0], ffn_lands + [out_land], per_dest2 + [(2, True)], "send_ffn0_grads", 4)
    dproj_a, dbias, dsinks, dmemkv0 = _mix_a_bwd(proj_a, bias, sinks, memkv0, dmix, "mix_a_bwd", deps=ffn0_g.srcs)
    dh, d_gm0 = _matmul_nt_normbwd(dproj_a, pl.BlockSpec((tm, 640), lambda i, j: (i, j)), w_ia,
                                   pl.BlockSpec((640, D), lambda i, j: (j, 0)), IN_A // 640, h0, norm_mix_g[0:1], dh, "d_in_a",
                                   w_t=True)
    d_wia = _matmul_tn(dproj_a, pl.BlockSpec((tm, IN_A), lambda j, r: (r, 0)), hn_a, pl.BlockSpec((tm, D), lambda j, r: (r, 0)),
                       s, 1, (IN_A, D), (N_DEV, IA_SHARD, D), pl.BlockSpec((N_DEV, IA_SHARD, D), lambda j, r: (0, 0, 0)),
                       "d_w_in_a")
    d_wmk0, d_gmem0 = mem_bwd(0, dmemkv0, memn0)
    d_rel = _bias_reduce(dbias, bucket, "bias_reduce")
    small = _pack_small(d_rel, (d_cb0, d_cb1), (d_cw0, d_cw1), d_convw, (d_gm0, d_gm1), (d_gmem0, d_gmem1),
                        (d_gf0, d_gf1), d_final_g, dsinks, d_par, d_ng, loss_row, "pack_small")
    mix0_g = _seq_exchange([d_wia, d_wmk0, small],
                           [((N_DEV, IA_SHARD, D), _WIRE), mk_land, ((N_DEV, SMALL_ROWS, D_FF), F32)],
                           [(0, True), (1, True), (2, False)], "send_mix0_grads", 7)

    res = {}
    last = []

    def update(nm, parts, tr, restore=False, transposed=False):
        view = t_ if transposed else (lambda a: a)
        out = _adamw(parts, view(wts[nm]), view(moms[nm]), view(vars_[nm]), tr, "adamw_" + nm, restore_b=restore, deps=last[-1:])
        res[nm] = [view(o) for o in out]
        last.append(out[1])

    r_dn1, r_gu1 = ffn1_g.lands
    r_dn0, r_gu0, r_out0 = ffn0_g.lands
    r_out1, r_ib, r_mk1 = mix1_g.lands
    update("w_gate_up", [r_gu0, r_gu1], 176, transposed=True)
    update("w_down", [r_dn0, r_dn1], 176)
    update("w_in_b", [r_ib], 32, True)
    r_ia, r_mk0, r_small = mix0_g.lands
    update("w_mem_kv", [r_mk0, r_mk1], 128)
    update("w_out", [r_out0, r_out1], 128)
    update("w_in_a", [r_ia], IA_SHARD, transposed=True)

    my = 4 * lax.axis_index("x") + 2 * lax.axis_index("y") + lax.axis_index("c")
    cq = conv_qkv_b.shape[-1]
    cf = ffn_conv_w.shape[-1]
    rc_qkv = lax.dynamic_slice_in_dim(r_small[:, SP_QKV:SP_QKV + B_CONV, :B_QKV], my * cq, cq, axis=2)[:, None]
    rc_ffn = lax.dynamic_slice_in_dim(r_small[:, SP_CW:SP_CW + 2 * FFN_CONV, :], my * cf, cf, axis=2).reshape(N_DEV, 2, FFN_CONV, cf)
    as2d = lambda a: a[None, :] if a.ndim == 1 else a
    small_out = _adamw_small(r_small, rc_qkv, rc_ffn, [as2d(wts[n]) for n in _SMALL], [as2d(moms[n]) for n in _SMALL],
                             [as2d(vars_[n]) for n in _SMALL], "adamw_small", deps=last[-1:])
    ns = len(_SMALL)
    for i, nm in enumerate(_SMALL):
        res[nm] = [small_out[k * ns + i].reshape(wts[nm].shape) for k in range(4)]

    return (small_out[-1][0, 0], dh[None], *[res[n][0] for n in order], *[res[n][1] for n in order],
            *[res[n][2] for n in order], *[res[n][3] for n in order])
```

```python
import functools
import math

import numpy as np

import jax
import jax.numpy as jnp
from jax import lax
from jax.experimental import pallas as pl
from jax.experimental.pallas import tpu as pltpu
from jax.experimental.pallas import tpu_sc as plsc

F32 = jnp.float32
_MXU = jnp.bfloat16
_ACT = jnp.bfloat16
_WIRE = jnp.bfloat16
_HI = lax.Precision.HIGH
_TM = 1024
_TM_GLU = 512
_TM_BIG = 2048
_VMEM_LIMIT = 48 * 1024 * 1024
_SDS = jax.ShapeDtypeStruct

D = 1024
EPS = 1e-6
A_HEADS, A_KV_HEADS, A_HD, BLK = 12, 2, 64, 128
N_BUCKETS, MAX_DISTANCE = 32, 128
B_QK_HEADS, B_V_HEADS, B_HD, B_CONV, CHUNK = 3, 6, 128, 4, 64
X_HEADS, X_HD, MEM_LEN = 4, 64, 256
D_FF, FFN_CONV = 2816, 3
A_Q, A_KV, X_Q = 768, 128, 256
B_QK, B_V, B_QKV = 384, 768, 1536
IN_A, IN_B = 1280, 2572
IN_BP = 2688
BP_Z, BP_XQ, BP_GATE = 1536, 2304, 2560
HALO = 8
GLU_HALO = 16

N_DEV = 8
GU_SHARD = 2 * D_FF // N_DEV
FF_BLOCKS = D_FF // GU_SHARD
DN_SHARD = D_FF // N_DEV
IA_SHARD = IN_A // N_DEV

ADAM_LR, ADAM_B1, ADAM_B2, ADAM_EPS, ADAM_WD, ADAM_STEP = 0.001, 0.9, 0.999, 1e-08, 0.01, 10

SP_CB, SP_CW, SP_QKV, SP_MIX, SP_MEM, SP_FFN, SP_FINAL, SP_MISC, SMALL_ROWS = 0, 2, 8, 12, 14, 16, 18, 19, 24
SP_REL_LANE = B_QKV


def _cp(*sems):
    return pltpu.CompilerParams(dimension_semantics=sems, vmem_limit_bytes=_VMEM_LIMIT)


def _mm(a, b):
    return jnp.dot(a.astype(_MXU), b.astype(_MXU), preferred_element_type=F32)


def _mm_nt(a, b):
    return lax.dot_general(a.astype(_MXU), b.astype(_MXU), (((1,), (1,)), ((), ())), preferred_element_type=F32)


def _mm_tn(a, b):
    return lax.dot_general(a.astype(_MXU), b.astype(_MXU), (((0,), (0,)), ((), ())), preferred_element_type=F32)


def _mmf(a, b):
    return jnp.dot(a, b, preferred_element_type=F32, precision=_HI)


def _mmf_nt(a, b):
    return lax.dot_general(a, b, (((1,), (1,)), ((), ())), preferred_element_type=F32, precision=_HI)


def _mmf_tn(a, b):
    return lax.dot_general(a, b, (((0,), (0,)), ((), ())), preferred_element_type=F32, precision=_HI)


def _silu(x):
    return x * jax.nn.sigmoid(x)


def _w2d(ref):
    v = ref[...]
    return v.reshape(-1, v.shape[-1])


def _rows(m):
    return min(m, _TM)


def _spec_rowsharded(layer, rows, cols, col_block=None):
    if col_block is None:
        return pl.BlockSpec((N_DEV, None, rows, cols), lambda *_: (0, layer, 0, 0))
    return pl.BlockSpec((N_DEV, None, rows, cols), lambda *ids: (0, layer, 0, ids[col_block]))


def _spec_gate_up(axis):
    return pl.BlockSpec((None, GU_SHARD, D), lambda *ids: (ids[axis], 0, 0))


def _spec_down(axis):
    return pl.BlockSpec((2, DN_SHARD, D), lambda *ids: (ids[axis], 0, 0))


def _dep_specs(deps):
    return [pl.BlockSpec(memory_space=pl.ANY) for d in deps]


def _spec_gu_act(row_axis, axis, tm):
    return pl.BlockSpec((None, None, tm, GU_SHARD), lambda *ids: (ids[axis] // FF_BLOCKS, ids[axis] % FF_BLOCKS, ids[row_axis], 0))


def _norm_matmul(x, g, w, w_spec, n_blocks, out_shape, out_spec, name, deps=(), out_dtype=F32, w_t=False, tm=None):
    m, k = x.shape
    tm = tm or _rows(m)

    def body(x_ref, g_ref, w_ref, *rest):
        y_ref, hn_ref = rest[-2:]

        @pl.when(pl.program_id(1) == 0)
        def _():
            xv = x_ref[...]
            r = lax.rsqrt(jnp.mean(xv * xv, axis=-1, keepdims=True) + EPS)
            hn_ref[...] = (xv * r * g_ref[...]).astype(hn_ref.dtype)

        y_ref[...] = (_mm_nt if w_t else _mm)(hn_ref[...], _w2d(w_ref)).astype(y_ref.dtype)

    return pl.pallas_call(
        body, grid=(m // tm, n_blocks),
        in_specs=[pl.BlockSpec((tm, k), lambda i, j: (i, 0)), pl.BlockSpec((1, k), lambda i, j: (0, 0)), w_spec]
        + _dep_specs(deps),
        out_specs=[out_spec, pl.BlockSpec((tm, k), lambda i, j: (i, 0))],
        out_shape=[_SDS(out_shape, out_dtype), _SDS((m, k), _ACT)],
        name=name, compiler_params=_cp("arbitrary", "arbitrary"))(x, g, w, *deps)


def _matmul_res(a, a_spec, w, w_spec, n_k, res, name):
    m, n = res.shape
    tm = _rows(m)

    def body(a_ref, w_ref, r_ref, o_ref):
        part = _mm(a_ref[...], _w2d(w_ref))

        @pl.when(pl.program_id(1) == 0)
        def _():
            o_ref[...] = r_ref[...] + part

        @pl.when(pl.program_id(1) > 0)
        def _():
            o_ref[...] += part

    return pl.pallas_call(
        body, grid=(m // tm, n_k),
        in_specs=[a_spec, w_spec, pl.BlockSpec((tm, n), lambda i, j: (i, 0))],
        out_specs=pl.BlockSpec((tm, n), lambda i, j: (i, 0)),
        out_shape=_SDS((m, n), F32), name=name, compiler_params=_cp("arbitrary", "arbitrary"))(a, w, res)


def _matmul_nt(dy, w, w_spec, n_blocks, out_shape, out_spec, name, deps=(), out_dtype=F32):
    m, n = dy.shape
    tm = _rows(m)

    def body(dy_ref, w_ref, *rest):
        o_ref = rest[-1]
        o_ref[...] = _mm_nt(dy_ref[...], _w2d(w_ref)).astype(o_ref.dtype)

    return pl.pallas_call(
        body, grid=(m // tm, n_blocks),
        in_specs=[pl.BlockSpec((tm, n), lambda i, j: (i, 0)), w_spec] + _dep_specs(deps),
        out_specs=out_spec, out_shape=_SDS(out_shape, out_dtype),
        name=name, compiler_params=_cp("arbitrary", "arbitrary"))(dy, w, *deps)


def _matmul_nt_normbwd(dy, dy_spec, w, w_spec, nj, h, g, dh_in, name, w_t=False):
    m, k = h.shape
    tm = _rows(m)

    def body(dy_ref, w_ref, h_ref, g_ref, dhin_ref, dh_ref, dg_ref, acc_ref):
        i, j = pl.program_id(0), pl.program_id(1)

        @pl.when(j == 0)
        def _():
            acc_ref[...] = jnp.zeros_like(acc_ref)

        acc_ref[...] += (_mm if w_t else _mm_nt)(dy_ref[...], _w2d(w_ref))

        @pl.when(j == nj - 1)
        def _():
            xv = h_ref[...]
            r = lax.rsqrt(jnp.mean(xv * xv, axis=-1, keepdims=True) + EPS)
            xh = xv * r
            dhn = acc_ref[...]
            part = jnp.sum(dhn * xh, axis=0, keepdims=True)

            @pl.when(i == 0)
            def _():
                dg_ref[...] = part

            @pl.when(i > 0)
            def _():
                dg_ref[...] += part

            t = dhn * g_ref[...]
            dh_ref[...] = dhin_ref[...] + r * (t - xh * jnp.mean(t * xh, axis=-1, keepdims=True))

    return pl.pallas_call(
        body, grid=(m // tm, nj),
        in_specs=[dy_spec, w_spec, pl.BlockSpec((tm, k), lambda i, j: (i, 0)), pl.BlockSpec((1, k), lambda i, j: (0, 0)),
                  pl.BlockSpec((tm, k), lambda i, j: (i, 0))],
        out_specs=[pl.BlockSpec((tm, k), lambda i, j: (i, 0)), pl.BlockSpec((1, k), lambda i, j: (0, 0))],
        out_shape=[_SDS((m, k), F32), _SDS((1, k), F32)],
        scratch_shapes=[pltpu.VMEM((tm, k), F32)],
        name=name, compiler_params=_cp("arbitrary", "arbitrary"))(dy, w, h, g, dh_in)


def _matmul_tn(x, x_spec, dy, dy_spec, m, n_blocks, acc_shape, out_shape, out_spec, name, tm=None):
    tm = tm or _rows(m)
    nm = m // tm

    def body(x_ref, dy_ref, o_ref, acc_ref):
        @pl.when(pl.program_id(1) == 0)
        def _():
            acc_ref[...] = jnp.zeros_like(acc_ref)

        acc_ref[...] += _mm_tn(x_ref[...], dy_ref[...])

        @pl.when(pl.program_id(1) == nm - 1)
        def _():
            o_ref[...] = acc_ref[...].reshape(o_ref.shape).astype(o_ref.dtype)

    return pl.pallas_call(
        body, grid=(n_blocks, nm), in_specs=[x_spec, dy_spec], out_specs=out_spec,
        out_shape=_SDS(out_shape, _WIRE), scratch_shapes=[pltpu.VMEM(acc_shape, F32)],
        name=name, compiler_params=_cp("arbitrary", "arbitrary"))(x, dy)


def _loss_head(h, g, tgt, name):
    m, k = h.shape
    tm = _rows(m)

    def body(h_ref, g_ref, t_ref, loss_ref, dh_ref, dg_ref):
        i = pl.program_id(0)
        xv = h_ref[...]
        r = lax.rsqrt(jnp.mean(xv * xv, axis=-1, keepdims=True) + EPS)
        xh = xv * r
        gv = g_ref[...]
        err = xh * gv - t_ref[...]
        lpart = jnp.zeros((1, 128), F32) + 0.5 * jnp.sum(jnp.mean(err * err, axis=-1, keepdims=True), axis=0, keepdims=True)
        dy = err * (1.0 / k)
        gpart = jnp.sum(dy * xh, axis=0, keepdims=True)

        @pl.when(i == 0)
        def _():
            loss_ref[...] = lpart
            dg_ref[...] = gpart

        @pl.when(i > 0)
        def _():
            loss_ref[...] += lpart
            dg_ref[...] += gpart

        t = dy * gv
        dh_ref[...] = r * (t - xh * jnp.mean(t * xh, axis=-1, keepdims=True))

    return pl.pallas_call(
        body, grid=(m // tm,),
        in_specs=[pl.BlockSpec((tm, k), lambda i: (i, 0)), pl.BlockSpec((1, k), lambda i: (0, 0)),
                  pl.BlockSpec((tm, k), lambda i: (i, 0))],
        out_specs=[pl.BlockSpec((1, 128), lambda i: (0, 0)), pl.BlockSpec((tm, k), lambda i: (i, 0)),
                   pl.BlockSpec((1, k), lambda i: (0, 0))],
        out_shape=[_SDS((1, 128), F32), _SDS((m, k), F32), _SDS((1, k), F32)],
        name=name, compiler_params=_cp("arbitrary"))(h, g, tgt)


def _glu_down(gu, conv_w, conv_b, w_down, res, name):
    s = gu.shape[2]
    tm = min(s, _TM_GLU)

    def body(gu_ref, prev_ref, w_ref, b_ref, wdn_ref, r_ref, o_ref, act_ref):
        i, j = pl.program_id(0), pl.program_id(1)
        prev = jnp.where(i > 0, prev_ref[...].astype(F32), 0.0)
        ext = jnp.concatenate([prev, gu_ref[0].astype(F32)], axis=0)
        gc = b_ref[...] + w_ref[FFN_CONV - 1:FFN_CONV, :] * ext
        for k in range(FFN_CONV - 1):
            gc = gc + w_ref[k:k + 1, :] * pltpu.roll(ext, FFN_CONV - 1 - k, 0)
        act = (_silu(gc[GLU_HALO:]) * gu_ref[1].astype(F32)).astype(act_ref.dtype)
        act_ref[...] = act
        part = _mm(act, _w2d(wdn_ref))

        @pl.when(j == 0)
        def _():
            o_ref[...] = r_ref[...] + part

        @pl.when(j > 0)
        def _():
            o_ref[...] += part

    return pl.pallas_call(
        body, grid=(s // tm, FF_BLOCKS),
        in_specs=[pl.BlockSpec((2, None, tm, GU_SHARD), lambda i, j: (0, j, i, 0)),
                  pl.BlockSpec((None, None, GLU_HALO, GU_SHARD),
                               lambda i, j: (0, j, jnp.maximum(i * (tm // GLU_HALO) - 1, 0), 0)),
                  pl.BlockSpec((None, HALO, GU_SHARD), lambda i, j: (j, 0, 0)),
                  pl.BlockSpec((None, 1, GU_SHARD), lambda i, j: (j, 0, 0)),
                  _spec_down(1), pl.BlockSpec((tm, D), lambda i, j: (i, 0))],
        out_specs=[pl.BlockSpec((tm, D), lambda i, j: (i, 0)), pl.BlockSpec((None, tm, GU_SHARD), lambda i, j: (j, i, 0))],
        out_shape=[_SDS((s, D), F32), _SDS((FF_BLOCKS, s, GU_SHARD), _ACT)], name=name,
        compiler_params=_cp("arbitrary", "arbitrary"))(gu, gu, conv_w, conv_b, w_down, res)


def _glu_bwd(gu, conv_w, conv_b, dh, w_down, name, deps=()):
    s = gu.shape[2]
    tm = min(s, _TM_GLU)
    nt = s // tm
    ext_rows = tm + GLU_HALO

    def body(gu_ref, prev_ref, w_ref, b_ref, dh_ref, wdn_ref, *rest):
        dgu_ref, dw_ref, db_ref, carry_ref = rest[-4:]
        t = pl.program_id(1)
        i = nt - 1 - t

        @pl.when(t == 0)
        def _():
            carry_ref[...] = jnp.zeros_like(carry_ref)
            dw_ref[...] = jnp.zeros_like(dw_ref)
            db_ref[...] = jnp.zeros_like(db_ref)

        up = gu_ref[1].astype(F32)
        prev = jnp.where(i > 0, prev_ref[...].astype(F32), 0.0)
        ext = jnp.concatenate([prev, gu_ref[0].astype(F32)], axis=0)
        shifted = [pltpu.roll(ext, FFN_CONV - 1 - j, 0) if j < FFN_CONV - 1 else ext for j in range(FFN_CONV)]
        gc = b_ref[...] + shifted[0] * w_ref[0:1, :]
        for j in range(1, FFN_CONV):
            gc = gc + shifted[j] * w_ref[j:j + 1, :]
        gc = gc[GLU_HALO:]
        sg = jax.nn.sigmoid(gc)
        da = _mm_nt(dh_ref[...], _w2d(wdn_ref))
        dup = da * (gc * sg)
        dgc = da * up * (sg * (1.0 + gc * (1.0 - sg)))
        db_ref[...] += jnp.sum(dgc, axis=0, keepdims=True)
        dgc_ext = jnp.concatenate([jnp.zeros((GLU_HALO, GU_SHARD), F32), dgc], axis=0)
        dext = dgc_ext * w_ref[FFN_CONV - 1:FFN_CONV, :]
        for j in range(FFN_CONV):
            dw_ref[j:j + 1, :] += jnp.sum(shifted[j] * dgc_ext, axis=0, keepdims=True)
            if j < FFN_CONV - 1:
                dext = dext + w_ref[j:j + 1, :] * pltpu.roll(dgc_ext, ext_rows - (FFN_CONV - 1 - j), 0)
        tail = jnp.concatenate([jnp.zeros((tm - GLU_HALO, GU_SHARD), F32), carry_ref[...]], axis=0)
        dgate = dext[GLU_HALO:] + tail
        carry_ref[...] = dext[:GLU_HALO]
        dgu_ref[0] = dgate.astype(dgu_ref.dtype)
        dgu_ref[1] = dup.astype(dgu_ref.dtype)

    return pl.pallas_call(
        body, grid=(FF_BLOCKS, nt),
        in_specs=[pl.BlockSpec((2, None, tm, GU_SHARD), lambda j, t: (0, j, nt - 1 - t, 0)),
                  pl.BlockSpec((None, None, GLU_HALO, GU_SHARD),
                               lambda j, t: (0, j, jnp.maximum((nt - 1 - t) * (tm // GLU_HALO) - 1, 0), 0)),
                  pl.BlockSpec((None, HALO, GU_SHARD), lambda j, t: (j, 0, 0)),
                  pl.BlockSpec((None, 1, GU_SHARD), lambda j, t: (j, 0, 0)),
                  pl.BlockSpec((tm, D), lambda j, t: (nt - 1 - t, 0)), _spec_down(0)] + _dep_specs(deps),
        out_specs=[pl.BlockSpec((2, None, tm, GU_SHARD), lambda j, t: (0, j, nt - 1 - t, 0)),
                   pl.BlockSpec((None, HALO, GU_SHARD), lambda j, t: (j, 0, 0)),
                   pl.BlockSpec((None, 1, GU_SHARD), lambda j, t: (j, 0, 0))],
        out_shape=[_SDS(gu.shape, _ACT), _SDS((FF_BLOCKS, HALO, GU_SHARD), F32), _SDS((FF_BLOCKS, 1, GU_SHARD), F32)],
        scratch_shapes=[pltpu.VMEM((GLU_HALO, GU_SHARD), F32)],
        name=name, compiler_params=_cp("arbitrary", "arbitrary"))(gu, gu, conv_w, conv_b, dh, w_down, *deps)


def _bucket_table():
    qi = np.arange(BLK)[:, None]
    kj = np.arange(BLK)[None, :]
    n = np.where(kj > qi, BLK + qi - kj, qi - kj)
    max_exact = N_BUCKETS // 2
    nf = np.maximum(n, 1).astype(np.float32)
    large = max_exact + (np.log(nf / max_exact) / math.log(MAX_DISTANCE / max_exact)
                         * (N_BUCKETS - max_exact)).astype(np.int32)
    large = np.minimum(large, N_BUCKETS - 1)
    return np.where(n < max_exact, n, large).astype(np.int32)


def _lane_low():
    return lax.broadcasted_iota(jnp.int32, (1, 128), 1) < A_HD


def _swa_groups(q, kd, vd, sink, bias, upper, first):
    n = A_HEADS // A_KV_HEADS
    ng = A_KV_HEADS
    low = _lane_low()
    qm = [jnp.concatenate([jnp.where(low == (h % 2 == 0), q[g][:, (h // 2) * 128:(h // 2 + 1) * 128], 0.0) for h in range(n)], axis=0)
          for g in range(ng)]
    s2 = [_mm_nt(qm[g], kd[g]) * (A_HD ** -0.5) for g in range(ng)]
    s = [jnp.where(upper[None], s2[g][:, :BLK].reshape(n, BLK, BLK), s2[g][:, BLK:].reshape(n, BLK, BLK)) + bias[g] for g in range(ng)]
    s = [jnp.where((upper & first)[None], -jnp.inf, t) for t in s]
    m = [jnp.maximum(jnp.max(s[g], axis=-1, keepdims=True), sink[g]) for g in range(ng)]
    p = [jnp.exp(s[g] - m[g]) for g in range(ng)]
    split = [jnp.concatenate([jnp.where(upper[None], t, 0.0), jnp.where(upper[None], 0.0, t)], axis=-1).reshape(n * BLK, 2 * BLK)
             for t in p]
    ones = jnp.ones((BLK, 128), F32)
    den = [_mm(p[g].reshape(n * BLK, BLK), ones) + jnp.exp(sink[g] - m[g]).reshape(n * BLK, 1) for g in range(ng)]
    o = [_mm(split[g], vd[g]) / den[g] for g in range(ng)]
    return [jnp.concatenate([jnp.where(low, t[2 * k * BLK:(2 * k + 1) * BLK], t[(2 * k + 1) * BLK:(2 * k + 2) * BLK])
                             for k in range(n // 2)], axis=1) for t in o]


def _mix_a_core(q, kd, vd, sink, bias, xq, mk, mv, upper, first):
    return _swa_groups(q, kd, vd, sink, bias, upper, first), _cross_pairs(xq, mk, mv)


def _swa_sinks(sink_ref, g):
    n = A_HEADS // A_KV_HEADS
    return jnp.concatenate([sink_ref[:, h:h + 1] for h in range(g * n, (g + 1) * n)], axis=0).reshape(n, 1, 1)


def _both_halves(t, t_rolled, g):
    low = _lane_low()
    return jnp.where(low, t, t_rolled) if g == 0 else jnp.where(low, t_rolled, t)


def _cross_pairs(q, mk, mv):
    rows = q.shape[0]
    low = _lane_low()
    qm = [jnp.concatenate([jnp.where(low, q[:, p * 128:(p + 1) * 128], 0.0), jnp.where(low, 0.0, q[:, p * 128:(p + 1) * 128])], axis=0)
          for p in range(X_HEADS // 2)]
    s = [_mm_nt(qm[p], mk[:, p * 128:(p + 1) * 128]) * (X_HD ** -0.5) for p in range(X_HEADS // 2)]
    e = [jnp.exp(t - jnp.max(t, axis=-1, keepdims=True)) for t in s]
    pr = [t / jnp.sum(t, axis=-1, keepdims=True) for t in e]
    o = [_mm(pr[p], mv[:, p * 128:(p + 1) * 128]) for p in range(X_HEADS // 2)]
    return jnp.concatenate([jnp.where(low, t[:rows], t[rows:]) for t in o], axis=1)


def _swa_upper():
    qi = lax.broadcasted_iota(jnp.int32, (BLK, BLK), 0)
    kj = lax.broadcasted_iota(jnp.int32, (BLK, BLK), 1)
    return kj > qi


def _bias_build(rel_bias, bucket, name):
    def body(rb_ref, bucket_ref, o_ref):
        b = bucket_ref[...]
        for h in range(A_HEADS):
            acc = jnp.zeros((BLK, BLK), F32)
            for k in range(N_BUCKETS):
                acc = jnp.where(b == k, rb_ref[k, h], acc)
            o_ref[h] = acc

    return pl.pallas_call(
        body, in_specs=[pl.BlockSpec(memory_space=pltpu.SMEM), pl.BlockSpec(memory_space=pltpu.VMEM)],
        out_specs=pl.BlockSpec(memory_space=pltpu.VMEM),
        out_shape=_SDS((A_HEADS, BLK, BLK), F32), name=name)(rel_bias, bucket)


def _bias_reduce(dbias, bucket, name):
    def body(db_ref, bucket_ref, o_ref):
        b = bucket_ref[...]
        row = lax.broadcasted_iota(jnp.int32, (N_BUCKETS, 128), 0)
        lane = lax.broadcasted_iota(jnp.int32, (N_BUCKETS, 128), 1)
        acc = jnp.zeros((N_BUCKETS, 128), F32)
        for h in range(A_HEADS):
            v = db_ref[h]
            for k in range(N_BUCKETS):
                sk = jnp.sum(jnp.sum(jnp.where(b == k, v, 0.0), axis=1, keepdims=True), axis=0, keepdims=True)
                acc = acc + jnp.where((row == k) & (lane == h), sk, 0.0)
        o_ref[...] = acc

    return pl.pallas_call(
        body, in_specs=[pl.BlockSpec(memory_space=pltpu.VMEM)] * 2,
        out_specs=pl.BlockSpec(memory_space=pltpu.VMEM),
        out_shape=_SDS((N_BUCKETS, 128), F32), name=name)(dbias, bucket)


def _mix_a_fwd(proj, bias, sinks, memkv, name):
    s = proj.shape[0]
    nb = s // BLK
    grp = A_HEADS // A_KV_HEADS

    def body(proj_ref, prev_ref, bias_ref, sink_ref, memkv_ref, o_ref):
        i = pl.program_id(0)
        upper = _swa_upper()
        prev = prev_ref[...].astype(F32)
        proj = proj_ref[...].astype(F32)
        kb = jnp.concatenate([prev[:, :A_KV], proj[:, A_Q:A_Q + A_KV]], axis=0)
        vb = jnp.concatenate([prev[:, A_KV:], proj[:, A_Q + A_KV:A_Q + 2 * A_KV]], axis=0)
        kb_r = pltpu.roll(kb, A_HD, 1)
        vb_r = pltpu.roll(vb, A_HD, 1)
        gw = A_Q // A_KV_HEADS
        groups = range(A_KV_HEADS)
        swa, cross = _mix_a_core([proj[:, g * gw:(g + 1) * gw] for g in groups], [_both_halves(kb, kb_r, g) for g in groups],
                                 [_both_halves(vb, vb_r, g) for g in groups], [_swa_sinks(sink_ref, g) for g in groups],
                                 [bias_ref[g * grp:(g + 1) * grp] for g in groups], proj[:, A_Q + 2 * A_KV:],
                                 memkv_ref[:, :X_Q], memkv_ref[:, X_Q:], upper, i == 0)
        o_ref[...] = jnp.concatenate(swa + [cross], axis=1).astype(o_ref.dtype)

    return pl.pallas_call(
        body, grid=(nb,),
        in_specs=[pl.BlockSpec((BLK, IN_A), lambda i: (i, 0)),
                  pl.BlockSpec((BLK, 2 * A_KV), lambda i: (jnp.maximum(i - 1, 0), A_Q // (2 * A_KV))),
                  pl.BlockSpec((A_HEADS, BLK, BLK), lambda i: (0, 0, 0)),
                  pl.BlockSpec((1, 128), lambda i: (0, 0)),
                  pl.BlockSpec((MEM_LEN, 2 * X_Q), lambda i: (0, 0))],
        out_specs=pl.BlockSpec((BLK, D), lambda i: (i, 0)),
        out_shape=_SDS((s, D), _ACT), name=name, compiler_params=_cp("arbitrary"))(proj, proj, bias, sinks, memkv)


def _mix_a_bwd(proj, bias, sinks, memkv, dmix, name, deps=()):
    s = proj.shape[0]
    nb = s // BLK
    grp = A_HEADS // A_KV_HEADS

    def body(proj_ref, prev_ref, bias_ref, sink_ref, memkv_ref, dmix_ref, *rest):
        dproj_ref, dbias_ref, dsink_ref, dmemkv_ref, carry_ref = rest[-5:]
        t = pl.program_id(0)
        i = nb - 1 - t

        @pl.when(t == 0)
        def _():
            carry_ref[...] = jnp.zeros_like(carry_ref)
            dbias_ref[...] = jnp.zeros_like(dbias_ref)
            dsink_ref[...] = jnp.zeros_like(dsink_ref)
            dmemkv_ref[...] = jnp.zeros_like(dmemkv_ref)

        upper = _swa_upper()
        lane = lax.broadcasted_iota(jnp.int32, (1, 128), 1)
        low = _lane_low()
        prev = prev_ref[...].astype(F32)
        proj = proj_ref[...].astype(F32)
        kb = jnp.concatenate([prev[:, :A_KV], proj[:, A_Q:A_Q + A_KV]], axis=0)
        vb = jnp.concatenate([prev[:, A_KV:], proj[:, A_Q + A_KV:A_Q + 2 * A_KV]], axis=0)
        kb_r = pltpu.roll(kb, A_HD, 1)
        vb_r = pltpu.roll(vb, A_HD, 1)
        gw = A_Q // A_KV_HEADS
        groups = range(A_KV_HEADS)
        _, vjp = jax.vjp(
            functools.partial(_mix_a_core, upper=upper, first=i == 0),
            [proj[:, g * gw:(g + 1) * gw] for g in groups], [_both_halves(kb, kb_r, g) for g in groups],
            [_both_halves(vb, vb_r, g) for g in groups], [_swa_sinks(sink_ref, g) for g in groups],
            [bias_ref[g * grp:(g + 1) * grp] for g in groups], proj[:, A_Q + 2 * A_KV:], memkv_ref[:, :X_Q], memkv_ref[:, X_Q:])
        dqs, dk, dv, ds, db, dxq, dmk, dmv = vjp(
            ([dmix_ref[:, g * gw:(g + 1) * gw].astype(F32) for g in groups], dmix_ref[:, A_Q:].astype(F32)))
        dkd = [t + pltpu.roll(t, A_HD, 1) for t in dk]
        dvd = [t + pltpu.roll(t, A_HD, 1) for t in dv]
        dsink = jnp.zeros((1, 128), F32)
        for g in groups:
            for h in range(grp):
                dsink = dsink + jnp.where(lane == g * grp + h, ds[g][h], 0.0)
            dbias_ref[g * grp:(g + 1) * grp] += db[g]
        dsink_ref[...] += dsink
        dkb = jnp.where(low, dkd[0], dkd[1])
        dvb = jnp.where(low, dvd[0], dvd[1])
        dmemkv_ref[...] += jnp.concatenate([dmk, dmv], axis=1)
        dkv_cur = jnp.concatenate([dkb[BLK:], dvb[BLK:]], axis=1) + carry_ref[...]
        carry_ref[...] = jnp.concatenate([dkb[:BLK], dvb[:BLK]], axis=1)
        dproj_ref[...] = jnp.concatenate(list(dqs) + [dkv_cur, dxq], axis=1).astype(dproj_ref.dtype)

    return pl.pallas_call(
        body, grid=(nb,),
        in_specs=[pl.BlockSpec((BLK, IN_A), lambda t: (nb - 1 - t, 0)),
                  pl.BlockSpec((BLK, 2 * A_KV), lambda t: (jnp.maximum(nb - 2 - t, 0), A_Q // (2 * A_KV))),
                  pl.BlockSpec((A_HEADS, BLK, BLK), lambda t: (0, 0, 0)),
                  pl.BlockSpec((1, 128), lambda t: (0, 0)),
                  pl.BlockSpec((MEM_LEN, 2 * X_Q), lambda t: (0, 0)),
                  pl.BlockSpec((BLK, D), lambda t: (nb - 1 - t, 0))] + _dep_specs(deps),
        out_specs=[pl.BlockSpec((BLK, IN_A), lambda t: (nb - 1 - t, 0)),
                   pl.BlockSpec((A_HEADS, BLK, BLK), lambda t: (0, 0, 0)),
                   pl.BlockSpec((1, 128), lambda t: (0, 0)),
                   pl.BlockSpec((MEM_LEN, 2 * X_Q), lambda t: (0, 0))],
        out_shape=[_SDS((s, IN_A), _ACT), _SDS((A_HEADS, BLK, BLK), F32), _SDS((1, 128), F32),
                   _SDS((MEM_LEN, 2 * X_Q), F32)],
        scratch_shapes=[pltpu.VMEM((BLK, 2 * A_KV), F32)],
        name=name, compiler_params=_cp("arbitrary"))(proj, proj, bias, sinks, memkv, dmix, *deps)


def _neumann(pw, rhs):
    nh = len(pw)
    x = rhs
    for lvl in range(6):
        if lvl < 5:
            prod = [_mmf(pw[h], jnp.concatenate([x[h], pw[h]], axis=1)) for h in range(nh)]
            x = [x[h] + prod[h][:, :B_HD] for h in range(nh)]
            pw = [t[:, B_HD:] for t in prod]
        else:
            x = [x[h] + _mmf(pw[h], x[h]) for h in range(nh)]
    return x


@jax.custom_vjp
def _tri_solve(pw, rhs):
    return _neumann(pw, rhs)


def _tri_solve_fwd(pw, rhs):
    x = _neumann(pw, rhs)
    return x, (pw, x)


def _tri_solve_bwd(res, dx):
    pw, x = res
    d_rhs = _neumann([t.T for t in pw], list(dx))
    return [_mmf_nt(d_rhs[h], x[h]) for h in range(len(pw))], d_rhs


_tri_solve.defvjp(_tri_solve_fwd, _tri_solve_bwd)


@jax.custom_vjp
def _tri_solved(pw, rhs, x):
    return x


def _tri_solved_fwd(pw, rhs, x):
    return x, (pw, x)


def _tri_solved_bwd(res, dx):
    d_pw, d_rhs = _tri_solve_bwd(res, dx)
    return d_pw, d_rhs, [jnp.zeros_like(t) for t in res[1]]


_tri_solved.defvjp(_tri_solved_fwd, _tri_solved_bwd)


@jax.custom_vjp
def _known(x, value):
    return value


def _known_fwd(x, value):
    return value, None


def _known_bwd(_, g):
    return g, jnp.zeros_like(g)


_known.defvjp(_known_fwd, _known_bwd)


def _dn_heads(yq, yk, yv, z, bl, al, a_log, dtb, ng, s0, solved=None, out_known=None):
    c = CHUNK
    nh = B_V_HEADS
    rep = B_V_HEADS // B_QK_HEADS
    r = lax.broadcasted_iota(jnp.int32, (c, c), 0)
    cc = lax.broadcasted_iota(jnp.int32, (c, c), 1)
    q = [_silu(t) for t in yq]
    k = [_silu(t) for t in yk]
    v = [_silu(t) for t in yv]
    q = [t * lax.rsqrt(jnp.sum(t * t, axis=-1, keepdims=True) + EPS) * (B_HD ** -0.5) for t in q]
    k = [t * lax.rsqrt(jnp.sum(t * t, axis=-1, keepdims=True) + EPS) for t in k]
    beta = [jax.nn.sigmoid(t) for t in bl]
    g = [-jnp.exp(a_log[h]) * jax.nn.softplus(al[h] + dtb[h]) for h in range(nh)]
    gb = [jnp.broadcast_to(t, (c, c)) for t in g]
    gc_col = [jnp.sum(jnp.where(cc <= r, t.T, 0.0), axis=1, keepdims=True) for t in gb]
    gc_row = [jnp.sum(jnp.where(r <= cc, t, 0.0), axis=0, keepdims=True) for t in gb]
    gc_last = [jnp.sum(t, axis=0, keepdims=True) for t in g]
    decay = [jnp.exp(jnp.where(r >= cc, gc_col[h] - gc_row[h], -jnp.inf)) for h in range(nh)]
    kq = [_mmf_nt(jnp.concatenate([k[h], q[h]], axis=0), k[h]) for h in range(B_QK_HEADS)]
    kk = [t[:c] for t in kq]
    qk = [t[c:] for t in kq]
    egc = [jnp.exp(t) for t in gc_col]
    both = [_mmf(jnp.concatenate([(beta[h] * egc[h]) * k[h // rep], q[h // rep] * egc[h]], axis=0), s0[h]) for h in range(nh)]
    rhs = [beta[h] * v[h] - both[h][:c] for h in range(nh)]
    qs0 = [t[c:] for t in both]
    pw = [-(beta[h] * kk[h // rep] * jnp.where(r > cc, decay[h], 0.0)) for h in range(nh)]
    delta = _tri_solve(pw, rhs) if solved is None else _tri_solved(pw, rhs, solved)
    last = [_mmf(jnp.concatenate([qk[h // rep] * decay[h], (k[h // rep] * jnp.exp(gc_last[h] - gc_col[h])).T], axis=0), delta[h])
            for h in range(nh)]
    out = [qs0[h] + last[h][:c] for h in range(nh)]
    if out_known is not None:
        out = [_known(out[h], out_known[h]) for h in range(nh)]
    s1 = [jnp.exp(gc_last[h]) * s0[h] + last[h][c:] for h in range(nh)]
    o = [t * lax.rsqrt(jnp.mean(t * t, axis=-1, keepdims=True) + EPS) * ng for t in out]
    return [o[h] * _silu(z[h]) for h in range(nh)], s1, delta, out


def _dn_conv(ext, w_ref):
    y = ext * w_ref[B_CONV - 1:B_CONV, :]
    for j in range(B_CONV - 1):
        y = y + w_ref[j:j + 1, :] * pltpu.roll(ext, B_CONV - 1 - j, 0)
    return y


def _dn_args(y, cur_ref, par_ref, ng_ref):
    nh = B_V_HEADS
    return ([y[:, h * B_HD:(h + 1) * B_HD] for h in range(B_QK_HEADS)],
            [y[:, B_QK + h * B_HD:B_QK + (h + 1) * B_HD] for h in range(B_QK_HEADS)],
            [y[:, 2 * B_QK + h * B_HD:2 * B_QK + (h + 1) * B_HD] for h in range(nh)],
            [cur_ref[:, BP_Z + h * B_HD:BP_Z + (h + 1) * B_HD] for h in range(nh)],
            [cur_ref[:, BP_GATE + h:BP_GATE + h + 1] for h in range(nh)],
            [cur_ref[:, BP_GATE + nh + h:BP_GATE + nh + h + 1] for h in range(nh)],
            [par_ref[:, h:h + 1] for h in range(nh)], [par_ref[:, nh + h:nh + h + 1] for h in range(nh)], ng_ref[...])


def _mix_b_fwd(proj, conv_w, par, ng, memkv, name):
    s = proj.shape[0]
    nc = s // CHUNK

    def body(cur_ref, prev_ref, w_ref, par_ref, ng_ref, memkv_ref, o_ref, st_ref, dl_ref, state_ref):
        n = pl.program_id(0)

        @pl.when(n == 0)
        def _():
            state_ref[...] = jnp.zeros_like(state_ref)

        prev = jnp.where(n > 0, prev_ref[...], 0.0)
        ext = jnp.concatenate([prev, cur_ref[:, :B_QKV]], axis=0)
        y = _dn_conv(ext, w_ref)[HALO:]
        s0 = [state_ref[hv] for hv in range(B_V_HEADS)]
        st_ref[0] = state_ref[...]
        outs, s1, delta, raw = _dn_heads(*_dn_args(y, cur_ref, par_ref, ng_ref), s0)
        for hv in range(B_V_HEADS):
            state_ref[hv] = s1[hv]
            dl_ref[0, hv] = delta[hv]
            dl_ref[0, B_V_HEADS + hv] = raw[hv]
        outs = outs + [_cross_pairs(cur_ref[:, BP_XQ:BP_XQ + X_Q], memkv_ref[:, :X_Q], memkv_ref[:, X_Q:])]
        o_ref[...] = jnp.concatenate(outs, axis=1).astype(o_ref.dtype)

    return pl.pallas_call(
        body, grid=(nc,),
        in_specs=[pl.BlockSpec((CHUNK, IN_BP), lambda n: (n, 0)),
                  pl.BlockSpec((HALO, B_QKV), lambda n: (jnp.maximum(n * (CHUNK // HALO) - 1, 0), 0)),
                  pl.BlockSpec((HALO, B_QKV), lambda n: (0, 0)),
                  pl.BlockSpec((1, 128), lambda n: (0, 0)), pl.BlockSpec((1, 128), lambda n: (0, 0)),
                  pl.BlockSpec((MEM_LEN, 2 * X_Q), lambda n: (0, 0))],
        out_specs=[pl.BlockSpec((CHUNK, D), lambda n: (n, 0)),
                   pl.BlockSpec((1, B_V_HEADS, B_HD, B_HD), lambda n: (n, 0, 0, 0)),
                   pl.BlockSpec((1, 2 * B_V_HEADS, CHUNK, B_HD), lambda n: (n, 0, 0, 0))],
        out_shape=[_SDS((s, D), _ACT), _SDS((nc, B_V_HEADS, B_HD, B_HD), F32), _SDS((nc, 2 * B_V_HEADS, CHUNK, B_HD), F32)],
        scratch_shapes=[pltpu.VMEM((B_V_HEADS, B_HD, B_HD), F32)],
        name=name, compiler_params=_cp("arbitrary"))(proj, proj, conv_w, par, ng, memkv)


def _mix_b_bwd(proj, conv_w, par, ng, memkv, states, deltas, dmix, name):
    s = proj.shape[0]
    nc = s // CHUNK
    ext_rows = CHUNK + HALO

    def body(cur_ref, prev_ref, w_ref, par_ref, ng_ref, memkv_ref, st_ref, dl_ref, dmix_ref,
             dproj_ref, dw_ref, dpar_ref, dng_ref, dmemkv_ref, dstate_ref, carry_ref):
        t = pl.program_id(0)
        n = nc - 1 - t

        @pl.when(t == 0)
        def _():
            dstate_ref[...] = jnp.zeros_like(dstate_ref)
            carry_ref[...] = jnp.zeros_like(carry_ref)
            dw_ref[...] = jnp.zeros_like(dw_ref)
            dpar_ref[...] = jnp.zeros_like(dpar_ref)
            dng_ref[...] = jnp.zeros_like(dng_ref)
            dmemkv_ref[...] = jnp.zeros_like(dmemkv_ref)

        lane = lax.broadcasted_iota(jnp.int32, (1, 128), 1)
        prev = jnp.where(n > 0, prev_ref[...], 0.0)
        ext = jnp.concatenate([prev, cur_ref[:, :B_QKV]], axis=0)
        y = _dn_conv(ext, w_ref)[HALO:]
        solved = [dl_ref[0, hv] for hv in range(B_V_HEADS)]
        raw = [dl_ref[0, B_V_HEADS + hv] for hv in range(B_V_HEADS)]
        _, vjp = jax.vjp(functools.partial(_dn_heads, solved=solved, out_known=raw), *_dn_args(y, cur_ref, par_ref, ng_ref),
                         [st_ref[0, hv] for hv in range(B_V_HEADS)])
        none = [jnp.zeros((CHUNK, B_HD), F32)] * B_V_HEADS
        dyq, dyk, dyv, dz, gbl, gal, ga_log, gdtb, dng, gs0 = vjp(
            ([dmix_ref[:, hv * B_HD:(hv + 1) * B_HD].astype(F32) for hv in range(B_V_HEADS)],
             [dstate_ref[hv] for hv in range(B_V_HEADS)], none, none))
        dgate = jnp.zeros((CHUNK, 128), F32)
        dpar = jnp.zeros((1, 128), F32)
        for hv in range(B_V_HEADS):
            dstate_ref[hv] = gs0[hv]
            dgate = dgate + jnp.where(lane == hv, gbl[hv], 0.0) + jnp.where(lane == B_V_HEADS + hv, gal[hv], 0.0)
            dpar = dpar + jnp.where(lane == hv, ga_log[hv], 0.0) + jnp.where(lane == B_V_HEADS + hv, gdtb[hv], 0.0)
        dpar_ref[...] += dpar
        dng_ref[...] += dng
        _, vjp = jax.vjp(_cross_pairs, cur_ref[:, BP_XQ:BP_XQ + X_Q], memkv_ref[:, :X_Q], memkv_ref[:, X_Q:])
        dxq, dmk, dmv = vjp(dmix_ref[:, B_V:].astype(F32))
        dmemkv_ref[...] += jnp.concatenate([dmk, dmv], axis=1)
        dy = jnp.concatenate(list(dyq) + list(dyk) + list(dyv), axis=1)
        dy_ext = jnp.concatenate([jnp.zeros((HALO, B_QKV), F32), dy], axis=0)
        dext = dy_ext * w_ref[B_CONV - 1:B_CONV, :]
        dw_ref[B_CONV - 1:B_CONV, :] += jnp.sum(ext * dy_ext, axis=0, keepdims=True)
        for j in range(B_CONV - 1):
            sh = B_CONV - 1 - j
            dw_ref[j:j + 1, :] += jnp.sum(pltpu.roll(ext, sh, 0) * dy_ext, axis=0, keepdims=True)
            dext = dext + w_ref[j:j + 1, :] * pltpu.roll(dy_ext, ext_rows - sh, 0)
        tail = jnp.concatenate([jnp.zeros((CHUNK - HALO, B_QKV), F32), carry_ref[...]], axis=0)
        dqkv = dext[HALO:] + tail
        carry_ref[...] = dext[:HALO]
        dproj_ref[...] = jnp.concatenate([dqkv] + list(dz) + [dxq, dgate], axis=1).astype(dproj_ref.dtype)

    return pl.pallas_call(
        body, grid=(nc,),
        in_specs=[pl.BlockSpec((CHUNK, IN_BP), lambda t: (nc - 1 - t, 0)),
                  pl.BlockSpec((HALO, B_QKV), lambda t: (jnp.maximum((nc - 1 - t) * (CHUNK // HALO) - 1, 0), 0)),
                  pl.BlockSpec((HALO, B_QKV), lambda t: (0, 0)),
                  pl.BlockSpec((1, 128), lambda t: (0, 0)), pl.BlockSpec((1, 128), lambda t: (0, 0)),
                  pl.BlockSpec((MEM_LEN, 2 * X_Q), lambda t: (0, 0)),
                  pl.BlockSpec((1, B_V_HEADS, B_HD, B_HD), lambda t: (nc - 1 - t, 0, 0, 0)),
                  pl.BlockSpec((1, 2 * B_V_HEADS, CHUNK, B_HD), lambda t: (nc - 1 - t, 0, 0, 0)),
                  pl.BlockSpec((CHUNK, D), lambda t: (nc - 1 - t, 0))],
        out_specs=[pl.BlockSpec((CHUNK, IN_BP), lambda t: (nc - 1 - t, 0)),
                   pl.BlockSpec((HALO, B_QKV), lambda t: (0, 0)),
                   pl.BlockSpec((1, 128), lambda t: (0, 0)), pl.BlockSpec((1, 128), lambda t: (0, 0)),
                   pl.BlockSpec((MEM_LEN, 2 * X_Q), lambda t: (0, 0))],
        out_shape=[_SDS((s, IN_BP), _ACT), _SDS((HALO, B_QKV), F32), _SDS((1, 128), F32), _SDS((1, 128), F32),
                   _SDS((MEM_LEN, 2 * X_Q), F32)],
        scratch_shapes=[pltpu.VMEM((B_V_HEADS, B_HD, B_HD), F32), pltpu.VMEM((HALO, B_QKV), F32)],
        name=name, compiler_params=_cp("arbitrary"))(proj, proj, conv_w, par, ng, memkv, states, deltas, dmix)


def _place():
    return lax.axis_index("x"), lax.axis_index("y"), lax.axis_index("c")


def _all_gather(shards, name):
    n = len(shards)

    def body(*refs):
        ins, outs = refs[:n], refs[n:2 * n]
        send_sems, recv_sems, local_sems = refs[2 * n:]
        x, y, c = _place()
        me, sibling = (x, y, c), (x, y, 1 - c)
        chips = [(1 - x, y), (x, 1 - y), (1 - x, 1 - y)]

        def rows(a, px, py, pc):
            return outs[a].at[4 * px + 2 * py + pc]

        def copy(a, k, block, to, src=None):
            return pltpu.make_async_remote_copy(
                src_ref=rows(a, *block) if src is None else src, dst_ref=rows(a, *block),
                send_sem=send_sems.at[a, k], recv_sem=recv_sems.at[a, k],
                device_id=to, device_id_type=pl.DeviceIdType.MESH)

        mine = [pltpu.make_async_copy(ins[a], rows(a, *me), local_sems.at[a]) for a in range(n)]
        for cp in mine:
            cp.start()
        first = []
        for a in range(n):
            first.append(copy(a, 0, me, sibling, src=ins[a]))
            first += [copy(a, 1 + j, me, (*chip, c), src=ins[a]) for j, chip in enumerate(chips)]
        for cp in first:
            cp.start()
        passed = []
        for j, chip in enumerate(chips):
            for a in range(n):
                copy(a, 1 + j, (*chip, c), me).wait_recv()
                fwd = copy(a, 4 + j, (*chip, c), sibling)
                fwd.start()
                passed.append(fwd)
        for a in range(n):
            copy(a, 0, sibling, me).wait_recv()
            for j, chip in enumerate(chips):
                copy(a, 4 + j, (*chip, 1 - c), me).wait_recv()
        for cp in first + passed:
            cp.wait_send()
        for cp in mine:
            cp.wait()

    hbm = pl.BlockSpec(memory_space=pl.ANY)
    return pl.pallas_call(
        body, out_shape=[_SDS((N_DEV,) + s.shape, s.dtype) for s in shards],
        in_specs=[hbm] * n, out_specs=[hbm] * n,
        scratch_shapes=[pltpu.SemaphoreType.DMA((n, 7)), pltpu.SemaphoreType.DMA((n, 7)), pltpu.SemaphoreType.DMA((n,))],
        name=name)(*shards)


class _Exchange:
    def __init__(self, lands, srcs):
        self.lands, self.srcs = lands, srcs


def _seq_exchange(srcs, land_shapes, plan, name, cid):
    n, nl = len(srcs), len(land_shapes)

    def launch(*refs):
        src_refs, land_refs = refs[:n], refs[n:n + nl]
        send_sems, recv_sems, local_sems = refs[n + nl:]
        x, y, c = _place()
        my = 4 * x + 2 * y + c
        peers = [(x ^ ((k + 1) >> 2 & 1), y ^ ((k + 1) >> 1 & 1), c ^ ((k + 1) & 1)) for k in range(N_DEV - 1)]
        barrier = pltpu.get_barrier_semaphore()
        for p in peers:
            pl.semaphore_signal(barrier, inc=1, device_id=p, device_id_type=pl.DeviceIdType.MESH)
        pl.semaphore_wait(barrier, N_DEV - 1)

        def src_for(a, dest):
            return src_refs[a].at[dest] if plan[a][1] else src_refs[a]

        def slot(a, source):
            return land_refs[plan[a][0]].at[source]

        mine = [pltpu.make_async_copy(src_for(a, my), slot(a, my), local_sems.at[a]) for a in range(n)]
        for cp in mine:
            cp.start()
        sends, recvs = [], []
        for k, (px, py, pc) in enumerate(peers):
            peer = 4 * px + 2 * py + pc
            for a in range(n):
                kw = dict(send_sem=send_sems.at[a * (N_DEV - 1) + k], recv_sem=recv_sems.at[a * (N_DEV - 1) + k],
                          device_id=(px, py, pc), device_id_type=pl.DeviceIdType.MESH)
                sends.append(pltpu.make_async_remote_copy(src_ref=src_for(a, peer), dst_ref=slot(a, my), **kw))
                recvs.append(pltpu.make_async_remote_copy(src_ref=src_for(a, my), dst_ref=slot(a, peer), **kw))
        for cp in sends:
            cp.start()
        for cp in recvs:
            cp.wait_recv()
        for cp in sends:
            cp.wait_send()
        for cp in mine:
            cp.wait()

    lands = pl.kernel(
        launch, out_type=[_SDS(s, d) for s, d in land_shapes],
        mesh=plsc.ScalarSubcoreMesh(axis_name="sequencer", num_cores=1), name=name,
        scratch_types=(pltpu.SemaphoreType.DMA((n * (N_DEV - 1),)), pltpu.SemaphoreType.DMA((n * (N_DEV - 1),)),
                       pltpu.SemaphoreType.DMA((n,))),
        compiler_params=pltpu.CompilerParams(collective_id=cid))(*srcs)
    return _Exchange(list(lands), list(srcs))


def _adam_update(g, w, m, v):
    c1 = 1.0 - ADAM_B1 ** ADAM_STEP
    c2 = 1.0 - ADAM_B2 ** ADAM_STEP
    mm = ADAM_B1 * m + (1.0 - ADAM_B1) * g
    vv = ADAM_B2 * v + (1.0 - ADAM_B2) * (g * g)
    delta = -ADAM_LR * ((mm / c1) / (jnp.sqrt(vv / c2) + ADAM_EPS) + ADAM_WD * w)
    return delta, mm, vv


def _sum_sources(p_ref):
    g = p_ref[0].astype(F32)
    for s in range(1, N_DEV):
        g = g + p_ref[s].astype(F32)
    return g


def _adamw(parts, w, m, v, tr, name, restore_b=False, deps=()):
    nl, r, c = w.shape
    cp = parts[0].shape[-1]

    def body(*refs):
        p_refs = refs[:nl]
        w_ref, m_ref, v_ref = refs[nl:nl + 3]
        g_ref, d_ref, nm_ref, nv_ref = refs[-4:]
        g = _sum_sources(p_refs[0])
        for l in range(1, nl):
            g = jnp.where(pl.program_id(0) == l, _sum_sources(p_refs[l]), g)
        if restore_b:
            g = jnp.concatenate([g[:, :BP_XQ], g[:, BP_GATE:BP_GATE + 2 * B_V_HEADS], g[:, BP_XQ:BP_GATE]], axis=1)
        delta, mm, vv = _adam_update(g, w_ref[...], m_ref[...], v_ref[...])
        g_ref[...] = g
        d_ref[...] = delta
        nm_ref[...] = mm
        nv_ref[...] = vv

    spec = pl.BlockSpec((None, tr, c), lambda l, i: (l, i, 0))
    part_specs = [pl.BlockSpec((N_DEV, tr, cp), functools.partial(lambda l, i, k: (0, jnp.where(l == k, i, 0), 0), k=k))
                  for k in range(nl)]
    return pl.pallas_call(
        body, grid=(nl, r // tr),
        in_specs=part_specs + [spec, spec, spec] + _dep_specs(deps),
        out_specs=[spec] * 4, out_shape=[_SDS(w.shape, F32)] * 4,
        name=name, compiler_params=_cp("arbitrary", "arbitrary"))(*parts, w, m, v, *deps)


def _pack_small(d_rel, d_cb, d_cw, d_qkv, d_mix, d_mem, d_ffn, d_final, d_sinks, d_par, d_ng, loss_row, name):
    flat = [d_rel, *d_cb, *d_cw, d_qkv, *d_mix, *d_mem, *d_ffn, d_final, d_sinks, d_par, d_ng, loss_row]
    n = len(flat)

    def body(*refs):
        ins, o_ref = refs[:n], refs[n]
        rel, cb0, cb1, cw0, cw1, qkv, mx0, mx1, me0, me1, ff0, ff1, fin, snk, par, ng, lss = ins
        o_ref[...] = jnp.zeros_like(o_ref)
        for k in range(N_BUCKETS):
            lane = SP_REL_LANE + 128 * (k % 8)
            o_ref[SP_QKV + k // 8:SP_QKV + k // 8 + 1, lane:lane + 128] = rel[k:k + 1, :]
        for l, (cb, cw) in enumerate(((cb0, cw0), (cb1, cw1))):
            o_ref[SP_CB + l:SP_CB + l + 1, :] = jnp.concatenate([cb[j] for j in range(FF_BLOCKS)], axis=1)
            full = jnp.concatenate([cw[j] for j in range(FF_BLOCKS)], axis=1)
            o_ref[SP_CW + FFN_CONV * l:SP_CW + FFN_CONV * (l + 1), :] = full[:FFN_CONV]
        o_ref[SP_QKV:SP_QKV + B_CONV, 0:B_QKV] = qkv[0:B_CONV, :]
        for base, pair in ((SP_MIX, (mx0, mx1)), (SP_MEM, (me0, me1)), (SP_FFN, (ff0, ff1))):
            for l in range(2):
                o_ref[base + l:base + l + 1, 0:D] = pair[l][...]
        o_ref[SP_FINAL:SP_FINAL + 1, 0:D] = fin[...]
        o_ref[SP_MISC:SP_MISC + 1, 0:128] = snk[...]
        o_ref[SP_MISC:SP_MISC + 1, 128:256] = par[...]
        o_ref[SP_MISC:SP_MISC + 1, 256:384] = ng[...]
        o_ref[SP_MISC:SP_MISC + 1, 384:512] = lss[...]

    vm = pl.BlockSpec(memory_space=pltpu.VMEM)
    return pl.pallas_call(body, in_specs=[vm] * n, out_specs=vm, out_shape=_SDS((SMALL_ROWS, D_FF), F32), name=name)(*flat)


_SMALL = ["rel_bias", "norm_mix_g", "norm_mem_g", "sinks_a", "a_log_b", "dt_bias_b", "out_norm_g_b", "norm_ffn_g",
          "ffn_conv_b", "final_norm_g", "conv_qkv_b", "ffn_conv_w"]


def _adamw_small(recv, rc_qkv, rc_ffn, ws, ms, vs, name, deps=()):
    n = len(_SMALL)

    def body(*refs):
        recv_ref, qkv_ref, ffn_ref = refs[:3]
        w_refs, m_refs, v_refs = refs[3:3 + n], refs[3 + n:3 + 2 * n], refs[3 + 2 * n:3 + 3 * n]
        outs, loss_ref = refs[len(refs) - 4 * n - 1:len(refs) - 1], refs[-1]
        gs = _sum_sources(recv_ref)
        loss_ref[...] = gs[SP_MISC:SP_MISC + 1, 384:512]
        grads = {
            "rel_bias": jnp.concatenate(
                [gs[SP_QKV + k // 8:SP_QKV + k // 8 + 1, SP_REL_LANE + 128 * (k % 8):SP_REL_LANE + 128 * (k % 8) + A_HEADS]
                 for k in range(N_BUCKETS)], axis=0),
            "norm_mix_g": gs[SP_MIX:SP_MIX + 2, 0:D], "norm_mem_g": gs[SP_MEM:SP_MEM + 2, 0:D],
            "sinks_a": gs[SP_MISC:SP_MISC + 1, 0:A_HEADS],
            "a_log_b": gs[SP_MISC:SP_MISC + 1, 128:128 + B_V_HEADS],
            "dt_bias_b": gs[SP_MISC:SP_MISC + 1, 128 + B_V_HEADS:128 + 2 * B_V_HEADS],
            "out_norm_g_b": gs[SP_MISC:SP_MISC + 1, 256:256 + B_HD],
            "norm_ffn_g": gs[SP_FFN:SP_FFN + 2, 0:D], "ffn_conv_b": gs[SP_CB:SP_CB + 2, :],
            "final_norm_g": gs[SP_FINAL:SP_FINAL + 1, 0:D],
            "conv_qkv_b": _sum_sources(qkv_ref), "ffn_conv_w": _sum_sources(ffn_ref),
        }
        for i, nm in enumerate(_SMALL):
            g = grads[nm]
            delta, mm, vv = _adam_update(g, w_refs[i][...], m_refs[i][...], v_refs[i][...])
            outs[i][...] = g
            outs[n + i][...] = delta
            outs[2 * n + i][...] = mm
            outs[3 * n + i][...] = vv

    vm = pl.BlockSpec(memory_space=pltpu.VMEM)
    shapes = [_SDS(w.shape, F32) for w in ws]
    return pl.pallas_call(
        body, in_specs=[vm] * (3 + 3 * n) + _dep_specs(deps), out_specs=[vm] * (4 * n + 1),
        out_shape=shapes * 4 + [_SDS((1, 128), F32)],
        name=name)(recv, rc_qkv, rc_ffn, *ws, *ms, *vs, *deps)


def _assemble(gathered, axis):
    g = jnp.moveaxis(gathered, 0, axis)
    shp = list(g.shape)
    return g.reshape(shp[:axis] + [shp[axis] * shp[axis + 1]] + shp[axis + 2:])


def _pad_rows(a, rows):
    return jnp.pad(a, ((0, rows - a.shape[0]), (0, 0)))


def _pad_lanes(a, lanes=128):
    return jnp.pad(a, ((0, 0), (0, lanes - a.shape[1])))


def _ff_blocks(a):
    return jnp.moveaxis(a.reshape(a.shape[0], FF_BLOCKS, GU_SHARD), 1, 0)


def _reorder_b(w):
    qkv_z = w[..., :B_QKV + B_V]
    gates = w[..., B_QKV + B_V:B_QKV + B_V + 2 * B_V_HEADS]
    xq = w[..., IN_B - X_Q:]
    pad = jnp.zeros(w.shape[:-1] + (IN_BP - IN_B,), w.dtype)
    return jnp.concatenate([qkv_z, xq, gates, pad], axis=-1)


def kernel(x, mem, rel_bias, norm_mix_g, norm_mem_g, w_mem_kv, w_out, w_in_a, sinks_a, w_in_b, conv_qkv_b, a_log_b, dt_bias_b, out_norm_g_b, norm_ffn_g, w_gate_up, ffn_conv_w, ffn_conv_b, w_down, final_norm_g, loss_target, m_rel_bias, m_norm_mix_g, m_norm_mem_g, m_w_mem_kv, m_w_out, m_w_in_a, m_sinks_a, m_w_in_b, m_conv_qkv_b, m_a_log_b, m_dt_bias_b, m_out_norm_g_b, m_norm_ffn_g, m_w_gate_up, m_ffn_conv_w, m_ffn_conv_b, m_w_down, m_final_norm_g, v_rel_bias, v_norm_mix_g, v_norm_mem_g, v_w_mem_kv, v_w_out, v_w_in_a, v_sinks_a, v_w_in_b, v_conv_qkv_b, v_a_log_b, v_dt_bias_b, v_out_norm_g_b, v_norm_ffn_g, v_w_gate_up, v_ffn_conv_w, v_ffn_conv_b, v_w_down, v_final_norm_g):
    local = dict(locals())
    order = ["rel_bias", "norm_mix_g", "norm_mem_g", "w_mem_kv", "w_out", "w_in_a", "sinks_a", "w_in_b", "conv_qkv_b",
             "a_log_b", "dt_bias_b", "out_norm_g_b", "norm_ffn_g", "w_gate_up", "ffn_conv_w", "ffn_conv_b", "w_down",
             "final_norm_g"]
    wts = {n: local[n] for n in order}
    moms = {n: local["m_" + n] for n in order}
    vars_ = {n: local["v_" + n] for n in order}
    h0 = x[0]
    memx = mem[0]
    tgt = loss_target[0]
    s = h0.shape[0]
    tm = _rows(s)
    tb = min(s, _TM_BIG)

    t_ = lambda a: jnp.swapaxes(a, 1, 2)
    g_mk, g_out, g_ia, g_cq, g_cw = _all_gather(
        [w_mem_kv.astype(_MXU), w_out.astype(_MXU), t_(w_in_a).astype(_MXU), conv_qkv_b, ffn_conv_w], "gather_first")
    gu_land = ((N_DEV, GU_SHARD, D), _MXU)
    dn_land = ((N_DEV, DN_SHARD, D), _MXU)
    whole = [(0, False), (1, False)]
    def after(a, b):
        return a + (b[(0,) * b.ndim] * 0).astype(a.dtype)

    ffn0_w = _seq_exchange([after(t_(w_gate_up)[0].astype(_MXU), g_ia), after(w_down[0].astype(_MXU), g_ia)], [gu_land, dn_land],
                           whole, "gather_ffn0", 1)
    w_ia = g_ia.reshape(IN_A, D)
    conv_qkv = _pad_rows(_assemble(g_cq, 2)[0], HALO)
    ffn_cw_full = _assemble(g_cw, 2)
    ffn_cw = [_ff_blocks(_pad_rows(ffn_cw_full[i], HALO)) for i in range(2)]
    ffn_cb = [_ff_blocks(ffn_conv_b[i:i + 1]) for i in range(2)]
    bucket = jnp.asarray(_bucket_table())
    bias = _bias_build(rel_bias, bucket, "bias_build")
    sinks = _pad_lanes(sinks_a)
    par_b = _pad_lanes(jnp.concatenate([a_log_b, dt_bias_b], axis=1))

    row_x = pl.BlockSpec((tm, D), lambda i, j: (i, 0))
    gu_shape = (2, FF_BLOCKS, s, GU_SHARD)

    def in_proj(h, g, w, w_spec, n_cols, tn, name, deps=(), out_dtype=F32, w_t=False):
        return _norm_matmul(h, g, w, w_spec, n_cols // tn, (h.shape[0], n_cols),
                            pl.BlockSpec((_rows(h.shape[0]), tn), lambda i, j: (i, j)), name, deps=deps, out_dtype=out_dtype,
                            w_t=w_t)

    def ffn_fwd(i, h, g_gu, g_dn, deps=()):
        gu, hn = _norm_matmul(h, norm_ffn_g[i:i + 1], g_gu, _spec_gate_up(1), N_DEV, gu_shape,
                              _spec_gu_act(0, 1, tb), f"gate_up_{i}", deps=deps, out_dtype=_ACT, w_t=True, tm=tb)
        h_new, act = _glu_down(gu, ffn_cw[i], ffn_cb[i], g_dn, h, f"glu_down_{i}")
        return h_new, gu, hn, act

    def out_proj(i, mix, h):
        return _matmul_res(mix, row_x, g_out, _spec_rowsharded(i, D // N_DEV, D), 1, h, f"out_proj_{i}")

    proj_a, hn_a = in_proj(h0, norm_mix_g[0:1], w_ia, pl.BlockSpec((640, D), lambda i, j: (j, 0)), IN_A, 640, "in_proj_a",
                           deps=ffn0_w.srcs, out_dtype=_ACT, w_t=True)
    memkv0, memn0 = in_proj(memx, norm_mem_g[0:1], g_mk, _spec_rowsharded(0, D // N_DEV, 2 * X_Q), 2 * X_Q, 2 * X_Q, "mem_proj_0")
    mix_a = _mix_a_fwd(proj_a, bias, sinks, memkv0, "mix_a_fwd")
    h1 = out_proj(0, mix_a, h0)
    g_gu0, g_dn0 = ffn0_w.lands
    in_b_w = _seq_exchange([after(_reorder_b(w_in_b).astype(_MXU), h1)], [((N_DEV, 1, D // N_DEV, IN_BP), _MXU)], [(0, False)],
                           "gather_in_b", 2)
    ffn1_w = _seq_exchange([after(t_(w_gate_up)[1].astype(_MXU), h1), after(w_down[1].astype(_MXU), h1)], [gu_land, dn_land], whole,
                           "gather_ffn1", 3)
    h2, gu0, hn_f0, act0 = ffn_fwd(0, h1, g_gu0, g_dn0, deps=in_b_w.srcs + ffn1_w.srcs)
    g_ib, = in_b_w.lands
    proj_b, hn_b = in_proj(h2, norm_mix_g[1:2], g_ib, _spec_rowsharded(0, D // N_DEV, 896, col_block=1), IN_BP, 896, "in_proj_b")
    memkv1, memn1 = in_proj(memx, norm_mem_g[1:2], g_mk, _spec_rowsharded(1, D // N_DEV, 2 * X_Q), 2 * X_Q, 2 * X_Q, "mem_proj_1")
    mix_b, states, deltas = _mix_b_fwd(proj_b, conv_qkv, par_b, out_norm_g_b, memkv1, "mix_b_fwd")
    h3 = out_proj(1, mix_b, h2)
    g_gu1, g_dn1 = ffn1_w.lands
    h4, gu1, hn_f1, act1 = ffn_fwd(1, h3, g_gu1, g_dn1)
    loss_row, dh, d_final_g = _loss_head(h4, final_norm_g[None, :], tgt, "loss_head")

    zeros_mem = jnp.zeros_like(memx)
    per_dest2 = [(0, True), (1, True)]

    def ffn_bwd(i, dh, h_in, gu, hn_f, act, g_gu, g_dn, deps=()):
        dgu, d_cw, d_cb = _glu_bwd(gu, ffn_cw[i], ffn_cb[i], dh, g_dn, f"glu_bwd_{i}", deps=deps)
        d_wdown = _matmul_tn(act, pl.BlockSpec((None, tm, GU_SHARD), lambda j, r: (j, r, 0)),
                             dh, pl.BlockSpec((tm, D), lambda j, r: (r, 0)), s, FF_BLOCKS, (GU_SHARD, D),
                             (N_DEV, DN_SHARD, D), pl.BlockSpec((2, DN_SHARD, D), lambda j, r: (j, 0, 0)), f"d_w_down_{i}")
        dh_new, d_g = _matmul_nt_normbwd(dgu, _spec_gu_act(0, 1, tm), g_gu, _spec_gate_up(1), N_DEV, h_in,
                                         norm_ffn_g[i:i + 1], dh, f"d_ffn_in_{i}", w_t=True)
        d_wgu = _matmul_tn(dgu, _spec_gu_act(1, 0, tb), hn_f, pl.BlockSpec((tb, D), lambda j, r: (r, 0)), s, N_DEV,
                           (GU_SHARD, D), (N_DEV, GU_SHARD, D), pl.BlockSpec((None, GU_SHARD, D), lambda j, r: (j, 0, 0)),
                           f"d_w_gate_up_{i}", tm=tb)
        return dh_new, [d_wdown, d_wgu], d_cw, d_cb, d_g

    def out_bwd(i, dh, mix, deps):
        dmix = _matmul_nt(dh, g_out, _spec_rowsharded(i, D // N_DEV, D), 1, (s, D), row_x, f"d_mix_{i}", deps=deps, out_dtype=_ACT)
        d_wout = _matmul_tn(mix, pl.BlockSpec((tm, D), lambda j, r: (r, 0)), dh, pl.BlockSpec((tm, D), lambda j, r: (r, 0)),
                            s, 1, (D, D), (N_DEV, D // N_DEV, D), pl.BlockSpec((N_DEV, D // N_DEV, D), lambda j, r: (0, 0, 0)),
                            f"d_w_out_{i}")
        return dmix, d_wout

    def mem_bwd(i, dmemkv, memn):
        tmm = _rows(MEM_LEN)
        _, d_g = _matmul_nt_normbwd(dmemkv, pl.BlockSpec((tmm, 2 * X_Q), lambda r, j: (r, 0)), g_mk,
                                    _spec_rowsharded(i, D // N_DEV, 2 * X_Q), 1, memx, norm_mem_g[i:i + 1], zeros_mem,
                                    f"d_mem_in_{i}")
        by_row = lambda j, r: (r, 0)
        d_w = _matmul_tn(memn, pl.BlockSpec((tmm, D), by_row), dmemkv, pl.BlockSpec((tmm, 2 * X_Q), by_row), MEM_LEN, 1,
                         (D, 2 * X_Q), (N_DEV, D // N_DEV, 2 * X_Q),
                         pl.BlockSpec((N_DEV, D // N_DEV, 2 * X_Q), lambda j, r: (0, 0, 0)), f"d_w_mem_kv_{i}")
        return d_w, d_g

    out_land = ((N_DEV, D // N_DEV, D), _WIRE)
    mk_land = ((N_DEV, D // N_DEV, 2 * X_Q), _WIRE)
    ffn_lands = [((N_DEV, DN_SHARD, D), _WIRE), ((N_DEV, GU_SHARD, D), _WIRE)]
    dh, d_ffn1, d_cw1, d_cb1, d_gf1 = ffn_bwd(1, dh, h3, gu1, hn_f1, act1, g_gu1, g_dn1)
    ffn1_g = _seq_exchange(d_ffn1, ffn_lands, per_dest2, "send_ffn1_grads", 5)
    dmix, d_wout1 = out_bwd(1, dh, mix_b, ffn1_g.srcs)
    dproj_b, d_convw, d_par, d_ng, dmemkv1 = _mix_b_bwd(proj_b, conv_qkv, par_b, out_norm_g_b, memkv1, states, deltas, dmix, "mix_b_bwd")
    dh, d_gm1 = _matmul_nt_normbwd(dproj_b, pl.BlockSpec((tm, 896), lambda i, j: (i, j)), g_ib,
                                   _spec_rowsharded(0, D // N_DEV, 896, col_block=1), IN_BP // 896, h2, norm_mix_g[1:2], dh, "d_in_b")
    d_wib = _matmul_tn(hn_b, pl.BlockSpec((tm, D), lambda j, r: (r, 0)), dproj_b, pl.BlockSpec((tm, 896), lambda j, r: (r, j)),
                       s, IN_BP // 896, (D, 896), (N_DEV, D // N_DEV, IN_BP),
                       pl.BlockSpec((N_DEV, D // N_DEV, 896), lambda j, r: (0, 0, j)), "d_w_in_b")
    d_wmk1, d_gmem1 = mem_bwd(1, dmemkv1, memn1)
    mix1_g = _seq_exchange([d_wout1, d_wib, d_wmk1], [out_land, ((N_DEV, D // N_DEV, IN_BP), _WIRE), mk_land],
                           [(0, True), (1, True), (2, True)], "send_mix1_grads", 6)
    dh, d_ffn0, d_cw0, d_cb0, d_gf0 = ffn_bwd(0, dh, h1, gu0, hn_f0, act0, g_gu0, g_dn0, deps=mix1_g.srcs)
    dmix, d_wout0 = out_bwd(0, dh, mix_a, d_ffn0 + ffn1_g.lands[:1])
    ffn0_g = _seq_exchange(d_ffn0 + [d_wout0], ffn_lands + [out_land], per_dest2 + [(2, True)], "send_ffn0_grads", 4)
    dproj_a, dbias, dsinks, dmemkv0 = _mix_a_bwd(proj_a, bias, sinks, memkv0, dmix, "mix_a_bwd", deps=ffn0_g.srcs)
    dh, d_gm0 = _matmul_nt_normbwd(dproj_a, pl.BlockSpec((tm, 640), lambda i, j: (i, j)), w_ia,
                                   pl.BlockSpec((640, D), lambda i, j: (j, 0)), IN_A // 640, h0, norm_mix_g[0:1], dh, "d_in_a",
                                   w_t=True)
    d_wia = _matmul_tn(dproj_a, pl.BlockSpec((tm, IN_A), lambda j, r: (r, 0)), hn_a, pl.BlockSpec((tm, D), lambda j, r: (r, 0)),
                       s, 1, (IN_A, D), (N_DEV, IA_SHARD, D), pl.BlockSpec((N_DEV, IA_SHARD, D), lambda j, r: (0, 0, 0)),
                       "d_w_in_a")
    d_wmk0, d_gmem0 = mem_bwd(0, dmemkv0, memn0)
    d_rel = _bias_reduce(dbias, bucket, "bias_reduce")
    small = _pack_small(d_rel, (d_cb0, d_cb1), (d_cw0, d_cw1), d_convw, (d_gm0, d_gm1), (d_gmem0, d_gmem1),
                        (d_gf0, d_gf1), d_final_g, dsinks, d_par, d_ng, loss_row, "pack_small")
    mix0_g = _seq_exchange([d_wia, d_wmk0, small],
                           [((N_DEV, IA_SHARD, D), _WIRE), mk_land, ((N_DEV, SMALL_ROWS, D_FF), F32)],
                           [(0, True), (1, True), (2, False)], "send_mix0_grads", 7)

    res = {}
    last = []

    def update(nm, parts, tr, restore=False, transposed=False):
        view = t_ if transposed else (lambda a: a)
        out = _adamw(parts, view(wts[nm]), view(moms[nm]), view(vars_[nm]), tr, "adamw_" + nm, restore_b=restore, deps=last[-1:])
        res[nm] = [view(o) for o in out]
        last.append(out[1])

    r_dn1, r_gu1 = ffn1_g.lands
    r_dn0, r_gu0, r_out0 = ffn0_g.lands
    r_out1, r_ib, r_mk1 = mix1_g.lands
    update("w_gate_up", [r_gu0, r_gu1], 176, transposed=True)
    update("w_down", [r_dn0, r_dn1], 176)
    update("w_in_b", [r_ib], 32, True)
    r_ia, r_mk0, r_small = mix0_g.lands
    update("w_mem_kv", [r_mk0, r_mk1], 128)
    update("w_out", [r_out0, r_out1], 128)
    update("w_in_a", [r_ia], IA_SHARD, transposed=True)

    my = 4 * lax.axis_index("x") + 2 * lax.axis_index("y") + lax.axis_index("c")
    cq = conv_qkv_b.shape[-1]
    cf = ffn_conv_w.shape[-1]
    rc_qkv = lax.dynamic_slice_in_dim(r_small[:, SP_QKV:SP_QKV + B_CONV, :B_QKV], my * cq, cq, axis=2)[:, None]
    rc_ffn = lax.dynamic_slice_in_dim(r_small[:, SP_CW:SP_CW + 2 * FFN_CONV, :], my * cf, cf, axis=2).reshape(N_DEV, 2, FFN_CONV, cf)
    as2d = lambda a: a[None, :] if a.ndim == 1 else a
    small_out = _adamw_small(r_small, rc_qkv, rc_ffn, [as2d(wts[n]) for n in _SMALL], [as2d(moms[n]) for n in _SMALL],
                             [as2d(vars_[n]) for n in _SMALL], "adamw_small", deps=last[-1:])
    ns = len(_SMALL)
    for i, nm in enumerate(_SMALL):
        res[nm] = [small_out[k * ns + i].reshape(wts[nm].shape) for k in range(4)]

    return (small_out[-1][0, 0], dh[None], *[res[n][0] for n in order], *[res[n][1] for n in order],
            *[res[n][2] for n in order], *[res[n][3] for n in order])
```

```python
import functools
import math

import numpy as np

import jax
import jax.numpy as jnp
from jax import lax
from jax.experimental import pallas as pl
from jax.experimental.pallas import tpu as pltpu
from jax.experimental.pallas import tpu_sc as plsc

F32 = jnp.float32
_MXU = jnp.bfloat16
_ACT = jnp.bfloat16
_WIRE = jnp.bfloat16
_HI = lax.Precision.HIGH
_TM = 1024
_TM_GLU = 512
_TM_BIG = 2048
_VMEM_LIMIT = 48 * 1024 * 1024
_SDS = jax.ShapeDtypeStruct

D = 1024
EPS = 1e-6
A_HEADS, A_KV_HEADS, A_HD, BLK = 12, 2, 64, 128
N_BUCKETS, MAX_DISTANCE = 32, 128
B_QK_HEADS, B_V_HEADS, B_HD, B_CONV, CHUNK = 3, 6, 128, 4, 64
X_HEADS, X_HD, MEM_LEN = 4, 64, 256
D_FF, FFN_CONV = 2816, 3
A_Q, A_KV, X_Q = 768, 128, 256
B_QK, B_V, B_QKV = 384, 768, 1536
IN_A, IN_B = 1280, 2572
IN_BP = 2688
BP_Z, BP_XQ, BP_GATE = 1536, 2304, 2560
HALO = 8
GLU_HALO = 16

N_DEV = 8
GU_SHARD = 2 * D_FF // N_DEV
FF_BLOCKS = D_FF // GU_SHARD
DN_SHARD = D_FF // N_DEV
IA_SHARD = IN_A // N_DEV

ADAM_LR, ADAM_B1, ADAM_B2, ADAM_EPS, ADAM_WD, ADAM_STEP = 0.001, 0.9, 0.999, 1e-08, 0.01, 10

SP_CB, SP_CW, SP_QKV, SP_MIX, SP_MEM, SP_FFN, SP_FINAL, SP_MISC, SMALL_ROWS = 0, 2, 8, 12, 14, 16, 18, 19, 24
SP_REL_LANE = B_QKV


def _cp(*sems):
    return pltpu.CompilerParams(dimension_semantics=sems, vmem_limit_bytes=_VMEM_LIMIT)


def _mm(a, b):
    return jnp.dot(a.astype(_MXU), b.astype(_MXU), preferred_element_type=F32)


def _mm_nt(a, b):
    return lax.dot_general(a.astype(_MXU), b.astype(_MXU), (((1,), (1,)), ((), ())), preferred_element_type=F32)


def _mm_tn(a, b):
    return lax.dot_general(a.astype(_MXU), b.astype(_MXU), (((0,), (0,)), ((), ())), preferred_element_type=F32)


def _mmf(a, b):
    return jnp.dot(a, b, preferred_element_type=F32, precision=_HI)


def _mmf_nt(a, b):
    return lax.dot_general(a, b, (((1,), (1,)), ((), ())), preferred_element_type=F32, precision=_HI)


def _silu(x):
    return x * jax.nn.sigmoid(x)


def _w2d(ref):
    v = ref[...]
    return v.reshape(-1, v.shape[-1])


def _rows(m):
    return min(m, _TM)


def _spec_rowsharded(layer, rows, cols, col_block=None):
    if col_block is None:
        return pl.BlockSpec((N_DEV, None, rows, cols), lambda *_: (0, layer, 0, 0))
    return pl.BlockSpec((N_DEV, None, rows, cols), lambda *ids: (0, layer, 0, ids[col_block]))


def _spec_gate_up(axis):
    return pl.BlockSpec((None, GU_SHARD, D), lambda *ids: (ids[axis], 0, 0))


def _spec_down(axis):
    return pl.BlockSpec((2, DN_SHARD, D), lambda *ids: (ids[axis], 0, 0))


def _dep_specs(deps):
    return [pl.BlockSpec(memory_space=pl.ANY) for d in deps]


def _spec_gu_act(row_axis, axis, tm):
    return pl.BlockSpec((None, None, tm, GU_SHARD), lambda *ids: (ids[axis] // FF_BLOCKS, ids[axis] % FF_BLOCKS, ids[row_axis], 0))


def _norm_matmul(x, g, w, w_spec, n_blocks, out_shape, out_spec, name, deps=(), out_dtype=F32, w_t=False, tm=None):
    m, k = x.shape
    tm = tm or _rows(m)

    def body(x_ref, g_ref, w_ref, *rest):
        y_ref, hn_ref = rest[-2:]

        @pl.when(pl.program_id(1) == 0)
        def _():
            xv = x_ref[...]
            r = lax.rsqrt(jnp.mean(xv * xv, axis=-1, keepdims=True) + EPS)
            hn_ref[...] = (xv * r * g_ref[...]).astype(hn_ref.dtype)

        y_ref[...] = (_mm_nt if w_t else _mm)(hn_ref[...], _w2d(w_ref)).astype(y_ref.dtype)

    return pl.pallas_call(
        body, grid=(m // tm, n_blocks),
        in_specs=[pl.BlockSpec((tm, k), lambda i, j: (i, 0)), pl.BlockSpec((1, k), lambda i, j: (0, 0)), w_spec]
        + _dep_specs(deps),
        out_specs=[out_spec, pl.BlockSpec((tm, k), lambda i, j: (i, 0))],
        out_shape=[_SDS(out_shape, out_dtype), _SDS((m, k), _ACT)],
        name=name, compiler_params=_cp("arbitrary", "arbitrary"))(x, g, w, *deps)


def _matmul_res(a, a_spec, w, w_spec, n_k, res, name):
    m, n = res.shape
    tm = _rows(m)

    def body(a_ref, w_ref, r_ref, o_ref):
        part = _mm(a_ref[...], _w2d(w_ref))

        @pl.when(pl.program_id(1) == 0)
        def _():
            o_ref[...] = r_ref[...] + part

        @pl.when(pl.program_id(1) > 0)
        def _():
            o_ref[...] += part

    return pl.pallas_call(
        body, grid=(m // tm, n_k),
        in_specs=[a_spec, w_spec, pl.BlockSpec((tm, n), lambda i, j: (i, 0))],
        out_specs=pl.BlockSpec((tm, n), lambda i, j: (i, 0)),
        out_shape=_SDS((m, n), F32), name=name, compiler_params=_cp("arbitrary", "arbitrary"))(a, w, res)


def _matmul_nt(dy, w, w_spec, n_blocks, out_shape, out_spec, name, deps=(), out_dtype=F32):
    m, n = dy.shape
    tm = _rows(m)

    def body(dy_ref, w_ref, *rest):
        o_ref = rest[-1]
        o_ref[...] = _mm_nt(dy_ref[...], _w2d(w_ref)).astype(o_ref.dtype)

    return pl.pallas_call(
        body, grid=(m // tm, n_blocks),
        in_specs=[pl.BlockSpec((tm, n), lambda i, j: (i, 0)), w_spec] + _dep_specs(deps),
        out_specs=out_spec, out_shape=_SDS(out_shape, out_dtype),
        name=name, compiler_params=_cp("arbitrary", "arbitrary"))(dy, w, *deps)


def _matmul_nt_normbwd(dy, dy_spec, w, w_spec, nj, h, g, dh_in, name, w_t=False):
    m, k = h.shape
    tm = _rows(m)

    def body(dy_ref, w_ref, h_ref, g_ref, dhin_ref, dh_ref, dg_ref, acc_ref):
        i, j = pl.program_id(0), pl.program_id(1)

        @pl.when(j == 0)
        def _():
            acc_ref[...] = jnp.zeros_like(acc_ref)

        acc_ref[...] += (_mm if w_t else _mm_nt)(dy_ref[...], _w2d(w_ref))

        @pl.when(j == nj - 1)
        def _():
            xv = h_ref[...]
            r = lax.rsqrt(jnp.mean(xv * xv, axis=-1, keepdims=True) + EPS)
            xh = xv * r
            dhn = acc_ref[...]
            part = jnp.sum(dhn * xh, axis=0, keepdims=True)

            @pl.when(i == 0)
            def _():
                dg_ref[...] = part

            @pl.when(i > 0)
            def _():
                dg_ref[...] += part

            t = dhn * g_ref[...]
            dh_ref[...] = dhin_ref[...] + r * (t - xh * jnp.mean(t * xh, axis=-1, keepdims=True))

    return pl.pallas_call(
        body, grid=(m // tm, nj),
        in_specs=[dy_spec, w_spec, pl.BlockSpec((tm, k), lambda i, j: (i, 0)), pl.BlockSpec((1, k), lambda i, j: (0, 0)),
                  pl.BlockSpec((tm, k), lambda i, j: (i, 0))],
        out_specs=[pl.BlockSpec((tm, k), lambda i, j: (i, 0)), pl.BlockSpec((1, k), lambda i, j: (0, 0))],
        out_shape=[_SDS((m, k), F32), _SDS((1, k), F32)],
        scratch_shapes=[pltpu.VMEM((tm, k), F32)],
        name=name, compiler_params=_cp("arbitrary", "arbitrary"))(dy, w, h, g, dh_in)


def _matmul_tn(x, x_spec, dy, dy_spec, m, n_blocks, acc_shape, out_shape, out_spec, name, tm=None):
    tm = tm or _rows(m)
    nm = m // tm

    def body(x_ref, dy_ref, o_ref, acc_ref):
        @pl.when(pl.program_id(1) == 0)
        def _():
            acc_ref[...] = jnp.zeros_like(acc_ref)

        acc_ref[...] += _mm_tn(x_ref[...], dy_ref[...])

        @pl.when(pl.program_id(1) == nm - 1)
        def _():
            o_ref[...] = acc_ref[...].reshape(o_ref.shape).astype(o_ref.dtype)

    return pl.pallas_call(
        body, grid=(n_blocks, nm), in_specs=[x_spec, dy_spec], out_specs=out_spec,
        out_shape=_SDS(out_shape, _WIRE), scratch_shapes=[pltpu.VMEM(acc_shape, F32)],
        name=name, compiler_params=_cp("arbitrary", "arbitrary"))(x, dy)


def _loss_head(h, g, tgt, name):
    m, k = h.shape
    tm = _rows(m)

    def body(h_ref, g_ref, t_ref, loss_ref, dh_ref, dg_ref):
        i = pl.program_id(0)
        xv = h_ref[...]
        r = lax.rsqrt(jnp.mean(xv * xv, axis=-1, keepdims=True) + EPS)
        xh = xv * r
        gv = g_ref[...]
        err = xh * gv - t_ref[...]
        lpart = jnp.zeros((1, 128), F32) + 0.5 * jnp.sum(jnp.mean(err * err, axis=-1, keepdims=True), axis=0, keepdims=True)
        dy = err * (1.0 / k)
        gpart = jnp.sum(dy * xh, axis=0, keepdims=True)

        @pl.when(i == 0)
        def _():
            loss_ref[...] = lpart
            dg_ref[...] = gpart

        @pl.when(i > 0)
        def _():
            loss_ref[...] += lpart
            dg_ref[...] += gpart

        t = dy * gv
        dh_ref[...] = r * (t - xh * jnp.mean(t * xh, axis=-1, keepdims=True))

    return pl.pallas_call(
        body, grid=(m // tm,),
        in_specs=[pl.BlockSpec((tm, k), lambda i: (i, 0)), pl.BlockSpec((1, k), lambda i: (0, 0)),
                  pl.BlockSpec((tm, k), lambda i: (i, 0))],
        out_specs=[pl.BlockSpec((1, 128), lambda i: (0, 0)), pl.BlockSpec((tm, k), lambda i: (i, 0)),
                   pl.BlockSpec((1, k), lambda i: (0, 0))],
        out_shape=[_SDS((1, 128), F32), _SDS((m, k), F32), _SDS((1, k), F32)],
        name=name, compiler_params=_cp("arbitrary"))(h, g, tgt)


def _glu_down(gu, conv_w, conv_b, w_down, res, name):
    s = gu.shape[2]
    tm = min(s, _TM_GLU)

    def body(gu_ref, prev_ref, w_ref, b_ref, wdn_ref, r_ref, o_ref, act_ref):
        i, j = pl.program_id(0), pl.program_id(1)
        prev = jnp.where(i > 0, prev_ref[...].astype(F32), 0.0)
        ext = jnp.concatenate([prev, gu_ref[0].astype(F32)], axis=0)
        gc = b_ref[...] + w_ref[FFN_CONV - 1:FFN_CONV, :] * ext
        for k in range(FFN_CONV - 1):
            gc = gc + w_ref[k:k + 1, :] * pltpu.roll(ext, FFN_CONV - 1 - k, 0)
        act = (_silu(gc[GLU_HALO:]) * gu_ref[1].astype(F32)).astype(act_ref.dtype)
        act_ref[...] = act
        part = _mm(act, _w2d(wdn_ref))

        @pl.when(j == 0)
        def _():
            o_ref[...] = r_ref[...] + part

        @pl.when(j > 0)
        def _():
            o_ref[...] += part

    return pl.pallas_call(
        body, grid=(s // tm, FF_BLOCKS),
        in_specs=[pl.BlockSpec((2, None, tm, GU_SHARD), lambda i, j: (0, j, i, 0)),
                  pl.BlockSpec((None, None, GLU_HALO, GU_SHARD),
                               lambda i, j: (0, j, jnp.maximum(i * (tm // GLU_HALO) - 1, 0), 0)),
                  pl.BlockSpec((None, HALO, GU_SHARD), lambda i, j: (j, 0, 0)),
                  pl.BlockSpec((None, 1, GU_SHARD), lambda i, j: (j, 0, 0)),
                  _spec_down(1), pl.BlockSpec((tm, D), lambda i, j: (i, 0))],
        out_specs=[pl.BlockSpec((tm, D), lambda i, j: (i, 0)), pl.BlockSpec((None, tm, GU_SHARD), lambda i, j: (j, i, 0))],
        out_shape=[_SDS((s, D), F32), _SDS((FF_BLOCKS, s, GU_SHARD), _ACT)], name=name,
        compiler_params=_cp("arbitrary", "arbitrary"))(gu, gu, conv_w, conv_b, w_down, res)


def _glu_bwd(gu, conv_w, conv_b, dh, w_down, name, deps=()):
    s = gu.shape[2]
    tm = min(s, _TM_GLU)
    nt = s // tm
    ext_rows = tm + GLU_HALO

    def body(gu_ref, prev_ref, w_ref, b_ref, dh_ref, wdn_ref, *rest):
        dgu_ref, dw_ref, db_ref, carry_ref = rest[-4:]
        t = pl.program_id(1)
        i = nt - 1 - t

        @pl.when(t == 0)
        def _():
            carry_ref[...] = jnp.zeros_like(carry_ref)
            dw_ref[...] = jnp.zeros_like(dw_ref)
            db_ref[...] = jnp.zeros_like(db_ref)

        up = gu_ref[1].astype(F32)
        prev = jnp.where(i > 0, prev_ref[...].astype(F32), 0.0)
        ext = jnp.concatenate([prev, gu_ref[0].astype(F32)], axis=0)
        shifted = [pltpu.roll(ext, FFN_CONV - 1 - j, 0) if j < FFN_CONV - 1 else ext for j in range(FFN_CONV)]
        gc = b_ref[...] + shifted[0] * w_ref[0:1, :]
        for j in range(1, FFN_CONV):
            gc = gc + shifted[j] * w_ref[j:j + 1, :]
        gc = gc[GLU_HALO:]
        sg = jax.nn.sigmoid(gc)
        da = _mm_nt(dh_ref[...], _w2d(wdn_ref))
        dup = da * (gc * sg)
        dgc = da * up * (sg * (1.0 + gc * (1.0 - sg)))
        db_ref[...] += jnp.sum(dgc, axis=0, keepdims=True)
        dgc_ext = jnp.concatenate([jnp.zeros((GLU_HALO, GU_SHARD), F32), dgc], axis=0)
        dext = dgc_ext * w_ref[FFN_CONV - 1:FFN_CONV, :]
        for j in range(FFN_CONV):
            dw_ref[j:j + 1, :] += jnp.sum(shifted[j] * dgc_ext, axis=0, keepdims=True)
            if j < FFN_CONV - 1:
                dext = dext + w_ref[j:j + 1, :] * pltpu.roll(dgc_ext, ext_rows - (FFN_CONV - 1 - j), 0)
        tail = jnp.concatenate([jnp.zeros((tm - GLU_HALO, GU_SHARD), F32), carry_ref[...]], axis=0)
        dgate = dext[GLU_HALO:] + tail
        carry_ref[...] = dext[:GLU_HALO]
        dgu_ref[0] = dgate.astype(dgu_ref.dtype)
        dgu_ref[1] = dup.astype(dgu_ref.dtype)

    return pl.pallas_call(
        body, grid=(FF_BLOCKS, nt),
        in_specs=[pl.BlockSpec((2, None, tm, GU_SHARD), lambda j, t: (0, j, nt - 1 - t, 0)),
                  pl.BlockSpec((None, None, GLU_HALO, GU_SHARD),
                               lambda j, t: (0, j, jnp.maximum((nt - 1 - t) * (tm // GLU_HALO) - 1, 0), 0)),
                  pl.BlockSpec((None, HALO, GU_SHARD), lambda j, t: (j, 0, 0)),
                  pl.BlockSpec((None, 1, GU_SHARD), lambda j, t: (j, 0, 0)),
                  pl.BlockSpec((tm, D), lambda j, t: (nt - 1 - t, 0)), _spec_down(0)] + _dep_specs(deps),
        out_specs=[pl.BlockSpec((2, None, tm, GU_SHARD), lambda j, t: (0, j, nt - 1 - t, 0)),
                   pl.BlockSpec((None, HALO, GU_SHARD), lambda j, t: (j, 0, 0)),
                   pl.BlockSpec((None, 1, GU_SHARD), lambda j, t: (j, 0, 0))],
        out_shape=[_SDS(gu.shape, _ACT), _SDS((FF_BLOCKS, HALO, GU_SHARD), F32), _SDS((FF_BLOCKS, 1, GU_SHARD), F32)],
        scratch_shapes=[pltpu.VMEM((GLU_HALO, GU_SHARD), F32)],
        name=name, compiler_params=_cp("arbitrary", "arbitrary"))(gu, gu, conv_w, conv_b, dh, w_down, *deps)


def _bucket_table():
    qi = np.arange(BLK)[:, None]
    kj = np.arange(BLK)[None, :]
    n = np.where(kj > qi, BLK + qi - kj, qi - kj)
    max_exact = N_BUCKETS // 2
    nf = np.maximum(n, 1).astype(np.float32)
    large = max_exact + (np.log(nf / max_exact) / math.log(MAX_DISTANCE / max_exact)
                         * (N_BUCKETS - max_exact)).astype(np.int32)
    large = np.minimum(large, N_BUCKETS - 1)
    return np.where(n < max_exact, n, large).astype(np.int32)


def _lane_low():
    return lax.broadcasted_iota(jnp.int32, (1, 128), 1) < A_HD


def _swa_groups(q, kd, vd, sink, bias, upper, first):
    n = A_HEADS // A_KV_HEADS
    ng = A_KV_HEADS
    low = _lane_low()
    qm = [jnp.concatenate([jnp.where(low == (h % 2 == 0), q[g][:, (h // 2) * 128:(h // 2 + 1) * 128], 0.0) for h in range(n)], axis=0)
          for g in range(ng)]
    s2 = [_mm_nt(qm[g], kd[g]) * (A_HD ** -0.5) for g in range(ng)]
    s = [jnp.where(upper[None], s2[g][:, :BLK].reshape(n, BLK, BLK), s2[g][:, BLK:].reshape(n, BLK, BLK)) + bias[g] for g in range(ng)]
    s = [jnp.where((upper & first)[None], -jnp.inf, t) for t in s]
    m = [lax.stop_gradient(jnp.maximum(jnp.max(s[g], axis=-1, keepdims=True), sink[g])) for g in range(ng)]
    p = [jnp.exp(s[g] - m[g]) for g in range(ng)]
    split = [jnp.concatenate([jnp.where(upper[None], t, 0.0), jnp.where(upper[None], 0.0, t)], axis=-1).reshape(n * BLK, 2 * BLK)
             for t in p]
    ones = jnp.ones((BLK, 128), F32)
    den = [_mm(p[g].reshape(n * BLK, BLK), ones) + jnp.exp(sink[g] - m[g]).reshape(n * BLK, 1) for g in range(ng)]
    o = [_mm(split[g], vd[g]) / den[g] for g in range(ng)]
    return [jnp.concatenate([jnp.where(low, t[2 * k * BLK:(2 * k + 1) * BLK], t[(2 * k + 1) * BLK:(2 * k + 2) * BLK])
                             for k in range(n // 2)], axis=1) for t in o]


def _mix_a_core(q, kd, vd, sink, bias, xq, mk, mv, upper, first):
    return _swa_groups(q, kd, vd, sink, bias, upper, first), _cross_pairs(xq, mk, mv)


def _swa_sinks(sink_ref, g):
    n = A_HEADS // A_KV_HEADS
    return jnp.concatenate([sink_ref[:, h:h + 1] for h in range(g * n, (g + 1) * n)], axis=0).reshape(n, 1, 1)


def _both_halves(t, t_rolled, g):
    low = _lane_low()
    return jnp.where(low, t, t_rolled) if g == 0 else jnp.where(low, t_rolled, t)


def _cross_pairs(q, mk, mv):
    rows = q.shape[0]
    low = _lane_low()
    qm = [jnp.concatenate([jnp.where(low, q[:, p * 128:(p + 1) * 128], 0.0), jnp.where(low, 0.0, q[:, p * 128:(p + 1) * 128])], axis=0)
          for p in range(X_HEADS // 2)]
    s = [_mm_nt(qm[p], mk[:, p * 128:(p + 1) * 128]) * (X_HD ** -0.5) for p in range(X_HEADS // 2)]
    e = [jnp.exp(t - lax.stop_gradient(jnp.max(t, axis=-1, keepdims=True))) for t in s]
    pr = [t / jnp.sum(t, axis=-1, keepdims=True) for t in e]
    o = [_mm(pr[p], mv[:, p * 128:(p + 1) * 128]) for p in range(X_HEADS // 2)]
    return jnp.concatenate([jnp.where(low, t[:rows], t[rows:]) for t in o], axis=1)


def _swa_upper():
    qi = lax.broadcasted_iota(jnp.int32, (BLK, BLK), 0)
    kj = lax.broadcasted_iota(jnp.int32, (BLK, BLK), 1)
    return kj > qi


def _bias_build(rel_bias, bucket, name):
    def body(rb_ref, bucket_ref, o_ref):
        b = bucket_ref[...]
        for h in range(A_HEADS):
            acc = jnp.zeros((BLK, BLK), F32)
            for k in range(N_BUCKETS):
                acc = jnp.where(b == k, rb_ref[k, h], acc)
            o_ref[h] = acc

    return pl.pallas_call(
        body, in_specs=[pl.BlockSpec(memory_space=pltpu.SMEM), pl.BlockSpec(memory_space=pltpu.VMEM)],
        out_specs=pl.BlockSpec(memory_space=pltpu.VMEM),
        out_shape=_SDS((A_HEADS, BLK, BLK), F32), name=name)(rel_bias, bucket)


def _bias_reduce(dbias, bucket, name):
    def body(db_ref, bucket_ref, o_ref):
        b = bucket_ref[...]
        row = lax.broadcasted_iota(jnp.int32, (N_BUCKETS, 128), 0)
        lane = lax.broadcasted_iota(jnp.int32, (N_BUCKETS, 128), 1)
        acc = jnp.zeros((N_BUCKETS, 128), F32)
        for h in range(A_HEADS):
            v = db_ref[h]
            for k in range(N_BUCKETS):
                sk = jnp.sum(jnp.sum(jnp.where(b == k, v, 0.0), axis=1, keepdims=True), axis=0, keepdims=True)
                acc = acc + jnp.where((row == k) & (lane == h), sk, 0.0)
        o_ref[...] = acc

    return pl.pallas_call(
        body, in_specs=[pl.BlockSpec(memory_space=pltpu.VMEM)] * 2,
        out_specs=pl.BlockSpec(memory_space=pltpu.VMEM),
        out_shape=_SDS((N_BUCKETS, 128), F32), name=name)(dbias, bucket)


def _mix_a_fwd(proj, bias, sinks, memkv, name):
    s = proj.shape[0]
    nb = s // BLK
    grp = A_HEADS // A_KV_HEADS

    def body(proj_ref, prev_ref, bias_ref, sink_ref, memkv_ref, o_ref):
        i = pl.program_id(0)
        upper = _swa_upper()
        prev = prev_ref[...].astype(F32)
        proj = proj_ref[...].astype(F32)
        kb = jnp.concatenate([prev[:, :A_KV], proj[:, A_Q:A_Q + A_KV]], axis=0)
        vb = jnp.concatenate([prev[:, A_KV:], proj[:, A_Q + A_KV:A_Q + 2 * A_KV]], axis=0)
        kb_r = pltpu.roll(kb, A_HD, 1)
        vb_r = pltpu.roll(vb, A_HD, 1)
        gw = A_Q // A_KV_HEADS
        groups = range(A_KV_HEADS)
        swa, cross = _mix_a_core([proj[:, g * gw:(g + 1) * gw] for g in groups], [_both_halves(kb, kb_r, g) for g in groups],
                                 [_both_halves(vb, vb_r, g) for g in groups], [_swa_sinks(sink_ref, g) for g in groups],
                                 [bias_ref[g * grp:(g + 1) * grp] for g in groups], proj[:, A_Q + 2 * A_KV:],
                                 memkv_ref[:, :X_Q], memkv_ref[:, X_Q:], upper, i == 0)
        o_ref[...] = jnp.concatenate(swa + [cross], axis=1).astype(o_ref.dtype)

    return pl.pallas_call(
        body, grid=(nb,),
        in_specs=[pl.BlockSpec((BLK, IN_A), lambda i: (i, 0)),
                  pl.BlockSpec((BLK, 2 * A_KV), lambda i: (jnp.maximum(i - 1, 0), A_Q // (2 * A_KV))),
                  pl.BlockSpec((A_HEADS, BLK, BLK), lambda i: (0, 0, 0)),
                  pl.BlockSpec((1, 128), lambda i: (0, 0)),
                  pl.BlockSpec((MEM_LEN, 2 * X_Q), lambda i: (0, 0))],
        out_specs=pl.BlockSpec((BLK, D), lambda i: (i, 0)),
        out_shape=_SDS((s, D), _ACT), name=name, compiler_params=_cp("arbitrary"))(proj, proj, bias, sinks, memkv)


def _mix_a_bwd(proj, bias, sinks, memkv, dmix, name, deps=()):
    s = proj.shape[0]
    nb = s // BLK
    grp = A_HEADS // A_KV_HEADS

    def body(proj_ref, prev_ref, bias_ref, sink_ref, memkv_ref, dmix_ref, *rest):
        dproj_ref, dbias_ref, dsink_ref, dmemkv_ref, carry_ref = rest[-5:]
        t = pl.program_id(0)
        i = nb - 1 - t

        @pl.when(t == 0)
        def _():
            carry_ref[...] = jnp.zeros_like(carry_ref)
            dbias_ref[...] = jnp.zeros_like(dbias_ref)
            dsink_ref[...] = jnp.zeros_like(dsink_ref)
            dmemkv_ref[...] = jnp.zeros_like(dmemkv_ref)

        upper = _swa_upper()
        lane = lax.broadcasted_iota(jnp.int32, (1, 128), 1)
        low = _lane_low()
        prev = prev_ref[...].astype(F32)
        proj = proj_ref[...].astype(F32)
        kb = jnp.concatenate([prev[:, :A_KV], proj[:, A_Q:A_Q + A_KV]], axis=0)
        vb = jnp.concatenate([prev[:, A_KV:], proj[:, A_Q + A_KV:A_Q + 2 * A_KV]], axis=0)
        kb_r = pltpu.roll(kb, A_HD, 1)
        vb_r = pltpu.roll(vb, A_HD, 1)
        gw = A_Q // A_KV_HEADS
        groups = range(A_KV_HEADS)
        _, vjp = jax.vjp(
            functools.partial(_mix_a_core, upper=upper, first=i == 0),
            [proj[:, g * gw:(g + 1) * gw] for g in groups], [_both_halves(kb, kb_r, g) for g in groups],
            [_both_halves(vb, vb_r, g) for g in groups], [_swa_sinks(sink_ref, g) for g in groups],
            [bias_ref[g * grp:(g + 1) * grp] for g in groups], proj[:, A_Q + 2 * A_KV:], memkv_ref[:, :X_Q], memkv_ref[:, X_Q:])
        dqs, dk, dv, ds, db, dxq, dmk, dmv = vjp(
            ([dmix_ref[:, g * gw:(g + 1) * gw].astype(F32) for g in groups], dmix_ref[:, A_Q:].astype(F32)))
        dkd = [t + pltpu.roll(t, A_HD, 1) for t in dk]
        dvd = [t + pltpu.roll(t, A_HD, 1) for t in dv]
        dsink = jnp.zeros((1, 128), F32)
        for g in groups:
            for h in range(grp):
                dsink = dsink + jnp.where(lane == g * grp + h, ds[g][h], 0.0)
            dbias_ref[g * grp:(g + 1) * grp] += db[g]
        dsink_ref[...] += dsink
        dkb = jnp.where(low, dkd[0], dkd[1])
        dvb = jnp.where(low, dvd[0], dvd[1])
        dmemkv_ref[...] += jnp.concatenate([dmk, dmv], axis=1)
        dkv_cur = jnp.concatenate([dkb[BLK:], dvb[BLK:]], axis=1) + carry_ref[...]
        carry_ref[...] = jnp.concatenate([dkb[:BLK], dvb[:BLK]], axis=1)
        dproj_ref[...] = jnp.concatenate(list(dqs) + [dkv_cur, dxq], axis=1).astype(dproj_ref.dtype)

    return pl.pallas_call(
        body, grid=(nb,),
        in_specs=[pl.BlockSpec((BLK, IN_A), lambda t: (nb - 1 - t, 0)),
                  pl.BlockSpec((BLK, 2 * A_KV), lambda t: (jnp.maximum(nb - 2 - t, 0), A_Q // (2 * A_KV))),
                  pl.BlockSpec((A_HEADS, BLK, BLK), lambda t: (0, 0, 0)),
                  pl.BlockSpec((1, 128), lambda t: (0, 0)),
                  pl.BlockSpec((MEM_LEN, 2 * X_Q), lambda t: (0, 0)),
                  pl.BlockSpec((BLK, D), lambda t: (nb - 1 - t, 0))] + _dep_specs(deps),
        out_specs=[pl.BlockSpec((BLK, IN_A), lambda t: (nb - 1 - t, 0)),
                   pl.BlockSpec((A_HEADS, BLK, BLK), lambda t: (0, 0, 0)),
                   pl.BlockSpec((1, 128), lambda t: (0, 0)),
                   pl.BlockSpec((MEM_LEN, 2 * X_Q), lambda t: (0, 0))],
        out_shape=[_SDS((s, IN_A), _ACT), _SDS((A_HEADS, BLK, BLK), F32), _SDS((1, 128), F32),
                   _SDS((MEM_LEN, 2 * X_Q), F32)],
        scratch_shapes=[pltpu.VMEM((BLK, 2 * A_KV), F32)],
        name=name, compiler_params=_cp("arbitrary"))(proj, proj, bias, sinks, memkv, dmix, *deps)


def _neumann(pw, rhs):
    nh = len(pw)
    x = rhs
    for lvl in range(6):
        if lvl < 5:
            prod = [_mmf(pw[h], jnp.concatenate([x[h], pw[h]], axis=1)) for h in range(nh)]
            x = [x[h] + prod[h][:, :B_HD] for h in range(nh)]
            pw = [t[:, B_HD:] for t in prod]
        else:
            x = [x[h] + _mmf(pw[h], x[h]) for h in range(nh)]
    return x


@jax.custom_vjp
def _tri_solve(pw, rhs):
    return _neumann(pw, rhs)


def _tri_solve_fwd(pw, rhs):
    x = _neumann(pw, rhs)
    return x, (pw, x)


def _tri_solve_bwd(res, dx):
    pw, x = res
    d_rhs = _neumann([t.T for t in pw], list(dx))
    return [_mmf_nt(d_rhs[h], x[h]) for h in range(len(pw))], d_rhs


_tri_solve.defvjp(_tri_solve_fwd, _tri_solve_bwd)


@jax.custom_vjp
def _tri_solved(pw, rhs, x):
    return x


def _tri_solved_fwd(pw, rhs, x):
    return x, (pw, x)


def _tri_solved_bwd(res, dx):
    d_pw, d_rhs = _tri_solve_bwd(res, dx)
    return d_pw, d_rhs, [jnp.zeros_like(t) for t in res[1]]


_tri_solved.defvjp(_tri_solved_fwd, _tri_solved_bwd)


@jax.custom_vjp
def _known(x, value):
    return value


def _known_fwd(x, value):
    return value, None


def _known_bwd(_, g):
    return g, jnp.zeros_like(g)


_known.defvjp(_known_fwd, _known_bwd)


def _dn_heads(yq, yk, yv, z, bl, al, a_log, dtb, ng, s0, solved=None, out_known=None):
    c = CHUNK
    nh = B_V_HEADS
    rep = B_V_HEADS // B_QK_HEADS
    r = lax.broadcasted_iota(jnp.int32, (c, c), 0)
    cc = lax.broadcasted_iota(jnp.int32, (c, c), 1)
    q = [_silu(t) for t in yq]
    k = [_silu(t) for t in yk]
    v = [_silu(t) for t in yv]
    q = [t * lax.rsqrt(jnp.sum(t * t, axis=-1, keepdims=True) + EPS) * (B_HD ** -0.5) for t in q]
    k = [t * lax.rsqrt(jnp.sum(t * t, axis=-1, keepdims=True) + EPS) for t in k]
    beta = [jax.nn.sigmoid(t) for t in bl]
    g = [-jnp.exp(a_log[h]) * jax.nn.softplus(al[h] + dtb[h]) for h in range(nh)]
    gb = [jnp.broadcast_to(t, (c, c)) for t in g]
    gc_col = [jnp.sum(jnp.where(cc <= r, t.T, 0.0), axis=1, keepdims=True) for t in gb]
    gc_row = [jnp.sum(jnp.where(r <= cc, t, 0.0), axis=0, keepdims=True) for t in gb]
    gc_last = [jnp.sum(t, axis=0, keepdims=True) for t in g]
    decay = [jnp.exp(jnp.where(r >= cc, gc_col[h] - gc_row[h], -jnp.inf)) for h in range(nh)]
    kq = [_mmf_nt(jnp.concatenate([k[h], q[h]], axis=0), k[h]) for h in range(B_QK_HEADS)]
    kk = [t[:c] for t in kq]
    qk = [t[c:] for t in kq]
    egc = [jnp.exp(t) for t in gc_col]
    both = [_mmf(jnp.concatenate([(beta[h] * egc[h]) * k[h // rep], q[h // rep] * egc[h]], axis=0), s0[h]) for h in range(nh)]
    rhs = [beta[h] * v[h] - both[h][:c] for h in range(nh)]
    qs0 = [t[c:] for t in both]
    pw = [-(beta[h] * kk[h // rep] * jnp.where(r > cc, decay[h], 0.0)) for h in range(nh)]
    delta = _tri_solve(pw, rhs) if solved is None else _tri_solved(pw, rhs, solved)
    last = [_mmf(jnp.concatenate([qk[h // rep] * decay[h], (k[h // rep] * jnp.exp(gc_last[h] - gc_col[h])).T], axis=0), delta[h])
            for h in range(nh)]
    out = [qs0[h] + last[h][:c] for h in range(nh)]
    if out_known is not None:
        out = [_known(out[h], out_known[h]) for h in range(nh)]
    s1 = [jnp.exp(gc_last[h]) * s0[h] + last[h][c:] for h in range(nh)]
    o = [t * lax.rsqrt(jnp.mean(t * t, axis=-1, keepdims=True) + EPS) * ng for t in out]
    return [o[h] * _silu(z[h]) for h in range(nh)], s1, delta, out


def _dn_conv(ext, w_ref):
    y = ext * w_ref[B_CONV - 1:B_CONV, :]
    for j in range(B_CONV - 1):
        y = y + w_ref[j:j + 1, :] * pltpu.roll(ext, B_CONV - 1 - j, 0)
    return y


def _dn_args(y, cur_ref, par_ref, ng_ref):
    nh = B_V_HEADS
    return ([y[:, h * B_HD:(h + 1) * B_HD] for h in range(B_QK_HEADS)],
            [y[:, B_QK + h * B_HD:B_QK + (h + 1) * B_HD] for h in range(B_QK_HEADS)],
            [y[:, 2 * B_QK + h * B_HD:2 * B_QK + (h + 1) * B_HD] for h in range(nh)],
            [cur_ref[:, BP_Z + h * B_HD:BP_Z + (h + 1) * B_HD] for h in range(nh)],
            [cur_ref[:, BP_GATE + h:BP_GATE + h + 1] for h in range(nh)],
            [cur_ref[:, BP_GATE + nh + h:BP_GATE + nh + h + 1] for h in range(nh)],
            [par_ref[:, h:h + 1] for h in range(nh)], [par_ref[:, nh + h:nh + h + 1] for h in range(nh)], ng_ref[...])


def _mix_b_fwd(proj, conv_w, par, ng, memkv, name):
    s = proj.shape[0]
    nc = s // CHUNK

    def body(cur_ref, prev_ref, w_ref, par_ref, ng_ref, memkv_ref, o_ref, st_ref, dl_ref, state_ref):
        n = pl.program_id(0)

        @pl.when(n == 0)
        def _():
            state_ref[...] = jnp.zeros_like(state_ref)

        prev = jnp.where(n > 0, prev_ref[...], 0.0)
        ext = jnp.concatenate([prev, cur_ref[:, :B_QKV]], axis=0)
        y = _dn_conv(ext, w_ref)[HALO:]
        s0 = [state_ref[hv] for hv in range(B_V_HEADS)]
        st_ref[0] = state_ref[...]
        outs, s1, delta, raw = _dn_heads(*_dn_args(y, cur_ref, par_ref, ng_ref), s0)
        for hv in range(B_V_HEADS):
            state_ref[hv] = s1[hv]
            dl_ref[0, hv] = delta[hv]
            dl_ref[0, B_V_HEADS + hv] = raw[hv]
        outs = outs + [_cross_pairs(cur_ref[:, BP_XQ:BP_XQ + X_Q], memkv_ref[:, :X_Q], memkv_ref[:, X_Q:])]
        o_ref[...] = jnp.concatenate(outs, axis=1).astype(o_ref.dtype)

    return pl.pallas_call(
        body, grid=(nc,),
        in_specs=[pl.BlockSpec((CHUNK, IN_BP), lambda n: (n, 0)),
                  pl.BlockSpec((HALO, B_QKV), lambda n: (jnp.maximum(n * (CHUNK // HALO) - 1, 0), 0)),
                  pl.BlockSpec((HALO, B_QKV), lambda n: (0, 0)),
                  pl.BlockSpec((1, 128), lambda n: (0, 0)), pl.BlockSpec((1, 128), lambda n: (0, 0)),
                  pl.BlockSpec((MEM_LEN, 2 * X_Q), lambda n: (0, 0))],
        out_specs=[pl.BlockSpec((CHUNK, D), lambda n: (n, 0)),
                   pl.BlockSpec((1, B_V_HEADS, B_HD, B_HD), lambda n: (n, 0, 0, 0)),
                   pl.BlockSpec((1, 2 * B_V_HEADS, CHUNK, B_HD), lambda n: (n, 0, 0, 0))],
        out_shape=[_SDS((s, D), _ACT), _SDS((nc, B_V_HEADS, B_HD, B_HD), F32), _SDS((nc, 2 * B_V_HEADS, CHUNK, B_HD), F32)],
        scratch_shapes=[pltpu.VMEM((B_V_HEADS, B_HD, B_HD), F32)],
        name=name, compiler_params=_cp("arbitrary"))(proj, proj, conv_w, par, ng, memkv)


def _mix_b_bwd(proj, conv_w, par, ng, memkv, states, deltas, dmix, name):
    s = proj.shape[0]
    nc = s // CHUNK
    ext_rows = CHUNK + HALO

    def body(cur_ref, prev_ref, w_ref, par_ref, ng_ref, memkv_ref, st_ref, dl_ref, dmix_ref,
             dproj_ref, dw_ref, dpar_ref, dng_ref, dmemkv_ref, dstate_ref, carry_ref):
        t = pl.program_id(0)
        n = nc - 1 - t

        @pl.when(t == 0)
        def _():
            dstate_ref[...] = jnp.zeros_like(dstate_ref)
            carry_ref[...] = jnp.zeros_like(carry_ref)
            dw_ref[...] = jnp.zeros_like(dw_ref)
            dpar_ref[...] = jnp.zeros_like(dpar_ref)
            dng_ref[...] = jnp.zeros_like(dng_ref)
            dmemkv_ref[...] = jnp.zeros_like(dmemkv_ref)

        lane = lax.broadcasted_iota(jnp.int32, (1, 128), 1)
        prev = jnp.where(n > 0, prev_ref[...], 0.0)
        ext = jnp.concatenate([prev, cur_ref[:, :B_QKV]], axis=0)
        y = _dn_conv(ext, w_ref)[HALO:]
        solved = [dl_ref[0, hv] for hv in range(B_V_HEADS)]
        raw = [dl_ref[0, B_V_HEADS + hv] for hv in range(B_V_HEADS)]
        _, vjp = jax.vjp(functools.partial(_dn_heads, solved=solved, out_known=raw), *_dn_args(y, cur_ref, par_ref, ng_ref),
                         [st_ref[0, hv] for hv in range(B_V_HEADS)])
        none = [jnp.zeros((CHUNK, B_HD), F32)] * B_V_HEADS
        dyq, dyk, dyv, dz, gbl, gal, ga_log, gdtb, dng, gs0 = vjp(
            ([dmix_ref[:, hv * B_HD:(hv + 1) * B_HD].astype(F32) for hv in range(B_V_HEADS)],
             [dstate_ref[hv] for hv in range(B_V_HEADS)], none, none))
        dgate = jnp.zeros((CHUNK, 128), F32)
        dpar = jnp.zeros((1, 128), F32)
        for hv in range(B_V_HEADS):
            dstate_ref[hv] = gs0[hv]
            dgate = dgate + jnp.where(lane == hv, gbl[hv], 0.0) + jnp.where(lane == B_V_HEADS + hv, gal[hv], 0.0)
            dpar = dpar + jnp.where(lane == hv, ga_log[hv], 0.0) + jnp.where(lane == B_V_HEADS + hv, gdtb[hv], 0.0)
        dpar_ref[...] += dpar
        dng_ref[...] += dng
        _, vjp = jax.vjp(_cross_pairs, cur_ref[:, BP_XQ:BP_XQ + X_Q], memkv_ref[:, :X_Q], memkv_ref[:, X_Q:])
        dxq, dmk, dmv = vjp(dmix_ref[:, B_V:].astype(F32))
        dmemkv_ref[...] += jnp.concatenate([dmk, dmv], axis=1)
        dy = jnp.concatenate(list(dyq) + list(dyk) + list(dyv), axis=1)
        dy_ext = jnp.concatenate([jnp.zeros((HALO, B_QKV), F32), dy], axis=0)
        dext = dy_ext * w_ref[B_CONV - 1:B_CONV, :]
        dw_ref[B_CONV - 1:B_CONV, :] += jnp.sum(ext * dy_ext, axis=0, keepdims=True)
        for j in range(B_CONV - 1):
            sh = B_CONV - 1 - j
            dw_ref[j:j + 1, :] += jnp.sum(pltpu.roll(ext, sh, 0) * dy_ext, axis=0, keepdims=True)
            dext = dext + w_ref[j:j + 1, :] * pltpu.roll(dy_ext, ext_rows - sh, 0)
        tail = jnp.concatenate([jnp.zeros((CHUNK - HALO, B_QKV), F32), carry_ref[...]], axis=0)
        dqkv = dext[HALO:] + tail
        carry_ref[...] = dext[:HALO]
        dproj_ref[...] = jnp.concatenate([dqkv] + list(dz) + [dxq, dgate], axis=1).astype(dproj_ref.dtype)

    return pl.pallas_call(
        body, grid=(nc,),
        in_specs=[pl.BlockSpec((CHUNK, IN_BP), lambda t: (nc - 1 - t, 0)),
                  pl.BlockSpec((HALO, B_QKV), lambda t: (jnp.maximum((nc - 1 - t) * (CHUNK // HALO) - 1, 0), 0)),
                  pl.BlockSpec((HALO, B_QKV), lambda t: (0, 0)),
                  pl.BlockSpec((1, 128), lambda t: (0, 0)), pl.BlockSpec((1, 128), lambda t: (0, 0)),
                  pl.BlockSpec((MEM_LEN, 2 * X_Q), lambda t: (0, 0)),
                  pl.BlockSpec((1, B_V_HEADS, B_HD, B_HD), lambda t: (nc - 1 - t, 0, 0, 0)),
                  pl.BlockSpec((1, 2 * B_V_HEADS, CHUNK, B_HD), lambda t: (nc - 1 - t, 0, 0, 0)),
                  pl.BlockSpec((CHUNK, D), lambda t: (nc - 1 - t, 0))],
        out_specs=[pl.BlockSpec((CHUNK, IN_BP), lambda t: (nc - 1 - t, 0)),
                   pl.BlockSpec((HALO, B_QKV), lambda t: (0, 0)),
                   pl.BlockSpec((1, 128), lambda t: (0, 0)), pl.BlockSpec((1, 128), lambda t: (0, 0)),
                   pl.BlockSpec((MEM_LEN, 2 * X_Q), lambda t: (0, 0))],
        out_shape=[_SDS((s, IN_BP), _ACT), _SDS((HALO, B_QKV), F32), _SDS((1, 128), F32), _SDS((1, 128), F32),
                   _SDS((MEM_LEN, 2 * X_Q), F32)],
        scratch_shapes=[pltpu.VMEM((B_V_HEADS, B_HD, B_HD), F32), pltpu.VMEM((HALO, B_QKV), F32)],
        name=name, compiler_params=_cp("arbitrary"))(proj, proj, conv_w, par, ng, memkv, states, deltas, dmix)


def _place():
    return lax.axis_index("x"), lax.axis_index("y"), lax.axis_index("c")


def _all_gather(shards, name):
    n = len(shards)

    def body(*refs):
        ins, outs = refs[:n], refs[n:2 * n]
        send_sems, recv_sems, local_sems = refs[2 * n:]
        x, y, c = _place()
        me, sibling = (x, y, c), (x, y, 1 - c)
        chips = [(1 - x, y), (x, 1 - y), (1 - x, 1 - y)]

        def rows(a, px, py, pc):
            return outs[a].at[4 * px + 2 * py + pc]

        def copy(a, k, block, to, src=None):
            return pltpu.make_async_remote_copy(
                src_ref=rows(a, *block) if src is None else src, dst_ref=rows(a, *block),
                send_sem=send_sems.at[a, k], recv_sem=recv_sems.at[a, k],
                device_id=to, device_id_type=pl.DeviceIdType.MESH)

        mine = [pltpu.make_async_copy(ins[a], rows(a, *me), local_sems.at[a]) for a in range(n)]
        for cp in mine:
            cp.start()
        first = []
        for a in range(n):
            first.append(copy(a, 0, me, sibling, src=ins[a]))
            first += [copy(a, 1 + j, me, (*chip, c), src=ins[a]) for j, chip in enumerate(chips)]
        for cp in first:
            cp.start()
        passed = []
        for j, chip in enumerate(chips):
            for a in range(n):
                copy(a, 1 + j, (*chip, c), me).wait_recv()
                fwd = copy(a, 4 + j, (*chip, c), sibling)
                fwd.start()
                passed.append(fwd)
        for a in range(n):
            copy(a, 0, sibling, me).wait_recv()
            for j, chip in enumerate(chips):
                copy(a, 4 + j, (*chip, 1 - c), me).wait_recv()
        for cp in first + passed:
            cp.wait_send()
        for cp in mine:
            cp.wait()

    hbm = pl.BlockSpec(memory_space=pl.ANY)
    return pl.pallas_call(
        body, out_shape=[_SDS((N_DEV,) + s.shape, s.dtype) for s in shards],
        in_specs=[hbm] * n, out_specs=[hbm] * n,
        scratch_shapes=[pltpu.SemaphoreType.DMA((n, 7)), pltpu.SemaphoreType.DMA((n, 7)), pltpu.SemaphoreType.DMA((n,))],
        name=name)(*shards)


class _Exchange:
    def __init__(self, lands, srcs):
        self.lands, self.srcs = lands, srcs


def _seq_exchange(srcs, land_shapes, plan, name, cid):
    n, nl = len(srcs), len(land_shapes)

    def launch(*refs):
        src_refs, land_refs = refs[:n], refs[n:n + nl]
        send_sems, recv_sems, local_sems = refs[n + nl:]
        x, y, c = _place()
        my = 4 * x + 2 * y + c
        peers = [(x ^ ((k + 1) >> 2 & 1), y ^ ((k + 1) >> 1 & 1), c ^ ((k + 1) & 1)) for k in range(N_DEV - 1)]
        barrier = pltpu.get_barrier_semaphore()
        for p in peers:
            pl.semaphore_signal(barrier, inc=1, device_id=p, device_id_type=pl.DeviceIdType.MESH)
        pl.semaphore_wait(barrier, N_DEV - 1)

        def src_for(a, dest):
            return src_refs[a].at[dest] if plan[a][1] else src_refs[a]

        def slot(a, source):
            return land_refs[plan[a][0]].at[source]

        mine = [pltpu.make_async_copy(src_for(a, my), slot(a, my), local_sems.at[a]) for a in range(n)]
        for cp in mine:
            cp.start()
        sends, recvs = [], []
        for k, (px, py, pc) in enumerate(peers):
            peer = 4 * px + 2 * py + pc
            for a in range(n):
                kw = dict(send_sem=send_sems.at[a * (N_DEV - 1) + k], recv_sem=recv_sems.at[a * (N_DEV - 1) + k],
                          device_id=(px, py, pc), device_id_type=pl.DeviceIdType.MESH)
                sends.append(pltpu.make_async_remote_copy(src_ref=src_for(a, peer), dst_ref=slot(a, my), **kw))
                recvs.append(pltpu.make_async_remote_copy(src_ref=src_for(a, my), dst_ref=slot(a, peer), **kw))
        for cp in sends:
            cp.start()
        for cp in recvs:
            cp.wait_recv()
        for cp in sends:
            cp.wait_send()
        for cp in mine:
            cp.wait()

    lands = pl.kernel(
        launch, out_type=[_SDS(s, d) for s, d in land_shapes],
        mesh=plsc.ScalarSubcoreMesh(axis_name="sequencer", num_cores=1), name=name,
        scratch_types=(pltpu.SemaphoreType.DMA((n * (N_DEV - 1),)), pltpu.SemaphoreType.DMA((n * (N_DEV - 1),)),
                       pltpu.SemaphoreType.DMA((n,))),
        compiler_params=pltpu.CompilerParams(collective_id=cid))(*srcs)
    return _Exchange(list(lands), list(srcs))


def _adam_update(g, w, m, v):
    c1 = 1.0 - ADAM_B1 ** ADAM_STEP
    c2 = 1.0 - ADAM_B2 ** ADAM_STEP
    mm = ADAM_B1 * m + (1.0 - ADAM_B1) * g
    vv = ADAM_B2 * v + (1.0 - ADAM_B2) * (g * g)
    delta = -ADAM_LR * ((mm / c1) / (jnp.sqrt(vv / c2) + ADAM_EPS) + ADAM_WD * w)
    return delta, mm, vv


def _sum_sources(p_ref):
    g = p_ref[0].astype(F32)
    for s in range(1, N_DEV):
        g = g + p_ref[s].astype(F32)
    return g


def _adamw(parts, w, m, v, tr, name, restore_b=False, deps=()):
    nl, r, c = w.shape
    cp = parts[0].shape[-1]

    def body(*refs):
        p_refs = refs[:nl]
        w_ref, m_ref, v_ref = refs[nl:nl + 3]
        g_ref, d_ref, nm_ref, nv_ref = refs[-4:]
        g = _sum_sources(p_refs[0])
        for l in range(1, nl):
            g = jnp.where(pl.program_id(0) == l, _sum_sources(p_refs[l]), g)
        if restore_b:
            g = jnp.concatenate([g[:, :BP_XQ], g[:, BP_GATE:BP_GATE + 2 * B_V_HEADS], g[:, BP_XQ:BP_GATE]], axis=1)
        delta, mm, vv = _adam_update(g, w_ref[...], m_ref[...], v_ref[...])
        g_ref[...] = g
        d_ref[...] = delta
        nm_ref[...] = mm
        nv_ref[...] = vv

    spec = pl.BlockSpec((None, tr, c), lambda l, i: (l, i, 0))
    part_specs = [pl.BlockSpec((N_DEV, tr, cp), functools.partial(lambda l, i, k: (0, jnp.where(l == k, i, 0), 0), k=k))
                  for k in range(nl)]
    return pl.pallas_call(
        body, grid=(nl, r // tr),
        in_specs=part_specs + [spec, spec, spec] + _dep_specs(deps),
        out_specs=[spec] * 4, out_shape=[_SDS(w.shape, F32)] * 4,
        name=name, compiler_params=_cp("arbitrary", "arbitrary"))(*parts, w, m, v, *deps)


def _pack_small(d_rel, d_cb, d_cw, d_qkv, d_mix, d_mem, d_ffn, d_final, d_sinks, d_par, d_ng, loss_row, name):
    flat = [d_rel, *d_cb, *d_cw, d_qkv, *d_mix, *d_mem, *d_ffn, d_final, d_sinks, d_par, d_ng, loss_row]
    n = len(flat)

    def body(*refs):
        ins, o_ref = refs[:n], refs[n]
        rel, cb0, cb1, cw0, cw1, qkv, mx0, mx1, me0, me1, ff0, ff1, fin, snk, par, ng, lss = ins
        o_ref[...] = jnp.zeros_like(o_ref)
        for k in range(N_BUCKETS):
            lane = SP_REL_LANE + 128 * (k % 8)
            o_ref[SP_QKV + k // 8:SP_QKV + k // 8 + 1, lane:lane + 128] = rel[k:k + 1, :]
        for l, (cb, cw) in enumerate(((cb0, cw0), (cb1, cw1))):
            o_ref[SP_CB + l:SP_CB + l + 1, :] = jnp.concatenate([cb[j] for j in range(FF_BLOCKS)], axis=1)
            full = jnp.concatenate([cw[j] for j in range(FF_BLOCKS)], axis=1)
            o_ref[SP_CW + FFN_CONV * l:SP_CW + FFN_CONV * (l + 1), :] = full[:FFN_CONV]
        o_ref[SP_QKV:SP_QKV + B_CONV, 0:B_QKV] = qkv[0:B_CONV, :]
        for base, pair in ((SP_MIX, (mx0, mx1)), (SP_MEM, (me0, me1)), (SP_FFN, (ff0, ff1))):
            for l in range(2):
                o_ref[base + l:base + l + 1, 0:D] = pair[l][...]
        o_ref[SP_FINAL:SP_FINAL + 1, 0:D] = fin[...]
        o_ref[SP_MISC:SP_MISC + 1, 0:128] = snk[...]
        o_ref[SP_MISC:SP_MISC + 1, 128:256] = par[...]
        o_ref[SP_MISC:SP_MISC + 1, 256:384] = ng[...]
        o_ref[SP_MISC:SP_MISC + 1, 384:512] = lss[...]

    vm = pl.BlockSpec(memory_space=pltpu.VMEM)
    return pl.pallas_call(body, in_specs=[vm] * n, out_specs=vm, out_shape=_SDS((SMALL_ROWS, D_FF), F32), name=name)(*flat)


_SMALL = ["rel_bias", "norm_mix_g", "norm_mem_g", "sinks_a", "a_log_b", "dt_bias_b", "out_norm_g_b", "norm_ffn_g",
          "ffn_conv_b", "final_norm_g", "conv_qkv_b", "ffn_conv_w"]


def _adamw_small(recv, rc_qkv, rc_ffn, ws, ms, vs, name, deps=()):
    n = len(_SMALL)

    def body(*refs):
        recv_ref, qkv_ref, ffn_ref = refs[:3]
        w_refs, m_refs, v_refs = refs[3:3 + n], refs[3 + n:3 + 2 * n], refs[3 + 2 * n:3 + 3 * n]
        outs, loss_ref = refs[len(refs) - 4 * n - 1:len(refs) - 1], refs[-1]
        gs = _sum_sources(recv_ref)
        loss_ref[...] = gs[SP_MISC:SP_MISC + 1, 384:512]
        grads = {
            "rel_bias": jnp.concatenate(
                [gs[SP_QKV + k // 8:SP_QKV + k // 8 + 1, SP_REL_LANE + 128 * (k % 8):SP_REL_LANE + 128 * (k % 8) + A_HEADS]
                 for k in range(N_BUCKETS)], axis=0),
            "norm_mix_g": gs[SP_MIX:SP_MIX + 2, 0:D], "norm_mem_g": gs[SP_MEM:SP_MEM + 2, 0:D],
            "sinks_a": gs[SP_MISC:SP_MISC + 1, 0:A_HEADS],
            "a_log_b": gs[SP_MISC:SP_MISC + 1, 128:128 + B_V_HEADS],
            "dt_bias_b": gs[SP_MISC:SP_MISC + 1, 128 + B_V_HEADS:128 + 2 * B_V_HEADS],
            "out_norm_g_b": gs[SP_MISC:SP_MISC + 1, 256:256 + B_HD],
            "norm_ffn_g": gs[SP_FFN:SP_FFN + 2, 0:D], "ffn_conv_b": gs[SP_CB:SP_CB + 2, :],
            "final_norm_g": gs[SP_FINAL:SP_FINAL + 1, 0:D],
            "conv_qkv_b": _sum_sources(qkv_ref), "ffn_conv_w": _sum_sources(ffn_ref),
        }
        for i, nm in enumerate(_SMALL):
            g = grads[nm]
            delta, mm, vv = _adam_update(g, w_refs[i][...], m_refs[i][...], v_refs[i][...])
            outs[i][...] = g
            outs[n + i][...] = delta
            outs[2 * n + i][...] = mm
            outs[3 * n + i][...] = vv

    vm = pl.BlockSpec(memory_space=pltpu.VMEM)
    shapes = [_SDS(w.shape, F32) for w in ws]
    return pl.pallas_call(
        body, in_specs=[vm] * (3 + 3 * n) + _dep_specs(deps), out_specs=[vm] * (4 * n + 1),
        out_shape=shapes * 4 + [_SDS((1, 128), F32)],
        name=name)(recv, rc_qkv, rc_ffn, *ws, *ms, *vs, *deps)


def _assemble(gathered, axis):
    g = jnp.moveaxis(gathered, 0, axis)
    shp = list(g.shape)
    return g.reshape(shp[:axis] + [shp[axis] * shp[axis + 1]] + shp[axis + 2:])


def _pad_rows(a, rows):
    return jnp.pad(a, ((0, rows - a.shape[0]), (0, 0)))


def _pad_lanes(a, lanes=128):
    return jnp.pad(a, ((0, 0), (0, lanes - a.shape[1])))


def _ff_blocks(a):
    return jnp.moveaxis(a.reshape(a.shape[0], FF_BLOCKS, GU_SHARD), 1, 0)


def _reorder_b(w):
    qkv_z = w[..., :B_QKV + B_V]
    gates = w[..., B_QKV + B_V:B_QKV + B_V + 2 * B_V_HEADS]
    xq = w[..., IN_B - X_Q:]
    pad = jnp.zeros(w.shape[:-1] + (IN_BP - IN_B,), w.dtype)
    return jnp.concatenate([qkv_z, xq, gates, pad], axis=-1)


def kernel(x, mem, rel_bias, norm_mix_g, norm_mem_g, w_mem_kv, w_out, w_in_a, sinks_a, w_in_b, conv_qkv_b, a_log_b, dt_bias_b, out_norm_g_b, norm_ffn_g, w_gate_up, ffn_conv_w, ffn_conv_b, w_down, final_norm_g, loss_target, m_rel_bias, m_norm_mix_g, m_norm_mem_g, m_w_mem_kv, m_w_out, m_w_in_a, m_sinks_a, m_w_in_b, m_conv_qkv_b, m_a_log_b, m_dt_bias_b, m_out_norm_g_b, m_norm_ffn_g, m_w_gate_up, m_ffn_conv_w, m_ffn_conv_b, m_w_down, m_final_norm_g, v_rel_bias, v_norm_mix_g, v_norm_mem_g, v_w_mem_kv, v_w_out, v_w_in_a, v_sinks_a, v_w_in_b, v_conv_qkv_b, v_a_log_b, v_dt_bias_b, v_out_norm_g_b, v_norm_ffn_g, v_w_gate_up, v_ffn_conv_w, v_ffn_conv_b, v_w_down, v_final_norm_g):
    local = dict(locals())
    order = ["rel_bias", "norm_mix_g", "norm_mem_g", "w_mem_kv", "w_out", "w_in_a", "sinks_a", "w_in_b", "conv_qkv_b",
             "a_log_b", "dt_bias_b", "out_norm_g_b", "norm_ffn_g", "w_gate_up", "ffn_conv_w", "ffn_conv_b", "w_down",
             "final_norm_g"]
    wts = {n: local[n] for n in order}
    moms = {n: local["m_" + n] for n in order}
    vars_ = {n: local["v_" + n] for n in order}
    h0 = x[0]
    memx = mem[0]
    tgt = loss_target[0]
    s = h0.shape[0]
    tm = _rows(s)
    tb = min(s, _TM_BIG)

    t_ = lambda a: jnp.swapaxes(a, 1, 2)
    g_mk, g_out, g_ia, g_cq, g_cw = _all_gather(
        [w_mem_kv.astype(_MXU), w_out.astype(_MXU), t_(w_in_a).astype(_MXU), conv_qkv_b, ffn_conv_w], "gather_first")
    gu_land = ((N_DEV, GU_SHARD, D), _MXU)
    dn_land = ((N_DEV, DN_SHARD, D), _MXU)
    whole = [(0, False), (1, False)]
    def after(a, b):
        return a + (b[(0,) * b.ndim] * 0).astype(a.dtype)

    ffn0_w = _seq_exchange([after(t_(w_gate_up)[0].astype(_MXU), g_ia), after(w_down[0].astype(_MXU), g_ia)], [gu_land, dn_land],
                           whole, "gather_ffn0", 1)
    w_ia = g_ia.reshape(IN_A, D)
    conv_qkv = _pad_rows(_assemble(g_cq, 2)[0], HALO)
    ffn_cw_full = _assemble(g_cw, 2)
    ffn_cw = [_ff_blocks(_pad_rows(ffn_cw_full[i], HALO)) for i in range(2)]
    ffn_cb = [_ff_blocks(ffn_conv_b[i:i + 1]) for i in range(2)]
    bucket = jnp.asarray(_bucket_table())
    bias = _bias_build(rel_bias, bucket, "bias_build")
    sinks = _pad_lanes(sinks_a)
    par_b = _pad_lanes(jnp.concatenate([a_log_b, dt_bias_b], axis=1))

    row_x = pl.BlockSpec((tm, D), lambda i, j: (i, 0))
    gu_shape = (2, FF_BLOCKS, s, GU_SHARD)

    def in_proj(h, g, w, w_spec, n_cols, tn, name, deps=(), out_dtype=F32, w_t=False):
        return _norm_matmul(h, g, w, w_spec, n_cols // tn, (h.shape[0], n_cols),
                            pl.BlockSpec((_rows(h.shape[0]), tn), lambda i, j: (i, j)), name, deps=deps, out_dtype=out_dtype,
                            w_t=w_t)

    def ffn_fwd(i, h, g_gu, g_dn, deps=()):
        gu, hn = _norm_matmul(h, norm_ffn_g[i:i + 1], g_gu, _spec_gate_up(1), N_DEV, gu_shape,
                              _spec_gu_act(0, 1, tb), f"gate_up_{i}", deps=deps, out_dtype=_ACT, w_t=True, tm=tb)
        h_new, act = _glu_down(gu, ffn_cw[i], ffn_cb[i], g_dn, h, f"glu_down_{i}")
        return h_new, gu, hn, act

    def out_proj(i, mix, h):
        return _matmul_res(mix, row_x, g_out, _spec_rowsharded(i, D // N_DEV, D), 1, h, f"out_proj_{i}")

    proj_a, hn_a = in_proj(h0, norm_mix_g[0:1], w_ia, pl.BlockSpec((640, D), lambda i, j: (j, 0)), IN_A, 640, "in_proj_a",
                           deps=ffn0_w.srcs, out_dtype=_ACT, w_t=True)
    memkv0, memn0 = in_proj(memx, norm_mem_g[0:1], g_mk, _spec_rowsharded(0, D // N_DEV, 2 * X_Q), 2 * X_Q, 2 * X_Q, "mem_proj_0")
    mix_a = _mix_a_fwd(proj_a, bias, sinks, memkv0, "mix_a_fwd")
    h1 = out_proj(0, mix_a, h0)
    g_gu0, g_dn0 = ffn0_w.lands
    in_b_w = _seq_exchange([after(_reorder_b(w_in_b).astype(_MXU), h1)], [((N_DEV, 1, D // N_DEV, IN_BP), _MXU)], [(0, False)],
                           "gather_in_b", 2)
    ffn1_w = _seq_exchange([after(t_(w_gate_up)[1].astype(_MXU), h1), after(w_down[1].astype(_MXU), h1)], [gu_land, dn_land], whole,
                           "gather_ffn1", 3)
    h2, gu0, hn_f0, act0 = ffn_fwd(0, h1, g_gu0, g_dn0, deps=in_b_w.srcs + ffn1_w.srcs)
    g_ib, = in_b_w.lands
    proj_b, hn_b = in_proj(h2, norm_mix_g[1:2], g_ib, _spec_rowsharded(0, D // N_DEV, 896, col_block=1), IN_BP, 896, "in_proj_b")
    memkv1, memn1 = in_proj(memx, norm_mem_g[1:2], g_mk, _spec_rowsharded(1, D // N_DEV, 2 * X_Q), 2 * X_Q, 2 * X_Q, "mem_proj_1")
    mix_b, states, deltas = _mix_b_fwd(proj_b, conv_qkv, par_b, out_norm_g_b, memkv1, "mix_b_fwd")
    h3 = out_proj(1, mix_b, h2)
    g_gu1, g_dn1 = ffn1_w.lands
    h4, gu1, hn_f1, act1 = ffn_fwd(1, h3, g_gu1, g_dn1)
    loss_row, dh, d_final_g = _loss_head(h4, final_norm_g[None, :], tgt, "loss_head")

    zeros_mem = jnp.zeros_like(memx)
    per_dest2 = [(0, True), (1, True)]

    def ffn_bwd(i, dh, h_in, gu, hn_f, act, g_gu, g_dn, deps=()):
        dgu, d_cw, d_cb = _glu_bwd(gu, ffn_cw[i], ffn_cb[i], dh, g_dn, f"glu_bwd_{i}", deps=deps)
        d_wdown = _matmul_tn(act, pl.BlockSpec((None, tm, GU_SHARD), lambda j, r: (j, r, 0)),
                             dh, pl.BlockSpec((tm, D), lambda j, r: (r, 0)), s, FF_BLOCKS, (GU_SHARD, D),
                             (N_DEV, DN_SHARD, D), pl.BlockSpec((2, DN_SHARD, D), lambda j, r: (j, 0, 0)), f"d_w_down_{i}")
        dh_new, d_g = _matmul_nt_normbwd(dgu, _spec_gu_act(0, 1, tm), g_gu, _spec_gate_up(1), N_DEV, h_in,
                                         norm_ffn_g[i:i + 1], dh, f"d_ffn_in_{i}", w_t=True)
        d_wgu = _matmul_tn(dgu, _spec_gu_act(1, 0, tb), hn_f, pl.BlockSpec((tb, D), lambda j, r: (r, 0)), s, N_DEV,
                           (GU_SHARD, D), (N_DEV, GU_SHARD, D), pl.BlockSpec((None, GU_SHARD, D), lambda j, r: (j, 0, 0)),
                           f"d_w_gate_up_{i}", tm=tb)
        return dh_new, [d_wdown, d_wgu], d_cw, d_cb, d_g

    def out_bwd(i, dh, mix, deps):
        dmix = _matmul_nt(dh, g_out, _spec_rowsharded(i, D // N_DEV, D), 1, (s, D), row_x, f"d_mix_{i}", deps=deps, out_dtype=_ACT)
        d_wout = _matmul_tn(mix, pl.BlockSpec((tm, D), lambda j, r: (r, 0)), dh, pl.BlockSpec((tm, D), lambda j, r: (r, 0)),
                            s, 1, (D, D), (N_DEV, D // N_DEV, D), pl.BlockSpec((N_DEV, D // N_DEV, D), lambda j, r: (0, 0, 0)),
                            f"d_w_out_{i}")
        return dmix, d_wout

    def mem_bwd(i, dmemkv, memn):
        tmm = _rows(MEM_LEN)
        _, d_g = _matmul_nt_normbwd(dmemkv, pl.BlockSpec((tmm, 2 * X_Q), lambda r, j: (r, 0)), g_mk,
                                    _spec_rowsharded(i, D // N_DEV, 2 * X_Q), 1, memx, norm_mem_g[i:i + 1], zeros_mem,
                                    f"d_mem_in_{i}")
        by_row = lambda j, r: (r, 0)
        d_w = _matmul_tn(memn, pl.BlockSpec((tmm, D), by_row), dmemkv, pl.BlockSpec((tmm, 2 * X_Q), by_row), MEM_LEN, 1,
                         (D, 2 * X_Q), (N_DEV, D // N_DEV, 2 * X_Q),
                         pl.BlockSpec((N_DEV, D // N_DEV, 2 * X_Q), lambda j, r: (0, 0, 0)), f"d_w_mem_kv_{i}")
        return d_w, d_g

    out_land = ((N_DEV, D // N_DEV, D), _WIRE)
    mk_land = ((N_DEV, D // N_DEV, 2 * X_Q), _WIRE)
    ffn_lands = [((N_DEV, DN_SHARD, D), _WIRE), ((N_DEV, GU_SHARD, D), _WIRE)]
    dh, d_ffn1, d_cw1, d_cb1, d_gf1 = ffn_bwd(1, dh, h3, gu1, hn_f1, act1, g_gu1, g_dn1)
    ffn1_g = _seq_exchange(d_ffn1, ffn_lands, per_dest2, "send_ffn1_grads", 5)
    dmix, d_wout1 = out_bwd(1, dh, mix_b, ffn1_g.srcs)
    dproj_b, d_convw, d_par, d_ng, dmemkv1 = _mix_b_bwd(proj_b, conv_qkv, par_b, out_norm_g_b, memkv1, states, deltas, dmix, "mix_b_bwd")
    dh, d_gm1 = _matmul_nt_normbwd(dproj_b, pl.BlockSpec((tm, 896), lambda i, j: (i, j)), g_ib,
                                   _spec_rowsharded(0, D // N_DEV, 896, col_block=1), IN_BP // 896, h2, norm_mix_g[1:2], dh, "d_in_b")
    d_wib = _matmul_tn(hn_b, pl.BlockSpec((tm, D), lambda j, r: (r, 0)), dproj_b, pl.BlockSpec((tm, 896), lambda j, r: (r, j)),
                       s, IN_BP // 896, (D, 896), (N_DEV, D // N_DEV, IN_BP),
                       pl.BlockSpec((N_DEV, D // N_DEV, 896), lambda j, r: (0, 0, j)), "d_w_in_b")
    d_wmk1, d_gmem1 = mem_bwd(1, dmemkv1, memn1)
    mix1_g = _seq_exchange([d_wout1, d_wib, d_wmk1], [out_land, ((N_DEV, D // N_DEV, IN_BP), _WIRE), mk_land],
                           [(0, True), (1, True), (2, True)], "send_mix1_grads", 6)
    dh, d_ffn0, d_cw0, d_cb0, d_gf0 = ffn_bwd(0, dh, h1, gu0, hn_f0, act0, g_gu0, g_dn0, deps=mix1_g.srcs)
    dmix, d_wout0 = out_bwd(0, dh, mix_a, d_ffn0 + ffn1_g.lands[:1])
    ffn0_g = _seq_exchange(d_ffn0 + [d_wout0], ffn_lands + [out_land], per_dest2 + [(2, True)], "send_ffn0_grads", 4)
    dproj_a, dbias, dsinks, dmemkv0 = _mix_a_bwd(proj_a, bias, sinks, memkv0, dmix, "mix_a_bwd", deps=ffn0_g.srcs)
    dh, d_gm0 = _matmul_nt_normbwd(dproj_a, pl.BlockSpec((tm, 640), lambda i, j: (i, j)), w_ia,
                                   pl.BlockSpec((640, D), lambda i, j: (j, 0)), IN_A // 640, h0, norm_mix_g[0:1], dh, "d_in_a",
                                   w_t=True)
    d_wia = _matmul_tn(dproj_a, pl.BlockSpec((tm, IN_A), lambda j, r: (r, 0)), hn_a, pl.BlockSpec((tm, D), lambda j, r: (r, 0)),
                       s, 1, (IN_A, D), (N_DEV, IA_SHARD, D), pl.BlockSpec((N_DEV, IA_SHARD, D), lambda j, r: (0, 0, 0)),
                       "d_w_in_a")
    d_wmk0, d_gmem0 = mem_bwd(0, dmemkv0, memn0)
    d_rel = _bias_reduce(dbias, bucket, "bias_reduce")
    small = _pack_small(d_rel, (d_cb0, d_cb1), (d_cw0, d_cw1), d_convw, (d_gm0, d_gm1), (d_gmem0, d_gmem1),
                        (d_gf0, d_gf1), d_final_g, dsinks, d_par, d_ng, loss_row, "pack_small")
    mix0_g = _seq_exchange([d_wia, d_wmk0, small],
                           [((N_DEV, IA_SHARD, D), _WIRE), mk_land, ((N_DEV, SMALL_ROWS, D_FF), F32)],
                           [(0, True), (1, True), (2, False)], "send_mix0_grads", 7)

    res = {}
    last = []

    def update(nm, parts, tr, restore=False, transposed=False):
        view = t_ if transposed else (lambda a: a)
        out = _adamw(parts, view(wts[nm]), view(moms[nm]), view(vars_[nm]), tr, "adamw_" + nm, restore_b=restore, deps=last[-1:])
        res[nm] = [view(o) for o in out]
        last.append(out[1])

    r_dn1, r_gu1 = ffn1_g.lands
    r_dn0, r_gu0, r_out0 = ffn0_g.lands
    r_out1, r_ib, r_mk1 = mix1_g.lands
    update("w_gate_up", [r_gu0, r_gu1], 176, transposed=True)
    update("w_down", [r_dn0, r_dn1], 176)
    update("w_in_b", [r_ib], 32, True)
    r_ia, r_mk0, r_small = mix0_g.lands
    update("w_mem_kv", [r_mk0, r_mk1], 128)
    update("w_out", [r_out0, r_out1], 128)
    update("w_in_a", [r_ia], IA_SHARD, transposed=True)

    my = 4 * lax.axis_index("x") + 2 * lax.axis_index("y") + lax.axis_index("c")
    cq = conv_qkv_b.shape[-1]
    cf = ffn_conv_w.shape[-1]
    rc_qkv = lax.dynamic_slice_in_dim(r_small[:, SP_QKV:SP_QKV + B_CONV, :B_QKV], my * cq, cq, axis=2)[:, None]
    rc_ffn = lax.dynamic_slice_in_dim(r_small[:, SP_CW:SP_CW + 2 * FFN_CONV, :], my * cf, cf, axis=2).reshape(N_DEV, 2, FFN_CONV, cf)
    as2d = lambda a: a[None, :] if a.ndim == 1 else a
    small_out = _adamw_small(r_small, rc_qkv, rc_ffn, [as2d(wts[n]) for n in _SMALL], [as2d(moms[n]) for n in _SMALL],
                             [as2d(vars_[n]) for n in _SMALL], "adamw_small", deps=last[-1:])
    ns = len(_SMALL)
    for i, nm in enumerate(_SMALL):
        res[nm] = [small_out[k * ns + i].reshape(wts[nm].shape) for k in range(4)]

    return (small_out[-1][0, 0], dh[None], *[res[n][0] for n in order], *[res[n][1] for n in order],
            *[res[n][2] for n in order], *[res[n][3] for n in order])
```

```python
import functools
import math

import numpy as np

import jax
import jax.numpy as jnp
from jax import lax
from jax.experimental import pallas as pl
from jax.experimental.pallas import tpu as pltpu
from jax.experimental.pallas import tpu_sc as plsc

F32 = jnp.float32
_MXU = jnp.bfloat16
_ACT = jnp.bfloat16
_WIRE = jnp.bfloat16
_HI = lax.Precision.HIGH
_TM = 1024
_TM_GLU = 512
_TM_BIG = 2048
_VMEM_LIMIT = 48 * 1024 * 1024
_SDS = jax.ShapeDtypeStruct

D = 1024
EPS = 1e-6
A_HEADS, A_KV_HEADS, A_HD, BLK = 12, 2, 64, 128
N_BUCKETS, MAX_DISTANCE = 32, 128
B_QK_HEADS, B_V_HEADS, B_HD, B_CONV, CHUNK = 3, 6, 128, 4, 64
X_HEADS, X_HD, MEM_LEN = 4, 64, 256
D_FF, FFN_CONV = 2816, 3
A_Q, A_KV, X_Q = 768, 128, 256
B_QK, B_V, B_QKV = 384, 768, 1536
IN_A, IN_B = 1280, 2572
IN_BP = 2688
BP_Z, BP_XQ, BP_GATE = 1536, 2304, 2560
HALO = 8
GLU_HALO = 16

N_DEV = 8
GU_SHARD = 2 * D_FF // N_DEV
FF_BLOCKS = D_FF // GU_SHARD
DN_SHARD = D_FF // N_DEV
IA_SHARD = IN_A // N_DEV

ADAM_LR, ADAM_B1, ADAM_B2, ADAM_EPS, ADAM_WD, ADAM_STEP = 0.001, 0.9, 0.999, 1e-08, 0.01, 10

SP_CB, SP_CW, SP_QKV, SP_MIX, SP_MEM, SP_FFN, SP_FINAL, SP_MISC, SMALL_ROWS = 0, 2, 8, 12, 14, 16, 18, 19, 24
SP_REL_LANE = B_QKV


def _cp(*sems):
    return pltpu.CompilerParams(dimension_semantics=sems, vmem_limit_bytes=_VMEM_LIMIT)


def _mm(a, b):
    return jnp.dot(a.astype(_MXU), b.astype(_MXU), preferred_element_type=F32)


def _mm_nt(a, b):
    return lax.dot_general(a.astype(_MXU), b.astype(_MXU), (((1,), (1,)), ((), ())), preferred_element_type=F32)


def _mm_tn(a, b):
    return lax.dot_general(a.astype(_MXU), b.astype(_MXU), (((0,), (0,)), ((), ())), preferred_element_type=F32)


def _mmf(a, b):
    return jnp.dot(a, b, preferred_element_type=F32, precision=_HI)


def _mmf_nt(a, b):
    return lax.dot_general(a, b, (((1,), (1,)), ((), ())), preferred_element_type=F32, precision=_HI)


def _silu(x):
    return x * jax.nn.sigmoid(x)


def _w2d(ref):
    v = ref[...]
    return v.reshape(-1, v.shape[-1])


def _rows(m):
    return min(m, _TM)


def _spec_rowsharded(layer, rows, cols, col_block=None):
    if col_block is None:
        return pl.BlockSpec((N_DEV, None, rows, cols), lambda *_: (0, layer, 0, 0))
    return pl.BlockSpec((N_DEV, None, rows, cols), lambda *ids: (0, layer, 0, ids[col_block]))


def _spec_gate_up(axis):
    return pl.BlockSpec((None, GU_SHARD, D), lambda *ids: (ids[axis], 0, 0))


def _spec_down(axis):
    return pl.BlockSpec((2, DN_SHARD, D), lambda *ids: (ids[axis], 0, 0))


def _dep_specs(deps):
    return [pl.BlockSpec(memory_space=pl.ANY) for d in deps]


def _spec_gu_act(row_axis, axis, tm):
    return pl.BlockSpec((None, None, tm, GU_SHARD), lambda *ids: (ids[axis] // FF_BLOCKS, ids[axis] % FF_BLOCKS, ids[row_axis], 0))


def _norm_matmul(x, g, w, w_spec, n_blocks, out_shape, out_spec, name, deps=(), out_dtype=F32, w_t=False, tm=None):
    m, k = x.shape
    tm = tm or _rows(m)

    def body(x_ref, g_ref, w_ref, *rest):
        y_ref, hn_ref = rest[-2:]

        @pl.when(pl.program_id(1) == 0)
        def _():
            xv = x_ref[...]
            r = lax.rsqrt(jnp.mean(xv * xv, axis=-1, keepdims=True) + EPS)
            hn_ref[...] = (xv * r * g_ref[...]).astype(hn_ref.dtype)

        y_ref[...] = (_mm_nt if w_t else _mm)(hn_ref[...], _w2d(w_ref)).astype(y_ref.dtype)

    return pl.pallas_call(
        body, grid=(m // tm, n_blocks),
        in_specs=[pl.BlockSpec((tm, k), lambda i, j: (i, 0)), pl.BlockSpec((1, k), lambda i, j: (0, 0)), w_spec]
        + _dep_specs(deps),
        out_specs=[out_spec, pl.BlockSpec((tm, k), lambda i, j: (i, 0))],
        out_shape=[_SDS(out_shape, out_dtype), _SDS((m, k), _ACT)],
        name=name, compiler_params=_cp("arbitrary", "arbitrary"))(x, g, w, *deps)


def _matmul_res(a, a_spec, w, w_spec, n_k, res, name):
    m, n = res.shape
    tm = _rows(m)

    def body(a_ref, w_ref, r_ref, o_ref):
        part = _mm(a_ref[...], _w2d(w_ref))

        @pl.when(pl.program_id(1) == 0)
        def _():
            o_ref[...] = r_ref[...] + part

        @pl.when(pl.program_id(1) > 0)
        def _():
            o_ref[...] += part

    return pl.pallas_call(
        body, grid=(m // tm, n_k),
        in_specs=[a_spec, w_spec, pl.BlockSpec((tm, n), lambda i, j: (i, 0))],
        out_specs=pl.BlockSpec((tm, n), lambda i, j: (i, 0)),
        out_shape=_SDS((m, n), F32), name=name, compiler_params=_cp("arbitrary", "arbitrary"))(a, w, res)


def _matmul_nt(dy, w, w_spec, n_blocks, out_shape, out_spec, name, deps=(), out_dtype=F32):
    m, n = dy.shape
    tm = _rows(m)

    def body(dy_ref, w_ref, *rest):
        o_ref = rest[-1]
        o_ref[...] = _mm_nt(dy_ref[...], _w2d(w_ref)).astype(o_ref.dtype)

    return pl.pallas_call(
        body, grid=(m // tm, n_blocks),
        in_specs=[pl.BlockSpec((tm, n), lambda i, j: (i, 0)), w_spec] + _dep_specs(deps),
        out_specs=out_spec, out_shape=_SDS(out_shape, out_dtype),
        name=name, compiler_params=_cp("arbitrary", "arbitrary"))(dy, w, *deps)


def _matmul_nt_normbwd(dy, dy_spec, w, w_spec, nj, h, g, dh_in, name, w_t=False):
    m, k = h.shape
    tm = _rows(m)

    def body(dy_ref, w_ref, h_ref, g_ref, dhin_ref, dh_ref, dg_ref, acc_ref):
        i, j = pl.program_id(0), pl.program_id(1)

        @pl.when(j == 0)
        def _():
            acc_ref[...] = jnp.zeros_like(acc_ref)

        acc_ref[...] += (_mm if w_t else _mm_nt)(dy_ref[...], _w2d(w_ref))

        @pl.when(j == nj - 1)
        def _():
            xv = h_ref[...]
            r = lax.rsqrt(jnp.mean(xv * xv, axis=-1, keepdims=True) + EPS)
            xh = xv * r
            dhn = acc_ref[...]
            part = jnp.sum(dhn * xh, axis=0, keepdims=True)

            @pl.when(i == 0)
            def _():
                dg_ref[...] = part

            @pl.when(i > 0)
            def _():
                dg_ref[...] += part

            t = dhn * g_ref[...]
            dh_ref[...] = dhin_ref[...] + r * (t - xh * jnp.mean(t * xh, axis=-1, keepdims=True))

    return pl.pallas_call(
        body, grid=(m // tm, nj),
        in_specs=[dy_spec, w_spec, pl.BlockSpec((tm, k), lambda i, j: (i, 0)), pl.BlockSpec((1, k), lambda i, j: (0, 0)),
                  pl.BlockSpec((tm, k), lambda i, j: (i, 0))],
        out_specs=[pl.BlockSpec((tm, k), lambda i, j: (i, 0)), pl.BlockSpec((1, k), lambda i, j: (0, 0))],
        out_shape=[_SDS((m, k), F32), _SDS((1, k), F32)],
        scratch_shapes=[pltpu.VMEM((tm, k), F32)],
        name=name, compiler_params=_cp("arbitrary", "arbitrary"))(dy, w, h, g, dh_in)


def _matmul_tn(x, x_spec, dy, dy_spec, m, n_blocks, acc_shape, out_shape, out_spec, name, tm=None):
    tm = tm or _rows(m)
    nm = m // tm

    def body(x_ref, dy_ref, o_ref, acc_ref):
        @pl.when(pl.program_id(1) == 0)
        def _():
            acc_ref[...] = jnp.zeros_like(acc_ref)

        acc_ref[...] += _mm_tn(x_ref[...], dy_ref[...])

        @pl.when(pl.program_id(1) == nm - 1)
        def _():
            o_ref[...] = acc_ref[...].reshape(o_ref.shape).astype(o_ref.dtype)

    return pl.pallas_call(
        body, grid=(n_blocks, nm), in_specs=[x_spec, dy_spec], out_specs=out_spec,
        out_shape=_SDS(out_shape, _WIRE), scratch_shapes=[pltpu.VMEM(acc_shape, F32)],
        name=name, compiler_params=_cp("arbitrary", "arbitrary"))(x, dy)


def _loss_head(h, g, tgt, name):
    m, k = h.shape
    tm = _rows(m)

    def body(h_ref, g_ref, t_ref, loss_ref, dh_ref, dg_ref):
        i = pl.program_id(0)
        xv = h_ref[...]
        r = lax.rsqrt(jnp.mean(xv * xv, axis=-1, keepdims=True) + EPS)
        xh = xv * r
        gv = g_ref[...]
        err = xh * gv - t_ref[...]
        lpart = jnp.zeros((1, 128), F32) + 0.5 * jnp.sum(jnp.mean(err * err, axis=-1, keepdims=True), axis=0, keepdims=True)
        dy = err * (1.0 / k)
        gpart = jnp.sum(dy * xh, axis=0, keepdims=True)

        @pl.when(i == 0)
        def _():
            loss_ref[...] = lpart
            dg_ref[...] = gpart

        @pl.when(i > 0)
        def _():
            loss_ref[...] += lpart
            dg_ref[...] += gpart

        t = dy * gv
        dh_ref[...] = r * (t - xh * jnp.mean(t * xh, axis=-1, keepdims=True))

    return pl.pallas_call(
        body, grid=(m // tm,),
        in_specs=[pl.BlockSpec((tm, k), lambda i: (i, 0)), pl.BlockSpec((1, k), lambda i: (0, 0)),
                  pl.BlockSpec((tm, k), lambda i: (i, 0))],
        out_specs=[pl.BlockSpec((1, 128), lambda i: (0, 0)), pl.BlockSpec((tm, k), lambda i: (i, 0)),
                   pl.BlockSpec((1, k), lambda i: (0, 0))],
        out_shape=[_SDS((1, 128), F32), _SDS((m, k), F32), _SDS((1, k), F32)],
        name=name, compiler_params=_cp("arbitrary"))(h, g, tgt)


def _glu_down(gu, conv_w, conv_b, w_down, res, name):
    s = gu.shape[2]
    tm = min(s, _TM_GLU)

    def body(gu_ref, prev_ref, w_ref, b_ref, wdn_ref, r_ref, o_ref, act_ref):
        i, j = pl.program_id(0), pl.program_id(1)
        prev = jnp.where(i > 0, prev_ref[...].astype(F32), 0.0)
        ext = jnp.concatenate([prev, gu_ref[0].astype(F32)], axis=0)
        gc = b_ref[...] + w_ref[FFN_CONV - 1:FFN_CONV, :] * ext
        for k in range(FFN_CONV - 1):
            gc = gc + w_ref[k:k + 1, :] * pltpu.roll(ext, FFN_CONV - 1 - k, 0)
        act = (_silu(gc[GLU_HALO:]) * gu_ref[1].astype(F32)).astype(act_ref.dtype)
        act_ref[...] = act
        part = _mm(act, _w2d(wdn_ref))

        @pl.when(j == 0)
        def _():
            o_ref[...] = r_ref[...] + part

        @pl.when(j > 0)
        def _():
            o_ref[...] += part

    return pl.pallas_call(
        body, grid=(s // tm, FF_BLOCKS),
        in_specs=[pl.BlockSpec((2, None, tm, GU_SHARD), lambda i, j: (0, j, i, 0)),
                  pl.BlockSpec((None, None, GLU_HALO, GU_SHARD),
                               lambda i, j: (0, j, jnp.maximum(i * (tm // GLU_HALO) - 1, 0), 0)),
                  pl.BlockSpec((None, HALO, GU_SHARD), lambda i, j: (j, 0, 0)),
                  pl.BlockSpec((None, 1, GU_SHARD), lambda i, j: (j, 0, 0)),
                  _spec_down(1), pl.BlockSpec((tm, D), lambda i, j: (i, 0))],
        out_specs=[pl.BlockSpec((tm, D), lambda i, j: (i, 0)), pl.BlockSpec((None, tm, GU_SHARD), lambda i, j: (j, i, 0))],
        out_shape=[_SDS((s, D), F32), _SDS((FF_BLOCKS, s, GU_SHARD), _ACT)], name=name,
        compiler_params=_cp("arbitrary", "arbitrary"))(gu, gu, conv_w, conv_b, w_down, res)


def _glu_bwd(gu, conv_w, conv_b, dh, w_down, name, deps=()):
    s = gu.shape[2]
    tm = min(s, _TM_GLU)
    nt = s // tm
    ext_rows = tm + GLU_HALO

    def body(gu_ref, prev_ref, w_ref, b_ref, dh_ref, wdn_ref, *rest):
        dgu_ref, dw_ref, db_ref, carry_ref = rest[-4:]
        t = pl.program_id(1)
        i = nt - 1 - t

        @pl.when(t == 0)
        def _():
            carry_ref[...] = jnp.zeros_like(carry_ref)
            dw_ref[...] = jnp.zeros_like(dw_ref)
            db_ref[...] = jnp.zeros_like(db_ref)

        up = gu_ref[1].astype(F32)
        prev = jnp.where(i > 0, prev_ref[...].astype(F32), 0.0)
        ext = jnp.concatenate([prev, gu_ref[0].astype(F32)], axis=0)
        shifted = [pltpu.roll(ext, FFN_CONV - 1 - j, 0) if j < FFN_CONV - 1 else ext for j in range(FFN_CONV)]
        gc = b_ref[...] + shifted[0] * w_ref[0:1, :]
        for j in range(1, FFN_CONV):
            gc = gc + shifted[j] * w_ref[j:j + 1, :]
        gc = gc[GLU_HALO:]
        sg = jax.nn.sigmoid(gc)
        da = _mm_nt(dh_ref[...], _w2d(wdn_ref))
        dup = da * (gc * sg)
        dgc = da * up * (sg * (1.0 + gc * (1.0 - sg)))
        db_ref[...] += jnp.sum(dgc, axis=0, keepdims=True)
        dgc_ext = jnp.concatenate([jnp.zeros((GLU_HALO, GU_SHARD), F32), dgc], axis=0)
        dext = dgc_ext * w_ref[FFN_CONV - 1:FFN_CONV, :]
        for j in range(FFN_CONV):
            dw_ref[j:j + 1, :] += jnp.sum(shifted[j] * dgc_ext, axis=0, keepdims=True)
            if j < FFN_CONV - 1:
                dext = dext + w_ref[j:j + 1, :] * pltpu.roll(dgc_ext, ext_rows - (FFN_CONV - 1 - j), 0)
        tail = jnp.concatenate([jnp.zeros((tm - GLU_HALO, GU_SHARD), F32), carry_ref[...]], axis=0)
        dgate = dext[GLU_HALO:] + tail
        carry_ref[...] = dext[:GLU_HALO]
        dgu_ref[0] = dgate.astype(dgu_ref.dtype)
        dgu_ref[1] = dup.astype(dgu_ref.dtype)

    return pl.pallas_call(
        body, grid=(FF_BLOCKS, nt),
        in_specs=[pl.BlockSpec((2, None, tm, GU_SHARD), lambda j, t: (0, j, nt - 1 - t, 0)),
                  pl.BlockSpec((None, None, GLU_HALO, GU_SHARD),
                               lambda j, t: (0, j, jnp.maximum((nt - 1 - t) * (tm // GLU_HALO) - 1, 0), 0)),
                  pl.BlockSpec((None, HALO, GU_SHARD), lambda j, t: (j, 0, 0)),
                  pl.BlockSpec((None, 1, GU_SHARD), lambda j, t: (j, 0, 0)),
                  pl.BlockSpec((tm, D), lambda j, t: (nt - 1 - t, 0)), _spec_down(0)] + _dep_specs(deps),
        out_specs=[pl.BlockSpec((2, None, tm, GU_SHARD), lambda j, t: (0, j, nt - 1 - t, 0)),
                   pl.BlockSpec((None, HALO, GU_SHARD), lambda j, t: (j, 0, 0)),
                   pl.BlockSpec((None, 1, GU_SHARD), lambda j, t: (j, 0, 0))],
        out_shape=[_SDS(gu.shape, _ACT), _SDS((FF_BLOCKS, HALO, GU_SHARD), F32), _SDS((FF_BLOCKS, 1, GU_SHARD), F32)],
        scratch_shapes=[pltpu.VMEM((GLU_HALO, GU_SHARD), F32)],
        name=name, compiler_params=_cp("arbitrary", "arbitrary"))(gu, gu, conv_w, conv_b, dh, w_down, *deps)


def _bucket_table():
    qi = np.arange(BLK)[:, None]
    kj = np.arange(BLK)[None, :]
    n = np.where(kj > qi, BLK + qi - kj, qi - kj)
    max_exact = N_BUCKETS // 2
    nf = np.maximum(n, 1).astype(np.float32)
    large = max_exact + (np.log(nf / max_exact) / math.log(MAX_DISTANCE / max_exact)
                         * (N_BUCKETS - max_exact)).astype(np.int32)
    large = np.minimum(large, N_BUCKETS - 1)
    return np.where(n < max_exact, n, large).astype(np.int32)


def _lane_low():
    return lax.broadcasted_iota(jnp.int32, (1, 128), 1) < A_HD


def _swa_groups(q, kd, vd, sink, bias, upper, first):
    n = A_HEADS // A_KV_HEADS
    ng = A_KV_HEADS
    low = _lane_low()
    qm = [jnp.concatenate([jnp.where(low == (h % 2 == 0), q[g][:, (h // 2) * 128:(h // 2 + 1) * 128], 0.0) for h in range(n)], axis=0)
          for g in range(ng)]
    s2 = [_mm_nt(qm[g], kd[g]) * (A_HD ** -0.5) for g in range(ng)]
    s = [jnp.where(upper[None], s2[g][:, :BLK].reshape(n, BLK, BLK), s2[g][:, BLK:].reshape(n, BLK, BLK)) + bias[g] for g in range(ng)]
    s = [jnp.where((upper & first)[None], -jnp.inf, t) for t in s]
    m = [lax.stop_gradient(jnp.maximum(jnp.max(s[g], axis=-1, keepdims=True), sink[g])) for g in range(ng)]
    p = [jnp.exp(s[g] - m[g]) for g in range(ng)]
    split = [jnp.concatenate([jnp.where(upper[None], t, 0.0), jnp.where(upper[None], 0.0, t)], axis=-1).reshape(n * BLK, 2 * BLK)
             for t in p]
    ones = jnp.ones((BLK, 128), F32)
    den = [_mm(p[g].reshape(n * BLK, BLK), ones) + jnp.exp(sink[g] - m[g]).reshape(n * BLK, 1) for g in range(ng)]
    o = [_mm(split[g], vd[g]) / den[g] for g in range(ng)]
    return [jnp.concatenate([jnp.where(low, t[2 * k * BLK:(2 * k + 1) * BLK], t[(2 * k + 1) * BLK:(2 * k + 2) * BLK])
                             for k in range(n // 2)], axis=1) for t in o]


def _mix_a_core(q, kd, vd, sink, bias, xq, mk, mv, upper, first):
    return _swa_groups(q, kd, vd, sink, bias, upper, first), _cross_pairs(xq, mk, mv)


def _swa_sinks(sink_ref, g):
    n = A_HEADS // A_KV_HEADS
    return jnp.concatenate([sink_ref[:, h:h + 1] for h in range(g * n, (g + 1) * n)], axis=0).reshape(n, 1, 1)


def _both_halves(t, t_rolled, g):
    low = _lane_low()
    return jnp.where(low, t, t_rolled) if g == 0 else jnp.where(low, t_rolled, t)


def _cross_pairs(q, mk, mv):
    rows = q.shape[0]
    low = _lane_low()
    qm = [jnp.concatenate([jnp.where(low, q[:, p * 128:(p + 1) * 128], 0.0), jnp.where(low, 0.0, q[:, p * 128:(p + 1) * 128])], axis=0)
          for p in range(X_HEADS // 2)]
    s = [_mm_nt(qm[p], mk[:, p * 128:(p + 1) * 128]) * (X_HD ** -0.5) for p in range(X_HEADS // 2)]
    e = [jnp.exp(t - lax.stop_gradient(jnp.max(t, axis=-1, keepdims=True))) for t in s]
    pr = [t / jnp.sum(t, axis=-1, keepdims=True) for t in e]
    o = [_mm(pr[p], mv[:, p * 128:(p + 1) * 128]) for p in range(X_HEADS // 2)]
    return jnp.concatenate([jnp.where(low, t[:rows], t[rows:]) for t in o], axis=1)


def _swa_upper():
    qi = lax.broadcasted_iota(jnp.int32, (BLK, BLK), 0)
    kj = lax.broadcasted_iota(jnp.int32, (BLK, BLK), 1)
    return kj > qi


def _bias_build(rel_bias, bucket, name):
    def body(rb_ref, bucket_ref, o_ref):
        b = bucket_ref[...]
        for h in range(A_HEADS):
            acc = jnp.zeros((BLK, BLK), F32)
            for k in range(N_BUCKETS):
                acc = jnp.where(b == k, rb_ref[k, h], acc)
            o_ref[h] = acc

    return pl.pallas_call(
        body, in_specs=[pl.BlockSpec(memory_space=pltpu.SMEM), pl.BlockSpec(memory_space=pltpu.VMEM)],
        out_specs=pl.BlockSpec(memory_space=pltpu.VMEM),
        out_shape=_SDS((A_HEADS, BLK, BLK), F32), name=name)(rel_bias, bucket)


def _bias_reduce(dbias, bucket, name):
    def body(db_ref, bucket_ref, o_ref):
        b = bucket_ref[...]
        row = lax.broadcasted_iota(jnp.int32, (N_BUCKETS, 128), 0)
        lane = lax.broadcasted_iota(jnp.int32, (N_BUCKETS, 128), 1)
        acc = jnp.zeros((N_BUCKETS, 128), F32)
        for h in range(A_HEADS):
            v = db_ref[h]
            for k in range(N_BUCKETS):
                sk = jnp.sum(jnp.sum(jnp.where(b == k, v, 0.0), axis=1, keepdims=True), axis=0, keepdims=True)
                acc = acc + jnp.where((row == k) & (lane == h), sk, 0.0)
        o_ref[...] = acc

    return pl.pallas_call(
        body, in_specs=[pl.BlockSpec(memory_space=pltpu.VMEM)] * 2,
        out_specs=pl.BlockSpec(memory_space=pltpu.VMEM),
        out_shape=_SDS((N_BUCKETS, 128), F32), name=name)(dbias, bucket)


def _mix_a_fwd(proj, bias, sinks, memkv, name):
    s = proj.shape[0]
    nb = s // BLK
    grp = A_HEADS // A_KV_HEADS

    def body(proj_ref, prev_ref, bias_ref, sink_ref, memkv_ref, o_ref):
        i = pl.program_id(0)
        upper = _swa_upper()
        prev = prev_ref[...].astype(F32)
        proj = proj_ref[...].astype(F32)
        kb = jnp.concatenate([prev[:, :A_KV], proj[:, A_Q:A_Q + A_KV]], axis=0)
        vb = jnp.concatenate([prev[:, A_KV:], proj[:, A_Q + A_KV:A_Q + 2 * A_KV]], axis=0)
        kb_r = pltpu.roll(kb, A_HD, 1)
        vb_r = pltpu.roll(vb, A_HD, 1)
        gw = A_Q // A_KV_HEADS
        groups = range(A_KV_HEADS)
        swa, cross = _mix_a_core([proj[:, g * gw:(g + 1) * gw] for g in groups], [_both_halves(kb, kb_r, g) for g in groups],
                                 [_both_halves(vb, vb_r, g) for g in groups], [_swa_sinks(sink_ref, g) for g in groups],
                                 [bias_ref[g * grp:(g + 1) * grp] for g in groups], proj[:, A_Q + 2 * A_KV:],
                                 memkv_ref[:, :X_Q], memkv_ref[:, X_Q:], upper, i == 0)
        o_ref[...] = jnp.concatenate(swa + [cross], axis=1).astype(o_ref.dtype)

    return pl.pallas_call(
        body, grid=(nb,),
        in_specs=[pl.BlockSpec((BLK, IN_A), lambda i: (i, 0)),
                  pl.BlockSpec((BLK, 2 * A_KV), lambda i: (jnp.maximum(i - 1, 0), A_Q // (2 * A_KV))),
                  pl.BlockSpec((A_HEADS, BLK, BLK), lambda i: (0, 0, 0)),
                  pl.BlockSpec((1, 128), lambda i: (0, 0)),
                  pl.BlockSpec((MEM_LEN, 2 * X_Q), lambda i: (0, 0))],
        out_specs=pl.BlockSpec((BLK, D), lambda i: (i, 0)),
        out_shape=_SDS((s, D), _ACT), name=name, compiler_params=_cp("arbitrary"))(proj, proj, bias, sinks, memkv)


def _mix_a_bwd(proj, bias, sinks, memkv, dmix, name, deps=()):
    s = proj.shape[0]
    nb = s // BLK
    grp = A_HEADS // A_KV_HEADS

    def body(proj_ref, prev_ref, bias_ref, sink_ref, memkv_ref, dmix_ref, *rest):
        dproj_ref, dbias_ref, dsink_ref, dmemkv_ref, carry_ref = rest[-5:]
        t = pl.program_id(0)
        i = nb - 1 - t

        @pl.when(t == 0)
        def _():
            carry_ref[...] = jnp.zeros_like(carry_ref)
            dbias_ref[...] = jnp.zeros_like(dbias_ref)
            dsink_ref[...] = jnp.zeros_like(dsink_ref)
            dmemkv_ref[...] = jnp.zeros_like(dmemkv_ref)

        upper = _swa_upper()
        lane = lax.broadcasted_iota(jnp.int32, (1, 128), 1)
        low = _lane_low()
        prev = prev_ref[...].astype(F32)
        proj = proj_ref[...].astype(F32)
        kb = jnp.concatenate([prev[:, :A_KV], proj[:, A_Q:A_Q + A_KV]], axis=0)
        vb = jnp.concatenate([prev[:, A_KV:], proj[:, A_Q + A_KV:A_Q + 2 * A_KV]], axis=0)
        kb_r = pltpu.roll(kb, A_HD, 1)
        vb_r = pltpu.roll(vb, A_HD, 1)
        gw = A_Q // A_KV_HEADS
        groups = range(A_KV_HEADS)
        _, vjp = jax.vjp(
            functools.partial(_mix_a_core, upper=upper, first=i == 0),
            [proj[:, g * gw:(g + 1) * gw] for g in groups], [_both_halves(kb, kb_r, g) for g in groups],
            [_both_halves(vb, vb_r, g) for g in groups], [_swa_sinks(sink_ref, g) for g in groups],
            [bias_ref[g * grp:(g + 1) * grp] for g in groups], proj[:, A_Q + 2 * A_KV:], memkv_ref[:, :X_Q], memkv_ref[:, X_Q:])
        dqs, dk, dv, ds, db, dxq, dmk, dmv = vjp(
            ([dmix_ref[:, g * gw:(g + 1) * gw].astype(F32) for g in groups], dmix_ref[:, A_Q:].astype(F32)))
        dkd = [t + pltpu.roll(t, A_HD, 1) for t in dk]
        dvd = [t + pltpu.roll(t, A_HD, 1) for t in dv]
        dsink = jnp.zeros((1, 128), F32)
        for g in groups:
            for h in range(grp):
                dsink = dsink + jnp.where(lane == g * grp + h, ds[g][h], 0.0)
            dbias_ref[g * grp:(g + 1) * grp] += db[g]
        dsink_ref[...] += dsink
        dkb = jnp.where(low, dkd[0], dkd[1])
        dvb = jnp.where(low, dvd[0], dvd[1])
        dmemkv_ref[...] += jnp.concatenate([dmk, dmv], axis=1)
        dkv_cur = jnp.concatenate([dkb[BLK:], dvb[BLK:]], axis=1) + carry_ref[...]
        carry_ref[...] = jnp.concatenate([dkb[:BLK], dvb[:BLK]], axis=1)
        dproj_ref[...] = jnp.concatenate(list(dqs) + [dkv_cur, dxq], axis=1).astype(dproj_ref.dtype)

    return pl.pallas_call(
        body, grid=(nb,),
        in_specs=[pl.BlockSpec((BLK, IN_A), lambda t: (nb - 1 - t, 0)),
                  pl.BlockSpec((BLK, 2 * A_KV), lambda t: (jnp.maximum(nb - 2 - t, 0), A_Q // (2 * A_KV))),
                  pl.BlockSpec((A_HEADS, BLK, BLK), lambda t: (0, 0, 0)),
                  pl.BlockSpec((1, 128), lambda t: (0, 0)),
                  pl.BlockSpec((MEM_LEN, 2 * X_Q), lambda t: (0, 0)),
                  pl.BlockSpec((BLK, D), lambda t: (nb - 1 - t, 0))] + _dep_specs(deps),
        out_specs=[pl.BlockSpec((BLK, IN_A), lambda t: (nb - 1 - t, 0)),
                   pl.BlockSpec((A_HEADS, BLK, BLK), lambda t: (0, 0, 0)),
                   pl.BlockSpec((1, 128), lambda t: (0, 0)),
                   pl.BlockSpec((MEM_LEN, 2 * X_Q), lambda t: (0, 0))],
        out_shape=[_SDS((s, IN_A), _ACT), _SDS((A_HEADS, BLK, BLK), F32), _SDS((1, 128), F32),
                   _SDS((MEM_LEN, 2 * X_Q), F32)],
        scratch_shapes=[pltpu.VMEM((BLK, 2 * A_KV), F32)],
        name=name, compiler_params=_cp("arbitrary"))(proj, proj, bias, sinks, memkv, dmix, *deps)


def _neumann(pw, rhs):
    nh = len(pw)
    x = rhs
    for lvl in range(6):
        if lvl < 5:
            prod = [_mmf(pw[h], jnp.concatenate([x[h], pw[h]], axis=1)) for h in range(nh)]
            x = [x[h] + prod[h][:, :B_HD] for h in range(nh)]
            pw = [t[:, B_HD:] for t in prod]
        else:
            x = [x[h] + _mmf(pw[h], x[h]) for h in range(nh)]
    return x


@jax.custom_vjp
def _tri_solve(pw, rhs):
    return _neumann(pw, rhs)


def _tri_solve_fwd(pw, rhs):
    x = _neumann(pw, rhs)
    return x, (pw, x)


def _tri_solve_bwd(res, dx):
    pw, x = res
    d_rhs = _neumann([t.T for t in pw], list(dx))
    return [_mmf_nt(d_rhs[h], x[h]) for h in range(len(pw))], d_rhs


_tri_solve.defvjp(_tri_solve_fwd, _tri_solve_bwd)


@jax.custom_vjp
def _tri_solved(pw, rhs, x):
    return x


def _tri_solved_fwd(pw, rhs, x):
    return x, (pw, x)


def _tri_solved_bwd(res, dx):
    d_pw, d_rhs = _tri_solve_bwd(res, dx)
    return d_pw, d_rhs, [jnp.zeros_like(t) for t in res[1]]


_tri_solved.defvjp(_tri_solved_fwd, _tri_solved_bwd)


@jax.custom_vjp
def _known(x, value):
    return value


def _known_fwd(x, value):
    return value, None


def _known_bwd(_, g):
    return g, jnp.zeros_like(g)


_known.defvjp(_known_fwd, _known_bwd)


def _dn_heads(yq, yk, yv, z, bl, al, a_log, dtb, ng, s0, solved=None, out_known=None):
    c = CHUNK
    nh = B_V_HEADS
    rep = B_V_HEADS // B_QK_HEADS
    r = lax.broadcasted_iota(jnp.int32, (c, c), 0)
    cc = lax.broadcasted_iota(jnp.int32, (c, c), 1)
    q = [_silu(t) for t in yq]
    k = [_silu(t) for t in yk]
    v = [_silu(t) for t in yv]
    q = [t * lax.rsqrt(jnp.sum(t * t, axis=-1, keepdims=True) + EPS) * (B_HD ** -0.5) for t in q]
    k = [t * lax.rsqrt(jnp.sum(t * t, axis=-1, keepdims=True) + EPS) for t in k]
    beta = [jax.nn.sigmoid(t) for t in bl]
    g = [-jnp.exp(a_log[h]) * jax.nn.softplus(al[h] + dtb[h]) for h in range(nh)]
    gb = [jnp.broadcast_to(t, (c, c)) for t in g]
    gc_col = [jnp.sum(jnp.where(cc <= r, t.T, 0.0), axis=1, keepdims=True) for t in gb]
    gc_row = [jnp.sum(jnp.where(r <= cc, t, 0.0), axis=0, keepdims=True) for t in gb]
    gc_last = [jnp.sum(t, axis=0, keepdims=True) for t in g]
    decay = [jnp.exp(jnp.where(r >= cc, gc_col[h] - gc_row[h], -jnp.inf)) for h in range(nh)]
    kq = [_mmf_nt(jnp.concatenate([k[h], q[h]], axis=0), k[h]) for h in range(B_QK_HEADS)]
    kk = [t[:c] for t in kq]
    qk = [t[c:] for t in kq]
    egc = [jnp.exp(t) for t in gc_col]
    both = [_mmf(jnp.concatenate([(beta[h] * egc[h]) * k[h // rep], q[h // rep] * egc[h]], axis=0), s0[h]) for h in range(nh)]
    rhs = [beta[h] * v[h] - both[h][:c] for h in range(nh)]
    qs0 = [t[c:] for t in both]
    pw = [-(beta[h] * kk[h // rep] * jnp.where(r > cc, decay[h], 0.0)) for h in range(nh)]
    delta = _tri_solve(pw, rhs) if solved is None else _tri_solved(pw, rhs, solved)
    last = [_mmf(jnp.concatenate([qk[h // rep] * decay[h], (k[h // rep] * jnp.exp(gc_last[h] - gc_col[h])).T], axis=0), delta[h])
            for h in range(nh)]
    out = [qs0[h] + last[h][:c] for h in range(nh)]
    if out_known is not None:
        out = [_known(out[h], out_known[h]) for h in range(nh)]
    s1 = [jnp.exp(gc_last[h]) * s0[h] + last[h][c:] for h in range(nh)]
    o = [t * lax.rsqrt(jnp.mean(t * t, axis=-1, keepdims=True) + EPS) * ng for t in out]
    return [o[h] * _silu(z[h]) for h in range(nh)], s1, delta, out


def _dn_conv(ext, w_ref):
    y = ext * w_ref[B_CONV - 1:B_CONV, :]
    for j in range(B_CONV - 1):
        y = y + w_ref[j:j + 1, :] * pltpu.roll(ext, B_CONV - 1 - j, 0)
    return y


def _dn_args(y, cur_ref, par_ref, ng_ref):
    nh = B_V_HEADS
    return ([y[:, h * B_HD:(h + 1) * B_HD] for h in range(B_QK_HEADS)],
            [y[:, B_QK + h * B_HD:B_QK + (h + 1) * B_HD] for h in range(B_QK_HEADS)],
            [y[:, 2 * B_QK + h * B_HD:2 * B_QK + (h + 1) * B_HD] for h in range(nh)],
            [cur_ref[:, BP_Z + h * B_HD:BP_Z + (h + 1) * B_HD] for h in range(nh)],
            [cur_ref[:, BP_GATE + h:BP_GATE + h + 1] for h in range(nh)],
            [cur_ref[:, BP_GATE + nh + h:BP_GATE + nh + h + 1] for h in range(nh)],
            [par_ref[:, h:h + 1] for h in range(nh)], [par_ref[:, nh + h:nh + h + 1] for h in range(nh)], ng_ref[...])


def _mix_b_fwd(proj, conv_w, par, ng, memkv, name):
    s = proj.shape[0]
    nc = s // CHUNK

    def body(cur_ref, prev_ref, w_ref, par_ref, ng_ref, memkv_ref, o_ref, st_ref, dl_ref, state_ref):
        n = pl.program_id(0)

        @pl.when(n == 0)
        def _():
            state_ref[...] = jnp.zeros_like(state_ref)

        prev = jnp.where(n > 0, prev_ref[...], 0.0)
        ext = jnp.concatenate([prev, cur_ref[:, :B_QKV]], axis=0)
        y = _dn_conv(ext, w_ref)[HALO:]
        s0 = [state_ref[hv] for hv in range(B_V_HEADS)]
        st_ref[0] = state_ref[...]
        outs, s1, delta, raw = _dn_heads(*_dn_args(y, cur_ref, par_ref, ng_ref), s0)
        for hv in range(B_V_HEADS):
            state_ref[hv] = s1[hv]
            dl_ref[0, hv] = delta[hv]
            dl_ref[0, B_V_HEADS + hv] = raw[hv]
        outs = outs + [_cross_pairs(cur_ref[:, BP_XQ:BP_XQ + X_Q], memkv_ref[:, :X_Q], memkv_ref[:, X_Q:])]
        o_ref[...] = jnp.concatenate(outs, axis=1).astype(o_ref.dtype)

    return pl.pallas_call(
        body, grid=(nc,),
        in_specs=[pl.BlockSpec((CHUNK, IN_BP), lambda n: (n, 0)),
                  pl.BlockSpec((HALO, B_QKV), lambda n: (jnp.maximum(n * (CHUNK // HALO) - 1, 0), 0)),
                  pl.BlockSpec((HALO, B_QKV), lambda n: (0, 0)),
                  pl.BlockSpec((1, 128), lambda n: (0, 0)), pl.BlockSpec((1, 128), lambda n: (0, 0)),
                  pl.BlockSpec((MEM_LEN, 2 * X_Q), lambda n: (0, 0))],
        out_specs=[pl.BlockSpec((CHUNK, D), lambda n: (n, 0)),
                   pl.BlockSpec((1, B_V_HEADS, B_HD, B_HD), lambda n: (n, 0, 0, 0)),
                   pl.BlockSpec((1, 2 * B_V_HEADS, CHUNK, B_HD), lambda n: (n, 0, 0, 0))],
        out_shape=[_SDS((s, D), _ACT), _SDS((nc, B_V_HEADS, B_HD, B_HD), F32), _SDS((nc, 2 * B_V_HEADS, CHUNK, B_HD), F32)],
        scratch_shapes=[pltpu.VMEM((B_V_HEADS, B_HD, B_HD), F32)],
        name=name, compiler_params=_cp("arbitrary"))(proj, proj, conv_w, par, ng, memkv)


def _mix_b_bwd(proj, conv_w, par, ng, memkv, states, deltas, dmix, name):
    s = proj.shape[0]
    nc = s // CHUNK
    ext_rows = CHUNK + HALO

    def body(cur_ref, prev_ref, w_ref, par_ref, ng_ref, memkv_ref, st_ref, dl_ref, dmix_ref,
             dproj_ref, dw_ref, dpar_ref, dng_ref, dmemkv_ref, dstate_ref, carry_ref):
        t = pl.program_id(0)
        n = nc - 1 - t

        @pl.when(t == 0)
        def _():
            dstate_ref[...] = jnp.zeros_like(dstate_ref)
            carry_ref[...] = jnp.zeros_like(carry_ref)
            dw_ref[...] = jnp.zeros_like(dw_ref)
            dpar_ref[...] = jnp.zeros_like(dpar_ref)
            dng_ref[...] = jnp.zeros_like(dng_ref)
            dmemkv_ref[...] = jnp.zeros_like(dmemkv_ref)

        lane = lax.broadcasted_iota(jnp.int32, (1, 128), 1)
        prev = jnp.where(n > 0, prev_ref[...], 0.0)
        ext = jnp.concatenate([prev, cur_ref[:, :B_QKV]], axis=0)
        y = _dn_conv(ext, w_ref)[HALO:]
        solved = [dl_ref[0, hv] for hv in range(B_V_HEADS)]
        raw = [dl_ref[0, B_V_HEADS + hv] for hv in range(B_V_HEADS)]
        _, vjp = jax.vjp(functools.partial(_dn_heads, solved=solved, out_known=raw), *_dn_args(y, cur_ref, par_ref, ng_ref),
                         [st_ref[0, hv] for hv in range(B_V_HEADS)])
        none = [jnp.zeros((CHUNK, B_HD), F32)] * B_V_HEADS
        dyq, dyk, dyv, dz, gbl, gal, ga_log, gdtb, dng, gs0 = vjp(
            ([dmix_ref[:, hv * B_HD:(hv + 1) * B_HD].astype(F32) for hv in range(B_V_HEADS)],
             [dstate_ref[hv] for hv in range(B_V_HEADS)], none, none))
        dgate = jnp.zeros((CHUNK, 128), F32)
        dpar = jnp.zeros((1, 128), F32)
        for hv in range(B_V_HEADS):
            dstate_ref[hv] = gs0[hv]
            dgate = dgate + jnp.where(lane == hv, gbl[hv], 0.0) + jnp.where(lane == B_V_HEADS + hv, gal[hv], 0.0)
            dpar = dpar + jnp.where(lane == hv, ga_log[hv], 0.0) + jnp.where(lane == B_V_HEADS + hv, gdtb[hv], 0.0)
        dpar_ref[...] += dpar
        dng_ref[...] += dng
        _, vjp = jax.vjp(_cross_pairs, cur_ref[:, BP_XQ:BP_XQ + X_Q], memkv_ref[:, :X_Q], memkv_ref[:, X_Q:])
        dxq, dmk, dmv = vjp(dmix_ref[:, B_V:].astype(F32))
        dmemkv_ref[...] += jnp.concatenate([dmk, dmv], axis=1)
        dy = jnp.concatenate(list(dyq) + list(dyk) + list(dyv), axis=1)
        dy_ext = jnp.concatenate([jnp.zeros((HALO, B_QKV), F32), dy], axis=0)
        dext = dy_ext * w_ref[B_CONV - 1:B_CONV, :]
        dw_ref[B_CONV - 1:B_CONV, :] += jnp.sum(ext * dy_ext, axis=0, keepdims=True)
        for j in range(B_CONV - 1):
            sh = B_CONV - 1 - j
            dw_ref[j:j + 1, :] += jnp.sum(pltpu.roll(ext, sh, 0) * dy_ext, axis=0, keepdims=True)
            dext = dext + w_ref[j:j + 1, :] * pltpu.roll(dy_ext, ext_rows - sh, 0)
        tail = jnp.concatenate([jnp.zeros((CHUNK - HALO, B_QKV), F32), carry_ref[...]], axis=0)
        dqkv = dext[HALO:] + tail
        carry_ref[...] = dext[:HALO]
        dproj_ref[...] = jnp.concatenate([dqkv] + list(dz) + [dxq, dgate], axis=1).astype(dproj_ref.dtype)

    return pl.pallas_call(
        body, grid=(nc,),
        in_specs=[pl.BlockSpec((CHUNK, IN_BP), lambda t: (nc - 1 - t, 0)),
                  pl.BlockSpec((HALO, B_QKV), lambda t: (jnp.maximum((nc - 1 - t) * (CHUNK // HALO) - 1, 0), 0)),
                  pl.BlockSpec((HALO, B_QKV), lambda t: (0, 0)),
                  pl.BlockSpec((1, 128), lambda t: (0, 0)), pl.BlockSpec((1, 128), lambda t: (0, 0)),
                  pl.BlockSpec((MEM_LEN, 2 * X_Q), lambda t: (0, 0)),
                  pl.BlockSpec((1, B_V_HEADS, B_HD, B_HD), lambda t: (nc - 1 - t, 0, 0, 0)),
                  pl.BlockSpec((1, 2 * B_V_HEADS, CHUNK, B_HD), lambda t: (nc - 1 - t, 0, 0, 0)),
                  pl.BlockSpec((CHUNK, D), lambda t: (nc - 1 - t, 0))],
        out_specs=[pl.BlockSpec((CHUNK, IN_BP), lambda t: (nc - 1 - t, 0)),
                   pl.BlockSpec((HALO, B_QKV), lambda t: (0, 0)),
                   pl.BlockSpec((1, 128), lambda t: (0, 0)), pl.BlockSpec((1, 128), lambda t: (0, 0)),
                   pl.BlockSpec((MEM_LEN, 2 * X_Q), lambda t: (0, 0))],
        out_shape=[_SDS((s, IN_BP), _ACT), _SDS((HALO, B_QKV), F32), _SDS((1, 128), F32), _SDS((1, 128), F32),
                   _SDS((MEM_LEN, 2 * X_Q), F32)],
        scratch_shapes=[pltpu.VMEM((B_V_HEADS, B_HD, B_HD), F32), pltpu.VMEM((HALO, B_QKV), F32)],
        name=name, compiler_params=_cp("arbitrary"))(proj, proj, conv_w, par, ng, memkv, states, deltas, dmix)


def _place():
    return lax.axis_index("x"), lax.axis_index("y"), lax.axis_index("c")


def _all_gather(shards, name):
    n = len(shards)

    def body(*refs):
        ins, outs = refs[:n], refs[n:2 * n]
        send_sems, recv_sems, local_sems = refs[2 * n:]
        x, y, c = _place()
        me, sibling = (x, y, c), (x, y, 1 - c)
        chips = [(1 - x, y), (x, 1 - y), (1 - x, 1 - y)]

        def rows(a, px, py, pc):
            return outs[a].at[4 * px + 2 * py + pc]

        def copy(a, k, block, to, src=None):
            return pltpu.make_async_remote_copy(
                src_ref=rows(a, *block) if src is None else src, dst_ref=rows(a, *block),
                send_sem=send_sems.at[a, k], recv_sem=recv_sems.at[a, k],
                device_id=to, device_id_type=pl.DeviceIdType.MESH)

        mine = [pltpu.make_async_copy(ins[a], rows(a, *me), local_sems.at[a]) for a in range(n)]
        for cp in mine:
            cp.start()
        first = []
        for a in range(n):
            first.append(copy(a, 0, me, sibling, src=ins[a]))
            first += [copy(a, 1 + j, me, (*chip, c), src=ins[a]) for j, chip in enumerate(chips)]
        for cp in first:
            cp.start()
        passed = []
        for j, chip in enumerate(chips):
            for a in range(n):
                copy(a, 1 + j, (*chip, c), me).wait_recv()
                fwd = copy(a, 4 + j, (*chip, c), sibling)
                fwd.start()
                passed.append(fwd)
        for a in range(n):
            copy(a, 0, sibling, me).wait_recv()
            for j, chip in enumerate(chips):
                copy(a, 4 + j, (*chip, 1 - c), me).wait_recv()
        for cp in first + passed:
            cp.wait_send()
        for cp in mine:
            cp.wait()

    hbm = pl.BlockSpec(memory_space=pl.ANY)
    return pl.pallas_call(
        body, out_shape=[_SDS((N_DEV,) + s.shape, s.dtype) for s in shards],
        in_specs=[hbm] * n, out_specs=[hbm] * n,
        scratch_shapes=[pltpu.SemaphoreType.DMA((n, 7)), pltpu.SemaphoreType.DMA((n, 7)), pltpu.SemaphoreType.DMA((n,))],
        name=name)(*shards)


class _Exchange:
    def __init__(self, lands, srcs):
        self.lands, self.srcs = lands, srcs


def _seq_exchange(srcs, land_shapes, plan, name, cid):
    n, nl = len(srcs), len(land_shapes)

    def launch(*refs):
        src_refs, land_refs = refs[:n], refs[n:n + nl]
        send_sems, recv_sems, local_sems = refs[n + nl:]
        x, y, c = _place()
        my = 4 * x + 2 * y + c
        peers = [(x ^ ((k + 1) >> 2 & 1), y ^ ((k + 1) >> 1 & 1), c ^ ((k + 1) & 1)) for k in range(N_DEV - 1)]
        barrier = pltpu.get_barrier_semaphore()
        for p in peers:
            pl.semaphore_signal(barrier, inc=1, device_id=p, device_id_type=pl.DeviceIdType.MESH)
        pl.semaphore_wait(barrier, N_DEV - 1)

        def src_for(a, dest):
            return src_refs[a].at[dest] if plan[a][1] else src_refs[a]

        def slot(a, source):
            return land_refs[plan[a][0]].at[source]

        mine = [pltpu.make_async_copy(src_for(a, my), slot(a, my), local_sems.at[a]) for a in range(n)]
        for cp in mine:
            cp.start()
        sends, recvs = [], []
        for k, (px, py, pc) in enumerate(peers):
            peer = 4 * px + 2 * py + pc
            for a in range(n):
                kw = dict(send_sem=send_sems.at[a * (N_DEV - 1) + k], recv_sem=recv_sems.at[a * (N_DEV - 1) + k],
                          device_id=(px, py, pc), device_id_type=pl.DeviceIdType.MESH)
                sends.append(pltpu.make_async_remote_copy(src_ref=src_for(a, peer), dst_ref=slot(a, my), **kw))
                recvs.append(pltpu.make_async_remote_copy(src_ref=src_for(a, my), dst_ref=slot(a, peer), **kw))
        for cp in sends:
            cp.start()
        for cp in recvs:
            cp.wait_recv()
        for cp in sends:
            cp.wait_send()
        for cp in mine:
            cp.wait()

    lands = pl.kernel(
        launch, out_type=[_SDS(s, d) for s, d in land_shapes],
        mesh=plsc.ScalarSubcoreMesh(axis_name="sequencer", num_cores=1), name=name,
        scratch_types=(pltpu.SemaphoreType.DMA((n * (N_DEV - 1),)), pltpu.SemaphoreType.DMA((n * (N_DEV - 1),)),
                       pltpu.SemaphoreType.DMA((n,))),
        compiler_params=pltpu.CompilerParams(collective_id=cid))(*srcs)
    return _Exchange(list(lands), list(srcs))


def _adam_update(g, w, m, v):
    c1 = 1.0 - ADAM_B1 ** ADAM_STEP
    c2 = 1.0 - ADAM_B2 ** ADAM_STEP
    mm = ADAM_B1 * m + (1.0 - ADAM_B1) * g
    vv = ADAM_B2 * v + (1.0 - ADAM_B2) * (g * g)
    delta = -ADAM_LR * ((mm / c1) / (jnp.sqrt(vv / c2) + ADAM_EPS) + ADAM_WD * w)
    return delta, mm, vv


def _sum_sources(p_ref):
    g = p_ref[0].astype(F32)
    for s in range(1, N_DEV):
        g = g + p_ref[s].astype(F32)
    return g


def _adamw(parts, w, m, v, tr, name, restore_b=False, deps=()):
    nl, r, c = w.shape
    cp = parts[0].shape[-1]

    def body(*refs):
        p_refs = refs[:nl]
        w_ref, m_ref, v_ref = refs[nl:nl + 3]
        g_ref, d_ref, nm_ref, nv_ref = refs[-4:]
        g = _sum_sources(p_refs[0])
        for l in range(1, nl):
            g = jnp.where(pl.program_id(0) == l, _sum_sources(p_refs[l]), g)
        if restore_b:
            g = jnp.concatenate([g[:, :BP_XQ], g[:, BP_GATE:BP_GATE + 2 * B_V_HEADS], g[:, BP_XQ:BP_GATE]], axis=1)
        delta, mm, vv = _adam_update(g, w_ref[...], m_ref[...], v_ref[...])
        g_ref[...] = g
        d_ref[...] = delta
        nm_ref[...] = mm
        nv_ref[...] = vv

    spec = pl.BlockSpec((None, tr, c), lambda l, i: (l, i, 0))
    part_specs = [pl.BlockSpec((N_DEV, tr, cp), functools.partial(lambda l, i, k: (0, jnp.where(l == k, i, 0), 0), k=k))
                  for k in range(nl)]
    return pl.pallas_call(
        body, grid=(nl, r // tr),
        in_specs=part_specs + [spec, spec, spec] + _dep_specs(deps),
        out_specs=[spec] * 4, out_shape=[_SDS(w.shape, F32)] * 4,
        name=name, compiler_params=_cp("arbitrary", "arbitrary"))(*parts, w, m, v, *deps)


def _pack_small(d_rel, d_cb, d_cw, d_qkv, d_mix, d_mem, d_ffn, d_final, d_sinks, d_par, d_ng, loss_row, name):
    flat = [d_rel, *d_cb, *d_cw, d_qkv, *d_mix, *d_mem, *d_ffn, d_final, d_sinks, d_par, d_ng, loss_row]
    n = len(flat)

    def body(*refs):
        ins, o_ref = refs[:n], refs[n]
        rel, cb0, cb1, cw0, cw1, qkv, mx0, mx1, me0, me1, ff0, ff1, fin, snk, par, ng, lss = ins
        o_ref[...] = jnp.zeros_like(o_ref)
        for k in range(N_BUCKETS):
            lane = SP_REL_LANE + 128 * (k % 8)
            o_ref[SP_QKV + k // 8:SP_QKV + k // 8 + 1, lane:lane + 128] = rel[k:k + 1, :]
        for l, (cb, cw) in enumerate(((cb0, cw0), (cb1, cw1))):
            o_ref[SP_CB + l:SP_CB + l + 1, :] = jnp.concatenate([cb[j] for j in range(FF_BLOCKS)], axis=1)
            full = jnp.concatenate([cw[j] for j in range(FF_BLOCKS)], axis=1)
            o_ref[SP_CW + FFN_CONV * l:SP_CW + FFN_CONV * (l + 1), :] = full[:FFN_CONV]
        o_ref[SP_QKV:SP_QKV + B_CONV, 0:B_QKV] = qkv[0:B_CONV, :]
        for base, pair in ((SP_MIX, (mx0, mx1)), (SP_MEM, (me0, me1)), (SP_FFN, (ff0, ff1))):
            for l in range(2):
                o_ref[base + l:base + l + 1, 0:D] = pair[l][...]
        o_ref[SP_FINAL:SP_FINAL + 1, 0:D] = fin[...]
        o_ref[SP_MISC:SP_MISC + 1, 0:128] = snk[...]
        o_ref[SP_MISC:SP_MISC + 1, 128:256] = par[...]
        o_ref[SP_MISC:SP_MISC + 1, 256:384] = ng[...]
        o_ref[SP_MISC:SP_MISC + 1, 384:512] = lss[...]

    vm = pl.BlockSpec(memory_space=pltpu.VMEM)
    return pl.pallas_call(body, in_specs=[vm] * n, out_specs=vm, out_shape=_SDS((SMALL_ROWS, D_FF), F32), name=name)(*flat)


_SMALL = ["rel_bias", "norm_mix_g", "norm_mem_g", "sinks_a", "a_log_b", "dt_bias_b", "out_norm_g_b", "norm_ffn_g",
          "ffn_conv_b", "final_norm_g", "conv_qkv_b", "ffn_conv_w"]


def _adamw_small(recv, rc_qkv, rc_ffn, ws, ms, vs, name, deps=()):
    n = len(_SMALL)

    def body(*refs):
        recv_ref, qkv_ref, ffn_ref = refs[:3]
        w_refs, m_refs, v_refs = refs[3:3 + n], refs[3 + n:3 + 2 * n], refs[3 + 2 * n:3 + 3 * n]
        outs, loss_ref = refs[len(refs) - 4 * n - 1:len(refs) - 1], refs[-1]
        gs = _sum_sources(recv_ref)
        loss_ref[...] = gs[SP_MISC:SP_MISC + 1, 384:512]
        grads = {
            "rel_bias": jnp.concatenate(
                [gs[SP_QKV + k // 8:SP_QKV + k // 8 + 1, SP_REL_LANE + 128 * (k % 8):SP_REL_LANE + 128 * (k % 8) + A_HEADS]
                 for k in range(N_BUCKETS)], axis=0),
            "norm_mix_g": gs[SP_MIX:SP_MIX + 2, 0:D], "norm_mem_g": gs[SP_MEM:SP_MEM + 2, 0:D],
            "sinks_a": gs[SP_MISC:SP_MISC + 1, 0:A_HEADS],
            "a_log_b": gs[SP_MISC:SP_MISC + 1, 128:128 + B_V_HEADS],
            "dt_bias_b": gs[SP_MISC:SP_MISC + 1, 128 + B_V_HEADS:128 + 2 * B_V_HEADS],
            "out_norm_g_b": gs[SP_MISC:SP_MISC + 1, 256:256 + B_HD],
            "norm_ffn_g": gs[SP_FFN:SP_FFN + 2, 0:D], "ffn_conv_b": gs[SP_CB:SP_CB + 2, :],
            "final_norm_g": gs[SP_FINAL:SP_FINAL + 1, 0:D],
            "conv_qkv_b": _sum_sources(qkv_ref), "ffn_conv_w": _sum_sources(ffn_ref),
        }
        for i, nm in enumerate(_SMALL):
            g = grads[nm]
            delta, mm, vv = _adam_update(g, w_refs[i][...], m_refs[i][...], v_refs[i][...])
            outs[i][...] = g
            outs[n + i][...] = delta
            outs[2 * n + i][...] = mm
            outs[3 * n + i][...] = vv

    vm = pl.BlockSpec(memory_space=pltpu.VMEM)
    shapes = [_SDS(w.shape, F32) for w in ws]
    return pl.pallas_call(
        body, in_specs=[vm] * (3 + 3 * n) + _dep_specs(deps), out_specs=[vm] * (4 * n + 1),
        out_shape=shapes * 4 + [_SDS((1, 128), F32)],
        name=name)(recv, rc_qkv, rc_ffn, *ws, *ms, *vs, *deps)


def _assemble(gathered, axis):
    g = jnp.moveaxis(gathered, 0, axis)
    shp = list(g.shape)
    return g.reshape(shp[:axis] + [shp[axis] * shp[axis + 1]] + shp[axis + 2:])


def _pad_rows(a, rows):
    return jnp.pad(a, ((0, rows - a.shape[0]), (0, 0)))


def _pad_lanes(a, lanes=128):
    return jnp.pad(a, ((0, 0), (0, lanes - a.shape[1])))


def _ff_blocks(a):
    return jnp.moveaxis(a.reshape(a.shape[0], FF_BLOCKS, GU_SHARD), 1, 0)


def _reorder_b(w):
    qkv_z = w[..., :B_QKV + B_V]
    gates = w[..., B_QKV + B_V:B_QKV + B_V + 2 * B_V_HEADS]
    xq = w[..., IN_B - X_Q:]
    pad = jnp.zeros(w.shape[:-1] + (IN_BP - IN_B,), w.dtype)
    return jnp.concatenate([qkv_z, xq, gates, pad], axis=-1)


def kernel(x, mem, rel_bias, norm_mix_g, norm_mem_g, w_mem_kv, w_out, w_in_a, sinks_a, w_in_b, conv_qkv_b, a_log_b, dt_bias_b, out_norm_g_b, norm_ffn_g, w_gate_up, ffn_conv_w, ffn_conv_b, w_down, final_norm_g, loss_target, m_rel_bias, m_norm_mix_g, m_norm_mem_g, m_w_mem_kv, m_w_out, m_w_in_a, m_sinks_a, m_w_in_b, m_conv_qkv_b, m_a_log_b, m_dt_bias_b, m_out_norm_g_b, m_norm_ffn_g, m_w_gate_up, m_ffn_conv_w, m_ffn_conv_b, m_w_down, m_final_norm_g, v_rel_bias, v_norm_mix_g, v_norm_mem_g, v_w_mem_kv, v_w_out, v_w_in_a, v_sinks_a, v_w_in_b, v_conv_qkv_b, v_a_log_b, v_dt_bias_b, v_out_norm_g_b, v_norm_ffn_g, v_w_gate_up, v_ffn_conv_w, v_ffn_conv_b, v_w_down, v_final_norm_g):
    local = dict(locals())
    order = ["rel_bias", "norm_mix_g", "norm_mem_g", "w_mem_kv", "w_out", "w_in_a", "sinks_a", "w_in_b", "conv_qkv_b",
             "a_log_b", "dt_bias_b", "out_norm_g_b", "norm_ffn_g", "w_gate_up", "ffn_conv_w", "ffn_conv_b", "w_down",
             "final_norm_g"]
    wts = {n: local[n] for n in order}
    moms = {n: local["m_" + n] for n in order}
    vars_ = {n: local["v_" + n] for n in order}
    h0 = x[0]
    memx = mem[0]
    tgt = loss_target[0]
    s = h0.shape[0]
    tm = _rows(s)
    tb = min(s, _TM_BIG)

    t_ = lambda a: jnp.swapaxes(a, 1, 2)
    g_mk, g_out, g_ia, g_cq, g_cw = _all_gather(
        [w_mem_kv.astype(_MXU), w_out.astype(_MXU), t_(w_in_a).astype(_MXU), conv_qkv_b, ffn_conv_w], "gather_first")
    gu_land = ((N_DEV, GU_SHARD, D), _MXU)
    dn_land = ((N_DEV, DN_SHARD, D), _MXU)
    whole = [(0, False), (1, False)]
    def after(a, b):
        return a + (b[(0,) * b.ndim] * 0).astype(a.dtype)

    ffn0_w = _seq_exchange([after(t_(w_gate_up)[0].astype(_MXU), g_ia), after(w_down[0].astype(_MXU), g_ia)], [gu_land, dn_land],
                           whole, "gather_ffn0", 1)
    w_ia = g_ia.reshape(IN_A, D)
    conv_qkv = _pad_rows(_assemble(g_cq, 2)[0], HALO)
    ffn_cw_full = _assemble(g_cw, 2)
    ffn_cw = [_ff_blocks(_pad_rows(ffn_cw_full[i], HALO)) for i in range(2)]
    ffn_cb = [_ff_blocks(ffn_conv_b[i:i + 1]) for i in range(2)]
    bucket = jnp.asarray(_bucket_table())
    bias = _bias_build(rel_bias, bucket, "bias_build")
    sinks = _pad_lanes(sinks_a)
    par_b = _pad_lanes(jnp.concatenate([a_log_b, dt_bias_b], axis=1))

    row_x = pl.BlockSpec((tm, D), lambda i, j: (i, 0))
    gu_shape = (2, FF_BLOCKS, s, GU_SHARD)

    def in_proj(h, g, w, w_spec, n_cols, tn, name, deps=(), out_dtype=F32, w_t=False):
        return _norm_matmul(h, g, w, w_spec, n_cols // tn, (h.shape[0], n_cols),
                            pl.BlockSpec((_rows(h.shape[0]), tn), lambda i, j: (i, j)), name, deps=deps, out_dtype=out_dtype,
                            w_t=w_t)

    def ffn_fwd(i, h, g_gu, g_dn, deps=()):
        gu, hn = _norm_matmul(h, norm_ffn_g[i:i + 1], g_gu, _spec_gate_up(1), N_DEV, gu_shape,
                              _spec_gu_act(0, 1, tb), f"gate_up_{i}", deps=deps, out_dtype=_ACT, w_t=True, tm=tb)
        h_new, act = _glu_down(gu, ffn_cw[i], ffn_cb[i], g_dn, h, f"glu_down_{i}")
        return h_new, gu, hn, act

    def out_proj(i, mix, h):
        return _matmul_res(mix, row_x, g_out, _spec_rowsharded(i, D // N_DEV, D), 1, h, f"out_proj_{i}")

    proj_a, hn_a = in_proj(h0, norm_mix_g[0:1], w_ia, pl.BlockSpec((640, D), lambda i, j: (j, 0)), IN_A, 640, "in_proj_a",
                           deps=ffn0_w.srcs, out_dtype=_ACT, w_t=True)
    memkv0, memn0 = in_proj(memx, norm_mem_g[0:1], g_mk, _spec_rowsharded(0, D // N_DEV, 2 * X_Q), 2 * X_Q, 2 * X_Q, "mem_proj_0")
    mix_a = _mix_a_fwd(proj_a, bias, sinks, memkv0, "mix_a_fwd")
    h1 = out_proj(0, mix_a, h0)
    g_gu0, g_dn0 = ffn0_w.lands
    in_b_w = _seq_exchange([after(_reorder_b(w_in_b).astype(_MXU), h1)], [((N_DEV, 1, D // N_DEV, IN_BP), _MXU)], [(0, False)],
                           "gather_in_b", 2)
    ffn1_w = _seq_exchange([after(t_(w_gate_up)[1].astype(_MXU), h1), after(w_down[1].astype(_MXU), h1)], [gu_land, dn_land], whole,
                           "gather_ffn1", 3)
    h2, gu0, hn_f0, act0 = ffn_fwd(0, h1, g_gu0, g_dn0, deps=in_b_w.srcs + ffn1_w.srcs)
    g_ib, = in_b_w.lands
    proj_b, hn_b = in_proj(h2, norm_mix_g[1:2], g_ib, _spec_rowsharded(0, D // N_DEV, 896, col_block=1), IN_BP, 896, "in_proj_b")
    memkv1, memn1 = in_proj(memx, norm_mem_g[1:2], g_mk, _spec_rowsharded(1, D // N_DEV, 2 * X_Q), 2 * X_Q, 2 * X_Q, "mem_proj_1")
    mix_b, states, deltas = _mix_b_fwd(proj_b, conv_qkv, par_b, out_norm_g_b, memkv1, "mix_b_fwd")
    h3 = out_proj(1, mix_b, h2)
    g_gu1, g_dn1 = ffn1_w.lands
    h4, gu1, hn_f1, act1 = ffn_fwd(1, h3, g_gu1, g_dn1)
    loss_row, dh, d_final_g = _loss_head(h4, final_norm_g[None, :], tgt, "loss_head")

    zeros_mem = jnp.zeros_like(memx)
    per_dest2 = [(0, True), (1, True)]

    def ffn_bwd(i, dh, h_in, gu, hn_f, act, g_gu, g_dn, deps=()):
        dgu, d_cw, d_cb = _glu_bwd(gu, ffn_cw[i], ffn_cb[i], dh, g_dn, f"glu_bwd_{i}", deps=deps)
        d_wdown = _matmul_tn(act, pl.BlockSpec((None, tm, GU_SHARD), lambda j, r: (j, r, 0)),
                             dh, pl.BlockSpec((tm, D), lambda j, r: (r, 0)), s, FF_BLOCKS, (GU_SHARD, D),
                             (N_DEV, DN_SHARD, D), pl.BlockSpec((2, DN_SHARD, D), lambda j, r: (j, 0, 0)), f"d_w_down_{i}")
        dh_new, d_g = _matmul_nt_normbwd(dgu, _spec_gu_act(0, 1, tm), g_gu, _spec_gate_up(1), N_DEV, h_in,
                                         norm_ffn_g[i:i + 1], dh, f"d_ffn_in_{i}", w_t=True)
        d_wgu = _matmul_tn(dgu, _spec_gu_act(1, 0, tb), hn_f, pl.BlockSpec((tb, D), lambda j, r: (r, 0)), s, N_DEV,
                           (GU_SHARD, D), (N_DEV, GU_SHARD, D), pl.BlockSpec((None, GU_SHARD, D), lambda j, r: (j, 0, 0)),
                           f"d_w_gate_up_{i}", tm=tb)
        return dh_new, [d_wdown, d_wgu], d_cw, d_cb, d_g

    def out_bwd(i, dh, mix, deps):
        dmix = _matmul_nt(dh, g_out, _spec_rowsharded(i, D // N_DEV, D), 1, (s, D), row_x, f"d_mix_{i}", deps=deps, out_dtype=_ACT)
        d_wout = _matmul_tn(mix, pl.BlockSpec((tm, D), lambda j, r: (r, 0)), dh, pl.BlockSpec((tm, D), lambda j, r: (r, 0)),
                            s, 1, (D, D), (N_DEV, D // N_DEV, D), pl.BlockSpec((N_DEV, D // N_DEV, D), lambda j, r: (0, 0, 0)),
                            f"d_w_out_{i}")
        return dmix, d_wout

    def mem_bwd(i, dmemkv, memn):
        tmm = _rows(MEM_LEN)
        _, d_g = _matmul_nt_normbwd(dmemkv, pl.BlockSpec((tmm, 2 * X_Q), lambda r, j: (r, 0)), g_mk,
                                    _spec_rowsharded(i, D // N_DEV, 2 * X_Q), 1, memx, norm_mem_g[i:i + 1], zeros_mem,
                                    f"d_mem_in_{i}")
        by_row = lambda j, r: (r, 0)
        d_w = _matmul_tn(memn, pl.BlockSpec((tmm, D), by_row), dmemkv, pl.BlockSpec((tmm, 2 * X_Q), by_row), MEM_LEN, 1,
                         (D, 2 * X_Q), (N_DEV, D // N_DEV, 2 * X_Q),
                         pl.BlockSpec((N_DEV, D // N_DEV, 2 * X_Q), lambda j, r: (0, 0, 0)), f"d_w_mem_kv_{i}")
        return d_w, d_g

    out_land = ((N_DEV, D // N_DEV, D), _WIRE)
    mk_land = ((N_DEV, D // N_DEV, 2 * X_Q), _WIRE)
    ffn_lands = [((N_DEV, DN_SHARD, D), _WIRE), ((N_DEV, GU_SHARD, D), _WIRE)]
    dh, d_ffn1, d_cw1, d_cb1, d_gf1 = ffn_bwd(1, dh, h3, gu1, hn_f1, act1, g_gu1, g_dn1)
    ffn1_g = _seq_exchange(d_ffn1, ffn_lands, per_dest2, "send_ffn1_grads", 5)
    dmix, d_wout1 = out_bwd(1, dh, mix_b, ffn1_g.srcs)
    dproj_b, d_convw, d_par, d_ng, dmemkv1 = _mix_b_bwd(proj_b, conv_qkv, par_b, out_norm_g_b, memkv1, states, deltas, dmix, "mix_b_bwd")
    dh, d_gm1 = _matmul_nt_normbwd(dproj_b, pl.BlockSpec((tm, 896), lambda i, j: (i, j)), g_ib,
                                   _spec_rowsharded(0, D // N_DEV, 896, col_block=1), IN_BP // 896, h2, norm_mix_g[1:2], dh, "d_in_b")
    d_wib = _matmul_tn(hn_b, pl.BlockSpec((tm, D), lambda j, r: (r, 0)), dproj_b, pl.BlockSpec((tm, 896), lambda j, r: (r, j)),
                       s, IN_BP // 896, (D, 896), (N_DEV, D // N_DEV, IN_BP),
                       pl.BlockSpec((N_DEV, D // N_DEV, 896), lambda j, r: (0, 0, j)), "d_w_in_b")
    d_wmk1, d_gmem1 = mem_bwd(1, dmemkv1, memn1)
    mix1_g = _seq_exchange([d_wout1, d_wib, d_wmk1], [out_land, ((N_DEV, D // N_DEV, IN_BP), _WIRE), mk_land],
                           [(0, True), (1, True), (2, True)], "send_mix1_grads", 6)
    dh, d_ffn0, d_cw0, d_cb0, d_gf0 = ffn_bwd(0, dh, h1, gu0, hn_f0, act0, g_gu0, g_dn0, deps=mix1_g.srcs)
    dmix, d_wout0 = out_bwd(0, dh, mix_a, d_ffn0 + ffn1_g.lands[:1])
    ffn0_g = _seq_exchange(d_ffn0 + [d_wout0], ffn_lands + [out_land], per_dest2 + [(2, True)], "send_ffn0_grads", 4)
    dproj_a, dbias, dsinks, dmemkv0 = _mix_a_bwd(proj_a, bias, sinks, memkv0, dmix, "mix_a_bwd", deps=ffn0_g.srcs)
    dh, d_gm0 = _matmul_nt_normbwd(dproj_a, pl.BlockSpec((tm, 640), lambda i, j: (i, j)), w_ia,
                                   pl.BlockSpec((640, D), lambda i, j: (j, 0)), IN_A // 640, h0, norm_mix_g[0:1], dh, "d_in_a",
                                   w_t=True)
    d_wia = _matmul_tn(dproj_a, pl.BlockSpec((tm, IN_A), lambda j, r: (r, 0)), hn_a, pl.BlockSpec((tm, D), lambda j, r: (r, 0)),
                       s, 1, (IN_A, D), (N_DEV, IA_SHARD, D), pl.BlockSpec((N_DEV, IA_SHARD, D), lambda j, r: (0, 0, 0)),
                       "d_w_in_a")
    d_wmk0, d_gmem0 = mem_bwd(0, dmemkv0, memn0)
    d_rel = _bias_reduce(dbias, bucket, "bias_reduce")
    small = _pack_small(d_rel, (d_cb0, d_cb1), (d_cw0, d_cw1), d_convw, (d_gm0, d_gm1), (d_gmem0, d_gmem1),
                        (d_gf0, d_gf1), d_final_g, dsinks, d_par, d_ng, loss_row, "pack_small")
    mix0_g = _seq_exchange([d_wia, d_wmk0, small],
                           [((N_DEV, IA_SHARD, D), _WIRE), mk_land, ((N_DEV, SMALL_ROWS, D_FF), F32)],
                           [(0, True), (1, True), (2, False)], "send_mix0_grads", 7)

    res = {}
    last = []

    def update(nm, parts, tr, restore=False, transposed=False):
        view = t_ if transposed else (lambda a: a)
        out = _adamw(parts, view(wts[nm]), view(moms[nm]), view(vars_[nm]), tr, "adamw_" + nm, restore_b=restore, deps=last[-1:])
        res[nm] = [view(o) for o in out]
        last.append(out[1])

    r_dn1, r_gu1 = ffn1_g.lands
    r_dn0, r_gu0, r_out0 = ffn0_g.lands
    r_out1, r_ib, r_mk1 = mix1_g.lands
    update("w_in_b", [r_ib], 32, True)
    update("w_gate_up", [r_gu0, r_gu1], 176, transposed=True)
    update("w_down", [r_dn0, r_dn1], 176)
    r_ia, r_mk0, r_small = mix0_g.lands
    update("w_mem_kv", [r_mk0, r_mk1], 128)
    update("w_out", [r_out0, r_out1], 128)
    update("w_in_a", [r_ia], IA_SHARD, transposed=True)

    my = 4 * lax.axis_index("x") + 2 * lax.axis_index("y") + lax.axis_index("c")
    cq = conv_qkv_b.shape[-1]
    cf = ffn_conv_w.shape[-1]
    rc_qkv = lax.dynamic_slice_in_dim(r_small[:, SP_QKV:SP_QKV + B_CONV, :B_QKV], my * cq, cq, axis=2)[:, None]
    rc_ffn = lax.dynamic_slice_in_dim(r_small[:, SP_CW:SP_CW + 2 * FFN_CONV, :], my * cf, cf, axis=2).reshape(N_DEV, 2, FFN_CONV, cf)
    as2d = lambda a: a[None, :] if a.ndim == 1 else a
    small_out = _adamw_small(r_small, rc_qkv, rc_ffn, [as2d(wts[n]) for n in _SMALL], [as2d(moms[n]) for n in _SMALL],
                             [as2d(vars_[n]) for n in _SMALL], "adamw_small", deps=last[-1:])
    ns = len(_SMALL)
    for i, nm in enumerate(_SMALL):
        res[nm] = [small_out[k * ns + i].reshape(wts[nm].shape) for k in range(4)]

    return (small_out[-1][0, 0], dh[None], *[res[n][0] for n in order], *[res[n][1] for n in order],
            *[res[n][2] for n in order], *[res[n][3] for n in order])
```

```python
import functools
import math

import numpy as np

import jax
import jax.numpy as jnp
from jax import lax
from jax.experimental import pallas as pl
from jax.experimental.pallas import tpu as pltpu
from jax.experimental.pallas import tpu_sc as plsc

F32 = jnp.float32
_MXU = jnp.bfloat16
_ACT = jnp.bfloat16
_WIRE = jnp.bfloat16
_HI = lax.Precision.HIGH
_TM = 1024
_TM_GLU = 512
_TM_BIG = 2048
_VMEM_LIMIT = 48 * 1024 * 1024
_SDS = jax.ShapeDtypeStruct

D = 1024
EPS = 1e-6
A_HEADS, A_KV_HEADS, A_HD, BLK = 12, 2, 64, 128
N_BUCKETS, MAX_DISTANCE = 32, 128
B_QK_HEADS, B_V_HEADS, B_HD, B_CONV, CHUNK = 3, 6, 128, 4, 64
X_HEADS, X_HD, MEM_LEN = 4, 64, 256
D_FF, FFN_CONV = 2816, 3
A_Q, A_KV, X_Q = 768, 128, 256
B_QK, B_V, B_QKV = 384, 768, 1536
IN_A, IN_B = 1280, 2572
IN_BP = 2688
BP_Z, BP_XQ, BP_GATE = 1536, 2304, 2560
HALO = 8
GLU_HALO = 16

N_DEV = 8
GU_SHARD = 2 * D_FF // N_DEV
FF_BLOCKS = D_FF // GU_SHARD
DN_SHARD = D_FF // N_DEV
IA_SHARD = IN_A // N_DEV

ADAM_LR, ADAM_B1, ADAM_B2, ADAM_EPS, ADAM_WD, ADAM_STEP = 0.001, 0.9, 0.999, 1e-08, 0.01, 10

SP_CB, SP_CW, SP_QKV, SP_MIX, SP_MEM, SP_FFN, SP_FINAL, SP_MISC, SMALL_ROWS = 0, 2, 8, 12, 14, 16, 18, 19, 24
SP_REL_LANE = B_QKV


def _cp(*sems):
    return pltpu.CompilerParams(dimension_semantics=sems, vmem_limit_bytes=_VMEM_LIMIT)


def _mm(a, b):
    return jnp.dot(a.astype(_MXU), b.astype(_MXU), preferred_element_type=F32)


def _mm_nt(a, b):
    return lax.dot_general(a.astype(_MXU), b.astype(_MXU), (((1,), (1,)), ((), ())), preferred_element_type=F32)


def _mm_tn(a, b):
    return lax.dot_general(a.astype(_MXU), b.astype(_MXU), (((0,), (0,)), ((), ())), preferred_element_type=F32)


def _mmf(a, b):
    return jnp.dot(a, b, preferred_element_type=F32, precision=_HI)


def _mmf_nt(a, b):
    return lax.dot_general(a, b, (((1,), (1,)), ((), ())), preferred_element_type=F32, precision=_HI)


def _silu(x):
    return x * jax.nn.sigmoid(x)


def _w2d(ref):
    v = ref[...]
    return v.reshape(-1, v.shape[-1])


def _rows(m):
    return min(m, _TM)


def _spec_rowsharded(layer, rows, cols, col_block=None):
    if col_block is None:
        return pl.BlockSpec((N_DEV, None, rows, cols), lambda *_: (0, layer, 0, 0))
    return pl.BlockSpec((N_DEV, None, rows, cols), lambda *ids: (0, layer, 0, ids[col_block]))


def _spec_gate_up(axis):
    return pl.BlockSpec((None, GU_SHARD, D), lambda *ids: (ids[axis], 0, 0))


def _spec_down(axis):
    return pl.BlockSpec((2, DN_SHARD, D), lambda *ids: (ids[axis], 0, 0))


def _dep_specs(deps):
    return [pl.BlockSpec(memory_space=pl.ANY) for d in deps]


def _spec_gu_act(row_axis, axis, tm):
    return pl.BlockSpec((None, None, tm, GU_SHARD), lambda *ids: (ids[axis] // FF_BLOCKS, ids[axis] % FF_BLOCKS, ids[row_axis], 0))


def _norm_matmul(x, g, w, w_spec, n_blocks, out_shape, out_spec, name, deps=(), out_dtype=F32, w_t=False, tm=None):
    m, k = x.shape
    tm = tm or _rows(m)

    def body(x_ref, g_ref, w_ref, *rest):
        y_ref, hn_ref = rest[-2:]

        @pl.when(pl.program_id(1) == 0)
        def _():
            xv = x_ref[...]
            r = lax.rsqrt(jnp.mean(xv * xv, axis=-1, keepdims=True) + EPS)
            hn_ref[...] = (xv * r * g_ref[...]).astype(hn_ref.dtype)

        y_ref[...] = (_mm_nt if w_t else _mm)(hn_ref[...], _w2d(w_ref)).astype(y_ref.dtype)

    return pl.pallas_call(
        body, grid=(m // tm, n_blocks),
        in_specs=[pl.BlockSpec((tm, k), lambda i, j: (i, 0)), pl.BlockSpec((1, k), lambda i, j: (0, 0)), w_spec]
        + _dep_specs(deps),
        out_specs=[out_spec, pl.BlockSpec((tm, k), lambda i, j: (i, 0))],
        out_shape=[_SDS(out_shape, out_dtype), _SDS((m, k), _ACT)],
        name=name, compiler_params=_cp("arbitrary", "arbitrary"))(x, g, w, *deps)


def _matmul_res(a, a_spec, w, w_spec, n_k, res, name):
    m, n = res.shape
    tm = _rows(m)

    def body(a_ref, w_ref, r_ref, o_ref):
        part = _mm(a_ref[...], _w2d(w_ref))

        @pl.when(pl.program_id(1) == 0)
        def _():
            o_ref[...] = r_ref[...] + part

        @pl.when(pl.program_id(1) > 0)
        def _():
            o_ref[...] += part

    return pl.pallas_call(
        body, grid=(m // tm, n_k),
        in_specs=[a_spec, w_spec, pl.BlockSpec((tm, n), lambda i, j: (i, 0))],
        out_specs=pl.BlockSpec((tm, n), lambda i, j: (i, 0)),
        out_shape=_SDS((m, n), F32), name=name, compiler_params=_cp("arbitrary", "arbitrary"))(a, w, res)


def _matmul_nt(dy, w, w_spec, n_blocks, out_shape, out_spec, name, deps=(), out_dtype=F32):
    m, n = dy.shape
    tm = _rows(m)

    def body(dy_ref, w_ref, *rest):
        o_ref = rest[-1]
        o_ref[...] = _mm_nt(dy_ref[...], _w2d(w_ref)).astype(o_ref.dtype)

    return pl.pallas_call(
        body, grid=(m // tm, n_blocks),
        in_specs=[pl.BlockSpec((tm, n), lambda i, j: (i, 0)), w_spec] + _dep_specs(deps),
        out_specs=out_spec, out_shape=_SDS(out_shape, out_dtype),
        name=name, compiler_params=_cp("arbitrary", "arbitrary"))(dy, w, *deps)


def _matmul_nt_normbwd(dy, dy_spec, w, w_spec, nj, h, g, dh_in, name, w_t=False):
    m, k = h.shape
    tm = _rows(m)

    def body(dy_ref, w_ref, h_ref, g_ref, dhin_ref, dh_ref, dg_ref, acc_ref):
        i, j = pl.program_id(0), pl.program_id(1)

        @pl.when(j == 0)
        def _():
            acc_ref[...] = jnp.zeros_like(acc_ref)

        acc_ref[...] += (_mm if w_t else _mm_nt)(dy_ref[...], _w2d(w_ref))

        @pl.when(j == nj - 1)
        def _():
            xv = h_ref[...]
            r = lax.rsqrt(jnp.mean(xv * xv, axis=-1, keepdims=True) + EPS)
            xh = xv * r
            dhn = acc_ref[...]
            part = jnp.sum(dhn * xh, axis=0, keepdims=True)

            @pl.when(i == 0)
            def _():
                dg_ref[...] = part

            @pl.when(i > 0)
            def _():
                dg_ref[...] += part

            t = dhn * g_ref[...]
            dh_ref[...] = dhin_ref[...] + r * (t - xh * jnp.mean(t * xh, axis=-1, keepdims=True))

    return pl.pallas_call(
        body, grid=(m // tm, nj),
        in_specs=[dy_spec, w_spec, pl.BlockSpec((tm, k), lambda i, j: (i, 0)), pl.BlockSpec((1, k), lambda i, j: (0, 0)),
                  pl.BlockSpec((tm, k), lambda i, j: (i, 0))],
        out_specs=[pl.BlockSpec((tm, k), lambda i, j: (i, 0)), pl.BlockSpec((1, k), lambda i, j: (0, 0))],
        out_shape=[_SDS((m, k), F32), _SDS((1, k), F32)],
        scratch_shapes=[pltpu.VMEM((tm, k), F32)],
        name=name, compiler_params=_cp("arbitrary", "arbitrary"))(dy, w, h, g, dh_in)


def _matmul_tn(x, x_spec, dy, dy_spec, m, n_blocks, acc_shape, out_shape, out_spec, name, tm=None):
    tm = tm or _rows(m)
    nm = m // tm

    def body(x_ref, dy_ref, o_ref, acc_ref):
        @pl.when(pl.program_id(1) == 0)
        def _():
            acc_ref[...] = jnp.zeros_like(acc_ref)

        acc_ref[...] += _mm_tn(x_ref[...], dy_ref[...])

        @pl.when(pl.program_id(1) == nm - 1)
        def _():
            o_ref[...] = acc_ref[...].reshape(o_ref.shape).astype(o_ref.dtype)

    return pl.pallas_call(
        body, grid=(n_blocks, nm), in_specs=[x_spec, dy_spec], out_specs=out_spec,
        out_shape=_SDS(out_shape, _WIRE), scratch_shapes=[pltpu.VMEM(acc_shape, F32)],
        name=name, compiler_params=_cp("arbitrary", "arbitrary"))(x, dy)


def _loss_head(h, g, tgt, name):
    m, k = h.shape
    tm = _rows(m)

    def body(h_ref, g_ref, t_ref, loss_ref, dh_ref, dg_ref):
        i = pl.program_id(0)
        xv = h_ref[...]
        r = lax.rsqrt(jnp.mean(xv * xv, axis=-1, keepdims=True) + EPS)
        xh = xv * r
        gv = g_ref[...]
        err = xh * gv - t_ref[...]
        lpart = jnp.zeros((1, 128), F32) + 0.5 * jnp.sum(jnp.mean(err * err, axis=-1, keepdims=True), axis=0, keepdims=True)
        dy = err * (1.0 / k)
        gpart = jnp.sum(dy * xh, axis=0, keepdims=True)

        @pl.when(i == 0)
        def _():
            loss_ref[...] = lpart
            dg_ref[...] = gpart

        @pl.when(i > 0)
        def _():
            loss_ref[...] += lpart
            dg_ref[...] += gpart

        t = dy * gv
        dh_ref[...] = r * (t - xh * jnp.mean(t * xh, axis=-1, keepdims=True))

    return pl.pallas_call(
        body, grid=(m // tm,),
        in_specs=[pl.BlockSpec((tm, k), lambda i: (i, 0)), pl.BlockSpec((1, k), lambda i: (0, 0)),
                  pl.BlockSpec((tm, k), lambda i: (i, 0))],
        out_specs=[pl.BlockSpec((1, 128), lambda i: (0, 0)), pl.BlockSpec((tm, k), lambda i: (i, 0)),
                   pl.BlockSpec((1, k), lambda i: (0, 0))],
        out_shape=[_SDS((1, 128), F32), _SDS((m, k), F32), _SDS((1, k), F32)],
        name=name, compiler_params=_cp("arbitrary"))(h, g, tgt)


def _glu_down(gu, conv_w, conv_b, w_down, res, name):
    s = gu.shape[2]
    tm = min(s, _TM_GLU)

    def body(gu_ref, prev_ref, w_ref, b_ref, wdn_ref, r_ref, o_ref, act_ref):
        i, j = pl.program_id(0), pl.program_id(1)
        prev = jnp.where(i > 0, prev_ref[...].astype(F32), 0.0)
        ext = jnp.concatenate([prev, gu_ref[0].astype(F32)], axis=0)
        gc = b_ref[...] + w_ref[FFN_CONV - 1:FFN_CONV, :] * ext
        for k in range(FFN_CONV - 1):
            gc = gc + w_ref[k:k + 1, :] * pltpu.roll(ext, FFN_CONV - 1 - k, 0)
        act = (_silu(gc[GLU_HALO:]) * gu_ref[1].astype(F32)).astype(act_ref.dtype)
        act_ref[...] = act
        part = _mm(act, _w2d(wdn_ref))

        @pl.when(j == 0)
        def _():
            o_ref[...] = r_ref[...] + part

        @pl.when(j > 0)
        def _():
            o_ref[...] += part

    return pl.pallas_call(
        body, grid=(s // tm, FF_BLOCKS),
        in_specs=[pl.BlockSpec((2, None, tm, GU_SHARD), lambda i, j: (0, j, i, 0)),
                  pl.BlockSpec((None, None, GLU_HALO, GU_SHARD),
                               lambda i, j: (0, j, jnp.maximum(i * (tm // GLU_HALO) - 1, 0), 0)),
                  pl.BlockSpec((None, HALO, GU_SHARD), lambda i, j: (j, 0, 0)),
                  pl.BlockSpec((None, 1, GU_SHARD), lambda i, j: (j, 0, 0)),
                  _spec_down(1), pl.BlockSpec((tm, D), lambda i, j: (i, 0))],
        out_specs=[pl.BlockSpec((tm, D), lambda i, j: (i, 0)), pl.BlockSpec((None, tm, GU_SHARD), lambda i, j: (j, i, 0))],
        out_shape=[_SDS((s, D), F32), _SDS((FF_BLOCKS, s, GU_SHARD), _ACT)], name=name,
        compiler_params=_cp("arbitrary", "arbitrary"))(gu, gu, conv_w, conv_b, w_down, res)


def _glu_bwd(gu, conv_w, conv_b, dh, w_down, name, deps=()):
    s = gu.shape[2]
    tm = min(s, _TM_GLU)
    nt = s // tm
    ext_rows = tm + GLU_HALO

    def body(gu_ref, prev_ref, w_ref, b_ref, dh_ref, wdn_ref, *rest):
        dgu_ref, dw_ref, db_ref, carry_ref = rest[-4:]
        t = pl.program_id(1)
        i = nt - 1 - t

        @pl.when(t == 0)
        def _():
            carry_ref[...] = jnp.zeros_like(carry_ref)
            dw_ref[...] = jnp.zeros_like(dw_ref)
            db_ref[...] = jnp.zeros_like(db_ref)

        up = gu_ref[1].astype(F32)
        prev = jnp.where(i > 0, prev_ref[...].astype(F32), 0.0)
        ext = jnp.concatenate([prev, gu_ref[0].astype(F32)], axis=0)
        shifted = [pltpu.roll(ext, FFN_CONV - 1 - j, 0) if j < FFN_CONV - 1 else ext for j in range(FFN_CONV)]
        gc = b_ref[...] + shifted[0] * w_ref[0:1, :]
        for j in range(1, FFN_CONV):
            gc = gc + shifted[j] * w_ref[j:j + 1, :]
        gc = gc[GLU_HALO:]
        sg = jax.nn.sigmoid(gc)
        da = _mm_nt(dh_ref[...], _w2d(wdn_ref))
        dup = da * (gc * sg)
        dgc = da * up * (sg * (1.0 + gc * (1.0 - sg)))
        db_ref[...] += jnp.sum(dgc, axis=0, keepdims=True)
        dgc_ext = jnp.concatenate([jnp.zeros((GLU_HALO, GU_SHARD), F32), dgc], axis=0)
        dext = dgc_ext * w_ref[FFN_CONV - 1:FFN_CONV, :]
        for j in range(FFN_CONV):
            dw_ref[j:j + 1, :] += jnp.sum(shifted[j] * dgc_ext, axis=0, keepdims=True)
            if j < FFN_CONV - 1:
                dext = dext + w_ref[j:j + 1, :] * pltpu.roll(dgc_ext, ext_rows - (FFN_CONV - 1 - j), 0)
        tail = jnp.concatenate([jnp.zeros((tm - GLU_HALO, GU_SHARD), F32), carry_ref[...]], axis=0)
        dgate = dext[GLU_HALO:] + tail
        carry_ref[...] = dext[:GLU_HALO]
        dgu_ref[0] = dgate.astype(dgu_ref.dtype)
        dgu_ref[1] = dup.astype(dgu_ref.dtype)

    return pl.pallas_call(
        body, grid=(FF_BLOCKS, nt),
        in_specs=[pl.BlockSpec((2, None, tm, GU_SHARD), lambda j, t: (0, j, nt - 1 - t, 0)),
                  pl.BlockSpec((None, None, GLU_HALO, GU_SHARD),
                               lambda j, t: (0, j, jnp.maximum((nt - 1 - t) * (tm // GLU_HALO) - 1, 0), 0)),
                  pl.BlockSpec((None, HALO, GU_SHARD), lambda j, t: (j, 0, 0)),
                  pl.BlockSpec((None, 1, GU_SHARD), lambda j, t: (j, 0, 0)),
                  pl.BlockSpec((tm, D), lambda j, t: (nt - 1 - t, 0)), _spec_down(0)] + _dep_specs(deps),
        out_specs=[pl.BlockSpec((2, None, tm, GU_SHARD), lambda j, t: (0, j, nt - 1 - t, 0)),
                   pl.BlockSpec((None, HALO, GU_SHARD), lambda j, t: (j, 0, 0)),
                   pl.BlockSpec((None, 1, GU_SHARD), lambda j, t: (j, 0, 0))],
        out_shape=[_SDS(gu.shape, _ACT), _SDS((FF_BLOCKS, HALO, GU_SHARD), F32), _SDS((FF_BLOCKS, 1, GU_SHARD), F32)],
        scratch_shapes=[pltpu.VMEM((GLU_HALO, GU_SHARD), F32)],
        name=name, compiler_params=_cp("arbitrary", "arbitrary"))(gu, gu, conv_w, conv_b, dh, w_down, *deps)


def _bucket_table():
    qi = np.arange(BLK)[:, None]
    kj = np.arange(BLK)[None, :]
    n = np.where(kj > qi, BLK + qi - kj, qi - kj)
    max_exact = N_BUCKETS // 2
    nf = np.maximum(n, 1).astype(np.float32)
    large = max_exact + (np.log(nf / max_exact) / math.log(MAX_DISTANCE / max_exact)
                         * (N_BUCKETS - max_exact)).astype(np.int32)
    large = np.minimum(large, N_BUCKETS - 1)
    return np.where(n < max_exact, n, large).astype(np.int32)


def _lane_low():
    return lax.broadcasted_iota(jnp.int32, (1, 128), 1) < A_HD


def _swa_groups(q, kd, vd, sink, bias, upper, first):
    n = A_HEADS // A_KV_HEADS
    ng = A_KV_HEADS
    low = _lane_low()
    qm = [jnp.concatenate([jnp.where(low == (h % 2 == 0), q[g][:, (h // 2) * 128:(h // 2 + 1) * 128], 0.0) for h in range(n)], axis=0)
          for g in range(ng)]
    s2 = [_mm_nt(qm[g], kd[g]) * (A_HD ** -0.5) for g in range(ng)]
    s = [jnp.where(upper[None], s2[g][:, :BLK].reshape(n, BLK, BLK), s2[g][:, BLK:].reshape(n, BLK, BLK)) + bias[g] for g in range(ng)]
    s = [jnp.where((upper & first)[None], -jnp.inf, t) for t in s]
    m = [lax.stop_gradient(jnp.maximum(jnp.max(s[g], axis=-1, keepdims=True), sink[g])) for g in range(ng)]
    p = [jnp.exp(s[g] - m[g]) for g in range(ng)]
    split = [jnp.concatenate([jnp.where(upper[None], t, 0.0), jnp.where(upper[None], 0.0, t)], axis=-1).reshape(n * BLK, 2 * BLK)
             for t in p]
    ones = jnp.ones((BLK, 128), F32)
    den = [_mm(p[g].reshape(n * BLK, BLK), ones) + jnp.exp(sink[g] - m[g]).reshape(n * BLK, 1) for g in range(ng)]
    o = [_mm(split[g], vd[g]) / den[g] for g in range(ng)]
    return [jnp.concatenate([jnp.where(low, t[2 * k * BLK:(2 * k + 1) * BLK], t[(2 * k + 1) * BLK:(2 * k + 2) * BLK])
                             for k in range(n // 2)], axis=1) for t in o]


def _mix_a_core(q, kd, vd, sink, bias, xq, mk, mv, upper, first):
    return _swa_groups(q, kd, vd, sink, bias, upper, first), _cross_pairs(xq, mk, mv)


def _swa_sinks(sink_ref, g):
    n = A_HEADS // A_KV_HEADS
    return jnp.concatenate([sink_ref[:, h:h + 1] for h in range(g * n, (g + 1) * n)], axis=0).reshape(n, 1, 1)


def _both_halves(t, t_rolled, g):
    low = _lane_low()
    return jnp.where(low, t, t_rolled) if g == 0 else jnp.where(low, t_rolled, t)


def _cross_pairs(q, mk, mv):
    rows = q.shape[0]
    low = _lane_low()
    qm = [jnp.concatenate([jnp.where(low, q[:, p * 128:(p + 1) * 128], 0.0), jnp.where(low, 0.0, q[:, p * 128:(p + 1) * 128])], axis=0)
          for p in range(X_HEADS // 2)]
    s = [_mm_nt(qm[p], mk[:, p * 128:(p + 1) * 128]) * (X_HD ** -0.5) for p in range(X_HEADS // 2)]
    e = [jnp.exp(t - lax.stop_gradient(jnp.max(t, axis=-1, keepdims=True))) for t in s]
    pr = [t / jnp.sum(t, axis=-1, keepdims=True) for t in e]
    o = [_mm(pr[p], mv[:, p * 128:(p + 1) * 128]) for p in range(X_HEADS // 2)]
    return jnp.concatenate([jnp.where(low, t[:rows], t[rows:]) for t in o], axis=1)


def _swa_upper():
    qi = lax.broadcasted_iota(jnp.int32, (BLK, BLK), 0)
    kj = lax.broadcasted_iota(jnp.int32, (BLK, BLK), 1)
    return kj > qi


def _bias_build(rel_bias, bucket, name):
    def body(rb_ref, bucket_ref, o_ref):
        b = bucket_ref[...]
        for h in range(A_HEADS):
            acc = jnp.zeros((BLK, BLK), F32)
            for k in range(N_BUCKETS):
                acc = jnp.where(b == k, rb_ref[k, h], acc)
            o_ref[h] = acc

    return pl.pallas_call(
        body, in_specs=[pl.BlockSpec(memory_space=pltpu.SMEM), pl.BlockSpec(memory_space=pltpu.VMEM)],
        out_specs=pl.BlockSpec(memory_space=pltpu.VMEM),
        out_shape=_SDS((A_HEADS, BLK, BLK), F32), name=name)(rel_bias, bucket)


def _bias_reduce(dbias, bucket, name):
    def body(db_ref, bucket_ref, o_ref):
        b = bucket_ref[...]
        row = lax.broadcasted_iota(jnp.int32, (N_BUCKETS, 128), 0)
        lane = lax.broadcasted_iota(jnp.int32, (N_BUCKETS, 128), 1)
        acc = jnp.zeros((N_BUCKETS, 128), F32)
        for h in range(A_HEADS):
            v = db_ref[h]
            for k in range(N_BUCKETS):
                sk = jnp.sum(jnp.sum(jnp.where(b == k, v, 0.0), axis=1, keepdims=True), axis=0, keepdims=True)
                acc = acc + jnp.where((row == k) & (lane == h), sk, 0.0)
        o_ref[...] = acc

    return pl.pallas_call(
        body, in_specs=[pl.BlockSpec(memory_space=pltpu.VMEM)] * 2,
        out_specs=pl.BlockSpec(memory_space=pltpu.VMEM),
        out_shape=_SDS((N_BUCKETS, 128), F32), name=name)(dbias, bucket)


def _mix_a_fwd(proj, bias, sinks, memkv, name):
    s = proj.shape[0]
    nb = s // BLK
    grp = A_HEADS // A_KV_HEADS

    def body(proj_ref, prev_ref, bias_ref, sink_ref, memkv_ref, o_ref):
        i = pl.program_id(0)
        upper = _swa_upper()
        prev = prev_ref[...].astype(F32)
        proj = proj_ref[...].astype(F32)
        kb = jnp.concatenate([prev[:, :A_KV], proj[:, A_Q:A_Q + A_KV]], axis=0)
        vb = jnp.concatenate([prev[:, A_KV:], proj[:, A_Q + A_KV:A_Q + 2 * A_KV]], axis=0)
        kb_r = pltpu.roll(kb, A_HD, 1)
        vb_r = pltpu.roll(vb, A_HD, 1)
        gw = A_Q // A_KV_HEADS
        groups = range(A_KV_HEADS)
        swa, cross = _mix_a_core([proj[:, g * gw:(g + 1) * gw] for g in groups], [_both_halves(kb, kb_r, g) for g in groups],
                                 [_both_halves(vb, vb_r, g) for g in groups], [_swa_sinks(sink_ref, g) for g in groups],
                                 [bias_ref[g * grp:(g + 1) * grp] for g in groups], proj[:, A_Q + 2 * A_KV:],
                                 memkv_ref[:, :X_Q], memkv_ref[:, X_Q:], upper, i == 0)
        o_ref[...] = jnp.concatenate(swa + [cross], axis=1).astype(o_ref.dtype)

    return pl.pallas_call(
        body, grid=(nb,),
        in_specs=[pl.BlockSpec((BLK, IN_A), lambda i: (i, 0)),
                  pl.BlockSpec((BLK, 2 * A_KV), lambda i: (jnp.maximum(i - 1, 0), A_Q // (2 * A_KV))),
                  pl.BlockSpec((A_HEADS, BLK, BLK), lambda i: (0, 0, 0)),
                  pl.BlockSpec((1, 128), lambda i: (0, 0)),
                  pl.BlockSpec((MEM_LEN, 2 * X_Q), lambda i: (0, 0))],
        out_specs=pl.BlockSpec((BLK, D), lambda i: (i, 0)),
        out_shape=_SDS((s, D), _ACT), name=name, compiler_params=_cp("arbitrary"))(proj, proj, bias, sinks, memkv)


def _mix_a_bwd(proj, bias, sinks, memkv, dmix, name, deps=()):
    s = proj.shape[0]
    nb = s // BLK
    grp = A_HEADS // A_KV_HEADS

    def body(proj_ref, prev_ref, bias_ref, sink_ref, memkv_ref, dmix_ref, *rest):
        dproj_ref, dbias_ref, dsink_ref, dmemkv_ref, carry_ref = rest[-5:]
        t = pl.program_id(0)
        i = nb - 1 - t

        @pl.when(t == 0)
        def _():
            carry_ref[...] = jnp.zeros_like(carry_ref)
            dbias_ref[...] = jnp.zeros_like(dbias_ref)
            dsink_ref[...] = jnp.zeros_like(dsink_ref)
            dmemkv_ref[...] = jnp.zeros_like(dmemkv_ref)

        upper = _swa_upper()
        lane = lax.broadcasted_iota(jnp.int32, (1, 128), 1)
        low = _lane_low()
        prev = prev_ref[...].astype(F32)
        proj = proj_ref[...].astype(F32)
        kb = jnp.concatenate([prev[:, :A_KV], proj[:, A_Q:A_Q + A_KV]], axis=0)
        vb = jnp.concatenate([prev[:, A_KV:], proj[:, A_Q + A_KV:A_Q + 2 * A_KV]], axis=0)
        kb_r = pltpu.roll(kb, A_HD, 1)
        vb_r = pltpu.roll(vb, A_HD, 1)
        gw = A_Q // A_KV_HEADS
        groups = range(A_KV_HEADS)
        _, vjp = jax.vjp(
            functools.partial(_mix_a_core, upper=upper, first=i == 0),
            [proj[:, g * gw:(g + 1) * gw] for g in groups], [_both_halves(kb, kb_r, g) for g in groups],
            [_both_halves(vb, vb_r, g) for g in groups], [_swa_sinks(sink_ref, g) for g in groups],
            [bias_ref[g * grp:(g + 1) * grp] for g in groups], proj[:, A_Q + 2 * A_KV:], memkv_ref[:, :X_Q], memkv_ref[:, X_Q:])
        dqs, dk, dv, ds, db, dxq, dmk, dmv = vjp(
            ([dmix_ref[:, g * gw:(g + 1) * gw].astype(F32) for g in groups], dmix_ref[:, A_Q:].astype(F32)))
        dkd = [t + pltpu.roll(t, A_HD, 1) for t in dk]
        dvd = [t + pltpu.roll(t, A_HD, 1) for t in dv]
        dsink = jnp.zeros((1, 128), F32)
        for g in groups:
            for h in range(grp):
                dsink = dsink + jnp.where(lane == g * grp + h, ds[g][h], 0.0)
            dbias_ref[g * grp:(g + 1) * grp] += db[g]
        dsink_ref[...] += dsink
        dkb = jnp.where(low, dkd[0], dkd[1])
        dvb = jnp.where(low, dvd[0], dvd[1])
        dmemkv_ref[...] += jnp.concatenate([dmk, dmv], axis=1)
        dkv_cur = jnp.concatenate([dkb[BLK:], dvb[BLK:]], axis=1) + carry_ref[...]
        carry_ref[...] = jnp.concatenate([dkb[:BLK], dvb[:BLK]], axis=1)
        dproj_ref[...] = jnp.concatenate(list(dqs) + [dkv_cur, dxq], axis=1).astype(dproj_ref.dtype)

    return pl.pallas_call(
        body, grid=(nb,),
        in_specs=[pl.BlockSpec((BLK, IN_A), lambda t: (nb - 1 - t, 0)),
                  pl.BlockSpec((BLK, 2 * A_KV), lambda t: (jnp.maximum(nb - 2 - t, 0), A_Q // (2 * A_KV))),
                  pl.BlockSpec((A_HEADS, BLK, BLK), lambda t: (0, 0, 0)),
                  pl.BlockSpec((1, 128), lambda t: (0, 0)),
                  pl.BlockSpec((MEM_LEN, 2 * X_Q), lambda t: (0, 0)),
                  pl.BlockSpec((BLK, D), lambda t: (nb - 1 - t, 0))] + _dep_specs(deps),
        out_specs=[pl.BlockSpec((BLK, IN_A), lambda t: (nb - 1 - t, 0)),
                   pl.BlockSpec((A_HEADS, BLK, BLK), lambda t: (0, 0, 0)),
                   pl.BlockSpec((1, 128), lambda t: (0, 0)),
                   pl.BlockSpec((MEM_LEN, 2 * X_Q), lambda t: (0, 0))],
        out_shape=[_SDS((s, IN_A), _ACT), _SDS((A_HEADS, BLK, BLK), F32), _SDS((1, 128), F32),
                   _SDS((MEM_LEN, 2 * X_Q), F32)],
        scratch_shapes=[pltpu.VMEM((BLK, 2 * A_KV), F32)],
        name=name, compiler_params=_cp("arbitrary"))(proj, proj, bias, sinks, memkv, dmix, *deps)


def _neumann(pw, rhs):
    nh = len(pw)
    x = rhs
    for lvl in range(6):
        if lvl < 5:
            prod = [_mmf(pw[h], jnp.concatenate([x[h], pw[h]], axis=1)) for h in range(nh)]
            x = [x[h] + prod[h][:, :B_HD] for h in range(nh)]
            pw = [t[:, B_HD:] for t in prod]
        else:
            x = [x[h] + _mmf(pw[h], x[h]) for h in range(nh)]
    return x


@jax.custom_vjp
def _tri_solve(pw, rhs):
    return _neumann(pw, rhs)


def _tri_solve_fwd(pw, rhs):
    x = _neumann(pw, rhs)
    return x, (pw, x)


def _tri_solve_bwd(res, dx):
    pw, x = res
    d_rhs = _neumann([t.T for t in pw], list(dx))
    return [_mmf_nt(d_rhs[h], x[h]) for h in range(len(pw))], d_rhs


_tri_solve.defvjp(_tri_solve_fwd, _tri_solve_bwd)


@jax.custom_vjp
def _tri_solved(pw, rhs, x):
    return x


def _tri_solved_fwd(pw, rhs, x):
    return x, (pw, x)


def _tri_solved_bwd(res, dx):
    d_pw, d_rhs = _tri_solve_bwd(res, dx)
    return d_pw, d_rhs, [jnp.zeros_like(t) for t in res[1]]


_tri_solved.defvjp(_tri_solved_fwd, _tri_solved_bwd)


@jax.custom_vjp
def _known(x, value):
    return value


def _known_fwd(x, value):
    return value, None


def _known_bwd(_, g):
    return g, jnp.zeros_like(g)


_known.defvjp(_known_fwd, _known_bwd)


def _dn_heads(yq, yk, yv, z, bl, al, a_log, dtb, ng, s0, solved=None, out_known=None):
    c = CHUNK
    nh = B_V_HEADS
    rep = B_V_HEADS // B_QK_HEADS
    r = lax.broadcasted_iota(jnp.int32, (c, c), 0)
    cc = lax.broadcasted_iota(jnp.int32, (c, c), 1)
    q = [_silu(t) for t in yq]
    k = [_silu(t) for t in yk]
    v = [_silu(t) for t in yv]
    q = [t * lax.rsqrt(jnp.sum(t * t, axis=-1, keepdims=True) + EPS) * (B_HD ** -0.5) for t in q]
    k = [t * lax.rsqrt(jnp.sum(t * t, axis=-1, keepdims=True) + EPS) for t in k]
    beta = [jax.nn.sigmoid(t) for t in bl]
    g = [-jnp.exp(a_log[h]) * jax.nn.softplus(al[h] + dtb[h]) for h in range(nh)]
    gb = [jnp.broadcast_to(t, (c, c)) for t in g]
    gc_col = [jnp.sum(jnp.where(cc <= r, t.T, 0.0), axis=1, keepdims=True) for t in gb]
    gc_row = [jnp.sum(jnp.where(r <= cc, t, 0.0), axis=0, keepdims=True) for t in gb]
    gc_last = [jnp.sum(t, axis=0, keepdims=True) for t in g]
    decay = [jnp.exp(jnp.where(r >= cc, gc_col[h] - gc_row[h], -jnp.inf)) for h in range(nh)]
    kq = [_mmf_nt(jnp.concatenate([k[h], q[h]], axis=0), k[h]) for h in range(B_QK_HEADS)]
    kk = [t[:c] for t in kq]
    qk = [t[c:] for t in kq]
    egc = [jnp.exp(t) for t in gc_col]
    both = [_mmf(jnp.concatenate([(beta[h] * egc[h]) * k[h // rep], q[h // rep] * egc[h]], axis=0), s0[h]) for h in range(nh)]
    rhs = [beta[h] * v[h] - both[h][:c] for h in range(nh)]
    qs0 = [t[c:] for t in both]
    pw = [-(beta[h] * kk[h // rep] * jnp.where(r > cc, decay[h], 0.0)) for h in range(nh)]
    delta = _tri_solve(pw, rhs) if solved is None else _tri_solved(pw, rhs, solved)
    last = [_mmf(jnp.concatenate([qk[h // rep] * decay[h], (k[h // rep] * jnp.exp(gc_last[h] - gc_col[h])).T], axis=0), delta[h])
            for h in range(nh)]
    out = [qs0[h] + last[h][:c] for h in range(nh)]
    if out_known is not None:
        out = [_known(out[h], out_known[h]) for h in range(nh)]
    s1 = [jnp.exp(gc_last[h]) * s0[h] + last[h][c:] for h in range(nh)]
    o = [t * lax.rsqrt(jnp.mean(t * t, axis=-1, keepdims=True) + EPS) * ng for t in out]
    return [o[h] * _silu(z[h]) for h in range(nh)], s1, delta, out


def _dn_conv(ext, w_ref):
    y = ext * w_ref[B_CONV - 1:B_CONV, :]
    for j in range(B_CONV - 1):
        y = y + w_ref[j:j + 1, :] * pltpu.roll(ext, B_CONV - 1 - j, 0)
    return y


def _dn_args(y, cur_ref, par_ref, ng_ref):
    nh = B_V_HEADS
    return ([y[:, h * B_HD:(h + 1) * B_HD] for h in range(B_QK_HEADS)],
            [y[:, B_QK + h * B_HD:B_QK + (h + 1) * B_HD] for h in range(B_QK_HEADS)],
            [y[:, 2 * B_QK + h * B_HD:2 * B_QK + (h + 1) * B_HD] for h in range(nh)],
            [cur_ref[:, BP_Z + h * B_HD:BP_Z + (h + 1) * B_HD] for h in range(nh)],
            [cur_ref[:, BP_GATE + h:BP_GATE + h + 1] for h in range(nh)],
            [cur_ref[:, BP_GATE + nh + h:BP_GATE + nh + h + 1] for h in range(nh)],
            [par_ref[:, h:h + 1] for h in range(nh)], [par_ref[:, nh + h:nh + h + 1] for h in range(nh)], ng_ref[...])


def _mix_b_fwd(proj, conv_w, par, ng, memkv, name):
    s = proj.shape[0]
    nc = s // CHUNK

    def body(cur_ref, prev_ref, w_ref, par_ref, ng_ref, memkv_ref, o_ref, st_ref, dl_ref, state_ref):
        n = pl.program_id(0)

        @pl.when(n == 0)
        def _():
            state_ref[...] = jnp.zeros_like(state_ref)

        prev = jnp.where(n > 0, prev_ref[...], 0.0)
        ext = jnp.concatenate([prev, cur_ref[:, :B_QKV]], axis=0)
        y = _dn_conv(ext, w_ref)[HALO:]
        s0 = [state_ref[hv] for hv in range(B_V_HEADS)]
        st_ref[0] = state_ref[...]
        outs, s1, delta, raw = _dn_heads(*_dn_args(y, cur_ref, par_ref, ng_ref), s0)
        for hv in range(B_V_HEADS):
            state_ref[hv] = s1[hv]
            dl_ref[0, hv] = delta[hv]
            dl_ref[0, B_V_HEADS + hv] = raw[hv]
        outs = outs + [_cross_pairs(cur_ref[:, BP_XQ:BP_XQ + X_Q], memkv_ref[:, :X_Q], memkv_ref[:, X_Q:])]
        o_ref[...] = jnp.concatenate(outs, axis=1).astype(o_ref.dtype)

    return pl.pallas_call(
        body, grid=(nc,),
        in_specs=[pl.BlockSpec((CHUNK, IN_BP), lambda n: (n, 0)),
                  pl.BlockSpec((HALO, B_QKV), lambda n: (jnp.maximum(n * (CHUNK // HALO) - 1, 0), 0)),
                  pl.BlockSpec((HALO, B_QKV), lambda n: (0, 0)),
                  pl.BlockSpec((1, 128), lambda n: (0, 0)), pl.BlockSpec((1, 128), lambda n: (0, 0)),
                  pl.BlockSpec((MEM_LEN, 2 * X_Q), lambda n: (0, 0))],
        out_specs=[pl.BlockSpec((CHUNK, D), lambda n: (n, 0)),
                   pl.BlockSpec((1, B_V_HEADS, B_HD, B_HD), lambda n: (n, 0, 0, 0)),
                   pl.BlockSpec((1, 2 * B_V_HEADS, CHUNK, B_HD), lambda n: (n, 0, 0, 0))],
        out_shape=[_SDS((s, D), _ACT), _SDS((nc, B_V_HEADS, B_HD, B_HD), F32), _SDS((nc, 2 * B_V_HEADS, CHUNK, B_HD), F32)],
        scratch_shapes=[pltpu.VMEM((B_V_HEADS, B_HD, B_HD), F32)],
        name=name, compiler_params=_cp("arbitrary"))(proj, proj, conv_w, par, ng, memkv)


def _mix_b_bwd(proj, conv_w, par, ng, memkv, states, deltas, dmix, name):
    s = proj.shape[0]
    nc = s // CHUNK
    ext_rows = CHUNK + HALO

    def body(cur_ref, prev_ref, w_ref, par_ref, ng_ref, memkv_ref, st_ref, dl_ref, dmix_ref,
             dproj_ref, dw_ref, dpar_ref, dng_ref, dmemkv_ref, dstate_ref, carry_ref):
        t = pl.program_id(0)
        n = nc - 1 - t

        @pl.when(t == 0)
        def _():
            dstate_ref[...] = jnp.zeros_like(dstate_ref)
            carry_ref[...] = jnp.zeros_like(carry_ref)
            dw_ref[...] = jnp.zeros_like(dw_ref)
            dpar_ref[...] = jnp.zeros_like(dpar_ref)
            dng_ref[...] = jnp.zeros_like(dng_ref)
            dmemkv_ref[...] = jnp.zeros_like(dmemkv_ref)

        lane = lax.broadcasted_iota(jnp.int32, (1, 128), 1)
        prev = jnp.where(n > 0, prev_ref[...], 0.0)
        ext = jnp.concatenate([prev, cur_ref[:, :B_QKV]], axis=0)
        y = _dn_conv(ext, w_ref)[HALO:]
        solved = [dl_ref[0, hv] for hv in range(B_V_HEADS)]
        raw = [dl_ref[0, B_V_HEADS + hv] for hv in range(B_V_HEADS)]
        _, vjp = jax.vjp(functools.partial(_dn_heads, solved=solved, out_known=raw), *_dn_args(y, cur_ref, par_ref, ng_ref),
                         [st_ref[0, hv] for hv in range(B_V_HEADS)])
        none = [jnp.zeros((CHUNK, B_HD), F32)] * B_V_HEADS
        dyq, dyk, dyv, dz, gbl, gal, ga_log, gdtb, dng, gs0 = vjp(
            ([dmix_ref[:, hv * B_HD:(hv + 1) * B_HD].astype(F32) for hv in range(B_V_HEADS)],
             [dstate_ref[hv] for hv in range(B_V_HEADS)], none, none))
        dgate = jnp.zeros((CHUNK, 128), F32)
        dpar = jnp.zeros((1, 128), F32)
        for hv in range(B_V_HEADS):
            dstate_ref[hv] = gs0[hv]
            dgate = dgate + jnp.where(lane == hv, gbl[hv], 0.0) + jnp.where(lane == B_V_HEADS + hv, gal[hv], 0.0)
            dpar = dpar + jnp.where(lane == hv, ga_log[hv], 0.0) + jnp.where(lane == B_V_HEADS + hv, gdtb[hv], 0.0)
        dpar_ref[...] += dpar
        dng_ref[...] += dng
        _, vjp = jax.vjp(_cross_pairs, cur_ref[:, BP_XQ:BP_XQ + X_Q], memkv_ref[:, :X_Q], memkv_ref[:, X_Q:])
        dxq, dmk, dmv = vjp(dmix_ref[:, B_V:].astype(F32))
        dmemkv_ref[...] += jnp.concatenate([dmk, dmv], axis=1)
        dy = jnp.concatenate(list(dyq) + list(dyk) + list(dyv), axis=1)
        dy_ext = jnp.concatenate([jnp.zeros((HALO, B_QKV), F32), dy], axis=0)
        dext = dy_ext * w_ref[B_CONV - 1:B_CONV, :]
        dw_ref[B_CONV - 1:B_CONV, :] += jnp.sum(ext * dy_ext, axis=0, keepdims=True)
        for j in range(B_CONV - 1):
            sh = B_CONV - 1 - j
            dw_ref[j:j + 1, :] += jnp.sum(pltpu.roll(ext, sh, 0) * dy_ext, axis=0, keepdims=True)
            dext = dext + w_ref[j:j + 1, :] * pltpu.roll(dy_ext, ext_rows - sh, 0)
        tail = jnp.concatenate([jnp.zeros((CHUNK - HALO, B_QKV), F32), carry_ref[...]], axis=0)
        dqkv = dext[HALO:] + tail
        carry_ref[...] = dext[:HALO]
        dproj_ref[...] = jnp.concatenate([dqkv] + list(dz) + [dxq, dgate], axis=1).astype(dproj_ref.dtype)

    return pl.pallas_call(
        body, grid=(nc,),
        in_specs=[pl.BlockSpec((CHUNK, IN_BP), lambda t: (nc - 1 - t, 0)),
                  pl.BlockSpec((HALO, B_QKV), lambda t: (jnp.maximum((nc - 1 - t) * (CHUNK // HALO) - 1, 0), 0)),
                  pl.BlockSpec((HALO, B_QKV), lambda t: (0, 0)),
                  pl.BlockSpec((1, 128), lambda t: (0, 0)), pl.BlockSpec((1, 128), lambda t: (0, 0)),
                  pl.BlockSpec((MEM_LEN, 2 * X_Q), lambda t: (0, 0)),
                  pl.BlockSpec((1, B_V_HEADS, B_HD, B_HD), lambda t: (nc - 1 - t, 0, 0, 0)),
                  pl.BlockSpec((1, 2 * B_V_HEADS, CHUNK, B_HD), lambda t: (nc - 1 - t, 0, 0, 0)),
                  pl.BlockSpec((CHUNK, D), lambda t: (nc - 1 - t, 0))],
        out_specs=[pl.BlockSpec((CHUNK, IN_BP), lambda t: (nc - 1 - t, 0)),
                   pl.BlockSpec((HALO, B_QKV), lambda t: (0, 0)),
                   pl.BlockSpec((1, 128), lambda t: (0, 0)), pl.BlockSpec((1, 128), lambda t: (0, 0)),
                   pl.BlockSpec((MEM_LEN, 2 * X_Q), lambda t: (0, 0))],
        out_shape=[_SDS((s, IN_BP), _ACT), _SDS((HALO, B_QKV), F32), _SDS((1, 128), F32), _SDS((1, 128), F32),
                   _SDS((MEM_LEN, 2 * X_Q), F32)],
        scratch_shapes=[pltpu.VMEM((B_V_HEADS, B_HD, B_HD), F32), pltpu.VMEM((HALO, B_QKV), F32)],
        name=name, compiler_params=_cp("arbitrary"))(proj, proj, conv_w, par, ng, memkv, states, deltas, dmix)


def _place():
    return lax.axis_index("x"), lax.axis_index("y"), lax.axis_index("c")


def _all_gather(shards, name):
    n = len(shards)

    def body(*refs):
        ins, outs = refs[:n], refs[n:2 * n]
        send_sems, recv_sems, local_sems = refs[2 * n:]
        x, y, c = _place()
        me, sibling = (x, y, c), (x, y, 1 - c)
        chips = [(1 - x, y), (x, 1 - y), (1 - x, 1 - y)]

        def rows(a, px, py, pc):
            return outs[a].at[4 * px + 2 * py + pc]

        def copy(a, k, block, to, src=None):
            return pltpu.make_async_remote_copy(
                src_ref=rows(a, *block) if src is None else src, dst_ref=rows(a, *block),
                send_sem=send_sems.at[a, k], recv_sem=recv_sems.at[a, k],
                device_id=to, device_id_type=pl.DeviceIdType.MESH)

        mine = [pltpu.make_async_copy(ins[a], rows(a, *me), local_sems.at[a]) for a in range(n)]
        for cp in mine:
            cp.start()
        first = []
        for a in range(n):
            first.append(copy(a, 0, me, sibling, src=ins[a]))
            first += [copy(a, 1 + j, me, (*chip, c), src=ins[a]) for j, chip in enumerate(chips)]
        for cp in first:
            cp.start()
        passed = []
        for j, chip in enumerate(chips):
            for a in range(n):
                copy(a, 1 + j, (*chip, c), me).wait_recv()
                fwd = copy(a, 4 + j, (*chip, c), sibling)
                fwd.start()
                passed.append(fwd)
        for a in range(n):
            copy(a, 0, sibling, me).wait_recv()
            for j, chip in enumerate(chips):
                copy(a, 4 + j, (*chip, 1 - c), me).wait_recv()
        for cp in first + passed:
            cp.wait_send()
        for cp in mine:
            cp.wait()

    hbm = pl.BlockSpec(memory_space=pl.ANY)
    return pl.pallas_call(
        body, out_shape=[_SDS((N_DEV,) + s.shape, s.dtype) for s in shards],
        in_specs=[hbm] * n, out_specs=[hbm] * n,
        scratch_shapes=[pltpu.SemaphoreType.DMA((n, 7)), pltpu.SemaphoreType.DMA((n, 7)), pltpu.SemaphoreType.DMA((n,))],
        name=name)(*shards)


class _Exchange:
    def __init__(self, lands, srcs):
        self.lands, self.srcs = lands, srcs


def _seq_exchange(srcs, land_shapes, plan, name, cid):
    n, nl = len(srcs), len(land_shapes)

    def launch(*refs):
        src_refs, land_refs = refs[:n], refs[n:n + nl]
        send_sems, recv_sems, local_sems = refs[n + nl:]
        x, y, c = _place()
        my = 4 * x + 2 * y + c
        peers = [(x ^ ((k + 1) >> 2 & 1), y ^ ((k + 1) >> 1 & 1), c ^ ((k + 1) & 1)) for k in range(N_DEV - 1)]
        barrier = pltpu.get_barrier_semaphore()
        for p in peers:
            pl.semaphore_signal(barrier, inc=1, device_id=p, device_id_type=pl.DeviceIdType.MESH)
        pl.semaphore_wait(barrier, N_DEV - 1)

        def src_for(a, dest):
            return src_refs[a].at[dest] if plan[a][1] else src_refs[a]

        def slot(a, source):
            return land_refs[plan[a][0]].at[source]

        mine = [pltpu.make_async_copy(src_for(a, my), slot(a, my), local_sems.at[a]) for a in range(n)]
        for cp in mine:
            cp.start()
        sends, recvs = [], []
        for k, (px, py, pc) in enumerate(peers):
            peer = 4 * px + 2 * py + pc
            for a in range(n):
                kw = dict(send_sem=send_sems.at[a * (N_DEV - 1) + k], recv_sem=recv_sems.at[a * (N_DEV - 1) + k],
                          device_id=(px, py, pc), device_id_type=pl.DeviceIdType.MESH)
                sends.append(pltpu.make_async_remote_copy(src_ref=src_for(a, peer), dst_ref=slot(a, my), **kw))
                recvs.append(pltpu.make_async_remote_copy(src_ref=src_for(a, my), dst_ref=slot(a, peer), **kw))
        for cp in sends:
            cp.start()
        for cp in recvs:
            cp.wait_recv()
        for cp in sends:
            cp.wait_send()
        for cp in mine:
            cp.wait()

    lands = pl.kernel(
        launch, out_type=[_SDS(s, d) for s, d in land_shapes],
        mesh=plsc.ScalarSubcoreMesh(axis_name="sequencer", num_cores=1), name=name,
        scratch_types=(pltpu.SemaphoreType.DMA((n * (N_DEV - 1),)), pltpu.SemaphoreType.DMA((n * (N_DEV - 1),)),
                       pltpu.SemaphoreType.DMA((n,))),
        compiler_params=pltpu.CompilerParams(collective_id=cid))(*srcs)
    return _Exchange(list(lands), list(srcs))


def _adam_update(g, w, m, v):
    c1 = 1.0 - ADAM_B1 ** ADAM_STEP
    c2 = 1.0 - ADAM_B2 ** ADAM_STEP
    mm = ADAM_B1 * m + (1.0 - ADAM_B1) * g
    vv = ADAM_B2 * v + (1.0 - ADAM_B2) * (g * g)
    delta = -ADAM_LR * ((mm / c1) / (jnp.sqrt(vv / c2) + ADAM_EPS) + ADAM_WD * w)
    return delta, mm, vv


def _sum_sources(p_ref):
    g = p_ref[0].astype(F32)
    for s in range(1, N_DEV):
        g = g + p_ref[s].astype(F32)
    return g


def _adamw(parts, w, m, v, tr, name, restore_b=False, deps=()):
    nl, r, c = w.shape
    cp = parts[0].shape[-1]

    def body(*refs):
        p_refs = refs[:nl]
        w_ref, m_ref, v_ref = refs[nl:nl + 3]
        g_ref, d_ref, nm_ref, nv_ref = refs[-4:]
        g = _sum_sources(p_refs[0])
        for l in range(1, nl):
            g = jnp.where(pl.program_id(0) == l, _sum_sources(p_refs[l]), g)
        if restore_b:
            g = jnp.concatenate([g[:, :BP_XQ], g[:, BP_GATE:BP_GATE + 2 * B_V_HEADS], g[:, BP_XQ:BP_GATE]], axis=1)
        delta, mm, vv = _adam_update(g, w_ref[...], m_ref[...], v_ref[...])
        g_ref[...] = g
        d_ref[...] = delta
        nm_ref[...] = mm
        nv_ref[...] = vv

    spec = pl.BlockSpec((None, tr, c), lambda l, i: (l, i, 0))
    part_specs = [pl.BlockSpec((N_DEV, tr, cp), functools.partial(lambda l, i, k: (0, jnp.where(l == k, i, 0), 0), k=k))
                  for k in range(nl)]
    return pl.pallas_call(
        body, grid=(nl, r // tr),
        in_specs=part_specs + [spec, spec, spec] + _dep_specs(deps),
        out_specs=[spec] * 4, out_shape=[_SDS(w.shape, F32)] * 4,
        name=name, compiler_params=_cp("arbitrary", "arbitrary"))(*parts, w, m, v, *deps)


def _pack_small(d_rel, d_cb, d_cw, d_qkv, d_mix, d_mem, d_ffn, d_final, d_sinks, d_par, d_ng, loss_row, name):
    flat = [d_rel, *d_cb, *d_cw, d_qkv, *d_mix, *d_mem, *d_ffn, d_final, d_sinks, d_par, d_ng, loss_row]
    n = len(flat)

    def body(*refs):
        ins, o_ref = refs[:n], refs[n]
        rel, cb0, cb1, cw0, cw1, qkv, mx0, mx1, me0, me1, ff0, ff1, fin, snk, par, ng, lss = ins
        o_ref[...] = jnp.zeros_like(o_ref)
        for k in range(N_BUCKETS):
            lane = SP_REL_LANE + 128 * (k % 8)
            o_ref[SP_QKV + k // 8:SP_QKV + k // 8 + 1, lane:lane + 128] = rel[k:k + 1, :]
        for l, (cb, cw) in enumerate(((cb0, cw0), (cb1, cw1))):
            o_ref[SP_CB + l:SP_CB + l + 1, :] = jnp.concatenate([cb[j] for j in range(FF_BLOCKS)], axis=1)
            full = jnp.concatenate([cw[j] for j in range(FF_BLOCKS)], axis=1)
            o_ref[SP_CW + FFN_CONV * l:SP_CW + FFN_CONV * (l + 1), :] = full[:FFN_CONV]
        o_ref[SP_QKV:SP_QKV + B_CONV, 0:B_QKV] = qkv[0:B_CONV, :]
        for base, pair in ((SP_MIX, (mx0, mx1)), (SP_MEM, (me0, me1)), (SP_FFN, (ff0, ff1))):
            for l in range(2):
                o_ref[base + l:base + l + 1, 0:D] = pair[l][...]
        o_ref[SP_FINAL:SP_FINAL + 1, 0:D] = fin[...]
        o_ref[SP_MISC:SP_MISC + 1, 0:128] = snk[...]
        o_ref[SP_MISC:SP_MISC + 1, 128:256] = par[...]
        o_ref[SP_MISC:SP_MISC + 1, 256:384] = ng[...]
        o_ref[SP_MISC:SP_MISC + 1, 384:512] = lss[...]

    vm = pl.BlockSpec(memory_space=pltpu.VMEM)
    return pl.pallas_call(body, in_specs=[vm] * n, out_specs=vm, out_shape=_SDS((SMALL_ROWS, D_FF), F32), name=name)(*flat)


_SMALL = ["rel_bias", "norm_mix_g", "norm_mem_g", "sinks_a", "a_log_b", "dt_bias_b", "out_norm_g_b", "norm_ffn_g",
          "ffn_conv_b", "final_norm_g", "conv_qkv_b", "ffn_conv_w"]


def _adamw_small(recv, rc_qkv, rc_ffn, ws, ms, vs, name, deps=()):
    n = len(_SMALL)

    def body(*refs):
        recv_ref, qkv_ref, ffn_ref = refs[:3]
        w_refs, m_refs, v_refs = refs[3:3 + n], refs[3 + n:3 + 2 * n], refs[3 + 2 * n:3 + 3 * n]
        outs, loss_ref = refs[len(refs) - 4 * n - 1:len(refs) - 1], refs[-1]
        gs = _sum_sources(recv_ref)
        loss_ref[...] = gs[SP_MISC:SP_MISC + 1, 384:512]
        grads = {
            "rel_bias": jnp.concatenate(
                [gs[SP_QKV + k // 8:SP_QKV + k // 8 + 1, SP_REL_LANE + 128 * (k % 8):SP_REL_LANE + 128 * (k % 8) + A_HEADS]
                 for k in range(N_BUCKETS)], axis=0),
            "norm_mix_g": gs[SP_MIX:SP_MIX + 2, 0:D], "norm_mem_g": gs[SP_MEM:SP_MEM + 2, 0:D],
            "sinks_a": gs[SP_MISC:SP_MISC + 1, 0:A_HEADS],
            "a_log_b": gs[SP_MISC:SP_MISC + 1, 128:128 + B_V_HEADS],
            "dt_bias_b": gs[SP_MISC:SP_MISC + 1, 128 + B_V_HEADS:128 + 2 * B_V_HEADS],
            "out_norm_g_b": gs[SP_MISC:SP_MISC + 1, 256:256 + B_HD],
            "norm_ffn_g": gs[SP_FFN:SP_FFN + 2, 0:D], "ffn_conv_b": gs[SP_CB:SP_CB + 2, :],
            "final_norm_g": gs[SP_FINAL:SP_FINAL + 1, 0:D],
            "conv_qkv_b": _sum_sources(qkv_ref), "ffn_conv_w": _sum_sources(ffn_ref),
        }
        for i, nm in enumerate(_SMALL):
            g = grads[nm]
            delta, mm, vv = _adam_update(g, w_refs[i][...], m_refs[i][...], v_refs[i][...])
            outs[i][...] = g
            outs[n + i][...] = delta
            outs[2 * n + i][...] = mm
            outs[3 * n + i][...] = vv

    vm = pl.BlockSpec(memory_space=pltpu.VMEM)
    shapes = [_SDS(w.shape, F32) for w in ws]
    return pl.pallas_call(
        body, in_specs=[vm] * (3 + 3 * n) + _dep_specs(deps), out_specs=[vm] * (4 * n + 1),
        out_shape=shapes * 4 + [_SDS((1, 128), F32)],
        name=name)(recv, rc_qkv, rc_ffn, *ws, *ms, *vs, *deps)


def _assemble(gathered, axis):
    g = jnp.moveaxis(gathered, 0, axis)
    shp = list(g.shape)
    return g.reshape(shp[:axis] + [shp[axis] * shp[axis + 1]] + shp[axis + 2:])


def _pad_rows(a, rows):
    return jnp.pad(a, ((0, rows - a.shape[0]), (0, 0)))


def _pad_lanes(a, lanes=128):
    return jnp.pad(a, ((0, 0), (0, lanes - a.shape[1])))


def _ff_blocks(a):
    return jnp.moveaxis(a.reshape(a.shape[0], FF_BLOCKS, GU_SHARD), 1, 0)


def _reorder_b(w):
    qkv_z = w[..., :B_QKV + B_V]
    gates = w[..., B_QKV + B_V:B_QKV + B_V + 2 * B_V_HEADS]
    xq = w[..., IN_B - X_Q:]
    pad = jnp.zeros(w.shape[:-1] + (IN_BP - IN_B,), w.dtype)
    return jnp.concatenate([qkv_z, xq, gates, pad], axis=-1)


def kernel(x, mem, rel_bias, norm_mix_g, norm_mem_g, w_mem_kv, w_out, w_in_a, sinks_a, w_in_b, conv_qkv_b, a_log_b, dt_bias_b, out_norm_g_b, norm_ffn_g, w_gate_up, ffn_conv_w, ffn_conv_b, w_down, final_norm_g, loss_target, m_rel_bias, m_norm_mix_g, m_norm_mem_g, m_w_mem_kv, m_w_out, m_w_in_a, m_sinks_a, m_w_in_b, m_conv_qkv_b, m_a_log_b, m_dt_bias_b, m_out_norm_g_b, m_norm_ffn_g, m_w_gate_up, m_ffn_conv_w, m_ffn_conv_b, m_w_down, m_final_norm_g, v_rel_bias, v_norm_mix_g, v_norm_mem_g, v_w_mem_kv, v_w_out, v_w_in_a, v_sinks_a, v_w_in_b, v_conv_qkv_b, v_a_log_b, v_dt_bias_b, v_out_norm_g_b, v_norm_ffn_g, v_w_gate_up, v_ffn_conv_w, v_ffn_conv_b, v_w_down, v_final_norm_g):
    local = dict(locals())
    order = ["rel_bias", "norm_mix_g", "norm_mem_g", "w_mem_kv", "w_out", "w_in_a", "sinks_a", "w_in_b", "conv_qkv_b",
             "a_log_b", "dt_bias_b", "out_norm_g_b", "norm_ffn_g", "w_gate_up", "ffn_conv_w", "ffn_conv_b", "w_down",
             "final_norm_g"]
    wts = {n: local[n] for n in order}
    moms = {n: local["m_" + n] for n in order}
    vars_ = {n: local["v_" + n] for n in order}
    h0 = x[0]
    memx = mem[0]
    tgt = loss_target[0]
    s = h0.shape[0]
    tm = _rows(s)
    tb = min(s, _TM_BIG)

    t_ = lambda a: jnp.swapaxes(a, 1, 2)
    g_mk0, g_out0, g_ia, g_cq, g_cw = _all_gather(
        [w_mem_kv[0:1].astype(_MXU), w_out[0:1].astype(_MXU), t_(w_in_a).astype(_MXU), conv_qkv_b, ffn_conv_w], "gather_first")
    g_mk, g_out = [g_mk0], [g_out0]
    gu_land = ((N_DEV, GU_SHARD, D), _MXU)
    dn_land = ((N_DEV, DN_SHARD, D), _MXU)
    whole = [(0, False), (1, False)]
    def after(a, b):
        return a + (b[(0,) * b.ndim] * 0).astype(a.dtype)

    ffn0_w = _seq_exchange([after(t_(w_gate_up)[0].astype(_MXU), g_ia), after(w_down[0].astype(_MXU), g_ia)], [gu_land, dn_land],
                           whole, "gather_ffn0", 1)
    w_ia = g_ia.reshape(IN_A, D)
    conv_qkv = _pad_rows(_assemble(g_cq, 2)[0], HALO)
    ffn_cw_full = _assemble(g_cw, 2)
    ffn_cw = [_ff_blocks(_pad_rows(ffn_cw_full[i], HALO)) for i in range(2)]
    ffn_cb = [_ff_blocks(ffn_conv_b[i:i + 1]) for i in range(2)]
    bucket = jnp.asarray(_bucket_table())
    bias = _bias_build(rel_bias, bucket, "bias_build")
    sinks = _pad_lanes(sinks_a)
    par_b = _pad_lanes(jnp.concatenate([a_log_b, dt_bias_b], axis=1))

    row_x = pl.BlockSpec((tm, D), lambda i, j: (i, 0))
    gu_shape = (2, FF_BLOCKS, s, GU_SHARD)

    def in_proj(h, g, w, w_spec, n_cols, tn, name, deps=(), out_dtype=F32, w_t=False, tm=None):
        return _norm_matmul(h, g, w, w_spec, n_cols // tn, (h.shape[0], n_cols),
                            pl.BlockSpec((tm or _rows(h.shape[0]), tn), lambda i, j: (i, j)), name, deps=deps, out_dtype=out_dtype,
                            w_t=w_t, tm=tm)

    def ffn_fwd(i, h, g_gu, g_dn, deps=()):
        gu, hn = _norm_matmul(h, norm_ffn_g[i:i + 1], g_gu, _spec_gate_up(1), N_DEV, gu_shape,
                              _spec_gu_act(0, 1, tb), f"gate_up_{i}", deps=deps, out_dtype=_ACT, w_t=True, tm=tb)
        h_new, act = _glu_down(gu, ffn_cw[i], ffn_cb[i], g_dn, h, f"glu_down_{i}")
        return h_new, gu, hn, act

    def out_proj(i, mix, h):
        return _matmul_res(mix, row_x, g_out[i], _spec_rowsharded(0, D // N_DEV, D), 1, h, f"out_proj_{i}")

    proj_a, hn_a = in_proj(h0, norm_mix_g[0:1], w_ia, pl.BlockSpec((640, D), lambda i, j: (j, 0)), IN_A, 640, "in_proj_a",
                           deps=ffn0_w.srcs, out_dtype=_ACT, w_t=True)
    memkv0, memn0 = in_proj(memx, norm_mem_g[0:1], g_mk[0], _spec_rowsharded(0, D // N_DEV, 2 * X_Q), 2 * X_Q, 2 * X_Q, "mem_proj_0")
    mix_a = _mix_a_fwd(proj_a, bias, sinks, memkv0, "mix_a_fwd")
    h1 = out_proj(0, mix_a, h0)
    g_gu0, g_dn0 = ffn0_w.lands
    in_b_w = _seq_exchange([after(_reorder_b(w_in_b).astype(_MXU), h1), after(w_mem_kv[1:2].astype(_MXU), h1),
                            after(w_out[1:2].astype(_MXU), h1)],
                           [((N_DEV, 1, D // N_DEV, IN_BP), _MXU), ((N_DEV, 1, D // N_DEV, 2 * X_Q), _MXU),
                            ((N_DEV, 1, D // N_DEV, D), _MXU)], [(0, False), (1, False), (2, False)], "gather_in_b", 2)
    ffn1_w = _seq_exchange([after(t_(w_gate_up)[1].astype(_MXU), h1), after(w_down[1].astype(_MXU), h1)], [gu_land, dn_land], whole,
                           "gather_ffn1", 3)
    h2, gu0, hn_f0, act0 = ffn_fwd(0, h1, g_gu0, g_dn0, deps=in_b_w.srcs + ffn1_w.srcs)
    g_ib, g_mk1, g_out1 = in_b_w.lands
    g_mk.append(g_mk1)
    g_out.append(g_out1)
    proj_b, hn_b = in_proj(h2, norm_mix_g[1:2], g_ib, _spec_rowsharded(0, D // N_DEV, 896, col_block=1), IN_BP, 896, "in_proj_b")
    memkv1, memn1 = in_proj(memx, norm_mem_g[1:2], g_mk[1], _spec_rowsharded(0, D // N_DEV, 2 * X_Q), 2 * X_Q, 2 * X_Q, "mem_proj_1")
    mix_b, states, deltas = _mix_b_fwd(proj_b, conv_qkv, par_b, out_norm_g_b, memkv1, "mix_b_fwd")
    h3 = out_proj(1, mix_b, h2)
    g_gu1, g_dn1 = ffn1_w.lands
    h4, gu1, hn_f1, act1 = ffn_fwd(1, h3, g_gu1, g_dn1)
    loss_row, dh, d_final_g = _loss_head(h4, final_norm_g[None, :], tgt, "loss_head")

    zeros_mem = jnp.zeros_like(memx)
    per_dest2 = [(0, True), (1, True)]

    def ffn_bwd(i, dh, h_in, gu, hn_f, act, g_gu, g_dn, deps=()):
        dgu, d_cw, d_cb = _glu_bwd(gu, ffn_cw[i], ffn_cb[i], dh, g_dn, f"glu_bwd_{i}", deps=deps)
        d_wdown = _matmul_tn(act, pl.BlockSpec((None, tm, GU_SHARD), lambda j, r: (j, r, 0)),
                             dh, pl.BlockSpec((tm, D), lambda j, r: (r, 0)), s, FF_BLOCKS, (GU_SHARD, D),
                             (N_DEV, DN_SHARD, D), pl.BlockSpec((2, DN_SHARD, D), lambda j, r: (j, 0, 0)), f"d_w_down_{i}")
        dh_new, d_g = _matmul_nt_normbwd(dgu, _spec_gu_act(0, 1, tm), g_gu, _spec_gate_up(1), N_DEV, h_in,
                                         norm_ffn_g[i:i + 1], dh, f"d_ffn_in_{i}", w_t=True)
        d_wgu = _matmul_tn(dgu, _spec_gu_act(1, 0, tb), hn_f, pl.BlockSpec((tb, D), lambda j, r: (r, 0)), s, N_DEV,
                           (GU_SHARD, D), (N_DEV, GU_SHARD, D), pl.BlockSpec((None, GU_SHARD, D), lambda j, r: (j, 0, 0)),
                           f"d_w_gate_up_{i}", tm=tb)
        return dh_new, [d_wdown, d_wgu], d_cw, d_cb, d_g

    def out_bwd(i, dh, mix, deps):
        dmix = _matmul_nt(dh, g_out[i], _spec_rowsharded(0, D // N_DEV, D), 1, (s, D), row_x, f"d_mix_{i}", deps=deps, out_dtype=_ACT)
        d_wout = _matmul_tn(mix, pl.BlockSpec((tm, D), lambda j, r: (r, 0)), dh, pl.BlockSpec((tm, D), lambda j, r: (r, 0)),
                            s, 1, (D, D), (N_DEV, D // N_DEV, D), pl.BlockSpec((N_DEV, D // N_DEV, D), lambda j, r: (0, 0, 0)),
                            f"d_w_out_{i}")
        return dmix, d_wout

    def mem_bwd(i, dmemkv, memn):
        tmm = _rows(MEM_LEN)
        _, d_g = _matmul_nt_normbwd(dmemkv, pl.BlockSpec((tmm, 2 * X_Q), lambda r, j: (r, 0)), g_mk[i],
                                    _spec_rowsharded(0, D // N_DEV, 2 * X_Q), 1, memx, norm_mem_g[i:i + 1], zeros_mem,
                                    f"d_mem_in_{i}")
        by_row = lambda j, r: (r, 0)
        d_w = _matmul_tn(memn, pl.BlockSpec((tmm, D), by_row), dmemkv, pl.BlockSpec((tmm, 2 * X_Q), by_row), MEM_LEN, 1,
                         (D, 2 * X_Q), (N_DEV, D // N_DEV, 2 * X_Q),
                         pl.BlockSpec((N_DEV, D // N_DEV, 2 * X_Q), lambda j, r: (0, 0, 0)), f"d_w_mem_kv_{i}")
        return d_w, d_g

    out_land = ((N_DEV, D // N_DEV, D), _WIRE)
    mk_land = ((N_DEV, D // N_DEV, 2 * X_Q), _WIRE)
    ffn_lands = [((N_DEV, DN_SHARD, D), _WIRE), ((N_DEV, GU_SHARD, D), _WIRE)]
    dh, d_ffn1, d_cw1, d_cb1, d_gf1 = ffn_bwd(1, dh, h3, gu1, hn_f1, act1, g_gu1, g_dn1)
    ffn1_g = _seq_exchange(d_ffn1, ffn_lands, per_dest2, "send_ffn1_grads", 5)
    dmix, d_wout1 = out_bwd(1, dh, mix_b, ffn1_g.srcs)
    dproj_b, d_convw, d_par, d_ng, dmemkv1 = _mix_b_bwd(proj_b, conv_qkv, par_b, out_norm_g_b, memkv1, states, deltas, dmix, "mix_b_bwd")
    dh, d_gm1 = _matmul_nt_normbwd(dproj_b, pl.BlockSpec((tm, 896), lambda i, j: (i, j)), g_ib,
                                   _spec_rowsharded(0, D // N_DEV, 896, col_block=1), IN_BP // 896, h2, norm_mix_g[1:2], dh, "d_in_b")
    d_wib = _matmul_tn(hn_b, pl.BlockSpec((tb, D), lambda j, r: (r, 0)), dproj_b, pl.BlockSpec((tb, 896), lambda j, r: (r, j)),
                       s, IN_BP // 896, (D, 896), (N_DEV, D // N_DEV, IN_BP),
                       pl.BlockSpec((N_DEV, D // N_DEV, 896), lambda j, r: (0, 0, j)), "d_w_in_b", tm=tb)
    d_wmk1, d_gmem1 = mem_bwd(1, dmemkv1, memn1)
    mix1_g = _seq_exchange([d_wout1, d_wib, d_wmk1], [out_land, ((N_DEV, D // N_DEV, IN_BP), _WIRE), mk_land],
                           [(0, True), (1, True), (2, True)], "send_mix1_grads", 6)
    dh, d_ffn0, d_cw0, d_cb0, d_gf0 = ffn_bwd(0, dh, h1, gu0, hn_f0, act0, g_gu0, g_dn0, deps=mix1_g.srcs)
    dmix, d_wout0 = out_bwd(0, dh, mix_a, d_ffn0 + ffn1_g.lands[:1])
    ffn0_g = _seq_exchange(d_ffn0 + [d_wout0], ffn_lands + [out_land], per_dest2 + [(2, True)], "send_ffn0_grads", 4)
    dproj_a, dbias, dsinks, dmemkv0 = _mix_a_bwd(proj_a, bias, sinks, memkv0, dmix, "mix_a_bwd", deps=ffn0_g.srcs)
    dh, d_gm0 = _matmul_nt_normbwd(dproj_a, pl.BlockSpec((tm, 640), lambda i, j: (i, j)), w_ia,
                                   pl.BlockSpec((640, D), lambda i, j: (j, 0)), IN_A // 640, h0, norm_mix_g[0:1], dh, "d_in_a",
                                   w_t=True)
    d_wia = _matmul_tn(dproj_a, pl.BlockSpec((tm, IN_A), lambda j, r: (r, 0)), hn_a, pl.BlockSpec((tm, D), lambda j, r: (r, 0)),
                       s, 1, (IN_A, D), (N_DEV, IA_SHARD, D), pl.BlockSpec((N_DEV, IA_SHARD, D), lambda j, r: (0, 0, 0)),
                       "d_w_in_a")
    d_wmk0, d_gmem0 = mem_bwd(0, dmemkv0, memn0)
    d_rel = _bias_reduce(dbias, bucket, "bias_reduce")
    small = _pack_small(d_rel, (d_cb0, d_cb1), (d_cw0, d_cw1), d_convw, (d_gm0, d_gm1), (d_gmem0, d_gmem1),
                        (d_gf0, d_gf1), d_final_g, dsinks, d_par, d_ng, loss_row, "pack_small")
    mix0_g = _seq_exchange([d_wia, d_wmk0, small],
                           [((N_DEV, IA_SHARD, D), _WIRE), mk_land, ((N_DEV, SMALL_ROWS, D_FF), F32)],
                           [(0, True), (1, True), (2, False)], "send_mix0_grads", 7)

    res = {}
    last = []

    def update(nm, parts, tr, restore=False, transposed=False):
        view = t_ if transposed else (lambda a: a)
        out = _adamw(parts, view(wts[nm]), view(moms[nm]), view(vars_[nm]), tr, "adamw_" + nm, restore_b=restore, deps=last[-1:])
        res[nm] = [view(o) for o in out]
        last.append(out[1])

    r_dn1, r_gu1 = ffn1_g.lands
    r_dn0, r_gu0, r_out0 = ffn0_g.lands
    r_out1, r_ib, r_mk1 = mix1_g.lands
    update("w_in_b", [r_ib], 32, True)
    update("w_gate_up", [r_gu0, r_gu1], 176, transposed=True)
    update("w_down", [r_dn0, r_dn1], 176)
    r_ia, r_mk0, r_small = mix0_g.lands
    update("w_mem_kv", [r_mk0, r_mk1], 128)
    update("w_out", [r_out0, r_out1], 128)
    update("w_in_a", [r_ia], IA_SHARD, transposed=True)

    my = 4 * lax.axis_index("x") + 2 * lax.axis_index("y") + lax.axis_index("c")
    cq = conv_qkv_b.shape[-1]
    cf = ffn_conv_w.shape[-1]
    rc_qkv = lax.dynamic_slice_in_dim(r_small[:, SP_QKV:SP_QKV + B_CONV, :B_QKV], my * cq, cq, axis=2)[:, None]
    rc_ffn = lax.dynamic_slice_in_dim(r_small[:, SP_CW:SP_CW + 2 * FFN_CONV, :], my * cf, cf, axis=2).reshape(N_DEV, 2, FFN_CONV, cf)
    as2d = lambda a: a[None, :] if a.ndim == 1 else a
    small_out = _adamw_small(r_small, rc_qkv, rc_ffn, [as2d(wts[n]) for n in _SMALL], [as2d(moms[n]) for n in _SMALL],
                             [as2d(vars_[n]) for n in _SMALL], "adamw_small", deps=last[-1:])
    ns = len(_SMALL)
    for i, nm in enumerate(_SMALL):
        res[nm] = [small_out[k * ns + i].reshape(wts[nm].shape) for k in range(4)]

    return (small_out[-1][0, 0], dh[None], *[res[n][0] for n in order], *[res[n][1] for n in order],
            *[res[n][2] for n in order], *[res[n][3] for n in order])
```

```python
import functools
import math

import numpy as np

import jax
import jax.numpy as jnp
from jax import lax
from jax.experimental import pallas as pl
from jax.experimental.pallas import tpu as pltpu
from jax.experimental.pallas import tpu_sc as plsc

F32 = jnp.float32
_MXU = jnp.bfloat16
_ACT = jnp.bfloat16
_WIRE = jnp.bfloat16
_HI = lax.Precision.HIGH
_TM = 1024
_TM_GLU = 512
_TM_BIG = 2048
_VMEM_LIMIT = 48 * 1024 * 1024
_SDS = jax.ShapeDtypeStruct

D = 1024
EPS = 1e-6
A_HEADS, A_KV_HEADS, A_HD, BLK = 12, 2, 64, 128
N_BUCKETS, MAX_DISTANCE = 32, 128
B_QK_HEADS, B_V_HEADS, B_HD, B_CONV, CHUNK = 3, 6, 128, 4, 64
X_HEADS, X_HD, MEM_LEN = 4, 64, 256
D_FF, FFN_CONV = 2816, 3
A_Q, A_KV, X_Q = 768, 128, 256
B_QK, B_V, B_QKV = 384, 768, 1536
IN_A, IN_B = 1280, 2572
IN_BP = 2688
BP_Z, BP_XQ, BP_GATE = 1536, 2304, 2560
HALO = 8
GLU_HALO = 16

N_DEV = 8
GU_SHARD = 2 * D_FF // N_DEV
FF_BLOCKS = D_FF // GU_SHARD
DN_SHARD = D_FF // N_DEV
IA_SHARD = IN_A // N_DEV

ADAM_LR, ADAM_B1, ADAM_B2, ADAM_EPS, ADAM_WD, ADAM_STEP = 0.001, 0.9, 0.999, 1e-08, 0.01, 10

SP_CB, SP_CW, SP_QKV, SP_MIX, SP_MEM, SP_FFN, SP_FINAL, SP_MISC, SMALL_ROWS = 0, 2, 8, 12, 14, 16, 18, 19, 24
SP_REL_LANE = B_QKV


def _cp(*sems):
    return pltpu.CompilerParams(dimension_semantics=sems, vmem_limit_bytes=_VMEM_LIMIT)


def _mm(a, b):
    return jnp.dot(a.astype(_MXU), b.astype(_MXU), preferred_element_type=F32)


def _mm_nt(a, b):
    return lax.dot_general(a.astype(_MXU), b.astype(_MXU), (((1,), (1,)), ((), ())), preferred_element_type=F32)


def _mm_tn(a, b):
    return lax.dot_general(a.astype(_MXU), b.astype(_MXU), (((0,), (0,)), ((), ())), preferred_element_type=F32)


def _mmf(a, b):
    return jnp.dot(a, b, preferred_element_type=F32, precision=_HI)


def _mmf_nt(a, b):
    return lax.dot_general(a, b, (((1,), (1,)), ((), ())), preferred_element_type=F32, precision=_HI)


def _silu(x):
    return x * jax.nn.sigmoid(x)


def _w2d(ref):
    v = ref[...]
    return v.reshape(-1, v.shape[-1])


def _rows(m):
    return min(m, _TM)


def _spec_rowsharded(layer, rows, cols, col_block=None):
    if col_block is None:
        return pl.BlockSpec((N_DEV, None, rows, cols), lambda *_: (0, layer, 0, 0))
    return pl.BlockSpec((N_DEV, None, rows, cols), lambda *ids: (0, layer, 0, ids[col_block]))


def _spec_gate_up(axis):
    return pl.BlockSpec((None, GU_SHARD, D), lambda *ids: (ids[axis], 0, 0))


def _spec_down(axis):
    return pl.BlockSpec((2, DN_SHARD, D), lambda *ids: (ids[axis], 0, 0))


def _dep_specs(deps):
    return [pl.BlockSpec(memory_space=pl.ANY) for d in deps]


def _spec_gu_act(row_axis, axis, tm):
    return pl.BlockSpec((None, None, tm, GU_SHARD), lambda *ids: (ids[axis] // FF_BLOCKS, ids[axis] % FF_BLOCKS, ids[row_axis], 0))


def _norm_matmul(x, g, w, w_spec, n_blocks, out_shape, out_spec, name, deps=(), out_dtype=F32, w_t=False, tm=None):
    m, k = x.shape
    tm = tm or _rows(m)

    def body(x_ref, g_ref, w_ref, *rest):
        y_ref, hn_ref = rest[-2:]

        @pl.when(pl.program_id(1) == 0)
        def _():
            xv = x_ref[...]
            r = lax.rsqrt(jnp.mean(xv * xv, axis=-1, keepdims=True) + EPS)
            hn_ref[...] = (xv * r * g_ref[...]).astype(hn_ref.dtype)

        y_ref[...] = (_mm_nt if w_t else _mm)(hn_ref[...], _w2d(w_ref)).astype(y_ref.dtype)

    return pl.pallas_call(
        body, grid=(m // tm, n_blocks),
        in_specs=[pl.BlockSpec((tm, k), lambda i, j: (i, 0)), pl.BlockSpec((1, k), lambda i, j: (0, 0)), w_spec]
        + _dep_specs(deps),
        out_specs=[out_spec, pl.BlockSpec((tm, k), lambda i, j: (i, 0))],
        out_shape=[_SDS(out_shape, out_dtype), _SDS((m, k), _ACT)],
        name=name, compiler_params=_cp("arbitrary", "arbitrary"))(x, g, w, *deps)


def _matmul_res(a, a_spec, w, w_spec, n_k, res, name):
    m, n = res.shape
    tm = _rows(m)

    def body(a_ref, w_ref, r_ref, o_ref):
        part = _mm(a_ref[...], _w2d(w_ref))

        @pl.when(pl.program_id(1) == 0)
        def _():
            o_ref[...] = r_ref[...] + part

        @pl.when(pl.program_id(1) > 0)
        def _():
            o_ref[...] += part

    return pl.pallas_call(
        body, grid=(m // tm, n_k),
        in_specs=[a_spec, w_spec, pl.BlockSpec((tm, n), lambda i, j: (i, 0))],
        out_specs=pl.BlockSpec((tm, n), lambda i, j: (i, 0)),
        out_shape=_SDS((m, n), F32), name=name, compiler_params=_cp("arbitrary", "arbitrary"))(a, w, res)


def _matmul_nt(dy, w, w_spec, n_blocks, out_shape, out_spec, name, deps=(), out_dtype=F32):
    m, n = dy.shape
    tm = _rows(m)

    def body(dy_ref, w_ref, *rest):
        o_ref = rest[-1]
        o_ref[...] = _mm_nt(dy_ref[...], _w2d(w_ref)).astype(o_ref.dtype)

    return pl.pallas_call(
        body, grid=(m // tm, n_blocks),
        in_specs=[pl.BlockSpec((tm, n), lambda i, j: (i, 0)), w_spec] + _dep_specs(deps),
        out_specs=out_spec, out_shape=_SDS(out_shape, out_dtype),
        name=name, compiler_params=_cp("arbitrary", "arbitrary"))(dy, w, *deps)


def _matmul_nt_normbwd(dy, dy_spec, w, w_spec, nj, h, g, dh_in, name, w_t=False):
    m, k = h.shape
    tm = _rows(m)

    def body(dy_ref, w_ref, h_ref, g_ref, dhin_ref, dh_ref, dg_ref, acc_ref):
        i, j = pl.program_id(0), pl.program_id(1)

        @pl.when(j == 0)
        def _():
            acc_ref[...] = jnp.zeros_like(acc_ref)

        acc_ref[...] += (_mm if w_t else _mm_nt)(dy_ref[...], _w2d(w_ref))

        @pl.when(j == nj - 1)
        def _():
            xv = h_ref[...]
            r = lax.rsqrt(jnp.mean(xv * xv, axis=-1, keepdims=True) + EPS)
            xh = xv * r
            dhn = acc_ref[...]
            part = jnp.sum(dhn * xh, axis=0, keepdims=True)

            @pl.when(i == 0)
            def _():
                dg_ref[...] = part

            @pl.when(i > 0)
            def _():
                dg_ref[...] += part

            t = dhn * g_ref[...]
            dh_ref[...] = dhin_ref[...] + r * (t - xh * jnp.mean(t * xh, axis=-1, keepdims=True))

    return pl.pallas_call(
        body, grid=(m // tm, nj),
        in_specs=[dy_spec, w_spec, pl.BlockSpec((tm, k), lambda i, j: (i, 0)), pl.BlockSpec((1, k), lambda i, j: (0, 0)),
                  pl.BlockSpec((tm, k), lambda i, j: (i, 0))],
        out_specs=[pl.BlockSpec((tm, k), lambda i, j: (i, 0)), pl.BlockSpec((1, k), lambda i, j: (0, 0))],
        out_shape=[_SDS((m, k), F32), _SDS((1, k), F32)],
        scratch_shapes=[pltpu.VMEM((tm, k), F32)],
        name=name, compiler_params=_cp("arbitrary", "arbitrary"))(dy, w, h, g, dh_in)


def _matmul_tn(x, x_spec, dy, dy_spec, m, n_blocks, acc_shape, out_shape, out_spec, name, tm=None):
    tm = tm or _rows(m)
    nm = m // tm

    def body(x_ref, dy_ref, o_ref, acc_ref):
        @pl.when(pl.program_id(1) == 0)
        def _():
            acc_ref[...] = jnp.zeros_like(acc_ref)

        acc_ref[...] += _mm_tn(x_ref[...], dy_ref[...])

        @pl.when(pl.program_id(1) == nm - 1)
        def _():
            o_ref[...] = acc_ref[...].reshape(o_ref.shape).astype(o_ref.dtype)

    return pl.pallas_call(
        body, grid=(n_blocks, nm), in_specs=[x_spec, dy_spec], out_specs=out_spec,
        out_shape=_SDS(out_shape, _WIRE), scratch_shapes=[pltpu.VMEM(acc_shape, F32)],
        name=name, compiler_params=_cp("arbitrary", "arbitrary"))(x, dy)


def _loss_head(h, g, tgt, name):
    m, k = h.shape
    tm = _rows(m)

    def body(h_ref, g_ref, t_ref, loss_ref, dh_ref, dg_ref):
        i = pl.program_id(0)
        xv = h_ref[...]
        r = lax.rsqrt(jnp.mean(xv * xv, axis=-1, keepdims=True) + EPS)
        xh = xv * r
        gv = g_ref[...]
        err = xh * gv - t_ref[...]
        lpart = jnp.zeros((1, 128), F32) + 0.5 * jnp.sum(jnp.mean(err * err, axis=-1, keepdims=True), axis=0, keepdims=True)
        dy = err * (1.0 / k)
        gpart = jnp.sum(dy * xh, axis=0, keepdims=True)

        @pl.when(i == 0)
        def _():
            loss_ref[...] = lpart
            dg_ref[...] = gpart

        @pl.when(i > 0)
        def _():
            loss_ref[...] += lpart
            dg_ref[...] += gpart

        t = dy * gv
        dh_ref[...] = r * (t - xh * jnp.mean(t * xh, axis=-1, keepdims=True))

    return pl.pallas_call(
        body, grid=(m // tm,),
        in_specs=[pl.BlockSpec((tm, k), lambda i: (i, 0)), pl.BlockSpec((1, k), lambda i: (0, 0)),
                  pl.BlockSpec((tm, k), lambda i: (i, 0))],
        out_specs=[pl.BlockSpec((1, 128), lambda i: (0, 0)), pl.BlockSpec((tm, k), lambda i: (i, 0)),
                   pl.BlockSpec((1, k), lambda i: (0, 0))],
        out_shape=[_SDS((1, 128), F32), _SDS((m, k), F32), _SDS((1, k), F32)],
        name=name, compiler_params=_cp("arbitrary"))(h, g, tgt)


def _glu_down(gu, conv_w, conv_b, w_down, res, name):
    s = gu.shape[2]
    tm = min(s, _TM_GLU)

    def body(gu_ref, prev_ref, w_ref, b_ref, wdn_ref, r_ref, o_ref, act_ref):
        i, j = pl.program_id(0), pl.program_id(1)
        prev = jnp.where(i > 0, prev_ref[...].astype(F32), 0.0)
        ext = jnp.concatenate([prev, gu_ref[0].astype(F32)], axis=0)
        gc = b_ref[...] + w_ref[FFN_CONV - 1:FFN_CONV, :] * ext
        for k in range(FFN_CONV - 1):
            gc = gc + w_ref[k:k + 1, :] * pltpu.roll(ext, FFN_CONV - 1 - k, 0)
        act = (_silu(gc[GLU_HALO:]) * gu_ref[1].astype(F32)).astype(act_ref.dtype)
        act_ref[...] = act
        part = _mm(act, _w2d(wdn_ref))

        @pl.when(j == 0)
        def _():
            o_ref[...] = r_ref[...] + part

        @pl.when(j > 0)
        def _():
            o_ref[...] += part

    return pl.pallas_call(
        body, grid=(s // tm, FF_BLOCKS),
        in_specs=[pl.BlockSpec((2, None, tm, GU_SHARD), lambda i, j: (0, j, i, 0)),
                  pl.BlockSpec((None, None, GLU_HALO, GU_SHARD),
                               lambda i, j: (0, j, jnp.maximum(i * (tm // GLU_HALO) - 1, 0), 0)),
                  pl.BlockSpec((None, HALO, GU_SHARD), lambda i, j: (j, 0, 0)),
                  pl.BlockSpec((None, 1, GU_SHARD), lambda i, j: (j, 0, 0)),
                  _spec_down(1), pl.BlockSpec((tm, D), lambda i, j: (i, 0))],
        out_specs=[pl.BlockSpec((tm, D), lambda i, j: (i, 0)), pl.BlockSpec((None, tm, GU_SHARD), lambda i, j: (j, i, 0))],
        out_shape=[_SDS((s, D), F32), _SDS((FF_BLOCKS, s, GU_SHARD), _ACT)], name=name,
        compiler_params=_cp("arbitrary", "arbitrary"))(gu, gu, conv_w, conv_b, w_down, res)


def _glu_bwd(gu, conv_w, conv_b, dh, w_down, name, deps=()):
    s = gu.shape[2]
    tm = min(s, _TM_GLU)
    nt = s // tm
    ext_rows = tm + GLU_HALO

    def body(gu_ref, prev_ref, w_ref, b_ref, dh_ref, wdn_ref, *rest):
        dgu_ref, dw_ref, db_ref, carry_ref = rest[-4:]
        t = pl.program_id(1)
        i = nt - 1 - t

        @pl.when(t == 0)
        def _():
            carry_ref[...] = jnp.zeros_like(carry_ref)
            dw_ref[...] = jnp.zeros_like(dw_ref)
            db_ref[...] = jnp.zeros_like(db_ref)

        up = gu_ref[1].astype(F32)
        prev = jnp.where(i > 0, prev_ref[...].astype(F32), 0.0)
        ext = jnp.concatenate([prev, gu_ref[0].astype(F32)], axis=0)
        shifted = [pltpu.roll(ext, FFN_CONV - 1 - j, 0) if j < FFN_CONV - 1 else ext for j in range(FFN_CONV)]
        gc = b_ref[...] + shifted[0] * w_ref[0:1, :]
        for j in range(1, FFN_CONV):
            gc = gc + shifted[j] * w_ref[j:j + 1, :]
        gc = gc[GLU_HALO:]
        sg = jax.nn.sigmoid(gc)
        da = _mm_nt(dh_ref[...], _w2d(wdn_ref))
        dup = da * (gc * sg)
        dgc = da * up * (sg * (1.0 + gc * (1.0 - sg)))
        db_ref[...] += jnp.sum(dgc, axis=0, keepdims=True)
        dgc_ext = jnp.concatenate([jnp.zeros((GLU_HALO, GU_SHARD), F32), dgc], axis=0)
        dext = dgc_ext * w_ref[FFN_CONV - 1:FFN_CONV, :]
        for j in range(FFN_CONV):
            dw_ref[j:j + 1, :] += jnp.sum(shifted[j] * dgc_ext, axis=0, keepdims=True)
            if j < FFN_CONV - 1:
                dext = dext + w_ref[j:j + 1, :] * pltpu.roll(dgc_ext, ext_rows - (FFN_CONV - 1 - j), 0)
        tail = jnp.concatenate([jnp.zeros((tm - GLU_HALO, GU_SHARD), F32), carry_ref[...]], axis=0)
        dgate = dext[GLU_HALO:] + tail
        carry_ref[...] = dext[:GLU_HALO]
        dgu_ref[0] = dgate.astype(dgu_ref.dtype)
        dgu_ref[1] = dup.astype(dgu_ref.dtype)

    return pl.pallas_call(
        body, grid=(FF_BLOCKS, nt),
        in_specs=[pl.BlockSpec((2, None, tm, GU_SHARD), lambda j, t: (0, j, nt - 1 - t, 0)),
                  pl.BlockSpec((None, None, GLU_HALO, GU_SHARD),
                               lambda j, t: (0, j, jnp.maximum((nt - 1 - t) * (tm // GLU_HALO) - 1, 0), 0)),
                  pl.BlockSpec((None, HALO, GU_SHARD), lambda j, t: (j, 0, 0)),
                  pl.BlockSpec((None, 1, GU_SHARD), lambda j, t: (j, 0, 0)),
                  pl.BlockSpec((tm, D), lambda j, t: (nt - 1 - t, 0)), _spec_down(0)] + _dep_specs(deps),
        out_specs=[pl.BlockSpec((2, None, tm, GU_SHARD), lambda j, t: (0, j, nt - 1 - t, 0)),
                   pl.BlockSpec((None, HALO, GU_SHARD), lambda j, t: (j, 0, 0)),
                   pl.BlockSpec((None, 1, GU_SHARD), lambda j, t: (j, 0, 0))],
        out_shape=[_SDS(gu.shape, _ACT), _SDS((FF_BLOCKS, HALO, GU_SHARD), F32), _SDS((FF_BLOCKS, 1, GU_SHARD), F32)],
        scratch_shapes=[pltpu.VMEM((GLU_HALO, GU_SHARD), F32)],
        name=name, compiler_params=_cp("arbitrary", "arbitrary"))(gu, gu, conv_w, conv_b, dh, w_down, *deps)


def _bucket_table():
    qi = np.arange(BLK)[:, None]
    kj = np.arange(BLK)[None, :]
    n = np.where(kj > qi, BLK + qi - kj, qi - kj)
    max_exact = N_BUCKETS // 2
    nf = np.maximum(n, 1).astype(np.float32)
    large = max_exact + (np.log(nf / max_exact) / math.log(MAX_DISTANCE / max_exact)
                         * (N_BUCKETS - max_exact)).astype(np.int32)
    large = np.minimum(large, N_BUCKETS - 1)
    return np.where(n < max_exact, n, large).astype(np.int32)


def _lane_low():
    return lax.broadcasted_iota(jnp.int32, (1, 128), 1) < A_HD


def _swa_groups(q, kd, vd, sink, bias, upper, first):
    n = A_HEADS // A_KV_HEADS
    ng = A_KV_HEADS
    low = _lane_low()
    qm = [jnp.concatenate([jnp.where(low == (h % 2 == 0), q[g][:, (h // 2) * 128:(h // 2 + 1) * 128], 0.0) for h in range(n)], axis=0)
          for g in range(ng)]
    s2 = [_mm_nt(qm[g], kd[g]) * (A_HD ** -0.5) for g in range(ng)]
    s = [jnp.where(upper[None], s2[g][:, :BLK].reshape(n, BLK, BLK), s2[g][:, BLK:].reshape(n, BLK, BLK)) + bias[g] for g in range(ng)]
    s = [jnp.where((upper & first)[None], -jnp.inf, t) for t in s]
    m = [lax.stop_gradient(jnp.maximum(jnp.max(s[g], axis=-1, keepdims=True), sink[g])) for g in range(ng)]
    p = [jnp.exp(s[g] - m[g]) for g in range(ng)]
    split = [jnp.concatenate([jnp.where(upper[None], t, 0.0), jnp.where(upper[None], 0.0, t)], axis=-1).reshape(n * BLK, 2 * BLK)
             for t in p]
    ones = jnp.ones((BLK, 128), F32)
    den = [_mm(p[g].reshape(n * BLK, BLK), ones) + jnp.exp(sink[g] - m[g]).reshape(n * BLK, 1) for g in range(ng)]
    o = [_mm(split[g], vd[g]) / den[g] for g in range(ng)]
    return [jnp.concatenate([jnp.where(low, t[2 * k * BLK:(2 * k + 1) * BLK], t[(2 * k + 1) * BLK:(2 * k + 2) * BLK])
                             for k in range(n // 2)], axis=1) for t in o]


def _mix_a_core(q, kd, vd, sink, bias, xq, mk, mv, upper, first):
    return _swa_groups(q, kd, vd, sink, bias, upper, first), _cross_pairs(xq, mk, mv)


def _swa_sinks(sink_ref, g):
    n = A_HEADS // A_KV_HEADS
    return jnp.concatenate([sink_ref[:, h:h + 1] for h in range(g * n, (g + 1) * n)], axis=0).reshape(n, 1, 1)


def _both_halves(t, t_rolled, g):
    low = _lane_low()
    return jnp.where(low, t, t_rolled) if g == 0 else jnp.where(low, t_rolled, t)


def _cross_pairs(q, mk, mv):
    rows = q.shape[0]
    low = _lane_low()
    qm = [jnp.concatenate([jnp.where(low, q[:, p * 128:(p + 1) * 128], 0.0), jnp.where(low, 0.0, q[:, p * 128:(p + 1) * 128])], axis=0)
          for p in range(X_HEADS // 2)]
    s = [_mm_nt(qm[p], mk[:, p * 128:(p + 1) * 128]) * (X_HD ** -0.5) for p in range(X_HEADS // 2)]
    e = [jnp.exp(t - lax.stop_gradient(jnp.max(t, axis=-1, keepdims=True))) for t in s]
    pr = [t / jnp.sum(t, axis=-1, keepdims=True) for t in e]
    o = [_mm(pr[p], mv[:, p * 128:(p + 1) * 128]) for p in range(X_HEADS // 2)]
    return jnp.concatenate([jnp.where(low, t[:rows], t[rows:]) for t in o], axis=1)


def _swa_upper():
    qi = lax.broadcasted_iota(jnp.int32, (BLK, BLK), 0)
    kj = lax.broadcasted_iota(jnp.int32, (BLK, BLK), 1)
    return kj > qi


def _bias_build(rel_bias, bucket, name):
    def body(rb_ref, bucket_ref, o_ref):
        b = bucket_ref[...]
        for h in range(A_HEADS):
            acc = jnp.zeros((BLK, BLK), F32)
            for k in range(N_BUCKETS):
                acc = jnp.where(b == k, rb_ref[k, h], acc)
            o_ref[h] = acc

    return pl.pallas_call(
        body, in_specs=[pl.BlockSpec(memory_space=pltpu.SMEM), pl.BlockSpec(memory_space=pltpu.VMEM)],
        out_specs=pl.BlockSpec(memory_space=pltpu.VMEM),
        out_shape=_SDS((A_HEADS, BLK, BLK), F32), name=name)(rel_bias, bucket)


def _bias_reduce(dbias, bucket, name):
    def body(db_ref, bucket_ref, o_ref):
        b = bucket_ref[...]
        row = lax.broadcasted_iota(jnp.int32, (N_BUCKETS, 128), 0)
        lane = lax.broadcasted_iota(jnp.int32, (N_BUCKETS, 128), 1)
        acc = jnp.zeros((N_BUCKETS, 128), F32)
        for h in range(A_HEADS):
            v = db_ref[h]
            for k in range(N_BUCKETS):
                sk = jnp.sum(jnp.sum(jnp.where(b == k, v, 0.0), axis=1, keepdims=True), axis=0, keepdims=True)
                acc = acc + jnp.where((row == k) & (lane == h), sk, 0.0)
        o_ref[...] = acc

    return pl.pallas_call(
        body, in_specs=[pl.BlockSpec(memory_space=pltpu.VMEM)] * 2,
        out_specs=pl.BlockSpec(memory_space=pltpu.VMEM),
        out_shape=_SDS((N_BUCKETS, 128), F32), name=name)(dbias, bucket)


def _mix_a_fwd(proj, bias, sinks, memkv, name):
    s = proj.shape[0]
    nb = s // BLK
    grp = A_HEADS // A_KV_HEADS

    def body(proj_ref, prev_ref, bias_ref, sink_ref, memkv_ref, o_ref):
        i = pl.program_id(0)
        upper = _swa_upper()
        prev = prev_ref[...].astype(F32)
        proj = proj_ref[...].astype(F32)
        kb = jnp.concatenate([prev[:, :A_KV], proj[:, A_Q:A_Q + A_KV]], axis=0)
        vb = jnp.concatenate([prev[:, A_KV:], proj[:, A_Q + A_KV:A_Q + 2 * A_KV]], axis=0)
        kb_r = pltpu.roll(kb, A_HD, 1)
        vb_r = pltpu.roll(vb, A_HD, 1)
        gw = A_Q // A_KV_HEADS
        groups = range(A_KV_HEADS)
        swa, cross = _mix_a_core([proj[:, g * gw:(g + 1) * gw] for g in groups], [_both_halves(kb, kb_r, g) for g in groups],
                                 [_both_halves(vb, vb_r, g) for g in groups], [_swa_sinks(sink_ref, g) for g in groups],
                                 [bias_ref[g * grp:(g + 1) * grp] for g in groups], proj[:, A_Q + 2 * A_KV:],
                                 memkv_ref[:, :X_Q], memkv_ref[:, X_Q:], upper, i == 0)
        o_ref[...] = jnp.concatenate(swa + [cross], axis=1).astype(o_ref.dtype)

    return pl.pallas_call(
        body, grid=(nb,),
        in_specs=[pl.BlockSpec((BLK, IN_A), lambda i: (i, 0)),
                  pl.BlockSpec((BLK, 2 * A_KV), lambda i: (jnp.maximum(i - 1, 0), A_Q // (2 * A_KV))),
                  pl.BlockSpec((A_HEADS, BLK, BLK), lambda i: (0, 0, 0)),
                  pl.BlockSpec((1, 128), lambda i: (0, 0)),
                  pl.BlockSpec((MEM_LEN, 2 * X_Q), lambda i: (0, 0))],
        out_specs=pl.BlockSpec((BLK, D), lambda i: (i, 0)),
        out_shape=_SDS((s, D), _ACT), name=name, compiler_params=_cp("arbitrary"))(proj, proj, bias, sinks, memkv)


def _mix_a_bwd(proj, bias, sinks, memkv, dmix, name, deps=()):
    s = proj.shape[0]
    nb = s // BLK
    grp = A_HEADS // A_KV_HEADS

    def body(proj_ref, prev_ref, bias_ref, sink_ref, memkv_ref, dmix_ref, *rest):
        dproj_ref, dbias_ref, dsink_ref, dmemkv_ref, carry_ref = rest[-5:]
        t = pl.program_id(0)
        i = nb - 1 - t

        @pl.when(t == 0)
        def _():
            carry_ref[...] = jnp.zeros_like(carry_ref)
            dbias_ref[...] = jnp.zeros_like(dbias_ref)
            dsink_ref[...] = jnp.zeros_like(dsink_ref)
            dmemkv_ref[...] = jnp.zeros_like(dmemkv_ref)

        upper = _swa_upper()
        lane = lax.broadcasted_iota(jnp.int32, (1, 128), 1)
        low = _lane_low()
        prev = prev_ref[...].astype(F32)
        proj = proj_ref[...].astype(F32)
        kb = jnp.concatenate([prev[:, :A_KV], proj[:, A_Q:A_Q + A_KV]], axis=0)
        vb = jnp.concatenate([prev[:, A_KV:], proj[:, A_Q + A_KV:A_Q + 2 * A_KV]], axis=0)
        kb_r = pltpu.roll(kb, A_HD, 1)
        vb_r = pltpu.roll(vb, A_HD, 1)
        gw = A_Q // A_KV_HEADS
        groups = range(A_KV_HEADS)
        _, vjp = jax.vjp(
            functools.partial(_mix_a_core, upper=upper, first=i == 0),
            [proj[:, g * gw:(g + 1) * gw] for g in groups], [_both_halves(kb, kb_r, g) for g in groups],
            [_both_halves(vb, vb_r, g) for g in groups], [_swa_sinks(sink_ref, g) for g in groups],
            [bias_ref[g * grp:(g + 1) * grp] for g in groups], proj[:, A_Q + 2 * A_KV:], memkv_ref[:, :X_Q], memkv_ref[:, X_Q:])
        dqs, dk, dv, ds, db, dxq, dmk, dmv = vjp(
            ([dmix_ref[:, g * gw:(g + 1) * gw].astype(F32) for g in groups], dmix_ref[:, A_Q:].astype(F32)))
        dkd = [t + pltpu.roll(t, A_HD, 1) for t in dk]
        dvd = [t + pltpu.roll(t, A_HD, 1) for t in dv]
        dsink = jnp.zeros((1, 128), F32)
        for g in groups:
            for h in range(grp):
                dsink = dsink + jnp.where(lane == g * grp + h, ds[g][h], 0.0)
            dbias_ref[g * grp:(g + 1) * grp] += db[g]
        dsink_ref[...] += dsink
        dkb = jnp.where(low, dkd[0], dkd[1])
        dvb = jnp.where(low, dvd[0], dvd[1])
        dmemkv_ref[...] += jnp.concatenate([dmk, dmv], axis=1)
        dkv_cur = jnp.concatenate([dkb[BLK:], dvb[BLK:]], axis=1) + carry_ref[...]
        carry_ref[...] = jnp.concatenate([dkb[:BLK], dvb[:BLK]], axis=1)
        dproj_ref[...] = jnp.concatenate(list(dqs) + [dkv_cur, dxq], axis=1).astype(dproj_ref.dtype)

    return pl.pallas_call(
        body, grid=(nb,),
        in_specs=[pl.BlockSpec((BLK, IN_A), lambda t: (nb - 1 - t, 0)),
                  pl.BlockSpec((BLK, 2 * A_KV), lambda t: (jnp.maximum(nb - 2 - t, 0), A_Q // (2 * A_KV))),
                  pl.BlockSpec((A_HEADS, BLK, BLK), lambda t: (0, 0, 0)),
                  pl.BlockSpec((1, 128), lambda t: (0, 0)),
                  pl.BlockSpec((MEM_LEN, 2 * X_Q), lambda t: (0, 0)),
                  pl.BlockSpec((BLK, D), lambda t: (nb - 1 - t, 0))] + _dep_specs(deps),
        out_specs=[pl.BlockSpec((BLK, IN_A), lambda t: (nb - 1 - t, 0)),
                   pl.BlockSpec((A_HEADS, BLK, BLK), lambda t: (0, 0, 0)),
                   pl.BlockSpec((1, 128), lambda t: (0, 0)),
                   pl.BlockSpec((MEM_LEN, 2 * X_Q), lambda t: (0, 0))],
        out_shape=[_SDS((s, IN_A), _ACT), _SDS((A_HEADS, BLK, BLK), F32), _SDS((1, 128), F32),
                   _SDS((MEM_LEN, 2 * X_Q), F32)],
        scratch_shapes=[pltpu.VMEM((BLK, 2 * A_KV), F32)],
        name=name, compiler_params=_cp("arbitrary"))(proj, proj, bias, sinks, memkv, dmix, *deps)


def _neumann(pw, rhs):
    nh = len(pw)
    x = rhs
    for lvl in range(6):
        if lvl < 5:
            prod = [_mmf(pw[h], jnp.concatenate([x[h], pw[h]], axis=1)) for h in range(nh)]
            x = [x[h] + prod[h][:, :B_HD] for h in range(nh)]
            pw = [t[:, B_HD:] for t in prod]
        else:
            x = [x[h] + _mmf(pw[h], x[h]) for h in range(nh)]
    return x


@jax.custom_vjp
def _tri_solve(pw, rhs):
    return _neumann(pw, rhs)


def _tri_solve_fwd(pw, rhs):
    x = _neumann(pw, rhs)
    return x, (pw, x)


def _tri_solve_bwd(res, dx):
    pw, x = res
    d_rhs = _neumann([t.T for t in pw], list(dx))
    return [_mmf_nt(d_rhs[h], x[h]) for h in range(len(pw))], d_rhs


_tri_solve.defvjp(_tri_solve_fwd, _tri_solve_bwd)


@jax.custom_vjp
def _tri_solved(pw, rhs, x):
    return x


def _tri_solved_fwd(pw, rhs, x):
    return x, (pw, x)


def _tri_solved_bwd(res, dx):
    d_pw, d_rhs = _tri_solve_bwd(res, dx)
    return d_pw, d_rhs, [jnp.zeros_like(t) for t in res[1]]


_tri_solved.defvjp(_tri_solved_fwd, _tri_solved_bwd)


@jax.custom_vjp
def _known(x, value):
    return value


def _known_fwd(x, value):
    return value, None


def _known_bwd(_, g):
    return g, jnp.zeros_like(g)


_known.defvjp(_known_fwd, _known_bwd)


def _dn_heads(yq, yk, yv, z, bl, al, a_log, dtb, ng, s0, solved=None, out_known=None):
    c = CHUNK
    nh = B_V_HEADS
    rep = B_V_HEADS // B_QK_HEADS
    r = lax.broadcasted_iota(jnp.int32, (c, c), 0)
    cc = lax.broadcasted_iota(jnp.int32, (c, c), 1)
    q = [_silu(t) for t in yq]
    k = [_silu(t) for t in yk]
    v = [_silu(t) for t in yv]
    q = [t * lax.rsqrt(jnp.sum(t * t, axis=-1, keepdims=True) + EPS) * (B_HD ** -0.5) for t in q]
    k = [t * lax.rsqrt(jnp.sum(t * t, axis=-1, keepdims=True) + EPS) for t in k]
    beta = [jax.nn.sigmoid(t) for t in bl]
    g = [-jnp.exp(a_log[h]) * jax.nn.softplus(al[h] + dtb[h]) for h in range(nh)]
    gb = [jnp.broadcast_to(t, (c, c)) for t in g]
    gc_col = [jnp.sum(jnp.where(cc <= r, t.T, 0.0), axis=1, keepdims=True) for t in gb]
    gc_row = [jnp.sum(jnp.where(r <= cc, t, 0.0), axis=0, keepdims=True) for t in gb]
    gc_last = [jnp.sum(t, axis=0, keepdims=True) for t in g]
    decay = [jnp.exp(jnp.where(r >= cc, gc_col[h] - gc_row[h], -jnp.inf)) for h in range(nh)]
    kq = [_mmf_nt(jnp.concatenate([k[h], q[h]], axis=0), k[h]) for h in range(B_QK_HEADS)]
    kk = [t[:c] for t in kq]
    qk = [t[c:] for t in kq]
    egc = [jnp.exp(t) for t in gc_col]
    both = [_mmf(jnp.concatenate([(beta[h] * egc[h]) * k[h // rep], q[h // rep] * egc[h]], axis=0), s0[h]) for h in range(nh)]
    rhs = [beta[h] * v[h] - both[h][:c] for h in range(nh)]
    qs0 = [t[c:] for t in both]
    pw = [-(beta[h] * kk[h // rep] * jnp.where(r > cc, decay[h], 0.0)) for h in range(nh)]
    delta = _tri_solve(pw, rhs) if solved is None else _tri_solved(pw, rhs, solved)
    last = [_mmf(jnp.concatenate([qk[h // rep] * decay[h], (k[h // rep] * jnp.exp(gc_last[h] - gc_col[h])).T], axis=0), delta[h])
            for h in range(nh)]
    out = [qs0[h] + last[h][:c] for h in range(nh)]
    if out_known is not None:
        out = [_known(out[h], out_known[h]) for h in range(nh)]
    s1 = [jnp.exp(gc_last[h]) * s0[h] + last[h][c:] for h in range(nh)]
    o = [t * lax.rsqrt(jnp.mean(t * t, axis=-1, keepdims=True) + EPS) * ng for t in out]
    return [o[h] * _silu(z[h]) for h in range(nh)], s1, delta, out


def _dn_conv(ext, w_ref):
    y = ext * w_ref[B_CONV - 1:B_CONV, :]
    for j in range(B_CONV - 1):
        y = y + w_ref[j:j + 1, :] * pltpu.roll(ext, B_CONV - 1 - j, 0)
    return y


def _dn_args(y, cur_ref, par_ref, ng_ref):
    nh = B_V_HEADS
    return ([y[:, h * B_HD:(h + 1) * B_HD] for h in range(B_QK_HEADS)],
            [y[:, B_QK + h * B_HD:B_QK + (h + 1) * B_HD] for h in range(B_QK_HEADS)],
            [y[:, 2 * B_QK + h * B_HD:2 * B_QK + (h + 1) * B_HD] for h in range(nh)],
            [cur_ref[:, BP_Z + h * B_HD:BP_Z + (h + 1) * B_HD] for h in range(nh)],
            [cur_ref[:, BP_GATE + h:BP_GATE + h + 1] for h in range(nh)],
            [cur_ref[:, BP_GATE + nh + h:BP_GATE + nh + h + 1] for h in range(nh)],
            [par_ref[:, h:h + 1] for h in range(nh)], [par_ref[:, nh + h:nh + h + 1] for h in range(nh)], ng_ref[...])


def _mix_b_fwd(proj, conv_w, par, ng, memkv, name):
    s = proj.shape[0]
    nc = s // CHUNK

    def body(cur_ref, prev_ref, w_ref, par_ref, ng_ref, memkv_ref, o_ref, st_ref, dl_ref, state_ref):
        n = pl.program_id(0)

        @pl.when(n == 0)
        def _():
            state_ref[...] = jnp.zeros_like(state_ref)

        prev = jnp.where(n > 0, prev_ref[...], 0.0)
        ext = jnp.concatenate([prev, cur_ref[:, :B_QKV]], axis=0)
        y = _dn_conv(ext, w_ref)[HALO:]
        s0 = [state_ref[hv] for hv in range(B_V_HEADS)]
        st_ref[0] = state_ref[...]
        outs, s1, delta, raw = _dn_heads(*_dn_args(y, cur_ref, par_ref, ng_ref), s0)
        for hv in range(B_V_HEADS):
            state_ref[hv] = s1[hv]
            dl_ref[0, hv] = delta[hv]
            dl_ref[0, B_V_HEADS + hv] = raw[hv]
        outs = outs + [_cross_pairs(cur_ref[:, BP_XQ:BP_XQ + X_Q], memkv_ref[:, :X_Q], memkv_ref[:, X_Q:])]
        o_ref[...] = jnp.concatenate(outs, axis=1).astype(o_ref.dtype)

    return pl.pallas_call(
        body, grid=(nc,),
        in_specs=[pl.BlockSpec((CHUNK, IN_BP), lambda n: (n, 0)),
                  pl.BlockSpec((HALO, B_QKV), lambda n: (jnp.maximum(n * (CHUNK // HALO) - 1, 0), 0)),
                  pl.BlockSpec((HALO, B_QKV), lambda n: (0, 0)),
                  pl.BlockSpec((1, 128), lambda n: (0, 0)), pl.BlockSpec((1, 128), lambda n: (0, 0)),
                  pl.BlockSpec((MEM_LEN, 2 * X_Q), lambda n: (0, 0))],
        out_specs=[pl.BlockSpec((CHUNK, D), lambda n: (n, 0)),
                   pl.BlockSpec((1, B_V_HEADS, B_HD, B_HD), lambda n: (n, 0, 0, 0)),
                   pl.BlockSpec((1, 2 * B_V_HEADS, CHUNK, B_HD), lambda n: (n, 0, 0, 0))],
        out_shape=[_SDS((s, D), _ACT), _SDS((nc, B_V_HEADS, B_HD, B_HD), F32), _SDS((nc, 2 * B_V_HEADS, CHUNK, B_HD), F32)],
        scratch_shapes=[pltpu.VMEM((B_V_HEADS, B_HD, B_HD), F32)],
        name=name, compiler_params=_cp("arbitrary"))(proj, proj, conv_w, par, ng, memkv)


def _mix_b_bwd(proj, conv_w, par, ng, memkv, states, deltas, dmix, name):
    s = proj.shape[0]
    nc = s // CHUNK
    ext_rows = CHUNK + HALO

    def body(cur_ref, prev_ref, w_ref, par_ref, ng_ref, memkv_ref, st_ref, dl_ref, dmix_ref,
             dproj_ref, dw_ref, dpar_ref, dng_ref, dmemkv_ref, dstate_ref, carry_ref):
        t = pl.program_id(0)
        n = nc - 1 - t

        @pl.when(t == 0)
        def _():
            dstate_ref[...] = jnp.zeros_like(dstate_ref)
            carry_ref[...] = jnp.zeros_like(carry_ref)
            dw_ref[...] = jnp.zeros_like(dw_ref)
            dpar_ref[...] = jnp.zeros_like(dpar_ref)
            dng_ref[...] = jnp.zeros_like(dng_ref)
            dmemkv_ref[...] = jnp.zeros_like(dmemkv_ref)

        lane = lax.broadcasted_iota(jnp.int32, (1, 128), 1)
        prev = jnp.where(n > 0, prev_ref[...], 0.0)
        ext = jnp.concatenate([prev, cur_ref[:, :B_QKV]], axis=0)
        y = _dn_conv(ext, w_ref)[HALO:]
        solved = [dl_ref[0, hv] for hv in range(B_V_HEADS)]
        raw = [dl_ref[0, B_V_HEADS + hv] for hv in range(B_V_HEADS)]
        _, vjp = jax.vjp(functools.partial(_dn_heads, solved=solved, out_known=raw), *_dn_args(y, cur_ref, par_ref, ng_ref),
                         [st_ref[0, hv] for hv in range(B_V_HEADS)])
        none = [jnp.zeros((CHUNK, B_HD), F32)] * B_V_HEADS
        dyq, dyk, dyv, dz, gbl, gal, ga_log, gdtb, dng, gs0 = vjp(
            ([dmix_ref[:, hv * B_HD:(hv + 1) * B_HD].astype(F32) for hv in range(B_V_HEADS)],
             [dstate_ref[hv] for hv in range(B_V_HEADS)], none, none))
        dgate = jnp.zeros((CHUNK, 128), F32)
        dpar = jnp.zeros((1, 128), F32)
        for hv in range(B_V_HEADS):
            dstate_ref[hv] = gs0[hv]
            dgate = dgate + jnp.where(lane == hv, gbl[hv], 0.0) + jnp.where(lane == B_V_HEADS + hv, gal[hv], 0.0)
            dpar = dpar + jnp.where(lane == hv, ga_log[hv], 0.0) + jnp.where(lane == B_V_HEADS + hv, gdtb[hv], 0.0)
        dpar_ref[...] += dpar
        dng_ref[...] += dng
        _, vjp = jax.vjp(_cross_pairs, cur_ref[:, BP_XQ:BP_XQ + X_Q], memkv_ref[:, :X_Q], memkv_ref[:, X_Q:])
        dxq, dmk, dmv = vjp(dmix_ref[:, B_V:].astype(F32))
        dmemkv_ref[...] += jnp.concatenate([dmk, dmv], axis=1)
        dy = jnp.concatenate(list(dyq) + list(dyk) + list(dyv), axis=1)
        dy_ext = jnp.concatenate([jnp.zeros((HALO, B_QKV), F32), dy], axis=0)
        dext = dy_ext * w_ref[B_CONV - 1:B_CONV, :]
        dw_ref[B_CONV - 1:B_CONV, :] += jnp.sum(ext * dy_ext, axis=0, keepdims=True)
        for j in range(B_CONV - 1):
            sh = B_CONV - 1 - j
            dw_ref[j:j + 1, :] += jnp.sum(pltpu.roll(ext, sh, 0) * dy_ext, axis=0, keepdims=True)
            dext = dext + w_ref[j:j + 1, :] * pltpu.roll(dy_ext, ext_rows - sh, 0)
        tail = jnp.concatenate([jnp.zeros((CHUNK - HALO, B_QKV), F32), carry_ref[...]], axis=0)
        dqkv = dext[HALO:] + tail
        carry_ref[...] = dext[:HALO]
        dproj_ref[...] = jnp.concatenate([dqkv] + list(dz) + [dxq, dgate], axis=1).astype(dproj_ref.dtype)

    return pl.pallas_call(
        body, grid=(nc,),
        in_specs=[pl.BlockSpec((CHUNK, IN_BP), lambda t: (nc - 1 - t, 0)),
                  pl.BlockSpec((HALO, B_QKV), lambda t: (jnp.maximum((nc - 1 - t) * (CHUNK // HALO) - 1, 0), 0)),
                  pl.BlockSpec((HALO, B_QKV), lambda t: (0, 0)),
                  pl.BlockSpec((1, 128), lambda t: (0, 0)), pl.BlockSpec((1, 128), lambda t: (0, 0)),
                  pl.BlockSpec((MEM_LEN, 2 * X_Q), lambda t: (0, 0)),
                  pl.BlockSpec((1, B_V_HEADS, B_HD, B_HD), lambda t: (nc - 1 - t, 0, 0, 0)),
                  pl.BlockSpec((1, 2 * B_V_HEADS, CHUNK, B_HD), lambda t: (nc - 1 - t, 0, 0, 0)),
                  pl.BlockSpec((CHUNK, D), lambda t: (nc - 1 - t, 0))],
        out_specs=[pl.BlockSpec((CHUNK, IN_BP), lambda t: (nc - 1 - t, 0)),
                   pl.BlockSpec((HALO, B_QKV), lambda t: (0, 0)),
                   pl.BlockSpec((1, 128), lambda t: (0, 0)), pl.BlockSpec((1, 128), lambda t: (0, 0)),
                   pl.BlockSpec((MEM_LEN, 2 * X_Q), lambda t: (0, 0))],
        out_shape=[_SDS((s, IN_BP), _ACT), _SDS((HALO, B_QKV), F32), _SDS((1, 128), F32), _SDS((1, 128), F32),
                   _SDS((MEM_LEN, 2 * X_Q), F32)],
        scratch_shapes=[pltpu.VMEM((B_V_HEADS, B_HD, B_HD), F32), pltpu.VMEM((HALO, B_QKV), F32)],
        name=name, compiler_params=_cp("arbitrary"))(proj, proj, conv_w, par, ng, memkv, states, deltas, dmix)


def _place():
    return lax.axis_index("x"), lax.axis_index("y"), lax.axis_index("c")


def _all_gather(shards, name):
    n = len(shards)

    def body(*refs):
        ins, outs = refs[:n], refs[n:2 * n]
        send_sems, recv_sems, local_sems = refs[2 * n:]
        x, y, c = _place()
        me, sibling = (x, y, c), (x, y, 1 - c)
        chips = [(1 - x, y), (x, 1 - y), (1 - x, 1 - y)]

        def rows(a, px, py, pc):
            return outs[a].at[4 * px + 2 * py + pc]

        def copy(a, k, block, to, src=None):
            return pltpu.make_async_remote_copy(
                src_ref=rows(a, *block) if src is None else src, dst_ref=rows(a, *block),
                send_sem=send_sems.at[a, k], recv_sem=recv_sems.at[a, k],
                device_id=to, device_id_type=pl.DeviceIdType.MESH)

        mine = [pltpu.make_async_copy(ins[a], rows(a, *me), local_sems.at[a]) for a in range(n)]
        for cp in mine:
            cp.start()
        first = []
        for a in range(n):
            first.append(copy(a, 0, me, sibling, src=ins[a]))
            first += [copy(a, 1 + j, me, (*chip, c), src=ins[a]) for j, chip in enumerate(chips)]
        for cp in first:
            cp.start()
        passed = []
        for j, chip in enumerate(chips):
            for a in range(n):
                copy(a, 1 + j, (*chip, c), me).wait_recv()
                fwd = copy(a, 4 + j, (*chip, c), sibling)
                fwd.start()
                passed.append(fwd)
        for a in range(n):
            copy(a, 0, sibling, me).wait_recv()
            for j, chip in enumerate(chips):
                copy(a, 4 + j, (*chip, 1 - c), me).wait_recv()
        for cp in first + passed:
            cp.wait_send()
        for cp in mine:
            cp.wait()

    hbm = pl.BlockSpec(memory_space=pl.ANY)
    return pl.pallas_call(
        body, out_shape=[_SDS((N_DEV,) + s.shape, s.dtype) for s in shards],
        in_specs=[hbm] * n, out_specs=[hbm] * n,
        scratch_shapes=[pltpu.SemaphoreType.DMA((n, 7)), pltpu.SemaphoreType.DMA((n, 7)), pltpu.SemaphoreType.DMA((n,))],
        name=name)(*shards)


class _Exchange:
    def __init__(self, lands, srcs):
        self.lands, self.srcs = lands, srcs


def _seq_exchange(srcs, land_shapes, plan, name, cid):
    n, nl = len(srcs), len(land_shapes)

    def launch(*refs):
        src_refs, land_refs = refs[:n], refs[n:n + nl]
        send_sems, recv_sems, local_sems = refs[n + nl:]
        x, y, c = _place()
        my = 4 * x + 2 * y + c
        peers = [(x ^ ((k + 1) >> 2 & 1), y ^ ((k + 1) >> 1 & 1), c ^ ((k + 1) & 1)) for k in range(N_DEV - 1)]
        barrier = pltpu.get_barrier_semaphore()
        for p in peers:
            pl.semaphore_signal(barrier, inc=1, device_id=p, device_id_type=pl.DeviceIdType.MESH)
        pl.semaphore_wait(barrier, N_DEV - 1)

        def src_for(a, dest):
            return src_refs[a].at[dest] if plan[a][1] else src_refs[a]

        def slot(a, source):
            return land_refs[plan[a][0]].at[source]

        mine = [pltpu.make_async_copy(src_for(a, my), slot(a, my), local_sems.at[a]) for a in range(n)]
        for cp in mine:
            cp.start()
        sends, recvs = [], []
        for k, (px, py, pc) in enumerate(peers):
            peer = 4 * px + 2 * py + pc
            for a in range(n):
                kw = dict(send_sem=send_sems.at[a * (N_DEV - 1) + k], recv_sem=recv_sems.at[a * (N_DEV - 1) + k],
                          device_id=(px, py, pc), device_id_type=pl.DeviceIdType.MESH)
                sends.append(pltpu.make_async_remote_copy(src_ref=src_for(a, peer), dst_ref=slot(a, my), **kw))
                recvs.append(pltpu.make_async_remote_copy(src_ref=src_for(a, my), dst_ref=slot(a, peer), **kw))
        for cp in sends:
            cp.start()
        for cp in recvs:
            cp.wait_recv()
        for cp in sends:
            cp.wait_send()
        for cp in mine:
            cp.wait()

    lands = pl.kernel(
        launch, out_type=[_SDS(s, d) for s, d in land_shapes],
        mesh=plsc.ScalarSubcoreMesh(axis_name="sequencer", num_cores=1), name=name,
        scratch_types=(pltpu.SemaphoreType.DMA((n * (N_DEV - 1),)), pltpu.SemaphoreType.DMA((n * (N_DEV - 1),)),
                       pltpu.SemaphoreType.DMA((n,))),
        compiler_params=pltpu.CompilerParams(collective_id=cid))(*srcs)
    return _Exchange(list(lands), list(srcs))


def _adam_update(g, w, m, v):
    c1 = 1.0 - ADAM_B1 ** ADAM_STEP
    c2 = 1.0 - ADAM_B2 ** ADAM_STEP
    mm = ADAM_B1 * m + (1.0 - ADAM_B1) * g
    vv = ADAM_B2 * v + (1.0 - ADAM_B2) * (g * g)
    delta = -ADAM_LR * ((mm / c1) / (jnp.sqrt(vv / c2) + ADAM_EPS) + ADAM_WD * w)
    return delta, mm, vv


def _sum_sources(p_ref):
    g = p_ref[0].astype(F32)
    for s in range(1, N_DEV):
        g = g + p_ref[s].astype(F32)
    return g


def _adamw(parts, w, m, v, tr, name, restore_b=False, deps=()):
    nl, r, c = w.shape
    cp = parts[0].shape[-1]

    def body(*refs):
        p_refs = refs[:nl]
        w_ref, m_ref, v_ref = refs[nl:nl + 3]
        g_ref, d_ref, nm_ref, nv_ref = refs[-4:]
        g = _sum_sources(p_refs[0])
        for l in range(1, nl):
            g = jnp.where(pl.program_id(0) == l, _sum_sources(p_refs[l]), g)
        if restore_b:
            g = jnp.concatenate([g[:, :BP_XQ], g[:, BP_GATE:BP_GATE + 2 * B_V_HEADS], g[:, BP_XQ:BP_GATE]], axis=1)
        delta, mm, vv = _adam_update(g, w_ref[...], m_ref[...], v_ref[...])
        g_ref[...] = g
        d_ref[...] = delta
        nm_ref[...] = mm
        nv_ref[...] = vv

    spec = pl.BlockSpec((None, tr, c), lambda l, i: (l, i, 0))
    part_specs = [pl.BlockSpec((N_DEV, tr, cp), functools.partial(lambda l, i, k: (0, jnp.where(l == k, i, 0), 0), k=k))
                  for k in range(nl)]
    return pl.pallas_call(
        body, grid=(nl, r // tr),
        in_specs=part_specs + [spec, spec, spec] + _dep_specs(deps),
        out_specs=[spec] * 4, out_shape=[_SDS(w.shape, F32)] * 4,
        name=name, compiler_params=_cp("arbitrary", "arbitrary"))(*parts, w, m, v, *deps)


def _pack_small(d_rel, d_cb, d_cw, d_qkv, d_mix, d_mem, d_ffn, d_final, d_sinks, d_par, d_ng, loss_row, name):
    flat = [d_rel, *d_cb, *d_cw, d_qkv, *d_mix, *d_mem, *d_ffn, d_final, d_sinks, d_par, d_ng, loss_row]
    n = len(flat)

    def body(*refs):
        ins, o_ref = refs[:n], refs[n]
        rel, cb0, cb1, cw0, cw1, qkv, mx0, mx1, me0, me1, ff0, ff1, fin, snk, par, ng, lss = ins
        o_ref[...] = jnp.zeros_like(o_ref)
        for k in range(N_BUCKETS):
            lane = SP_REL_LANE + 128 * (k % 8)
            o_ref[SP_QKV + k // 8:SP_QKV + k // 8 + 1, lane:lane + 128] = rel[k:k + 1, :]
        for l, (cb, cw) in enumerate(((cb0, cw0), (cb1, cw1))):
            o_ref[SP_CB + l:SP_CB + l + 1, :] = jnp.concatenate([cb[j] for j in range(FF_BLOCKS)], axis=1)
            full = jnp.concatenate([cw[j] for j in range(FF_BLOCKS)], axis=1)
            o_ref[SP_CW + FFN_CONV * l:SP_CW + FFN_CONV * (l + 1), :] = full[:FFN_CONV]
        o_ref[SP_QKV:SP_QKV + B_CONV, 0:B_QKV] = qkv[0:B_CONV, :]
        for base, pair in ((SP_MIX, (mx0, mx1)), (SP_MEM, (me0, me1)), (SP_FFN, (ff0, ff1))):
            for l in range(2):
                o_ref[base + l:base + l + 1, 0:D] = pair[l][...]
        o_ref[SP_FINAL:SP_FINAL + 1, 0:D] = fin[...]
        o_ref[SP_MISC:SP_MISC + 1, 0:128] = snk[...]
        o_ref[SP_MISC:SP_MISC + 1, 128:256] = par[...]
        o_ref[SP_MISC:SP_MISC + 1, 256:384] = ng[...]
        o_ref[SP_MISC:SP_MISC + 1, 384:512] = lss[...]

    vm = pl.BlockSpec(memory_space=pltpu.VMEM)
    return pl.pallas_call(body, in_specs=[vm] * n, out_specs=vm, out_shape=_SDS((SMALL_ROWS, D_FF), F32), name=name)(*flat)


_SMALL = ["rel_bias", "norm_mix_g", "norm_mem_g", "sinks_a", "a_log_b", "dt_bias_b", "out_norm_g_b", "norm_ffn_g",
          "ffn_conv_b", "final_norm_g", "conv_qkv_b", "ffn_conv_w"]


def _adamw_small(recv, rc_qkv, rc_ffn, ws, ms, vs, name, deps=()):
    n = len(_SMALL)

    def body(*refs):
        recv_ref, qkv_ref, ffn_ref = refs[:3]
        w_refs, m_refs, v_refs = refs[3:3 + n], refs[3 + n:3 + 2 * n], refs[3 + 2 * n:3 + 3 * n]
        outs, loss_ref = refs[len(refs) - 4 * n - 1:len(refs) - 1], refs[-1]
        gs = _sum_sources(recv_ref)
        loss_ref[...] = gs[SP_MISC:SP_MISC + 1, 384:512]
        grads = {
            "rel_bias": jnp.concatenate(
                [gs[SP_QKV + k // 8:SP_QKV + k // 8 + 1, SP_REL_LANE + 128 * (k % 8):SP_REL_LANE + 128 * (k % 8) + A_HEADS]
                 for k in range(N_BUCKETS)], axis=0),
            "norm_mix_g": gs[SP_MIX:SP_MIX + 2, 0:D], "norm_mem_g": gs[SP_MEM:SP_MEM + 2, 0:D],
            "sinks_a": gs[SP_MISC:SP_MISC + 1, 0:A_HEADS],
            "a_log_b": gs[SP_MISC:SP_MISC + 1, 128:128 + B_V_HEADS],
            "dt_bias_b": gs[SP_MISC:SP_MISC + 1, 128 + B_V_HEADS:128 + 2 * B_V_HEADS],
            "out_norm_g_b": gs[SP_MISC:SP_MISC + 1, 256:256 + B_HD],
            "norm_ffn_g": gs[SP_FFN:SP_FFN + 2, 0:D], "ffn_conv_b": gs[SP_CB:SP_CB + 2, :],
            "final_norm_g": gs[SP_FINAL:SP_FINAL + 1, 0:D],
            "conv_qkv_b": _sum_sources(qkv_ref), "ffn_conv_w": _sum_sources(ffn_ref),
        }
        for i, nm in enumerate(_SMALL):
            g = grads[nm]
            delta, mm, vv = _adam_update(g, w_refs[i][...], m_refs[i][...], v_refs[i][...])
            outs[i][...] = g
            outs[n + i][...] = delta
            outs[2 * n + i][...] = mm
            outs[3 * n + i][...] = vv

    vm = pl.BlockSpec(memory_space=pltpu.VMEM)
    shapes = [_SDS(w.shape, F32) for w in ws]
    return pl.pallas_call(
        body, in_specs=[vm] * (3 + 3 * n) + _dep_specs(deps), out_specs=[vm] * (4 * n + 1),
        out_shape=shapes * 4 + [_SDS((1, 128), F32)],
        name=name)(recv, rc_qkv, rc_ffn, *ws, *ms, *vs, *deps)


def _assemble(gathered, axis):
    g = jnp.moveaxis(gathered, 0, axis)
    shp = list(g.shape)
    return g.reshape(shp[:axis] + [shp[axis] * shp[axis + 1]] + shp[axis + 2:])


def _pad_rows(a, rows):
    return jnp.pad(a, ((0, rows - a.shape[0]), (0, 0)))


def _pad_lanes(a, lanes=128):
    return jnp.pad(a, ((0, 0), (0, lanes - a.shape[1])))


def _ff_blocks(a):
    return jnp.moveaxis(a.reshape(a.shape[0], FF_BLOCKS, GU_SHARD), 1, 0)


def _reorder_b(w):
    qkv_z = w[..., :B_QKV + B_V]
    gates = w[..., B_QKV + B_V:B_QKV + B_V + 2 * B_V_HEADS]
    xq = w[..., IN_B - X_Q:]
    pad = jnp.zeros(w.shape[:-1] + (IN_BP - IN_B,), w.dtype)
    return jnp.concatenate([qkv_z, xq, gates, pad], axis=-1)


def kernel(x, mem, rel_bias, norm_mix_g, norm_mem_g, w_mem_kv, w_out, w_in_a, sinks_a, w_in_b, conv_qkv_b, a_log_b, dt_bias_b, out_norm_g_b, norm_ffn_g, w_gate_up, ffn_conv_w, ffn_conv_b, w_down, final_norm_g, loss_target, m_rel_bias, m_norm_mix_g, m_norm_mem_g, m_w_mem_kv, m_w_out, m_w_in_a, m_sinks_a, m_w_in_b, m_conv_qkv_b, m_a_log_b, m_dt_bias_b, m_out_norm_g_b, m_norm_ffn_g, m_w_gate_up, m_ffn_conv_w, m_ffn_conv_b, m_w_down, m_final_norm_g, v_rel_bias, v_norm_mix_g, v_norm_mem_g, v_w_mem_kv, v_w_out, v_w_in_a, v_sinks_a, v_w_in_b, v_conv_qkv_b, v_a_log_b, v_dt_bias_b, v_out_norm_g_b, v_norm_ffn_g, v_w_gate_up, v_ffn_conv_w, v_ffn_conv_b, v_w_down, v_final_norm_g):
    local = dict(locals())
    order = ["rel_bias", "norm_mix_g", "norm_mem_g", "w_mem_kv", "w_out", "w_in_a", "sinks_a", "w_in_b", "conv_qkv_b",
             "a_log_b", "dt_bias_b", "out_norm_g_b", "norm_ffn_g", "w_gate_up", "ffn_conv_w", "ffn_conv_b", "w_down",
             "final_norm_g"]
    wts = {n: local[n] for n in order}
    moms = {n: local["m_" + n] for n in order}
    vars_ = {n: local["v_" + n] for n in order}
    h0 = x[0]
    memx = mem[0]
    tgt = loss_target[0]
    s = h0.shape[0]
    tm = _rows(s)
    tb = min(s, _TM_BIG)

    t_ = lambda a: jnp.swapaxes(a, 1, 2)
    g_mk0, g_out0, g_ia, g_cq, g_cw = _all_gather(
        [w_mem_kv[0:1].astype(_MXU), w_out[0:1].astype(_MXU), t_(w_in_a).astype(_MXU), conv_qkv_b, ffn_conv_w], "gather_first")
    g_mk, g_out = [g_mk0], [g_out0]
    gu_land = ((N_DEV, GU_SHARD, D), _MXU)
    dn_land = ((N_DEV, DN_SHARD, D), _MXU)
    whole = [(0, False), (1, False)]
    def after(a, b):
        return a + (b[(0,) * b.ndim] * 0).astype(a.dtype)

    ffn0_w = _seq_exchange([after(t_(w_gate_up)[0].astype(_MXU), g_ia), after(w_down[0].astype(_MXU), g_ia)], [gu_land, dn_land],
                           whole, "gather_ffn0", 1)
    w_ia = g_ia.reshape(IN_A, D)
    conv_qkv = _pad_rows(_assemble(g_cq, 2)[0], HALO)
    ffn_cw_full = _assemble(g_cw, 2)
    ffn_cw = [_ff_blocks(_pad_rows(ffn_cw_full[i], HALO)) for i in range(2)]
    ffn_cb = [_ff_blocks(ffn_conv_b[i:i + 1]) for i in range(2)]
    bucket = jnp.asarray(_bucket_table())
    bias = _bias_build(rel_bias, bucket, "bias_build")
    sinks = _pad_lanes(sinks_a)
    par_b = _pad_lanes(jnp.concatenate([a_log_b, dt_bias_b], axis=1))

    row_x = pl.BlockSpec((tm, D), lambda i, j: (i, 0))
    gu_shape = (2, FF_BLOCKS, s, GU_SHARD)

    def in_proj(h, g, w, w_spec, n_cols, tn, name, deps=(), out_dtype=F32, w_t=False, tm=None):
        return _norm_matmul(h, g, w, w_spec, n_cols // tn, (h.shape[0], n_cols),
                            pl.BlockSpec((tm or _rows(h.shape[0]), tn), lambda i, j: (i, j)), name, deps=deps, out_dtype=out_dtype,
                            w_t=w_t, tm=tm)

    def ffn_fwd(i, h, g_gu, g_dn, deps=()):
        gu, hn = _norm_matmul(h, norm_ffn_g[i:i + 1], g_gu, _spec_gate_up(1), N_DEV, gu_shape,
                              _spec_gu_act(0, 1, tb), f"gate_up_{i}", deps=deps, out_dtype=_ACT, w_t=True, tm=tb)
        h_new, act = _glu_down(gu, ffn_cw[i], ffn_cb[i], g_dn, h, f"glu_down_{i}")
        return h_new, gu, hn, act

    def out_proj(i, mix, h):
        return _matmul_res(mix, row_x, g_out[i], _spec_rowsharded(0, D // N_DEV, D), 1, h, f"out_proj_{i}")

    proj_a, hn_a = in_proj(h0, norm_mix_g[0:1], w_ia, pl.BlockSpec((640, D), lambda i, j: (j, 0)), IN_A, 640, "in_proj_a",
                           deps=ffn0_w.srcs, out_dtype=_ACT, w_t=True)
    memkv0, memn0 = in_proj(memx, norm_mem_g[0:1], g_mk[0], _spec_rowsharded(0, D // N_DEV, 2 * X_Q), 2 * X_Q, 2 * X_Q, "mem_proj_0")
    mix_a = _mix_a_fwd(proj_a, bias, sinks, memkv0, "mix_a_fwd")
    h1 = out_proj(0, mix_a, h0)
    g_gu0, g_dn0 = ffn0_w.lands
    in_b_w = _seq_exchange([after(_reorder_b(w_in_b).astype(_MXU), h1), after(w_mem_kv[1:2].astype(_MXU), h1),
                            after(w_out[1:2].astype(_MXU), h1)],
                           [((N_DEV, 1, D // N_DEV, IN_BP), _MXU), ((N_DEV, 1, D // N_DEV, 2 * X_Q), _MXU),
                            ((N_DEV, 1, D // N_DEV, D), _MXU)], [(0, False), (1, False), (2, False)], "gather_in_b", 2)
    ffn1_w = _seq_exchange([after(t_(w_gate_up)[1].astype(_MXU), h1), after(w_down[1].astype(_MXU), h1)], [gu_land, dn_land], whole,
                           "gather_ffn1", 3)
    h2, gu0, hn_f0, act0 = ffn_fwd(0, h1, g_gu0, g_dn0, deps=in_b_w.srcs + ffn1_w.srcs)
    g_ib, g_mk1, g_out1 = in_b_w.lands
    g_mk.append(g_mk1)
    g_out.append(g_out1)
    proj_b, hn_b = in_proj(h2, norm_mix_g[1:2], g_ib, _spec_rowsharded(0, D // N_DEV, 896, col_block=1), IN_BP, 896, "in_proj_b")
    memkv1, memn1 = in_proj(memx, norm_mem_g[1:2], g_mk[1], _spec_rowsharded(0, D // N_DEV, 2 * X_Q), 2 * X_Q, 2 * X_Q, "mem_proj_1",
                            deps=[h2])
    mix_b, states, deltas = _mix_b_fwd(proj_b, conv_qkv, par_b, out_norm_g_b, memkv1, "mix_b_fwd")
    h3 = out_proj(1, mix_b, h2)
    g_gu1, g_dn1 = ffn1_w.lands
    h4, gu1, hn_f1, act1 = ffn_fwd(1, h3, g_gu1, g_dn1)
    loss_row, dh, d_final_g = _loss_head(h4, final_norm_g[None, :], tgt, "loss_head")

    zeros_mem = jnp.zeros_like(memx)
    per_dest2 = [(0, True), (1, True)]

    def ffn_bwd(i, dh, h_in, gu, hn_f, act, g_gu, g_dn, deps=()):
        dgu, d_cw, d_cb = _glu_bwd(gu, ffn_cw[i], ffn_cb[i], dh, g_dn, f"glu_bwd_{i}", deps=deps)
        d_wdown = _matmul_tn(act, pl.BlockSpec((None, tm, GU_SHARD), lambda j, r: (j, r, 0)),
                             dh, pl.BlockSpec((tm, D), lambda j, r: (r, 0)), s, FF_BLOCKS, (GU_SHARD, D),
                             (N_DEV, DN_SHARD, D), pl.BlockSpec((2, DN_SHARD, D), lambda j, r: (j, 0, 0)), f"d_w_down_{i}")
        dh_new, d_g = _matmul_nt_normbwd(dgu, _spec_gu_act(0, 1, tm), g_gu, _spec_gate_up(1), N_DEV, h_in,
                                         norm_ffn_g[i:i + 1], dh, f"d_ffn_in_{i}", w_t=True)
        d_wgu = _matmul_tn(dgu, _spec_gu_act(1, 0, tb), hn_f, pl.BlockSpec((tb, D), lambda j, r: (r, 0)), s, N_DEV,
                           (GU_SHARD, D), (N_DEV, GU_SHARD, D), pl.BlockSpec((None, GU_SHARD, D), lambda j, r: (j, 0, 0)),
                           f"d_w_gate_up_{i}", tm=tb)
        return dh_new, [d_wdown, d_wgu], d_cw, d_cb, d_g

    def out_bwd(i, dh, mix, deps):
        dmix = _matmul_nt(dh, g_out[i], _spec_rowsharded(0, D // N_DEV, D), 1, (s, D), row_x, f"d_mix_{i}", deps=deps, out_dtype=_ACT)
        d_wout = _matmul_tn(mix, pl.BlockSpec((tm, D), lambda j, r: (r, 0)), dh, pl.BlockSpec((tm, D), lambda j, r: (r, 0)),
                            s, 1, (D, D), (N_DEV, D // N_DEV, D), pl.BlockSpec((N_DEV, D // N_DEV, D), lambda j, r: (0, 0, 0)),
                            f"d_w_out_{i}")
        return dmix, d_wout

    def mem_bwd(i, dmemkv, memn):
        tmm = _rows(MEM_LEN)
        _, d_g = _matmul_nt_normbwd(dmemkv, pl.BlockSpec((tmm, 2 * X_Q), lambda r, j: (r, 0)), g_mk[i],
                                    _spec_rowsharded(0, D // N_DEV, 2 * X_Q), 1, memx, norm_mem_g[i:i + 1], zeros_mem,
                                    f"d_mem_in_{i}")
        by_row = lambda j, r: (r, 0)
        d_w = _matmul_tn(memn, pl.BlockSpec((tmm, D), by_row), dmemkv, pl.BlockSpec((tmm, 2 * X_Q), by_row), MEM_LEN, 1,
                         (D, 2 * X_Q), (N_DEV, D // N_DEV, 2 * X_Q),
                         pl.BlockSpec((N_DEV, D // N_DEV, 2 * X_Q), lambda j, r: (0, 0, 0)), f"d_w_mem_kv_{i}")
        return d_w, d_g

    out_land = ((N_DEV, D // N_DEV, D), _WIRE)
    mk_land = ((N_DEV, D // N_DEV, 2 * X_Q), _WIRE)
    ffn_lands = [((N_DEV, DN_SHARD, D), _WIRE), ((N_DEV, GU_SHARD, D), _WIRE)]
    dh, d_ffn1, d_cw1, d_cb1, d_gf1 = ffn_bwd(1, dh, h3, gu1, hn_f1, act1, g_gu1, g_dn1)
    ffn1_g = _seq_exchange(d_ffn1, ffn_lands, per_dest2, "send_ffn1_grads", 5)
    dmix, d_wout1 = out_bwd(1, dh, mix_b, ffn1_g.srcs)
    dproj_b, d_convw, d_par, d_ng, dmemkv1 = _mix_b_bwd(proj_b, conv_qkv, par_b, out_norm_g_b, memkv1, states, deltas, dmix, "mix_b_bwd")
    dh, d_gm1 = _matmul_nt_normbwd(dproj_b, pl.BlockSpec((tm, 896), lambda i, j: (i, j)), g_ib,
                                   _spec_rowsharded(0, D // N_DEV, 896, col_block=1), IN_BP // 896, h2, norm_mix_g[1:2], dh, "d_in_b")
    d_wib = _matmul_tn(hn_b, pl.BlockSpec((tb, D), lambda j, r: (r, 0)), dproj_b, pl.BlockSpec((tb, 896), lambda j, r: (r, j)),
                       s, IN_BP // 896, (D, 896), (N_DEV, D // N_DEV, IN_BP),
                       pl.BlockSpec((N_DEV, D // N_DEV, 896), lambda j, r: (0, 0, j)), "d_w_in_b", tm=tb)
    d_wmk1, d_gmem1 = mem_bwd(1, dmemkv1, memn1)
    mix1_g = _seq_exchange([d_wout1, d_wib, d_wmk1], [out_land, ((N_DEV, D // N_DEV, IN_BP), _WIRE), mk_land],
                           [(0, True), (1, True), (2, True)], "send_mix1_grads", 6)
    dh, d_ffn0, d_cw0, d_cb0, d_gf0 = ffn_bwd(0, dh, h1, gu0, hn_f0, act0, g_gu0, g_dn0, deps=mix1_g.srcs)
    dmix, d_wout0 = out_bwd(0, dh, mix_a, d_ffn0 + ffn1_g.lands[:1])
    ffn0_g = _seq_exchange(d_ffn0 + [d_wout0], ffn_lands + [out_land], per_dest2 + [(2, True)], "send_ffn0_grads", 4)
    dproj_a, dbias, dsinks, dmemkv0 = _mix_a_bwd(proj_a, bias, sinks, memkv0, dmix, "mix_a_bwd", deps=ffn0_g.srcs)
    dh, d_gm0 = _matmul_nt_normbwd(dproj_a, pl.BlockSpec((tm, 640), lambda i, j: (i, j)), w_ia,
                                   pl.BlockSpec((640, D), lambda i, j: (j, 0)), IN_A // 640, h0, norm_mix_g[0:1], dh, "d_in_a",
                                   w_t=True)
    d_wia = _matmul_tn(dproj_a, pl.BlockSpec((tm, IN_A), lambda j, r: (r, 0)), hn_a, pl.BlockSpec((tm, D), lambda j, r: (r, 0)),
                       s, 1, (IN_A, D), (N_DEV, IA_SHARD, D), pl.BlockSpec((N_DEV, IA_SHARD, D), lambda j, r: (0, 0, 0)),
                       "d_w_in_a")
    d_wmk0, d_gmem0 = mem_bwd(0, dmemkv0, memn0)
    d_rel = _bias_reduce(dbias, bucket, "bias_reduce")
    small = _pack_small(d_rel, (d_cb0, d_cb1), (d_cw0, d_cw1), d_convw, (d_gm0, d_gm1), (d_gmem0, d_gmem1),
                        (d_gf0, d_gf1), d_final_g, dsinks, d_par, d_ng, loss_row, "pack_small")
    mix0_g = _seq_exchange([d_wia, d_wmk0, small],
                           [((N_DEV, IA_SHARD, D), _WIRE), mk_land, ((N_DEV, SMALL_ROWS, D_FF), F32)],
                           [(0, True), (1, True), (2, False)], "send_mix0_grads", 7)

    res = {}
    last = []

    def update(nm, parts, tr, restore=False, transposed=False):
        view = t_ if transposed else (lambda a: a)
        out = _adamw(parts, view(wts[nm]), view(moms[nm]), view(vars_[nm]), tr, "adamw_" + nm, restore_b=restore, deps=last[-1:])
        res[nm] = [view(o) for o in out]
        last.append(out[1])

    r_dn1, r_gu1 = ffn1_g.lands
    r_dn0, r_gu0, r_out0 = ffn0_g.lands
    r_out1, r_ib, r_mk1 = mix1_g.lands
    update("w_in_b", [r_ib], 32, True)
    update("w_gate_up", [r_gu0, r_gu1], 176, transposed=True)
    update("w_down", [r_dn0, r_dn1], 176)
    r_ia, r_mk0, r_small = mix0_g.lands
    update("w_mem_kv", [r_mk0, r_mk1], 128)
    update("w_out", [r_out0, r_out1], 128)
    update("w_in_a", [r_ia], IA_SHARD, transposed=True)

    my = 4 * lax.axis_index("x") + 2 * lax.axis_index("y") + lax.axis_index("c")
    cq = conv_qkv_b.shape[-1]
    cf = ffn_conv_w.shape[-1]
    rc_qkv = lax.dynamic_slice_in_dim(r_small[:, SP_QKV:SP_QKV + B_CONV, :B_QKV], my * cq, cq, axis=2)[:, None]
    rc_ffn = lax.dynamic_slice_in_dim(r_small[:, SP_CW:SP_CW + 2 * FFN_CONV, :], my * cf, cf, axis=2).reshape(N_DEV, 2, FFN_CONV, cf)
    as2d = lambda a: a[None, :] if a.ndim == 1 else a
    small_out = _adamw_small(r_small, rc_qkv, rc_ffn, [as2d(wts[n]) for n in _SMALL], [as2d(moms[n]) for n in _SMALL],
                             [as2d(vars_[n]) for n in _SMALL], "adamw_small", deps=last[-1:])
    ns = len(_SMALL)
    for i, nm in enumerate(_SMALL):
        res[nm] = [small_out[k * ns + i].reshape(wts[nm].shape) for k in range(4)]

    return (small_out[-1][0, 0], dh[None], *[res[n][0] for n in order], *[res[n][1] for n in order],
            *[res[n][2] for n in order], *[res[n][3] for n in order])
```

```python
import functools
import math

import numpy as np

import jax
import jax.numpy as jnp
from jax import lax
from jax.experimental import pallas as pl
from jax.experimental.pallas import tpu as pltpu
from jax.experimental.pallas import tpu_sc as plsc

F32 = jnp.float32
_MXU = jnp.bfloat16
_ACT = jnp.bfloat16
_WIRE = jnp.bfloat16
_HI = lax.Precision.HIGH
_TM = 1024
_TM_GLU = 512
_TM_BIG = 2048
_VMEM_LIMIT = 48 * 1024 * 1024
_SDS = jax.ShapeDtypeStruct

D = 1024
EPS = 1e-6
A_HEADS, A_KV_HEADS, A_HD, BLK = 12, 2, 64, 128
N_BUCKETS, MAX_DISTANCE = 32, 128
B_QK_HEADS, B_V_HEADS, B_HD, B_CONV, CHUNK = 3, 6, 128, 4, 64
X_HEADS, X_HD, MEM_LEN = 4, 64, 256
D_FF, FFN_CONV = 2816, 3
A_Q, A_KV, X_Q = 768, 128, 256
B_QK, B_V, B_QKV = 384, 768, 1536
IN_A, IN_B = 1280, 2572
IN_BP = 2688
BP_Z, BP_XQ, BP_GATE = 1536, 2304, 2560
HALO = 8
GLU_HALO = 16

N_DEV = 8
GU_SHARD = 2 * D_FF // N_DEV
FF_BLOCKS = D_FF // GU_SHARD
DN_SHARD = D_FF // N_DEV
IA_SHARD = IN_A // N_DEV

ADAM_LR, ADAM_B1, ADAM_B2, ADAM_EPS, ADAM_WD, ADAM_STEP = 0.001, 0.9, 0.999, 1e-08, 0.01, 10

SP_CB, SP_CW, SP_QKV, SP_MIX, SP_MEM, SP_FFN, SP_FINAL, SP_MISC, SMALL_ROWS = 0, 2, 8, 12, 14, 16, 18, 19, 24
SP_REL_LANE = B_QKV


def _cp(*sems):
    return pltpu.CompilerParams(dimension_semantics=sems, vmem_limit_bytes=_VMEM_LIMIT)


def _mm(a, b):
    return jnp.dot(a.astype(_MXU), b.astype(_MXU), preferred_element_type=F32)


def _mm_nt(a, b):
    return lax.dot_general(a.astype(_MXU), b.astype(_MXU), (((1,), (1,)), ((), ())), preferred_element_type=F32)


def _mm_tn(a, b):
    return lax.dot_general(a.astype(_MXU), b.astype(_MXU), (((0,), (0,)), ((), ())), preferred_element_type=F32)


def _mmf(a, b):
    return jnp.dot(a, b, preferred_element_type=F32, precision=_HI)


def _mmf_nt(a, b):
    return lax.dot_general(a, b, (((1,), (1,)), ((), ())), preferred_element_type=F32, precision=_HI)


def _silu(x):
    return x * jax.nn.sigmoid(x)


def _w2d(ref):
    v = ref[...]
    return v.reshape(-1, v.shape[-1])


def _rows(m):
    return min(m, _TM)


def _spec_rowsharded(layer, rows, cols, col_block=None):
    if col_block is None:
        return pl.BlockSpec((N_DEV, None, rows, cols), lambda *_: (0, layer, 0, 0))
    return pl.BlockSpec((N_DEV, None, rows, cols), lambda *ids: (0, layer, 0, ids[col_block]))


def _spec_gate_up(axis):
    return pl.BlockSpec((None, GU_SHARD, D), lambda *ids: (ids[axis], 0, 0))


def _spec_down(axis):
    return pl.BlockSpec((2, DN_SHARD, D), lambda *ids: (ids[axis], 0, 0))


def _dep_specs(deps):
    return [pl.BlockSpec(memory_space=pl.ANY) for d in deps]


def _spec_gu_act(row_axis, axis, tm):
    return pl.BlockSpec((None, None, tm, GU_SHARD), lambda *ids: (ids[axis] // FF_BLOCKS, ids[axis] % FF_BLOCKS, ids[row_axis], 0))


def _norm_matmul(x, g, w, w_spec, n_blocks, out_shape, out_spec, name, deps=(), out_dtype=F32, w_t=False, tm=None):
    m, k = x.shape
    tm = tm or _rows(m)

    def body(x_ref, g_ref, w_ref, *rest):
        y_ref, hn_ref = rest[-2:]

        @pl.when(pl.program_id(1) == 0)
        def _():
            xv = x_ref[...]
            r = lax.rsqrt(jnp.mean(xv * xv, axis=-1, keepdims=True) + EPS)
            hn_ref[...] = (xv * r * g_ref[...]).astype(hn_ref.dtype)

        y_ref[...] = (_mm_nt if w_t else _mm)(hn_ref[...], _w2d(w_ref)).astype(y_ref.dtype)

    return pl.pallas_call(
        body, grid=(m // tm, n_blocks),
        in_specs=[pl.BlockSpec((tm, k), lambda i, j: (i, 0)), pl.BlockSpec((1, k), lambda i, j: (0, 0)), w_spec]
        + _dep_specs(deps),
        out_specs=[out_spec, pl.BlockSpec((tm, k), lambda i, j: (i, 0))],
        out_shape=[_SDS(out_shape, out_dtype), _SDS((m, k), _ACT)],
        name=name, compiler_params=_cp("arbitrary", "arbitrary"))(x, g, w, *deps)


def _matmul_res(a, a_spec, w, w_spec, n_k, res, name):
    m, n = res.shape
    tm = _rows(m)

    def body(a_ref, w_ref, r_ref, o_ref):
        part = _mm(a_ref[...], _w2d(w_ref))

        @pl.when(pl.program_id(1) == 0)
        def _():
            o_ref[...] = r_ref[...] + part

        @pl.when(pl.program_id(1) > 0)
        def _():
            o_ref[...] += part

    return pl.pallas_call(
        body, grid=(m // tm, n_k),
        in_specs=[a_spec, w_spec, pl.BlockSpec((tm, n), lambda i, j: (i, 0))],
        out_specs=pl.BlockSpec((tm, n), lambda i, j: (i, 0)),
        out_shape=_SDS((m, n), F32), name=name, compiler_params=_cp("arbitrary", "arbitrary"))(a, w, res)


def _matmul_nt(dy, w, w_spec, n_blocks, out_shape, out_spec, name, deps=(), out_dtype=F32):
    m, n = dy.shape
    tm = _rows(m)

    def body(dy_ref, w_ref, *rest):
        o_ref = rest[-1]
        o_ref[...] = _mm_nt(dy_ref[...], _w2d(w_ref)).astype(o_ref.dtype)

    return pl.pallas_call(
        body, grid=(m // tm, n_blocks),
        in_specs=[pl.BlockSpec((tm, n), lambda i, j: (i, 0)), w_spec] + _dep_specs(deps),
        out_specs=out_spec, out_shape=_SDS(out_shape, out_dtype),
        name=name, compiler_params=_cp("arbitrary", "arbitrary"))(dy, w, *deps)


def _matmul_nt_normbwd(dy, dy_spec, w, w_spec, nj, h, g, dh_in, name, w_t=False):
    m, k = h.shape
    tm = _rows(m)

    def body(dy_ref, w_ref, h_ref, g_ref, dhin_ref, dh_ref, dg_ref, acc_ref):
        i, j = pl.program_id(0), pl.program_id(1)

        @pl.when(j == 0)
        def _():
            acc_ref[...] = jnp.zeros_like(acc_ref)

        acc_ref[...] += (_mm if w_t else _mm_nt)(dy_ref[...], _w2d(w_ref))

        @pl.when(j == nj - 1)
        def _():
            xv = h_ref[...]
            r = lax.rsqrt(jnp.mean(xv * xv, axis=-1, keepdims=True) + EPS)
            xh = xv * r
            dhn = acc_ref[...]
            part = jnp.sum(dhn * xh, axis=0, keepdims=True)

            @pl.when(i == 0)
            def _():
                dg_ref[...] = part

            @pl.when(i > 0)
            def _():
                dg_ref[...] += part

            t = dhn * g_ref[...]
            dh_ref[...] = dhin_ref[...] + r * (t - xh * jnp.mean(t * xh, axis=-1, keepdims=True))

    return pl.pallas_call(
        body, grid=(m // tm, nj),
        in_specs=[dy_spec, w_spec, pl.BlockSpec((tm, k), lambda i, j: (i, 0)), pl.BlockSpec((1, k), lambda i, j: (0, 0)),
                  pl.BlockSpec((tm, k), lambda i, j: (i, 0))],
        out_specs=[pl.BlockSpec((tm, k), lambda i, j: (i, 0)), pl.BlockSpec((1, k), lambda i, j: (0, 0))],
        out_shape=[_SDS((m, k), F32), _SDS((1, k), F32)],
        scratch_shapes=[pltpu.VMEM((tm, k), F32)],
        name=name, compiler_params=_cp("arbitrary", "arbitrary"))(dy, w, h, g, dh_in)


def _matmul_tn(x, x_spec, dy, dy_spec, m, n_blocks, acc_shape, out_shape, out_spec, name, tm=None):
    tm = tm or _rows(m)
    nm = m // tm

    def body(x_ref, dy_ref, o_ref, acc_ref):
        @pl.when(pl.program_id(1) == 0)
        def _():
            acc_ref[...] = jnp.zeros_like(acc_ref)

        acc_ref[...] += _mm_tn(x_ref[...], dy_ref[...])

        @pl.when(pl.program_id(1) == nm - 1)
        def _():
            o_ref[...] = acc_ref[...].reshape(o_ref.shape).astype(o_ref.dtype)

    return pl.pallas_call(
        body, grid=(n_blocks, nm), in_specs=[x_spec, dy_spec], out_specs=out_spec,
        out_shape=_SDS(out_shape, _WIRE), scratch_shapes=[pltpu.VMEM(acc_shape, F32)],
        name=name, compiler_params=_cp("arbitrary", "arbitrary"))(x, dy)


def _loss_head(h, g, tgt, name):
    m, k = h.shape
    tm = _rows(m)

    def body(h_ref, g_ref, t_ref, loss_ref, dh_ref, dg_ref):
        i = pl.program_id(0)
        xv = h_ref[...]
        r = lax.rsqrt(jnp.mean(xv * xv, axis=-1, keepdims=True) + EPS)
        xh = xv * r
        gv = g_ref[...]
        err = xh * gv - t_ref[...]
        lpart = jnp.zeros((1, 128), F32) + 0.5 * jnp.sum(jnp.mean(err * err, axis=-1, keepdims=True), axis=0, keepdims=True)
        dy = err * (1.0 / k)
        gpart = jnp.sum(dy * xh, axis=0, keepdims=True)

        @pl.when(i == 0)
        def _():
            loss_ref[...] = lpart
            dg_ref[...] = gpart

        @pl.when(i > 0)
        def _():
            loss_ref[...] += lpart
            dg_ref[...] += gpart

        t = dy * gv
        dh_ref[...] = r * (t - xh * jnp.mean(t * xh, axis=-1, keepdims=True))

    return pl.pallas_call(
        body, grid=(m // tm,),
        in_specs=[pl.BlockSpec((tm, k), lambda i: (i, 0)), pl.BlockSpec((1, k), lambda i: (0, 0)),
                  pl.BlockSpec((tm, k), lambda i: (i, 0))],
        out_specs=[pl.BlockSpec((1, 128), lambda i: (0, 0)), pl.BlockSpec((tm, k), lambda i: (i, 0)),
                   pl.BlockSpec((1, k), lambda i: (0, 0))],
        out_shape=[_SDS((1, 128), F32), _SDS((m, k), F32), _SDS((1, k), F32)],
        name=name, compiler_params=_cp("arbitrary"))(h, g, tgt)


def _glu_down(gu, conv_w, conv_b, w_down, res, name):
    s = gu.shape[2]
    tm = min(s, _TM_GLU)

    def body(gu_ref, prev_ref, w_ref, b_ref, wdn_ref, r_ref, o_ref, act_ref):
        i, j = pl.program_id(0), pl.program_id(1)
        prev = jnp.where(i > 0, prev_ref[...].astype(F32), 0.0)
        ext = jnp.concatenate([prev, gu_ref[0].astype(F32)], axis=0)
        gc = b_ref[...] + w_ref[FFN_CONV - 1:FFN_CONV, :] * ext
        for k in range(FFN_CONV - 1):
            gc = gc + w_ref[k:k + 1, :] * pltpu.roll(ext, FFN_CONV - 1 - k, 0)
        act = (_silu(gc[GLU_HALO:]) * gu_ref[1].astype(F32)).astype(act_ref.dtype)
        act_ref[...] = act
        part = _mm(act, _w2d(wdn_ref))

        @pl.when(j == 0)
        def _():
            o_ref[...] = r_ref[...] + part

        @pl.when(j > 0)
        def _():
            o_ref[...] += part

    return pl.pallas_call(
        body, grid=(s // tm, FF_BLOCKS),
        in_specs=[pl.BlockSpec((2, None, tm, GU_SHARD), lambda i, j: (0, j, i, 0)),
                  pl.BlockSpec((None, None, GLU_HALO, GU_SHARD),
                               lambda i, j: (0, j, jnp.maximum(i * (tm // GLU_HALO) - 1, 0), 0)),
                  pl.BlockSpec((None, HALO, GU_SHARD), lambda i, j: (j, 0, 0)),
                  pl.BlockSpec((None, 1, GU_SHARD), lambda i, j: (j, 0, 0)),
                  _spec_down(1), pl.BlockSpec((tm, D), lambda i, j: (i, 0))],
        out_specs=[pl.BlockSpec((tm, D), lambda i, j: (i, 0)), pl.BlockSpec((None, tm, GU_SHARD), lambda i, j: (j, i, 0))],
        out_shape=[_SDS((s, D), F32), _SDS((FF_BLOCKS, s, GU_SHARD), _ACT)], name=name,
        compiler_params=_cp("arbitrary", "arbitrary"))(gu, gu, conv_w, conv_b, w_down, res)


def _glu_bwd(gu, conv_w, conv_b, dh, w_down, name, deps=()):
    s = gu.shape[2]
    tm = min(s, _TM_GLU)
    nt = s // tm
    ext_rows = tm + GLU_HALO

    def body(gu_ref, prev_ref, w_ref, b_ref, dh_ref, wdn_ref, *rest):
        dgu_ref, dw_ref, db_ref, carry_ref = rest[-4:]
        t = pl.program_id(1)
        i = nt - 1 - t

        @pl.when(t == 0)
        def _():
            carry_ref[...] = jnp.zeros_like(carry_ref)
            dw_ref[...] = jnp.zeros_like(dw_ref)
            db_ref[...] = jnp.zeros_like(db_ref)

        up = gu_ref[1].astype(F32)
        prev = jnp.where(i > 0, prev_ref[...].astype(F32), 0.0)
        ext = jnp.concatenate([prev, gu_ref[0].astype(F32)], axis=0)
        shifted = [pltpu.roll(ext, FFN_CONV - 1 - j, 0) if j < FFN_CONV - 1 else ext for j in range(FFN_CONV)]
        gc = b_ref[...] + shifted[0] * w_ref[0:1, :]
        for j in range(1, FFN_CONV):
            gc = gc + shifted[j] * w_ref[j:j + 1, :]
        gc = gc[GLU_HALO:]
        sg = jax.nn.sigmoid(gc)
        da = _mm_nt(dh_ref[...], _w2d(wdn_ref))
        dup = da * (gc * sg)
        dgc = da * up * (sg * (1.0 + gc * (1.0 - sg)))
        db_ref[...] += jnp.sum(dgc, axis=0, keepdims=True)
        dgc_ext = jnp.concatenate([jnp.zeros((GLU_HALO, GU_SHARD), F32), dgc], axis=0)
        dext = dgc_ext * w_ref[FFN_CONV - 1:FFN_CONV, :]
        for j in range(FFN_CONV):
            dw_ref[j:j + 1, :] += jnp.sum(shifted[j] * dgc_ext, axis=0, keepdims=True)
            if j < FFN_CONV - 1:
                dext = dext + w_ref[j:j + 1, :] * pltpu.roll(dgc_ext, ext_rows - (FFN_CONV - 1 - j), 0)
        tail = jnp.concatenate([jnp.zeros((tm - GLU_HALO, GU_SHARD), F32), carry_ref[...]], axis=0)
        dgate = dext[GLU_HALO:] + tail
        carry_ref[...] = dext[:GLU_HALO]
        dgu_ref[0] = dgate.astype(dgu_ref.dtype)
        dgu_ref[1] = dup.astype(dgu_ref.dtype)

    return pl.pallas_call(
        body, grid=(FF_BLOCKS, nt),
        in_specs=[pl.BlockSpec((2, None, tm, GU_SHARD), lambda j, t: (0, j, nt - 1 - t, 0)),
                  pl.BlockSpec((None, None, GLU_HALO, GU_SHARD),
                               lambda j, t: (0, j, jnp.maximum((nt - 1 - t) * (tm // GLU_HALO) - 1, 0), 0)),
                  pl.BlockSpec((None, HALO, GU_SHARD), lambda j, t: (j, 0, 0)),
                  pl.BlockSpec((None, 1, GU_SHARD), lambda j, t: (j, 0, 0)),
                  pl.BlockSpec((tm, D), lambda j, t: (nt - 1 - t, 0)), _spec_down(0)] + _dep_specs(deps),
        out_specs=[pl.BlockSpec((2, None, tm, GU_SHARD), lambda j, t: (0, j, nt - 1 - t, 0)),
                   pl.BlockSpec((None, HALO, GU_SHARD), lambda j, t: (j, 0, 0)),
                   pl.BlockSpec((None, 1, GU_SHARD), lambda j, t: (j, 0, 0))],
        out_shape=[_SDS(gu.shape, _ACT), _SDS((FF_BLOCKS, HALO, GU_SHARD), F32), _SDS((FF_BLOCKS, 1, GU_SHARD), F32)],
        scratch_shapes=[pltpu.VMEM((GLU_HALO, GU_SHARD), F32)],
        name=name, compiler_params=_cp("arbitrary", "arbitrary"))(gu, gu, conv_w, conv_b, dh, w_down, *deps)


def _bucket_table():
    qi = np.arange(BLK)[:, None]
    kj = np.arange(BLK)[None, :]
    n = np.where(kj > qi, BLK + qi - kj, qi - kj)
    max_exact = N_BUCKETS // 2
    nf = np.maximum(n, 1).astype(np.float32)
    large = max_exact + (np.log(nf / max_exact) / math.log(MAX_DISTANCE / max_exact)
                         * (N_BUCKETS - max_exact)).astype(np.int32)
    large = np.minimum(large, N_BUCKETS - 1)
    return np.where(n < max_exact, n, large).astype(np.int32)


def _lane_low():
    return lax.broadcasted_iota(jnp.int32, (1, 128), 1) < A_HD


def _swa_groups(q, kd, vd, sink, bias, upper, first):
    n = A_HEADS // A_KV_HEADS
    ng = A_KV_HEADS
    low = _lane_low()
    qm = [jnp.concatenate([jnp.where(low == (h % 2 == 0), q[g][:, (h // 2) * 128:(h // 2 + 1) * 128], 0.0) for h in range(n)], axis=0)
          for g in range(ng)]
    s2 = [_mm_nt(qm[g], kd[g]) * (A_HD ** -0.5) for g in range(ng)]
    s = [jnp.where(upper[None], s2[g][:, :BLK].reshape(n, BLK, BLK), s2[g][:, BLK:].reshape(n, BLK, BLK)) + bias[g] for g in range(ng)]
    s = [jnp.where((upper & first)[None], -jnp.inf, t) for t in s]
    m = [lax.stop_gradient(jnp.maximum(jnp.max(s[g], axis=-1, keepdims=True), sink[g])) for g in range(ng)]
    p = [jnp.exp(s[g] - m[g]) for g in range(ng)]
    split = [jnp.concatenate([jnp.where(upper[None], t, 0.0), jnp.where(upper[None], 0.0, t)], axis=-1).reshape(n * BLK, 2 * BLK)
             for t in p]
    ones = jnp.ones((BLK, 128), F32)
    den = [_mm(p[g].reshape(n * BLK, BLK), ones) + jnp.exp(sink[g] - m[g]).reshape(n * BLK, 1) for g in range(ng)]
    o = [_mm(split[g], vd[g]) / den[g] for g in range(ng)]
    return [jnp.concatenate([jnp.where(low, t[2 * k * BLK:(2 * k + 1) * BLK], t[(2 * k + 1) * BLK:(2 * k + 2) * BLK])
                             for k in range(n // 2)], axis=1) for t in o]


def _mix_a_core(q, kd, vd, sink, bias, xq, mk, mv, upper, first):
    return _swa_groups(q, kd, vd, sink, bias, upper, first), _cross_pairs(xq, mk, mv)


def _swa_sinks(sink_ref, g):
    n = A_HEADS // A_KV_HEADS
    return jnp.concatenate([sink_ref[:, h:h + 1] for h in range(g * n, (g + 1) * n)], axis=0).reshape(n, 1, 1)


def _both_halves(t, t_rolled, g):
    low = _lane_low()
    return jnp.where(low, t, t_rolled) if g == 0 else jnp.where(low, t_rolled, t)


def _cross_pairs(q, mk, mv):
    rows = q.shape[0]
    low = _lane_low()
    qm = [jnp.concatenate([jnp.where(low, q[:, p * 128:(p + 1) * 128], 0.0), jnp.where(low, 0.0, q[:, p * 128:(p + 1) * 128])], axis=0)
          for p in range(X_HEADS // 2)]
    s = [_mm_nt(qm[p], mk[:, p * 128:(p + 1) * 128]) * (X_HD ** -0.5) for p in range(X_HEADS // 2)]
    e = [jnp.exp(t - lax.stop_gradient(jnp.max(t, axis=-1, keepdims=True))) for t in s]
    pr = [t / jnp.sum(t, axis=-1, keepdims=True) for t in e]
    o = [_mm(pr[p], mv[:, p * 128:(p + 1) * 128]) for p in range(X_HEADS // 2)]
    return jnp.concatenate([jnp.where(low, t[:rows], t[rows:]) for t in o], axis=1)


def _swa_upper():
    qi = lax.broadcasted_iota(jnp.int32, (BLK, BLK), 0)
    kj = lax.broadcasted_iota(jnp.int32, (BLK, BLK), 1)
    return kj > qi


def _bias_build(rel_bias, bucket, name):
    def body(rb_ref, bucket_ref, o_ref):
        b = bucket_ref[...]
        for h in range(A_HEADS):
            acc = jnp.zeros((BLK, BLK), F32)
            for k in range(N_BUCKETS):
                acc = jnp.where(b == k, rb_ref[k, h], acc)
            o_ref[h] = acc

    return pl.pallas_call(
        body, in_specs=[pl.BlockSpec(memory_space=pltpu.SMEM), pl.BlockSpec(memory_space=pltpu.VMEM)],
        out_specs=pl.BlockSpec(memory_space=pltpu.VMEM),
        out_shape=_SDS((A_HEADS, BLK, BLK), F32), name=name)(rel_bias, bucket)


def _bias_reduce(dbias, bucket, name):
    def body(db_ref, bucket_ref, o_ref):
        b = bucket_ref[...]
        row = lax.broadcasted_iota(jnp.int32, (N_BUCKETS, 128), 0)
        lane = lax.broadcasted_iota(jnp.int32, (N_BUCKETS, 128), 1)
        acc = jnp.zeros((N_BUCKETS, 128), F32)
        for h in range(A_HEADS):
            v = db_ref[h]
            for k in range(N_BUCKETS):
                sk = jnp.sum(jnp.sum(jnp.where(b == k, v, 0.0), axis=1, keepdims=True), axis=0, keepdims=True)
                acc = acc + jnp.where((row == k) & (lane == h), sk, 0.0)
        o_ref[...] = acc

    return pl.pallas_call(
        body, in_specs=[pl.BlockSpec(memory_space=pltpu.VMEM)] * 2,
        out_specs=pl.BlockSpec(memory_space=pltpu.VMEM),
        out_shape=_SDS((N_BUCKETS, 128), F32), name=name)(dbias, bucket)


def _mix_a_fwd(proj, bias, sinks, memkv, name):
    s = proj.shape[0]
    nb = s // BLK
    grp = A_HEADS // A_KV_HEADS

    def body(proj_ref, prev_ref, bias_ref, sink_ref, memkv_ref, o_ref):
        i = pl.program_id(0)
        upper = _swa_upper()
        prev = prev_ref[...].astype(F32)
        proj = proj_ref[...].astype(F32)
        kb = jnp.concatenate([prev[:, :A_KV], proj[:, A_Q:A_Q + A_KV]], axis=0)
        vb = jnp.concatenate([prev[:, A_KV:], proj[:, A_Q + A_KV:A_Q + 2 * A_KV]], axis=0)
        kb_r = pltpu.roll(kb, A_HD, 1)
        vb_r = pltpu.roll(vb, A_HD, 1)
        gw = A_Q // A_KV_HEADS
        groups = range(A_KV_HEADS)
        swa, cross = _mix_a_core([proj[:, g * gw:(g + 1) * gw] for g in groups], [_both_halves(kb, kb_r, g) for g in groups],
                                 [_both_halves(vb, vb_r, g) for g in groups], [_swa_sinks(sink_ref, g) for g in groups],
                                 [bias_ref[g * grp:(g + 1) * grp] for g in groups], proj[:, A_Q + 2 * A_KV:],
                                 memkv_ref[:, :X_Q], memkv_ref[:, X_Q:], upper, i == 0)
        o_ref[...] = jnp.concatenate(swa + [cross], axis=1).astype(o_ref.dtype)

    return pl.pallas_call(
        body, grid=(nb,),
        in_specs=[pl.BlockSpec((BLK, IN_A), lambda i: (i, 0)),
                  pl.BlockSpec((BLK, 2 * A_KV), lambda i: (jnp.maximum(i - 1, 0), A_Q // (2 * A_KV))),
                  pl.BlockSpec((A_HEADS, BLK, BLK), lambda i: (0, 0, 0)),
                  pl.BlockSpec((1, 128), lambda i: (0, 0)),
                  pl.BlockSpec((MEM_LEN, 2 * X_Q), lambda i: (0, 0))],
        out_specs=pl.BlockSpec((BLK, D), lambda i: (i, 0)),
        out_shape=_SDS((s, D), _ACT), name=name, compiler_params=_cp("arbitrary"))(proj, proj, bias, sinks, memkv)


def _mix_a_bwd(proj, bias, sinks, memkv, dmix, name, deps=()):
    s = proj.shape[0]
    nb = s // BLK
    grp = A_HEADS // A_KV_HEADS

    def body(proj_ref, prev_ref, bias_ref, sink_ref, memkv_ref, dmix_ref, *rest):
        dproj_ref, dbias_ref, dsink_ref, dmemkv_ref, carry_ref = rest[-5:]
        t = pl.program_id(0)
        i = nb - 1 - t

        @pl.when(t == 0)
        def _():
            carry_ref[...] = jnp.zeros_like(carry_ref)
            dbias_ref[...] = jnp.zeros_like(dbias_ref)
            dsink_ref[...] = jnp.zeros_like(dsink_ref)
            dmemkv_ref[...] = jnp.zeros_like(dmemkv_ref)

        upper = _swa_upper()
        lane = lax.broadcasted_iota(jnp.int32, (1, 128), 1)
        low = _lane_low()
        prev = prev_ref[...].astype(F32)
        proj = proj_ref[...].astype(F32)
        kb = jnp.concatenate([prev[:, :A_KV], proj[:, A_Q:A_Q + A_KV]], axis=0)
        vb = jnp.concatenate([prev[:, A_KV:], proj[:, A_Q + A_KV:A_Q + 2 * A_KV]], axis=0)
        kb_r = pltpu.roll(kb, A_HD, 1)
        vb_r = pltpu.roll(vb, A_HD, 1)
        gw = A_Q // A_KV_HEADS
        groups = range(A_KV_HEADS)
        _, vjp = jax.vjp(
            functools.partial(_mix_a_core, upper=upper, first=i == 0),
            [proj[:, g * gw:(g + 1) * gw] for g in groups], [_both_halves(kb, kb_r, g) for g in groups],
            [_both_halves(vb, vb_r, g) for g in groups], [_swa_sinks(sink_ref, g) for g in groups],
            [bias_ref[g * grp:(g + 1) * grp] for g in groups], proj[:, A_Q + 2 * A_KV:], memkv_ref[:, :X_Q], memkv_ref[:, X_Q:])
        dqs, dk, dv, ds, db, dxq, dmk, dmv = vjp(
            ([dmix_ref[:, g * gw:(g + 1) * gw].astype(F32) for g in groups], dmix_ref[:, A_Q:].astype(F32)))
        dkd = [t + pltpu.roll(t, A_HD, 1) for t in dk]
        dvd = [t + pltpu.roll(t, A_HD, 1) for t in dv]
        dsink = jnp.zeros((1, 128), F32)
        for g in groups:
            for h in range(grp):
                dsink = dsink + jnp.where(lane == g * grp + h, ds[g][h], 0.0)
            dbias_ref[g * grp:(g + 1) * grp] += db[g]
        dsink_ref[...] += dsink
        dkb = jnp.where(low, dkd[0], dkd[1])
        dvb = jnp.where(low, dvd[0], dvd[1])
        dmemkv_ref[...] += jnp.concatenate([dmk, dmv], axis=1)
        dkv_cur = jnp.concatenate([dkb[BLK:], dvb[BLK:]], axis=1) + carry_ref[...]
        carry_ref[...] = jnp.concatenate([dkb[:BLK], dvb[:BLK]], axis=1)
        dproj_ref[...] = jnp.concatenate(list(dqs) + [dkv_cur, dxq], axis=1).astype(dproj_ref.dtype)

    return pl.pallas_call(
        body, grid=(nb,),
        in_specs=[pl.BlockSpec((BLK, IN_A), lambda t: (nb - 1 - t, 0)),
                  pl.BlockSpec((BLK, 2 * A_KV), lambda t: (jnp.maximum(nb - 2 - t, 0), A_Q // (2 * A_KV))),
                  pl.BlockSpec((A_HEADS, BLK, BLK), lambda t: (0, 0, 0)),
                  pl.BlockSpec((1, 128), lambda t: (0, 0)),
                  pl.BlockSpec((MEM_LEN, 2 * X_Q), lambda t: (0, 0)),
                  pl.BlockSpec((BLK, D), lambda t: (nb - 1 - t, 0))] + _dep_specs(deps),
        out_specs=[pl.BlockSpec((BLK, IN_A), lambda t: (nb - 1 - t, 0)),
                   pl.BlockSpec((A_HEADS, BLK, BLK), lambda t: (0, 0, 0)),
                   pl.BlockSpec((1, 128), lambda t: (0, 0)),
                   pl.BlockSpec((MEM_LEN, 2 * X_Q), lambda t: (0, 0))],
        out_shape=[_SDS((s, IN_A), _ACT), _SDS((A_HEADS, BLK, BLK), F32), _SDS((1, 128), F32),
                   _SDS((MEM_LEN, 2 * X_Q), F32)],
        scratch_shapes=[pltpu.VMEM((BLK, 2 * A_KV), F32)],
        name=name, compiler_params=_cp("arbitrary"))(proj, proj, bias, sinks, memkv, dmix, *deps)


def _neumann(pw, rhs):
    nh = len(pw)
    x = rhs
    for lvl in range(6):
        if lvl < 5:
            prod = [_mmf(pw[h], jnp.concatenate([x[h], pw[h]], axis=1)) for h in range(nh)]
            x = [x[h] + prod[h][:, :B_HD] for h in range(nh)]
            pw = [t[:, B_HD:] for t in prod]
        else:
            x = [x[h] + _mmf(pw[h], x[h]) for h in range(nh)]
    return x


@jax.custom_vjp
def _tri_solve(pw, rhs):
    return _neumann(pw, rhs)


def _tri_solve_fwd(pw, rhs):
    x = _neumann(pw, rhs)
    return x, (pw, x)


def _tri_solve_bwd(res, dx):
    pw, x = res
    d_rhs = _neumann([t.T for t in pw], list(dx))
    return [_mmf_nt(d_rhs[h], x[h]) for h in range(len(pw))], d_rhs


_tri_solve.defvjp(_tri_solve_fwd, _tri_solve_bwd)


@jax.custom_vjp
def _tri_solved(pw, rhs, x):
    return x


def _tri_solved_fwd(pw, rhs, x):
    return x, (pw, x)


def _tri_solved_bwd(res, dx):
    d_pw, d_rhs = _tri_solve_bwd(res, dx)
    return d_pw, d_rhs, [jnp.zeros_like(t) for t in res[1]]


_tri_solved.defvjp(_tri_solved_fwd, _tri_solved_bwd)


@jax.custom_vjp
def _known(x, value):
    return value


def _known_fwd(x, value):
    return value, None


def _known_bwd(_, g):
    return g, jnp.zeros_like(g)


_known.defvjp(_known_fwd, _known_bwd)


def _dn_heads(yq, yk, yv, z, bl, al, a_log, dtb, ng, s0, solved=None, out_known=None):
    c = CHUNK
    nh = B_V_HEADS
    rep = B_V_HEADS // B_QK_HEADS
    r = lax.broadcasted_iota(jnp.int32, (c, c), 0)
    cc = lax.broadcasted_iota(jnp.int32, (c, c), 1)
    q = [_silu(t) for t in yq]
    k = [_silu(t) for t in yk]
    v = [_silu(t) for t in yv]
    q = [t * lax.rsqrt(jnp.sum(t * t, axis=-1, keepdims=True) + EPS) * (B_HD ** -0.5) for t in q]
    k = [t * lax.rsqrt(jnp.sum(t * t, axis=-1, keepdims=True) + EPS) for t in k]
    beta = [jax.nn.sigmoid(t) for t in bl]
    g = [-jnp.exp(a_log[h]) * jax.nn.softplus(al[h] + dtb[h]) for h in range(nh)]
    gb = [jnp.broadcast_to(t, (c, c)) for t in g]
    gc_col = [jnp.sum(jnp.where(cc <= r, t.T, 0.0), axis=1, keepdims=True) for t in gb]
    gc_row = [jnp.sum(jnp.where(r <= cc, t, 0.0), axis=0, keepdims=True) for t in gb]
    gc_last = [jnp.sum(t, axis=0, keepdims=True) for t in g]
    decay = [jnp.exp(jnp.where(r >= cc, gc_col[h] - gc_row[h], -jnp.inf)) for h in range(nh)]
    kq = [_mmf_nt(jnp.concatenate([k[h], q[h]], axis=0), k[h]) for h in range(B_QK_HEADS)]
    kk = [t[:c] for t in kq]
    qk = [t[c:] for t in kq]
    egc = [jnp.exp(t) for t in gc_col]
    both = [_mmf(jnp.concatenate([(beta[h] * egc[h]) * k[h // rep], q[h // rep] * egc[h]], axis=0), s0[h]) for h in range(nh)]
    rhs = [beta[h] * v[h] - both[h][:c] for h in range(nh)]
    qs0 = [t[c:] for t in both]
    pw = [-(beta[h] * kk[h // rep] * jnp.where(r > cc, decay[h], 0.0)) for h in range(nh)]
    delta = _tri_solve(pw, rhs) if solved is None else _tri_solved(pw, rhs, solved)
    last = [_mmf(jnp.concatenate([qk[h // rep] * decay[h], (k[h // rep] * jnp.exp(gc_last[h] - gc_col[h])).T], axis=0), delta[h])
            for h in range(nh)]
    out = [qs0[h] + last[h][:c] for h in range(nh)]
    if out_known is not None:
        out = [_known(out[h], out_known[h]) for h in range(nh)]
    s1 = [jnp.exp(gc_last[h]) * s0[h] + last[h][c:] for h in range(nh)]
    o = [t * lax.rsqrt(jnp.mean(t * t, axis=-1, keepdims=True) + EPS) * ng for t in out]
    return [o[h] * _silu(z[h]) for h in range(nh)], s1, delta, out


def _dn_conv(ext, w_ref):
    y = ext * w_ref[B_CONV - 1:B_CONV, :]
    for j in range(B_CONV - 1):
        y = y + w_ref[j:j + 1, :] * pltpu.roll(ext, B_CONV - 1 - j, 0)
    return y


def _dn_args(y, cur_ref, par_ref, ng_ref):
    nh = B_V_HEADS
    return ([y[:, h * B_HD:(h + 1) * B_HD] for h in range(B_QK_HEADS)],
            [y[:, B_QK + h * B_HD:B_QK + (h + 1) * B_HD] for h in range(B_QK_HEADS)],
            [y[:, 2 * B_QK + h * B_HD:2 * B_QK + (h + 1) * B_HD] for h in range(nh)],
            [cur_ref[:, BP_Z + h * B_HD:BP_Z + (h + 1) * B_HD] for h in range(nh)],
            [cur_ref[:, BP_GATE + h:BP_GATE + h + 1] for h in range(nh)],
            [cur_ref[:, BP_GATE + nh + h:BP_GATE + nh + h + 1] for h in range(nh)],
            [par_ref[:, h:h + 1] for h in range(nh)], [par_ref[:, nh + h:nh + h + 1] for h in range(nh)], ng_ref[...])


def _mix_b_fwd(proj, conv_w, par, ng, memkv, name):
    s = proj.shape[0]
    nc = s // CHUNK

    def body(cur_ref, prev_ref, w_ref, par_ref, ng_ref, memkv_ref, o_ref, st_ref, dl_ref, state_ref):
        n = pl.program_id(0)

        @pl.when(n == 0)
        def _():
            state_ref[...] = jnp.zeros_like(state_ref)

        prev = jnp.where(n > 0, prev_ref[...], 0.0)
        ext = jnp.concatenate([prev, cur_ref[:, :B_QKV]], axis=0)
        y = _dn_conv(ext, w_ref)[HALO:]
        s0 = [state_ref[hv] for hv in range(B_V_HEADS)]
        st_ref[0] = state_ref[...]
        outs, s1, delta, raw = _dn_heads(*_dn_args(y, cur_ref, par_ref, ng_ref), s0)
        for hv in range(B_V_HEADS):
            state_ref[hv] = s1[hv]
            dl_ref[0, hv] = delta[hv]
            dl_ref[0, B_V_HEADS + hv] = raw[hv]
        outs = outs + [_cross_pairs(cur_ref[:, BP_XQ:BP_XQ + X_Q], memkv_ref[:, :X_Q], memkv_ref[:, X_Q:])]
        o_ref[...] = jnp.concatenate(outs, axis=1).astype(o_ref.dtype)

    return pl.pallas_call(
        body, grid=(nc,),
        in_specs=[pl.BlockSpec((CHUNK, IN_BP), lambda n: (n, 0)),
                  pl.BlockSpec((HALO, B_QKV), lambda n: (jnp.maximum(n * (CHUNK // HALO) - 1, 0), 0)),
                  pl.BlockSpec((HALO, B_QKV), lambda n: (0, 0)),
                  pl.BlockSpec((1, 128), lambda n: (0, 0)), pl.BlockSpec((1, 128), lambda n: (0, 0)),
                  pl.BlockSpec((MEM_LEN, 2 * X_Q), lambda n: (0, 0))],
        out_specs=[pl.BlockSpec((CHUNK, D), lambda n: (n, 0)),
                   pl.BlockSpec((1, B_V_HEADS, B_HD, B_HD), lambda n: (n, 0, 0, 0)),
                   pl.BlockSpec((1, 2 * B_V_HEADS, CHUNK, B_HD), lambda n: (n, 0, 0, 0))],
        out_shape=[_SDS((s, D), _ACT), _SDS((nc, B_V_HEADS, B_HD, B_HD), F32), _SDS((nc, 2 * B_V_HEADS, CHUNK, B_HD), F32)],
        scratch_shapes=[pltpu.VMEM((B_V_HEADS, B_HD, B_HD), F32)],
        name=name, compiler_params=_cp("arbitrary"))(proj, proj, conv_w, par, ng, memkv)


def _mix_b_bwd(proj, conv_w, par, ng, memkv, states, deltas, dmix, name):
    s = proj.shape[0]
    nc = s // CHUNK
    ext_rows = CHUNK + HALO

    def body(cur_ref, prev_ref, w_ref, par_ref, ng_ref, memkv_ref, st_ref, dl_ref, dmix_ref,
             dproj_ref, dw_ref, dpar_ref, dng_ref, dmemkv_ref, dstate_ref, carry_ref):
        t = pl.program_id(0)
        n = nc - 1 - t

        @pl.when(t == 0)
        def _():
            dstate_ref[...] = jnp.zeros_like(dstate_ref)
            carry_ref[...] = jnp.zeros_like(carry_ref)
            dw_ref[...] = jnp.zeros_like(dw_ref)
            dpar_ref[...] = jnp.zeros_like(dpar_ref)
            dng_ref[...] = jnp.zeros_like(dng_ref)
            dmemkv_ref[...] = jnp.zeros_like(dmemkv_ref)

        lane = lax.broadcasted_iota(jnp.int32, (1, 128), 1)
        prev = jnp.where(n > 0, prev_ref[...], 0.0)
        ext = jnp.concatenate([prev, cur_ref[:, :B_QKV]], axis=0)
        y = _dn_conv(ext, w_ref)[HALO:]
        solved = [dl_ref[0, hv] for hv in range(B_V_HEADS)]
        raw = [dl_ref[0, B_V_HEADS + hv] for hv in range(B_V_HEADS)]
        _, vjp = jax.vjp(functools.partial(_dn_heads, solved=solved, out_known=raw), *_dn_args(y, cur_ref, par_ref, ng_ref),
                         [st_ref[0, hv] for hv in range(B_V_HEADS)])
        none = [jnp.zeros((CHUNK, B_HD), F32)] * B_V_HEADS
        dyq, dyk, dyv, dz, gbl, gal, ga_log, gdtb, dng, gs0 = vjp(
            ([dmix_ref[:, hv * B_HD:(hv + 1) * B_HD].astype(F32) for hv in range(B_V_HEADS)],
             [dstate_ref[hv] for hv in range(B_V_HEADS)], none, none))
        dgate = jnp.zeros((CHUNK, 128), F32)
        dpar = jnp.zeros((1, 128), F32)
        for hv in range(B_V_HEADS):
            dstate_ref[hv] = gs0[hv]
            dgate = dgate + jnp.where(lane == hv, gbl[hv], 0.0) + jnp.where(lane == B_V_HEADS + hv, gal[hv], 0.0)
            dpar = dpar + jnp.where(lane == hv, ga_log[hv], 0.0) + jnp.where(lane == B_V_HEADS + hv, gdtb[hv], 0.0)
        dpar_ref[...] += dpar
        dng_ref[...] += dng
        _, vjp = jax.vjp(_cross_pairs, cur_ref[:, BP_XQ:BP_XQ + X_Q], memkv_ref[:, :X_Q], memkv_ref[:, X_Q:])
        dxq, dmk, dmv = vjp(dmix_ref[:, B_V:].astype(F32))
        dmemkv_ref[...] += jnp.concatenate([dmk, dmv], axis=1)
        dy = jnp.concatenate(list(dyq) + list(dyk) + list(dyv), axis=1)
        dy_ext = jnp.concatenate([jnp.zeros((HALO, B_QKV), F32), dy], axis=0)
        dext = dy_ext * w_ref[B_CONV - 1:B_CONV, :]
        dw_ref[B_CONV - 1:B_CONV, :] += jnp.sum(ext * dy_ext, axis=0, keepdims=True)
        for j in range(B_CONV - 1):
            sh = B_CONV - 1 - j
            dw_ref[j:j + 1, :] += jnp.sum(pltpu.roll(ext, sh, 0) * dy_ext, axis=0, keepdims=True)
            dext = dext + w_ref[j:j + 1, :] * pltpu.roll(dy_ext, ext_rows - sh, 0)
        tail = jnp.concatenate([jnp.zeros((CHUNK - HALO, B_QKV), F32), carry_ref[...]], axis=0)
        dqkv = dext[HALO:] + tail
        carry_ref[...] = dext[:HALO]
        dproj_ref[...] = jnp.concatenate([dqkv] + list(dz) + [dxq, dgate], axis=1).astype(dproj_ref.dtype)

    return pl.pallas_call(
        body, grid=(nc,),
        in_specs=[pl.BlockSpec((CHUNK, IN_BP), lambda t: (nc - 1 - t, 0)),
                  pl.BlockSpec((HALO, B_QKV), lambda t: (jnp.maximum((nc - 1 - t) * (CHUNK // HALO) - 1, 0), 0)),
                  pl.BlockSpec((HALO, B_QKV), lambda t: (0, 0)),
                  pl.BlockSpec((1, 128), lambda t: (0, 0)), pl.BlockSpec((1, 128), lambda t: (0, 0)),
                  pl.BlockSpec((MEM_LEN, 2 * X_Q), lambda t: (0, 0)),
                  pl.BlockSpec((1, B_V_HEADS, B_HD, B_HD), lambda t: (nc - 1 - t, 0, 0, 0)),
                  pl.BlockSpec((1, 2 * B_V_HEADS, CHUNK, B_HD), lambda t: (nc - 1 - t, 0, 0, 0)),
                  pl.BlockSpec((CHUNK, D), lambda t: (nc - 1 - t, 0))],
        out_specs=[pl.BlockSpec((CHUNK, IN_BP), lambda t: (nc - 1 - t, 0)),
                   pl.BlockSpec((HALO, B_QKV), lambda t: (0, 0)),
                   pl.BlockSpec((1, 128), lambda t: (0, 0)), pl.BlockSpec((1, 128), lambda t: (0, 0)),
                   pl.BlockSpec((MEM_LEN, 2 * X_Q), lambda t: (0, 0))],
        out_shape=[_SDS((s, IN_BP), _ACT), _SDS((HALO, B_QKV), F32), _SDS((1, 128), F32), _SDS((1, 128), F32),
                   _SDS((MEM_LEN, 2 * X_Q), F32)],
        scratch_shapes=[pltpu.VMEM((B_V_HEADS, B_HD, B_HD), F32), pltpu.VMEM((HALO, B_QKV), F32)],
        name=name, compiler_params=_cp("arbitrary"))(proj, proj, conv_w, par, ng, memkv, states, deltas, dmix)


def _place():
    return lax.axis_index("x"), lax.axis_index("y"), lax.axis_index("c")


def _all_gather(shards, name):
    n = len(shards)

    def body(*refs):
        ins, outs = refs[:n], refs[n:2 * n]
        send_sems, recv_sems, local_sems = refs[2 * n:]
        x, y, c = _place()
        me, sibling = (x, y, c), (x, y, 1 - c)
        chips = [(1 - x, y), (x, 1 - y), (1 - x, 1 - y)]

        def rows(a, px, py, pc):
            return outs[a].at[4 * px + 2 * py + pc]

        def copy(a, k, block, to, src=None):
            return pltpu.make_async_remote_copy(
                src_ref=rows(a, *block) if src is None else src, dst_ref=rows(a, *block),
                send_sem=send_sems.at[a, k], recv_sem=recv_sems.at[a, k],
                device_id=to, device_id_type=pl.DeviceIdType.MESH)

        mine = [pltpu.make_async_copy(ins[a], rows(a, *me), local_sems.at[a]) for a in range(n)]
        for cp in mine:
            cp.start()
        first = []
        for a in range(n):
            first.append(copy(a, 0, me, sibling, src=ins[a]))
            first += [copy(a, 1 + j, me, (*chip, c), src=ins[a]) for j, chip in enumerate(chips)]
        for cp in first:
            cp.start()
        passed = []
        for j, chip in enumerate(chips):
            for a in range(n):
                copy(a, 1 + j, (*chip, c), me).wait_recv()
                fwd = copy(a, 4 + j, (*chip, c), sibling)
                fwd.start()
                passed.append(fwd)
        for a in range(n):
            copy(a, 0, sibling, me).wait_recv()
            for j, chip in enumerate(chips):
                copy(a, 4 + j, (*chip, 1 - c), me).wait_recv()
        for cp in first + passed:
            cp.wait_send()
        for cp in mine:
            cp.wait()

    hbm = pl.BlockSpec(memory_space=pl.ANY)
    return pl.pallas_call(
        body, out_shape=[_SDS((N_DEV,) + s.shape, s.dtype) for s in shards],
        in_specs=[hbm] * n, out_specs=[hbm] * n,
        scratch_shapes=[pltpu.SemaphoreType.DMA((n, 7)), pltpu.SemaphoreType.DMA((n, 7)), pltpu.SemaphoreType.DMA((n,))],
        name=name)(*shards)


class _Exchange:
    def __init__(self, lands, srcs):
        self.lands, self.srcs = lands, srcs


def _seq_exchange(srcs, land_shapes, plan, name, cid):
    n, nl = len(srcs), len(land_shapes)

    def launch(*refs):
        src_refs, land_refs = refs[:n], refs[n:n + nl]
        send_sems, recv_sems, local_sems = refs[n + nl:]
        x, y, c = _place()
        my = 4 * x + 2 * y + c
        peers = [(x ^ ((k + 1) >> 2 & 1), y ^ ((k + 1) >> 1 & 1), c ^ ((k + 1) & 1)) for k in range(N_DEV - 1)]
        barrier = pltpu.get_barrier_semaphore()
        for p in peers:
            pl.semaphore_signal(barrier, inc=1, device_id=p, device_id_type=pl.DeviceIdType.MESH)
        pl.semaphore_wait(barrier, N_DEV - 1)

        def src_for(a, dest):
            return src_refs[a].at[dest] if plan[a][1] else src_refs[a]

        def slot(a, source):
            return land_refs[plan[a][0]].at[source]

        mine = [pltpu.make_async_copy(src_for(a, my), slot(a, my), local_sems.at[a]) for a in range(n)]
        for cp in mine:
            cp.start()
        sends, recvs = [], []
        for k, (px, py, pc) in enumerate(peers):
            peer = 4 * px + 2 * py + pc
            for a in range(n):
                kw = dict(send_sem=send_sems.at[a * (N_DEV - 1) + k], recv_sem=recv_sems.at[a * (N_DEV - 1) + k],
                          device_id=(px, py, pc), device_id_type=pl.DeviceIdType.MESH)
                sends.append(pltpu.make_async_remote_copy(src_ref=src_for(a, peer), dst_ref=slot(a, my), **kw))
                recvs.append(pltpu.make_async_remote_copy(src_ref=src_for(a, my), dst_ref=slot(a, peer), **kw))
        for cp in sends:
            cp.start()
        for cp in recvs:
            cp.wait_recv()
        for cp in sends:
            cp.wait_send()
        for cp in mine:
            cp.wait()

    lands = pl.kernel(
        launch, out_type=[_SDS(s, d) for s, d in land_shapes],
        mesh=plsc.ScalarSubcoreMesh(axis_name="sequencer", num_cores=1), name=name,
        scratch_types=(pltpu.SemaphoreType.DMA((n * (N_DEV - 1),)), pltpu.SemaphoreType.DMA((n * (N_DEV - 1),)),
                       pltpu.SemaphoreType.DMA((n,))),
        compiler_params=pltpu.CompilerParams(collective_id=cid))(*srcs)
    return _Exchange(list(lands), list(srcs))


def _adam_update(g, w, m, v):
    c1 = 1.0 - ADAM_B1 ** ADAM_STEP
    c2 = 1.0 - ADAM_B2 ** ADAM_STEP
    mm = ADAM_B1 * m + (1.0 - ADAM_B1) * g
    vv = ADAM_B2 * v + (1.0 - ADAM_B2) * (g * g)
    delta = -ADAM_LR * ((mm / c1) / (jnp.sqrt(vv / c2) + ADAM_EPS) + ADAM_WD * w)
    return delta, mm, vv


def _sum_sources(p_ref):
    g = p_ref[0].astype(F32)
    for s in range(1, N_DEV):
        g = g + p_ref[s].astype(F32)
    return g


def _adamw(parts, w, m, v, tr, name, restore_b=False, deps=()):
    nl, r, c = w.shape
    cp = parts[0].shape[-1]

    def body(*refs):
        p_refs = refs[:nl]
        w_ref, m_ref, v_ref = refs[nl:nl + 3]
        g_ref, d_ref, nm_ref, nv_ref = refs[-4:]
        g = _sum_sources(p_refs[0])
        for l in range(1, nl):
            g = jnp.where(pl.program_id(0) == l, _sum_sources(p_refs[l]), g)
        if restore_b:
            g = jnp.concatenate([g[:, :BP_XQ], g[:, BP_GATE:BP_GATE + 2 * B_V_HEADS], g[:, BP_XQ:BP_GATE]], axis=1)
        delta, mm, vv = _adam_update(g, w_ref[...], m_ref[...], v_ref[...])
        g_ref[...] = g
        d_ref[...] = delta
        nm_ref[...] = mm
        nv_ref[...] = vv

    spec = pl.BlockSpec((None, tr, c), lambda l, i: (l, i, 0))
    part_specs = [pl.BlockSpec((N_DEV, tr, cp), functools.partial(lambda l, i, k: (0, jnp.where(l == k, i, 0), 0), k=k))
                  for k in range(nl)]
    return pl.pallas_call(
        body, grid=(nl, r // tr),
        in_specs=part_specs + [spec, spec, spec] + _dep_specs(deps),
        out_specs=[spec] * 4, out_shape=[_SDS(w.shape, F32)] * 4,
        name=name, compiler_params=_cp("arbitrary", "arbitrary"))(*parts, w, m, v, *deps)


def _pack_small(d_rel, d_cb, d_cw, d_qkv, d_mix, d_mem, d_ffn, d_final, d_sinks, d_par, d_ng, loss_row, name):
    flat = [d_rel, *d_cb, *d_cw, d_qkv, *d_mix, *d_mem, *d_ffn, d_final, d_sinks, d_par, d_ng, loss_row]
    n = len(flat)

    def body(*refs):
        ins, o_ref = refs[:n], refs[n]
        rel, cb0, cb1, cw0, cw1, qkv, mx0, mx1, me0, me1, ff0, ff1, fin, snk, par, ng, lss = ins
        o_ref[...] = jnp.zeros_like(o_ref)
        for k in range(N_BUCKETS):
            lane = SP_REL_LANE + 128 * (k % 8)
            o_ref[SP_QKV + k // 8:SP_QKV + k // 8 + 1, lane:lane + 128] = rel[k:k + 1, :]
        for l, (cb, cw) in enumerate(((cb0, cw0), (cb1, cw1))):
            o_ref[SP_CB + l:SP_CB + l + 1, :] = jnp.concatenate([cb[j] for j in range(FF_BLOCKS)], axis=1)
            full = jnp.concatenate([cw[j] for j in range(FF_BLOCKS)], axis=1)
            o_ref[SP_CW + FFN_CONV * l:SP_CW + FFN_CONV * (l + 1), :] = full[:FFN_CONV]
        o_ref[SP_QKV:SP_QKV + B_CONV, 0:B_QKV] = qkv[0:B_CONV, :]
        for base, pair in ((SP_MIX, (mx0, mx1)), (SP_MEM, (me0, me1)), (SP_FFN, (ff0, ff1))):
            for l in range(2):
                o_ref[base + l:base + l + 1, 0:D] = pair[l][...]
        o_ref[SP_FINAL:SP_FINAL + 1, 0:D] = fin[...]
        o_ref[SP_MISC:SP_MISC + 1, 0:128] = snk[...]
        o_ref[SP_MISC:SP_MISC + 1, 128:256] = par[...]
        o_ref[SP_MISC:SP_MISC + 1, 256:384] = ng[...]
        o_ref[SP_MISC:SP_MISC + 1, 384:512] = lss[...]

    vm = pl.BlockSpec(memory_space=pltpu.VMEM)
    return pl.pallas_call(body, in_specs=[vm] * n, out_specs=vm, out_shape=_SDS((SMALL_ROWS, D_FF), F32), name=name)(*flat)


_SMALL = ["rel_bias", "norm_mix_g", "norm_mem_g", "sinks_a", "a_log_b", "dt_bias_b", "out_norm_g_b", "norm_ffn_g",
          "ffn_conv_b", "final_norm_g", "conv_qkv_b", "ffn_conv_w"]


def _adamw_small(recv, rc_qkv, rc_ffn, ws, ms, vs, name, deps=()):
    n = len(_SMALL)

    def body(*refs):
        recv_ref, qkv_ref, ffn_ref = refs[:3]
        w_refs, m_refs, v_refs = refs[3:3 + n], refs[3 + n:3 + 2 * n], refs[3 + 2 * n:3 + 3 * n]
        outs, loss_ref = refs[len(refs) - 4 * n - 1:len(refs) - 1], refs[-1]
        gs = _sum_sources(recv_ref)
        loss_ref[...] = gs[SP_MISC:SP_MISC + 1, 384:512]
        grads = {
            "rel_bias": jnp.concatenate(
                [gs[SP_QKV + k // 8:SP_QKV + k // 8 + 1, SP_REL_LANE + 128 * (k % 8):SP_REL_LANE + 128 * (k % 8) + A_HEADS]
                 for k in range(N_BUCKETS)], axis=0),
            "norm_mix_g": gs[SP_MIX:SP_MIX + 2, 0:D], "norm_mem_g": gs[SP_MEM:SP_MEM + 2, 0:D],
            "sinks_a": gs[SP_MISC:SP_MISC + 1, 0:A_HEADS],
            "a_log_b": gs[SP_MISC:SP_MISC + 1, 128:128 + B_V_HEADS],
            "dt_bias_b": gs[SP_MISC:SP_MISC + 1, 128 + B_V_HEADS:128 + 2 * B_V_HEADS],
            "out_norm_g_b": gs[SP_MISC:SP_MISC + 1, 256:256 + B_HD],
            "norm_ffn_g": gs[SP_FFN:SP_FFN + 2, 0:D], "ffn_conv_b": gs[SP_CB:SP_CB + 2, :],
            "final_norm_g": gs[SP_FINAL:SP_FINAL + 1, 0:D],
            "conv_qkv_b": _sum_sources(qkv_ref), "ffn_conv_w": _sum_sources(ffn_ref),
        }
        for i, nm in enumerate(_SMALL):
            g = grads[nm]
            delta, mm, vv = _adam_update(g, w_refs[i][...], m_refs[i][...], v_refs[i][...])
            outs[i][...] = g
            outs[n + i][...] = delta
            outs[2 * n + i][...] = mm
            outs[3 * n + i][...] = vv

    vm = pl.BlockSpec(memory_space=pltpu.VMEM)
    shapes = [_SDS(w.shape, F32) for w in ws]
    return pl.pallas_call(
        body, in_specs=[vm] * (3 + 3 * n) + _dep_specs(deps), out_specs=[vm] * (4 * n + 1),
        out_shape=shapes * 4 + [_SDS((1, 128), F32)],
        name=name)(recv, rc_qkv, rc_ffn, *ws, *ms, *vs, *deps)


def _assemble(gathered, axis):
    g = jnp.moveaxis(gathered, 0, axis)
    shp = list(g.shape)
    return g.reshape(shp[:axis] + [shp[axis] * shp[axis + 1]] + shp[axis + 2:])


def _pad_rows(a, rows):
    return jnp.pad(a, ((0, rows - a.shape[0]), (0, 0)))


def _pad_lanes(a, lanes=128):
    return jnp.pad(a, ((0, 0), (0, lanes - a.shape[1])))


def _ff_blocks(a):
    return jnp.moveaxis(a.reshape(a.shape[0], FF_BLOCKS, GU_SHARD), 1, 0)


def _reorder_b(w):
    qkv_z = w[..., :B_QKV + B_V]
    gates = w[..., B_QKV + B_V:B_QKV + B_V + 2 * B_V_HEADS]
    xq = w[..., IN_B - X_Q:]
    pad = jnp.zeros(w.shape[:-1] + (IN_BP - IN_B,), w.dtype)
    return jnp.concatenate([qkv_z, xq, gates, pad], axis=-1)


def kernel(x, mem, rel_bias, norm_mix_g, norm_mem_g, w_mem_kv, w_out, w_in_a, sinks_a, w_in_b, conv_qkv_b, a_log_b, dt_bias_b, out_norm_g_b, norm_ffn_g, w_gate_up, ffn_conv_w, ffn_conv_b, w_down, final_norm_g, loss_target, m_rel_bias, m_norm_mix_g, m_norm_mem_g, m_w_mem_kv, m_w_out, m_w_in_a, m_sinks_a, m_w_in_b, m_conv_qkv_b, m_a_log_b, m_dt_bias_b, m_out_norm_g_b, m_norm_ffn_g, m_w_gate_up, m_ffn_conv_w, m_ffn_conv_b, m_w_down, m_final_norm_g, v_rel_bias, v_norm_mix_g, v_norm_mem_g, v_w_mem_kv, v_w_out, v_w_in_a, v_sinks_a, v_w_in_b, v_conv_qkv_b, v_a_log_b, v_dt_bias_b, v_out_norm_g_b, v_norm_ffn_g, v_w_gate_up, v_ffn_conv_w, v_ffn_conv_b, v_w_down, v_final_norm_g):
    local = dict(locals())
    order = ["rel_bias", "norm_mix_g", "norm_mem_g", "w_mem_kv", "w_out", "w_in_a", "sinks_a", "w_in_b", "conv_qkv_b",
             "a_log_b", "dt_bias_b", "out_norm_g_b", "norm_ffn_g", "w_gate_up", "ffn_conv_w", "ffn_conv_b", "w_down",
             "final_norm_g"]
    wts = {n: local[n] for n in order}
    moms = {n: local["m_" + n] for n in order}
    vars_ = {n: local["v_" + n] for n in order}
    h0 = x[0]
    memx = mem[0]
    tgt = loss_target[0]
    s = h0.shape[0]
    tm = _rows(s)
    tb = min(s, _TM_BIG)

    t_ = lambda a: jnp.swapaxes(a, 1, 2)
    g_mk0, g_out0, g_ia, g_cq, g_cw = _all_gather(
        [w_mem_kv[0:1].astype(_MXU), w_out[0:1].astype(_MXU), t_(w_in_a).astype(_MXU), conv_qkv_b, ffn_conv_w], "gather_first")
    g_mk, g_out = [g_mk0], [g_out0]
    gu_land = ((N_DEV, GU_SHARD, D), _MXU)
    dn_land = ((N_DEV, DN_SHARD, D), _MXU)
    whole = [(0, False), (1, False)]
    def after(a, b):
        return a + (b[(0,) * b.ndim] * 0).astype(a.dtype)

    gu0_w = _seq_exchange([after(t_(w_gate_up)[0].astype(_MXU), g_ia)], [gu_land], [(0, False)], "gather_gate_up0", 1)
    dn0_w = _seq_exchange([after(w_down[0].astype(_MXU), g_ia)], [dn_land], [(0, False)], "gather_down0", 8)
    w_ia = g_ia.reshape(IN_A, D)
    conv_qkv = _pad_rows(_assemble(g_cq, 2)[0], HALO)
    ffn_cw_full = _assemble(g_cw, 2)
    ffn_cw = [_ff_blocks(_pad_rows(ffn_cw_full[i], HALO)) for i in range(2)]
    ffn_cb = [_ff_blocks(ffn_conv_b[i:i + 1]) for i in range(2)]
    bucket = jnp.asarray(_bucket_table())
    bias = _bias_build(rel_bias, bucket, "bias_build")
    sinks = _pad_lanes(sinks_a)
    par_b = _pad_lanes(jnp.concatenate([a_log_b, dt_bias_b], axis=1))

    row_x = pl.BlockSpec((tm, D), lambda i, j: (i, 0))
    gu_shape = (2, FF_BLOCKS, s, GU_SHARD)

    def in_proj(h, g, w, w_spec, n_cols, tn, name, deps=(), out_dtype=F32, w_t=False, tm=None):
        return _norm_matmul(h, g, w, w_spec, n_cols // tn, (h.shape[0], n_cols),
                            pl.BlockSpec((tm or _rows(h.shape[0]), tn), lambda i, j: (i, j)), name, deps=deps, out_dtype=out_dtype,
                            w_t=w_t, tm=tm)

    def ffn_fwd(i, h, g_gu, g_dn, deps=()):
        gu, hn = _norm_matmul(h, norm_ffn_g[i:i + 1], g_gu, _spec_gate_up(1), N_DEV, gu_shape,
                              _spec_gu_act(0, 1, tb), f"gate_up_{i}", deps=deps, out_dtype=_ACT, w_t=True, tm=tb)
        h_new, act = _glu_down(gu, ffn_cw[i], ffn_cb[i], g_dn, h, f"glu_down_{i}")
        return h_new, gu, hn, act

    def out_proj(i, mix, h):
        return _matmul_res(mix, row_x, g_out[i], _spec_rowsharded(0, D // N_DEV, D), 1, h, f"out_proj_{i}")

    proj_a, hn_a = in_proj(h0, norm_mix_g[0:1], w_ia, pl.BlockSpec((640, D), lambda i, j: (j, 0)), IN_A, 640, "in_proj_a",
                           deps=gu0_w.srcs + dn0_w.srcs, out_dtype=_ACT, w_t=True)
    memkv0, memn0 = in_proj(memx, norm_mem_g[0:1], g_mk[0], _spec_rowsharded(0, D // N_DEV, 2 * X_Q), 2 * X_Q, 2 * X_Q, "mem_proj_0")
    mix_a = _mix_a_fwd(proj_a, bias, sinks, memkv0, "mix_a_fwd")
    h1 = out_proj(0, mix_a, h0)
    g_gu0, g_dn0 = gu0_w.lands[0], dn0_w.lands[0]
    in_b_w = _seq_exchange([after(_reorder_b(w_in_b).astype(_MXU), h1), after(w_mem_kv[1:2].astype(_MXU), h1),
                            after(w_out[1:2].astype(_MXU), h1)],
                           [((N_DEV, 1, D // N_DEV, IN_BP), _MXU), ((N_DEV, 1, D // N_DEV, 2 * X_Q), _MXU),
                            ((N_DEV, 1, D // N_DEV, D), _MXU)], [(0, False), (1, False), (2, False)], "gather_in_b", 2)
    ffn1_w = _seq_exchange([after(t_(w_gate_up)[1].astype(_MXU), h1), after(w_down[1].astype(_MXU), h1)], [gu_land, dn_land], whole,
                           "gather_ffn1", 3)
    h2, gu0, hn_f0, act0 = ffn_fwd(0, h1, g_gu0, g_dn0, deps=in_b_w.srcs + ffn1_w.srcs)
    g_ib, g_mk1, g_out1 = in_b_w.lands
    g_mk.append(g_mk1)
    g_out.append(g_out1)
    proj_b, hn_b = in_proj(h2, norm_mix_g[1:2], g_ib, _spec_rowsharded(0, D // N_DEV, 896, col_block=1), IN_BP, 896, "in_proj_b")
    memkv1, memn1 = in_proj(memx, norm_mem_g[1:2], g_mk[1], _spec_rowsharded(0, D // N_DEV, 2 * X_Q), 2 * X_Q, 2 * X_Q, "mem_proj_1",
                            deps=[h2])
    mix_b, states, deltas = _mix_b_fwd(proj_b, conv_qkv, par_b, out_norm_g_b, memkv1, "mix_b_fwd")
    h3 = out_proj(1, mix_b, h2)
    g_gu1, g_dn1 = ffn1_w.lands
    h4, gu1, hn_f1, act1 = ffn_fwd(1, h3, g_gu1, g_dn1)
    loss_row, dh, d_final_g = _loss_head(h4, final_norm_g[None, :], tgt, "loss_head")

    zeros_mem = jnp.zeros_like(memx)
    per_dest2 = [(0, True), (1, True)]

    def ffn_bwd(i, dh, h_in, gu, hn_f, act, g_gu, g_dn, deps=()):
        dgu, d_cw, d_cb = _glu_bwd(gu, ffn_cw[i], ffn_cb[i], dh, g_dn, f"glu_bwd_{i}", deps=deps)
        d_wdown = _matmul_tn(act, pl.BlockSpec((None, tm, GU_SHARD), lambda j, r: (j, r, 0)),
                             dh, pl.BlockSpec((tm, D), lambda j, r: (r, 0)), s, FF_BLOCKS, (GU_SHARD, D),
                             (N_DEV, DN_SHARD, D), pl.BlockSpec((2, DN_SHARD, D), lambda j, r: (j, 0, 0)), f"d_w_down_{i}")
        dh_new, d_g = _matmul_nt_normbwd(dgu, _spec_gu_act(0, 1, tm), g_gu, _spec_gate_up(1), N_DEV, h_in,
                                         norm_ffn_g[i:i + 1], dh, f"d_ffn_in_{i}", w_t=True)
        d_wgu = _matmul_tn(dgu, _spec_gu_act(1, 0, tb), hn_f, pl.BlockSpec((tb, D), lambda j, r: (r, 0)), s, N_DEV,
                           (GU_SHARD, D), (N_DEV, GU_SHARD, D), pl.BlockSpec((None, GU_SHARD, D), lambda j, r: (j, 0, 0)),
                           f"d_w_gate_up_{i}", tm=tb)
        return dh_new, [d_wdown, d_wgu], d_cw, d_cb, d_g

    def out_bwd(i, dh, mix, deps):
        dmix = _matmul_nt(dh, g_out[i], _spec_rowsharded(0, D // N_DEV, D), 1, (s, D), row_x, f"d_mix_{i}", deps=deps, out_dtype=_ACT)
        d_wout = _matmul_tn(mix, pl.BlockSpec((tm, D), lambda j, r: (r, 0)), dh, pl.BlockSpec((tm, D), lambda j, r: (r, 0)),
                            s, 1, (D, D), (N_DEV, D // N_DEV, D), pl.BlockSpec((N_DEV, D // N_DEV, D), lambda j, r: (0, 0, 0)),
                            f"d_w_out_{i}")
        return dmix, d_wout

    def mem_bwd(i, dmemkv, memn):
        tmm = _rows(MEM_LEN)
        _, d_g = _matmul_nt_normbwd(dmemkv, pl.BlockSpec((tmm, 2 * X_Q), lambda r, j: (r, 0)), g_mk[i],
                                    _spec_rowsharded(0, D // N_DEV, 2 * X_Q), 1, memx, norm_mem_g[i:i + 1], zeros_mem,
                                    f"d_mem_in_{i}")
        by_row = lambda j, r: (r, 0)
        d_w = _matmul_tn(memn, pl.BlockSpec((tmm, D), by_row), dmemkv, pl.BlockSpec((tmm, 2 * X_Q), by_row), MEM_LEN, 1,
                         (D, 2 * X_Q), (N_DEV, D // N_DEV, 2 * X_Q),
                         pl.BlockSpec((N_DEV, D // N_DEV, 2 * X_Q), lambda j, r: (0, 0, 0)), f"d_w_mem_kv_{i}")
        return d_w, d_g

    out_land = ((N_DEV, D // N_DEV, D), _WIRE)
    mk_land = ((N_DEV, D // N_DEV, 2 * X_Q), _WIRE)
    ffn_lands = [((N_DEV, DN_SHARD, D), _WIRE), ((N_DEV, GU_SHARD, D), _WIRE)]
    dh, d_ffn1, d_cw1, d_cb1, d_gf1 = ffn_bwd(1, dh, h3, gu1, hn_f1, act1, g_gu1, g_dn1)
    ffn1_g = _seq_exchange(d_ffn1, ffn_lands, per_dest2, "send_ffn1_grads", 5)
    dmix, d_wout1 = out_bwd(1, dh, mix_b, ffn1_g.srcs)
    dproj_b, d_convw, d_par, d_ng, dmemkv1 = _mix_b_bwd(proj_b, conv_qkv, par_b, out_norm_g_b, memkv1, states, deltas, dmix, "mix_b_bwd")
    dh, d_gm1 = _matmul_nt_normbwd(dproj_b, pl.BlockSpec((tm, 896), lambda i, j: (i, j)), g_ib,
                                   _spec_rowsharded(0, D // N_DEV, 896, col_block=1), IN_BP // 896, h2, norm_mix_g[1:2], dh, "d_in_b")
    d_wib = _matmul_tn(hn_b, pl.BlockSpec((tb, D), lambda j, r: (r, 0)), dproj_b, pl.BlockSpec((tb, 896), lambda j, r: (r, j)),
                       s, IN_BP // 896, (D, 896), (N_DEV, D // N_DEV, IN_BP),
                       pl.BlockSpec((N_DEV, D // N_DEV, 896), lambda j, r: (0, 0, j)), "d_w_in_b", tm=tb)
    d_wmk1, d_gmem1 = mem_bwd(1, dmemkv1, memn1)
    mix1_g = _seq_exchange([d_wout1, d_wib, d_wmk1], [out_land, ((N_DEV, D // N_DEV, IN_BP), _WIRE), mk_land],
                           [(0, True), (1, True), (2, True)], "send_mix1_grads", 6)
    dh, d_ffn0, d_cw0, d_cb0, d_gf0 = ffn_bwd(0, dh, h1, gu0, hn_f0, act0, g_gu0, g_dn0, deps=mix1_g.srcs)
    dmix, d_wout0 = out_bwd(0, dh, mix_a, d_ffn0 + ffn1_g.lands[:1])
    ffn0_g = _seq_exchange(d_ffn0 + [d_wout0], ffn_lands + [out_land], per_dest2 + [(2, True)], "send_ffn0_grads", 4)
    dproj_a, dbias, dsinks, dmemkv0 = _mix_a_bwd(proj_a, bias, sinks, memkv0, dmix, "mix_a_bwd", deps=ffn0_g.srcs)
    dh, d_gm0 = _matmul_nt_normbwd(dproj_a, pl.BlockSpec((tm, 640), lambda i, j: (i, j)), w_ia,
                                   pl.BlockSpec((640, D), lambda i, j: (j, 0)), IN_A // 640, h0, norm_mix_g[0:1], dh, "d_in_a",
                                   w_t=True)
    d_wia = _matmul_tn(dproj_a, pl.BlockSpec((tm, IN_A), lambda j, r: (r, 0)), hn_a, pl.BlockSpec((tm, D), lambda j, r: (r, 0)),
                       s, 1, (IN_A, D), (N_DEV, IA_SHARD, D), pl.BlockSpec((N_DEV, IA_SHARD, D), lambda j, r: (0, 0, 0)),
                       "d_w_in_a")
    d_wmk0, d_gmem0 = mem_bwd(0, dmemkv0, memn0)
    d_rel = _bias_reduce(dbias, bucket, "bias_reduce")
    small = _pack_small(d_rel, (d_cb0, d_cb1), (d_cw0, d_cw1), d_convw, (d_gm0, d_gm1), (d_gmem0, d_gmem1),
                        (d_gf0, d_gf1), d_final_g, dsinks, d_par, d_ng, loss_row, "pack_small")
    mix0_g = _seq_exchange([d_wia, d_wmk0, small],
                           [((N_DEV, IA_SHARD, D), _WIRE), mk_land, ((N_DEV, SMALL_ROWS, D_FF), F32)],
                           [(0, True), (1, True), (2, False)], "send_mix0_grads", 7)

    res = {}
    last = []

    def update(nm, parts, tr, restore=False, transposed=False):
        view = t_ if transposed else (lambda a: a)
        out = _adamw(parts, view(wts[nm]), view(moms[nm]), view(vars_[nm]), tr, "adamw_" + nm, restore_b=restore, deps=last[-1:])
        res[nm] = [view(o) for o in out]
        last.append(out[1])

    r_dn1, r_gu1 = ffn1_g.lands
    r_dn0, r_gu0, r_out0 = ffn0_g.lands
    r_out1, r_ib, r_mk1 = mix1_g.lands
    update("w_in_b", [r_ib], 32, True)
    update("w_gate_up", [r_gu0, r_gu1], 176, transposed=True)
    update("w_down", [r_dn0, r_dn1], 176)
    r_ia, r_mk0, r_small = mix0_g.lands
    update("w_mem_kv", [r_mk0, r_mk1], 128)
    update("w_out", [r_out0, r_out1], 128)
    update("w_in_a", [r_ia], IA_SHARD, transposed=True)

    my = 4 * lax.axis_index("x") + 2 * lax.axis_index("y") + lax.axis_index("c")
    cq = conv_qkv_b.shape[-1]
    cf = ffn_conv_w.shape[-1]
    rc_qkv = lax.dynamic_slice_in_dim(r_small[:, SP_QKV:SP_QKV + B_CONV, :B_QKV], my * cq, cq, axis=2)[:, None]
    rc_ffn = lax.dynamic_slice_in_dim(r_small[:, SP_CW:SP_CW + 2 * FFN_CONV, :], my * cf, cf, axis=2).reshape(N_DEV, 2, FFN_CONV, cf)
    as2d = lambda a: a[None, :] if a.ndim == 1 else a
    small_out = _adamw_small(r_small, rc_qkv, rc_ffn, [as2d(wts[n]) for n in _SMALL], [as2d(moms[n]) for n in _SMALL],
                             [as2d(vars_[n]) for n in _SMALL], "adamw_small", deps=last[-1:])
    ns = len(_SMALL)
    for i, nm in enumerate(_SMALL):
        res[nm] = [small_out[k * ns + i].reshape(wts[nm].shape) for k in range(4)]

    return (small_out[-1][0, 0], dh[None], *[res[n][0] for n in order], *[res[n][1] for n in order],
            *[res[n][2] for n in order], *[res[n][3] for n in order])
```

```python
import functools
import math

import numpy as np

import jax
import jax.numpy as jnp
from jax import lax
from jax.experimental import pallas as pl
from jax.experimental.pallas import tpu as pltpu
from jax.experimental.pallas import tpu_sc as plsc

F32 = jnp.float32
_MXU = jnp.bfloat16
_ACT = jnp.bfloat16
_WIRE = jnp.bfloat16
_HI = lax.Precision.HIGH
_TM = 1024
_TM_GLU = 512
_TM_BIG = 2048
_VMEM_LIMIT = 48 * 1024 * 1024
_SDS = jax.ShapeDtypeStruct

D = 1024
EPS = 1e-6
A_HEADS, A_KV_HEADS, A_HD, BLK = 12, 2, 64, 128
N_BUCKETS, MAX_DISTANCE = 32, 128
B_QK_HEADS, B_V_HEADS, B_HD, B_CONV, CHUNK = 3, 6, 128, 4, 64
X_HEADS, X_HD, MEM_LEN = 4, 64, 256
D_FF, FFN_CONV = 2816, 3
A_Q, A_KV, X_Q = 768, 128, 256
B_QK, B_V, B_QKV = 384, 768, 1536
IN_A, IN_B = 1280, 2572
IN_BP = 2688
BP_Z, BP_XQ, BP_GATE = 1536, 2304, 2560
HALO = 8
GLU_HALO = 16

N_DEV = 8
GU_SHARD = 2 * D_FF // N_DEV
FF_BLOCKS = D_FF // GU_SHARD
DN_SHARD = D_FF // N_DEV
IA_SHARD = IN_A // N_DEV

ADAM_LR, ADAM_B1, ADAM_B2, ADAM_EPS, ADAM_WD, ADAM_STEP = 0.001, 0.9, 0.999, 1e-08, 0.01, 10

SP_CB, SP_CW, SP_QKV, SP_MIX, SP_MEM, SP_FFN, SP_FINAL, SP_MISC, SMALL_ROWS = 0, 2, 8, 12, 14, 16, 18, 19, 24
SP_REL_LANE = B_QKV


def _cp(*sems):
    return pltpu.CompilerParams(dimension_semantics=sems, vmem_limit_bytes=_VMEM_LIMIT)


def _mm(a, b):
    return jnp.dot(a.astype(_MXU), b.astype(_MXU), preferred_element_type=F32)


def _mm_nt(a, b):
    return lax.dot_general(a.astype(_MXU), b.astype(_MXU), (((1,), (1,)), ((), ())), preferred_element_type=F32)


def _mm_tn(a, b):
    return lax.dot_general(a.astype(_MXU), b.astype(_MXU), (((0,), (0,)), ((), ())), preferred_element_type=F32)


def _mmf(a, b):
    return jnp.dot(a, b, preferred_element_type=F32, precision=_HI)


def _mmf_nt(a, b):
    return lax.dot_general(a, b, (((1,), (1,)), ((), ())), preferred_element_type=F32, precision=_HI)


def _silu(x):
    return x * jax.nn.sigmoid(x)


def _w2d(ref):
    v = ref[...]
    return v.reshape(-1, v.shape[-1])


def _rows(m):
    return min(m, _TM)


def _spec_rowsharded(layer, rows, cols, col_block=None):
    if col_block is None:
        return pl.BlockSpec((N_DEV, None, rows, cols), lambda *_: (0, layer, 0, 0))
    return pl.BlockSpec((N_DEV, None, rows, cols), lambda *ids: (0, layer, 0, ids[col_block]))


def _spec_gate_up(axis):
    return pl.BlockSpec((None, GU_SHARD, D), lambda *ids: (ids[axis], 0, 0))


def _spec_down(axis):
    return pl.BlockSpec((2, DN_SHARD, D), lambda *ids: (ids[axis], 0, 0))


def _dep_specs(deps):
    return [pl.BlockSpec(memory_space=pl.ANY) for d in deps]


def _spec_gu_act(row_axis, axis, tm):
    return pl.BlockSpec((None, None, tm, GU_SHARD), lambda *ids: (ids[axis] // FF_BLOCKS, ids[axis] % FF_BLOCKS, ids[row_axis], 0))


def _norm_matmul(x, g, w, w_spec, n_blocks, out_shape, out_spec, name, deps=(), out_dtype=F32, w_t=False, tm=None):
    m, k = x.shape
    tm = tm or _rows(m)

    def body(x_ref, g_ref, w_ref, *rest):
        y_ref, hn_ref = rest[-2:]

        @pl.when(pl.program_id(1) == 0)
        def _():
            xv = x_ref[...]
            r = lax.rsqrt(jnp.mean(xv * xv, axis=-1, keepdims=True) + EPS)
            hn_ref[...] = (xv * r * g_ref[...]).astype(hn_ref.dtype)

        y_ref[...] = (_mm_nt if w_t else _mm)(hn_ref[...], _w2d(w_ref)).astype(y_ref.dtype)

    return pl.pallas_call(
        body, grid=(m // tm, n_blocks),
        in_specs=[pl.BlockSpec((tm, k), lambda i, j: (i, 0)), pl.BlockSpec((1, k), lambda i, j: (0, 0)), w_spec]
        + _dep_specs(deps),
        out_specs=[out_spec, pl.BlockSpec((tm, k), lambda i, j: (i, 0))],
        out_shape=[_SDS(out_shape, out_dtype), _SDS((m, k), _ACT)],
        name=name, compiler_params=_cp("arbitrary", "arbitrary"))(x, g, w, *deps)


def _matmul_res(a, a_spec, w, w_spec, n_k, res, name):
    m, n = res.shape
    tm = _rows(m)

    def body(a_ref, w_ref, r_ref, o_ref):
        part = _mm(a_ref[...], _w2d(w_ref))

        @pl.when(pl.program_id(1) == 0)
        def _():
            o_ref[...] = r_ref[...] + part

        @pl.when(pl.program_id(1) > 0)
        def _():
            o_ref[...] += part

    return pl.pallas_call(
        body, grid=(m // tm, n_k),
        in_specs=[a_spec, w_spec, pl.BlockSpec((tm, n), lambda i, j: (i, 0))],
        out_specs=pl.BlockSpec((tm, n), lambda i, j: (i, 0)),
        out_shape=_SDS((m, n), F32), name=name, compiler_params=_cp("arbitrary", "arbitrary"))(a, w, res)


def _matmul_nt(dy, w, w_spec, n_blocks, out_shape, out_spec, name, deps=(), out_dtype=F32):
    m, n = dy.shape
    tm = _rows(m)

    def body(dy_ref, w_ref, *rest):
        o_ref = rest[-1]
        o_ref[...] = _mm_nt(dy_ref[...], _w2d(w_ref)).astype(o_ref.dtype)

    return pl.pallas_call(
        body, grid=(m // tm, n_blocks),
        in_specs=[pl.BlockSpec((tm, n), lambda i, j: (i, 0)), w_spec] + _dep_specs(deps),
        out_specs=out_spec, out_shape=_SDS(out_shape, out_dtype),
        name=name, compiler_params=_cp("arbitrary", "arbitrary"))(dy, w, *deps)


def _matmul_nt_normbwd(dy, dy_spec, w, w_spec, nj, h, g, dh_in, name, w_t=False):
    m, k = h.shape
    tm = _rows(m)

    def body(dy_ref, w_ref, h_ref, g_ref, dhin_ref, dh_ref, dg_ref, acc_ref):
        i, j = pl.program_id(0), pl.program_id(1)

        @pl.when(j == 0)
        def _():
            acc_ref[...] = jnp.zeros_like(acc_ref)

        acc_ref[...] += (_mm if w_t else _mm_nt)(dy_ref[...], _w2d(w_ref))

        @pl.when(j == nj - 1)
        def _():
            xv = h_ref[...]
            r = lax.rsqrt(jnp.mean(xv * xv, axis=-1, keepdims=True) + EPS)
            xh = xv * r
            dhn = acc_ref[...]
            part = jnp.sum(dhn * xh, axis=0, keepdims=True)

            @pl.when(i == 0)
            def _():
                dg_ref[...] = part

            @pl.when(i > 0)
            def _():
                dg_ref[...] += part

            t = dhn * g_ref[...]
            dh_ref[...] = dhin_ref[...] + r * (t - xh * jnp.mean(t * xh, axis=-1, keepdims=True))

    return pl.pallas_call(
        body, grid=(m // tm, nj),
        in_specs=[dy_spec, w_spec, pl.BlockSpec((tm, k), lambda i, j: (i, 0)), pl.BlockSpec((1, k), lambda i, j: (0, 0)),
                  pl.BlockSpec((tm, k), lambda i, j: (i, 0))],
        out_specs=[pl.BlockSpec((tm, k), lambda i, j: (i, 0)), pl.BlockSpec((1, k), lambda i, j: (0, 0))],
        out_shape=[_SDS((m, k), F32), _SDS((1, k), F32)],
        scratch_shapes=[pltpu.VMEM((tm, k), F32)],
        name=name, compiler_params=_cp("arbitrary", "arbitrary"))(dy, w, h, g, dh_in)


def _matmul_tn(x, x_spec, dy, dy_spec, m, n_blocks, acc_shape, out_shape, out_spec, name, tm=None):
    tm = tm or _rows(m)
    nm = m // tm

    def body(x_ref, dy_ref, o_ref, acc_ref):
        @pl.when(pl.program_id(1) == 0)
        def _():
            acc_ref[...] = jnp.zeros_like(acc_ref)

        acc_ref[...] += _mm_tn(x_ref[...], dy_ref[...])

        @pl.when(pl.program_id(1) == nm - 1)
        def _():
            o_ref[...] = acc_ref[...].reshape(o_ref.shape).astype(o_ref.dtype)

    return pl.pallas_call(
        body, grid=(n_blocks, nm), in_specs=[x_spec, dy_spec], out_specs=out_spec,
        out_shape=_SDS(out_shape, _WIRE), scratch_shapes=[pltpu.VMEM(acc_shape, F32)],
        name=name, compiler_params=_cp("arbitrary", "arbitrary"))(x, dy)


def _loss_head(h, g, tgt, name):
    m, k = h.shape
    tm = _rows(m)

    def body(h_ref, g_ref, t_ref, loss_ref, dh_ref, dg_ref):
        i = pl.program_id(0)
        xv = h_ref[...]
        r = lax.rsqrt(jnp.mean(xv * xv, axis=-1, keepdims=True) + EPS)
        xh = xv * r
        gv = g_ref[...]
        err = xh * gv - t_ref[...]
        lpart = jnp.zeros((1, 128), F32) + 0.5 * jnp.sum(jnp.mean(err * err, axis=-1, keepdims=True), axis=0, keepdims=True)
        dy = err * (1.0 / k)
        gpart = jnp.sum(dy * xh, axis=0, keepdims=True)

        @pl.when(i == 0)
        def _():
            loss_ref[...] = lpart
            dg_ref[...] = gpart

        @pl.when(i > 0)
        def _():
            loss_ref[...] += lpart
            dg_ref[...] += gpart

        t = dy * gv
        dh_ref[...] = r * (t - xh * jnp.mean(t * xh, axis=-1, keepdims=True))

    return pl.pallas_call(
        body, grid=(m // tm,),
        in_specs=[pl.BlockSpec((tm, k), lambda i: (i, 0)), pl.BlockSpec((1, k), lambda i: (0, 0)),
                  pl.BlockSpec((tm, k), lambda i: (i, 0))],
        out_specs=[pl.BlockSpec((1, 128), lambda i: (0, 0)), pl.BlockSpec((tm, k), lambda i: (i, 0)),
                   pl.BlockSpec((1, k), lambda i: (0, 0))],
        out_shape=[_SDS((1, 128), F32), _SDS((m, k), F32), _SDS((1, k), F32)],
        name=name, compiler_params=_cp("arbitrary"))(h, g, tgt)


def _glu_down(gu, conv_w, conv_b, w_down, res, name):
    s = gu.shape[2]
    tm = min(s, _TM_GLU)

    def body(gu_ref, prev_ref, w_ref, b_ref, wdn_ref, r_ref, o_ref, act_ref, gc_ref):
        i, j = pl.program_id(0), pl.program_id(1)
        prev = jnp.where(i > 0, prev_ref[...].astype(F32), 0.0)
        ext = jnp.concatenate([prev, gu_ref[0].astype(F32)], axis=0)
        gc = b_ref[...] + w_ref[FFN_CONV - 1:FFN_CONV, :] * ext
        for k in range(FFN_CONV - 1):
            gc = gc + w_ref[k:k + 1, :] * pltpu.roll(ext, FFN_CONV - 1 - k, 0)
        gc = gc[GLU_HALO:]
        gc_ref[...] = gc
        act = (_silu(gc) * gu_ref[1].astype(F32)).astype(act_ref.dtype)
        act_ref[...] = act
        part = _mm(act, _w2d(wdn_ref))

        @pl.when(j == 0)
        def _():
            o_ref[...] = r_ref[...] + part

        @pl.when(j > 0)
        def _():
            o_ref[...] += part

    return pl.pallas_call(
        body, grid=(s // tm, FF_BLOCKS),
        in_specs=[pl.BlockSpec((2, None, tm, GU_SHARD), lambda i, j: (0, j, i, 0)),
                  pl.BlockSpec((None, None, GLU_HALO, GU_SHARD),
                               lambda i, j: (0, j, jnp.maximum(i * (tm // GLU_HALO) - 1, 0), 0)),
                  pl.BlockSpec((None, HALO, GU_SHARD), lambda i, j: (j, 0, 0)),
                  pl.BlockSpec((None, 1, GU_SHARD), lambda i, j: (j, 0, 0)),
                  _spec_down(1), pl.BlockSpec((tm, D), lambda i, j: (i, 0))],
        out_specs=[pl.BlockSpec((tm, D), lambda i, j: (i, 0)), pl.BlockSpec((None, tm, GU_SHARD), lambda i, j: (j, i, 0)),
                   pl.BlockSpec((None, tm, GU_SHARD), lambda i, j: (j, i, 0))],
        out_shape=[_SDS((s, D), F32), _SDS((FF_BLOCKS, s, GU_SHARD), _ACT), _SDS((FF_BLOCKS, s, GU_SHARD), F32)], name=name,
        compiler_params=_cp("arbitrary", "arbitrary"))(gu, gu, conv_w, conv_b, w_down, res)


def _glu_bwd(gu, gc, conv_w, dh, w_down, name, deps=()):
    s = gu.shape[2]
    tm = min(s, _TM_GLU)
    nt = s // tm
    ext_rows = tm + GLU_HALO

    def body(gu_ref, prev_ref, gc_ref, w_ref, dh_ref, wdn_ref, *rest):
        dgu_ref, dw_ref, db_ref, carry_ref = rest[-4:]
        t = pl.program_id(1)
        i = nt - 1 - t

        @pl.when(t == 0)
        def _():
            carry_ref[...] = jnp.zeros_like(carry_ref)
            dw_ref[...] = jnp.zeros_like(dw_ref)
            db_ref[...] = jnp.zeros_like(db_ref)

        up = gu_ref[1].astype(F32)
        prev = jnp.where(i > 0, prev_ref[...].astype(F32), 0.0)
        ext = jnp.concatenate([prev, gu_ref[0].astype(F32)], axis=0)
        gc = gc_ref[...]
        sg = jax.nn.sigmoid(gc)
        da = _mm_nt(dh_ref[...], _w2d(wdn_ref))
        dup = da * (gc * sg)
        dgc = da * up * (sg * (1.0 + gc * (1.0 - sg)))
        db_ref[...] += jnp.sum(dgc, axis=0, keepdims=True)
        dgc_ext = jnp.concatenate([jnp.zeros((GLU_HALO, GU_SHARD), F32), dgc], axis=0)
        ahead = [pltpu.roll(dgc_ext, ext_rows - (FFN_CONV - 1 - j), 0) if j < FFN_CONV - 1 else dgc_ext
                 for j in range(FFN_CONV)]
        dext = ahead[0] * w_ref[0:1, :]
        for j in range(FFN_CONV):
            dw_ref[j:j + 1, :] += jnp.sum(ext * ahead[j], axis=0, keepdims=True)
            if j > 0:
                dext = dext + ahead[j] * w_ref[j:j + 1, :]
        tail = jnp.concatenate([jnp.zeros((tm - GLU_HALO, GU_SHARD), F32), carry_ref[...]], axis=0)
        dgate = dext[GLU_HALO:] + tail
        carry_ref[...] = dext[:GLU_HALO]
        dgu_ref[0] = dgate.astype(dgu_ref.dtype)
        dgu_ref[1] = dup.astype(dgu_ref.dtype)

    return pl.pallas_call(
        body, grid=(FF_BLOCKS, nt),
        in_specs=[pl.BlockSpec((2, None, tm, GU_SHARD), lambda j, t: (0, j, nt - 1 - t, 0)),
                  pl.BlockSpec((None, None, GLU_HALO, GU_SHARD),
                               lambda j, t: (0, j, jnp.maximum((nt - 1 - t) * (tm // GLU_HALO) - 1, 0), 0)),
                  pl.BlockSpec((None, tm, GU_SHARD), lambda j, t: (j, nt - 1 - t, 0)),
                  pl.BlockSpec((None, HALO, GU_SHARD), lambda j, t: (j, 0, 0)),
                  pl.BlockSpec((tm, D), lambda j, t: (nt - 1 - t, 0)), _spec_down(0)] + _dep_specs(deps),
        out_specs=[pl.BlockSpec((2, None, tm, GU_SHARD), lambda j, t: (0, j, nt - 1 - t, 0)),
                   pl.BlockSpec((None, HALO, GU_SHARD), lambda j, t: (j, 0, 0)),
                   pl.BlockSpec((None, 1, GU_SHARD), lambda j, t: (j, 0, 0))],
        out_shape=[_SDS(gu.shape, _ACT), _SDS((FF_BLOCKS, HALO, GU_SHARD), F32), _SDS((FF_BLOCKS, 1, GU_SHARD), F32)],
        scratch_shapes=[pltpu.VMEM((GLU_HALO, GU_SHARD), F32)],
        name=name, compiler_params=_cp("arbitrary", "arbitrary"))(gu, gu, gc, conv_w, dh, w_down, *deps)


def _bucket_table():
    qi = np.arange(BLK)[:, None]
    kj = np.arange(BLK)[None, :]
    n = np.where(kj > qi, BLK + qi - kj, qi - kj)
    max_exact = N_BUCKETS // 2
    nf = np.maximum(n, 1).astype(np.float32)
    large = max_exact + (np.log(nf / max_exact) / math.log(MAX_DISTANCE / max_exact)
                         * (N_BUCKETS - max_exact)).astype(np.int32)
    large = np.minimum(large, N_BUCKETS - 1)
    return np.where(n < max_exact, n, large).astype(np.int32)


def _lane_low():
    return lax.broadcasted_iota(jnp.int32, (1, 128), 1) < A_HD


def _swa_groups(q, kd, vd, sink, bias, upper, first):
    n = A_HEADS // A_KV_HEADS
    ng = A_KV_HEADS
    low = _lane_low()
    qm = [jnp.concatenate([jnp.where(low == (h % 2 == 0), q[g][:, (h // 2) * 128:(h // 2 + 1) * 128], 0.0) for h in range(n)], axis=0)
          for g in range(ng)]
    s2 = [_mm_nt(qm[g], kd[g]) * (A_HD ** -0.5) for g in range(ng)]
    s = [jnp.where(upper[None], s2[g][:, :BLK].reshape(n, BLK, BLK), s2[g][:, BLK:].reshape(n, BLK, BLK)) + bias[g] for g in range(ng)]
    s = [jnp.where((upper & first)[None], -jnp.inf, t) for t in s]
    m = [lax.stop_gradient(jnp.maximum(jnp.max(s[g], axis=-1, keepdims=True), sink[g])) for g in range(ng)]
    p = [jnp.exp(s[g] - m[g]) for g in range(ng)]
    split = [jnp.concatenate([jnp.where(upper[None], t, 0.0), jnp.where(upper[None], 0.0, t)], axis=-1).reshape(n * BLK, 2 * BLK)
             for t in p]
    ones = jnp.ones((BLK, 128), F32)
    den = [_mm(p[g].reshape(n * BLK, BLK), ones) + jnp.exp(sink[g] - m[g]).reshape(n * BLK, 1) for g in range(ng)]
    o = [_mm(split[g], vd[g]) / den[g] for g in range(ng)]
    return [jnp.concatenate([jnp.where(low, t[2 * k * BLK:(2 * k + 1) * BLK], t[(2 * k + 1) * BLK:(2 * k + 2) * BLK])
                             for k in range(n // 2)], axis=1) for t in o]


def _mix_a_core(q, kd, vd, sink, bias, xq, mk, mv, upper, first):
    return _swa_groups(q, kd, vd, sink, bias, upper, first), _cross_pairs(xq, mk, mv)


def _swa_sinks(sink_ref, g):
    n = A_HEADS // A_KV_HEADS
    return jnp.concatenate([sink_ref[:, h:h + 1] for h in range(g * n, (g + 1) * n)], axis=0).reshape(n, 1, 1)


def _both_halves(t, t_rolled, g):
    low = _lane_low()
    return jnp.where(low, t, t_rolled) if g == 0 else jnp.where(low, t_rolled, t)


def _cross_pairs(q, mk, mv):
    rows = q.shape[0]
    low = _lane_low()
    qm = [jnp.concatenate([jnp.where(low, q[:, p * 128:(p + 1) * 128], 0.0), jnp.where(low, 0.0, q[:, p * 128:(p + 1) * 128])], axis=0)
          for p in range(X_HEADS // 2)]
    s = [_mm_nt(qm[p], mk[:, p * 128:(p + 1) * 128]) * (X_HD ** -0.5) for p in range(X_HEADS // 2)]
    e = [jnp.exp(t - lax.stop_gradient(jnp.max(t, axis=-1, keepdims=True))) for t in s]
    pr = [t / jnp.sum(t, axis=-1, keepdims=True) for t in e]
    o = [_mm(pr[p], mv[:, p * 128:(p + 1) * 128]) for p in range(X_HEADS // 2)]
    return jnp.concatenate([jnp.where(low, t[:rows], t[rows:]) for t in o], axis=1)


def _swa_upper():
    qi = lax.broadcasted_iota(jnp.int32, (BLK, BLK), 0)
    kj = lax.broadcasted_iota(jnp.int32, (BLK, BLK), 1)
    return kj > qi


def _bias_build(rel_bias, bucket, name):
    def body(rb_ref, bucket_ref, o_ref):
        b = bucket_ref[...]
        for h in range(A_HEADS):
            acc = jnp.zeros((BLK, BLK), F32)
            for k in range(N_BUCKETS):
                acc = jnp.where(b == k, rb_ref[k, h], acc)
            o_ref[h] = acc

    return pl.pallas_call(
        body, in_specs=[pl.BlockSpec(memory_space=pltpu.SMEM), pl.BlockSpec(memory_space=pltpu.VMEM)],
        out_specs=pl.BlockSpec(memory_space=pltpu.VMEM),
        out_shape=_SDS((A_HEADS, BLK, BLK), F32), name=name)(rel_bias, bucket)


def _bias_reduce(dbias, bucket, name):
    def body(db_ref, bucket_ref, o_ref):
        b = bucket_ref[...]
        row = lax.broadcasted_iota(jnp.int32, (N_BUCKETS, 128), 0)
        lane = lax.broadcasted_iota(jnp.int32, (N_BUCKETS, 128), 1)
        acc = jnp.zeros((N_BUCKETS, 128), F32)
        for h in range(A_HEADS):
            v = db_ref[h]
            for k in range(N_BUCKETS):
                sk = jnp.sum(jnp.sum(jnp.where(b == k, v, 0.0), axis=1, keepdims=True), axis=0, keepdims=True)
                acc = acc + jnp.where((row == k) & (lane == h), sk, 0.0)
        o_ref[...] = acc

    return pl.pallas_call(
        body, in_specs=[pl.BlockSpec(memory_space=pltpu.VMEM)] * 2,
        out_specs=pl.BlockSpec(memory_space=pltpu.VMEM),
        out_shape=_SDS((N_BUCKETS, 128), F32), name=name)(dbias, bucket)


def _mix_a_fwd(proj, bias, sinks, memkv, name):
    s = proj.shape[0]
    nb = s // BLK
    grp = A_HEADS // A_KV_HEADS

    def body(proj_ref, prev_ref, bias_ref, sink_ref, memkv_ref, o_ref):
        i = pl.program_id(0)
        upper = _swa_upper()
        prev = prev_ref[...].astype(F32)
        proj = proj_ref[...].astype(F32)
        kb = jnp.concatenate([prev[:, :A_KV], proj[:, A_Q:A_Q + A_KV]], axis=0)
        vb = jnp.concatenate([prev[:, A_KV:], proj[:, A_Q + A_KV:A_Q + 2 * A_KV]], axis=0)
        kb_r = pltpu.roll(kb, A_HD, 1)
        vb_r = pltpu.roll(vb, A_HD, 1)
        gw = A_Q // A_KV_HEADS
        groups = range(A_KV_HEADS)
        swa, cross = _mix_a_core([proj[:, g * gw:(g + 1) * gw] for g in groups], [_both_halves(kb, kb_r, g) for g in groups],
                                 [_both_halves(vb, vb_r, g) for g in groups], [_swa_sinks(sink_ref, g) for g in groups],
                                 [bias_ref[g * grp:(g + 1) * grp] for g in groups], proj[:, A_Q + 2 * A_KV:],
                                 memkv_ref[:, :X_Q], memkv_ref[:, X_Q:], upper, i == 0)
        o_ref[...] = jnp.concatenate(swa + [cross], axis=1).astype(o_ref.dtype)

    return pl.pallas_call(
        body, grid=(nb,),
        in_specs=[pl.BlockSpec((BLK, IN_A), lambda i: (i, 0)),
                  pl.BlockSpec((BLK, 2 * A_KV), lambda i: (jnp.maximum(i - 1, 0), A_Q // (2 * A_KV))),
                  pl.BlockSpec((A_HEADS, BLK, BLK), lambda i: (0, 0, 0)),
                  pl.BlockSpec((1, 128), lambda i: (0, 0)),
                  pl.BlockSpec((MEM_LEN, 2 * X_Q), lambda i: (0, 0))],
        out_specs=pl.BlockSpec((BLK, D), lambda i: (i, 0)),
        out_shape=_SDS((s, D), _ACT), name=name, compiler_params=_cp("arbitrary"))(proj, proj, bias, sinks, memkv)


def _mix_a_bwd(proj, bias, sinks, memkv, dmix, name, deps=()):
    s = proj.shape[0]
    nb = s // BLK
    grp = A_HEADS // A_KV_HEADS

    def body(proj_ref, prev_ref, bias_ref, sink_ref, memkv_ref, dmix_ref, *rest):
        dproj_ref, dbias_ref, dsink_ref, dmemkv_ref, carry_ref = rest[-5:]
        t = pl.program_id(0)
        i = nb - 1 - t

        @pl.when(t == 0)
        def _():
            carry_ref[...] = jnp.zeros_like(carry_ref)
            dbias_ref[...] = jnp.zeros_like(dbias_ref)
            dsink_ref[...] = jnp.zeros_like(dsink_ref)
            dmemkv_ref[...] = jnp.zeros_like(dmemkv_ref)

        upper = _swa_upper()
        lane = lax.broadcasted_iota(jnp.int32, (1, 128), 1)
        low = _lane_low()
        prev = prev_ref[...].astype(F32)
        proj = proj_ref[...].astype(F32)
        kb = jnp.concatenate([prev[:, :A_KV], proj[:, A_Q:A_Q + A_KV]], axis=0)
        vb = jnp.concatenate([prev[:, A_KV:], proj[:, A_Q + A_KV:A_Q + 2 * A_KV]], axis=0)
        kb_r = pltpu.roll(kb, A_HD, 1)
        vb_r = pltpu.roll(vb, A_HD, 1)
        gw = A_Q // A_KV_HEADS
        groups = range(A_KV_HEADS)
        _, vjp = jax.vjp(
            functools.partial(_mix_a_core, upper=upper, first=i == 0),
            [proj[:, g * gw:(g + 1) * gw] for g in groups], [_both_halves(kb, kb_r, g) for g in groups],
            [_both_halves(vb, vb_r, g) for g in groups], [_swa_sinks(sink_ref, g) for g in groups],
            [bias_ref[g * grp:(g + 1) * grp] for g in groups], proj[:, A_Q + 2 * A_KV:], memkv_ref[:, :X_Q], memkv_ref[:, X_Q:])
        dqs, dk, dv, ds, db, dxq, dmk, dmv = vjp(
            ([dmix_ref[:, g * gw:(g + 1) * gw].astype(F32) for g in groups], dmix_ref[:, A_Q:].astype(F32)))
        dkd = [t + pltpu.roll(t, A_HD, 1) for t in dk]
        dvd = [t + pltpu.roll(t, A_HD, 1) for t in dv]
        dsink = jnp.zeros((1, 128), F32)
        for g in groups:
            for h in range(grp):
                dsink = dsink + jnp.where(lane == g * grp + h, ds[g][h], 0.0)
            dbias_ref[g * grp:(g + 1) * grp] += db[g]
        dsink_ref[...] += dsink
        dkb = jnp.where(low, dkd[0], dkd[1])
        dvb = jnp.where(low, dvd[0], dvd[1])
        dmemkv_ref[...] += jnp.concatenate([dmk, dmv], axis=1)
        dkv_cur = jnp.concatenate([dkb[BLK:], dvb[BLK:]], axis=1) + carry_ref[...]
        carry_ref[...] = jnp.concatenate([dkb[:BLK], dvb[:BLK]], axis=1)
        dproj_ref[...] = jnp.concatenate(list(dqs) + [dkv_cur, dxq], axis=1).astype(dproj_ref.dtype)

    return pl.pallas_call(
        body, grid=(nb,),
        in_specs=[pl.BlockSpec((BLK, IN_A), lambda t: (nb - 1 - t, 0)),
                  pl.BlockSpec((BLK, 2 * A_KV), lambda t: (jnp.maximum(nb - 2 - t, 0), A_Q // (2 * A_KV))),
                  pl.BlockSpec((A_HEADS, BLK, BLK), lambda t: (0, 0, 0)),
                  pl.BlockSpec((1, 128), lambda t: (0, 0)),
                  pl.BlockSpec((MEM_LEN, 2 * X_Q), lambda t: (0, 0)),
                  pl.BlockSpec((BLK, D), lambda t: (nb - 1 - t, 0))] + _dep_specs(deps),
        out_specs=[pl.BlockSpec((BLK, IN_A), lambda t: (nb - 1 - t, 0)),
                   pl.BlockSpec((A_HEADS, BLK, BLK), lambda t: (0, 0, 0)),
                   pl.BlockSpec((1, 128), lambda t: (0, 0)),
                   pl.BlockSpec((MEM_LEN, 2 * X_Q), lambda t: (0, 0))],
        out_shape=[_SDS((s, IN_A), _ACT), _SDS((A_HEADS, BLK, BLK), F32), _SDS((1, 128), F32),
                   _SDS((MEM_LEN, 2 * X_Q), F32)],
        scratch_shapes=[pltpu.VMEM((BLK, 2 * A_KV), F32)],
        name=name, compiler_params=_cp("arbitrary"))(proj, proj, bias, sinks, memkv, dmix, *deps)


def _neumann(pw, rhs):
    nh = len(pw)
    x = rhs
    for lvl in range(6):
        if lvl < 5:
            prod = [_mmf(pw[h], jnp.concatenate([x[h], pw[h]], axis=1)) for h in range(nh)]
            x = [x[h] + prod[h][:, :B_HD] for h in range(nh)]
            pw = [t[:, B_HD:] for t in prod]
        else:
            x = [x[h] + _mmf(pw[h], x[h]) for h in range(nh)]
    return x


@jax.custom_vjp
def _tri_solve(pw, rhs):
    return _neumann(pw, rhs)


def _tri_solve_fwd(pw, rhs):
    x = _neumann(pw, rhs)
    return x, (pw, x)


def _tri_solve_bwd(res, dx):
    pw, x = res
    d_rhs = _neumann([t.T for t in pw], list(dx))
    return [_mmf_nt(d_rhs[h], x[h]) for h in range(len(pw))], d_rhs


_tri_solve.defvjp(_tri_solve_fwd, _tri_solve_bwd)


@jax.custom_vjp
def _tri_solved(pw, rhs, x):
    return x


def _tri_solved_fwd(pw, rhs, x):
    return x, (pw, x)


def _tri_solved_bwd(res, dx):
    d_pw, d_rhs = _tri_solve_bwd(res, dx)
    return d_pw, d_rhs, [jnp.zeros_like(t) for t in res[1]]


_tri_solved.defvjp(_tri_solved_fwd, _tri_solved_bwd)


@jax.custom_vjp
def _known(x, value):
    return value


def _known_fwd(x, value):
    return value, None


def _known_bwd(_, g):
    return g, jnp.zeros_like(g)


_known.defvjp(_known_fwd, _known_bwd)


def _dn_heads(yq, yk, yv, z, bl, al, a_log, dtb, ng, s0, solved=None, out_known=None):
    c = CHUNK
    nh = B_V_HEADS
    rep = B_V_HEADS // B_QK_HEADS
    r = lax.broadcasted_iota(jnp.int32, (c, c), 0)
    cc = lax.broadcasted_iota(jnp.int32, (c, c), 1)
    q = [_silu(t) for t in yq]
    k = [_silu(t) for t in yk]
    v = [_silu(t) for t in yv]
    q = [t * lax.rsqrt(jnp.sum(t * t, axis=-1, keepdims=True) + EPS) * (B_HD ** -0.5) for t in q]
    k = [t * lax.rsqrt(jnp.sum(t * t, axis=-1, keepdims=True) + EPS) for t in k]
    beta = [jax.nn.sigmoid(t) for t in bl]
    g = [-jnp.exp(a_log[h]) * jax.nn.softplus(al[h] + dtb[h]) for h in range(nh)]
    gb = [jnp.broadcast_to(t, (c, c)) for t in g]
    gc_col = [jnp.sum(jnp.where(cc <= r, t.T, 0.0), axis=1, keepdims=True) for t in gb]
    gc_row = [jnp.sum(jnp.where(r <= cc, t, 0.0), axis=0, keepdims=True) for t in gb]
    gc_last = [jnp.sum(t, axis=0, keepdims=True) for t in g]
    decay = [jnp.exp(jnp.where(r >= cc, gc_col[h] - gc_row[h], -jnp.inf)) for h in range(nh)]
    kq = [_mmf_nt(jnp.concatenate([k[h], q[h]], axis=0), k[h]) for h in range(B_QK_HEADS)]
    kk = [t[:c] for t in kq]
    qk = [t[c:] for t in kq]
    egc = [jnp.exp(t) for t in gc_col]
    both = [_mmf(jnp.concatenate([(beta[h] * egc[h]) * k[h // rep], q[h // rep] * egc[h]], axis=0), s0[h]) for h in range(nh)]
    rhs = [beta[h] * v[h] - both[h][:c] for h in range(nh)]
    qs0 = [t[c:] for t in both]
    pw = [-(beta[h] * kk[h // rep] * jnp.where(r > cc, decay[h], 0.0)) for h in range(nh)]
    delta = _tri_solve(pw, rhs) if solved is None else _tri_solved(pw, rhs, solved)
    last = [_mmf(jnp.concatenate([qk[h // rep] * decay[h], (k[h // rep] * jnp.exp(gc_last[h] - gc_col[h])).T], axis=0), delta[h])
            for h in range(nh)]
    out = [qs0[h] + last[h][:c] for h in range(nh)]
    if out_known is not None:
        out = [_known(out[h], out_known[h]) for h in range(nh)]
    s1 = [jnp.exp(gc_last[h]) * s0[h] + last[h][c:] for h in range(nh)]
    o = [t * lax.rsqrt(jnp.mean(t * t, axis=-1, keepdims=True) + EPS) * ng for t in out]
    return [o[h] * _silu(z[h]) for h in range(nh)], s1, delta, out


def _dn_conv(ext, w_ref):
    y = ext * w_ref[B_CONV - 1:B_CONV, :]
    for j in range(B_CONV - 1):
        y = y + w_ref[j:j + 1, :] * pltpu.roll(ext, B_CONV - 1 - j, 0)
    return y


def _dn_args(y, cur_ref, par_ref, ng_ref):
    nh = B_V_HEADS
    return ([y[:, h * B_HD:(h + 1) * B_HD] for h in range(B_QK_HEADS)],
            [y[:, B_QK + h * B_HD:B_QK + (h + 1) * B_HD] for h in range(B_QK_HEADS)],
            [y[:, 2 * B_QK + h * B_HD:2 * B_QK + (h + 1) * B_HD] for h in range(nh)],
            [cur_ref[:, BP_Z + h * B_HD:BP_Z + (h + 1) * B_HD] for h in range(nh)],
            [cur_ref[:, BP_GATE + h:BP_GATE + h + 1] for h in range(nh)],
            [cur_ref[:, BP_GATE + nh + h:BP_GATE + nh + h + 1] for h in range(nh)],
            [par_ref[:, h:h + 1] for h in range(nh)], [par_ref[:, nh + h:nh + h + 1] for h in range(nh)], ng_ref[...])


def _mix_b_fwd(proj, conv_w, par, ng, memkv, name):
    s = proj.shape[0]
    nc = s // CHUNK

    def body(cur_ref, prev_ref, w_ref, par_ref, ng_ref, memkv_ref, o_ref, st_ref, dl_ref, state_ref):
        n = pl.program_id(0)

        @pl.when(n == 0)
        def _():
            state_ref[...] = jnp.zeros_like(state_ref)

        prev = jnp.where(n > 0, prev_ref[...], 0.0)
        ext = jnp.concatenate([prev, cur_ref[:, :B_QKV]], axis=0)
        y = _dn_conv(ext, w_ref)[HALO:]
        s0 = [state_ref[hv] for hv in range(B_V_HEADS)]
        st_ref[0] = state_ref[...]
        outs, s1, delta, raw = _dn_heads(*_dn_args(y, cur_ref, par_ref, ng_ref), s0)
        for hv in range(B_V_HEADS):
            state_ref[hv] = s1[hv]
            dl_ref[0, hv] = delta[hv]
            dl_ref[0, B_V_HEADS + hv] = raw[hv]
        outs = outs + [_cross_pairs(cur_ref[:, BP_XQ:BP_XQ + X_Q], memkv_ref[:, :X_Q], memkv_ref[:, X_Q:])]
        o_ref[...] = jnp.concatenate(outs, axis=1).astype(o_ref.dtype)

    return pl.pallas_call(
        body, grid=(nc,),
        in_specs=[pl.BlockSpec((CHUNK, IN_BP), lambda n: (n, 0)),
                  pl.BlockSpec((HALO, B_QKV), lambda n: (jnp.maximum(n * (CHUNK // HALO) - 1, 0), 0)),
                  pl.BlockSpec((HALO, B_QKV), lambda n: (0, 0)),
                  pl.BlockSpec((1, 128), lambda n: (0, 0)), pl.BlockSpec((1, 128), lambda n: (0, 0)),
                  pl.BlockSpec((MEM_LEN, 2 * X_Q), lambda n: (0, 0))],
        out_specs=[pl.BlockSpec((CHUNK, D), lambda n: (n, 0)),
                   pl.BlockSpec((1, B_V_HEADS, B_HD, B_HD), lambda n: (n, 0, 0, 0)),
                   pl.BlockSpec((1, 2 * B_V_HEADS, CHUNK, B_HD), lambda n: (n, 0, 0, 0))],
        out_shape=[_SDS((s, D), _ACT), _SDS((nc, B_V_HEADS, B_HD, B_HD), F32), _SDS((nc, 2 * B_V_HEADS, CHUNK, B_HD), F32)],
        scratch_shapes=[pltpu.VMEM((B_V_HEADS, B_HD, B_HD), F32)],
        name=name, compiler_params=_cp("arbitrary"))(proj, proj, conv_w, par, ng, memkv)


def _mix_b_bwd(proj, conv_w, par, ng, memkv, states, deltas, dmix, name):
    s = proj.shape[0]
    nc = s // CHUNK
    ext_rows = CHUNK + HALO

    def body(cur_ref, prev_ref, w_ref, par_ref, ng_ref, memkv_ref, st_ref, dl_ref, dmix_ref,
             dproj_ref, dw_ref, dpar_ref, dng_ref, dmemkv_ref, dstate_ref, carry_ref):
        t = pl.program_id(0)
        n = nc - 1 - t

        @pl.when(t == 0)
        def _():
            dstate_ref[...] = jnp.zeros_like(dstate_ref)
            carry_ref[...] = jnp.zeros_like(carry_ref)
            dw_ref[...] = jnp.zeros_like(dw_ref)
            dpar_ref[...] = jnp.zeros_like(dpar_ref)
            dng_ref[...] = jnp.zeros_like(dng_ref)
            dmemkv_ref[...] = jnp.zeros_like(dmemkv_ref)

        lane = lax.broadcasted_iota(jnp.int32, (1, 128), 1)
        prev = jnp.where(n > 0, prev_ref[...], 0.0)
        ext = jnp.concatenate([prev, cur_ref[:, :B_QKV]], axis=0)
        y = _dn_conv(ext, w_ref)[HALO:]
        solved = [dl_ref[0, hv] for hv in range(B_V_HEADS)]
        raw = [dl_ref[0, B_V_HEADS + hv] for hv in range(B_V_HEADS)]
        _, vjp = jax.vjp(functools.partial(_dn_heads, solved=solved, out_known=raw), *_dn_args(y, cur_ref, par_ref, ng_ref),
                         [st_ref[0, hv] for hv in range(B_V_HEADS)])
        none = [jnp.zeros((CHUNK, B_HD), F32)] * B_V_HEADS
        dyq, dyk, dyv, dz, gbl, gal, ga_log, gdtb, dng, gs0 = vjp(
            ([dmix_ref[:, hv * B_HD:(hv + 1) * B_HD].astype(F32) for hv in range(B_V_HEADS)],
             [dstate_ref[hv] for hv in range(B_V_HEADS)], none, none))
        dgate = jnp.zeros((CHUNK, 128), F32)
        dpar = jnp.zeros((1, 128), F32)
        for hv in range(B_V_HEADS):
            dstate_ref[hv] = gs0[hv]
            dgate = dgate + jnp.where(lane == hv, gbl[hv], 0.0) + jnp.where(lane == B_V_HEADS + hv, gal[hv], 0.0)
            dpar = dpar + jnp.where(lane == hv, ga_log[hv], 0.0) + jnp.where(lane == B_V_HEADS + hv, gdtb[hv], 0.0)
        dpar_ref[...] += dpar
        dng_ref[...] += dng
        _, vjp = jax.vjp(_cross_pairs, cur_ref[:, BP_XQ:BP_XQ + X_Q], memkv_ref[:, :X_Q], memkv_ref[:, X_Q:])
        dxq, dmk, dmv = vjp(dmix_ref[:, B_V:].astype(F32))
        dmemkv_ref[...] += jnp.concatenate([dmk, dmv], axis=1)
        dy = jnp.concatenate(list(dyq) + list(dyk) + list(dyv), axis=1)
        dy_ext = jnp.concatenate([jnp.zeros((HALO, B_QKV), F32), dy], axis=0)
        dext = dy_ext * w_ref[B_CONV - 1:B_CONV, :]
        dw_ref[B_CONV - 1:B_CONV, :] += jnp.sum(ext * dy_ext, axis=0, keepdims=True)
        for j in range(B_CONV - 1):
            sh = B_CONV - 1 - j
            dw_ref[j:j + 1, :] += jnp.sum(pltpu.roll(ext, sh, 0) * dy_ext, axis=0, keepdims=True)
            dext = dext + w_ref[j:j + 1, :] * pltpu.roll(dy_ext, ext_rows - sh, 0)
        tail = jnp.concatenate([jnp.zeros((CHUNK - HALO, B_QKV), F32), carry_ref[...]], axis=0)
        dqkv = dext[HALO:] + tail
        carry_ref[...] = dext[:HALO]
        dproj_ref[...] = jnp.concatenate([dqkv] + list(dz) + [dxq, dgate], axis=1).astype(dproj_ref.dtype)

    return pl.pallas_call(
        body, grid=(nc,),
        in_specs=[pl.BlockSpec((CHUNK, IN_BP), lambda t: (nc - 1 - t, 0)),
                  pl.BlockSpec((HALO, B_QKV), lambda t: (jnp.maximum((nc - 1 - t) * (CHUNK // HALO) - 1, 0), 0)),
                  pl.BlockSpec((HALO, B_QKV), lambda t: (0, 0)),
                  pl.BlockSpec((1, 128), lambda t: (0, 0)), pl.BlockSpec((1, 128), lambda t: (0, 0)),
                  pl.BlockSpec((MEM_LEN, 2 * X_Q), lambda t: (0, 0)),
                  pl.BlockSpec((1, B_V_HEADS, B_HD, B_HD), lambda t: (nc - 1 - t, 0, 0, 0)),
                  pl.BlockSpec((1, 2 * B_V_HEADS, CHUNK, B_HD), lambda t: (nc - 1 - t, 0, 0, 0)),
                  pl.BlockSpec((CHUNK, D), lambda t: (nc - 1 - t, 0))],
        out_specs=[pl.BlockSpec((CHUNK, IN_BP), lambda t: (nc - 1 - t, 0)),
                   pl.BlockSpec((HALO, B_QKV), lambda t: (0, 0)),
                   pl.BlockSpec((1, 128), lambda t: (0, 0)), pl.BlockSpec((1, 128), lambda t: (0, 0)),
                   pl.BlockSpec((MEM_LEN, 2 * X_Q), lambda t: (0, 0))],
        out_shape=[_SDS((s, IN_BP), _ACT), _SDS((HALO, B_QKV), F32), _SDS((1, 128), F32), _SDS((1, 128), F32),
                   _SDS((MEM_LEN, 2 * X_Q), F32)],
        scratch_shapes=[pltpu.VMEM((B_V_HEADS, B_HD, B_HD), F32), pltpu.VMEM((HALO, B_QKV), F32)],
        name=name, compiler_params=_cp("arbitrary"))(proj, proj, conv_w, par, ng, memkv, states, deltas, dmix)


def _place():
    return lax.axis_index("x"), lax.axis_index("y"), lax.axis_index("c")


def _all_gather(shards, name):
    n = len(shards)

    def body(*refs):
        ins, outs = refs[:n], refs[n:2 * n]
        send_sems, recv_sems, local_sems = refs[2 * n:]
        x, y, c = _place()
        me, sibling = (x, y, c), (x, y, 1 - c)
        chips = [(1 - x, y), (x, 1 - y), (1 - x, 1 - y)]

        def rows(a, px, py, pc):
            return outs[a].at[4 * px + 2 * py + pc]

        def copy(a, k, block, to, src=None):
            return pltpu.make_async_remote_copy(
                src_ref=rows(a, *block) if src is None else src, dst_ref=rows(a, *block),
                send_sem=send_sems.at[a, k], recv_sem=recv_sems.at[a, k],
                device_id=to, device_id_type=pl.DeviceIdType.MESH)

        mine = [pltpu.make_async_copy(ins[a], rows(a, *me), local_sems.at[a]) for a in range(n)]
        for cp in mine:
            cp.start()
        first = []
        for a in range(n):
            first.append(copy(a, 0, me, sibling, src=ins[a]))
            first += [copy(a, 1 + j, me, (*chip, c), src=ins[a]) for j, chip in enumerate(chips)]
        for cp in first:
            cp.start()
        passed = []
        for j, chip in enumerate(chips):
            for a in range(n):
                copy(a, 1 + j, (*chip, c), me).wait_recv()
                fwd = copy(a, 4 + j, (*chip, c), sibling)
                fwd.start()
                passed.append(fwd)
        for a in range(n):
            copy(a, 0, sibling, me).wait_recv()
            for j, chip in enumerate(chips):
                copy(a, 4 + j, (*chip, 1 - c), me).wait_recv()
        for cp in first + passed:
            cp.wait_send()
        for cp in mine:
            cp.wait()

    hbm = pl.BlockSpec(memory_space=pl.ANY)
    return pl.pallas_call(
        body, out_shape=[_SDS((N_DEV,) + s.shape, s.dtype) for s in shards],
        in_specs=[hbm] * n, out_specs=[hbm] * n,
        scratch_shapes=[pltpu.SemaphoreType.DMA((n, 7)), pltpu.SemaphoreType.DMA((n, 7)), pltpu.SemaphoreType.DMA((n,))],
        name=name)(*shards)


class _Exchange:
    def __init__(self, lands, srcs):
        self.lands, self.srcs = lands, srcs


def _seq_exchange(srcs, land_shapes, plan, name, cid):
    n, nl = len(srcs), len(land_shapes)

    def launch(*refs):
        src_refs, land_refs = refs[:n], refs[n:n + nl]
        send_sems, recv_sems, local_sems = refs[n + nl:]
        x, y, c = _place()
        my = 4 * x + 2 * y + c
        peers = [(x ^ ((k + 1) >> 2 & 1), y ^ ((k + 1) >> 1 & 1), c ^ ((k + 1) & 1)) for k in range(N_DEV - 1)]
        barrier = pltpu.get_barrier_semaphore()
        for p in peers:
            pl.semaphore_signal(barrier, inc=1, device_id=p, device_id_type=pl.DeviceIdType.MESH)
        pl.semaphore_wait(barrier, N_DEV - 1)

        def src_for(a, dest):
            return src_refs[a].at[dest] if plan[a][1] else src_refs[a]

        def slot(a, source):
            return land_refs[plan[a][0]].at[source]

        mine = [pltpu.make_async_copy(src_for(a, my), slot(a, my), local_sems.at[a]) for a in range(n)]
        for cp in mine:
            cp.start()
        sends, recvs = [], []
        for k, (px, py, pc) in enumerate(peers):
            peer = 4 * px + 2 * py + pc
            for a in range(n):
                kw = dict(send_sem=send_sems.at[a * (N_DEV - 1) + k], recv_sem=recv_sems.at[a * (N_DEV - 1) + k],
                          device_id=(px, py, pc), device_id_type=pl.DeviceIdType.MESH)
                sends.append(pltpu.make_async_remote_copy(src_ref=src_for(a, peer), dst_ref=slot(a, my), **kw))
                recvs.append(pltpu.make_async_remote_copy(src_ref=src_for(a, my), dst_ref=slot(a, peer), **kw))
        for cp in sends:
            cp.start()
        for cp in recvs:
            cp.wait_recv()
        for cp in sends:
            cp.wait_send()
        for cp in mine:
            cp.wait()

    lands = pl.kernel(
        launch, out_type=[_SDS(s, d) for s, d in land_shapes],
        mesh=plsc.ScalarSubcoreMesh(axis_name="sequencer", num_cores=1), name=name,
        scratch_types=(pltpu.SemaphoreType.DMA((n * (N_DEV - 1),)), pltpu.SemaphoreType.DMA((n * (N_DEV - 1),)),
                       pltpu.SemaphoreType.DMA((n,))),
        compiler_params=pltpu.CompilerParams(collective_id=cid))(*srcs)
    return _Exchange(list(lands), list(srcs))


def _adam_update(g, w, m, v):
    c1 = 1.0 - ADAM_B1 ** ADAM_STEP
    c2 = 1.0 - ADAM_B2 ** ADAM_STEP
    mm = ADAM_B1 * m + (1.0 - ADAM_B1) * g
    vv = ADAM_B2 * v + (1.0 - ADAM_B2) * (g * g)
    delta = -ADAM_LR * ((mm / c1) / (jnp.sqrt(vv / c2) + ADAM_EPS) + ADAM_WD * w)
    return delta, mm, vv


def _sum_sources(p_ref):
    g = p_ref[0].astype(F32)
    for s in range(1, N_DEV):
        g = g + p_ref[s].astype(F32)
    return g


def _adamw(parts, w, m, v, tr, name, restore_b=False, deps=()):
    nl, r, c = w.shape
    cp = parts[0].shape[-1]

    def body(*refs):
        p_refs = refs[:nl]
        w_ref, m_ref, v_ref = refs[nl:nl + 3]
        g_ref, d_ref, nm_ref, nv_ref = refs[-4:]
        g = _sum_sources(p_refs[0])
        for l in range(1, nl):
            g = jnp.where(pl.program_id(0) == l, _sum_sources(p_refs[l]), g)
        if restore_b:
            g = jnp.concatenate([g[:, :BP_XQ], g[:, BP_GATE:BP_GATE + 2 * B_V_HEADS], g[:, BP_XQ:BP_GATE]], axis=1)
        delta, mm, vv = _adam_update(g, w_ref[...], m_ref[...], v_ref[...])
        g_ref[...] = g
        d_ref[...] = delta
        nm_ref[...] = mm
        nv_ref[...] = vv

    spec = pl.BlockSpec((None, tr, c), lambda l, i: (l, i, 0))
    part_specs = [pl.BlockSpec((N_DEV, tr, cp), functools.partial(lambda l, i, k: (0, jnp.where(l == k, i, 0), 0), k=k))
                  for k in range(nl)]
    return pl.pallas_call(
        body, grid=(nl, r // tr),
        in_specs=part_specs + [spec, spec, spec] + _dep_specs(deps),
        out_specs=[spec] * 4, out_shape=[_SDS(w.shape, F32)] * 4,
        name=name, compiler_params=_cp("arbitrary", "arbitrary"))(*parts, w, m, v, *deps)


def _pack_small(d_rel, d_cb, d_cw, d_qkv, d_mix, d_mem, d_ffn, d_final, d_sinks, d_par, d_ng, loss_row, name):
    flat = [d_rel, *d_cb, *d_cw, d_qkv, *d_mix, *d_mem, *d_ffn, d_final, d_sinks, d_par, d_ng, loss_row]
    n = len(flat)

    def body(*refs):
        ins, o_ref = refs[:n], refs[n]
        rel, cb0, cb1, cw0, cw1, qkv, mx0, mx1, me0, me1, ff0, ff1, fin, snk, par, ng, lss = ins
        o_ref[...] = jnp.zeros_like(o_ref)
        for k in range(N_BUCKETS):
            lane = SP_REL_LANE + 128 * (k % 8)
            o_ref[SP_QKV + k // 8:SP_QKV + k // 8 + 1, lane:lane + 128] = rel[k:k + 1, :]
        for l, (cb, cw) in enumerate(((cb0, cw0), (cb1, cw1))):
            o_ref[SP_CB + l:SP_CB + l + 1, :] = jnp.concatenate([cb[j] for j in range(FF_BLOCKS)], axis=1)
            full = jnp.concatenate([cw[j] for j in range(FF_BLOCKS)], axis=1)
            o_ref[SP_CW + FFN_CONV * l:SP_CW + FFN_CONV * (l + 1), :] = full[:FFN_CONV]
        o_ref[SP_QKV:SP_QKV + B_CONV, 0:B_QKV] = qkv[0:B_CONV, :]
        for base, pair in ((SP_MIX, (mx0, mx1)), (SP_MEM, (me0, me1)), (SP_FFN, (ff0, ff1))):
            for l in range(2):
                o_ref[base + l:base + l + 1, 0:D] = pair[l][...]
        o_ref[SP_FINAL:SP_FINAL + 1, 0:D] = fin[...]
        o_ref[SP_MISC:SP_MISC + 1, 0:128] = snk[...]
        o_ref[SP_MISC:SP_MISC + 1, 128:256] = par[...]
        o_ref[SP_MISC:SP_MISC + 1, 256:384] = ng[...]
        o_ref[SP_MISC:SP_MISC + 1, 384:512] = lss[...]

    vm = pl.BlockSpec(memory_space=pltpu.VMEM)
    return pl.pallas_call(body, in_specs=[vm] * n, out_specs=vm, out_shape=_SDS((SMALL_ROWS, D_FF), F32), name=name)(*flat)


_SMALL = ["rel_bias", "norm_mix_g", "norm_mem_g", "sinks_a", "a_log_b", "dt_bias_b", "out_norm_g_b", "norm_ffn_g",
          "ffn_conv_b", "final_norm_g", "conv_qkv_b", "ffn_conv_w"]


def _adamw_small(recv, rc_qkv, rc_ffn, ws, ms, vs, name, deps=()):
    n = len(_SMALL)

    def body(*refs):
        recv_ref, qkv_ref, ffn_ref = refs[:3]
        w_refs, m_refs, v_refs = refs[3:3 + n], refs[3 + n:3 + 2 * n], refs[3 + 2 * n:3 + 3 * n]
        outs, loss_ref = refs[len(refs) - 4 * n - 1:len(refs) - 1], refs[-1]
        gs = _sum_sources(recv_ref)
        loss_ref[...] = gs[SP_MISC:SP_MISC + 1, 384:512]
        grads = {
            "rel_bias": jnp.concatenate(
                [gs[SP_QKV + k // 8:SP_QKV + k // 8 + 1, SP_REL_LANE + 128 * (k % 8):SP_REL_LANE + 128 * (k % 8) + A_HEADS]
                 for k in range(N_BUCKETS)], axis=0),
            "norm_mix_g": gs[SP_MIX:SP_MIX + 2, 0:D], "norm_mem_g": gs[SP_MEM:SP_MEM + 2, 0:D],
            "sinks_a": gs[SP_MISC:SP_MISC + 1, 0:A_HEADS],
            "a_log_b": gs[SP_MISC:SP_MISC + 1, 128:128 + B_V_HEADS],
            "dt_bias_b": gs[SP_MISC:SP_MISC + 1, 128 + B_V_HEADS:128 + 2 * B_V_HEADS],
            "out_norm_g_b": gs[SP_MISC:SP_MISC + 1, 256:256 + B_HD],
            "norm_ffn_g": gs[SP_FFN:SP_FFN + 2, 0:D], "ffn_conv_b": gs[SP_CB:SP_CB + 2, :],
            "final_norm_g": gs[SP_FINAL:SP_FINAL + 1, 0:D],
            "conv_qkv_b": _sum_sources(qkv_ref), "ffn_conv_w": _sum_sources(ffn_ref),
        }
        for i, nm in enumerate(_SMALL):
            g = grads[nm]
            delta, mm, vv = _adam_update(g, w_refs[i][...], m_refs[i][...], v_refs[i][...])
            outs[i][...] = g
            outs[n + i][...] = delta
            outs[2 * n + i][...] = mm
            outs[3 * n + i][...] = vv

    vm = pl.BlockSpec(memory_space=pltpu.VMEM)
    shapes = [_SDS(w.shape, F32) for w in ws]
    return pl.pallas_call(
        body, in_specs=[vm] * (3 + 3 * n) + _dep_specs(deps), out_specs=[vm] * (4 * n + 1),
        out_shape=shapes * 4 + [_SDS((1, 128), F32)],
        name=name)(recv, rc_qkv, rc_ffn, *ws, *ms, *vs, *deps)


def _assemble(gathered, axis):
    g = jnp.moveaxis(gathered, 0, axis)
    shp = list(g.shape)
    return g.reshape(shp[:axis] + [shp[axis] * shp[axis + 1]] + shp[axis + 2:])


def _pad_rows(a, rows):
    return jnp.pad(a, ((0, rows - a.shape[0]), (0, 0)))


def _pad_lanes(a, lanes=128):
    return jnp.pad(a, ((0, 0), (0, lanes - a.shape[1])))


def _ff_blocks(a):
    return jnp.moveaxis(a.reshape(a.shape[0], FF_BLOCKS, GU_SHARD), 1, 0)


def _reorder_b(w):
    qkv_z = w[..., :B_QKV + B_V]
    gates = w[..., B_QKV + B_V:B_QKV + B_V + 2 * B_V_HEADS]
    xq = w[..., IN_B - X_Q:]
    pad = jnp.zeros(w.shape[:-1] + (IN_BP - IN_B,), w.dtype)
    return jnp.concatenate([qkv_z, xq, gates, pad], axis=-1)


def kernel(x, mem, rel_bias, norm_mix_g, norm_mem_g, w_mem_kv, w_out, w_in_a, sinks_a, w_in_b, conv_qkv_b, a_log_b, dt_bias_b, out_norm_g_b, norm_ffn_g, w_gate_up, ffn_conv_w, ffn_conv_b, w_down, final_norm_g, loss_target, m_rel_bias, m_norm_mix_g, m_norm_mem_g, m_w_mem_kv, m_w_out, m_w_in_a, m_sinks_a, m_w_in_b, m_conv_qkv_b, m_a_log_b, m_dt_bias_b, m_out_norm_g_b, m_norm_ffn_g, m_w_gate_up, m_ffn_conv_w, m_ffn_conv_b, m_w_down, m_final_norm_g, v_rel_bias, v_norm_mix_g, v_norm_mem_g, v_w_mem_kv, v_w_out, v_w_in_a, v_sinks_a, v_w_in_b, v_conv_qkv_b, v_a_log_b, v_dt_bias_b, v_out_norm_g_b, v_norm_ffn_g, v_w_gate_up, v_ffn_conv_w, v_ffn_conv_b, v_w_down, v_final_norm_g):
    local = dict(locals())
    order = ["rel_bias", "norm_mix_g", "norm_mem_g", "w_mem_kv", "w_out", "w_in_a", "sinks_a", "w_in_b", "conv_qkv_b",
             "a_log_b", "dt_bias_b", "out_norm_g_b", "norm_ffn_g", "w_gate_up", "ffn_conv_w", "ffn_conv_b", "w_down",
             "final_norm_g"]
    wts = {n: local[n] for n in order}
    moms = {n: local["m_" + n] for n in order}
    vars_ = {n: local["v_" + n] for n in order}
    h0 = x[0]
    memx = mem[0]
    tgt = loss_target[0]
    s = h0.shape[0]
    tm = _rows(s)
    tb = min(s, _TM_BIG)

    t_ = lambda a: jnp.swapaxes(a, 1, 2)
    g_mk0, g_out0, g_ia, g_cq, g_cw = _all_gather(
        [w_mem_kv[0:1].astype(_MXU), w_out[0:1].astype(_MXU), t_(w_in_a).astype(_MXU), conv_qkv_b, ffn_conv_w], "gather_first")
    g_mk, g_out = [g_mk0], [g_out0]
    gu_land = ((N_DEV, GU_SHARD, D), _MXU)
    dn_land = ((N_DEV, DN_SHARD, D), _MXU)
    whole = [(0, False), (1, False)]
    def after(a, b):
        return a + (b[(0,) * b.ndim] * 0).astype(a.dtype)

    gu0_w = _seq_exchange([after(t_(w_gate_up)[0].astype(_MXU), g_ia)], [gu_land], [(0, False)], "gather_gate_up0", 1)
    dn0_w = _seq_exchange([after(w_down[0].astype(_MXU), g_ia)], [dn_land], [(0, False)], "gather_down0", 8)
    w_ia = g_ia.reshape(IN_A, D)
    conv_qkv = _pad_rows(_assemble(g_cq, 2)[0], HALO)
    ffn_cw_full = _assemble(g_cw, 2)
    ffn_cw = [_ff_blocks(_pad_rows(ffn_cw_full[i], HALO)) for i in range(2)]
    ffn_cb = [_ff_blocks(ffn_conv_b[i:i + 1]) for i in range(2)]
    bucket = jnp.asarray(_bucket_table())
    bias = _bias_build(rel_bias, bucket, "bias_build")
    sinks = _pad_lanes(sinks_a)
    par_b = _pad_lanes(jnp.concatenate([a_log_b, dt_bias_b], axis=1))

    row_x = pl.BlockSpec((tm, D), lambda i, j: (i, 0))
    gu_shape = (2, FF_BLOCKS, s, GU_SHARD)

    def in_proj(h, g, w, w_spec, n_cols, tn, name, deps=(), out_dtype=F32, w_t=False, tm=None):
        return _norm_matmul(h, g, w, w_spec, n_cols // tn, (h.shape[0], n_cols),
                            pl.BlockSpec((tm or _rows(h.shape[0]), tn), lambda i, j: (i, j)), name, deps=deps, out_dtype=out_dtype,
                            w_t=w_t, tm=tm)

    def ffn_fwd(i, h, g_gu, g_dn, deps=()):
        gu, hn = _norm_matmul(h, norm_ffn_g[i:i + 1], g_gu, _spec_gate_up(1), N_DEV, gu_shape,
                              _spec_gu_act(0, 1, tb), f"gate_up_{i}", deps=deps, out_dtype=_ACT, w_t=True, tm=tb)
        h_new, act, gc = _glu_down(gu, ffn_cw[i], ffn_cb[i], g_dn, h, f"glu_down_{i}")
        return h_new, gu, hn, (act, gc)

    def out_proj(i, mix, h):
        return _matmul_res(mix, row_x, g_out[i], _spec_rowsharded(0, D // N_DEV, D), 1, h, f"out_proj_{i}")

    proj_a, hn_a = in_proj(h0, norm_mix_g[0:1], w_ia, pl.BlockSpec((640, D), lambda i, j: (j, 0)), IN_A, 640, "in_proj_a",
                           deps=gu0_w.srcs + dn0_w.srcs, out_dtype=_ACT, w_t=True)
    memkv0, memn0 = in_proj(memx, norm_mem_g[0:1], g_mk[0], _spec_rowsharded(0, D // N_DEV, 2 * X_Q), 2 * X_Q, 2 * X_Q, "mem_proj_0")
    mix_a = _mix_a_fwd(proj_a, bias, sinks, memkv0, "mix_a_fwd")
    h1 = out_proj(0, mix_a, h0)
    g_gu0, g_dn0 = gu0_w.lands[0], dn0_w.lands[0]
    in_b_w = _seq_exchange([after(_reorder_b(w_in_b).astype(_MXU), h1), after(w_mem_kv[1:2].astype(_MXU), h1),
                            after(w_out[1:2].astype(_MXU), h1)],
                           [((N_DEV, 1, D // N_DEV, IN_BP), _MXU), ((N_DEV, 1, D // N_DEV, 2 * X_Q), _MXU),
                            ((N_DEV, 1, D // N_DEV, D), _MXU)], [(0, False), (1, False), (2, False)], "gather_in_b", 2)
    ffn1_w = _seq_exchange([after(t_(w_gate_up)[1].astype(_MXU), h1), after(w_down[1].astype(_MXU), h1)], [gu_land, dn_land], whole,
                           "gather_ffn1", 3)
    h2, gu0, hn_f0, act0 = ffn_fwd(0, h1, g_gu0, g_dn0, deps=in_b_w.srcs + ffn1_w.srcs)
    g_ib, g_mk1, g_out1 = in_b_w.lands
    g_mk.append(g_mk1)
    g_out.append(g_out1)
    proj_b, hn_b = in_proj(h2, norm_mix_g[1:2], g_ib, _spec_rowsharded(0, D // N_DEV, 896, col_block=1), IN_BP, 896, "in_proj_b")
    memkv1, memn1 = in_proj(memx, norm_mem_g[1:2], g_mk[1], _spec_rowsharded(0, D // N_DEV, 2 * X_Q), 2 * X_Q, 2 * X_Q, "mem_proj_1",
                            deps=[h2])
    mix_b, states, deltas = _mix_b_fwd(proj_b, conv_qkv, par_b, out_norm_g_b, memkv1, "mix_b_fwd")
    h3 = out_proj(1, mix_b, h2)
    g_gu1, g_dn1 = ffn1_w.lands
    h4, gu1, hn_f1, act1 = ffn_fwd(1, h3, g_gu1, g_dn1)
    loss_row, dh, d_final_g = _loss_head(h4, final_norm_g[None, :], tgt, "loss_head")

    zeros_mem = jnp.zeros_like(memx)
    per_dest2 = [(0, True), (1, True)]

    def ffn_bwd(i, dh, h_in, gu, hn_f, act_gc, g_gu, g_dn, deps=()):
        act, gc = act_gc
        dgu, d_cw, d_cb = _glu_bwd(gu, gc, ffn_cw[i], dh, g_dn, f"glu_bwd_{i}", deps=deps)
        d_wdown = _matmul_tn(act, pl.BlockSpec((None, tm, GU_SHARD), lambda j, r: (j, r, 0)),
                             dh, pl.BlockSpec((tm, D), lambda j, r: (r, 0)), s, FF_BLOCKS, (GU_SHARD, D),
                             (N_DEV, DN_SHARD, D), pl.BlockSpec((2, DN_SHARD, D), lambda j, r: (j, 0, 0)), f"d_w_down_{i}")
        dh_new, d_g = _matmul_nt_normbwd(dgu, _spec_gu_act(0, 1, tm), g_gu, _spec_gate_up(1), N_DEV, h_in,
                                         norm_ffn_g[i:i + 1], dh, f"d_ffn_in_{i}", w_t=True)
        d_wgu = _matmul_tn(dgu, _spec_gu_act(1, 0, tb), hn_f, pl.BlockSpec((tb, D), lambda j, r: (r, 0)), s, N_DEV,
                           (GU_SHARD, D), (N_DEV, GU_SHARD, D), pl.BlockSpec((None, GU_SHARD, D), lambda j, r: (j, 0, 0)),
                           f"d_w_gate_up_{i}", tm=tb)
        return dh_new, [d_wdown, d_wgu], d_cw, d_cb, d_g

    def out_bwd(i, dh, mix, deps):
        dmix = _matmul_nt(dh, g_out[i], _spec_rowsharded(0, D // N_DEV, D), 1, (s, D), row_x, f"d_mix_{i}", deps=deps, out_dtype=_ACT)
        d_wout = _matmul_tn(mix, pl.BlockSpec((tm, D), lambda j, r: (r, 0)), dh, pl.BlockSpec((tm, D), lambda j, r: (r, 0)),
                            s, 1, (D, D), (N_DEV, D // N_DEV, D), pl.BlockSpec((N_DEV, D // N_DEV, D), lambda j, r: (0, 0, 0)),
                            f"d_w_out_{i}")
        return dmix, d_wout

    def mem_bwd(i, dmemkv, memn):
        tmm = _rows(MEM_LEN)
        _, d_g = _matmul_nt_normbwd(dmemkv, pl.BlockSpec((tmm, 2 * X_Q), lambda r, j: (r, 0)), g_mk[i],
                                    _spec_rowsharded(0, D // N_DEV, 2 * X_Q), 1, memx, norm_mem_g[i:i + 1], zeros_mem,
                                    f"d_mem_in_{i}")
        by_row = lambda j, r: (r, 0)
        d_w = _matmul_tn(memn, pl.BlockSpec((tmm, D), by_row), dmemkv, pl.BlockSpec((tmm, 2 * X_Q), by_row), MEM_LEN, 1,
                         (D, 2 * X_Q), (N_DEV, D // N_DEV, 2 * X_Q),
                         pl.BlockSpec((N_DEV, D // N_DEV, 2 * X_Q), lambda j, r: (0, 0, 0)), f"d_w_mem_kv_{i}")
        return d_w, d_g

    out_land = ((N_DEV, D // N_DEV, D), _WIRE)
    mk_land = ((N_DEV, D // N_DEV, 2 * X_Q), _WIRE)
    ffn_lands = [((N_DEV, DN_SHARD, D), _WIRE), ((N_DEV, GU_SHARD, D), _WIRE)]
    dh, d_ffn1, d_cw1, d_cb1, d_gf1 = ffn_bwd(1, dh, h3, gu1, hn_f1, act1, g_gu1, g_dn1)
    ffn1_g = _seq_exchange(d_ffn1, ffn_lands, per_dest2, "send_ffn1_grads", 5)
    dmix, d_wout1 = out_bwd(1, dh, mix_b, ffn1_g.srcs)
    dproj_b, d_convw, d_par, d_ng, dmemkv1 = _mix_b_bwd(proj_b, conv_qkv, par_b, out_norm_g_b, memkv1, states, deltas, dmix, "mix_b_bwd")
    dh, d_gm1 = _matmul_nt_normbwd(dproj_b, pl.BlockSpec((tm, 896), lambda i, j: (i, j)), g_ib,
                                   _spec_rowsharded(0, D // N_DEV, 896, col_block=1), IN_BP // 896, h2, norm_mix_g[1:2], dh, "d_in_b")
    d_wib = _matmul_tn(hn_b, pl.BlockSpec((tb, D), lambda j, r: (r, 0)), dproj_b, pl.BlockSpec((tb, 896), lambda j, r: (r, j)),
                       s, IN_BP // 896, (D, 896), (N_DEV, D // N_DEV, IN_BP),
                       pl.BlockSpec((N_DEV, D // N_DEV, 896), lambda j, r: (0, 0, j)), "d_w_in_b", tm=tb)
    d_wmk1, d_gmem1 = mem_bwd(1, dmemkv1, memn1)
    mix1_g = _seq_exchange([d_wout1, d_wib, d_wmk1], [out_land, ((N_DEV, D // N_DEV, IN_BP), _WIRE), mk_land],
                           [(0, True), (1, True), (2, True)], "send_mix1_grads", 6)
    dh, d_ffn0, d_cw0, d_cb0, d_gf0 = ffn_bwd(0, dh, h1, gu0, hn_f0, act0, g_gu0, g_dn0, deps=mix1_g.srcs)
    dmix, d_wout0 = out_bwd(0, dh, mix_a, d_ffn0 + ffn1_g.lands[:1])
    ffn0_g = _seq_exchange(d_ffn0 + [d_wout0], ffn_lands + [out_land], per_dest2 + [(2, True)], "send_ffn0_grads", 4)
    dproj_a, dbias, dsinks, dmemkv0 = _mix_a_bwd(proj_a, bias, sinks, memkv0, dmix, "mix_a_bwd", deps=ffn0_g.srcs)
    dh, d_gm0 = _matmul_nt_normbwd(dproj_a, pl.BlockSpec((tm, 640), lambda i, j: (i, j)), w_ia,
                                   pl.BlockSpec((640, D), lambda i, j: (j, 0)), IN_A // 640, h0, norm_mix_g[0:1], dh, "d_in_a",
                                   w_t=True)
    d_wia = _matmul_tn(dproj_a, pl.BlockSpec((tm, IN_A), lambda j, r: (r, 0)), hn_a, pl.BlockSpec((tm, D), lambda j, r: (r, 0)),
                       s, 1, (IN_A, D), (N_DEV, IA_SHARD, D), pl.BlockSpec((N_DEV, IA_SHARD, D), lambda j, r: (0, 0, 0)),
                       "d_w_in_a")
    d_wmk0, d_gmem0 = mem_bwd(0, dmemkv0, memn0)
    d_rel = _bias_reduce(dbias, bucket, "bias_reduce")
    small = _pack_small(d_rel, (d_cb0, d_cb1), (d_cw0, d_cw1), d_convw, (d_gm0, d_gm1), (d_gmem0, d_gmem1),
                        (d_gf0, d_gf1), d_final_g, dsinks, d_par, d_ng, loss_row, "pack_small")
    mix0_g = _seq_exchange([d_wia, d_wmk0, small],
                           [((N_DEV, IA_SHARD, D), _WIRE), mk_land, ((N_DEV, SMALL_ROWS, D_FF), F32)],
                           [(0, True), (1, True), (2, False)], "send_mix0_grads", 7)

    res = {}
    last = []

    def update(nm, parts, tr, restore=False, transposed=False):
        view = t_ if transposed else (lambda a: a)
        out = _adamw(parts, view(wts[nm]), view(moms[nm]), view(vars_[nm]), tr, "adamw_" + nm, restore_b=restore, deps=last[-1:])
        res[nm] = [view(o) for o in out]
        last.append(out[1])

    r_dn1, r_gu1 = ffn1_g.lands
    r_dn0, r_gu0, r_out0 = ffn0_g.lands
    r_out1, r_ib, r_mk1 = mix1_g.lands
    update("w_in_b", [r_ib], 32, True)
    update("w_gate_up", [r_gu0, r_gu1], 176, transposed=True)
    update("w_down", [r_dn0, r_dn1], 176)
    r_ia, r_mk0, r_small = mix0_g.lands
    update("w_mem_kv", [r_mk0, r_mk1], 128)
    update("w_out", [r_out0, r_out1], 128)
    update("w_in_a", [r_ia], IA_SHARD, transposed=True)

    my = 4 * lax.axis_index("x") + 2 * lax.axis_index("y") + lax.axis_index("c")
    cq = conv_qkv_b.shape[-1]
    cf = ffn_conv_w.shape[-1]
    rc_qkv = lax.dynamic_slice_in_dim(r_small[:, SP_QKV:SP_QKV + B_CONV, :B_QKV], my * cq, cq, axis=2)[:, None]
    rc_ffn = lax.dynamic_slice_in_dim(r_small[:, SP_CW:SP_CW + 2 * FFN_CONV, :], my * cf, cf, axis=2).reshape(N_DEV, 2, FFN_CONV, cf)
    as2d = lambda a: a[None, :] if a.ndim == 1 else a
    small_out = _adamw_small(r_small, rc_qkv, rc_ffn, [as2d(wts[n]) for n in _SMALL], [as2d(moms[n]) for n in _SMALL],
                             [as2d(vars_[n]) for n in _SMALL], "adamw_small", deps=last[-1:])
    ns = len(_SMALL)
    for i, nm in enumerate(_SMALL):
        res[nm] = [small_out[k * ns + i].reshape(wts[nm].shape) for k in range(4)]

    return (small_out[-1][0, 0], dh[None], *[res[n][0] for n in order], *[res[n][1] for n in order],
            *[res[n][2] for n in order], *[res[n][3] for n in order])
```

```python
import functools
import math

import numpy as np

import jax
import jax.numpy as jnp
from jax import lax
from jax.experimental import pallas as pl
from jax.experimental.pallas import tpu as pltpu
from jax.experimental.pallas import tpu_sc as plsc

F32 = jnp.float32
_MXU = jnp.bfloat16
_ACT = jnp.bfloat16
_WIRE = jnp.bfloat16
_HI = lax.Precision.HIGH
_TM = 1024
_TM_GLU = 512
_TM_BIG = 2048
_VMEM_LIMIT = 48 * 1024 * 1024
_SDS = jax.ShapeDtypeStruct

D = 1024
EPS = 1e-6
A_HEADS, A_KV_HEADS, A_HD, BLK = 12, 2, 64, 128
N_BUCKETS, MAX_DISTANCE = 32, 128
B_QK_HEADS, B_V_HEADS, B_HD, B_CONV, CHUNK = 3, 6, 128, 4, 64
X_HEADS, X_HD, MEM_LEN = 4, 64, 256
D_FF, FFN_CONV = 2816, 3
A_Q, A_KV, X_Q = 768, 128, 256
B_QK, B_V, B_QKV = 384, 768, 1536
IN_A, IN_B = 1280, 2572
IN_BP = 2688
BP_Z, BP_XQ, BP_GATE = 1536, 2304, 2560
HALO = 8
GLU_HALO = 16

N_DEV = 8
GU_SHARD = 2 * D_FF // N_DEV
FF_BLOCKS = D_FF // GU_SHARD
DN_SHARD = D_FF // N_DEV
IA_SHARD = IN_A // N_DEV

ADAM_LR, ADAM_B1, ADAM_B2, ADAM_EPS, ADAM_WD, ADAM_STEP = 0.001, 0.9, 0.999, 1e-08, 0.01, 10

SP_CB, SP_CW, SP_QKV, SP_MIX, SP_MEM, SP_FFN, SP_FINAL, SP_MISC, SMALL_ROWS = 0, 2, 8, 12, 14, 16, 18, 19, 24
SP_REL_LANE = B_QKV


def _cp(*sems):
    return pltpu.CompilerParams(dimension_semantics=sems, vmem_limit_bytes=_VMEM_LIMIT)


def _mm(a, b):
    return jnp.dot(a.astype(_MXU), b.astype(_MXU), preferred_element_type=F32)


def _mm_nt(a, b):
    return lax.dot_general(a.astype(_MXU), b.astype(_MXU), (((1,), (1,)), ((), ())), preferred_element_type=F32)


def _mm_tn(a, b):
    return lax.dot_general(a.astype(_MXU), b.astype(_MXU), (((0,), (0,)), ((), ())), preferred_element_type=F32)


def _mmf(a, b):
    return jnp.dot(a, b, preferred_element_type=F32, precision=_HI)


def _mmf_nt(a, b):
    return lax.dot_general(a, b, (((1,), (1,)), ((), ())), preferred_element_type=F32, precision=_HI)


def _silu(x):
    return x * jax.nn.sigmoid(x)


def _w2d(ref):
    v = ref[...]
    return v.reshape(-1, v.shape[-1])


def _rows(m):
    return min(m, _TM)


def _spec_rowsharded(layer, rows, cols, col_block=None):
    if col_block is None:
        return pl.BlockSpec((N_DEV, None, rows, cols), lambda *_: (0, layer, 0, 0))
    return pl.BlockSpec((N_DEV, None, rows, cols), lambda *ids: (0, layer, 0, ids[col_block]))


def _spec_gate_up(axis):
    return pl.BlockSpec((None, GU_SHARD, D), lambda *ids: (ids[axis], 0, 0))


def _spec_down(axis):
    return pl.BlockSpec((2, DN_SHARD, D), lambda *ids: (ids[axis], 0, 0))


def _dep_specs(deps):
    return [pl.BlockSpec(memory_space=pl.ANY) for d in deps]


def _spec_gu_act(row_axis, axis, tm):
    return pl.BlockSpec((None, None, tm, GU_SHARD), lambda *ids: (ids[axis] // FF_BLOCKS, ids[axis] % FF_BLOCKS, ids[row_axis], 0))


def _norm_matmul(x, g, w, w_spec, n_blocks, out_shape, out_spec, name, deps=(), out_dtype=F32, w_t=False, tm=None):
    m, k = x.shape
    tm = tm or _rows(m)

    def body(x_ref, g_ref, w_ref, *rest):
        y_ref, hn_ref = rest[-2:]

        @pl.when(pl.program_id(1) == 0)
        def _():
            xv = x_ref[...]
            r = lax.rsqrt(jnp.mean(xv * xv, axis=-1, keepdims=True) + EPS)
            hn_ref[...] = (xv * r * g_ref[...]).astype(hn_ref.dtype)

        y_ref[...] = (_mm_nt if w_t else _mm)(hn_ref[...], _w2d(w_ref)).astype(y_ref.dtype)

    return pl.pallas_call(
        body, grid=(m // tm, n_blocks),
        in_specs=[pl.BlockSpec((tm, k), lambda i, j: (i, 0)), pl.BlockSpec((1, k), lambda i, j: (0, 0)), w_spec]
        + _dep_specs(deps),
        out_specs=[out_spec, pl.BlockSpec((tm, k), lambda i, j: (i, 0))],
        out_shape=[_SDS(out_shape, out_dtype), _SDS((m, k), _ACT)],
        name=name, compiler_params=_cp("arbitrary", "arbitrary"))(x, g, w, *deps)


def _matmul_res(a, a_spec, w, w_spec, n_k, res, name):
    m, n = res.shape
    tm = _rows(m)

    def body(a_ref, w_ref, r_ref, o_ref):
        part = _mm(a_ref[...], _w2d(w_ref))

        @pl.when(pl.program_id(1) == 0)
        def _():
            o_ref[...] = r_ref[...] + part

        @pl.when(pl.program_id(1) > 0)
        def _():
            o_ref[...] += part

    return pl.pallas_call(
        body, grid=(m // tm, n_k),
        in_specs=[a_spec, w_spec, pl.BlockSpec((tm, n), lambda i, j: (i, 0))],
        out_specs=pl.BlockSpec((tm, n), lambda i, j: (i, 0)),
        out_shape=_SDS((m, n), F32), name=name, compiler_params=_cp("arbitrary", "arbitrary"))(a, w, res)


def _matmul_nt(dy, w, w_spec, n_blocks, out_shape, out_spec, name, deps=(), out_dtype=F32):
    m, n = dy.shape
    tm = _rows(m)

    def body(dy_ref, w_ref, *rest):
        o_ref = rest[-1]
        o_ref[...] = _mm_nt(dy_ref[...], _w2d(w_ref)).astype(o_ref.dtype)

    return pl.pallas_call(
        body, grid=(m // tm, n_blocks),
        in_specs=[pl.BlockSpec((tm, n), lambda i, j: (i, 0)), w_spec] + _dep_specs(deps),
        out_specs=out_spec, out_shape=_SDS(out_shape, out_dtype),
        name=name, compiler_params=_cp("arbitrary", "arbitrary"))(dy, w, *deps)


def _matmul_nt_normbwd(dy, dy_spec, w, w_spec, nj, h, g, dh_in, name, w_t=False):
    m, k = h.shape
    tm = _rows(m)

    def body(dy_ref, w_ref, h_ref, g_ref, dhin_ref, dh_ref, dg_ref, acc_ref):
        i, j = pl.program_id(0), pl.program_id(1)

        @pl.when(j == 0)
        def _():
            acc_ref[...] = jnp.zeros_like(acc_ref)

        acc_ref[...] += (_mm if w_t else _mm_nt)(dy_ref[...], _w2d(w_ref))

        @pl.when(j == nj - 1)
        def _():
            xv = h_ref[...]
            r = lax.rsqrt(jnp.mean(xv * xv, axis=-1, keepdims=True) + EPS)
            xh = xv * r
            dhn = acc_ref[...]
            part = jnp.sum(dhn * xh, axis=0, keepdims=True)

            @pl.when(i == 0)
            def _():
                dg_ref[...] = part

            @pl.when(i > 0)
            def _():
                dg_ref[...] += part

            t = dhn * g_ref[...]
            dh_ref[...] = dhin_ref[...] + r * (t - xh * jnp.mean(t * xh, axis=-1, keepdims=True))

    return pl.pallas_call(
        body, grid=(m // tm, nj),
        in_specs=[dy_spec, w_spec, pl.BlockSpec((tm, k), lambda i, j: (i, 0)), pl.BlockSpec((1, k), lambda i, j: (0, 0)),
                  pl.BlockSpec((tm, k), lambda i, j: (i, 0))],
        out_specs=[pl.BlockSpec((tm, k), lambda i, j: (i, 0)), pl.BlockSpec((1, k), lambda i, j: (0, 0))],
        out_shape=[_SDS((m, k), F32), _SDS((1, k), F32)],
        scratch_shapes=[pltpu.VMEM((tm, k), F32)],
        name=name, compiler_params=_cp("arbitrary", "arbitrary"))(dy, w, h, g, dh_in)


def _matmul_tn(x, x_spec, dy, dy_spec, m, n_blocks, acc_shape, out_shape, out_spec, name, tm=None):
    tm = tm or _rows(m)
    nm = m // tm

    def body(x_ref, dy_ref, o_ref, acc_ref):
        @pl.when(pl.program_id(1) == 0)
        def _():
            acc_ref[...] = jnp.zeros_like(acc_ref)

        acc_ref[...] += _mm_tn(x_ref[...], dy_ref[...])

        @pl.when(pl.program_id(1) == nm - 1)
        def _():
            o_ref[...] = acc_ref[...].reshape(o_ref.shape).astype(o_ref.dtype)

    return pl.pallas_call(
        body, grid=(n_blocks, nm), in_specs=[x_spec, dy_spec], out_specs=out_spec,
        out_shape=_SDS(out_shape, _WIRE), scratch_shapes=[pltpu.VMEM(acc_shape, F32)],
        name=name, compiler_params=_cp("arbitrary", "arbitrary"))(x, dy)


def _loss_head(h, g, tgt, name):
    m, k = h.shape
    tm = _rows(m)

    def body(h_ref, g_ref, t_ref, loss_ref, dh_ref, dg_ref):
        i = pl.program_id(0)
        xv = h_ref[...]
        r = lax.rsqrt(jnp.mean(xv * xv, axis=-1, keepdims=True) + EPS)
        xh = xv * r
        gv = g_ref[...]
        err = xh * gv - t_ref[...]
        lpart = jnp.zeros((1, 128), F32) + 0.5 * jnp.sum(jnp.mean(err * err, axis=-1, keepdims=True), axis=0, keepdims=True)
        dy = err * (1.0 / k)
        gpart = jnp.sum(dy * xh, axis=0, keepdims=True)

        @pl.when(i == 0)
        def _():
            loss_ref[...] = lpart
            dg_ref[...] = gpart

        @pl.when(i > 0)
        def _():
            loss_ref[...] += lpart
            dg_ref[...] += gpart

        t = dy * gv
        dh_ref[...] = r * (t - xh * jnp.mean(t * xh, axis=-1, keepdims=True))

    return pl.pallas_call(
        body, grid=(m // tm,),
        in_specs=[pl.BlockSpec((tm, k), lambda i: (i, 0)), pl.BlockSpec((1, k), lambda i: (0, 0)),
                  pl.BlockSpec((tm, k), lambda i: (i, 0))],
        out_specs=[pl.BlockSpec((1, 128), lambda i: (0, 0)), pl.BlockSpec((tm, k), lambda i: (i, 0)),
                   pl.BlockSpec((1, k), lambda i: (0, 0))],
        out_shape=[_SDS((1, 128), F32), _SDS((m, k), F32), _SDS((1, k), F32)],
        name=name, compiler_params=_cp("arbitrary"))(h, g, tgt)


def _glu_down(gu, conv_w, conv_b, w_down, res, name):
    s = gu.shape[2]
    tm = min(s, _TM_GLU)

    def body(gu_ref, prev_ref, w_ref, b_ref, wdn_ref, r_ref, o_ref, act_ref, gc_ref):
        i, j = pl.program_id(0), pl.program_id(1)
        prev = jnp.where(i > 0, prev_ref[...].astype(F32), 0.0)
        ext = jnp.concatenate([prev, gu_ref[0].astype(F32)], axis=0)
        gc = b_ref[...] + w_ref[FFN_CONV - 1:FFN_CONV, :] * ext
        for k in range(FFN_CONV - 1):
            gc = gc + w_ref[k:k + 1, :] * pltpu.roll(ext, FFN_CONV - 1 - k, 0)
        gc = gc[GLU_HALO:]
        gc_ref[...] = gc.astype(gc_ref.dtype)
        act =(_silu(gc) * gu_ref[1].astype(F32)).astype(act_ref.dtype)
        act_ref[...] = act
        part = _mm(act, _w2d(wdn_ref))

        @pl.when(j == 0)
        def _():
            o_ref[...] = r_ref[...] + part

        @pl.when(j > 0)
        def _():
            o_ref[...] += part

    return pl.pallas_call(
        body, grid=(s // tm, FF_BLOCKS),
        in_specs=[pl.BlockSpec((2, None, tm, GU_SHARD), lambda i, j: (0, j, i, 0)),
                  pl.BlockSpec((None, None, GLU_HALO, GU_SHARD),
                               lambda i, j: (0, j, jnp.maximum(i * (tm // GLU_HALO) - 1, 0), 0)),
                  pl.BlockSpec((None, HALO, GU_SHARD), lambda i, j: (j, 0, 0)),
                  pl.BlockSpec((None, 1, GU_SHARD), lambda i, j: (j, 0, 0)),
                  _spec_down(1), pl.BlockSpec((tm, D), lambda i, j: (i, 0))],
        out_specs=[pl.BlockSpec((tm, D), lambda i, j: (i, 0)), pl.BlockSpec((None, tm, GU_SHARD), lambda i, j: (j, i, 0)),
                   pl.BlockSpec((None, tm, GU_SHARD), lambda i, j: (j, i, 0))],
        out_shape=[_SDS((s, D), F32), _SDS((FF_BLOCKS, s, GU_SHARD), _ACT), _SDS((FF_BLOCKS, s, GU_SHARD), _ACT)], name=name,
        compiler_params=_cp("arbitrary", "arbitrary"))(gu, gu, conv_w, conv_b, w_down, res)


def _glu_bwd(gu, gc, conv_w, dh, w_down, name, deps=()):
    s = gu.shape[2]
    tm = min(s, _TM_GLU)
    nt = s // tm
    ext_rows = tm + GLU_HALO

    def body(gu_ref, prev_ref, gc_ref, w_ref, dh_ref, wdn_ref, *rest):
        dgu_ref, dw_ref, db_ref, carry_ref = rest[-4:]
        t = pl.program_id(1)
        i = nt - 1 - t

        @pl.when(t == 0)
        def _():
            carry_ref[...] = jnp.zeros_like(carry_ref)
            dw_ref[...] = jnp.zeros_like(dw_ref)
            db_ref[...] = jnp.zeros_like(db_ref)

        up = gu_ref[1].astype(F32)
        prev = jnp.where(i > 0, prev_ref[...].astype(F32), 0.0)
        ext = jnp.concatenate([prev, gu_ref[0].astype(F32)], axis=0)
        gc = gc_ref[...].astype(F32)
        sg = jax.nn.sigmoid(gc)
        da = _mm_nt(dh_ref[...], _w2d(wdn_ref))
        dup = da * (gc * sg)
        dgc = da * up * (sg * (1.0 + gc * (1.0 - sg)))
        db_ref[...] += jnp.sum(dgc, axis=0, keepdims=True)
        dgc_ext = jnp.concatenate([jnp.zeros((GLU_HALO, GU_SHARD), F32), dgc], axis=0)
        ahead = [pltpu.roll(dgc_ext, ext_rows - (FFN_CONV - 1 - j), 0) if j < FFN_CONV - 1 else dgc_ext
                 for j in range(FFN_CONV)]
        dext = ahead[0] * w_ref[0:1, :]
        for j in range(FFN_CONV):
            dw_ref[j:j + 1, :] += jnp.sum(ext * ahead[j], axis=0, keepdims=True)
            if j > 0:
                dext = dext + ahead[j] * w_ref[j:j + 1, :]
        tail = jnp.concatenate([jnp.zeros((tm - GLU_HALO, GU_SHARD), F32), carry_ref[...]], axis=0)
        dgate = dext[GLU_HALO:] + tail
        carry_ref[...] = dext[:GLU_HALO]
        dgu_ref[0] = dgate.astype(dgu_ref.dtype)
        dgu_ref[1] = dup.astype(dgu_ref.dtype)

    return pl.pallas_call(
        body, grid=(FF_BLOCKS, nt),
        in_specs=[pl.BlockSpec((2, None, tm, GU_SHARD), lambda j, t: (0, j, nt - 1 - t, 0)),
                  pl.BlockSpec((None, None, GLU_HALO, GU_SHARD),
                               lambda j, t: (0, j, jnp.maximum((nt - 1 - t) * (tm // GLU_HALO) - 1, 0), 0)),
                  pl.BlockSpec((None, tm, GU_SHARD), lambda j, t: (j, nt - 1 - t, 0)),
                  pl.BlockSpec((None, HALO, GU_SHARD), lambda j, t: (j, 0, 0)),
                  pl.BlockSpec((tm, D), lambda j, t: (nt - 1 - t, 0)), _spec_down(0)] + _dep_specs(deps),
        out_specs=[pl.BlockSpec((2, None, tm, GU_SHARD), lambda j, t: (0, j, nt - 1 - t, 0)),
                   pl.BlockSpec((None, HALO, GU_SHARD), lambda j, t: (j, 0, 0)),
                   pl.BlockSpec((None, 1, GU_SHARD), lambda j, t: (j, 0, 0))],
        out_shape=[_SDS(gu.shape, _ACT), _SDS((FF_BLOCKS, HALO, GU_SHARD), F32), _SDS((FF_BLOCKS, 1, GU_SHARD), F32)],
        scratch_shapes=[pltpu.VMEM((GLU_HALO, GU_SHARD), F32)],
        name=name, compiler_params=_cp("arbitrary", "arbitrary"))(gu, gu, gc, conv_w, dh, w_down, *deps)


def _bucket_table():
    qi = np.arange(BLK)[:, None]
    kj = np.arange(BLK)[None, :]
    n = np.where(kj > qi, BLK + qi - kj, qi - kj)
    max_exact = N_BUCKETS // 2
    nf = np.maximum(n, 1).astype(np.float32)
    large = max_exact + (np.log(nf / max_exact) / math.log(MAX_DISTANCE / max_exact)
                         * (N_BUCKETS - max_exact)).astype(np.int32)
    large = np.minimum(large, N_BUCKETS - 1)
    return np.where(n < max_exact, n, large).astype(np.int32)


def _lane_low():
    return lax.broadcasted_iota(jnp.int32, (1, 128), 1) < A_HD


def _swa_groups(q, kd, vd, sink, bias, upper, first):
    n = A_HEADS // A_KV_HEADS
    ng = A_KV_HEADS
    low = _lane_low()
    qm = [jnp.concatenate([jnp.where(low == (h % 2 == 0), q[g][:, (h // 2) * 128:(h // 2 + 1) * 128], 0.0) for h in range(n)], axis=0)
          for g in range(ng)]
    s2 = [_mm_nt(qm[g], kd[g]) * (A_HD ** -0.5) for g in range(ng)]
    s = [jnp.where(upper[None], s2[g][:, :BLK].reshape(n, BLK, BLK), s2[g][:, BLK:].reshape(n, BLK, BLK)) + bias[g] for g in range(ng)]
    s = [jnp.where((upper & first)[None], -jnp.inf, t) for t in s]
    m = [lax.stop_gradient(jnp.maximum(jnp.max(s[g], axis=-1, keepdims=True), sink[g])) for g in range(ng)]
    p = [jnp.exp(s[g] - m[g]) for g in range(ng)]
    split = [jnp.concatenate([jnp.where(upper[None], t, 0.0), jnp.where(upper[None], 0.0, t)], axis=-1).reshape(n * BLK, 2 * BLK)
             for t in p]
    ones = jnp.ones((BLK, 128), F32)
    den = [_mm(p[g].reshape(n * BLK, BLK), ones) + jnp.exp(sink[g] - m[g]).reshape(n * BLK, 1) for g in range(ng)]
    o = [_mm(split[g], vd[g]) / den[g] for g in range(ng)]
    return [jnp.concatenate([jnp.where(low, t[2 * k * BLK:(2 * k + 1) * BLK], t[(2 * k + 1) * BLK:(2 * k + 2) * BLK])
                             for k in range(n // 2)], axis=1) for t in o]


def _mix_a_core(q, kd, vd, sink, bias, xq, mk, mv, upper, first):
    return _swa_groups(q, kd, vd, sink, bias, upper, first), _cross_pairs(xq, mk, mv)


def _swa_sinks(sink_ref, g):
    n = A_HEADS // A_KV_HEADS
    return jnp.concatenate([sink_ref[:, h:h + 1] for h in range(g * n, (g + 1) * n)], axis=0).reshape(n, 1, 1)


def _both_halves(t, t_rolled, g):
    low = _lane_low()
    return jnp.where(low, t, t_rolled) if g == 0 else jnp.where(low, t_rolled, t)


def _cross_pairs(q, mk, mv):
    rows = q.shape[0]
    low = _lane_low()
    qm = [jnp.concatenate([jnp.where(low, q[:, p * 128:(p + 1) * 128], 0.0), jnp.where(low, 0.0, q[:, p * 128:(p + 1) * 128])], axis=0)
          for p in range(X_HEADS // 2)]
    s = [_mm_nt(qm[p], mk[:, p * 128:(p + 1) * 128]) * (X_HD ** -0.5) for p in range(X_HEADS // 2)]
    e = [jnp.exp(t - lax.stop_gradient(jnp.max(t, axis=-1, keepdims=True))) for t in s]
    pr = [t / jnp.sum(t, axis=-1, keepdims=True) for t in e]
    o = [_mm(pr[p], mv[:, p * 128:(p + 1) * 128]) for p in range(X_HEADS // 2)]
    return jnp.concatenate([jnp.where(low, t[:rows], t[rows:]) for t in o], axis=1)


def _swa_upper():
    qi = lax.broadcasted_iota(jnp.int32, (BLK, BLK), 0)
    kj = lax.broadcasted_iota(jnp.int32, (BLK, BLK), 1)
    return kj > qi


def _bias_build(rel_bias, bucket, name):
    def body(rb_ref, bucket_ref, o_ref):
        b = bucket_ref[...]
        for h in range(A_HEADS):
            acc = jnp.zeros((BLK, BLK), F32)
            for k in range(N_BUCKETS):
                acc = jnp.where(b == k, rb_ref[k, h], acc)
            o_ref[h] = acc

    return pl.pallas_call(
        body, in_specs=[pl.BlockSpec(memory_space=pltpu.SMEM), pl.BlockSpec(memory_space=pltpu.VMEM)],
        out_specs=pl.BlockSpec(memory_space=pltpu.VMEM),
        out_shape=_SDS((A_HEADS, BLK, BLK), F32), name=name)(rel_bias, bucket)


def _bias_reduce(dbias, bucket, name):
    def body(db_ref, bucket_ref, o_ref):
        b = bucket_ref[...]
        row = lax.broadcasted_iota(jnp.int32, (N_BUCKETS, 128), 0)
        lane = lax.broadcasted_iota(jnp.int32, (N_BUCKETS, 128), 1)
        acc = jnp.zeros((N_BUCKETS, 128), F32)
        for h in range(A_HEADS):
            v = db_ref[h]
            for k in range(N_BUCKETS):
                sk = jnp.sum(jnp.sum(jnp.where(b == k, v, 0.0), axis=1, keepdims=True), axis=0, keepdims=True)
                acc = acc + jnp.where((row == k) & (lane == h), sk, 0.0)
        o_ref[...] = acc

    return pl.pallas_call(
        body, in_specs=[pl.BlockSpec(memory_space=pltpu.VMEM)] * 2,
        out_specs=pl.BlockSpec(memory_space=pltpu.VMEM),
        out_shape=_SDS((N_BUCKETS, 128), F32), name=name)(dbias, bucket)


def _mix_a_fwd(proj, bias, sinks, memkv, name):
    s = proj.shape[0]
    nb = s // BLK
    grp = A_HEADS // A_KV_HEADS

    def body(proj_ref, prev_ref, bias_ref, sink_ref, memkv_ref, o_ref):
        i = pl.program_id(0)
        upper = _swa_upper()
        prev = prev_ref[...].astype(F32)
        proj = proj_ref[...].astype(F32)
        kb = jnp.concatenate([prev[:, :A_KV], proj[:, A_Q:A_Q + A_KV]], axis=0)
        vb = jnp.concatenate([prev[:, A_KV:], proj[:, A_Q + A_KV:A_Q + 2 * A_KV]], axis=0)
        kb_r = pltpu.roll(kb, A_HD, 1)
        vb_r = pltpu.roll(vb, A_HD, 1)
        gw = A_Q // A_KV_HEADS
        groups = range(A_KV_HEADS)
        swa, cross = _mix_a_core([proj[:, g * gw:(g + 1) * gw] for g in groups], [_both_halves(kb, kb_r, g) for g in groups],
                                 [_both_halves(vb, vb_r, g) for g in groups], [_swa_sinks(sink_ref, g) for g in groups],
                                 [bias_ref[g * grp:(g + 1) * grp] for g in groups], proj[:, A_Q + 2 * A_KV:],
                                 memkv_ref[:, :X_Q], memkv_ref[:, X_Q:], upper, i == 0)
        o_ref[...] = jnp.concatenate(swa + [cross], axis=1).astype(o_ref.dtype)

    return pl.pallas_call(
        body, grid=(nb,),
        in_specs=[pl.BlockSpec((BLK, IN_A), lambda i: (i, 0)),
                  pl.BlockSpec((BLK, 2 * A_KV), lambda i: (jnp.maximum(i - 1, 0), A_Q // (2 * A_KV))),
                  pl.BlockSpec((A_HEADS, BLK, BLK), lambda i: (0, 0, 0)),
                  pl.BlockSpec((1, 128), lambda i: (0, 0)),
                  pl.BlockSpec((MEM_LEN, 2 * X_Q), lambda i: (0, 0))],
        out_specs=pl.BlockSpec((BLK, D), lambda i: (i, 0)),
        out_shape=_SDS((s, D), _ACT), name=name, compiler_params=_cp("arbitrary"))(proj, proj, bias, sinks, memkv)


def _mix_a_bwd(proj, bias, sinks, memkv, dmix, name, deps=()):
    s = proj.shape[0]
    nb = s // BLK
    grp = A_HEADS // A_KV_HEADS

    def body(proj_ref, prev_ref, bias_ref, sink_ref, memkv_ref, dmix_ref, *rest):
        dproj_ref, dbias_ref, dsink_ref, dmemkv_ref, carry_ref = rest[-5:]
        t = pl.program_id(0)
        i = nb - 1 - t

        @pl.when(t == 0)
        def _():
            carry_ref[...] = jnp.zeros_like(carry_ref)
            dbias_ref[...] = jnp.zeros_like(dbias_ref)
            dsink_ref[...] = jnp.zeros_like(dsink_ref)
            dmemkv_ref[...] = jnp.zeros_like(dmemkv_ref)

        upper = _swa_upper()
        lane = lax.broadcasted_iota(jnp.int32, (1, 128), 1)
        low = _lane_low()
        prev = prev_ref[...].astype(F32)
        proj = proj_ref[...].astype(F32)
        kb = jnp.concatenate([prev[:, :A_KV], proj[:, A_Q:A_Q + A_KV]], axis=0)
        vb = jnp.concatenate([prev[:, A_KV:], proj[:, A_Q + A_KV:A_Q + 2 * A_KV]], axis=0)
        kb_r = pltpu.roll(kb, A_HD, 1)
        vb_r = pltpu.roll(vb, A_HD, 1)
        gw = A_Q // A_KV_HEADS
        groups = range(A_KV_HEADS)
        _, vjp = jax.vjp(
            functools.partial(_mix_a_core, upper=upper, first=i == 0),
            [proj[:, g * gw:(g + 1) * gw] for g in groups], [_both_halves(kb, kb_r, g) for g in groups],
            [_both_halves(vb, vb_r, g) for g in groups], [_swa_sinks(sink_ref, g) for g in groups],
            [bias_ref[g * grp:(g + 1) * grp] for g in groups], proj[:, A_Q + 2 * A_KV:], memkv_ref[:, :X_Q], memkv_ref[:, X_Q:])
        dqs, dk, dv, ds, db, dxq, dmk, dmv = vjp(
            ([dmix_ref[:, g * gw:(g + 1) * gw].astype(F32) for g in groups], dmix_ref[:, A_Q:].astype(F32)))
        dkd = [t + pltpu.roll(t, A_HD, 1) for t in dk]
        dvd = [t + pltpu.roll(t, A_HD, 1) for t in dv]
        dsink = jnp.zeros((1, 128), F32)
        for g in groups:
            for h in range(grp):
                dsink = dsink + jnp.where(lane == g * grp + h, ds[g][h], 0.0)
            dbias_ref[g * grp:(g + 1) * grp] += db[g]
        dsink_ref[...] += dsink
        dkb = jnp.where(low, dkd[0], dkd[1])
        dvb = jnp.where(low, dvd[0], dvd[1])
        dmemkv_ref[...] += jnp.concatenate([dmk, dmv], axis=1)
        dkv_cur = jnp.concatenate([dkb[BLK:], dvb[BLK:]], axis=1) + carry_ref[...]
        carry_ref[...] = jnp.concatenate([dkb[:BLK], dvb[:BLK]], axis=1)
        dproj_ref[...] = jnp.concatenate(list(dqs) + [dkv_cur, dxq], axis=1).astype(dproj_ref.dtype)

    return pl.pallas_call(
        body, grid=(nb,),
        in_specs=[pl.BlockSpec((BLK, IN_A), lambda t: (nb - 1 - t, 0)),
                  pl.BlockSpec((BLK, 2 * A_KV), lambda t: (jnp.maximum(nb - 2 - t, 0), A_Q // (2 * A_KV))),
                  pl.BlockSpec((A_HEADS, BLK, BLK), lambda t: (0, 0, 0)),
                  pl.BlockSpec((1, 128), lambda t: (0, 0)),
                  pl.BlockSpec((MEM_LEN, 2 * X_Q), lambda t: (0, 0)),
                  pl.BlockSpec((BLK, D), lambda t: (nb - 1 - t, 0))] + _dep_specs(deps),
        out_specs=[pl.BlockSpec((BLK, IN_A), lambda t: (nb - 1 - t, 0)),
                   pl.BlockSpec((A_HEADS, BLK, BLK), lambda t: (0, 0, 0)),
                   pl.BlockSpec((1, 128), lambda t: (0, 0)),
                   pl.BlockSpec((MEM_LEN, 2 * X_Q), lambda t: (0, 0))],
        out_shape=[_SDS((s, IN_A), _ACT), _SDS((A_HEADS, BLK, BLK), F32), _SDS((1, 128), F32),
                   _SDS((MEM_LEN, 2 * X_Q), F32)],
        scratch_shapes=[pltpu.VMEM((BLK, 2 * A_KV), F32)],
        name=name, compiler_params=_cp("arbitrary"))(proj, proj, bias, sinks, memkv, dmix, *deps)


def _neumann(pw, rhs):
    nh = len(pw)
    x = rhs
    for lvl in range(6):
        if lvl < 5:
            prod = [_mmf(pw[h], jnp.concatenate([x[h], pw[h]], axis=1)) for h in range(nh)]
            x = [x[h] + prod[h][:, :B_HD] for h in range(nh)]
            pw = [t[:, B_HD:] for t in prod]
        else:
            x = [x[h] + _mmf(pw[h], x[h]) for h in range(nh)]
    return x


@jax.custom_vjp
def _tri_solve(pw, rhs):
    return _neumann(pw, rhs)


def _tri_solve_fwd(pw, rhs):
    x = _neumann(pw, rhs)
    return x, (pw, x)


def _tri_solve_bwd(res, dx):
    pw, x = res
    d_rhs = _neumann([t.T for t in pw], list(dx))
    return [_mmf_nt(d_rhs[h], x[h]) for h in range(len(pw))], d_rhs


_tri_solve.defvjp(_tri_solve_fwd, _tri_solve_bwd)


@jax.custom_vjp
def _tri_solved(pw, rhs, x):
    return x


def _tri_solved_fwd(pw, rhs, x):
    return x, (pw, x)


def _tri_solved_bwd(res, dx):
    d_pw, d_rhs = _tri_solve_bwd(res, dx)
    return d_pw, d_rhs, [jnp.zeros_like(t) for t in res[1]]


_tri_solved.defvjp(_tri_solved_fwd, _tri_solved_bwd)


@jax.custom_vjp
def _known(x, value):
    return value


def _known_fwd(x, value):
    return value, None


def _known_bwd(_, g):
    return g, jnp.zeros_like(g)


_known.defvjp(_known_fwd, _known_bwd)


def _dn_heads(yq, yk, yv, z, bl, al, a_log, dtb, ng, s0, solved=None, out_known=None):
    c = CHUNK
    nh = B_V_HEADS
    rep = B_V_HEADS // B_QK_HEADS
    r = lax.broadcasted_iota(jnp.int32, (c, c), 0)
    cc = lax.broadcasted_iota(jnp.int32, (c, c), 1)
    q = [_silu(t) for t in yq]
    k = [_silu(t) for t in yk]
    v = [_silu(t) for t in yv]
    q = [t * lax.rsqrt(jnp.sum(t * t, axis=-1, keepdims=True) + EPS) * (B_HD ** -0.5) for t in q]
    k = [t * lax.rsqrt(jnp.sum(t * t, axis=-1, keepdims=True) + EPS) for t in k]
    beta = [jax.nn.sigmoid(t) for t in bl]
    g = [-jnp.exp(a_log[h]) * jax.nn.softplus(al[h] + dtb[h]) for h in range(nh)]
    gb = [jnp.broadcast_to(t, (c, c)) for t in g]
    gc_col = [jnp.sum(jnp.where(cc <= r, t.T, 0.0), axis=1, keepdims=True) for t in gb]
    gc_row = [jnp.sum(jnp.where(r <= cc, t, 0.0), axis=0, keepdims=True) for t in gb]
    gc_last = [jnp.sum(t, axis=0, keepdims=True) for t in g]
    decay = [jnp.exp(jnp.where(r >= cc, gc_col[h] - gc_row[h], -jnp.inf)) for h in range(nh)]
    kq = [_mmf_nt(jnp.concatenate([k[h], q[h]], axis=0), k[h]) for h in range(B_QK_HEADS)]
    kk = [t[:c] for t in kq]
    qk = [t[c:] for t in kq]
    egc = [jnp.exp(t) for t in gc_col]
    both = [_mmf(jnp.concatenate([(beta[h] * egc[h]) * k[h // rep], q[h // rep] * egc[h]], axis=0), s0[h]) for h in range(nh)]
    rhs = [beta[h] * v[h] - both[h][:c] for h in range(nh)]
    qs0 = [t[c:] for t in both]
    pw = [-(beta[h] * kk[h // rep] * jnp.where(r > cc, decay[h], 0.0)) for h in range(nh)]
    delta = _tri_solve(pw, rhs) if solved is None else _tri_solved(pw, rhs, solved)
    last = [_mmf(jnp.concatenate([qk[h // rep] * decay[h], (k[h // rep] * jnp.exp(gc_last[h] - gc_col[h])).T], axis=0), delta[h])
            for h in range(nh)]
    out = [qs0[h] + last[h][:c] for h in range(nh)]
    if out_known is not None:
        out = [_known(out[h], out_known[h]) for h in range(nh)]
    s1 = [jnp.exp(gc_last[h]) * s0[h] + last[h][c:] for h in range(nh)]
    o = [t * lax.rsqrt(jnp.mean(t * t, axis=-1, keepdims=True) + EPS) * ng for t in out]
    return [o[h] * _silu(z[h]) for h in range(nh)], s1, delta, out


def _dn_conv(ext, w_ref):
    y = ext * w_ref[B_CONV - 1:B_CONV, :]
    for j in range(B_CONV - 1):
        y = y + w_ref[j:j + 1, :] * pltpu.roll(ext, B_CONV - 1 - j, 0)
    return y


def _dn_args(y, cur_ref, par_ref, ng_ref):
    nh = B_V_HEADS
    return ([y[:, h * B_HD:(h + 1) * B_HD] for h in range(B_QK_HEADS)],
            [y[:, B_QK + h * B_HD:B_QK + (h + 1) * B_HD] for h in range(B_QK_HEADS)],
            [y[:, 2 * B_QK + h * B_HD:2 * B_QK + (h + 1) * B_HD] for h in range(nh)],
            [cur_ref[:, BP_Z + h * B_HD:BP_Z + (h + 1) * B_HD] for h in range(nh)],
            [cur_ref[:, BP_GATE + h:BP_GATE + h + 1] for h in range(nh)],
            [cur_ref[:, BP_GATE + nh + h:BP_GATE + nh + h + 1] for h in range(nh)],
            [par_ref[:, h:h + 1] for h in range(nh)], [par_ref[:, nh + h:nh + h + 1] for h in range(nh)], ng_ref[...])


def _mix_b_fwd(proj, conv_w, par, ng, memkv, name):
    s = proj.shape[0]
    nc = s // CHUNK

    def body(cur_ref, prev_ref, w_ref, par_ref, ng_ref, memkv_ref, o_ref, st_ref, dl_ref, state_ref):
        n = pl.program_id(0)

        @pl.when(n == 0)
        def _():
            state_ref[...] = jnp.zeros_like(state_ref)

        prev = jnp.where(n > 0, prev_ref[...], 0.0)
        ext = jnp.concatenate([prev, cur_ref[:, :B_QKV]], axis=0)
        y = _dn_conv(ext, w_ref)[HALO:]
        s0 = [state_ref[hv] for hv in range(B_V_HEADS)]
        st_ref[0] = state_ref[...]
        outs, s1, delta, raw = _dn_heads(*_dn_args(y, cur_ref, par_ref, ng_ref), s0)
        for hv in range(B_V_HEADS):
            state_ref[hv] = s1[hv]
            dl_ref[0, hv] = delta[hv]
            dl_ref[0, B_V_HEADS + hv] = raw[hv]
        outs = outs + [_cross_pairs(cur_ref[:, BP_XQ:BP_XQ + X_Q], memkv_ref[:, :X_Q], memkv_ref[:, X_Q:])]
        o_ref[...] = jnp.concatenate(outs, axis=1).astype(o_ref.dtype)

    return pl.pallas_call(
        body, grid=(nc,),
        in_specs=[pl.BlockSpec((CHUNK, IN_BP), lambda n: (n, 0)),
                  pl.BlockSpec((HALO, B_QKV), lambda n: (jnp.maximum(n * (CHUNK // HALO) - 1, 0), 0)),
                  pl.BlockSpec((HALO, B_QKV), lambda n: (0, 0)),
                  pl.BlockSpec((1, 128), lambda n: (0, 0)), pl.BlockSpec((1, 128), lambda n: (0, 0)),
                  pl.BlockSpec((MEM_LEN, 2 * X_Q), lambda n: (0, 0))],
        out_specs=[pl.BlockSpec((CHUNK, D), lambda n: (n, 0)),
                   pl.BlockSpec((1, B_V_HEADS, B_HD, B_HD), lambda n: (n, 0, 0, 0)),
                   pl.BlockSpec((1, 2 * B_V_HEADS, CHUNK, B_HD), lambda n: (n, 0, 0, 0))],
        out_shape=[_SDS((s, D), _ACT), _SDS((nc, B_V_HEADS, B_HD, B_HD), F32), _SDS((nc, 2 * B_V_HEADS, CHUNK, B_HD), F32)],
        scratch_shapes=[pltpu.VMEM((B_V_HEADS, B_HD, B_HD), F32)],
        name=name, compiler_params=_cp("arbitrary"))(proj, proj, conv_w, par, ng, memkv)


def _mix_b_bwd(proj, conv_w, par, ng, memkv, states, deltas, dmix, name):
    s = proj.shape[0]
    nc = s // CHUNK
    ext_rows = CHUNK + HALO

    def body(cur_ref, prev_ref, w_ref, par_ref, ng_ref, memkv_ref, st_ref, dl_ref, dmix_ref,
             dproj_ref, dw_ref, dpar_ref, dng_ref, dmemkv_ref, dstate_ref, carry_ref):
        t = pl.program_id(0)
        n = nc - 1 - t

        @pl.when(t == 0)
        def _():
            dstate_ref[...] = jnp.zeros_like(dstate_ref)
            carry_ref[...] = jnp.zeros_like(carry_ref)
            dw_ref[...] = jnp.zeros_like(dw_ref)
            dpar_ref[...] = jnp.zeros_like(dpar_ref)
            dng_ref[...] = jnp.zeros_like(dng_ref)
            dmemkv_ref[...] = jnp.zeros_like(dmemkv_ref)

        lane = lax.broadcasted_iota(jnp.int32, (1, 128), 1)
        prev = jnp.where(n > 0, prev_ref[...], 0.0)
        ext = jnp.concatenate([prev, cur_ref[:, :B_QKV]], axis=0)
        y = _dn_conv(ext, w_ref)[HALO:]
        solved = [dl_ref[0, hv] for hv in range(B_V_HEADS)]
        raw = [dl_ref[0, B_V_HEADS + hv] for hv in range(B_V_HEADS)]
        _, vjp = jax.vjp(functools.partial(_dn_heads, solved=solved, out_known=raw), *_dn_args(y, cur_ref, par_ref, ng_ref),
                         [st_ref[0, hv] for hv in range(B_V_HEADS)])
        none = [jnp.zeros((CHUNK, B_HD), F32)] * B_V_HEADS
        dyq, dyk, dyv, dz, gbl, gal, ga_log, gdtb, dng, gs0 = vjp(
            ([dmix_ref[:, hv * B_HD:(hv + 1) * B_HD].astype(F32) for hv in range(B_V_HEADS)],
             [dstate_ref[hv] for hv in range(B_V_HEADS)], none, none))
        dgate = jnp.zeros((CHUNK, 128), F32)
        dpar = jnp.zeros((1, 128), F32)
        for hv in range(B_V_HEADS):
            dstate_ref[hv] = gs0[hv]
            dgate = dgate + jnp.where(lane == hv, gbl[hv], 0.0) + jnp.where(lane == B_V_HEADS + hv, gal[hv], 0.0)
            dpar = dpar + jnp.where(lane == hv, ga_log[hv], 0.0) + jnp.where(lane == B_V_HEADS + hv, gdtb[hv], 0.0)
        dpar_ref[...] += dpar
        dng_ref[...] += dng
        _, vjp = jax.vjp(_cross_pairs, cur_ref[:, BP_XQ:BP_XQ + X_Q], memkv_ref[:, :X_Q], memkv_ref[:, X_Q:])
        dxq, dmk, dmv = vjp(dmix_ref[:, B_V:].astype(F32))
        dmemkv_ref[...] += jnp.concatenate([dmk, dmv], axis=1)
        dy = jnp.concatenate(list(dyq) + list(dyk) + list(dyv), axis=1)
        dy_ext = jnp.concatenate([jnp.zeros((HALO, B_QKV), F32), dy], axis=0)
        dext = dy_ext * w_ref[B_CONV - 1:B_CONV, :]
        dw_ref[B_CONV - 1:B_CONV, :] += jnp.sum(ext * dy_ext, axis=0, keepdims=True)
        for j in range(B_CONV - 1):
            sh = B_CONV - 1 - j
            dw_ref[j:j + 1, :] += jnp.sum(pltpu.roll(ext, sh, 0) * dy_ext, axis=0, keepdims=True)
            dext = dext + w_ref[j:j + 1, :] * pltpu.roll(dy_ext, ext_rows - sh, 0)
        tail = jnp.concatenate([jnp.zeros((CHUNK - HALO, B_QKV), F32), carry_ref[...]], axis=0)
        dqkv = dext[HALO:] + tail
        carry_ref[...] = dext[:HALO]
        dproj_ref[...] = jnp.concatenate([dqkv] + list(dz) + [dxq, dgate], axis=1).astype(dproj_ref.dtype)

    return pl.pallas_call(
        body, grid=(nc,),
        in_specs=[pl.BlockSpec((CHUNK, IN_BP), lambda t: (nc - 1 - t, 0)),
                  pl.BlockSpec((HALO, B_QKV), lambda t: (jnp.maximum((nc - 1 - t) * (CHUNK // HALO) - 1, 0), 0)),
                  pl.BlockSpec((HALO, B_QKV), lambda t: (0, 0)),
                  pl.BlockSpec((1, 128), lambda t: (0, 0)), pl.BlockSpec((1, 128), lambda t: (0, 0)),
                  pl.BlockSpec((MEM_LEN, 2 * X_Q), lambda t: (0, 0)),
                  pl.BlockSpec((1, B_V_HEADS, B_HD, B_HD), lambda t: (nc - 1 - t, 0, 0, 0)),
                  pl.BlockSpec((1, 2 * B_V_HEADS, CHUNK, B_HD), lambda t: (nc - 1 - t, 0, 0, 0)),
                  pl.BlockSpec((CHUNK, D), lambda t: (nc - 1 - t, 0))],
        out_specs=[pl.BlockSpec((CHUNK, IN_BP), lambda t: (nc - 1 - t, 0)),
                   pl.BlockSpec((HALO, B_QKV), lambda t: (0, 0)),
                   pl.BlockSpec((1, 128), lambda t: (0, 0)), pl.BlockSpec((1, 128), lambda t: (0, 0)),
                   pl.BlockSpec((MEM_LEN, 2 * X_Q), lambda t: (0, 0))],
        out_shape=[_SDS((s, IN_BP), _ACT), _SDS((HALO, B_QKV), F32), _SDS((1, 128), F32), _SDS((1, 128), F32),
                   _SDS((MEM_LEN, 2 * X_Q), F32)],
        scratch_shapes=[pltpu.VMEM((B_V_HEADS, B_HD, B_HD), F32), pltpu.VMEM((HALO, B_QKV), F32)],
        name=name, compiler_params=_cp("arbitrary"))(proj, proj, conv_w, par, ng, memkv, states, deltas, dmix)


def _place():
    return lax.axis_index("x"), lax.axis_index("y"), lax.axis_index("c")


def _all_gather(shards, name):
    n = len(shards)

    def body(*refs):
        ins, outs = refs[:n], refs[n:2 * n]
        send_sems, recv_sems, local_sems = refs[2 * n:]
        x, y, c = _place()
        me, sibling = (x, y, c), (x, y, 1 - c)
        chips = [(1 - x, y), (x, 1 - y), (1 - x, 1 - y)]

        def rows(a, px, py, pc):
            return outs[a].at[4 * px + 2 * py + pc]

        def copy(a, k, block, to, src=None):
            return pltpu.make_async_remote_copy(
                src_ref=rows(a, *block) if src is None else src, dst_ref=rows(a, *block),
                send_sem=send_sems.at[a, k], recv_sem=recv_sems.at[a, k],
                device_id=to, device_id_type=pl.DeviceIdType.MESH)

        mine = [pltpu.make_async_copy(ins[a], rows(a, *me), local_sems.at[a]) for a in range(n)]
        for cp in mine:
            cp.start()
        first = []
        for a in range(n):
            first.append(copy(a, 0, me, sibling, src=ins[a]))
            first += [copy(a, 1 + j, me, (*chip, c), src=ins[a]) for j, chip in enumerate(chips)]
        for cp in first:
            cp.start()
        passed = []
        for j, chip in enumerate(chips):
            for a in range(n):
                copy(a, 1 + j, (*chip, c), me).wait_recv()
                fwd = copy(a, 4 + j, (*chip, c), sibling)
                fwd.start()
                passed.append(fwd)
        for a in range(n):
            copy(a, 0, sibling, me).wait_recv()
            for j, chip in enumerate(chips):
                copy(a, 4 + j, (*chip, 1 - c), me).wait_recv()
        for cp in first + passed:
            cp.wait_send()
        for cp in mine:
            cp.wait()

    hbm = pl.BlockSpec(memory_space=pl.ANY)
    return pl.pallas_call(
        body, out_shape=[_SDS((N_DEV,) + s.shape, s.dtype) for s in shards],
        in_specs=[hbm] * n, out_specs=[hbm] * n,
        scratch_shapes=[pltpu.SemaphoreType.DMA((n, 7)), pltpu.SemaphoreType.DMA((n, 7)), pltpu.SemaphoreType.DMA((n,))],
        name=name)(*shards)


class _Exchange:
    def __init__(self, lands, srcs):
        self.lands, self.srcs = lands, srcs


def _seq_exchange(srcs, land_shapes, plan, name, cid):
    n, nl = len(srcs), len(land_shapes)

    def launch(*refs):
        src_refs, land_refs = refs[:n], refs[n:n + nl]
        send_sems, recv_sems, local_sems = refs[n + nl:]
        x, y, c = _place()
        my = 4 * x + 2 * y + c
        peers = [(x ^ ((k + 1) >> 2 & 1), y ^ ((k + 1) >> 1 & 1), c ^ ((k + 1) & 1)) for k in range(N_DEV - 1)]
        barrier = pltpu.get_barrier_semaphore()
        for p in peers:
            pl.semaphore_signal(barrier, inc=1, device_id=p, device_id_type=pl.DeviceIdType.MESH)
        pl.semaphore_wait(barrier, N_DEV - 1)

        def src_for(a, dest):
            return src_refs[a].at[dest] if plan[a][1] else src_refs[a]

        def slot(a, source):
            return land_refs[plan[a][0]].at[source]

        mine = [pltpu.make_async_copy(src_for(a, my), slot(a, my), local_sems.at[a]) for a in range(n)]
        for cp in mine:
            cp.start()
        sends, recvs = [], []
        for k, (px, py, pc) in enumerate(peers):
            peer = 4 * px + 2 * py + pc
            for a in range(n):
                kw = dict(send_sem=send_sems.at[a * (N_DEV - 1) + k], recv_sem=recv_sems.at[a * (N_DEV - 1) + k],
                          device_id=(px, py, pc), device_id_type=pl.DeviceIdType.MESH)
                sends.append(pltpu.make_async_remote_copy(src_ref=src_for(a, peer), dst_ref=slot(a, my), **kw))
                recvs.append(pltpu.make_async_remote_copy(src_ref=src_for(a, my), dst_ref=slot(a, peer), **kw))
        for cp in sends:
            cp.start()
        for cp in recvs:
            cp.wait_recv()
        for cp in sends:
            cp.wait_send()
        for cp in mine:
            cp.wait()

    lands = pl.kernel(
        launch, out_type=[_SDS(s, d) for s, d in land_shapes],
        mesh=plsc.ScalarSubcoreMesh(axis_name="sequencer", num_cores=1), name=name,
        scratch_types=(pltpu.SemaphoreType.DMA((n * (N_DEV - 1),)), pltpu.SemaphoreType.DMA((n * (N_DEV - 1),)),
                       pltpu.SemaphoreType.DMA((n,))),
        compiler_params=pltpu.CompilerParams(collective_id=cid))(*srcs)
    return _Exchange(list(lands), list(srcs))


def _adam_update(g, w, m, v):
    c1 = 1.0 - ADAM_B1 ** ADAM_STEP
    c2 = 1.0 - ADAM_B2 ** ADAM_STEP
    mm = ADAM_B1 * m + (1.0 - ADAM_B1) * g
    vv = ADAM_B2 * v + (1.0 - ADAM_B2) * (g * g)
    delta = -ADAM_LR * ((mm / c1) / (jnp.sqrt(vv / c2) + ADAM_EPS) + ADAM_WD * w)
    return delta, mm, vv


def _sum_sources(p_ref):
    g = p_ref[0].astype(F32)
    for s in range(1, N_DEV):
        g = g + p_ref[s].astype(F32)
    return g


def _adamw(parts, w, m, v, tr, name, restore_b=False, deps=()):
    nl, r, c = w.shape
    cp = parts[0].shape[-1]

    def body(*refs):
        p_refs = refs[:nl]
        w_ref, m_ref, v_ref = refs[nl:nl + 3]
        g_ref, d_ref, nm_ref, nv_ref = refs[-4:]
        g = _sum_sources(p_refs[0])
        for l in range(1, nl):
            g = jnp.where(pl.program_id(0) == l, _sum_sources(p_refs[l]), g)
        if restore_b:
            g = jnp.concatenate([g[:, :BP_XQ], g[:, BP_GATE:BP_GATE + 2 * B_V_HEADS], g[:, BP_XQ:BP_GATE]], axis=1)
        delta, mm, vv = _adam_update(g, w_ref[...], m_ref[...], v_ref[...])
        g_ref[...] = g
        d_ref[...] = delta
        nm_ref[...] = mm
        nv_ref[...] = vv

    spec = pl.BlockSpec((None, tr, c), lambda l, i: (l, i, 0))
    part_specs = [pl.BlockSpec((N_DEV, tr, cp), functools.partial(lambda l, i, k: (0, jnp.where(l == k, i, 0), 0), k=k))
                  for k in range(nl)]
    return pl.pallas_call(
        body, grid=(nl, r // tr),
        in_specs=part_specs + [spec, spec, spec] + _dep_specs(deps),
        out_specs=[spec] * 4, out_shape=[_SDS(w.shape, F32)] * 4,
        name=name, compiler_params=_cp("arbitrary", "arbitrary"))(*parts, w, m, v, *deps)


def _pack_small(d_rel, d_cb, d_cw, d_qkv, d_mix, d_mem, d_ffn, d_final, d_sinks, d_par, d_ng, loss_row, name):
    flat = [d_rel, *d_cb, *d_cw, d_qkv, *d_mix, *d_mem, *d_ffn, d_final, d_sinks, d_par, d_ng, loss_row]
    n = len(flat)

    def body(*refs):
        ins, o_ref = refs[:n], refs[n]
        rel, cb0, cb1, cw0, cw1, qkv, mx0, mx1, me0, me1, ff0, ff1, fin, snk, par, ng, lss = ins
        o_ref[...] = jnp.zeros_like(o_ref)
        for k in range(N_BUCKETS):
            lane = SP_REL_LANE + 128 * (k % 8)
            o_ref[SP_QKV + k // 8:SP_QKV + k // 8 + 1, lane:lane + 128] = rel[k:k + 1, :]
        for l, (cb, cw) in enumerate(((cb0, cw0), (cb1, cw1))):
            o_ref[SP_CB + l:SP_CB + l + 1, :] = jnp.concatenate([cb[j] for j in range(FF_BLOCKS)], axis=1)
            full = jnp.concatenate([cw[j] for j in range(FF_BLOCKS)], axis=1)
            o_ref[SP_CW + FFN_CONV * l:SP_CW + FFN_CONV * (l + 1), :] = full[:FFN_CONV]
        o_ref[SP_QKV:SP_QKV + B_CONV, 0:B_QKV] = qkv[0:B_CONV, :]
        for base, pair in ((SP_MIX, (mx0, mx1)), (SP_MEM, (me0, me1)), (SP_FFN, (ff0, ff1))):
            for l in range(2):
                o_ref[base + l:base + l + 1, 0:D] = pair[l][...]
        o_ref[SP_FINAL:SP_FINAL + 1, 0:D] = fin[...]
        o_ref[SP_MISC:SP_MISC + 1, 0:128] = snk[...]
        o_ref[SP_MISC:SP_MISC + 1, 128:256] = par[...]
        o_ref[SP_MISC:SP_MISC + 1, 256:384] = ng[...]
        o_ref[SP_MISC:SP_MISC + 1, 384:512] = lss[...]

    vm = pl.BlockSpec(memory_space=pltpu.VMEM)
    return pl.pallas_call(body, in_specs=[vm] * n, out_specs=vm, out_shape=_SDS((SMALL_ROWS, D_FF), F32), name=name)(*flat)


_SMALL = ["rel_bias", "norm_mix_g", "norm_mem_g", "sinks_a", "a_log_b", "dt_bias_b", "out_norm_g_b", "norm_ffn_g",
          "ffn_conv_b", "final_norm_g", "conv_qkv_b", "ffn_conv_w"]


def _adamw_small(recv, rc_qkv, rc_ffn, ws, ms, vs, name, deps=()):
    n = len(_SMALL)

    def body(*refs):
        recv_ref, qkv_ref, ffn_ref = refs[:3]
        w_refs, m_refs, v_refs = refs[3:3 + n], refs[3 + n:3 + 2 * n], refs[3 + 2 * n:3 + 3 * n]
        outs, loss_ref = refs[len(refs) - 4 * n - 1:len(refs) - 1], refs[-1]
        gs = _sum_sources(recv_ref)
        loss_ref[...] = gs[SP_MISC:SP_MISC + 1, 384:512]
        grads = {
            "rel_bias": jnp.concatenate(
                [gs[SP_QKV + k // 8:SP_QKV + k // 8 + 1, SP_REL_LANE + 128 * (k % 8):SP_REL_LANE + 128 * (k % 8) + A_HEADS]
                 for k in range(N_BUCKETS)], axis=0),
            "norm_mix_g": gs[SP_MIX:SP_MIX + 2, 0:D], "norm_mem_g": gs[SP_MEM:SP_MEM + 2, 0:D],
            "sinks_a": gs[SP_MISC:SP_MISC + 1, 0:A_HEADS],
            "a_log_b": gs[SP_MISC:SP_MISC + 1, 128:128 + B_V_HEADS],
            "dt_bias_b": gs[SP_MISC:SP_MISC + 1, 128 + B_V_HEADS:128 + 2 * B_V_HEADS],
            "out_norm_g_b": gs[SP_MISC:SP_MISC + 1, 256:256 + B_HD],
            "norm_ffn_g": gs[SP_FFN:SP_FFN + 2, 0:D], "ffn_conv_b": gs[SP_CB:SP_CB + 2, :],
            "final_norm_g": gs[SP_FINAL:SP_FINAL + 1, 0:D],
            "conv_qkv_b": _sum_sources(qkv_ref), "ffn_conv_w": _sum_sources(ffn_ref),
        }
        for i, nm in enumerate(_SMALL):
            g = grads[nm]
            delta, mm, vv = _adam_update(g, w_refs[i][...], m_refs[i][...], v_refs[i][...])
            outs[i][...] = g
            outs[n + i][...] = delta
            outs[2 * n + i][...] = mm
            outs[3 * n + i][...] = vv

    vm = pl.BlockSpec(memory_space=pltpu.VMEM)
    shapes = [_SDS(w.shape, F32) for w in ws]
    return pl.pallas_call(
        body, in_specs=[vm] * (3 + 3 * n) + _dep_specs(deps), out_specs=[vm] * (4 * n + 1),
        out_shape=shapes * 4 + [_SDS((1, 128), F32)],
        name=name)(recv, rc_qkv, rc_ffn, *ws, *ms, *vs, *deps)


def _assemble(gathered, axis):
    g = jnp.moveaxis(gathered, 0, axis)
    shp = list(g.shape)
    return g.reshape(shp[:axis] + [shp[axis] * shp[axis + 1]] + shp[axis + 2:])


def _pad_rows(a, rows):
    return jnp.pad(a, ((0, rows - a.shape[0]), (0, 0)))


def _pad_lanes(a, lanes=128):
    return jnp.pad(a, ((0, 0), (0, lanes - a.shape[1])))


def _ff_blocks(a):
    return jnp.moveaxis(a.reshape(a.shape[0], FF_BLOCKS, GU_SHARD), 1, 0)


def _reorder_b(w):
    qkv_z = w[..., :B_QKV + B_V]
    gates = w[..., B_QKV + B_V:B_QKV + B_V + 2 * B_V_HEADS]
    xq = w[..., IN_B - X_Q:]
    pad = jnp.zeros(w.shape[:-1] + (IN_BP - IN_B,), w.dtype)
    return jnp.concatenate([qkv_z, xq, gates, pad], axis=-1)


def kernel(x, mem, rel_bias, norm_mix_g, norm_mem_g, w_mem_kv, w_out, w_in_a, sinks_a, w_in_b, conv_qkv_b, a_log_b, dt_bias_b, out_norm_g_b, norm_ffn_g, w_gate_up, ffn_conv_w, ffn_conv_b, w_down, final_norm_g, loss_target, m_rel_bias, m_norm_mix_g, m_norm_mem_g, m_w_mem_kv, m_w_out, m_w_in_a, m_sinks_a, m_w_in_b, m_conv_qkv_b, m_a_log_b, m_dt_bias_b, m_out_norm_g_b, m_norm_ffn_g, m_w_gate_up, m_ffn_conv_w, m_ffn_conv_b, m_w_down, m_final_norm_g, v_rel_bias, v_norm_mix_g, v_norm_mem_g, v_w_mem_kv, v_w_out, v_w_in_a, v_sinks_a, v_w_in_b, v_conv_qkv_b, v_a_log_b, v_dt_bias_b, v_out_norm_g_b, v_norm_ffn_g, v_w_gate_up, v_ffn_conv_w, v_ffn_conv_b, v_w_down, v_final_norm_g):
    local = dict(locals())
    order = ["rel_bias", "norm_mix_g", "norm_mem_g", "w_mem_kv", "w_out", "w_in_a", "sinks_a", "w_in_b", "conv_qkv_b",
             "a_log_b", "dt_bias_b", "out_norm_g_b", "norm_ffn_g", "w_gate_up", "ffn_conv_w", "ffn_conv_b", "w_down",
             "final_norm_g"]
    wts = {n: local[n] for n in order}
    moms = {n: local["m_" + n] for n in order}
    vars_ = {n: local["v_" + n] for n in order}
    h0 = x[0]
    memx = mem[0]
    tgt = loss_target[0]
    s = h0.shape[0]
    tm = _rows(s)
    tb = min(s, _TM_BIG)

    t_ = lambda a: jnp.swapaxes(a, 1, 2)
    g_mk0, g_out0, g_ia, g_cq, g_cw = _all_gather(
        [w_mem_kv[0:1].astype(_MXU), w_out[0:1].astype(_MXU), t_(w_in_a).astype(_MXU), conv_qkv_b, ffn_conv_w], "gather_first")
    g_mk, g_out = [g_mk0], [g_out0]
    gu_land = ((N_DEV, GU_SHARD, D), _MXU)
    dn_land = ((N_DEV, DN_SHARD, D), _MXU)
    whole = [(0, False), (1, False)]
    def after(a, b):
        return a + (b[(0,) * b.ndim] * 0).astype(a.dtype)

    gu0_w = _seq_exchange([after(t_(w_gate_up)[0].astype(_MXU), g_ia)], [gu_land], [(0, False)], "gather_gate_up0", 1)
    dn0_w = _seq_exchange([after(w_down[0].astype(_MXU), g_ia)], [dn_land], [(0, False)], "gather_down0", 8)
    w_ia = g_ia.reshape(IN_A, D)
    conv_qkv = _pad_rows(_assemble(g_cq, 2)[0], HALO)
    ffn_cw_full = _assemble(g_cw, 2)
    ffn_cw = [_ff_blocks(_pad_rows(ffn_cw_full[i], HALO)) for i in range(2)]
    ffn_cb = [_ff_blocks(ffn_conv_b[i:i + 1]) for i in range(2)]
    bucket = jnp.asarray(_bucket_table())
    bias = _bias_build(rel_bias, bucket, "bias_build")
    sinks = _pad_lanes(sinks_a)
    par_b = _pad_lanes(jnp.concatenate([a_log_b, dt_bias_b], axis=1))

    row_x = pl.BlockSpec((tm, D), lambda i, j: (i, 0))
    gu_shape = (2, FF_BLOCKS, s, GU_SHARD)

    def in_proj(h, g, w, w_spec, n_cols, tn, name, deps=(), out_dtype=F32, w_t=False, tm=None):
        return _norm_matmul(h, g, w, w_spec, n_cols // tn, (h.shape[0], n_cols),
                            pl.BlockSpec((tm or _rows(h.shape[0]), tn), lambda i, j: (i, j)), name, deps=deps, out_dtype=out_dtype,
                            w_t=w_t, tm=tm)

    def ffn_fwd(i, h, g_gu, g_dn, deps=()):
        gu, hn = _norm_matmul(h, norm_ffn_g[i:i + 1], g_gu, _spec_gate_up(1), N_DEV, gu_shape,
                              _spec_gu_act(0, 1, tb), f"gate_up_{i}", deps=deps, out_dtype=_ACT, w_t=True, tm=tb)
        h_new, act, gc = _glu_down(gu, ffn_cw[i], ffn_cb[i], g_dn, h, f"glu_down_{i}")
        return h_new, gu, hn, (act, gc)

    def out_proj(i, mix, h):
        return _matmul_res(mix, row_x, g_out[i], _spec_rowsharded(0, D // N_DEV, D), 1, h, f"out_proj_{i}")

    proj_a, hn_a = in_proj(h0, norm_mix_g[0:1], w_ia, pl.BlockSpec((640, D), lambda i, j: (j, 0)), IN_A, 640, "in_proj_a",
                           deps=gu0_w.srcs + dn0_w.srcs, out_dtype=_ACT, w_t=True)
    memkv0, memn0 = in_proj(memx, norm_mem_g[0:1], g_mk[0], _spec_rowsharded(0, D // N_DEV, 2 * X_Q), 2 * X_Q, 2 * X_Q, "mem_proj_0")
    mix_a = _mix_a_fwd(proj_a, bias, sinks, memkv0, "mix_a_fwd")
    h1 = out_proj(0, mix_a, h0)
    g_gu0, g_dn0 = gu0_w.lands[0], dn0_w.lands[0]
    in_b_w = _seq_exchange([after(_reorder_b(w_in_b).astype(_MXU), h1), after(w_mem_kv[1:2].astype(_MXU), h1),
                            after(w_out[1:2].astype(_MXU), h1)],
                           [((N_DEV, 1, D // N_DEV, IN_BP), _MXU), ((N_DEV, 1, D // N_DEV, 2 * X_Q), _MXU),
                            ((N_DEV, 1, D // N_DEV, D), _MXU)], [(0, False), (1, False), (2, False)], "gather_in_b", 2)
    ffn1_w = _seq_exchange([after(t_(w_gate_up)[1].astype(_MXU), h1), after(w_down[1].astype(_MXU), h1)], [gu_land, dn_land], whole,
                           "gather_ffn1", 3)
    h2, gu0, hn_f0, act0 = ffn_fwd(0, h1, g_gu0, g_dn0, deps=in_b_w.srcs + ffn1_w.srcs)
    g_ib, g_mk1, g_out1 = in_b_w.lands
    g_mk.append(g_mk1)
    g_out.append(g_out1)
    proj_b, hn_b = in_proj(h2, norm_mix_g[1:2], g_ib, _spec_rowsharded(0, D // N_DEV, 896, col_block=1), IN_BP, 896, "in_proj_b")
    memkv1, memn1 = in_proj(memx, norm_mem_g[1:2], g_mk[1], _spec_rowsharded(0, D // N_DEV, 2 * X_Q), 2 * X_Q, 2 * X_Q, "mem_proj_1",
                            deps=[h2])
    mix_b, states, deltas = _mix_b_fwd(proj_b, conv_qkv, par_b, out_norm_g_b, memkv1, "mix_b_fwd")
    h3 = out_proj(1, mix_b, h2)
    g_gu1, g_dn1 = ffn1_w.lands
    h4, gu1, hn_f1, act1 = ffn_fwd(1, h3, g_gu1, g_dn1)
    loss_row, dh, d_final_g = _loss_head(h4, final_norm_g[None, :], tgt, "loss_head")

    zeros_mem = jnp.zeros_like(memx)
    per_dest2 = [(0, True), (1, True)]

    def ffn_bwd(i, dh, h_in, gu, hn_f, act_gc, g_gu, g_dn, deps=()):
        act, gc = act_gc
        dgu, d_cw, d_cb = _glu_bwd(gu, gc, ffn_cw[i], dh, g_dn, f"glu_bwd_{i}", deps=deps)
        d_wdown = _matmul_tn(act, pl.BlockSpec((None, tm, GU_SHARD), lambda j, r: (j, r, 0)),
                             dh, pl.BlockSpec((tm, D), lambda j, r: (r, 0)), s, FF_BLOCKS, (GU_SHARD, D),
                             (N_DEV, DN_SHARD, D), pl.BlockSpec((2, DN_SHARD, D), lambda j, r: (j, 0, 0)), f"d_w_down_{i}")
        dh_new, d_g = _matmul_nt_normbwd(dgu, _spec_gu_act(0, 1, tm), g_gu, _spec_gate_up(1), N_DEV, h_in,
                                         norm_ffn_g[i:i + 1], dh, f"d_ffn_in_{i}", w_t=True)
        d_wgu = _matmul_tn(dgu, _spec_gu_act(1, 0, tb), hn_f, pl.BlockSpec((tb, D), lambda j, r: (r, 0)), s, N_DEV,
                           (GU_SHARD, D), (N_DEV, GU_SHARD, D), pl.BlockSpec((None, GU_SHARD, D), lambda j, r: (j, 0, 0)),
                           f"d_w_gate_up_{i}", tm=tb)
        return dh_new, [d_wdown, d_wgu], d_cw, d_cb, d_g

    def out_bwd(i, dh, mix, deps):
        dmix = _matmul_nt(dh, g_out[i], _spec_rowsharded(0, D // N_DEV, D), 1, (s, D), row_x, f"d_mix_{i}", deps=deps, out_dtype=_ACT)
        d_wout = _matmul_tn(mix, pl.BlockSpec((tm, D), lambda j, r: (r, 0)), dh, pl.BlockSpec((tm, D), lambda j, r: (r, 0)),
                            s, 1, (D, D), (N_DEV, D // N_DEV, D), pl.BlockSpec((N_DEV, D // N_DEV, D), lambda j, r: (0, 0, 0)),
                            f"d_w_out_{i}")
        return dmix, d_wout

    def mem_bwd(i, dmemkv, memn):
        tmm = _rows(MEM_LEN)
        _, d_g = _matmul_nt_normbwd(dmemkv, pl.BlockSpec((tmm, 2 * X_Q), lambda r, j: (r, 0)), g_mk[i],
                                    _spec_rowsharded(0, D // N_DEV, 2 * X_Q), 1, memx, norm_mem_g[i:i + 1], zeros_mem,
                                    f"d_mem_in_{i}")
        by_row = lambda j, r: (r, 0)
        d_w = _matmul_tn(memn, pl.BlockSpec((tmm, D), by_row), dmemkv, pl.BlockSpec((tmm, 2 * X_Q), by_row), MEM_LEN, 1,
                         (D, 2 * X_Q), (N_DEV, D // N_DEV, 2 * X_Q),
                         pl.BlockSpec((N_DEV, D // N_DEV, 2 * X_Q), lambda j, r: (0, 0, 0)), f"d_w_mem_kv_{i}")
        return d_w, d_g

    out_land = ((N_DEV, D // N_DEV, D), _WIRE)
    mk_land = ((N_DEV, D // N_DEV, 2 * X_Q), _WIRE)
    ffn_lands = [((N_DEV, DN_SHARD, D), _WIRE), ((N_DEV, GU_SHARD, D), _WIRE)]
    dh, d_ffn1, d_cw1, d_cb1, d_gf1 = ffn_bwd(1, dh, h3, gu1, hn_f1, act1, g_gu1, g_dn1)
    ffn1_g = _seq_exchange(d_ffn1, ffn_lands, per_dest2, "send_ffn1_grads", 5)
    dmix, d_wout1 = out_bwd(1, dh, mix_b, ffn1_g.srcs)
    dproj_b, d_convw, d_par, d_ng, dmemkv1 = _mix_b_bwd(proj_b, conv_qkv, par_b, out_norm_g_b, memkv1, states, deltas, dmix, "mix_b_bwd")
    dh, d_gm1 = _matmul_nt_normbwd(dproj_b, pl.BlockSpec((tm, 896), lambda i, j: (i, j)), g_ib,
                                   _spec_rowsharded(0, D // N_DEV, 896, col_block=1), IN_BP // 896, h2, norm_mix_g[1:2], dh, "d_in_b")
    d_wib = _matmul_tn(hn_b, pl.BlockSpec((tb, D), lambda j, r: (r, 0)), dproj_b, pl.BlockSpec((tb, 896), lambda j, r: (r, j)),
                       s, IN_BP // 896, (D, 896), (N_DEV, D // N_DEV, IN_BP),
                       pl.BlockSpec((N_DEV, D // N_DEV, 896), lambda j, r: (0, 0, j)), "d_w_in_b", tm=tb)
    d_wmk1, d_gmem1 = mem_bwd(1, dmemkv1, memn1)
    mix1_g = _seq_exchange([d_wout1, d_wib, d_wmk1], [out_land, ((N_DEV, D // N_DEV, IN_BP), _WIRE), mk_land],
                           [(0, True), (1, True), (2, True)], "send_mix1_grads", 6)
    dh, d_ffn0, d_cw0, d_cb0, d_gf0 = ffn_bwd(0, dh, h1, gu0, hn_f0, act0, g_gu0, g_dn0, deps=mix1_g.srcs)
    dmix, d_wout0 = out_bwd(0, dh, mix_a, d_ffn0 + ffn1_g.lands[:1])
    ffn0_g = _seq_exchange(d_ffn0 + [d_wout0], ffn_lands + [out_land], per_dest2 + [(2, True)], "send_ffn0_grads", 4)
    dproj_a, dbias, dsinks, dmemkv0 = _mix_a_bwd(proj_a, bias, sinks, memkv0, dmix, "mix_a_bwd", deps=ffn0_g.srcs)
    dh, d_gm0 = _matmul_nt_normbwd(dproj_a, pl.BlockSpec((tm, 640), lambda i, j: (i, j)), w_ia,
                                   pl.BlockSpec((640, D), lambda i, j: (j, 0)), IN_A // 640, h0, norm_mix_g[0:1], dh, "d_in_a",
                                   w_t=True)
    d_wia = _matmul_tn(dproj_a, pl.BlockSpec((tm, IN_A), lambda j, r: (r, 0)), hn_a, pl.BlockSpec((tm, D), lambda j, r: (r, 0)),
                       s, 1, (IN_A, D), (N_DEV, IA_SHARD, D), pl.BlockSpec((N_DEV, IA_SHARD, D), lambda j, r: (0, 0, 0)),
                       "d_w_in_a")
    d_wmk0, d_gmem0 = mem_bwd(0, dmemkv0, memn0)
    d_rel = _bias_reduce(dbias, bucket, "bias_reduce")
    small = _pack_small(d_rel, (d_cb0, d_cb1), (d_cw0, d_cw1), d_convw, (d_gm0, d_gm1), (d_gmem0, d_gmem1),
                        (d_gf0, d_gf1), d_final_g, dsinks, d_par, d_ng, loss_row, "pack_small")
    mix0_g = _seq_exchange([d_wia, d_wmk0, small],
                           [((N_DEV, IA_SHARD, D), _WIRE), mk_land, ((N_DEV, SMALL_ROWS, D_FF), F32)],
                           [(0, True), (1, True), (2, False)], "send_mix0_grads", 7)

    res = {}
    last = []

    def update(nm, parts, tr, restore=False, transposed=False):
        view = t_ if transposed else (lambda a: a)
        out = _adamw(parts, view(wts[nm]), view(moms[nm]), view(vars_[nm]), tr, "adamw_" + nm, restore_b=restore, deps=last[-1:])
        res[nm] = [view(o) for o in out]
        last.append(out[1])

    r_dn1, r_gu1 = ffn1_g.lands
    r_dn0, r_gu0, r_out0 = ffn0_g.lands
    r_out1, r_ib, r_mk1 = mix1_g.lands
    update("w_in_b", [r_ib], 32, True)
    update("w_gate_up", [r_gu0, r_gu1], 176, transposed=True)
    update("w_down", [r_dn0, r_dn1], 176)
    r_ia, r_mk0, r_small = mix0_g.lands
    update("w_mem_kv", [r_mk0, r_mk1], 128)
    update("w_out", [r_out0, r_out1], 128)
    update("w_in_a", [r_ia], IA_SHARD, transposed=True)

    my = 4 * lax.axis_index("x") + 2 * lax.axis_index("y") + lax.axis_index("c")
    cq = conv_qkv_b.shape[-1]
    cf = ffn_conv_w.shape[-1]
    rc_qkv = lax.dynamic_slice_in_dim(r_small[:, SP_QKV:SP_QKV + B_CONV, :B_QKV], my * cq, cq, axis=2)[:, None]
    rc_ffn = lax.dynamic_slice_in_dim(r_small[:, SP_CW:SP_CW + 2 * FFN_CONV, :], my * cf, cf, axis=2).reshape(N_DEV, 2, FFN_CONV, cf)
    as2d = lambda a: a[None, :] if a.ndim == 1 else a
    small_out = _adamw_small(r_small, rc_qkv, rc_ffn, [as2d(wts[n]) for n in _SMALL], [as2d(moms[n]) for n in _SMALL],
                             [as2d(vars_[n]) for n in _SMALL], "adamw_small", deps=last[-1:])
    ns = len(_SMALL)
    for i, nm in enumerate(_SMALL):
        res[nm] = [small_out[k * ns + i].reshape(wts[nm].shape) for k in range(4)]

    return (small_out[-1][0, 0], dh[None], *[res[n][0] for n in order], *[res[n][1] for n in order],
            *[res[n][2] for n in order], *[res[n][3] for n in order])
```

```python
import functools
import math

import numpy as np

import jax
import jax.numpy as jnp
from jax import lax
from jax.experimental import pallas as pl
from jax.experimental.pallas import tpu as pltpu
from jax.experimental.pallas import tpu_sc as plsc

F32 = jnp.float32
_MXU = jnp.bfloat16
_ACT = jnp.bfloat16
_WIRE = jnp.bfloat16
_HI = lax.Precision.HIGH
_TM = 1024
_TM_GLU = 512
_TM_BIG = 2048
_VMEM_LIMIT = 48 * 1024 * 1024
_SDS = jax.ShapeDtypeStruct

D = 1024
EPS = 1e-6
A_HEADS, A_KV_HEADS, A_HD, BLK = 12, 2, 64, 128
N_BUCKETS, MAX_DISTANCE = 32, 128
B_QK_HEADS, B_V_HEADS, B_HD, B_CONV, CHUNK = 3, 6, 128, 4, 64
X_HEADS, X_HD, MEM_LEN = 4, 64, 256
D_FF, FFN_CONV = 2816, 3
A_Q, A_KV, X_Q = 768, 128, 256
B_QK, B_V, B_QKV = 384, 768, 1536
IN_A, IN_B = 1280, 2572
IN_BP = 2688
BP_Z, BP_XQ, BP_GATE = 1536, 2304, 2560
HALO = 8
GLU_HALO = 16

N_DEV = 8
GU_SHARD = 2 * D_FF // N_DEV
FF_BLOCKS = D_FF // GU_SHARD
DN_SHARD = D_FF // N_DEV
IA_SHARD = IN_A // N_DEV

ADAM_LR, ADAM_B1, ADAM_B2, ADAM_EPS, ADAM_WD, ADAM_STEP = 0.001, 0.9, 0.999, 1e-08, 0.01, 10

SP_CB, SP_CW, SP_QKV, SP_MIX, SP_MEM, SP_FFN, SP_FINAL, SP_MISC, SMALL_ROWS = 0, 2, 8, 12, 14, 16, 18, 19, 24
SP_REL_LANE = B_QKV


def _cp(*sems):
    return pltpu.CompilerParams(dimension_semantics=sems, vmem_limit_bytes=_VMEM_LIMIT)


def _mm(a, b):
    return jnp.dot(a.astype(_MXU), b.astype(_MXU), preferred_element_type=F32)


def _mm_nt(a, b):
    return lax.dot_general(a.astype(_MXU), b.astype(_MXU), (((1,), (1,)), ((), ())), preferred_element_type=F32)


def _mm_tn(a, b):
    return lax.dot_general(a.astype(_MXU), b.astype(_MXU), (((0,), (0,)), ((), ())), preferred_element_type=F32)


def _mmf(a, b):
    return jnp.dot(a, b, preferred_element_type=F32, precision=_HI)


def _mmf_nt(a, b):
    return lax.dot_general(a, b, (((1,), (1,)), ((), ())), preferred_element_type=F32, precision=_HI)


def _silu(x):
    return x * jax.nn.sigmoid(x)


def _w2d(ref):
    v = ref[...]
    return v.reshape(-1, v.shape[-1])


def _rows(m):
    return min(m, _TM)


def _spec_rowsharded(layer, rows, cols, col_block=None):
    if col_block is None:
        return pl.BlockSpec((N_DEV, None, rows, cols), lambda *_: (0, layer, 0, 0))
    return pl.BlockSpec((N_DEV, None, rows, cols), lambda *ids: (0, layer, 0, ids[col_block]))


def _spec_gate_up(axis):
    return pl.BlockSpec((None, GU_SHARD, D), lambda *ids: (ids[axis], 0, 0))


def _spec_down(axis):
    return pl.BlockSpec((2, DN_SHARD, D), lambda *ids: (ids[axis], 0, 0))


def _dep_specs(deps):
    return [pl.BlockSpec(memory_space=pl.ANY) for d in deps]


def _spec_gu_act(row_axis, axis, tm):
    return pl.BlockSpec((None, None, tm, GU_SHARD), lambda *ids: (ids[axis] // FF_BLOCKS, ids[axis] % FF_BLOCKS, ids[row_axis], 0))


def _norm_matmul(x, g, w, w_spec, n_blocks, out_shape, out_spec, name, deps=(), out_dtype=F32, w_t=False, tm=None):
    m, k = x.shape
    tm = tm or _rows(m)

    def body(x_ref, g_ref, w_ref, *rest):
        y_ref, hn_ref = rest[-2:]

        @pl.when(pl.program_id(1) == 0)
        def _():
            xv = x_ref[...]
            r = lax.rsqrt(jnp.mean(xv * xv, axis=-1, keepdims=True) + EPS)
            hn_ref[...] = (xv * r * g_ref[...]).astype(hn_ref.dtype)

        y_ref[...] = (_mm_nt if w_t else _mm)(hn_ref[...], _w2d(w_ref)).astype(y_ref.dtype)

    return pl.pallas_call(
        body, grid=(m // tm, n_blocks),
        in_specs=[pl.BlockSpec((tm, k), lambda i, j: (i, 0)), pl.BlockSpec((1, k), lambda i, j: (0, 0)), w_spec]
        + _dep_specs(deps),
        out_specs=[out_spec, pl.BlockSpec((tm, k), lambda i, j: (i, 0))],
        out_shape=[_SDS(out_shape, out_dtype), _SDS((m, k), _ACT)],
        name=name, compiler_params=_cp("arbitrary", "arbitrary"))(x, g, w, *deps)


def _matmul_res(a, a_spec, w, w_spec, n_k, res, name):
    m, n = res.shape
    tm = _rows(m)

    def body(a_ref, w_ref, r_ref, o_ref):
        part = _mm(a_ref[...], _w2d(w_ref))

        @pl.when(pl.program_id(1) == 0)
        def _():
            o_ref[...] = r_ref[...] + part

        @pl.when(pl.program_id(1) > 0)
        def _():
            o_ref[...] += part

    return pl.pallas_call(
        body, grid=(m // tm, n_k),
        in_specs=[a_spec, w_spec, pl.BlockSpec((tm, n), lambda i, j: (i, 0))],
        out_specs=pl.BlockSpec((tm, n), lambda i, j: (i, 0)),
        out_shape=_SDS((m, n), F32), name=name, compiler_params=_cp("arbitrary", "arbitrary"))(a, w, res)


def _matmul_nt(dy, w, w_spec, n_blocks, out_shape, out_spec, name, deps=(), out_dtype=F32):
    m, n = dy.shape
    tm = _rows(m)

    def body(dy_ref, w_ref, *rest):
        o_ref = rest[-1]
        o_ref[...] = _mm_nt(dy_ref[...], _w2d(w_ref)).astype(o_ref.dtype)

    return pl.pallas_call(
        body, grid=(m // tm, n_blocks),
        in_specs=[pl.BlockSpec((tm, n), lambda i, j: (i, 0)), w_spec] + _dep_specs(deps),
        out_specs=out_spec, out_shape=_SDS(out_shape, out_dtype),
        name=name, compiler_params=_cp("arbitrary", "arbitrary"))(dy, w, *deps)


def _matmul_nt_normbwd(dy, dy_spec, w, w_spec, nj, h, g, dh_in, name, w_t=False, act_copy=False):
    m, k = h.shape
    tm = _rows(m)

    def body(dy_ref, w_ref, h_ref, g_ref, dhin_ref, dh_ref, *rest):
        dg_ref, acc_ref = rest[-2:]
        i, j = pl.program_id(0), pl.program_id(1)

        @pl.when(j == 0)
        def _():
            acc_ref[...] = jnp.zeros_like(acc_ref)

        acc_ref[...] += (_mm if w_t else _mm_nt)(dy_ref[...], _w2d(w_ref))

        @pl.when(j == nj - 1)
        def _():
            xv = h_ref[...]
            r = lax.rsqrt(jnp.mean(xv * xv, axis=-1, keepdims=True) + EPS)
            xh = xv * r
            dhn = acc_ref[...]
            part = jnp.sum(dhn * xh, axis=0, keepdims=True)

            @pl.when(i == 0)
            def _():
                dg_ref[...] = part

            @pl.when(i > 0)
            def _():
                dg_ref[...] += part

            t = dhn * g_ref[...]
            dh = dhin_ref[...] + r * (t - xh * jnp.mean(t * xh, axis=-1, keepdims=True))
            dh_ref[...] = dh
            if act_copy:
                rest[0][...] = dh.astype(_ACT)

    rows = pl.BlockSpec((tm, k), lambda i, j: (i, 0))
    outs = pl.pallas_call(
        body, grid=(m // tm, nj),
        in_specs=[dy_spec, w_spec, rows, pl.BlockSpec((1, k), lambda i, j: (0, 0)), rows],
        out_specs=[rows] + [rows] * act_copy + [pl.BlockSpec((1, k), lambda i, j: (0, 0))],
        out_shape=[_SDS((m, k), F32)] + [_SDS((m, k), _ACT)] * act_copy + [_SDS((1, k), F32)],
        scratch_shapes=[pltpu.VMEM((tm, k), F32)],
        name=name, compiler_params=_cp("arbitrary", "arbitrary"))(dy, w, h, g, dh_in)
    return outs[0], (outs[1] if act_copy else None), outs[-1]


def _matmul_tn(x, x_spec, dy, dy_spec, m, n_blocks, acc_shape, out_shape, out_spec, name, tm=None):
    tm = tm or _rows(m)
    nm = m // tm

    def body(x_ref, dy_ref, o_ref, acc_ref):
        @pl.when(pl.program_id(1) == 0)
        def _():
            acc_ref[...] = jnp.zeros_like(acc_ref)

        acc_ref[...] += _mm_tn(x_ref[...], dy_ref[...])

        @pl.when(pl.program_id(1) == nm - 1)
        def _():
            o_ref[...] = acc_ref[...].reshape(o_ref.shape).astype(o_ref.dtype)

    return pl.pallas_call(
        body, grid=(n_blocks, nm), in_specs=[x_spec, dy_spec], out_specs=out_spec,
        out_shape=_SDS(out_shape, _WIRE), scratch_shapes=[pltpu.VMEM(acc_shape, F32)],
        name=name, compiler_params=_cp("arbitrary", "arbitrary"))(x, dy)


def _loss_head(h, g, tgt, name):
    m, k = h.shape
    tm = _rows(m)

    def body(h_ref, g_ref, t_ref, loss_ref, dh_ref, dha_ref, dg_ref):
        i = pl.program_id(0)
        xv = h_ref[...]
        r = lax.rsqrt(jnp.mean(xv * xv, axis=-1, keepdims=True) + EPS)
        xh = xv * r
        gv = g_ref[...]
        err = xh * gv - t_ref[...]
        lpart = jnp.zeros((1, 128), F32) + 0.5 * jnp.sum(jnp.mean(err * err, axis=-1, keepdims=True), axis=0, keepdims=True)
        dy = err * (1.0 / k)
        gpart = jnp.sum(dy * xh, axis=0, keepdims=True)

        @pl.when(i == 0)
        def _():
            loss_ref[...] = lpart
            dg_ref[...] = gpart

        @pl.when(i > 0)
        def _():
            loss_ref[...] += lpart
            dg_ref[...] += gpart

        t = dy * gv
        dh = r * (t - xh * jnp.mean(t * xh, axis=-1, keepdims=True))
        dh_ref[...] = dh
        dha_ref[...] = dh.astype(_ACT)

    rows = pl.BlockSpec((tm, k), lambda i: (i, 0))
    return pl.pallas_call(
        body, grid=(m // tm,),
        in_specs=[rows, pl.BlockSpec((1, k), lambda i: (0, 0)), rows],
        out_specs=[pl.BlockSpec((1, 128), lambda i: (0, 0)), rows, rows, pl.BlockSpec((1, k), lambda i: (0, 0))],
        out_shape=[_SDS((1, 128), F32), _SDS((m, k), F32), _SDS((m, k), _ACT), _SDS((1, k), F32)],
        name=name, compiler_params=_cp("arbitrary"))(h, g, tgt)


def _glu_down(gu, conv_w, conv_b, w_down, res, name):
    s = gu.shape[2]
    tm = min(s, _TM_GLU)

    def body(gu_ref, prev_ref, w_ref, b_ref, wdn_ref, r_ref, o_ref, act_ref, gc_ref):
        i, j = pl.program_id(0), pl.program_id(1)
        prev = jnp.where(i > 0, prev_ref[...].astype(F32), 0.0)
        ext = jnp.concatenate([prev, gu_ref[0].astype(F32)], axis=0)
        gc = b_ref[...] + w_ref[FFN_CONV - 1:FFN_CONV, :] * ext
        for k in range(FFN_CONV - 1):
            gc = gc + w_ref[k:k + 1, :] * pltpu.roll(ext, FFN_CONV - 1 - k, 0)
        gc = gc[GLU_HALO:]
        gc_ref[...] = gc.astype(gc_ref.dtype)
        act =(_silu(gc) * gu_ref[1].astype(F32)).astype(act_ref.dtype)
        act_ref[...] = act
        part = _mm(act, _w2d(wdn_ref))

        @pl.when(j == 0)
        def _():
            o_ref[...] = r_ref[...] + part

        @pl.when(j > 0)
        def _():
            o_ref[...] += part

    return pl.pallas_call(
        body, grid=(s // tm, FF_BLOCKS),
        in_specs=[pl.BlockSpec((2, None, tm, GU_SHARD), lambda i, j: (0, j, i, 0)),
                  pl.BlockSpec((None, None, GLU_HALO, GU_SHARD),
                               lambda i, j: (0, j, jnp.maximum(i * (tm // GLU_HALO) - 1, 0), 0)),
                  pl.BlockSpec((None, HALO, GU_SHARD), lambda i, j: (j, 0, 0)),
                  pl.BlockSpec((None, 1, GU_SHARD), lambda i, j: (j, 0, 0)),
                  _spec_down(1), pl.BlockSpec((tm, D), lambda i, j: (i, 0))],
        out_specs=[pl.BlockSpec((tm, D), lambda i, j: (i, 0)), pl.BlockSpec((None, tm, GU_SHARD), lambda i, j: (j, i, 0)),
                   pl.BlockSpec((None, tm, GU_SHARD), lambda i, j: (j, i, 0))],
        out_shape=[_SDS((s, D), F32), _SDS((FF_BLOCKS, s, GU_SHARD), _ACT), _SDS((FF_BLOCKS, s, GU_SHARD), _ACT)], name=name,
        compiler_params=_cp("arbitrary", "arbitrary"))(gu, gu, conv_w, conv_b, w_down, res)


def _glu_bwd(gu, gc, conv_w, dh, w_down, name, deps=()):
    s = gu.shape[2]
    tm = min(s, _TM_GLU)
    nt = s // tm
    ext_rows = tm + GLU_HALO

    def body(gu_ref, prev_ref, gc_ref, w_ref, dh_ref, wdn_ref, *rest):
        dgu_ref, dw_ref, db_ref, carry_ref = rest[-4:]
        t = pl.program_id(1)
        i = nt - 1 - t

        @pl.when(t == 0)
        def _():
            carry_ref[...] = jnp.zeros_like(carry_ref)
            dw_ref[...] = jnp.zeros_like(dw_ref)
            db_ref[...] = jnp.zeros_like(db_ref)

        up = gu_ref[1].astype(F32)
        prev = jnp.where(i > 0, prev_ref[...].astype(F32), 0.0)
        ext = jnp.concatenate([prev, gu_ref[0].astype(F32)], axis=0)
        gc = gc_ref[...].astype(F32)
        sg = jax.nn.sigmoid(gc)
        da = _mm_nt(dh_ref[...], _w2d(wdn_ref))
        dup = da * (gc * sg)
        dgc = da * up * (sg * (1.0 + gc * (1.0 - sg)))
        db_ref[...] += jnp.sum(dgc, axis=0, keepdims=True)
        dgc_ext = jnp.concatenate([jnp.zeros((GLU_HALO, GU_SHARD), F32), dgc], axis=0)
        ahead = [pltpu.roll(dgc_ext, ext_rows - (FFN_CONV - 1 - j), 0) if j < FFN_CONV - 1 else dgc_ext
                 for j in range(FFN_CONV)]
        dext = ahead[0] * w_ref[0:1, :]
        for j in range(FFN_CONV):
            dw_ref[j:j + 1, :] += jnp.sum(ext * ahead[j], axis=0, keepdims=True)
            if j > 0:
                dext = dext + ahead[j] * w_ref[j:j + 1, :]
        tail = jnp.concatenate([jnp.zeros((tm - GLU_HALO, GU_SHARD), F32), carry_ref[...]], axis=0)
        dgate = dext[GLU_HALO:] + tail
        carry_ref[...] = dext[:GLU_HALO]
        dgu_ref[0] = dgate.astype(dgu_ref.dtype)
        dgu_ref[1] = dup.astype(dgu_ref.dtype)

    return pl.pallas_call(
        body, grid=(FF_BLOCKS, nt),
        in_specs=[pl.BlockSpec((2, None, tm, GU_SHARD), lambda j, t: (0, j, nt - 1 - t, 0)),
                  pl.BlockSpec((None, None, GLU_HALO, GU_SHARD),
                               lambda j, t: (0, j, jnp.maximum((nt - 1 - t) * (tm // GLU_HALO) - 1, 0), 0)),
                  pl.BlockSpec((None, tm, GU_SHARD), lambda j, t: (j, nt - 1 - t, 0)),
                  pl.BlockSpec((None, HALO, GU_SHARD), lambda j, t: (j, 0, 0)),
                  pl.BlockSpec((tm, D), lambda j, t: (nt - 1 - t, 0)), _spec_down(0)] + _dep_specs(deps),
        out_specs=[pl.BlockSpec((2, None, tm, GU_SHARD), lambda j, t: (0, j, nt - 1 - t, 0)),
                   pl.BlockSpec((None, HALO, GU_SHARD), lambda j, t: (j, 0, 0)),
                   pl.BlockSpec((None, 1, GU_SHARD), lambda j, t: (j, 0, 0))],
        out_shape=[_SDS(gu.shape, _ACT), _SDS((FF_BLOCKS, HALO, GU_SHARD), F32), _SDS((FF_BLOCKS, 1, GU_SHARD), F32)],
        scratch_shapes=[pltpu.VMEM((GLU_HALO, GU_SHARD), F32)],
        name=name, compiler_params=_cp("arbitrary", "arbitrary"))(gu, gu, gc, conv_w, dh, w_down, *deps)


def _bucket_table():
    qi = np.arange(BLK)[:, None]
    kj = np.arange(BLK)[None, :]
    n = np.where(kj > qi, BLK + qi - kj, qi - kj)
    max_exact = N_BUCKETS // 2
    nf = np.maximum(n, 1).astype(np.float32)
    large = max_exact + (np.log(nf / max_exact) / math.log(MAX_DISTANCE / max_exact)
                         * (N_BUCKETS - max_exact)).astype(np.int32)
    large = np.minimum(large, N_BUCKETS - 1)
    return np.where(n < max_exact, n, large).astype(np.int32)


def _lane_low():
    return lax.broadcasted_iota(jnp.int32, (1, 128), 1) < A_HD


def _swa_groups(q, kd, vd, sink, bias, upper, first):
    n = A_HEADS // A_KV_HEADS
    ng = A_KV_HEADS
    low = _lane_low()
    qm = [jnp.concatenate([jnp.where(low == (h % 2 == 0), q[g][:, (h // 2) * 128:(h // 2 + 1) * 128], 0.0) for h in range(n)], axis=0)
          for g in range(ng)]
    s2 = [_mm_nt(qm[g], kd[g]) * (A_HD ** -0.5) for g in range(ng)]
    s = [jnp.where(upper[None], s2[g][:, :BLK].reshape(n, BLK, BLK), s2[g][:, BLK:].reshape(n, BLK, BLK)) + bias[g] for g in range(ng)]
    s = [jnp.where((upper & first)[None], -jnp.inf, t) for t in s]
    m = [lax.stop_gradient(jnp.maximum(jnp.max(s[g], axis=-1, keepdims=True), sink[g])) for g in range(ng)]
    p = [jnp.exp(s[g] - m[g]) for g in range(ng)]
    split = [jnp.concatenate([jnp.where(upper[None], t, 0.0), jnp.where(upper[None], 0.0, t)], axis=-1).reshape(n * BLK, 2 * BLK)
             for t in p]
    ones = jnp.ones((BLK, 128), F32)
    den = [_mm(p[g].reshape(n * BLK, BLK), ones) + jnp.exp(sink[g] - m[g]).reshape(n * BLK, 1) for g in range(ng)]
    o = [_mm(split[g], vd[g]) / den[g] for g in range(ng)]
    return [jnp.concatenate([jnp.where(low, t[2 * k * BLK:(2 * k + 1) * BLK], t[(2 * k + 1) * BLK:(2 * k + 2) * BLK])
                             for k in range(n // 2)], axis=1) for t in o]


def _mix_a_core(q, kd, vd, sink, bias, xq, mk, mv, upper, first):
    return _swa_groups(q, kd, vd, sink, bias, upper, first), _cross_pairs(xq, mk, mv)


def _swa_sinks(sink_ref, g):
    n = A_HEADS // A_KV_HEADS
    return jnp.concatenate([sink_ref[:, h:h + 1] for h in range(g * n, (g + 1) * n)], axis=0).reshape(n, 1, 1)


def _both_halves(t, t_rolled, g):
    low = _lane_low()
    return jnp.where(low, t, t_rolled) if g == 0 else jnp.where(low, t_rolled, t)


def _cross_pairs(q, mk, mv):
    rows = q.shape[0]
    low = _lane_low()
    qm = [jnp.concatenate([jnp.where(low, q[:, p * 128:(p + 1) * 128], 0.0), jnp.where(low, 0.0, q[:, p * 128:(p + 1) * 128])], axis=0)
          for p in range(X_HEADS // 2)]
    s = [_mm_nt(qm[p], mk[:, p * 128:(p + 1) * 128]) * (X_HD ** -0.5) for p in range(X_HEADS // 2)]
    e = [jnp.exp(t - lax.stop_gradient(jnp.max(t, axis=-1, keepdims=True))) for t in s]
    pr = [t / jnp.sum(t, axis=-1, keepdims=True) for t in e]
    o = [_mm(pr[p], mv[:, p * 128:(p + 1) * 128]) for p in range(X_HEADS // 2)]
    return jnp.concatenate([jnp.where(low, t[:rows], t[rows:]) for t in o], axis=1)


def _swa_upper():
    qi = lax.broadcasted_iota(jnp.int32, (BLK, BLK), 0)
    kj = lax.broadcasted_iota(jnp.int32, (BLK, BLK), 1)
    return kj > qi


def _bias_build(rel_bias, bucket, name):
    def body(rb_ref, bucket_ref, o_ref):
        b = bucket_ref[...]
        for h in range(A_HEADS):
            acc = jnp.zeros((BLK, BLK), F32)
            for k in range(N_BUCKETS):
                acc = jnp.where(b == k, rb_ref[k, h], acc)
            o_ref[h] = acc

    return pl.pallas_call(
        body, in_specs=[pl.BlockSpec(memory_space=pltpu.SMEM), pl.BlockSpec(memory_space=pltpu.VMEM)],
        out_specs=pl.BlockSpec(memory_space=pltpu.VMEM),
        out_shape=_SDS((A_HEADS, BLK, BLK), F32), name=name)(rel_bias, bucket)


def _bias_reduce(dbias, bucket, name):
    def body(db_ref, bucket_ref, o_ref):
        b = bucket_ref[...]
        row = lax.broadcasted_iota(jnp.int32, (N_BUCKETS, 128), 0)
        lane = lax.broadcasted_iota(jnp.int32, (N_BUCKETS, 128), 1)
        acc = jnp.zeros((N_BUCKETS, 128), F32)
        for h in range(A_HEADS):
            v = db_ref[h]
            for k in range(N_BUCKETS):
                sk = jnp.sum(jnp.sum(jnp.where(b == k, v, 0.0), axis=1, keepdims=True), axis=0, keepdims=True)
                acc = acc + jnp.where((row == k) & (lane == h), sk, 0.0)
        o_ref[...] = acc

    return pl.pallas_call(
        body, in_specs=[pl.BlockSpec(memory_space=pltpu.VMEM)] * 2,
        out_specs=pl.BlockSpec(memory_space=pltpu.VMEM),
        out_shape=_SDS((N_BUCKETS, 128), F32), name=name)(dbias, bucket)


def _mix_a_fwd(proj, bias, sinks, memkv, name):
    s = proj.shape[0]
    nb = s // BLK
    grp = A_HEADS // A_KV_HEADS

    def body(proj_ref, prev_ref, bias_ref, sink_ref, memkv_ref, o_ref):
        i = pl.program_id(0)
        upper = _swa_upper()
        prev = prev_ref[...].astype(F32)
        proj = proj_ref[...].astype(F32)
        kb = jnp.concatenate([prev[:, :A_KV], proj[:, A_Q:A_Q + A_KV]], axis=0)
        vb = jnp.concatenate([prev[:, A_KV:], proj[:, A_Q + A_KV:A_Q + 2 * A_KV]], axis=0)
        kb_r = pltpu.roll(kb, A_HD, 1)
        vb_r = pltpu.roll(vb, A_HD, 1)
        gw = A_Q // A_KV_HEADS
        groups = range(A_KV_HEADS)
        swa, cross = _mix_a_core([proj[:, g * gw:(g + 1) * gw] for g in groups], [_both_halves(kb, kb_r, g) for g in groups],
                                 [_both_halves(vb, vb_r, g) for g in groups], [_swa_sinks(sink_ref, g) for g in groups],
                                 [bias_ref[g * grp:(g + 1) * grp] for g in groups], proj[:, A_Q + 2 * A_KV:],
                                 memkv_ref[:, :X_Q], memkv_ref[:, X_Q:], upper, i == 0)
        o_ref[...] = jnp.concatenate(swa + [cross], axis=1).astype(o_ref.dtype)

    return pl.pallas_call(
        body, grid=(nb,),
        in_specs=[pl.BlockSpec((BLK, IN_A), lambda i: (i, 0)),
                  pl.BlockSpec((BLK, 2 * A_KV), lambda i: (jnp.maximum(i - 1, 0), A_Q // (2 * A_KV))),
                  pl.BlockSpec((A_HEADS, BLK, BLK), lambda i: (0, 0, 0)),
                  pl.BlockSpec((1, 128), lambda i: (0, 0)),
                  pl.BlockSpec((MEM_LEN, 2 * X_Q), lambda i: (0, 0))],
        out_specs=pl.BlockSpec((BLK, D), lambda i: (i, 0)),
        out_shape=_SDS((s, D), _ACT), name=name, compiler_params=_cp("arbitrary"))(proj, proj, bias, sinks, memkv)


def _mix_a_bwd(proj, bias, sinks, memkv, dmix, name, deps=()):
    s = proj.shape[0]
    nb = s // BLK
    grp = A_HEADS // A_KV_HEADS

    def body(proj_ref, prev_ref, bias_ref, sink_ref, memkv_ref, dmix_ref, *rest):
        dproj_ref, dbias_ref, dsink_ref, dmemkv_ref, carry_ref = rest[-5:]
        t = pl.program_id(0)
        i = nb - 1 - t

        @pl.when(t == 0)
        def _():
            carry_ref[...] = jnp.zeros_like(carry_ref)
            dbias_ref[...] = jnp.zeros_like(dbias_ref)
            dsink_ref[...] = jnp.zeros_like(dsink_ref)
            dmemkv_ref[...] = jnp.zeros_like(dmemkv_ref)

        upper = _swa_upper()
        lane = lax.broadcasted_iota(jnp.int32, (1, 128), 1)
        low = _lane_low()
        prev = prev_ref[...].astype(F32)
        proj = proj_ref[...].astype(F32)
        kb = jnp.concatenate([prev[:, :A_KV], proj[:, A_Q:A_Q + A_KV]], axis=0)
        vb = jnp.concatenate([prev[:, A_KV:], proj[:, A_Q + A_KV:A_Q + 2 * A_KV]], axis=0)
        kb_r = pltpu.roll(kb, A_HD, 1)
        vb_r = pltpu.roll(vb, A_HD, 1)
        gw = A_Q // A_KV_HEADS
        groups = range(A_KV_HEADS)
        _, vjp = jax.vjp(
            functools.partial(_mix_a_core, upper=upper, first=i == 0),
            [proj[:, g * gw:(g + 1) * gw] for g in groups], [_both_halves(kb, kb_r, g) for g in groups],
            [_both_halves(vb, vb_r, g) for g in groups], [_swa_sinks(sink_ref, g) for g in groups],
            [bias_ref[g * grp:(g + 1) * grp] for g in groups], proj[:, A_Q + 2 * A_KV:], memkv_ref[:, :X_Q], memkv_ref[:, X_Q:])
        dqs, dk, dv, ds, db, dxq, dmk, dmv = vjp(
            ([dmix_ref[:, g * gw:(g + 1) * gw].astype(F32) for g in groups], dmix_ref[:, A_Q:].astype(F32)))
        dkd = [t + pltpu.roll(t, A_HD, 1) for t in dk]
        dvd = [t + pltpu.roll(t, A_HD, 1) for t in dv]
        dsink = jnp.zeros((1, 128), F32)
        for g in groups:
            for h in range(grp):
                dsink = dsink + jnp.where(lane == g * grp + h, ds[g][h], 0.0)
            dbias_ref[g * grp:(g + 1) * grp] += db[g]
        dsink_ref[...] += dsink
        dkb = jnp.where(low, dkd[0], dkd[1])
        dvb = jnp.where(low, dvd[0], dvd[1])
        dmemkv_ref[...] += jnp.concatenate([dmk, dmv], axis=1)
        dkv_cur = jnp.concatenate([dkb[BLK:], dvb[BLK:]], axis=1) + carry_ref[...]
        carry_ref[...] = jnp.concatenate([dkb[:BLK], dvb[:BLK]], axis=1)
        dproj_ref[...] = jnp.concatenate(list(dqs) + [dkv_cur, dxq], axis=1).astype(dproj_ref.dtype)

    return pl.pallas_call(
        body, grid=(nb,),
        in_specs=[pl.BlockSpec((BLK, IN_A), lambda t: (nb - 1 - t, 0)),
                  pl.BlockSpec((BLK, 2 * A_KV), lambda t: (jnp.maximum(nb - 2 - t, 0), A_Q // (2 * A_KV))),
                  pl.BlockSpec((A_HEADS, BLK, BLK), lambda t: (0, 0, 0)),
                  pl.BlockSpec((1, 128), lambda t: (0, 0)),
                  pl.BlockSpec((MEM_LEN, 2 * X_Q), lambda t: (0, 0)),
                  pl.BlockSpec((BLK, D), lambda t: (nb - 1 - t, 0))] + _dep_specs(deps),
        out_specs=[pl.BlockSpec((BLK, IN_A), lambda t: (nb - 1 - t, 0)),
                   pl.BlockSpec((A_HEADS, BLK, BLK), lambda t: (0, 0, 0)),
                   pl.BlockSpec((1, 128), lambda t: (0, 0)),
                   pl.BlockSpec((MEM_LEN, 2 * X_Q), lambda t: (0, 0))],
        out_shape=[_SDS((s, IN_A), _ACT), _SDS((A_HEADS, BLK, BLK), F32), _SDS((1, 128), F32),
                   _SDS((MEM_LEN, 2 * X_Q), F32)],
        scratch_shapes=[pltpu.VMEM((BLK, 2 * A_KV), F32)],
        name=name, compiler_params=_cp("arbitrary"))(proj, proj, bias, sinks, memkv, dmix, *deps)


def _neumann(pw, rhs):
    nh = len(pw)
    x = rhs
    for lvl in range(6):
        if lvl < 5:
            prod = [_mmf(pw[h], jnp.concatenate([x[h], pw[h]], axis=1)) for h in range(nh)]
            x = [x[h] + prod[h][:, :B_HD] for h in range(nh)]
            pw = [t[:, B_HD:] for t in prod]
        else:
            x = [x[h] + _mmf(pw[h], x[h]) for h in range(nh)]
    return x


@jax.custom_vjp
def _tri_solve(pw, rhs):
    return _neumann(pw, rhs)


def _tri_solve_fwd(pw, rhs):
    x = _neumann(pw, rhs)
    return x, (pw, x)


def _tri_solve_bwd(res, dx):
    pw, x = res
    d_rhs = _neumann([t.T for t in pw], list(dx))
    return [_mmf_nt(d_rhs[h], x[h]) for h in range(len(pw))], d_rhs


_tri_solve.defvjp(_tri_solve_fwd, _tri_solve_bwd)


@jax.custom_vjp
def _tri_solved(pw, rhs, x):
    return x


def _tri_solved_fwd(pw, rhs, x):
    return x, (pw, x)


def _tri_solved_bwd(res, dx):
    d_pw, d_rhs = _tri_solve_bwd(res, dx)
    return d_pw, d_rhs, [jnp.zeros_like(t) for t in res[1]]


_tri_solved.defvjp(_tri_solved_fwd, _tri_solved_bwd)


@jax.custom_vjp
def _known(x, value):
    return value


def _known_fwd(x, value):
    return value, None


def _known_bwd(_, g):
    return g, jnp.zeros_like(g)


_known.defvjp(_known_fwd, _known_bwd)


def _dn_heads(yq, yk, yv, z, bl, al, a_log, dtb, ng, s0, solved=None, out_known=None):
    c = CHUNK
    nh = B_V_HEADS
    rep = B_V_HEADS // B_QK_HEADS
    r = lax.broadcasted_iota(jnp.int32, (c, c), 0)
    cc = lax.broadcasted_iota(jnp.int32, (c, c), 1)
    q = [_silu(t) for t in yq]
    k = [_silu(t) for t in yk]
    v = [_silu(t) for t in yv]
    q = [t * lax.rsqrt(jnp.sum(t * t, axis=-1, keepdims=True) + EPS) * (B_HD ** -0.5) for t in q]
    k = [t * lax.rsqrt(jnp.sum(t * t, axis=-1, keepdims=True) + EPS) for t in k]
    beta = [jax.nn.sigmoid(t) for t in bl]
    g = [-jnp.exp(a_log[h]) * jax.nn.softplus(al[h] + dtb[h]) for h in range(nh)]
    gb = [jnp.broadcast_to(t, (c, c)) for t in g]
    gc_col = [jnp.sum(jnp.where(cc <= r, t.T, 0.0), axis=1, keepdims=True) for t in gb]
    gc_row = [jnp.sum(jnp.where(r <= cc, t, 0.0), axis=0, keepdims=True) for t in gb]
    gc_last = [jnp.sum(t, axis=0, keepdims=True) for t in g]
    decay = [jnp.exp(jnp.where(r >= cc, gc_col[h] - gc_row[h], -jnp.inf)) for h in range(nh)]
    kq = [_mmf_nt(jnp.concatenate([k[h], q[h]], axis=0), k[h]) for h in range(B_QK_HEADS)]
    kk = [t[:c] for t in kq]
    qk = [t[c:] for t in kq]
    egc = [jnp.exp(t) for t in gc_col]
    both = [_mmf(jnp.concatenate([(beta[h] * egc[h]) * k[h // rep], q[h // rep] * egc[h]], axis=0), s0[h]) for h in range(nh)]
    rhs = [beta[h] * v[h] - both[h][:c] for h in range(nh)]
    qs0 = [t[c:] for t in both]
    pw = [-(beta[h] * kk[h // rep] * jnp.where(r > cc, decay[h], 0.0)) for h in range(nh)]
    delta = _tri_solve(pw, rhs) if solved is None else _tri_solved(pw, rhs, solved)
    last = [_mmf(jnp.concatenate([qk[h // rep] * decay[h], (k[h // rep] * jnp.exp(gc_last[h] - gc_col[h])).T], axis=0), delta[h])
            for h in range(nh)]
    out = [qs0[h] + last[h][:c] for h in range(nh)]
    if out_known is not None:
        out = [_known(out[h], out_known[h]) for h in range(nh)]
    s1 = [jnp.exp(gc_last[h]) * s0[h] + last[h][c:] for h in range(nh)]
    o = [t * lax.rsqrt(jnp.mean(t * t, axis=-1, keepdims=True) + EPS) * ng for t in out]
    return [o[h] * _silu(z[h]) for h in range(nh)], s1, delta, out


def _dn_conv(ext, w_ref):
    y = ext * w_ref[B_CONV - 1:B_CONV, :]
    for j in range(B_CONV - 1):
        y = y + w_ref[j:j + 1, :] * pltpu.roll(ext, B_CONV - 1 - j, 0)
    return y


def _dn_args(y, cur_ref, par_ref, ng_ref):
    nh = B_V_HEADS
    return ([y[:, h * B_HD:(h + 1) * B_HD] for h in range(B_QK_HEADS)],
            [y[:, B_QK + h * B_HD:B_QK + (h + 1) * B_HD] for h in range(B_QK_HEADS)],
            [y[:, 2 * B_QK + h * B_HD:2 * B_QK + (h + 1) * B_HD] for h in range(nh)],
            [cur_ref[:, BP_Z + h * B_HD:BP_Z + (h + 1) * B_HD] for h in range(nh)],
            [cur_ref[:, BP_GATE + h:BP_GATE + h + 1] for h in range(nh)],
            [cur_ref[:, BP_GATE + nh + h:BP_GATE + nh + h + 1] for h in range(nh)],
            [par_ref[:, h:h + 1] for h in range(nh)], [par_ref[:, nh + h:nh + h + 1] for h in range(nh)], ng_ref[...])


def _mix_b_fwd(proj, conv_w, par, ng, memkv, name):
    s = proj.shape[0]
    nc = s // CHUNK

    def body(cur_ref, prev_ref, w_ref, par_ref, ng_ref, memkv_ref, o_ref, st_ref, dl_ref, state_ref):
        n = pl.program_id(0)

        @pl.when(n == 0)
        def _():
            state_ref[...] = jnp.zeros_like(state_ref)

        prev = jnp.where(n > 0, prev_ref[...], 0.0)
        ext = jnp.concatenate([prev, cur_ref[:, :B_QKV]], axis=0)
        y = _dn_conv(ext, w_ref)[HALO:]
        s0 = [state_ref[hv] for hv in range(B_V_HEADS)]
        st_ref[0] = state_ref[...]
        outs, s1, delta, raw = _dn_heads(*_dn_args(y, cur_ref, par_ref, ng_ref), s0)
        for hv in range(B_V_HEADS):
            state_ref[hv] = s1[hv]
            dl_ref[0, hv] = delta[hv]
            dl_ref[0, B_V_HEADS + hv] = raw[hv]
        outs = outs + [_cross_pairs(cur_ref[:, BP_XQ:BP_XQ + X_Q], memkv_ref[:, :X_Q], memkv_ref[:, X_Q:])]
        o_ref[...] = jnp.concatenate(outs, axis=1).astype(o_ref.dtype)

    return pl.pallas_call(
        body, grid=(nc,),
        in_specs=[pl.BlockSpec((CHUNK, IN_BP), lambda n: (n, 0)),
                  pl.BlockSpec((HALO, B_QKV), lambda n: (jnp.maximum(n * (CHUNK // HALO) - 1, 0), 0)),
                  pl.BlockSpec((HALO, B_QKV), lambda n: (0, 0)),
                  pl.BlockSpec((1, 128), lambda n: (0, 0)), pl.BlockSpec((1, 128), lambda n: (0, 0)),
                  pl.BlockSpec((MEM_LEN, 2 * X_Q), lambda n: (0, 0))],
        out_specs=[pl.BlockSpec((CHUNK, D), lambda n: (n, 0)),
                   pl.BlockSpec((1, B_V_HEADS, B_HD, B_HD), lambda n: (n, 0, 0, 0)),
                   pl.BlockSpec((1, 2 * B_V_HEADS, CHUNK, B_HD), lambda n: (n, 0, 0, 0))],
        out_shape=[_SDS((s, D), _ACT), _SDS((nc, B_V_HEADS, B_HD, B_HD), F32), _SDS((nc, 2 * B_V_HEADS, CHUNK, B_HD), F32)],
        scratch_shapes=[pltpu.VMEM((B_V_HEADS, B_HD, B_HD), F32)],
        name=name, compiler_params=_cp("arbitrary"))(proj, proj, conv_w, par, ng, memkv)


def _mix_b_bwd(proj, conv_w, par, ng, memkv, states, deltas, dmix, name):
    s = proj.shape[0]
    nc = s // CHUNK
    ext_rows = CHUNK + HALO

    def body(cur_ref, prev_ref, w_ref, par_ref, ng_ref, memkv_ref, st_ref, dl_ref, dmix_ref,
             dproj_ref, dw_ref, dpar_ref, dng_ref, dmemkv_ref, dstate_ref, carry_ref):
        t = pl.program_id(0)
        n = nc - 1 - t

        @pl.when(t == 0)
        def _():
            dstate_ref[...] = jnp.zeros_like(dstate_ref)
            carry_ref[...] = jnp.zeros_like(carry_ref)
            dw_ref[...] = jnp.zeros_like(dw_ref)
            dpar_ref[...] = jnp.zeros_like(dpar_ref)
            dng_ref[...] = jnp.zeros_like(dng_ref)
            dmemkv_ref[...] = jnp.zeros_like(dmemkv_ref)

        lane = lax.broadcasted_iota(jnp.int32, (1, 128), 1)
        prev = jnp.where(n > 0, prev_ref[...], 0.0)
        ext = jnp.concatenate([prev, cur_ref[:, :B_QKV]], axis=0)
        y = _dn_conv(ext, w_ref)[HALO:]
        solved = [dl_ref[0, hv] for hv in range(B_V_HEADS)]
        raw = [dl_ref[0, B_V_HEADS + hv] for hv in range(B_V_HEADS)]
        _, vjp = jax.vjp(functools.partial(_dn_heads, solved=solved, out_known=raw), *_dn_args(y, cur_ref, par_ref, ng_ref),
                         [st_ref[0, hv] for hv in range(B_V_HEADS)])
        none = [jnp.zeros((CHUNK, B_HD), F32)] * B_V_HEADS
        dyq, dyk, dyv, dz, gbl, gal, ga_log, gdtb, dng, gs0 = vjp(
            ([dmix_ref[:, hv * B_HD:(hv + 1) * B_HD].astype(F32) for hv in range(B_V_HEADS)],
             [dstate_ref[hv] for hv in range(B_V_HEADS)], none, none))
        dgate = jnp.zeros((CHUNK, 128), F32)
        dpar = jnp.zeros((1, 128), F32)
        for hv in range(B_V_HEADS):
            dstate_ref[hv] = gs0[hv]
            dgate = dgate + jnp.where(lane == hv, gbl[hv], 0.0) + jnp.where(lane == B_V_HEADS + hv, gal[hv], 0.0)
            dpar = dpar + jnp.where(lane == hv, ga_log[hv], 0.0) + jnp.where(lane == B_V_HEADS + hv, gdtb[hv], 0.0)
        dpar_ref[...] += dpar
        dng_ref[...] += dng
        _, vjp = jax.vjp(_cross_pairs, cur_ref[:, BP_XQ:BP_XQ + X_Q], memkv_ref[:, :X_Q], memkv_ref[:, X_Q:])
        dxq, dmk, dmv = vjp(dmix_ref[:, B_V:].astype(F32))
        dmemkv_ref[...] += jnp.concatenate([dmk, dmv], axis=1)
        dy = jnp.concatenate(list(dyq) + list(dyk) + list(dyv), axis=1)
        dy_ext = jnp.concatenate([jnp.zeros((HALO, B_QKV), F32), dy], axis=0)
        dext = dy_ext * w_ref[B_CONV - 1:B_CONV, :]
        dw_ref[B_CONV - 1:B_CONV, :] += jnp.sum(ext * dy_ext, axis=0, keepdims=True)
        for j in range(B_CONV - 1):
            sh = B_CONV - 1 - j
            dw_ref[j:j + 1, :] += jnp.sum(pltpu.roll(ext, sh, 0) * dy_ext, axis=0, keepdims=True)
            dext = dext + w_ref[j:j + 1, :] * pltpu.roll(dy_ext, ext_rows - sh, 0)
        tail = jnp.concatenate([jnp.zeros((CHUNK - HALO, B_QKV), F32), carry_ref[...]], axis=0)
        dqkv = dext[HALO:] + tail
        carry_ref[...] = dext[:HALO]
        dproj_ref[...] = jnp.concatenate([dqkv] + list(dz) + [dxq, dgate], axis=1).astype(dproj_ref.dtype)

    return pl.pallas_call(
        body, grid=(nc,),
        in_specs=[pl.BlockSpec((CHUNK, IN_BP), lambda t: (nc - 1 - t, 0)),
                  pl.BlockSpec((HALO, B_QKV), lambda t: (jnp.maximum((nc - 1 - t) * (CHUNK // HALO) - 1, 0), 0)),
                  pl.BlockSpec((HALO, B_QKV), lambda t: (0, 0)),
                  pl.BlockSpec((1, 128), lambda t: (0, 0)), pl.BlockSpec((1, 128), lambda t: (0, 0)),
                  pl.BlockSpec((MEM_LEN, 2 * X_Q), lambda t: (0, 0)),
                  pl.BlockSpec((1, B_V_HEADS, B_HD, B_HD), lambda t: (nc - 1 - t, 0, 0, 0)),
                  pl.BlockSpec((1, 2 * B_V_HEADS, CHUNK, B_HD), lambda t: (nc - 1 - t, 0, 0, 0)),
                  pl.BlockSpec((CHUNK, D), lambda t: (nc - 1 - t, 0))],
        out_specs=[pl.BlockSpec((CHUNK, IN_BP), lambda t: (nc - 1 - t, 0)),
                   pl.BlockSpec((HALO, B_QKV), lambda t: (0, 0)),
                   pl.BlockSpec((1, 128), lambda t: (0, 0)), pl.BlockSpec((1, 128), lambda t: (0, 0)),
                   pl.BlockSpec((MEM_LEN, 2 * X_Q), lambda t: (0, 0))],
        out_shape=[_SDS((s, IN_BP), _ACT), _SDS((HALO, B_QKV), F32), _SDS((1, 128), F32), _SDS((1, 128), F32),
                   _SDS((MEM_LEN, 2 * X_Q), F32)],
        scratch_shapes=[pltpu.VMEM((B_V_HEADS, B_HD, B_HD), F32), pltpu.VMEM((HALO, B_QKV), F32)],
        name=name, compiler_params=_cp("arbitrary"))(proj, proj, conv_w, par, ng, memkv, states, deltas, dmix)


def _place():
    return lax.axis_index("x"), lax.axis_index("y"), lax.axis_index("c")


def _all_gather(shards, name):
    n = len(shards)

    def body(*refs):
        ins, outs = refs[:n], refs[n:2 * n]
        send_sems, recv_sems, local_sems = refs[2 * n:]
        x, y, c = _place()
        me, sibling = (x, y, c), (x, y, 1 - c)
        chips = [(1 - x, y), (x, 1 - y), (1 - x, 1 - y)]

        def rows(a, px, py, pc):
            return outs[a].at[4 * px + 2 * py + pc]

        def copy(a, k, block, to, src=None):
            return pltpu.make_async_remote_copy(
                src_ref=rows(a, *block) if src is None else src, dst_ref=rows(a, *block),
                send_sem=send_sems.at[a, k], recv_sem=recv_sems.at[a, k],
                device_id=to, device_id_type=pl.DeviceIdType.MESH)

        mine = [pltpu.make_async_copy(ins[a], rows(a, *me), local_sems.at[a]) for a in range(n)]
        for cp in mine:
            cp.start()
        first = []
        for a in range(n):
            first.append(copy(a, 0, me, sibling, src=ins[a]))
            first += [copy(a, 1 + j, me, (*chip, c), src=ins[a]) for j, chip in enumerate(chips)]
        for cp in first:
            cp.start()
        passed = []
        for j, chip in enumerate(chips):
            for a in range(n):
                copy(a, 1 + j, (*chip, c), me).wait_recv()
                fwd = copy(a, 4 + j, (*chip, c), sibling)
                fwd.start()
                passed.append(fwd)
        for a in range(n):
            copy(a, 0, sibling, me).wait_recv()
            for j, chip in enumerate(chips):
                copy(a, 4 + j, (*chip, 1 - c), me).wait_recv()
        for cp in first + passed:
            cp.wait_send()
        for cp in mine:
            cp.wait()

    hbm = pl.BlockSpec(memory_space=pl.ANY)
    return pl.pallas_call(
        body, out_shape=[_SDS((N_DEV,) + s.shape, s.dtype) for s in shards],
        in_specs=[hbm] * n, out_specs=[hbm] * n,
        scratch_shapes=[pltpu.SemaphoreType.DMA((n, 7)), pltpu.SemaphoreType.DMA((n, 7)), pltpu.SemaphoreType.DMA((n,))],
        name=name)(*shards)


class _Exchange:
    def __init__(self, lands, srcs):
        self.lands, self.srcs = lands, srcs


def _seq_exchange(srcs, land_shapes, plan, name, cid):
    n, nl = len(srcs), len(land_shapes)

    def launch(*refs):
        src_refs, land_refs = refs[:n], refs[n:n + nl]
        send_sems, recv_sems, local_sems = refs[n + nl:]
        x, y, c = _place()
        my = 4 * x + 2 * y + c
        peers = [(x ^ ((k + 1) >> 2 & 1), y ^ ((k + 1) >> 1 & 1), c ^ ((k + 1) & 1)) for k in range(N_DEV - 1)]
        barrier = pltpu.get_barrier_semaphore()
        for p in peers:
            pl.semaphore_signal(barrier, inc=1, device_id=p, device_id_type=pl.DeviceIdType.MESH)
        pl.semaphore_wait(barrier, N_DEV - 1)

        def src_for(a, dest):
            return src_refs[a].at[dest] if plan[a][1] else src_refs[a]

        def slot(a, source):
            return land_refs[plan[a][0]].at[source]

        mine = [pltpu.make_async_copy(src_for(a, my), slot(a, my), local_sems.at[a]) for a in range(n)]
        for cp in mine:
            cp.start()
        sends, recvs = [], []
        for k, (px, py, pc) in enumerate(peers):
            peer = 4 * px + 2 * py + pc
            for a in range(n):
                kw = dict(send_sem=send_sems.at[a * (N_DEV - 1) + k], recv_sem=recv_sems.at[a * (N_DEV - 1) + k],
                          device_id=(px, py, pc), device_id_type=pl.DeviceIdType.MESH)
                sends.append(pltpu.make_async_remote_copy(src_ref=src_for(a, peer), dst_ref=slot(a, my), **kw))
                recvs.append(pltpu.make_async_remote_copy(src_ref=src_for(a, my), dst_ref=slot(a, peer), **kw))
        for cp in sends:
            cp.start()
        for cp in recvs:
            cp.wait_recv()
        for cp in sends:
            cp.wait_send()
        for cp in mine:
            cp.wait()

    lands = pl.kernel(
        launch, out_type=[_SDS(s, d) for s, d in land_shapes],
        mesh=plsc.ScalarSubcoreMesh(axis_name="sequencer", num_cores=1), name=name,
        scratch_types=(pltpu.SemaphoreType.DMA((n * (N_DEV - 1),)), pltpu.SemaphoreType.DMA((n * (N_DEV - 1),)),
                       pltpu.SemaphoreType.DMA((n,))),
        compiler_params=pltpu.CompilerParams(collective_id=cid))(*srcs)
    return _Exchange(list(lands), list(srcs))


def _adam_update(g, w, m, v):
    c1 = 1.0 - ADAM_B1 ** ADAM_STEP
    c2 = 1.0 - ADAM_B2 ** ADAM_STEP
    mm = ADAM_B1 * m + (1.0 - ADAM_B1) * g
    vv = ADAM_B2 * v + (1.0 - ADAM_B2) * (g * g)
    delta = -ADAM_LR * ((mm / c1) / (jnp.sqrt(vv / c2) + ADAM_EPS) + ADAM_WD * w)
    return delta, mm, vv


def _sum_sources(p_ref):
    g = p_ref[0].astype(F32)
    for s in range(1, N_DEV):
        g = g + p_ref[s].astype(F32)
    return g


def _adamw(parts, w, m, v, tr, name, restore_b=False, deps=()):
    nl, r, c = w.shape
    cp = parts[0].shape[-1]

    def body(*refs):
        p_refs = refs[:nl]
        w_ref, m_ref, v_ref = refs[nl:nl + 3]
        g_ref, d_ref, nm_ref, nv_ref = refs[-4:]
        g = _sum_sources(p_refs[0])
        for l in range(1, nl):
            g = jnp.where(pl.program_id(0) == l, _sum_sources(p_refs[l]), g)
        if restore_b:
            g = jnp.concatenate([g[:, :BP_XQ], g[:, BP_GATE:BP_GATE + 2 * B_V_HEADS], g[:, BP_XQ:BP_GATE]], axis=1)
        delta, mm, vv = _adam_update(g, w_ref[...], m_ref[...], v_ref[...])
        g_ref[...] = g
        d_ref[...] = delta
        nm_ref[...] = mm
        nv_ref[...] = vv

    spec = pl.BlockSpec((None, tr, c), lambda l, i: (l, i, 0))
    part_specs = [pl.BlockSpec((N_DEV, tr, cp), functools.partial(lambda l, i, k: (0, jnp.where(l == k, i, 0), 0), k=k))
                  for k in range(nl)]
    return pl.pallas_call(
        body, grid=(nl, r // tr),
        in_specs=part_specs + [spec, spec, spec] + _dep_specs(deps),
        out_specs=[spec] * 4, out_shape=[_SDS(w.shape, F32)] * 4,
        name=name, compiler_params=_cp("arbitrary", "arbitrary"))(*parts, w, m, v, *deps)


def _pack_small(d_rel, d_cb, d_cw, d_qkv, d_mix, d_mem, d_ffn, d_final, d_sinks, d_par, d_ng, loss_row, name):
    flat = [d_rel, *d_cb, *d_cw, d_qkv, *d_mix, *d_mem, *d_ffn, d_final, d_sinks, d_par, d_ng, loss_row]
    n = len(flat)

    def body(*refs):
        ins, o_ref = refs[:n], refs[n]
        rel, cb0, cb1, cw0, cw1, qkv, mx0, mx1, me0, me1, ff0, ff1, fin, snk, par, ng, lss = ins
        o_ref[...] = jnp.zeros_like(o_ref)
        for k in range(N_BUCKETS):
            lane = SP_REL_LANE + 128 * (k % 8)
            o_ref[SP_QKV + k // 8:SP_QKV + k // 8 + 1, lane:lane + 128] = rel[k:k + 1, :]
        for l, (cb, cw) in enumerate(((cb0, cw0), (cb1, cw1))):
            o_ref[SP_CB + l:SP_CB + l + 1, :] = jnp.concatenate([cb[j] for j in range(FF_BLOCKS)], axis=1)
            full = jnp.concatenate([cw[j] for j in range(FF_BLOCKS)], axis=1)
            o_ref[SP_CW + FFN_CONV * l:SP_CW + FFN_CONV * (l + 1), :] = full[:FFN_CONV]
        o_ref[SP_QKV:SP_QKV + B_CONV, 0:B_QKV] = qkv[0:B_CONV, :]
        for base, pair in ((SP_MIX, (mx0, mx1)), (SP_MEM, (me0, me1)), (SP_FFN, (ff0, ff1))):
            for l in range(2):
                o_ref[base + l:base + l + 1, 0:D] = pair[l][...]
        o_ref[SP_FINAL:SP_FINAL + 1, 0:D] = fin[...]
        o_ref[SP_MISC:SP_MISC + 1, 0:128] = snk[...]
        o_ref[SP_MISC:SP_MISC + 1, 128:256] = par[...]
        o_ref[SP_MISC:SP_MISC + 1, 256:384] = ng[...]
        o_ref[SP_MISC:SP_MISC + 1, 384:512] = lss[...]

    vm = pl.BlockSpec(memory_space=pltpu.VMEM)
    return pl.pallas_call(body, in_specs=[vm] * n, out_specs=vm, out_shape=_SDS((SMALL_ROWS, D_FF), F32), name=name)(*flat)


_SMALL = ["rel_bias", "norm_mix_g", "norm_mem_g", "sinks_a", "a_log_b", "dt_bias_b", "out_norm_g_b", "norm_ffn_g",
          "ffn_conv_b", "final_norm_g", "conv_qkv_b", "ffn_conv_w"]


def _adamw_small(recv, rc_qkv, rc_ffn, ws, ms, vs, name, deps=()):
    n = len(_SMALL)

    def body(*refs):
        recv_ref, qkv_ref, ffn_ref = refs[:3]
        w_refs, m_refs, v_refs = refs[3:3 + n], refs[3 + n:3 + 2 * n], refs[3 + 2 * n:3 + 3 * n]
        outs, loss_ref = refs[len(refs) - 4 * n - 1:len(refs) - 1], refs[-1]
        gs = _sum_sources(recv_ref)
        loss_ref[...] = gs[SP_MISC:SP_MISC + 1, 384:512]
        grads = {
            "rel_bias": jnp.concatenate(
                [gs[SP_QKV + k // 8:SP_QKV + k // 8 + 1, SP_REL_LANE + 128 * (k % 8):SP_REL_LANE + 128 * (k % 8) + A_HEADS]
                 for k in range(N_BUCKETS)], axis=0),
            "norm_mix_g": gs[SP_MIX:SP_MIX + 2, 0:D], "norm_mem_g": gs[SP_MEM:SP_MEM + 2, 0:D],
            "sinks_a": gs[SP_MISC:SP_MISC + 1, 0:A_HEADS],
            "a_log_b": gs[SP_MISC:SP_MISC + 1, 128:128 + B_V_HEADS],
            "dt_bias_b": gs[SP_MISC:SP_MISC + 1, 128 + B_V_HEADS:128 + 2 * B_V_HEADS],
            "out_norm_g_b": gs[SP_MISC:SP_MISC + 1, 256:256 + B_HD],
            "norm_ffn_g": gs[SP_FFN:SP_FFN + 2, 0:D], "ffn_conv_b": gs[SP_CB:SP_CB + 2, :],
            "final_norm_g": gs[SP_FINAL:SP_FINAL + 1, 0:D],
            "conv_qkv_b": _sum_sources(qkv_ref), "ffn_conv_w": _sum_sources(ffn_ref),
        }
        for i, nm in enumerate(_SMALL):
            g = grads[nm]
            delta, mm, vv = _adam_update(g, w_refs[i][...], m_refs[i][...], v_refs[i][...])
            outs[i][...] = g
            outs[n + i][...] = delta
            outs[2 * n + i][...] = mm
            outs[3 * n + i][...] = vv

    vm = pl.BlockSpec(memory_space=pltpu.VMEM)
    shapes = [_SDS(w.shape, F32) for w in ws]
    return pl.pallas_call(
        body, in_specs=[vm] * (3 + 3 * n) + _dep_specs(deps), out_specs=[vm] * (4 * n + 1),
        out_shape=shapes * 4 + [_SDS((1, 128), F32)],
        name=name)(recv, rc_qkv, rc_ffn, *ws, *ms, *vs, *deps)


def _assemble(gathered, axis):
    g = jnp.moveaxis(gathered, 0, axis)
    shp = list(g.shape)
    return g.reshape(shp[:axis] + [shp[axis] * shp[axis + 1]] + shp[axis + 2:])


def _pad_rows(a, rows):
    return jnp.pad(a, ((0, rows - a.shape[0]), (0, 0)))


def _pad_lanes(a, lanes=128):
    return jnp.pad(a, ((0, 0), (0, lanes - a.shape[1])))


def _ff_blocks(a):
    return jnp.moveaxis(a.reshape(a.shape[0], FF_BLOCKS, GU_SHARD), 1, 0)


def _reorder_b(w):
    qkv_z = w[..., :B_QKV + B_V]
    gates = w[..., B_QKV + B_V:B_QKV + B_V + 2 * B_V_HEADS]
    xq = w[..., IN_B - X_Q:]
    pad = jnp.zeros(w.shape[:-1] + (IN_BP - IN_B,), w.dtype)
    return jnp.concatenate([qkv_z, xq, gates, pad], axis=-1)


def kernel(x, mem, rel_bias, norm_mix_g, norm_mem_g, w_mem_kv, w_out, w_in_a, sinks_a, w_in_b, conv_qkv_b, a_log_b, dt_bias_b, out_norm_g_b, norm_ffn_g, w_gate_up, ffn_conv_w, ffn_conv_b, w_down, final_norm_g, loss_target, m_rel_bias, m_norm_mix_g, m_norm_mem_g, m_w_mem_kv, m_w_out, m_w_in_a, m_sinks_a, m_w_in_b, m_conv_qkv_b, m_a_log_b, m_dt_bias_b, m_out_norm_g_b, m_norm_ffn_g, m_w_gate_up, m_ffn_conv_w, m_ffn_conv_b, m_w_down, m_final_norm_g, v_rel_bias, v_norm_mix_g, v_norm_mem_g, v_w_mem_kv, v_w_out, v_w_in_a, v_sinks_a, v_w_in_b, v_conv_qkv_b, v_a_log_b, v_dt_bias_b, v_out_norm_g_b, v_norm_ffn_g, v_w_gate_up, v_ffn_conv_w, v_ffn_conv_b, v_w_down, v_final_norm_g):
    local = dict(locals())
    order = ["rel_bias", "norm_mix_g", "norm_mem_g", "w_mem_kv", "w_out", "w_in_a", "sinks_a", "w_in_b", "conv_qkv_b",
             "a_log_b", "dt_bias_b", "out_norm_g_b", "norm_ffn_g", "w_gate_up", "ffn_conv_w", "ffn_conv_b", "w_down",
             "final_norm_g"]
    wts = {n: local[n] for n in order}
    moms = {n: local["m_" + n] for n in order}
    vars_ = {n: local["v_" + n] for n in order}
    h0 = x[0]
    memx = mem[0]
    tgt = loss_target[0]
    s = h0.shape[0]
    tm = _rows(s)
    tb = min(s, _TM_BIG)

    t_ = lambda a: jnp.swapaxes(a, 1, 2)
    g_mk0, g_out0, g_ia, g_cq, g_cw = _all_gather(
        [w_mem_kv[0:1].astype(_MXU), w_out[0:1].astype(_MXU), t_(w_in_a).astype(_MXU), conv_qkv_b, ffn_conv_w], "gather_first")
    g_mk, g_out = [g_mk0], [g_out0]
    gu_land = ((N_DEV, GU_SHARD, D), _MXU)
    dn_land = ((N_DEV, DN_SHARD, D), _MXU)
    whole = [(0, False), (1, False)]
    def after(a, b):
        return a + (b[(0,) * b.ndim] * 0).astype(a.dtype)

    gu0_w = _seq_exchange([after(t_(w_gate_up)[0].astype(_MXU), g_ia)], [gu_land], [(0, False)], "gather_gate_up0", 1)
    dn0_w = _seq_exchange([after(w_down[0].astype(_MXU), g_ia)], [dn_land], [(0, False)], "gather_down0", 8)
    w_ia = g_ia.reshape(IN_A, D)
    conv_qkv = _pad_rows(_assemble(g_cq, 2)[0], HALO)
    ffn_cw_full = _assemble(g_cw, 2)
    ffn_cw = [_ff_blocks(_pad_rows(ffn_cw_full[i], HALO)) for i in range(2)]
    ffn_cb = [_ff_blocks(ffn_conv_b[i:i + 1]) for i in range(2)]
    bucket = jnp.asarray(_bucket_table())
    bias = _bias_build(rel_bias, bucket, "bias_build")
    sinks = _pad_lanes(sinks_a)
    par_b = _pad_lanes(jnp.concatenate([a_log_b, dt_bias_b], axis=1))

    row_x = pl.BlockSpec((tm, D), lambda i, j: (i, 0))
    gu_shape = (2, FF_BLOCKS, s, GU_SHARD)

    def in_proj(h, g, w, w_spec, n_cols, tn, name, deps=(), out_dtype=F32, w_t=False, tm=None):
        return _norm_matmul(h, g, w, w_spec, n_cols // tn, (h.shape[0], n_cols),
                            pl.BlockSpec((tm or _rows(h.shape[0]), tn), lambda i, j: (i, j)), name, deps=deps, out_dtype=out_dtype,
                            w_t=w_t, tm=tm)

    def ffn_fwd(i, h, g_gu, g_dn, deps=()):
        gu, hn = _norm_matmul(h, norm_ffn_g[i:i + 1], g_gu, _spec_gate_up(1), N_DEV, gu_shape,
                              _spec_gu_act(0, 1, tb), f"gate_up_{i}", deps=deps, out_dtype=_ACT, w_t=True, tm=tb)
        h_new, act, gc = _glu_down(gu, ffn_cw[i], ffn_cb[i], g_dn, h, f"glu_down_{i}")
        return h_new, gu, hn, (act, gc)

    def out_proj(i, mix, h):
        return _matmul_res(mix, row_x, g_out[i], _spec_rowsharded(0, D // N_DEV, D), 1, h, f"out_proj_{i}")

    proj_a, hn_a = in_proj(h0, norm_mix_g[0:1], w_ia, pl.BlockSpec((640, D), lambda i, j: (j, 0)), IN_A, 640, "in_proj_a",
                           deps=gu0_w.srcs + dn0_w.srcs, out_dtype=_ACT, w_t=True)
    memkv0, memn0 = in_proj(memx, norm_mem_g[0:1], g_mk[0], _spec_rowsharded(0, D // N_DEV, 2 * X_Q), 2 * X_Q, 2 * X_Q, "mem_proj_0")
    mix_a = _mix_a_fwd(proj_a, bias, sinks, memkv0, "mix_a_fwd")
    h1 = out_proj(0, mix_a, h0)
    g_gu0, g_dn0 = gu0_w.lands[0], dn0_w.lands[0]
    in_b_w = _seq_exchange([after(_reorder_b(w_in_b).astype(_MXU), h1), after(w_mem_kv[1:2].astype(_MXU), h1),
                            after(w_out[1:2].astype(_MXU), h1)],
                           [((N_DEV, 1, D // N_DEV, IN_BP), _MXU), ((N_DEV, 1, D // N_DEV, 2 * X_Q), _MXU),
                            ((N_DEV, 1, D // N_DEV, D), _MXU)], [(0, False), (1, False), (2, False)], "gather_in_b", 2)
    ffn1_w = _seq_exchange([after(t_(w_gate_up)[1].astype(_MXU), h1), after(w_down[1].astype(_MXU), h1)], [gu_land, dn_land], whole,
                           "gather_ffn1", 3)
    h2, gu0, hn_f0, act0 = ffn_fwd(0, h1, g_gu0, g_dn0, deps=in_b_w.srcs + ffn1_w.srcs)
    g_ib, g_mk1, g_out1 = in_b_w.lands
    g_mk.append(g_mk1)
    g_out.append(g_out1)
    proj_b, hn_b = in_proj(h2, norm_mix_g[1:2], g_ib, _spec_rowsharded(0, D // N_DEV, 896, col_block=1), IN_BP, 896, "in_proj_b")
    memkv1, memn1 = in_proj(memx, norm_mem_g[1:2], g_mk[1], _spec_rowsharded(0, D // N_DEV, 2 * X_Q), 2 * X_Q, 2 * X_Q, "mem_proj_1",
                            deps=[h2])
    mix_b, states, deltas = _mix_b_fwd(proj_b, conv_qkv, par_b, out_norm_g_b, memkv1, "mix_b_fwd")
    h3 = out_proj(1, mix_b, h2)
    g_gu1, g_dn1 = ffn1_w.lands
    h4, gu1, hn_f1, act1 = ffn_fwd(1, h3, g_gu1, g_dn1)
    loss_row, *dh, d_final_g = _loss_head(h4, final_norm_g[None, :], tgt, "loss_head")

    zeros_mem = jnp.zeros_like(memx)
    per_dest2 = [(0, True), (1, True)]

    def ffn_bwd(i, dh, h_in, gu, hn_f, act_gc, g_gu, g_dn, deps=()):
        act, gc = act_gc
        dgu, d_cw, d_cb = _glu_bwd(gu, gc, ffn_cw[i], dh[1], g_dn, f"glu_bwd_{i}", deps=deps)
        d_wdown = _matmul_tn(act, pl.BlockSpec((None, tm, GU_SHARD), lambda j, r: (j, r, 0)),
                             dh[1], pl.BlockSpec((tm, D), lambda j, r: (r, 0)), s, FF_BLOCKS, (GU_SHARD, D),
                             (N_DEV, DN_SHARD, D), pl.BlockSpec((2, DN_SHARD, D), lambda j, r: (j, 0, 0)), f"d_w_down_{i}")
        *dh_new, d_g = _matmul_nt_normbwd(dgu, _spec_gu_act(0, 1, tm), g_gu, _spec_gate_up(1), N_DEV, h_in,
                                          norm_ffn_g[i:i + 1], dh[0], f"d_ffn_in_{i}", w_t=True, act_copy=True)
        d_wgu = _matmul_tn(dgu, _spec_gu_act(1, 0, tb), hn_f, pl.BlockSpec((tb, D), lambda j, r: (r, 0)), s, N_DEV,
                           (GU_SHARD, D), (N_DEV, GU_SHARD, D), pl.BlockSpec((None, GU_SHARD, D), lambda j, r: (j, 0, 0)),
                           f"d_w_gate_up_{i}", tm=tb)
        return dh_new, [d_wdown, d_wgu], d_cw, d_cb, d_g

    def out_bwd(i, dh, mix, deps):
        dmix = _matmul_nt(dh[1], g_out[i], _spec_rowsharded(0, D // N_DEV, D), 1, (s, D), row_x, f"d_mix_{i}", deps=deps, out_dtype=_ACT)
        d_wout = _matmul_tn(mix, pl.BlockSpec((tm, D), lambda j, r: (r, 0)), dh[1], pl.BlockSpec((tm, D), lambda j, r: (r, 0)),
                            s, 1, (D, D), (N_DEV, D // N_DEV, D), pl.BlockSpec((N_DEV, D // N_DEV, D), lambda j, r: (0, 0, 0)),
                            f"d_w_out_{i}")
        return dmix, d_wout

    def mem_bwd(i, dmemkv, memn):
        tmm = _rows(MEM_LEN)
        *_, d_g = _matmul_nt_normbwd(dmemkv, pl.BlockSpec((tmm, 2 * X_Q), lambda r, j: (r, 0)), g_mk[i],
                                     _spec_rowsharded(0, D // N_DEV, 2 * X_Q), 1, memx, norm_mem_g[i:i + 1], zeros_mem,
                                     f"d_mem_in_{i}")
        by_row = lambda j, r: (r, 0)
        d_w = _matmul_tn(memn, pl.BlockSpec((tmm, D), by_row), dmemkv, pl.BlockSpec((tmm, 2 * X_Q), by_row), MEM_LEN, 1,
                         (D, 2 * X_Q), (N_DEV, D // N_DEV, 2 * X_Q),
                         pl.BlockSpec((N_DEV, D // N_DEV, 2 * X_Q), lambda j, r: (0, 0, 0)), f"d_w_mem_kv_{i}")
        return d_w, d_g

    out_land = ((N_DEV, D // N_DEV, D), _WIRE)
    mk_land = ((N_DEV, D // N_DEV, 2 * X_Q), _WIRE)
    ffn_lands = [((N_DEV, DN_SHARD, D), _WIRE), ((N_DEV, GU_SHARD, D), _WIRE)]
    dh, d_ffn1, d_cw1, d_cb1, d_gf1 = ffn_bwd(1, dh, h3, gu1, hn_f1, act1, g_gu1, g_dn1)
    ffn1_g = _seq_exchange(d_ffn1, ffn_lands, per_dest2, "send_ffn1_grads", 5)
    dmix, d_wout1 = out_bwd(1, dh, mix_b, ffn1_g.srcs)
    dproj_b, d_convw, d_par, d_ng, dmemkv1 = _mix_b_bwd(proj_b, conv_qkv, par_b, out_norm_g_b, memkv1, states, deltas, dmix, "mix_b_bwd")
    *dh, d_gm1 = _matmul_nt_normbwd(dproj_b, pl.BlockSpec((tm, 896), lambda i, j: (i, j)), g_ib,
                                    _spec_rowsharded(0, D // N_DEV, 896, col_block=1), IN_BP // 896, h2, norm_mix_g[1:2], dh[0],
                                    "d_in_b", act_copy=True)
    d_wib = _matmul_tn(hn_b, pl.BlockSpec((tb, D), lambda j, r: (r, 0)), dproj_b, pl.BlockSpec((tb, 896), lambda j, r: (r, j)),
                       s, IN_BP // 896, (D, 896), (N_DEV, D // N_DEV, IN_BP),
                       pl.BlockSpec((N_DEV, D // N_DEV, 896), lambda j, r: (0, 0, j)), "d_w_in_b", tm=tb)
    d_wmk1, d_gmem1 = mem_bwd(1, dmemkv1, memn1)
    mix1_g = _seq_exchange([d_wout1, d_wib, d_wmk1], [out_land, ((N_DEV, D // N_DEV, IN_BP), _WIRE), mk_land],
                           [(0, True), (1, True), (2, True)], "send_mix1_grads", 6)
    dh, d_ffn0, d_cw0, d_cb0, d_gf0 = ffn_bwd(0, dh, h1, gu0, hn_f0, act0, g_gu0, g_dn0, deps=mix1_g.srcs)
    dmix, d_wout0 = out_bwd(0, dh, mix_a, d_ffn0 + ffn1_g.lands[:1])
    ffn0_g = _seq_exchange(d_ffn0 + [d_wout0], ffn_lands + [out_land], per_dest2 + [(2, True)], "send_ffn0_grads", 4)
    dproj_a, dbias, dsinks, dmemkv0 = _mix_a_bwd(proj_a, bias, sinks, memkv0, dmix, "mix_a_bwd", deps=ffn0_g.srcs)
    dx, _, d_gm0 = _matmul_nt_normbwd(dproj_a, pl.BlockSpec((tm, 640), lambda i, j: (i, j)), w_ia,
                                      pl.BlockSpec((640, D), lambda i, j: (j, 0)), IN_A // 640, h0, norm_mix_g[0:1], dh[0],
                                      "d_in_a", w_t=True)
    d_wia = _matmul_tn(dproj_a, pl.BlockSpec((tm, IN_A), lambda j, r: (r, 0)), hn_a, pl.BlockSpec((tm, D), lambda j, r: (r, 0)),
                       s, 1, (IN_A, D), (N_DEV, IA_SHARD, D), pl.BlockSpec((N_DEV, IA_SHARD, D), lambda j, r: (0, 0, 0)),
                       "d_w_in_a")
    d_wmk0, d_gmem0 = mem_bwd(0, dmemkv0, memn0)
    d_rel = _bias_reduce(dbias, bucket, "bias_reduce")
    small = _pack_small(d_rel, (d_cb0, d_cb1), (d_cw0, d_cw1), d_convw, (d_gm0, d_gm1), (d_gmem0, d_gmem1),
                        (d_gf0, d_gf1), d_final_g, dsinks, d_par, d_ng, loss_row, "pack_small")
    mix0_g = _seq_exchange([d_wia, d_wmk0, small],
                           [((N_DEV, IA_SHARD, D), _WIRE), mk_land, ((N_DEV, SMALL_ROWS, D_FF), F32)],
                           [(0, True), (1, True), (2, False)], "send_mix0_grads", 7)

    res = {}
    last = []

    def update(nm, parts, tr, restore=False, transposed=False):
        view = t_ if transposed else (lambda a: a)
        out = _adamw(parts, view(wts[nm]), view(moms[nm]), view(vars_[nm]), tr, "adamw_" + nm, restore_b=restore, deps=last[-1:])
        res[nm] = [view(o) for o in out]
        last.append(out[1])

    r_dn1, r_gu1 = ffn1_g.lands
    r_dn0, r_gu0, r_out0 = ffn0_g.lands
    r_out1, r_ib, r_mk1 = mix1_g.lands
    update("w_in_b", [r_ib], 32, True)
    update("w_gate_up", [r_gu0, r_gu1], 176, transposed=True)
    update("w_down", [r_dn0, r_dn1], 176)
    r_ia, r_mk0, r_small = mix0_g.lands
    update("w_mem_kv", [r_mk0, r_mk1], 128)
    update("w_out", [r_out0, r_out1], 128)
    update("w_in_a", [r_ia], IA_SHARD, transposed=True)

    my = 4 * lax.axis_index("x") + 2 * lax.axis_index("y") + lax.axis_index("c")
    cq = conv_qkv_b.shape[-1]
    cf = ffn_conv_w.shape[-1]
    rc_qkv = lax.dynamic_slice_in_dim(r_small[:, SP_QKV:SP_QKV + B_CONV, :B_QKV], my * cq, cq, axis=2)[:, None]
    rc_ffn = lax.dynamic_slice_in_dim(r_small[:, SP_CW:SP_CW + 2 * FFN_CONV, :], my * cf, cf, axis=2).reshape(N_DEV, 2, FFN_CONV, cf)
    as2d = lambda a: a[None, :] if a.ndim == 1 else a
    small_out = _adamw_small(r_small, rc_qkv, rc_ffn, [as2d(wts[n]) for n in _SMALL], [as2d(moms[n]) for n in _SMALL],
                             [as2d(vars_[n]) for n in _SMALL], "adamw_small", deps=last[-1:])
    ns = len(_SMALL)
    for i, nm in enumerate(_SMALL):
        res[nm] = [small_out[k * ns + i].reshape(wts[nm].shape) for k in range(4)]

    return (small_out[-1][0, 0], dx[None], *[res[n][0] for n in order], *[res[n][1] for n in order],
            *[res[n][2] for n in order], *[res[n][3] for n in order])
```

```python
import functools
import math

import numpy as np

import jax
import jax.numpy as jnp
from jax import lax
from jax.experimental import pallas as pl
from jax.experimental.pallas import tpu as pltpu
from jax.experimental.pallas import tpu_sc as plsc

F32 = jnp.float32
_MXU = jnp.bfloat16
_ACT = jnp.bfloat16
_WIRE = jnp.bfloat16
_HI = lax.Precision.HIGH
_TM = 1024
_TM_GLU = 512
_TM_BIG = 2048
_VMEM_LIMIT = 48 * 1024 * 1024
_SDS = jax.ShapeDtypeStruct

D = 1024
EPS = 1e-6
A_HEADS, A_KV_HEADS, A_HD, BLK = 12, 2, 64, 128
N_BUCKETS, MAX_DISTANCE = 32, 128
B_QK_HEADS, B_V_HEADS, B_HD, B_CONV, CHUNK = 3, 6, 128, 4, 64
X_HEADS, X_HD, MEM_LEN = 4, 64, 256
D_FF, FFN_CONV = 2816, 3
A_Q, A_KV, X_Q = 768, 128, 256
B_QK, B_V, B_QKV = 384, 768, 1536
IN_A, IN_B = 1280, 2572
IN_BP = 2688
BP_Z, BP_XQ, BP_GATE = 1536, 2304, 2560
HALO = 8
GLU_HALO = 16

N_DEV = 8
GU_SHARD = 2 * D_FF // N_DEV
FF_BLOCKS = D_FF // GU_SHARD
DN_SHARD = D_FF // N_DEV
IA_SHARD = IN_A // N_DEV

ADAM_LR, ADAM_B1, ADAM_B2, ADAM_EPS, ADAM_WD, ADAM_STEP = 0.001, 0.9, 0.999, 1e-08, 0.01, 10

SP_CB, SP_CW, SP_QKV, SP_MIX, SP_MEM, SP_FFN, SP_FINAL, SP_MISC, SMALL_ROWS = 0, 2, 8, 12, 14, 16, 18, 19, 24
SP_REL_LANE = B_QKV


def _cp(*sems):
    return pltpu.CompilerParams(dimension_semantics=sems, vmem_limit_bytes=_VMEM_LIMIT)


def _mm(a, b):
    return jnp.dot(a.astype(_MXU), b.astype(_MXU), preferred_element_type=F32)


def _mm_nt(a, b):
    return lax.dot_general(a.astype(_MXU), b.astype(_MXU), (((1,), (1,)), ((), ())), preferred_element_type=F32)


def _mm_tn(a, b):
    return lax.dot_general(a.astype(_MXU), b.astype(_MXU), (((0,), (0,)), ((), ())), preferred_element_type=F32)


def _mmf(a, b):
    return jnp.dot(a, b, preferred_element_type=F32, precision=_HI)


def _mmf_nt(a, b):
    return lax.dot_general(a, b, (((1,), (1,)), ((), ())), preferred_element_type=F32, precision=_HI)


def _silu(x):
    return x * jax.nn.sigmoid(x)


def _w2d(ref):
    v = ref[...]
    return v.reshape(-1, v.shape[-1])


def _rows(m):
    return min(m, _TM)


def _spec_rowsharded(layer, rows, cols, col_block=None):
    if col_block is None:
        return pl.BlockSpec((N_DEV, None, rows, cols), lambda *_: (0, layer, 0, 0))
    return pl.BlockSpec((N_DEV, None, rows, cols), lambda *ids: (0, layer, 0, ids[col_block]))


def _spec_gate_up(axis):
    return pl.BlockSpec((None, GU_SHARD, D), lambda *ids: (ids[axis], 0, 0))


def _spec_down(axis):
    return pl.BlockSpec((2, DN_SHARD, D), lambda *ids: (ids[axis], 0, 0))


def _dep_specs(deps):
    return [pl.BlockSpec(memory_space=pl.ANY) for d in deps]


def _spec_gu_act(row_axis, axis, tm):
    return pl.BlockSpec((None, None, tm, GU_SHARD), lambda *ids: (ids[axis] // FF_BLOCKS, ids[axis] % FF_BLOCKS, ids[row_axis], 0))


def _norm_matmul(x, g, w, w_spec, n_blocks, out_shape, out_spec, name, deps=(), out_dtype=F32, w_t=False, tm=None):
    m, k = x.shape
    tm = tm or _rows(m)

    def body(x_ref, g_ref, w_ref, *rest):
        y_ref, hn_ref = rest[-2:]

        @pl.when(pl.program_id(1) == 0)
        def _():
            xv = x_ref[...]
            r = lax.rsqrt(jnp.mean(xv * xv, axis=-1, keepdims=True) + EPS)
            hn_ref[...] = (xv * r * g_ref[...]).astype(hn_ref.dtype)

        y_ref[...] = (_mm_nt if w_t else _mm)(hn_ref[...], _w2d(w_ref)).astype(y_ref.dtype)

    return pl.pallas_call(
        body, grid=(m // tm, n_blocks),
        in_specs=[pl.BlockSpec((tm, k), lambda i, j: (i, 0)), pl.BlockSpec((1, k), lambda i, j: (0, 0)), w_spec]
        + _dep_specs(deps),
        out_specs=[out_spec, pl.BlockSpec((tm, k), lambda i, j: (i, 0))],
        out_shape=[_SDS(out_shape, out_dtype), _SDS((m, k), _ACT)],
        name=name, compiler_params=_cp("arbitrary", "arbitrary"))(x, g, w, *deps)


def _matmul_res(a, a_spec, w, w_spec, n_k, res, name):
    m, n = res.shape
    tm = _rows(m)

    def body(a_ref, w_ref, r_ref, o_ref):
        part = _mm(a_ref[...], _w2d(w_ref))

        @pl.when(pl.program_id(1) == 0)
        def _():
            o_ref[...] = r_ref[...] + part

        @pl.when(pl.program_id(1) > 0)
        def _():
            o_ref[...] += part

    return pl.pallas_call(
        body, grid=(m // tm, n_k),
        in_specs=[a_spec, w_spec, pl.BlockSpec((tm, n), lambda i, j: (i, 0))],
        out_specs=pl.BlockSpec((tm, n), lambda i, j: (i, 0)),
        out_shape=_SDS((m, n), F32), name=name, compiler_params=_cp("arbitrary", "arbitrary"))(a, w, res)


def _matmul_nt(dy, w, w_spec, n_blocks, out_shape, out_spec, name, deps=(), out_dtype=F32):
    m, n = dy.shape
    tm = _rows(m)

    def body(dy_ref, w_ref, *rest):
        o_ref = rest[-1]
        o_ref[...] = _mm_nt(dy_ref[...], _w2d(w_ref)).astype(o_ref.dtype)

    return pl.pallas_call(
        body, grid=(m // tm, n_blocks),
        in_specs=[pl.BlockSpec((tm, n), lambda i, j: (i, 0)), w_spec] + _dep_specs(deps),
        out_specs=out_spec, out_shape=_SDS(out_shape, out_dtype),
        name=name, compiler_params=_cp("arbitrary", "arbitrary"))(dy, w, *deps)


def _matmul_nt_normbwd(dy, dy_spec, w, w_spec, nj, h, g, dh_in, name, w_t=False, act_copy=False):
    m, k = h.shape
    tm = _rows(m)

    def body(dy_ref, w_ref, h_ref, g_ref, dhin_ref, dh_ref, *rest):
        dg_ref, acc_ref = rest[-2:]
        i, j = pl.program_id(0), pl.program_id(1)

        @pl.when(j == 0)
        def _():
            acc_ref[...] = jnp.zeros_like(acc_ref)

        acc_ref[...] += (_mm if w_t else _mm_nt)(dy_ref[...], _w2d(w_ref))

        @pl.when(j == nj - 1)
        def _():
            xv = h_ref[...]
            r = lax.rsqrt(jnp.mean(xv * xv, axis=-1, keepdims=True) + EPS)
            xh = xv * r
            dhn = acc_ref[...]
            part = jnp.sum(dhn * xh, axis=0, keepdims=True)

            @pl.when(i == 0)
            def _():
                dg_ref[...] = part

            @pl.when(i > 0)
            def _():
                dg_ref[...] += part

            t = dhn * g_ref[...]
            dh = dhin_ref[...] + r * (t - xh * jnp.mean(t * xh, axis=-1, keepdims=True))
            dh_ref[...] = dh
            if act_copy:
                rest[0][...] = dh.astype(_ACT)

    rows = pl.BlockSpec((tm, k), lambda i, j: (i, 0))
    outs = pl.pallas_call(
        body, grid=(m // tm, nj),
        in_specs=[dy_spec, w_spec, rows, pl.BlockSpec((1, k), lambda i, j: (0, 0)), rows],
        out_specs=[rows] + [rows] * act_copy + [pl.BlockSpec((1, k), lambda i, j: (0, 0))],
        out_shape=[_SDS((m, k), F32)] + [_SDS((m, k), _ACT)] * act_copy + [_SDS((1, k), F32)],
        scratch_shapes=[pltpu.VMEM((tm, k), F32)],
        name=name, compiler_params=_cp("arbitrary", "arbitrary"))(dy, w, h, g, dh_in)
    return outs[0], (outs[1] if act_copy else None), outs[-1]


def _matmul_tn(x, x_spec, dy, dy_spec, m, n_blocks, acc_shape, out_shape, out_spec, name, tm=None):
    tm = tm or _rows(m)
    nm = m // tm

    def body(x_ref, dy_ref, o_ref, acc_ref):
        @pl.when(pl.program_id(1) == 0)
        def _():
            acc_ref[...] = jnp.zeros_like(acc_ref)

        acc_ref[...] += _mm_tn(x_ref[...], dy_ref[...])

        @pl.when(pl.program_id(1) == nm - 1)
        def _():
            o_ref[...] = acc_ref[...].reshape(o_ref.shape).astype(o_ref.dtype)

    return pl.pallas_call(
        body, grid=(n_blocks, nm), in_specs=[x_spec, dy_spec], out_specs=out_spec,
        out_shape=_SDS(out_shape, _WIRE), scratch_shapes=[pltpu.VMEM(acc_shape, F32)],
        name=name, compiler_params=_cp("arbitrary", "arbitrary"))(x, dy)


def _loss_head(h, g, tgt, name):
    m, k = h.shape
    tm = _rows(m)

    def body(h_ref, g_ref, t_ref, loss_ref, dh_ref, dha_ref, dg_ref):
        i = pl.program_id(0)
        xv = h_ref[...]
        r = lax.rsqrt(jnp.mean(xv * xv, axis=-1, keepdims=True) + EPS)
        xh = xv * r
        gv = g_ref[...]
        err = xh * gv - t_ref[...]
        lpart = jnp.zeros((1, 128), F32) + 0.5 * jnp.sum(jnp.mean(err * err, axis=-1, keepdims=True), axis=0, keepdims=True)
        dy = err * (1.0 / k)
        gpart = jnp.sum(dy * xh, axis=0, keepdims=True)

        @pl.when(i == 0)
        def _():
            loss_ref[...] = lpart
            dg_ref[...] = gpart

        @pl.when(i > 0)
        def _():
            loss_ref[...] += lpart
            dg_ref[...] += gpart

        t = dy * gv
        dh = r * (t - xh * jnp.mean(t * xh, axis=-1, keepdims=True))
        dh_ref[...] = dh
        dha_ref[...] = dh.astype(_ACT)

    rows = pl.BlockSpec((tm, k), lambda i: (i, 0))
    return pl.pallas_call(
        body, grid=(m // tm,),
        in_specs=[rows, pl.BlockSpec((1, k), lambda i: (0, 0)), rows],
        out_specs=[pl.BlockSpec((1, 128), lambda i: (0, 0)), rows, rows, pl.BlockSpec((1, k), lambda i: (0, 0))],
        out_shape=[_SDS((1, 128), F32), _SDS((m, k), F32), _SDS((m, k), _ACT), _SDS((1, k), F32)],
        name=name, compiler_params=_cp("arbitrary"))(h, g, tgt)


def _glu_down(gu, conv_w, conv_b, w_down, res, name):
    s = gu.shape[2]
    tm = min(s, _TM_GLU)

    def body(gu_ref, prev_ref, w_ref, b_ref, wdn_ref, r_ref, o_ref, act_ref, gc_ref):
        i, j = pl.program_id(0), pl.program_id(1)
        prev = jnp.where(i > 0, prev_ref[...].astype(F32), 0.0)
        ext = jnp.concatenate([prev, gu_ref[0].astype(F32)], axis=0)
        gc = b_ref[...] + w_ref[FFN_CONV - 1:FFN_CONV, :] * ext
        for k in range(FFN_CONV - 1):
            gc = gc + w_ref[k:k + 1, :] * pltpu.roll(ext, FFN_CONV - 1 - k, 0)
        gc = gc[GLU_HALO:]
        gc_ref[...] = gc.astype(gc_ref.dtype)
        act =(_silu(gc) * gu_ref[1].astype(F32)).astype(act_ref.dtype)
        act_ref[...] = act
        part = _mm(act, _w2d(wdn_ref))

        @pl.when(j == 0)
        def _():
            o_ref[...] = r_ref[...] + part

        @pl.when(j > 0)
        def _():
            o_ref[...] += part

    return pl.pallas_call(
        body, grid=(s // tm, FF_BLOCKS),
        in_specs=[pl.BlockSpec((2, None, tm, GU_SHARD), lambda i, j: (0, j, i, 0)),
                  pl.BlockSpec((None, None, GLU_HALO, GU_SHARD),
                               lambda i, j: (0, j, jnp.maximum(i * (tm // GLU_HALO) - 1, 0), 0)),
                  pl.BlockSpec((None, HALO, GU_SHARD), lambda i, j: (j, 0, 0)),
                  pl.BlockSpec((None, 1, GU_SHARD), lambda i, j: (j, 0, 0)),
                  _spec_down(1), pl.BlockSpec((tm, D), lambda i, j: (i, 0))],
        out_specs=[pl.BlockSpec((tm, D), lambda i, j: (i, 0)), pl.BlockSpec((None, tm, GU_SHARD), lambda i, j: (j, i, 0)),
                   pl.BlockSpec((None, tm, GU_SHARD), lambda i, j: (j, i, 0))],
        out_shape=[_SDS((s, D), F32), _SDS((FF_BLOCKS, s, GU_SHARD), _ACT), _SDS((FF_BLOCKS, s, GU_SHARD), _ACT)], name=name,
        compiler_params=_cp("arbitrary", "arbitrary"))(gu, gu, conv_w, conv_b, w_down, res)


def _glu_bwd(gu, gc, conv_w, dh, w_down, name, deps=()):
    s = gu.shape[2]
    tm = min(s, _TM_GLU)
    nt = s // tm
    ext_rows = tm + GLU_HALO

    def body(gu_ref, prev_ref, gc_ref, w_ref, dh_ref, wdn_ref, *rest):
        dgu_ref, dw_ref, db_ref, carry_ref = rest[-4:]
        t = pl.program_id(1)
        i = nt - 1 - t

        @pl.when(t == 0)
        def _():
            carry_ref[...] = jnp.zeros_like(carry_ref)
            dw_ref[...] = jnp.zeros_like(dw_ref)
            db_ref[...] = jnp.zeros_like(db_ref)

        up = gu_ref[1].astype(F32)
        prev = jnp.where(i > 0, prev_ref[...].astype(F32), 0.0)
        ext = jnp.concatenate([prev, gu_ref[0].astype(F32)], axis=0)
        gc = gc_ref[...].astype(F32)
        sg = jax.nn.sigmoid(gc)
        da = _mm_nt(dh_ref[...], _w2d(wdn_ref))
        dup = da * (gc * sg)
        dgc = da * up * (sg * (1.0 + gc * (1.0 - sg)))
        db_ref[...] += jnp.sum(dgc, axis=0, keepdims=True)
        dgc_ext = jnp.concatenate([jnp.zeros((GLU_HALO, GU_SHARD), F32), dgc], axis=0)
        ahead = [pltpu.roll(dgc_ext, ext_rows - (FFN_CONV - 1 - j), 0) if j < FFN_CONV - 1 else dgc_ext
                 for j in range(FFN_CONV)]
        dext = ahead[0] * w_ref[0:1, :]
        for j in range(FFN_CONV):
            dw_ref[j:j + 1, :] += jnp.sum(ext * ahead[j], axis=0, keepdims=True)
            if j > 0:
                dext = dext + ahead[j] * w_ref[j:j + 1, :]
        tail = jnp.concatenate([jnp.zeros((tm - GLU_HALO, GU_SHARD), F32), carry_ref[...]], axis=0)
        dgate = dext[GLU_HALO:] + tail
        carry_ref[...] = dext[:GLU_HALO]
        dgu_ref[0] = dgate.astype(dgu_ref.dtype)
        dgu_ref[1] = dup.astype(dgu_ref.dtype)

    return pl.pallas_call(
        body, grid=(FF_BLOCKS, nt),
        in_specs=[pl.BlockSpec((2, None, tm, GU_SHARD), lambda j, t: (0, j, nt - 1 - t, 0)),
                  pl.BlockSpec((None, None, GLU_HALO, GU_SHARD),
                               lambda j, t: (0, j, jnp.maximum((nt - 1 - t) * (tm // GLU_HALO) - 1, 0), 0)),
                  pl.BlockSpec((None, tm, GU_SHARD), lambda j, t: (j, nt - 1 - t, 0)),
                  pl.BlockSpec((None, HALO, GU_SHARD), lambda j, t: (j, 0, 0)),
                  pl.BlockSpec((tm, D), lambda j, t: (nt - 1 - t, 0)), _spec_down(0)] + _dep_specs(deps),
        out_specs=[pl.BlockSpec((2, None, tm, GU_SHARD), lambda j, t: (0, j, nt - 1 - t, 0)),
                   pl.BlockSpec((None, HALO, GU_SHARD), lambda j, t: (j, 0, 0)),
                   pl.BlockSpec((None, 1, GU_SHARD), lambda j, t: (j, 0, 0))],
        out_shape=[_SDS(gu.shape, _ACT), _SDS((FF_BLOCKS, HALO, GU_SHARD), F32), _SDS((FF_BLOCKS, 1, GU_SHARD), F32)],
        scratch_shapes=[pltpu.VMEM((GLU_HALO, GU_SHARD), F32)],
        name=name, compiler_params=_cp("arbitrary", "arbitrary"))(gu, gu, gc, conv_w, dh, w_down, *deps)


def _bucket_table():
    qi = np.arange(BLK)[:, None]
    kj = np.arange(BLK)[None, :]
    n = np.where(kj > qi, BLK + qi - kj, qi - kj)
    max_exact = N_BUCKETS // 2
    nf = np.maximum(n, 1).astype(np.float32)
    large = max_exact + (np.log(nf / max_exact) / math.log(MAX_DISTANCE / max_exact)
                         * (N_BUCKETS - max_exact)).astype(np.int32)
    large = np.minimum(large, N_BUCKETS - 1)
    return np.where(n < max_exact, n, large).astype(np.int32)


def _lane_low():
    return lax.broadcasted_iota(jnp.int32, (1, 128), 1) < A_HD


def _swa_groups(q, kd, vd, sink, bias, upper, first):
    n = A_HEADS // A_KV_HEADS
    ng = A_KV_HEADS
    low = _lane_low()
    qm = [jnp.concatenate([jnp.where(low == (h % 2 == 0), q[g][:, (h // 2) * 128:(h // 2 + 1) * 128], 0.0) for h in range(n)], axis=0)
          for g in range(ng)]
    s2 = [_mm_nt(qm[g], kd[g]) * (A_HD ** -0.5) for g in range(ng)]
    s = [jnp.where(upper[None], s2[g][:, :BLK].reshape(n, BLK, BLK), s2[g][:, BLK:].reshape(n, BLK, BLK)) + bias[g] for g in range(ng)]
    s = [jnp.where((upper & first)[None], -jnp.inf, t) for t in s]
    m = [lax.stop_gradient(jnp.maximum(jnp.max(s[g], axis=-1, keepdims=True), sink[g])) for g in range(ng)]
    p = [jnp.exp(s[g] - m[g]) for g in range(ng)]
    split = [jnp.concatenate([jnp.where(upper[None], t, 0.0), jnp.where(upper[None], 0.0, t)], axis=-1).reshape(n * BLK, 2 * BLK)
             for t in p]
    ones = jnp.ones((BLK, 128), F32)
    den = [_mm(p[g].reshape(n * BLK, BLK), ones) + jnp.exp(sink[g] - m[g]).reshape(n * BLK, 1) for g in range(ng)]
    o = [_mm(split[g], vd[g]) / den[g] for g in range(ng)]
    return [jnp.concatenate([jnp.where(low, t[2 * k * BLK:(2 * k + 1) * BLK], t[(2 * k + 1) * BLK:(2 * k + 2) * BLK])
                             for k in range(n // 2)], axis=1) for t in o]


def _mix_a_core(q, kd, vd, sink, bias, xq, mk, mv, upper, first):
    return _swa_groups(q, kd, vd, sink, bias, upper, first), _cross_pairs(xq, mk, mv)


def _swa_sinks(sink_ref, g):
    n = A_HEADS // A_KV_HEADS
    return jnp.concatenate([sink_ref[:, h:h + 1] for h in range(g * n, (g + 1) * n)], axis=0).reshape(n, 1, 1)


def _both_halves(t, t_rolled, g):
    low = _lane_low()
    return jnp.where(low, t, t_rolled) if g == 0 else jnp.where(low, t_rolled, t)


def _cross_pairs(q, mk, mv):
    rows = q.shape[0]
    low = _lane_low()
    qm = [jnp.concatenate([jnp.where(low, q[:, p * 128:(p + 1) * 128], 0.0), jnp.where(low, 0.0, q[:, p * 128:(p + 1) * 128])], axis=0)
          for p in range(X_HEADS // 2)]
    s = [_mm_nt(qm[p], mk[:, p * 128:(p + 1) * 128]) * (X_HD ** -0.5) for p in range(X_HEADS // 2)]
    e = [jnp.exp(t - lax.stop_gradient(jnp.max(t, axis=-1, keepdims=True))) for t in s]
    pr = [t / jnp.sum(t, axis=-1, keepdims=True) for t in e]
    o = [_mm(pr[p], mv[:, p * 128:(p + 1) * 128]) for p in range(X_HEADS // 2)]
    return jnp.concatenate([jnp.where(low, t[:rows], t[rows:]) for t in o], axis=1)


def _swa_upper():
    qi = lax.broadcasted_iota(jnp.int32, (BLK, BLK), 0)
    kj = lax.broadcasted_iota(jnp.int32, (BLK, BLK), 1)
    return kj > qi


def _bias_build(rel_bias, bucket, name):
    def body(rb_ref, bucket_ref, o_ref):
        b = bucket_ref[...]
        for h in range(A_HEADS):
            acc = jnp.zeros((BLK, BLK), F32)
            for k in range(N_BUCKETS):
                acc = jnp.where(b == k, rb_ref[k, h], acc)
            o_ref[h] = acc

    return pl.pallas_call(
        body, in_specs=[pl.BlockSpec(memory_space=pltpu.SMEM), pl.BlockSpec(memory_space=pltpu.VMEM)],
        out_specs=pl.BlockSpec(memory_space=pltpu.VMEM),
        out_shape=_SDS((A_HEADS, BLK, BLK), F32), name=name)(rel_bias, bucket)


def _bias_reduce(dbias, bucket, name):
    def body(db_ref, bucket_ref, o_ref):
        b = bucket_ref[...]
        row = lax.broadcasted_iota(jnp.int32, (N_BUCKETS, 128), 0)
        lane = lax.broadcasted_iota(jnp.int32, (N_BUCKETS, 128), 1)
        acc = jnp.zeros((N_BUCKETS, 128), F32)
        for h in range(A_HEADS):
            v = db_ref[h]
            for k in range(N_BUCKETS):
                sk = jnp.sum(jnp.sum(jnp.where(b == k, v, 0.0), axis=1, keepdims=True), axis=0, keepdims=True)
                acc = acc + jnp.where((row == k) & (lane == h), sk, 0.0)
        o_ref[...] = acc

    return pl.pallas_call(
        body, in_specs=[pl.BlockSpec(memory_space=pltpu.VMEM)] * 2,
        out_specs=pl.BlockSpec(memory_space=pltpu.VMEM),
        out_shape=_SDS((N_BUCKETS, 128), F32), name=name)(dbias, bucket)


def _mix_a_fwd(proj, bias, sinks, memkv, name):
    s = proj.shape[0]
    nb = s // BLK
    grp = A_HEADS // A_KV_HEADS

    def body(proj_ref, prev_ref, bias_ref, sink_ref, memkv_ref, o_ref):
        i = pl.program_id(0)
        upper = _swa_upper()
        prev = prev_ref[...].astype(F32)
        proj = proj_ref[...].astype(F32)
        kb = jnp.concatenate([prev[:, :A_KV], proj[:, A_Q:A_Q + A_KV]], axis=0)
        vb = jnp.concatenate([prev[:, A_KV:], proj[:, A_Q + A_KV:A_Q + 2 * A_KV]], axis=0)
        kb_r = pltpu.roll(kb, A_HD, 1)
        vb_r = pltpu.roll(vb, A_HD, 1)
        gw = A_Q // A_KV_HEADS
        groups = range(A_KV_HEADS)
        swa, cross = _mix_a_core([proj[:, g * gw:(g + 1) * gw] for g in groups], [_both_halves(kb, kb_r, g) for g in groups],
                                 [_both_halves(vb, vb_r, g) for g in groups], [_swa_sinks(sink_ref, g) for g in groups],
                                 [bias_ref[g * grp:(g + 1) * grp] for g in groups], proj[:, A_Q + 2 * A_KV:],
                                 memkv_ref[:, :X_Q], memkv_ref[:, X_Q:], upper, i == 0)
        o_ref[...] = jnp.concatenate(swa + [cross], axis=1).astype(o_ref.dtype)

    return pl.pallas_call(
        body, grid=(nb,),
        in_specs=[pl.BlockSpec((BLK, IN_A), lambda i: (i, 0)),
                  pl.BlockSpec((BLK, 2 * A_KV), lambda i: (jnp.maximum(i - 1, 0), A_Q // (2 * A_KV))),
                  pl.BlockSpec((A_HEADS, BLK, BLK), lambda i: (0, 0, 0)),
                  pl.BlockSpec((1, 128), lambda i: (0, 0)),
                  pl.BlockSpec((MEM_LEN, 2 * X_Q), lambda i: (0, 0))],
        out_specs=pl.BlockSpec((BLK, D), lambda i: (i, 0)),
        out_shape=_SDS((s, D), _ACT), name=name, compiler_params=_cp("arbitrary"))(proj, proj, bias, sinks, memkv)


def _mix_a_bwd(proj, bias, sinks, memkv, dmix, name, deps=()):
    s = proj.shape[0]
    nb = s // BLK
    grp = A_HEADS // A_KV_HEADS

    def body(proj_ref, prev_ref, bias_ref, sink_ref, memkv_ref, dmix_ref, *rest):
        dproj_ref, dbias_ref, dsink_ref, dmemkv_ref, carry_ref = rest[-5:]
        t = pl.program_id(0)
        i = nb - 1 - t

        @pl.when(t == 0)
        def _():
            carry_ref[...] = jnp.zeros_like(carry_ref)
            dbias_ref[...] = jnp.zeros_like(dbias_ref)
            dsink_ref[...] = jnp.zeros_like(dsink_ref)
            dmemkv_ref[...] = jnp.zeros_like(dmemkv_ref)

        upper = _swa_upper()
        lane = lax.broadcasted_iota(jnp.int32, (1, 128), 1)
        low = _lane_low()
        prev = prev_ref[...].astype(F32)
        proj = proj_ref[...].astype(F32)
        kb = jnp.concatenate([prev[:, :A_KV], proj[:, A_Q:A_Q + A_KV]], axis=0)
        vb = jnp.concatenate([prev[:, A_KV:], proj[:, A_Q + A_KV:A_Q + 2 * A_KV]], axis=0)
        kb_r = pltpu.roll(kb, A_HD, 1)
        vb_r = pltpu.roll(vb, A_HD, 1)
        gw = A_Q // A_KV_HEADS
        groups = range(A_KV_HEADS)
        _, vjp = jax.vjp(
            functools.partial(_mix_a_core, upper=upper, first=i == 0),
            [proj[:, g * gw:(g + 1) * gw] for g in groups], [_both_halves(kb, kb_r, g) for g in groups],
            [_both_halves(vb, vb_r, g) for g in groups], [_swa_sinks(sink_ref, g) for g in groups],
            [bias_ref[g * grp:(g + 1) * grp] for g in groups], proj[:, A_Q + 2 * A_KV:], memkv_ref[:, :X_Q], memkv_ref[:, X_Q:])
        dqs, dk, dv, ds, db, dxq, dmk, dmv = vjp(
            ([dmix_ref[:, g * gw:(g + 1) * gw].astype(F32) for g in groups], dmix_ref[:, A_Q:].astype(F32)))
        dkd = [t + pltpu.roll(t, A_HD, 1) for t in dk]
        dvd = [t + pltpu.roll(t, A_HD, 1) for t in dv]
        dsink = jnp.zeros((1, 128), F32)
        for g in groups:
            for h in range(grp):
                dsink = dsink + jnp.where(lane == g * grp + h, ds[g][h], 0.0)
            dbias_ref[g * grp:(g + 1) * grp] += db[g]
        dsink_ref[...] += dsink
        dkb = jnp.where(low, dkd[0], dkd[1])
        dvb = jnp.where(low, dvd[0], dvd[1])
        dmemkv_ref[...] += jnp.concatenate([dmk, dmv], axis=1)
        dkv_cur = jnp.concatenate([dkb[BLK:], dvb[BLK:]], axis=1) + carry_ref[...]
        carry_ref[...] = jnp.concatenate([dkb[:BLK], dvb[:BLK]], axis=1)
        dproj_ref[...] = jnp.concatenate(list(dqs) + [dkv_cur, dxq], axis=1).astype(dproj_ref.dtype)

    return pl.pallas_call(
        body, grid=(nb,),
        in_specs=[pl.BlockSpec((BLK, IN_A), lambda t: (nb - 1 - t, 0)),
                  pl.BlockSpec((BLK, 2 * A_KV), lambda t: (jnp.maximum(nb - 2 - t, 0), A_Q // (2 * A_KV))),
                  pl.BlockSpec((A_HEADS, BLK, BLK), lambda t: (0, 0, 0)),
                  pl.BlockSpec((1, 128), lambda t: (0, 0)),
                  pl.BlockSpec((MEM_LEN, 2 * X_Q), lambda t: (0, 0)),
                  pl.BlockSpec((BLK, D), lambda t: (nb - 1 - t, 0))] + _dep_specs(deps),
        out_specs=[pl.BlockSpec((BLK, IN_A), lambda t: (nb - 1 - t, 0)),
                   pl.BlockSpec((A_HEADS, BLK, BLK), lambda t: (0, 0, 0)),
                   pl.BlockSpec((1, 128), lambda t: (0, 0)),
                   pl.BlockSpec((MEM_LEN, 2 * X_Q), lambda t: (0, 0))],
        out_shape=[_SDS((s, IN_A), _ACT), _SDS((A_HEADS, BLK, BLK), F32), _SDS((1, 128), F32),
                   _SDS((MEM_LEN, 2 * X_Q), F32)],
        scratch_shapes=[pltpu.VMEM((BLK, 2 * A_KV), F32)],
        name=name, compiler_params=_cp("arbitrary"))(proj, proj, bias, sinks, memkv, dmix, *deps)


def _neumann(pw, rhs):
    nh = len(pw)
    x = rhs
    for lvl in range(6):
        if lvl < 5:
            prod = [_mmf(pw[h], jnp.concatenate([x[h], pw[h]], axis=1)) for h in range(nh)]
            x = [x[h] + prod[h][:, :B_HD] for h in range(nh)]
            pw = [t[:, B_HD:] for t in prod]
        else:
            x = [x[h] + _mmf(pw[h], x[h]) for h in range(nh)]
    return x


@jax.custom_vjp
def _tri_solve(pw, rhs):
    return _neumann(pw, rhs)


def _tri_solve_fwd(pw, rhs):
    x = _neumann(pw, rhs)
    return x, (pw, x)


def _tri_solve_bwd(res, dx):
    pw, x = res
    d_rhs = _neumann([t.T for t in pw], list(dx))
    return [_mmf_nt(d_rhs[h], x[h]) for h in range(len(pw))], d_rhs


_tri_solve.defvjp(_tri_solve_fwd, _tri_solve_bwd)


@jax.custom_vjp
def _tri_solved(pw, rhs, x):
    return x


def _tri_solved_fwd(pw, rhs, x):
    return x, (pw, x)


def _tri_solved_bwd(res, dx):
    d_pw, d_rhs = _tri_solve_bwd(res, dx)
    return d_pw, d_rhs, [jnp.zeros_like(t) for t in res[1]]


_tri_solved.defvjp(_tri_solved_fwd, _tri_solved_bwd)


@jax.custom_vjp
def _known(x, value):
    return value


def _known_fwd(x, value):
    return value, None


def _known_bwd(_, g):
    return g, jnp.zeros_like(g)


_known.defvjp(_known_fwd, _known_bwd)


def _dn_heads(yq, yk, yv, z, bl, al, a_log, dtb, ng, s0, solved=None, out_known=None):
    c = CHUNK
    nh = B_V_HEADS
    rep = B_V_HEADS // B_QK_HEADS
    r = lax.broadcasted_iota(jnp.int32, (c, c), 0)
    cc = lax.broadcasted_iota(jnp.int32, (c, c), 1)
    q = [_silu(t) for t in yq]
    k = [_silu(t) for t in yk]
    v = [_silu(t) for t in yv]
    q = [t * lax.rsqrt(jnp.sum(t * t, axis=-1, keepdims=True) + EPS) * (B_HD ** -0.5) for t in q]
    k = [t * lax.rsqrt(jnp.sum(t * t, axis=-1, keepdims=True) + EPS) for t in k]
    beta = [jax.nn.sigmoid(t) for t in bl]
    g = [-jnp.exp(a_log[h]) * jax.nn.softplus(al[h] + dtb[h]) for h in range(nh)]
    gb = [jnp.broadcast_to(t, (c, c)) for t in g]
    gc_col = [jnp.sum(jnp.where(cc <= r, t.T, 0.0), axis=1, keepdims=True) for t in gb]
    gc_row = [jnp.sum(jnp.where(r <= cc, t, 0.0), axis=0, keepdims=True) for t in gb]
    gc_last = [jnp.sum(t, axis=0, keepdims=True) for t in g]
    decay = [jnp.exp(jnp.where(r >= cc, gc_col[h] - gc_row[h], -jnp.inf)) for h in range(nh)]
    kq = [_mmf_nt(jnp.concatenate([k[h], q[h]], axis=0), k[h]) for h in range(B_QK_HEADS)]
    kk = [t[:c] for t in kq]
    qk = [t[c:] for t in kq]
    egc = [jnp.exp(t) for t in gc_col]
    both = [_mmf(jnp.concatenate([(beta[h] * egc[h]) * k[h // rep], q[h // rep] * egc[h]], axis=0), s0[h]) for h in range(nh)]
    rhs = [beta[h] * v[h] - both[h][:c] for h in range(nh)]
    qs0 = [t[c:] for t in both]
    pw = [-(beta[h] * kk[h // rep] * jnp.where(r > cc, decay[h], 0.0)) for h in range(nh)]
    delta = _tri_solve(pw, rhs) if solved is None else _tri_solved(pw, rhs, solved)
    last = [_mmf(jnp.concatenate([qk[h // rep] * decay[h], (k[h // rep] * jnp.exp(gc_last[h] - gc_col[h])).T], axis=0), delta[h])
            for h in range(nh)]
    out = [qs0[h] + last[h][:c] for h in range(nh)]
    if out_known is not None:
        out = [_known(out[h], out_known[h]) for h in range(nh)]
    s1 = [jnp.exp(gc_last[h]) * s0[h] + last[h][c:] for h in range(nh)]
    o = [t * lax.rsqrt(jnp.mean(t * t, axis=-1, keepdims=True) + EPS) * ng for t in out]
    return [o[h] * _silu(z[h]) for h in range(nh)], s1, delta, out


def _dn_conv(ext, w_ref):
    y = ext * w_ref[B_CONV - 1:B_CONV, :]
    for j in range(B_CONV - 1):
        y = y + w_ref[j:j + 1, :] * pltpu.roll(ext, B_CONV - 1 - j, 0)
    return y


def _dn_args(y, cur_ref, par_ref, ng_ref):
    nh = B_V_HEADS
    return ([y[:, h * B_HD:(h + 1) * B_HD] for h in range(B_QK_HEADS)],
            [y[:, B_QK + h * B_HD:B_QK + (h + 1) * B_HD] for h in range(B_QK_HEADS)],
            [y[:, 2 * B_QK + h * B_HD:2 * B_QK + (h + 1) * B_HD] for h in range(nh)],
            [cur_ref[:, BP_Z + h * B_HD:BP_Z + (h + 1) * B_HD] for h in range(nh)],
            [cur_ref[:, BP_GATE + h:BP_GATE + h + 1] for h in range(nh)],
            [cur_ref[:, BP_GATE + nh + h:BP_GATE + nh + h + 1] for h in range(nh)],
            [par_ref[:, h:h + 1] for h in range(nh)], [par_ref[:, nh + h:nh + h + 1] for h in range(nh)], ng_ref[...])


def _mix_b_fwd(proj, conv_w, par, ng, memkv, name):
    s = proj.shape[0]
    nc = s // CHUNK

    def body(cur_ref, prev_ref, w_ref, par_ref, ng_ref, memkv_ref, o_ref, st_ref, dl_ref, state_ref):
        n = pl.program_id(0)

        @pl.when(n == 0)
        def _():
            state_ref[...] = jnp.zeros_like(state_ref)

        prev = jnp.where(n > 0, prev_ref[...], 0.0)
        ext = jnp.concatenate([prev, cur_ref[:, :B_QKV]], axis=0)
        y = _dn_conv(ext, w_ref)[HALO:]
        s0 = [state_ref[hv] for hv in range(B_V_HEADS)]
        st_ref[0] = state_ref[...]
        outs, s1, delta, raw = _dn_heads(*_dn_args(y, cur_ref, par_ref, ng_ref), s0)
        for hv in range(B_V_HEADS):
            state_ref[hv] = s1[hv]
            dl_ref[0, hv] = delta[hv]
            dl_ref[0, B_V_HEADS + hv] = raw[hv]
        outs = outs + [_cross_pairs(cur_ref[:, BP_XQ:BP_XQ + X_Q], memkv_ref[:, :X_Q], memkv_ref[:, X_Q:])]
        o_ref[...] = jnp.concatenate(outs, axis=1).astype(o_ref.dtype)

    return pl.pallas_call(
        body, grid=(nc,),
        in_specs=[pl.BlockSpec((CHUNK, IN_BP), lambda n: (n, 0)),
                  pl.BlockSpec((HALO, B_QKV), lambda n: (jnp.maximum(n * (CHUNK // HALO) - 1, 0), 0)),
                  pl.BlockSpec((HALO, B_QKV), lambda n: (0, 0)),
                  pl.BlockSpec((1, 128), lambda n: (0, 0)), pl.BlockSpec((1, 128), lambda n: (0, 0)),
                  pl.BlockSpec((MEM_LEN, 2 * X_Q), lambda n: (0, 0))],
        out_specs=[pl.BlockSpec((CHUNK, D), lambda n: (n, 0)),
                   pl.BlockSpec((1, B_V_HEADS, B_HD, B_HD), lambda n: (n, 0, 0, 0)),
                   pl.BlockSpec((1, 2 * B_V_HEADS, CHUNK, B_HD), lambda n: (n, 0, 0, 0))],
        out_shape=[_SDS((s, D), _ACT), _SDS((nc, B_V_HEADS, B_HD, B_HD), F32), _SDS((nc, 2 * B_V_HEADS, CHUNK, B_HD), F32)],
        scratch_shapes=[pltpu.VMEM((B_V_HEADS, B_HD, B_HD), F32)],
        name=name, compiler_params=_cp("arbitrary"))(proj, proj, conv_w, par, ng, memkv)


def _mix_b_bwd(proj, conv_w, par, ng, memkv, states, deltas, dmix, name):
    s = proj.shape[0]
    nc = s // CHUNK
    ext_rows = CHUNK + HALO

    def body(cur_ref, prev_ref, w_ref, par_ref, ng_ref, memkv_ref, st_ref, dl_ref, dmix_ref,
             dproj_ref, dw_ref, dpar_ref, dng_ref, dmemkv_ref, dstate_ref, carry_ref):
        t = pl.program_id(0)
        n = nc - 1 - t

        @pl.when(t == 0)
        def _():
            dstate_ref[...] = jnp.zeros_like(dstate_ref)
            carry_ref[...] = jnp.zeros_like(carry_ref)
            dw_ref[...] = jnp.zeros_like(dw_ref)
            dpar_ref[...] = jnp.zeros_like(dpar_ref)
            dng_ref[...] = jnp.zeros_like(dng_ref)
            dmemkv_ref[...] = jnp.zeros_like(dmemkv_ref)

        lane = lax.broadcasted_iota(jnp.int32, (1, 128), 1)
        prev = jnp.where(n > 0, prev_ref[...], 0.0)
        ext = jnp.concatenate([prev, cur_ref[:, :B_QKV]], axis=0)
        y = _dn_conv(ext, w_ref)[HALO:]
        solved = [dl_ref[0, hv] for hv in range(B_V_HEADS)]
        raw = [dl_ref[0, B_V_HEADS + hv] for hv in range(B_V_HEADS)]
        _, vjp = jax.vjp(functools.partial(_dn_heads, solved=solved, out_known=raw), *_dn_args(y, cur_ref, par_ref, ng_ref),
                         [st_ref[0, hv] for hv in range(B_V_HEADS)])
        none = [jnp.zeros((CHUNK, B_HD), F32)] * B_V_HEADS
        dyq, dyk, dyv, dz, gbl, gal, ga_log, gdtb, dng, gs0 = vjp(
            ([dmix_ref[:, hv * B_HD:(hv + 1) * B_HD].astype(F32) for hv in range(B_V_HEADS)],
             [dstate_ref[hv] for hv in range(B_V_HEADS)], none, none))
        dgate = jnp.zeros((CHUNK, 128), F32)
        dpar = jnp.zeros((1, 128), F32)
        for hv in range(B_V_HEADS):
            dstate_ref[hv] = gs0[hv]
            dgate = dgate + jnp.where(lane == hv, gbl[hv], 0.0) + jnp.where(lane == B_V_HEADS + hv, gal[hv], 0.0)
            dpar = dpar + jnp.where(lane == hv, ga_log[hv], 0.0) + jnp.where(lane == B_V_HEADS + hv, gdtb[hv], 0.0)
        dpar_ref[...] += dpar
        dng_ref[...] += dng
        _, vjp = jax.vjp(_cross_pairs, cur_ref[:, BP_XQ:BP_XQ + X_Q], memkv_ref[:, :X_Q], memkv_ref[:, X_Q:])
        dxq, dmk, dmv = vjp(dmix_ref[:, B_V:].astype(F32))
        dmemkv_ref[...] += jnp.concatenate([dmk, dmv], axis=1)
        dy = jnp.concatenate(list(dyq) + list(dyk) + list(dyv), axis=1)
        dy_ext = jnp.concatenate([jnp.zeros((HALO, B_QKV), F32), dy], axis=0)
        dext = dy_ext * w_ref[B_CONV - 1:B_CONV, :]
        dw_ref[B_CONV - 1:B_CONV, :] += jnp.sum(ext * dy_ext, axis=0, keepdims=True)
        for j in range(B_CONV - 1):
            sh = B_CONV - 1 - j
            dw_ref[j:j + 1, :] += jnp.sum(pltpu.roll(ext, sh, 0) * dy_ext, axis=0, keepdims=True)
            dext = dext + w_ref[j:j + 1, :] * pltpu.roll(dy_ext, ext_rows - sh, 0)
        tail = jnp.concatenate([jnp.zeros((CHUNK - HALO, B_QKV), F32), carry_ref[...]], axis=0)
        dqkv = dext[HALO:] + tail
        carry_ref[...] = dext[:HALO]
        dproj_ref[...] = jnp.concatenate([dqkv] + list(dz) + [dxq, dgate], axis=1).astype(dproj_ref.dtype)

    return pl.pallas_call(
        body, grid=(nc,),
        in_specs=[pl.BlockSpec((CHUNK, IN_BP), lambda t: (nc - 1 - t, 0)),
                  pl.BlockSpec((HALO, B_QKV), lambda t: (jnp.maximum((nc - 1 - t) * (CHUNK // HALO) - 1, 0), 0)),
                  pl.BlockSpec((HALO, B_QKV), lambda t: (0, 0)),
                  pl.BlockSpec((1, 128), lambda t: (0, 0)), pl.BlockSpec((1, 128), lambda t: (0, 0)),
                  pl.BlockSpec((MEM_LEN, 2 * X_Q), lambda t: (0, 0)),
                  pl.BlockSpec((1, B_V_HEADS, B_HD, B_HD), lambda t: (nc - 1 - t, 0, 0, 0)),
                  pl.BlockSpec((1, 2 * B_V_HEADS, CHUNK, B_HD), lambda t: (nc - 1 - t, 0, 0, 0)),
                  pl.BlockSpec((CHUNK, D), lambda t: (nc - 1 - t, 0))],
        out_specs=[pl.BlockSpec((CHUNK, IN_BP), lambda t: (nc - 1 - t, 0)),
                   pl.BlockSpec((HALO, B_QKV), lambda t: (0, 0)),
                   pl.BlockSpec((1, 128), lambda t: (0, 0)), pl.BlockSpec((1, 128), lambda t: (0, 0)),
                   pl.BlockSpec((MEM_LEN, 2 * X_Q), lambda t: (0, 0))],
        out_shape=[_SDS((s, IN_BP), _ACT), _SDS((HALO, B_QKV), F32), _SDS((1, 128), F32), _SDS((1, 128), F32),
                   _SDS((MEM_LEN, 2 * X_Q), F32)],
        scratch_shapes=[pltpu.VMEM((B_V_HEADS, B_HD, B_HD), F32), pltpu.VMEM((HALO, B_QKV), F32)],
        name=name, compiler_params=_cp("arbitrary"))(proj, proj, conv_w, par, ng, memkv, states, deltas, dmix)


def _place():
    return lax.axis_index("x"), lax.axis_index("y"), lax.axis_index("c")


def _all_gather(shards, name):
    n = len(shards)

    def body(*refs):
        ins, outs = refs[:n], refs[n:2 * n]
        send_sems, recv_sems, local_sems = refs[2 * n:]
        x, y, c = _place()
        me, sibling = (x, y, c), (x, y, 1 - c)
        chips = [(1 - x, y), (x, 1 - y), (1 - x, 1 - y)]

        def rows(a, px, py, pc):
            return outs[a].at[4 * px + 2 * py + pc]

        def copy(a, k, block, to, src=None):
            return pltpu.make_async_remote_copy(
                src_ref=rows(a, *block) if src is None else src, dst_ref=rows(a, *block),
                send_sem=send_sems.at[a, k], recv_sem=recv_sems.at[a, k],
                device_id=to, device_id_type=pl.DeviceIdType.MESH)

        mine = [pltpu.make_async_copy(ins[a], rows(a, *me), local_sems.at[a]) for a in range(n)]
        for cp in mine:
            cp.start()
        first = []
        for a in range(n):
            first.append(copy(a, 0, me, sibling, src=ins[a]))
            first += [copy(a, 1 + j, me, (*chip, c), src=ins[a]) for j, chip in enumerate(chips)]
        for cp in first:
            cp.start()
        passed = []
        for j, chip in enumerate(chips):
            for a in range(n):
                copy(a, 1 + j, (*chip, c), me).wait_recv()
                fwd = copy(a, 4 + j, (*chip, c), sibling)
                fwd.start()
                passed.append(fwd)
        for a in range(n):
            copy(a, 0, sibling, me).wait_recv()
            for j, chip in enumerate(chips):
                copy(a, 4 + j, (*chip, 1 - c), me).wait_recv()
        for cp in first + passed:
            cp.wait_send()
        for cp in mine:
            cp.wait()

    hbm = pl.BlockSpec(memory_space=pl.ANY)
    return pl.pallas_call(
        body, out_shape=[_SDS((N_DEV,) + s.shape, s.dtype) for s in shards],
        in_specs=[hbm] * n, out_specs=[hbm] * n,
        scratch_shapes=[pltpu.SemaphoreType.DMA((n, 7)), pltpu.SemaphoreType.DMA((n, 7)), pltpu.SemaphoreType.DMA((n,))],
        name=name)(*shards)


class _Exchange:
    def __init__(self, lands, srcs):
        self.lands, self.srcs = lands, srcs


def _seq_exchange(srcs, land_shapes, plan, name, cid):
    n, nl = len(srcs), len(land_shapes)

    def launch(*refs):
        src_refs, land_refs = refs[:n], refs[n:n + nl]
        send_sems, recv_sems, local_sems = refs[n + nl:]
        x, y, c = _place()
        my = 4 * x + 2 * y + c
        peers = [(x ^ ((k + 1) >> 2 & 1), y ^ ((k + 1) >> 1 & 1), c ^ ((k + 1) & 1)) for k in range(N_DEV - 1)]
        barrier = pltpu.get_barrier_semaphore()
        for p in peers:
            pl.semaphore_signal(barrier, inc=1, device_id=p, device_id_type=pl.DeviceIdType.MESH)
        pl.semaphore_wait(barrier, N_DEV - 1)

        def src_for(a, dest):
            return src_refs[a].at[dest] if plan[a][1] else src_refs[a]

        def slot(a, source):
            return land_refs[plan[a][0]].at[source]

        mine = [pltpu.make_async_copy(src_for(a, my), slot(a, my), local_sems.at[a]) for a in range(n)]
        for cp in mine:
            cp.start()
        sends, recvs = [], []
        for k, (px, py, pc) in enumerate(peers):
            peer = 4 * px + 2 * py + pc
            for a in range(n):
                kw = dict(send_sem=send_sems.at[a * (N_DEV - 1) + k], recv_sem=recv_sems.at[a * (N_DEV - 1) + k],
                          device_id=(px, py, pc), device_id_type=pl.DeviceIdType.MESH)
                sends.append(pltpu.make_async_remote_copy(src_ref=src_for(a, peer), dst_ref=slot(a, my), **kw))
                recvs.append(pltpu.make_async_remote_copy(src_ref=src_for(a, my), dst_ref=slot(a, peer), **kw))
        for cp in sends:
            cp.start()
        for cp in recvs:
            cp.wait_recv()
        for cp in sends:
            cp.wait_send()
        for cp in mine:
            cp.wait()

    lands = pl.kernel(
        launch, out_type=[_SDS(s, d) for s, d in land_shapes],
        mesh=plsc.ScalarSubcoreMesh(axis_name="sequencer", num_cores=1), name=name,
        scratch_types=(pltpu.SemaphoreType.DMA((n * (N_DEV - 1),)), pltpu.SemaphoreType.DMA((n * (N_DEV - 1),)),
                       pltpu.SemaphoreType.DMA((n,))),
        compiler_params=pltpu.CompilerParams(collective_id=cid))(*srcs)
    return _Exchange(list(lands), list(srcs))


def _adam_update(g, w, m, v):
    c1 = 1.0 - ADAM_B1 ** ADAM_STEP
    c2 = 1.0 - ADAM_B2 ** ADAM_STEP
    mm = ADAM_B1 * m + (1.0 - ADAM_B1) * g
    vv = ADAM_B2 * v + (1.0 - ADAM_B2) * (g * g)
    delta = -ADAM_LR * ((mm / c1) / (jnp.sqrt(vv / c2) + ADAM_EPS) + ADAM_WD * w)
    return delta, mm, vv


def _sum_sources(p_ref):
    g = p_ref[0].astype(F32)
    for s in range(1, N_DEV):
        g = g + p_ref[s].astype(F32)
    return g


def _adamw(parts, w, m, v, tr, name, restore_b=False, deps=()):
    nl, r, c = w.shape
    cp = parts[0].shape[-1]

    def body(*refs):
        p_refs = refs[:nl]
        w_ref, m_ref, v_ref = refs[nl:nl + 3]
        g_ref, d_ref, nm_ref, nv_ref = refs[-4:]
        g = _sum_sources(p_refs[0])
        for l in range(1, nl):
            g = jnp.where(pl.program_id(0) == l, _sum_sources(p_refs[l]), g)
        if restore_b:
            g = jnp.concatenate([g[:, :BP_XQ], g[:, BP_GATE:BP_GATE + 2 * B_V_HEADS], g[:, BP_XQ:BP_GATE]], axis=1)
        delta, mm, vv = _adam_update(g, w_ref[...], m_ref[...], v_ref[...])
        g_ref[...] = g
        d_ref[...] = delta
        nm_ref[...] = mm
        nv_ref[...] = vv

    spec = pl.BlockSpec((None, tr, c), lambda l, i: (l, i, 0))
    part_specs = [pl.BlockSpec((N_DEV, tr, cp), functools.partial(lambda l, i, k: (0, jnp.where(l == k, i, 0), 0), k=k))
                  for k in range(nl)]
    return pl.pallas_call(
        body, grid=(nl, r // tr),
        in_specs=part_specs + [spec, spec, spec] + _dep_specs(deps),
        out_specs=[spec] * 4, out_shape=[_SDS(w.shape, F32)] * 4,
        name=name, compiler_params=_cp("arbitrary", "arbitrary"))(*parts, w, m, v, *deps)


def _pack_small(d_rel, d_cb, d_cw, d_qkv, d_mix, d_mem, d_ffn, d_final, d_sinks, d_par, d_ng, loss_row, name):
    flat = [d_rel, *d_cb, *d_cw, d_qkv, *d_mix, *d_mem, *d_ffn, d_final, d_sinks, d_par, d_ng, loss_row]
    n = len(flat)

    def body(*refs):
        ins, o_ref = refs[:n], refs[n]
        rel, cb0, cb1, cw0, cw1, qkv, mx0, mx1, me0, me1, ff0, ff1, fin, snk, par, ng, lss = ins
        o_ref[...] = jnp.zeros_like(o_ref)
        for k in range(N_BUCKETS):
            lane = SP_REL_LANE + 128 * (k % 8)
            o_ref[SP_QKV + k // 8:SP_QKV + k // 8 + 1, lane:lane + 128] = rel[k:k + 1, :]
        for l, (cb, cw) in enumerate(((cb0, cw0), (cb1, cw1))):
            o_ref[SP_CB + l:SP_CB + l + 1, :] = jnp.concatenate([cb[j] for j in range(FF_BLOCKS)], axis=1)
            full = jnp.concatenate([cw[j] for j in range(FF_BLOCKS)], axis=1)
            o_ref[SP_CW + FFN_CONV * l:SP_CW + FFN_CONV * (l + 1), :] = full[:FFN_CONV]
        o_ref[SP_QKV:SP_QKV + B_CONV, 0:B_QKV] = qkv[0:B_CONV, :]
        for base, pair in ((SP_MIX, (mx0, mx1)), (SP_MEM, (me0, me1)), (SP_FFN, (ff0, ff1))):
            for l in range(2):
                o_ref[base + l:base + l + 1, 0:D] = pair[l][...]
        o_ref[SP_FINAL:SP_FINAL + 1, 0:D] = fin[...]
        o_ref[SP_MISC:SP_MISC + 1, 0:128] = snk[...]
        o_ref[SP_MISC:SP_MISC + 1, 128:256] = par[...]
        o_ref[SP_MISC:SP_MISC + 1, 256:384] = ng[...]
        o_ref[SP_MISC:SP_MISC + 1, 384:512] = lss[...]

    vm = pl.BlockSpec(memory_space=pltpu.VMEM)
    return pl.pallas_call(body, in_specs=[vm] * n, out_specs=vm, out_shape=_SDS((SMALL_ROWS, D_FF), F32), name=name)(*flat)


_SMALL = ["rel_bias", "norm_mix_g", "norm_mem_g", "sinks_a", "a_log_b", "dt_bias_b", "out_norm_g_b", "norm_ffn_g",
          "ffn_conv_b", "final_norm_g", "conv_qkv_b", "ffn_conv_w"]


def _adamw_small(recv, rc_qkv, rc_ffn, ws, ms, vs, name, deps=()):
    n = len(_SMALL)

    def body(*refs):
        recv_ref, qkv_ref, ffn_ref = refs[:3]
        w_refs, m_refs, v_refs = refs[3:3 + n], refs[3 + n:3 + 2 * n], refs[3 + 2 * n:3 + 3 * n]
        outs, loss_ref = refs[len(refs) - 4 * n - 1:len(refs) - 1], refs[-1]
        gs = _sum_sources(recv_ref)
        loss_ref[...] = gs[SP_MISC:SP_MISC + 1, 384:512]
        grads = {
            "rel_bias": jnp.concatenate(
                [gs[SP_QKV + k // 8:SP_QKV + k // 8 + 1, SP_REL_LANE + 128 * (k % 8):SP_REL_LANE + 128 * (k % 8) + A_HEADS]
                 for k in range(N_BUCKETS)], axis=0),
            "norm_mix_g": gs[SP_MIX:SP_MIX + 2, 0:D], "norm_mem_g": gs[SP_MEM:SP_MEM + 2, 0:D],
            "sinks_a": gs[SP_MISC:SP_MISC + 1, 0:A_HEADS],
            "a_log_b": gs[SP_MISC:SP_MISC + 1, 128:128 + B_V_HEADS],
            "dt_bias_b": gs[SP_MISC:SP_MISC + 1, 128 + B_V_HEADS:128 + 2 * B_V_HEADS],
            "out_norm_g_b": gs[SP_MISC:SP_MISC + 1, 256:256 + B_HD],
            "norm_ffn_g": gs[SP_FFN:SP_FFN + 2, 0:D], "ffn_conv_b": gs[SP_CB:SP_CB + 2, :],
            "final_norm_g": gs[SP_FINAL:SP_FINAL + 1, 0:D],
            "conv_qkv_b": _sum_sources(qkv_ref), "ffn_conv_w": _sum_sources(ffn_ref),
        }
        for i, nm in enumerate(_SMALL):
            g = grads[nm]
            delta, mm, vv = _adam_update(g, w_refs[i][...], m_refs[i][...], v_refs[i][...])
            outs[i][...] = g
            outs[n + i][...] = delta
            outs[2 * n + i][...] = mm
            outs[3 * n + i][...] = vv

    vm = pl.BlockSpec(memory_space=pltpu.VMEM)
    shapes = [_SDS(w.shape, F32) for w in ws]
    return pl.pallas_call(
        body, in_specs=[vm] * (3 + 3 * n) + _dep_specs(deps), out_specs=[vm] * (4 * n + 1),
        out_shape=shapes * 4 + [_SDS((1, 128), F32)],
        name=name)(recv, rc_qkv, rc_ffn, *ws, *ms, *vs, *deps)


def _assemble(gathered, axis):
    g = jnp.moveaxis(gathered, 0, axis)
    shp = list(g.shape)
    return g.reshape(shp[:axis] + [shp[axis] * shp[axis + 1]] + shp[axis + 2:])


def _pad_rows(a, rows):
    return jnp.pad(a, ((0, rows - a.shape[0]), (0, 0)))


def _pad_lanes(a, lanes=128):
    return jnp.pad(a, ((0, 0), (0, lanes - a.shape[1])))


def _ff_blocks(a):
    return jnp.moveaxis(a.reshape(a.shape[0], FF_BLOCKS, GU_SHARD), 1, 0)


def _reorder_b(w):
    qkv_z = w[..., :B_QKV + B_V]
    gates = w[..., B_QKV + B_V:B_QKV + B_V + 2 * B_V_HEADS]
    xq = w[..., IN_B - X_Q:]
    pad = jnp.zeros(w.shape[:-1] + (IN_BP - IN_B,), w.dtype)
    return jnp.concatenate([qkv_z, xq, gates, pad], axis=-1)


def kernel(x, mem, rel_bias, norm_mix_g, norm_mem_g, w_mem_kv, w_out, w_in_a, sinks_a, w_in_b, conv_qkv_b, a_log_b, dt_bias_b, out_norm_g_b, norm_ffn_g, w_gate_up, ffn_conv_w, ffn_conv_b, w_down, final_norm_g, loss_target, m_rel_bias, m_norm_mix_g, m_norm_mem_g, m_w_mem_kv, m_w_out, m_w_in_a, m_sinks_a, m_w_in_b, m_conv_qkv_b, m_a_log_b, m_dt_bias_b, m_out_norm_g_b, m_norm_ffn_g, m_w_gate_up, m_ffn_conv_w, m_ffn_conv_b, m_w_down, m_final_norm_g, v_rel_bias, v_norm_mix_g, v_norm_mem_g, v_w_mem_kv, v_w_out, v_w_in_a, v_sinks_a, v_w_in_b, v_conv_qkv_b, v_a_log_b, v_dt_bias_b, v_out_norm_g_b, v_norm_ffn_g, v_w_gate_up, v_ffn_conv_w, v_ffn_conv_b, v_w_down, v_final_norm_g):
    local = dict(locals())
    order = ["rel_bias", "norm_mix_g", "norm_mem_g", "w_mem_kv", "w_out", "w_in_a", "sinks_a", "w_in_b", "conv_qkv_b",
             "a_log_b", "dt_bias_b", "out_norm_g_b", "norm_ffn_g", "w_gate_up", "ffn_conv_w", "ffn_conv_b", "w_down",
             "final_norm_g"]
    wts = {n: local[n] for n in order}
    moms = {n: local["m_" + n] for n in order}
    vars_ = {n: local["v_" + n] for n in order}
    h0 = x[0]
    memx = mem[0]
    tgt = loss_target[0]
    s = h0.shape[0]
    tm = _rows(s)
    tb = min(s, _TM_BIG)

    t_ = lambda a: jnp.swapaxes(a, 1, 2)
    g_ia, = _all_gather([t_(w_in_a).astype(_MXU)], "gather_first")
    early = [w_mem_kv[0:1].astype(_MXU), w_out[0:1].astype(_MXU), conv_qkv_b, ffn_conv_w]
    early_w = _seq_exchange(early, [((N_DEV,) + a.shape, a.dtype) for a in early], [(k, False) for k in range(len(early))],
                            "gather_early", 9)
    g_mk0, g_out0, g_cq, g_cw = early_w.lands
    g_mk, g_out = [g_mk0], [g_out0]
    gu_land = ((N_DEV, GU_SHARD, D), _MXU)
    dn_land = ((N_DEV, DN_SHARD, D), _MXU)
    whole = [(0, False), (1, False)]
    def after(a, b):
        return a + (b[(0,) * b.ndim] * 0).astype(a.dtype)

    gu0_w = _seq_exchange([after(t_(w_gate_up)[0].astype(_MXU), g_ia)], [gu_land], [(0, False)], "gather_gate_up0", 1)
    dn0_w = _seq_exchange([after(w_down[0].astype(_MXU), g_ia)], [dn_land], [(0, False)], "gather_down0", 8)
    w_ia = g_ia.reshape(IN_A, D)
    conv_qkv = _pad_rows(_assemble(g_cq, 2)[0], HALO)
    ffn_cw_full = _assemble(g_cw, 2)
    ffn_cw = [_ff_blocks(_pad_rows(ffn_cw_full[i], HALO)) for i in range(2)]
    ffn_cb = [_ff_blocks(ffn_conv_b[i:i + 1]) for i in range(2)]
    bucket = jnp.asarray(_bucket_table())
    bias = _bias_build(rel_bias, bucket, "bias_build")
    sinks = _pad_lanes(sinks_a)
    par_b = _pad_lanes(jnp.concatenate([a_log_b, dt_bias_b], axis=1))

    row_x = pl.BlockSpec((tm, D), lambda i, j: (i, 0))
    gu_shape = (2, FF_BLOCKS, s, GU_SHARD)

    def in_proj(h, g, w, w_spec, n_cols, tn, name, deps=(), out_dtype=F32, w_t=False, tm=None):
        return _norm_matmul(h, g, w, w_spec, n_cols // tn, (h.shape[0], n_cols),
                            pl.BlockSpec((tm or _rows(h.shape[0]), tn), lambda i, j: (i, j)), name, deps=deps, out_dtype=out_dtype,
                            w_t=w_t, tm=tm)

    def ffn_fwd(i, h, g_gu, g_dn, deps=()):
        gu, hn = _norm_matmul(h, norm_ffn_g[i:i + 1], g_gu, _spec_gate_up(1), N_DEV, gu_shape,
                              _spec_gu_act(0, 1, tb), f"gate_up_{i}", deps=deps, out_dtype=_ACT, w_t=True, tm=tb)
        h_new, act, gc = _glu_down(gu, ffn_cw[i], ffn_cb[i], g_dn, h, f"glu_down_{i}")
        return h_new, gu, hn, (act, gc)

    def out_proj(i, mix, h):
        return _matmul_res(mix, row_x, g_out[i], _spec_rowsharded(0, D // N_DEV, D), 1, h, f"out_proj_{i}")

    proj_a, hn_a = in_proj(h0, norm_mix_g[0:1], w_ia, pl.BlockSpec((640, D), lambda i, j: (j, 0)), IN_A, 640, "in_proj_a",
                           deps=early_w.srcs + gu0_w.srcs + dn0_w.srcs, out_dtype=_ACT, w_t=True)
    memkv0, memn0 = in_proj(memx, norm_mem_g[0:1], g_mk[0], _spec_rowsharded(0, D // N_DEV, 2 * X_Q), 2 * X_Q, 2 * X_Q, "mem_proj_0")
    mix_a = _mix_a_fwd(proj_a, bias, sinks, memkv0, "mix_a_fwd")
    h1 = out_proj(0, mix_a, h0)
    g_gu0, g_dn0 = gu0_w.lands[0], dn0_w.lands[0]
    in_b_w = _seq_exchange([after(_reorder_b(w_in_b).astype(_MXU), h1), after(w_mem_kv[1:2].astype(_MXU), h1),
                            after(w_out[1:2].astype(_MXU), h1)],
                           [((N_DEV, 1, D // N_DEV, IN_BP), _MXU), ((N_DEV, 1, D // N_DEV, 2 * X_Q), _MXU),
                            ((N_DEV, 1, D // N_DEV, D), _MXU)], [(0, False), (1, False), (2, False)], "gather_in_b", 2)
    ffn1_w = _seq_exchange([after(t_(w_gate_up)[1].astype(_MXU), h1), after(w_down[1].astype(_MXU), h1)], [gu_land, dn_land], whole,
                           "gather_ffn1", 3)
    h2, gu0, hn_f0, act0 = ffn_fwd(0, h1, g_gu0, g_dn0, deps=in_b_w.srcs + ffn1_w.srcs)
    g_ib, g_mk1, g_out1 = in_b_w.lands
    g_mk.append(g_mk1)
    g_out.append(g_out1)
    proj_b, hn_b = in_proj(h2, norm_mix_g[1:2], g_ib, _spec_rowsharded(0, D // N_DEV, 896, col_block=1), IN_BP, 896, "in_proj_b")
    memkv1, memn1 = in_proj(memx, norm_mem_g[1:2], g_mk[1], _spec_rowsharded(0, D // N_DEV, 2 * X_Q), 2 * X_Q, 2 * X_Q, "mem_proj_1",
                            deps=[h2])
    mix_b, states, deltas = _mix_b_fwd(proj_b, conv_qkv, par_b, out_norm_g_b, memkv1, "mix_b_fwd")
    h3 = out_proj(1, mix_b, h2)
    g_gu1, g_dn1 = ffn1_w.lands
    h4, gu1, hn_f1, act1 = ffn_fwd(1, h3, g_gu1, g_dn1)
    loss_row, *dh, d_final_g = _loss_head(h4, final_norm_g[None, :], tgt, "loss_head")

    zeros_mem = jnp.zeros_like(memx)
    per_dest2 = [(0, True), (1, True)]

    def ffn_bwd(i, dh, h_in, gu, hn_f, act_gc, g_gu, g_dn, deps=()):
        act, gc = act_gc
        dgu, d_cw, d_cb = _glu_bwd(gu, gc, ffn_cw[i], dh[1], g_dn, f"glu_bwd_{i}", deps=deps)
        d_wdown = _matmul_tn(act, pl.BlockSpec((None, tm, GU_SHARD), lambda j, r: (j, r, 0)),
                             dh[1], pl.BlockSpec((tm, D), lambda j, r: (r, 0)), s, FF_BLOCKS, (GU_SHARD, D),
                             (N_DEV, DN_SHARD, D), pl.BlockSpec((2, DN_SHARD, D), lambda j, r: (j, 0, 0)), f"d_w_down_{i}")
        *dh_new, d_g = _matmul_nt_normbwd(dgu, _spec_gu_act(0, 1, tm), g_gu, _spec_gate_up(1), N_DEV, h_in,
                                          norm_ffn_g[i:i + 1], dh[0], f"d_ffn_in_{i}", w_t=True, act_copy=True)
        d_wgu = _matmul_tn(dgu, _spec_gu_act(1, 0, tb), hn_f, pl.BlockSpec((tb, D), lambda j, r: (r, 0)), s, N_DEV,
                           (GU_SHARD, D), (N_DEV, GU_SHARD, D), pl.BlockSpec((None, GU_SHARD, D), lambda j, r: (j, 0, 0)),
                           f"d_w_gate_up_{i}", tm=tb)
        return dh_new, [d_wdown, d_wgu], d_cw, d_cb, d_g

    def out_bwd(i, dh, mix, deps):
        dmix = _matmul_nt(dh[1], g_out[i], _spec_rowsharded(0, D // N_DEV, D), 1, (s, D), row_x, f"d_mix_{i}", deps=deps, out_dtype=_ACT)
        d_wout = _matmul_tn(mix, pl.BlockSpec((tm, D), lambda j, r: (r, 0)), dh[1], pl.BlockSpec((tm, D), lambda j, r: (r, 0)),
                            s, 1, (D, D), (N_DEV, D // N_DEV, D), pl.BlockSpec((N_DEV, D // N_DEV, D), lambda j, r: (0, 0, 0)),
                            f"d_w_out_{i}")
        return dmix, d_wout

    def mem_bwd(i, dmemkv, memn):
        tmm = _rows(MEM_LEN)
        *_, d_g = _matmul_nt_normbwd(dmemkv, pl.BlockSpec((tmm, 2 * X_Q), lambda r, j: (r, 0)), g_mk[i],
                                     _spec_rowsharded(0, D // N_DEV, 2 * X_Q), 1, memx, norm_mem_g[i:i + 1], zeros_mem,
                                     f"d_mem_in_{i}")
        by_row = lambda j, r: (r, 0)
        d_w = _matmul_tn(memn, pl.BlockSpec((tmm, D), by_row), dmemkv, pl.BlockSpec((tmm, 2 * X_Q), by_row), MEM_LEN, 1,
                         (D, 2 * X_Q), (N_DEV, D // N_DEV, 2 * X_Q),
                         pl.BlockSpec((N_DEV, D // N_DEV, 2 * X_Q), lambda j, r: (0, 0, 0)), f"d_w_mem_kv_{i}")
        return d_w, d_g

    out_land = ((N_DEV, D // N_DEV, D), _WIRE)
    mk_land = ((N_DEV, D // N_DEV, 2 * X_Q), _WIRE)
    ffn_lands = [((N_DEV, DN_SHARD, D), _WIRE), ((N_DEV, GU_SHARD, D), _WIRE)]
    dh, d_ffn1, d_cw1, d_cb1, d_gf1 = ffn_bwd(1, dh, h3, gu1, hn_f1, act1, g_gu1, g_dn1)
    ffn1_g = _seq_exchange(d_ffn1, ffn_lands, per_dest2, "send_ffn1_grads", 5)
    dmix, d_wout1 = out_bwd(1, dh, mix_b, ffn1_g.srcs)
    dproj_b, d_convw, d_par, d_ng, dmemkv1 = _mix_b_bwd(proj_b, conv_qkv, par_b, out_norm_g_b, memkv1, states, deltas, dmix, "mix_b_bwd")
    *dh, d_gm1 = _matmul_nt_normbwd(dproj_b, pl.BlockSpec((tm, 896), lambda i, j: (i, j)), g_ib,
                                    _spec_rowsharded(0, D // N_DEV, 896, col_block=1), IN_BP // 896, h2, norm_mix_g[1:2], dh[0],
                                    "d_in_b", act_copy=True)
    d_wib = _matmul_tn(hn_b, pl.BlockSpec((tb, D), lambda j, r: (r, 0)), dproj_b, pl.BlockSpec((tb, 896), lambda j, r: (r, j)),
                       s, IN_BP // 896, (D, 896), (N_DEV, D // N_DEV, IN_BP),
                       pl.BlockSpec((N_DEV, D // N_DEV, 896), lambda j, r: (0, 0, j)), "d_w_in_b", tm=tb)
    d_wmk1, d_gmem1 = mem_bwd(1, dmemkv1, memn1)
    mix1_g = _seq_exchange([d_wout1, d_wib, d_wmk1], [out_land, ((N_DEV, D // N_DEV, IN_BP), _WIRE), mk_land],
                           [(0, True), (1, True), (2, True)], "send_mix1_grads", 6)
    dh, d_ffn0, d_cw0, d_cb0, d_gf0 = ffn_bwd(0, dh, h1, gu0, hn_f0, act0, g_gu0, g_dn0, deps=mix1_g.srcs)
    dmix, d_wout0 = out_bwd(0, dh, mix_a, d_ffn0 + ffn1_g.lands[:1])
    ffn0_g = _seq_exchange(d_ffn0 + [d_wout0], ffn_lands + [out_land], per_dest2 + [(2, True)], "send_ffn0_grads", 4)
    dproj_a, dbias, dsinks, dmemkv0 = _mix_a_bwd(proj_a, bias, sinks, memkv0, dmix, "mix_a_bwd", deps=ffn0_g.srcs)
    dx, _, d_gm0 = _matmul_nt_normbwd(dproj_a, pl.BlockSpec((tm, 640), lambda i, j: (i, j)), w_ia,
                                      pl.BlockSpec((640, D), lambda i, j: (j, 0)), IN_A // 640, h0, norm_mix_g[0:1], dh[0],
                                      "d_in_a", w_t=True)
    d_wia = _matmul_tn(dproj_a, pl.BlockSpec((tm, IN_A), lambda j, r: (r, 0)), hn_a, pl.BlockSpec((tm, D), lambda j, r: (r, 0)),
                       s, 1, (IN_A, D), (N_DEV, IA_SHARD, D), pl.BlockSpec((N_DEV, IA_SHARD, D), lambda j, r: (0, 0, 0)),
                       "d_w_in_a")
    d_wmk0, d_gmem0 = mem_bwd(0, dmemkv0, memn0)
    d_rel = _bias_reduce(dbias, bucket, "bias_reduce")
    small = _pack_small(d_rel, (d_cb0, d_cb1), (d_cw0, d_cw1), d_convw, (d_gm0, d_gm1), (d_gmem0, d_gmem1),
                        (d_gf0, d_gf1), d_final_g, dsinks, d_par, d_ng, loss_row, "pack_small")
    mix0_g = _seq_exchange([d_wia, d_wmk0, small],
                           [((N_DEV, IA_SHARD, D), _WIRE), mk_land, ((N_DEV, SMALL_ROWS, D_FF), F32)],
                           [(0, True), (1, True), (2, False)], "send_mix0_grads", 7)

    res = {}
    last = []

    def update(nm, parts, tr, restore=False, transposed=False):
        view = t_ if transposed else (lambda a: a)
        out = _adamw(parts, view(wts[nm]), view(moms[nm]), view(vars_[nm]), tr, "adamw_" + nm, restore_b=restore, deps=last[-1:])
        res[nm] = [view(o) for o in out]
        last.append(out[1])

    r_dn1, r_gu1 = ffn1_g.lands
    r_dn0, r_gu0, r_out0 = ffn0_g.lands
    r_out1, r_ib, r_mk1 = mix1_g.lands
    update("w_in_b", [r_ib], 32, True)
    update("w_gate_up", [r_gu0, r_gu1], 176, transposed=True)
    update("w_down", [r_dn0, r_dn1], 176)
    r_ia, r_mk0, r_small = mix0_g.lands
    update("w_mem_kv", [r_mk0, r_mk1], 128)
    update("w_out", [r_out0, r_out1], 128)
    update("w_in_a", [r_ia], IA_SHARD, transposed=True)

    my = 4 * lax.axis_index("x") + 2 * lax.axis_index("y") + lax.axis_index("c")
    cq = conv_qkv_b.shape[-1]
    cf = ffn_conv_w.shape[-1]
    rc_qkv = lax.dynamic_slice_in_dim(r_small[:, SP_QKV:SP_QKV + B_CONV, :B_QKV], my * cq, cq, axis=2)[:, None]
    rc_ffn = lax.dynamic_slice_in_dim(r_small[:, SP_CW:SP_CW + 2 * FFN_CONV, :], my * cf, cf, axis=2).reshape(N_DEV, 2, FFN_CONV, cf)
    as2d = lambda a: a[None, :] if a.ndim == 1 else a
    small_out = _adamw_small(r_small, rc_qkv, rc_ffn, [as2d(wts[n]) for n in _SMALL], [as2d(moms[n]) for n in _SMALL],
                             [as2d(vars_[n]) for n in _SMALL], "adamw_small", deps=last[-1:])
    ns = len(_SMALL)
    for i, nm in enumerate(_SMALL):
        res[nm] = [small_out[k * ns + i].reshape(wts[nm].shape) for k in range(4)]

    return (small_out[-1][0, 0], dx[None], *[res[n][0] for n in order], *[res[n][1] for n in order],
            *[res[n][2] for n in order], *[res[n][3] for n in order])
```

```python
import functools
import math

import numpy as np

import jax
import jax.numpy as jnp
from jax import lax
from jax.experimental import pallas as pl
from jax.experimental.pallas import tpu as pltpu
from jax.experimental.pallas import tpu_sc as plsc

F32 = jnp.float32
_MXU = jnp.bfloat16
_ACT = jnp.bfloat16
_WIRE = jnp.bfloat16
_HI = lax.Precision.HIGH
_TM = 1024
_TM_GLU = 1024
_TM_BIG = 2048
_VMEM_LIMIT = 48 * 1024 * 1024
_SDS = jax.ShapeDtypeStruct

D = 1024
EPS = 1e-6
A_HEADS, A_KV_HEADS, A_HD, BLK = 12, 2, 64, 128
N_BUCKETS, MAX_DISTANCE = 32, 128
B_QK_HEADS, B_V_HEADS, B_HD, B_CONV, CHUNK = 3, 6, 128, 4, 64
X_HEADS, X_HD, MEM_LEN = 4, 64, 256
D_FF, FFN_CONV = 2816, 3
A_Q, A_KV, X_Q = 768, 128, 256
B_QK, B_V, B_QKV = 384, 768, 1536
IN_A, IN_B = 1280, 2572
IN_BP = 2688
BP_Z, BP_XQ, BP_GATE = 1536, 2304, 2560
HALO = 8
GLU_HALO = 16

N_DEV = 8
GU_SHARD = 2 * D_FF // N_DEV
FF_BLOCKS = D_FF // GU_SHARD
DN_SHARD = D_FF // N_DEV
IA_SHARD = IN_A // N_DEV

ADAM_LR, ADAM_B1, ADAM_B2, ADAM_EPS, ADAM_WD, ADAM_STEP = 0.001, 0.9, 0.999, 1e-08, 0.01, 10

SP_CB, SP_CW, SP_QKV, SP_MIX, SP_MEM, SP_FFN, SP_FINAL, SP_MISC, SMALL_ROWS = 0, 2, 8, 12, 14, 16, 18, 19, 24
SP_REL_LANE = B_QKV


def _cp(*sems):
    return pltpu.CompilerParams(dimension_semantics=sems, vmem_limit_bytes=_VMEM_LIMIT)


def _mm(a, b):
    return jnp.dot(a.astype(_MXU), b.astype(_MXU), preferred_element_type=F32)


def _mm_nt(a, b):
    return lax.dot_general(a.astype(_MXU), b.astype(_MXU), (((1,), (1,)), ((), ())), preferred_element_type=F32)


def _mm_tn(a, b):
    return lax.dot_general(a.astype(_MXU), b.astype(_MXU), (((0,), (0,)), ((), ())), preferred_element_type=F32)


def _mmf(a, b):
    return jnp.dot(a, b, preferred_element_type=F32, precision=_HI)


def _mmf_nt(a, b):
    return lax.dot_general(a, b, (((1,), (1,)), ((), ())), preferred_element_type=F32, precision=_HI)


def _silu(x):
    return x * jax.nn.sigmoid(x)


def _w2d(ref):
    v = ref[...]
    return v.reshape(-1, v.shape[-1])


def _rows(m):
    return min(m, _TM)


def _spec_rowsharded(layer, rows, cols, col_block=None):
    if col_block is None:
        return pl.BlockSpec((N_DEV, None, rows, cols), lambda *_: (0, layer, 0, 0))
    return pl.BlockSpec((N_DEV, None, rows, cols), lambda *ids: (0, layer, 0, ids[col_block]))


def _spec_gate_up(axis):
    return pl.BlockSpec((None, GU_SHARD, D), lambda *ids: (ids[axis], 0, 0))


def _spec_down(axis):
    return pl.BlockSpec((2, DN_SHARD, D), lambda *ids: (ids[axis], 0, 0))


def _dep_specs(deps):
    return [pl.BlockSpec(memory_space=pl.ANY) for d in deps]


def _spec_gu_act(row_axis, axis, tm):
    return pl.BlockSpec((None, None, tm, GU_SHARD), lambda *ids: (ids[axis] // FF_BLOCKS, ids[axis] % FF_BLOCKS, ids[row_axis], 0))


def _norm_matmul(x, g, w, w_spec, n_blocks, out_shape, out_spec, name, deps=(), out_dtype=F32, w_t=False, tm=None):
    m, k = x.shape
    tm = tm or _rows(m)

    def body(x_ref, g_ref, w_ref, *rest):
        y_ref, hn_ref = rest[-2:]

        @pl.when(pl.program_id(1) == 0)
        def _():
            xv = x_ref[...]
            r = lax.rsqrt(jnp.mean(xv * xv, axis=-1, keepdims=True) + EPS)
            hn_ref[...] = (xv * r * g_ref[...]).astype(hn_ref.dtype)

        y_ref[...] = (_mm_nt if w_t else _mm)(hn_ref[...], _w2d(w_ref)).astype(y_ref.dtype)

    return pl.pallas_call(
        body, grid=(m // tm, n_blocks),
        in_specs=[pl.BlockSpec((tm, k), lambda i, j: (i, 0)), pl.BlockSpec((1, k), lambda i, j: (0, 0)), w_spec]
        + _dep_specs(deps),
        out_specs=[out_spec, pl.BlockSpec((tm, k), lambda i, j: (i, 0))],
        out_shape=[_SDS(out_shape, out_dtype), _SDS((m, k), _ACT)],
        name=name, compiler_params=_cp("arbitrary", "arbitrary"))(x, g, w, *deps)


def _matmul_res(a, a_spec, w, w_spec, n_k, res, name):
    m, n = res.shape
    tm = _rows(m)

    def body(a_ref, w_ref, r_ref, o_ref):
        part = _mm(a_ref[...], _w2d(w_ref))

        @pl.when(pl.program_id(1) == 0)
        def _():
            o_ref[...] = r_ref[...] + part

        @pl.when(pl.program_id(1) > 0)
        def _():
            o_ref[...] += part

    return pl.pallas_call(
        body, grid=(m // tm, n_k),
        in_specs=[a_spec, w_spec, pl.BlockSpec((tm, n), lambda i, j: (i, 0))],
        out_specs=pl.BlockSpec((tm, n), lambda i, j: (i, 0)),
        out_shape=_SDS((m, n), F32), name=name, compiler_params=_cp("arbitrary", "arbitrary"))(a, w, res)


def _matmul_nt(dy, w, w_spec, n_blocks, out_shape, out_spec, name, deps=(), out_dtype=F32):
    m, n = dy.shape
    tm = _rows(m)

    def body(dy_ref, w_ref, *rest):
        o_ref = rest[-1]
        o_ref[...] = _mm_nt(dy_ref[...], _w2d(w_ref)).astype(o_ref.dtype)

    return pl.pallas_call(
        body, grid=(m // tm, n_blocks),
        in_specs=[pl.BlockSpec((tm, n), lambda i, j: (i, 0)), w_spec] + _dep_specs(deps),
        out_specs=out_spec, out_shape=_SDS(out_shape, out_dtype),
        name=name, compiler_params=_cp("arbitrary", "arbitrary"))(dy, w, *deps)


def _matmul_nt_normbwd(dy, dy_spec, w, w_spec, nj, h, g, dh_in, name, w_t=False, act_copy=False):
    m, k = h.shape
    tm = _rows(m)

    def body(dy_ref, w_ref, h_ref, g_ref, dhin_ref, dh_ref, *rest):
        dg_ref, acc_ref = rest[-2:]
        i, j = pl.program_id(0), pl.program_id(1)

        @pl.when(j == 0)
        def _():
            acc_ref[...] = jnp.zeros_like(acc_ref)

        acc_ref[...] += (_mm if w_t else _mm_nt)(dy_ref[...], _w2d(w_ref))

        @pl.when(j == nj - 1)
        def _():
            xv = h_ref[...]
            r = lax.rsqrt(jnp.mean(xv * xv, axis=-1, keepdims=True) + EPS)
            xh = xv * r
            dhn = acc_ref[...]
            part = jnp.sum(dhn * xh, axis=0, keepdims=True)

            @pl.when(i == 0)
            def _():
                dg_ref[...] = part

            @pl.when(i > 0)
            def _():
                dg_ref[...] += part

            t = dhn * g_ref[...]
            dh = dhin_ref[...] + r * (t - xh * jnp.mean(t * xh, axis=-1, keepdims=True))
            dh_ref[...] = dh
            if act_copy:
                rest[0][...] = dh.astype(_ACT)

    rows = pl.BlockSpec((tm, k), lambda i, j: (i, 0))
    outs = pl.pallas_call(
        body, grid=(m // tm, nj),
        in_specs=[dy_spec, w_spec, rows, pl.BlockSpec((1, k), lambda i, j: (0, 0)), rows],
        out_specs=[rows] + [rows] * act_copy + [pl.BlockSpec((1, k), lambda i, j: (0, 0))],
        out_shape=[_SDS((m, k), F32)] + [_SDS((m, k), _ACT)] * act_copy + [_SDS((1, k), F32)],
        scratch_shapes=[pltpu.VMEM((tm, k), F32)],
        name=name, compiler_params=_cp("arbitrary", "arbitrary"))(dy, w, h, g, dh_in)
    return outs[0], (outs[1] if act_copy else None), outs[-1]


def _matmul_tn(x, x_spec, dy, dy_spec, m, n_blocks, acc_shape, out_shape, out_spec, name, tm=None):
    tm = tm or _rows(m)
    nm = m // tm

    def body(x_ref, dy_ref, o_ref, acc_ref):
        @pl.when(pl.program_id(1) == 0)
        def _():
            acc_ref[...] = jnp.zeros_like(acc_ref)

        acc_ref[...] += _mm_tn(x_ref[...], dy_ref[...])

        @pl.when(pl.program_id(1) == nm - 1)
        def _():
            o_ref[...] = acc_ref[...].reshape(o_ref.shape).astype(o_ref.dtype)

    return pl.pallas_call(
        body, grid=(n_blocks, nm), in_specs=[x_spec, dy_spec], out_specs=out_spec,
        out_shape=_SDS(out_shape, _WIRE), scratch_shapes=[pltpu.VMEM(acc_shape, F32)],
        name=name, compiler_params=_cp("arbitrary", "arbitrary"))(x, dy)


def _loss_head(h, g, tgt, name):
    m, k = h.shape
    tm = _rows(m)

    def body(h_ref, g_ref, t_ref, loss_ref, dh_ref, dha_ref, dg_ref):
        i = pl.program_id(0)
        xv = h_ref[...]
        r = lax.rsqrt(jnp.mean(xv * xv, axis=-1, keepdims=True) + EPS)
        xh = xv * r
        gv = g_ref[...]
        err = xh * gv - t_ref[...]
        lpart = jnp.zeros((1, 128), F32) + 0.5 * jnp.sum(jnp.mean(err * err, axis=-1, keepdims=True), axis=0, keepdims=True)
        dy = err * (1.0 / k)
        gpart = jnp.sum(dy * xh, axis=0, keepdims=True)

        @pl.when(i == 0)
        def _():
            loss_ref[...] = lpart
            dg_ref[...] = gpart

        @pl.when(i > 0)
        def _():
            loss_ref[...] += lpart
            dg_ref[...] += gpart

        t = dy * gv
        dh = r * (t - xh * jnp.mean(t * xh, axis=-1, keepdims=True))
        dh_ref[...] = dh
        dha_ref[...] = dh.astype(_ACT)

    rows = pl.BlockSpec((tm, k), lambda i: (i, 0))
    return pl.pallas_call(
        body, grid=(m // tm,),
        in_specs=[rows, pl.BlockSpec((1, k), lambda i: (0, 0)), rows],
        out_specs=[pl.BlockSpec((1, 128), lambda i: (0, 0)), rows, rows, pl.BlockSpec((1, k), lambda i: (0, 0))],
        out_shape=[_SDS((1, 128), F32), _SDS((m, k), F32), _SDS((m, k), _ACT), _SDS((1, k), F32)],
        name=name, compiler_params=_cp("arbitrary"))(h, g, tgt)


def _glu_down(gu, conv_w, conv_b, w_down, res, name):
    s = gu.shape[2]
    tm = min(s, _TM_GLU)

    def body(gu_ref, prev_ref, w_ref, b_ref, wdn_ref, r_ref, o_ref, act_ref, gc_ref):
        i, j = pl.program_id(0), pl.program_id(1)
        prev = jnp.where(i > 0, prev_ref[...].astype(F32), 0.0)
        ext = jnp.concatenate([prev, gu_ref[0].astype(F32)], axis=0)
        gc = b_ref[...] + w_ref[FFN_CONV - 1:FFN_CONV, :] * ext
        for k in range(FFN_CONV - 1):
            gc = gc + w_ref[k:k + 1, :] * pltpu.roll(ext, FFN_CONV - 1 - k, 0)
        gc = gc[GLU_HALO:]
        gc_ref[...] = gc.astype(gc_ref.dtype)
        act =(_silu(gc) * gu_ref[1].astype(F32)).astype(act_ref.dtype)
        act_ref[...] = act
        part = _mm(act, _w2d(wdn_ref))

        @pl.when(j == 0)
        def _():
            o_ref[...] = r_ref[...] + part

        @pl.when(j > 0)
        def _():
            o_ref[...] += part

    return pl.pallas_call(
        body, grid=(s // tm, FF_BLOCKS),
        in_specs=[pl.BlockSpec((2, None, tm, GU_SHARD), lambda i, j: (0, j, i, 0)),
                  pl.BlockSpec((None, None, GLU_HALO, GU_SHARD),
                               lambda i, j: (0, j, jnp.maximum(i * (tm // GLU_HALO) - 1, 0), 0)),
                  pl.BlockSpec((None, HALO, GU_SHARD), lambda i, j: (j, 0, 0)),
                  pl.BlockSpec((None, 1, GU_SHARD), lambda i, j: (j, 0, 0)),
                  _spec_down(1), pl.BlockSpec((tm, D), lambda i, j: (i, 0))],
        out_specs=[pl.BlockSpec((tm, D), lambda i, j: (i, 0)), pl.BlockSpec((None, tm, GU_SHARD), lambda i, j: (j, i, 0)),
                   pl.BlockSpec((None, tm, GU_SHARD), lambda i, j: (j, i, 0))],
        out_shape=[_SDS((s, D), F32), _SDS((FF_BLOCKS, s, GU_SHARD), _ACT), _SDS((FF_BLOCKS, s, GU_SHARD), _ACT)], name=name,
        compiler_params=_cp("arbitrary", "arbitrary"))(gu, gu, conv_w, conv_b, w_down, res)


def _glu_bwd(gu, gc, conv_w, dh, w_down, name, deps=()):
    s = gu.shape[2]
    tm = min(s, _TM_GLU)
    nt = s // tm
    ext_rows = tm + GLU_HALO

    def body(gu_ref, prev_ref, gc_ref, w_ref, dh_ref, wdn_ref, *rest):
        dgu_ref, dw_ref, db_ref, carry_ref = rest[-4:]
        t = pl.program_id(1)
        i = nt - 1 - t

        @pl.when(t == 0)
        def _():
            carry_ref[...] = jnp.zeros_like(carry_ref)
            dw_ref[...] = jnp.zeros_like(dw_ref)
            db_ref[...] = jnp.zeros_like(db_ref)

        up = gu_ref[1].astype(F32)
        prev = jnp.where(i > 0, prev_ref[...].astype(F32), 0.0)
        ext = jnp.concatenate([prev, gu_ref[0].astype(F32)], axis=0)
        gc = gc_ref[...].astype(F32)
        sg = jax.nn.sigmoid(gc)
        da = _mm_nt(dh_ref[...], _w2d(wdn_ref))
        dup = da * (gc * sg)
        dgc = da * up * (sg * (1.0 + gc * (1.0 - sg)))
        db_ref[...] += jnp.sum(dgc, axis=0, keepdims=True)
        dgc_ext = jnp.concatenate([jnp.zeros((GLU_HALO, GU_SHARD), F32), dgc], axis=0)
        ahead = [pltpu.roll(dgc_ext, ext_rows - (FFN_CONV - 1 - j), 0) if j < FFN_CONV - 1 else dgc_ext
                 for j in range(FFN_CONV)]
        dext = ahead[0] * w_ref[0:1, :]
        for j in range(FFN_CONV):
            dw_ref[j:j + 1, :] += jnp.sum(ext * ahead[j], axis=0, keepdims=True)
            if j > 0:
                dext = dext + ahead[j] * w_ref[j:j + 1, :]
        tail = jnp.concatenate([jnp.zeros((tm - GLU_HALO, GU_SHARD), F32), carry_ref[...]], axis=0)
        dgate = dext[GLU_HALO:] + tail
        carry_ref[...] = dext[:GLU_HALO]
        dgu_ref[0] = dgate.astype(dgu_ref.dtype)
        dgu_ref[1] = dup.astype(dgu_ref.dtype)

    return pl.pallas_call(
        body, grid=(FF_BLOCKS, nt),
        in_specs=[pl.BlockSpec((2, None, tm, GU_SHARD), lambda j, t: (0, j, nt - 1 - t, 0)),
                  pl.BlockSpec((None, None, GLU_HALO, GU_SHARD),
                               lambda j, t: (0, j, jnp.maximum((nt - 1 - t) * (tm // GLU_HALO) - 1, 0), 0)),
                  pl.BlockSpec((None, tm, GU_SHARD), lambda j, t: (j, nt - 1 - t, 0)),
                  pl.BlockSpec((None, HALO, GU_SHARD), lambda j, t: (j, 0, 0)),
                  pl.BlockSpec((tm, D), lambda j, t: (nt - 1 - t, 0)), _spec_down(0)] + _dep_specs(deps),
        out_specs=[pl.BlockSpec((2, None, tm, GU_SHARD), lambda j, t: (0, j, nt - 1 - t, 0)),
                   pl.BlockSpec((None, HALO, GU_SHARD), lambda j, t: (j, 0, 0)),
                   pl.BlockSpec((None, 1, GU_SHARD), lambda j, t: (j, 0, 0))],
        out_shape=[_SDS(gu.shape, _ACT), _SDS((FF_BLOCKS, HALO, GU_SHARD), F32), _SDS((FF_BLOCKS, 1, GU_SHARD), F32)],
        scratch_shapes=[pltpu.VMEM((GLU_HALO, GU_SHARD), F32)],
        name=name, compiler_params=_cp("arbitrary", "arbitrary"))(gu, gu, gc, conv_w, dh, w_down, *deps)


def _bucket_table():
    qi = np.arange(BLK)[:, None]
    kj = np.arange(BLK)[None, :]
    n = np.where(kj > qi, BLK + qi - kj, qi - kj)
    max_exact = N_BUCKETS // 2
    nf = np.maximum(n, 1).astype(np.float32)
    large = max_exact + (np.log(nf / max_exact) / math.log(MAX_DISTANCE / max_exact)
                         * (N_BUCKETS - max_exact)).astype(np.int32)
    large = np.minimum(large, N_BUCKETS - 1)
    return np.where(n < max_exact, n, large).astype(np.int32)


def _lane_low():
    return lax.broadcasted_iota(jnp.int32, (1, 128), 1) < A_HD


def _swa_groups(q, kd, vd, sink, bias, upper, first):
    n = A_HEADS // A_KV_HEADS
    ng = A_KV_HEADS
    low = _lane_low()
    qm = [jnp.concatenate([jnp.where(low == (h % 2 == 0), q[g][:, (h // 2) * 128:(h // 2 + 1) * 128], 0.0) for h in range(n)], axis=0)
          for g in range(ng)]
    s2 = [_mm_nt(qm[g], kd[g]) * (A_HD ** -0.5) for g in range(ng)]
    s = [jnp.where(upper[None], s2[g][:, :BLK].reshape(n, BLK, BLK), s2[g][:, BLK:].reshape(n, BLK, BLK)) + bias[g] for g in range(ng)]
    s = [jnp.where((upper & first)[None], -jnp.inf, t) for t in s]
    m = [lax.stop_gradient(jnp.maximum(jnp.max(s[g], axis=-1, keepdims=True), sink[g])) for g in range(ng)]
    p = [jnp.exp(s[g] - m[g]) for g in range(ng)]
    split = [jnp.concatenate([jnp.where(upper[None], t, 0.0), jnp.where(upper[None], 0.0, t)], axis=-1).reshape(n * BLK, 2 * BLK)
             for t in p]
    ones = jnp.ones((BLK, 128), F32)
    den = [_mm(p[g].reshape(n * BLK, BLK), ones) + jnp.exp(sink[g] - m[g]).reshape(n * BLK, 1) for g in range(ng)]
    o = [_mm(split[g], vd[g]) / den[g] for g in range(ng)]
    return [jnp.concatenate([jnp.where(low, t[2 * k * BLK:(2 * k + 1) * BLK], t[(2 * k + 1) * BLK:(2 * k + 2) * BLK])
                             for k in range(n // 2)], axis=1) for t in o]


def _mix_a_core(q, kd, vd, sink, bias, xq, mk, mv, upper, first):
    return _swa_groups(q, kd, vd, sink, bias, upper, first), _cross_pairs(xq, mk, mv)


def _swa_sinks(sink_ref, g):
    n = A_HEADS // A_KV_HEADS
    return jnp.concatenate([sink_ref[:, h:h + 1] for h in range(g * n, (g + 1) * n)], axis=0).reshape(n, 1, 1)


def _both_halves(t, t_rolled, g):
    low = _lane_low()
    return jnp.where(low, t, t_rolled) if g == 0 else jnp.where(low, t_rolled, t)


def _cross_pairs(q, mk, mv):
    rows = q.shape[0]
    low = _lane_low()
    qm = [jnp.concatenate([jnp.where(low, q[:, p * 128:(p + 1) * 128], 0.0), jnp.where(low, 0.0, q[:, p * 128:(p + 1) * 128])], axis=0)
          for p in range(X_HEADS // 2)]
    s = [_mm_nt(qm[p], mk[:, p * 128:(p + 1) * 128]) * (X_HD ** -0.5) for p in range(X_HEADS // 2)]
    e = [jnp.exp(t - lax.stop_gradient(jnp.max(t, axis=-1, keepdims=True))) for t in s]
    pr = [t / jnp.sum(t, axis=-1, keepdims=True) for t in e]
    o = [_mm(pr[p], mv[:, p * 128:(p + 1) * 128]) for p in range(X_HEADS // 2)]
    return jnp.concatenate([jnp.where(low, t[:rows], t[rows:]) for t in o], axis=1)


def _swa_upper():
    qi = lax.broadcasted_iota(jnp.int32, (BLK, BLK), 0)
    kj = lax.broadcasted_iota(jnp.int32, (BLK, BLK), 1)
    return kj > qi


def _bias_build(rel_bias, bucket, name):
    def body(rb_ref, bucket_ref, o_ref):
        b = bucket_ref[...]
        for h in range(A_HEADS):
            acc = jnp.zeros((BLK, BLK), F32)
            for k in range(N_BUCKETS):
                acc = jnp.where(b == k, rb_ref[k, h], acc)
            o_ref[h] = acc

    return pl.pallas_call(
        body, in_specs=[pl.BlockSpec(memory_space=pltpu.SMEM), pl.BlockSpec(memory_space=pltpu.VMEM)],
        out_specs=pl.BlockSpec(memory_space=pltpu.VMEM),
        out_shape=_SDS((A_HEADS, BLK, BLK), F32), name=name)(rel_bias, bucket)


def _bias_reduce(dbias, bucket, name):
    def body(db_ref, bucket_ref, o_ref):
        b = bucket_ref[...]
        row = lax.broadcasted_iota(jnp.int32, (N_BUCKETS, 128), 0)
        lane = lax.broadcasted_iota(jnp.int32, (N_BUCKETS, 128), 1)
        acc = jnp.zeros((N_BUCKETS, 128), F32)
        for h in range(A_HEADS):
            v = db_ref[h]
            for k in range(N_BUCKETS):
                sk = jnp.sum(jnp.sum(jnp.where(b == k, v, 0.0), axis=1, keepdims=True), axis=0, keepdims=True)
                acc = acc + jnp.where((row == k) & (lane == h), sk, 0.0)
        o_ref[...] = acc

    return pl.pallas_call(
        body, in_specs=[pl.BlockSpec(memory_space=pltpu.VMEM)] * 2,
        out_specs=pl.BlockSpec(memory_space=pltpu.VMEM),
        out_shape=_SDS((N_BUCKETS, 128), F32), name=name)(dbias, bucket)


def _mix_a_fwd(proj, bias, sinks, memkv, name):
    s = proj.shape[0]
    nb = s // BLK
    grp = A_HEADS // A_KV_HEADS

    def body(proj_ref, prev_ref, bias_ref, sink_ref, memkv_ref, o_ref):
        i = pl.program_id(0)
        upper = _swa_upper()
        prev = prev_ref[...].astype(F32)
        proj = proj_ref[...].astype(F32)
        kb = jnp.concatenate([prev[:, :A_KV], proj[:, A_Q:A_Q + A_KV]], axis=0)
        vb = jnp.concatenate([prev[:, A_KV:], proj[:, A_Q + A_KV:A_Q + 2 * A_KV]], axis=0)
        kb_r = pltpu.roll(kb, A_HD, 1)
        vb_r = pltpu.roll(vb, A_HD, 1)
        gw = A_Q // A_KV_HEADS
        groups = range(A_KV_HEADS)
        swa, cross = _mix_a_core([proj[:, g * gw:(g + 1) * gw] for g in groups], [_both_halves(kb, kb_r, g) for g in groups],
                                 [_both_halves(vb, vb_r, g) for g in groups], [_swa_sinks(sink_ref, g) for g in groups],
                                 [bias_ref[g * grp:(g + 1) * grp] for g in groups], proj[:, A_Q + 2 * A_KV:],
                                 memkv_ref[:, :X_Q], memkv_ref[:, X_Q:], upper, i == 0)
        o_ref[...] = jnp.concatenate(swa + [cross], axis=1).astype(o_ref.dtype)

    return pl.pallas_call(
        body, grid=(nb,),
        in_specs=[pl.BlockSpec((BLK, IN_A), lambda i: (i, 0)),
                  pl.BlockSpec((BLK, 2 * A_KV), lambda i: (jnp.maximum(i - 1, 0), A_Q // (2 * A_KV))),
                  pl.BlockSpec((A_HEADS, BLK, BLK), lambda i: (0, 0, 0)),
                  pl.BlockSpec((1, 128), lambda i: (0, 0)),
                  pl.BlockSpec((MEM_LEN, 2 * X_Q), lambda i: (0, 0))],
        out_specs=pl.BlockSpec((BLK, D), lambda i: (i, 0)),
        out_shape=_SDS((s, D), _ACT), name=name, compiler_params=_cp("arbitrary"))(proj, proj, bias, sinks, memkv)


def _mix_a_bwd(proj, bias, sinks, memkv, dmix, name, deps=()):
    s = proj.shape[0]
    nb = s // BLK
    grp = A_HEADS // A_KV_HEADS

    def body(proj_ref, prev_ref, bias_ref, sink_ref, memkv_ref, dmix_ref, *rest):
        dproj_ref, dbias_ref, dsink_ref, dmemkv_ref, carry_ref = rest[-5:]
        t = pl.program_id(0)
        i = nb - 1 - t

        @pl.when(t == 0)
        def _():
            carry_ref[...] = jnp.zeros_like(carry_ref)
            dbias_ref[...] = jnp.zeros_like(dbias_ref)
            dsink_ref[...] = jnp.zeros_like(dsink_ref)
            dmemkv_ref[...] = jnp.zeros_like(dmemkv_ref)

        upper = _swa_upper()
        lane = lax.broadcasted_iota(jnp.int32, (1, 128), 1)
        low = _lane_low()
        prev = prev_ref[...].astype(F32)
        proj = proj_ref[...].astype(F32)
        kb = jnp.concatenate([prev[:, :A_KV], proj[:, A_Q:A_Q + A_KV]], axis=0)
        vb = jnp.concatenate([prev[:, A_KV:], proj[:, A_Q + A_KV:A_Q + 2 * A_KV]], axis=0)
        kb_r = pltpu.roll(kb, A_HD, 1)
        vb_r = pltpu.roll(vb, A_HD, 1)
        gw = A_Q // A_KV_HEADS
        groups = range(A_KV_HEADS)
        _, vjp = jax.vjp(
            functools.partial(_mix_a_core, upper=upper, first=i == 0),
            [proj[:, g * gw:(g + 1) * gw] for g in groups], [_both_halves(kb, kb_r, g) for g in groups],
            [_both_halves(vb, vb_r, g) for g in groups], [_swa_sinks(sink_ref, g) for g in groups],
            [bias_ref[g * grp:(g + 1) * grp] for g in groups], proj[:, A_Q + 2 * A_KV:], memkv_ref[:, :X_Q], memkv_ref[:, X_Q:])
        dqs, dk, dv, ds, db, dxq, dmk, dmv = vjp(
            ([dmix_ref[:, g * gw:(g + 1) * gw].astype(F32) for g in groups], dmix_ref[:, A_Q:].astype(F32)))
        dkd = [t + pltpu.roll(t, A_HD, 1) for t in dk]
        dvd = [t + pltpu.roll(t, A_HD, 1) for t in dv]
        dsink = jnp.zeros((1, 128), F32)
        for g in groups:
            for h in range(grp):
                dsink = dsink + jnp.where(lane == g * grp + h, ds[g][h], 0.0)
            dbias_ref[g * grp:(g + 1) * grp] += db[g]
        dsink_ref[...] += dsink
        dkb = jnp.where(low, dkd[0], dkd[1])
        dvb = jnp.where(low, dvd[0], dvd[1])
        dmemkv_ref[...] += jnp.concatenate([dmk, dmv], axis=1)
        dkv_cur = jnp.concatenate([dkb[BLK:], dvb[BLK:]], axis=1) + carry_ref[...]
        carry_ref[...] = jnp.concatenate([dkb[:BLK], dvb[:BLK]], axis=1)
        dproj_ref[...] = jnp.concatenate(list(dqs) + [dkv_cur, dxq], axis=1).astype(dproj_ref.dtype)

    return pl.pallas_call(
        body, grid=(nb,),
        in_specs=[pl.BlockSpec((BLK, IN_A), lambda t: (nb - 1 - t, 0)),
                  pl.BlockSpec((BLK, 2 * A_KV), lambda t: (jnp.maximum(nb - 2 - t, 0), A_Q // (2 * A_KV))),
                  pl.BlockSpec((A_HEADS, BLK, BLK), lambda t: (0, 0, 0)),
                  pl.BlockSpec((1, 128), lambda t: (0, 0)),
                  pl.BlockSpec((MEM_LEN, 2 * X_Q), lambda t: (0, 0)),
                  pl.BlockSpec((BLK, D), lambda t: (nb - 1 - t, 0))] + _dep_specs(deps),
        out_specs=[pl.BlockSpec((BLK, IN_A), lambda t: (nb - 1 - t, 0)),
                   pl.BlockSpec((A_HEADS, BLK, BLK), lambda t: (0, 0, 0)),
                   pl.BlockSpec((1, 128), lambda t: (0, 0)),
                   pl.BlockSpec((MEM_LEN, 2 * X_Q), lambda t: (0, 0))],
        out_shape=[_SDS((s, IN_A), _ACT), _SDS((A_HEADS, BLK, BLK), F32), _SDS((1, 128), F32),
                   _SDS((MEM_LEN, 2 * X_Q), F32)],
        scratch_shapes=[pltpu.VMEM((BLK, 2 * A_KV), F32)],
        name=name, compiler_params=_cp("arbitrary"))(proj, proj, bias, sinks, memkv, dmix, *deps)


def _neumann(pw, rhs):
    nh = len(pw)
    x = rhs
    for lvl in range(6):
        if lvl < 5:
            prod = [_mmf(pw[h], jnp.concatenate([x[h], pw[h]], axis=1)) for h in range(nh)]
            x = [x[h] + prod[h][:, :B_HD] for h in range(nh)]
            pw = [t[:, B_HD:] for t in prod]
        else:
            x = [x[h] + _mmf(pw[h], x[h]) for h in range(nh)]
    return x


@jax.custom_vjp
def _tri_solve(pw, rhs):
    return _neumann(pw, rhs)


def _tri_solve_fwd(pw, rhs):
    x = _neumann(pw, rhs)
    return x, (pw, x)


def _tri_solve_bwd(res, dx):
    pw, x = res
    d_rhs = _neumann([t.T for t in pw], list(dx))
    return [_mmf_nt(d_rhs[h], x[h]) for h in range(len(pw))], d_rhs


_tri_solve.defvjp(_tri_solve_fwd, _tri_solve_bwd)


@jax.custom_vjp
def _tri_solved(pw, rhs, x):
    return x


def _tri_solved_fwd(pw, rhs, x):
    return x, (pw, x)


def _tri_solved_bwd(res, dx):
    d_pw, d_rhs = _tri_solve_bwd(res, dx)
    return d_pw, d_rhs, [jnp.zeros_like(t) for t in res[1]]


_tri_solved.defvjp(_tri_solved_fwd, _tri_solved_bwd)


@jax.custom_vjp
def _known(x, value):
    return value


def _known_fwd(x, value):
    return value, None


def _known_bwd(_, g):
    return g, jnp.zeros_like(g)


_known.defvjp(_known_fwd, _known_bwd)


def _dn_heads(yq, yk, yv, z, bl, al, a_log, dtb, ng, s0, solved=None, out_known=None):
    c = CHUNK
    nh = B_V_HEADS
    rep = B_V_HEADS // B_QK_HEADS
    r = lax.broadcasted_iota(jnp.int32, (c, c), 0)
    cc = lax.broadcasted_iota(jnp.int32, (c, c), 1)
    q = [_silu(t) for t in yq]
    k = [_silu(t) for t in yk]
    v = [_silu(t) for t in yv]
    q = [t * lax.rsqrt(jnp.sum(t * t, axis=-1, keepdims=True) + EPS) * (B_HD ** -0.5) for t in q]
    k = [t * lax.rsqrt(jnp.sum(t * t, axis=-1, keepdims=True) + EPS) for t in k]
    beta = [jax.nn.sigmoid(t) for t in bl]
    g = [-jnp.exp(a_log[h]) * jax.nn.softplus(al[h] + dtb[h]) for h in range(nh)]
    gb = [jnp.broadcast_to(t, (c, c)) for t in g]
    gc_col = [jnp.sum(jnp.where(cc <= r, t.T, 0.0), axis=1, keepdims=True) for t in gb]
    gc_row = [jnp.sum(jnp.where(r <= cc, t, 0.0), axis=0, keepdims=True) for t in gb]
    gc_last = [jnp.sum(t, axis=0, keepdims=True) for t in g]
    decay = [jnp.exp(jnp.where(r >= cc, gc_col[h] - gc_row[h], -jnp.inf)) for h in range(nh)]
    kq = [_mmf_nt(jnp.concatenate([k[h], q[h]], axis=0), k[h]) for h in range(B_QK_HEADS)]
    kk = [t[:c] for t in kq]
    qk = [t[c:] for t in kq]
    egc = [jnp.exp(t) for t in gc_col]
    both = [_mmf(jnp.concatenate([(beta[h] * egc[h]) * k[h // rep], q[h // rep] * egc[h]], axis=0), s0[h]) for h in range(nh)]
    rhs = [beta[h] * v[h] - both[h][:c] for h in range(nh)]
    qs0 = [t[c:] for t in both]
    pw = [-(beta[h] * kk[h // rep] * jnp.where(r > cc, decay[h], 0.0)) for h in range(nh)]
    delta = _tri_solve(pw, rhs) if solved is None else _tri_solved(pw, rhs, solved)
    last = [_mmf(jnp.concatenate([qk[h // rep] * decay[h], (k[h // rep] * jnp.exp(gc_last[h] - gc_col[h])).T], axis=0), delta[h])
            for h in range(nh)]
    out = [qs0[h] + last[h][:c] for h in range(nh)]
    if out_known is not None:
        out = [_known(out[h], out_known[h]) for h in range(nh)]
    s1 = [jnp.exp(gc_last[h]) * s0[h] + last[h][c:] for h in range(nh)]
    o = [t * lax.rsqrt(jnp.mean(t * t, axis=-1, keepdims=True) + EPS) * ng for t in out]
    return [o[h] * _silu(z[h]) for h in range(nh)], s1, delta, out


def _dn_conv(ext, w_ref):
    y = ext * w_ref[B_CONV - 1:B_CONV, :]
    for j in range(B_CONV - 1):
        y = y + w_ref[j:j + 1, :] * pltpu.roll(ext, B_CONV - 1 - j, 0)
    return y


def _dn_args(y, cur_ref, par_ref, ng_ref):
    nh = B_V_HEADS
    return ([y[:, h * B_HD:(h + 1) * B_HD] for h in range(B_QK_HEADS)],
            [y[:, B_QK + h * B_HD:B_QK + (h + 1) * B_HD] for h in range(B_QK_HEADS)],
            [y[:, 2 * B_QK + h * B_HD:2 * B_QK + (h + 1) * B_HD] for h in range(nh)],
            [cur_ref[:, BP_Z + h * B_HD:BP_Z + (h + 1) * B_HD] for h in range(nh)],
            [cur_ref[:, BP_GATE + h:BP_GATE + h + 1] for h in range(nh)],
            [cur_ref[:, BP_GATE + nh + h:BP_GATE + nh + h + 1] for h in range(nh)],
            [par_ref[:, h:h + 1] for h in range(nh)], [par_ref[:, nh + h:nh + h + 1] for h in range(nh)], ng_ref[...])


def _mix_b_fwd(proj, conv_w, par, ng, memkv, name):
    s = proj.shape[0]
    nc = s // CHUNK

    def body(cur_ref, prev_ref, w_ref, par_ref, ng_ref, memkv_ref, o_ref, st_ref, dl_ref, state_ref):
        n = pl.program_id(0)

        @pl.when(n == 0)
        def _():
            state_ref[...] = jnp.zeros_like(state_ref)

        prev = jnp.where(n > 0, prev_ref[...], 0.0)
        ext = jnp.concatenate([prev, cur_ref[:, :B_QKV]], axis=0)
        y = _dn_conv(ext, w_ref)[HALO:]
        s0 = [state_ref[hv] for hv in range(B_V_HEADS)]
        st_ref[0] = state_ref[...]
        outs, s1, delta, raw = _dn_heads(*_dn_args(y, cur_ref, par_ref, ng_ref), s0)
        for hv in range(B_V_HEADS):
            state_ref[hv] = s1[hv]
            dl_ref[0, hv] = delta[hv]
            dl_ref[0, B_V_HEADS + hv] = raw[hv]
        outs = outs + [_cross_pairs(cur_ref[:, BP_XQ:BP_XQ + X_Q], memkv_ref[:, :X_Q], memkv_ref[:, X_Q:])]
        o_ref[...] = jnp.concatenate(outs, axis=1).astype(o_ref.dtype)

    return pl.pallas_call(
        body, grid=(nc,),
        in_specs=[pl.BlockSpec((CHUNK, IN_BP), lambda n: (n, 0)),
                  pl.BlockSpec((HALO, B_QKV), lambda n: (jnp.maximum(n * (CHUNK // HALO) - 1, 0), 0)),
                  pl.BlockSpec((HALO, B_QKV), lambda n: (0, 0)),
                  pl.BlockSpec((1, 128), lambda n: (0, 0)), pl.BlockSpec((1, 128), lambda n: (0, 0)),
                  pl.BlockSpec((MEM_LEN, 2 * X_Q), lambda n: (0, 0))],
        out_specs=[pl.BlockSpec((CHUNK, D), lambda n: (n, 0)),
                   pl.BlockSpec((1, B_V_HEADS, B_HD, B_HD), lambda n: (n, 0, 0, 0)),
                   pl.BlockSpec((1, 2 * B_V_HEADS, CHUNK, B_HD), lambda n: (n, 0, 0, 0))],
        out_shape=[_SDS((s, D), _ACT), _SDS((nc, B_V_HEADS, B_HD, B_HD), F32), _SDS((nc, 2 * B_V_HEADS, CHUNK, B_HD), F32)],
        scratch_shapes=[pltpu.VMEM((B_V_HEADS, B_HD, B_HD), F32)],
        name=name, compiler_params=_cp("arbitrary"))(proj, proj, conv_w, par, ng, memkv)


def _mix_b_bwd(proj, conv_w, par, ng, memkv, states, deltas, dmix, name):
    s = proj.shape[0]
    nc = s // CHUNK
    ext_rows = CHUNK + HALO

    def body(cur_ref, prev_ref, w_ref, par_ref, ng_ref, memkv_ref, st_ref, dl_ref, dmix_ref,
             dproj_ref, dw_ref, dpar_ref, dng_ref, dmemkv_ref, dstate_ref, carry_ref):
        t = pl.program_id(0)
        n = nc - 1 - t

        @pl.when(t == 0)
        def _():
            dstate_ref[...] = jnp.zeros_like(dstate_ref)
            carry_ref[...] = jnp.zeros_like(carry_ref)
            dw_ref[...] = jnp.zeros_like(dw_ref)
            dpar_ref[...] = jnp.zeros_like(dpar_ref)
            dng_ref[...] = jnp.zeros_like(dng_ref)
            dmemkv_ref[...] = jnp.zeros_like(dmemkv_ref)

        lane = lax.broadcasted_iota(jnp.int32, (1, 128), 1)
        prev = jnp.where(n > 0, prev_ref[...], 0.0)
        ext = jnp.concatenate([prev, cur_ref[:, :B_QKV]], axis=0)
        y = _dn_conv(ext, w_ref)[HALO:]
        solved = [dl_ref[0, hv] for hv in range(B_V_HEADS)]
        raw = [dl_ref[0, B_V_HEADS + hv] for hv in range(B_V_HEADS)]
        _, vjp = jax.vjp(functools.partial(_dn_heads, solved=solved, out_known=raw), *_dn_args(y, cur_ref, par_ref, ng_ref),
                         [st_ref[0, hv] for hv in range(B_V_HEADS)])
        none = [jnp.zeros((CHUNK, B_HD), F32)] * B_V_HEADS
        dyq, dyk, dyv, dz, gbl, gal, ga_log, gdtb, dng, gs0 = vjp(
            ([dmix_ref[:, hv * B_HD:(hv + 1) * B_HD].astype(F32) for hv in range(B_V_HEADS)],
             [dstate_ref[hv] for hv in range(B_V_HEADS)], none, none))
        dgate = jnp.zeros((CHUNK, 128), F32)
        dpar = jnp.zeros((1, 128), F32)
        for hv in range(B_V_HEADS):
            dstate_ref[hv] = gs0[hv]
            dgate = dgate + jnp.where(lane == hv, gbl[hv], 0.0) + jnp.where(lane == B_V_HEADS + hv, gal[hv], 0.0)
            dpar = dpar + jnp.where(lane == hv, ga_log[hv], 0.0) + jnp.where(lane == B_V_HEADS + hv, gdtb[hv], 0.0)
        dpar_ref[...] += dpar
        dng_ref[...] += dng
        _, vjp = jax.vjp(_cross_pairs, cur_ref[:, BP_XQ:BP_XQ + X_Q], memkv_ref[:, :X_Q], memkv_ref[:, X_Q:])
        dxq, dmk, dmv = vjp(dmix_ref[:, B_V:].astype(F32))
        dmemkv_ref[...] += jnp.concatenate([dmk, dmv], axis=1)
        dy = jnp.concatenate(list(dyq) + list(dyk) + list(dyv), axis=1)
        dy_ext = jnp.concatenate([jnp.zeros((HALO, B_QKV), F32), dy], axis=0)
        dext = dy_ext * w_ref[B_CONV - 1:B_CONV, :]
        dw_ref[B_CONV - 1:B_CONV, :] += jnp.sum(ext * dy_ext, axis=0, keepdims=True)
        for j in range(B_CONV - 1):
            sh = B_CONV - 1 - j
            dw_ref[j:j + 1, :] += jnp.sum(pltpu.roll(ext, sh, 0) * dy_ext, axis=0, keepdims=True)
            dext = dext + w_ref[j:j + 1, :] * pltpu.roll(dy_ext, ext_rows - sh, 0)
        tail = jnp.concatenate([jnp.zeros((CHUNK - HALO, B_QKV), F32), carry_ref[...]], axis=0)
        dqkv = dext[HALO:] + tail
        carry_ref[...] = dext[:HALO]
        dproj_ref[...] = jnp.concatenate([dqkv] + list(dz) + [dxq, dgate], axis=1).astype(dproj_ref.dtype)

    return pl.pallas_call(
        body, grid=(nc,),
        in_specs=[pl.BlockSpec((CHUNK, IN_BP), lambda t: (nc - 1 - t, 0)),
                  pl.BlockSpec((HALO, B_QKV), lambda t: (jnp.maximum((nc - 1 - t) * (CHUNK // HALO) - 1, 0), 0)),
                  pl.BlockSpec((HALO, B_QKV), lambda t: (0, 0)),
                  pl.BlockSpec((1, 128), lambda t: (0, 0)), pl.BlockSpec((1, 128), lambda t: (0, 0)),
                  pl.BlockSpec((MEM_LEN, 2 * X_Q), lambda t: (0, 0)),
                  pl.BlockSpec((1, B_V_HEADS, B_HD, B_HD), lambda t: (nc - 1 - t, 0, 0, 0)),
                  pl.BlockSpec((1, 2 * B_V_HEADS, CHUNK, B_HD), lambda t: (nc - 1 - t, 0, 0, 0)),
                  pl.BlockSpec((CHUNK, D), lambda t: (nc - 1 - t, 0))],
        out_specs=[pl.BlockSpec((CHUNK, IN_BP), lambda t: (nc - 1 - t, 0)),
                   pl.BlockSpec((HALO, B_QKV), lambda t: (0, 0)),
                   pl.BlockSpec((1, 128), lambda t: (0, 0)), pl.BlockSpec((1, 128), lambda t: (0, 0)),
                   pl.BlockSpec((MEM_LEN, 2 * X_Q), lambda t: (0, 0))],
        out_shape=[_SDS((s, IN_BP), _ACT), _SDS((HALO, B_QKV), F32), _SDS((1, 128), F32), _SDS((1, 128), F32),
                   _SDS((MEM_LEN, 2 * X_Q), F32)],
        scratch_shapes=[pltpu.VMEM((B_V_HEADS, B_HD, B_HD), F32), pltpu.VMEM((HALO, B_QKV), F32)],
        name=name, compiler_params=_cp("arbitrary"))(proj, proj, conv_w, par, ng, memkv, states, deltas, dmix)


def _place():
    return lax.axis_index("x"), lax.axis_index("y"), lax.axis_index("c")


def _all_gather(shards, name):
    n = len(shards)

    def body(*refs):
        ins, outs = refs[:n], refs[n:2 * n]
        send_sems, recv_sems, local_sems = refs[2 * n:]
        x, y, c = _place()
        me, sibling = (x, y, c), (x, y, 1 - c)
        chips = [(1 - x, y), (x, 1 - y), (1 - x, 1 - y)]

        def rows(a, px, py, pc):
            return outs[a].at[4 * px + 2 * py + pc]

        def copy(a, k, block, to, src=None):
            return pltpu.make_async_remote_copy(
                src_ref=rows(a, *block) if src is None else src, dst_ref=rows(a, *block),
                send_sem=send_sems.at[a, k], recv_sem=recv_sems.at[a, k],
                device_id=to, device_id_type=pl.DeviceIdType.MESH)

        mine = [pltpu.make_async_copy(ins[a], rows(a, *me), local_sems.at[a]) for a in range(n)]
        for cp in mine:
            cp.start()
        first = []
        for a in range(n):
            first.append(copy(a, 0, me, sibling, src=ins[a]))
            first += [copy(a, 1 + j, me, (*chip, c), src=ins[a]) for j, chip in enumerate(chips)]
        for cp in first:
            cp.start()
        passed = []
        for j, chip in enumerate(chips):
            for a in range(n):
                copy(a, 1 + j, (*chip, c), me).wait_recv()
                fwd = copy(a, 4 + j, (*chip, c), sibling)
                fwd.start()
                passed.append(fwd)
        for a in range(n):
            copy(a, 0, sibling, me).wait_recv()
            for j, chip in enumerate(chips):
                copy(a, 4 + j, (*chip, 1 - c), me).wait_recv()
        for cp in first + passed:
            cp.wait_send()
        for cp in mine:
            cp.wait()

    hbm = pl.BlockSpec(memory_space=pl.ANY)
    return pl.pallas_call(
        body, out_shape=[_SDS((N_DEV,) + s.shape, s.dtype) for s in shards],
        in_specs=[hbm] * n, out_specs=[hbm] * n,
        scratch_shapes=[pltpu.SemaphoreType.DMA((n, 7)), pltpu.SemaphoreType.DMA((n, 7)), pltpu.SemaphoreType.DMA((n,))],
        name=name)(*shards)


class _Exchange:
    def __init__(self, lands, srcs):
        self.lands, self.srcs = lands, srcs


def _seq_exchange(srcs, land_shapes, plan, name, cid):
    n, nl = len(srcs), len(land_shapes)

    def launch(*refs):
        src_refs, land_refs = refs[:n], refs[n:n + nl]
        send_sems, recv_sems, local_sems = refs[n + nl:]
        x, y, c = _place()
        my = 4 * x + 2 * y + c
        peers = [(x ^ ((k + 1) >> 2 & 1), y ^ ((k + 1) >> 1 & 1), c ^ ((k + 1) & 1)) for k in range(N_DEV - 1)]
        barrier = pltpu.get_barrier_semaphore()
        for p in peers:
            pl.semaphore_signal(barrier, inc=1, device_id=p, device_id_type=pl.DeviceIdType.MESH)
        pl.semaphore_wait(barrier, N_DEV - 1)

        def src_for(a, dest):
            return src_refs[a].at[dest] if plan[a][1] else src_refs[a]

        def slot(a, source):
            return land_refs[plan[a][0]].at[source]

        mine = [pltpu.make_async_copy(src_for(a, my), slot(a, my), local_sems.at[a]) for a in range(n)]
        for cp in mine:
            cp.start()
        sends, recvs = [], []
        for k, (px, py, pc) in enumerate(peers):
            peer = 4 * px + 2 * py + pc
            for a in range(n):
                kw = dict(send_sem=send_sems.at[a * (N_DEV - 1) + k], recv_sem=recv_sems.at[a * (N_DEV - 1) + k],
                          device_id=(px, py, pc), device_id_type=pl.DeviceIdType.MESH)
                sends.append(pltpu.make_async_remote_copy(src_ref=src_for(a, peer), dst_ref=slot(a, my), **kw))
                recvs.append(pltpu.make_async_remote_copy(src_ref=src_for(a, my), dst_ref=slot(a, peer), **kw))
        for cp in sends:
            cp.start()
        for cp in recvs:
            cp.wait_recv()
        for cp in sends:
            cp.wait_send()
        for cp in mine:
            cp.wait()

    lands = pl.kernel(
        launch, out_type=[_SDS(s, d) for s, d in land_shapes],
        mesh=plsc.ScalarSubcoreMesh(axis_name="sequencer", num_cores=1), name=name,
        scratch_types=(pltpu.SemaphoreType.DMA((n * (N_DEV - 1),)), pltpu.SemaphoreType.DMA((n * (N_DEV - 1),)),
                       pltpu.SemaphoreType.DMA((n,))),
        compiler_params=pltpu.CompilerParams(collective_id=cid))(*srcs)
    return _Exchange(list(lands), list(srcs))


def _adam_update(g, w, m, v):
    c1 = 1.0 - ADAM_B1 ** ADAM_STEP
    c2 = 1.0 - ADAM_B2 ** ADAM_STEP
    mm = ADAM_B1 * m + (1.0 - ADAM_B1) * g
    vv = ADAM_B2 * v + (1.0 - ADAM_B2) * (g * g)
    delta = -ADAM_LR * ((mm / c1) / (jnp.sqrt(vv / c2) + ADAM_EPS) + ADAM_WD * w)
    return delta, mm, vv


def _sum_sources(p_ref):
    g = p_ref[0].astype(F32)
    for s in range(1, N_DEV):
        g = g + p_ref[s].astype(F32)
    return g


def _adamw(parts, w, m, v, tr, name, restore_b=False, deps=()):
    nl, r, c = w.shape
    cp = parts[0].shape[-1]

    def body(*refs):
        p_refs = refs[:nl]
        w_ref, m_ref, v_ref = refs[nl:nl + 3]
        g_ref, d_ref, nm_ref, nv_ref = refs[-4:]
        g = _sum_sources(p_refs[0])
        for l in range(1, nl):
            g = jnp.where(pl.program_id(0) == l, _sum_sources(p_refs[l]), g)
        if restore_b:
            g = jnp.concatenate([g[:, :BP_XQ], g[:, BP_GATE:BP_GATE + 2 * B_V_HEADS], g[:, BP_XQ:BP_GATE]], axis=1)
        delta, mm, vv = _adam_update(g, w_ref[...], m_ref[...], v_ref[...])
        g_ref[...] = g
        d_ref[...] = delta
        nm_ref[...] = mm
        nv_ref[...] = vv

    spec = pl.BlockSpec((None, tr, c), lambda l, i: (l, i, 0))
    part_specs = [pl.BlockSpec((N_DEV, tr, cp), functools.partial(lambda l, i, k: (0, jnp.where(l == k, i, 0), 0), k=k))
                  for k in range(nl)]
    return pl.pallas_call(
        body, grid=(nl, r // tr),
        in_specs=part_specs + [spec, spec, spec] + _dep_specs(deps),
        out_specs=[spec] * 4, out_shape=[_SDS(w.shape, F32)] * 4,
        name=name, compiler_params=_cp("arbitrary", "arbitrary"))(*parts, w, m, v, *deps)


def _pack_small(d_rel, d_cb, d_cw, d_qkv, d_mix, d_mem, d_ffn, d_final, d_sinks, d_par, d_ng, loss_row, name):
    flat = [d_rel, *d_cb, *d_cw, d_qkv, *d_mix, *d_mem, *d_ffn, d_final, d_sinks, d_par, d_ng, loss_row]
    n = len(flat)

    def body(*refs):
        ins, o_ref = refs[:n], refs[n]
        rel, cb0, cb1, cw0, cw1, qkv, mx0, mx1, me0, me1, ff0, ff1, fin, snk, par, ng, lss = ins
        o_ref[...] = jnp.zeros_like(o_ref)
        for k in range(N_BUCKETS):
            lane = SP_REL_LANE + 128 * (k % 8)
            o_ref[SP_QKV + k // 8:SP_QKV + k // 8 + 1, lane:lane + 128] = rel[k:k + 1, :]
        for l, (cb, cw) in enumerate(((cb0, cw0), (cb1, cw1))):
            o_ref[SP_CB + l:SP_CB + l + 1, :] = jnp.concatenate([cb[j] for j in range(FF_BLOCKS)], axis=1)
            full = jnp.concatenate([cw[j] for j in range(FF_BLOCKS)], axis=1)
            o_ref[SP_CW + FFN_CONV * l:SP_CW + FFN_CONV * (l + 1), :] = full[:FFN_CONV]
        o_ref[SP_QKV:SP_QKV + B_CONV, 0:B_QKV] = qkv[0:B_CONV, :]
        for base, pair in ((SP_MIX, (mx0, mx1)), (SP_MEM, (me0, me1)), (SP_FFN, (ff0, ff1))):
            for l in range(2):
                o_ref[base + l:base + l + 1, 0:D] = pair[l][...]
        o_ref[SP_FINAL:SP_FINAL + 1, 0:D] = fin[...]
        o_ref[SP_MISC:SP_MISC + 1, 0:128] = snk[...]
        o_ref[SP_MISC:SP_MISC + 1, 128:256] = par[...]
        o_ref[SP_MISC:SP_MISC + 1, 256:384] = ng[...]
        o_ref[SP_MISC:SP_MISC + 1, 384:512] = lss[...]

    vm = pl.BlockSpec(memory_space=pltpu.VMEM)
    return pl.pallas_call(body, in_specs=[vm] * n, out_specs=vm, out_shape=_SDS((SMALL_ROWS, D_FF), F32), name=name)(*flat)


_SMALL = ["rel_bias", "norm_mix_g", "norm_mem_g", "sinks_a", "a_log_b", "dt_bias_b", "out_norm_g_b", "norm_ffn_g",
          "ffn_conv_b", "final_norm_g", "conv_qkv_b", "ffn_conv_w"]


def _adamw_small(recv, rc_qkv, rc_ffn, ws, ms, vs, name, deps=()):
    n = len(_SMALL)

    def body(*refs):
        recv_ref, qkv_ref, ffn_ref = refs[:3]
        w_refs, m_refs, v_refs = refs[3:3 + n], refs[3 + n:3 + 2 * n], refs[3 + 2 * n:3 + 3 * n]
        outs, loss_ref = refs[len(refs) - 4 * n - 1:len(refs) - 1], refs[-1]
        gs = _sum_sources(recv_ref)
        loss_ref[...] = gs[SP_MISC:SP_MISC + 1, 384:512]
        grads = {
            "rel_bias": jnp.concatenate(
                [gs[SP_QKV + k // 8:SP_QKV + k // 8 + 1, SP_REL_LANE + 128 * (k % 8):SP_REL_LANE + 128 * (k % 8) + A_HEADS]
                 for k in range(N_BUCKETS)], axis=0),
            "norm_mix_g": gs[SP_MIX:SP_MIX + 2, 0:D], "norm_mem_g": gs[SP_MEM:SP_MEM + 2, 0:D],
            "sinks_a": gs[SP_MISC:SP_MISC + 1, 0:A_HEADS],
            "a_log_b": gs[SP_MISC:SP_MISC + 1, 128:128 + B_V_HEADS],
            "dt_bias_b": gs[SP_MISC:SP_MISC + 1, 128 + B_V_HEADS:128 + 2 * B_V_HEADS],
            "out_norm_g_b": gs[SP_MISC:SP_MISC + 1, 256:256 + B_HD],
            "norm_ffn_g": gs[SP_FFN:SP_FFN + 2, 0:D], "ffn_conv_b": gs[SP_CB:SP_CB + 2, :],
            "final_norm_g": gs[SP_FINAL:SP_FINAL + 1, 0:D],
            "conv_qkv_b": _sum_sources(qkv_ref), "ffn_conv_w": _sum_sources(ffn_ref),
        }
        for i, nm in enumerate(_SMALL):
            g = grads[nm]
            delta, mm, vv = _adam_update(g, w_refs[i][...], m_refs[i][...], v_refs[i][...])
            outs[i][...] = g
            outs[n + i][...] = delta
            outs[2 * n + i][...] = mm
            outs[3 * n + i][...] = vv

    vm = pl.BlockSpec(memory_space=pltpu.VMEM)
    shapes = [_SDS(w.shape, F32) for w in ws]
    return pl.pallas_call(
        body, in_specs=[vm] * (3 + 3 * n) + _dep_specs(deps), out_specs=[vm] * (4 * n + 1),
        out_shape=shapes * 4 + [_SDS((1, 128), F32)],
        name=name)(recv, rc_qkv, rc_ffn, *ws, *ms, *vs, *deps)


def _assemble(gathered, axis):
    g = jnp.moveaxis(gathered, 0, axis)
    shp = list(g.shape)
    return g.reshape(shp[:axis] + [shp[axis] * shp[axis + 1]] + shp[axis + 2:])


def _pad_rows(a, rows):
    return jnp.pad(a, ((0, rows - a.shape[0]), (0, 0)))


def _pad_lanes(a, lanes=128):
    return jnp.pad(a, ((0, 0), (0, lanes - a.shape[1])))


def _ff_blocks(a):
    return jnp.moveaxis(a.reshape(a.shape[0], FF_BLOCKS, GU_SHARD), 1, 0)


def _reorder_b(w):
    qkv_z = w[..., :B_QKV + B_V]
    gates = w[..., B_QKV + B_V:B_QKV + B_V + 2 * B_V_HEADS]
    xq = w[..., IN_B - X_Q:]
    pad = jnp.zeros(w.shape[:-1] + (IN_BP - IN_B,), w.dtype)
    return jnp.concatenate([qkv_z, xq, gates, pad], axis=-1)


def kernel(x, mem, rel_bias, norm_mix_g, norm_mem_g, w_mem_kv, w_out, w_in_a, sinks_a, w_in_b, conv_qkv_b, a_log_b, dt_bias_b, out_norm_g_b, norm_ffn_g, w_gate_up, ffn_conv_w, ffn_conv_b, w_down, final_norm_g, loss_target, m_rel_bias, m_norm_mix_g, m_norm_mem_g, m_w_mem_kv, m_w_out, m_w_in_a, m_sinks_a, m_w_in_b, m_conv_qkv_b, m_a_log_b, m_dt_bias_b, m_out_norm_g_b, m_norm_ffn_g, m_w_gate_up, m_ffn_conv_w, m_ffn_conv_b, m_w_down, m_final_norm_g, v_rel_bias, v_norm_mix_g, v_norm_mem_g, v_w_mem_kv, v_w_out, v_w_in_a, v_sinks_a, v_w_in_b, v_conv_qkv_b, v_a_log_b, v_dt_bias_b, v_out_norm_g_b, v_norm_ffn_g, v_w_gate_up, v_ffn_conv_w, v_ffn_conv_b, v_w_down, v_final_norm_g):
    local = dict(locals())
    order = ["rel_bias", "norm_mix_g", "norm_mem_g", "w_mem_kv", "w_out", "w_in_a", "sinks_a", "w_in_b", "conv_qkv_b",
             "a_log_b", "dt_bias_b", "out_norm_g_b", "norm_ffn_g", "w_gate_up", "ffn_conv_w", "ffn_conv_b", "w_down",
             "final_norm_g"]
    wts = {n: local[n] for n in order}
    moms = {n: local["m_" + n] for n in order}
    vars_ = {n: local["v_" + n] for n in order}
    h0 = x[0]
    memx = mem[0]
    tgt = loss_target[0]
    s = h0.shape[0]
    tm = _rows(s)
    tb = min(s, _TM_BIG)

    t_ = lambda a: jnp.swapaxes(a, 1, 2)
    g_mk0, g_out0, g_ia, g_cq, g_cw = _all_gather(
        [w_mem_kv[0:1].astype(_MXU), w_out[0:1].astype(_MXU), t_(w_in_a).astype(_MXU), conv_qkv_b, ffn_conv_w], "gather_first")
    g_mk, g_out = [g_mk0], [g_out0]
    gu_land = ((N_DEV, GU_SHARD, D), _MXU)
    dn_land = ((N_DEV, DN_SHARD, D), _MXU)
    whole = [(0, False), (1, False)]
    def after(a, b):
        return a + (b[(0,) * b.ndim] * 0).astype(a.dtype)

    gu0_w = _seq_exchange([after(t_(w_gate_up)[0].astype(_MXU), g_ia)], [gu_land], [(0, False)], "gather_gate_up0", 1)
    dn0_w = _seq_exchange([after(w_down[0].astype(_MXU), g_ia)], [dn_land], [(0, False)], "gather_down0", 8)
    w_ia = g_ia.reshape(IN_A, D)
    conv_qkv = _pad_rows(_assemble(g_cq, 2)[0], HALO)
    ffn_cw_full = _assemble(g_cw, 2)
    ffn_cw = [_ff_blocks(_pad_rows(ffn_cw_full[i], HALO)) for i in range(2)]
    ffn_cb = [_ff_blocks(ffn_conv_b[i:i + 1]) for i in range(2)]
    bucket = jnp.asarray(_bucket_table())
    bias = _bias_build(rel_bias, bucket, "bias_build")
    sinks = _pad_lanes(sinks_a)
    par_b = _pad_lanes(jnp.concatenate([a_log_b, dt_bias_b], axis=1))

    row_x = pl.BlockSpec((tm, D), lambda i, j: (i, 0))
    gu_shape = (2, FF_BLOCKS, s, GU_SHARD)

    def in_proj(h, g, w, w_spec, n_cols, tn, name, deps=(), out_dtype=F32, w_t=False, tm=None):
        return _norm_matmul(h, g, w, w_spec, n_cols // tn, (h.shape[0], n_cols),
                            pl.BlockSpec((tm or _rows(h.shape[0]), tn), lambda i, j: (i, j)), name, deps=deps, out_dtype=out_dtype,
                            w_t=w_t, tm=tm)

    def ffn_fwd(i, h, g_gu, g_dn, deps=()):
        gu, hn = _norm_matmul(h, norm_ffn_g[i:i + 1], g_gu, _spec_gate_up(1), N_DEV, gu_shape,
                              _spec_gu_act(0, 1, tb), f"gate_up_{i}", deps=deps, out_dtype=_ACT, w_t=True, tm=tb)
        h_new, act, gc = _glu_down(gu, ffn_cw[i], ffn_cb[i], g_dn, h, f"glu_down_{i}")
        return h_new, gu, hn, (act, gc)

    def out_proj(i, mix, h):
        return _matmul_res(mix, row_x, g_out[i], _spec_rowsharded(0, D // N_DEV, D), 1, h, f"out_proj_{i}")

    proj_a, hn_a = in_proj(h0, norm_mix_g[0:1], w_ia, pl.BlockSpec((640, D), lambda i, j: (j, 0)), IN_A, 640, "in_proj_a",
                           deps=gu0_w.srcs + dn0_w.srcs, out_dtype=_ACT, w_t=True)
    memkv0, memn0 = in_proj(memx, norm_mem_g[0:1], g_mk[0], _spec_rowsharded(0, D // N_DEV, 2 * X_Q), 2 * X_Q, 2 * X_Q, "mem_proj_0")
    mix_a = _mix_a_fwd(proj_a, bias, sinks, memkv0, "mix_a_fwd")
    h1 = out_proj(0, mix_a, h0)
    g_gu0, g_dn0 = gu0_w.lands[0], dn0_w.lands[0]
    in_b_w = _seq_exchange([after(_reorder_b(w_in_b).astype(_MXU), h1), after(w_mem_kv[1:2].astype(_MXU), h1),
                            after(w_out[1:2].astype(_MXU), h1)],
                           [((N_DEV, 1, D // N_DEV, IN_BP), _MXU), ((N_DEV, 1, D // N_DEV, 2 * X_Q), _MXU),
                            ((N_DEV, 1, D // N_DEV, D), _MXU)], [(0, False), (1, False), (2, False)], "gather_in_b", 2)
    ffn1_w = _seq_exchange([after(t_(w_gate_up)[1].astype(_MXU), h1), after(w_down[1].astype(_MXU), h1)], [gu_land, dn_land], whole,
                           "gather_ffn1", 3)
    h2, gu0, hn_f0, act0 = ffn_fwd(0, h1, g_gu0, g_dn0, deps=in_b_w.srcs + ffn1_w.srcs)
    g_ib, g_mk1, g_out1 = in_b_w.lands
    g_mk.append(g_mk1)
    g_out.append(g_out1)
    proj_b, hn_b = in_proj(h2, norm_mix_g[1:2], g_ib, _spec_rowsharded(0, D // N_DEV, 896, col_block=1), IN_BP, 896, "in_proj_b")
    memkv1, memn1 = in_proj(memx, norm_mem_g[1:2], g_mk[1], _spec_rowsharded(0, D // N_DEV, 2 * X_Q), 2 * X_Q, 2 * X_Q, "mem_proj_1",
                            deps=[h2])
    mix_b, states, deltas = _mix_b_fwd(proj_b, conv_qkv, par_b, out_norm_g_b, memkv1, "mix_b_fwd")
    h3 = out_proj(1, mix_b, h2)
    g_gu1, g_dn1 = ffn1_w.lands
    h4, gu1, hn_f1, act1 = ffn_fwd(1, h3, g_gu1, g_dn1)
    loss_row, *dh, d_final_g = _loss_head(h4, final_norm_g[None, :], tgt, "loss_head")

    zeros_mem = jnp.zeros_like(memx)
    per_dest2 = [(0, True), (1, True)]

    def ffn_bwd(i, dh, h_in, gu, hn_f, act_gc, g_gu, g_dn, deps=()):
        act, gc = act_gc
        dgu, d_cw, d_cb = _glu_bwd(gu, gc, ffn_cw[i], dh[1], g_dn, f"glu_bwd_{i}", deps=deps)
        d_wdown = _matmul_tn(act, pl.BlockSpec((None, tm, GU_SHARD), lambda j, r: (j, r, 0)),
                             dh[1], pl.BlockSpec((tm, D), lambda j, r: (r, 0)), s, FF_BLOCKS, (GU_SHARD, D),
                             (N_DEV, DN_SHARD, D), pl.BlockSpec((2, DN_SHARD, D), lambda j, r: (j, 0, 0)), f"d_w_down_{i}")
        *dh_new, d_g = _matmul_nt_normbwd(dgu, _spec_gu_act(0, 1, tm), g_gu, _spec_gate_up(1), N_DEV, h_in,
                                          norm_ffn_g[i:i + 1], dh[0], f"d_ffn_in_{i}", w_t=True, act_copy=True)
        d_wgu = _matmul_tn(dgu, _spec_gu_act(1, 0, tb), hn_f, pl.BlockSpec((tb, D), lambda j, r: (r, 0)), s, N_DEV,
                           (GU_SHARD, D), (N_DEV, GU_SHARD, D), pl.BlockSpec((None, GU_SHARD, D), lambda j, r: (j, 0, 0)),
                           f"d_w_gate_up_{i}", tm=tb)
        return dh_new, [d_wdown, d_wgu], d_cw, d_cb, d_g

    def out_bwd(i, dh, mix, deps):
        dmix = _matmul_nt(dh[1], g_out[i], _spec_rowsharded(0, D // N_DEV, D), 1, (s, D), row_x, f"d_mix_{i}", deps=deps, out_dtype=_ACT)
        d_wout = _matmul_tn(mix, pl.BlockSpec((tm, D), lambda j, r: (r, 0)), dh[1], pl.BlockSpec((tm, D), lambda j, r: (r, 0)),
                            s, 1, (D, D), (N_DEV, D // N_DEV, D), pl.BlockSpec((N_DEV, D // N_DEV, D), lambda j, r: (0, 0, 0)),
                            f"d_w_out_{i}")
        return dmix, d_wout

    def mem_bwd(i, dmemkv, memn):
        tmm = _rows(MEM_LEN)
        *_, d_g = _matmul_nt_normbwd(dmemkv, pl.BlockSpec((tmm, 2 * X_Q), lambda r, j: (r, 0)), g_mk[i],
                                     _spec_rowsharded(0, D // N_DEV, 2 * X_Q), 1, memx, norm_mem_g[i:i + 1], zeros_mem,
                                     f"d_mem_in_{i}")
        by_row = lambda j, r: (r, 0)
        d_w = _matmul_tn(memn, pl.BlockSpec((tmm, D), by_row), dmemkv, pl.BlockSpec((tmm, 2 * X_Q), by_row), MEM_LEN, 1,
                         (D, 2 * X_Q), (N_DEV, D // N_DEV, 2 * X_Q),
                         pl.BlockSpec((N_DEV, D // N_DEV, 2 * X_Q), lambda j, r: (0, 0, 0)), f"d_w_mem_kv_{i}")
        return d_w, d_g

    out_land = ((N_DEV, D // N_DEV, D), _WIRE)
    mk_land = ((N_DEV, D // N_DEV, 2 * X_Q), _WIRE)
    ffn_lands = [((N_DEV, DN_SHARD, D), _WIRE), ((N_DEV, GU_SHARD, D), _WIRE)]
    dh, d_ffn1, d_cw1, d_cb1, d_gf1 = ffn_bwd(1, dh, h3, gu1, hn_f1, act1, g_gu1, g_dn1)
    ffn1_g = _seq_exchange(d_ffn1, ffn_lands, per_dest2, "send_ffn1_grads", 5)
    dmix, d_wout1 = out_bwd(1, dh, mix_b, ffn1_g.srcs)
    dproj_b, d_convw, d_par, d_ng, dmemkv1 = _mix_b_bwd(proj_b, conv_qkv, par_b, out_norm_g_b, memkv1, states, deltas, dmix, "mix_b_bwd")
    *dh, d_gm1 = _matmul_nt_normbwd(dproj_b, pl.BlockSpec((tm, 896), lambda i, j: (i, j)), g_ib,
                                    _spec_rowsharded(0, D // N_DEV, 896, col_block=1), IN_BP // 896, h2, norm_mix_g[1:2], dh[0],
                                    "d_in_b", act_copy=True)
    d_wib = _matmul_tn(hn_b, pl.BlockSpec((tb, D), lambda j, r: (r, 0)), dproj_b, pl.BlockSpec((tb, 896), lambda j, r: (r, j)),
                       s, IN_BP // 896, (D, 896), (N_DEV, D // N_DEV, IN_BP),
                       pl.BlockSpec((N_DEV, D // N_DEV, 896), lambda j, r: (0, 0, j)), "d_w_in_b", tm=tb)
    d_wmk1, d_gmem1 = mem_bwd(1, dmemkv1, memn1)
    mix1_g = _seq_exchange([d_wout1, d_wib, d_wmk1], [out_land, ((N_DEV, D // N_DEV, IN_BP), _WIRE), mk_land],
                           [(0, True), (1, True), (2, True)], "send_mix1_grads", 6)
    dh, d_ffn0, d_cw0, d_cb0, d_gf0 = ffn_bwd(0, dh, h1, gu0, hn_f0, act0, g_gu0, g_dn0, deps=mix1_g.srcs)
    dmix, d_wout0 = out_bwd(0, dh, mix_a, d_ffn0 + ffn1_g.lands[:1])
    ffn0_g = _seq_exchange(d_ffn0 + [d_wout0], ffn_lands + [out_land], per_dest2 + [(2, True)], "send_ffn0_grads", 4)
    dproj_a, dbias, dsinks, dmemkv0 = _mix_a_bwd(proj_a, bias, sinks, memkv0, dmix, "mix_a_bwd", deps=ffn0_g.srcs)
    dx, _, d_gm0 = _matmul_nt_normbwd(dproj_a, pl.BlockSpec((tm, 640), lambda i, j: (i, j)), w_ia,
                                      pl.BlockSpec((640, D), lambda i, j: (j, 0)), IN_A // 640, h0, norm_mix_g[0:1], dh[0],
                                      "d_in_a", w_t=True)
    d_wia = _matmul_tn(dproj_a, pl.BlockSpec((tm, IN_A), lambda j, r: (r, 0)), hn_a, pl.BlockSpec((tm, D), lambda j, r: (r, 0)),
                       s, 1, (IN_A, D), (N_DEV, IA_SHARD, D), pl.BlockSpec((N_DEV, IA_SHARD, D), lambda j, r: (0, 0, 0)),
                       "d_w_in_a")
    d_wmk0, d_gmem0 = mem_bwd(0, dmemkv0, memn0)
    d_rel = _bias_reduce(dbias, bucket, "bias_reduce")
    small = _pack_small(d_rel, (d_cb0, d_cb1), (d_cw0, d_cw1), d_convw, (d_gm0, d_gm1), (d_gmem0, d_gmem1),
                        (d_gf0, d_gf1), d_final_g, dsinks, d_par, d_ng, loss_row, "pack_small")
    mix0_g = _seq_exchange([d_wia, d_wmk0, small],
                           [((N_DEV, IA_SHARD, D), _WIRE), mk_land, ((N_DEV, SMALL_ROWS, D_FF), F32)],
                           [(0, True), (1, True), (2, False)], "send_mix0_grads", 7)

    res = {}
    last = []

    def update(nm, parts, tr, restore=False, transposed=False):
        view = t_ if transposed else (lambda a: a)
        out = _adamw(parts, view(wts[nm]), view(moms[nm]), view(vars_[nm]), tr, "adamw_" + nm, restore_b=restore, deps=last[-1:])
        res[nm] = [view(o) for o in out]
        last.append(out[1])

    r_dn1, r_gu1 = ffn1_g.lands
    r_dn0, r_gu0, r_out0 = ffn0_g.lands
    r_out1, r_ib, r_mk1 = mix1_g.lands
    update("w_in_b", [r_ib], 32, True)
    update("w_gate_up", [r_gu0, r_gu1], 176, transposed=True)
    update("w_down", [r_dn0, r_dn1], 176)
    r_ia, r_mk0, r_small = mix0_g.lands
    update("w_mem_kv", [r_mk0, r_mk1], 128)
    update("w_out", [r_out0, r_out1], 128)
    update("w_in_a", [r_ia], IA_SHARD, transposed=True)

    my = 4 * lax.axis_index("x") + 2 * lax.axis_index("y") + lax.axis_index("c")
    cq = conv_qkv_b.shape[-1]
    cf = ffn_conv_w.shape[-1]
    rc_qkv = lax.dynamic_slice_in_dim(r_small[:, SP_QKV:SP_QKV + B_CONV, :B_QKV], my * cq, cq, axis=2)[:, None]
    rc_ffn = lax.dynamic_slice_in_dim(r_small[:, SP_CW:SP_CW + 2 * FFN_CONV, :], my * cf, cf, axis=2).reshape(N_DEV, 2, FFN_CONV, cf)
    as2d = lambda a: a[None, :] if a.ndim == 1 else a
    small_out = _adamw_small(r_small, rc_qkv, rc_ffn, [as2d(wts[n]) for n in _SMALL], [as2d(moms[n]) for n in _SMALL],
                             [as2d(vars_[n]) for n in _SMALL], "adamw_small", deps=last[-1:])
    ns = len(_SMALL)
    for i, nm in enumerate(_SMALL):
        res[nm] = [small_out[k * ns + i].reshape(wts[nm].shape) for k in range(4)]

    return (small_out[-1][0, 0], dx[None], *[res[n][0] for n in order], *[res[n][1] for n in order],
            *[res[n][2] for n in order], *[res[n][3] for n in order])
```

```python
import functools
import math

import numpy as np

import jax
import jax.numpy as jnp
from jax import lax
from jax.experimental import pallas as pl
from jax.experimental.pallas import tpu as pltpu
from jax.experimental.pallas import tpu_sc as plsc

F32 = jnp.float32
_MXU = jnp.bfloat16
_ACT = jnp.bfloat16
_WIRE = jnp.bfloat16
_HI = lax.Precision.HIGH
_TM = 1024
_TM_GLU = 1024
_TM_BIG = 2048
_VMEM_LIMIT = 48 * 1024 * 1024
_SDS = jax.ShapeDtypeStruct

D = 1024
EPS = 1e-6
A_HEADS, A_KV_HEADS, A_HD, BLK = 12, 2, 64, 128
N_BUCKETS, MAX_DISTANCE = 32, 128
B_QK_HEADS, B_V_HEADS, B_HD, B_CONV, CHUNK = 3, 6, 128, 4, 64
X_HEADS, X_HD, MEM_LEN = 4, 64, 256
D_FF, FFN_CONV = 2816, 3
A_Q, A_KV, X_Q = 768, 128, 256
B_QK, B_V, B_QKV = 384, 768, 1536
IN_A, IN_B = 1280, 2572
IN_BP = 2688
BP_Z, BP_XQ, BP_GATE = 1536, 2304, 2560
HALO = 8
GLU_HALO = 16

N_DEV = 8
GU_SHARD = 2 * D_FF // N_DEV
FF_BLOCKS = D_FF // GU_SHARD
DN_SHARD = D_FF // N_DEV
IA_SHARD = IN_A // N_DEV

ADAM_LR, ADAM_B1, ADAM_B2, ADAM_EPS, ADAM_WD, ADAM_STEP = 0.001, 0.9, 0.999, 1e-08, 0.01, 10

SP_CB, SP_CW, SP_QKV, SP_MIX, SP_MEM, SP_FFN, SP_FINAL, SP_MISC, SMALL_ROWS = 0, 2, 8, 12, 14, 16, 18, 19, 24
SP_REL_LANE = B_QKV


def _cp(*sems):
    return pltpu.CompilerParams(dimension_semantics=sems, vmem_limit_bytes=_VMEM_LIMIT)


def _mm(a, b):
    return jnp.dot(a.astype(_MXU), b.astype(_MXU), preferred_element_type=F32)


def _mm_nt(a, b):
    return lax.dot_general(a.astype(_MXU), b.astype(_MXU), (((1,), (1,)), ((), ())), preferred_element_type=F32)


def _mm_tn(a, b):
    return lax.dot_general(a.astype(_MXU), b.astype(_MXU), (((0,), (0,)), ((), ())), preferred_element_type=F32)


def _mmf(a, b):
    return jnp.dot(a, b, preferred_element_type=F32, precision=_HI)


def _mmf_nt(a, b):
    return lax.dot_general(a, b, (((1,), (1,)), ((), ())), preferred_element_type=F32, precision=_HI)


def _silu(x):
    return x * jax.nn.sigmoid(x)


def _w2d(ref):
    v = ref[...]
    return v.reshape(-1, v.shape[-1])


def _rows(m):
    return min(m, _TM)


def _spec_rowsharded(layer, rows, cols, col_block=None):
    if col_block is None:
        return pl.BlockSpec((N_DEV, None, rows, cols), lambda *_: (0, layer, 0, 0))
    return pl.BlockSpec((N_DEV, None, rows, cols), lambda *ids: (0, layer, 0, ids[col_block]))


def _spec_gate_up(axis):
    return pl.BlockSpec((None, GU_SHARD, D), lambda *ids: (ids[axis], 0, 0))


def _spec_down(axis):
    return pl.BlockSpec((2, DN_SHARD, D), lambda *ids: (ids[axis], 0, 0))


def _dep_specs(deps):
    return [pl.BlockSpec(memory_space=pl.ANY) for d in deps]


def _spec_gu_act(row_axis, axis, tm):
    return pl.BlockSpec((None, None, tm, GU_SHARD), lambda *ids: (ids[axis] // FF_BLOCKS, ids[axis] % FF_BLOCKS, ids[row_axis], 0))


def _norm_matmul(x, g, w, w_spec, n_blocks, out_shape, out_spec, name, deps=(), out_dtype=F32, w_t=False, tm=None):
    m, k = x.shape
    tm = tm or _rows(m)

    def body(x_ref, g_ref, w_ref, *rest):
        y_ref, hn_ref = rest[-2:]

        @pl.when(pl.program_id(1) == 0)
        def _():
            xv = x_ref[...]
            r = lax.rsqrt(jnp.mean(xv * xv, axis=-1, keepdims=True) + EPS)
            hn_ref[...] = (xv * r * g_ref[...]).astype(hn_ref.dtype)

        y_ref[...] = (_mm_nt if w_t else _mm)(hn_ref[...], _w2d(w_ref)).astype(y_ref.dtype)

    return pl.pallas_call(
        body, grid=(m // tm, n_blocks),
        in_specs=[pl.BlockSpec((tm, k), lambda i, j: (i, 0)), pl.BlockSpec((1, k), lambda i, j: (0, 0)), w_spec]
        + _dep_specs(deps),
        out_specs=[out_spec, pl.BlockSpec((tm, k), lambda i, j: (i, 0))],
        out_shape=[_SDS(out_shape, out_dtype), _SDS((m, k), _ACT)],
        name=name, compiler_params=_cp("arbitrary", "arbitrary"))(x, g, w, *deps)


def _matmul_res(a, a_spec, w, w_spec, n_k, res, name):
    m, n = res.shape
    tm = _rows(m)

    def body(a_ref, w_ref, r_ref, o_ref):
        part = _mm(a_ref[...], _w2d(w_ref))

        @pl.when(pl.program_id(1) == 0)
        def _():
            o_ref[...] = r_ref[...] + part

        @pl.when(pl.program_id(1) > 0)
        def _():
            o_ref[...] += part

    return pl.pallas_call(
        body, grid=(m // tm, n_k),
        in_specs=[a_spec, w_spec, pl.BlockSpec((tm, n), lambda i, j: (i, 0))],
        out_specs=pl.BlockSpec((tm, n), lambda i, j: (i, 0)),
        out_shape=_SDS((m, n), F32), name=name, compiler_params=_cp("arbitrary", "arbitrary"))(a, w, res)


def _matmul_nt(dy, w, w_spec, n_blocks, out_shape, out_spec, name, deps=(), out_dtype=F32):
    m, n = dy.shape
    tm = _rows(m)

    def body(dy_ref, w_ref, *rest):
        o_ref = rest[-1]
        o_ref[...] = _mm_nt(dy_ref[...], _w2d(w_ref)).astype(o_ref.dtype)

    return pl.pallas_call(
        body, grid=(m // tm, n_blocks),
        in_specs=[pl.BlockSpec((tm, n), lambda i, j: (i, 0)), w_spec] + _dep_specs(deps),
        out_specs=out_spec, out_shape=_SDS(out_shape, out_dtype),
        name=name, compiler_params=_cp("arbitrary", "arbitrary"))(dy, w, *deps)


def _matmul_nt_normbwd(dy, dy_spec, w, w_spec, nj, h, g, dh_in, name, w_t=False, act_copy=False):
    m, k = h.shape
    tm = _rows(m)

    def body(dy_ref, w_ref, h_ref, g_ref, dhin_ref, dh_ref, *rest):
        dg_ref, acc_ref = rest[-2:]
        i, j = pl.program_id(0), pl.program_id(1)

        @pl.when(j == 0)
        def _():
            acc_ref[...] = jnp.zeros_like(acc_ref)

        acc_ref[...] += (_mm if w_t else _mm_nt)(dy_ref[...], _w2d(w_ref))

        @pl.when(j == nj - 1)
        def _():
            xv = h_ref[...]
            r = lax.rsqrt(jnp.mean(xv * xv, axis=-1, keepdims=True) + EPS)
            xh = xv * r
            dhn = acc_ref[...]
            part = jnp.sum(dhn * xh, axis=0, keepdims=True)

            @pl.when(i == 0)
            def _():
                dg_ref[...] = part

            @pl.when(i > 0)
            def _():
                dg_ref[...] += part

            t = dhn * g_ref[...]
            dh = dhin_ref[...] + r * (t - xh * jnp.mean(t * xh, axis=-1, keepdims=True))
            dh_ref[...] = dh
            if act_copy:
                rest[0][...] = dh.astype(_ACT)

    rows = pl.BlockSpec((tm, k), lambda i, j: (i, 0))
    outs = pl.pallas_call(
        body, grid=(m // tm, nj),
        in_specs=[dy_spec, w_spec, rows, pl.BlockSpec((1, k), lambda i, j: (0, 0)), rows],
        out_specs=[rows] + [rows] * act_copy + [pl.BlockSpec((1, k), lambda i, j: (0, 0))],
        out_shape=[_SDS((m, k), F32)] + [_SDS((m, k), _ACT)] * act_copy + [_SDS((1, k), F32)],
        scratch_shapes=[pltpu.VMEM((tm, k), F32)],
        name=name, compiler_params=_cp("arbitrary", "arbitrary"))(dy, w, h, g, dh_in)
    return outs[0], (outs[1] if act_copy else None), outs[-1]


def _matmul_tn(x, x_spec, dy, dy_spec, m, n_blocks, acc_shape, out_shape, out_spec, name, tm=None):
    tm = tm or _rows(m)
    nm = m // tm

    def body(x_ref, dy_ref, o_ref, acc_ref):
        @pl.when(pl.program_id(1) == 0)
        def _():
            acc_ref[...] = jnp.zeros_like(acc_ref)

        acc_ref[...] += _mm_tn(x_ref[...], dy_ref[...])

        @pl.when(pl.program_id(1) == nm - 1)
        def _():
            o_ref[...] = acc_ref[...].reshape(o_ref.shape).astype(o_ref.dtype)

    return pl.pallas_call(
        body, grid=(n_blocks, nm), in_specs=[x_spec, dy_spec], out_specs=out_spec,
        out_shape=_SDS(out_shape, _WIRE), scratch_shapes=[pltpu.VMEM(acc_shape, F32)],
        name=name, compiler_params=_cp("arbitrary", "arbitrary"))(x, dy)


def _loss_head(h, g, tgt, name):
    m, k = h.shape
    tm = _rows(m)

    def body(h_ref, g_ref, t_ref, loss_ref, dh_ref, dha_ref, dg_ref):
        i = pl.program_id(0)
        xv = h_ref[...]
        r = lax.rsqrt(jnp.mean(xv * xv, axis=-1, keepdims=True) + EPS)
        xh = xv * r
        gv = g_ref[...]
        err = xh * gv - t_ref[...]
        lpart = jnp.zeros((1, 128), F32) + 0.5 * jnp.sum(jnp.mean(err * err, axis=-1, keepdims=True), axis=0, keepdims=True)
        dy = err * (1.0 / k)
        gpart = jnp.sum(dy * xh, axis=0, keepdims=True)

        @pl.when(i == 0)
        def _():
            loss_ref[...] = lpart
            dg_ref[...] = gpart

        @pl.when(i > 0)
        def _():
            loss_ref[...] += lpart
            dg_ref[...] += gpart

        t = dy * gv
        dh = r * (t - xh * jnp.mean(t * xh, axis=-1, keepdims=True))
        dh_ref[...] = dh
        dha_ref[...] = dh.astype(_ACT)

    rows = pl.BlockSpec((tm, k), lambda i: (i, 0))
    return pl.pallas_call(
        body, grid=(m // tm,),
        in_specs=[rows, pl.BlockSpec((1, k), lambda i: (0, 0)), rows],
        out_specs=[pl.BlockSpec((1, 128), lambda i: (0, 0)), rows, rows, pl.BlockSpec((1, k), lambda i: (0, 0))],
        out_shape=[_SDS((1, 128), F32), _SDS((m, k), F32), _SDS((m, k), _ACT), _SDS((1, k), F32)],
        name=name, compiler_params=_cp("arbitrary"))(h, g, tgt)


def _glu_down(gu, conv_w, conv_b, w_down, res, name):
    s = gu.shape[2]
    tm = min(s, _TM_GLU)

    def body(gu_ref, prev_ref, w_ref, b_ref, wdn_ref, r_ref, o_ref, act_ref, gc_ref):
        i, j = pl.program_id(0), pl.program_id(1)
        prev = jnp.where(i > 0, prev_ref[...].astype(F32), 0.0)
        ext = jnp.concatenate([prev, gu_ref[0].astype(F32)], axis=0)
        gc = b_ref[...] + w_ref[FFN_CONV - 1:FFN_CONV, :] * ext
        for k in range(FFN_CONV - 1):
            gc = gc + w_ref[k:k + 1, :] * pltpu.roll(ext, FFN_CONV - 1 - k, 0)
        gc = gc[GLU_HALO:]
        gc_ref[...] = gc.astype(gc_ref.dtype)
        act =(_silu(gc) * gu_ref[1].astype(F32)).astype(act_ref.dtype)
        act_ref[...] = act
        part = _mm(act, _w2d(wdn_ref))

        @pl.when(j == 0)
        def _():
            o_ref[...] = r_ref[...] + part

        @pl.when(j > 0)
        def _():
            o_ref[...] += part

    return pl.pallas_call(
        body, grid=(s // tm, FF_BLOCKS),
        in_specs=[pl.BlockSpec((2, None, tm, GU_SHARD), lambda i, j: (0, j, i, 0)),
                  pl.BlockSpec((None, None, GLU_HALO, GU_SHARD),
                               lambda i, j: (0, j, jnp.maximum(i * (tm // GLU_HALO) - 1, 0), 0)),
                  pl.BlockSpec((None, HALO, GU_SHARD), lambda i, j: (j, 0, 0)),
                  pl.BlockSpec((None, 1, GU_SHARD), lambda i, j: (j, 0, 0)),
                  _spec_down(1), pl.BlockSpec((tm, D), lambda i, j: (i, 0))],
        out_specs=[pl.BlockSpec((tm, D), lambda i, j: (i, 0)), pl.BlockSpec((None, tm, GU_SHARD), lambda i, j: (j, i, 0)),
                   pl.BlockSpec((None, tm, GU_SHARD), lambda i, j: (j, i, 0))],
        out_shape=[_SDS((s, D), F32), _SDS((FF_BLOCKS, s, GU_SHARD), _ACT), _SDS((FF_BLOCKS, s, GU_SHARD), _ACT)], name=name,
        compiler_params=_cp("arbitrary", "arbitrary"))(gu, gu, conv_w, conv_b, w_down, res)


def _glu_bwd(gu, gc, conv_w, dh, w_down, name, deps=()):
    s = gu.shape[2]
    tm = min(s, _TM_GLU)
    nt = s // tm
    ext_rows = tm + GLU_HALO

    def body(gu_ref, prev_ref, gc_ref, w_ref, dh_ref, wdn_ref, *rest):
        dgu_ref, dw_ref, db_ref, carry_ref = rest[-4:]
        t = pl.program_id(1)
        i = nt - 1 - t

        @pl.when(t == 0)
        def _():
            carry_ref[...] = jnp.zeros_like(carry_ref)
            dw_ref[...] = jnp.zeros_like(dw_ref)
            db_ref[...] = jnp.zeros_like(db_ref)

        up = gu_ref[1].astype(F32)
        prev = jnp.where(i > 0, prev_ref[...].astype(F32), 0.0)
        ext = jnp.concatenate([prev, gu_ref[0].astype(F32)], axis=0)
        gc = gc_ref[...].astype(F32)
        sg = jax.nn.sigmoid(gc)
        da = _mm_nt(dh_ref[...], _w2d(wdn_ref))
        dup = da * (gc * sg)
        dgc = da * up * (sg * (1.0 + gc * (1.0 - sg)))
        db_ref[...] += jnp.sum(dgc, axis=0, keepdims=True)
        dgc_ext = jnp.concatenate([jnp.zeros((GLU_HALO, GU_SHARD), F32), dgc], axis=0)
        ahead = [pltpu.roll(dgc_ext, ext_rows - (FFN_CONV - 1 - j), 0) if j < FFN_CONV - 1 else dgc_ext
                 for j in range(FFN_CONV)]
        dext = ahead[0] * w_ref[0:1, :]
        for j in range(FFN_CONV):
            dw_ref[j:j + 1, :] += jnp.sum(ext * ahead[j], axis=0, keepdims=True)
            if j > 0:
                dext = dext + ahead[j] * w_ref[j:j + 1, :]
        tail = jnp.concatenate([jnp.zeros((tm - GLU_HALO, GU_SHARD), F32), carry_ref[...]], axis=0)
        dgate = dext[GLU_HALO:] + tail
        carry_ref[...] = dext[:GLU_HALO]
        dgu_ref[0] = dgate.astype(dgu_ref.dtype)
        dgu_ref[1] = dup.astype(dgu_ref.dtype)

    return pl.pallas_call(
        body, grid=(FF_BLOCKS, nt),
        in_specs=[pl.BlockSpec((2, None, tm, GU_SHARD), lambda j, t: (0, j, nt - 1 - t, 0)),
                  pl.BlockSpec((None, None, GLU_HALO, GU_SHARD),
                               lambda j, t: (0, j, jnp.maximum((nt - 1 - t) * (tm // GLU_HALO) - 1, 0), 0)),
                  pl.BlockSpec((None, tm, GU_SHARD), lambda j, t: (j, nt - 1 - t, 0)),
                  pl.BlockSpec((None, HALO, GU_SHARD), lambda j, t: (j, 0, 0)),
                  pl.BlockSpec((tm, D), lambda j, t: (nt - 1 - t, 0)), _spec_down(0)] + _dep_specs(deps),
        out_specs=[pl.BlockSpec((2, None, tm, GU_SHARD), lambda j, t: (0, j, nt - 1 - t, 0)),
                   pl.BlockSpec((None, HALO, GU_SHARD), lambda j, t: (j, 0, 0)),
                   pl.BlockSpec((None, 1, GU_SHARD), lambda j, t: (j, 0, 0))],
        out_shape=[_SDS(gu.shape, _ACT), _SDS((FF_BLOCKS, HALO, GU_SHARD), F32), _SDS((FF_BLOCKS, 1, GU_SHARD), F32)],
        scratch_shapes=[pltpu.VMEM((GLU_HALO, GU_SHARD), F32)],
        name=name, compiler_params=_cp("arbitrary", "arbitrary"))(gu, gu, gc, conv_w, dh, w_down, *deps)


def _bucket_table():
    qi = np.arange(BLK)[:, None]
    kj = np.arange(BLK)[None, :]
    n = np.where(kj > qi, BLK + qi - kj, qi - kj)
    max_exact = N_BUCKETS // 2
    nf = np.maximum(n, 1).astype(np.float32)
    large = max_exact + (np.log(nf / max_exact) / math.log(MAX_DISTANCE / max_exact)
                         * (N_BUCKETS - max_exact)).astype(np.int32)
    large = np.minimum(large, N_BUCKETS - 1)
    return np.where(n < max_exact, n, large).astype(np.int32)


def _lane_low():
    return lax.broadcasted_iota(jnp.int32, (1, 128), 1) < A_HD


def _swa_groups(q, kd, vd, sink, bias, upper, first):
    n = A_HEADS // A_KV_HEADS
    ng = A_KV_HEADS
    low = _lane_low()
    qm = [jnp.concatenate([jnp.where(low == (h % 2 == 0), q[g][:, (h // 2) * 128:(h // 2 + 1) * 128], 0.0) for h in range(n)], axis=0)
          for g in range(ng)]
    s2 = [_mm_nt(qm[g], kd[g]) * (A_HD ** -0.5) for g in range(ng)]
    s = [jnp.where(upper[None], s2[g][:, :BLK].reshape(n, BLK, BLK), s2[g][:, BLK:].reshape(n, BLK, BLK)) + bias[g] for g in range(ng)]
    s = [jnp.where((upper & first)[None], -jnp.inf, t) for t in s]
    m = [lax.stop_gradient(jnp.maximum(jnp.max(s[g], axis=-1, keepdims=True), sink[g])) for g in range(ng)]
    p = [jnp.exp(s[g] - m[g]) for g in range(ng)]
    split = [jnp.concatenate([jnp.where(upper[None], t, 0.0), jnp.where(upper[None], 0.0, t)], axis=-1).reshape(n * BLK, 2 * BLK)
             for t in p]
    ones = jnp.ones((BLK, 128), F32)
    den = [_mm(p[g].reshape(n * BLK, BLK), ones) + jnp.exp(sink[g] - m[g]).reshape(n * BLK, 1) for g in range(ng)]
    o = [_mm(split[g], vd[g]) / den[g] for g in range(ng)]
    return [jnp.concatenate([jnp.where(low, t[2 * k * BLK:(2 * k + 1) * BLK], t[(2 * k + 1) * BLK:(2 * k + 2) * BLK])
                             for k in range(n // 2)], axis=1) for t in o]


def _mix_a_core(q, kd, vd, sink, bias, xq, mk, mv, upper, first):
    return _swa_groups(q, kd, vd, sink, bias, upper, first), _cross_pairs(xq, mk, mv)


def _swa_sinks(sink_ref, g):
    n = A_HEADS // A_KV_HEADS
    return jnp.concatenate([sink_ref[:, h:h + 1] for h in range(g * n, (g + 1) * n)], axis=0).reshape(n, 1, 1)


def _both_halves(t, t_rolled, g):
    low = _lane_low()
    return jnp.where(low, t, t_rolled) if g == 0 else jnp.where(low, t_rolled, t)


def _cross_pairs(q, mk, mv):
    rows = q.shape[0]
    low = _lane_low()
    qm = [jnp.concatenate([jnp.where(low, q[:, p * 128:(p + 1) * 128], 0.0), jnp.where(low, 0.0, q[:, p * 128:(p + 1) * 128])], axis=0)
          for p in range(X_HEADS // 2)]
    s = [_mm_nt(qm[p], mk[:, p * 128:(p + 1) * 128]) * (X_HD ** -0.5) for p in range(X_HEADS // 2)]
    e = [jnp.exp(t - lax.stop_gradient(jnp.max(t, axis=-1, keepdims=True))) for t in s]
    pr = [t / jnp.sum(t, axis=-1, keepdims=True) for t in e]
    o = [_mm(pr[p], mv[:, p * 128:(p + 1) * 128]) for p in range(X_HEADS // 2)]
    return jnp.concatenate([jnp.where(low, t[:rows], t[rows:]) for t in o], axis=1)


def _swa_upper():
    qi = lax.broadcasted_iota(jnp.int32, (BLK, BLK), 0)
    kj = lax.broadcasted_iota(jnp.int32, (BLK, BLK), 1)
    return kj > qi


def _bias_build(rel_bias, bucket, name):
    def body(rb_ref, bucket_ref, o_ref):
        b = bucket_ref[...]
        for h in range(A_HEADS):
            acc = jnp.zeros((BLK, BLK), F32)
            for k in range(N_BUCKETS):
                acc = jnp.where(b == k, rb_ref[k, h], acc)
            o_ref[h] = acc

    return pl.pallas_call(
        body, in_specs=[pl.BlockSpec(memory_space=pltpu.SMEM), pl.BlockSpec(memory_space=pltpu.VMEM)],
        out_specs=pl.BlockSpec(memory_space=pltpu.VMEM),
        out_shape=_SDS((A_HEADS, BLK, BLK), F32), name=name)(rel_bias, bucket)


def _bias_reduce(dbias, bucket, name):
    def body(db_ref, bucket_ref, o_ref):
        b = bucket_ref[...]
        row = lax.broadcasted_iota(jnp.int32, (N_BUCKETS, 128), 0)
        lane = lax.broadcasted_iota(jnp.int32, (N_BUCKETS, 128), 1)
        acc = jnp.zeros((N_BUCKETS, 128), F32)
        for h in range(A_HEADS):
            v = db_ref[h]
            for k in range(N_BUCKETS):
                sk = jnp.sum(jnp.sum(jnp.where(b == k, v, 0.0), axis=1, keepdims=True), axis=0, keepdims=True)
                acc = acc + jnp.where((row == k) & (lane == h), sk, 0.0)
        o_ref[...] = acc

    return pl.pallas_call(
        body, in_specs=[pl.BlockSpec(memory_space=pltpu.VMEM)] * 2,
        out_specs=pl.BlockSpec(memory_space=pltpu.VMEM),
        out_shape=_SDS((N_BUCKETS, 128), F32), name=name)(dbias, bucket)


def _mix_a_fwd(proj, bias, sinks, memkv, name):
    s = proj.shape[0]
    nb = s // BLK
    grp = A_HEADS // A_KV_HEADS

    def body(proj_ref, prev_ref, bias_ref, sink_ref, memkv_ref, o_ref):
        i = pl.program_id(0)
        upper = _swa_upper()
        prev = prev_ref[...].astype(F32)
        proj = proj_ref[...].astype(F32)
        kb = jnp.concatenate([prev[:, :A_KV], proj[:, A_Q:A_Q + A_KV]], axis=0)
        vb = jnp.concatenate([prev[:, A_KV:], proj[:, A_Q + A_KV:A_Q + 2 * A_KV]], axis=0)
        kb_r = pltpu.roll(kb, A_HD, 1)
        vb_r = pltpu.roll(vb, A_HD, 1)
        gw = A_Q // A_KV_HEADS
        groups = range(A_KV_HEADS)
        swa, cross = _mix_a_core([proj[:, g * gw:(g + 1) * gw] for g in groups], [_both_halves(kb, kb_r, g) for g in groups],
                                 [_both_halves(vb, vb_r, g) for g in groups], [_swa_sinks(sink_ref, g) for g in groups],
                                 [bias_ref[g * grp:(g + 1) * grp] for g in groups], proj[:, A_Q + 2 * A_KV:],
                                 memkv_ref[:, :X_Q], memkv_ref[:, X_Q:], upper, i == 0)
        o_ref[...] = jnp.concatenate(swa + [cross], axis=1).astype(o_ref.dtype)

    return pl.pallas_call(
        body, grid=(nb,),
        in_specs=[pl.BlockSpec((BLK, IN_A), lambda i: (i, 0)),
                  pl.BlockSpec((BLK, 2 * A_KV), lambda i: (jnp.maximum(i - 1, 0), A_Q // (2 * A_KV))),
                  pl.BlockSpec((A_HEADS, BLK, BLK), lambda i: (0, 0, 0)),
                  pl.BlockSpec((1, 128), lambda i: (0, 0)),
                  pl.BlockSpec((MEM_LEN, 2 * X_Q), lambda i: (0, 0))],
        out_specs=pl.BlockSpec((BLK, D), lambda i: (i, 0)),
        out_shape=_SDS((s, D), _ACT), name=name, compiler_params=_cp("arbitrary"))(proj, proj, bias, sinks, memkv)


def _mix_a_bwd(proj, bias, sinks, memkv, dmix, name, deps=()):
    s = proj.shape[0]
    nb = s // BLK
    grp = A_HEADS // A_KV_HEADS

    def body(proj_ref, prev_ref, bias_ref, sink_ref, memkv_ref, dmix_ref, *rest):
        dproj_ref, dbias_ref, dsink_ref, dmemkv_ref, carry_ref = rest[-5:]
        t = pl.program_id(0)
        i = nb - 1 - t

        @pl.when(t == 0)
        def _():
            carry_ref[...] = jnp.zeros_like(carry_ref)
            dbias_ref[...] = jnp.zeros_like(dbias_ref)
            dsink_ref[...] = jnp.zeros_like(dsink_ref)
            dmemkv_ref[...] = jnp.zeros_like(dmemkv_ref)

        upper = _swa_upper()
        lane = lax.broadcasted_iota(jnp.int32, (1, 128), 1)
        low = _lane_low()
        prev = prev_ref[...].astype(F32)
        proj = proj_ref[...].astype(F32)
        kb = jnp.concatenate([prev[:, :A_KV], proj[:, A_Q:A_Q + A_KV]], axis=0)
        vb = jnp.concatenate([prev[:, A_KV:], proj[:, A_Q + A_KV:A_Q + 2 * A_KV]], axis=0)
        kb_r = pltpu.roll(kb, A_HD, 1)
        vb_r = pltpu.roll(vb, A_HD, 1)
        gw = A_Q // A_KV_HEADS
        groups = range(A_KV_HEADS)
        _, vjp = jax.vjp(
            functools.partial(_mix_a_core, upper=upper, first=i == 0),
            [proj[:, g * gw:(g + 1) * gw] for g in groups], [_both_halves(kb, kb_r, g) for g in groups],
            [_both_halves(vb, vb_r, g) for g in groups], [_swa_sinks(sink_ref, g) for g in groups],
            [bias_ref[g * grp:(g + 1) * grp] for g in groups], proj[:, A_Q + 2 * A_KV:], memkv_ref[:, :X_Q], memkv_ref[:, X_Q:])
        dqs, dk, dv, ds, db, dxq, dmk, dmv = vjp(
            ([dmix_ref[:, g * gw:(g + 1) * gw].astype(F32) for g in groups], dmix_ref[:, A_Q:].astype(F32)))
        dkd = [t + pltpu.roll(t, A_HD, 1) for t in dk]
        dvd = [t + pltpu.roll(t, A_HD, 1) for t in dv]
        dsink = jnp.zeros((1, 128), F32)
        for g in groups:
            for h in range(grp):
                dsink = dsink + jnp.where(lane == g * grp + h, ds[g][h], 0.0)
            dbias_ref[g * grp:(g + 1) * grp] += db[g]
        dsink_ref[...] += dsink
        dkb = jnp.where(low, dkd[0], dkd[1])
        dvb = jnp.where(low, dvd[0], dvd[1])
        dmemkv_ref[...] += jnp.concatenate([dmk, dmv], axis=1)
        dkv_cur = jnp.concatenate([dkb[BLK:], dvb[BLK:]], axis=1) + carry_ref[...]
        carry_ref[...] = jnp.concatenate([dkb[:BLK], dvb[:BLK]], axis=1)
        dproj_ref[...] = jnp.concatenate(list(dqs) + [dkv_cur, dxq], axis=1).astype(dproj_ref.dtype)

    return pl.pallas_call(
        body, grid=(nb,),
        in_specs=[pl.BlockSpec((BLK, IN_A), lambda t: (nb - 1 - t, 0)),
                  pl.BlockSpec((BLK, 2 * A_KV), lambda t: (jnp.maximum(nb - 2 - t, 0), A_Q // (2 * A_KV))),
                  pl.BlockSpec((A_HEADS, BLK, BLK), lambda t: (0, 0, 0)),
                  pl.BlockSpec((1, 128), lambda t: (0, 0)),
                  pl.BlockSpec((MEM_LEN, 2 * X_Q), lambda t: (0, 0)),
                  pl.BlockSpec((BLK, D), lambda t: (nb - 1 - t, 0))] + _dep_specs(deps),
        out_specs=[pl.BlockSpec((BLK, IN_A), lambda t: (nb - 1 - t, 0)),
                   pl.BlockSpec((A_HEADS, BLK, BLK), lambda t: (0, 0, 0)),
                   pl.BlockSpec((1, 128), lambda t: (0, 0)),
                   pl.BlockSpec((MEM_LEN, 2 * X_Q), lambda t: (0, 0))],
        out_shape=[_SDS((s, IN_A), _ACT), _SDS((A_HEADS, BLK, BLK), F32), _SDS((1, 128), F32),
                   _SDS((MEM_LEN, 2 * X_Q), F32)],
        scratch_shapes=[pltpu.VMEM((BLK, 2 * A_KV), F32)],
        name=name, compiler_params=_cp("arbitrary"))(proj, proj, bias, sinks, memkv, dmix, *deps)


def _neumann(pw, rhs):
    nh = len(pw)
    x = rhs
    for lvl in range(6):
        if lvl < 5:
            prod = [_mmf(pw[h], jnp.concatenate([x[h], pw[h]], axis=1)) for h in range(nh)]
            x = [x[h] + prod[h][:, :B_HD] for h in range(nh)]
            pw = [t[:, B_HD:] for t in prod]
        else:
            x = [x[h] + _mmf(pw[h], x[h]) for h in range(nh)]
    return x


@jax.custom_vjp
def _tri_solve(pw, rhs):
    return _neumann(pw, rhs)


def _tri_solve_fwd(pw, rhs):
    x = _neumann(pw, rhs)
    return x, (pw, x)


def _tri_solve_bwd(res, dx):
    pw, x = res
    d_rhs = _neumann([t.T for t in pw], list(dx))
    return [_mmf_nt(d_rhs[h], x[h]) for h in range(len(pw))], d_rhs


_tri_solve.defvjp(_tri_solve_fwd, _tri_solve_bwd)


@jax.custom_vjp
def _tri_solved(pw, rhs, x):
    return x


def _tri_solved_fwd(pw, rhs, x):
    return x, (pw, x)


def _tri_solved_bwd(res, dx):
    d_pw, d_rhs = _tri_solve_bwd(res, dx)
    return d_pw, d_rhs, [jnp.zeros_like(t) for t in res[1]]


_tri_solved.defvjp(_tri_solved_fwd, _tri_solved_bwd)


@jax.custom_vjp
def _known(x, value):
    return value


def _known_fwd(x, value):
    return value, None


def _known_bwd(_, g):
    return g, jnp.zeros_like(g)


_known.defvjp(_known_fwd, _known_bwd)


def _dn_heads(yq, yk, yv, z, bl, al, a_log, dtb, ng, s0, solved=None, out_known=None):
    c = CHUNK
    nh = B_V_HEADS
    rep = B_V_HEADS // B_QK_HEADS
    r = lax.broadcasted_iota(jnp.int32, (c, c), 0)
    cc = lax.broadcasted_iota(jnp.int32, (c, c), 1)
    q = [_silu(t) for t in yq]
    k = [_silu(t) for t in yk]
    v = [_silu(t) for t in yv]
    q = [t * lax.rsqrt(jnp.sum(t * t, axis=-1, keepdims=True) + EPS) * (B_HD ** -0.5) for t in q]
    k = [t * lax.rsqrt(jnp.sum(t * t, axis=-1, keepdims=True) + EPS) for t in k]
    beta = [jax.nn.sigmoid(t) for t in bl]
    g = [-jnp.exp(a_log[h]) * jax.nn.softplus(al[h] + dtb[h]) for h in range(nh)]
    gb = [jnp.broadcast_to(t, (c, c)) for t in g]
    gc_col = [jnp.sum(jnp.where(cc <= r, t.T, 0.0), axis=1, keepdims=True) for t in gb]
    gc_row = [jnp.sum(jnp.where(r <= cc, t, 0.0), axis=0, keepdims=True) for t in gb]
    gc_last = [jnp.sum(t, axis=0, keepdims=True) for t in g]
    decay = [jnp.exp(jnp.where(r >= cc, gc_col[h] - gc_row[h], -jnp.inf)) for h in range(nh)]
    kq = [_mmf_nt(jnp.concatenate([k[h], q[h]], axis=0), k[h]) for h in range(B_QK_HEADS)]
    kk = [t[:c] for t in kq]
    qk = [t[c:] for t in kq]
    egc = [jnp.exp(t) for t in gc_col]
    both = [_mmf(jnp.concatenate([(beta[h] * egc[h]) * k[h // rep], q[h // rep] * egc[h]], axis=0), s0[h]) for h in range(nh)]
    rhs = [beta[h] * v[h] - both[h][:c] for h in range(nh)]
    qs0 = [t[c:] for t in both]
    pw = [-(beta[h] * kk[h // rep] * jnp.where(r > cc, decay[h], 0.0)) for h in range(nh)]
    delta = _tri_solve(pw, rhs) if solved is None else _tri_solved(pw, rhs, solved)
    last = [_mmf(jnp.concatenate([qk[h // rep] * decay[h], (k[h // rep] * jnp.exp(gc_last[h] - gc_col[h])).T], axis=0), delta[h])
            for h in range(nh)]
    out = [qs0[h] + last[h][:c] for h in range(nh)]
    if out_known is not None:
        out = [_known(out[h], out_known[h]) for h in range(nh)]
    s1 = [jnp.exp(gc_last[h]) * s0[h] + last[h][c:] for h in range(nh)]
    o = [t * lax.rsqrt(jnp.mean(t * t, axis=-1, keepdims=True) + EPS) * ng for t in out]
    return [o[h] * _silu(z[h]) for h in range(nh)], s1, delta, out


def _dn_conv(ext, w_ref):
    y = ext * w_ref[B_CONV - 1:B_CONV, :]
    for j in range(B_CONV - 1):
        y = y + w_ref[j:j + 1, :] * pltpu.roll(ext, B_CONV - 1 - j, 0)
    return y


def _dn_args(y, cur_ref, par_ref, ng_ref):
    nh = B_V_HEADS
    return ([y[:, h * B_HD:(h + 1) * B_HD] for h in range(B_QK_HEADS)],
            [y[:, B_QK + h * B_HD:B_QK + (h + 1) * B_HD] for h in range(B_QK_HEADS)],
            [y[:, 2 * B_QK + h * B_HD:2 * B_QK + (h + 1) * B_HD] for h in range(nh)],
            [cur_ref[:, BP_Z + h * B_HD:BP_Z + (h + 1) * B_HD] for h in range(nh)],
            [cur_ref[:, BP_GATE + h:BP_GATE + h + 1] for h in range(nh)],
            [cur_ref[:, BP_GATE + nh + h:BP_GATE + nh + h + 1] for h in range(nh)],
            [par_ref[:, h:h + 1] for h in range(nh)], [par_ref[:, nh + h:nh + h + 1] for h in range(nh)], ng_ref[...])


def _mix_b_fwd(proj, conv_w, par, ng, memkv, name):
    s = proj.shape[0]
    nc = s // CHUNK

    def body(cur_ref, prev_ref, w_ref, par_ref, ng_ref, memkv_ref, o_ref, st_ref, dl_ref, state_ref):
        n = pl.program_id(0)

        @pl.when(n == 0)
        def _():
            state_ref[...] = jnp.zeros_like(state_ref)

        prev = jnp.where(n > 0, prev_ref[...], 0.0)
        ext = jnp.concatenate([prev, cur_ref[:, :B_QKV]], axis=0)
        y = _dn_conv(ext, w_ref)[HALO:]
        s0 = [state_ref[hv] for hv in range(B_V_HEADS)]
        st_ref[0] = state_ref[...]
        outs, s1, delta, raw = _dn_heads(*_dn_args(y, cur_ref, par_ref, ng_ref), s0)
        for hv in range(B_V_HEADS):
            state_ref[hv] = s1[hv]
            dl_ref[0, hv] = delta[hv]
            dl_ref[0, B_V_HEADS + hv] = raw[hv]
        outs = outs + [_cross_pairs(cur_ref[:, BP_XQ:BP_XQ + X_Q], memkv_ref[:, :X_Q], memkv_ref[:, X_Q:])]
        o_ref[...] = jnp.concatenate(outs, axis=1).astype(o_ref.dtype)

    return pl.pallas_call(
        body, grid=(nc,),
        in_specs=[pl.BlockSpec((CHUNK, IN_BP), lambda n: (n, 0)),
                  pl.BlockSpec((HALO, B_QKV), lambda n: (jnp.maximum(n * (CHUNK // HALO) - 1, 0), 0)),
                  pl.BlockSpec((HALO, B_QKV), lambda n: (0, 0)),
                  pl.BlockSpec((1, 128), lambda n: (0, 0)), pl.BlockSpec((1, 128), lambda n: (0, 0)),
                  pl.BlockSpec((MEM_LEN, 2 * X_Q), lambda n: (0, 0))],
        out_specs=[pl.BlockSpec((CHUNK, D), lambda n: (n, 0)),
                   pl.BlockSpec((1, B_V_HEADS, B_HD, B_HD), lambda n: (n, 0, 0, 0)),
                   pl.BlockSpec((1, 2 * B_V_HEADS, CHUNK, B_HD), lambda n: (n, 0, 0, 0))],
        out_shape=[_SDS((s, D), _ACT), _SDS((nc, B_V_HEADS, B_HD, B_HD), F32), _SDS((nc, 2 * B_V_HEADS, CHUNK, B_HD), F32)],
        scratch_shapes=[pltpu.VMEM((B_V_HEADS, B_HD, B_HD), F32)],
        name=name, compiler_params=_cp("arbitrary"))(proj, proj, conv_w, par, ng, memkv)


def _mix_b_bwd(proj, conv_w, par, ng, memkv, states, deltas, dmix, name):
    s = proj.shape[0]
    nc = s // CHUNK
    ext_rows = CHUNK + HALO

    def body(cur_ref, prev_ref, w_ref, par_ref, ng_ref, memkv_ref, st_ref, dl_ref, dmix_ref,
             dproj_ref, dw_ref, dpar_ref, dng_ref, dmemkv_ref, dstate_ref, carry_ref):
        t = pl.program_id(0)
        n = nc - 1 - t

        @pl.when(t == 0)
        def _():
            dstate_ref[...] = jnp.zeros_like(dstate_ref)
            carry_ref[...] = jnp.zeros_like(carry_ref)
            dw_ref[...] = jnp.zeros_like(dw_ref)
            dpar_ref[...] = jnp.zeros_like(dpar_ref)
            dng_ref[...] = jnp.zeros_like(dng_ref)
            dmemkv_ref[...] = jnp.zeros_like(dmemkv_ref)

        lane = lax.broadcasted_iota(jnp.int32, (1, 128), 1)
        prev = jnp.where(n > 0, prev_ref[...], 0.0)
        ext = jnp.concatenate([prev, cur_ref[:, :B_QKV]], axis=0)
        y = _dn_conv(ext, w_ref)[HALO:]
        solved = [dl_ref[0, hv] for hv in range(B_V_HEADS)]
        raw = [dl_ref[0, B_V_HEADS + hv] for hv in range(B_V_HEADS)]
        _, vjp = jax.vjp(functools.partial(_dn_heads, solved=solved, out_known=raw), *_dn_args(y, cur_ref, par_ref, ng_ref),
                         [st_ref[0, hv] for hv in range(B_V_HEADS)])
        none = [jnp.zeros((CHUNK, B_HD), F32)] * B_V_HEADS
        dyq, dyk, dyv, dz, gbl, gal, ga_log, gdtb, dng, gs0 = vjp(
            ([dmix_ref[:, hv * B_HD:(hv + 1) * B_HD].astype(F32) for hv in range(B_V_HEADS)],
             [dstate_ref[hv] for hv in range(B_V_HEADS)], none, none))
        dgate = jnp.zeros((CHUNK, 128), F32)
        dpar = jnp.zeros((1, 128), F32)
        for hv in range(B_V_HEADS):
            dstate_ref[hv] = gs0[hv]
            dgate = dgate + jnp.where(lane == hv, gbl[hv], 0.0) + jnp.where(lane == B_V_HEADS + hv, gal[hv], 0.0)
            dpar = dpar + jnp.where(lane == hv, ga_log[hv], 0.0) + jnp.where(lane == B_V_HEADS + hv, gdtb[hv], 0.0)
        dpar_ref[...] += dpar
        dng_ref[...] += dng
        _, vjp = jax.vjp(_cross_pairs, cur_ref[:, BP_XQ:BP_XQ + X_Q], memkv_ref[:, :X_Q], memkv_ref[:, X_Q:])
        dxq, dmk, dmv = vjp(dmix_ref[:, B_V:].astype(F32))
        dmemkv_ref[...] += jnp.concatenate([dmk, dmv], axis=1)
        dy = jnp.concatenate(list(dyq) + list(dyk) + list(dyv), axis=1)
        dy_ext = jnp.concatenate([jnp.zeros((HALO, B_QKV), F32), dy], axis=0)
        dext = dy_ext * w_ref[B_CONV - 1:B_CONV, :]
        dw_ref[B_CONV - 1:B_CONV, :] += jnp.sum(ext * dy_ext, axis=0, keepdims=True)
        for j in range(B_CONV - 1):
            sh = B_CONV - 1 - j
            dw_ref[j:j + 1, :] += jnp.sum(pltpu.roll(ext, sh, 0) * dy_ext, axis=0, keepdims=True)
            dext = dext + w_ref[j:j + 1, :] * pltpu.roll(dy_ext, ext_rows - sh, 0)
        tail = jnp.concatenate([jnp.zeros((CHUNK - HALO, B_QKV), F32), carry_ref[...]], axis=0)
        dqkv = dext[HALO:] + tail
        carry_ref[...] = dext[:HALO]
        dproj_ref[...] = jnp.concatenate([dqkv] + list(dz) + [dxq, dgate], axis=1).astype(dproj_ref.dtype)

    return pl.pallas_call(
        body, grid=(nc,),
        in_specs=[pl.BlockSpec((CHUNK, IN_BP), lambda t: (nc - 1 - t, 0)),
                  pl.BlockSpec((HALO, B_QKV), lambda t: (jnp.maximum((nc - 1 - t) * (CHUNK // HALO) - 1, 0), 0)),
                  pl.BlockSpec((HALO, B_QKV), lambda t: (0, 0)),
                  pl.BlockSpec((1, 128), lambda t: (0, 0)), pl.BlockSpec((1, 128), lambda t: (0, 0)),
                  pl.BlockSpec((MEM_LEN, 2 * X_Q), lambda t: (0, 0)),
                  pl.BlockSpec((1, B_V_HEADS, B_HD, B_HD), lambda t: (nc - 1 - t, 0, 0, 0)),
                  pl.BlockSpec((1, 2 * B_V_HEADS, CHUNK, B_HD), lambda t: (nc - 1 - t, 0, 0, 0)),
                  pl.BlockSpec((CHUNK, D), lambda t: (nc - 1 - t, 0))],
        out_specs=[pl.BlockSpec((CHUNK, IN_BP), lambda t: (nc - 1 - t, 0)),
                   pl.BlockSpec((HALO, B_QKV), lambda t: (0, 0)),
                   pl.BlockSpec((1, 128), lambda t: (0, 0)), pl.BlockSpec((1, 128), lambda t: (0, 0)),
                   pl.BlockSpec((MEM_LEN, 2 * X_Q), lambda t: (0, 0))],
        out_shape=[_SDS((s, IN_BP), _ACT), _SDS((HALO, B_QKV), F32), _SDS((1, 128), F32), _SDS((1, 128), F32),
                   _SDS((MEM_LEN, 2 * X_Q), F32)],
        scratch_shapes=[pltpu.VMEM((B_V_HEADS, B_HD, B_HD), F32), pltpu.VMEM((HALO, B_QKV), F32)],
        name=name, compiler_params=_cp("arbitrary"))(proj, proj, conv_w, par, ng, memkv, states, deltas, dmix)


def _place():
    return lax.axis_index("x"), lax.axis_index("y"), lax.axis_index("c")


def _all_gather(shards, name):
    n = len(shards)

    def body(*refs):
        ins, outs = refs[:n], refs[n:2 * n]
        send_sems, recv_sems, local_sems = refs[2 * n:]
        x, y, c = _place()
        me, sibling = (x, y, c), (x, y, 1 - c)
        chips = [(1 - x, y), (x, 1 - y), (1 - x, 1 - y)]

        def rows(a, px, py, pc):
            return outs[a].at[4 * px + 2 * py + pc]

        def copy(a, k, block, to, src=None):
            return pltpu.make_async_remote_copy(
                src_ref=rows(a, *block) if src is None else src, dst_ref=rows(a, *block),
                send_sem=send_sems.at[a, k], recv_sem=recv_sems.at[a, k],
                device_id=to, device_id_type=pl.DeviceIdType.MESH)

        mine = [pltpu.make_async_copy(ins[a], rows(a, *me), local_sems.at[a]) for a in range(n)]
        for cp in mine:
            cp.start()
        first = []
        for a in range(n):
            first.append(copy(a, 0, me, sibling, src=ins[a]))
            first += [copy(a, 1 + j, me, (*chip, c), src=ins[a]) for j, chip in enumerate(chips)]
        for cp in first:
            cp.start()
        passed = []
        for j, chip in enumerate(chips):
            for a in range(n):
                copy(a, 1 + j, (*chip, c), me).wait_recv()
                fwd = copy(a, 4 + j, (*chip, c), sibling)
                fwd.start()
                passed.append(fwd)
        for a in range(n):
            copy(a, 0, sibling, me).wait_recv()
            for j, chip in enumerate(chips):
                copy(a, 4 + j, (*chip, 1 - c), me).wait_recv()
        for cp in first + passed:
            cp.wait_send()
        for cp in mine:
            cp.wait()

    hbm = pl.BlockSpec(memory_space=pl.ANY)
    return pl.pallas_call(
        body, out_shape=[_SDS((N_DEV,) + s.shape, s.dtype) for s in shards],
        in_specs=[hbm] * n, out_specs=[hbm] * n,
        scratch_shapes=[pltpu.SemaphoreType.DMA((n, 7)), pltpu.SemaphoreType.DMA((n, 7)), pltpu.SemaphoreType.DMA((n,))],
        name=name)(*shards)


class _Exchange:
    def __init__(self, lands, srcs):
        self.lands, self.srcs = lands, srcs


def _seq_exchange(srcs, land_shapes, plan, name, cid):
    n, nl = len(srcs), len(land_shapes)

    def launch(*refs):
        src_refs, land_refs = refs[:n], refs[n:n + nl]
        send_sems, recv_sems, local_sems = refs[n + nl:]
        x, y, c = _place()
        my = 4 * x + 2 * y + c
        peers = [(x ^ ((k + 1) >> 2 & 1), y ^ ((k + 1) >> 1 & 1), c ^ ((k + 1) & 1)) for k in range(N_DEV - 1)]
        barrier = pltpu.get_barrier_semaphore()
        for p in peers:
            pl.semaphore_signal(barrier, inc=1, device_id=p, device_id_type=pl.DeviceIdType.MESH)
        pl.semaphore_wait(barrier, N_DEV - 1)

        def src_for(a, dest):
            return src_refs[a].at[dest] if plan[a][1] else src_refs[a]

        def slot(a, source):
            return land_refs[plan[a][0]].at[source]

        mine = [pltpu.make_async_copy(src_for(a, my), slot(a, my), local_sems.at[a]) for a in range(n)]
        for cp in mine:
            cp.start()
        sends, recvs = [], []
        for k, (px, py, pc) in enumerate(peers):
            peer = 4 * px + 2 * py + pc
            for a in range(n):
                kw = dict(send_sem=send_sems.at[a * (N_DEV - 1) + k], recv_sem=recv_sems.at[a * (N_DEV - 1) + k],
                          device_id=(px, py, pc), device_id_type=pl.DeviceIdType.MESH)
                sends.append(pltpu.make_async_remote_copy(src_ref=src_for(a, peer), dst_ref=slot(a, my), **kw))
                recvs.append(pltpu.make_async_remote_copy(src_ref=src_for(a, my), dst_ref=slot(a, peer), **kw))
        for cp in sends:
            cp.start()
        for cp in recvs:
            cp.wait_recv()
        for cp in sends:
            cp.wait_send()
        for cp in mine:
            cp.wait()

    lands = pl.kernel(
        launch, out_type=[_SDS(s, d) for s, d in land_shapes],
        mesh=plsc.ScalarSubcoreMesh(axis_name="sequencer", num_cores=1), name=name,
        scratch_types=(pltpu.SemaphoreType.DMA((n * (N_DEV - 1),)), pltpu.SemaphoreType.DMA((n * (N_DEV - 1),)),
                       pltpu.SemaphoreType.DMA((n,))),
        compiler_params=pltpu.CompilerParams(collective_id=cid))(*srcs)
    return _Exchange(list(lands), list(srcs))


def _adam_update(g, w, m, v):
    c1 = 1.0 - ADAM_B1 ** ADAM_STEP
    c2 = 1.0 - ADAM_B2 ** ADAM_STEP
    mm = ADAM_B1 * m + (1.0 - ADAM_B1) * g
    vv = ADAM_B2 * v + (1.0 - ADAM_B2) * (g * g)
    delta = -ADAM_LR * ((mm / c1) / (jnp.sqrt(vv / c2) + ADAM_EPS) + ADAM_WD * w)
    return delta, mm, vv


def _sum_sources(p_ref):
    g = p_ref[0].astype(F32)
    for s in range(1, N_DEV):
        g = g + p_ref[s].astype(F32)
    return g


def _adamw(parts, w, m, v, tr, name, restore_b=False, deps=()):
    nl, r, c = w.shape
    cp = parts[0].shape[-1]

    def body(*refs):
        p_refs = refs[:nl]
        w_ref, m_ref, v_ref = refs[nl:nl + 3]
        g_ref, d_ref, nm_ref, nv_ref = refs[-4:]
        g = _sum_sources(p_refs[0])
        for l in range(1, nl):
            g = jnp.where(pl.program_id(0) == l, _sum_sources(p_refs[l]), g)
        if restore_b:
            g = jnp.concatenate([g[:, :BP_XQ], g[:, BP_GATE:BP_GATE + 2 * B_V_HEADS], g[:, BP_XQ:BP_GATE]], axis=1)
        delta, mm, vv = _adam_update(g, w_ref[...], m_ref[...], v_ref[...])
        g_ref[...] = g
        d_ref[...] = delta
        nm_ref[...] = mm
        nv_ref[...] = vv

    spec = pl.BlockSpec((None, tr, c), lambda l, i: (l, i, 0))
    part_specs = [pl.BlockSpec((N_DEV, tr, cp), functools.partial(lambda l, i, k: (0, jnp.where(l == k, i, 0), 0), k=k))
                  for k in range(nl)]
    return pl.pallas_call(
        body, grid=(nl, r // tr),
        in_specs=part_specs + [spec, spec, spec] + _dep_specs(deps),
        out_specs=[spec] * 4, out_shape=[_SDS(w.shape, F32)] * 4,
        name=name, compiler_params=_cp("arbitrary", "arbitrary"))(*parts, w, m, v, *deps)


def _pack_small(d_rel, d_cb, d_cw, d_qkv, d_mix, d_mem, d_ffn, d_final, d_sinks, d_par, d_ng, loss_row, name):
    flat = [d_rel, *d_cb, *d_cw, d_qkv, *d_mix, *d_mem, *d_ffn, d_final, d_sinks, d_par, d_ng, loss_row]
    n = len(flat)

    def body(*refs):
        ins, o_ref = refs[:n], refs[n]
        rel, cb0, cb1, cw0, cw1, qkv, mx0, mx1, me0, me1, ff0, ff1, fin, snk, par, ng, lss = ins
        o_ref[...] = jnp.zeros_like(o_ref)
        for k in range(N_BUCKETS):
            lane = SP_REL_LANE + 128 * (k % 8)
            o_ref[SP_QKV + k // 8:SP_QKV + k // 8 + 1, lane:lane + 128] = rel[k:k + 1, :]
        for l, (cb, cw) in enumerate(((cb0, cw0), (cb1, cw1))):
            o_ref[SP_CB + l:SP_CB + l + 1, :] = jnp.concatenate([cb[j] for j in range(FF_BLOCKS)], axis=1)
            full = jnp.concatenate([cw[j] for j in range(FF_BLOCKS)], axis=1)
            o_ref[SP_CW + FFN_CONV * l:SP_CW + FFN_CONV * (l + 1), :] = full[:FFN_CONV]
        o_ref[SP_QKV:SP_QKV + B_CONV, 0:B_QKV] = qkv[0:B_CONV, :]
        for base, pair in ((SP_MIX, (mx0, mx1)), (SP_MEM, (me0, me1)), (SP_FFN, (ff0, ff1))):
            for l in range(2):
                o_ref[base + l:base + l + 1, 0:D] = pair[l][...]
        o_ref[SP_FINAL:SP_FINAL + 1, 0:D] = fin[...]
        o_ref[SP_MISC:SP_MISC + 1, 0:128] = snk[...]
        o_ref[SP_MISC:SP_MISC + 1, 128:256] = par[...]
        o_ref[SP_MISC:SP_MISC + 1, 256:384] = ng[...]
        o_ref[SP_MISC:SP_MISC + 1, 384:512] = lss[...]

    vm = pl.BlockSpec(memory_space=pltpu.VMEM)
    return pl.pallas_call(body, in_specs=[vm] * n, out_specs=vm, out_shape=_SDS((SMALL_ROWS, D_FF), F32), name=name)(*flat)


_SMALL = ["rel_bias", "norm_mix_g", "norm_mem_g", "sinks_a", "a_log_b", "dt_bias_b", "out_norm_g_b", "norm_ffn_g",
          "ffn_conv_b", "final_norm_g", "conv_qkv_b", "ffn_conv_w"]


def _adamw_small(recv, rc_qkv, rc_ffn, ws, ms, vs, name, deps=()):
    n = len(_SMALL)

    def body(*refs):
        recv_ref, qkv_ref, ffn_ref = refs[:3]
        w_refs, m_refs, v_refs = refs[3:3 + n], refs[3 + n:3 + 2 * n], refs[3 + 2 * n:3 + 3 * n]
        outs, loss_ref = refs[len(refs) - 4 * n - 1:len(refs) - 1], refs[-1]
        gs = _sum_sources(recv_ref)
        loss_ref[...] = gs[SP_MISC:SP_MISC + 1, 384:512]
        grads = {
            "rel_bias": jnp.concatenate(
                [gs[SP_QKV + k // 8:SP_QKV + k // 8 + 1, SP_REL_LANE + 128 * (k % 8):SP_REL_LANE + 128 * (k % 8) + A_HEADS]
                 for k in range(N_BUCKETS)], axis=0),
            "norm_mix_g": gs[SP_MIX:SP_MIX + 2, 0:D], "norm_mem_g": gs[SP_MEM:SP_MEM + 2, 0:D],
            "sinks_a": gs[SP_MISC:SP_MISC + 1, 0:A_HEADS],
            "a_log_b": gs[SP_MISC:SP_MISC + 1, 128:128 + B_V_HEADS],
            "dt_bias_b": gs[SP_MISC:SP_MISC + 1, 128 + B_V_HEADS:128 + 2 * B_V_HEADS],
            "out_norm_g_b": gs[SP_MISC:SP_MISC + 1, 256:256 + B_HD],
            "norm_ffn_g": gs[SP_FFN:SP_FFN + 2, 0:D], "ffn_conv_b": gs[SP_CB:SP_CB + 2, :],
            "final_norm_g": gs[SP_FINAL:SP_FINAL + 1, 0:D],
            "conv_qkv_b": _sum_sources(qkv_ref), "ffn_conv_w": _sum_sources(ffn_ref),
        }
        for i, nm in enumerate(_SMALL):
            g = grads[nm]
            delta, mm, vv = _adam_update(g, w_refs[i][...], m_refs[i][...], v_refs[i][...])
            outs[i][...] = g
            outs[n + i][...] = delta
            outs[2 * n + i][...] = mm
            outs[3 * n + i][...] = vv

    vm = pl.BlockSpec(memory_space=pltpu.VMEM)
    shapes = [_SDS(w.shape, F32) for w in ws]
    return pl.pallas_call(
        body, in_specs=[vm] * (3 + 3 * n) + _dep_specs(deps), out_specs=[vm] * (4 * n + 1),
        out_shape=shapes * 4 + [_SDS((1, 128), F32)],
        name=name)(recv, rc_qkv, rc_ffn, *ws, *ms, *vs, *deps)


def _assemble(gathered, axis):
    g = jnp.moveaxis(gathered, 0, axis)
    shp = list(g.shape)
    return g.reshape(shp[:axis] + [shp[axis] * shp[axis + 1]] + shp[axis + 2:])


def _pad_rows(a, rows):
    return jnp.pad(a, ((0, rows - a.shape[0]), (0, 0)))


def _pad_lanes(a, lanes=128):
    return jnp.pad(a, ((0, 0), (0, lanes - a.shape[1])))


def _ff_blocks(a):
    return jnp.moveaxis(a.reshape(a.shape[0], FF_BLOCKS, GU_SHARD), 1, 0)


def _reorder_b(w):
    qkv_z = w[..., :B_QKV + B_V]
    gates = w[..., B_QKV + B_V:B_QKV + B_V + 2 * B_V_HEADS]
    xq = w[..., IN_B - X_Q:]
    pad = jnp.zeros(w.shape[:-1] + (IN_BP - IN_B,), w.dtype)
    return jnp.concatenate([qkv_z, xq, gates, pad], axis=-1)


def kernel(x, mem, rel_bias, norm_mix_g, norm_mem_g, w_mem_kv, w_out, w_in_a, sinks_a, w_in_b, conv_qkv_b, a_log_b, dt_bias_b, out_norm_g_b, norm_ffn_g, w_gate_up, ffn_conv_w, ffn_conv_b, w_down, final_norm_g, loss_target, m_rel_bias, m_norm_mix_g, m_norm_mem_g, m_w_mem_kv, m_w_out, m_w_in_a, m_sinks_a, m_w_in_b, m_conv_qkv_b, m_a_log_b, m_dt_bias_b, m_out_norm_g_b, m_norm_ffn_g, m_w_gate_up, m_ffn_conv_w, m_ffn_conv_b, m_w_down, m_final_norm_g, v_rel_bias, v_norm_mix_g, v_norm_mem_g, v_w_mem_kv, v_w_out, v_w_in_a, v_sinks_a, v_w_in_b, v_conv_qkv_b, v_a_log_b, v_dt_bias_b, v_out_norm_g_b, v_norm_ffn_g, v_w_gate_up, v_ffn_conv_w, v_ffn_conv_b, v_w_down, v_final_norm_g):
    local = dict(locals())
    order = ["rel_bias", "norm_mix_g", "norm_mem_g", "w_mem_kv", "w_out", "w_in_a", "sinks_a", "w_in_b", "conv_qkv_b",
             "a_log_b", "dt_bias_b", "out_norm_g_b", "norm_ffn_g", "w_gate_up", "ffn_conv_w", "ffn_conv_b", "w_down",
             "final_norm_g"]
    wts = {n: local[n] for n in order}
    moms = {n: local["m_" + n] for n in order}
    vars_ = {n: local["v_" + n] for n in order}
    h0 = x[0]
    memx = mem[0]
    tgt = loss_target[0]
    s = h0.shape[0]
    tm = _rows(s)
    tb = min(s, _TM_BIG)

    t_ = lambda a: jnp.swapaxes(a, 1, 2)
    g_mk0, g_out0, g_ia, g_cq, g_cw = _all_gather(
        [w_mem_kv[0:1].astype(_MXU), w_out[0:1].astype(_MXU), t_(w_in_a).astype(_MXU), conv_qkv_b, ffn_conv_w], "gather_first")
    g_mk, g_out = [g_mk0], [g_out0]
    gu_land = ((N_DEV, GU_SHARD, D), _MXU)
    dn_land = ((N_DEV, DN_SHARD, D), _MXU)
    whole = [(0, False), (1, False)]
    def after(a, b):
        return a + (b[(0,) * b.ndim] * 0).astype(a.dtype)

    gu0_w = _seq_exchange([after(t_(w_gate_up)[0].astype(_MXU), g_ia)], [gu_land], [(0, False)], "gather_gate_up0", 1)
    dn0_w = _seq_exchange([after(w_down[0].astype(_MXU), g_ia)], [dn_land], [(0, False)], "gather_down0", 8)
    w_ia = g_ia.reshape(IN_A, D)
    conv_qkv = _pad_rows(_assemble(g_cq, 2)[0], HALO)
    ffn_cw_full = _assemble(g_cw, 2)
    ffn_cw = [_ff_blocks(_pad_rows(ffn_cw_full[i], HALO)) for i in range(2)]
    ffn_cb = [_ff_blocks(ffn_conv_b[i:i + 1]) for i in range(2)]
    bucket = jnp.asarray(_bucket_table())
    bias = _bias_build(rel_bias, bucket, "bias_build")
    sinks = _pad_lanes(sinks_a)
    par_b = _pad_lanes(jnp.concatenate([a_log_b, dt_bias_b], axis=1))

    row_x = pl.BlockSpec((tm, D), lambda i, j: (i, 0))
    gu_shape = (2, FF_BLOCKS, s, GU_SHARD)

    def in_proj(h, g, w, w_spec, n_cols, tn, name, deps=(), out_dtype=F32, w_t=False, tm=None):
        return _norm_matmul(h, g, w, w_spec, n_cols // tn, (h.shape[0], n_cols),
                            pl.BlockSpec((tm or _rows(h.shape[0]), tn), lambda i, j: (i, j)), name, deps=deps, out_dtype=out_dtype,
                            w_t=w_t, tm=tm)

    def ffn_fwd(i, h, g_gu, g_dn, deps=()):
        gu, hn = _norm_matmul(h, norm_ffn_g[i:i + 1], g_gu, _spec_gate_up(1), N_DEV, gu_shape,
                              _spec_gu_act(0, 1, tb), f"gate_up_{i}", deps=deps, out_dtype=_ACT, w_t=True, tm=tb)
        h_new, act, gc = _glu_down(gu, ffn_cw[i], ffn_cb[i], g_dn, h, f"glu_down_{i}")
        return h_new, gu, hn, (act, gc)

    def out_proj(i, mix, h):
        return _matmul_res(mix, row_x, g_out[i], _spec_rowsharded(0, D // N_DEV, D), 1, h, f"out_proj_{i}")

    proj_a, hn_a = in_proj(h0, norm_mix_g[0:1], w_ia, pl.BlockSpec((640, D), lambda i, j: (j, 0)), IN_A, 640, "in_proj_a",
                           deps=gu0_w.srcs + dn0_w.srcs, out_dtype=_ACT, w_t=True)
    memkv0, memn0 = in_proj(memx, norm_mem_g[0:1], g_mk[0], _spec_rowsharded(0, D // N_DEV, 2 * X_Q), 2 * X_Q, 2 * X_Q, "mem_proj_0")
    mix_a = _mix_a_fwd(proj_a, bias, sinks, memkv0, "mix_a_fwd")
    h1 = out_proj(0, mix_a, h0)
    g_gu0, g_dn0 = gu0_w.lands[0], dn0_w.lands[0]
    in_b_w = _seq_exchange([after(_reorder_b(w_in_b).astype(_MXU), h1), after(w_mem_kv[1:2].astype(_MXU), h1),
                            after(w_out[1:2].astype(_MXU), h1)],
                           [((N_DEV, 1, D // N_DEV, IN_BP), _MXU), ((N_DEV, 1, D // N_DEV, 2 * X_Q), _MXU),
                            ((N_DEV, 1, D // N_DEV, D), _MXU)], [(0, False), (1, False), (2, False)], "gather_in_b", 2)
    ffn1_w = _seq_exchange([after(t_(w_gate_up)[1].astype(_MXU), h1), after(w_down[1].astype(_MXU), h1)], [gu_land, dn_land], whole,
                           "gather_ffn1", 3)
    h2, gu0, hn_f0, act0 = ffn_fwd(0, h1, g_gu0, g_dn0, deps=in_b_w.srcs + ffn1_w.srcs)
    g_ib, g_mk1, g_out1 = in_b_w.lands
    g_mk.append(g_mk1)
    g_out.append(g_out1)
    proj_b, hn_b = in_proj(h2, norm_mix_g[1:2], g_ib, _spec_rowsharded(0, D // N_DEV, 896, col_block=1), IN_BP, 896, "in_proj_b")
    memkv1, memn1 = in_proj(memx, norm_mem_g[1:2], g_mk[1], _spec_rowsharded(0, D // N_DEV, 2 * X_Q), 2 * X_Q, 2 * X_Q, "mem_proj_1",
                            deps=[h2])
    mix_b, states, deltas = _mix_b_fwd(proj_b, conv_qkv, par_b, out_norm_g_b, memkv1, "mix_b_fwd")
    h3 = out_proj(1, mix_b, h2)
    g_gu1, g_dn1 = ffn1_w.lands
    h4, gu1, hn_f1, act1 = ffn_fwd(1, h3, g_gu1, g_dn1)
    loss_row, *dh, d_final_g = _loss_head(h4, final_norm_g[None, :], tgt, "loss_head")

    zeros_mem = jnp.zeros_like(memx)
    per_dest2 = [(0, True), (1, True)]

    def ffn_bwd(i, dh, h_in, gu, hn_f, act_gc, g_gu, g_dn, deps=()):
        act, gc = act_gc
        dgu, d_cw, d_cb = _glu_bwd(gu, gc, ffn_cw[i], dh[1], g_dn, f"glu_bwd_{i}", deps=deps)
        d_wdown = _matmul_tn(act, pl.BlockSpec((None, tb, GU_SHARD), lambda j, r: (j, r, 0)),
                             dh[1], pl.BlockSpec((tb, D), lambda j, r: (r, 0)), s, FF_BLOCKS, (GU_SHARD, D),
                             (N_DEV, DN_SHARD, D), pl.BlockSpec((2, DN_SHARD, D), lambda j, r: (j, 0, 0)), f"d_w_down_{i}",
                             tm=tb)
        *dh_new, d_g = _matmul_nt_normbwd(dgu, _spec_gu_act(0, 1, tm), g_gu, _spec_gate_up(1), N_DEV, h_in,
                                          norm_ffn_g[i:i + 1], dh[0], f"d_ffn_in_{i}", w_t=True, act_copy=True)
        d_wgu = _matmul_tn(dgu, _spec_gu_act(1, 0, tb), hn_f, pl.BlockSpec((tb, D), lambda j, r: (r, 0)), s, N_DEV,
                           (GU_SHARD, D), (N_DEV, GU_SHARD, D), pl.BlockSpec((None, GU_SHARD, D), lambda j, r: (j, 0, 0)),
                           f"d_w_gate_up_{i}", tm=tb)
        return dh_new, [d_wdown, d_wgu], d_cw, d_cb, d_g

    def out_bwd(i, dh, mix, deps):
        dmix = _matmul_nt(dh[1], g_out[i], _spec_rowsharded(0, D // N_DEV, D), 1, (s, D), row_x, f"d_mix_{i}", deps=deps, out_dtype=_ACT)
        d_wout = _matmul_tn(mix, pl.BlockSpec((tb, D), lambda j, r: (r, 0)), dh[1], pl.BlockSpec((tb, D), lambda j, r: (r, 0)),
                            s, 1, (D, D), (N_DEV, D // N_DEV, D), pl.BlockSpec((N_DEV, D // N_DEV, D), lambda j, r: (0, 0, 0)),
                            f"d_w_out_{i}", tm=tb)
        return dmix, d_wout

    def mem_bwd(i, dmemkv, memn):
        tmm = _rows(MEM_LEN)
        *_, d_g = _matmul_nt_normbwd(dmemkv, pl.BlockSpec((tmm, 2 * X_Q), lambda r, j: (r, 0)), g_mk[i],
                                     _spec_rowsharded(0, D // N_DEV, 2 * X_Q), 1, memx, norm_mem_g[i:i + 1], zeros_mem,
                                     f"d_mem_in_{i}")
        by_row = lambda j, r: (r, 0)
        d_w = _matmul_tn(memn, pl.BlockSpec((tmm, D), by_row), dmemkv, pl.BlockSpec((tmm, 2 * X_Q), by_row), MEM_LEN, 1,
                         (D, 2 * X_Q), (N_DEV, D // N_DEV, 2 * X_Q),
                         pl.BlockSpec((N_DEV, D // N_DEV, 2 * X_Q), lambda j, r: (0, 0, 0)), f"d_w_mem_kv_{i}")
        return d_w, d_g

    out_land = ((N_DEV, D // N_DEV, D), _WIRE)
    mk_land = ((N_DEV, D // N_DEV, 2 * X_Q), _WIRE)
    ffn_lands = [((N_DEV, DN_SHARD, D), _WIRE), ((N_DEV, GU_SHARD, D), _WIRE)]
    dh, d_ffn1, d_cw1, d_cb1, d_gf1 = ffn_bwd(1, dh, h3, gu1, hn_f1, act1, g_gu1, g_dn1)
    ffn1_g = _seq_exchange(d_ffn1, ffn_lands, per_dest2, "send_ffn1_grads", 5)
    dmix, d_wout1 = out_bwd(1, dh, mix_b, ffn1_g.srcs)
    dproj_b, d_convw, d_par, d_ng, dmemkv1 = _mix_b_bwd(proj_b, conv_qkv, par_b, out_norm_g_b, memkv1, states, deltas, dmix, "mix_b_bwd")
    *dh, d_gm1 = _matmul_nt_normbwd(dproj_b, pl.BlockSpec((tm, 896), lambda i, j: (i, j)), g_ib,
                                    _spec_rowsharded(0, D // N_DEV, 896, col_block=1), IN_BP // 896, h2, norm_mix_g[1:2], dh[0],
                                    "d_in_b", act_copy=True)
    d_wib = _matmul_tn(hn_b, pl.BlockSpec((tb, D), lambda j, r: (r, 0)), dproj_b, pl.BlockSpec((tb, 896), lambda j, r: (r, j)),
                       s, IN_BP // 896, (D, 896), (N_DEV, D // N_DEV, IN_BP),
                       pl.BlockSpec((N_DEV, D // N_DEV, 896), lambda j, r: (0, 0, j)), "d_w_in_b", tm=tb)
    d_wmk1, d_gmem1 = mem_bwd(1, dmemkv1, memn1)
    mix1_g = _seq_exchange([d_wout1, d_wib, d_wmk1], [out_land, ((N_DEV, D // N_DEV, IN_BP), _WIRE), mk_land],
                           [(0, True), (1, True), (2, True)], "send_mix1_grads", 6)
    dh, d_ffn0, d_cw0, d_cb0, d_gf0 = ffn_bwd(0, dh, h1, gu0, hn_f0, act0, g_gu0, g_dn0, deps=mix1_g.srcs)
    dmix, d_wout0 = out_bwd(0, dh, mix_a, d_ffn0 + ffn1_g.lands[:1])
    ffn0_g = _seq_exchange(d_ffn0 + [d_wout0], ffn_lands + [out_land], per_dest2 + [(2, True)], "send_ffn0_grads", 4)
    dproj_a, dbias, dsinks, dmemkv0 = _mix_a_bwd(proj_a, bias, sinks, memkv0, dmix, "mix_a_bwd", deps=ffn0_g.srcs)
    dx, _, d_gm0 = _matmul_nt_normbwd(dproj_a, pl.BlockSpec((tm, 640), lambda i, j: (i, j)), w_ia,
                                      pl.BlockSpec((640, D), lambda i, j: (j, 0)), IN_A // 640, h0, norm_mix_g[0:1], dh[0],
                                      "d_in_a", w_t=True)
    d_wia = _matmul_tn(dproj_a, pl.BlockSpec((tb, IN_A), lambda j, r: (r, 0)), hn_a, pl.BlockSpec((tb, D), lambda j, r: (r, 0)),
                       s, 1, (IN_A, D), (N_DEV, IA_SHARD, D), pl.BlockSpec((N_DEV, IA_SHARD, D), lambda j, r: (0, 0, 0)),
                       "d_w_in_a", tm=tb)
    d_wmk0, d_gmem0 = mem_bwd(0, dmemkv0, memn0)
    d_rel = _bias_reduce(dbias, bucket, "bias_reduce")
    small = _pack_small(d_rel, (d_cb0, d_cb1), (d_cw0, d_cw1), d_convw, (d_gm0, d_gm1), (d_gmem0, d_gmem1),
                        (d_gf0, d_gf1), d_final_g, dsinks, d_par, d_ng, loss_row, "pack_small")
    mix0_g = _seq_exchange([d_wia, d_wmk0, small],
                           [((N_DEV, IA_SHARD, D), _WIRE), mk_land, ((N_DEV, SMALL_ROWS, D_FF), F32)],
                           [(0, True), (1, True), (2, False)], "send_mix0_grads", 7)

    res = {}
    last = []

    def update(nm, parts, tr, restore=False, transposed=False):
        view = t_ if transposed else (lambda a: a)
        out = _adamw(parts, view(wts[nm]), view(moms[nm]), view(vars_[nm]), tr, "adamw_" + nm, restore_b=restore, deps=last[-1:])
        res[nm] = [view(o) for o in out]
        last.append(out[1])

    r_dn1, r_gu1 = ffn1_g.lands
    r_dn0, r_gu0, r_out0 = ffn0_g.lands
    r_out1, r_ib, r_mk1 = mix1_g.lands
    update("w_in_b", [r_ib], 32, True)
    update("w_gate_up", [r_gu0, r_gu1], 176, transposed=True)
    update("w_down", [r_dn0, r_dn1], 176)
    r_ia, r_mk0, r_small = mix0_g.lands
    update("w_mem_kv", [r_mk0, r_mk1], 128)
    update("w_out", [r_out0, r_out1], 128)
    update("w_in_a", [r_ia], IA_SHARD, transposed=True)

    my = 4 * lax.axis_index("x") + 2 * lax.axis_index("y") + lax.axis_index("c")
    cq = conv_qkv_b.shape[-1]
    cf = ffn_conv_w.shape[-1]
    rc_qkv = lax.dynamic_slice_in_dim(r_small[:, SP_QKV:SP_QKV + B_CONV, :B_QKV], my * cq, cq, axis=2)[:, None]
    rc_ffn = lax.dynamic_slice_in_dim(r_small[:, SP_CW:SP_CW + 2 * FFN_CONV, :], my * cf, cf, axis=2).reshape(N_DEV, 2, FFN_CONV, cf)
    as2d = lambda a: a[None, :] if a.ndim == 1 else a
    small_out = _adamw_small(r_small, rc_qkv, rc_ffn, [as2d(wts[n]) for n in _SMALL], [as2d(moms[n]) for n in _SMALL],
                             [as2d(vars_[n]) for n in _SMALL], "adamw_small", deps=last[-1:])
    ns = len(_SMALL)
    for i, nm in enumerate(_SMALL):
        res[nm] = [small_out[k * ns + i].reshape(wts[nm].shape) for k in range(4)]

    return (small_out[-1][0, 0], dx[None], *[res[n][0] for n in order], *[res[n][1] for n in order],
            *[res[n][2] for n in order], *[res[n][3] for n in order])
```

```python
import functools
import math

import numpy as np

import jax
import jax.numpy as jnp
from jax import lax
from jax.experimental import pallas as pl
from jax.experimental.pallas import tpu as pltpu
from jax.experimental.pallas import tpu_sc as plsc

F32 = jnp.float32
_MXU = jnp.bfloat16
_ACT = jnp.bfloat16
_WIRE = jnp.bfloat16
_HI = lax.Precision.HIGH
_TM = 1024
_TM_GLU = 1024
_TM_BIG = 2048
_VMEM_LIMIT = 48 * 1024 * 1024
_SDS = jax.ShapeDtypeStruct

D = 1024
EPS = 1e-6
A_HEADS, A_KV_HEADS, A_HD, BLK = 12, 2, 64, 128
N_BUCKETS, MAX_DISTANCE = 32, 128
B_QK_HEADS, B_V_HEADS, B_HD, B_CONV, CHUNK = 3, 6, 128, 4, 64
X_HEADS, X_HD, MEM_LEN = 4, 64, 256
D_FF, FFN_CONV = 2816, 3
A_Q, A_KV, X_Q = 768, 128, 256
B_QK, B_V, B_QKV = 384, 768, 1536
IN_A, IN_B = 1280, 2572
IN_BP = 2688
BP_Z, BP_XQ, BP_GATE = 1536, 2304, 2560
HALO = 8
GLU_HALO = 16

N_DEV = 8
GU_SHARD = 2 * D_FF // N_DEV
FF_BLOCKS = D_FF // GU_SHARD
DN_SHARD = D_FF // N_DEV
IA_SHARD = IN_A // N_DEV

ADAM_LR, ADAM_B1, ADAM_B2, ADAM_EPS, ADAM_WD, ADAM_STEP = 0.001, 0.9, 0.999, 1e-08, 0.01, 10

SP_CB, SP_CW, SP_QKV, SP_MIX, SP_MEM, SP_FFN, SP_FINAL, SP_MISC, SMALL_ROWS = 0, 2, 8, 12, 14, 16, 18, 19, 24
SP_REL_LANE = B_QKV


def _cp(*sems):
    return pltpu.CompilerParams(dimension_semantics=sems, vmem_limit_bytes=_VMEM_LIMIT)


def _mm(a, b):
    return jnp.dot(a.astype(_MXU), b.astype(_MXU), preferred_element_type=F32)


def _mm_nt(a, b):
    return lax.dot_general(a.astype(_MXU), b.astype(_MXU), (((1,), (1,)), ((), ())), preferred_element_type=F32)


def _mm_tn(a, b):
    return lax.dot_general(a.astype(_MXU), b.astype(_MXU), (((0,), (0,)), ((), ())), preferred_element_type=F32)


def _mmf(a, b):
    return jnp.dot(a, b, preferred_element_type=F32, precision=_HI)


def _mmf_nt(a, b):
    return lax.dot_general(a, b, (((1,), (1,)), ((), ())), preferred_element_type=F32, precision=_HI)


def _silu(x):
    return x * jax.nn.sigmoid(x)


def _w2d(ref):
    v = ref[...]
    return v.reshape(-1, v.shape[-1])


def _rows(m):
    return min(m, _TM)


def _spec_rowsharded(layer, rows, cols, col_block=None):
    if col_block is None:
        return pl.BlockSpec((N_DEV, None, rows, cols), lambda *_: (0, layer, 0, 0))
    return pl.BlockSpec((N_DEV, None, rows, cols), lambda *ids: (0, layer, 0, ids[col_block]))


def _spec_gate_up(axis):
    return pl.BlockSpec((None, GU_SHARD, D), lambda *ids: (ids[axis], 0, 0))


def _spec_down(axis):
    return pl.BlockSpec((2, DN_SHARD, D), lambda *ids: (ids[axis], 0, 0))


def _dep_specs(deps):
    return [pl.BlockSpec(memory_space=pl.ANY) for d in deps]


def _spec_gu_act(row_axis, axis, tm):
    return pl.BlockSpec((None, None, tm, GU_SHARD), lambda *ids: (ids[axis] // FF_BLOCKS, ids[axis] % FF_BLOCKS, ids[row_axis], 0))


def _norm_matmul(x, g, w, w_spec, n_blocks, out_shape, out_spec, name, deps=(), out_dtype=F32, w_t=False, tm=None):
    m, k = x.shape
    tm = tm or _rows(m)

    def body(x_ref, g_ref, w_ref, *rest):
        y_ref, hn_ref = rest[-2:]

        @pl.when(pl.program_id(1) == 0)
        def _():
            xv = x_ref[...]
            r = lax.rsqrt(jnp.mean(xv * xv, axis=-1, keepdims=True) + EPS)
            hn_ref[...] = (xv * r * g_ref[...]).astype(hn_ref.dtype)

        y_ref[...] = (_mm_nt if w_t else _mm)(hn_ref[...], _w2d(w_ref)).astype(y_ref.dtype)

    return pl.pallas_call(
        body, grid=(m // tm, n_blocks),
        in_specs=[pl.BlockSpec((tm, k), lambda i, j: (i, 0)), pl.BlockSpec((1, k), lambda i, j: (0, 0)), w_spec]
        + _dep_specs(deps),
        out_specs=[out_spec, pl.BlockSpec((tm, k), lambda i, j: (i, 0))],
        out_shape=[_SDS(out_shape, out_dtype), _SDS((m, k), _ACT)],
        name=name, compiler_params=_cp("arbitrary", "arbitrary"))(x, g, w, *deps)


def _matmul_res(a, a_spec, w, w_spec, n_k, res, name):
    m, n = res.shape
    tm = _rows(m)

    def body(a_ref, w_ref, r_ref, o_ref):
        part = _mm(a_ref[...], _w2d(w_ref))

        @pl.when(pl.program_id(1) == 0)
        def _():
            o_ref[...] = r_ref[...] + part

        @pl.when(pl.program_id(1) > 0)
        def _():
            o_ref[...] += part

    return pl.pallas_call(
        body, grid=(m // tm, n_k),
        in_specs=[a_spec, w_spec, pl.BlockSpec((tm, n), lambda i, j: (i, 0))],
        out_specs=pl.BlockSpec((tm, n), lambda i, j: (i, 0)),
        out_shape=_SDS((m, n), F32), name=name, compiler_params=_cp("arbitrary", "arbitrary"))(a, w, res)


def _matmul_nt(dy, w, w_spec, n_blocks, out_shape, out_spec, name, deps=(), out_dtype=F32):
    m, n = dy.shape
    tm = _rows(m)

    def body(dy_ref, w_ref, *rest):
        o_ref = rest[-1]
        o_ref[...] = _mm_nt(dy_ref[...], _w2d(w_ref)).astype(o_ref.dtype)

    return pl.pallas_call(
        body, grid=(m // tm, n_blocks),
        in_specs=[pl.BlockSpec((tm, n), lambda i, j: (i, 0)), w_spec] + _dep_specs(deps),
        out_specs=out_spec, out_shape=_SDS(out_shape, out_dtype),
        name=name, compiler_params=_cp("arbitrary", "arbitrary"))(dy, w, *deps)


def _matmul_nt_normbwd(dy, dy_spec, w, w_spec, nj, h, g, dh_in, name, w_t=False, act_copy=False):
    m, k = h.shape
    tm = _rows(m)

    def body(dy_ref, w_ref, h_ref, g_ref, dhin_ref, dh_ref, *rest):
        dg_ref, acc_ref = rest[-2:]
        i, j = pl.program_id(0), pl.program_id(1)

        @pl.when(j == 0)
        def _():
            acc_ref[...] = jnp.zeros_like(acc_ref)

        acc_ref[...] += (_mm if w_t else _mm_nt)(dy_ref[...], _w2d(w_ref))

        @pl.when(j == nj - 1)
        def _():
            xv = h_ref[...]
            r = lax.rsqrt(jnp.mean(xv * xv, axis=-1, keepdims=True) + EPS)
            xh = xv * r
            dhn = acc_ref[...]
            part = jnp.sum(dhn * xh, axis=0, keepdims=True)

            @pl.when(i == 0)
            def _():
                dg_ref[...] = part

            @pl.when(i > 0)
            def _():
                dg_ref[...] += part

            t = dhn * g_ref[...]
            dh = dhin_ref[...] + r * (t - xh * jnp.mean(t * xh, axis=-1, keepdims=True))
            dh_ref[...] = dh
            if act_copy:
                rest[0][...] = dh.astype(_ACT)

    rows = pl.BlockSpec((tm, k), lambda i, j: (i, 0))
    outs = pl.pallas_call(
        body, grid=(m // tm, nj),
        in_specs=[dy_spec, w_spec, rows, pl.BlockSpec((1, k), lambda i, j: (0, 0)), rows],
        out_specs=[rows] + [rows] * act_copy + [pl.BlockSpec((1, k), lambda i, j: (0, 0))],
        out_shape=[_SDS((m, k), F32)] + [_SDS((m, k), _ACT)] * act_copy + [_SDS((1, k), F32)],
        scratch_shapes=[pltpu.VMEM((tm, k), F32)],
        name=name, compiler_params=_cp("arbitrary", "arbitrary"))(dy, w, h, g, dh_in)
    return outs[0], (outs[1] if act_copy else None), outs[-1]


def _matmul_tn(x, x_spec, dy, dy_spec, m, n_blocks, acc_shape, out_shape, out_spec, name, tm=None):
    tm = tm or _rows(m)
    nm = m // tm

    def body(x_ref, dy_ref, o_ref, acc_ref):
        @pl.when(pl.program_id(1) == 0)
        def _():
            acc_ref[...] = jnp.zeros_like(acc_ref)

        acc_ref[...] += _mm_tn(x_ref[...], dy_ref[...])

        @pl.when(pl.program_id(1) == nm - 1)
        def _():
            o_ref[...] = acc_ref[...].reshape(o_ref.shape).astype(o_ref.dtype)

    return pl.pallas_call(
        body, grid=(n_blocks, nm), in_specs=[x_spec, dy_spec], out_specs=out_spec,
        out_shape=_SDS(out_shape, _WIRE), scratch_shapes=[pltpu.VMEM(acc_shape, F32)],
        name=name, compiler_params=_cp("arbitrary", "arbitrary"))(x, dy)


def _loss_head(h, g, tgt, name):
    m, k = h.shape
    tm = _rows(m)

    def body(h_ref, g_ref, t_ref, loss_ref, dh_ref, dha_ref, dg_ref):
        i = pl.program_id(0)
        xv = h_ref[...]
        r = lax.rsqrt(jnp.mean(xv * xv, axis=-1, keepdims=True) + EPS)
        xh = xv * r
        gv = g_ref[...]
        err = xh * gv - t_ref[...]
        lpart = jnp.zeros((1, 128), F32) + 0.5 * jnp.sum(jnp.mean(err * err, axis=-1, keepdims=True), axis=0, keepdims=True)
        dy = err * (1.0 / k)
        gpart = jnp.sum(dy * xh, axis=0, keepdims=True)

        @pl.when(i == 0)
        def _():
            loss_ref[...] = lpart
            dg_ref[...] = gpart

        @pl.when(i > 0)
        def _():
            loss_ref[...] += lpart
            dg_ref[...] += gpart

        t = dy * gv
        dh = r * (t - xh * jnp.mean(t * xh, axis=-1, keepdims=True))
        dh_ref[...] = dh
        dha_ref[...] = dh.astype(_ACT)

    rows = pl.BlockSpec((tm, k), lambda i: (i, 0))
    return pl.pallas_call(
        body, grid=(m // tm,),
        in_specs=[rows, pl.BlockSpec((1, k), lambda i: (0, 0)), rows],
        out_specs=[pl.BlockSpec((1, 128), lambda i: (0, 0)), rows, rows, pl.BlockSpec((1, k), lambda i: (0, 0))],
        out_shape=[_SDS((1, 128), F32), _SDS((m, k), F32), _SDS((m, k), _ACT), _SDS((1, k), F32)],
        name=name, compiler_params=_cp("arbitrary"))(h, g, tgt)


def _glu_down(gu, conv_w, conv_b, w_down, res, name):
    s = gu.shape[2]
    tm = min(s, _TM_GLU)

    def body(gu_ref, prev_ref, w_ref, b_ref, wdn_ref, r_ref, o_ref, act_ref, gc_ref):
        i, j = pl.program_id(0), pl.program_id(1)
        prev = jnp.where(i > 0, prev_ref[...].astype(F32), 0.0)
        ext = jnp.concatenate([prev, gu_ref[0].astype(F32)], axis=0)
        gc = b_ref[...] + w_ref[FFN_CONV - 1:FFN_CONV, :] * ext
        for k in range(FFN_CONV - 1):
            gc = gc + w_ref[k:k + 1, :] * pltpu.roll(ext, FFN_CONV - 1 - k, 0)
        gc = gc[GLU_HALO:]
        gc_ref[...] = gc.astype(gc_ref.dtype)
        act =(_silu(gc) * gu_ref[1].astype(F32)).astype(act_ref.dtype)
        act_ref[...] = act
        part = _mm(act, _w2d(wdn_ref))

        @pl.when(j == 0)
        def _():
            o_ref[...] = r_ref[...] + part

        @pl.when(j > 0)
        def _():
            o_ref[...] += part

    return pl.pallas_call(
        body, grid=(s // tm, FF_BLOCKS),
        in_specs=[pl.BlockSpec((2, None, tm, GU_SHARD), lambda i, j: (0, j, i, 0)),
                  pl.BlockSpec((None, None, GLU_HALO, GU_SHARD),
                               lambda i, j: (0, j, jnp.maximum(i * (tm // GLU_HALO) - 1, 0), 0)),
                  pl.BlockSpec((None, HALO, GU_SHARD), lambda i, j: (j, 0, 0)),
                  pl.BlockSpec((None, 1, GU_SHARD), lambda i, j: (j, 0, 0)),
                  _spec_down(1), pl.BlockSpec((tm, D), lambda i, j: (i, 0))],
        out_specs=[pl.BlockSpec((tm, D), lambda i, j: (i, 0)), pl.BlockSpec((None, tm, GU_SHARD), lambda i, j: (j, i, 0)),
                   pl.BlockSpec((None, tm, GU_SHARD), lambda i, j: (j, i, 0))],
        out_shape=[_SDS((s, D), F32), _SDS((FF_BLOCKS, s, GU_SHARD), _ACT), _SDS((FF_BLOCKS, s, GU_SHARD), _ACT)], name=name,
        compiler_params=_cp("arbitrary", "arbitrary"))(gu, gu, conv_w, conv_b, w_down, res)


def _glu_bwd(gu, gc, conv_w, dh, w_down, name, deps=()):
    s = gu.shape[2]
    tm = min(s, _TM_GLU)
    nt = s // tm
    ext_rows = tm + GLU_HALO

    def body(gu_ref, prev_ref, gc_ref, w_ref, dh_ref, wdn_ref, *rest):
        dgu_ref, dw_ref, db_ref, carry_ref = rest[-4:]
        t = pl.program_id(1)
        i = nt - 1 - t

        @pl.when(t == 0)
        def _():
            carry_ref[...] = jnp.zeros_like(carry_ref)
            dw_ref[...] = jnp.zeros_like(dw_ref)
            db_ref[...] = jnp.zeros_like(db_ref)

        up = gu_ref[1].astype(F32)
        prev = jnp.where(i > 0, prev_ref[...].astype(F32), 0.0)
        ext = jnp.concatenate([prev, gu_ref[0].astype(F32)], axis=0)
        gc = gc_ref[...].astype(F32)
        sg = jax.nn.sigmoid(gc)
        da = _mm_nt(dh_ref[...], _w2d(wdn_ref))
        dup = da * (gc * sg)
        dgc = da * up * (sg * (1.0 + gc * (1.0 - sg)))
        db_ref[...] += jnp.sum(dgc, axis=0, keepdims=True)
        dgc_ext = jnp.concatenate([jnp.zeros((GLU_HALO, GU_SHARD), F32), dgc], axis=0)
        ahead = [pltpu.roll(dgc_ext, ext_rows - (FFN_CONV - 1 - j), 0) if j < FFN_CONV - 1 else dgc_ext
                 for j in range(FFN_CONV)]
        dext = ahead[0] * w_ref[0:1, :]
        for j in range(FFN_CONV):
            dw_ref[j:j + 1, :] += jnp.sum(ext * ahead[j], axis=0, keepdims=True)
            if j > 0:
                dext = dext + ahead[j] * w_ref[j:j + 1, :]
        tail = jnp.concatenate([jnp.zeros((tm - GLU_HALO, GU_SHARD), F32), carry_ref[...]], axis=0)
        dgate = dext[GLU_HALO:] + tail
        carry_ref[...] = dext[:GLU_HALO]
        dgu_ref[0] = dgate.astype(dgu_ref.dtype)
        dgu_ref[1] = dup.astype(dgu_ref.dtype)

    return pl.pallas_call(
        body, grid=(FF_BLOCKS, nt),
        in_specs=[pl.BlockSpec((2, None, tm, GU_SHARD), lambda j, t: (0, j, nt - 1 - t, 0)),
                  pl.BlockSpec((None, None, GLU_HALO, GU_SHARD),
                               lambda j, t: (0, j, jnp.maximum((nt - 1 - t) * (tm // GLU_HALO) - 1, 0), 0)),
                  pl.BlockSpec((None, tm, GU_SHARD), lambda j, t: (j, nt - 1 - t, 0)),
                  pl.BlockSpec((None, HALO, GU_SHARD), lambda j, t: (j, 0, 0)),
                  pl.BlockSpec((tm, D), lambda j, t: (nt - 1 - t, 0)), _spec_down(0)] + _dep_specs(deps),
        out_specs=[pl.BlockSpec((2, None, tm, GU_SHARD), lambda j, t: (0, j, nt - 1 - t, 0)),
                   pl.BlockSpec((None, HALO, GU_SHARD), lambda j, t: (j, 0, 0)),
                   pl.BlockSpec((None, 1, GU_SHARD), lambda j, t: (j, 0, 0))],
        out_shape=[_SDS(gu.shape, _ACT), _SDS((FF_BLOCKS, HALO, GU_SHARD), F32), _SDS((FF_BLOCKS, 1, GU_SHARD), F32)],
        scratch_shapes=[pltpu.VMEM((GLU_HALO, GU_SHARD), F32)],
        name=name, compiler_params=_cp("arbitrary", "arbitrary"))(gu, gu, gc, conv_w, dh, w_down, *deps)


def _bucket_table():
    qi = np.arange(BLK)[:, None]
    kj = np.arange(BLK)[None, :]
    n = np.where(kj > qi, BLK + qi - kj, qi - kj)
    max_exact = N_BUCKETS // 2
    nf = np.maximum(n, 1).astype(np.float32)
    large = max_exact + (np.log(nf / max_exact) / math.log(MAX_DISTANCE / max_exact)
                         * (N_BUCKETS - max_exact)).astype(np.int32)
    large = np.minimum(large, N_BUCKETS - 1)
    return np.where(n < max_exact, n, large).astype(np.int32)


def _lane_low():
    return lax.broadcasted_iota(jnp.int32, (1, 128), 1) < A_HD


def _swa_groups(q, kd, vd, sink, bias, upper, first):
    n = A_HEADS // A_KV_HEADS
    ng = len(q)
    low = _lane_low()
    qm = [jnp.concatenate([jnp.where(low == (h % 2 == 0), q[g][:, (h // 2) * 128:(h // 2 + 1) * 128], 0.0) for h in range(n)], axis=0)
          for g in range(ng)]
    s2 = [_mm_nt(qm[g], kd[g]) * (A_HD ** -0.5) for g in range(ng)]
    s = [jnp.where(upper[None], s2[g][:, :BLK].reshape(n, BLK, BLK), s2[g][:, BLK:].reshape(n, BLK, BLK)) + bias[g] for g in range(ng)]
    s = [t if f is None else jnp.where((upper & f)[None], -jnp.inf, t) for t, f in zip(s, first)]
    m = [lax.stop_gradient(jnp.maximum(jnp.max(s[g], axis=-1, keepdims=True), sink[g])) for g in range(ng)]
    p = [jnp.exp(s[g] - m[g]) for g in range(ng)]
    split = [jnp.concatenate([jnp.where(upper[None], t, 0.0), jnp.where(upper[None], 0.0, t)], axis=-1).reshape(n * BLK, 2 * BLK)
             for t in p]
    ones = jnp.ones((BLK, 128), F32)
    den = [_mm(p[g].reshape(n * BLK, BLK), ones) + jnp.exp(sink[g] - m[g]).reshape(n * BLK, 1) for g in range(ng)]
    o = [_mm(split[g], vd[g]) / den[g] for g in range(ng)]
    return [jnp.concatenate([jnp.where(low, t[2 * k * BLK:(2 * k + 1) * BLK], t[(2 * k + 1) * BLK:(2 * k + 2) * BLK])
                             for k in range(n // 2)], axis=1) for t in o]


def _mix_a_core(q, kd, vd, sink, bias, xq, mk, mv, upper, first):
    return _swa_groups(q, kd, vd, sink, bias, upper, first), _cross_pairs(xq, mk, mv)


def _swa_sinks(sink_ref, g):
    n = A_HEADS // A_KV_HEADS
    return jnp.concatenate([sink_ref[:, h:h + 1] for h in range(g * n, (g + 1) * n)], axis=0).reshape(n, 1, 1)


def _both_halves(t, t_rolled, g):
    low = _lane_low()
    return jnp.where(low, t, t_rolled) if g == 0 else jnp.where(low, t_rolled, t)


def _cross_pairs(q, mk, mv):
    rows = q.shape[0]
    low = _lane_low()
    qm = [jnp.concatenate([jnp.where(low, q[:, p * 128:(p + 1) * 128], 0.0), jnp.where(low, 0.0, q[:, p * 128:(p + 1) * 128])], axis=0)
          for p in range(X_HEADS // 2)]
    s = [_mm_nt(qm[p], mk[:, p * 128:(p + 1) * 128]) * (X_HD ** -0.5) for p in range(X_HEADS // 2)]
    e = [jnp.exp(t - lax.stop_gradient(jnp.max(t, axis=-1, keepdims=True))) for t in s]
    pr = [t / jnp.sum(t, axis=-1, keepdims=True) for t in e]
    o = [_mm(pr[p], mv[:, p * 128:(p + 1) * 128]) for p in range(X_HEADS // 2)]
    return jnp.concatenate([jnp.where(low, t[:rows], t[rows:]) for t in o], axis=1)


def _swa_upper():
    qi = lax.broadcasted_iota(jnp.int32, (BLK, BLK), 0)
    kj = lax.broadcasted_iota(jnp.int32, (BLK, BLK), 1)
    return kj > qi


def _bias_build(rel_bias, bucket, name):
    def body(rb_ref, bucket_ref, o_ref):
        b = bucket_ref[...]
        for h in range(A_HEADS):
            acc = jnp.zeros((BLK, BLK), F32)
            for k in range(N_BUCKETS):
                acc = jnp.where(b == k, rb_ref[k, h], acc)
            o_ref[h] = acc

    return pl.pallas_call(
        body, in_specs=[pl.BlockSpec(memory_space=pltpu.SMEM), pl.BlockSpec(memory_space=pltpu.VMEM)],
        out_specs=pl.BlockSpec(memory_space=pltpu.VMEM),
        out_shape=_SDS((A_HEADS, BLK, BLK), F32), name=name)(rel_bias, bucket)


def _bias_reduce(dbias, bucket, name):
    def body(db_ref, bucket_ref, o_ref):
        b = bucket_ref[...]
        row = lax.broadcasted_iota(jnp.int32, (N_BUCKETS, 128), 0)
        lane = lax.broadcasted_iota(jnp.int32, (N_BUCKETS, 128), 1)
        acc = jnp.zeros((N_BUCKETS, 128), F32)
        for h in range(A_HEADS):
            v = db_ref[h]
            for k in range(N_BUCKETS):
                sk = jnp.sum(jnp.sum(jnp.where(b == k, v, 0.0), axis=1, keepdims=True), axis=0, keepdims=True)
                acc = acc + jnp.where((row == k) & (lane == h), sk, 0.0)
        o_ref[...] = acc

    return pl.pallas_call(
        body, in_specs=[pl.BlockSpec(memory_space=pltpu.VMEM)] * 2,
        out_specs=pl.BlockSpec(memory_space=pltpu.VMEM),
        out_shape=_SDS((N_BUCKETS, 128), F32), name=name)(dbias, bucket)


def _mix_a_fwd(proj, bias, sinks, memkv, name):
    s = proj.shape[0]
    per = 2
    nb = s // (per * BLK)
    grp = A_HEADS // A_KV_HEADS

    def body(proj_ref, prev_ref, bias_ref, sink_ref, memkv_ref, o_ref):
        i = pl.program_id(0)
        upper = _swa_upper()
        proj = proj_ref[...].astype(F32)
        kv = jnp.concatenate([prev_ref[...].astype(F32), proj[:, A_Q:A_Q + 2 * A_KV]], axis=0)
        k, v = kv[:, :A_KV], kv[:, A_KV:]
        k_r = pltpu.roll(k, A_HD, 1)
        v_r = pltpu.roll(v, A_HD, 1)
        gw = A_Q // A_KV_HEADS
        each = [(b, g) for b in range(per) for g in range(A_KV_HEADS)]

        def window(a, a_r, b, g):
            return _both_halves(a[b * BLK:(b + 2) * BLK], a_r[b * BLK:(b + 2) * BLK], g)

        swa, cross = _mix_a_core([proj[b * BLK:(b + 1) * BLK, g * gw:(g + 1) * gw] for b, g in each],
                                 [window(k, k_r, b, g) for b, g in each], [window(v, v_r, b, g) for b, g in each],
                                 [_swa_sinks(sink_ref, g) for b, g in each], [bias_ref[g * grp:(g + 1) * grp] for b, g in each],
                                 proj[:, A_Q + 2 * A_KV:], memkv_ref[:, :X_Q], memkv_ref[:, X_Q:], upper,
                                 [(i == 0) if b == 0 else None for b, g in each])
        for b in range(per):
            o_ref[b * BLK:(b + 1) * BLK, :] = jnp.concatenate(
                swa[b * A_KV_HEADS:(b + 1) * A_KV_HEADS] + [cross[b * BLK:(b + 1) * BLK]], axis=1).astype(o_ref.dtype)

    return pl.pallas_call(
        body, grid=(nb,),
        in_specs=[pl.BlockSpec((per * BLK, IN_A), lambda i: (i, 0)),
                  pl.BlockSpec((BLK, 2 * A_KV), lambda i: (jnp.maximum(per * i - 1, 0), A_Q // (2 * A_KV))),
                  pl.BlockSpec((A_HEADS, BLK, BLK), lambda i: (0, 0, 0)),
                  pl.BlockSpec((1, 128), lambda i: (0, 0)),
                  pl.BlockSpec((MEM_LEN, 2 * X_Q), lambda i: (0, 0))],
        out_specs=pl.BlockSpec((per * BLK, D), lambda i: (i, 0)),
        out_shape=_SDS((s, D), _ACT), name=name, compiler_params=_cp("arbitrary"))(proj, proj, bias, sinks, memkv)


def _mix_a_bwd(proj, bias, sinks, memkv, dmix, name, deps=()):
    s = proj.shape[0]
    nb = s // BLK
    grp = A_HEADS // A_KV_HEADS

    def body(proj_ref, prev_ref, bias_ref, sink_ref, memkv_ref, dmix_ref, *rest):
        dproj_ref, dbias_ref, dsink_ref, dmemkv_ref, carry_ref = rest[-5:]
        t = pl.program_id(0)
        i = nb - 1 - t

        @pl.when(t == 0)
        def _():
            carry_ref[...] = jnp.zeros_like(carry_ref)
            dbias_ref[...] = jnp.zeros_like(dbias_ref)
            dsink_ref[...] = jnp.zeros_like(dsink_ref)
            dmemkv_ref[...] = jnp.zeros_like(dmemkv_ref)

        upper = _swa_upper()
        lane = lax.broadcasted_iota(jnp.int32, (1, 128), 1)
        low = _lane_low()
        prev = prev_ref[...].astype(F32)
        proj = proj_ref[...].astype(F32)
        kb = jnp.concatenate([prev[:, :A_KV], proj[:, A_Q:A_Q + A_KV]], axis=0)
        vb = jnp.concatenate([prev[:, A_KV:], proj[:, A_Q + A_KV:A_Q + 2 * A_KV]], axis=0)
        kb_r = pltpu.roll(kb, A_HD, 1)
        vb_r = pltpu.roll(vb, A_HD, 1)
        gw = A_Q // A_KV_HEADS
        groups = range(A_KV_HEADS)
        _, vjp = jax.vjp(
            functools.partial(_mix_a_core, upper=upper, first=[i == 0] * A_KV_HEADS),
            [proj[:, g * gw:(g + 1) * gw] for g in groups], [_both_halves(kb, kb_r, g) for g in groups],
            [_both_halves(vb, vb_r, g) for g in groups], [_swa_sinks(sink_ref, g) for g in groups],
            [bias_ref[g * grp:(g + 1) * grp] for g in groups], proj[:, A_Q + 2 * A_KV:], memkv_ref[:, :X_Q], memkv_ref[:, X_Q:])
        dqs, dk, dv, ds, db, dxq, dmk, dmv = vjp(
            ([dmix_ref[:, g * gw:(g + 1) * gw].astype(F32) for g in groups], dmix_ref[:, A_Q:].astype(F32)))
        dkd = [t + pltpu.roll(t, A_HD, 1) for t in dk]
        dvd = [t + pltpu.roll(t, A_HD, 1) for t in dv]
        dsink = jnp.zeros((1, 128), F32)
        for g in groups:
            for h in range(grp):
                dsink = dsink + jnp.where(lane == g * grp + h, ds[g][h], 0.0)
            dbias_ref[g * grp:(g + 1) * grp] += db[g]
        dsink_ref[...] += dsink
        dkb = jnp.where(low, dkd[0], dkd[1])
        dvb = jnp.where(low, dvd[0], dvd[1])
        dmemkv_ref[...] += jnp.concatenate([dmk, dmv], axis=1)
        dkv_cur = jnp.concatenate([dkb[BLK:], dvb[BLK:]], axis=1) + carry_ref[...]
        carry_ref[...] = jnp.concatenate([dkb[:BLK], dvb[:BLK]], axis=1)
        dproj_ref[...] = jnp.concatenate(list(dqs) + [dkv_cur, dxq], axis=1).astype(dproj_ref.dtype)

    return pl.pallas_call(
        body, grid=(nb,),
        in_specs=[pl.BlockSpec((BLK, IN_A), lambda t: (nb - 1 - t, 0)),
                  pl.BlockSpec((BLK, 2 * A_KV), lambda t: (jnp.maximum(nb - 2 - t, 0), A_Q // (2 * A_KV))),
                  pl.BlockSpec((A_HEADS, BLK, BLK), lambda t: (0, 0, 0)),
                  pl.BlockSpec((1, 128), lambda t: (0, 0)),
                  pl.BlockSpec((MEM_LEN, 2 * X_Q), lambda t: (0, 0)),
                  pl.BlockSpec((BLK, D), lambda t: (nb - 1 - t, 0))] + _dep_specs(deps),
        out_specs=[pl.BlockSpec((BLK, IN_A), lambda t: (nb - 1 - t, 0)),
                   pl.BlockSpec((A_HEADS, BLK, BLK), lambda t: (0, 0, 0)),
                   pl.BlockSpec((1, 128), lambda t: (0, 0)),
                   pl.BlockSpec((MEM_LEN, 2 * X_Q), lambda t: (0, 0))],
        out_shape=[_SDS((s, IN_A), _ACT), _SDS((A_HEADS, BLK, BLK), F32), _SDS((1, 128), F32),
                   _SDS((MEM_LEN, 2 * X_Q), F32)],
        scratch_shapes=[pltpu.VMEM((BLK, 2 * A_KV), F32)],
        name=name, compiler_params=_cp("arbitrary"))(proj, proj, bias, sinks, memkv, dmix, *deps)


def _neumann(pw, rhs):
    nh = len(pw)
    x = rhs
    for lvl in range(6):
        if lvl < 5:
            prod = [_mmf(pw[h], jnp.concatenate([x[h], pw[h]], axis=1)) for h in range(nh)]
            x = [x[h] + prod[h][:, :B_HD] for h in range(nh)]
            pw = [t[:, B_HD:] for t in prod]
        else:
            x = [x[h] + _mmf(pw[h], x[h]) for h in range(nh)]
    return x


@jax.custom_vjp
def _tri_solve(pw, rhs):
    return _neumann(pw, rhs)


def _tri_solve_fwd(pw, rhs):
    x = _neumann(pw, rhs)
    return x, (pw, x)


def _tri_solve_bwd(res, dx):
    pw, x = res
    d_rhs = _neumann([t.T for t in pw], list(dx))
    return [_mmf_nt(d_rhs[h], x[h]) for h in range(len(pw))], d_rhs


_tri_solve.defvjp(_tri_solve_fwd, _tri_solve_bwd)


@jax.custom_vjp
def _tri_solved(pw, rhs, x):
    return x


def _tri_solved_fwd(pw, rhs, x):
    return x, (pw, x)


def _tri_solved_bwd(res, dx):
    d_pw, d_rhs = _tri_solve_bwd(res, dx)
    return d_pw, d_rhs, [jnp.zeros_like(t) for t in res[1]]


_tri_solved.defvjp(_tri_solved_fwd, _tri_solved_bwd)


@jax.custom_vjp
def _known(x, value):
    return value


def _known_fwd(x, value):
    return value, None


def _known_bwd(_, g):
    return g, jnp.zeros_like(g)


_known.defvjp(_known_fwd, _known_bwd)


def _dn_heads(yq, yk, yv, z, bl, al, a_log, dtb, ng, s0, solved=None, out_known=None):
    c = CHUNK
    nh = B_V_HEADS
    rep = B_V_HEADS // B_QK_HEADS
    r = lax.broadcasted_iota(jnp.int32, (c, c), 0)
    cc = lax.broadcasted_iota(jnp.int32, (c, c), 1)
    q = [_silu(t) for t in yq]
    k = [_silu(t) for t in yk]
    v = [_silu(t) for t in yv]
    q = [t * lax.rsqrt(jnp.sum(t * t, axis=-1, keepdims=True) + EPS) * (B_HD ** -0.5) for t in q]
    k = [t * lax.rsqrt(jnp.sum(t * t, axis=-1, keepdims=True) + EPS) for t in k]
    beta = [jax.nn.sigmoid(t) for t in bl]
    g = [-jnp.exp(a_log[h]) * jax.nn.softplus(al[h] + dtb[h]) for h in range(nh)]
    gb = [jnp.broadcast_to(t, (c, c)) for t in g]
    gc_col = [jnp.sum(jnp.where(cc <= r, t.T, 0.0), axis=1, keepdims=True) for t in gb]
    gc_row = [jnp.sum(jnp.where(r <= cc, t, 0.0), axis=0, keepdims=True) for t in gb]
    gc_last = [jnp.sum(t, axis=0, keepdims=True) for t in g]
    decay = [jnp.exp(jnp.where(r >= cc, gc_col[h] - gc_row[h], -jnp.inf)) for h in range(nh)]
    kq = [_mmf_nt(jnp.concatenate([k[h], q[h]], axis=0), k[h]) for h in range(B_QK_HEADS)]
    kk = [t[:c] for t in kq]
    qk = [t[c:] for t in kq]
    egc = [jnp.exp(t) for t in gc_col]
    both = [_mmf(jnp.concatenate([(beta[h] * egc[h]) * k[h // rep], q[h // rep] * egc[h]], axis=0), s0[h]) for h in range(nh)]
    rhs = [beta[h] * v[h] - both[h][:c] for h in range(nh)]
    qs0 = [t[c:] for t in both]
    pw = [-(beta[h] * kk[h // rep] * jnp.where(r > cc, decay[h], 0.0)) for h in range(nh)]
    delta = _tri_solve(pw, rhs) if solved is None else _tri_solved(pw, rhs, solved)
    last = [_mmf(jnp.concatenate([qk[h // rep] * decay[h], (k[h // rep] * jnp.exp(gc_last[h] - gc_col[h])).T], axis=0), delta[h])
            for h in range(nh)]
    out = [qs0[h] + last[h][:c] for h in range(nh)]
    if out_known is not None:
        out = [_known(out[h], out_known[h]) for h in range(nh)]
    s1 = [jnp.exp(gc_last[h]) * s0[h] + last[h][c:] for h in range(nh)]
    o = [t * lax.rsqrt(jnp.mean(t * t, axis=-1, keepdims=True) + EPS) * ng for t in out]
    return [o[h] * _silu(z[h]) for h in range(nh)], s1, delta, out


def _dn_conv(ext, w_ref):
    y = ext * w_ref[B_CONV - 1:B_CONV, :]
    for j in range(B_CONV - 1):
        y = y + w_ref[j:j + 1, :] * pltpu.roll(ext, B_CONV - 1 - j, 0)
    return y


def _dn_args(y, cur_ref, par_ref, ng_ref):
    nh = B_V_HEADS
    return ([y[:, h * B_HD:(h + 1) * B_HD] for h in range(B_QK_HEADS)],
            [y[:, B_QK + h * B_HD:B_QK + (h + 1) * B_HD] for h in range(B_QK_HEADS)],
            [y[:, 2 * B_QK + h * B_HD:2 * B_QK + (h + 1) * B_HD] for h in range(nh)],
            [cur_ref[:, BP_Z + h * B_HD:BP_Z + (h + 1) * B_HD] for h in range(nh)],
            [cur_ref[:, BP_GATE + h:BP_GATE + h + 1] for h in range(nh)],
            [cur_ref[:, BP_GATE + nh + h:BP_GATE + nh + h + 1] for h in range(nh)],
            [par_ref[:, h:h + 1] for h in range(nh)], [par_ref[:, nh + h:nh + h + 1] for h in range(nh)], ng_ref[...])


def _mix_b_fwd(proj, conv_w, par, ng, memkv, name):
    s = proj.shape[0]
    nc = s // CHUNK

    def body(cur_ref, prev_ref, w_ref, par_ref, ng_ref, memkv_ref, o_ref, st_ref, dl_ref, state_ref):
        n = pl.program_id(0)

        @pl.when(n == 0)
        def _():
            state_ref[...] = jnp.zeros_like(state_ref)

        prev = jnp.where(n > 0, prev_ref[...], 0.0)
        ext = jnp.concatenate([prev, cur_ref[:, :B_QKV]], axis=0)
        y = _dn_conv(ext, w_ref)[HALO:]
        s0 = [state_ref[hv] for hv in range(B_V_HEADS)]
        st_ref[0] = state_ref[...]
        outs, s1, delta, raw = _dn_heads(*_dn_args(y, cur_ref, par_ref, ng_ref), s0)
        for hv in range(B_V_HEADS):
            state_ref[hv] = s1[hv]
            dl_ref[0, hv] = delta[hv]
            dl_ref[0, B_V_HEADS + hv] = raw[hv]
        outs = outs + [_cross_pairs(cur_ref[:, BP_XQ:BP_XQ + X_Q], memkv_ref[:, :X_Q], memkv_ref[:, X_Q:])]
        o_ref[...] = jnp.concatenate(outs, axis=1).astype(o_ref.dtype)

    return pl.pallas_call(
        body, grid=(nc,),
        in_specs=[pl.BlockSpec((CHUNK, IN_BP), lambda n: (n, 0)),
                  pl.BlockSpec((HALO, B_QKV), lambda n: (jnp.maximum(n * (CHUNK // HALO) - 1, 0), 0)),
                  pl.BlockSpec((HALO, B_QKV), lambda n: (0, 0)),
                  pl.BlockSpec((1, 128), lambda n: (0, 0)), pl.BlockSpec((1, 128), lambda n: (0, 0)),
                  pl.BlockSpec((MEM_LEN, 2 * X_Q), lambda n: (0, 0))],
        out_specs=[pl.BlockSpec((CHUNK, D), lambda n: (n, 0)),
                   pl.BlockSpec((1, B_V_HEADS, B_HD, B_HD), lambda n: (n, 0, 0, 0)),
                   pl.BlockSpec((1, 2 * B_V_HEADS, CHUNK, B_HD), lambda n: (n, 0, 0, 0))],
        out_shape=[_SDS((s, D), _ACT), _SDS((nc, B_V_HEADS, B_HD, B_HD), F32), _SDS((nc, 2 * B_V_HEADS, CHUNK, B_HD), F32)],
        scratch_shapes=[pltpu.VMEM((B_V_HEADS, B_HD, B_HD), F32)],
        name=name, compiler_params=_cp("arbitrary"))(proj, proj, conv_w, par, ng, memkv)


def _mix_b_bwd(proj, conv_w, par, ng, memkv, states, deltas, dmix, name):
    s = proj.shape[0]
    nc = s // CHUNK
    ext_rows = CHUNK + HALO

    def body(cur_ref, prev_ref, w_ref, par_ref, ng_ref, memkv_ref, st_ref, dl_ref, dmix_ref,
             dproj_ref, dw_ref, dpar_ref, dng_ref, dmemkv_ref, dstate_ref, carry_ref):
        t = pl.program_id(0)
        n = nc - 1 - t

        @pl.when(t == 0)
        def _():
            dstate_ref[...] = jnp.zeros_like(dstate_ref)
            carry_ref[...] = jnp.zeros_like(carry_ref)
            dw_ref[...] = jnp.zeros_like(dw_ref)
            dpar_ref[...] = jnp.zeros_like(dpar_ref)
            dng_ref[...] = jnp.zeros_like(dng_ref)
            dmemkv_ref[...] = jnp.zeros_like(dmemkv_ref)

        lane = lax.broadcasted_iota(jnp.int32, (1, 128), 1)
        prev = jnp.where(n > 0, prev_ref[...], 0.0)
        ext = jnp.concatenate([prev, cur_ref[:, :B_QKV]], axis=0)
        y = _dn_conv(ext, w_ref)[HALO:]
        solved = [dl_ref[0, hv] for hv in range(B_V_HEADS)]
        raw = [dl_ref[0, B_V_HEADS + hv] for hv in range(B_V_HEADS)]
        _, vjp = jax.vjp(functools.partial(_dn_heads, solved=solved, out_known=raw), *_dn_args(y, cur_ref, par_ref, ng_ref),
                         [st_ref[0, hv] for hv in range(B_V_HEADS)])
        none = [jnp.zeros((CHUNK, B_HD), F32)] * B_V_HEADS
        dyq, dyk, dyv, dz, gbl, gal, ga_log, gdtb, dng, gs0 = vjp(
            ([dmix_ref[:, hv * B_HD:(hv + 1) * B_HD].astype(F32) for hv in range(B_V_HEADS)],
             [dstate_ref[hv] for hv in range(B_V_HEADS)], none, none))
        dgate = jnp.zeros((CHUNK, 128), F32)
        dpar = jnp.zeros((1, 128), F32)
        for hv in range(B_V_HEADS):
            dstate_ref[hv] = gs0[hv]
            dgate = dgate + jnp.where(lane == hv, gbl[hv], 0.0) + jnp.where(lane == B_V_HEADS + hv, gal[hv], 0.0)
            dpar = dpar + jnp.where(lane == hv, ga_log[hv], 0.0) + jnp.where(lane == B_V_HEADS + hv, gdtb[hv], 0.0)
        dpar_ref[...] += dpar
        dng_ref[...] += dng
        _, vjp = jax.vjp(_cross_pairs, cur_ref[:, BP_XQ:BP_XQ + X_Q], memkv_ref[:, :X_Q], memkv_ref[:, X_Q:])
        dxq, dmk, dmv = vjp(dmix_ref[:, B_V:].astype(F32))
        dmemkv_ref[...] += jnp.concatenate([dmk, dmv], axis=1)
        dy = jnp.concatenate(list(dyq) + list(dyk) + list(dyv), axis=1)
        dy_ext = jnp.concatenate([jnp.zeros((HALO, B_QKV), F32), dy], axis=0)
        dext = dy_ext * w_ref[B_CONV - 1:B_CONV, :]
        dw_ref[B_CONV - 1:B_CONV, :] += jnp.sum(ext * dy_ext, axis=0, keepdims=True)
        for j in range(B_CONV - 1):
            sh = B_CONV - 1 - j
            dw_ref[j:j + 1, :] += jnp.sum(pltpu.roll(ext, sh, 0) * dy_ext, axis=0, keepdims=True)
            dext = dext + w_ref[j:j + 1, :] * pltpu.roll(dy_ext, ext_rows - sh, 0)
        tail = jnp.concatenate([jnp.zeros((CHUNK - HALO, B_QKV), F32), carry_ref[...]], axis=0)
        dqkv = dext[HALO:] + tail
        carry_ref[...] = dext[:HALO]
        dproj_ref[...] = jnp.concatenate([dqkv] + list(dz) + [dxq, dgate], axis=1).astype(dproj_ref.dtype)

    return pl.pallas_call(
        body, grid=(nc,),
        in_specs=[pl.BlockSpec((CHUNK, IN_BP), lambda t: (nc - 1 - t, 0)),
                  pl.BlockSpec((HALO, B_QKV), lambda t: (jnp.maximum((nc - 1 - t) * (CHUNK // HALO) - 1, 0), 0)),
                  pl.BlockSpec((HALO, B_QKV), lambda t: (0, 0)),
                  pl.BlockSpec((1, 128), lambda t: (0, 0)), pl.BlockSpec((1, 128), lambda t: (0, 0)),
                  pl.BlockSpec((MEM_LEN, 2 * X_Q), lambda t: (0, 0)),
                  pl.BlockSpec((1, B_V_HEADS, B_HD, B_HD), lambda t: (nc - 1 - t, 0, 0, 0)),
                  pl.BlockSpec((1, 2 * B_V_HEADS, CHUNK, B_HD), lambda t: (nc - 1 - t, 0, 0, 0)),
                  pl.BlockSpec((CHUNK, D), lambda t: (nc - 1 - t, 0))],
        out_specs=[pl.BlockSpec((CHUNK, IN_BP), lambda t: (nc - 1 - t, 0)),
                   pl.BlockSpec((HALO, B_QKV), lambda t: (0, 0)),
                   pl.BlockSpec((1, 128), lambda t: (0, 0)), pl.BlockSpec((1, 128), lambda t: (0, 0)),
                   pl.BlockSpec((MEM_LEN, 2 * X_Q), lambda t: (0, 0))],
        out_shape=[_SDS((s, IN_BP), _ACT), _SDS((HALO, B_QKV), F32), _SDS((1, 128), F32), _SDS((1, 128), F32),
                   _SDS((MEM_LEN, 2 * X_Q), F32)],
        scratch_shapes=[pltpu.VMEM((B_V_HEADS, B_HD, B_HD), F32), pltpu.VMEM((HALO, B_QKV), F32)],
        name=name, compiler_params=_cp("arbitrary"))(proj, proj, conv_w, par, ng, memkv, states, deltas, dmix)


def _place():
    return lax.axis_index("x"), lax.axis_index("y"), lax.axis_index("c")


def _all_gather(shards, name):
    n = len(shards)

    def body(*refs):
        ins, outs = refs[:n], refs[n:2 * n]
        send_sems, recv_sems, local_sems = refs[2 * n:]
        x, y, c = _place()
        me, sibling = (x, y, c), (x, y, 1 - c)
        chips = [(1 - x, y), (x, 1 - y), (1 - x, 1 - y)]

        def rows(a, px, py, pc):
            return outs[a].at[4 * px + 2 * py + pc]

        def copy(a, k, block, to, src=None):
            return pltpu.make_async_remote_copy(
                src_ref=rows(a, *block) if src is None else src, dst_ref=rows(a, *block),
                send_sem=send_sems.at[a, k], recv_sem=recv_sems.at[a, k],
                device_id=to, device_id_type=pl.DeviceIdType.MESH)

        mine = [pltpu.make_async_copy(ins[a], rows(a, *me), local_sems.at[a]) for a in range(n)]
        for cp in mine:
            cp.start()
        first = []
        for a in range(n):
            first.append(copy(a, 0, me, sibling, src=ins[a]))
            first += [copy(a, 1 + j, me, (*chip, c), src=ins[a]) for j, chip in enumerate(chips)]
        for cp in first:
            cp.start()
        passed = []
        for j, chip in enumerate(chips):
            for a in range(n):
                copy(a, 1 + j, (*chip, c), me).wait_recv()
                fwd = copy(a, 4 + j, (*chip, c), sibling)
                fwd.start()
                passed.append(fwd)
        for a in range(n):
            copy(a, 0, sibling, me).wait_recv()
            for j, chip in enumerate(chips):
                copy(a, 4 + j, (*chip, 1 - c), me).wait_recv()
        for cp in first + passed:
            cp.wait_send()
        for cp in mine:
            cp.wait()

    hbm = pl.BlockSpec(memory_space=pl.ANY)
    return pl.pallas_call(
        body, out_shape=[_SDS((N_DEV,) + s.shape, s.dtype) for s in shards],
        in_specs=[hbm] * n, out_specs=[hbm] * n,
        scratch_shapes=[pltpu.SemaphoreType.DMA((n, 7)), pltpu.SemaphoreType.DMA((n, 7)), pltpu.SemaphoreType.DMA((n,))],
        name=name)(*shards)


class _Exchange:
    def __init__(self, lands, srcs):
        self.lands, self.srcs = lands, srcs


def _seq_exchange(srcs, land_shapes, plan, name, cid):
    n, nl = len(srcs), len(land_shapes)

    def launch(*refs):
        src_refs, land_refs = refs[:n], refs[n:n + nl]
        send_sems, recv_sems, local_sems = refs[n + nl:]
        x, y, c = _place()
        my = 4 * x + 2 * y + c
        peers = [(x ^ ((k + 1) >> 2 & 1), y ^ ((k + 1) >> 1 & 1), c ^ ((k + 1) & 1)) for k in range(N_DEV - 1)]
        barrier = pltpu.get_barrier_semaphore()
        for p in peers:
            pl.semaphore_signal(barrier, inc=1, device_id=p, device_id_type=pl.DeviceIdType.MESH)
        pl.semaphore_wait(barrier, N_DEV - 1)

        def src_for(a, dest):
            return src_refs[a].at[dest] if plan[a][1] else src_refs[a]

        def slot(a, source):
            return land_refs[plan[a][0]].at[source]

        mine = [pltpu.make_async_copy(src_for(a, my), slot(a, my), local_sems.at[a]) for a in range(n)]
        for cp in mine:
            cp.start()
        sends, recvs = [], []
        for k, (px, py, pc) in enumerate(peers):
            peer = 4 * px + 2 * py + pc
            for a in range(n):
                kw = dict(send_sem=send_sems.at[a * (N_DEV - 1) + k], recv_sem=recv_sems.at[a * (N_DEV - 1) + k],
                          device_id=(px, py, pc), device_id_type=pl.DeviceIdType.MESH)
                sends.append(pltpu.make_async_remote_copy(src_ref=src_for(a, peer), dst_ref=slot(a, my), **kw))
                recvs.append(pltpu.make_async_remote_copy(src_ref=src_for(a, my), dst_ref=slot(a, peer), **kw))
        for cp in sends:
            cp.start()
        for cp in recvs:
            cp.wait_recv()
        for cp in sends:
            cp.wait_send()
        for cp in mine:
            cp.wait()

    lands = pl.kernel(
        launch, out_type=[_SDS(s, d) for s, d in land_shapes],
        mesh=plsc.ScalarSubcoreMesh(axis_name="sequencer", num_cores=1), name=name,
        scratch_types=(pltpu.SemaphoreType.DMA((n * (N_DEV - 1),)), pltpu.SemaphoreType.DMA((n * (N_DEV - 1),)),
                       pltpu.SemaphoreType.DMA((n,))),
        compiler_params=pltpu.CompilerParams(collective_id=cid))(*srcs)
    return _Exchange(list(lands), list(srcs))


def _adam_update(g, w, m, v):
    c1 = 1.0 - ADAM_B1 ** ADAM_STEP
    c2 = 1.0 - ADAM_B2 ** ADAM_STEP
    mm = ADAM_B1 * m + (1.0 - ADAM_B1) * g
    vv = ADAM_B2 * v + (1.0 - ADAM_B2) * (g * g)
    delta = -ADAM_LR * ((mm / c1) / (jnp.sqrt(vv / c2) + ADAM_EPS) + ADAM_WD * w)
    return delta, mm, vv


def _sum_sources(p_ref):
    g = p_ref[0].astype(F32)
    for s in range(1, N_DEV):
        g = g + p_ref[s].astype(F32)
    return g


def _adamw(parts, w, m, v, tr, name, restore_b=False, deps=()):
    nl, r, c = w.shape
    cp = parts[0].shape[-1]

    def body(*refs):
        p_refs = refs[:nl]
        w_ref, m_ref, v_ref = refs[nl:nl + 3]
        g_ref, d_ref, nm_ref, nv_ref = refs[-4:]
        g = _sum_sources(p_refs[0])
        for l in range(1, nl):
            g = jnp.where(pl.program_id(0) == l, _sum_sources(p_refs[l]), g)
        if restore_b:
            g = jnp.concatenate([g[:, :BP_XQ], g[:, BP_GATE:BP_GATE + 2 * B_V_HEADS], g[:, BP_XQ:BP_GATE]], axis=1)
        delta, mm, vv = _adam_update(g, w_ref[...], m_ref[...], v_ref[...])
        g_ref[...] = g
        d_ref[...] = delta
        nm_ref[...] = mm
        nv_ref[...] = vv

    spec = pl.BlockSpec((None, tr, c), lambda l, i: (l, i, 0))
    part_specs = [pl.BlockSpec((N_DEV, tr, cp), functools.partial(lambda l, i, k: (0, jnp.where(l == k, i, 0), 0), k=k))
                  for k in range(nl)]
    return pl.pallas_call(
        body, grid=(nl, r // tr),
        in_specs=part_specs + [spec, spec, spec] + _dep_specs(deps),
        out_specs=[spec] * 4, out_shape=[_SDS(w.shape, F32)] * 4,
        name=name, compiler_params=_cp("arbitrary", "arbitrary"))(*parts, w, m, v, *deps)


def _pack_small(d_rel, d_cb, d_cw, d_qkv, d_mix, d_mem, d_ffn, d_final, d_sinks, d_par, d_ng, loss_row, name):
    flat = [d_rel, *d_cb, *d_cw, d_qkv, *d_mix, *d_mem, *d_ffn, d_final, d_sinks, d_par, d_ng, loss_row]
    n = len(flat)

    def body(*refs):
        ins, o_ref = refs[:n], refs[n]
        rel, cb0, cb1, cw0, cw1, qkv, mx0, mx1, me0, me1, ff0, ff1, fin, snk, par, ng, lss = ins
        o_ref[...] = jnp.zeros_like(o_ref)
        for k in range(N_BUCKETS):
            lane = SP_REL_LANE + 128 * (k % 8)
            o_ref[SP_QKV + k // 8:SP_QKV + k // 8 + 1, lane:lane + 128] = rel[k:k + 1, :]
        for l, (cb, cw) in enumerate(((cb0, cw0), (cb1, cw1))):
            o_ref[SP_CB + l:SP_CB + l + 1, :] = jnp.concatenate([cb[j] for j in range(FF_BLOCKS)], axis=1)
            full = jnp.concatenate([cw[j] for j in range(FF_BLOCKS)], axis=1)
            o_ref[SP_CW + FFN_CONV * l:SP_CW + FFN_CONV * (l + 1), :] = full[:FFN_CONV]
        o_ref[SP_QKV:SP_QKV + B_CONV, 0:B_QKV] = qkv[0:B_CONV, :]
        for base, pair in ((SP_MIX, (mx0, mx1)), (SP_MEM, (me0, me1)), (SP_FFN, (ff0, ff1))):
            for l in range(2):
                o_ref[base + l:base + l + 1, 0:D] = pair[l][...]
        o_ref[SP_FINAL:SP_FINAL + 1, 0:D] = fin[...]
        o_ref[SP_MISC:SP_MISC + 1, 0:128] = snk[...]
        o_ref[SP_MISC:SP_MISC + 1, 128:256] = par[...]
        o_ref[SP_MISC:SP_MISC + 1, 256:384] = ng[...]
        o_ref[SP_MISC:SP_MISC + 1, 384:512] = lss[...]

    vm = pl.BlockSpec(memory_space=pltpu.VMEM)
    return pl.pallas_call(body, in_specs=[vm] * n, out_specs=vm, out_shape=_SDS((SMALL_ROWS, D_FF), F32), name=name)(*flat)


_SMALL = ["rel_bias", "norm_mix_g", "norm_mem_g", "sinks_a", "a_log_b", "dt_bias_b", "out_norm_g_b", "norm_ffn_g",
          "ffn_conv_b", "final_norm_g", "conv_qkv_b", "ffn_conv_w"]


def _adamw_small(recv, rc_qkv, rc_ffn, ws, ms, vs, name, deps=()):
    n = len(_SMALL)

    def body(*refs):
        recv_ref, qkv_ref, ffn_ref = refs[:3]
        w_refs, m_refs, v_refs = refs[3:3 + n], refs[3 + n:3 + 2 * n], refs[3 + 2 * n:3 + 3 * n]
        outs, loss_ref = refs[len(refs) - 4 * n - 1:len(refs) - 1], refs[-1]
        gs = _sum_sources(recv_ref)
        loss_ref[...] = gs[SP_MISC:SP_MISC + 1, 384:512]
        grads = {
            "rel_bias": jnp.concatenate(
                [gs[SP_QKV + k // 8:SP_QKV + k // 8 + 1, SP_REL_LANE + 128 * (k % 8):SP_REL_LANE + 128 * (k % 8) + A_HEADS]
                 for k in range(N_BUCKETS)], axis=0),
            "norm_mix_g": gs[SP_MIX:SP_MIX + 2, 0:D], "norm_mem_g": gs[SP_MEM:SP_MEM + 2, 0:D],
            "sinks_a": gs[SP_MISC:SP_MISC + 1, 0:A_HEADS],
            "a_log_b": gs[SP_MISC:SP_MISC + 1, 128:128 + B_V_HEADS],
            "dt_bias_b": gs[SP_MISC:SP_MISC + 1, 128 + B_V_HEADS:128 + 2 * B_V_HEADS],
            "out_norm_g_b": gs[SP_MISC:SP_MISC + 1, 256:256 + B_HD],
            "norm_ffn_g": gs[SP_FFN:SP_FFN + 2, 0:D], "ffn_conv_b": gs[SP_CB:SP_CB + 2, :],
            "final_norm_g": gs[SP_FINAL:SP_FINAL + 1, 0:D],
            "conv_qkv_b": _sum_sources(qkv_ref), "ffn_conv_w": _sum_sources(ffn_ref),
        }
        for i, nm in enumerate(_SMALL):
            g = grads[nm]
            delta, mm, vv = _adam_update(g, w_refs[i][...], m_refs[i][...], v_refs[i][...])
            outs[i][...] = g
            outs[n + i][...] = delta
            outs[2 * n + i][...] = mm
            outs[3 * n + i][...] = vv

    vm = pl.BlockSpec(memory_space=pltpu.VMEM)
    shapes = [_SDS(w.shape, F32) for w in ws]
    return pl.pallas_call(
        body, in_specs=[vm] * (3 + 3 * n) + _dep_specs(deps), out_specs=[vm] * (4 * n + 1),
        out_shape=shapes * 4 + [_SDS((1, 128), F32)],
        name=name)(recv, rc_qkv, rc_ffn, *ws, *ms, *vs, *deps)


def _assemble(gathered, axis):
    g = jnp.moveaxis(gathered, 0, axis)
    shp = list(g.shape)
    return g.reshape(shp[:axis] + [shp[axis] * shp[axis + 1]] + shp[axis + 2:])


def _pad_rows(a, rows):
    return jnp.pad(a, ((0, rows - a.shape[0]), (0, 0)))


def _pad_lanes(a, lanes=128):
    return jnp.pad(a, ((0, 0), (0, lanes - a.shape[1])))


def _ff_blocks(a):
    return jnp.moveaxis(a.reshape(a.shape[0], FF_BLOCKS, GU_SHARD), 1, 0)


def _reorder_b(w):
    qkv_z = w[..., :B_QKV + B_V]
    gates = w[..., B_QKV + B_V:B_QKV + B_V + 2 * B_V_HEADS]
    xq = w[..., IN_B - X_Q:]
    pad = jnp.zeros(w.shape[:-1] + (IN_BP - IN_B,), w.dtype)
    return jnp.concatenate([qkv_z, xq, gates, pad], axis=-1)


def kernel(x, mem, rel_bias, norm_mix_g, norm_mem_g, w_mem_kv, w_out, w_in_a, sinks_a, w_in_b, conv_qkv_b, a_log_b, dt_bias_b, out_norm_g_b, norm_ffn_g, w_gate_up, ffn_conv_w, ffn_conv_b, w_down, final_norm_g, loss_target, m_rel_bias, m_norm_mix_g, m_norm_mem_g, m_w_mem_kv, m_w_out, m_w_in_a, m_sinks_a, m_w_in_b, m_conv_qkv_b, m_a_log_b, m_dt_bias_b, m_out_norm_g_b, m_norm_ffn_g, m_w_gate_up, m_ffn_conv_w, m_ffn_conv_b, m_w_down, m_final_norm_g, v_rel_bias, v_norm_mix_g, v_norm_mem_g, v_w_mem_kv, v_w_out, v_w_in_a, v_sinks_a, v_w_in_b, v_conv_qkv_b, v_a_log_b, v_dt_bias_b, v_out_norm_g_b, v_norm_ffn_g, v_w_gate_up, v_ffn_conv_w, v_ffn_conv_b, v_w_down, v_final_norm_g):
    local = dict(locals())
    order = ["rel_bias", "norm_mix_g", "norm_mem_g", "w_mem_kv", "w_out", "w_in_a", "sinks_a", "w_in_b", "conv_qkv_b",
             "a_log_b", "dt_bias_b", "out_norm_g_b", "norm_ffn_g", "w_gate_up", "ffn_conv_w", "ffn_conv_b", "w_down",
             "final_norm_g"]
    wts = {n: local[n] for n in order}
    moms = {n: local["m_" + n] for n in order}
    vars_ = {n: local["v_" + n] for n in order}
    h0 = x[0]
    memx = mem[0]
    tgt = loss_target[0]
    s = h0.shape[0]
    tm = _rows(s)
    tb = min(s, _TM_BIG)

    t_ = lambda a: jnp.swapaxes(a, 1, 2)
    g_mk0, g_out0, g_ia, g_cq, g_cw = _all_gather(
        [w_mem_kv[0:1].astype(_MXU), w_out[0:1].astype(_MXU), t_(w_in_a).astype(_MXU), conv_qkv_b, ffn_conv_w], "gather_first")
    g_mk, g_out = [g_mk0], [g_out0]
    gu_land = ((N_DEV, GU_SHARD, D), _MXU)
    dn_land = ((N_DEV, DN_SHARD, D), _MXU)
    whole = [(0, False), (1, False)]
    def after(a, b):
        return a + (b[(0,) * b.ndim] * 0).astype(a.dtype)

    gu0_w = _seq_exchange([after(t_(w_gate_up)[0].astype(_MXU), g_ia)], [gu_land], [(0, False)], "gather_gate_up0", 1)
    dn0_w = _seq_exchange([after(w_down[0].astype(_MXU), g_ia)], [dn_land], [(0, False)], "gather_down0", 8)
    w_ia = g_ia.reshape(IN_A, D)
    conv_qkv = _pad_rows(_assemble(g_cq, 2)[0], HALO)
    ffn_cw_full = _assemble(g_cw, 2)
    ffn_cw = [_ff_blocks(_pad_rows(ffn_cw_full[i], HALO)) for i in range(2)]
    ffn_cb = [_ff_blocks(ffn_conv_b[i:i + 1]) for i in range(2)]
    bucket = jnp.asarray(_bucket_table())
    bias = _bias_build(rel_bias, bucket, "bias_build")
    sinks = _pad_lanes(sinks_a)
    par_b = _pad_lanes(jnp.concatenate([a_log_b, dt_bias_b], axis=1))

    row_x = pl.BlockSpec((tm, D), lambda i, j: (i, 0))
    gu_shape = (2, FF_BLOCKS, s, GU_SHARD)

    def in_proj(h, g, w, w_spec, n_cols, tn, name, deps=(), out_dtype=F32, w_t=False, tm=None):
        return _norm_matmul(h, g, w, w_spec, n_cols // tn, (h.shape[0], n_cols),
                            pl.BlockSpec((tm or _rows(h.shape[0]), tn), lambda i, j: (i, j)), name, deps=deps, out_dtype=out_dtype,
                            w_t=w_t, tm=tm)

    def ffn_fwd(i, h, g_gu, g_dn, deps=()):
        gu, hn = _norm_matmul(h, norm_ffn_g[i:i + 1], g_gu, _spec_gate_up(1), N_DEV, gu_shape,
                              _spec_gu_act(0, 1, tb), f"gate_up_{i}", deps=deps, out_dtype=_ACT, w_t=True, tm=tb)
        h_new, act, gc = _glu_down(gu, ffn_cw[i], ffn_cb[i], g_dn, h, f"glu_down_{i}")
        return h_new, gu, hn, (act, gc)

    def out_proj(i, mix, h):
        return _matmul_res(mix, row_x, g_out[i], _spec_rowsharded(0, D // N_DEV, D), 1, h, f"out_proj_{i}")

    proj_a, hn_a = in_proj(h0, norm_mix_g[0:1], w_ia, pl.BlockSpec((640, D), lambda i, j: (j, 0)), IN_A, 640, "in_proj_a",
                           deps=gu0_w.srcs + dn0_w.srcs, out_dtype=_ACT, w_t=True)
    memkv0, memn0 = in_proj(memx, norm_mem_g[0:1], g_mk[0], _spec_rowsharded(0, D // N_DEV, 2 * X_Q), 2 * X_Q, 2 * X_Q, "mem_proj_0")
    mix_a = _mix_a_fwd(proj_a, bias, sinks, memkv0, "mix_a_fwd")
    h1 = out_proj(0, mix_a, h0)
    g_gu0, g_dn0 = gu0_w.lands[0], dn0_w.lands[0]
    in_b_w = _seq_exchange([after(_reorder_b(w_in_b).astype(_MXU), h1), after(w_mem_kv[1:2].astype(_MXU), h1),
                            after(w_out[1:2].astype(_MXU), h1)],
                           [((N_DEV, 1, D // N_DEV, IN_BP), _MXU), ((N_DEV, 1, D // N_DEV, 2 * X_Q), _MXU),
                            ((N_DEV, 1, D // N_DEV, D), _MXU)], [(0, False), (1, False), (2, False)], "gather_in_b", 2)
    ffn1_w = _seq_exchange([after(t_(w_gate_up)[1].astype(_MXU), h1), after(w_down[1].astype(_MXU), h1)], [gu_land, dn_land], whole,
                           "gather_ffn1", 3)
    h2, gu0, hn_f0, act0 = ffn_fwd(0, h1, g_gu0, g_dn0, deps=in_b_w.srcs + ffn1_w.srcs)
    g_ib, g_mk1, g_out1 = in_b_w.lands
    g_mk.append(g_mk1)
    g_out.append(g_out1)
    proj_b, hn_b = in_proj(h2, norm_mix_g[1:2], g_ib, _spec_rowsharded(0, D // N_DEV, 896, col_block=1), IN_BP, 896, "in_proj_b")
    memkv1, memn1 = in_proj(memx, norm_mem_g[1:2], g_mk[1], _spec_rowsharded(0, D // N_DEV, 2 * X_Q), 2 * X_Q, 2 * X_Q, "mem_proj_1",
                            deps=[h2])
    mix_b, states, deltas = _mix_b_fwd(proj_b, conv_qkv, par_b, out_norm_g_b, memkv1, "mix_b_fwd")
    h3 = out_proj(1, mix_b, h2)
    g_gu1, g_dn1 = ffn1_w.lands
    h4, gu1, hn_f1, act1 = ffn_fwd(1, h3, g_gu1, g_dn1)
    loss_row, *dh, d_final_g = _loss_head(h4, final_norm_g[None, :], tgt, "loss_head")

    zeros_mem = jnp.zeros_like(memx)
    per_dest2 = [(0, True), (1, True)]

    def ffn_bwd(i, dh, h_in, gu, hn_f, act_gc, g_gu, g_dn, deps=()):
        act, gc = act_gc
        dgu, d_cw, d_cb = _glu_bwd(gu, gc, ffn_cw[i], dh[1], g_dn, f"glu_bwd_{i}", deps=deps)
        d_wdown = _matmul_tn(act, pl.BlockSpec((None, tb, GU_SHARD), lambda j, r: (j, r, 0)),
                             dh[1], pl.BlockSpec((tb, D), lambda j, r: (r, 0)), s, FF_BLOCKS, (GU_SHARD, D),
                             (N_DEV, DN_SHARD, D), pl.BlockSpec((2, DN_SHARD, D), lambda j, r: (j, 0, 0)), f"d_w_down_{i}",
                             tm=tb)
        *dh_new, d_g = _matmul_nt_normbwd(dgu, _spec_gu_act(0, 1, tm), g_gu, _spec_gate_up(1), N_DEV, h_in,
                                          norm_ffn_g[i:i + 1], dh[0], f"d_ffn_in_{i}", w_t=True, act_copy=True)
        d_wgu = _matmul_tn(dgu, _spec_gu_act(1, 0, tb), hn_f, pl.BlockSpec((tb, D), lambda j, r: (r, 0)), s, N_DEV,
                           (GU_SHARD, D), (N_DEV, GU_SHARD, D), pl.BlockSpec((None, GU_SHARD, D), lambda j, r: (j, 0, 0)),
                           f"d_w_gate_up_{i}", tm=tb)
        return dh_new, [d_wdown, d_wgu], d_cw, d_cb, d_g

    def out_bwd(i, dh, mix, deps):
        dmix = _matmul_nt(dh[1], g_out[i], _spec_rowsharded(0, D // N_DEV, D), 1, (s, D), row_x, f"d_mix_{i}", deps=deps, out_dtype=_ACT)
        d_wout = _matmul_tn(mix, pl.BlockSpec((tb, D), lambda j, r: (r, 0)), dh[1], pl.BlockSpec((tb, D), lambda j, r: (r, 0)),
                            s, 1, (D, D), (N_DEV, D // N_DEV, D), pl.BlockSpec((N_DEV, D // N_DEV, D), lambda j, r: (0, 0, 0)),
                            f"d_w_out_{i}", tm=tb)
        return dmix, d_wout

    def mem_bwd(i, dmemkv, memn):
        tmm = _rows(MEM_LEN)
        *_, d_g = _matmul_nt_normbwd(dmemkv, pl.BlockSpec((tmm, 2 * X_Q), lambda r, j: (r, 0)), g_mk[i],
                                     _spec_rowsharded(0, D // N_DEV, 2 * X_Q), 1, memx, norm_mem_g[i:i + 1], zeros_mem,
                                     f"d_mem_in_{i}")
        by_row = lambda j, r: (r, 0)
        d_w = _matmul_tn(memn, pl.BlockSpec((tmm, D), by_row), dmemkv, pl.BlockSpec((tmm, 2 * X_Q), by_row), MEM_LEN, 1,
                         (D, 2 * X_Q), (N_DEV, D // N_DEV, 2 * X_Q),
                         pl.BlockSpec((N_DEV, D // N_DEV, 2 * X_Q), lambda j, r: (0, 0, 0)), f"d_w_mem_kv_{i}")
        return d_w, d_g

    out_land = ((N_DEV, D // N_DEV, D), _WIRE)
    mk_land = ((N_DEV, D // N_DEV, 2 * X_Q), _WIRE)
    ffn_lands = [((N_DEV, DN_SHARD, D), _WIRE), ((N_DEV, GU_SHARD, D), _WIRE)]
    dh, d_ffn1, d_cw1, d_cb1, d_gf1 = ffn_bwd(1, dh, h3, gu1, hn_f1, act1, g_gu1, g_dn1)
    ffn1_g = _seq_exchange(d_ffn1, ffn_lands, per_dest2, "send_ffn1_grads", 5)
    dmix, d_wout1 = out_bwd(1, dh, mix_b, ffn1_g.srcs)
    dproj_b, d_convw, d_par, d_ng, dmemkv1 = _mix_b_bwd(proj_b, conv_qkv, par_b, out_norm_g_b, memkv1, states, deltas, dmix, "mix_b_bwd")
    *dh, d_gm1 = _matmul_nt_normbwd(dproj_b, pl.BlockSpec((tm, 896), lambda i, j: (i, j)), g_ib,
                                    _spec_rowsharded(0, D // N_DEV, 896, col_block=1), IN_BP // 896, h2, norm_mix_g[1:2], dh[0],
                                    "d_in_b", act_copy=True)
    d_wib = _matmul_tn(hn_b, pl.BlockSpec((tb, D), lambda j, r: (r, 0)), dproj_b, pl.BlockSpec((tb, 896), lambda j, r: (r, j)),
                       s, IN_BP // 896, (D, 896), (N_DEV, D // N_DEV, IN_BP),
                       pl.BlockSpec((N_DEV, D // N_DEV, 896), lambda j, r: (0, 0, j)), "d_w_in_b", tm=tb)
    d_wmk1, d_gmem1 = mem_bwd(1, dmemkv1, memn1)
    mix1_g = _seq_exchange([d_wout1, d_wib, d_wmk1], [out_land, ((N_DEV, D // N_DEV, IN_BP), _WIRE), mk_land],
                           [(0, True), (1, True), (2, True)], "send_mix1_grads", 6)
    dh, d_ffn0, d_cw0, d_cb0, d_gf0 = ffn_bwd(0, dh, h1, gu0, hn_f0, act0, g_gu0, g_dn0, deps=mix1_g.srcs)
    dmix, d_wout0 = out_bwd(0, dh, mix_a, d_ffn0 + ffn1_g.lands[:1])
    ffn0_g = _seq_exchange(d_ffn0 + [d_wout0], ffn_lands + [out_land], per_dest2 + [(2, True)], "send_ffn0_grads", 4)
    dproj_a, dbias, dsinks, dmemkv0 = _mix_a_bwd(proj_a, bias, sinks, memkv0, dmix, "mix_a_bwd", deps=ffn0_g.srcs)
    dx, _, d_gm0 = _matmul_nt_normbwd(dproj_a, pl.BlockSpec((tm, 640), lambda i, j: (i, j)), w_ia,
                                      pl.BlockSpec((640, D), lambda i, j: (j, 0)), IN_A // 640, h0, norm_mix_g[0:1], dh[0],
                                      "d_in_a", w_t=True)
    d_wia = _matmul_tn(dproj_a, pl.BlockSpec((tb, IN_A), lambda j, r: (r, 0)), hn_a, pl.BlockSpec((tb, D), lambda j, r: (r, 0)),
                       s, 1, (IN_A, D), (N_DEV, IA_SHARD, D), pl.BlockSpec((N_DEV, IA_SHARD, D), lambda j, r: (0, 0, 0)),
                       "d_w_in_a", tm=tb)
    d_wmk0, d_gmem0 = mem_bwd(0, dmemkv0, memn0)
    d_rel = _bias_reduce(dbias, bucket, "bias_reduce")
    small = _pack_small(d_rel, (d_cb0, d_cb1), (d_cw0, d_cw1), d_convw, (d_gm0, d_gm1), (d_gmem0, d_gmem1),
                        (d_gf0, d_gf1), d_final_g, dsinks, d_par, d_ng, loss_row, "pack_small")
    mix0_g = _seq_exchange([d_wia, d_wmk0, small],
                           [((N_DEV, IA_SHARD, D), _WIRE), mk_land, ((N_DEV, SMALL_ROWS, D_FF), F32)],
                           [(0, True), (1, True), (2, False)], "send_mix0_grads", 7)

    res = {}
    last = []

    def update(nm, parts, tr, restore=False, transposed=False):
        view = t_ if transposed else (lambda a: a)
        out = _adamw(parts, view(wts[nm]), view(moms[nm]), view(vars_[nm]), tr, "adamw_" + nm, restore_b=restore, deps=last[-1:])
        res[nm] = [view(o) for o in out]
        last.append(out[1])

    r_dn1, r_gu1 = ffn1_g.lands
    r_dn0, r_gu0, r_out0 = ffn0_g.lands
    r_out1, r_ib, r_mk1 = mix1_g.lands
    update("w_in_b", [r_ib], 32, True)
    update("w_gate_up", [r_gu0, r_gu1], 176, transposed=True)
    update("w_down", [r_dn0, r_dn1], 176)
    r_ia, r_mk0, r_small = mix0_g.lands
    update("w_mem_kv", [r_mk0, r_mk1], 128)
    update("w_out", [r_out0, r_out1], 128)
    update("w_in_a", [r_ia], IA_SHARD, transposed=True)

    my = 4 * lax.axis_index("x") + 2 * lax.axis_index("y") + lax.axis_index("c")
    cq = conv_qkv_b.shape[-1]
    cf = ffn_conv_w.shape[-1]
    rc_qkv = lax.dynamic_slice_in_dim(r_small[:, SP_QKV:SP_QKV + B_CONV, :B_QKV], my * cq, cq, axis=2)[:, None]
    rc_ffn = lax.dynamic_slice_in_dim(r_small[:, SP_CW:SP_CW + 2 * FFN_CONV, :], my * cf, cf, axis=2).reshape(N_DEV, 2, FFN_CONV, cf)
    as2d = lambda a: a[None, :] if a.ndim == 1 else a
    small_out = _adamw_small(r_small, rc_qkv, rc_ffn, [as2d(wts[n]) for n in _SMALL], [as2d(moms[n]) for n in _SMALL],
                             [as2d(vars_[n]) for n in _SMALL], "adamw_small", deps=last[-1:])
    ns = len(_SMALL)
    for i, nm in enumerate(_SMALL):
        res[nm] = [small_out[k * ns + i].reshape(wts[nm].shape) for k in range(4)]

    return (small_out[-1][0, 0], dx[None], *[res[n][0] for n in order], *[res[n][1] for n in order],
            *[res[n][2] for n in order], *[res[n][3] for n in order])
```

```python
import functools
import math

import numpy as np

import jax
import jax.numpy as jnp
from jax import lax
from jax.experimental import pallas as pl
from jax.experimental.pallas import tpu as pltpu
from jax.experimental.pallas import tpu_sc as plsc

F32 = jnp.float32
_MXU = jnp.bfloat16
_ACT = jnp.bfloat16
_WIRE = jnp.bfloat16
_HI = lax.Precision.HIGH
_TM = 1024
_TM_GLU = 1024
_TM_BIG = 2048
_VMEM_LIMIT = 48 * 1024 * 1024
_SDS = jax.ShapeDtypeStruct

D = 1024
EPS = 1e-6
A_HEADS, A_KV_HEADS, A_HD, BLK = 12, 2, 64, 128
N_BUCKETS, MAX_DISTANCE = 32, 128
B_QK_HEADS, B_V_HEADS, B_HD, B_CONV, CHUNK = 3, 6, 128, 4, 64
X_HEADS, X_HD, MEM_LEN = 4, 64, 256
D_FF, FFN_CONV = 2816, 3
A_Q, A_KV, X_Q = 768, 128, 256
B_QK, B_V, B_QKV = 384, 768, 1536
IN_A, IN_B = 1280, 2572
IN_BP = 2688
BP_Z, BP_XQ, BP_GATE = 1536, 2304, 2560
HALO = 8
GLU_HALO = 16

N_DEV = 8
GU_SHARD = 2 * D_FF // N_DEV
FF_BLOCKS = D_FF // GU_SHARD
DN_SHARD = D_FF // N_DEV
IA_SHARD = IN_A // N_DEV

ADAM_LR, ADAM_B1, ADAM_B2, ADAM_EPS, ADAM_WD, ADAM_STEP = 0.001, 0.9, 0.999, 1e-08, 0.01, 10

SP_CB, SP_CW, SP_QKV, SP_MIX, SP_MEM, SP_FFN, SP_FINAL, SP_MISC, SMALL_ROWS = 0, 2, 8, 12, 14, 16, 18, 19, 24
SP_REL_LANE = B_QKV


def _cp(*sems):
    return pltpu.CompilerParams(dimension_semantics=sems, vmem_limit_bytes=_VMEM_LIMIT)


def _mm(a, b):
    return jnp.dot(a.astype(_MXU), b.astype(_MXU), preferred_element_type=F32)


def _mm_nt(a, b):
    return lax.dot_general(a.astype(_MXU), b.astype(_MXU), (((1,), (1,)), ((), ())), preferred_element_type=F32)


def _mm_tn(a, b):
    return lax.dot_general(a.astype(_MXU), b.astype(_MXU), (((0,), (0,)), ((), ())), preferred_element_type=F32)


def _mmf(a, b):
    return jnp.dot(a, b, preferred_element_type=F32, precision=_HI)


def _mmf_nt(a, b):
    return lax.dot_general(a, b, (((1,), (1,)), ((), ())), preferred_element_type=F32, precision=_HI)


def _silu(x):
    return x * jax.nn.sigmoid(x)


def _w2d(ref):
    v = ref[...]
    return v.reshape(-1, v.shape[-1])


def _rows(m):
    return min(m, _TM)


def _spec_rowsharded(layer, rows, cols, col_block=None):
    if col_block is None:
        return pl.BlockSpec((N_DEV, None, rows, cols), lambda *_: (0, layer, 0, 0))
    return pl.BlockSpec((N_DEV, None, rows, cols), lambda *ids: (0, layer, 0, ids[col_block]))


def _spec_gate_up(axis):
    return pl.BlockSpec((None, GU_SHARD, D), lambda *ids: (ids[axis], 0, 0))


def _spec_down(axis):
    return pl.BlockSpec((2, DN_SHARD, D), lambda *ids: (ids[axis], 0, 0))


def _dep_specs(deps):
    return [pl.BlockSpec(memory_space=pl.ANY) for d in deps]


def _spec_gu_act(row_axis, axis, tm):
    return pl.BlockSpec((None, None, tm, GU_SHARD), lambda *ids: (ids[axis] // FF_BLOCKS, ids[axis] % FF_BLOCKS, ids[row_axis], 0))


def _norm_matmul(x, g, w, w_spec, n_blocks, out_shape, out_spec, name, deps=(), out_dtype=F32, w_t=False, tm=None):
    m, k = x.shape
    tm = tm or _rows(m)

    def body(x_ref, g_ref, w_ref, *rest):
        y_ref, hn_ref = rest[-2:]

        @pl.when(pl.program_id(1) == 0)
        def _():
            xv = x_ref[...]
            r = lax.rsqrt(jnp.mean(xv * xv, axis=-1, keepdims=True) + EPS)
            hn_ref[...] = (xv * r * g_ref[...]).astype(hn_ref.dtype)

        y_ref[...] = (_mm_nt if w_t else _mm)(hn_ref[...], _w2d(w_ref)).astype(y_ref.dtype)

    return pl.pallas_call(
        body, grid=(m // tm, n_blocks),
        in_specs=[pl.BlockSpec((tm, k), lambda i, j: (i, 0)), pl.BlockSpec((1, k), lambda i, j: (0, 0)), w_spec]
        + _dep_specs(deps),
        out_specs=[out_spec, pl.BlockSpec((tm, k), lambda i, j: (i, 0))],
        out_shape=[_SDS(out_shape, out_dtype), _SDS((m, k), _ACT)],
        name=name, compiler_params=_cp("arbitrary", "arbitrary"))(x, g, w, *deps)


def _matmul_res(a, a_spec, w, w_spec, n_k, res, name):
    m, n = res.shape
    tm = _rows(m)

    def body(a_ref, w_ref, r_ref, o_ref):
        part = _mm(a_ref[...], _w2d(w_ref))

        @pl.when(pl.program_id(1) == 0)
        def _():
            o_ref[...] = r_ref[...] + part

        @pl.when(pl.program_id(1) > 0)
        def _():
            o_ref[...] += part

    return pl.pallas_call(
        body, grid=(m // tm, n_k),
        in_specs=[a_spec, w_spec, pl.BlockSpec((tm, n), lambda i, j: (i, 0))],
        out_specs=pl.BlockSpec((tm, n), lambda i, j: (i, 0)),
        out_shape=_SDS((m, n), F32), name=name, compiler_params=_cp("arbitrary", "arbitrary"))(a, w, res)


def _matmul_nt(dy, w, w_spec, n_blocks, out_shape, out_spec, name, deps=(), out_dtype=F32):
    m, n = dy.shape
    tm = _rows(m)

    def body(dy_ref, w_ref, *rest):
        o_ref = rest[-1]
        o_ref[...] = _mm_nt(dy_ref[...], _w2d(w_ref)).astype(o_ref.dtype)

    return pl.pallas_call(
        body, grid=(m // tm, n_blocks),
        in_specs=[pl.BlockSpec((tm, n), lambda i, j: (i, 0)), w_spec] + _dep_specs(deps),
        out_specs=out_spec, out_shape=_SDS(out_shape, out_dtype),
        name=name, compiler_params=_cp("arbitrary", "arbitrary"))(dy, w, *deps)


def _matmul_nt_normbwd(dy, dy_spec, w, w_spec, nj, h, g, dh_in, name, w_t=False, act_copy=False):
    m, k = h.shape
    tm = _rows(m)

    def body(dy_ref, w_ref, h_ref, g_ref, dhin_ref, dh_ref, *rest):
        dg_ref, acc_ref = rest[-2:]
        i, j = pl.program_id(0), pl.program_id(1)

        @pl.when(j == 0)
        def _():
            acc_ref[...] = jnp.zeros_like(acc_ref)

        acc_ref[...] += (_mm if w_t else _mm_nt)(dy_ref[...], _w2d(w_ref))

        @pl.when(j == nj - 1)
        def _():
            xv = h_ref[...]
            r = lax.rsqrt(jnp.mean(xv * xv, axis=-1, keepdims=True) + EPS)
            xh = xv * r
            dhn = acc_ref[...]
            part = jnp.sum(dhn * xh, axis=0, keepdims=True)

            @pl.when(i == 0)
            def _():
                dg_ref[...] = part

            @pl.when(i > 0)
            def _():
                dg_ref[...] += part

            t = dhn * g_ref[...]
            dh = dhin_ref[...] + r * (t - xh * jnp.mean(t * xh, axis=-1, keepdims=True))
            dh_ref[...] = dh
            if act_copy:
                rest[0][...] = dh.astype(_ACT)

    rows = pl.BlockSpec((tm, k), lambda i, j: (i, 0))
    outs = pl.pallas_call(
        body, grid=(m // tm, nj),
        in_specs=[dy_spec, w_spec, rows, pl.BlockSpec((1, k), lambda i, j: (0, 0)), rows],
        out_specs=[rows] + [rows] * act_copy + [pl.BlockSpec((1, k), lambda i, j: (0, 0))],
        out_shape=[_SDS((m, k), F32)] + [_SDS((m, k), _ACT)] * act_copy + [_SDS((1, k), F32)],
        scratch_shapes=[pltpu.VMEM((tm, k), F32)],
        name=name, compiler_params=_cp("arbitrary", "arbitrary"))(dy, w, h, g, dh_in)
    return outs[0], (outs[1] if act_copy else None), outs[-1]


def _matmul_tn(x, x_spec, dy, dy_spec, m, n_blocks, acc_shape, out_shape, out_spec, name, tm=None):
    tm = tm or _rows(m)
    nm = m // tm

    def body(x_ref, dy_ref, o_ref, acc_ref):
        @pl.when(pl.program_id(1) == 0)
        def _():
            acc_ref[...] = jnp.zeros_like(acc_ref)

        acc_ref[...] += _mm_tn(x_ref[...], dy_ref[...])

        @pl.when(pl.program_id(1) == nm - 1)
        def _():
            o_ref[...] = acc_ref[...].reshape(o_ref.shape).astype(o_ref.dtype)

    return pl.pallas_call(
        body, grid=(n_blocks, nm), in_specs=[x_spec, dy_spec], out_specs=out_spec,
        out_shape=_SDS(out_shape, _WIRE), scratch_shapes=[pltpu.VMEM(acc_shape, F32)],
        name=name, compiler_params=_cp("arbitrary", "arbitrary"))(x, dy)


def _loss_head(h, g, tgt, name):
    m, k = h.shape
    tm = _rows(m)

    def body(h_ref, g_ref, t_ref, loss_ref, dh_ref, dha_ref, dg_ref):
        i = pl.program_id(0)
        xv = h_ref[...]
        r = lax.rsqrt(jnp.mean(xv * xv, axis=-1, keepdims=True) + EPS)
        xh = xv * r
        gv = g_ref[...]
        err = xh * gv - t_ref[...]
        lpart = jnp.zeros((1, 128), F32) + 0.5 * jnp.sum(jnp.mean(err * err, axis=-1, keepdims=True), axis=0, keepdims=True)
        dy = err * (1.0 / k)
        gpart = jnp.sum(dy * xh, axis=0, keepdims=True)

        @pl.when(i == 0)
        def _():
            loss_ref[...] = lpart
            dg_ref[...] = gpart

        @pl.when(i > 0)
        def _():
            loss_ref[...] += lpart
            dg_ref[...] += gpart

        t = dy * gv
        dh = r * (t - xh * jnp.mean(t * xh, axis=-1, keepdims=True))
        dh_ref[...] = dh
        dha_ref[...] = dh.astype(_ACT)

    rows = pl.BlockSpec((tm, k), lambda i: (i, 0))
    return pl.pallas_call(
        body, grid=(m // tm,),
        in_specs=[rows, pl.BlockSpec((1, k), lambda i: (0, 0)), rows],
        out_specs=[pl.BlockSpec((1, 128), lambda i: (0, 0)), rows, rows, pl.BlockSpec((1, k), lambda i: (0, 0))],
        out_shape=[_SDS((1, 128), F32), _SDS((m, k), F32), _SDS((m, k), _ACT), _SDS((1, k), F32)],
        name=name, compiler_params=_cp("arbitrary"))(h, g, tgt)


def _glu_down(gu, conv_w, conv_b, w_down, res, name):
    s = gu.shape[2]
    tm = min(s, _TM_GLU)

    def body(gu_ref, prev_ref, w_ref, b_ref, wdn_ref, r_ref, o_ref, act_ref, gc_ref):
        i, j = pl.program_id(0), pl.program_id(1)
        prev = jnp.where(i > 0, prev_ref[...].astype(F32), 0.0)
        ext = jnp.concatenate([prev, gu_ref[0].astype(F32)], axis=0)
        gc = b_ref[...] + w_ref[FFN_CONV - 1:FFN_CONV, :] * ext
        for k in range(FFN_CONV - 1):
            gc = gc + w_ref[k:k + 1, :] * pltpu.roll(ext, FFN_CONV - 1 - k, 0)
        gc = gc[GLU_HALO:]
        gc_ref[...] = gc.astype(gc_ref.dtype)
        act =(_silu(gc) * gu_ref[1].astype(F32)).astype(act_ref.dtype)
        act_ref[...] = act
        part = _mm(act, _w2d(wdn_ref))

        @pl.when(j == 0)
        def _():
            o_ref[...] = r_ref[...] + part

        @pl.when(j > 0)
        def _():
            o_ref[...] += part

    return pl.pallas_call(
        body, grid=(s // tm, FF_BLOCKS),
        in_specs=[pl.BlockSpec((2, None, tm, GU_SHARD), lambda i, j: (0, j, i, 0)),
                  pl.BlockSpec((None, None, GLU_HALO, GU_SHARD),
                               lambda i, j: (0, j, jnp.maximum(i * (tm // GLU_HALO) - 1, 0), 0)),
                  pl.BlockSpec((None, HALO, GU_SHARD), lambda i, j: (j, 0, 0)),
                  pl.BlockSpec((None, 1, GU_SHARD), lambda i, j: (j, 0, 0)),
                  _spec_down(1), pl.BlockSpec((tm, D), lambda i, j: (i, 0))],
        out_specs=[pl.BlockSpec((tm, D), lambda i, j: (i, 0)), pl.BlockSpec((None, tm, GU_SHARD), lambda i, j: (j, i, 0)),
                   pl.BlockSpec((None, tm, GU_SHARD), lambda i, j: (j, i, 0))],
        out_shape=[_SDS((s, D), F32), _SDS((FF_BLOCKS, s, GU_SHARD), _ACT), _SDS((FF_BLOCKS, s, GU_SHARD), _ACT)], name=name,
        compiler_params=_cp("arbitrary", "arbitrary"))(gu, gu, conv_w, conv_b, w_down, res)


def _glu_bwd(gu, gc, conv_w, dh, w_down, name, deps=()):
    s = gu.shape[2]
    tm = min(s, _TM_GLU)
    nt = s // tm
    ext_rows = tm + GLU_HALO

    def body(gu_ref, prev_ref, gc_ref, w_ref, dh_ref, wdn_ref, *rest):
        dgu_ref, dw_ref, db_ref, carry_ref = rest[-4:]
        t = pl.program_id(1)
        i = nt - 1 - t

        @pl.when(t == 0)
        def _():
            carry_ref[...] = jnp.zeros_like(carry_ref)
            dw_ref[...] = jnp.zeros_like(dw_ref)
            db_ref[...] = jnp.zeros_like(db_ref)

        up = gu_ref[1].astype(F32)
        prev = jnp.where(i > 0, prev_ref[...].astype(F32), 0.0)
        ext = jnp.concatenate([prev, gu_ref[0].astype(F32)], axis=0)
        gc = gc_ref[...].astype(F32)
        sg = jax.nn.sigmoid(gc)
        da = _mm_nt(dh_ref[...], _w2d(wdn_ref))
        dup = da * (gc * sg)
        dgc = da * up * (sg * (1.0 + gc * (1.0 - sg)))
        db_ref[...] += jnp.sum(dgc, axis=0, keepdims=True)
        dgc_ext = jnp.concatenate([jnp.zeros((GLU_HALO, GU_SHARD), F32), dgc], axis=0)
        ahead = [pltpu.roll(dgc_ext, ext_rows - (FFN_CONV - 1 - j), 0) if j < FFN_CONV - 1 else dgc_ext
                 for j in range(FFN_CONV)]
        dext = ahead[0] * w_ref[0:1, :]
        for j in range(FFN_CONV):
            dw_ref[j:j + 1, :] += jnp.sum(ext * ahead[j], axis=0, keepdims=True)
            if j > 0:
                dext = dext + ahead[j] * w_ref[j:j + 1, :]
        tail = jnp.concatenate([jnp.zeros((tm - GLU_HALO, GU_SHARD), F32), carry_ref[...]], axis=0)
        dgate = dext[GLU_HALO:] + tail
        carry_ref[...] = dext[:GLU_HALO]
        dgu_ref[0] = dgate.astype(dgu_ref.dtype)
        dgu_ref[1] = dup.astype(dgu_ref.dtype)

    return pl.pallas_call(
        body, grid=(FF_BLOCKS, nt),
        in_specs=[pl.BlockSpec((2, None, tm, GU_SHARD), lambda j, t: (0, j, nt - 1 - t, 0)),
                  pl.BlockSpec((None, None, GLU_HALO, GU_SHARD),
                               lambda j, t: (0, j, jnp.maximum((nt - 1 - t) * (tm // GLU_HALO) - 1, 0), 0)),
                  pl.BlockSpec((None, tm, GU_SHARD), lambda j, t: (j, nt - 1 - t, 0)),
                  pl.BlockSpec((None, HALO, GU_SHARD), lambda j, t: (j, 0, 0)),
                  pl.BlockSpec((tm, D), lambda j, t: (nt - 1 - t, 0)), _spec_down(0)] + _dep_specs(deps),
        out_specs=[pl.BlockSpec((2, None, tm, GU_SHARD), lambda j, t: (0, j, nt - 1 - t, 0)),
                   pl.BlockSpec((None, HALO, GU_SHARD), lambda j, t: (j, 0, 0)),
                   pl.BlockSpec((None, 1, GU_SHARD), lambda j, t: (j, 0, 0))],
        out_shape=[_SDS(gu.shape, _ACT), _SDS((FF_BLOCKS, HALO, GU_SHARD), F32), _SDS((FF_BLOCKS, 1, GU_SHARD), F32)],
        scratch_shapes=[pltpu.VMEM((GLU_HALO, GU_SHARD), F32)],
        name=name, compiler_params=_cp("arbitrary", "arbitrary"))(gu, gu, gc, conv_w, dh, w_down, *deps)


def _bucket_table():
    qi = np.arange(BLK)[:, None]
    kj = np.arange(BLK)[None, :]
    n = np.where(kj > qi, BLK + qi - kj, qi - kj)
    max_exact = N_BUCKETS // 2
    nf = np.maximum(n, 1).astype(np.float32)
    large = max_exact + (np.log(nf / max_exact) / math.log(MAX_DISTANCE / max_exact)
                         * (N_BUCKETS - max_exact)).astype(np.int32)
    large = np.minimum(large, N_BUCKETS - 1)
    return np.where(n < max_exact, n, large).astype(np.int32)


def _lane_low():
    return lax.broadcasted_iota(jnp.int32, (1, 128), 1) < A_HD


def _swa_groups(q, kd, vd, sink, bias, upper, first):
    n = A_HEADS // A_KV_HEADS
    ng = len(q)
    low = _lane_low()
    qm = [jnp.concatenate([jnp.where(low == (h % 2 == 0), q[g][:, (h // 2) * 128:(h // 2 + 1) * 128], 0.0) for h in range(n)], axis=0)
          for g in range(ng)]
    s2 = [_mm_nt(qm[g], kd[g]) * (A_HD ** -0.5) for g in range(ng)]
    s = [jnp.where(upper[None], s2[g][:, :BLK].reshape(n, BLK, BLK), s2[g][:, BLK:].reshape(n, BLK, BLK)) + bias[g] for g in range(ng)]
    s = [t if f is None else jnp.where((upper & f)[None], -jnp.inf, t) for t, f in zip(s, first)]
    m = [lax.stop_gradient(jnp.maximum(jnp.max(s[g], axis=-1, keepdims=True), sink[g])) for g in range(ng)]
    p = [jnp.exp(s[g] - m[g]) for g in range(ng)]
    split = [jnp.concatenate([jnp.where(upper[None], t, 0.0), jnp.where(upper[None], 0.0, t)], axis=-1).reshape(n * BLK, 2 * BLK)
             for t in p]
    ones = jnp.ones((BLK, 128), F32)
    den = [_mm(p[g].reshape(n * BLK, BLK), ones) + jnp.exp(sink[g] - m[g]).reshape(n * BLK, 1) for g in range(ng)]
    o = [_mm(split[g], vd[g]) / den[g] for g in range(ng)]
    return [jnp.concatenate([jnp.where(low, t[2 * k * BLK:(2 * k + 1) * BLK], t[(2 * k + 1) * BLK:(2 * k + 2) * BLK])
                             for k in range(n // 2)], axis=1) for t in o]


def _mix_a_core(q, kd, vd, sink, bias, xq, mk, mv, upper, first):
    return _swa_groups(q, kd, vd, sink, bias, upper, first), _cross_pairs(xq, mk, mv)


def _swa_sinks(sink_ref, g):
    n = A_HEADS // A_KV_HEADS
    return jnp.concatenate([sink_ref[:, h:h + 1] for h in range(g * n, (g + 1) * n)], axis=0).reshape(n, 1, 1)


def _both_halves(t, t_rolled, g):
    low = _lane_low()
    return jnp.where(low, t, t_rolled) if g == 0 else jnp.where(low, t_rolled, t)


def _cross_pairs(q, mk, mv):
    rows = q.shape[0]
    low = _lane_low()
    qm = [jnp.concatenate([jnp.where(low, q[:, p * 128:(p + 1) * 128], 0.0), jnp.where(low, 0.0, q[:, p * 128:(p + 1) * 128])], axis=0)
          for p in range(X_HEADS // 2)]
    s = [_mm_nt(qm[p], mk[:, p * 128:(p + 1) * 128]) * (X_HD ** -0.5) for p in range(X_HEADS // 2)]
    e = [jnp.exp(t - lax.stop_gradient(jnp.max(t, axis=-1, keepdims=True))) for t in s]
    pr = [t / jnp.sum(t, axis=-1, keepdims=True) for t in e]
    o = [_mm(pr[p], mv[:, p * 128:(p + 1) * 128]) for p in range(X_HEADS // 2)]
    return jnp.concatenate([jnp.where(low, t[:rows], t[rows:]) for t in o], axis=1)


def _swa_upper():
    qi = lax.broadcasted_iota(jnp.int32, (BLK, BLK), 0)
    kj = lax.broadcasted_iota(jnp.int32, (BLK, BLK), 1)
    return kj > qi


def _bias_build(rel_bias, bucket, name):
    def body(rb_ref, bucket_ref, o_ref):
        b = bucket_ref[...]
        for h in range(A_HEADS):
            acc = jnp.zeros((BLK, BLK), F32)
            for k in range(N_BUCKETS):
                acc = jnp.where(b == k, rb_ref[k, h], acc)
            o_ref[h] = acc

    return pl.pallas_call(
        body, in_specs=[pl.BlockSpec(memory_space=pltpu.SMEM), pl.BlockSpec(memory_space=pltpu.VMEM)],
        out_specs=pl.BlockSpec(memory_space=pltpu.VMEM),
        out_shape=_SDS((A_HEADS, BLK, BLK), F32), name=name)(rel_bias, bucket)


def _bias_reduce(dbias, bucket, name):
    def body(db_ref, bucket_ref, o_ref):
        b = bucket_ref[...]
        row = lax.broadcasted_iota(jnp.int32, (N_BUCKETS, 128), 0)
        lane = lax.broadcasted_iota(jnp.int32, (N_BUCKETS, 128), 1)
        acc = jnp.zeros((N_BUCKETS, 128), F32)
        for h in range(A_HEADS):
            v = db_ref[h]
            for k in range(N_BUCKETS):
                sk = jnp.sum(jnp.sum(jnp.where(b == k, v, 0.0), axis=1, keepdims=True), axis=0, keepdims=True)
                acc = acc + jnp.where((row == k) & (lane == h), sk, 0.0)
        o_ref[...] = acc

    return pl.pallas_call(
        body, in_specs=[pl.BlockSpec(memory_space=pltpu.VMEM)] * 2,
        out_specs=pl.BlockSpec(memory_space=pltpu.VMEM),
        out_shape=_SDS((N_BUCKETS, 128), F32), name=name)(dbias, bucket)


def _mix_a_fwd(proj, bias, sinks, memkv, name):
    s = proj.shape[0]
    per = 2
    nb = s // (per * BLK)
    grp = A_HEADS // A_KV_HEADS

    def body(proj_ref, prev_ref, bias_ref, sink_ref, memkv_ref, o_ref):
        i = pl.program_id(0)
        upper = _swa_upper()
        proj = proj_ref[...].astype(F32)
        kv = jnp.concatenate([prev_ref[...].astype(F32), proj[:, A_Q:A_Q + 2 * A_KV]], axis=0)
        k, v = kv[:, :A_KV], kv[:, A_KV:]
        k_r = pltpu.roll(k, A_HD, 1)
        v_r = pltpu.roll(v, A_HD, 1)
        gw = A_Q // A_KV_HEADS
        each = [(b, g) for b in range(per) for g in range(A_KV_HEADS)]

        def window(a, a_r, b, g):
            return _both_halves(a[b * BLK:(b + 2) * BLK], a_r[b * BLK:(b + 2) * BLK], g)

        swa, cross = _mix_a_core([proj[b * BLK:(b + 1) * BLK, g * gw:(g + 1) * gw] for b, g in each],
                                 [window(k, k_r, b, g) for b, g in each], [window(v, v_r, b, g) for b, g in each],
                                 [_swa_sinks(sink_ref, g) for b, g in each], [bias_ref[g * grp:(g + 1) * grp] for b, g in each],
                                 proj[:, A_Q + 2 * A_KV:], memkv_ref[:, :X_Q], memkv_ref[:, X_Q:], upper,
                                 [(i == 0) if b == 0 else None for b, g in each])
        for b in range(per):
            o_ref[b * BLK:(b + 1) * BLK, :] = jnp.concatenate(
                swa[b * A_KV_HEADS:(b + 1) * A_KV_HEADS] + [cross[b * BLK:(b + 1) * BLK]], axis=1).astype(o_ref.dtype)

    return pl.pallas_call(
        body, grid=(nb,),
        in_specs=[pl.BlockSpec((per * BLK, IN_A), lambda i: (i, 0)),
                  pl.BlockSpec((BLK, 2 * A_KV), lambda i: (jnp.maximum(per * i - 1, 0), A_Q // (2 * A_KV))),
                  pl.BlockSpec((A_HEADS, BLK, BLK), lambda i: (0, 0, 0)),
                  pl.BlockSpec((1, 128), lambda i: (0, 0)),
                  pl.BlockSpec((MEM_LEN, 2 * X_Q), lambda i: (0, 0))],
        out_specs=pl.BlockSpec((per * BLK, D), lambda i: (i, 0)),
        out_shape=_SDS((s, D), _ACT), name=name, compiler_params=_cp("arbitrary"))(proj, proj, bias, sinks, memkv)


def _mix_a_bwd(proj, bias, sinks, memkv, dmix, name, deps=()):
    s = proj.shape[0]
    per = 2
    nb = s // (per * BLK)
    grp = A_HEADS // A_KV_HEADS

    def body(proj_ref, prev_ref, bias_ref, sink_ref, memkv_ref, dmix_ref, *rest):
        dproj_ref, dbias_ref, dsink_ref, dmemkv_ref, carry_ref = rest[-5:]
        t = pl.program_id(0)
        i = nb - 1 - t

        @pl.when(t == 0)
        def _():
            carry_ref[...] = jnp.zeros_like(carry_ref)
            dbias_ref[...] = jnp.zeros_like(dbias_ref)
            dsink_ref[...] = jnp.zeros_like(dsink_ref)
            dmemkv_ref[...] = jnp.zeros_like(dmemkv_ref)

        upper = _swa_upper()
        lane = lax.broadcasted_iota(jnp.int32, (1, 128), 1)
        low = _lane_low()
        proj = proj_ref[...].astype(F32)
        kv = jnp.concatenate([prev_ref[...].astype(F32), proj[:, A_Q:A_Q + 2 * A_KV]], axis=0)
        k, v = kv[:, :A_KV], kv[:, A_KV:]
        k_r = pltpu.roll(k, A_HD, 1)
        v_r = pltpu.roll(v, A_HD, 1)
        gw = A_Q // A_KV_HEADS
        each = [(b, g) for b in range(per) for g in range(A_KV_HEADS)]

        def window(a, a_r, b, g):
            return _both_halves(a[b * BLK:(b + 2) * BLK], a_r[b * BLK:(b + 2) * BLK], g)

        _, vjp = jax.vjp(
            functools.partial(_mix_a_core, upper=upper, first=[(i == 0) if b == 0 else None for b, g in each]),
            [proj[b * BLK:(b + 1) * BLK, g * gw:(g + 1) * gw] for b, g in each],
            [window(k, k_r, b, g) for b, g in each], [window(v, v_r, b, g) for b, g in each],
            [_swa_sinks(sink_ref, g) for b, g in each], [bias_ref[g * grp:(g + 1) * grp] for b, g in each],
            proj[:, A_Q + 2 * A_KV:], memkv_ref[:, :X_Q], memkv_ref[:, X_Q:])
        dqs, dk, dv, ds, db, dxq, dmk, dmv = vjp(
            ([dmix_ref[b * BLK:(b + 1) * BLK, g * gw:(g + 1) * gw].astype(F32) for b, g in each], dmix_ref[:, A_Q:].astype(F32)))
        dkd = [t + pltpu.roll(t, A_HD, 1) for t in dk]
        dvd = [t + pltpu.roll(t, A_HD, 1) for t in dv]
        dsink = jnp.zeros((1, 128), F32)
        for e, (b, g) in enumerate(each):
            for h in range(grp):
                dsink = dsink + jnp.where(lane == g * grp + h, ds[e][h], 0.0)
        for g in range(A_KV_HEADS):
            dbias_ref[g * grp:(g + 1) * grp] += db[g] + db[A_KV_HEADS + g]
        dsink_ref[...] += dsink
        dmemkv_ref[...] += jnp.concatenate([dmk, dmv], axis=1)
        dkv = [jnp.concatenate([jnp.where(low, dkd[b * A_KV_HEADS], dkd[b * A_KV_HEADS + 1]),
                                jnp.where(low, dvd[b * A_KV_HEADS], dvd[b * A_KV_HEADS + 1])], axis=1) for b in range(per)]
        own = [dkv[0][BLK:] + dkv[1][:BLK], dkv[1][BLK:] + carry_ref[...]]
        carry_ref[...] = dkv[0][:BLK]
        for b in range(per):
            dproj_ref[b * BLK:(b + 1) * BLK, :] = jnp.concatenate(
                list(dqs[b * A_KV_HEADS:(b + 1) * A_KV_HEADS]) + [own[b], dxq[b * BLK:(b + 1) * BLK]], axis=1).astype(dproj_ref.dtype)

    return pl.pallas_call(
        body, grid=(nb,),
        in_specs=[pl.BlockSpec((per * BLK, IN_A), lambda t: (nb - 1 - t, 0)),
                  pl.BlockSpec((BLK, 2 * A_KV), lambda t: (jnp.maximum(per * (nb - 1 - t) - 1, 0), A_Q // (2 * A_KV))),
                  pl.BlockSpec((A_HEADS, BLK, BLK), lambda t: (0, 0, 0)),
                  pl.BlockSpec((1, 128), lambda t: (0, 0)),
                  pl.BlockSpec((MEM_LEN, 2 * X_Q), lambda t: (0, 0)),
                  pl.BlockSpec((per * BLK, D), lambda t: (nb - 1 - t, 0))] + _dep_specs(deps),
        out_specs=[pl.BlockSpec((per * BLK, IN_A), lambda t: (nb - 1 - t, 0)),
                   pl.BlockSpec((A_HEADS, BLK, BLK), lambda t: (0, 0, 0)),
                   pl.BlockSpec((1, 128), lambda t: (0, 0)),
                   pl.BlockSpec((MEM_LEN, 2 * X_Q), lambda t: (0, 0))],
        out_shape=[_SDS((s, IN_A), _ACT), _SDS((A_HEADS, BLK, BLK), F32), _SDS((1, 128), F32),
                   _SDS((MEM_LEN, 2 * X_Q), F32)],
        scratch_shapes=[pltpu.VMEM((BLK, 2 * A_KV), F32)],
        name=name, compiler_params=_cp("arbitrary"))(proj, proj, bias, sinks, memkv, dmix, *deps)


def _neumann(pw, rhs):
    nh = len(pw)
    x = rhs
    for lvl in range(6):
        if lvl < 5:
            prod = [_mmf(pw[h], jnp.concatenate([x[h], pw[h]], axis=1)) for h in range(nh)]
            x = [x[h] + prod[h][:, :B_HD] for h in range(nh)]
            pw = [t[:, B_HD:] for t in prod]
        else:
            x = [x[h] + _mmf(pw[h], x[h]) for h in range(nh)]
    return x


@jax.custom_vjp
def _tri_solve(pw, rhs):
    return _neumann(pw, rhs)


def _tri_solve_fwd(pw, rhs):
    x = _neumann(pw, rhs)
    return x, (pw, x)


def _tri_solve_bwd(res, dx):
    pw, x = res
    d_rhs = _neumann([t.T for t in pw], list(dx))
    return [_mmf_nt(d_rhs[h], x[h]) for h in range(len(pw))], d_rhs


_tri_solve.defvjp(_tri_solve_fwd, _tri_solve_bwd)


@jax.custom_vjp
def _tri_solved(pw, rhs, x):
    return x


def _tri_solved_fwd(pw, rhs, x):
    return x, (pw, x)


def _tri_solved_bwd(res, dx):
    d_pw, d_rhs = _tri_solve_bwd(res, dx)
    return d_pw, d_rhs, [jnp.zeros_like(t) for t in res[1]]


_tri_solved.defvjp(_tri_solved_fwd, _tri_solved_bwd)


@jax.custom_vjp
def _known(x, value):
    return value


def _known_fwd(x, value):
    return value, None


def _known_bwd(_, g):
    return g, jnp.zeros_like(g)


_known.defvjp(_known_fwd, _known_bwd)


def _dn_heads(yq, yk, yv, z, bl, al, a_log, dtb, ng, s0, solved=None, out_known=None):
    c = CHUNK
    nh = B_V_HEADS
    rep = B_V_HEADS // B_QK_HEADS
    r = lax.broadcasted_iota(jnp.int32, (c, c), 0)
    cc = lax.broadcasted_iota(jnp.int32, (c, c), 1)
    q = [_silu(t) for t in yq]
    k = [_silu(t) for t in yk]
    v = [_silu(t) for t in yv]
    q = [t * lax.rsqrt(jnp.sum(t * t, axis=-1, keepdims=True) + EPS) * (B_HD ** -0.5) for t in q]
    k = [t * lax.rsqrt(jnp.sum(t * t, axis=-1, keepdims=True) + EPS) for t in k]
    beta = [jax.nn.sigmoid(t) for t in bl]
    g = [-jnp.exp(a_log[h]) * jax.nn.softplus(al[h] + dtb[h]) for h in range(nh)]
    gb = [jnp.broadcast_to(t, (c, c)) for t in g]
    gc_col = [jnp.sum(jnp.where(cc <= r, t.T, 0.0), axis=1, keepdims=True) for t in gb]
    gc_row = [jnp.sum(jnp.where(r <= cc, t, 0.0), axis=0, keepdims=True) for t in gb]
    gc_last = [jnp.sum(t, axis=0, keepdims=True) for t in g]
    decay = [jnp.exp(jnp.where(r >= cc, gc_col[h] - gc_row[h], -jnp.inf)) for h in range(nh)]
    kq = [_mmf_nt(jnp.concatenate([k[h], q[h]], axis=0), k[h]) for h in range(B_QK_HEADS)]
    kk = [t[:c] for t in kq]
    qk = [t[c:] for t in kq]
    egc = [jnp.exp(t) for t in gc_col]
    both = [_mmf(jnp.concatenate([(beta[h] * egc[h]) * k[h // rep], q[h // rep] * egc[h]], axis=0), s0[h]) for h in range(nh)]
    rhs = [beta[h] * v[h] - both[h][:c] for h in range(nh)]
    qs0 = [t[c:] for t in both]
    pw = [-(beta[h] * kk[h // rep] * jnp.where(r > cc, decay[h], 0.0)) for h in range(nh)]
    delta = _tri_solve(pw, rhs) if solved is None else _tri_solved(pw, rhs, solved)
    last = [_mmf(jnp.concatenate([qk[h // rep] * decay[h], (k[h // rep] * jnp.exp(gc_last[h] - gc_col[h])).T], axis=0), delta[h])
            for h in range(nh)]
    out = [qs0[h] + last[h][:c] for h in range(nh)]
    if out_known is not None:
        out = [_known(out[h], out_known[h]) for h in range(nh)]
    s1 = [jnp.exp(gc_last[h]) * s0[h] + last[h][c:] for h in range(nh)]
    o = [t * lax.rsqrt(jnp.mean(t * t, axis=-1, keepdims=True) + EPS) * ng for t in out]
    return [o[h] * _silu(z[h]) for h in range(nh)], s1, delta, out


def _dn_conv(ext, w_ref):
    y = ext * w_ref[B_CONV - 1:B_CONV, :]
    for j in range(B_CONV - 1):
        y = y + w_ref[j:j + 1, :] * pltpu.roll(ext, B_CONV - 1 - j, 0)
    return y


def _dn_args(y, cur_ref, par_ref, ng_ref):
    nh = B_V_HEADS
    return ([y[:, h * B_HD:(h + 1) * B_HD] for h in range(B_QK_HEADS)],
            [y[:, B_QK + h * B_HD:B_QK + (h + 1) * B_HD] for h in range(B_QK_HEADS)],
            [y[:, 2 * B_QK + h * B_HD:2 * B_QK + (h + 1) * B_HD] for h in range(nh)],
            [cur_ref[:, BP_Z + h * B_HD:BP_Z + (h + 1) * B_HD] for h in range(nh)],
            [cur_ref[:, BP_GATE + h:BP_GATE + h + 1] for h in range(nh)],
            [cur_ref[:, BP_GATE + nh + h:BP_GATE + nh + h + 1] for h in range(nh)],
            [par_ref[:, h:h + 1] for h in range(nh)], [par_ref[:, nh + h:nh + h + 1] for h in range(nh)], ng_ref[...])


def _mix_b_fwd(proj, conv_w, par, ng, memkv, name):
    s = proj.shape[0]
    nc = s // CHUNK

    def body(cur_ref, prev_ref, w_ref, par_ref, ng_ref, memkv_ref, o_ref, st_ref, dl_ref, state_ref):
        n = pl.program_id(0)

        @pl.when(n == 0)
        def _():
            state_ref[...] = jnp.zeros_like(state_ref)

        prev = jnp.where(n > 0, prev_ref[...], 0.0)
        ext = jnp.concatenate([prev, cur_ref[:, :B_QKV]], axis=0)
        y = _dn_conv(ext, w_ref)[HALO:]
        s0 = [state_ref[hv] for hv in range(B_V_HEADS)]
        st_ref[0] = state_ref[...]
        outs, s1, delta, raw = _dn_heads(*_dn_args(y, cur_ref, par_ref, ng_ref), s0)
        for hv in range(B_V_HEADS):
            state_ref[hv] = s1[hv]
            dl_ref[0, hv] = delta[hv]
            dl_ref[0, B_V_HEADS + hv] = raw[hv]
        outs = outs + [_cross_pairs(cur_ref[:, BP_XQ:BP_XQ + X_Q], memkv_ref[:, :X_Q], memkv_ref[:, X_Q:])]
        o_ref[...] = jnp.concatenate(outs, axis=1).astype(o_ref.dtype)

    return pl.pallas_call(
        body, grid=(nc,),
        in_specs=[pl.BlockSpec((CHUNK, IN_BP), lambda n: (n, 0)),
                  pl.BlockSpec((HALO, B_QKV), lambda n: (jnp.maximum(n * (CHUNK // HALO) - 1, 0), 0)),
                  pl.BlockSpec((HALO, B_QKV), lambda n: (0, 0)),
                  pl.BlockSpec((1, 128), lambda n: (0, 0)), pl.BlockSpec((1, 128), lambda n: (0, 0)),
                  pl.BlockSpec((MEM_LEN, 2 * X_Q), lambda n: (0, 0))],
        out_specs=[pl.BlockSpec((CHUNK, D), lambda n: (n, 0)),
                   pl.BlockSpec((1, B_V_HEADS, B_HD, B_HD), lambda n: (n, 0, 0, 0)),
                   pl.BlockSpec((1, 2 * B_V_HEADS, CHUNK, B_HD), lambda n: (n, 0, 0, 0))],
        out_shape=[_SDS((s, D), _ACT), _SDS((nc, B_V_HEADS, B_HD, B_HD), F32), _SDS((nc, 2 * B_V_HEADS, CHUNK, B_HD), F32)],
        scratch_shapes=[pltpu.VMEM((B_V_HEADS, B_HD, B_HD), F32)],
        name=name, compiler_params=_cp("arbitrary"))(proj, proj, conv_w, par, ng, memkv)


def _mix_b_bwd(proj, conv_w, par, ng, memkv, states, deltas, dmix, name):
    s = proj.shape[0]
    nc = s // CHUNK
    ext_rows = CHUNK + HALO

    def body(cur_ref, prev_ref, w_ref, par_ref, ng_ref, memkv_ref, st_ref, dl_ref, dmix_ref,
             dproj_ref, dw_ref, dpar_ref, dng_ref, dmemkv_ref, dstate_ref, carry_ref):
        t = pl.program_id(0)
        n = nc - 1 - t

        @pl.when(t == 0)
        def _():
            dstate_ref[...] = jnp.zeros_like(dstate_ref)
            carry_ref[...] = jnp.zeros_like(carry_ref)
            dw_ref[...] = jnp.zeros_like(dw_ref)
            dpar_ref[...] = jnp.zeros_like(dpar_ref)
            dng_ref[...] = jnp.zeros_like(dng_ref)
            dmemkv_ref[...] = jnp.zeros_like(dmemkv_ref)

        lane = lax.broadcasted_iota(jnp.int32, (1, 128), 1)
        prev = jnp.where(n > 0, prev_ref[...], 0.0)
        ext = jnp.concatenate([prev, cur_ref[:, :B_QKV]], axis=0)
        y = _dn_conv(ext, w_ref)[HALO:]
        solved = [dl_ref[0, hv] for hv in range(B_V_HEADS)]
        raw = [dl_ref[0, B_V_HEADS + hv] for hv in range(B_V_HEADS)]
        _, vjp = jax.vjp(functools.partial(_dn_heads, solved=solved, out_known=raw), *_dn_args(y, cur_ref, par_ref, ng_ref),
                         [st_ref[0, hv] for hv in range(B_V_HEADS)])
        none = [jnp.zeros((CHUNK, B_HD), F32)] * B_V_HEADS
        dyq, dyk, dyv, dz, gbl, gal, ga_log, gdtb, dng, gs0 = vjp(
            ([dmix_ref[:, hv * B_HD:(hv + 1) * B_HD].astype(F32) for hv in range(B_V_HEADS)],
             [dstate_ref[hv] for hv in range(B_V_HEADS)], none, none))
        dgate = jnp.zeros((CHUNK, 128), F32)
        dpar = jnp.zeros((1, 128), F32)
        for hv in range(B_V_HEADS):
            dstate_ref[hv] = gs0[hv]
            dgate = dgate + jnp.where(lane == hv, gbl[hv], 0.0) + jnp.where(lane == B_V_HEADS + hv, gal[hv], 0.0)
            dpar = dpar + jnp.where(lane == hv, ga_log[hv], 0.0) + jnp.where(lane == B_V_HEADS + hv, gdtb[hv], 0.0)
        dpar_ref[...] += dpar
        dng_ref[...] += dng
        _, vjp = jax.vjp(_cross_pairs, cur_ref[:, BP_XQ:BP_XQ + X_Q], memkv_ref[:, :X_Q], memkv_ref[:, X_Q:])
        dxq, dmk, dmv = vjp(dmix_ref[:, B_V:].astype(F32))
        dmemkv_ref[...] += jnp.concatenate([dmk, dmv], axis=1)
        dy = jnp.concatenate(list(dyq) + list(dyk) + list(dyv), axis=1)
        dy_ext = jnp.concatenate([jnp.zeros((HALO, B_QKV), F32), dy], axis=0)
        dext = dy_ext * w_ref[B_CONV - 1:B_CONV, :]
        dw_ref[B_CONV - 1:B_CONV, :] += jnp.sum(ext * dy_ext, axis=0, keepdims=True)
        for j in range(B_CONV - 1):
            sh = B_CONV - 1 - j
            dw_ref[j:j + 1, :] += jnp.sum(pltpu.roll(ext, sh, 0) * dy_ext, axis=0, keepdims=True)
            dext = dext + w_ref[j:j + 1, :] * pltpu.roll(dy_ext, ext_rows - sh, 0)
        tail = jnp.concatenate([jnp.zeros((CHUNK - HALO, B_QKV), F32), carry_ref[...]], axis=0)
        dqkv = dext[HALO:] + tail
        carry_ref[...] = dext[:HALO]
        dproj_ref[...] = jnp.concatenate([dqkv] + list(dz) + [dxq, dgate], axis=1).astype(dproj_ref.dtype)

    return pl.pallas_call(
        body, grid=(nc,),
        in_specs=[pl.BlockSpec((CHUNK, IN_BP), lambda t: (nc - 1 - t, 0)),
                  pl.BlockSpec((HALO, B_QKV), lambda t: (jnp.maximum((nc - 1 - t) * (CHUNK // HALO) - 1, 0), 0)),
                  pl.BlockSpec((HALO, B_QKV), lambda t: (0, 0)),
                  pl.BlockSpec((1, 128), lambda t: (0, 0)), pl.BlockSpec((1, 128), lambda t: (0, 0)),
                  pl.BlockSpec((MEM_LEN, 2 * X_Q), lambda t: (0, 0)),
                  pl.BlockSpec((1, B_V_HEADS, B_HD, B_HD), lambda t: (nc - 1 - t, 0, 0, 0)),
                  pl.BlockSpec((1, 2 * B_V_HEADS, CHUNK, B_HD), lambda t: (nc - 1 - t, 0, 0, 0)),
                  pl.BlockSpec((CHUNK, D), lambda t: (nc - 1 - t, 0))],
        out_specs=[pl.BlockSpec((CHUNK, IN_BP), lambda t: (nc - 1 - t, 0)),
                   pl.BlockSpec((HALO, B_QKV), lambda t: (0, 0)),
                   pl.BlockSpec((1, 128), lambda t: (0, 0)), pl.BlockSpec((1, 128), lambda t: (0, 0)),
                   pl.BlockSpec((MEM_LEN, 2 * X_Q), lambda t: (0, 0))],
        out_shape=[_SDS((s, IN_BP), _ACT), _SDS((HALO, B_QKV), F32), _SDS((1, 128), F32), _SDS((1, 128), F32),
                   _SDS((MEM_LEN, 2 * X_Q), F32)],
        scratch_shapes=[pltpu.VMEM((B_V_HEADS, B_HD, B_HD), F32), pltpu.VMEM((HALO, B_QKV), F32)],
        name=name, compiler_params=_cp("arbitrary"))(proj, proj, conv_w, par, ng, memkv, states, deltas, dmix)


def _place():
    return lax.axis_index("x"), lax.axis_index("y"), lax.axis_index("c")


def _all_gather(shards, name):
    n = len(shards)

    def body(*refs):
        ins, outs = refs[:n], refs[n:2 * n]
        send_sems, recv_sems, local_sems = refs[2 * n:]
        x, y, c = _place()
        me, sibling = (x, y, c), (x, y, 1 - c)
        chips = [(1 - x, y), (x, 1 - y), (1 - x, 1 - y)]

        def rows(a, px, py, pc):
            return outs[a].at[4 * px + 2 * py + pc]

        def copy(a, k, block, to, src=None):
            return pltpu.make_async_remote_copy(
                src_ref=rows(a, *block) if src is None else src, dst_ref=rows(a, *block),
                send_sem=send_sems.at[a, k], recv_sem=recv_sems.at[a, k],
                device_id=to, device_id_type=pl.DeviceIdType.MESH)

        mine = [pltpu.make_async_copy(ins[a], rows(a, *me), local_sems.at[a]) for a in range(n)]
        for cp in mine:
            cp.start()
        first = []
        for a in range(n):
            first.append(copy(a, 0, me, sibling, src=ins[a]))
            first += [copy(a, 1 + j, me, (*chip, c), src=ins[a]) for j, chip in enumerate(chips)]
        for cp in first:
            cp.start()
        passed = []
        for j, chip in enumerate(chips):
            for a in range(n):
                copy(a, 1 + j, (*chip, c), me).wait_recv()
                fwd = copy(a, 4 + j, (*chip, c), sibling)
                fwd.start()
                passed.append(fwd)
        for a in range(n):
            copy(a, 0, sibling, me).wait_recv()
            for j, chip in enumerate(chips):
                copy(a, 4 + j, (*chip, 1 - c), me).wait_recv()
        for cp in first + passed:
            cp.wait_send()
        for cp in mine:
            cp.wait()

    hbm = pl.BlockSpec(memory_space=pl.ANY)
    return pl.pallas_call(
        body, out_shape=[_SDS((N_DEV,) + s.shape, s.dtype) for s in shards],
        in_specs=[hbm] * n, out_specs=[hbm] * n,
        scratch_shapes=[pltpu.SemaphoreType.DMA((n, 7)), pltpu.SemaphoreType.DMA((n, 7)), pltpu.SemaphoreType.DMA((n,))],
        name=name)(*shards)


class _Exchange:
    def __init__(self, lands, srcs):
        self.lands, self.srcs = lands, srcs


def _seq_exchange(srcs, land_shapes, plan, name, cid):
    n, nl = len(srcs), len(land_shapes)

    def launch(*refs):
        src_refs, land_refs = refs[:n], refs[n:n + nl]
        send_sems, recv_sems, local_sems = refs[n + nl:]
        x, y, c = _place()
        my = 4 * x + 2 * y + c
        peers = [(x ^ ((k + 1) >> 2 & 1), y ^ ((k + 1) >> 1 & 1), c ^ ((k + 1) & 1)) for k in range(N_DEV - 1)]
        barrier = pltpu.get_barrier_semaphore()
        for p in peers:
            pl.semaphore_signal(barrier, inc=1, device_id=p, device_id_type=pl.DeviceIdType.MESH)
        pl.semaphore_wait(barrier, N_DEV - 1)

        def src_for(a, dest):
            return src_refs[a].at[dest] if plan[a][1] else src_refs[a]

        def slot(a, source):
            return land_refs[plan[a][0]].at[source]

        mine = [pltpu.make_async_copy(src_for(a, my), slot(a, my), local_sems.at[a]) for a in range(n)]
        for cp in mine:
            cp.start()
        sends, recvs = [], []
        for k, (px, py, pc) in enumerate(peers):
            peer = 4 * px + 2 * py + pc
            for a in range(n):
                kw = dict(send_sem=send_sems.at[a * (N_DEV - 1) + k], recv_sem=recv_sems.at[a * (N_DEV - 1) + k],
                          device_id=(px, py, pc), device_id_type=pl.DeviceIdType.MESH)
                sends.append(pltpu.make_async_remote_copy(src_ref=src_for(a, peer), dst_ref=slot(a, my), **kw))
                recvs.append(pltpu.make_async_remote_copy(src_ref=src_for(a, my), dst_ref=slot(a, peer), **kw))
        for cp in sends:
            cp.start()
        for cp in recvs:
            cp.wait_recv()
        for cp in sends:
            cp.wait_send()
        for cp in mine:
            cp.wait()

    lands = pl.kernel(
        launch, out_type=[_SDS(s, d) for s, d in land_shapes],
        mesh=plsc.ScalarSubcoreMesh(axis_name="sequencer", num_cores=1), name=name,
        scratch_types=(pltpu.SemaphoreType.DMA((n * (N_DEV - 1),)), pltpu.SemaphoreType.DMA((n * (N_DEV - 1),)),
                       pltpu.SemaphoreType.DMA((n,))),
        compiler_params=pltpu.CompilerParams(collective_id=cid))(*srcs)
    return _Exchange(list(lands), list(srcs))


def _adam_update(g, w, m, v):
    c1 = 1.0 - ADAM_B1 ** ADAM_STEP
    c2 = 1.0 - ADAM_B2 ** ADAM_STEP
    mm = ADAM_B1 * m + (1.0 - ADAM_B1) * g
    vv = ADAM_B2 * v + (1.0 - ADAM_B2) * (g * g)
    delta = -ADAM_LR * ((mm / c1) / (jnp.sqrt(vv / c2) + ADAM_EPS) + ADAM_WD * w)
    return delta, mm, vv


def _sum_sources(p_ref):
    g = p_ref[0].astype(F32)
    for s in range(1, N_DEV):
        g = g + p_ref[s].astype(F32)
    return g


def _adamw(parts, w, m, v, tr, name, restore_b=False, deps=()):
    nl, r, c = w.shape
    cp = parts[0].shape[-1]

    def body(*refs):
        p_refs = refs[:nl]
        w_ref, m_ref, v_ref = refs[nl:nl + 3]
        g_ref, d_ref, nm_ref, nv_ref = refs[-4:]
        g = _sum_sources(p_refs[0])
        for l in range(1, nl):
            g = jnp.where(pl.program_id(0) == l, _sum_sources(p_refs[l]), g)
        if restore_b:
            g = jnp.concatenate([g[:, :BP_XQ], g[:, BP_GATE:BP_GATE + 2 * B_V_HEADS], g[:, BP_XQ:BP_GATE]], axis=1)
        delta, mm, vv = _adam_update(g, w_ref[...], m_ref[...], v_ref[...])
        g_ref[...] = g
        d_ref[...] = delta
        nm_ref[...] = mm
        nv_ref[...] = vv

    spec = pl.BlockSpec((None, tr, c), lambda l, i: (l, i, 0))
    part_specs = [pl.BlockSpec((N_DEV, tr, cp), functools.partial(lambda l, i, k: (0, jnp.where(l == k, i, 0), 0), k=k))
                  for k in range(nl)]
    return pl.pallas_call(
        body, grid=(nl, r // tr),
        in_specs=part_specs + [spec, spec, spec] + _dep_specs(deps),
        out_specs=[spec] * 4, out_shape=[_SDS(w.shape, F32)] * 4,
        name=name, compiler_params=_cp("arbitrary", "arbitrary"))(*parts, w, m, v, *deps)


def _pack_small(d_rel, d_cb, d_cw, d_qkv, d_mix, d_mem, d_ffn, d_final, d_sinks, d_par, d_ng, loss_row, name):
    flat = [d_rel, *d_cb, *d_cw, d_qkv, *d_mix, *d_mem, *d_ffn, d_final, d_sinks, d_par, d_ng, loss_row]
    n = len(flat)

    def body(*refs):
        ins, o_ref = refs[:n], refs[n]
        rel, cb0, cb1, cw0, cw1, qkv, mx0, mx1, me0, me1, ff0, ff1, fin, snk, par, ng, lss = ins
        o_ref[...] = jnp.zeros_like(o_ref)
        for k in range(N_BUCKETS):
            lane = SP_REL_LANE + 128 * (k % 8)
            o_ref[SP_QKV + k // 8:SP_QKV + k // 8 + 1, lane:lane + 128] = rel[k:k + 1, :]
        for l, (cb, cw) in enumerate(((cb0, cw0), (cb1, cw1))):
            o_ref[SP_CB + l:SP_CB + l + 1, :] = jnp.concatenate([cb[j] for j in range(FF_BLOCKS)], axis=1)
            full = jnp.concatenate([cw[j] for j in range(FF_BLOCKS)], axis=1)
            o_ref[SP_CW + FFN_CONV * l:SP_CW + FFN_CONV * (l + 1), :] = full[:FFN_CONV]
        o_ref[SP_QKV:SP_QKV + B_CONV, 0:B_QKV] = qkv[0:B_CONV, :]
        for base, pair in ((SP_MIX, (mx0, mx1)), (SP_MEM, (me0, me1)), (SP_FFN, (ff0, ff1))):
            for l in range(2):
                o_ref[base + l:base + l + 1, 0:D] = pair[l][...]
        o_ref[SP_FINAL:SP_FINAL + 1, 0:D] = fin[...]
        o_ref[SP_MISC:SP_MISC + 1, 0:128] = snk[...]
        o_ref[SP_MISC:SP_MISC + 1, 128:256] = par[...]
        o_ref[SP_MISC:SP_MISC + 1, 256:384] = ng[...]
        o_ref[SP_MISC:SP_MISC + 1, 384:512] = lss[...]

    vm = pl.BlockSpec(memory_space=pltpu.VMEM)
    return pl.pallas_call(body, in_specs=[vm] * n, out_specs=vm, out_shape=_SDS((SMALL_ROWS, D_FF), F32), name=name)(*flat)


_SMALL = ["rel_bias", "norm_mix_g", "norm_mem_g", "sinks_a", "a_log_b", "dt_bias_b", "out_norm_g_b", "norm_ffn_g",
          "ffn_conv_b", "final_norm_g", "conv_qkv_b", "ffn_conv_w"]


def _adamw_small(recv, rc_qkv, rc_ffn, ws, ms, vs, name, deps=()):
    n = len(_SMALL)

    def body(*refs):
        recv_ref, qkv_ref, ffn_ref = refs[:3]
        w_refs, m_refs, v_refs = refs[3:3 + n], refs[3 + n:3 + 2 * n], refs[3 + 2 * n:3 + 3 * n]
        outs, loss_ref = refs[len(refs) - 4 * n - 1:len(refs) - 1], refs[-1]
        gs = _sum_sources(recv_ref)
        loss_ref[...] = gs[SP_MISC:SP_MISC + 1, 384:512]
        grads = {
            "rel_bias": jnp.concatenate(
                [gs[SP_QKV + k // 8:SP_QKV + k // 8 + 1, SP_REL_LANE + 128 * (k % 8):SP_REL_LANE + 128 * (k % 8) + A_HEADS]
                 for k in range(N_BUCKETS)], axis=0),
            "norm_mix_g": gs[SP_MIX:SP_MIX + 2, 0:D], "norm_mem_g": gs[SP_MEM:SP_MEM + 2, 0:D],
            "sinks_a": gs[SP_MISC:SP_MISC + 1, 0:A_HEADS],
            "a_log_b": gs[SP_MISC:SP_MISC + 1, 128:128 + B_V_HEADS],
            "dt_bias_b": gs[SP_MISC:SP_MISC + 1, 128 + B_V_HEADS:128 + 2 * B_V_HEADS],
            "out_norm_g_b": gs[SP_MISC:SP_MISC + 1, 256:256 + B_HD],
            "norm_ffn_g": gs[SP_FFN:SP_FFN + 2, 0:D], "ffn_conv_b": gs[SP_CB:SP_CB + 2, :],
            "final_norm_g": gs[SP_FINAL:SP_FINAL + 1, 0:D],
            "conv_qkv_b": _sum_sources(qkv_ref), "ffn_conv_w": _sum_sources(ffn_ref),
        }
        for i, nm in enumerate(_SMALL):
            g = grads[nm]
            delta, mm, vv = _adam_update(g, w_refs[i][...], m_refs[i][...], v_refs[i][...])
            outs[i][...] = g
            outs[n + i][...] = delta
            outs[2 * n + i][...] = mm
            outs[3 * n + i][...] = vv

    vm = pl.BlockSpec(memory_space=pltpu.VMEM)
    shapes = [_SDS(w.shape, F32) for w in ws]
    return pl.pallas_call(
        body, in_specs=[vm] * (3 + 3 * n) + _dep_specs(deps), out_specs=[vm] * (4 * n + 1),
        out_shape=shapes * 4 + [_SDS((1, 128), F32)],
        name=name)(recv, rc_qkv, rc_ffn, *ws, *ms, *vs, *deps)


def _assemble(gathered, axis):
    g = jnp.moveaxis(gathered, 0, axis)
    shp = list(g.shape)
    return g.reshape(shp[:axis] + [shp[axis] * shp[axis + 1]] + shp[axis + 2:])


def _pad_rows(a, rows):
    return jnp.pad(a, ((0, rows - a.shape[0]), (0, 0)))


def _pad_lanes(a, lanes=128):
    return jnp.pad(a, ((0, 0), (0, lanes - a.shape[1])))


def _ff_blocks(a):
    return jnp.moveaxis(a.reshape(a.shape[0], FF_BLOCKS, GU_SHARD), 1, 0)


def _reorder_b(w):
    qkv_z = w[..., :B_QKV + B_V]
    gates = w[..., B_QKV + B_V:B_QKV + B_V + 2 * B_V_HEADS]
    xq = w[..., IN_B - X_Q:]
    pad = jnp.zeros(w.shape[:-1] + (IN_BP - IN_B,), w.dtype)
    return jnp.concatenate([qkv_z, xq, gates, pad], axis=-1)


def kernel(x, mem, rel_bias, norm_mix_g, norm_mem_g, w_mem_kv, w_out, w_in_a, sinks_a, w_in_b, conv_qkv_b, a_log_b, dt_bias_b, out_norm_g_b, norm_ffn_g, w_gate_up, ffn_conv_w, ffn_conv_b, w_down, final_norm_g, loss_target, m_rel_bias, m_norm_mix_g, m_norm_mem_g, m_w_mem_kv, m_w_out, m_w_in_a, m_sinks_a, m_w_in_b, m_conv_qkv_b, m_a_log_b, m_dt_bias_b, m_out_norm_g_b, m_norm_ffn_g, m_w_gate_up, m_ffn_conv_w, m_ffn_conv_b, m_w_down, m_final_norm_g, v_rel_bias, v_norm_mix_g, v_norm_mem_g, v_w_mem_kv, v_w_out, v_w_in_a, v_sinks_a, v_w_in_b, v_conv_qkv_b, v_a_log_b, v_dt_bias_b, v_out_norm_g_b, v_norm_ffn_g, v_w_gate_up, v_ffn_conv_w, v_ffn_conv_b, v_w_down, v_final_norm_g):
    local = dict(locals())
    order = ["rel_bias", "norm_mix_g", "norm_mem_g", "w_mem_kv", "w_out", "w_in_a", "sinks_a", "w_in_b", "conv_qkv_b",
             "a_log_b", "dt_bias_b", "out_norm_g_b", "norm_ffn_g", "w_gate_up", "ffn_conv_w", "ffn_conv_b", "w_down",
             "final_norm_g"]
    wts = {n: local[n] for n in order}
    moms = {n: local["m_" + n] for n in order}
    vars_ = {n: local["v_" + n] for n in order}
    h0 = x[0]
    memx = mem[0]
    tgt = loss_target[0]
    s = h0.shape[0]
    tm = _rows(s)
    tb = min(s, _TM_BIG)

    t_ = lambda a: jnp.swapaxes(a, 1, 2)
    g_mk0, g_out0, g_ia, g_cq, g_cw = _all_gather(
        [w_mem_kv[0:1].astype(_MXU), w_out[0:1].astype(_MXU), t_(w_in_a).astype(_MXU), conv_qkv_b, ffn_conv_w], "gather_first")
    g_mk, g_out = [g_mk0], [g_out0]
    gu_land = ((N_DEV, GU_SHARD, D), _MXU)
    dn_land = ((N_DEV, DN_SHARD, D), _MXU)
    whole = [(0, False), (1, False)]
    def after(a, b):
        return a + (b[(0,) * b.ndim] * 0).astype(a.dtype)

    gu0_w = _seq_exchange([after(t_(w_gate_up)[0].astype(_MXU), g_ia)], [gu_land], [(0, False)], "gather_gate_up0", 1)
    dn0_w = _seq_exchange([after(w_down[0].astype(_MXU), g_ia)], [dn_land], [(0, False)], "gather_down0", 8)
    w_ia = g_ia.reshape(IN_A, D)
    conv_qkv = _pad_rows(_assemble(g_cq, 2)[0], HALO)
    ffn_cw_full = _assemble(g_cw, 2)
    ffn_cw = [_ff_blocks(_pad_rows(ffn_cw_full[i], HALO)) for i in range(2)]
    ffn_cb = [_ff_blocks(ffn_conv_b[i:i + 1]) for i in range(2)]
    bucket = jnp.asarray(_bucket_table())
    bias = _bias_build(rel_bias, bucket, "bias_build")
    sinks = _pad_lanes(sinks_a)
    par_b = _pad_lanes(jnp.concatenate([a_log_b, dt_bias_b], axis=1))

    row_x = pl.BlockSpec((tm, D), lambda i, j: (i, 0))
    gu_shape = (2, FF_BLOCKS, s, GU_SHARD)

    def in_proj(h, g, w, w_spec, n_cols, tn, name, deps=(), out_dtype=F32, w_t=False, tm=None):
        return _norm_matmul(h, g, w, w_spec, n_cols // tn, (h.shape[0], n_cols),
                            pl.BlockSpec((tm or _rows(h.shape[0]), tn), lambda i, j: (i, j)), name, deps=deps, out_dtype=out_dtype,
                            w_t=w_t, tm=tm)

    def ffn_fwd(i, h, g_gu, g_dn, deps=()):
        gu, hn = _norm_matmul(h, norm_ffn_g[i:i + 1], g_gu, _spec_gate_up(1), N_DEV, gu_shape,
                              _spec_gu_act(0, 1, tb), f"gate_up_{i}", deps=deps, out_dtype=_ACT, w_t=True, tm=tb)
        h_new, act, gc = _glu_down(gu, ffn_cw[i], ffn_cb[i], g_dn, h, f"glu_down_{i}")
        return h_new, gu, hn, (act, gc)

    def out_proj(i, mix, h):
        return _matmul_res(mix, row_x, g_out[i], _spec_rowsharded(0, D // N_DEV, D), 1, h, f"out_proj_{i}")

    proj_a, hn_a = in_proj(h0, norm_mix_g[0:1], w_ia, pl.BlockSpec((640, D), lambda i, j: (j, 0)), IN_A, 640, "in_proj_a",
                           deps=gu0_w.srcs + dn0_w.srcs, out_dtype=_ACT, w_t=True)
    memkv0, memn0 = in_proj(memx, norm_mem_g[0:1], g_mk[0], _spec_rowsharded(0, D // N_DEV, 2 * X_Q), 2 * X_Q, 2 * X_Q, "mem_proj_0")
    mix_a = _mix_a_fwd(proj_a, bias, sinks, memkv0, "mix_a_fwd")
    h1 = out_proj(0, mix_a, h0)
    g_gu0, g_dn0 = gu0_w.lands[0], dn0_w.lands[0]
    in_b_w = _seq_exchange([after(_reorder_b(w_in_b).astype(_MXU), h1), after(w_mem_kv[1:2].astype(_MXU), h1),
                            after(w_out[1:2].astype(_MXU), h1)],
                           [((N_DEV, 1, D // N_DEV, IN_BP), _MXU), ((N_DEV, 1, D // N_DEV, 2 * X_Q), _MXU),
                            ((N_DEV, 1, D // N_DEV, D), _MXU)], [(0, False), (1, False), (2, False)], "gather_in_b", 2)
    ffn1_w = _seq_exchange([after(t_(w_gate_up)[1].astype(_MXU), h1), after(w_down[1].astype(_MXU), h1)], [gu_land, dn_land], whole,
                           "gather_ffn1", 3)
    h2, gu0, hn_f0, act0 = ffn_fwd(0, h1, g_gu0, g_dn0, deps=in_b_w.srcs + ffn1_w.srcs)
    g_ib, g_mk1, g_out1 = in_b_w.lands
    g_mk.append(g_mk1)
    g_out.append(g_out1)
    proj_b, hn_b = in_proj(h2, norm_mix_g[1:2], g_ib, _spec_rowsharded(0, D // N_DEV, 896, col_block=1), IN_BP, 896, "in_proj_b")
    memkv1, memn1 = in_proj(memx, norm_mem_g[1:2], g_mk[1], _spec_rowsharded(0, D // N_DEV, 2 * X_Q), 2 * X_Q, 2 * X_Q, "mem_proj_1",
                            deps=[h2])
    mix_b, states, deltas = _mix_b_fwd(proj_b, conv_qkv, par_b, out_norm_g_b, memkv1, "mix_b_fwd")
    h3 = out_proj(1, mix_b, h2)
    g_gu1, g_dn1 = ffn1_w.lands
    h4, gu1, hn_f1, act1 = ffn_fwd(1, h3, g_gu1, g_dn1)
    loss_row, *dh, d_final_g = _loss_head(h4, final_norm_g[None, :], tgt, "loss_head")

    zeros_mem = jnp.zeros_like(memx)
    per_dest2 = [(0, True), (1, True)]

    def ffn_bwd(i, dh, h_in, gu, hn_f, act_gc, g_gu, g_dn, deps=()):
        act, gc = act_gc
        dgu, d_cw, d_cb = _glu_bwd(gu, gc, ffn_cw[i], dh[1], g_dn, f"glu_bwd_{i}", deps=deps)
        d_wdown = _matmul_tn(act, pl.BlockSpec((None, tb, GU_SHARD), lambda j, r: (j, r, 0)),
                             dh[1], pl.BlockSpec((tb, D), lambda j, r: (r, 0)), s, FF_BLOCKS, (GU_SHARD, D),
                             (N_DEV, DN_SHARD, D), pl.BlockSpec((2, DN_SHARD, D), lambda j, r: (j, 0, 0)), f"d_w_down_{i}",
                             tm=tb)
        *dh_new, d_g = _matmul_nt_normbwd(dgu, _spec_gu_act(0, 1, tm), g_gu, _spec_gate_up(1), N_DEV, h_in,
                                          norm_ffn_g[i:i + 1], dh[0], f"d_ffn_in_{i}", w_t=True, act_copy=True)
        d_wgu = _matmul_tn(dgu, _spec_gu_act(1, 0, tb), hn_f, pl.BlockSpec((tb, D), lambda j, r: (r, 0)), s, N_DEV,
                           (GU_SHARD, D), (N_DEV, GU_SHARD, D), pl.BlockSpec((None, GU_SHARD, D), lambda j, r: (j, 0, 0)),
                           f"d_w_gate_up_{i}", tm=tb)
        return dh_new, [d_wdown, d_wgu], d_cw, d_cb, d_g

    def out_bwd(i, dh, mix, deps):
        dmix = _matmul_nt(dh[1], g_out[i], _spec_rowsharded(0, D // N_DEV, D), 1, (s, D), row_x, f"d_mix_{i}", deps=deps, out_dtype=_ACT)
        d_wout = _matmul_tn(mix, pl.BlockSpec((tb, D), lambda j, r: (r, 0)), dh[1], pl.BlockSpec((tb, D), lambda j, r: (r, 0)),
                            s, 1, (D, D), (N_DEV, D // N_DEV, D), pl.BlockSpec((N_DEV, D // N_DEV, D), lambda j, r: (0, 0, 0)),
                            f"d_w_out_{i}", tm=tb)
        return dmix, d_wout

    def mem_bwd(i, dmemkv, memn):
        tmm = _rows(MEM_LEN)
        *_, d_g = _matmul_nt_normbwd(dmemkv, pl.BlockSpec((tmm, 2 * X_Q), lambda r, j: (r, 0)), g_mk[i],
                                     _spec_rowsharded(0, D // N_DEV, 2 * X_Q), 1, memx, norm_mem_g[i:i + 1], zeros_mem,
                                     f"d_mem_in_{i}")
        by_row = lambda j, r: (r, 0)
        d_w = _matmul_tn(memn, pl.BlockSpec((tmm, D), by_row), dmemkv, pl.BlockSpec((tmm, 2 * X_Q), by_row), MEM_LEN, 1,
                         (D, 2 * X_Q), (N_DEV, D // N_DEV, 2 * X_Q),
                         pl.BlockSpec((N_DEV, D // N_DEV, 2 * X_Q), lambda j, r: (0, 0, 0)), f"d_w_mem_kv_{i}")
        return d_w, d_g

    out_land = ((N_DEV, D // N_DEV, D), _WIRE)
    mk_land = ((N_DEV, D // N_DEV, 2 * X_Q), _WIRE)
    ffn_lands = [((N_DEV, DN_SHARD, D), _WIRE), ((N_DEV, GU_SHARD, D), _WIRE)]
    dh, d_ffn1, d_cw1, d_cb1, d_gf1 = ffn_bwd(1, dh, h3, gu1, hn_f1, act1, g_gu1, g_dn1)
    ffn1_g = _seq_exchange(d_ffn1, ffn_lands, per_dest2, "send_ffn1_grads", 5)
    dmix, d_wout1 = out_bwd(1, dh, mix_b, ffn1_g.srcs)
    dproj_b, d_convw, d_par, d_ng, dmemkv1 = _mix_b_bwd(proj_b, conv_qkv, par_b, out_norm_g_b, memkv1, states, deltas, dmix, "mix_b_bwd")
    *dh, d_gm1 = _matmul_nt_normbwd(dproj_b, pl.BlockSpec((tm, 896), lambda i, j: (i, j)), g_ib,
                                    _spec_rowsharded(0, D // N_DEV, 896, col_block=1), IN_BP // 896, h2, norm_mix_g[1:2], dh[0],
                                    "d_in_b", act_copy=True)
    d_wib = _matmul_tn(hn_b, pl.BlockSpec((tb, D), lambda j, r: (r, 0)), dproj_b, pl.BlockSpec((tb, 896), lambda j, r: (r, j)),
                       s, IN_BP // 896, (D, 896), (N_DEV, D // N_DEV, IN_BP),
                       pl.BlockSpec((N_DEV, D // N_DEV, 896), lambda j, r: (0, 0, j)), "d_w_in_b", tm=tb)
    d_wmk1, d_gmem1 = mem_bwd(1, dmemkv1, memn1)
    mix1_g = _seq_exchange([d_wout1, d_wib, d_wmk1], [out_land, ((N_DEV, D // N_DEV, IN_BP), _WIRE), mk_land],
                           [(0, True), (1, True), (2, True)], "send_mix1_grads", 6)
    dh, d_ffn0, d_cw0, d_cb0, d_gf0 = ffn_bwd(0, dh, h1, gu0, hn_f0, act0, g_gu0, g_dn0, deps=mix1_g.srcs)
    dmix, d_wout0 = out_bwd(0, dh, mix_a, d_ffn0 + ffn1_g.lands[:1])
    ffn0_g = _seq_exchange(d_ffn0 + [d_wout0], ffn_lands + [out_land], per_dest2 + [(2, True)], "send_ffn0_grads", 4)
    dproj_a, dbias, dsinks, dmemkv0 = _mix_a_bwd(proj_a, bias, sinks, memkv0, dmix, "mix_a_bwd", deps=ffn0_g.srcs)
    dx, _, d_gm0 = _matmul_nt_normbwd(dproj_a, pl.BlockSpec((tm, 640), lambda i, j: (i, j)), w_ia,
                                      pl.BlockSpec((640, D), lambda i, j: (j, 0)), IN_A // 640, h0, norm_mix_g[0:1], dh[0],
                                      "d_in_a", w_t=True)
    d_wia = _matmul_tn(dproj_a, pl.BlockSpec((tb, IN_A), lambda j, r: (r, 0)), hn_a, pl.BlockSpec((tb, D), lambda j, r: (r, 0)),
                       s, 1, (IN_A, D), (N_DEV, IA_SHARD, D), pl.BlockSpec((N_DEV, IA_SHARD, D), lambda j, r: (0, 0, 0)),
                       "d_w_in_a", tm=tb)
    d_wmk0, d_gmem0 = mem_bwd(0, dmemkv0, memn0)
    d_rel = _bias_reduce(dbias, bucket, "bias_reduce")
    small = _pack_small(d_rel, (d_cb0, d_cb1), (d_cw0, d_cw1), d_convw, (d_gm0, d_gm1), (d_gmem0, d_gmem1),
                        (d_gf0, d_gf1), d_final_g, dsinks, d_par, d_ng, loss_row, "pack_small")
    mix0_g = _seq_exchange([d_wia, d_wmk0, small],
                           [((N_DEV, IA_SHARD, D), _WIRE), mk_land, ((N_DEV, SMALL_ROWS, D_FF), F32)],
                           [(0, True), (1, True), (2, False)], "send_mix0_grads", 7)

    res = {}
    last = []

    def update(nm, parts, tr, restore=False, transposed=False):
        view = t_ if transposed else (lambda a: a)
        out = _adamw(parts, view(wts[nm]), view(moms[nm]), view(vars_[nm]), tr, "adamw_" + nm, restore_b=restore, deps=last[-1:])
        res[nm] = [view(o) for o in out]
        last.append(out[1])

    r_dn1, r_gu1 = ffn1_g.lands
    r_dn0, r_gu0, r_out0 = ffn0_g.lands
    r_out1, r_ib, r_mk1 = mix1_g.lands
    update("w_in_b", [r_ib], 32, True)
    update("w_gate_up", [r_gu0, r_gu1], 176, transposed=True)
    update("w_down", [r_dn0, r_dn1], 176)
    r_ia, r_mk0, r_small = mix0_g.lands
    update("w_mem_kv", [r_mk0, r_mk1], 128)
    update("w_out", [r_out0, r_out1], 128)
    update("w_in_a", [r_ia], IA_SHARD, transposed=True)

    my = 4 * lax.axis_index("x") + 2 * lax.axis_index("y") + lax.axis_index("c")
    cq = conv_qkv_b.shape[-1]
    cf = ffn_conv_w.shape[-1]
    rc_qkv = lax.dynamic_slice_in_dim(r_small[:, SP_QKV:SP_QKV + B_CONV, :B_QKV], my * cq, cq, axis=2)[:, None]
    rc_ffn = lax.dynamic_slice_in_dim(r_small[:, SP_CW:SP_CW + 2 * FFN_CONV, :], my * cf, cf, axis=2).reshape(N_DEV, 2, FFN_CONV, cf)
    as2d = lambda a: a[None, :] if a.ndim == 1 else a
    small_out = _adamw_small(r_small, rc_qkv, rc_ffn, [as2d(wts[n]) for n in _SMALL], [as2d(moms[n]) for n in _SMALL],
                             [as2d(vars_[n]) for n in _SMALL], "adamw_small", deps=last[-1:])
    ns = len(_SMALL)
    for i, nm in enumerate(_SMALL):
        res[nm] = [small_out[k * ns + i].reshape(wts[nm].shape) for k in range(4)]

    return (small_out[-1][0, 0], dx[None], *[res[n][0] for n in order], *[res[n][1] for n in order],
            *[res[n][2] for n in order], *[res[n][3] for n in order])
```

```python
import functools
import math

import numpy as np

import jax
import jax.numpy as jnp
from jax import lax
from jax.experimental import pallas as pl
from jax.experimental.pallas import tpu as pltpu
from jax.experimental.pallas import tpu_sc as plsc

F32 = jnp.float32
_MXU = jnp.bfloat16
_ACT = jnp.bfloat16
_WIRE = jnp.bfloat16
_HI = lax.Precision.HIGH
_TM = 1024
_TM_GLU = 1024
_TM_BIG = 2048
_VMEM_LIMIT = 48 * 1024 * 1024
_SDS = jax.ShapeDtypeStruct

D = 1024
EPS = 1e-6
A_HEADS, A_KV_HEADS, A_HD, BLK = 12, 2, 64, 128
N_BUCKETS, MAX_DISTANCE = 32, 128
B_QK_HEADS, B_V_HEADS, B_HD, B_CONV, CHUNK = 3, 6, 128, 4, 64
X_HEADS, X_HD, MEM_LEN = 4, 64, 256
D_FF, FFN_CONV = 2816, 3
A_Q, A_KV, X_Q = 768, 128, 256
B_QK, B_V, B_QKV = 384, 768, 1536
IN_A, IN_B = 1280, 2572
IN_BP = 2688
BP_Z, BP_XQ, BP_GATE = 1536, 2304, 2560
HALO = 8
GLU_HALO = 16

N_DEV = 8
GU_SHARD = 2 * D_FF // N_DEV
FF_BLOCKS = D_FF // GU_SHARD
DN_SHARD = D_FF // N_DEV
IA_SHARD = IN_A // N_DEV

ADAM_LR, ADAM_B1, ADAM_B2, ADAM_EPS, ADAM_WD, ADAM_STEP = 0.001, 0.9, 0.999, 1e-08, 0.01, 10

SP_CB, SP_CW, SP_QKV, SP_MIX, SP_MEM, SP_FFN, SP_FINAL, SP_MISC, SMALL_ROWS = 0, 2, 8, 12, 14, 16, 18, 19, 24
SP_REL_LANE = B_QKV


def _cp(*sems):
    return pltpu.CompilerParams(dimension_semantics=sems, vmem_limit_bytes=_VMEM_LIMIT)


def _mm(a, b):
    return jnp.dot(a.astype(_MXU), b.astype(_MXU), preferred_element_type=F32)


def _mm_nt(a, b):
    return lax.dot_general(a.astype(_MXU), b.astype(_MXU), (((1,), (1,)), ((), ())), preferred_element_type=F32)


def _mm_tn(a, b):
    return lax.dot_general(a.astype(_MXU), b.astype(_MXU), (((0,), (0,)), ((), ())), preferred_element_type=F32)


def _mmf(a, b):
    return jnp.dot(a, b, preferred_element_type=F32, precision=_HI)


def _mmf_nt(a, b):
    return lax.dot_general(a, b, (((1,), (1,)), ((), ())), preferred_element_type=F32, precision=_HI)


def _silu(x):
    return x * jax.nn.sigmoid(x)


def _w2d(ref):
    v = ref[...]
    return v.reshape(-1, v.shape[-1])


def _rows(m):
    return min(m, _TM)


def _spec_rowsharded(layer, rows, cols, col_block=None):
    if col_block is None:
        return pl.BlockSpec((N_DEV, None, rows, cols), lambda *_: (0, layer, 0, 0))
    return pl.BlockSpec((N_DEV, None, rows, cols), lambda *ids: (0, layer, 0, ids[col_block]))


def _spec_gate_up(axis):
    return pl.BlockSpec((None, GU_SHARD, D), lambda *ids: (ids[axis], 0, 0))


def _spec_down(axis):
    return pl.BlockSpec((2, DN_SHARD, D), lambda *ids: (ids[axis], 0, 0))


def _dep_specs(deps):
    return [pl.BlockSpec(memory_space=pl.ANY) for d in deps]


def _spec_gu_act(row_axis, axis, tm):
    return pl.BlockSpec((None, None, tm, GU_SHARD), lambda *ids: (ids[axis] // FF_BLOCKS, ids[axis] % FF_BLOCKS, ids[row_axis], 0))


def _norm_matmul(x, g, w, w_spec, n_blocks, out_shape, out_spec, name, deps=(), out_dtype=F32, w_t=False, tm=None):
    m, k = x.shape
    tm = tm or _rows(m)

    def body(x_ref, g_ref, w_ref, *rest):
        y_ref, hn_ref = rest[-2:]

        @pl.when(pl.program_id(1) == 0)
        def _():
            xv = x_ref[...]
            r = lax.rsqrt(jnp.mean(xv * xv, axis=-1, keepdims=True) + EPS)
            hn_ref[...] = (xv * r * g_ref[...]).astype(hn_ref.dtype)

        y_ref[...] = (_mm_nt if w_t else _mm)(hn_ref[...], _w2d(w_ref)).astype(y_ref.dtype)

    return pl.pallas_call(
        body, grid=(m // tm, n_blocks),
        in_specs=[pl.BlockSpec((tm, k), lambda i, j: (i, 0)), pl.BlockSpec((1, k), lambda i, j: (0, 0)), w_spec]
        + _dep_specs(deps),
        out_specs=[out_spec, pl.BlockSpec((tm, k), lambda i, j: (i, 0))],
        out_shape=[_SDS(out_shape, out_dtype), _SDS((m, k), _ACT)],
        name=name, compiler_params=_cp("arbitrary", "arbitrary"))(x, g, w, *deps)


def _matmul_res(a, a_spec, w, w_spec, n_k, res, name):
    m, n = res.shape
    tm = _rows(m)

    def body(a_ref, w_ref, r_ref, o_ref):
        part = _mm(a_ref[...], _w2d(w_ref))

        @pl.when(pl.program_id(1) == 0)
        def _():
            o_ref[...] = r_ref[...] + part

        @pl.when(pl.program_id(1) > 0)
        def _():
            o_ref[...] += part

    return pl.pallas_call(
        body, grid=(m // tm, n_k),
        in_specs=[a_spec, w_spec, pl.BlockSpec((tm, n), lambda i, j: (i, 0))],
        out_specs=pl.BlockSpec((tm, n), lambda i, j: (i, 0)),
        out_shape=_SDS((m, n), F32), name=name, compiler_params=_cp("arbitrary", "arbitrary"))(a, w, res)


def _matmul_nt(dy, w, w_spec, n_blocks, out_shape, out_spec, name, deps=(), out_dtype=F32):
    m, n = dy.shape
    tm = _rows(m)

    def body(dy_ref, w_ref, *rest):
        o_ref = rest[-1]
        o_ref[...] = _mm_nt(dy_ref[...], _w2d(w_ref)).astype(o_ref.dtype)

    return pl.pallas_call(
        body, grid=(m // tm, n_blocks),
        in_specs=[pl.BlockSpec((tm, n), lambda i, j: (i, 0)), w_spec] + _dep_specs(deps),
        out_specs=out_spec, out_shape=_SDS(out_shape, out_dtype),
        name=name, compiler_params=_cp("arbitrary", "arbitrary"))(dy, w, *deps)


def _matmul_nt_normbwd(dy, dy_spec, w, w_spec, nj, h, g, dh_in, name, w_t=False, act_copy=False):
    m, k = h.shape
    tm = _rows(m)

    def body(dy_ref, w_ref, h_ref, g_ref, dhin_ref, dh_ref, *rest):
        dg_ref, acc_ref = rest[-2:]
        i, j = pl.program_id(0), pl.program_id(1)

        @pl.when(j == 0)
        def _():
            acc_ref[...] = jnp.zeros_like(acc_ref)

        acc_ref[...] += (_mm if w_t else _mm_nt)(dy_ref[...], _w2d(w_ref))

        @pl.when(j == nj - 1)
        def _():
            xv = h_ref[...]
            r = lax.rsqrt(jnp.mean(xv * xv, axis=-1, keepdims=True) + EPS)
            xh = xv * r
            dhn = acc_ref[...]
            part = jnp.sum(dhn * xh, axis=0, keepdims=True)

            @pl.when(i == 0)
            def _():
                dg_ref[...] = part

            @pl.when(i > 0)
            def _():
                dg_ref[...] += part

            t = dhn * g_ref[...]
            dh = dhin_ref[...] + r * (t - xh * jnp.mean(t * xh, axis=-1, keepdims=True))
            dh_ref[...] = dh
            if act_copy:
                rest[0][...] = dh.astype(_ACT)

    rows = pl.BlockSpec((tm, k), lambda i, j: (i, 0))
    outs = pl.pallas_call(
        body, grid=(m // tm, nj),
        in_specs=[dy_spec, w_spec, rows, pl.BlockSpec((1, k), lambda i, j: (0, 0)), rows],
        out_specs=[rows] + [rows] * act_copy + [pl.BlockSpec((1, k), lambda i, j: (0, 0))],
        out_shape=[_SDS((m, k), F32)] + [_SDS((m, k), _ACT)] * act_copy + [_SDS((1, k), F32)],
        scratch_shapes=[pltpu.VMEM((tm, k), F32)],
        name=name, compiler_params=_cp("arbitrary", "arbitrary"))(dy, w, h, g, dh_in)
    return outs[0], (outs[1] if act_copy else None), outs[-1]


def _matmul_tn(x, x_spec, dy, dy_spec, m, n_blocks, acc_shape, out_shape, out_spec, name, tm=None):
    tm = tm or _rows(m)
    nm = m // tm

    def body(x_ref, dy_ref, o_ref, acc_ref):
        @pl.when(pl.program_id(1) == 0)
        def _():
            acc_ref[...] = jnp.zeros_like(acc_ref)

        acc_ref[...] += _mm_tn(x_ref[...], dy_ref[...])

        @pl.when(pl.program_id(1) == nm - 1)
        def _():
            o_ref[...] = acc_ref[...].reshape(o_ref.shape).astype(o_ref.dtype)

    return pl.pallas_call(
        body, grid=(n_blocks, nm), in_specs=[x_spec, dy_spec], out_specs=out_spec,
        out_shape=_SDS(out_shape, _WIRE), scratch_shapes=[pltpu.VMEM(acc_shape, F32)],
        name=name, compiler_params=_cp("arbitrary", "arbitrary"))(x, dy)


def _loss_head(h, g, tgt, name):
    m, k = h.shape
    tm = _rows(m)

    def body(h_ref, g_ref, t_ref, loss_ref, dh_ref, dha_ref, dg_ref):
        i = pl.program_id(0)
        xv = h_ref[...]
        r = lax.rsqrt(jnp.mean(xv * xv, axis=-1, keepdims=True) + EPS)
        xh = xv * r
        gv = g_ref[...]
        err = xh * gv - t_ref[...]
        lpart = jnp.zeros((1, 128), F32) + 0.5 * jnp.sum(jnp.mean(err * err, axis=-1, keepdims=True), axis=0, keepdims=True)
        dy = err * (1.0 / k)
        gpart = jnp.sum(dy * xh, axis=0, keepdims=True)

        @pl.when(i == 0)
        def _():
            loss_ref[...] = lpart
            dg_ref[...] = gpart

        @pl.when(i > 0)
        def _():
            loss_ref[...] += lpart
            dg_ref[...] += gpart

        t = dy * gv
        dh = r * (t - xh * jnp.mean(t * xh, axis=-1, keepdims=True))
        dh_ref[...] = dh
        dha_ref[...] = dh.astype(_ACT)

    rows = pl.BlockSpec((tm, k), lambda i: (i, 0))
    return pl.pallas_call(
        body, grid=(m // tm,),
        in_specs=[rows, pl.BlockSpec((1, k), lambda i: (0, 0)), rows],
        out_specs=[pl.BlockSpec((1, 128), lambda i: (0, 0)), rows, rows, pl.BlockSpec((1, k), lambda i: (0, 0))],
        out_shape=[_SDS((1, 128), F32), _SDS((m, k), F32), _SDS((m, k), _ACT), _SDS((1, k), F32)],
        name=name, compiler_params=_cp("arbitrary"))(h, g, tgt)


def _glu_down(gu, conv_w, conv_b, w_down, res, name):
    s = gu.shape[2]
    tm = min(s, _TM_GLU)

    def body(gu_ref, prev_ref, w_ref, b_ref, wdn_ref, r_ref, o_ref, act_ref, gc_ref):
        i, j = pl.program_id(0), pl.program_id(1)
        prev = jnp.where(i > 0, prev_ref[...].astype(F32), 0.0)
        ext = jnp.concatenate([prev, gu_ref[0].astype(F32)], axis=0)
        gc = b_ref[...] + w_ref[FFN_CONV - 1:FFN_CONV, :] * ext
        for k in range(FFN_CONV - 1):
            gc = gc + w_ref[k:k + 1, :] * pltpu.roll(ext, FFN_CONV - 1 - k, 0)
        gc = gc[GLU_HALO:]
        gc_ref[...] = gc.astype(gc_ref.dtype)
        act =(_silu(gc) * gu_ref[1].astype(F32)).astype(act_ref.dtype)
        act_ref[...] = act
        part = _mm(act, _w2d(wdn_ref))

        @pl.when(j == 0)
        def _():
            o_ref[...] = r_ref[...] + part

        @pl.when(j > 0)
        def _():
            o_ref[...] += part

    return pl.pallas_call(
        body, grid=(s // tm, FF_BLOCKS),
        in_specs=[pl.BlockSpec((2, None, tm, GU_SHARD), lambda i, j: (0, j, i, 0)),
                  pl.BlockSpec((None, None, GLU_HALO, GU_SHARD),
                               lambda i, j: (0, j, jnp.maximum(i * (tm // GLU_HALO) - 1, 0), 0)),
                  pl.BlockSpec((None, HALO, GU_SHARD), lambda i, j: (j, 0, 0)),
                  pl.BlockSpec((None, 1, GU_SHARD), lambda i, j: (j, 0, 0)),
                  _spec_down(1), pl.BlockSpec((tm, D), lambda i, j: (i, 0))],
        out_specs=[pl.BlockSpec((tm, D), lambda i, j: (i, 0)), pl.BlockSpec((None, tm, GU_SHARD), lambda i, j: (j, i, 0)),
                   pl.BlockSpec((None, tm, GU_SHARD), lambda i, j: (j, i, 0))],
        out_shape=[_SDS((s, D), F32), _SDS((FF_BLOCKS, s, GU_SHARD), _ACT), _SDS((FF_BLOCKS, s, GU_SHARD), _ACT)], name=name,
        compiler_params=_cp("arbitrary", "arbitrary"))(gu, gu, conv_w, conv_b, w_down, res)


def _glu_bwd(gu, gc, conv_w, dh, w_down, name, deps=()):
    s = gu.shape[2]
    tm = min(s, _TM_GLU)
    nt = s // tm
    ext_rows = tm + GLU_HALO

    def body(gu_ref, prev_ref, gc_ref, w_ref, dh_ref, wdn_ref, *rest):
        dgu_ref, dw_ref, db_ref, carry_ref = rest[-4:]
        t = pl.program_id(1)
        i = nt - 1 - t

        @pl.when(t == 0)
        def _():
            carry_ref[...] = jnp.zeros_like(carry_ref)
            dw_ref[...] = jnp.zeros_like(dw_ref)
            db_ref[...] = jnp.zeros_like(db_ref)

        up = gu_ref[1].astype(F32)
        prev = jnp.where(i > 0, prev_ref[...].astype(F32), 0.0)
        ext = jnp.concatenate([prev, gu_ref[0].astype(F32)], axis=0)
        gc = gc_ref[...].astype(F32)
        sg = jax.nn.sigmoid(gc)
        da = _mm_nt(dh_ref[...], _w2d(wdn_ref))
        dup = da * (gc * sg)
        dgc = da * up * (sg * (1.0 + gc * (1.0 - sg)))
        db_ref[...] += jnp.sum(dgc, axis=0, keepdims=True)
        dgc_ext = jnp.concatenate([jnp.zeros((GLU_HALO, GU_SHARD), F32), dgc], axis=0)
        ahead = [pltpu.roll(dgc_ext, ext_rows - (FFN_CONV - 1 - j), 0) if j < FFN_CONV - 1 else dgc_ext
                 for j in range(FFN_CONV)]
        dext = ahead[0] * w_ref[0:1, :]
        for j in range(FFN_CONV):
            dw_ref[j:j + 1, :] += jnp.sum(ext * ahead[j], axis=0, keepdims=True)
            if j > 0:
                dext = dext + ahead[j] * w_ref[j:j + 1, :]
        tail = jnp.concatenate([jnp.zeros((tm - GLU_HALO, GU_SHARD), F32), carry_ref[...]], axis=0)
        dgate = dext[GLU_HALO:] + tail
        carry_ref[...] = dext[:GLU_HALO]
        dgu_ref[0] = dgate.astype(dgu_ref.dtype)
        dgu_ref[1] = dup.astype(dgu_ref.dtype)

    return pl.pallas_call(
        body, grid=(FF_BLOCKS, nt),
        in_specs=[pl.BlockSpec((2, None, tm, GU_SHARD), lambda j, t: (0, j, nt - 1 - t, 0)),
                  pl.BlockSpec((None, None, GLU_HALO, GU_SHARD),
                               lambda j, t: (0, j, jnp.maximum((nt - 1 - t) * (tm // GLU_HALO) - 1, 0), 0)),
                  pl.BlockSpec((None, tm, GU_SHARD), lambda j, t: (j, nt - 1 - t, 0)),
                  pl.BlockSpec((None, HALO, GU_SHARD), lambda j, t: (j, 0, 0)),
                  pl.BlockSpec((tm, D), lambda j, t: (nt - 1 - t, 0)), _spec_down(0)] + _dep_specs(deps),
        out_specs=[pl.BlockSpec((2, None, tm, GU_SHARD), lambda j, t: (0, j, nt - 1 - t, 0)),
                   pl.BlockSpec((None, HALO, GU_SHARD), lambda j, t: (j, 0, 0)),
                   pl.BlockSpec((None, 1, GU_SHARD), lambda j, t: (j, 0, 0))],
        out_shape=[_SDS(gu.shape, _ACT), _SDS((FF_BLOCKS, HALO, GU_SHARD), F32), _SDS((FF_BLOCKS, 1, GU_SHARD), F32)],
        scratch_shapes=[pltpu.VMEM((GLU_HALO, GU_SHARD), F32)],
        name=name, compiler_params=_cp("arbitrary", "arbitrary"))(gu, gu, gc, conv_w, dh, w_down, *deps)


def _bucket_table():
    qi = np.arange(BLK)[:, None]
    kj = np.arange(BLK)[None, :]
    n = np.where(kj > qi, BLK + qi - kj, qi - kj)
    max_exact = N_BUCKETS // 2
    nf = np.maximum(n, 1).astype(np.float32)
    large = max_exact + (np.log(nf / max_exact) / math.log(MAX_DISTANCE / max_exact)
                         * (N_BUCKETS - max_exact)).astype(np.int32)
    large = np.minimum(large, N_BUCKETS - 1)
    return np.where(n < max_exact, n, large).astype(np.int32)


def _lane_low():
    return lax.broadcasted_iota(jnp.int32, (1, 128), 1) < A_HD


def _swa_groups(q, kd, vd, sink, bias, upper, first):
    n = A_HEADS // A_KV_HEADS
    ng = len(q)
    low = _lane_low()
    qm = [jnp.concatenate([jnp.where(low == (h % 2 == 0), q[g][:, (h // 2) * 128:(h // 2 + 1) * 128], 0.0) for h in range(n)], axis=0)
          for g in range(ng)]
    s2 = [_mm_nt(qm[g], kd[g]) * (A_HD ** -0.5) for g in range(ng)]
    s = [jnp.where(upper[None], s2[g][:, :BLK].reshape(n, BLK, BLK), s2[g][:, BLK:].reshape(n, BLK, BLK)) + bias[g] for g in range(ng)]
    s = [t if f is None else jnp.where((upper & f)[None], -jnp.inf, t) for t, f in zip(s, first)]
    m = [lax.stop_gradient(jnp.maximum(jnp.max(s[g], axis=-1, keepdims=True), sink[g])) for g in range(ng)]
    p = [jnp.exp(s[g] - m[g]) for g in range(ng)]
    split = [jnp.concatenate([jnp.where(upper[None], t, 0.0), jnp.where(upper[None], 0.0, t)], axis=-1).reshape(n * BLK, 2 * BLK)
             for t in p]
    ones = jnp.ones((BLK, 128), F32)
    den = [_mm(p[g].reshape(n * BLK, BLK), ones) + jnp.exp(sink[g] - m[g]).reshape(n * BLK, 1) for g in range(ng)]
    o = [_mm(split[g], vd[g]) / den[g] for g in range(ng)]
    return [jnp.concatenate([jnp.where(low, t[2 * k * BLK:(2 * k + 1) * BLK], t[(2 * k + 1) * BLK:(2 * k + 2) * BLK])
                             for k in range(n // 2)], axis=1) for t in o]


def _mix_a_core(q, kd, vd, sink, bias, xq, mk, mv, upper, first):
    return _swa_groups(q, kd, vd, sink, bias, upper, first), _cross_pairs(xq, mk, mv)


def _swa_sinks(sink_ref, g):
    n = A_HEADS // A_KV_HEADS
    return jnp.concatenate([sink_ref[:, h:h + 1] for h in range(g * n, (g + 1) * n)], axis=0).reshape(n, 1, 1)


def _both_halves(t, t_rolled, g):
    low = _lane_low()
    return jnp.where(low, t, t_rolled) if g == 0 else jnp.where(low, t_rolled, t)


def _cross_pairs(q, mk, mv):
    rows = q.shape[0]
    low = _lane_low()
    qm = [jnp.concatenate([jnp.where(low, q[:, p * 128:(p + 1) * 128], 0.0), jnp.where(low, 0.0, q[:, p * 128:(p + 1) * 128])], axis=0)
          for p in range(X_HEADS // 2)]
    s = [_mm_nt(qm[p], mk[:, p * 128:(p + 1) * 128]) * (X_HD ** -0.5) for p in range(X_HEADS // 2)]
    e = [jnp.exp(t - lax.stop_gradient(jnp.max(t, axis=-1, keepdims=True))) for t in s]
    pr = [t / jnp.sum(t, axis=-1, keepdims=True) for t in e]
    o = [_mm(pr[p], mv[:, p * 128:(p + 1) * 128]) for p in range(X_HEADS // 2)]
    return jnp.concatenate([jnp.where(low, t[:rows], t[rows:]) for t in o], axis=1)


def _swa_upper():
    qi = lax.broadcasted_iota(jnp.int32, (BLK, BLK), 0)
    kj = lax.broadcasted_iota(jnp.int32, (BLK, BLK), 1)
    return kj > qi


def _bias_build(rel_bias, bucket, name):
    def body(rb_ref, bucket_ref, o_ref):
        b = bucket_ref[...]
        for h in range(A_HEADS):
            acc = jnp.zeros((BLK, BLK), F32)
            for k in range(N_BUCKETS):
                acc = jnp.where(b == k, rb_ref[k, h], acc)
            o_ref[h] = acc

    return pl.pallas_call(
        body, in_specs=[pl.BlockSpec(memory_space=pltpu.SMEM), pl.BlockSpec(memory_space=pltpu.VMEM)],
        out_specs=pl.BlockSpec(memory_space=pltpu.VMEM),
        out_shape=_SDS((A_HEADS, BLK, BLK), F32), name=name)(rel_bias, bucket)


def _bias_reduce(dbias, bucket, name):
    def body(db_ref, bucket_ref, o_ref):
        b = bucket_ref[...]
        row = lax.broadcasted_iota(jnp.int32, (N_BUCKETS, 128), 0)
        lane = lax.broadcasted_iota(jnp.int32, (N_BUCKETS, 128), 1)
        acc = jnp.zeros((N_BUCKETS, 128), F32)
        for h in range(A_HEADS):
            v = db_ref[h]
            for k in range(N_BUCKETS):
                sk = jnp.sum(jnp.sum(jnp.where(b == k, v, 0.0), axis=1, keepdims=True), axis=0, keepdims=True)
                acc = acc + jnp.where((row == k) & (lane == h), sk, 0.0)
        o_ref[...] = acc

    return pl.pallas_call(
        body, in_specs=[pl.BlockSpec(memory_space=pltpu.VMEM)] * 2,
        out_specs=pl.BlockSpec(memory_space=pltpu.VMEM),
        out_shape=_SDS((N_BUCKETS, 128), F32), name=name)(dbias, bucket)


def _mix_a_fwd(proj, bias, sinks, memkv, name):
    s = proj.shape[0]
    per = 2
    nb = s // (per * BLK)
    grp = A_HEADS // A_KV_HEADS

    def body(proj_ref, prev_ref, bias_ref, sink_ref, memkv_ref, o_ref):
        i = pl.program_id(0)
        upper = _swa_upper()
        proj = proj_ref[...].astype(F32)
        kv = jnp.concatenate([prev_ref[...].astype(F32), proj[:, A_Q:A_Q + 2 * A_KV]], axis=0)
        k, v = kv[:, :A_KV], kv[:, A_KV:]
        k_r = pltpu.roll(k, A_HD, 1)
        v_r = pltpu.roll(v, A_HD, 1)
        gw = A_Q // A_KV_HEADS
        each = [(b, g) for b in range(per) for g in range(A_KV_HEADS)]

        def window(a, a_r, b, g):
            return _both_halves(a[b * BLK:(b + 2) * BLK], a_r[b * BLK:(b + 2) * BLK], g)

        swa, cross = _mix_a_core([proj[b * BLK:(b + 1) * BLK, g * gw:(g + 1) * gw] for b, g in each],
                                 [window(k, k_r, b, g) for b, g in each], [window(v, v_r, b, g) for b, g in each],
                                 [_swa_sinks(sink_ref, g) for b, g in each], [bias_ref[g * grp:(g + 1) * grp] for b, g in each],
                                 proj[:, A_Q + 2 * A_KV:], memkv_ref[:, :X_Q], memkv_ref[:, X_Q:], upper,
                                 [(i == 0) if b == 0 else None for b, g in each])
        for b in range(per):
            o_ref[b * BLK:(b + 1) * BLK, :] = jnp.concatenate(
                swa[b * A_KV_HEADS:(b + 1) * A_KV_HEADS] + [cross[b * BLK:(b + 1) * BLK]], axis=1).astype(o_ref.dtype)

    return pl.pallas_call(
        body, grid=(nb,),
        in_specs=[pl.BlockSpec((per * BLK, IN_A), lambda i: (i, 0)),
                  pl.BlockSpec((BLK, 2 * A_KV), lambda i: (jnp.maximum(per * i - 1, 0), A_Q // (2 * A_KV))),
                  pl.BlockSpec((A_HEADS, BLK, BLK), lambda i: (0, 0, 0)),
                  pl.BlockSpec((1, 128), lambda i: (0, 0)),
                  pl.BlockSpec((MEM_LEN, 2 * X_Q), lambda i: (0, 0))],
        out_specs=pl.BlockSpec((per * BLK, D), lambda i: (i, 0)),
        out_shape=_SDS((s, D), _ACT), name=name, compiler_params=_cp("arbitrary"))(proj, proj, bias, sinks, memkv)


def _mix_a_bwd(proj, bias, sinks, memkv, dmix, name, deps=()):
    s = proj.shape[0]
    per = 2
    nb = s // (per * BLK)
    grp = A_HEADS // A_KV_HEADS

    def body(proj_ref, prev_ref, bias_ref, sink_ref, memkv_ref, dmix_ref, *rest):
        dproj_ref, dbias_ref, dsink_ref, dmemkv_ref, carry_ref = rest[-5:]
        t = pl.program_id(0)
        i = nb - 1 - t

        @pl.when(t == 0)
        def _():
            carry_ref[...] = jnp.zeros_like(carry_ref)
            dbias_ref[...] = jnp.zeros_like(dbias_ref)
            dsink_ref[...] = jnp.zeros_like(dsink_ref)
            dmemkv_ref[...] = jnp.zeros_like(dmemkv_ref)

        upper = _swa_upper()
        lane = lax.broadcasted_iota(jnp.int32, (1, 128), 1)
        low = _lane_low()
        proj = proj_ref[...].astype(F32)
        kv = jnp.concatenate([prev_ref[...].astype(F32), proj[:, A_Q:A_Q + 2 * A_KV]], axis=0)
        k, v = kv[:, :A_KV], kv[:, A_KV:]
        k_r = pltpu.roll(k, A_HD, 1)
        v_r = pltpu.roll(v, A_HD, 1)
        gw = A_Q // A_KV_HEADS
        each = [(b, g) for b in range(per) for g in range(A_KV_HEADS)]

        def window(a, a_r, b, g):
            return _both_halves(a[b * BLK:(b + 2) * BLK], a_r[b * BLK:(b + 2) * BLK], g)

        _, vjp = jax.vjp(
            functools.partial(_mix_a_core, upper=upper, first=[(i == 0) if b == 0 else None for b, g in each]),
            [proj[b * BLK:(b + 1) * BLK, g * gw:(g + 1) * gw] for b, g in each],
            [window(k, k_r, b, g) for b, g in each], [window(v, v_r, b, g) for b, g in each],
            [_swa_sinks(sink_ref, g) for b, g in each], [bias_ref[g * grp:(g + 1) * grp] for b, g in each],
            proj[:, A_Q + 2 * A_KV:], memkv_ref[:, :X_Q], memkv_ref[:, X_Q:])
        dqs, dk, dv, ds, db, dxq, dmk, dmv = vjp(
            ([dmix_ref[b * BLK:(b + 1) * BLK, g * gw:(g + 1) * gw].astype(F32) for b, g in each], dmix_ref[:, A_Q:].astype(F32)))
        dkd = [t + pltpu.roll(t, A_HD, 1) for t in dk]
        dvd = [t + pltpu.roll(t, A_HD, 1) for t in dv]
        dsink = jnp.zeros((1, 128), F32)
        for e, (b, g) in enumerate(each):
            for h in range(grp):
                dsink = dsink + jnp.where(lane == g * grp + h, ds[e][h], 0.0)
        for g in range(A_KV_HEADS):
            dbias_ref[g * grp:(g + 1) * grp] += db[g] + db[A_KV_HEADS + g]
        dsink_ref[...] += dsink
        dmemkv_ref[...] += jnp.concatenate([dmk, dmv], axis=1)
        dkv = [jnp.concatenate([jnp.where(low, dkd[b * A_KV_HEADS], dkd[b * A_KV_HEADS + 1]),
                                jnp.where(low, dvd[b * A_KV_HEADS], dvd[b * A_KV_HEADS + 1])], axis=1) for b in range(per)]
        own = [dkv[0][BLK:] + dkv[1][:BLK], dkv[1][BLK:] + carry_ref[...]]
        carry_ref[...] = dkv[0][:BLK]
        for b in range(per):
            dproj_ref[b * BLK:(b + 1) * BLK, :] = jnp.concatenate(
                list(dqs[b * A_KV_HEADS:(b + 1) * A_KV_HEADS]) + [own[b], dxq[b * BLK:(b + 1) * BLK]], axis=1).astype(dproj_ref.dtype)

    return pl.pallas_call(
        body, grid=(nb,),
        in_specs=[pl.BlockSpec((per * BLK, IN_A), lambda t: (nb - 1 - t, 0)),
                  pl.BlockSpec((BLK, 2 * A_KV), lambda t: (jnp.maximum(per * (nb - 1 - t) - 1, 0), A_Q // (2 * A_KV))),
                  pl.BlockSpec((A_HEADS, BLK, BLK), lambda t: (0, 0, 0)),
                  pl.BlockSpec((1, 128), lambda t: (0, 0)),
                  pl.BlockSpec((MEM_LEN, 2 * X_Q), lambda t: (0, 0)),
                  pl.BlockSpec((per * BLK, D), lambda t: (nb - 1 - t, 0))] + _dep_specs(deps),
        out_specs=[pl.BlockSpec((per * BLK, IN_A), lambda t: (nb - 1 - t, 0)),
                   pl.BlockSpec((A_HEADS, BLK, BLK), lambda t: (0, 0, 0)),
                   pl.BlockSpec((1, 128), lambda t: (0, 0)),
                   pl.BlockSpec((MEM_LEN, 2 * X_Q), lambda t: (0, 0))],
        out_shape=[_SDS((s, IN_A), _ACT), _SDS((A_HEADS, BLK, BLK), F32), _SDS((1, 128), F32),
                   _SDS((MEM_LEN, 2 * X_Q), F32)],
        scratch_shapes=[pltpu.VMEM((BLK, 2 * A_KV), F32)],
        name=name, compiler_params=_cp("arbitrary"))(proj, proj, bias, sinks, memkv, dmix, *deps)


def _neumann(pw, rhs):
    nh = len(pw)
    x = rhs
    for lvl in range(6):
        if lvl < 5:
            prod = [_mmf(pw[h], jnp.concatenate([x[h], pw[h]], axis=1)) for h in range(nh)]
            x = [x[h] + prod[h][:, :B_HD] for h in range(nh)]
            pw = [t[:, B_HD:] for t in prod]
        else:
            x = [x[h] + _mmf(pw[h], x[h]) for h in range(nh)]
    return x


@jax.custom_vjp
def _tri_solve(pw, rhs):
    return _neumann(pw, rhs)


def _tri_solve_fwd(pw, rhs):
    x = _neumann(pw, rhs)
    return x, (pw, x)


def _tri_solve_bwd(res, dx):
    pw, x = res
    d_rhs = _neumann([t.T for t in pw], list(dx))
    return [_mmf_nt(d_rhs[h], x[h]) for h in range(len(pw))], d_rhs


_tri_solve.defvjp(_tri_solve_fwd, _tri_solve_bwd)


@jax.custom_vjp
def _tri_solved(pw, rhs, x):
    return x


def _tri_solved_fwd(pw, rhs, x):
    return x, (pw, x)


def _tri_solved_bwd(res, dx):
    d_pw, d_rhs = _tri_solve_bwd(res, dx)
    return d_pw, d_rhs, [jnp.zeros_like(t) for t in res[1]]


_tri_solved.defvjp(_tri_solved_fwd, _tri_solved_bwd)


@jax.custom_vjp
def _known(x, value):
    return value


def _known_fwd(x, value):
    return value, None


def _known_bwd(_, g):
    return g, jnp.zeros_like(g)


_known.defvjp(_known_fwd, _known_bwd)


def _dn_heads(yq, yk, yv, z, bl, al, a_log, dtb, ng, s0, solved=None, out_known=None):
    c = CHUNK
    nh = B_V_HEADS
    rep = B_V_HEADS // B_QK_HEADS
    r = lax.broadcasted_iota(jnp.int32, (c, c), 0)
    cc = lax.broadcasted_iota(jnp.int32, (c, c), 1)
    q = [_silu(t) for t in yq]
    k = [_silu(t) for t in yk]
    v = [_silu(t) for t in yv]
    q = [t * lax.rsqrt(jnp.sum(t * t, axis=-1, keepdims=True) + EPS) * (B_HD ** -0.5) for t in q]
    k = [t * lax.rsqrt(jnp.sum(t * t, axis=-1, keepdims=True) + EPS) for t in k]
    beta = [jax.nn.sigmoid(t) for t in bl]
    g = [-jnp.exp(a_log[h]) * jax.nn.softplus(al[h] + dtb[h]) for h in range(nh)]
    gb = [jnp.broadcast_to(t, (c, c)) for t in g]
    gc_col = [jnp.sum(jnp.where(cc <= r, t.T, 0.0), axis=1, keepdims=True) for t in gb]
    gc_row = [jnp.sum(jnp.where(r <= cc, t, 0.0), axis=0, keepdims=True) for t in gb]
    gc_last = [jnp.sum(t, axis=0, keepdims=True) for t in g]
    decay = [jnp.exp(jnp.where(r >= cc, gc_col[h] - gc_row[h], -jnp.inf)) for h in range(nh)]
    kq = [_mmf_nt(jnp.concatenate([k[h], q[h]], axis=0), k[h]) for h in range(B_QK_HEADS)]
    kk = [t[:c] for t in kq]
    qk = [t[c:] for t in kq]
    egc = [jnp.exp(t) for t in gc_col]
    both = [_mmf(jnp.concatenate([(beta[h] * egc[h]) * k[h // rep], q[h // rep] * egc[h]], axis=0), s0[h]) for h in range(nh)]
    rhs = [beta[h] * v[h] - both[h][:c] for h in range(nh)]
    qs0 = [t[c:] for t in both]
    pw = [-(beta[h] * kk[h // rep] * jnp.where(r > cc, decay[h], 0.0)) for h in range(nh)]
    delta = _tri_solve(pw, rhs) if solved is None else _tri_solved(pw, rhs, solved)
    last = [_mmf(jnp.concatenate([qk[h // rep] * decay[h], (k[h // rep] * jnp.exp(gc_last[h] - gc_col[h])).T], axis=0), delta[h])
            for h in range(nh)]
    out = [qs0[h] + last[h][:c] for h in range(nh)]
    if out_known is not None:
        out = [_known(out[h], out_known[h]) for h in range(nh)]
    s1 = [jnp.exp(gc_last[h]) * s0[h] + last[h][c:] for h in range(nh)]
    o = [t * lax.rsqrt(jnp.mean(t * t, axis=-1, keepdims=True) + EPS) * ng for t in out]
    return [o[h] * _silu(z[h]) for h in range(nh)], s1, delta, out


def _dn_conv(ext, w_ref):
    y = ext * w_ref[B_CONV - 1:B_CONV, :]
    for j in range(B_CONV - 1):
        y = y + w_ref[j:j + 1, :] * pltpu.roll(ext, B_CONV - 1 - j, 0)
    return y


def _dn_args(y, cur_ref, par_ref, ng_ref):
    nh = B_V_HEADS
    return ([y[:, h * B_HD:(h + 1) * B_HD] for h in range(B_QK_HEADS)],
            [y[:, B_QK + h * B_HD:B_QK + (h + 1) * B_HD] for h in range(B_QK_HEADS)],
            [y[:, 2 * B_QK + h * B_HD:2 * B_QK + (h + 1) * B_HD] for h in range(nh)],
            [cur_ref[:, BP_Z + h * B_HD:BP_Z + (h + 1) * B_HD] for h in range(nh)],
            [cur_ref[:, BP_GATE + h:BP_GATE + h + 1] for h in range(nh)],
            [cur_ref[:, BP_GATE + nh + h:BP_GATE + nh + h + 1] for h in range(nh)],
            [par_ref[:, h:h + 1] for h in range(nh)], [par_ref[:, nh + h:nh + h + 1] for h in range(nh)], ng_ref[...])


def _mix_b_fwd(proj, conv_w, par, ng, memkv, name):
    s = proj.shape[0]
    nc = s // CHUNK

    def body(cur_ref, prev_ref, w_ref, par_ref, ng_ref, memkv_ref, o_ref, st_ref, dl_ref, state_ref):
        n = pl.program_id(0)

        @pl.when(n == 0)
        def _():
            state_ref[...] = jnp.zeros_like(state_ref)

        prev = jnp.where(n > 0, prev_ref[...], 0.0)
        ext = jnp.concatenate([prev, cur_ref[:, :B_QKV]], axis=0)
        y = _dn_conv(ext, w_ref)[HALO:]
        s0 = [state_ref[hv] for hv in range(B_V_HEADS)]
        st_ref[0] = state_ref[...]
        outs, s1, delta, raw = _dn_heads(*_dn_args(y, cur_ref, par_ref, ng_ref), s0)
        for hv in range(B_V_HEADS):
            state_ref[hv] = s1[hv]
            dl_ref[0, hv] = delta[hv]
            dl_ref[0, B_V_HEADS + hv] = raw[hv]
        outs = outs + [_cross_pairs(cur_ref[:, BP_XQ:BP_XQ + X_Q], memkv_ref[:, :X_Q], memkv_ref[:, X_Q:])]
        o_ref[...] = jnp.concatenate(outs, axis=1).astype(o_ref.dtype)

    return pl.pallas_call(
        body, grid=(nc,),
        in_specs=[pl.BlockSpec((CHUNK, IN_BP), lambda n: (n, 0)),
                  pl.BlockSpec((HALO, B_QKV), lambda n: (jnp.maximum(n * (CHUNK // HALO) - 1, 0), 0)),
                  pl.BlockSpec((HALO, B_QKV), lambda n: (0, 0)),
                  pl.BlockSpec((1, 128), lambda n: (0, 0)), pl.BlockSpec((1, 128), lambda n: (0, 0)),
                  pl.BlockSpec((MEM_LEN, 2 * X_Q), lambda n: (0, 0))],
        out_specs=[pl.BlockSpec((CHUNK, D), lambda n: (n, 0)),
                   pl.BlockSpec((1, B_V_HEADS, B_HD, B_HD), lambda n: (n, 0, 0, 0)),
                   pl.BlockSpec((1, 2 * B_V_HEADS, CHUNK, B_HD), lambda n: (n, 0, 0, 0))],
        out_shape=[_SDS((s, D), _ACT), _SDS((nc, B_V_HEADS, B_HD, B_HD), F32), _SDS((nc, 2 * B_V_HEADS, CHUNK, B_HD), F32)],
        scratch_shapes=[pltpu.VMEM((B_V_HEADS, B_HD, B_HD), F32)],
        name=name, compiler_params=_cp("arbitrary"))(proj, proj, conv_w, par, ng, memkv)


def _mix_b_bwd(proj, conv_w, par, ng, memkv, states, deltas, dmix, name):
    s = proj.shape[0]
    nc = s // CHUNK
    ext_rows = CHUNK + HALO

    def body(cur_ref, prev_ref, w_ref, par_ref, ng_ref, memkv_ref, st_ref, dl_ref, dmix_ref,
             dproj_ref, dw_ref, dpar_ref, dng_ref, dmemkv_ref, dstate_ref, carry_ref):
        t = pl.program_id(0)
        n = nc - 1 - t

        @pl.when(t == 0)
        def _():
            dstate_ref[...] = jnp.zeros_like(dstate_ref)
            carry_ref[...] = jnp.zeros_like(carry_ref)
            dw_ref[...] = jnp.zeros_like(dw_ref)
            dpar_ref[...] = jnp.zeros_like(dpar_ref)
            dng_ref[...] = jnp.zeros_like(dng_ref)
            dmemkv_ref[...] = jnp.zeros_like(dmemkv_ref)

        lane = lax.broadcasted_iota(jnp.int32, (1, 128), 1)
        prev = jnp.where(n > 0, prev_ref[...], 0.0)
        ext = jnp.concatenate([prev, cur_ref[:, :B_QKV]], axis=0)
        y = _dn_conv(ext, w_ref)[HALO:]
        solved = [dl_ref[0, hv] for hv in range(B_V_HEADS)]
        raw = [dl_ref[0, B_V_HEADS + hv] for hv in range(B_V_HEADS)]
        _, vjp = jax.vjp(functools.partial(_dn_heads, solved=solved, out_known=raw), *_dn_args(y, cur_ref, par_ref, ng_ref),
                         [st_ref[0, hv] for hv in range(B_V_HEADS)])
        none = [jnp.zeros((CHUNK, B_HD), F32)] * B_V_HEADS
        dyq, dyk, dyv, dz, gbl, gal, ga_log, gdtb, dng, gs0 = vjp(
            ([dmix_ref[:, hv * B_HD:(hv + 1) * B_HD].astype(F32) for hv in range(B_V_HEADS)],
             [dstate_ref[hv] for hv in range(B_V_HEADS)], none, none))
        dgate = jnp.zeros((CHUNK, 128), F32)
        dpar = jnp.zeros((1, 128), F32)
        for hv in range(B_V_HEADS):
            dstate_ref[hv] = gs0[hv]
            dgate = dgate + jnp.where(lane == hv, gbl[hv], 0.0) + jnp.where(lane == B_V_HEADS + hv, gal[hv], 0.0)
            dpar = dpar + jnp.where(lane == hv, ga_log[hv], 0.0) + jnp.where(lane == B_V_HEADS + hv, gdtb[hv], 0.0)
        dpar_ref[...] += dpar
        dng_ref[...] += dng
        _, vjp = jax.vjp(_cross_pairs, cur_ref[:, BP_XQ:BP_XQ + X_Q], memkv_ref[:, :X_Q], memkv_ref[:, X_Q:])
        dxq, dmk, dmv = vjp(dmix_ref[:, B_V:].astype(F32))
        dmemkv_ref[...] += jnp.concatenate([dmk, dmv], axis=1)
        dy = jnp.concatenate(list(dyq) + list(dyk) + list(dyv), axis=1)
        dy_ext = jnp.concatenate([jnp.zeros((HALO, B_QKV), F32), dy], axis=0)
        dext = dy_ext * w_ref[B_CONV - 1:B_CONV, :]
        dw_ref[B_CONV - 1:B_CONV, :] += jnp.sum(ext * dy_ext, axis=0, keepdims=True)
        for j in range(B_CONV - 1):
            sh = B_CONV - 1 - j
            dw_ref[j:j + 1, :] += jnp.sum(pltpu.roll(ext, sh, 0) * dy_ext, axis=0, keepdims=True)
            dext = dext + w_ref[j:j + 1, :] * pltpu.roll(dy_ext, ext_rows - sh, 0)
        tail = jnp.concatenate([jnp.zeros((CHUNK - HALO, B_QKV), F32), carry_ref[...]], axis=0)
        dqkv = dext[HALO:] + tail
        carry_ref[...] = dext[:HALO]
        dproj_ref[...] = jnp.concatenate([dqkv] + list(dz) + [dxq, dgate], axis=1).astype(dproj_ref.dtype)

    return pl.pallas_call(
        body, grid=(nc,),
        in_specs=[pl.BlockSpec((CHUNK, IN_BP), lambda t: (nc - 1 - t, 0)),
                  pl.BlockSpec((HALO, B_QKV), lambda t: (jnp.maximum((nc - 1 - t) * (CHUNK // HALO) - 1, 0), 0)),
                  pl.BlockSpec((HALO, B_QKV), lambda t: (0, 0)),
                  pl.BlockSpec((1, 128), lambda t: (0, 0)), pl.BlockSpec((1, 128), lambda t: (0, 0)),
                  pl.BlockSpec((MEM_LEN, 2 * X_Q), lambda t: (0, 0)),
                  pl.BlockSpec((1, B_V_HEADS, B_HD, B_HD), lambda t: (nc - 1 - t, 0, 0, 0)),
                  pl.BlockSpec((1, 2 * B_V_HEADS, CHUNK, B_HD), lambda t: (nc - 1 - t, 0, 0, 0)),
                  pl.BlockSpec((CHUNK, D), lambda t: (nc - 1 - t, 0))],
        out_specs=[pl.BlockSpec((CHUNK, IN_BP), lambda t: (nc - 1 - t, 0)),
                   pl.BlockSpec((HALO, B_QKV), lambda t: (0, 0)),
                   pl.BlockSpec((1, 128), lambda t: (0, 0)), pl.BlockSpec((1, 128), lambda t: (0, 0)),
                   pl.BlockSpec((MEM_LEN, 2 * X_Q), lambda t: (0, 0))],
        out_shape=[_SDS((s, IN_BP), _ACT), _SDS((HALO, B_QKV), F32), _SDS((1, 128), F32), _SDS((1, 128), F32),
                   _SDS((MEM_LEN, 2 * X_Q), F32)],
        scratch_shapes=[pltpu.VMEM((B_V_HEADS, B_HD, B_HD), F32), pltpu.VMEM((HALO, B_QKV), F32)],
        name=name, compiler_params=_cp("arbitrary"))(proj, proj, conv_w, par, ng, memkv, states, deltas, dmix)


def _place():
    return lax.axis_index("x"), lax.axis_index("y"), lax.axis_index("c")


def _all_gather(shards, name):
    n = len(shards)

    def body(*refs):
        ins, outs = refs[:n], refs[n:2 * n]
        send_sems, recv_sems, local_sems = refs[2 * n:]
        x, y, c = _place()
        me, sibling = (x, y, c), (x, y, 1 - c)
        chips = [(1 - x, y), (x, 1 - y), (1 - x, 1 - y)]

        def rows(a, px, py, pc):
            return outs[a].at[4 * px + 2 * py + pc]

        def copy(a, k, block, to, src=None):
            return pltpu.make_async_remote_copy(
                src_ref=rows(a, *block) if src is None else src, dst_ref=rows(a, *block),
                send_sem=send_sems.at[a, k], recv_sem=recv_sems.at[a, k],
                device_id=to, device_id_type=pl.DeviceIdType.MESH)

        mine = [pltpu.make_async_copy(ins[a], rows(a, *me), local_sems.at[a]) for a in range(n)]
        for cp in mine:
            cp.start()
        first = []
        for a in range(n):
            first.append(copy(a, 0, me, sibling, src=ins[a]))
            first += [copy(a, 1 + j, me, (*chip, c), src=ins[a]) for j, chip in enumerate(chips)]
        for cp in first:
            cp.start()
        passed = []
        for j, chip in enumerate(chips):
            for a in range(n):
                copy(a, 1 + j, (*chip, c), me).wait_recv()
                fwd = copy(a, 4 + j, (*chip, c), sibling)
                fwd.start()
                passed.append(fwd)
        for a in range(n):
            copy(a, 0, sibling, me).wait_recv()
            for j, chip in enumerate(chips):
                copy(a, 4 + j, (*chip, 1 - c), me).wait_recv()
        for cp in first + passed:
            cp.wait_send()
        for cp in mine:
            cp.wait()

    hbm = pl.BlockSpec(memory_space=pl.ANY)
    return pl.pallas_call(
        body, out_shape=[_SDS((N_DEV,) + s.shape, s.dtype) for s in shards],
        in_specs=[hbm] * n, out_specs=[hbm] * n,
        scratch_shapes=[pltpu.SemaphoreType.DMA((n, 7)), pltpu.SemaphoreType.DMA((n, 7)), pltpu.SemaphoreType.DMA((n,))],
        name=name)(*shards)


class _Exchange:
    def __init__(self, lands, srcs):
        self.lands, self.srcs = lands, srcs


def _seq_exchange(srcs, land_shapes, plan, name, cid):
    n, nl = len(srcs), len(land_shapes)

    def launch(*refs):
        src_refs, land_refs = refs[:n], refs[n:n + nl]
        send_sems, recv_sems, local_sems = refs[n + nl:]
        x, y, c = _place()
        my = 4 * x + 2 * y + c
        peers = [(x ^ ((k + 1) >> 2 & 1), y ^ ((k + 1) >> 1 & 1), c ^ ((k + 1) & 1)) for k in range(N_DEV - 1)]
        barrier = pltpu.get_barrier_semaphore()
        for p in peers:
            pl.semaphore_signal(barrier, inc=1, device_id=p, device_id_type=pl.DeviceIdType.MESH)
        pl.semaphore_wait(barrier, N_DEV - 1)

        def src_for(a, dest):
            return src_refs[a].at[dest] if plan[a][1] else src_refs[a]

        def slot(a, source):
            return land_refs[plan[a][0]].at[source]

        mine = [pltpu.make_async_copy(src_for(a, my), slot(a, my), local_sems.at[a]) for a in range(n)]
        for cp in mine:
            cp.start()
        sends, recvs = [], []
        for k, (px, py, pc) in enumerate(peers):
            peer = 4 * px + 2 * py + pc
            for a in range(n):
                kw = dict(send_sem=send_sems.at[a * (N_DEV - 1) + k], recv_sem=recv_sems.at[a * (N_DEV - 1) + k],
                          device_id=(px, py, pc), device_id_type=pl.DeviceIdType.MESH)
                sends.append(pltpu.make_async_remote_copy(src_ref=src_for(a, peer), dst_ref=slot(a, my), **kw))
                recvs.append(pltpu.make_async_remote_copy(src_ref=src_for(a, my), dst_ref=slot(a, peer), **kw))
        for cp in sends:
            cp.start()
        for cp in recvs:
            cp.wait_recv()
        for cp in sends:
            cp.wait_send()
        for cp in mine:
            cp.wait()

    lands = pl.kernel(
        launch, out_type=[_SDS(s, d) for s, d in land_shapes],
        mesh=plsc.ScalarSubcoreMesh(axis_name="sequencer", num_cores=1), name=name,
        scratch_types=(pltpu.SemaphoreType.DMA((n * (N_DEV - 1),)), pltpu.SemaphoreType.DMA((n * (N_DEV - 1),)),
                       pltpu.SemaphoreType.DMA((n,))),
        compiler_params=pltpu.CompilerParams(collective_id=cid))(*srcs)
    return _Exchange(list(lands), list(srcs))


def _adam_update(g, w, m, v):
    c1 = 1.0 - ADAM_B1 ** ADAM_STEP
    c2 = 1.0 - ADAM_B2 ** ADAM_STEP
    mm = ADAM_B1 * m + (1.0 - ADAM_B1) * g
    vv = ADAM_B2 * v + (1.0 - ADAM_B2) * (g * g)
    delta = -ADAM_LR * ((mm / c1) / (jnp.sqrt(vv / c2) + ADAM_EPS) + ADAM_WD * w)
    return delta, mm, vv


def _sum_sources(p_ref):
    g = p_ref[0].astype(F32)
    for s in range(1, N_DEV):
        g = g + p_ref[s].astype(F32)
    return g


def _adamw(parts, w, m, v, tr, name, restore_b=False, deps=()):
    nl, r, c = w.shape
    cp = parts[0].shape[-1]

    def body(*refs):
        p_refs = refs[:nl]
        w_ref, m_ref, v_ref = refs[nl:nl + 3]
        g_ref, d_ref, nm_ref, nv_ref = refs[-4:]
        g = _sum_sources(p_refs[0])
        for l in range(1, nl):
            g = jnp.where(pl.program_id(0) == l, _sum_sources(p_refs[l]), g)
        if restore_b:
            g = jnp.concatenate([g[:, :BP_XQ], g[:, BP_GATE:BP_GATE + 2 * B_V_HEADS], g[:, BP_XQ:BP_GATE]], axis=1)
        delta, mm, vv = _adam_update(g, w_ref[...], m_ref[...], v_ref[...])
        g_ref[...] = g
        d_ref[...] = delta
        nm_ref[...] = mm
        nv_ref[...] = vv

    spec = pl.BlockSpec((None, tr, c), lambda l, i: (l, i, 0))
    part_specs = [pl.BlockSpec((N_DEV, tr, cp), functools.partial(lambda l, i, k: (0, jnp.where(l == k, i, 0), 0), k=k))
                  for k in range(nl)]
    return pl.pallas_call(
        body, grid=(nl, r // tr),
        in_specs=part_specs + [spec, spec, spec] + _dep_specs(deps),
        out_specs=[spec] * 4, out_shape=[_SDS(w.shape, F32)] * 4,
        name=name, compiler_params=_cp("arbitrary", "arbitrary"))(*parts, w, m, v, *deps)


def _pack_small(d_rel, d_cb, d_cw, d_qkv, d_mix, d_mem, d_ffn, d_final, d_sinks, d_par, d_ng, loss_row, name):
    flat = [d_rel, *d_cb, *d_cw, d_qkv, *d_mix, *d_mem, *d_ffn, d_final, d_sinks, d_par, d_ng, loss_row]
    n = len(flat)

    def body(*refs):
        ins, o_ref = refs[:n], refs[n]
        rel, cb0, cb1, cw0, cw1, qkv, mx0, mx1, me0, me1, ff0, ff1, fin, snk, par, ng, lss = ins
        o_ref[...] = jnp.zeros_like(o_ref)
        for k in range(N_BUCKETS):
            lane = SP_REL_LANE + 128 * (k % 8)
            o_ref[SP_QKV + k // 8:SP_QKV + k // 8 + 1, lane:lane + 128] = rel[k:k + 1, :]
        for l, (cb, cw) in enumerate(((cb0, cw0), (cb1, cw1))):
            o_ref[SP_CB + l:SP_CB + l + 1, :] = jnp.concatenate([cb[j] for j in range(FF_BLOCKS)], axis=1)
            full = jnp.concatenate([cw[j] for j in range(FF_BLOCKS)], axis=1)
            o_ref[SP_CW + FFN_CONV * l:SP_CW + FFN_CONV * (l + 1), :] = full[:FFN_CONV]
        o_ref[SP_QKV:SP_QKV + B_CONV, 0:B_QKV] = qkv[0:B_CONV, :]
        for base, pair in ((SP_MIX, (mx0, mx1)), (SP_MEM, (me0, me1)), (SP_FFN, (ff0, ff1))):
            for l in range(2):
                o_ref[base + l:base + l + 1, 0:D] = pair[l][...]
        o_ref[SP_FINAL:SP_FINAL + 1, 0:D] = fin[...]
        o_ref[SP_MISC:SP_MISC + 1, 0:128] = snk[...]
        o_ref[SP_MISC:SP_MISC + 1, 128:256] = par[...]
        o_ref[SP_MISC:SP_MISC + 1, 256:384] = ng[...]
        o_ref[SP_MISC:SP_MISC + 1, 384:512] = lss[...]

    vm = pl.BlockSpec(memory_space=pltpu.VMEM)
    return pl.pallas_call(body, in_specs=[vm] * n, out_specs=vm, out_shape=_SDS((SMALL_ROWS, D_FF), F32), name=name)(*flat)


_SMALL = ["rel_bias", "norm_mix_g", "norm_mem_g", "sinks_a", "a_log_b", "dt_bias_b", "out_norm_g_b", "norm_ffn_g",
          "ffn_conv_b", "final_norm_g", "conv_qkv_b", "ffn_conv_w"]


def _adamw_small(recv, rc_qkv, rc_ffn, ws, ms, vs, name, deps=()):
    n = len(_SMALL)

    def body(*refs):
        recv_ref, qkv_ref, ffn_ref = refs[:3]
        w_refs, m_refs, v_refs = refs[3:3 + n], refs[3 + n:3 + 2 * n], refs[3 + 2 * n:3 + 3 * n]
        outs, loss_ref = refs[len(refs) - 4 * n - 1:len(refs) - 1], refs[-1]
        gs = _sum_sources(recv_ref)
        loss_ref[...] = gs[SP_MISC:SP_MISC + 1, 384:512]
        grads = {
            "rel_bias": jnp.concatenate(
                [gs[SP_QKV + k // 8:SP_QKV + k // 8 + 1, SP_REL_LANE + 128 * (k % 8):SP_REL_LANE + 128 * (k % 8) + A_HEADS]
                 for k in range(N_BUCKETS)], axis=0),
            "norm_mix_g": gs[SP_MIX:SP_MIX + 2, 0:D], "norm_mem_g": gs[SP_MEM:SP_MEM + 2, 0:D],
            "sinks_a": gs[SP_MISC:SP_MISC + 1, 0:A_HEADS],
            "a_log_b": gs[SP_MISC:SP_MISC + 1, 128:128 + B_V_HEADS],
            "dt_bias_b": gs[SP_MISC:SP_MISC + 1, 128 + B_V_HEADS:128 + 2 * B_V_HEADS],
            "out_norm_g_b": gs[SP_MISC:SP_MISC + 1, 256:256 + B_HD],
            "norm_ffn_g": gs[SP_FFN:SP_FFN + 2, 0:D], "ffn_conv_b": gs[SP_CB:SP_CB + 2, :],
            "final_norm_g": gs[SP_FINAL:SP_FINAL + 1, 0:D],
            "conv_qkv_b": _sum_sources(qkv_ref), "ffn_conv_w": _sum_sources(ffn_ref),
        }
        for i, nm in enumerate(_SMALL):
            g = grads[nm]
            delta, mm, vv = _adam_update(g, w_refs[i][...], m_refs[i][...], v_refs[i][...])
            outs[i][...] = g
            outs[n + i][...] = delta
            outs[2 * n + i][...] = mm
            outs[3 * n + i][...] = vv

    vm = pl.BlockSpec(memory_space=pltpu.VMEM)
    shapes = [_SDS(w.shape, F32) for w in ws]
    return pl.pallas_call(
        body, in_specs=[vm] * (3 + 3 * n) + _dep_specs(deps), out_specs=[vm] * (4 * n + 1),
        out_shape=shapes * 4 + [_SDS((1, 128), F32)],
        name=name)(recv, rc_qkv, rc_ffn, *ws, *ms, *vs, *deps)


def _assemble(gathered, axis):
    g = jnp.moveaxis(gathered, 0, axis)
    shp = list(g.shape)
    return g.reshape(shp[:axis] + [shp[axis] * shp[axis + 1]] + shp[axis + 2:])


def _pad_rows(a, rows):
    return jnp.pad(a, ((0, rows - a.shape[0]), (0, 0)))


def _pad_lanes(a, lanes=128):
    return jnp.pad(a, ((0, 0), (0, lanes - a.shape[1])))


def _ff_blocks(a):
    return jnp.moveaxis(a.reshape(a.shape[0], FF_BLOCKS, GU_SHARD), 1, 0)


def _reorder_b(w):
    qkv_z = w[..., :B_QKV + B_V]
    gates = w[..., B_QKV + B_V:B_QKV + B_V + 2 * B_V_HEADS]
    xq = w[..., IN_B - X_Q:]
    pad = jnp.zeros(w.shape[:-1] + (IN_BP - IN_B,), w.dtype)
    return jnp.concatenate([qkv_z, xq, gates, pad], axis=-1)


def kernel(x, mem, rel_bias, norm_mix_g, norm_mem_g, w_mem_kv, w_out, w_in_a, sinks_a, w_in_b, conv_qkv_b, a_log_b, dt_bias_b, out_norm_g_b, norm_ffn_g, w_gate_up, ffn_conv_w, ffn_conv_b, w_down, final_norm_g, loss_target, m_rel_bias, m_norm_mix_g, m_norm_mem_g, m_w_mem_kv, m_w_out, m_w_in_a, m_sinks_a, m_w_in_b, m_conv_qkv_b, m_a_log_b, m_dt_bias_b, m_out_norm_g_b, m_norm_ffn_g, m_w_gate_up, m_ffn_conv_w, m_ffn_conv_b, m_w_down, m_final_norm_g, v_rel_bias, v_norm_mix_g, v_norm_mem_g, v_w_mem_kv, v_w_out, v_w_in_a, v_sinks_a, v_w_in_b, v_conv_qkv_b, v_a_log_b, v_dt_bias_b, v_out_norm_g_b, v_norm_ffn_g, v_w_gate_up, v_ffn_conv_w, v_ffn_conv_b, v_w_down, v_final_norm_g):
    local = dict(locals())
    order = ["rel_bias", "norm_mix_g", "norm_mem_g", "w_mem_kv", "w_out", "w_in_a", "sinks_a", "w_in_b", "conv_qkv_b",
             "a_log_b", "dt_bias_b", "out_norm_g_b", "norm_ffn_g", "w_gate_up", "ffn_conv_w", "ffn_conv_b", "w_down",
             "final_norm_g"]
    wts = {n: local[n] for n in order}
    moms = {n: local["m_" + n] for n in order}
    vars_ = {n: local["v_" + n] for n in order}
    h0 = x[0]
    memx = mem[0]
    tgt = loss_target[0]
    s = h0.shape[0]
    tm = _rows(s)
    tb = min(s, _TM_BIG)

    t_ = lambda a: jnp.swapaxes(a, 1, 2)
    g_mk0, g_out0, g_ia, g_cq, g_cw = _all_gather(
        [w_mem_kv[0:1].astype(_MXU), w_out[0:1].astype(_MXU), t_(w_in_a).astype(_MXU), conv_qkv_b, ffn_conv_w], "gather_first")
    g_mk, g_out = [g_mk0], [g_out0]
    gu_land = ((N_DEV, GU_SHARD, D), _MXU)
    dn_land = ((N_DEV, DN_SHARD, D), _MXU)
    whole = [(0, False), (1, False)]
    def after(a, b):
        return a + (b[(0,) * b.ndim] * 0).astype(a.dtype)

    gu0_w = _seq_exchange([after(t_(w_gate_up)[0].astype(_MXU), g_ia)], [gu_land], [(0, False)], "gather_gate_up0", 1)
    dn0_w = _seq_exchange([after(w_down[0].astype(_MXU), g_ia)], [dn_land], [(0, False)], "gather_down0", 8)
    w_ia = g_ia.reshape(IN_A, D)
    conv_qkv = _pad_rows(_assemble(g_cq, 2)[0], HALO)
    ffn_cw_full = _assemble(g_cw, 2)
    ffn_cw = [_ff_blocks(_pad_rows(ffn_cw_full[i], HALO)) for i in range(2)]
    ffn_cb = [_ff_blocks(ffn_conv_b[i:i + 1]) for i in range(2)]
    bucket = jnp.asarray(_bucket_table())
    bias = _bias_build(rel_bias, bucket, "bias_build")
    sinks = _pad_lanes(sinks_a)
    par_b = _pad_lanes(jnp.concatenate([a_log_b, dt_bias_b], axis=1))

    row_x = pl.BlockSpec((tm, D), lambda i, j: (i, 0))
    gu_shape = (2, FF_BLOCKS, s, GU_SHARD)

    def in_proj(h, g, w, w_spec, n_cols, tn, name, deps=(), out_dtype=F32, w_t=False, tm=None):
        return _norm_matmul(h, g, w, w_spec, n_cols // tn, (h.shape[0], n_cols),
                            pl.BlockSpec((tm or _rows(h.shape[0]), tn), lambda i, j: (i, j)), name, deps=deps, out_dtype=out_dtype,
                            w_t=w_t, tm=tm)

    def ffn_fwd(i, h, g_gu, g_dn, deps=()):
        gu, hn = _norm_matmul(h, norm_ffn_g[i:i + 1], g_gu, _spec_gate_up(1), N_DEV, gu_shape,
                              _spec_gu_act(0, 1, tb), f"gate_up_{i}", deps=deps, out_dtype=_ACT, w_t=True, tm=tb)
        h_new, act, gc = _glu_down(gu, ffn_cw[i], ffn_cb[i], g_dn, h, f"glu_down_{i}")
        return h_new, gu, hn, (act, gc)

    def out_proj(i, mix, h):
        return _matmul_res(mix, row_x, g_out[i], _spec_rowsharded(0, D // N_DEV, D), 1, h, f"out_proj_{i}")

    proj_a, hn_a = in_proj(h0, norm_mix_g[0:1], w_ia, pl.BlockSpec((640, D), lambda i, j: (j, 0)), IN_A, 640, "in_proj_a",
                           deps=gu0_w.srcs + dn0_w.srcs, out_dtype=_ACT, w_t=True)
    memkv0, memn0 = in_proj(memx, norm_mem_g[0:1], g_mk[0], _spec_rowsharded(0, D // N_DEV, 2 * X_Q), 2 * X_Q, 2 * X_Q, "mem_proj_0")
    mix_a = _mix_a_fwd(proj_a, bias, sinks, memkv0, "mix_a_fwd")
    h1 = out_proj(0, mix_a, h0)
    g_gu0, g_dn0 = gu0_w.lands[0], dn0_w.lands[0]
    in_b_w = _seq_exchange([after(_reorder_b(w_in_b).astype(_MXU), h1), after(w_mem_kv[1:2].astype(_MXU), h1)],
                           [((N_DEV, 1, D // N_DEV, IN_BP), _MXU), ((N_DEV, 1, D // N_DEV, 2 * X_Q), _MXU)], whole, "gather_in_b", 2)
    ffn1_w = _seq_exchange([after(t_(w_gate_up)[1].astype(_MXU), h1), after(w_down[1].astype(_MXU), h1),
                            after(w_out[1:2].astype(_MXU), h1)], [gu_land, dn_land, ((N_DEV, 1, D // N_DEV, D), _MXU)],
                           whole + [(2, False)], "gather_ffn1", 3)
    h2, gu0, hn_f0, act0 = ffn_fwd(0, h1, g_gu0, g_dn0, deps=in_b_w.srcs + ffn1_w.srcs)
    g_ib, g_mk1 = in_b_w.lands
    g_gu1, g_dn1, g_out1 = ffn1_w.lands
    g_mk.append(g_mk1)
    g_out.append(g_out1)
    proj_b, hn_b = in_proj(h2, norm_mix_g[1:2], g_ib, _spec_rowsharded(0, D // N_DEV, 896, col_block=1), IN_BP, 896, "in_proj_b")
    memkv1, memn1 = in_proj(memx, norm_mem_g[1:2], g_mk[1], _spec_rowsharded(0, D // N_DEV, 2 * X_Q), 2 * X_Q, 2 * X_Q, "mem_proj_1",
                            deps=[h2])
    mix_b, states, deltas = _mix_b_fwd(proj_b, conv_qkv, par_b, out_norm_g_b, memkv1, "mix_b_fwd")
    h3 = out_proj(1, mix_b, h2)
    h4, gu1, hn_f1, act1 = ffn_fwd(1, h3, g_gu1, g_dn1)
    loss_row, *dh, d_final_g = _loss_head(h4, final_norm_g[None, :], tgt, "loss_head")

    zeros_mem = jnp.zeros_like(memx)
    per_dest2 = [(0, True), (1, True)]

    def ffn_bwd(i, dh, h_in, gu, hn_f, act_gc, g_gu, g_dn, deps=()):
        act, gc = act_gc
        dgu, d_cw, d_cb = _glu_bwd(gu, gc, ffn_cw[i], dh[1], g_dn, f"glu_bwd_{i}", deps=deps)
        d_wdown = _matmul_tn(act, pl.BlockSpec((None, tb, GU_SHARD), lambda j, r: (j, r, 0)),
                             dh[1], pl.BlockSpec((tb, D), lambda j, r: (r, 0)), s, FF_BLOCKS, (GU_SHARD, D),
                             (N_DEV, DN_SHARD, D), pl.BlockSpec((2, DN_SHARD, D), lambda j, r: (j, 0, 0)), f"d_w_down_{i}",
                             tm=tb)
        *dh_new, d_g = _matmul_nt_normbwd(dgu, _spec_gu_act(0, 1, tm), g_gu, _spec_gate_up(1), N_DEV, h_in,
                                          norm_ffn_g[i:i + 1], dh[0], f"d_ffn_in_{i}", w_t=True, act_copy=True)
        d_wgu = _matmul_tn(dgu, _spec_gu_act(1, 0, tb), hn_f, pl.BlockSpec((tb, D), lambda j, r: (r, 0)), s, N_DEV,
                           (GU_SHARD, D), (N_DEV, GU_SHARD, D), pl.BlockSpec((None, GU_SHARD, D), lambda j, r: (j, 0, 0)),
                           f"d_w_gate_up_{i}", tm=tb)
        return dh_new, [d_wdown, d_wgu], d_cw, d_cb, d_g

    def out_bwd(i, dh, mix, deps):
        dmix = _matmul_nt(dh[1], g_out[i], _spec_rowsharded(0, D // N_DEV, D), 1, (s, D), row_x, f"d_mix_{i}", deps=deps, out_dtype=_ACT)
        d_wout = _matmul_tn(mix, pl.BlockSpec((tb, D), lambda j, r: (r, 0)), dh[1], pl.BlockSpec((tb, D), lambda j, r: (r, 0)),
                            s, 1, (D, D), (N_DEV, D // N_DEV, D), pl.BlockSpec((N_DEV, D // N_DEV, D), lambda j, r: (0, 0, 0)),
                            f"d_w_out_{i}", tm=tb)
        return dmix, d_wout

    def mem_bwd(i, dmemkv, memn):
        tmm = _rows(MEM_LEN)
        *_, d_g = _matmul_nt_normbwd(dmemkv, pl.BlockSpec((tmm, 2 * X_Q), lambda r, j: (r, 0)), g_mk[i],
                                     _spec_rowsharded(0, D // N_DEV, 2 * X_Q), 1, memx, norm_mem_g[i:i + 1], zeros_mem,
                                     f"d_mem_in_{i}")
        by_row = lambda j, r: (r, 0)
        d_w = _matmul_tn(memn, pl.BlockSpec((tmm, D), by_row), dmemkv, pl.BlockSpec((tmm, 2 * X_Q), by_row), MEM_LEN, 1,
                         (D, 2 * X_Q), (N_DEV, D // N_DEV, 2 * X_Q),
                         pl.BlockSpec((N_DEV, D // N_DEV, 2 * X_Q), lambda j, r: (0, 0, 0)), f"d_w_mem_kv_{i}")
        return d_w, d_g

    out_land = ((N_DEV, D // N_DEV, D), _WIRE)
    mk_land = ((N_DEV, D // N_DEV, 2 * X_Q), _WIRE)
    ffn_lands = [((N_DEV, DN_SHARD, D), _WIRE), ((N_DEV, GU_SHARD, D), _WIRE)]
    dh, d_ffn1, d_cw1, d_cb1, d_gf1 = ffn_bwd(1, dh, h3, gu1, hn_f1, act1, g_gu1, g_dn1)
    ffn1_g = _seq_exchange(d_ffn1, ffn_lands, per_dest2, "send_ffn1_grads", 5)
    dmix, d_wout1 = out_bwd(1, dh, mix_b, ffn1_g.srcs)
    dproj_b, d_convw, d_par, d_ng, dmemkv1 = _mix_b_bwd(proj_b, conv_qkv, par_b, out_norm_g_b, memkv1, states, deltas, dmix, "mix_b_bwd")
    *dh, d_gm1 = _matmul_nt_normbwd(dproj_b, pl.BlockSpec((tm, 896), lambda i, j: (i, j)), g_ib,
                                    _spec_rowsharded(0, D // N_DEV, 896, col_block=1), IN_BP // 896, h2, norm_mix_g[1:2], dh[0],
                                    "d_in_b", act_copy=True)
    d_wib = _matmul_tn(hn_b, pl.BlockSpec((tb, D), lambda j, r: (r, 0)), dproj_b, pl.BlockSpec((tb, 896), lambda j, r: (r, j)),
                       s, IN_BP // 896, (D, 896), (N_DEV, D // N_DEV, IN_BP),
                       pl.BlockSpec((N_DEV, D // N_DEV, 896), lambda j, r: (0, 0, j)), "d_w_in_b", tm=tb)
    d_wmk1, d_gmem1 = mem_bwd(1, dmemkv1, memn1)
    mix1_g = _seq_exchange([d_wout1, d_wib, d_wmk1], [out_land, ((N_DEV, D // N_DEV, IN_BP), _WIRE), mk_land],
                           [(0, True), (1, True), (2, True)], "send_mix1_grads", 6)
    dh, d_ffn0, d_cw0, d_cb0, d_gf0 = ffn_bwd(0, dh, h1, gu0, hn_f0, act0, g_gu0, g_dn0, deps=mix1_g.srcs)
    dmix, d_wout0 = out_bwd(0, dh, mix_a, d_ffn0 + ffn1_g.lands[:1])
    ffn0_g = _seq_exchange(d_ffn0 + [d_wout0], ffn_lands + [out_land], per_dest2 + [(2, True)], "send_ffn0_grads", 4)
    dproj_a, dbias, dsinks, dmemkv0 = _mix_a_bwd(proj_a, bias, sinks, memkv0, dmix, "mix_a_bwd", deps=ffn0_g.srcs)
    dx, _, d_gm0 = _matmul_nt_normbwd(dproj_a, pl.BlockSpec((tm, 640), lambda i, j: (i, j)), w_ia,
                                      pl.BlockSpec((640, D), lambda i, j: (j, 0)), IN_A // 640, h0, norm_mix_g[0:1], dh[0],
                                      "d_in_a", w_t=True)
    d_wia = _matmul_tn(dproj_a, pl.BlockSpec((tb, IN_A), lambda j, r: (r, 0)), hn_a, pl.BlockSpec((tb, D), lambda j, r: (r, 0)),
                       s, 1, (IN_A, D), (N_DEV, IA_SHARD, D), pl.BlockSpec((N_DEV, IA_SHARD, D), lambda j, r: (0, 0, 0)),
                       "d_w_in_a", tm=tb)
    d_wmk0, d_gmem0 = mem_bwd(0, dmemkv0, memn0)
    d_rel = _bias_reduce(dbias, bucket, "bias_reduce")
    small = _pack_small(d_rel, (d_cb0, d_cb1), (d_cw0, d_cw1), d_convw, (d_gm0, d_gm1), (d_gmem0, d_gmem1),
                        (d_gf0, d_gf1), d_final_g, dsinks, d_par, d_ng, loss_row, "pack_small")
    mix0_g = _seq_exchange([d_wia, d_wmk0, small],
                           [((N_DEV, IA_SHARD, D), _WIRE), mk_land, ((N_DEV, SMALL_ROWS, D_FF), F32)],
                           [(0, True), (1, True), (2, False)], "send_mix0_grads", 7)

    res = {}
    last = []

    def update(nm, parts, tr, restore=False, transposed=False):
        view = t_ if transposed else (lambda a: a)
        out = _adamw(parts, view(wts[nm]), view(moms[nm]), view(vars_[nm]), tr, "adamw_" + nm, restore_b=restore, deps=last[-1:])
        res[nm] = [view(o) for o in out]
        last.append(out[1])

    r_dn1, r_gu1 = ffn1_g.lands
    r_dn0, r_gu0, r_out0 = ffn0_g.lands
    r_out1, r_ib, r_mk1 = mix1_g.lands
    update("w_in_b", [r_ib], 32, True)
    update("w_gate_up", [r_gu0, r_gu1], 176, transposed=True)
    update("w_down", [r_dn0, r_dn1], 176)
    r_ia, r_mk0, r_small = mix0_g.lands
    update("w_mem_kv", [r_mk0, r_mk1], 128)
    update("w_out", [r_out0, r_out1], 128)
    update("w_in_a", [r_ia], IA_SHARD, transposed=True)

    my = 4 * lax.axis_index("x") + 2 * lax.axis_index("y") + lax.axis_index("c")
    cq = conv_qkv_b.shape[-1]
    cf = ffn_conv_w.shape[-1]
    rc_qkv = lax.dynamic_slice_in_dim(r_small[:, SP_QKV:SP_QKV + B_CONV, :B_QKV], my * cq, cq, axis=2)[:, None]
    rc_ffn = lax.dynamic_slice_in_dim(r_small[:, SP_CW:SP_CW + 2 * FFN_CONV, :], my * cf, cf, axis=2).reshape(N_DEV, 2, FFN_CONV, cf)
    as2d = lambda a: a[None, :] if a.ndim == 1 else a
    small_out = _adamw_small(r_small, rc_qkv, rc_ffn, [as2d(wts[n]) for n in _SMALL], [as2d(moms[n]) for n in _SMALL],
                             [as2d(vars_[n]) for n in _SMALL], "adamw_small", deps=last[-1:])
    ns = len(_SMALL)
    for i, nm in enumerate(_SMALL):
        res[nm] = [small_out[k * ns + i].reshape(wts[nm].shape) for k in range(4)]

    return (small_out[-1][0, 0], dx[None], *[res[n][0] for n in order], *[res[n][1] for n in order],
            *[res[n][2] for n in order], *[res[n][3] for n in order])
```

```python
import functools
import math

import numpy as np

import jax
import jax.numpy as jnp
from jax import lax
from jax.experimental import pallas as pl
from jax.experimental.pallas import tpu as pltpu
from jax.experimental.pallas import tpu_sc as plsc

F32 = jnp.float32
_MXU = jnp.bfloat16
_ACT = jnp.bfloat16
_WIRE = jnp.bfloat16
_HI = lax.Precision.HIGH
_TM = 1024
_TM_GLU = 1024
_TM_BIG = 2048
_VMEM_LIMIT = 48 * 1024 * 1024
_SDS = jax.ShapeDtypeStruct

D = 1024
EPS = 1e-6
A_HEADS, A_KV_HEADS, A_HD, BLK = 12, 2, 64, 128
N_BUCKETS, MAX_DISTANCE = 32, 128
B_QK_HEADS, B_V_HEADS, B_HD, B_CONV, CHUNK = 3, 6, 128, 4, 64
X_HEADS, X_HD, MEM_LEN = 4, 64, 256
D_FF, FFN_CONV = 2816, 3
A_Q, A_KV, X_Q = 768, 128, 256
B_QK, B_V, B_QKV = 384, 768, 1536
IN_A, IN_B = 1280, 2572
IN_BP = 2688
BP_Z, BP_XQ, BP_GATE = 1536, 2304, 2560
HALO = 8
GLU_HALO = 16

N_DEV = 8
GU_SHARD = 2 * D_FF // N_DEV
FF_BLOCKS = D_FF // GU_SHARD
DN_SHARD = D_FF // N_DEV
IA_SHARD = IN_A // N_DEV

ADAM_LR, ADAM_B1, ADAM_B2, ADAM_EPS, ADAM_WD, ADAM_STEP = 0.001, 0.9, 0.999, 1e-08, 0.01, 10

SP_CB, SP_CW, SP_QKV, SP_MIX, SP_MEM, SP_FFN, SP_FINAL, SP_MISC, SMALL_ROWS = 0, 2, 8, 12, 14, 16, 18, 19, 24
SP_REL_LANE = B_QKV


def _cp(*sems):
    return pltpu.CompilerParams(dimension_semantics=sems, vmem_limit_bytes=_VMEM_LIMIT)


def _mm(a, b):
    return jnp.dot(a.astype(_MXU), b.astype(_MXU), preferred_element_type=F32)


def _mm_nt(a, b):
    return lax.dot_general(a.astype(_MXU), b.astype(_MXU), (((1,), (1,)), ((), ())), preferred_element_type=F32)


def _mm_tn(a, b):
    return lax.dot_general(a.astype(_MXU), b.astype(_MXU), (((0,), (0,)), ((), ())), preferred_element_type=F32)


def _mmf(a, b):
    return jnp.dot(a, b, preferred_element_type=F32, precision=_HI)


def _mmf_nt(a, b):
    return lax.dot_general(a, b, (((1,), (1,)), ((), ())), preferred_element_type=F32, precision=_HI)


def _silu(x):
    return x * jax.nn.sigmoid(x)


def _w2d(ref):
    v = ref[...]
    return v.reshape(-1, v.shape[-1])


def _rows(m):
    return min(m, _TM)


def _spec_rowsharded(layer, rows, cols, col_block=None):
    if col_block is None:
        return pl.BlockSpec((N_DEV, None, rows, cols), lambda *_: (0, layer, 0, 0))
    return pl.BlockSpec((N_DEV, None, rows, cols), lambda *ids: (0, layer, 0, ids[col_block]))


def _spec_gate_up(axis):
    return pl.BlockSpec((None, GU_SHARD, D), lambda *ids: (ids[axis], 0, 0))


def _spec_down(axis):
    return pl.BlockSpec((2, DN_SHARD, D), lambda *ids: (ids[axis], 0, 0))


def _dep_specs(deps):
    return [pl.BlockSpec(memory_space=pl.ANY) for d in deps]


def _spec_gu_act(row_axis, axis, tm):
    return pl.BlockSpec((None, None, tm, GU_SHARD), lambda *ids: (ids[axis] // FF_BLOCKS, ids[axis] % FF_BLOCKS, ids[row_axis], 0))


def _norm_matmul(x, g, w, w_spec, n_blocks, out_shape, out_spec, name, deps=(), out_dtype=F32, w_t=False, tm=None):
    m, k = x.shape
    tm = tm or _rows(m)

    def body(x_ref, g_ref, w_ref, *rest):
        y_ref, hn_ref = rest[-2:]

        @pl.when(pl.program_id(1) == 0)
        def _():
            xv = x_ref[...]
            r = lax.rsqrt(jnp.mean(xv * xv, axis=-1, keepdims=True) + EPS)
            hn_ref[...] = (xv * r * g_ref[...]).astype(hn_ref.dtype)

        y_ref[...] = (_mm_nt if w_t else _mm)(hn_ref[...], _w2d(w_ref)).astype(y_ref.dtype)

    return pl.pallas_call(
        body, grid=(m // tm, n_blocks),
        in_specs=[pl.BlockSpec((tm, k), lambda i, j: (i, 0)), pl.BlockSpec((1, k), lambda i, j: (0, 0)), w_spec]
        + _dep_specs(deps),
        out_specs=[out_spec, pl.BlockSpec((tm, k), lambda i, j: (i, 0))],
        out_shape=[_SDS(out_shape, out_dtype), _SDS((m, k), _ACT)],
        name=name, compiler_params=_cp("arbitrary", "arbitrary"))(x, g, w, *deps)


def _matmul_res(a, a_spec, w, w_spec, n_k, res, name):
    m, n = res.shape
    tm = _rows(m)

    def body(a_ref, w_ref, r_ref, o_ref):
        part = _mm(a_ref[...], _w2d(w_ref))

        @pl.when(pl.program_id(1) == 0)
        def _():
            o_ref[...] = r_ref[...] + part

        @pl.when(pl.program_id(1) > 0)
        def _():
            o_ref[...] += part

    return pl.pallas_call(
        body, grid=(m // tm, n_k),
        in_specs=[a_spec, w_spec, pl.BlockSpec((tm, n), lambda i, j: (i, 0))],
        out_specs=pl.BlockSpec((tm, n), lambda i, j: (i, 0)),
        out_shape=_SDS((m, n), F32), name=name, compiler_params=_cp("arbitrary", "arbitrary"))(a, w, res)


def _matmul_nt(dy, w, w_spec, n_blocks, out_shape, out_spec, name, deps=(), out_dtype=F32):
    m, n = dy.shape
    tm = _rows(m)

    def body(dy_ref, w_ref, *rest):
        o_ref = rest[-1]
        o_ref[...] = _mm_nt(dy_ref[...], _w2d(w_ref)).astype(o_ref.dtype)

    return pl.pallas_call(
        body, grid=(m // tm, n_blocks),
        in_specs=[pl.BlockSpec((tm, n), lambda i, j: (i, 0)), w_spec] + _dep_specs(deps),
        out_specs=out_spec, out_shape=_SDS(out_shape, out_dtype),
        name=name, compiler_params=_cp("arbitrary", "arbitrary"))(dy, w, *deps)


def _matmul_nt_normbwd(dy, dy_spec, w, w_spec, nj, h, g, dh_in, name, w_t=False, act_copy=False):
    m, k = h.shape
    tm = _rows(m)

    def body(dy_ref, w_ref, h_ref, g_ref, dhin_ref, dh_ref, *rest):
        dg_ref, acc_ref = rest[-2:]
        i, j = pl.program_id(0), pl.program_id(1)

        @pl.when(j == 0)
        def _():
            acc_ref[...] = jnp.zeros_like(acc_ref)

        acc_ref[...] += (_mm if w_t else _mm_nt)(dy_ref[...], _w2d(w_ref))

        @pl.when(j == nj - 1)
        def _():
            xv = h_ref[...]
            r = lax.rsqrt(jnp.mean(xv * xv, axis=-1, keepdims=True) + EPS)
            xh = xv * r
            dhn = acc_ref[...]
            part = jnp.sum(dhn * xh, axis=0, keepdims=True)

            @pl.when(i == 0)
            def _():
                dg_ref[...] = part

            @pl.when(i > 0)
            def _():
                dg_ref[...] += part

            t = dhn * g_ref[...]
            dh = dhin_ref[...] + r * (t - xh * jnp.mean(t * xh, axis=-1, keepdims=True))
            dh_ref[...] = dh
            if act_copy:
                rest[0][...] = dh.astype(_ACT)

    rows = pl.BlockSpec((tm, k), lambda i, j: (i, 0))
    outs = pl.pallas_call(
        body, grid=(m // tm, nj),
        in_specs=[dy_spec, w_spec, rows, pl.BlockSpec((1, k), lambda i, j: (0, 0)), rows],
        out_specs=[rows] + [rows] * act_copy + [pl.BlockSpec((1, k), lambda i, j: (0, 0))],
        out_shape=[_SDS((m, k), F32)] + [_SDS((m, k), _ACT)] * act_copy + [_SDS((1, k), F32)],
        scratch_shapes=[pltpu.VMEM((tm, k), F32)],
        name=name, compiler_params=_cp("arbitrary", "arbitrary"))(dy, w, h, g, dh_in)
    return outs[0], (outs[1] if act_copy else None), outs[-1]


def _matmul_tn(x, x_spec, dy, dy_spec, m, n_blocks, acc_shape, out_shape, out_spec, name, tm=None):
    tm = tm or _rows(m)
    nm = m // tm

    def body(x_ref, dy_ref, o_ref, acc_ref):
        @pl.when(pl.program_id(1) == 0)
        def _():
            acc_ref[...] = jnp.zeros_like(acc_ref)

        acc_ref[...] += _mm_tn(x_ref[...], dy_ref[...])

        @pl.when(pl.program_id(1) == nm - 1)
        def _():
            o_ref[...] = acc_ref[...].reshape(o_ref.shape).astype(o_ref.dtype)

    return pl.pallas_call(
        body, grid=(n_blocks, nm), in_specs=[x_spec, dy_spec], out_specs=out_spec,
        out_shape=_SDS(out_shape, _WIRE), scratch_shapes=[pltpu.VMEM(acc_shape, F32)],
        name=name, compiler_params=_cp("arbitrary", "arbitrary"))(x, dy)


def _loss_head(h, g, tgt, name):
    m, k = h.shape
    tm = _rows(m)

    def body(h_ref, g_ref, t_ref, loss_ref, dh_ref, dha_ref, dg_ref):
        i = pl.program_id(0)
        xv = h_ref[...]
        r = lax.rsqrt(jnp.mean(xv * xv, axis=-1, keepdims=True) + EPS)
        xh = xv * r
        gv = g_ref[...]
        err = xh * gv - t_ref[...]
        lpart = jnp.zeros((1, 128), F32) + 0.5 * jnp.sum(jnp.mean(err * err, axis=-1, keepdims=True), axis=0, keepdims=True)
        dy = err * (1.0 / k)
        gpart = jnp.sum(dy * xh, axis=0, keepdims=True)

        @pl.when(i == 0)
        def _():
            loss_ref[...] = lpart
            dg_ref[...] = gpart

        @pl.when(i > 0)
        def _():
            loss_ref[...] += lpart
            dg_ref[...] += gpart

        t = dy * gv
        dh = r * (t - xh * jnp.mean(t * xh, axis=-1, keepdims=True))
        dh_ref[...] = dh
        dha_ref[...] = dh.astype(_ACT)

    rows = pl.BlockSpec((tm, k), lambda i: (i, 0))
    return pl.pallas_call(
        body, grid=(m // tm,),
        in_specs=[rows, pl.BlockSpec((1, k), lambda i: (0, 0)), rows],
        out_specs=[pl.BlockSpec((1, 128), lambda i: (0, 0)), rows, rows, pl.BlockSpec((1, k), lambda i: (0, 0))],
        out_shape=[_SDS((1, 128), F32), _SDS((m, k), F32), _SDS((m, k), _ACT), _SDS((1, k), F32)],
        name=name, compiler_params=_cp("arbitrary"))(h, g, tgt)


def _glu_down(gu, conv_w, conv_b, w_down, res, name):
    s = gu.shape[2]
    tm = min(s, _TM_GLU)

    def body(gu_ref, prev_ref, w_ref, b_ref, wdn_ref, r_ref, o_ref, act_ref, gc_ref):
        i, j = pl.program_id(0), pl.program_id(1)
        prev = jnp.where(i > 0, prev_ref[...].astype(F32), 0.0)
        ext = jnp.concatenate([prev, gu_ref[0].astype(F32)], axis=0)
        gc = b_ref[...] + w_ref[FFN_CONV - 1:FFN_CONV, :] * ext
        for k in range(FFN_CONV - 1):
            gc = gc + w_ref[k:k + 1, :] * pltpu.roll(ext, FFN_CONV - 1 - k, 0)
        gc = gc[GLU_HALO:]
        gc_ref[...] = gc.astype(gc_ref.dtype)
        act =(_silu(gc) * gu_ref[1].astype(F32)).astype(act_ref.dtype)
        act_ref[...] = act
        part = _mm(act, _w2d(wdn_ref))

        @pl.when(j == 0)
        def _():
            o_ref[...] = r_ref[...] + part

        @pl.when(j > 0)
        def _():
            o_ref[...] += part

    return pl.pallas_call(
        body, grid=(s // tm, FF_BLOCKS),
        in_specs=[pl.BlockSpec((2, None, tm, GU_SHARD), lambda i, j: (0, j, i, 0)),
                  pl.BlockSpec((None, None, GLU_HALO, GU_SHARD),
                               lambda i, j: (0, j, jnp.maximum(i * (tm // GLU_HALO) - 1, 0), 0)),
                  pl.BlockSpec((None, HALO, GU_SHARD), lambda i, j: (j, 0, 0)),
                  pl.BlockSpec((None, 1, GU_SHARD), lambda i, j: (j, 0, 0)),
                  _spec_down(1), pl.BlockSpec((tm, D), lambda i, j: (i, 0))],
        out_specs=[pl.BlockSpec((tm, D), lambda i, j: (i, 0)), pl.BlockSpec((None, tm, GU_SHARD), lambda i, j: (j, i, 0)),
                   pl.BlockSpec((None, tm, GU_SHARD), lambda i, j: (j, i, 0))],
        out_shape=[_SDS((s, D), F32), _SDS((FF_BLOCKS, s, GU_SHARD), _ACT), _SDS((FF_BLOCKS, s, GU_SHARD), _ACT)], name=name,
        compiler_params=_cp("arbitrary", "arbitrary"))(gu, gu, conv_w, conv_b, w_down, res)


def _glu_bwd(gu, gc, conv_w, dh, w_down, name, deps=()):
    s = gu.shape[2]
    tm = min(s, _TM_GLU)
    nt = s // tm
    ext_rows = tm + GLU_HALO

    def body(gu_ref, prev_ref, gc_ref, w_ref, dh_ref, wdn_ref, *rest):
        dgu_ref, dw_ref, db_ref, carry_ref = rest[-4:]
        t = pl.program_id(1)
        i = nt - 1 - t

        @pl.when(t == 0)
        def _():
            carry_ref[...] = jnp.zeros_like(carry_ref)
            dw_ref[...] = jnp.zeros_like(dw_ref)
            db_ref[...] = jnp.zeros_like(db_ref)

        up = gu_ref[1].astype(F32)
        prev = jnp.where(i > 0, prev_ref[...].astype(F32), 0.0)
        ext = jnp.concatenate([prev, gu_ref[0].astype(F32)], axis=0)
        gc = gc_ref[...].astype(F32)
        sg = jax.nn.sigmoid(gc)
        da = _mm_nt(dh_ref[...], _w2d(wdn_ref))
        dup = da * (gc * sg)
        dgc = da * up * (sg * (1.0 + gc * (1.0 - sg)))
        db_ref[...] += jnp.sum(dgc, axis=0, keepdims=True)
        dgc_ext = jnp.concatenate([jnp.zeros((GLU_HALO, GU_SHARD), F32), dgc], axis=0)
        ahead = [pltpu.roll(dgc_ext, ext_rows - (FFN_CONV - 1 - j), 0) if j < FFN_CONV - 1 else dgc_ext
                 for j in range(FFN_CONV)]
        dext = ahead[0] * w_ref[0:1, :]
        for j in range(FFN_CONV):
            dw_ref[j:j + 1, :] += jnp.sum(ext * ahead[j], axis=0, keepdims=True)
            if j > 0:
                dext = dext + ahead[j] * w_ref[j:j + 1, :]
        tail = jnp.concatenate([jnp.zeros((tm - GLU_HALO, GU_SHARD), F32), carry_ref[...]], axis=0)
        dgate = dext[GLU_HALO:] + tail
        carry_ref[...] = dext[:GLU_HALO]
        dgu_ref[0] = dgate.astype(dgu_ref.dtype)
        dgu_ref[1] = dup.astype(dgu_ref.dtype)

    return pl.pallas_call(
        body, grid=(FF_BLOCKS, nt),
        in_specs=[pl.BlockSpec((2, None, tm, GU_SHARD), lambda j, t: (0, j, nt - 1 - t, 0)),
                  pl.BlockSpec((None, None, GLU_HALO, GU_SHARD),
                               lambda j, t: (0, j, jnp.maximum((nt - 1 - t) * (tm // GLU_HALO) - 1, 0), 0)),
                  pl.BlockSpec((None, tm, GU_SHARD), lambda j, t: (j, nt - 1 - t, 0)),
                  pl.BlockSpec((None, HALO, GU_SHARD), lambda j, t: (j, 0, 0)),
                  pl.BlockSpec((tm, D), lambda j, t: (nt - 1 - t, 0)), _spec_down(0)] + _dep_specs(deps),
        out_specs=[pl.BlockSpec((2, None, tm, GU_SHARD), lambda j, t: (0, j, nt - 1 - t, 0)),
                   pl.BlockSpec((None, HALO, GU_SHARD), lambda j, t: (j, 0, 0)),
                   pl.BlockSpec((None, 1, GU_SHARD), lambda j, t: (j, 0, 0))],
        out_shape=[_SDS(gu.shape, _ACT), _SDS((FF_BLOCKS, HALO, GU_SHARD), F32), _SDS((FF_BLOCKS, 1, GU_SHARD), F32)],
        scratch_shapes=[pltpu.VMEM((GLU_HALO, GU_SHARD), F32)],
        name=name, compiler_params=_cp("arbitrary", "arbitrary"))(gu, gu, gc, conv_w, dh, w_down, *deps)


def _bucket_table():
    qi = np.arange(BLK)[:, None]
    kj = np.arange(BLK)[None, :]
    n = np.where(kj > qi, BLK + qi - kj, qi - kj)
    max_exact = N_BUCKETS // 2
    nf = np.maximum(n, 1).astype(np.float32)
    large = max_exact + (np.log(nf / max_exact) / math.log(MAX_DISTANCE / max_exact)
                         * (N_BUCKETS - max_exact)).astype(np.int32)
    large = np.minimum(large, N_BUCKETS - 1)
    return np.where(n < max_exact, n, large).astype(np.int32)


def _lane_low():
    return lax.broadcasted_iota(jnp.int32, (1, 128), 1) < A_HD


def _swa_groups(q, kd, vd, sink, bias, upper, first):
    n = A_HEADS // A_KV_HEADS
    ng = len(q)
    low = _lane_low()
    qm = [jnp.concatenate([jnp.where(low == (h % 2 == 0), q[g][:, (h // 2) * 128:(h // 2 + 1) * 128], 0.0) for h in range(n)], axis=0)
          for g in range(ng)]
    s2 = [_mm_nt(qm[g], kd[g]) * (A_HD ** -0.5) for g in range(ng)]
    s = [jnp.where(upper[None], s2[g][:, :BLK].reshape(n, BLK, BLK), s2[g][:, BLK:].reshape(n, BLK, BLK)) + bias[g] for g in range(ng)]
    s = [t if f is None else jnp.where((upper & f)[None], -jnp.inf, t) for t, f in zip(s, first)]
    m = [lax.stop_gradient(jnp.maximum(jnp.max(s[g], axis=-1, keepdims=True), sink[g])) for g in range(ng)]
    p = [jnp.exp(s[g] - m[g]) for g in range(ng)]
    split = [jnp.concatenate([jnp.where(upper[None], t, 0.0), jnp.where(upper[None], 0.0, t)], axis=-1).reshape(n * BLK, 2 * BLK)
             for t in p]
    ones = jnp.ones((BLK, 128), F32)
    den = [_mm(p[g].reshape(n * BLK, BLK), ones) + jnp.exp(sink[g] - m[g]).reshape(n * BLK, 1) for g in range(ng)]
    o = [_mm(split[g], vd[g]) / den[g] for g in range(ng)]
    return [jnp.concatenate([jnp.where(low, t[2 * k * BLK:(2 * k + 1) * BLK], t[(2 * k + 1) * BLK:(2 * k + 2) * BLK])
                             for k in range(n // 2)], axis=1) for t in o]


def _mix_a_core(q, kd, vd, sink, bias, xq, mk, mv, upper, first):
    return _swa_groups(q, kd, vd, sink, bias, upper, first), _cross_pairs(xq, mk, mv)


def _swa_sinks(sink_ref, g):
    n = A_HEADS // A_KV_HEADS
    return jnp.concatenate([sink_ref[:, h:h + 1] for h in range(g * n, (g + 1) * n)], axis=0).reshape(n, 1, 1)


def _both_halves(t, t_rolled, g):
    low = _lane_low()
    return jnp.where(low, t, t_rolled) if g == 0 else jnp.where(low, t_rolled, t)


def _cross_pairs(q, mk, mv):
    rows = q.shape[0]
    low = _lane_low()
    qm = [jnp.concatenate([jnp.where(low, q[:, p * 128:(p + 1) * 128], 0.0), jnp.where(low, 0.0, q[:, p * 128:(p + 1) * 128])], axis=0)
          for p in range(X_HEADS // 2)]
    s = [_mm_nt(qm[p], mk[:, p * 128:(p + 1) * 128]) * (X_HD ** -0.5) for p in range(X_HEADS // 2)]
    e = [jnp.exp(t - lax.stop_gradient(jnp.max(t, axis=-1, keepdims=True))) for t in s]
    pr = [t / jnp.sum(t, axis=-1, keepdims=True) for t in e]
    o = [_mm(pr[p], mv[:, p * 128:(p + 1) * 128]) for p in range(X_HEADS // 2)]
    return jnp.concatenate([jnp.where(low, t[:rows], t[rows:]) for t in o], axis=1)


def _swa_upper():
    qi = lax.broadcasted_iota(jnp.int32, (BLK, BLK), 0)
    kj = lax.broadcasted_iota(jnp.int32, (BLK, BLK), 1)
    return kj > qi


def _bias_build(rel_bias, bucket, name):
    def body(rb_ref, bucket_ref, o_ref):
        b = bucket_ref[...]
        for h in range(A_HEADS):
            acc = jnp.zeros((BLK, BLK), F32)
            for k in range(N_BUCKETS):
                acc = jnp.where(b == k, rb_ref[k, h], acc)
            o_ref[h] = acc

    return pl.pallas_call(
        body, in_specs=[pl.BlockSpec(memory_space=pltpu.SMEM), pl.BlockSpec(memory_space=pltpu.VMEM)],
        out_specs=pl.BlockSpec(memory_space=pltpu.VMEM),
        out_shape=_SDS((A_HEADS, BLK, BLK), F32), name=name)(rel_bias, bucket)


def _bias_reduce(dbias, bucket, name):
    def body(db_ref, bucket_ref, o_ref):
        b = bucket_ref[...]
        row = lax.broadcasted_iota(jnp.int32, (N_BUCKETS, 128), 0)
        lane = lax.broadcasted_iota(jnp.int32, (N_BUCKETS, 128), 1)
        acc = jnp.zeros((N_BUCKETS, 128), F32)
        for h in range(A_HEADS):
            v = db_ref[h]
            for k in range(N_BUCKETS):
                sk = jnp.sum(jnp.sum(jnp.where(b == k, v, 0.0), axis=1, keepdims=True), axis=0, keepdims=True)
                acc = acc + jnp.where((row == k) & (lane == h), sk, 0.0)
        o_ref[...] = acc

    return pl.pallas_call(
        body, in_specs=[pl.BlockSpec(memory_space=pltpu.VMEM)] * 2,
        out_specs=pl.BlockSpec(memory_space=pltpu.VMEM),
        out_shape=_SDS((N_BUCKETS, 128), F32), name=name)(dbias, bucket)


def _mix_a_fwd(proj, bias, sinks, memkv, name):
    s = proj.shape[0]
    per = 2
    nb = s // (per * BLK)
    grp = A_HEADS // A_KV_HEADS

    def body(proj_ref, prev_ref, bias_ref, sink_ref, memkv_ref, o_ref):
        i = pl.program_id(0)
        upper = _swa_upper()
        proj = proj_ref[...].astype(F32)
        kv = jnp.concatenate([prev_ref[...].astype(F32), proj[:, A_Q:A_Q + 2 * A_KV]], axis=0)
        k, v = kv[:, :A_KV], kv[:, A_KV:]
        k_r = pltpu.roll(k, A_HD, 1)
        v_r = pltpu.roll(v, A_HD, 1)
        gw = A_Q // A_KV_HEADS
        each = [(b, g) for b in range(per) for g in range(A_KV_HEADS)]

        def window(a, a_r, b, g):
            return _both_halves(a[b * BLK:(b + 2) * BLK], a_r[b * BLK:(b + 2) * BLK], g)

        swa, cross = _mix_a_core([proj[b * BLK:(b + 1) * BLK, g * gw:(g + 1) * gw] for b, g in each],
                                 [window(k, k_r, b, g) for b, g in each], [window(v, v_r, b, g) for b, g in each],
                                 [_swa_sinks(sink_ref, g) for b, g in each], [bias_ref[g * grp:(g + 1) * grp] for b, g in each],
                                 proj[:, A_Q + 2 * A_KV:], memkv_ref[:, :X_Q], memkv_ref[:, X_Q:], upper,
                                 [(i == 0) if b == 0 else None for b, g in each])
        for b in range(per):
            o_ref[b * BLK:(b + 1) * BLK, :] = jnp.concatenate(
                swa[b * A_KV_HEADS:(b + 1) * A_KV_HEADS] + [cross[b * BLK:(b + 1) * BLK]], axis=1).astype(o_ref.dtype)

    return pl.pallas_call(
        body, grid=(nb,),
        in_specs=[pl.BlockSpec((per * BLK, IN_A), lambda i: (i, 0)),
                  pl.BlockSpec((BLK, 2 * A_KV), lambda i: (jnp.maximum(per * i - 1, 0), A_Q // (2 * A_KV))),
                  pl.BlockSpec((A_HEADS, BLK, BLK), lambda i: (0, 0, 0)),
                  pl.BlockSpec((1, 128), lambda i: (0, 0)),
                  pl.BlockSpec((MEM_LEN, 2 * X_Q), lambda i: (0, 0))],
        out_specs=pl.BlockSpec((per * BLK, D), lambda i: (i, 0)),
        out_shape=_SDS((s, D), _ACT), name=name, compiler_params=_cp("arbitrary"))(proj, proj, bias, sinks, memkv)


def _mix_a_bwd(proj, bias, sinks, memkv, dmix, name, deps=()):
    s = proj.shape[0]
    per = 2
    nb = s // (per * BLK)
    grp = A_HEADS // A_KV_HEADS

    def body(proj_ref, prev_ref, bias_ref, sink_ref, memkv_ref, dmix_ref, *rest):
        dproj_ref, dbias_ref, dsink_ref, dmemkv_ref, carry_ref = rest[-5:]
        t = pl.program_id(0)
        i = nb - 1 - t

        @pl.when(t == 0)
        def _():
            carry_ref[...] = jnp.zeros_like(carry_ref)
            dbias_ref[...] = jnp.zeros_like(dbias_ref)
            dsink_ref[...] = jnp.zeros_like(dsink_ref)
            dmemkv_ref[...] = jnp.zeros_like(dmemkv_ref)

        upper = _swa_upper()
        lane = lax.broadcasted_iota(jnp.int32, (1, 128), 1)
        low = _lane_low()
        proj = proj_ref[...].astype(F32)
        kv = jnp.concatenate([prev_ref[...].astype(F32), proj[:, A_Q:A_Q + 2 * A_KV]], axis=0)
        k, v = kv[:, :A_KV], kv[:, A_KV:]
        k_r = pltpu.roll(k, A_HD, 1)
        v_r = pltpu.roll(v, A_HD, 1)
        gw = A_Q // A_KV_HEADS
        each = [(b, g) for b in range(per) for g in range(A_KV_HEADS)]

        def window(a, a_r, b, g):
            return _both_halves(a[b * BLK:(b + 2) * BLK], a_r[b * BLK:(b + 2) * BLK], g)

        _, vjp = jax.vjp(
            functools.partial(_mix_a_core, upper=upper, first=[(i == 0) if b == 0 else None for b, g in each]),
            [proj[b * BLK:(b + 1) * BLK, g * gw:(g + 1) * gw] for b, g in each],
            [window(k, k_r, b, g) for b, g in each], [window(v, v_r, b, g) for b, g in each],
            [_swa_sinks(sink_ref, g) for b, g in each], [bias_ref[g * grp:(g + 1) * grp] for b, g in each],
            proj[:, A_Q + 2 * A_KV:], memkv_ref[:, :X_Q], memkv_ref[:, X_Q:])
        dqs, dk, dv, ds, db, dxq, dmk, dmv = vjp(
            ([dmix_ref[b * BLK:(b + 1) * BLK, g * gw:(g + 1) * gw].astype(F32) for b, g in each], dmix_ref[:, A_Q:].astype(F32)))
        dkd = [t + pltpu.roll(t, A_HD, 1) for t in dk]
        dvd = [t + pltpu.roll(t, A_HD, 1) for t in dv]
        dsink = jnp.zeros((1, 128), F32)
        for e, (b, g) in enumerate(each):
            for h in range(grp):
                dsink = dsink + jnp.where(lane == g * grp + h, ds[e][h], 0.0)
        for g in range(A_KV_HEADS):
            dbias_ref[g * grp:(g + 1) * grp] += db[g] + db[A_KV_HEADS + g]
        dsink_ref[...] += dsink
        dmemkv_ref[...] += jnp.concatenate([dmk, dmv], axis=1)
        dkv = [jnp.concatenate([jnp.where(low, dkd[b * A_KV_HEADS], dkd[b * A_KV_HEADS + 1]),
                                jnp.where(low, dvd[b * A_KV_HEADS], dvd[b * A_KV_HEADS + 1])], axis=1) for b in range(per)]
        own = [dkv[0][BLK:] + dkv[1][:BLK], dkv[1][BLK:] + carry_ref[...]]
        carry_ref[...] = dkv[0][:BLK]
        for b in range(per):
            dproj_ref[b * BLK:(b + 1) * BLK, :] = jnp.concatenate(
                list(dqs[b * A_KV_HEADS:(b + 1) * A_KV_HEADS]) + [own[b], dxq[b * BLK:(b + 1) * BLK]], axis=1).astype(dproj_ref.dtype)

    return pl.pallas_call(
        body, grid=(nb,),
        in_specs=[pl.BlockSpec((per * BLK, IN_A), lambda t: (nb - 1 - t, 0)),
                  pl.BlockSpec((BLK, 2 * A_KV), lambda t: (jnp.maximum(per * (nb - 1 - t) - 1, 0), A_Q // (2 * A_KV))),
                  pl.BlockSpec((A_HEADS, BLK, BLK), lambda t: (0, 0, 0)),
                  pl.BlockSpec((1, 128), lambda t: (0, 0)),
                  pl.BlockSpec((MEM_LEN, 2 * X_Q), lambda t: (0, 0)),
                  pl.BlockSpec((per * BLK, D), lambda t: (nb - 1 - t, 0))] + _dep_specs(deps),
        out_specs=[pl.BlockSpec((per * BLK, IN_A), lambda t: (nb - 1 - t, 0)),
                   pl.BlockSpec((A_HEADS, BLK, BLK), lambda t: (0, 0, 0)),
                   pl.BlockSpec((1, 128), lambda t: (0, 0)),
                   pl.BlockSpec((MEM_LEN, 2 * X_Q), lambda t: (0, 0))],
        out_shape=[_SDS((s, IN_A), _ACT), _SDS((A_HEADS, BLK, BLK), F32), _SDS((1, 128), F32),
                   _SDS((MEM_LEN, 2 * X_Q), F32)],
        scratch_shapes=[pltpu.VMEM((BLK, 2 * A_KV), F32)],
        name=name, compiler_params=_cp("arbitrary"))(proj, proj, bias, sinks, memkv, dmix, *deps)


def _neumann(pw, rhs):
    nh = len(pw)
    x = rhs
    for lvl in range(6):
        if lvl < 5:
            prod = [_mmf(pw[h], jnp.concatenate([x[h], pw[h]], axis=1)) for h in range(nh)]
            x = [x[h] + prod[h][:, :B_HD] for h in range(nh)]
            pw = [t[:, B_HD:] for t in prod]
        else:
            x = [x[h] + _mmf(pw[h], x[h]) for h in range(nh)]
    return x


@jax.custom_vjp
def _tri_solve(pw, rhs):
    return _neumann(pw, rhs)


def _tri_solve_fwd(pw, rhs):
    x = _neumann(pw, rhs)
    return x, (pw, x)


def _tri_solve_bwd(res, dx):
    pw, x = res
    d_rhs = _neumann([t.T for t in pw], list(dx))
    return [_mmf_nt(d_rhs[h], x[h]) for h in range(len(pw))], d_rhs


_tri_solve.defvjp(_tri_solve_fwd, _tri_solve_bwd)


@jax.custom_vjp
def _tri_solved(pw, rhs, x):
    return x


def _tri_solved_fwd(pw, rhs, x):
    return x, (pw, x)


def _tri_solved_bwd(res, dx):
    d_pw, d_rhs = _tri_solve_bwd(res, dx)
    return d_pw, d_rhs, [jnp.zeros_like(t) for t in res[1]]


_tri_solved.defvjp(_tri_solved_fwd, _tri_solved_bwd)


@jax.custom_vjp
def _known(x, value):
    return value


def _known_fwd(x, value):
    return value, None


def _known_bwd(_, g):
    return g, jnp.zeros_like(g)


_known.defvjp(_known_fwd, _known_bwd)


def _dn_heads(yq, yk, yv, z, bl, al, a_log, dtb, ng, s0, solved=None, out_known=None):
    c = CHUNK
    nh = B_V_HEADS
    rep = B_V_HEADS // B_QK_HEADS
    r = lax.broadcasted_iota(jnp.int32, (c, c), 0)
    cc = lax.broadcasted_iota(jnp.int32, (c, c), 1)
    q = [_silu(t) for t in yq]
    k = [_silu(t) for t in yk]
    v = [_silu(t) for t in yv]
    q = [t * lax.rsqrt(jnp.sum(t * t, axis=-1, keepdims=True) + EPS) * (B_HD ** -0.5) for t in q]
    k = [t * lax.rsqrt(jnp.sum(t * t, axis=-1, keepdims=True) + EPS) for t in k]
    beta = [jax.nn.sigmoid(t) for t in bl]
    g = [-jnp.exp(a_log[h]) * jax.nn.softplus(al[h] + dtb[h]) for h in range(nh)]
    gb = [jnp.broadcast_to(t, (c, c)) for t in g]
    gc_col = [jnp.sum(jnp.where(cc <= r, t.T, 0.0), axis=1, keepdims=True) for t in gb]
    gc_row = [jnp.sum(jnp.where(r <= cc, t, 0.0), axis=0, keepdims=True) for t in gb]
    gc_last = [jnp.sum(t, axis=0, keepdims=True) for t in g]
    decay = [jnp.exp(jnp.where(r >= cc, gc_col[h] - gc_row[h], -jnp.inf)) for h in range(nh)]
    kq = [_mmf_nt(jnp.concatenate([k[h], q[h]], axis=0), k[h]) for h in range(B_QK_HEADS)]
    kk = [t[:c] for t in kq]
    qk = [t[c:] for t in kq]
    egc = [jnp.exp(t) for t in gc_col]
    both = [_mmf(jnp.concatenate([(beta[h] * egc[h]) * k[h // rep], q[h // rep] * egc[h]], axis=0), s0[h]) for h in range(nh)]
    rhs = [beta[h] * v[h] - both[h][:c] for h in range(nh)]
    qs0 = [t[c:] for t in both]
    pw = [-(beta[h] * kk[h // rep] * jnp.where(r > cc, decay[h], 0.0)) for h in range(nh)]
    delta = _tri_solve(pw, rhs) if solved is None else _tri_solved(pw, rhs, solved)
    last = [_mmf(jnp.concatenate([qk[h // rep] * decay[h], (k[h // rep] * jnp.exp(gc_last[h] - gc_col[h])).T], axis=0), delta[h])
            for h in range(nh)]
    out = [qs0[h] + last[h][:c] for h in range(nh)]
    if out_known is not None:
        out = [_known(out[h], out_known[h]) for h in range(nh)]
    s1 = [jnp.exp(gc_last[h]) * s0[h] + last[h][c:] for h in range(nh)]
    o = [t * lax.rsqrt(jnp.mean(t * t, axis=-1, keepdims=True) + EPS) * ng for t in out]
    return [o[h] * _silu(z[h]) for h in range(nh)], s1, delta, out


def _dn_conv(ext, w_ref):
    y = ext * w_ref[B_CONV - 1:B_CONV, :]
    for j in range(B_CONV - 1):
        y = y + w_ref[j:j + 1, :] * pltpu.roll(ext, B_CONV - 1 - j, 0)
    return y


def _dn_args(y, cur_ref, par_ref, ng_ref):
    nh = B_V_HEADS
    return ([y[:, h * B_HD:(h + 1) * B_HD] for h in range(B_QK_HEADS)],
            [y[:, B_QK + h * B_HD:B_QK + (h + 1) * B_HD] for h in range(B_QK_HEADS)],
            [y[:, 2 * B_QK + h * B_HD:2 * B_QK + (h + 1) * B_HD] for h in range(nh)],
            [cur_ref[:, BP_Z + h * B_HD:BP_Z + (h + 1) * B_HD] for h in range(nh)],
            [cur_ref[:, BP_GATE + h:BP_GATE + h + 1] for h in range(nh)],
            [cur_ref[:, BP_GATE + nh + h:BP_GATE + nh + h + 1] for h in range(nh)],
            [par_ref[:, h:h + 1] for h in range(nh)], [par_ref[:, nh + h:nh + h + 1] for h in range(nh)], ng_ref[...])


def _mix_b_fwd(proj, conv_w, par, ng, memkv, name):
    s = proj.shape[0]
    nc = s // CHUNK

    def body(cur_ref, prev_ref, w_ref, par_ref, ng_ref, memkv_ref, o_ref, st_ref, dl_ref, state_ref):
        n = pl.program_id(0)

        @pl.when(n == 0)
        def _():
            state_ref[...] = jnp.zeros_like(state_ref)

        prev = jnp.where(n > 0, prev_ref[...], 0.0)
        ext = jnp.concatenate([prev, cur_ref[:, :B_QKV]], axis=0)
        y = _dn_conv(ext, w_ref)[HALO:]
        s0 = [state_ref[hv] for hv in range(B_V_HEADS)]
        st_ref[0] = state_ref[...]
        outs, s1, delta, raw = _dn_heads(*_dn_args(y, cur_ref, par_ref, ng_ref), s0)
        for hv in range(B_V_HEADS):
            state_ref[hv] = s1[hv]
            dl_ref[0, hv] = delta[hv]
            dl_ref[0, B_V_HEADS + hv] = raw[hv]
        outs = outs + [_cross_pairs(cur_ref[:, BP_XQ:BP_XQ + X_Q], memkv_ref[:, :X_Q], memkv_ref[:, X_Q:])]
        o_ref[...] = jnp.concatenate(outs, axis=1).astype(o_ref.dtype)

    return pl.pallas_call(
        body, grid=(nc,),
        in_specs=[pl.BlockSpec((CHUNK, IN_BP), lambda n: (n, 0)),
                  pl.BlockSpec((HALO, B_QKV), lambda n: (jnp.maximum(n * (CHUNK // HALO) - 1, 0), 0)),
                  pl.BlockSpec((HALO, B_QKV), lambda n: (0, 0)),
                  pl.BlockSpec((1, 128), lambda n: (0, 0)), pl.BlockSpec((1, 128), lambda n: (0, 0)),
                  pl.BlockSpec((MEM_LEN, 2 * X_Q), lambda n: (0, 0))],
        out_specs=[pl.BlockSpec((CHUNK, D), lambda n: (n, 0)),
                   pl.BlockSpec((1, B_V_HEADS, B_HD, B_HD), lambda n: (n, 0, 0, 0)),
                   pl.BlockSpec((1, 2 * B_V_HEADS, CHUNK, B_HD), lambda n: (n, 0, 0, 0))],
        out_shape=[_SDS((s, D), _ACT), _SDS((nc, B_V_HEADS, B_HD, B_HD), F32), _SDS((nc, 2 * B_V_HEADS, CHUNK, B_HD), F32)],
        scratch_shapes=[pltpu.VMEM((B_V_HEADS, B_HD, B_HD), F32)],
        name=name, compiler_params=_cp("arbitrary"))(proj, proj, conv_w, par, ng, memkv)


def _mix_b_bwd(proj, conv_w, par, ng, memkv, states, deltas, dmix, name):
    s = proj.shape[0]
    nc = s // CHUNK
    ext_rows = CHUNK + HALO

    def body(cur_ref, prev_ref, w_ref, par_ref, ng_ref, memkv_ref, st_ref, dl_ref, dmix_ref,
             dproj_ref, dw_ref, dpar_ref, dng_ref, dmemkv_ref, dstate_ref, carry_ref):
        t = pl.program_id(0)
        n = nc - 1 - t

        @pl.when(t == 0)
        def _():
            dstate_ref[...] = jnp.zeros_like(dstate_ref)
            carry_ref[...] = jnp.zeros_like(carry_ref)
            dw_ref[...] = jnp.zeros_like(dw_ref)
            dpar_ref[...] = jnp.zeros_like(dpar_ref)
            dng_ref[...] = jnp.zeros_like(dng_ref)
            dmemkv_ref[...] = jnp.zeros_like(dmemkv_ref)

        lane = lax.broadcasted_iota(jnp.int32, (1, 128), 1)
        prev = jnp.where(n > 0, prev_ref[...], 0.0)
        ext = jnp.concatenate([prev, cur_ref[:, :B_QKV]], axis=0)
        y = _dn_conv(ext, w_ref)[HALO:]
        solved = [dl_ref[0, hv] for hv in range(B_V_HEADS)]
        raw = [dl_ref[0, B_V_HEADS + hv] for hv in range(B_V_HEADS)]
        _, vjp = jax.vjp(functools.partial(_dn_heads, solved=solved, out_known=raw), *_dn_args(y, cur_ref, par_ref, ng_ref),
                         [st_ref[0, hv] for hv in range(B_V_HEADS)])
        none = [jnp.zeros((CHUNK, B_HD), F32)] * B_V_HEADS
        dyq, dyk, dyv, dz, gbl, gal, ga_log, gdtb, dng, gs0 = vjp(
            ([dmix_ref[:, hv * B_HD:(hv + 1) * B_HD].astype(F32) for hv in range(B_V_HEADS)],
             [dstate_ref[hv] for hv in range(B_V_HEADS)], none, none))
        dgate = jnp.zeros((CHUNK, 128), F32)
        dpar = jnp.zeros((1, 128), F32)
        for hv in range(B_V_HEADS):
            dstate_ref[hv] = gs0[hv]
            dgate = dgate + jnp.where(lane == hv, gbl[hv], 0.0) + jnp.where(lane == B_V_HEADS + hv, gal[hv], 0.0)
            dpar = dpar + jnp.where(lane == hv, ga_log[hv], 0.0) + jnp.where(lane == B_V_HEADS + hv, gdtb[hv], 0.0)
        dpar_ref[...] += dpar
        dng_ref[...] += dng
        _, vjp = jax.vjp(_cross_pairs, cur_ref[:, BP_XQ:BP_XQ + X_Q], memkv_ref[:, :X_Q], memkv_ref[:, X_Q:])
        dxq, dmk, dmv = vjp(dmix_ref[:, B_V:].astype(F32))
        dmemkv_ref[...] += jnp.concatenate([dmk, dmv], axis=1)
        dy = jnp.concatenate(list(dyq) + list(dyk) + list(dyv), axis=1)
        dy_ext = jnp.concatenate([jnp.zeros((HALO, B_QKV), F32), dy], axis=0)
        dext = dy_ext * w_ref[B_CONV - 1:B_CONV, :]
        dw_ref[B_CONV - 1:B_CONV, :] += jnp.sum(ext * dy_ext, axis=0, keepdims=True)
        for j in range(B_CONV - 1):
            sh = B_CONV - 1 - j
            dw_ref[j:j + 1, :] += jnp.sum(pltpu.roll(ext, sh, 0) * dy_ext, axis=0, keepdims=True)
            dext = dext + w_ref[j:j + 1, :] * pltpu.roll(dy_ext, ext_rows - sh, 0)
        tail = jnp.concatenate([jnp.zeros((CHUNK - HALO, B_QKV), F32), carry_ref[...]], axis=0)
        dqkv = dext[HALO:] + tail
        carry_ref[...] = dext[:HALO]
        dproj_ref[...] = jnp.concatenate([dqkv] + list(dz) + [dxq, dgate], axis=1).astype(dproj_ref.dtype)

    return pl.pallas_call(
        body, grid=(nc,),
        in_specs=[pl.BlockSpec((CHUNK, IN_BP), lambda t: (nc - 1 - t, 0)),
                  pl.BlockSpec((HALO, B_QKV), lambda t: (jnp.maximum((nc - 1 - t) * (CHUNK // HALO) - 1, 0), 0)),
                  pl.BlockSpec((HALO, B_QKV), lambda t: (0, 0)),
                  pl.BlockSpec((1, 128), lambda t: (0, 0)), pl.BlockSpec((1, 128), lambda t: (0, 0)),
                  pl.BlockSpec((MEM_LEN, 2 * X_Q), lambda t: (0, 0)),
                  pl.BlockSpec((1, B_V_HEADS, B_HD, B_HD), lambda t: (nc - 1 - t, 0, 0, 0)),
                  pl.BlockSpec((1, 2 * B_V_HEADS, CHUNK, B_HD), lambda t: (nc - 1 - t, 0, 0, 0)),
                  pl.BlockSpec((CHUNK, D), lambda t: (nc - 1 - t, 0))],
        out_specs=[pl.BlockSpec((CHUNK, IN_BP), lambda t: (nc - 1 - t, 0)),
                   pl.BlockSpec((HALO, B_QKV), lambda t: (0, 0)),
                   pl.BlockSpec((1, 128), lambda t: (0, 0)), pl.BlockSpec((1, 128), lambda t: (0, 0)),
                   pl.BlockSpec((MEM_LEN, 2 * X_Q), lambda t: (0, 0))],
        out_shape=[_SDS((s, IN_BP), _ACT), _SDS((HALO, B_QKV), F32), _SDS((1, 128), F32), _SDS((1, 128), F32),
                   _SDS((MEM_LEN, 2 * X_Q), F32)],
        scratch_shapes=[pltpu.VMEM((B_V_HEADS, B_HD, B_HD), F32), pltpu.VMEM((HALO, B_QKV), F32)],
        name=name, compiler_params=_cp("arbitrary"))(proj, proj, conv_w, par, ng, memkv, states, deltas, dmix)


def _place():
    return lax.axis_index("x"), lax.axis_index("y"), lax.axis_index("c")


def _all_gather(shards, name):
    n = len(shards)

    def body(*refs):
        ins, outs = refs[:n], refs[n:2 * n]
        send_sems, recv_sems, local_sems = refs[2 * n:]
        x, y, c = _place()
        me, sibling = (x, y, c), (x, y, 1 - c)
        chips = [(1 - x, y), (x, 1 - y), (1 - x, 1 - y)]

        def rows(a, px, py, pc):
            return outs[a].at[4 * px + 2 * py + pc]

        def copy(a, k, block, to, src=None):
            return pltpu.make_async_remote_copy(
                src_ref=rows(a, *block) if src is None else src, dst_ref=rows(a, *block),
                send_sem=send_sems.at[a, k], recv_sem=recv_sems.at[a, k],
                device_id=to, device_id_type=pl.DeviceIdType.MESH)

        mine = [pltpu.make_async_copy(ins[a], rows(a, *me), local_sems.at[a]) for a in range(n)]
        for cp in mine:
            cp.start()
        first = []
        for a in range(n):
            first.append(copy(a, 0, me, sibling, src=ins[a]))
            first += [copy(a, 1 + j, me, (*chip, c), src=ins[a]) for j, chip in enumerate(chips)]
        for cp in first:
            cp.start()
        passed = []
        for j, chip in enumerate(chips):
            for a in range(n):
                copy(a, 1 + j, (*chip, c), me).wait_recv()
                fwd = copy(a, 4 + j, (*chip, c), sibling)
                fwd.start()
                passed.append(fwd)
        for a in range(n):
            copy(a, 0, sibling, me).wait_recv()
            for j, chip in enumerate(chips):
                copy(a, 4 + j, (*chip, 1 - c), me).wait_recv()
        for cp in first + passed:
            cp.wait_send()
        for cp in mine:
            cp.wait()

    hbm = pl.BlockSpec(memory_space=pl.ANY)
    return pl.pallas_call(
        body, out_shape=[_SDS((N_DEV,) + s.shape, s.dtype) for s in shards],
        in_specs=[hbm] * n, out_specs=[hbm] * n,
        scratch_shapes=[pltpu.SemaphoreType.DMA((n, 7)), pltpu.SemaphoreType.DMA((n, 7)), pltpu.SemaphoreType.DMA((n,))],
        name=name)(*shards)


class _Exchange:
    def __init__(self, lands, srcs):
        self.lands, self.srcs = lands, srcs


def _seq_exchange(srcs, land_shapes, plan, name, cid):
    n, nl = len(srcs), len(land_shapes)

    def launch(*refs):
        src_refs, land_refs = refs[:n], refs[n:n + nl]
        send_sems, recv_sems, local_sems = refs[n + nl:]
        x, y, c = _place()
        my = 4 * x + 2 * y + c
        peers = [(x ^ ((k + 1) >> 2 & 1), y ^ ((k + 1) >> 1 & 1), c ^ ((k + 1) & 1)) for k in range(N_DEV - 1)]
        barrier = pltpu.get_barrier_semaphore()
        for p in peers:
            pl.semaphore_signal(barrier, inc=1, device_id=p, device_id_type=pl.DeviceIdType.MESH)
        pl.semaphore_wait(barrier, N_DEV - 1)

        def src_for(a, dest):
            return src_refs[a].at[dest] if plan[a][1] else src_refs[a]

        def slot(a, source):
            return land_refs[plan[a][0]].at[source]

        mine = [pltpu.make_async_copy(src_for(a, my), slot(a, my), local_sems.at[a]) for a in range(n)]
        for cp in mine:
            cp.start()
        sends, recvs = [], []
        for k, (px, py, pc) in enumerate(peers):
            peer = 4 * px + 2 * py + pc
            for a in range(n):
                kw = dict(send_sem=send_sems.at[a * (N_DEV - 1) + k], recv_sem=recv_sems.at[a * (N_DEV - 1) + k],
                          device_id=(px, py, pc), device_id_type=pl.DeviceIdType.MESH)
                sends.append(pltpu.make_async_remote_copy(src_ref=src_for(a, peer), dst_ref=slot(a, my), **kw))
                recvs.append(pltpu.make_async_remote_copy(src_ref=src_for(a, my), dst_ref=slot(a, peer), **kw))
        for cp in sends:
            cp.start()
        for cp in recvs:
            cp.wait_recv()
        for cp in sends:
            cp.wait_send()
        for cp in mine:
            cp.wait()

    lands = pl.kernel(
        launch, out_type=[_SDS(s, d) for s, d in land_shapes],
        mesh=plsc.ScalarSubcoreMesh(axis_name="sequencer", num_cores=1), name=name,
        scratch_types=(pltpu.SemaphoreType.DMA((n * (N_DEV - 1),)), pltpu.SemaphoreType.DMA((n * (N_DEV - 1),)),
                       pltpu.SemaphoreType.DMA((n,))),
        compiler_params=pltpu.CompilerParams(collective_id=cid))(*srcs)
    return _Exchange(list(lands), list(srcs))


def _adam_update(g, w, m, v):
    c1 = 1.0 - ADAM_B1 ** ADAM_STEP
    c2 = 1.0 - ADAM_B2 ** ADAM_STEP
    mm = ADAM_B1 * m + (1.0 - ADAM_B1) * g
    vv = ADAM_B2 * v + (1.0 - ADAM_B2) * (g * g)
    delta = -ADAM_LR * ((mm / c1) / (jnp.sqrt(vv / c2) + ADAM_EPS) + ADAM_WD * w)
    return delta, mm, vv


def _sum_sources(p_ref):
    g = p_ref[0].astype(F32)
    for s in range(1, N_DEV):
        g = g + p_ref[s].astype(F32)
    return g


def _adamw(parts, w, m, v, tr, name, restore_b=False, deps=()):
    nl, r, c = w.shape
    cp = parts[0].shape[-1]

    def body(*refs):
        p_refs = refs[:nl]
        w_ref, m_ref, v_ref = refs[nl:nl + 3]
        g_ref, d_ref, nm_ref, nv_ref = refs[-4:]
        g = _sum_sources(p_refs[0])
        for l in range(1, nl):
            g = jnp.where(pl.program_id(0) == l, _sum_sources(p_refs[l]), g)
        if restore_b:
            g = jnp.concatenate([g[:, :BP_XQ], g[:, BP_GATE:BP_GATE + 2 * B_V_HEADS], g[:, BP_XQ:BP_GATE]], axis=1)
        delta, mm, vv = _adam_update(g, w_ref[...], m_ref[...], v_ref[...])
        g_ref[...] = g
        d_ref[...] = delta
        nm_ref[...] = mm
        nv_ref[...] = vv

    spec = pl.BlockSpec((None, tr, c), lambda l, i: (l, i, 0))
    part_specs = [pl.BlockSpec((N_DEV, tr, cp), functools.partial(lambda l, i, k: (0, jnp.where(l == k, i, 0), 0), k=k))
                  for k in range(nl)]
    return pl.pallas_call(
        body, grid=(nl, r // tr),
        in_specs=part_specs + [spec, spec, spec] + _dep_specs(deps),
        out_specs=[spec] * 4, out_shape=[_SDS(w.shape, F32)] * 4,
        name=name, compiler_params=_cp("arbitrary", "arbitrary"))(*parts, w, m, v, *deps)


def _pack_small(d_rel, d_cb, d_cw, d_qkv, d_mix, d_mem, d_ffn, d_final, d_sinks, d_par, d_ng, loss_row, name):
    flat = [d_rel, *d_cb, *d_cw, d_qkv, *d_mix, *d_mem, *d_ffn, d_final, d_sinks, d_par, d_ng, loss_row]
    n = len(flat)

    def body(*refs):
        ins, o_ref = refs[:n], refs[n]
        rel, cb0, cb1, cw0, cw1, qkv, mx0, mx1, me0, me1, ff0, ff1, fin, snk, par, ng, lss = ins
        o_ref[...] = jnp.zeros_like(o_ref)
        for k in range(N_BUCKETS):
            lane = SP_REL_LANE + 128 * (k % 8)
            o_ref[SP_QKV + k // 8:SP_QKV + k // 8 + 1, lane:lane + 128] = rel[k:k + 1, :]
        for l, (cb, cw) in enumerate(((cb0, cw0), (cb1, cw1))):
            o_ref[SP_CB + l:SP_CB + l + 1, :] = jnp.concatenate([cb[j] for j in range(FF_BLOCKS)], axis=1)
            full = jnp.concatenate([cw[j] for j in range(FF_BLOCKS)], axis=1)
            o_ref[SP_CW + FFN_CONV * l:SP_CW + FFN_CONV * (l + 1), :] = full[:FFN_CONV]
        o_ref[SP_QKV:SP_QKV + B_CONV, 0:B_QKV] = qkv[0:B_CONV, :]
        for base, pair in ((SP_MIX, (mx0, mx1)), (SP_MEM, (me0, me1)), (SP_FFN, (ff0, ff1))):
            for l in range(2):
                o_ref[base + l:base + l + 1, 0:D] = pair[l][...]
        o_ref[SP_FINAL:SP_FINAL + 1, 0:D] = fin[...]
        o_ref[SP_MISC:SP_MISC + 1, 0:128] = snk[...]
        o_ref[SP_MISC:SP_MISC + 1, 128:256] = par[...]
        o_ref[SP_MISC:SP_MISC + 1, 256:384] = ng[...]
        o_ref[SP_MISC:SP_MISC + 1, 384:512] = lss[...]

    vm = pl.BlockSpec(memory_space=pltpu.VMEM)
    return pl.pallas_call(body, in_specs=[vm] * n, out_specs=vm, out_shape=_SDS((SMALL_ROWS, D_FF), F32), name=name)(*flat)


_SMALL = ["rel_bias", "norm_mix_g", "norm_mem_g", "sinks_a", "a_log_b", "dt_bias_b", "out_norm_g_b", "norm_ffn_g",
          "ffn_conv_b", "final_norm_g", "conv_qkv_b", "ffn_conv_w"]


def _adamw_small(recv, rc_qkv, rc_ffn, ws, ms, vs, name, deps=()):
    n = len(_SMALL)

    def body(*refs):
        recv_ref, qkv_ref, ffn_ref = refs[:3]
        w_refs, m_refs, v_refs = refs[3:3 + n], refs[3 + n:3 + 2 * n], refs[3 + 2 * n:3 + 3 * n]
        outs, loss_ref = refs[len(refs) - 4 * n - 1:len(refs) - 1], refs[-1]
        gs = _sum_sources(recv_ref)
        loss_ref[...] = gs[SP_MISC:SP_MISC + 1, 384:512]
        grads = {
            "rel_bias": jnp.concatenate(
                [gs[SP_QKV + k // 8:SP_QKV + k // 8 + 1, SP_REL_LANE + 128 * (k % 8):SP_REL_LANE + 128 * (k % 8) + A_HEADS]
                 for k in range(N_BUCKETS)], axis=0),
            "norm_mix_g": gs[SP_MIX:SP_MIX + 2, 0:D], "norm_mem_g": gs[SP_MEM:SP_MEM + 2, 0:D],
            "sinks_a": gs[SP_MISC:SP_MISC + 1, 0:A_HEADS],
            "a_log_b": gs[SP_MISC:SP_MISC + 1, 128:128 + B_V_HEADS],
            "dt_bias_b": gs[SP_MISC:SP_MISC + 1, 128 + B_V_HEADS:128 + 2 * B_V_HEADS],
            "out_norm_g_b": gs[SP_MISC:SP_MISC + 1, 256:256 + B_HD],
            "norm_ffn_g": gs[SP_FFN:SP_FFN + 2, 0:D], "ffn_conv_b": gs[SP_CB:SP_CB + 2, :],
            "final_norm_g": gs[SP_FINAL:SP_FINAL + 1, 0:D],
            "conv_qkv_b": _sum_sources(qkv_ref), "ffn_conv_w": _sum_sources(ffn_ref),
        }
        for i, nm in enumerate(_SMALL):
            g = grads[nm]
            delta, mm, vv = _adam_update(g, w_refs[i][...], m_refs[i][...], v_refs[i][...])
            outs[i][...] = g
            outs[n + i][...] = delta
            outs[2 * n + i][...] = mm
            outs[3 * n + i][...] = vv

    vm = pl.BlockSpec(memory_space=pltpu.VMEM)
    shapes = [_SDS(w.shape, F32) for w in ws]
    return pl.pallas_call(
        body, in_specs=[vm] * (3 + 3 * n) + _dep_specs(deps), out_specs=[vm] * (4 * n + 1),
        out_shape=shapes * 4 + [_SDS((1, 128), F32)],
        name=name)(recv, rc_qkv, rc_ffn, *ws, *ms, *vs, *deps)


def _assemble(gathered, axis):
    g = jnp.moveaxis(gathered, 0, axis)
    shp = list(g.shape)
    return g.reshape(shp[:axis] + [shp[axis] * shp[axis + 1]] + shp[axis + 2:])


def _pad_rows(a, rows):
    return jnp.pad(a, ((0, rows - a.shape[0]), (0, 0)))


def _pad_lanes(a, lanes=128):
    return jnp.pad(a, ((0, 0), (0, lanes - a.shape[1])))


def _ff_blocks(a):
    return jnp.moveaxis(a.reshape(a.shape[0], FF_BLOCKS, GU_SHARD), 1, 0)


def _reorder_b(w):
    qkv_z = w[..., :B_QKV + B_V]
    gates = w[..., B_QKV + B_V:B_QKV + B_V + 2 * B_V_HEADS]
    xq = w[..., IN_B - X_Q:]
    pad = jnp.zeros(w.shape[:-1] + (IN_BP - IN_B,), w.dtype)
    return jnp.concatenate([qkv_z, xq, gates, pad], axis=-1)


def kernel(x, mem, rel_bias, norm_mix_g, norm_mem_g, w_mem_kv, w_out, w_in_a, sinks_a, w_in_b, conv_qkv_b, a_log_b, dt_bias_b, out_norm_g_b, norm_ffn_g, w_gate_up, ffn_conv_w, ffn_conv_b, w_down, final_norm_g, loss_target, m_rel_bias, m_norm_mix_g, m_norm_mem_g, m_w_mem_kv, m_w_out, m_w_in_a, m_sinks_a, m_w_in_b, m_conv_qkv_b, m_a_log_b, m_dt_bias_b, m_out_norm_g_b, m_norm_ffn_g, m_w_gate_up, m_ffn_conv_w, m_ffn_conv_b, m_w_down, m_final_norm_g, v_rel_bias, v_norm_mix_g, v_norm_mem_g, v_w_mem_kv, v_w_out, v_w_in_a, v_sinks_a, v_w_in_b, v_conv_qkv_b, v_a_log_b, v_dt_bias_b, v_out_norm_g_b, v_norm_ffn_g, v_w_gate_up, v_ffn_conv_w, v_ffn_conv_b, v_w_down, v_final_norm_g):
    local = dict(locals())
    order = ["rel_bias", "norm_mix_g", "norm_mem_g", "w_mem_kv", "w_out", "w_in_a", "sinks_a", "w_in_b", "conv_qkv_b",
             "a_log_b", "dt_bias_b", "out_norm_g_b", "norm_ffn_g", "w_gate_up", "ffn_conv_w", "ffn_conv_b", "w_down",
             "final_norm_g"]
    wts = {n: local[n] for n in order}
    moms = {n: local["m_" + n] for n in order}
    vars_ = {n: local["v_" + n] for n in order}
    h0 = x[0]
    memx = mem[0]
    tgt = loss_target[0]
    s = h0.shape[0]
    tm = _rows(s)
    tb = min(s, _TM_BIG)

    t_ = lambda a: jnp.swapaxes(a, 1, 2)
    g_mk0, g_out0, g_ia, g_cq, g_cw = _all_gather(
        [w_mem_kv[0:1].astype(_MXU), w_out[0:1].astype(_MXU), t_(w_in_a).astype(_MXU), conv_qkv_b, ffn_conv_w], "gather_first")
    g_mk, g_out = [g_mk0], [g_out0]
    gu_land = ((N_DEV, GU_SHARD, D), _MXU)
    dn_land = ((N_DEV, DN_SHARD, D), _MXU)
    whole = [(0, False), (1, False)]
    def after(a, b):
        return a + (b[(0,) * b.ndim] * 0).astype(a.dtype)

    gu0_w = _seq_exchange([after(t_(w_gate_up)[0].astype(_MXU), g_ia)], [gu_land], [(0, False)], "gather_gate_up0", 1)
    dn0_w = _seq_exchange([after(w_down[0].astype(_MXU), g_ia)], [dn_land], [(0, False)], "gather_down0", 8)
    w_ia = g_ia.reshape(IN_A, D)
    conv_qkv = _pad_rows(_assemble(g_cq, 2)[0], HALO)
    ffn_cw_full = _assemble(g_cw, 2)
    ffn_cw = [_ff_blocks(_pad_rows(ffn_cw_full[i], HALO)) for i in range(2)]
    ffn_cb = [_ff_blocks(ffn_conv_b[i:i + 1]) for i in range(2)]
    bucket = jnp.asarray(_bucket_table())
    bias = _bias_build(rel_bias, bucket, "bias_build")
    sinks = _pad_lanes(sinks_a)
    par_b = _pad_lanes(jnp.concatenate([a_log_b, dt_bias_b], axis=1))

    row_x = pl.BlockSpec((tm, D), lambda i, j: (i, 0))
    gu_shape = (2, FF_BLOCKS, s, GU_SHARD)

    def in_proj(h, g, w, w_spec, n_cols, tn, name, deps=(), out_dtype=F32, w_t=False, tm=None):
        return _norm_matmul(h, g, w, w_spec, n_cols // tn, (h.shape[0], n_cols),
                            pl.BlockSpec((tm or _rows(h.shape[0]), tn), lambda i, j: (i, j)), name, deps=deps, out_dtype=out_dtype,
                            w_t=w_t, tm=tm)

    def ffn_fwd(i, h, g_gu, g_dn, deps=()):
        gu, hn = _norm_matmul(h, norm_ffn_g[i:i + 1], g_gu, _spec_gate_up(1), N_DEV, gu_shape,
                              _spec_gu_act(0, 1, tb), f"gate_up_{i}", deps=deps, out_dtype=_ACT, w_t=True, tm=tb)
        h_new, act, gc = _glu_down(gu, ffn_cw[i], ffn_cb[i], g_dn, h, f"glu_down_{i}")
        return h_new, gu, hn, (act, gc)

    def out_proj(i, mix, h):
        return _matmul_res(mix, row_x, g_out[i], _spec_rowsharded(0, D // N_DEV, D), 1, h, f"out_proj_{i}")

    proj_a, hn_a = in_proj(h0, norm_mix_g[0:1], w_ia, pl.BlockSpec((640, D), lambda i, j: (j, 0)), IN_A, 640, "in_proj_a",
                           deps=gu0_w.srcs + dn0_w.srcs, out_dtype=_ACT, w_t=True)
    memkv0, memn0 = in_proj(memx, norm_mem_g[0:1], g_mk[0], _spec_rowsharded(0, D // N_DEV, 2 * X_Q), 2 * X_Q, 2 * X_Q, "mem_proj_0")
    mix_a = _mix_a_fwd(proj_a, bias, sinks, memkv0, "mix_a_fwd")
    h1 = out_proj(0, mix_a, h0)
    g_gu0, g_dn0 = gu0_w.lands[0], dn0_w.lands[0]
    in_b_w = _seq_exchange([after(_reorder_b(w_in_b).astype(_MXU), h1), after(w_mem_kv[1:2].astype(_MXU), h1)],
                           [((N_DEV, 1, D // N_DEV, IN_BP), _MXU), ((N_DEV, 1, D // N_DEV, 2 * X_Q), _MXU)], whole, "gather_in_b", 2)
    ffn1_w = _seq_exchange([after(t_(w_gate_up)[1].astype(_MXU), h1), after(w_down[1].astype(_MXU), h1),
                            after(w_out[1:2].astype(_MXU), h1)], [gu_land, dn_land, ((N_DEV, 1, D // N_DEV, D), _MXU)],
                           whole + [(2, False)], "gather_ffn1", 3)
    h2, gu0, hn_f0, act0 = ffn_fwd(0, h1, g_gu0, g_dn0, deps=in_b_w.srcs + ffn1_w.srcs)
    g_ib, g_mk1 = in_b_w.lands
    g_gu1, g_dn1, g_out1 = ffn1_w.lands
    g_mk.append(g_mk1)
    g_out.append(g_out1)
    proj_b, hn_b = in_proj(h2, norm_mix_g[1:2], g_ib, _spec_rowsharded(0, D // N_DEV, 896, col_block=1), IN_BP, 896, "in_proj_b")
    memkv1, memn1 = in_proj(memx, norm_mem_g[1:2], g_mk[1], _spec_rowsharded(0, D // N_DEV, 2 * X_Q), 2 * X_Q, 2 * X_Q, "mem_proj_1",
                            deps=[h2])
    mix_b, states, deltas = _mix_b_fwd(proj_b, conv_qkv, par_b, out_norm_g_b, memkv1, "mix_b_fwd")
    h3 = out_proj(1, mix_b, h2)
    h4, gu1, hn_f1, act1 = ffn_fwd(1, h3, g_gu1, g_dn1)
    loss_row, *dh, d_final_g = _loss_head(h4, final_norm_g[None, :], tgt, "loss_head")

    zeros_mem = jnp.zeros_like(memx)
    per_dest2 = [(0, True), (1, True)]

    def ffn_bwd(i, dh, h_in, gu, hn_f, act_gc, g_gu, g_dn, deps=()):
        act, gc = act_gc
        dgu, d_cw, d_cb = _glu_bwd(gu, gc, ffn_cw[i], dh[1], g_dn, f"glu_bwd_{i}", deps=deps)
        d_wdown = _matmul_tn(act, pl.BlockSpec((None, tb, GU_SHARD), lambda j, r: (j, r, 0)),
                             dh[1], pl.BlockSpec((tb, D), lambda j, r: (r, 0)), s, FF_BLOCKS, (GU_SHARD, D),
                             (N_DEV, DN_SHARD, D), pl.BlockSpec((2, DN_SHARD, D), lambda j, r: (j, 0, 0)), f"d_w_down_{i}",
                             tm=tb)
        *dh_new, d_g = _matmul_nt_normbwd(dgu, _spec_gu_act(0, 1, tm), g_gu, _spec_gate_up(1), N_DEV, h_in,
                                          norm_ffn_g[i:i + 1], dh[0], f"d_ffn_in_{i}", w_t=True, act_copy=True)
        d_wgu = _matmul_tn(dgu, _spec_gu_act(1, 0, tb), hn_f, pl.BlockSpec((tb, D), lambda j, r: (r, 0)), s, N_DEV,
                           (GU_SHARD, D), (N_DEV, GU_SHARD, D), pl.BlockSpec((None, GU_SHARD, D), lambda j, r: (j, 0, 0)),
                           f"d_w_gate_up_{i}", tm=tb)
        return dh_new, [d_wdown, d_wgu], d_cw, d_cb, d_g

    def out_bwd(i, dh, mix, deps):
        dmix = _matmul_nt(dh[1], g_out[i], _spec_rowsharded(0, D // N_DEV, D), 1, (s, D), row_x, f"d_mix_{i}", deps=deps, out_dtype=_ACT)
        d_wout = _matmul_tn(mix, pl.BlockSpec((tb, D), lambda j, r: (r, 0)), dh[1], pl.BlockSpec((tb, D), lambda j, r: (r, 0)),
                            s, 1, (D, D), (N_DEV, D // N_DEV, D), pl.BlockSpec((N_DEV, D // N_DEV, D), lambda j, r: (0, 0, 0)),
                            f"d_w_out_{i}", tm=tb)
        return dmix, d_wout

    def mem_bwd(i, dmemkv, memn):
        tmm = _rows(MEM_LEN)
        *_, d_g = _matmul_nt_normbwd(dmemkv, pl.BlockSpec((tmm, 2 * X_Q), lambda r, j: (r, 0)), g_mk[i],
                                     _spec_rowsharded(0, D // N_DEV, 2 * X_Q), 1, memx, norm_mem_g[i:i + 1], zeros_mem,
                                     f"d_mem_in_{i}")
        by_row = lambda j, r: (r, 0)
        d_w = _matmul_tn(memn, pl.BlockSpec((tmm, D), by_row), dmemkv, pl.BlockSpec((tmm, 2 * X_Q), by_row), MEM_LEN, 1,
                         (D, 2 * X_Q), (N_DEV, D // N_DEV, 2 * X_Q),
                         pl.BlockSpec((N_DEV, D // N_DEV, 2 * X_Q), lambda j, r: (0, 0, 0)), f"d_w_mem_kv_{i}")
        return d_w, d_g

    out_land = ((N_DEV, D // N_DEV, D), _WIRE)
    mk_land = ((N_DEV, D // N_DEV, 2 * X_Q), _WIRE)
    ffn_lands = [((N_DEV, DN_SHARD, D), _WIRE), ((N_DEV, GU_SHARD, D), _WIRE)]
    dh, d_ffn1, d_cw1, d_cb1, d_gf1 = ffn_bwd(1, dh, h3, gu1, hn_f1, act1, g_gu1, g_dn1)
    ffn1_g = _seq_exchange(d_ffn1, ffn_lands, per_dest2, "send_ffn1_grads", 5)
    dmix, d_wout1 = out_bwd(1, dh, mix_b, ffn1_g.srcs)
    dproj_b, d_convw, d_par, d_ng, dmemkv1 = _mix_b_bwd(proj_b, conv_qkv, par_b, out_norm_g_b, memkv1, states, deltas, dmix, "mix_b_bwd")
    *dh, d_gm1 = _matmul_nt_normbwd(dproj_b, pl.BlockSpec((tm, 896), lambda i, j: (i, j)), g_ib,
                                    _spec_rowsharded(0, D // N_DEV, 896, col_block=1), IN_BP // 896, h2, norm_mix_g[1:2], dh[0],
                                    "d_in_b", act_copy=True)
    d_wib = _matmul_tn(hn_b, pl.BlockSpec((tb, D), lambda j, r: (r, 0)), dproj_b, pl.BlockSpec((tb, 896), lambda j, r: (r, j)),
                       s, IN_BP // 896, (D, 896), (N_DEV, D // N_DEV, IN_BP),
                       pl.BlockSpec((N_DEV, D // N_DEV, 896), lambda j, r: (0, 0, j)), "d_w_in_b", tm=tb)
    d_wmk1, d_gmem1 = mem_bwd(1, dmemkv1, memn1)
    mix1_g = _seq_exchange([d_wout1, d_wib, d_wmk1], [out_land, ((N_DEV, D // N_DEV, IN_BP), _WIRE), mk_land],
                           [(0, True), (1, True), (2, True)], "send_mix1_grads", 6)
    dh, d_ffn0, d_cw0, d_cb0, d_gf0 = ffn_bwd(0, dh, h1, gu0, hn_f0, act0, g_gu0, g_dn0, deps=mix1_g.srcs)
    dmix, d_wout0 = out_bwd(0, dh, mix_a, d_ffn0 + ffn1_g.lands[:1])
    ffn0_g = _seq_exchange(d_ffn0 + [d_wout0], ffn_lands + [out_land], per_dest2 + [(2, True)], "send_ffn0_grads", 4)
    dproj_a, dbias, dsinks, dmemkv0 = _mix_a_bwd(proj_a, bias, sinks, memkv0, dmix, "mix_a_bwd", deps=ffn0_g.srcs)
    dx, _, d_gm0 = _matmul_nt_normbwd(dproj_a, pl.BlockSpec((tm, 640), lambda i, j: (i, j)), w_ia,
                                      pl.BlockSpec((640, D), lambda i, j: (j, 0)), IN_A // 640, h0, norm_mix_g[0:1], dh[0],
                                      "d_in_a", w_t=True)
    d_wia = _matmul_tn(dproj_a, pl.BlockSpec((tb, IN_A), lambda j, r: (r, 0)), hn_a, pl.BlockSpec((tb, D), lambda j, r: (r, 0)),
                       s, 1, (IN_A, D), (N_DEV, IA_SHARD, D), pl.BlockSpec((N_DEV, IA_SHARD, D), lambda j, r: (0, 0, 0)),
                       "d_w_in_a", tm=tb)
    d_wmk0, d_gmem0 = mem_bwd(0, dmemkv0, memn0)
    d_rel = _bias_reduce(dbias, bucket, "bias_reduce")
    small = _pack_small(d_rel, (d_cb0, d_cb1), (d_cw0, d_cw1), d_convw, (d_gm0, d_gm1), (d_gmem0, d_gmem1),
                        (d_gf0, d_gf1), d_final_g, dsinks, d_par, d_ng, loss_row, "pack_small")
    mix0_g = _seq_exchange([d_wia, d_wmk0, small],
                           [((N_DEV, IA_SHARD, D), _WIRE), mk_land, ((N_DEV, SMALL_ROWS, D_FF), F32)],
                           [(0, True), (1, True), (2, False)], "send_mix0_grads", 7)

    res = {}
    last = []

    def update(nm, parts, tr, restore=False, transposed=False):
        view = t_ if transposed else (lambda a: a)
        out = _adamw(parts, view(wts[nm]), view(moms[nm]), view(vars_[nm]), tr, "adamw_" + nm, restore_b=restore, deps=last[-1:])
        res[nm] = [view(o) for o in out]
        last.append(out[1])

    r_dn1, r_gu1 = ffn1_g.lands
    r_dn0, r_gu0, r_out0 = ffn0_g.lands
    r_out1, r_ib, r_mk1 = mix1_g.lands
    update("w_in_b", [r_ib], 32, True)
    update("w_gate_up", [r_gu0, r_gu1], 176, transposed=True)
    update("w_down", [r_dn0, r_dn1], 176)
    update("w_out", [r_out0, r_out1], 128)
    r_ia, r_mk0, r_small = mix0_g.lands
    update("w_mem_kv", [r_mk0, r_mk1], 128)
    update("w_in_a", [r_ia], IA_SHARD, transposed=True)

    my = 4 * lax.axis_index("x") + 2 * lax.axis_index("y") + lax.axis_index("c")
    cq = conv_qkv_b.shape[-1]
    cf = ffn_conv_w.shape[-1]
    rc_qkv = lax.dynamic_slice_in_dim(r_small[:, SP_QKV:SP_QKV + B_CONV, :B_QKV], my * cq, cq, axis=2)[:, None]
    rc_ffn = lax.dynamic_slice_in_dim(r_small[:, SP_CW:SP_CW + 2 * FFN_CONV, :], my * cf, cf, axis=2).reshape(N_DEV, 2, FFN_CONV, cf)
    as2d = lambda a: a[None, :] if a.ndim == 1 else a
    small_out = _adamw_small(r_small, rc_qkv, rc_ffn, [as2d(wts[n]) for n in _SMALL], [as2d(moms[n]) for n in _SMALL],
                             [as2d(vars_[n]) for n in _SMALL], "adamw_small", deps=last[-1:])
    ns = len(_SMALL)
    for i, nm in enumerate(_SMALL):
        res[nm] = [small_out[k * ns + i].reshape(wts[nm].shape) for k in range(4)]

    return (small_out[-1][0, 0], dx[None], *[res[n][0] for n in order], *[res[n][1] for n in order],
            *[res[n][2] for n in order], *[res[n][3] for n in order])
```

```python
import functools
import math

import numpy as np

import jax
import jax.numpy as jnp
from jax import lax
from jax.experimental import pallas as pl
from jax.experimental.pallas import tpu as pltpu
from jax.experimental.pallas import tpu_sc as plsc

F32 = jnp.float32
_MXU = jnp.bfloat16
_ACT = jnp.bfloat16
_WIRE = jnp.bfloat16
_HI = lax.Precision.HIGH
_TM = 1024
_TM_GLU = 1024
_TM_BIG = 2048
_VMEM_LIMIT = 48 * 1024 * 1024
_SDS = jax.ShapeDtypeStruct

D = 1024
EPS = 1e-6
A_HEADS, A_KV_HEADS, A_HD, BLK = 12, 2, 64, 128
N_BUCKETS, MAX_DISTANCE = 32, 128
B_QK_HEADS, B_V_HEADS, B_HD, B_CONV, CHUNK = 3, 6, 128, 4, 64
X_HEADS, X_HD, MEM_LEN = 4, 64, 256
D_FF, FFN_CONV = 2816, 3
A_Q, A_KV, X_Q = 768, 128, 256
B_QK, B_V, B_QKV = 384, 768, 1536
IN_A, IN_B = 1280, 2572
IN_BP = 2688
BP_Z, BP_XQ, BP_GATE = 1536, 2304, 2560
HALO = 8
GLU_HALO = 16

N_DEV = 8
GU_SHARD = 2 * D_FF // N_DEV
FF_BLOCKS = D_FF // GU_SHARD
DN_SHARD = D_FF // N_DEV
IA_SHARD = IN_A // N_DEV

ADAM_LR, ADAM_B1, ADAM_B2, ADAM_EPS, ADAM_WD, ADAM_STEP = 0.001, 0.9, 0.999, 1e-08, 0.01, 10

SP_CB, SP_CW, SP_QKV, SP_MIX, SP_MEM, SP_FFN, SP_FINAL, SP_MISC, SMALL_ROWS = 0, 2, 8, 12, 14, 16, 18, 19, 24
SP_REL_LANE = B_QKV


def _cp(*sems):
    return pltpu.CompilerParams(dimension_semantics=sems, vmem_limit_bytes=_VMEM_LIMIT)


def _mm(a, b):
    return jnp.dot(a.astype(_MXU), b.astype(_MXU), preferred_element_type=F32)


def _mm_nt(a, b):
    return lax.dot_general(a.astype(_MXU), b.astype(_MXU), (((1,), (1,)), ((), ())), preferred_element_type=F32)


def _mm_tn(a, b):
    return lax.dot_general(a.astype(_MXU), b.astype(_MXU), (((0,), (0,)), ((), ())), preferred_element_type=F32)


def _mmf(a, b):
    return jnp.dot(a, b, preferred_element_type=F32, precision=_HI)


def _mmf_nt(a, b):
    return lax.dot_general(a, b, (((1,), (1,)), ((), ())), preferred_element_type=F32, precision=_HI)


def _silu(x):
    return x * jax.nn.sigmoid(x)


def _w2d(ref):
    v = ref[...]
    return v.reshape(-1, v.shape[-1])


def _rows(m):
    return min(m, _TM)


def _spec_rowsharded(layer, rows, cols, col_block=None):
    if col_block is None:
        return pl.BlockSpec((N_DEV, None, rows, cols), lambda *_: (0, layer, 0, 0))
    return pl.BlockSpec((N_DEV, None, rows, cols), lambda *ids: (0, layer, 0, ids[col_block]))


def _spec_gate_up(axis):
    return pl.BlockSpec((None, GU_SHARD, D), lambda *ids: (ids[axis], 0, 0))


def _spec_down(axis):
    return pl.BlockSpec((2, DN_SHARD, D), lambda *ids: (ids[axis], 0, 0))


def _dep_specs(deps):
    return [pl.BlockSpec(memory_space=pl.ANY) for d in deps]


def _spec_gu_act(row_axis, axis, tm):
    return pl.BlockSpec((None, None, tm, GU_SHARD), lambda *ids: (ids[axis] // FF_BLOCKS, ids[axis] % FF_BLOCKS, ids[row_axis], 0))


def _norm_matmul(x, g, w, w_spec, n_blocks, out_shape, out_spec, name, deps=(), out_dtype=F32, w_t=False, tm=None):
    m, k = x.shape
    tm = tm or _rows(m)

    def body(x_ref, g_ref, w_ref, *rest):
        y_ref, hn_ref = rest[-2:]

        @pl.when(pl.program_id(1) == 0)
        def _():
            xv = x_ref[...]
            r = lax.rsqrt(jnp.mean(xv * xv, axis=-1, keepdims=True) + EPS)
            hn_ref[...] = (xv * r * g_ref[...]).astype(hn_ref.dtype)

        y_ref[...] = (_mm_nt if w_t else _mm)(hn_ref[...], _w2d(w_ref)).astype(y_ref.dtype)

    return pl.pallas_call(
        body, grid=(m // tm, n_blocks),
        in_specs=[pl.BlockSpec((tm, k), lambda i, j: (i, 0)), pl.BlockSpec((1, k), lambda i, j: (0, 0)), w_spec]
        + _dep_specs(deps),
        out_specs=[out_spec, pl.BlockSpec((tm, k), lambda i, j: (i, 0))],
        out_shape=[_SDS(out_shape, out_dtype), _SDS((m, k), _ACT)],
        name=name, compiler_params=_cp("arbitrary", "arbitrary"))(x, g, w, *deps)


def _matmul_res(a, a_spec, w, w_spec, n_k, res, name):
    m, n = res.shape
    tm = _rows(m)

    def body(a_ref, w_ref, r_ref, o_ref):
        part = _mm(a_ref[...], _w2d(w_ref))

        @pl.when(pl.program_id(1) == 0)
        def _():
            o_ref[...] = r_ref[...] + part

        @pl.when(pl.program_id(1) > 0)
        def _():
            o_ref[...] += part

    return pl.pallas_call(
        body, grid=(m // tm, n_k),
        in_specs=[a_spec, w_spec, pl.BlockSpec((tm, n), lambda i, j: (i, 0))],
        out_specs=pl.BlockSpec((tm, n), lambda i, j: (i, 0)),
        out_shape=_SDS((m, n), F32), name=name, compiler_params=_cp("arbitrary", "arbitrary"))(a, w, res)


def _matmul_nt(dy, w, w_spec, n_blocks, out_shape, out_spec, name, deps=(), out_dtype=F32):
    m, n = dy.shape
    tm = _rows(m)

    def body(dy_ref, w_ref, *rest):
        o_ref = rest[-1]
        o_ref[...] = _mm_nt(dy_ref[...], _w2d(w_ref)).astype(o_ref.dtype)

    return pl.pallas_call(
        body, grid=(m // tm, n_blocks),
        in_specs=[pl.BlockSpec((tm, n), lambda i, j: (i, 0)), w_spec] + _dep_specs(deps),
        out_specs=out_spec, out_shape=_SDS(out_shape, out_dtype),
        name=name, compiler_params=_cp("arbitrary", "arbitrary"))(dy, w, *deps)


def _matmul_nt_normbwd(dy, dy_spec, w, w_spec, nj, h, g, dh_in, name, w_t=False, act_copy=False):
    m, k = h.shape
    tm = _rows(m)

    def body(dy_ref, w_ref, h_ref, g_ref, dhin_ref, dh_ref, *rest):
        dg_ref, acc_ref = rest[-2:]
        i, j = pl.program_id(0), pl.program_id(1)

        @pl.when(j == 0)
        def _():
            acc_ref[...] = jnp.zeros_like(acc_ref)

        acc_ref[...] += (_mm if w_t else _mm_nt)(dy_ref[...], _w2d(w_ref))

        @pl.when(j == nj - 1)
        def _():
            xv = h_ref[...]
            r = lax.rsqrt(jnp.mean(xv * xv, axis=-1, keepdims=True) + EPS)
            xh = xv * r
            dhn = acc_ref[...]
            part = jnp.sum(dhn * xh, axis=0, keepdims=True)

            @pl.when(i == 0)
            def _():
                dg_ref[...] = part

            @pl.when(i > 0)
            def _():
                dg_ref[...] += part

            t = dhn * g_ref[...]
            dh = dhin_ref[...] + r * (t - xh * jnp.mean(t * xh, axis=-1, keepdims=True))
            dh_ref[...] = dh
            if act_copy:
                rest[0][...] = dh.astype(_ACT)

    rows = pl.BlockSpec((tm, k), lambda i, j: (i, 0))
    outs = pl.pallas_call(
        body, grid=(m // tm, nj),
        in_specs=[dy_spec, w_spec, rows, pl.BlockSpec((1, k), lambda i, j: (0, 0)), rows],
        out_specs=[rows] + [rows] * act_copy + [pl.BlockSpec((1, k), lambda i, j: (0, 0))],
        out_shape=[_SDS((m, k), F32)] + [_SDS((m, k), _ACT)] * act_copy + [_SDS((1, k), F32)],
        scratch_shapes=[pltpu.VMEM((tm, k), F32)],
        name=name, compiler_params=_cp("arbitrary", "arbitrary"))(dy, w, h, g, dh_in)
    return outs[0], (outs[1] if act_copy else None), outs[-1]


def _matmul_tn(x, x_spec, dy, dy_spec, m, n_blocks, acc_shape, out_shape, out_spec, name, tm=None):
    tm = tm or _rows(m)
    nm = m // tm

    def body(x_ref, dy_ref, o_ref, acc_ref):
        @pl.when(pl.program_id(1) == 0)
        def _():
            acc_ref[...] = jnp.zeros_like(acc_ref)

        acc_ref[...] += _mm_tn(x_ref[...], dy_ref[...])

        @pl.when(pl.program_id(1) == nm - 1)
        def _():
            o_ref[...] = acc_ref[...].reshape(o_ref.shape).astype(o_ref.dtype)

    return pl.pallas_call(
        body, grid=(n_blocks, nm), in_specs=[x_spec, dy_spec], out_specs=out_spec,
        out_shape=_SDS(out_shape, _WIRE), scratch_shapes=[pltpu.VMEM(acc_shape, F32)],
        name=name, compiler_params=_cp("arbitrary", "arbitrary"))(x, dy)


def _loss_head(h, g, tgt, name):
    m, k = h.shape
    tm = _rows(m)

    def body(h_ref, g_ref, t_ref, loss_ref, dh_ref, dha_ref, dg_ref):
        i = pl.program_id(0)
        xv = h_ref[...]
        r = lax.rsqrt(jnp.mean(xv * xv, axis=-1, keepdims=True) + EPS)
        xh = xv * r
        gv = g_ref[...]
        err = xh * gv - t_ref[...]
        lpart = jnp.zeros((1, 128), F32) + 0.5 * jnp.sum(jnp.mean(err * err, axis=-1, keepdims=True), axis=0, keepdims=True)
        dy = err * (1.0 / k)
        gpart = jnp.sum(dy * xh, axis=0, keepdims=True)

        @pl.when(i == 0)
        def _():
            loss_ref[...] = lpart
            dg_ref[...] = gpart

        @pl.when(i > 0)
        def _():
            loss_ref[...] += lpart
            dg_ref[...] += gpart

        t = dy * gv
        dh = r * (t - xh * jnp.mean(t * xh, axis=-1, keepdims=True))
        dh_ref[...] = dh
        dha_ref[...] = dh.astype(_ACT)

    rows = pl.BlockSpec((tm, k), lambda i: (i, 0))
    return pl.pallas_call(
        body, grid=(m // tm,),
        in_specs=[rows, pl.BlockSpec((1, k), lambda i: (0, 0)), rows],
        out_specs=[pl.BlockSpec((1, 128), lambda i: (0, 0)), rows, rows, pl.BlockSpec((1, k), lambda i: (0, 0))],
        out_shape=[_SDS((1, 128), F32), _SDS((m, k), F32), _SDS((m, k), _ACT), _SDS((1, k), F32)],
        name=name, compiler_params=_cp("arbitrary"))(h, g, tgt)


def _glu_down(gu, conv_w, conv_b, w_down, res, name):
    s = gu.shape[2]
    tm = min(s, _TM_GLU)

    def body(gu_ref, prev_ref, w_ref, b_ref, wdn_ref, r_ref, o_ref, act_ref, gc_ref):
        i, j = pl.program_id(0), pl.program_id(1)
        prev = jnp.where(i > 0, prev_ref[...].astype(F32), 0.0)
        ext = jnp.concatenate([prev, gu_ref[0].astype(F32)], axis=0)
        gc = b_ref[...] + w_ref[FFN_CONV - 1:FFN_CONV, :] * ext
        for k in range(FFN_CONV - 1):
            gc = gc + w_ref[k:k + 1, :] * pltpu.roll(ext, FFN_CONV - 1 - k, 0)
        gc = gc[GLU_HALO:]
        gc_ref[...] = gc.astype(gc_ref.dtype)
        act =(_silu(gc) * gu_ref[1].astype(F32)).astype(act_ref.dtype)
        act_ref[...] = act
        part = _mm(act, _w2d(wdn_ref))

        @pl.when(j == 0)
        def _():
            o_ref[...] = r_ref[...] + part

        @pl.when(j > 0)
        def _():
            o_ref[...] += part

    return pl.pallas_call(
        body, grid=(s // tm, FF_BLOCKS),
        in_specs=[pl.BlockSpec((2, None, tm, GU_SHARD), lambda i, j: (0, j, i, 0)),
                  pl.BlockSpec((None, None, GLU_HALO, GU_SHARD),
                               lambda i, j: (0, j, jnp.maximum(i * (tm // GLU_HALO) - 1, 0), 0)),
                  pl.BlockSpec((None, HALO, GU_SHARD), lambda i, j: (j, 0, 0)),
                  pl.BlockSpec((None, 1, GU_SHARD), lambda i, j: (j, 0, 0)),
                  _spec_down(1), pl.BlockSpec((tm, D), lambda i, j: (i, 0))],
        out_specs=[pl.BlockSpec((tm, D), lambda i, j: (i, 0)), pl.BlockSpec((None, tm, GU_SHARD), lambda i, j: (j, i, 0)),
                   pl.BlockSpec((None, tm, GU_SHARD), lambda i, j: (j, i, 0))],
        out_shape=[_SDS((s, D), F32), _SDS((FF_BLOCKS, s, GU_SHARD), _ACT), _SDS((FF_BLOCKS, s, GU_SHARD), _ACT)], name=name,
        compiler_params=_cp("arbitrary", "arbitrary"))(gu, gu, conv_w, conv_b, w_down, res)


def _glu_bwd(gu, gc, conv_w, dh, w_down, name, deps=()):
    s = gu.shape[2]
    tm = min(s, _TM_GLU)
    nt = s // tm
    ext_rows = tm + GLU_HALO

    def body(gu_ref, prev_ref, gc_ref, w_ref, dh_ref, wdn_ref, *rest):
        dgu_ref, dw_ref, db_ref, carry_ref = rest[-4:]
        t = pl.program_id(1)
        i = nt - 1 - t

        @pl.when(t == 0)
        def _():
            carry_ref[...] = jnp.zeros_like(carry_ref)
            dw_ref[...] = jnp.zeros_like(dw_ref)
            db_ref[...] = jnp.zeros_like(db_ref)

        up = gu_ref[1].astype(F32)
        prev = jnp.where(i > 0, prev_ref[...].astype(F32), 0.0)
        ext = jnp.concatenate([prev, gu_ref[0].astype(F32)], axis=0)
        gc = gc_ref[...].astype(F32)
        sg = jax.nn.sigmoid(gc)
        da = _mm_nt(dh_ref[...], _w2d(wdn_ref))
        dup = da * (gc * sg)
        dgc = da * up * (sg * (1.0 + gc * (1.0 - sg)))
        db_ref[...] += jnp.sum(dgc, axis=0, keepdims=True)
        dgc_ext = jnp.concatenate([jnp.zeros((GLU_HALO, GU_SHARD), F32), dgc], axis=0)
        ahead = [pltpu.roll(dgc_ext, ext_rows - (FFN_CONV - 1 - j), 0) if j < FFN_CONV - 1 else dgc_ext
                 for j in range(FFN_CONV)]
        dext = ahead[0] * w_ref[0:1, :]
        for j in range(FFN_CONV):
            dw_ref[j:j + 1, :] += jnp.sum(ext * ahead[j], axis=0, keepdims=True)
            if j > 0:
                dext = dext + ahead[j] * w_ref[j:j + 1, :]
        tail = jnp.concatenate([jnp.zeros((tm - GLU_HALO, GU_SHARD), F32), carry_ref[...]], axis=0)
        dgate = dext[GLU_HALO:] + tail
        carry_ref[...] = dext[:GLU_HALO]
        dgu_ref[0] = dgate.astype(dgu_ref.dtype)
        dgu_ref[1] = dup.astype(dgu_ref.dtype)

    return pl.pallas_call(
        body, grid=(FF_BLOCKS, nt),
        in_specs=[pl.BlockSpec((2, None, tm, GU_SHARD), lambda j, t: (0, j, nt - 1 - t, 0)),
                  pl.BlockSpec((None, None, GLU_HALO, GU_SHARD),
                               lambda j, t: (0, j, jnp.maximum((nt - 1 - t) * (tm // GLU_HALO) - 1, 0), 0)),
                  pl.BlockSpec((None, tm, GU_SHARD), lambda j, t: (j, nt - 1 - t, 0)),
                  pl.BlockSpec((None, HALO, GU_SHARD), lambda j, t: (j, 0, 0)),
                  pl.BlockSpec((tm, D), lambda j, t: (nt - 1 - t, 0)), _spec_down(0)] + _dep_specs(deps),
        out_specs=[pl.BlockSpec((2, None, tm, GU_SHARD), lambda j, t: (0, j, nt - 1 - t, 0)),
                   pl.BlockSpec((None, HALO, GU_SHARD), lambda j, t: (j, 0, 0)),
                   pl.BlockSpec((None, 1, GU_SHARD), lambda j, t: (j, 0, 0))],
        out_shape=[_SDS(gu.shape, _ACT), _SDS((FF_BLOCKS, HALO, GU_SHARD), F32), _SDS((FF_BLOCKS, 1, GU_SHARD), F32)],
        scratch_shapes=[pltpu.VMEM((GLU_HALO, GU_SHARD), F32)],
        name=name, compiler_params=_cp("arbitrary", "arbitrary"))(gu, gu, gc, conv_w, dh, w_down, *deps)


def _bucket_table():
    qi = np.arange(BLK)[:, None]
    kj = np.arange(BLK)[None, :]
    n = np.where(kj > qi, BLK + qi - kj, qi - kj)
    max_exact = N_BUCKETS // 2
    nf = np.maximum(n, 1).astype(np.float32)
    large = max_exact + (np.log(nf / max_exact) / math.log(MAX_DISTANCE / max_exact)
                         * (N_BUCKETS - max_exact)).astype(np.int32)
    large = np.minimum(large, N_BUCKETS - 1)
    return np.where(n < max_exact, n, large).astype(np.int32)


def _lane_low():
    return lax.broadcasted_iota(jnp.int32, (1, 128), 1) < A_HD


def _swa_groups(q, kd, vd, sink, bias, upper, first):
    n = A_HEADS // A_KV_HEADS
    ng = len(q)
    low = _lane_low()
    qm = [jnp.concatenate([jnp.where(low == (h % 2 == 0), q[g][:, (h // 2) * 128:(h // 2 + 1) * 128], 0.0) for h in range(n)], axis=0)
          for g in range(ng)]
    s2 = [_mm_nt(qm[g], kd[g]) * (A_HD ** -0.5) for g in range(ng)]
    s = [jnp.where(upper[None], s2[g][:, :BLK].reshape(n, BLK, BLK), s2[g][:, BLK:].reshape(n, BLK, BLK)) + bias[g] for g in range(ng)]
    s = [t if f is None else jnp.where((upper & f)[None], -jnp.inf, t) for t, f in zip(s, first)]
    m = [lax.stop_gradient(jnp.maximum(jnp.max(s[g], axis=-1, keepdims=True), sink[g])) for g in range(ng)]
    p = [jnp.exp(s[g] - m[g]) for g in range(ng)]
    split = [jnp.concatenate([jnp.where(upper[None], t, 0.0), jnp.where(upper[None], 0.0, t)], axis=-1).reshape(n * BLK, 2 * BLK)
             for t in p]
    ones = jnp.ones((BLK, 128), F32)
    den = [_mm(p[g].reshape(n * BLK, BLK), ones) + jnp.exp(sink[g] - m[g]).reshape(n * BLK, 1) for g in range(ng)]
    o = [_mm(split[g], vd[g]) / den[g] for g in range(ng)]
    return [jnp.concatenate([jnp.where(low, t[2 * k * BLK:(2 * k + 1) * BLK], t[(2 * k + 1) * BLK:(2 * k + 2) * BLK])
                             for k in range(n // 2)], axis=1) for t in o]


def _mix_a_core(q, kd, vd, sink, bias, xq, mk, mv, upper, first):
    return _swa_groups(q, kd, vd, sink, bias, upper, first), _cross_pairs(xq, mk, mv)


def _swa_sinks(sink_ref, g):
    n = A_HEADS // A_KV_HEADS
    return jnp.concatenate([sink_ref[:, h:h + 1] for h in range(g * n, (g + 1) * n)], axis=0).reshape(n, 1, 1)


def _both_halves(t, t_rolled, g):
    low = _lane_low()
    return jnp.where(low, t, t_rolled) if g == 0 else jnp.where(low, t_rolled, t)


def _cross_pairs(q, mk, mv):
    rows = q.shape[0]
    low = _lane_low()
    qm = [jnp.concatenate([jnp.where(low, q[:, p * 128:(p + 1) * 128], 0.0), jnp.where(low, 0.0, q[:, p * 128:(p + 1) * 128])], axis=0)
          for p in range(X_HEADS // 2)]
    s = [_mm_nt(qm[p], mk[:, p * 128:(p + 1) * 128]) * (X_HD ** -0.5) for p in range(X_HEADS // 2)]
    e = [jnp.exp(t - lax.stop_gradient(jnp.max(t, axis=-1, keepdims=True))) for t in s]
    pr = [t / jnp.sum(t, axis=-1, keepdims=True) for t in e]
    o = [_mm(pr[p], mv[:, p * 128:(p + 1) * 128]) for p in range(X_HEADS // 2)]
    return jnp.concatenate([jnp.where(low, t[:rows], t[rows:]) for t in o], axis=1)


def _swa_upper():
    qi = lax.broadcasted_iota(jnp.int32, (BLK, BLK), 0)
    kj = lax.broadcasted_iota(jnp.int32, (BLK, BLK), 1)
    return kj > qi


def _bias_build(rel_bias, bucket, name):
    def body(rb_ref, bucket_ref, o_ref):
        b = bucket_ref[...]
        for h in range(A_HEADS):
            acc = jnp.zeros((BLK, BLK), F32)
            for k in range(N_BUCKETS):
                acc = jnp.where(b == k, rb_ref[k, h], acc)
            o_ref[h] = acc

    return pl.pallas_call(
        body, in_specs=[pl.BlockSpec(memory_space=pltpu.SMEM), pl.BlockSpec(memory_space=pltpu.VMEM)],
        out_specs=pl.BlockSpec(memory_space=pltpu.VMEM),
        out_shape=_SDS((A_HEADS, BLK, BLK), F32), name=name)(rel_bias, bucket)


def _bias_reduce(dbias, bucket, name):
    def body(db_ref, bucket_ref, o_ref):
        b = bucket_ref[...]
        row = lax.broadcasted_iota(jnp.int32, (N_BUCKETS, 128), 0)
        lane = lax.broadcasted_iota(jnp.int32, (N_BUCKETS, 128), 1)
        acc = jnp.zeros((N_BUCKETS, 128), F32)
        for h in range(A_HEADS):
            v = db_ref[h]
            for k in range(N_BUCKETS):
                sk = jnp.sum(jnp.sum(jnp.where(b == k, v, 0.0), axis=1, keepdims=True), axis=0, keepdims=True)
                acc = acc + jnp.where((row == k) & (lane == h), sk, 0.0)
        o_ref[...] = acc

    return pl.pallas_call(
        body, in_specs=[pl.BlockSpec(memory_space=pltpu.VMEM)] * 2,
        out_specs=pl.BlockSpec(memory_space=pltpu.VMEM),
        out_shape=_SDS((N_BUCKETS, 128), F32), name=name)(dbias, bucket)


def _mix_a_fwd(proj, bias, sinks, memkv, name):
    s = proj.shape[0]
    per = 4
    nb = s // (per * BLK)
    grp = A_HEADS // A_KV_HEADS

    def body(proj_ref, prev_ref, bias_ref, sink_ref, memkv_ref, o_ref):
        i = pl.program_id(0)
        upper = _swa_upper()
        proj = proj_ref[...].astype(F32)
        kv = jnp.concatenate([prev_ref[...].astype(F32), proj[:, A_Q:A_Q + 2 * A_KV]], axis=0)
        k, v = kv[:, :A_KV], kv[:, A_KV:]
        k_r = pltpu.roll(k, A_HD, 1)
        v_r = pltpu.roll(v, A_HD, 1)
        gw = A_Q // A_KV_HEADS
        each = [(b, g) for b in range(per) for g in range(A_KV_HEADS)]

        def window(a, a_r, b, g):
            return _both_halves(a[b * BLK:(b + 2) * BLK], a_r[b * BLK:(b + 2) * BLK], g)

        swa, cross = _mix_a_core([proj[b * BLK:(b + 1) * BLK, g * gw:(g + 1) * gw] for b, g in each],
                                 [window(k, k_r, b, g) for b, g in each], [window(v, v_r, b, g) for b, g in each],
                                 [_swa_sinks(sink_ref, g) for b, g in each], [bias_ref[g * grp:(g + 1) * grp] for b, g in each],
                                 proj[:, A_Q + 2 * A_KV:], memkv_ref[:, :X_Q], memkv_ref[:, X_Q:], upper,
                                 [(i == 0) if b == 0 else None for b, g in each])
        for b in range(per):
            o_ref[b * BLK:(b + 1) * BLK, :] = jnp.concatenate(
                swa[b * A_KV_HEADS:(b + 1) * A_KV_HEADS] + [cross[b * BLK:(b + 1) * BLK]], axis=1).astype(o_ref.dtype)

    return pl.pallas_call(
        body, grid=(nb,),
        in_specs=[pl.BlockSpec((per * BLK, IN_A), lambda i: (i, 0)),
                  pl.BlockSpec((BLK, 2 * A_KV), lambda i: (jnp.maximum(per * i - 1, 0), A_Q // (2 * A_KV))),
                  pl.BlockSpec((A_HEADS, BLK, BLK), lambda i: (0, 0, 0)),
                  pl.BlockSpec((1, 128), lambda i: (0, 0)),
                  pl.BlockSpec((MEM_LEN, 2 * X_Q), lambda i: (0, 0))],
        out_specs=pl.BlockSpec((per * BLK, D), lambda i: (i, 0)),
        out_shape=_SDS((s, D), _ACT), name=name, compiler_params=_cp("arbitrary"))(proj, proj, bias, sinks, memkv)


def _mix_a_bwd(proj, bias, sinks, memkv, dmix, name, deps=()):
    s = proj.shape[0]
    per = 2
    nb = s // (per * BLK)
    grp = A_HEADS // A_KV_HEADS

    def body(proj_ref, prev_ref, bias_ref, sink_ref, memkv_ref, dmix_ref, *rest):
        dproj_ref, dbias_ref, dsink_ref, dmemkv_ref, carry_ref = rest[-5:]
        t = pl.program_id(0)
        i = nb - 1 - t

        @pl.when(t == 0)
        def _():
            carry_ref[...] = jnp.zeros_like(carry_ref)
            dbias_ref[...] = jnp.zeros_like(dbias_ref)
            dsink_ref[...] = jnp.zeros_like(dsink_ref)
            dmemkv_ref[...] = jnp.zeros_like(dmemkv_ref)

        upper = _swa_upper()
        lane = lax.broadcasted_iota(jnp.int32, (1, 128), 1)
        low = _lane_low()
        proj = proj_ref[...].astype(F32)
        kv = jnp.concatenate([prev_ref[...].astype(F32), proj[:, A_Q:A_Q + 2 * A_KV]], axis=0)
        k, v = kv[:, :A_KV], kv[:, A_KV:]
        k_r = pltpu.roll(k, A_HD, 1)
        v_r = pltpu.roll(v, A_HD, 1)
        gw = A_Q // A_KV_HEADS
        each = [(b, g) for b in range(per) for g in range(A_KV_HEADS)]

        def window(a, a_r, b, g):
            return _both_halves(a[b * BLK:(b + 2) * BLK], a_r[b * BLK:(b + 2) * BLK], g)

        _, vjp = jax.vjp(
            functools.partial(_mix_a_core, upper=upper, first=[(i == 0) if b == 0 else None for b, g in each]),
            [proj[b * BLK:(b + 1) * BLK, g * gw:(g + 1) * gw] for b, g in each],
            [window(k, k_r, b, g) for b, g in each], [window(v, v_r, b, g) for b, g in each],
            [_swa_sinks(sink_ref, g) for b, g in each], [bias_ref[g * grp:(g + 1) * grp] for b, g in each],
            proj[:, A_Q + 2 * A_KV:], memkv_ref[:, :X_Q], memkv_ref[:, X_Q:])
        dqs, dk, dv, ds, db, dxq, dmk, dmv = vjp(
            ([dmix_ref[b * BLK:(b + 1) * BLK, g * gw:(g + 1) * gw].astype(F32) for b, g in each], dmix_ref[:, A_Q:].astype(F32)))
        dkd = [t + pltpu.roll(t, A_HD, 1) for t in dk]
        dvd = [t + pltpu.roll(t, A_HD, 1) for t in dv]
        dsink = jnp.zeros((1, 128), F32)
        for e, (b, g) in enumerate(each):
            for h in range(grp):
                dsink = dsink + jnp.where(lane == g * grp + h, ds[e][h], 0.0)
        for g in range(A_KV_HEADS):
            dbias_ref[g * grp:(g + 1) * grp] += db[g] + db[A_KV_HEADS + g]
        dsink_ref[...] += dsink
        dmemkv_ref[...] += jnp.concatenate([dmk, dmv], axis=1)
        dkv = [jnp.concatenate([jnp.where(low, dkd[b * A_KV_HEADS], dkd[b * A_KV_HEADS + 1]),
                                jnp.where(low, dvd[b * A_KV_HEADS], dvd[b * A_KV_HEADS + 1])], axis=1) for b in range(per)]
        own = [dkv[0][BLK:] + dkv[1][:BLK], dkv[1][BLK:] + carry_ref[...]]
        carry_ref[...] = dkv[0][:BLK]
        for b in range(per):
            dproj_ref[b * BLK:(b + 1) * BLK, :] = jnp.concatenate(
                list(dqs[b * A_KV_HEADS:(b + 1) * A_KV_HEADS]) + [own[b], dxq[b * BLK:(b + 1) * BLK]], axis=1).astype(dproj_ref.dtype)

    return pl.pallas_call(
        body, grid=(nb,),
        in_specs=[pl.BlockSpec((per * BLK, IN_A), lambda t: (nb - 1 - t, 0)),
                  pl.BlockSpec((BLK, 2 * A_KV), lambda t: (jnp.maximum(per * (nb - 1 - t) - 1, 0), A_Q // (2 * A_KV))),
                  pl.BlockSpec((A_HEADS, BLK, BLK), lambda t: (0, 0, 0)),
                  pl.BlockSpec((1, 128), lambda t: (0, 0)),
                  pl.BlockSpec((MEM_LEN, 2 * X_Q), lambda t: (0, 0)),
                  pl.BlockSpec((per * BLK, D), lambda t: (nb - 1 - t, 0))] + _dep_specs(deps),
        out_specs=[pl.BlockSpec((per * BLK, IN_A), lambda t: (nb - 1 - t, 0)),
                   pl.BlockSpec((A_HEADS, BLK, BLK), lambda t: (0, 0, 0)),
                   pl.BlockSpec((1, 128), lambda t: (0, 0)),
                   pl.BlockSpec((MEM_LEN, 2 * X_Q), lambda t: (0, 0))],
        out_shape=[_SDS((s, IN_A), _ACT), _SDS((A_HEADS, BLK, BLK), F32), _SDS((1, 128), F32),
                   _SDS((MEM_LEN, 2 * X_Q), F32)],
        scratch_shapes=[pltpu.VMEM((BLK, 2 * A_KV), F32)],
        name=name, compiler_params=_cp("arbitrary"))(proj, proj, bias, sinks, memkv, dmix, *deps)


def _neumann(pw, rhs):
    nh = len(pw)
    x = rhs
    for lvl in range(6):
        if lvl < 5:
            prod = [_mmf(pw[h], jnp.concatenate([x[h], pw[h]], axis=1)) for h in range(nh)]
            x = [x[h] + prod[h][:, :B_HD] for h in range(nh)]
            pw = [t[:, B_HD:] for t in prod]
        else:
            x = [x[h] + _mmf(pw[h], x[h]) for h in range(nh)]
    return x


@jax.custom_vjp
def _tri_solve(pw, rhs):
    return _neumann(pw, rhs)


def _tri_solve_fwd(pw, rhs):
    x = _neumann(pw, rhs)
    return x, (pw, x)


def _tri_solve_bwd(res, dx):
    pw, x = res
    d_rhs = _neumann([t.T for t in pw], list(dx))
    return [_mmf_nt(d_rhs[h], x[h]) for h in range(len(pw))], d_rhs


_tri_solve.defvjp(_tri_solve_fwd, _tri_solve_bwd)


@jax.custom_vjp
def _tri_solved(pw, rhs, x):
    return x


def _tri_solved_fwd(pw, rhs, x):
    return x, (pw, x)


def _tri_solved_bwd(res, dx):
    d_pw, d_rhs = _tri_solve_bwd(res, dx)
    return d_pw, d_rhs, [jnp.zeros_like(t) for t in res[1]]


_tri_solved.defvjp(_tri_solved_fwd, _tri_solved_bwd)


@jax.custom_vjp
def _known(x, value):
    return value


def _known_fwd(x, value):
    return value, None


def _known_bwd(_, g):
    return g, jnp.zeros_like(g)


_known.defvjp(_known_fwd, _known_bwd)


def _dn_heads(yq, yk, yv, z, bl, al, a_log, dtb, ng, s0, solved=None, out_known=None):
    c = CHUNK
    nh = B_V_HEADS
    rep = B_V_HEADS // B_QK_HEADS
    r = lax.broadcasted_iota(jnp.int32, (c, c), 0)
    cc = lax.broadcasted_iota(jnp.int32, (c, c), 1)
    q = [_silu(t) for t in yq]
    k = [_silu(t) for t in yk]
    v = [_silu(t) for t in yv]
    q = [t * lax.rsqrt(jnp.sum(t * t, axis=-1, keepdims=True) + EPS) * (B_HD ** -0.5) for t in q]
    k = [t * lax.rsqrt(jnp.sum(t * t, axis=-1, keepdims=True) + EPS) for t in k]
    beta = [jax.nn.sigmoid(t) for t in bl]
    g = [-jnp.exp(a_log[h]) * jax.nn.softplus(al[h] + dtb[h]) for h in range(nh)]
    gb = [jnp.broadcast_to(t, (c, c)) for t in g]
    gc_col = [jnp.sum(jnp.where(cc <= r, t.T, 0.0), axis=1, keepdims=True) for t in gb]
    gc_row = [jnp.sum(jnp.where(r <= cc, t, 0.0), axis=0, keepdims=True) for t in gb]
    gc_last = [jnp.sum(t, axis=0, keepdims=True) for t in g]
    decay = [jnp.exp(jnp.where(r >= cc, gc_col[h] - gc_row[h], -jnp.inf)) for h in range(nh)]
    kq = [_mmf_nt(jnp.concatenate([k[h], q[h]], axis=0), k[h]) for h in range(B_QK_HEADS)]
    kk = [t[:c] for t in kq]
    qk = [t[c:] for t in kq]
    egc = [jnp.exp(t) for t in gc_col]
    both = [_mmf(jnp.concatenate([(beta[h] * egc[h]) * k[h // rep], q[h // rep] * egc[h]], axis=0), s0[h]) for h in range(nh)]
    rhs = [beta[h] * v[h] - both[h][:c] for h in range(nh)]
    qs0 = [t[c:] for t in both]
    pw = [-(beta[h] * kk[h // rep] * jnp.where(r > cc, decay[h], 0.0)) for h in range(nh)]
    delta = _tri_solve(pw, rhs) if solved is None else _tri_solved(pw, rhs, solved)
    last = [_mmf(jnp.concatenate([qk[h // rep] * decay[h], (k[h // rep] * jnp.exp(gc_last[h] - gc_col[h])).T], axis=0), delta[h])
            for h in range(nh)]
    out = [qs0[h] + last[h][:c] for h in range(nh)]
    if out_known is not None:
        out = [_known(out[h], out_known[h]) for h in range(nh)]
    s1 = [jnp.exp(gc_last[h]) * s0[h] + last[h][c:] for h in range(nh)]
    o = [t * lax.rsqrt(jnp.mean(t * t, axis=-1, keepdims=True) + EPS) * ng for t in out]
    return [o[h] * _silu(z[h]) for h in range(nh)], s1, delta, out


def _dn_conv(ext, w_ref):
    y = ext * w_ref[B_CONV - 1:B_CONV, :]
    for j in range(B_CONV - 1):
        y = y + w_ref[j:j + 1, :] * pltpu.roll(ext, B_CONV - 1 - j, 0)
    return y


def _dn_args(y, cur_ref, par_ref, ng_ref):
    nh = B_V_HEADS
    return ([y[:, h * B_HD:(h + 1) * B_HD] for h in range(B_QK_HEADS)],
            [y[:, B_QK + h * B_HD:B_QK + (h + 1) * B_HD] for h in range(B_QK_HEADS)],
            [y[:, 2 * B_QK + h * B_HD:2 * B_QK + (h + 1) * B_HD] for h in range(nh)],
            [cur_ref[:, BP_Z + h * B_HD:BP_Z + (h + 1) * B_HD] for h in range(nh)],
            [cur_ref[:, BP_GATE + h:BP_GATE + h + 1] for h in range(nh)],
            [cur_ref[:, BP_GATE + nh + h:BP_GATE + nh + h + 1] for h in range(nh)],
            [par_ref[:, h:h + 1] for h in range(nh)], [par_ref[:, nh + h:nh + h + 1] for h in range(nh)], ng_ref[...])


def _mix_b_fwd(proj, conv_w, par, ng, memkv, name):
    s = proj.shape[0]
    nc = s // CHUNK

    def body(cur_ref, prev_ref, w_ref, par_ref, ng_ref, memkv_ref, o_ref, st_ref, dl_ref, state_ref):
        n = pl.program_id(0)

        @pl.when(n == 0)
        def _():
            state_ref[...] = jnp.zeros_like(state_ref)

        prev = jnp.where(n > 0, prev_ref[...], 0.0)
        ext = jnp.concatenate([prev, cur_ref[:, :B_QKV]], axis=0)
        y = _dn_conv(ext, w_ref)[HALO:]
        s0 = [state_ref[hv] for hv in range(B_V_HEADS)]
        st_ref[0] = state_ref[...]
        outs, s1, delta, raw = _dn_heads(*_dn_args(y, cur_ref, par_ref, ng_ref), s0)
        for hv in range(B_V_HEADS):
            state_ref[hv] = s1[hv]
            dl_ref[0, hv] = delta[hv]
            dl_ref[0, B_V_HEADS + hv] = raw[hv]
        outs = outs + [_cross_pairs(cur_ref[:, BP_XQ:BP_XQ + X_Q], memkv_ref[:, :X_Q], memkv_ref[:, X_Q:])]
        o_ref[...] = jnp.concatenate(outs, axis=1).astype(o_ref.dtype)

    return pl.pallas_call(
        body, grid=(nc,),
        in_specs=[pl.BlockSpec((CHUNK, IN_BP), lambda n: (n, 0)),
                  pl.BlockSpec((HALO, B_QKV), lambda n: (jnp.maximum(n * (CHUNK // HALO) - 1, 0), 0)),
                  pl.BlockSpec((HALO, B_QKV), lambda n: (0, 0)),
                  pl.BlockSpec((1, 128), lambda n: (0, 0)), pl.BlockSpec((1, 128), lambda n: (0, 0)),
                  pl.BlockSpec((MEM_LEN, 2 * X_Q), lambda n: (0, 0))],
        out_specs=[pl.BlockSpec((CHUNK, D), lambda n: (n, 0)),
                   pl.BlockSpec((1, B_V_HEADS, B_HD, B_HD), lambda n: (n, 0, 0, 0)),
                   pl.BlockSpec((1, 2 * B_V_HEADS, CHUNK, B_HD), lambda n: (n, 0, 0, 0))],
        out_shape=[_SDS((s, D), _ACT), _SDS((nc, B_V_HEADS, B_HD, B_HD), F32), _SDS((nc, 2 * B_V_HEADS, CHUNK, B_HD), F32)],
        scratch_shapes=[pltpu.VMEM((B_V_HEADS, B_HD, B_HD), F32)],
        name=name, compiler_params=_cp("arbitrary"))(proj, proj, conv_w, par, ng, memkv)


def _mix_b_bwd(proj, conv_w, par, ng, memkv, states, deltas, dmix, name):
    s = proj.shape[0]
    nc = s // CHUNK
    ext_rows = CHUNK + HALO

    def body(cur_ref, prev_ref, w_ref, par_ref, ng_ref, memkv_ref, st_ref, dl_ref, dmix_ref,
             dproj_ref, dw_ref, dpar_ref, dng_ref, dmemkv_ref, dstate_ref, carry_ref):
        t = pl.program_id(0)
        n = nc - 1 - t

        @pl.when(t == 0)
        def _():
            dstate_ref[...] = jnp.zeros_like(dstate_ref)
            carry_ref[...] = jnp.zeros_like(carry_ref)
            dw_ref[...] = jnp.zeros_like(dw_ref)
            dpar_ref[...] = jnp.zeros_like(dpar_ref)
            dng_ref[...] = jnp.zeros_like(dng_ref)
            dmemkv_ref[...] = jnp.zeros_like(dmemkv_ref)

        lane = lax.broadcasted_iota(jnp.int32, (1, 128), 1)
        prev = jnp.where(n > 0, prev_ref[...], 0.0)
        ext = jnp.concatenate([prev, cur_ref[:, :B_QKV]], axis=0)
        y = _dn_conv(ext, w_ref)[HALO:]
        solved = [dl_ref[0, hv] for hv in range(B_V_HEADS)]
        raw = [dl_ref[0, B_V_HEADS + hv] for hv in range(B_V_HEADS)]
        _, vjp = jax.vjp(functools.partial(_dn_heads, solved=solved, out_known=raw), *_dn_args(y, cur_ref, par_ref, ng_ref),
                         [st_ref[0, hv] for hv in range(B_V_HEADS)])
        none = [jnp.zeros((CHUNK, B_HD), F32)] * B_V_HEADS
        dyq, dyk, dyv, dz, gbl, gal, ga_log, gdtb, dng, gs0 = vjp(
            ([dmix_ref[:, hv * B_HD:(hv + 1) * B_HD].astype(F32) for hv in range(B_V_HEADS)],
             [dstate_ref[hv] for hv in range(B_V_HEADS)], none, none))
        dgate = jnp.zeros((CHUNK, 128), F32)
        dpar = jnp.zeros((1, 128), F32)
        for hv in range(B_V_HEADS):
            dstate_ref[hv] = gs0[hv]
            dgate = dgate + jnp.where(lane == hv, gbl[hv], 0.0) + jnp.where(lane == B_V_HEADS + hv, gal[hv], 0.0)
            dpar = dpar + jnp.where(lane == hv, ga_log[hv], 0.0) + jnp.where(lane == B_V_HEADS + hv, gdtb[hv], 0.0)
        dpar_ref[...] += dpar
        dng_ref[...] += dng
        _, vjp = jax.vjp(_cross_pairs, cur_ref[:, BP_XQ:BP_XQ + X_Q], memkv_ref[:, :X_Q], memkv_ref[:, X_Q:])
        dxq, dmk, dmv = vjp(dmix_ref[:, B_V:].astype(F32))
        dmemkv_ref[...] += jnp.concatenate([dmk, dmv], axis=1)
        dy = jnp.concatenate(list(dyq) + list(dyk) + list(dyv), axis=1)
        dy_ext = jnp.concatenate([jnp.zeros((HALO, B_QKV), F32), dy], axis=0)
        dext = dy_ext * w_ref[B_CONV - 1:B_CONV, :]
        dw_ref[B_CONV - 1:B_CONV, :] += jnp.sum(ext * dy_ext, axis=0, keepdims=True)
        for j in range(B_CONV - 1):
            sh = B_CONV - 1 - j
            dw_ref[j:j + 1, :] += jnp.sum(pltpu.roll(ext, sh, 0) * dy_ext, axis=0, keepdims=True)
            dext = dext + w_ref[j:j + 1, :] * pltpu.roll(dy_ext, ext_rows - sh, 0)
        tail = jnp.concatenate([jnp.zeros((CHUNK - HALO, B_QKV), F32), carry_ref[...]], axis=0)
        dqkv = dext[HALO:] + tail
        carry_ref[...] = dext[:HALO]
        dproj_ref[...] = jnp.concatenate([dqkv] + list(dz) + [dxq, dgate], axis=1).astype(dproj_ref.dtype)

    return pl.pallas_call(
        body, grid=(nc,),
        in_specs=[pl.BlockSpec((CHUNK, IN_BP), lambda t: (nc - 1 - t, 0)),
                  pl.BlockSpec((HALO, B_QKV), lambda t: (jnp.maximum((nc - 1 - t) * (CHUNK // HALO) - 1, 0), 0)),
                  pl.BlockSpec((HALO, B_QKV), lambda t: (0, 0)),
                  pl.BlockSpec((1, 128), lambda t: (0, 0)), pl.BlockSpec((1, 128), lambda t: (0, 0)),
                  pl.BlockSpec((MEM_LEN, 2 * X_Q), lambda t: (0, 0)),
                  pl.BlockSpec((1, B_V_HEADS, B_HD, B_HD), lambda t: (nc - 1 - t, 0, 0, 0)),
                  pl.BlockSpec((1, 2 * B_V_HEADS, CHUNK, B_HD), lambda t: (nc - 1 - t, 0, 0, 0)),
                  pl.BlockSpec((CHUNK, D), lambda t: (nc - 1 - t, 0))],
        out_specs=[pl.BlockSpec((CHUNK, IN_BP), lambda t: (nc - 1 - t, 0)),
                   pl.BlockSpec((HALO, B_QKV), lambda t: (0, 0)),
                   pl.BlockSpec((1, 128), lambda t: (0, 0)), pl.BlockSpec((1, 128), lambda t: (0, 0)),
                   pl.BlockSpec((MEM_LEN, 2 * X_Q), lambda t: (0, 0))],
        out_shape=[_SDS((s, IN_BP), _ACT), _SDS((HALO, B_QKV), F32), _SDS((1, 128), F32), _SDS((1, 128), F32),
                   _SDS((MEM_LEN, 2 * X_Q), F32)],
        scratch_shapes=[pltpu.VMEM((B_V_HEADS, B_HD, B_HD), F32), pltpu.VMEM((HALO, B_QKV), F32)],
        name=name, compiler_params=_cp("arbitrary"))(proj, proj, conv_w, par, ng, memkv, states, deltas, dmix)


def _place():
    return lax.axis_index("x"), lax.axis_index("y"), lax.axis_index("c")


def _two_level_gather(ins, outs, sems, local_sems):
    n = len(ins)
    x, y, c = _place()
    me, sibling = (x, y, c), (x, y, 1 - c)
    chips = [(1 - x, y), (x, 1 - y), (1 - x, 1 - y)]

    def rows(a, px, py, pc):
        return outs[a].at[4 * px + 2 * py + pc]

    def copy(a, k, block, to, src=None):
        return pltpu.make_async_remote_copy(
            src_ref=rows(a, *block) if src is None else src, dst_ref=rows(a, *block),
            device_id=to, device_id_type=pl.DeviceIdType.MESH, **sems(a, k))

    mine = [pltpu.make_async_copy(ins[a], rows(a, *me), local_sems.at[a]) for a in range(n)]
    for cp in mine:
        cp.start()
    first = []
    for a in range(n):
        first.append(copy(a, 0, me, sibling, src=ins[a]))
        first += [copy(a, 1 + j, me, (*chip, c), src=ins[a]) for j, chip in enumerate(chips)]
    for cp in first:
        cp.start()
    passed = []
    for j, chip in enumerate(chips):
        for a in range(n):
            copy(a, 1 + j, (*chip, c), me).wait_recv()
            fwd = copy(a, 4 + j, (*chip, c), sibling)
            fwd.start()
            passed.append(fwd)
    for a in range(n):
        copy(a, 0, sibling, me).wait_recv()
        for j, chip in enumerate(chips):
            copy(a, 4 + j, (*chip, 1 - c), me).wait_recv()
    for cp in first + passed:
        cp.wait_send()
    for cp in mine:
        cp.wait()


def _all_gather(shards, name):
    n = len(shards)

    def body(*refs):
        send_sems, recv_sems, local_sems = refs[2 * n:]
        _two_level_gather(refs[:n], refs[n:2 * n], lambda a, k: dict(send_sem=send_sems.at[a, k], recv_sem=recv_sems.at[a, k]),
                          local_sems)

    hbm = pl.BlockSpec(memory_space=pl.ANY)
    return pl.pallas_call(
        body, out_shape=[_SDS((N_DEV,) + s.shape, s.dtype) for s in shards],
        in_specs=[hbm] * n, out_specs=[hbm] * n,
        scratch_shapes=[pltpu.SemaphoreType.DMA((n, 7)), pltpu.SemaphoreType.DMA((n, 7)), pltpu.SemaphoreType.DMA((n,))],
        name=name)(*shards)


class _Exchange:
    def __init__(self, lands, srcs):
        self.lands, self.srcs = lands, srcs


def _seq_exchange(srcs, land_shapes, plan, name, cid):
    n, nl = len(srcs), len(land_shapes)

    def launch(*refs):
        src_refs, land_refs = refs[:n], refs[n:n + nl]
        send_sems, recv_sems, local_sems = refs[n + nl:]
        x, y, c = _place()
        my = 4 * x + 2 * y + c
        peers = [(x ^ ((k + 1) >> 2 & 1), y ^ ((k + 1) >> 1 & 1), c ^ ((k + 1) & 1)) for k in range(N_DEV - 1)]
        barrier = pltpu.get_barrier_semaphore()
        for p in peers:
            pl.semaphore_signal(barrier, inc=1, device_id=p, device_id_type=pl.DeviceIdType.MESH)
        pl.semaphore_wait(barrier, N_DEV - 1)

        def src_for(a, dest):
            return src_refs[a].at[dest] if plan[a][1] else src_refs[a]

        def slot(a, source):
            return land_refs[plan[a][0]].at[source]

        mine = [pltpu.make_async_copy(src_for(a, my), slot(a, my), local_sems.at[a]) for a in range(n)]
        for cp in mine:
            cp.start()
        sends, recvs = [], []
        for k, (px, py, pc) in enumerate(peers):
            peer = 4 * px + 2 * py + pc
            for a in range(n):
                kw = dict(send_sem=send_sems.at[a * (N_DEV - 1) + k], recv_sem=recv_sems.at[a * (N_DEV - 1) + k],
                          device_id=(px, py, pc), device_id_type=pl.DeviceIdType.MESH)
                sends.append(pltpu.make_async_remote_copy(src_ref=src_for(a, peer), dst_ref=slot(a, my), **kw))
                recvs.append(pltpu.make_async_remote_copy(src_ref=src_for(a, my), dst_ref=slot(a, peer), **kw))
        for cp in sends:
            cp.start()
        for cp in recvs:
            cp.wait_recv()
        for cp in sends:
            cp.wait_send()
        for cp in mine:
            cp.wait()

    lands = pl.kernel(
        launch, out_type=[_SDS(s, d) for s, d in land_shapes],
        mesh=plsc.ScalarSubcoreMesh(axis_name="sequencer", num_cores=1), name=name,
        scratch_types=(pltpu.SemaphoreType.DMA((n * (N_DEV - 1),)), pltpu.SemaphoreType.DMA((n * (N_DEV - 1),)),
                       pltpu.SemaphoreType.DMA((n,))),
        compiler_params=pltpu.CompilerParams(collective_id=cid))(*srcs)
    return _Exchange(list(lands), list(srcs))


def _seq_gather(srcs, name, cid):
    n = len(srcs)

    def launch(*refs):
        send_sems, recv_sems, local_sems = refs[2 * n:]
        x, y, c = _place()
        barrier = pltpu.get_barrier_semaphore()
        for k in range(1, N_DEV):
            pl.semaphore_signal(barrier, inc=1, device_id=(x ^ (k >> 2 & 1), y ^ (k >> 1 & 1), c ^ (k & 1)),
                                device_id_type=pl.DeviceIdType.MESH)
        pl.semaphore_wait(barrier, N_DEV - 1)
        at = lambda a, k: a * (N_DEV - 1) + k
        _two_level_gather(refs[:n], refs[n:2 * n], lambda a, k: dict(send_sem=send_sems.at[at(a, k)], recv_sem=recv_sems.at[at(a, k)]),
                          local_sems)

    lands = pl.kernel(
        launch, out_type=[_SDS((N_DEV,) + s.shape, s.dtype) for s in srcs],
        mesh=plsc.ScalarSubcoreMesh(axis_name="sequencer", num_cores=1), name=name,
        scratch_types=(pltpu.SemaphoreType.DMA((n * (N_DEV - 1),)), pltpu.SemaphoreType.DMA((n * (N_DEV - 1),)),
                       pltpu.SemaphoreType.DMA((n,))),
        compiler_params=pltpu.CompilerParams(collective_id=cid))(*srcs)
    return _Exchange(list(lands), list(srcs))


def _adam_update(g, w, m, v):
    c1 = 1.0 - ADAM_B1 ** ADAM_STEP
    c2 = 1.0 - ADAM_B2 ** ADAM_STEP
    mm = ADAM_B1 * m + (1.0 - ADAM_B1) * g
    vv = ADAM_B2 * v + (1.0 - ADAM_B2) * (g * g)
    delta = -ADAM_LR * ((mm / c1) / (jnp.sqrt(vv / c2) + ADAM_EPS) + ADAM_WD * w)
    return delta, mm, vv


def _sum_sources(p_ref):
    g = p_ref[0].astype(F32)
    for s in range(1, N_DEV):
        g = g + p_ref[s].astype(F32)
    return g


def _adamw(parts, w, m, v, tr, name, restore_b=False, deps=()):
    nl, r, c = w.shape
    cp = parts[0].shape[-1]

    def body(*refs):
        p_refs = refs[:nl]
        w_ref, m_ref, v_ref = refs[nl:nl + 3]
        g_ref, d_ref, nm_ref, nv_ref = refs[-4:]
        g = _sum_sources(p_refs[0])
        for l in range(1, nl):
            g = jnp.where(pl.program_id(0) == l, _sum_sources(p_refs[l]), g)
        if restore_b:
            g = jnp.concatenate([g[:, :BP_XQ], g[:, BP_GATE:BP_GATE + 2 * B_V_HEADS], g[:, BP_XQ:BP_GATE]], axis=1)
        delta, mm, vv = _adam_update(g, w_ref[...], m_ref[...], v_ref[...])
        g_ref[...] = g
        d_ref[...] = delta
        nm_ref[...] = mm
        nv_ref[...] = vv

    spec = pl.BlockSpec((None, tr, c), lambda l, i: (l, i, 0))
    part_specs = [pl.BlockSpec((N_DEV, tr, cp), functools.partial(lambda l, i, k: (0, jnp.where(l == k, i, 0), 0), k=k))
                  for k in range(nl)]
    return pl.pallas_call(
        body, grid=(nl, r // tr),
        in_specs=part_specs + [spec, spec, spec] + _dep_specs(deps),
        out_specs=[spec] * 4, out_shape=[_SDS(w.shape, F32)] * 4,
        name=name, compiler_params=_cp("arbitrary", "arbitrary"))(*parts, w, m, v, *deps)


def _pack_small(d_rel, d_cb, d_cw, d_qkv, d_mix, d_mem, d_ffn, d_final, d_sinks, d_par, d_ng, loss_row, name):
    flat = [d_rel, *d_cb, *d_cw, d_qkv, *d_mix, *d_mem, *d_ffn, d_final, d_sinks, d_par, d_ng, loss_row]
    n = len(flat)

    def body(*refs):
        ins, o_ref = refs[:n], refs[n]
        rel, cb0, cb1, cw0, cw1, qkv, mx0, mx1, me0, me1, ff0, ff1, fin, snk, par, ng, lss = ins
        o_ref[...] = jnp.zeros_like(o_ref)
        for k in range(N_BUCKETS):
            lane = SP_REL_LANE + 128 * (k % 8)
            o_ref[SP_QKV + k // 8:SP_QKV + k // 8 + 1, lane:lane + 128] = rel[k:k + 1, :]
        for l, (cb, cw) in enumerate(((cb0, cw0), (cb1, cw1))):
            o_ref[SP_CB + l:SP_CB + l + 1, :] = jnp.concatenate([cb[j] for j in range(FF_BLOCKS)], axis=1)
            full = jnp.concatenate([cw[j] for j in range(FF_BLOCKS)], axis=1)
            o_ref[SP_CW + FFN_CONV * l:SP_CW + FFN_CONV * (l + 1), :] = full[:FFN_CONV]
        o_ref[SP_QKV:SP_QKV + B_CONV, 0:B_QKV] = qkv[0:B_CONV, :]
        for base, pair in ((SP_MIX, (mx0, mx1)), (SP_MEM, (me0, me1)), (SP_FFN, (ff0, ff1))):
            for l in range(2):
                o_ref[base + l:base + l + 1, 0:D] = pair[l][...]
        o_ref[SP_FINAL:SP_FINAL + 1, 0:D] = fin[...]
        o_ref[SP_MISC:SP_MISC + 1, 0:128] = snk[...]
        o_ref[SP_MISC:SP_MISC + 1, 128:256] = par[...]
        o_ref[SP_MISC:SP_MISC + 1, 256:384] = ng[...]
        o_ref[SP_MISC:SP_MISC + 1, 384:512] = lss[...]

    vm = pl.BlockSpec(memory_space=pltpu.VMEM)
    return pl.pallas_call(body, in_specs=[vm] * n, out_specs=vm, out_shape=_SDS((SMALL_ROWS, D_FF), F32), name=name)(*flat)


_SMALL = ["rel_bias", "norm_mix_g", "norm_mem_g", "sinks_a", "a_log_b", "dt_bias_b", "out_norm_g_b", "norm_ffn_g",
          "ffn_conv_b", "final_norm_g", "conv_qkv_b", "ffn_conv_w"]


def _adamw_small(recv, rc_qkv, rc_ffn, ws, ms, vs, name, deps=()):
    n = len(_SMALL)

    def body(*refs):
        recv_ref, qkv_ref, ffn_ref = refs[:3]
        w_refs, m_refs, v_refs = refs[3:3 + n], refs[3 + n:3 + 2 * n], refs[3 + 2 * n:3 + 3 * n]
        outs, loss_ref = refs[len(refs) - 4 * n - 1:len(refs) - 1], refs[-1]
        gs = _sum_sources(recv_ref)
        loss_ref[...] = gs[SP_MISC:SP_MISC + 1, 384:512]
        grads = {
            "rel_bias": jnp.concatenate(
                [gs[SP_QKV + k // 8:SP_QKV + k // 8 + 1, SP_REL_LANE + 128 * (k % 8):SP_REL_LANE + 128 * (k % 8) + A_HEADS]
                 for k in range(N_BUCKETS)], axis=0),
            "norm_mix_g": gs[SP_MIX:SP_MIX + 2, 0:D], "norm_mem_g": gs[SP_MEM:SP_MEM + 2, 0:D],
            "sinks_a": gs[SP_MISC:SP_MISC + 1, 0:A_HEADS],
            "a_log_b": gs[SP_MISC:SP_MISC + 1, 128:128 + B_V_HEADS],
            "dt_bias_b": gs[SP_MISC:SP_MISC + 1, 128 + B_V_HEADS:128 + 2 * B_V_HEADS],
            "out_norm_g_b": gs[SP_MISC:SP_MISC + 1, 256:256 + B_HD],
            "norm_ffn_g": gs[SP_FFN:SP_FFN + 2, 0:D], "ffn_conv_b": gs[SP_CB:SP_CB + 2, :],
            "final_norm_g": gs[SP_FINAL:SP_FINAL + 1, 0:D],
            "conv_qkv_b": _sum_sources(qkv_ref), "ffn_conv_w": _sum_sources(ffn_ref),
        }
        for i, nm in enumerate(_SMALL):
            g = grads[nm]
            delta, mm, vv = _adam_update(g, w_refs[i][...], m_refs[i][...], v_refs[i][...])
            outs[i][...] = g
            outs[n + i][...] = delta
            outs[2 * n + i][...] = mm
            outs[3 * n + i][...] = vv

    vm = pl.BlockSpec(memory_space=pltpu.VMEM)
    shapes = [_SDS(w.shape, F32) for w in ws]
    return pl.pallas_call(
        body, in_specs=[vm] * (3 + 3 * n) + _dep_specs(deps), out_specs=[vm] * (4 * n + 1),
        out_shape=shapes * 4 + [_SDS((1, 128), F32)],
        name=name)(recv, rc_qkv, rc_ffn, *ws, *ms, *vs, *deps)


def _assemble(gathered, axis):
    g = jnp.moveaxis(gathered, 0, axis)
    shp = list(g.shape)
    return g.reshape(shp[:axis] + [shp[axis] * shp[axis + 1]] + shp[axis + 2:])


def _pad_rows(a, rows):
    return jnp.pad(a, ((0, rows - a.shape[0]), (0, 0)))


def _pad_lanes(a, lanes=128):
    return jnp.pad(a, ((0, 0), (0, lanes - a.shape[1])))


def _ff_blocks(a):
    return jnp.moveaxis(a.reshape(a.shape[0], FF_BLOCKS, GU_SHARD), 1, 0)


def _reorder_b(w):
    qkv_z = w[..., :B_QKV + B_V]
    gates = w[..., B_QKV + B_V:B_QKV + B_V + 2 * B_V_HEADS]
    xq = w[..., IN_B - X_Q:]
    pad = jnp.zeros(w.shape[:-1] + (IN_BP - IN_B,), w.dtype)
    return jnp.concatenate([qkv_z, xq, gates, pad], axis=-1)


def kernel(x, mem, rel_bias, norm_mix_g, norm_mem_g, w_mem_kv, w_out, w_in_a, sinks_a, w_in_b, conv_qkv_b, a_log_b, dt_bias_b, out_norm_g_b, norm_ffn_g, w_gate_up, ffn_conv_w, ffn_conv_b, w_down, final_norm_g, loss_target, m_rel_bias, m_norm_mix_g, m_norm_mem_g, m_w_mem_kv, m_w_out, m_w_in_a, m_sinks_a, m_w_in_b, m_conv_qkv_b, m_a_log_b, m_dt_bias_b, m_out_norm_g_b, m_norm_ffn_g, m_w_gate_up, m_ffn_conv_w, m_ffn_conv_b, m_w_down, m_final_norm_g, v_rel_bias, v_norm_mix_g, v_norm_mem_g, v_w_mem_kv, v_w_out, v_w_in_a, v_sinks_a, v_w_in_b, v_conv_qkv_b, v_a_log_b, v_dt_bias_b, v_out_norm_g_b, v_norm_ffn_g, v_w_gate_up, v_ffn_conv_w, v_ffn_conv_b, v_w_down, v_final_norm_g):
    local = dict(locals())
    order = ["rel_bias", "norm_mix_g", "norm_mem_g", "w_mem_kv", "w_out", "w_in_a", "sinks_a", "w_in_b", "conv_qkv_b",
             "a_log_b", "dt_bias_b", "out_norm_g_b", "norm_ffn_g", "w_gate_up", "ffn_conv_w", "ffn_conv_b", "w_down",
             "final_norm_g"]
    wts = {n: local[n] for n in order}
    moms = {n: local["m_" + n] for n in order}
    vars_ = {n: local["v_" + n] for n in order}
    h0 = x[0]
    memx = mem[0]
    tgt = loss_target[0]
    s = h0.shape[0]
    tm = _rows(s)
    tb = min(s, _TM_BIG)

    t_ = lambda a: jnp.swapaxes(a, 1, 2)
    g_mk0, g_out0, g_ia, g_cq, g_cw = _all_gather(
        [w_mem_kv[0:1].astype(_MXU), w_out[0:1].astype(_MXU), t_(w_in_a).astype(_MXU), conv_qkv_b, ffn_conv_w], "gather_first")
    g_mk, g_out = [g_mk0], [g_out0]
    def after(a, b):
        return a + (b[(0,) * b.ndim] * 0).astype(a.dtype)

    gu0_w = _seq_gather([after(t_(w_gate_up)[0].astype(_MXU), g_ia)], "gather_gate_up0", 1)
    dn0_w = _seq_gather([after(w_down[0].astype(_MXU), g_ia)], "gather_down0", 8)
    w_ia = g_ia.reshape(IN_A, D)
    conv_qkv = _pad_rows(_assemble(g_cq, 2)[0], HALO)
    ffn_cw_full = _assemble(g_cw, 2)
    ffn_cw = [_ff_blocks(_pad_rows(ffn_cw_full[i], HALO)) for i in range(2)]
    ffn_cb = [_ff_blocks(ffn_conv_b[i:i + 1]) for i in range(2)]
    bucket = jnp.asarray(_bucket_table())
    bias = _bias_build(rel_bias, bucket, "bias_build")
    sinks = _pad_lanes(sinks_a)
    par_b = _pad_lanes(jnp.concatenate([a_log_b, dt_bias_b], axis=1))

    row_x = pl.BlockSpec((tm, D), lambda i, j: (i, 0))
    gu_shape = (2, FF_BLOCKS, s, GU_SHARD)

    def in_proj(h, g, w, w_spec, n_cols, tn, name, deps=(), out_dtype=F32, w_t=False, tm=None):
        return _norm_matmul(h, g, w, w_spec, n_cols // tn, (h.shape[0], n_cols),
                            pl.BlockSpec((tm or _rows(h.shape[0]), tn), lambda i, j: (i, j)), name, deps=deps, out_dtype=out_dtype,
                            w_t=w_t, tm=tm)

    def ffn_fwd(i, h, g_gu, g_dn, deps=()):
        gu, hn = _norm_matmul(h, norm_ffn_g[i:i + 1], g_gu, _spec_gate_up(1), N_DEV, gu_shape,
                              _spec_gu_act(0, 1, tb), f"gate_up_{i}", deps=deps, out_dtype=_ACT, w_t=True, tm=tb)
        h_new, act, gc = _glu_down(gu, ffn_cw[i], ffn_cb[i], g_dn, h, f"glu_down_{i}")
        return h_new, gu, hn, (act, gc)

    def out_proj(i, mix, h):
        return _matmul_res(mix, row_x, g_out[i], _spec_rowsharded(0, D // N_DEV, D), 1, h, f"out_proj_{i}")

    proj_a, hn_a = in_proj(h0, norm_mix_g[0:1], w_ia, pl.BlockSpec((640, D), lambda i, j: (j, 0)), IN_A, 640, "in_proj_a",
                           deps=gu0_w.srcs + dn0_w.srcs, out_dtype=_ACT, w_t=True)
    memkv0, memn0 = in_proj(memx, norm_mem_g[0:1], g_mk[0], _spec_rowsharded(0, D // N_DEV, 2 * X_Q), 2 * X_Q, 2 * X_Q, "mem_proj_0")
    mix_a = _mix_a_fwd(proj_a, bias, sinks, memkv0, "mix_a_fwd")
    h1 = out_proj(0, mix_a, h0)
    g_gu0, g_dn0 = gu0_w.lands[0], dn0_w.lands[0]
    in_b_w = _seq_gather([after(_reorder_b(w_in_b).astype(_MXU), h1), after(w_mem_kv[1:2].astype(_MXU), h1)], "gather_in_b", 2)
    ffn1_w = _seq_gather([after(t_(w_gate_up)[1].astype(_MXU), h1), after(w_down[1].astype(_MXU), h1),
                          after(w_out[1:2].astype(_MXU), h1)], "gather_ffn1", 3)
    h2, gu0, hn_f0, act0 = ffn_fwd(0, h1, g_gu0, g_dn0, deps=in_b_w.srcs + ffn1_w.srcs)
    g_ib, g_mk1 = in_b_w.lands
    g_gu1, g_dn1, g_out1 = ffn1_w.lands
    g_mk.append(g_mk1)
    g_out.append(g_out1)
    proj_b, hn_b = in_proj(h2, norm_mix_g[1:2], g_ib, _spec_rowsharded(0, D // N_DEV, 896, col_block=1), IN_BP, 896, "in_proj_b")
    memkv1, memn1 = in_proj(memx, norm_mem_g[1:2], g_mk[1], _spec_rowsharded(0, D // N_DEV, 2 * X_Q), 2 * X_Q, 2 * X_Q, "mem_proj_1",
                            deps=[h2])
    mix_b, states, deltas = _mix_b_fwd(proj_b, conv_qkv, par_b, out_norm_g_b, memkv1, "mix_b_fwd")
    h3 = out_proj(1, mix_b, h2)
    h4, gu1, hn_f1, act1 = ffn_fwd(1, h3, g_gu1, g_dn1)
    loss_row, *dh, d_final_g = _loss_head(h4, final_norm_g[None, :], tgt, "loss_head")

    zeros_mem = jnp.zeros_like(memx)
    per_dest2 = [(0, True), (1, True)]

    def ffn_bwd(i, dh, h_in, gu, hn_f, act_gc, g_gu, g_dn, deps=()):
        act, gc = act_gc
        dgu, d_cw, d_cb = _glu_bwd(gu, gc, ffn_cw[i], dh[1], g_dn, f"glu_bwd_{i}", deps=deps)
        d_wdown = _matmul_tn(act, pl.BlockSpec((None, tb, GU_SHARD), lambda j, r: (j, r, 0)),
                             dh[1], pl.BlockSpec((tb, D), lambda j, r: (r, 0)), s, FF_BLOCKS, (GU_SHARD, D),
                             (N_DEV, DN_SHARD, D), pl.BlockSpec((2, DN_SHARD, D), lambda j, r: (j, 0, 0)), f"d_w_down_{i}",
                             tm=tb)
        *dh_new, d_g = _matmul_nt_normbwd(dgu, _spec_gu_act(0, 1, tm), g_gu, _spec_gate_up(1), N_DEV, h_in,
                                          norm_ffn_g[i:i + 1], dh[0], f"d_ffn_in_{i}", w_t=True, act_copy=True)
        d_wgu = _matmul_tn(dgu, _spec_gu_act(1, 0, tb), hn_f, pl.BlockSpec((tb, D), lambda j, r: (r, 0)), s, N_DEV,
                           (GU_SHARD, D), (N_DEV, GU_SHARD, D), pl.BlockSpec((None, GU_SHARD, D), lambda j, r: (j, 0, 0)),
                           f"d_w_gate_up_{i}", tm=tb)
        return dh_new, [d_wdown, d_wgu], d_cw, d_cb, d_g

    def out_bwd(i, dh, mix, deps):
        dmix = _matmul_nt(dh[1], g_out[i], _spec_rowsharded(0, D // N_DEV, D), 1, (s, D), row_x, f"d_mix_{i}", deps=deps, out_dtype=_ACT)
        d_wout = _matmul_tn(mix, pl.BlockSpec((tb, D), lambda j, r: (r, 0)), dh[1], pl.BlockSpec((tb, D), lambda j, r: (r, 0)),
                            s, 1, (D, D), (N_DEV, D // N_DEV, D), pl.BlockSpec((N_DEV, D // N_DEV, D), lambda j, r: (0, 0, 0)),
                            f"d_w_out_{i}", tm=tb)
        return dmix, d_wout

    def mem_bwd(i, dmemkv, memn):
        tmm = _rows(MEM_LEN)
        *_, d_g = _matmul_nt_normbwd(dmemkv, pl.BlockSpec((tmm, 2 * X_Q), lambda r, j: (r, 0)), g_mk[i],
                                     _spec_rowsharded(0, D // N_DEV, 2 * X_Q), 1, memx, norm_mem_g[i:i + 1], zeros_mem,
                                     f"d_mem_in_{i}")
        by_row = lambda j, r: (r, 0)
        d_w = _matmul_tn(memn, pl.BlockSpec((tmm, D), by_row), dmemkv, pl.BlockSpec((tmm, 2 * X_Q), by_row), MEM_LEN, 1,
                         (D, 2 * X_Q), (N_DEV, D // N_DEV, 2 * X_Q),
                         pl.BlockSpec((N_DEV, D // N_DEV, 2 * X_Q), lambda j, r: (0, 0, 0)), f"d_w_mem_kv_{i}")
        return d_w, d_g

    out_land = ((N_DEV, D // N_DEV, D), _WIRE)
    mk_land = ((N_DEV, D // N_DEV, 2 * X_Q), _WIRE)
    ffn_lands = [((N_DEV, DN_SHARD, D), _WIRE), ((N_DEV, GU_SHARD, D), _WIRE)]
    dh, d_ffn1, d_cw1, d_cb1, d_gf1 = ffn_bwd(1, dh, h3, gu1, hn_f1, act1, g_gu1, g_dn1)
    ffn1_g = _seq_exchange(d_ffn1, ffn_lands, per_dest2, "send_ffn1_grads", 5)
    dmix, d_wout1 = out_bwd(1, dh, mix_b, ffn1_g.srcs)
    dproj_b, d_convw, d_par, d_ng, dmemkv1 = _mix_b_bwd(proj_b, conv_qkv, par_b, out_norm_g_b, memkv1, states, deltas, dmix, "mix_b_bwd")
    *dh, d_gm1 = _matmul_nt_normbwd(dproj_b, pl.BlockSpec((tm, 896), lambda i, j: (i, j)), g_ib,
                                    _spec_rowsharded(0, D // N_DEV, 896, col_block=1), IN_BP // 896, h2, norm_mix_g[1:2], dh[0],
                                    "d_in_b", act_copy=True)
    d_wib = _matmul_tn(hn_b, pl.BlockSpec((tb, D), lambda j, r: (r, 0)), dproj_b, pl.BlockSpec((tb, 896), lambda j, r: (r, j)),
                       s, IN_BP // 896, (D, 896), (N_DEV, D // N_DEV, IN_BP),
                       pl.BlockSpec((N_DEV, D // N_DEV, 896), lambda j, r: (0, 0, j)), "d_w_in_b", tm=tb)
    d_wmk1, d_gmem1 = mem_bwd(1, dmemkv1, memn1)
    mix1_g = _seq_exchange([d_wout1, d_wib, d_wmk1], [out_land, ((N_DEV, D // N_DEV, IN_BP), _WIRE), mk_land],
                           [(0, True), (1, True), (2, True)], "send_mix1_grads", 6)
    dh, d_ffn0, d_cw0, d_cb0, d_gf0 = ffn_bwd(0, dh, h1, gu0, hn_f0, act0, g_gu0, g_dn0, deps=mix1_g.srcs)
    dmix, d_wout0 = out_bwd(0, dh, mix_a, d_ffn0 + ffn1_g.lands[:1])
    ffn0_g = _seq_exchange(d_ffn0 + [d_wout0], ffn_lands + [out_land], per_dest2 + [(2, True)], "send_ffn0_grads", 4)
    dproj_a, dbias, dsinks, dmemkv0 = _mix_a_bwd(proj_a, bias, sinks, memkv0, dmix, "mix_a_bwd", deps=ffn0_g.srcs)
    dx, _, d_gm0 = _matmul_nt_normbwd(dproj_a, pl.BlockSpec((tm, 640), lambda i, j: (i, j)), w_ia,
                                      pl.BlockSpec((640, D), lambda i, j: (j, 0)), IN_A // 640, h0, norm_mix_g[0:1], dh[0],
                                      "d_in_a", w_t=True)
    d_wia = _matmul_tn(dproj_a, pl.BlockSpec((tb, IN_A), lambda j, r: (r, 0)), hn_a, pl.BlockSpec((tb, D), lambda j, r: (r, 0)),
                       s, 1, (IN_A, D), (N_DEV, IA_SHARD, D), pl.BlockSpec((N_DEV, IA_SHARD, D), lambda j, r: (0, 0, 0)),
                       "d_w_in_a", tm=tb)
    d_wmk0, d_gmem0 = mem_bwd(0, dmemkv0, memn0)
    d_rel = _bias_reduce(dbias, bucket, "bias_reduce")
    small = _pack_small(d_rel, (d_cb0, d_cb1), (d_cw0, d_cw1), d_convw, (d_gm0, d_gm1), (d_gmem0, d_gmem1),
                        (d_gf0, d_gf1), d_final_g, dsinks, d_par, d_ng, loss_row, "pack_small")
    mix0_g = _seq_exchange([d_wia, d_wmk0, small],
                           [((N_DEV, IA_SHARD, D), _WIRE), mk_land, ((N_DEV, SMALL_ROWS, D_FF), F32)],
                           [(0, True), (1, True), (2, False)], "send_mix0_grads", 7)

    res = {}
    last = []

    def update(nm, parts, tr, restore=False, transposed=False):
        view = t_ if transposed else (lambda a: a)
        out = _adamw(parts, view(wts[nm]), view(moms[nm]), view(vars_[nm]), tr, "adamw_" + nm, restore_b=restore, deps=last[-1:])
        res[nm] = [view(o) for o in out]
        last.append(out[1])

    r_dn1, r_gu1 = ffn1_g.lands
    r_dn0, r_gu0, r_out0 = ffn0_g.lands
    r_out1, r_ib, r_mk1 = mix1_g.lands
    update("w_in_b", [r_ib], 32, True)
    update("w_gate_up", [r_gu0, r_gu1], 176, transposed=True)
    update("w_down", [r_dn0, r_dn1], 176)
    r_ia, r_mk0, r_small = mix0_g.lands
    update("w_mem_kv", [r_mk0, r_mk1], 128)
    update("w_out", [r_out0, r_out1], 128)
    update("w_in_a", [r_ia], IA_SHARD, transposed=True)

    my = 4 * lax.axis_index("x") + 2 * lax.axis_index("y") + lax.axis_index("c")
    cq = conv_qkv_b.shape[-1]
    cf = ffn_conv_w.shape[-1]
    rc_qkv = lax.dynamic_slice_in_dim(r_small[:, SP_QKV:SP_QKV + B_CONV, :B_QKV], my * cq, cq, axis=2)[:, None]
    rc_ffn = lax.dynamic_slice_in_dim(r_small[:, SP_CW:SP_CW + 2 * FFN_CONV, :], my * cf, cf, axis=2).reshape(N_DEV, 2, FFN_CONV, cf)
    as2d = lambda a: a[None, :] if a.ndim == 1 else a
    small_out = _adamw_small(r_small, rc_qkv, rc_ffn, [as2d(wts[n]) for n in _SMALL], [as2d(moms[n]) for n in _SMALL],
                             [as2d(vars_[n]) for n in _SMALL], "adamw_small", deps=last[-1:])
    ns = len(_SMALL)
    for i, nm in enumerate(_SMALL):
        res[nm] = [small_out[k * ns + i].reshape(wts[nm].shape) for k in range(4)]

    return (small_out[-1][0, 0], dx[None], *[res[n][0] for n in order], *[res[n][1] for n in order],
            *[res[n][2] for n in order], *[res[n][3] for n in order])
```

```python
import functools
import math

import numpy as np

import jax
import jax.numpy as jnp
from jax import lax
from jax.experimental import pallas as pl
from jax.experimental.pallas import tpu as pltpu
from jax.experimental.pallas import tpu_sc as plsc

F32 = jnp.float32
_MXU = jnp.bfloat16
_ACT = jnp.bfloat16
_WIRE = jnp.bfloat16
_HI = lax.Precision.HIGH
_TM = 1024
_TM_GLU = 1024
_TM_BIG = 2048
_VMEM_LIMIT = 48 * 1024 * 1024
_SDS = jax.ShapeDtypeStruct

D = 1024
EPS = 1e-6
A_HEADS, A_KV_HEADS, A_HD, BLK = 12, 2, 64, 128
N_BUCKETS, MAX_DISTANCE = 32, 128
B_QK_HEADS, B_V_HEADS, B_HD, B_CONV, CHUNK = 3, 6, 128, 4, 64
X_HEADS, X_HD, MEM_LEN = 4, 64, 256
D_FF, FFN_CONV = 2816, 3
A_Q, A_KV, X_Q = 768, 128, 256
B_QK, B_V, B_QKV = 384, 768, 1536
IN_A, IN_B = 1280, 2572
IN_BP = 2688
BP_Z, BP_XQ, BP_GATE = 1536, 2304, 2560
HALO = 8
GLU_HALO = 16

N_DEV = 8
GU_SHARD = 2 * D_FF // N_DEV
FF_BLOCKS = D_FF // GU_SHARD
DN_SHARD = D_FF // N_DEV
IA_SHARD = IN_A // N_DEV

ADAM_LR, ADAM_B1, ADAM_B2, ADAM_EPS, ADAM_WD, ADAM_STEP = 0.001, 0.9, 0.999, 1e-08, 0.01, 10

SP_CB, SP_CW, SP_QKV, SP_MIX, SP_MEM, SP_FFN, SP_FINAL, SP_MISC, SMALL_ROWS = 0, 2, 8, 12, 14, 16, 18, 19, 24
SP_REL_LANE = B_QKV


def _cp(*sems):
    return pltpu.CompilerParams(dimension_semantics=sems, vmem_limit_bytes=_VMEM_LIMIT)


def _mm(a, b):
    return jnp.dot(a.astype(_MXU), b.astype(_MXU), preferred_element_type=F32)


def _mm_nt(a, b):
    return lax.dot_general(a.astype(_MXU), b.astype(_MXU), (((1,), (1,)), ((), ())), preferred_element_type=F32)


def _mm_tn(a, b):
    return lax.dot_general(a.astype(_MXU), b.astype(_MXU), (((0,), (0,)), ((), ())), preferred_element_type=F32)


def _mmf(a, b):
    return jnp.dot(a, b, preferred_element_type=F32, precision=_HI)


def _mmf_nt(a, b):
    return lax.dot_general(a, b, (((1,), (1,)), ((), ())), preferred_element_type=F32, precision=_HI)


def _silu(x):
    return x * jax.nn.sigmoid(x)


def _w2d(ref):
    v = ref[...]
    return v.reshape(-1, v.shape[-1])


def _rows(m):
    return min(m, _TM)


def _spec_rowsharded(layer, rows, cols, col_block=None):
    if col_block is None:
        return pl.BlockSpec((N_DEV, None, rows, cols), lambda *_: (0, layer, 0, 0))
    return pl.BlockSpec((N_DEV, None, rows, cols), lambda *ids: (0, layer, 0, ids[col_block]))


def _spec_gate_up(axis):
    return pl.BlockSpec((None, GU_SHARD, D), lambda *ids: (ids[axis], 0, 0))


def _spec_down(axis):
    return pl.BlockSpec((2, DN_SHARD, D), lambda *ids: (ids[axis], 0, 0))


def _dep_specs(deps):
    return [pl.BlockSpec(memory_space=pl.ANY) for d in deps]


def _spec_gu_act(row_axis, axis, tm):
    return pl.BlockSpec((None, None, tm, GU_SHARD), lambda *ids: (ids[axis] // FF_BLOCKS, ids[axis] % FF_BLOCKS, ids[row_axis], 0))


def _norm_matmul(x, g, w, w_spec, n_blocks, out_shape, out_spec, name, deps=(), out_dtype=F32, w_t=False, tm=None):
    m, k = x.shape
    tm = tm or _rows(m)

    def body(x_ref, g_ref, w_ref, *rest):
        y_ref, hn_ref = rest[-2:]

        @pl.when(pl.program_id(1) == 0)
        def _():
            xv = x_ref[...]
            r = lax.rsqrt(jnp.mean(xv * xv, axis=-1, keepdims=True) + EPS)
            hn_ref[...] = (xv * r * g_ref[...]).astype(hn_ref.dtype)

        y_ref[...] = (_mm_nt if w_t else _mm)(hn_ref[...], _w2d(w_ref)).astype(y_ref.dtype)

    return pl.pallas_call(
        body, grid=(m // tm, n_blocks),
        in_specs=[pl.BlockSpec((tm, k), lambda i, j: (i, 0)), pl.BlockSpec((1, k), lambda i, j: (0, 0)), w_spec]
        + _dep_specs(deps),
        out_specs=[out_spec, pl.BlockSpec((tm, k), lambda i, j: (i, 0))],
        out_shape=[_SDS(out_shape, out_dtype), _SDS((m, k), _ACT)],
        name=name, compiler_params=_cp("arbitrary", "arbitrary"))(x, g, w, *deps)


def _matmul_res(a, a_spec, w, w_spec, n_k, res, name):
    m, n = res.shape
    tm = _rows(m)

    def body(a_ref, w_ref, r_ref, o_ref):
        part = _mm(a_ref[...], _w2d(w_ref))

        @pl.when(pl.program_id(1) == 0)
        def _():
            o_ref[...] = r_ref[...] + part

        @pl.when(pl.program_id(1) > 0)
        def _():
            o_ref[...] += part

    return pl.pallas_call(
        body, grid=(m // tm, n_k),
        in_specs=[a_spec, w_spec, pl.BlockSpec((tm, n), lambda i, j: (i, 0))],
        out_specs=pl.BlockSpec((tm, n), lambda i, j: (i, 0)),
        out_shape=_SDS((m, n), F32), name=name, compiler_params=_cp("arbitrary", "arbitrary"))(a, w, res)


def _matmul_nt(dy, w, w_spec, n_blocks, out_shape, out_spec, name, deps=(), out_dtype=F32):
    m, n = dy.shape
    tm = _rows(m)

    def body(dy_ref, w_ref, *rest):
        o_ref = rest[-1]
        o_ref[...] = _mm_nt(dy_ref[...], _w2d(w_ref)).astype(o_ref.dtype)

    return pl.pallas_call(
        body, grid=(m // tm, n_blocks),
        in_specs=[pl.BlockSpec((tm, n), lambda i, j: (i, 0)), w_spec] + _dep_specs(deps),
        out_specs=out_spec, out_shape=_SDS(out_shape, out_dtype),
        name=name, compiler_params=_cp("arbitrary", "arbitrary"))(dy, w, *deps)


def _matmul_nt_normbwd(dy, dy_spec, w, w_spec, nj, h, g, dh_in, name, w_t=False, act_copy=False):
    m, k = h.shape
    tm = _rows(m)

    def body(dy_ref, w_ref, h_ref, g_ref, dhin_ref, dh_ref, *rest):
        dg_ref, acc_ref = rest[-2:]
        i, j = pl.program_id(0), pl.program_id(1)

        @pl.when(j == 0)
        def _():
            acc_ref[...] = jnp.zeros_like(acc_ref)

        acc_ref[...] += (_mm if w_t else _mm_nt)(dy_ref[...], _w2d(w_ref))

        @pl.when(j == nj - 1)
        def _():
            xv = h_ref[...]
            r = lax.rsqrt(jnp.mean(xv * xv, axis=-1, keepdims=True) + EPS)
            xh = xv * r
            dhn = acc_ref[...]
            part = jnp.sum(dhn * xh, axis=0, keepdims=True)

            @pl.when(i == 0)
            def _():
                dg_ref[...] = part

            @pl.when(i > 0)
            def _():
                dg_ref[...] += part

            t = dhn * g_ref[...]
            dh = dhin_ref[...] + r * (t - xh * jnp.mean(t * xh, axis=-1, keepdims=True))
            dh_ref[...] = dh
            if act_copy:
                rest[0][...] = dh.astype(_ACT)

    rows = pl.BlockSpec((tm, k), lambda i, j: (i, 0))
    outs = pl.pallas_call(
        body, grid=(m // tm, nj),
        in_specs=[dy_spec, w_spec, rows, pl.BlockSpec((1, k), lambda i, j: (0, 0)), rows],
        out_specs=[rows] + [rows] * act_copy + [pl.BlockSpec((1, k), lambda i, j: (0, 0))],
        out_shape=[_SDS((m, k), F32)] + [_SDS((m, k), _ACT)] * act_copy + [_SDS((1, k), F32)],
        scratch_shapes=[pltpu.VMEM((tm, k), F32)],
        name=name, compiler_params=_cp("arbitrary", "arbitrary"))(dy, w, h, g, dh_in)
    return outs[0], (outs[1] if act_copy else None), outs[-1]


def _matmul_tn(x, x_spec, dy, dy_spec, m, n_blocks, acc_shape, out_shape, out_spec, name, tm=None):
    tm = tm or _rows(m)
    nm = m // tm

    def body(x_ref, dy_ref, o_ref, acc_ref):
        @pl.when(pl.program_id(1) == 0)
        def _():
            acc_ref[...] = jnp.zeros_like(acc_ref)

        acc_ref[...] += _mm_tn(x_ref[...], dy_ref[...])

        @pl.when(pl.program_id(1) == nm - 1)
        def _():
            o_ref[...] = acc_ref[...].reshape(o_ref.shape).astype(o_ref.dtype)

    return pl.pallas_call(
        body, grid=(n_blocks, nm), in_specs=[x_spec, dy_spec], out_specs=out_spec,
        out_shape=_SDS(out_shape, _WIRE), scratch_shapes=[pltpu.VMEM(acc_shape, F32)],
        name=name, compiler_params=_cp("arbitrary", "arbitrary"))(x, dy)


def _loss_head(h, g, tgt, name):
    m, k = h.shape
    tm = _rows(m)

    def body(h_ref, g_ref, t_ref, loss_ref, dh_ref, dha_ref, dg_ref):
        i = pl.program_id(0)
        xv = h_ref[...]
        r = lax.rsqrt(jnp.mean(xv * xv, axis=-1, keepdims=True) + EPS)
        xh = xv * r
        gv = g_ref[...]
        err = xh * gv - t_ref[...]
        lpart = jnp.zeros((1, 128), F32) + 0.5 * jnp.sum(jnp.mean(err * err, axis=-1, keepdims=True), axis=0, keepdims=True)
        dy = err * (1.0 / k)
        gpart = jnp.sum(dy * xh, axis=0, keepdims=True)

        @pl.when(i == 0)
        def _():
            loss_ref[...] = lpart
            dg_ref[...] = gpart

        @pl.when(i > 0)
        def _():
            loss_ref[...] += lpart
            dg_ref[...] += gpart

        t = dy * gv
        dh = r * (t - xh * jnp.mean(t * xh, axis=-1, keepdims=True))
        dh_ref[...] = dh
        dha_ref[...] = dh.astype(_ACT)

    rows = pl.BlockSpec((tm, k), lambda i: (i, 0))
    return pl.pallas_call(
        body, grid=(m // tm,),
        in_specs=[rows, pl.BlockSpec((1, k), lambda i: (0, 0)), rows],
        out_specs=[pl.BlockSpec((1, 128), lambda i: (0, 0)), rows, rows, pl.BlockSpec((1, k), lambda i: (0, 0))],
        out_shape=[_SDS((1, 128), F32), _SDS((m, k), F32), _SDS((m, k), _ACT), _SDS((1, k), F32)],
        name=name, compiler_params=_cp("arbitrary"))(h, g, tgt)


def _glu_down(gu, conv_w, conv_b, w_down, res, name):
    s = gu.shape[2]
    tm = min(s, _TM_GLU)

    def body(gu_ref, prev_ref, w_ref, b_ref, wdn_ref, r_ref, o_ref, act_ref, gc_ref):
        i, j = pl.program_id(0), pl.program_id(1)
        prev = jnp.where(i > 0, prev_ref[...].astype(F32), 0.0)
        ext = jnp.concatenate([prev, gu_ref[0].astype(F32)], axis=0)
        gc = b_ref[...] + w_ref[FFN_CONV - 1:FFN_CONV, :] * ext
        for k in range(FFN_CONV - 1):
            gc = gc + w_ref[k:k + 1, :] * pltpu.roll(ext, FFN_CONV - 1 - k, 0)
        gc = gc[GLU_HALO:]
        gc_ref[...] = gc.astype(gc_ref.dtype)
        act =(_silu(gc) * gu_ref[1].astype(F32)).astype(act_ref.dtype)
        act_ref[...] = act
        part = _mm(act, _w2d(wdn_ref))

        @pl.when(j == 0)
        def _():
            o_ref[...] = r_ref[...] + part

        @pl.when(j > 0)
        def _():
            o_ref[...] += part

    return pl.pallas_call(
        body, grid=(s // tm, FF_BLOCKS),
        in_specs=[pl.BlockSpec((2, None, tm, GU_SHARD), lambda i, j: (0, j, i, 0)),
                  pl.BlockSpec((None, None, GLU_HALO, GU_SHARD),
                               lambda i, j: (0, j, jnp.maximum(i * (tm // GLU_HALO) - 1, 0), 0)),
                  pl.BlockSpec((None, HALO, GU_SHARD), lambda i, j: (j, 0, 0)),
                  pl.BlockSpec((None, 1, GU_SHARD), lambda i, j: (j, 0, 0)),
                  _spec_down(1), pl.BlockSpec((tm, D), lambda i, j: (i, 0))],
        out_specs=[pl.BlockSpec((tm, D), lambda i, j: (i, 0)), pl.BlockSpec((None, tm, GU_SHARD), lambda i, j: (j, i, 0)),
                   pl.BlockSpec((None, tm, GU_SHARD), lambda i, j: (j, i, 0))],
        out_shape=[_SDS((s, D), F32), _SDS((FF_BLOCKS, s, GU_SHARD), _ACT), _SDS((FF_BLOCKS, s, GU_SHARD), _ACT)], name=name,
        compiler_params=_cp("arbitrary", "arbitrary"))(gu, gu, conv_w, conv_b, w_down, res)


def _glu_bwd(gu, gc, conv_w, dh, w_down, name, deps=()):
    s = gu.shape[2]
    tm = min(s, _TM_GLU)
    nt = s // tm
    ext_rows = tm + GLU_HALO

    def body(gu_ref, prev_ref, gc_ref, w_ref, dh_ref, wdn_ref, *rest):
        dgu_ref, dw_ref, db_ref, carry_ref = rest[-4:]
        t = pl.program_id(1)
        i = nt - 1 - t

        @pl.when(t == 0)
        def _():
            carry_ref[...] = jnp.zeros_like(carry_ref)
            dw_ref[...] = jnp.zeros_like(dw_ref)
            db_ref[...] = jnp.zeros_like(db_ref)

        up = gu_ref[1].astype(F32)
        prev = jnp.where(i > 0, prev_ref[...].astype(F32), 0.0)
        ext = jnp.concatenate([prev, gu_ref[0].astype(F32)], axis=0)
        gc = gc_ref[...].astype(F32)
        sg = jax.nn.sigmoid(gc)
        da = _mm_nt(dh_ref[...], _w2d(wdn_ref))
        dup = da * (gc * sg)
        dgc = da * up * (sg * (1.0 + gc * (1.0 - sg)))
        db_ref[...] += jnp.sum(dgc, axis=0, keepdims=True)
        dgc_ext = jnp.concatenate([jnp.zeros((GLU_HALO, GU_SHARD), F32), dgc], axis=0)
        ahead = [pltpu.roll(dgc_ext, ext_rows - (FFN_CONV - 1 - j), 0) if j < FFN_CONV - 1 else dgc_ext
                 for j in range(FFN_CONV)]
        dext = ahead[0] * w_ref[0:1, :]
        for j in range(FFN_CONV):
            dw_ref[j:j + 1, :] += jnp.sum(ext * ahead[j], axis=0, keepdims=True)
            if j > 0:
                dext = dext + ahead[j] * w_ref[j:j + 1, :]
        tail = jnp.concatenate([jnp.zeros((tm - GLU_HALO, GU_SHARD), F32), carry_ref[...]], axis=0)
        dgate = dext[GLU_HALO:] + tail
        carry_ref[...] = dext[:GLU_HALO]
        dgu_ref[0] = dgate.astype(dgu_ref.dtype)
        dgu_ref[1] = dup.astype(dgu_ref.dtype)

    return pl.pallas_call(
        body, grid=(FF_BLOCKS, nt),
        in_specs=[pl.BlockSpec((2, None, tm, GU_SHARD), lambda j, t: (0, j, nt - 1 - t, 0)),
                  pl.BlockSpec((None, None, GLU_HALO, GU_SHARD),
                               lambda j, t: (0, j, jnp.maximum((nt - 1 - t) * (tm // GLU_HALO) - 1, 0), 0)),
                  pl.BlockSpec((None, tm, GU_SHARD), lambda j, t: (j, nt - 1 - t, 0)),
                  pl.BlockSpec((None, HALO, GU_SHARD), lambda j, t: (j, 0, 0)),
                  pl.BlockSpec((tm, D), lambda j, t: (nt - 1 - t, 0)), _spec_down(0)] + _dep_specs(deps),
        out_specs=[pl.BlockSpec((2, None, tm, GU_SHARD), lambda j, t: (0, j, nt - 1 - t, 0)),
                   pl.BlockSpec((None, HALO, GU_SHARD), lambda j, t: (j, 0, 0)),
                   pl.BlockSpec((None, 1, GU_SHARD), lambda j, t: (j, 0, 0))],
        out_shape=[_SDS(gu.shape, _ACT), _SDS((FF_BLOCKS, HALO, GU_SHARD), F32), _SDS((FF_BLOCKS, 1, GU_SHARD), F32)],
        scratch_shapes=[pltpu.VMEM((GLU_HALO, GU_SHARD), F32)],
        name=name, compiler_params=_cp("arbitrary", "arbitrary"))(gu, gu, gc, conv_w, dh, w_down, *deps)


def _bucket_table():
    qi = np.arange(BLK)[:, None]
    kj = np.arange(BLK)[None, :]
    n = np.where(kj > qi, BLK + qi - kj, qi - kj)
    max_exact = N_BUCKETS // 2
    nf = np.maximum(n, 1).astype(np.float32)
    large = max_exact + (np.log(nf / max_exact) / math.log(MAX_DISTANCE / max_exact)
                         * (N_BUCKETS - max_exact)).astype(np.int32)
    large = np.minimum(large, N_BUCKETS - 1)
    return np.where(n < max_exact, n, large).astype(np.int32)


def _lane_low():
    return lax.broadcasted_iota(jnp.int32, (1, 128), 1) < A_HD


def _swa_groups(q, kd, vd, sink, bias, upper, first):
    n = A_HEADS // A_KV_HEADS
    ng = len(q)
    low = _lane_low()
    qm = [jnp.concatenate([jnp.where(low == (h % 2 == 0), q[g][:, (h // 2) * 128:(h // 2 + 1) * 128], 0.0) for h in range(n)], axis=0)
          for g in range(ng)]
    s2 = [_mm_nt(qm[g], kd[g]) * (A_HD ** -0.5) for g in range(ng)]
    s = [jnp.where(upper[None], s2[g][:, :BLK].reshape(n, BLK, BLK), s2[g][:, BLK:].reshape(n, BLK, BLK)) + bias[g] for g in range(ng)]
    s = [t if f is None else jnp.where((upper & f)[None], -jnp.inf, t) for t, f in zip(s, first)]
    m = [lax.stop_gradient(jnp.maximum(jnp.max(s[g], axis=-1, keepdims=True), sink[g])) for g in range(ng)]
    p = [jnp.exp(s[g] - m[g]) for g in range(ng)]
    split = [jnp.concatenate([jnp.where(upper[None], t, 0.0), jnp.where(upper[None], 0.0, t)], axis=-1).reshape(n * BLK, 2 * BLK)
             for t in p]
    ones = jnp.ones((BLK, 128), F32)
    den = [_mm(p[g].reshape(n * BLK, BLK), ones) + jnp.exp(sink[g] - m[g]).reshape(n * BLK, 1) for g in range(ng)]
    o = [_mm(split[g], vd[g]) / den[g] for g in range(ng)]
    return [jnp.concatenate([jnp.where(low, t[2 * k * BLK:(2 * k + 1) * BLK], t[(2 * k + 1) * BLK:(2 * k + 2) * BLK])
                             for k in range(n // 2)], axis=1) for t in o]


def _mix_a_core(q, kd, vd, sink, bias, xq, mk, mv, upper, first):
    return _swa_groups(q, kd, vd, sink, bias, upper, first), _cross_pairs(xq, mk, mv)


def _swa_sinks(sink_ref, g):
    n = A_HEADS // A_KV_HEADS
    return jnp.concatenate([sink_ref[:, h:h + 1] for h in range(g * n, (g + 1) * n)], axis=0).reshape(n, 1, 1)


def _both_halves(t, t_rolled, g):
    low = _lane_low()
    return jnp.where(low, t, t_rolled) if g == 0 else jnp.where(low, t_rolled, t)


def _cross_pairs(q, mk, mv):
    rows = q.shape[0]
    low = _lane_low()
    qm = [jnp.concatenate([jnp.where(low, q[:, p * 128:(p + 1) * 128], 0.0), jnp.where(low, 0.0, q[:, p * 128:(p + 1) * 128])], axis=0)
          for p in range(X_HEADS // 2)]
    s = [_mm_nt(qm[p], mk[:, p * 128:(p + 1) * 128]) * (X_HD ** -0.5) for p in range(X_HEADS // 2)]
    e = [jnp.exp(t - lax.stop_gradient(jnp.max(t, axis=-1, keepdims=True))) for t in s]
    pr = [t / jnp.sum(t, axis=-1, keepdims=True) for t in e]
    o = [_mm(pr[p], mv[:, p * 128:(p + 1) * 128]) for p in range(X_HEADS // 2)]
    return jnp.concatenate([jnp.where(low, t[:rows], t[rows:]) for t in o], axis=1)


def _swa_upper():
    qi = lax.broadcasted_iota(jnp.int32, (BLK, BLK), 0)
    kj = lax.broadcasted_iota(jnp.int32, (BLK, BLK), 1)
    return kj > qi


def _bias_build(rel_bias, bucket, name):
    def body(rb_ref, bucket_ref, o_ref):
        b = bucket_ref[...]
        for h in range(A_HEADS):
            acc = jnp.zeros((BLK, BLK), F32)
            for k in range(N_BUCKETS):
                acc = jnp.where(b == k, rb_ref[k, h], acc)
            o_ref[h] = acc

    return pl.pallas_call(
        body, in_specs=[pl.BlockSpec(memory_space=pltpu.SMEM), pl.BlockSpec(memory_space=pltpu.VMEM)],
        out_specs=pl.BlockSpec(memory_space=pltpu.VMEM),
        out_shape=_SDS((A_HEADS, BLK, BLK), F32), name=name)(rel_bias, bucket)


def _bias_reduce(dbias, bucket, name):
    def body(db_ref, bucket_ref, o_ref):
        b = bucket_ref[...]
        row = lax.broadcasted_iota(jnp.int32, (N_BUCKETS, 128), 0)
        lane = lax.broadcasted_iota(jnp.int32, (N_BUCKETS, 128), 1)
        acc = jnp.zeros((N_BUCKETS, 128), F32)
        for h in range(A_HEADS):
            v = db_ref[h]
            for k in range(N_BUCKETS):
                sk = jnp.sum(jnp.sum(jnp.where(b == k, v, 0.0), axis=1, keepdims=True), axis=0, keepdims=True)
                acc = acc + jnp.where((row == k) & (lane == h), sk, 0.0)
        o_ref[...] = acc

    return pl.pallas_call(
        body, in_specs=[pl.BlockSpec(memory_space=pltpu.VMEM)] * 2,
        out_specs=pl.BlockSpec(memory_space=pltpu.VMEM),
        out_shape=_SDS((N_BUCKETS, 128), F32), name=name)(dbias, bucket)


def _mix_a_fwd(proj, bias, sinks, memkv, name):
    s = proj.shape[0]
    per = 4
    nb = s // (per * BLK)
    grp = A_HEADS // A_KV_HEADS

    def body(proj_ref, prev_ref, bias_ref, sink_ref, memkv_ref, o_ref):
        i = pl.program_id(0)
        upper = _swa_upper()
        proj = proj_ref[...].astype(F32)
        kv = jnp.concatenate([prev_ref[...].astype(F32), proj[:, A_Q:A_Q + 2 * A_KV]], axis=0)
        k, v = kv[:, :A_KV], kv[:, A_KV:]
        k_r = pltpu.roll(k, A_HD, 1)
        v_r = pltpu.roll(v, A_HD, 1)
        gw = A_Q // A_KV_HEADS
        each = [(b, g) for b in range(per) for g in range(A_KV_HEADS)]

        def window(a, a_r, b, g):
            return _both_halves(a[b * BLK:(b + 2) * BLK], a_r[b * BLK:(b + 2) * BLK], g)

        swa, cross = _mix_a_core([proj[b * BLK:(b + 1) * BLK, g * gw:(g + 1) * gw] for b, g in each],
                                 [window(k, k_r, b, g) for b, g in each], [window(v, v_r, b, g) for b, g in each],
                                 [_swa_sinks(sink_ref, g) for b, g in each], [bias_ref[g * grp:(g + 1) * grp] for b, g in each],
                                 proj[:, A_Q + 2 * A_KV:], memkv_ref[:, :X_Q], memkv_ref[:, X_Q:], upper,
                                 [(i == 0) if b == 0 else None for b, g in each])
        for b in range(per):
            o_ref[b * BLK:(b + 1) * BLK, :] = jnp.concatenate(
                swa[b * A_KV_HEADS:(b + 1) * A_KV_HEADS] + [cross[b * BLK:(b + 1) * BLK]], axis=1).astype(o_ref.dtype)

    return pl.pallas_call(
        body, grid=(nb,),
        in_specs=[pl.BlockSpec((per * BLK, IN_A), lambda i: (i, 0)),
                  pl.BlockSpec((BLK, 2 * A_KV), lambda i: (jnp.maximum(per * i - 1, 0), A_Q // (2 * A_KV))),
                  pl.BlockSpec((A_HEADS, BLK, BLK), lambda i: (0, 0, 0)),
                  pl.BlockSpec((1, 128), lambda i: (0, 0)),
                  pl.BlockSpec((MEM_LEN, 2 * X_Q), lambda i: (0, 0))],
        out_specs=pl.BlockSpec((per * BLK, D), lambda i: (i, 0)),
        out_shape=_SDS((s, D), _ACT), name=name, compiler_params=_cp("arbitrary"))(proj, proj, bias, sinks, memkv)


def _mix_a_bwd(proj, bias, sinks, memkv, dmix, name, deps=()):
    s = proj.shape[0]
    per = 2
    nb = s // (per * BLK)
    grp = A_HEADS // A_KV_HEADS

    def body(proj_ref, prev_ref, bias_ref, sink_ref, memkv_ref, dmix_ref, *rest):
        dproj_ref, dbias_ref, dsink_ref, dmemkv_ref, carry_ref = rest[-5:]
        t = pl.program_id(0)
        i = nb - 1 - t

        @pl.when(t == 0)
        def _():
            carry_ref[...] = jnp.zeros_like(carry_ref)
            dbias_ref[...] = jnp.zeros_like(dbias_ref)
            dsink_ref[...] = jnp.zeros_like(dsink_ref)
            dmemkv_ref[...] = jnp.zeros_like(dmemkv_ref)

        upper = _swa_upper()
        lane = lax.broadcasted_iota(jnp.int32, (1, 128), 1)
        low = _lane_low()
        proj = proj_ref[...].astype(F32)
        kv = jnp.concatenate([prev_ref[...].astype(F32), proj[:, A_Q:A_Q + 2 * A_KV]], axis=0)
        k, v = kv[:, :A_KV], kv[:, A_KV:]
        k_r = pltpu.roll(k, A_HD, 1)
        v_r = pltpu.roll(v, A_HD, 1)
        gw = A_Q // A_KV_HEADS
        each = [(b, g) for b in range(per) for g in range(A_KV_HEADS)]

        def window(a, a_r, b, g):
            return _both_halves(a[b * BLK:(b + 2) * BLK], a_r[b * BLK:(b + 2) * BLK], g)

        _, vjp = jax.vjp(
            functools.partial(_mix_a_core, upper=upper, first=[(i == 0) if b == 0 else None for b, g in each]),
            [proj[b * BLK:(b + 1) * BLK, g * gw:(g + 1) * gw] for b, g in each],
            [window(k, k_r, b, g) for b, g in each], [window(v, v_r, b, g) for b, g in each],
            [_swa_sinks(sink_ref, g) for b, g in each], [bias_ref[g * grp:(g + 1) * grp] for b, g in each],
            proj[:, A_Q + 2 * A_KV:], memkv_ref[:, :X_Q], memkv_ref[:, X_Q:])
        dqs, dk, dv, ds, db, dxq, dmk, dmv = vjp(
            ([dmix_ref[b * BLK:(b + 1) * BLK, g * gw:(g + 1) * gw].astype(F32) for b, g in each], dmix_ref[:, A_Q:].astype(F32)))
        dkd = [t + pltpu.roll(t, A_HD, 1) for t in dk]
        dvd = [t + pltpu.roll(t, A_HD, 1) for t in dv]
        dsink = jnp.zeros((1, 128), F32)
        for e, (b, g) in enumerate(each):
            for h in range(grp):
                dsink = dsink + jnp.where(lane == g * grp + h, ds[e][h], 0.0)
        for g in range(A_KV_HEADS):
            dbias_ref[g * grp:(g + 1) * grp] += db[g] + db[A_KV_HEADS + g]
        dsink_ref[...] += dsink
        dmemkv_ref[...] += jnp.concatenate([dmk, dmv], axis=1)
        dkv = [jnp.concatenate([jnp.where(low, dkd[b * A_KV_HEADS], dkd[b * A_KV_HEADS + 1]),
                                jnp.where(low, dvd[b * A_KV_HEADS], dvd[b * A_KV_HEADS + 1])], axis=1) for b in range(per)]
        own = [dkv[0][BLK:] + dkv[1][:BLK], dkv[1][BLK:] + carry_ref[...]]
        carry_ref[...] = dkv[0][:BLK]
        for b in range(per):
            dproj_ref[b * BLK:(b + 1) * BLK, :] = jnp.concatenate(
                list(dqs[b * A_KV_HEADS:(b + 1) * A_KV_HEADS]) + [own[b], dxq[b * BLK:(b + 1) * BLK]], axis=1).astype(dproj_ref.dtype)

    return pl.pallas_call(
        body, grid=(nb,),
        in_specs=[pl.BlockSpec((per * BLK, IN_A), lambda t: (nb - 1 - t, 0)),
                  pl.BlockSpec((BLK, 2 * A_KV), lambda t: (jnp.maximum(per * (nb - 1 - t) - 1, 0), A_Q // (2 * A_KV))),
                  pl.BlockSpec((A_HEADS, BLK, BLK), lambda t: (0, 0, 0)),
                  pl.BlockSpec((1, 128), lambda t: (0, 0)),
                  pl.BlockSpec((MEM_LEN, 2 * X_Q), lambda t: (0, 0)),
                  pl.BlockSpec((per * BLK, D), lambda t: (nb - 1 - t, 0))] + _dep_specs(deps),
        out_specs=[pl.BlockSpec((per * BLK, IN_A), lambda t: (nb - 1 - t, 0)),
                   pl.BlockSpec((A_HEADS, BLK, BLK), lambda t: (0, 0, 0)),
                   pl.BlockSpec((1, 128), lambda t: (0, 0)),
                   pl.BlockSpec((MEM_LEN, 2 * X_Q), lambda t: (0, 0))],
        out_shape=[_SDS((s, IN_A), _ACT), _SDS((A_HEADS, BLK, BLK), F32), _SDS((1, 128), F32),
                   _SDS((MEM_LEN, 2 * X_Q), F32)],
        scratch_shapes=[pltpu.VMEM((BLK, 2 * A_KV), F32)],
        name=name, compiler_params=_cp("arbitrary"))(proj, proj, bias, sinks, memkv, dmix, *deps)


def _neumann(pw, rhs):
    nh = len(pw)
    x = rhs
    for lvl in range(6):
        if lvl < 5:
            prod = [_mmf(pw[h], jnp.concatenate([x[h], pw[h]], axis=1)) for h in range(nh)]
            x = [x[h] + prod[h][:, :B_HD] for h in range(nh)]
            pw = [t[:, B_HD:] for t in prod]
        else:
            x = [x[h] + _mmf(pw[h], x[h]) for h in range(nh)]
    return x


@jax.custom_vjp
def _tri_solve(pw, rhs):
    return _neumann(pw, rhs)


def _tri_solve_fwd(pw, rhs):
    x = _neumann(pw, rhs)
    return x, (pw, x)


def _tri_solve_bwd(res, dx):
    pw, x = res
    d_rhs = _neumann([t.T for t in pw], list(dx))
    return [_mmf_nt(d_rhs[h], x[h]) for h in range(len(pw))], d_rhs


_tri_solve.defvjp(_tri_solve_fwd, _tri_solve_bwd)


@jax.custom_vjp
def _tri_solved(pw, rhs, x):
    return x


def _tri_solved_fwd(pw, rhs, x):
    return x, (pw, x)


def _tri_solved_bwd(res, dx):
    d_pw, d_rhs = _tri_solve_bwd(res, dx)
    return d_pw, d_rhs, [jnp.zeros_like(t) for t in res[1]]


_tri_solved.defvjp(_tri_solved_fwd, _tri_solved_bwd)


@jax.custom_vjp
def _known(x, value):
    return value


def _known_fwd(x, value):
    return value, None


def _known_bwd(_, g):
    return g, jnp.zeros_like(g)


_known.defvjp(_known_fwd, _known_bwd)


def _dn_heads(yq, yk, yv, z, bl, al, a_log, dtb, ng, s0, solved=None, out_known=None):
    c = CHUNK
    nh = B_V_HEADS
    rep = B_V_HEADS // B_QK_HEADS
    r = lax.broadcasted_iota(jnp.int32, (c, c), 0)
    cc = lax.broadcasted_iota(jnp.int32, (c, c), 1)
    q = [_silu(t) for t in yq]
    k = [_silu(t) for t in yk]
    v = [_silu(t) for t in yv]
    q = [t * lax.rsqrt(jnp.sum(t * t, axis=-1, keepdims=True) + EPS) * (B_HD ** -0.5) for t in q]
    k = [t * lax.rsqrt(jnp.sum(t * t, axis=-1, keepdims=True) + EPS) for t in k]
    beta = [jax.nn.sigmoid(t) for t in bl]
    g = [-jnp.exp(a_log[h]) * jax.nn.softplus(al[h] + dtb[h]) for h in range(nh)]
    gb = [jnp.broadcast_to(t, (c, c)) for t in g]
    gc_col = [jnp.sum(jnp.where(cc <= r, t.T, 0.0), axis=1, keepdims=True) for t in gb]
    gc_row = [jnp.sum(jnp.where(r <= cc, t, 0.0), axis=0, keepdims=True) for t in gb]
    gc_last = [jnp.sum(t, axis=0, keepdims=True) for t in g]
    decay = [jnp.exp(jnp.where(r >= cc, gc_col[h] - gc_row[h], -jnp.inf)) for h in range(nh)]
    kq = [_mmf_nt(jnp.concatenate([k[h], q[h]], axis=0), k[h]) for h in range(B_QK_HEADS)]
    kk = [t[:c] for t in kq]
    qk = [t[c:] for t in kq]
    egc = [jnp.exp(t) for t in gc_col]
    both = [_mmf(jnp.concatenate([(beta[h] * egc[h]) * k[h // rep], q[h // rep] * egc[h]], axis=0), s0[h]) for h in range(nh)]
    rhs = [beta[h] * v[h] - both[h][:c] for h in range(nh)]
    qs0 = [t[c:] for t in both]
    pw = [-(beta[h] * kk[h // rep] * jnp.where(r > cc, decay[h], 0.0)) for h in range(nh)]
    delta = _tri_solve(pw, rhs) if solved is None else _tri_solved(pw, rhs, solved)
    last = [_mmf(jnp.concatenate([qk[h // rep] * decay[h], (k[h // rep] * jnp.exp(gc_last[h] - gc_col[h])).T], axis=0), delta[h])
            for h in range(nh)]
    out = [qs0[h] + last[h][:c] for h in range(nh)]
    if out_known is not None:
        out = [_known(out[h], out_known[h]) for h in range(nh)]
    s1 = [jnp.exp(gc_last[h]) * s0[h] + last[h][c:] for h in range(nh)]
    o = [t * lax.rsqrt(jnp.mean(t * t, axis=-1, keepdims=True) + EPS) * ng for t in out]
    return [o[h] * _silu(z[h]) for h in range(nh)], s1, delta, out


def _dn_conv(ext, w_ref):
    y = ext * w_ref[B_CONV - 1:B_CONV, :]
    for j in range(B_CONV - 1):
        y = y + w_ref[j:j + 1, :] * pltpu.roll(ext, B_CONV - 1 - j, 0)
    return y


def _dn_args(y, cur_ref, par_ref, ng_ref):
    nh = B_V_HEADS
    return ([y[:, h * B_HD:(h + 1) * B_HD] for h in range(B_QK_HEADS)],
            [y[:, B_QK + h * B_HD:B_QK + (h + 1) * B_HD] for h in range(B_QK_HEADS)],
            [y[:, 2 * B_QK + h * B_HD:2 * B_QK + (h + 1) * B_HD] for h in range(nh)],
            [cur_ref[:, BP_Z + h * B_HD:BP_Z + (h + 1) * B_HD] for h in range(nh)],
            [cur_ref[:, BP_GATE + h:BP_GATE + h + 1] for h in range(nh)],
            [cur_ref[:, BP_GATE + nh + h:BP_GATE + nh + h + 1] for h in range(nh)],
            [par_ref[:, h:h + 1] for h in range(nh)], [par_ref[:, nh + h:nh + h + 1] for h in range(nh)], ng_ref[...])


def _mix_b_fwd(proj, conv_w, par, ng, memkv, name):
    s = proj.shape[0]
    nc = s // CHUNK

    def body(cur_ref, prev_ref, w_ref, par_ref, ng_ref, memkv_ref, o_ref, st_ref, dl_ref, state_ref):
        n = pl.program_id(0)

        @pl.when(n == 0)
        def _():
            state_ref[...] = jnp.zeros_like(state_ref)

        prev = jnp.where(n > 0, prev_ref[...], 0.0)
        ext = jnp.concatenate([prev, cur_ref[:, :B_QKV]], axis=0)
        y = _dn_conv(ext, w_ref)[HALO:]
        s0 = [state_ref[hv] for hv in range(B_V_HEADS)]
        st_ref[0] = state_ref[...]
        outs, s1, delta, raw = _dn_heads(*_dn_args(y, cur_ref, par_ref, ng_ref), s0)
        for hv in range(B_V_HEADS):
            state_ref[hv] = s1[hv]
            dl_ref[0, hv] = delta[hv]
            dl_ref[0, B_V_HEADS + hv] = raw[hv]
        outs = outs + [_cross_pairs(cur_ref[:, BP_XQ:BP_XQ + X_Q], memkv_ref[:, :X_Q], memkv_ref[:, X_Q:])]
        o_ref[...] = jnp.concatenate(outs, axis=1).astype(o_ref.dtype)

    return pl.pallas_call(
        body, grid=(nc,),
        in_specs=[pl.BlockSpec((CHUNK, IN_BP), lambda n: (n, 0)),
                  pl.BlockSpec((HALO, B_QKV), lambda n: (jnp.maximum(n * (CHUNK // HALO) - 1, 0), 0)),
                  pl.BlockSpec((HALO, B_QKV), lambda n: (0, 0)),
                  pl.BlockSpec((1, 128), lambda n: (0, 0)), pl.BlockSpec((1, 128), lambda n: (0, 0)),
                  pl.BlockSpec((MEM_LEN, 2 * X_Q), lambda n: (0, 0))],
        out_specs=[pl.BlockSpec((CHUNK, D), lambda n: (n, 0)),
                   pl.BlockSpec((1, B_V_HEADS, B_HD, B_HD), lambda n: (n, 0, 0, 0)),
                   pl.BlockSpec((1, 2 * B_V_HEADS, CHUNK, B_HD), lambda n: (n, 0, 0, 0))],
        out_shape=[_SDS((s, D), _ACT), _SDS((nc, B_V_HEADS, B_HD, B_HD), F32), _SDS((nc, 2 * B_V_HEADS, CHUNK, B_HD), F32)],
        scratch_shapes=[pltpu.VMEM((B_V_HEADS, B_HD, B_HD), F32)],
        name=name, compiler_params=_cp("arbitrary"))(proj, proj, conv_w, par, ng, memkv)


def _mix_b_bwd(proj, conv_w, par, ng, memkv, states, deltas, dmix, name):
    s = proj.shape[0]
    nc = s // CHUNK
    ext_rows = CHUNK + HALO

    def body(cur_ref, prev_ref, w_ref, par_ref, ng_ref, memkv_ref, st_ref, dl_ref, dmix_ref,
             dproj_ref, dw_ref, dpar_ref, dng_ref, dmemkv_ref, dstate_ref, carry_ref):
        t = pl.program_id(0)
        n = nc - 1 - t

        @pl.when(t == 0)
        def _():
            dstate_ref[...] = jnp.zeros_like(dstate_ref)
            carry_ref[...] = jnp.zeros_like(carry_ref)
            dw_ref[...] = jnp.zeros_like(dw_ref)
            dpar_ref[...] = jnp.zeros_like(dpar_ref)
            dng_ref[...] = jnp.zeros_like(dng_ref)
            dmemkv_ref[...] = jnp.zeros_like(dmemkv_ref)

        lane = lax.broadcasted_iota(jnp.int32, (1, 128), 1)
        prev = jnp.where(n > 0, prev_ref[...], 0.0)
        ext = jnp.concatenate([prev, cur_ref[:, :B_QKV]], axis=0)
        y = _dn_conv(ext, w_ref)[HALO:]
        solved = [dl_ref[0, hv] for hv in range(B_V_HEADS)]
        raw = [dl_ref[0, B_V_HEADS + hv] for hv in range(B_V_HEADS)]
        _, vjp = jax.vjp(functools.partial(_dn_heads, solved=solved, out_known=raw), *_dn_args(y, cur_ref, par_ref, ng_ref),
                         [st_ref[0, hv] for hv in range(B_V_HEADS)])
        none = [jnp.zeros((CHUNK, B_HD), F32)] * B_V_HEADS
        dyq, dyk, dyv, dz, gbl, gal, ga_log, gdtb, dng, gs0 = vjp(
            ([dmix_ref[:, hv * B_HD:(hv + 1) * B_HD].astype(F32) for hv in range(B_V_HEADS)],
             [dstate_ref[hv] for hv in range(B_V_HEADS)], none, none))
        dgate = jnp.zeros((CHUNK, 128), F32)
        dpar = jnp.zeros((1, 128), F32)
        for hv in range(B_V_HEADS):
            dstate_ref[hv] = gs0[hv]
            dgate = dgate + jnp.where(lane == hv, gbl[hv], 0.0) + jnp.where(lane == B_V_HEADS + hv, gal[hv], 0.0)
            dpar = dpar + jnp.where(lane == hv, ga_log[hv], 0.0) + jnp.where(lane == B_V_HEADS + hv, gdtb[hv], 0.0)
        dpar_ref[...] += dpar
        dng_ref[...] += dng
        _, vjp = jax.vjp(_cross_pairs, cur_ref[:, BP_XQ:BP_XQ + X_Q], memkv_ref[:, :X_Q], memkv_ref[:, X_Q:])
        dxq, dmk, dmv = vjp(dmix_ref[:, B_V:].astype(F32))
        dmemkv_ref[...] += jnp.concatenate([dmk, dmv], axis=1)
        dy = jnp.concatenate(list(dyq) + list(dyk) + list(dyv), axis=1)
        dy_ext = jnp.concatenate([jnp.zeros((HALO, B_QKV), F32), dy], axis=0)
        dext = dy_ext * w_ref[B_CONV - 1:B_CONV, :]
        dw_ref[B_CONV - 1:B_CONV, :] += jnp.sum(ext * dy_ext, axis=0, keepdims=True)
        for j in range(B_CONV - 1):
            sh = B_CONV - 1 - j
            dw_ref[j:j + 1, :] += jnp.sum(pltpu.roll(ext, sh, 0) * dy_ext, axis=0, keepdims=True)
            dext = dext + w_ref[j:j + 1, :] * pltpu.roll(dy_ext, ext_rows - sh, 0)
        tail = jnp.concatenate([jnp.zeros((CHUNK - HALO, B_QKV), F32), carry_ref[...]], axis=0)
        dqkv = dext[HALO:] + tail
        carry_ref[...] = dext[:HALO]
        dproj_ref[...] = jnp.concatenate([dqkv] + list(dz) + [dxq, dgate], axis=1).astype(dproj_ref.dtype)

    return pl.pallas_call(
        body, grid=(nc,),
        in_specs=[pl.BlockSpec((CHUNK, IN_BP), lambda t: (nc - 1 - t, 0)),
                  pl.BlockSpec((HALO, B_QKV), lambda t: (jnp.maximum((nc - 1 - t) * (CHUNK // HALO) - 1, 0), 0)),
                  pl.BlockSpec((HALO, B_QKV), lambda t: (0, 0)),
                  pl.BlockSpec((1, 128), lambda t: (0, 0)), pl.BlockSpec((1, 128), lambda t: (0, 0)),
                  pl.BlockSpec((MEM_LEN, 2 * X_Q), lambda t: (0, 0)),
                  pl.BlockSpec((1, B_V_HEADS, B_HD, B_HD), lambda t: (nc - 1 - t, 0, 0, 0)),
                  pl.BlockSpec((1, 2 * B_V_HEADS, CHUNK, B_HD), lambda t: (nc - 1 - t, 0, 0, 0)),
                  pl.BlockSpec((CHUNK, D), lambda t: (nc - 1 - t, 0))],
        out_specs=[pl.BlockSpec((CHUNK, IN_BP), lambda t: (nc - 1 - t, 0)),
                   pl.BlockSpec((HALO, B_QKV), lambda t: (0, 0)),
                   pl.BlockSpec((1, 128), lambda t: (0, 0)), pl.BlockSpec((1, 128), lambda t: (0, 0)),
                   pl.BlockSpec((MEM_LEN, 2 * X_Q), lambda t: (0, 0))],
        out_shape=[_SDS((s, IN_BP), _ACT), _SDS((HALO, B_QKV), F32), _SDS((1, 128), F32), _SDS((1, 128), F32),
                   _SDS((MEM_LEN, 2 * X_Q), F32)],
        scratch_shapes=[pltpu.VMEM((B_V_HEADS, B_HD, B_HD), F32), pltpu.VMEM((HALO, B_QKV), F32)],
        name=name, compiler_params=_cp("arbitrary"))(proj, proj, conv_w, par, ng, memkv, states, deltas, dmix)


def _place():
    return lax.axis_index("x"), lax.axis_index("y"), lax.axis_index("c")


def _two_level_gather(ins, outs, sems, local_sems):
    n = len(ins)
    x, y, c = _place()
    me, sibling = (x, y, c), (x, y, 1 - c)
    chips = [(1 - x, y), (x, 1 - y), (1 - x, 1 - y)]

    def rows(a, px, py, pc):
        return outs[a].at[4 * px + 2 * py + pc]

    def copy(a, k, block, to, src=None):
        return pltpu.make_async_remote_copy(
            src_ref=rows(a, *block) if src is None else src, dst_ref=rows(a, *block),
            device_id=to, device_id_type=pl.DeviceIdType.MESH, **sems(a, k))

    mine = [pltpu.make_async_copy(ins[a], rows(a, *me), local_sems.at[a]) for a in range(n)]
    for cp in mine:
        cp.start()
    first = []
    for a in range(n):
        first.append(copy(a, 0, me, sibling, src=ins[a]))
        first += [copy(a, 1 + j, me, (*chip, c), src=ins[a]) for j, chip in enumerate(chips)]
    for cp in first:
        cp.start()
    passed = []
    for j, chip in enumerate(chips):
        for a in range(n):
            copy(a, 1 + j, (*chip, c), me).wait_recv()
            fwd = copy(a, 4 + j, (*chip, c), sibling)
            fwd.start()
            passed.append(fwd)
    for a in range(n):
        copy(a, 0, sibling, me).wait_recv()
        for j, chip in enumerate(chips):
            copy(a, 4 + j, (*chip, 1 - c), me).wait_recv()
    for cp in first + passed:
        cp.wait_send()
    for cp in mine:
        cp.wait()


class _Exchange:
    def __init__(self, lands, srcs):
        self.lands, self.srcs = lands, srcs


def _seq_exchange(srcs, land_shapes, plan, name, cid):
    n, nl = len(srcs), len(land_shapes)

    def launch(*refs):
        src_refs, land_refs = refs[:n], refs[n:n + nl]
        send_sems, recv_sems, local_sems = refs[n + nl:]
        x, y, c = _place()
        my = 4 * x + 2 * y + c
        peers = [(x ^ ((k + 1) >> 2 & 1), y ^ ((k + 1) >> 1 & 1), c ^ ((k + 1) & 1)) for k in range(N_DEV - 1)]
        barrier = pltpu.get_barrier_semaphore()
        for p in peers:
            pl.semaphore_signal(barrier, inc=1, device_id=p, device_id_type=pl.DeviceIdType.MESH)
        pl.semaphore_wait(barrier, N_DEV - 1)

        def src_for(a, dest):
            return src_refs[a].at[dest] if plan[a][1] else src_refs[a]

        def slot(a, source):
            return land_refs[plan[a][0]].at[source]

        mine = [pltpu.make_async_copy(src_for(a, my), slot(a, my), local_sems.at[a]) for a in range(n)]
        for cp in mine:
            cp.start()
        sends, recvs = [], []
        for k, (px, py, pc) in enumerate(peers):
            peer = 4 * px + 2 * py + pc
            for a in range(n):
                kw = dict(send_sem=send_sems.at[a * (N_DEV - 1) + k], recv_sem=recv_sems.at[a * (N_DEV - 1) + k],
                          device_id=(px, py, pc), device_id_type=pl.DeviceIdType.MESH)
                sends.append(pltpu.make_async_remote_copy(src_ref=src_for(a, peer), dst_ref=slot(a, my), **kw))
                recvs.append(pltpu.make_async_remote_copy(src_ref=src_for(a, my), dst_ref=slot(a, peer), **kw))
        for cp in sends:
            cp.start()
        for cp in recvs:
            cp.wait_recv()
        for cp in sends:
            cp.wait_send()
        for cp in mine:
            cp.wait()

    lands = pl.kernel(
        launch, out_type=[_SDS(s, d) for s, d in land_shapes],
        mesh=plsc.ScalarSubcoreMesh(axis_name="sequencer", num_cores=1), name=name,
        scratch_types=(pltpu.SemaphoreType.DMA((n * (N_DEV - 1),)), pltpu.SemaphoreType.DMA((n * (N_DEV - 1),)),
                       pltpu.SemaphoreType.DMA((n,))),
        compiler_params=pltpu.CompilerParams(collective_id=cid))(*srcs)
    return _Exchange(list(lands), list(srcs))


def _seq_gather(srcs, name, cid):
    n = len(srcs)

    def launch(*refs):
        send_sems, recv_sems, local_sems = refs[2 * n:]
        x, y, c = _place()
        barrier = pltpu.get_barrier_semaphore()
        for k in range(1, N_DEV):
            pl.semaphore_signal(barrier, inc=1, device_id=(x ^ (k >> 2 & 1), y ^ (k >> 1 & 1), c ^ (k & 1)),
                                device_id_type=pl.DeviceIdType.MESH)
        pl.semaphore_wait(barrier, N_DEV - 1)
        at = lambda a, k: a * (N_DEV - 1) + k
        _two_level_gather(refs[:n], refs[n:2 * n], lambda a, k: dict(send_sem=send_sems.at[at(a, k)], recv_sem=recv_sems.at[at(a, k)]),
                          local_sems)

    lands = pl.kernel(
        launch, out_type=[_SDS((N_DEV,) + s.shape, s.dtype) for s in srcs],
        mesh=plsc.ScalarSubcoreMesh(axis_name="sequencer", num_cores=1), name=name,
        scratch_types=(pltpu.SemaphoreType.DMA((n * (N_DEV - 1),)), pltpu.SemaphoreType.DMA((n * (N_DEV - 1),)),
                       pltpu.SemaphoreType.DMA((n,))),
        compiler_params=pltpu.CompilerParams(collective_id=cid))(*srcs)
    return _Exchange(list(lands), list(srcs))


def _adam_update(g, w, m, v):
    c1 = 1.0 - ADAM_B1 ** ADAM_STEP
    c2 = 1.0 - ADAM_B2 ** ADAM_STEP
    mm = ADAM_B1 * m + (1.0 - ADAM_B1) * g
    vv = ADAM_B2 * v + (1.0 - ADAM_B2) * (g * g)
    delta = -ADAM_LR * ((mm / c1) / (jnp.sqrt(vv / c2) + ADAM_EPS) + ADAM_WD * w)
    return delta, mm, vv


def _sum_sources(p_ref):
    g = p_ref[0].astype(F32)
    for s in range(1, N_DEV):
        g = g + p_ref[s].astype(F32)
    return g


def _adamw(parts, w, m, v, tr, name, restore_b=False, deps=()):
    nl, r, c = w.shape
    cp = parts[0].shape[-1]

    def body(*refs):
        p_refs = refs[:nl]
        w_ref, m_ref, v_ref = refs[nl:nl + 3]
        g_ref, d_ref, nm_ref, nv_ref = refs[-4:]
        g = _sum_sources(p_refs[0])
        for l in range(1, nl):
            g = jnp.where(pl.program_id(0) == l, _sum_sources(p_refs[l]), g)
        if restore_b:
            g = jnp.concatenate([g[:, :BP_XQ], g[:, BP_GATE:BP_GATE + 2 * B_V_HEADS], g[:, BP_XQ:BP_GATE]], axis=1)
        delta, mm, vv = _adam_update(g, w_ref[...], m_ref[...], v_ref[...])
        g_ref[...] = g
        d_ref[...] = delta
        nm_ref[...] = mm
        nv_ref[...] = vv

    spec = pl.BlockSpec((None, tr, c), lambda l, i: (l, i, 0))
    part_specs = [pl.BlockSpec((N_DEV, tr, cp), functools.partial(lambda l, i, k: (0, jnp.where(l == k, i, 0), 0), k=k))
                  for k in range(nl)]
    return pl.pallas_call(
        body, grid=(nl, r // tr),
        in_specs=part_specs + [spec, spec, spec] + _dep_specs(deps),
        out_specs=[spec] * 4, out_shape=[_SDS(w.shape, F32)] * 4,
        name=name, compiler_params=_cp("arbitrary", "arbitrary"))(*parts, w, m, v, *deps)


def _pack_small(d_rel, d_cb, d_cw, d_qkv, d_mix, d_mem, d_ffn, d_final, d_sinks, d_par, d_ng, loss_row, name):
    flat = [d_rel, *d_cb, *d_cw, d_qkv, *d_mix, *d_mem, *d_ffn, d_final, d_sinks, d_par, d_ng, loss_row]
    n = len(flat)

    def body(*refs):
        ins, o_ref = refs[:n], refs[n]
        rel, cb0, cb1, cw0, cw1, qkv, mx0, mx1, me0, me1, ff0, ff1, fin, snk, par, ng, lss = ins
        o_ref[...] = jnp.zeros_like(o_ref)
        for k in range(N_BUCKETS):
            lane = SP_REL_LANE + 128 * (k % 8)
            o_ref[SP_QKV + k // 8:SP_QKV + k // 8 + 1, lane:lane + 128] = rel[k:k + 1, :]
        for l, (cb, cw) in enumerate(((cb0, cw0), (cb1, cw1))):
            o_ref[SP_CB + l:SP_CB + l + 1, :] = jnp.concatenate([cb[j] for j in range(FF_BLOCKS)], axis=1)
            full = jnp.concatenate([cw[j] for j in range(FF_BLOCKS)], axis=1)
            o_ref[SP_CW + FFN_CONV * l:SP_CW + FFN_CONV * (l + 1), :] = full[:FFN_CONV]
        o_ref[SP_QKV:SP_QKV + B_CONV, 0:B_QKV] = qkv[0:B_CONV, :]
        for base, pair in ((SP_MIX, (mx0, mx1)), (SP_MEM, (me0, me1)), (SP_FFN, (ff0, ff1))):
            for l in range(2):
                o_ref[base + l:base + l + 1, 0:D] = pair[l][...]
        o_ref[SP_FINAL:SP_FINAL + 1, 0:D] = fin[...]
        o_ref[SP_MISC:SP_MISC + 1, 0:128] = snk[...]
        o_ref[SP_MISC:SP_MISC + 1, 128:256] = par[...]
        o_ref[SP_MISC:SP_MISC + 1, 256:384] = ng[...]
        o_ref[SP_MISC:SP_MISC + 1, 384:512] = lss[...]

    vm = pl.BlockSpec(memory_space=pltpu.VMEM)
    return pl.pallas_call(body, in_specs=[vm] * n, out_specs=vm, out_shape=_SDS((SMALL_ROWS, D_FF), F32), name=name)(*flat)


_SMALL = ["rel_bias", "norm_mix_g", "norm_mem_g", "sinks_a", "a_log_b", "dt_bias_b", "out_norm_g_b", "norm_ffn_g",
          "ffn_conv_b", "final_norm_g", "conv_qkv_b", "ffn_conv_w"]


def _adamw_small(recv, rc_qkv, rc_ffn, ws, ms, vs, name, deps=()):
    n = len(_SMALL)

    def body(*refs):
        recv_ref, qkv_ref, ffn_ref = refs[:3]
        w_refs, m_refs, v_refs = refs[3:3 + n], refs[3 + n:3 + 2 * n], refs[3 + 2 * n:3 + 3 * n]
        outs, loss_ref = refs[len(refs) - 4 * n - 1:len(refs) - 1], refs[-1]
        gs = _sum_sources(recv_ref)
        loss_ref[...] = gs[SP_MISC:SP_MISC + 1, 384:512]
        grads = {
            "rel_bias": jnp.concatenate(
                [gs[SP_QKV + k // 8:SP_QKV + k // 8 + 1, SP_REL_LANE + 128 * (k % 8):SP_REL_LANE + 128 * (k % 8) + A_HEADS]
                 for k in range(N_BUCKETS)], axis=0),
            "norm_mix_g": gs[SP_MIX:SP_MIX + 2, 0:D], "norm_mem_g": gs[SP_MEM:SP_MEM + 2, 0:D],
            "sinks_a": gs[SP_MISC:SP_MISC + 1, 0:A_HEADS],
            "a_log_b": gs[SP_MISC:SP_MISC + 1, 128:128 + B_V_HEADS],
            "dt_bias_b": gs[SP_MISC:SP_MISC + 1, 128 + B_V_HEADS:128 + 2 * B_V_HEADS],
            "out_norm_g_b": gs[SP_MISC:SP_MISC + 1, 256:256 + B_HD],
            "norm_ffn_g": gs[SP_FFN:SP_FFN + 2, 0:D], "ffn_conv_b": gs[SP_CB:SP_CB + 2, :],
            "final_norm_g": gs[SP_FINAL:SP_FINAL + 1, 0:D],
            "conv_qkv_b": _sum_sources(qkv_ref), "ffn_conv_w": _sum_sources(ffn_ref),
        }
        for i, nm in enumerate(_SMALL):
            g = grads[nm]
            delta, mm, vv = _adam_update(g, w_refs[i][...], m_refs[i][...], v_refs[i][...])
            outs[i][...] = g
            outs[n + i][...] = delta
            outs[2 * n + i][...] = mm
            outs[3 * n + i][...] = vv

    vm = pl.BlockSpec(memory_space=pltpu.VMEM)
    shapes = [_SDS(w.shape, F32) for w in ws]
    return pl.pallas_call(
        body, in_specs=[vm] * (3 + 3 * n) + _dep_specs(deps), out_specs=[vm] * (4 * n + 1),
        out_shape=shapes * 4 + [_SDS((1, 128), F32)],
        name=name)(recv, rc_qkv, rc_ffn, *ws, *ms, *vs, *deps)


def _assemble(gathered, axis):
    g = jnp.moveaxis(gathered, 0, axis)
    shp = list(g.shape)
    return g.reshape(shp[:axis] + [shp[axis] * shp[axis + 1]] + shp[axis + 2:])


def _pad_rows(a, rows):
    return jnp.pad(a, ((0, rows - a.shape[0]), (0, 0)))


def _pad_lanes(a, lanes=128):
    return jnp.pad(a, ((0, 0), (0, lanes - a.shape[1])))


def _ff_blocks(a):
    return jnp.moveaxis(a.reshape(a.shape[0], FF_BLOCKS, GU_SHARD), 1, 0)


def _reorder_b(w):
    qkv_z = w[..., :B_QKV + B_V]
    gates = w[..., B_QKV + B_V:B_QKV + B_V + 2 * B_V_HEADS]
    xq = w[..., IN_B - X_Q:]
    pad = jnp.zeros(w.shape[:-1] + (IN_BP - IN_B,), w.dtype)
    return jnp.concatenate([qkv_z, xq, gates, pad], axis=-1)


def kernel(x, mem, rel_bias, norm_mix_g, norm_mem_g, w_mem_kv, w_out, w_in_a, sinks_a, w_in_b, conv_qkv_b, a_log_b, dt_bias_b, out_norm_g_b, norm_ffn_g, w_gate_up, ffn_conv_w, ffn_conv_b, w_down, final_norm_g, loss_target, m_rel_bias, m_norm_mix_g, m_norm_mem_g, m_w_mem_kv, m_w_out, m_w_in_a, m_sinks_a, m_w_in_b, m_conv_qkv_b, m_a_log_b, m_dt_bias_b, m_out_norm_g_b, m_norm_ffn_g, m_w_gate_up, m_ffn_conv_w, m_ffn_conv_b, m_w_down, m_final_norm_g, v_rel_bias, v_norm_mix_g, v_norm_mem_g, v_w_mem_kv, v_w_out, v_w_in_a, v_sinks_a, v_w_in_b, v_conv_qkv_b, v_a_log_b, v_dt_bias_b, v_out_norm_g_b, v_norm_ffn_g, v_w_gate_up, v_ffn_conv_w, v_ffn_conv_b, v_w_down, v_final_norm_g):
    local = dict(locals())
    order = ["rel_bias", "norm_mix_g", "norm_mem_g", "w_mem_kv", "w_out", "w_in_a", "sinks_a", "w_in_b", "conv_qkv_b",
             "a_log_b", "dt_bias_b", "out_norm_g_b", "norm_ffn_g", "w_gate_up", "ffn_conv_w", "ffn_conv_b", "w_down",
             "final_norm_g"]
    wts = {n: local[n] for n in order}
    moms = {n: local["m_" + n] for n in order}
    vars_ = {n: local["v_" + n] for n in order}
    h0 = x[0]
    memx = mem[0]
    tgt = loss_target[0]
    s = h0.shape[0]
    tm = _rows(s)
    tb = min(s, _TM_BIG)

    t_ = lambda a: jnp.swapaxes(a, 1, 2)
    g_mk0, g_out0, g_ia, g_cq, g_cw = _seq_gather(
        [w_mem_kv[0:1].astype(_MXU), w_out[0:1].astype(_MXU), t_(w_in_a).astype(_MXU), conv_qkv_b, ffn_conv_w], "gather_first", 9).lands
    g_mk, g_out = [g_mk0], [g_out0]
    def after(a, b):
        return a + (b[(0,) * b.ndim] * 0).astype(a.dtype)

    gu0_w = _seq_gather([after(t_(w_gate_up)[0].astype(_MXU), g_ia)], "gather_gate_up0", 1)
    dn0_w = _seq_gather([after(w_down[0].astype(_MXU), g_ia)], "gather_down0", 8)
    w_ia = g_ia.reshape(IN_A, D)
    conv_qkv = _pad_rows(_assemble(g_cq, 2)[0], HALO)
    ffn_cw_full = _assemble(g_cw, 2)
    ffn_cw = [_ff_blocks(_pad_rows(ffn_cw_full[i], HALO)) for i in range(2)]
    ffn_cb = [_ff_blocks(ffn_conv_b[i:i + 1]) for i in range(2)]
    bucket = jnp.asarray(_bucket_table())
    bias = _bias_build(rel_bias, bucket, "bias_build")
    sinks = _pad_lanes(sinks_a)
    par_b = _pad_lanes(jnp.concatenate([a_log_b, dt_bias_b], axis=1))

    row_x = pl.BlockSpec((tm, D), lambda i, j: (i, 0))
    gu_shape = (2, FF_BLOCKS, s, GU_SHARD)

    def in_proj(h, g, w, w_spec, n_cols, tn, name, deps=(), out_dtype=F32, w_t=False, tm=None):
        return _norm_matmul(h, g, w, w_spec, n_cols // tn, (h.shape[0], n_cols),
                            pl.BlockSpec((tm or _rows(h.shape[0]), tn), lambda i, j: (i, j)), name, deps=deps, out_dtype=out_dtype,
                            w_t=w_t, tm=tm)

    def ffn_fwd(i, h, g_gu, g_dn, deps=()):
        gu, hn = _norm_matmul(h, norm_ffn_g[i:i + 1], g_gu, _spec_gate_up(1), N_DEV, gu_shape,
                              _spec_gu_act(0, 1, tb), f"gate_up_{i}", deps=deps, out_dtype=_ACT, w_t=True, tm=tb)
        h_new, act, gc = _glu_down(gu, ffn_cw[i], ffn_cb[i], g_dn, h, f"glu_down_{i}")
        return h_new, gu, hn, (act, gc)

    def out_proj(i, mix, h):
        return _matmul_res(mix, row_x, g_out[i], _spec_rowsharded(0, D // N_DEV, D), 1, h, f"out_proj_{i}")

    proj_a, hn_a = in_proj(h0, norm_mix_g[0:1], w_ia, pl.BlockSpec((640, D), lambda i, j: (j, 0)), IN_A, 640, "in_proj_a",
                           deps=gu0_w.srcs + dn0_w.srcs, out_dtype=_ACT, w_t=True)
    memkv0, memn0 = in_proj(memx, norm_mem_g[0:1], g_mk[0], _spec_rowsharded(0, D // N_DEV, 2 * X_Q), 2 * X_Q, 2 * X_Q, "mem_proj_0")
    mix_a = _mix_a_fwd(proj_a, bias, sinks, memkv0, "mix_a_fwd")
    h1 = out_proj(0, mix_a, h0)
    g_gu0, g_dn0 = gu0_w.lands[0], dn0_w.lands[0]
    in_b_w = _seq_gather([after(_reorder_b(w_in_b).astype(_MXU), h1), after(w_mem_kv[1:2].astype(_MXU), h1)], "gather_in_b", 2)
    ffn1_w = _seq_gather([after(t_(w_gate_up)[1].astype(_MXU), h1), after(w_down[1].astype(_MXU), h1),
                          after(w_out[1:2].astype(_MXU), h1)], "gather_ffn1", 3)
    h2, gu0, hn_f0, act0 = ffn_fwd(0, h1, g_gu0, g_dn0, deps=in_b_w.srcs + ffn1_w.srcs)
    g_ib, g_mk1 = in_b_w.lands
    g_gu1, g_dn1, g_out1 = ffn1_w.lands
    g_mk.append(g_mk1)
    g_out.append(g_out1)
    proj_b, hn_b = in_proj(h2, norm_mix_g[1:2], g_ib, _spec_rowsharded(0, D // N_DEV, 896, col_block=1), IN_BP, 896, "in_proj_b")
    memkv1, memn1 = in_proj(memx, norm_mem_g[1:2], g_mk[1], _spec_rowsharded(0, D // N_DEV, 2 * X_Q), 2 * X_Q, 2 * X_Q, "mem_proj_1",
                            deps=[h2])
    mix_b, states, deltas = _mix_b_fwd(proj_b, conv_qkv, par_b, out_norm_g_b, memkv1, "mix_b_fwd")
    h3 = out_proj(1, mix_b, h2)
    h4, gu1, hn_f1, act1 = ffn_fwd(1, h3, g_gu1, g_dn1)
    loss_row, *dh, d_final_g = _loss_head(h4, final_norm_g[None, :], tgt, "loss_head")

    zeros_mem = jnp.zeros_like(memx)
    per_dest2 = [(0, True), (1, True)]

    def ffn_bwd(i, dh, h_in, gu, hn_f, act_gc, g_gu, g_dn, deps=()):
        act, gc = act_gc
        dgu, d_cw, d_cb = _glu_bwd(gu, gc, ffn_cw[i], dh[1], g_dn, f"glu_bwd_{i}", deps=deps)
        d_wdown = _matmul_tn(act, pl.BlockSpec((None, tb, GU_SHARD), lambda j, r: (j, r, 0)),
                             dh[1], pl.BlockSpec((tb, D), lambda j, r: (r, 0)), s, FF_BLOCKS, (GU_SHARD, D),
                             (N_DEV, DN_SHARD, D), pl.BlockSpec((2, DN_SHARD, D), lambda j, r: (j, 0, 0)), f"d_w_down_{i}",
                             tm=tb)
        *dh_new, d_g = _matmul_nt_normbwd(dgu, _spec_gu_act(0, 1, tm), g_gu, _spec_gate_up(1), N_DEV, h_in,
                                          norm_ffn_g[i:i + 1], dh[0], f"d_ffn_in_{i}", w_t=True, act_copy=True)
        d_wgu = _matmul_tn(dgu, _spec_gu_act(1, 0, tb), hn_f, pl.BlockSpec((tb, D), lambda j, r: (r, 0)), s, N_DEV,
                           (GU_SHARD, D), (N_DEV, GU_SHARD, D), pl.BlockSpec((None, GU_SHARD, D), lambda j, r: (j, 0, 0)),
                           f"d_w_gate_up_{i}", tm=tb)
        return dh_new, [d_wdown, d_wgu], d_cw, d_cb, d_g

    def out_bwd(i, dh, mix, deps):
        dmix = _matmul_nt(dh[1], g_out[i], _spec_rowsharded(0, D // N_DEV, D), 1, (s, D), row_x, f"d_mix_{i}", deps=deps, out_dtype=_ACT)
        d_wout = _matmul_tn(mix, pl.BlockSpec((tb, D), lambda j, r: (r, 0)), dh[1], pl.BlockSpec((tb, D), lambda j, r: (r, 0)),
                            s, 1, (D, D), (N_DEV, D // N_DEV, D), pl.BlockSpec((N_DEV, D // N_DEV, D), lambda j, r: (0, 0, 0)),
                            f"d_w_out_{i}", tm=tb)
        return dmix, d_wout

    def mem_bwd(i, dmemkv, memn):
        tmm = _rows(MEM_LEN)
        *_, d_g = _matmul_nt_normbwd(dmemkv, pl.BlockSpec((tmm, 2 * X_Q), lambda r, j: (r, 0)), g_mk[i],
                                     _spec_rowsharded(0, D // N_DEV, 2 * X_Q), 1, memx, norm_mem_g[i:i + 1], zeros_mem,
                                     f"d_mem_in_{i}")
        by_row = lambda j, r: (r, 0)
        d_w = _matmul_tn(memn, pl.BlockSpec((tmm, D), by_row), dmemkv, pl.BlockSpec((tmm, 2 * X_Q), by_row), MEM_LEN, 1,
                         (D, 2 * X_Q), (N_DEV, D // N_DEV, 2 * X_Q),
                         pl.BlockSpec((N_DEV, D // N_DEV, 2 * X_Q), lambda j, r: (0, 0, 0)), f"d_w_mem_kv_{i}")
        return d_w, d_g

    out_land = ((N_DEV, D // N_DEV, D), _WIRE)
    mk_land = ((N_DEV, D // N_DEV, 2 * X_Q), _WIRE)
    ffn_lands = [((N_DEV, DN_SHARD, D), _WIRE), ((N_DEV, GU_SHARD, D), _WIRE)]
    dh, d_ffn1, d_cw1, d_cb1, d_gf1 = ffn_bwd(1, dh, h3, gu1, hn_f1, act1, g_gu1, g_dn1)
    ffn1_g = _seq_exchange(d_ffn1, ffn_lands, per_dest2, "send_ffn1_grads", 5)
    dmix, d_wout1 = out_bwd(1, dh, mix_b, ffn1_g.srcs)
    dproj_b, d_convw, d_par, d_ng, dmemkv1 = _mix_b_bwd(proj_b, conv_qkv, par_b, out_norm_g_b, memkv1, states, deltas, dmix, "mix_b_bwd")
    *dh, d_gm1 = _matmul_nt_normbwd(dproj_b, pl.BlockSpec((tm, 896), lambda i, j: (i, j)), g_ib,
                                    _spec_rowsharded(0, D // N_DEV, 896, col_block=1), IN_BP // 896, h2, norm_mix_g[1:2], dh[0],
                                    "d_in_b", act_copy=True)
    d_wib = _matmul_tn(hn_b, pl.BlockSpec((tb, D), lambda j, r: (r, 0)), dproj_b, pl.BlockSpec((tb, 896), lambda j, r: (r, j)),
                       s, IN_BP // 896, (D, 896), (N_DEV, D // N_DEV, IN_BP),
                       pl.BlockSpec((N_DEV, D // N_DEV, 896), lambda j, r: (0, 0, j)), "d_w_in_b", tm=tb)
    d_wmk1, d_gmem1 = mem_bwd(1, dmemkv1, memn1)
    mix1_g = _seq_exchange([d_wout1, d_wib, d_wmk1], [out_land, ((N_DEV, D // N_DEV, IN_BP), _WIRE), mk_land],
                           [(0, True), (1, True), (2, True)], "send_mix1_grads", 6)
    dh, d_ffn0, d_cw0, d_cb0, d_gf0 = ffn_bwd(0, dh, h1, gu0, hn_f0, act0, g_gu0, g_dn0, deps=mix1_g.srcs)
    dmix, d_wout0 = out_bwd(0, dh, mix_a, d_ffn0 + ffn1_g.lands[:1])
    ffn0_g = _seq_exchange(d_ffn0 + [d_wout0], ffn_lands + [out_land], per_dest2 + [(2, True)], "send_ffn0_grads", 4)
    dproj_a, dbias, dsinks, dmemkv0 = _mix_a_bwd(proj_a, bias, sinks, memkv0, dmix, "mix_a_bwd", deps=ffn0_g.srcs)
    dx, _, d_gm0 = _matmul_nt_normbwd(dproj_a, pl.BlockSpec((tm, 640), lambda i, j: (i, j)), w_ia,
                                      pl.BlockSpec((640, D), lambda i, j: (j, 0)), IN_A // 640, h0, norm_mix_g[0:1], dh[0],
                                      "d_in_a", w_t=True)
    d_wia = _matmul_tn(dproj_a, pl.BlockSpec((tb, IN_A), lambda j, r: (r, 0)), hn_a, pl.BlockSpec((tb, D), lambda j, r: (r, 0)),
                       s, 1, (IN_A, D), (N_DEV, IA_SHARD, D), pl.BlockSpec((N_DEV, IA_SHARD, D), lambda j, r: (0, 0, 0)),
                       "d_w_in_a", tm=tb)
    d_wmk0, d_gmem0 = mem_bwd(0, dmemkv0, memn0)
    d_rel = _bias_reduce(dbias, bucket, "bias_reduce")
    small = _pack_small(d_rel, (d_cb0, d_cb1), (d_cw0, d_cw1), d_convw, (d_gm0, d_gm1), (d_gmem0, d_gmem1),
                        (d_gf0, d_gf1), d_final_g, dsinks, d_par, d_ng, loss_row, "pack_small")
    mix0_g = _seq_exchange([d_wia, d_wmk0, small],
                           [((N_DEV, IA_SHARD, D), _WIRE), mk_land, ((N_DEV, SMALL_ROWS, D_FF), F32)],
                           [(0, True), (1, True), (2, False)], "send_mix0_grads", 7)

    res = {}
    last = []

    def update(nm, parts, tr, restore=False, transposed=False):
        view = t_ if transposed else (lambda a: a)
        out = _adamw(parts, view(wts[nm]), view(moms[nm]), view(vars_[nm]), tr, "adamw_" + nm, restore_b=restore, deps=last[-1:])
        res[nm] = [view(o) for o in out]
        last.append(out[1])

    r_dn1, r_gu1 = ffn1_g.lands
    r_dn0, r_gu0, r_out0 = ffn0_g.lands
    r_out1, r_ib, r_mk1 = mix1_g.lands
    update("w_in_b", [r_ib], 32, True)
    update("w_gate_up", [r_gu0, r_gu1], 176, transposed=True)
    update("w_down", [r_dn0, r_dn1], 176)
    r_ia, r_mk0, r_small = mix0_g.lands
    update("w_mem_kv", [r_mk0, r_mk1], 128)
    update("w_out", [r_out0, r_out1], 128)
    update("w_in_a", [r_ia], IA_SHARD, transposed=True)

    my = 4 * lax.axis_index("x") + 2 * lax.axis_index("y") + lax.axis_index("c")
    cq = conv_qkv_b.shape[-1]
    cf = ffn_conv_w.shape[-1]
    rc_qkv = lax.dynamic_slice_in_dim(r_small[:, SP_QKV:SP_QKV + B_CONV, :B_QKV], my * cq, cq, axis=2)[:, None]
    rc_ffn = lax.dynamic_slice_in_dim(r_small[:, SP_CW:SP_CW + 2 * FFN_CONV, :], my * cf, cf, axis=2).reshape(N_DEV, 2, FFN_CONV, cf)
    as2d = lambda a: a[None, :] if a.ndim == 1 else a
    small_out = _adamw_small(r_small, rc_qkv, rc_ffn, [as2d(wts[n]) for n in _SMALL], [as2d(moms[n]) for n in _SMALL],
                             [as2d(vars_[n]) for n in _SMALL], "adamw_small", deps=last[-1:])
    ns = len(_SMALL)
    for i, nm in enumerate(_SMALL):
        res[nm] = [small_out[k * ns + i].reshape(wts[nm].shape) for k in range(4)]

    return (small_out[-1][0, 0], dx[None], *[res[n][0] for n in order], *[res[n][1] for n in order],
            *[res[n][2] for n in order], *[res[n][3] for n in order])
```

```python
import functools
import math

import numpy as np

import jax
import jax.numpy as jnp
from jax import lax
from jax.experimental import pallas as pl
from jax.experimental.pallas import tpu as pltpu
from jax.experimental.pallas import tpu_sc as plsc

F32 = jnp.float32
_MXU = jnp.bfloat16
_ACT = jnp.bfloat16
_WIRE = jnp.bfloat16
_HI = lax.Precision.HIGH
_TM = 1024
_TM_GLU = 1024
_TM_BIG = 2048
_VMEM_LIMIT = 48 * 1024 * 1024
_SDS = jax.ShapeDtypeStruct

D = 1024
EPS = 1e-6
A_HEADS, A_KV_HEADS, A_HD, BLK = 12, 2, 64, 128
N_BUCKETS, MAX_DISTANCE = 32, 128
B_QK_HEADS, B_V_HEADS, B_HD, B_CONV, CHUNK = 3, 6, 128, 4, 64
X_HEADS, X_HD, MEM_LEN = 4, 64, 256
D_FF, FFN_CONV = 2816, 3
A_Q, A_KV, X_Q = 768, 128, 256
B_QK, B_V, B_QKV = 384, 768, 1536
IN_A, IN_B = 1280, 2572
IN_BP = 2688
BP_Z, BP_XQ, BP_GATE = 1536, 2304, 2560
HALO = 8
GLU_HALO = 16

N_DEV = 8
GU_SHARD = 2 * D_FF // N_DEV
FF_BLOCKS = D_FF // GU_SHARD
DN_SHARD = D_FF // N_DEV
IA_SHARD = IN_A // N_DEV

ADAM_LR, ADAM_B1, ADAM_B2, ADAM_EPS, ADAM_WD, ADAM_STEP = 0.001, 0.9, 0.999, 1e-08, 0.01, 10

SP_CB, SP_CW, SP_QKV, SP_MIX, SP_MEM, SP_FFN, SP_FINAL, SP_MISC, SMALL_ROWS = 0, 2, 8, 12, 14, 16, 18, 19, 24
SP_REL_LANE = B_QKV


def _cp(*sems):
    return pltpu.CompilerParams(dimension_semantics=sems, vmem_limit_bytes=_VMEM_LIMIT)


def _mm(a, b):
    return jnp.dot(a.astype(_MXU), b.astype(_MXU), preferred_element_type=F32)


def _mm_nt(a, b):
    return lax.dot_general(a.astype(_MXU), b.astype(_MXU), (((1,), (1,)), ((), ())), preferred_element_type=F32)


def _mm_tn(a, b):
    return lax.dot_general(a.astype(_MXU), b.astype(_MXU), (((0,), (0,)), ((), ())), preferred_element_type=F32)


def _mmf(a, b):
    return jnp.dot(a, b, preferred_element_type=F32, precision=_HI)


def _mmf_nt(a, b):
    return lax.dot_general(a, b, (((1,), (1,)), ((), ())), preferred_element_type=F32, precision=_HI)


def _silu(x):
    return x * jax.nn.sigmoid(x)


def _w2d(ref):
    v = ref[...]
    return v.reshape(-1, v.shape[-1])


def _rows(m):
    return min(m, _TM)


def _spec_rowsharded(layer, rows, cols, col_block=None):
    if col_block is None:
        return pl.BlockSpec((N_DEV, None, rows, cols), lambda *_: (0, layer, 0, 0))
    return pl.BlockSpec((N_DEV, None, rows, cols), lambda *ids: (0, layer, 0, ids[col_block]))


def _spec_gate_up(axis):
    return pl.BlockSpec((None, GU_SHARD, D), lambda *ids: (ids[axis], 0, 0))


def _spec_down(axis):
    return pl.BlockSpec((2, DN_SHARD, D), lambda *ids: (ids[axis], 0, 0))


def _dep_specs(deps):
    return [pl.BlockSpec(memory_space=pl.ANY) for d in deps]


def _spec_gu_act(row_axis, axis, tm):
    return pl.BlockSpec((None, None, tm, GU_SHARD), lambda *ids: (ids[axis] // FF_BLOCKS, ids[axis] % FF_BLOCKS, ids[row_axis], 0))


def _norm_matmul(x, g, w, w_spec, n_blocks, out_shape, out_spec, name, deps=(), out_dtype=F32, w_t=False, tm=None):
    m, k = x.shape
    tm = tm or _rows(m)

    def body(x_ref, g_ref, w_ref, *rest):
        y_ref, hn_ref = rest[-2:]

        @pl.when(pl.program_id(1) == 0)
        def _():
            xv = x_ref[...]
            r = lax.rsqrt(jnp.mean(xv * xv, axis=-1, keepdims=True) + EPS)
            hn_ref[...] = (xv * r * g_ref[...]).astype(hn_ref.dtype)

        y_ref[...] = (_mm_nt if w_t else _mm)(hn_ref[...], _w2d(w_ref)).astype(y_ref.dtype)

    return pl.pallas_call(
        body, grid=(m // tm, n_blocks),
        in_specs=[pl.BlockSpec((tm, k), lambda i, j: (i, 0)), pl.BlockSpec((1, k), lambda i, j: (0, 0)), w_spec]
        + _dep_specs(deps),
        out_specs=[out_spec, pl.BlockSpec((tm, k), lambda i, j: (i, 0))],
        out_shape=[_SDS(out_shape, out_dtype), _SDS((m, k), _ACT)],
        name=name, compiler_params=_cp("arbitrary", "arbitrary"))(x, g, w, *deps)


def _matmul_res(a, a_spec, w, w_spec, n_k, res, name):
    m, n = res.shape
    tm = _rows(m)

    def body(a_ref, w_ref, r_ref, o_ref):
        part = _mm(a_ref[...], _w2d(w_ref))

        @pl.when(pl.program_id(1) == 0)
        def _():
            o_ref[...] = r_ref[...] + part

        @pl.when(pl.program_id(1) > 0)
        def _():
            o_ref[...] += part

    return pl.pallas_call(
        body, grid=(m // tm, n_k),
        in_specs=[a_spec, w_spec, pl.BlockSpec((tm, n), lambda i, j: (i, 0))],
        out_specs=pl.BlockSpec((tm, n), lambda i, j: (i, 0)),
        out_shape=_SDS((m, n), F32), name=name, compiler_params=_cp("arbitrary", "arbitrary"))(a, w, res)


def _matmul_nt(dy, w, w_spec, n_blocks, out_shape, out_spec, name, deps=(), out_dtype=F32):
    m, n = dy.shape
    tm = _rows(m)

    def body(dy_ref, w_ref, *rest):
        o_ref = rest[-1]
        o_ref[...] = _mm_nt(dy_ref[...], _w2d(w_ref)).astype(o_ref.dtype)

    return pl.pallas_call(
        body, grid=(m // tm, n_blocks),
        in_specs=[pl.BlockSpec((tm, n), lambda i, j: (i, 0)), w_spec] + _dep_specs(deps),
        out_specs=out_spec, out_shape=_SDS(out_shape, out_dtype),
        name=name, compiler_params=_cp("arbitrary", "arbitrary"))(dy, w, *deps)


def _matmul_nt_normbwd(dy, dy_spec, w, w_spec, nj, h, g, dh_in, name, w_t=False, act_copy=False):
    m, k = h.shape
    tm = _rows(m)

    def body(dy_ref, w_ref, h_ref, g_ref, dhin_ref, dh_ref, *rest):
        dg_ref, acc_ref = rest[-2:]
        i, j = pl.program_id(0), pl.program_id(1)

        @pl.when(j == 0)
        def _():
            acc_ref[...] = jnp.zeros_like(acc_ref)

        acc_ref[...] += (_mm if w_t else _mm_nt)(dy_ref[...], _w2d(w_ref))

        @pl.when(j == nj - 1)
        def _():
            xv = h_ref[...]
            r = lax.rsqrt(jnp.mean(xv * xv, axis=-1, keepdims=True) + EPS)
            xh = xv * r
            dhn = acc_ref[...]
            part = jnp.sum(dhn * xh, axis=0, keepdims=True)

            @pl.when(i == 0)
            def _():
                dg_ref[...] = part

            @pl.when(i > 0)
            def _():
                dg_ref[...] += part

            t = dhn * g_ref[...]
            dh = dhin_ref[...] + r * (t - xh * jnp.mean(t * xh, axis=-1, keepdims=True))
            dh_ref[...] = dh
            if act_copy:
                rest[0][...] = dh.astype(_ACT)

    rows = pl.BlockSpec((tm, k), lambda i, j: (i, 0))
    outs = pl.pallas_call(
        body, grid=(m // tm, nj),
        in_specs=[dy_spec, w_spec, rows, pl.BlockSpec((1, k), lambda i, j: (0, 0)), rows],
        out_specs=[rows] + [rows] * act_copy + [pl.BlockSpec((1, k), lambda i, j: (0, 0))],
        out_shape=[_SDS((m, k), F32)] + [_SDS((m, k), _ACT)] * act_copy + [_SDS((1, k), F32)],
        scratch_shapes=[pltpu.VMEM((tm, k), F32)],
        name=name, compiler_params=_cp("arbitrary", "arbitrary"))(dy, w, h, g, dh_in)
    return outs[0], (outs[1] if act_copy else None), outs[-1]


def _matmul_tn(x, x_spec, dy, dy_spec, m, n_blocks, acc_shape, out_shape, out_spec, name, tm=None):
    tm = tm or _rows(m)
    nm = m // tm

    def body(x_ref, dy_ref, o_ref, acc_ref):
        @pl.when(pl.program_id(1) == 0)
        def _():
            acc_ref[...] = jnp.zeros_like(acc_ref)

        acc_ref[...] += _mm_tn(x_ref[...], dy_ref[...])

        @pl.when(pl.program_id(1) == nm - 1)
        def _():
            o_ref[...] = acc_ref[...].reshape(o_ref.shape).astype(o_ref.dtype)

    return pl.pallas_call(
        body, grid=(n_blocks, nm), in_specs=[x_spec, dy_spec], out_specs=out_spec,
        out_shape=_SDS(out_shape, _WIRE), scratch_shapes=[pltpu.VMEM(acc_shape, F32)],
        name=name, compiler_params=_cp("arbitrary", "arbitrary"))(x, dy)


def _loss_head(h, g, tgt, name):
    m, k = h.shape
    tm = _rows(m)

    def body(h_ref, g_ref, t_ref, loss_ref, dh_ref, dha_ref, dg_ref):
        i = pl.program_id(0)
        xv = h_ref[...]
        r = lax.rsqrt(jnp.mean(xv * xv, axis=-1, keepdims=True) + EPS)
        xh = xv * r
        gv = g_ref[...]
        err = xh * gv - t_ref[...]
        lpart = jnp.zeros((1, 128), F32) + 0.5 * jnp.sum(jnp.mean(err * err, axis=-1, keepdims=True), axis=0, keepdims=True)
        dy = err * (1.0 / k)
        gpart = jnp.sum(dy * xh, axis=0, keepdims=True)

        @pl.when(i == 0)
        def _():
            loss_ref[...] = lpart
            dg_ref[...] = gpart

        @pl.when(i > 0)
        def _():
            loss_ref[...] += lpart
            dg_ref[...] += gpart

        t = dy * gv
        dh = r * (t - xh * jnp.mean(t * xh, axis=-1, keepdims=True))
        dh_ref[...] = dh
        dha_ref[...] = dh.astype(_ACT)

    rows = pl.BlockSpec((tm, k), lambda i: (i, 0))
    return pl.pallas_call(
        body, grid=(m // tm,),
        in_specs=[rows, pl.BlockSpec((1, k), lambda i: (0, 0)), rows],
        out_specs=[pl.BlockSpec((1, 128), lambda i: (0, 0)), rows, rows, pl.BlockSpec((1, k), lambda i: (0, 0))],
        out_shape=[_SDS((1, 128), F32), _SDS((m, k), F32), _SDS((m, k), _ACT), _SDS((1, k), F32)],
        name=name, compiler_params=_cp("arbitrary"))(h, g, tgt)


def _glu_down(gu, conv_w, conv_b, w_down, res, name):
    s = gu.shape[2]
    tm = min(s, _TM_GLU)

    def body(gu_ref, prev_ref, w_ref, b_ref, wdn_ref, r_ref, o_ref, act_ref, gc_ref):
        i, j = pl.program_id(0), pl.program_id(1)
        prev = jnp.where(i > 0, prev_ref[...].astype(F32), 0.0)
        ext = jnp.concatenate([prev, gu_ref[0].astype(F32)], axis=0)
        gc = b_ref[...] + w_ref[FFN_CONV - 1:FFN_CONV, :] * ext
        for k in range(FFN_CONV - 1):
            gc = gc + w_ref[k:k + 1, :] * pltpu.roll(ext, FFN_CONV - 1 - k, 0)
        gc = gc[GLU_HALO:]
        gc_ref[...] = gc.astype(gc_ref.dtype)
        act =(_silu(gc) * gu_ref[1].astype(F32)).astype(act_ref.dtype)
        act_ref[...] = act
        part = _mm(act, _w2d(wdn_ref))

        @pl.when(j == 0)
        def _():
            o_ref[...] = r_ref[...] + part

        @pl.when(j > 0)
        def _():
            o_ref[...] += part

    return pl.pallas_call(
        body, grid=(s // tm, FF_BLOCKS),
        in_specs=[pl.BlockSpec((2, None, tm, GU_SHARD), lambda i, j: (0, j, i, 0)),
                  pl.BlockSpec((None, None, GLU_HALO, GU_SHARD),
                               lambda i, j: (0, j, jnp.maximum(i * (tm // GLU_HALO) - 1, 0), 0)),
                  pl.BlockSpec((None, HALO, GU_SHARD), lambda i, j: (j, 0, 0)),
                  pl.BlockSpec((None, 1, GU_SHARD), lambda i, j: (j, 0, 0)),
                  _spec_down(1), pl.BlockSpec((tm, D), lambda i, j: (i, 0))],
        out_specs=[pl.BlockSpec((tm, D), lambda i, j: (i, 0)), pl.BlockSpec((None, tm, GU_SHARD), lambda i, j: (j, i, 0)),
                   pl.BlockSpec((None, tm, GU_SHARD), lambda i, j: (j, i, 0))],
        out_shape=[_SDS((s, D), F32), _SDS((FF_BLOCKS, s, GU_SHARD), _ACT), _SDS((FF_BLOCKS, s, GU_SHARD), _ACT)], name=name,
        compiler_params=_cp("arbitrary", "arbitrary"))(gu, gu, conv_w, conv_b, w_down, res)


def _glu_bwd(gu, gc, conv_w, dh, w_down, name, deps=()):
    s = gu.shape[2]
    tm = min(s, _TM_GLU)
    nt = s // tm
    ext_rows = tm + GLU_HALO

    def body(gu_ref, prev_ref, gc_ref, w_ref, dh_ref, wdn_ref, *rest):
        dgu_ref, dw_ref, db_ref, carry_ref = rest[-4:]
        t = pl.program_id(1)
        i = nt - 1 - t

        @pl.when(t == 0)
        def _():
            carry_ref[...] = jnp.zeros_like(carry_ref)
            dw_ref[...] = jnp.zeros_like(dw_ref)
            db_ref[...] = jnp.zeros_like(db_ref)

        up = gu_ref[1].astype(F32)
        prev = jnp.where(i > 0, prev_ref[...].astype(F32), 0.0)
        ext = jnp.concatenate([prev, gu_ref[0].astype(F32)], axis=0)
        gc = gc_ref[...].astype(F32)
        sg = jax.nn.sigmoid(gc)
        da = _mm_nt(dh_ref[...], _w2d(wdn_ref))
        dup = da * (gc * sg)
        dgc = da * up * (sg * (1.0 + gc * (1.0 - sg)))
        db_ref[...] += jnp.sum(dgc, axis=0, keepdims=True)
        dgc_ext = jnp.concatenate([jnp.zeros((GLU_HALO, GU_SHARD), F32), dgc], axis=0)
        ahead = [pltpu.roll(dgc_ext, ext_rows - (FFN_CONV - 1 - j), 0) if j < FFN_CONV - 1 else dgc_ext
                 for j in range(FFN_CONV)]
        dext = ahead[0] * w_ref[0:1, :]
        for j in range(FFN_CONV):
            dw_ref[j:j + 1, :] += jnp.sum(ext * ahead[j], axis=0, keepdims=True)
            if j > 0:
                dext = dext + ahead[j] * w_ref[j:j + 1, :]
        tail = jnp.concatenate([jnp.zeros((tm - GLU_HALO, GU_SHARD), F32), carry_ref[...]], axis=0)
        dgate = dext[GLU_HALO:] + tail
        carry_ref[...] = dext[:GLU_HALO]
        dgu_ref[0] = dgate.astype(dgu_ref.dtype)
        dgu_ref[1] = dup.astype(dgu_ref.dtype)

    return pl.pallas_call(
        body, grid=(FF_BLOCKS, nt),
        in_specs=[pl.BlockSpec((2, None, tm, GU_SHARD), lambda j, t: (0, j, nt - 1 - t, 0)),
                  pl.BlockSpec((None, None, GLU_HALO, GU_SHARD),
                               lambda j, t: (0, j, jnp.maximum((nt - 1 - t) * (tm // GLU_HALO) - 1, 0), 0)),
                  pl.BlockSpec((None, tm, GU_SHARD), lambda j, t: (j, nt - 1 - t, 0)),
                  pl.BlockSpec((None, HALO, GU_SHARD), lambda j, t: (j, 0, 0)),
                  pl.BlockSpec((tm, D), lambda j, t: (nt - 1 - t, 0)), _spec_down(0)] + _dep_specs(deps),
        out_specs=[pl.BlockSpec((2, None, tm, GU_SHARD), lambda j, t: (0, j, nt - 1 - t, 0)),
                   pl.BlockSpec((None, HALO, GU_SHARD), lambda j, t: (j, 0, 0)),
                   pl.BlockSpec((None, 1, GU_SHARD), lambda j, t: (j, 0, 0))],
        out_shape=[_SDS(gu.shape, _ACT), _SDS((FF_BLOCKS, HALO, GU_SHARD), F32), _SDS((FF_BLOCKS, 1, GU_SHARD), F32)],
        scratch_shapes=[pltpu.VMEM((GLU_HALO, GU_SHARD), F32)],
        name=name, compiler_params=_cp("arbitrary", "arbitrary"))(gu, gu, gc, conv_w, dh, w_down, *deps)


def _bucket_table():
    qi = np.arange(BLK)[:, None]
    kj = np.arange(BLK)[None, :]
    n = np.where(kj > qi, BLK + qi - kj, qi - kj)
    max_exact = N_BUCKETS // 2
    nf = np.maximum(n, 1).astype(np.float32)
    large = max_exact + (np.log(nf / max_exact) / math.log(MAX_DISTANCE / max_exact)
                         * (N_BUCKETS - max_exact)).astype(np.int32)
    large = np.minimum(large, N_BUCKETS - 1)
    return np.where(n < max_exact, n, large).astype(np.int32)


def _lane_low():
    return lax.broadcasted_iota(jnp.int32, (1, 128), 1) < A_HD


def _swa_groups(q, kd, vd, sink, bias, upper, first):
    n = A_HEADS // A_KV_HEADS
    ng = len(q)
    low = _lane_low()
    qm = [jnp.concatenate([jnp.where(low == (h % 2 == 0), q[g][:, (h // 2) * 128:(h // 2 + 1) * 128], 0.0) for h in range(n)], axis=0)
          for g in range(ng)]
    s2 = [_mm_nt(qm[g], kd[g]) * (A_HD ** -0.5) for g in range(ng)]
    s = [jnp.where(upper[None], s2[g][:, :BLK].reshape(n, BLK, BLK), s2[g][:, BLK:].reshape(n, BLK, BLK)) + bias[g] for g in range(ng)]
    s = [t if f is None else jnp.where((upper & f)[None], -jnp.inf, t) for t, f in zip(s, first)]
    m = [lax.stop_gradient(jnp.maximum(jnp.max(s[g], axis=-1, keepdims=True), sink[g])) for g in range(ng)]
    p = [jnp.exp(s[g] - m[g]) for g in range(ng)]
    split = [jnp.concatenate([jnp.where(upper[None], t, 0.0), jnp.where(upper[None], 0.0, t)], axis=-1).reshape(n * BLK, 2 * BLK)
             for t in p]
    ones = jnp.ones((BLK, 128), F32)
    den = [_mm(p[g].reshape(n * BLK, BLK), ones) + jnp.exp(sink[g] - m[g]).reshape(n * BLK, 1) for g in range(ng)]
    o = [_mm(split[g], vd[g]) / den[g] for g in range(ng)]
    return [jnp.concatenate([jnp.where(low, t[2 * k * BLK:(2 * k + 1) * BLK], t[(2 * k + 1) * BLK:(2 * k + 2) * BLK])
                             for k in range(n // 2)], axis=1) for t in o]


def _mix_a_core(q, kd, vd, sink, bias, xq, mk, mv, upper, first):
    return _swa_groups(q, kd, vd, sink, bias, upper, first), _cross_pairs(xq, mk, mv)


def _swa_sinks(sink_ref, g):
    n = A_HEADS // A_KV_HEADS
    return jnp.concatenate([sink_ref[:, h:h + 1] for h in range(g * n, (g + 1) * n)], axis=0).reshape(n, 1, 1)


def _both_halves(t, t_rolled, g):
    low = _lane_low()
    return jnp.where(low, t, t_rolled) if g == 0 else jnp.where(low, t_rolled, t)


def _cross_pairs(q, mk, mv):
    rows = q.shape[0]
    low = _lane_low()
    qm = [jnp.concatenate([jnp.where(low, q[:, p * 128:(p + 1) * 128], 0.0), jnp.where(low, 0.0, q[:, p * 128:(p + 1) * 128])], axis=0)
          for p in range(X_HEADS // 2)]
    s = [_mm_nt(qm[p], mk[:, p * 128:(p + 1) * 128]) * (X_HD ** -0.5) for p in range(X_HEADS // 2)]
    e = [jnp.exp(t - lax.stop_gradient(jnp.max(t, axis=-1, keepdims=True))) for t in s]
    pr = [t / jnp.sum(t, axis=-1, keepdims=True) for t in e]
    o = [_mm(pr[p], mv[:, p * 128:(p + 1) * 128]) for p in range(X_HEADS // 2)]
    return jnp.concatenate([jnp.where(low, t[:rows], t[rows:]) for t in o], axis=1)


def _swa_upper():
    qi = lax.broadcasted_iota(jnp.int32, (BLK, BLK), 0)
    kj = lax.broadcasted_iota(jnp.int32, (BLK, BLK), 1)
    return kj > qi


def _bias_build(rel_bias, bucket, name):
    def body(rb_ref, bucket_ref, o_ref):
        b = bucket_ref[...]
        for h in range(A_HEADS):
            acc = jnp.zeros((BLK, BLK), F32)
            for k in range(N_BUCKETS):
                acc = jnp.where(b == k, rb_ref[k, h], acc)
            o_ref[h] = acc

    return pl.pallas_call(
        body, in_specs=[pl.BlockSpec(memory_space=pltpu.SMEM), pl.BlockSpec(memory_space=pltpu.VMEM)],
        out_specs=pl.BlockSpec(memory_space=pltpu.VMEM),
        out_shape=_SDS((A_HEADS, BLK, BLK), F32), name=name)(rel_bias, bucket)


def _bias_reduce(dbias, bucket, name):
    def body(db_ref, bucket_ref, o_ref):
        b = bucket_ref[...]
        row = lax.broadcasted_iota(jnp.int32, (N_BUCKETS, 128), 0)
        lane = lax.broadcasted_iota(jnp.int32, (N_BUCKETS, 128), 1)
        acc = jnp.zeros((N_BUCKETS, 128), F32)
        for h in range(A_HEADS):
            v = db_ref[h]
            for k in range(N_BUCKETS):
                sk = jnp.sum(jnp.sum(jnp.where(b == k, v, 0.0), axis=1, keepdims=True), axis=0, keepdims=True)
                acc = acc + jnp.where((row == k) & (lane == h), sk, 0.0)
        o_ref[...] = acc

    return pl.pallas_call(
        body, in_specs=[pl.BlockSpec(memory_space=pltpu.VMEM)] * 2,
        out_specs=pl.BlockSpec(memory_space=pltpu.VMEM),
        out_shape=_SDS((N_BUCKETS, 128), F32), name=name)(dbias, bucket)


def _mix_a_fwd(proj, bias, sinks, memkv, name):
    s = proj.shape[0]
    per = 4
    nb = s // (per * BLK)
    grp = A_HEADS // A_KV_HEADS

    def body(proj_ref, prev_ref, bias_ref, sink_ref, memkv_ref, o_ref):
        i = pl.program_id(0)
        upper = _swa_upper()
        proj = proj_ref[...].astype(F32)
        kv = jnp.concatenate([prev_ref[...].astype(F32), proj[:, A_Q:A_Q + 2 * A_KV]], axis=0)
        k, v = kv[:, :A_KV], kv[:, A_KV:]
        k_r = pltpu.roll(k, A_HD, 1)
        v_r = pltpu.roll(v, A_HD, 1)
        gw = A_Q // A_KV_HEADS
        each = [(b, g) for b in range(per) for g in range(A_KV_HEADS)]

        def window(a, a_r, b, g):
            return _both_halves(a[b * BLK:(b + 2) * BLK], a_r[b * BLK:(b + 2) * BLK], g)

        swa, cross = _mix_a_core([proj[b * BLK:(b + 1) * BLK, g * gw:(g + 1) * gw] for b, g in each],
                                 [window(k, k_r, b, g) for b, g in each], [window(v, v_r, b, g) for b, g in each],
                                 [_swa_sinks(sink_ref, g) for b, g in each], [bias_ref[g * grp:(g + 1) * grp] for b, g in each],
                                 proj[:, A_Q + 2 * A_KV:], memkv_ref[:, :X_Q], memkv_ref[:, X_Q:], upper,
                                 [(i == 0) if b == 0 else None for b, g in each])
        for b in range(per):
            o_ref[b * BLK:(b + 1) * BLK, :] = jnp.concatenate(
                swa[b * A_KV_HEADS:(b + 1) * A_KV_HEADS] + [cross[b * BLK:(b + 1) * BLK]], axis=1).astype(o_ref.dtype)

    return pl.pallas_call(
        body, grid=(nb,),
        in_specs=[pl.BlockSpec((per * BLK, IN_A), lambda i: (i, 0)),
                  pl.BlockSpec((BLK, 2 * A_KV), lambda i: (jnp.maximum(per * i - 1, 0), A_Q // (2 * A_KV))),
                  pl.BlockSpec((A_HEADS, BLK, BLK), lambda i: (0, 0, 0)),
                  pl.BlockSpec((1, 128), lambda i: (0, 0)),
                  pl.BlockSpec((MEM_LEN, 2 * X_Q), lambda i: (0, 0))],
        out_specs=pl.BlockSpec((per * BLK, D), lambda i: (i, 0)),
        out_shape=_SDS((s, D), _ACT), name=name, compiler_params=_cp("arbitrary"))(proj, proj, bias, sinks, memkv)


def _mix_a_bwd(proj, bias, sinks, memkv, dmix, name, deps=()):
    s = proj.shape[0]
    per = 2
    nb = s // (per * BLK)
    grp = A_HEADS // A_KV_HEADS

    def body(proj_ref, prev_ref, bias_ref, sink_ref, memkv_ref, dmix_ref, *rest):
        dproj_ref, dbias_ref, dsink_ref, dmemkv_ref, carry_ref = rest[-5:]
        t = pl.program_id(0)
        i = nb - 1 - t

        @pl.when(t == 0)
        def _():
            carry_ref[...] = jnp.zeros_like(carry_ref)
            dbias_ref[...] = jnp.zeros_like(dbias_ref)
            dsink_ref[...] = jnp.zeros_like(dsink_ref)
            dmemkv_ref[...] = jnp.zeros_like(dmemkv_ref)

        upper = _swa_upper()
        lane = lax.broadcasted_iota(jnp.int32, (1, 128), 1)
        low = _lane_low()
        proj = proj_ref[...].astype(F32)
        kv = jnp.concatenate([prev_ref[...].astype(F32), proj[:, A_Q:A_Q + 2 * A_KV]], axis=0)
        k, v = kv[:, :A_KV], kv[:, A_KV:]
        k_r = pltpu.roll(k, A_HD, 1)
        v_r = pltpu.roll(v, A_HD, 1)
        gw = A_Q // A_KV_HEADS
        each = [(b, g) for b in range(per) for g in range(A_KV_HEADS)]

        def window(a, a_r, b, g):
            return _both_halves(a[b * BLK:(b + 2) * BLK], a_r[b * BLK:(b + 2) * BLK], g)

        _, vjp = jax.vjp(
            functools.partial(_mix_a_core, upper=upper, first=[(i == 0) if b == 0 else None for b, g in each]),
            [proj[b * BLK:(b + 1) * BLK, g * gw:(g + 1) * gw] for b, g in each],
            [window(k, k_r, b, g) for b, g in each], [window(v, v_r, b, g) for b, g in each],
            [_swa_sinks(sink_ref, g) for b, g in each], [bias_ref[g * grp:(g + 1) * grp] for b, g in each],
            proj[:, A_Q + 2 * A_KV:], memkv_ref[:, :X_Q], memkv_ref[:, X_Q:])
        dqs, dk, dv, ds, db, dxq, dmk, dmv = vjp(
            ([dmix_ref[b * BLK:(b + 1) * BLK, g * gw:(g + 1) * gw].astype(F32) for b, g in each], dmix_ref[:, A_Q:].astype(F32)))
        dkd = [t + pltpu.roll(t, A_HD, 1) for t in dk]
        dvd = [t + pltpu.roll(t, A_HD, 1) for t in dv]
        dsink = jnp.zeros((1, 128), F32)
        for e, (b, g) in enumerate(each):
            for h in range(grp):
                dsink = dsink + jnp.where(lane == g * grp + h, ds[e][h], 0.0)
        for g in range(A_KV_HEADS):
            dbias_ref[g * grp:(g + 1) * grp] += db[g] + db[A_KV_HEADS + g]
        dsink_ref[...] += dsink
        dmemkv_ref[...] += jnp.concatenate([dmk, dmv], axis=1)
        dkv = [jnp.concatenate([jnp.where(low, dkd[b * A_KV_HEADS], dkd[b * A_KV_HEADS + 1]),
                                jnp.where(low, dvd[b * A_KV_HEADS], dvd[b * A_KV_HEADS + 1])], axis=1) for b in range(per)]
        own = [dkv[0][BLK:] + dkv[1][:BLK], dkv[1][BLK:] + carry_ref[...]]
        carry_ref[...] = dkv[0][:BLK]
        for b in range(per):
            dproj_ref[b * BLK:(b + 1) * BLK, :] = jnp.concatenate(
                list(dqs[b * A_KV_HEADS:(b + 1) * A_KV_HEADS]) + [own[b], dxq[b * BLK:(b + 1) * BLK]], axis=1).astype(dproj_ref.dtype)

    return pl.pallas_call(
        body, grid=(nb,),
        in_specs=[pl.BlockSpec((per * BLK, IN_A), lambda t: (nb - 1 - t, 0)),
                  pl.BlockSpec((BLK, 2 * A_KV), lambda t: (jnp.maximum(per * (nb - 1 - t) - 1, 0), A_Q // (2 * A_KV))),
                  pl.BlockSpec((A_HEADS, BLK, BLK), lambda t: (0, 0, 0)),
                  pl.BlockSpec((1, 128), lambda t: (0, 0)),
                  pl.BlockSpec((MEM_LEN, 2 * X_Q), lambda t: (0, 0)),
                  pl.BlockSpec((per * BLK, D), lambda t: (nb - 1 - t, 0))] + _dep_specs(deps),
        out_specs=[pl.BlockSpec((per * BLK, IN_A), lambda t: (nb - 1 - t, 0)),
                   pl.BlockSpec((A_HEADS, BLK, BLK), lambda t: (0, 0, 0)),
                   pl.BlockSpec((1, 128), lambda t: (0, 0)),
                   pl.BlockSpec((MEM_LEN, 2 * X_Q), lambda t: (0, 0))],
        out_shape=[_SDS((s, IN_A), _ACT), _SDS((A_HEADS, BLK, BLK), F32), _SDS((1, 128), F32),
                   _SDS((MEM_LEN, 2 * X_Q), F32)],
        scratch_shapes=[pltpu.VMEM((BLK, 2 * A_KV), F32)],
        name=name, compiler_params=_cp("arbitrary"))(proj, proj, bias, sinks, memkv, dmix, *deps)


def _neumann(pw, rhs):
    nh = len(pw)
    x = rhs
    for lvl in range(6):
        if lvl < 5:
            prod = [_mmf(pw[h], jnp.concatenate([x[h], pw[h]], axis=1)) for h in range(nh)]
            x = [x[h] + prod[h][:, :B_HD] for h in range(nh)]
            pw = [t[:, B_HD:] for t in prod]
        else:
            x = [x[h] + _mmf(pw[h], x[h]) for h in range(nh)]
    return x


@jax.custom_vjp
def _tri_solve(pw, rhs):
    return _neumann(pw, rhs)


def _tri_solve_fwd(pw, rhs):
    x = _neumann(pw, rhs)
    return x, (pw, x)


def _tri_solve_bwd(res, dx):
    pw, x = res
    d_rhs = _neumann([t.T for t in pw], list(dx))
    return [_mmf_nt(d_rhs[h], x[h]) for h in range(len(pw))], d_rhs


_tri_solve.defvjp(_tri_solve_fwd, _tri_solve_bwd)


@jax.custom_vjp
def _tri_solved(pw, rhs, x):
    return x


def _tri_solved_fwd(pw, rhs, x):
    return x, (pw, x)


def _tri_solved_bwd(res, dx):
    d_pw, d_rhs = _tri_solve_bwd(res, dx)
    return d_pw, d_rhs, [jnp.zeros_like(t) for t in res[1]]


_tri_solved.defvjp(_tri_solved_fwd, _tri_solved_bwd)


@jax.custom_vjp
def _known(x, value):
    return value


def _known_fwd(x, value):
    return value, None


def _known_bwd(_, g):
    return g, jnp.zeros_like(g)


_known.defvjp(_known_fwd, _known_bwd)


def _dn_heads(yq, yk, yv, z, bl, al, a_log, dtb, ng, s0, solved=None, out_known=None):
    c = CHUNK
    nh = B_V_HEADS
    rep = B_V_HEADS // B_QK_HEADS
    r = lax.broadcasted_iota(jnp.int32, (c, c), 0)
    cc = lax.broadcasted_iota(jnp.int32, (c, c), 1)
    q = [_silu(t) for t in yq]
    k = [_silu(t) for t in yk]
    v = [_silu(t) for t in yv]
    q = [t * lax.rsqrt(jnp.sum(t * t, axis=-1, keepdims=True) + EPS) * (B_HD ** -0.5) for t in q]
    k = [t * lax.rsqrt(jnp.sum(t * t, axis=-1, keepdims=True) + EPS) for t in k]
    beta = [jax.nn.sigmoid(t) for t in bl]
    g = [-jnp.exp(a_log[h]) * jax.nn.softplus(al[h] + dtb[h]) for h in range(nh)]
    gb = [jnp.broadcast_to(t, (c, c)) for t in g]
    gc_col = [jnp.sum(jnp.where(cc <= r, t.T, 0.0), axis=1, keepdims=True) for t in gb]
    gc_row = [jnp.sum(jnp.where(r <= cc, t, 0.0), axis=0, keepdims=True) for t in gb]
    gc_last = [jnp.sum(t, axis=0, keepdims=True) for t in g]
    decay = [jnp.exp(jnp.where(r >= cc, gc_col[h] - gc_row[h], -jnp.inf)) for h in range(nh)]
    kq = [_mmf_nt(jnp.concatenate([k[h], q[h]], axis=0), k[h]) for h in range(B_QK_HEADS)]
    kk = [t[:c] for t in kq]
    qk = [t[c:] for t in kq]
    egc = [jnp.exp(t) for t in gc_col]
    both = [_mmf(jnp.concatenate([(beta[h] * egc[h]) * k[h // rep], q[h // rep] * egc[h]], axis=0), s0[h]) for h in range(nh)]
    rhs = [beta[h] * v[h] - both[h][:c] for h in range(nh)]
    qs0 = [t[c:] for t in both]
    pw = [-(beta[h] * kk[h // rep] * jnp.where(r > cc, decay[h], 0.0)) for h in range(nh)]
    delta = _tri_solve(pw, rhs) if solved is None else _tri_solved(pw, rhs, solved)
    last = [_mmf(jnp.concatenate([qk[h // rep] * decay[h], (k[h // rep] * jnp.exp(gc_last[h] - gc_col[h])).T], axis=0), delta[h])
            for h in range(nh)]
    out = [qs0[h] + last[h][:c] for h in range(nh)]
    if out_known is not None:
        out = [_known(out[h], out_known[h]) for h in range(nh)]
    s1 = [jnp.exp(gc_last[h]) * s0[h] + last[h][c:] for h in range(nh)]
    o = [t * lax.rsqrt(jnp.mean(t * t, axis=-1, keepdims=True) + EPS) * ng for t in out]
    return [o[h] * _silu(z[h]) for h in range(nh)], s1, delta, out


def _dn_conv(ext, w_ref):
    y = ext * w_ref[B_CONV - 1:B_CONV, :]
    for j in range(B_CONV - 1):
        y = y + w_ref[j:j + 1, :] * pltpu.roll(ext, B_CONV - 1 - j, 0)
    return y


def _dn_args(y, cur_ref, par_ref, ng_ref):
    nh = B_V_HEADS
    return ([y[:, h * B_HD:(h + 1) * B_HD] for h in range(B_QK_HEADS)],
            [y[:, B_QK + h * B_HD:B_QK + (h + 1) * B_HD] for h in range(B_QK_HEADS)],
            [y[:, 2 * B_QK + h * B_HD:2 * B_QK + (h + 1) * B_HD] for h in range(nh)],
            [cur_ref[:, BP_Z + h * B_HD:BP_Z + (h + 1) * B_HD] for h in range(nh)],
            [cur_ref[:, BP_GATE + h:BP_GATE + h + 1] for h in range(nh)],
            [cur_ref[:, BP_GATE + nh + h:BP_GATE + nh + h + 1] for h in range(nh)],
            [par_ref[:, h:h + 1] for h in range(nh)], [par_ref[:, nh + h:nh + h + 1] for h in range(nh)], ng_ref[...])


def _mix_b_fwd(proj, conv_w, par, ng, memkv, name):
    s = proj.shape[0]
    nc = s // CHUNK

    def body(cur_ref, prev_ref, w_ref, par_ref, ng_ref, memkv_ref, o_ref, st_ref, dl_ref, state_ref):
        n = pl.program_id(0)

        @pl.when(n == 0)
        def _():
            state_ref[...] = jnp.zeros_like(state_ref)

        prev = jnp.where(n > 0, prev_ref[...], 0.0)
        ext = jnp.concatenate([prev, cur_ref[:, :B_QKV]], axis=0)
        y = _dn_conv(ext, w_ref)[HALO:]
        s0 = [state_ref[hv] for hv in range(B_V_HEADS)]
        st_ref[0] = state_ref[...]
        outs, s1, delta, raw = _dn_heads(*_dn_args(y, cur_ref, par_ref, ng_ref), s0)
        for hv in range(B_V_HEADS):
            state_ref[hv] = s1[hv]
            dl_ref[0, hv] = delta[hv]
            dl_ref[0, B_V_HEADS + hv] = raw[hv]
        outs = outs + [_cross_pairs(cur_ref[:, BP_XQ:BP_XQ + X_Q], memkv_ref[:, :X_Q], memkv_ref[:, X_Q:])]
        o_ref[...] = jnp.concatenate(outs, axis=1).astype(o_ref.dtype)

    return pl.pallas_call(
        body, grid=(nc,),
        in_specs=[pl.BlockSpec((CHUNK, IN_BP), lambda n: (n, 0)),
                  pl.BlockSpec((HALO, B_QKV), lambda n: (jnp.maximum(n * (CHUNK // HALO) - 1, 0), 0)),
                  pl.BlockSpec((HALO, B_QKV), lambda n: (0, 0)),
                  pl.BlockSpec((1, 128), lambda n: (0, 0)), pl.BlockSpec((1, 128), lambda n: (0, 0)),
                  pl.BlockSpec((MEM_LEN, 2 * X_Q), lambda n: (0, 0))],
        out_specs=[pl.BlockSpec((CHUNK, D), lambda n: (n, 0)),
                   pl.BlockSpec((1, B_V_HEADS, B_HD, B_HD), lambda n: (n, 0, 0, 0)),
                   pl.BlockSpec((1, 2 * B_V_HEADS, CHUNK, B_HD), lambda n: (n, 0, 0, 0))],
        out_shape=[_SDS((s, D), _ACT), _SDS((nc, B_V_HEADS, B_HD, B_HD), F32), _SDS((nc, 2 * B_V_HEADS, CHUNK, B_HD), F32)],
        scratch_shapes=[pltpu.VMEM((B_V_HEADS, B_HD, B_HD), F32)],
        name=name, compiler_params=_cp("arbitrary"))(proj, proj, conv_w, par, ng, memkv)


def _mix_b_bwd(proj, conv_w, par, ng, memkv, states, deltas, dmix, name):
    s = proj.shape[0]
    nc = s // CHUNK
    ext_rows = CHUNK + HALO

    def body(cur_ref, prev_ref, w_ref, par_ref, ng_ref, memkv_ref, st_ref, dl_ref, dmix_ref,
             dproj_ref, dw_ref, dpar_ref, dng_ref, dmemkv_ref, dstate_ref, carry_ref):
        t = pl.program_id(0)
        n = nc - 1 - t

        @pl.when(t == 0)
        def _():
            dstate_ref[...] = jnp.zeros_like(dstate_ref)
            carry_ref[...] = jnp.zeros_like(carry_ref)
            dw_ref[...] = jnp.zeros_like(dw_ref)
            dpar_ref[...] = jnp.zeros_like(dpar_ref)
            dng_ref[...] = jnp.zeros_like(dng_ref)
            dmemkv_ref[...] = jnp.zeros_like(dmemkv_ref)

        lane = lax.broadcasted_iota(jnp.int32, (1, 128), 1)
        prev = jnp.where(n > 0, prev_ref[...], 0.0)
        ext = jnp.concatenate([prev, cur_ref[:, :B_QKV]], axis=0)
        y = _dn_conv(ext, w_ref)[HALO:]
        solved = [dl_ref[0, hv] for hv in range(B_V_HEADS)]
        raw = [dl_ref[0, B_V_HEADS + hv] for hv in range(B_V_HEADS)]
        _, vjp = jax.vjp(functools.partial(_dn_heads, solved=solved, out_known=raw), *_dn_args(y, cur_ref, par_ref, ng_ref),
                         [st_ref[0, hv] for hv in range(B_V_HEADS)])
        none = [jnp.zeros((CHUNK, B_HD), F32)] * B_V_HEADS
        dyq, dyk, dyv, dz, gbl, gal, ga_log, gdtb, dng, gs0 = vjp(
            ([dmix_ref[:, hv * B_HD:(hv + 1) * B_HD].astype(F32) for hv in range(B_V_HEADS)],
             [dstate_ref[hv] for hv in range(B_V_HEADS)], none, none))
        dgate = jnp.zeros((CHUNK, 128), F32)
        dpar = jnp.zeros((1, 128), F32)
        for hv in range(B_V_HEADS):
            dstate_ref[hv] = gs0[hv]
            dgate = dgate + jnp.where(lane == hv, gbl[hv], 0.0) + jnp.where(lane == B_V_HEADS + hv, gal[hv], 0.0)
            dpar = dpar + jnp.where(lane == hv, ga_log[hv], 0.0) + jnp.where(lane == B_V_HEADS + hv, gdtb[hv], 0.0)
        dpar_ref[...] += dpar
        dng_ref[...] += dng
        _, vjp = jax.vjp(_cross_pairs, cur_ref[:, BP_XQ:BP_XQ + X_Q], memkv_ref[:, :X_Q], memkv_ref[:, X_Q:])
        dxq, dmk, dmv = vjp(dmix_ref[:, B_V:].astype(F32))
        dmemkv_ref[...] += jnp.concatenate([dmk, dmv], axis=1)
        dy = jnp.concatenate(list(dyq) + list(dyk) + list(dyv), axis=1)
        dy_ext = jnp.concatenate([jnp.zeros((HALO, B_QKV), F32), dy], axis=0)
        dext = dy_ext * w_ref[B_CONV - 1:B_CONV, :]
        dw_ref[B_CONV - 1:B_CONV, :] += jnp.sum(ext * dy_ext, axis=0, keepdims=True)
        for j in range(B_CONV - 1):
            sh = B_CONV - 1 - j
            dw_ref[j:j + 1, :] += jnp.sum(pltpu.roll(ext, sh, 0) * dy_ext, axis=0, keepdims=True)
            dext = dext + w_ref[j:j + 1, :] * pltpu.roll(dy_ext, ext_rows - sh, 0)
        tail = jnp.concatenate([jnp.zeros((CHUNK - HALO, B_QKV), F32), carry_ref[...]], axis=0)
        dqkv = dext[HALO:] + tail
        carry_ref[...] = dext[:HALO]
        dproj_ref[...] = jnp.concatenate([dqkv] + list(dz) + [dxq, dgate], axis=1).astype(dproj_ref.dtype)

    return pl.pallas_call(
        body, grid=(nc,),
        in_specs=[pl.BlockSpec((CHUNK, IN_BP), lambda t: (nc - 1 - t, 0)),
                  pl.BlockSpec((HALO, B_QKV), lambda t: (jnp.maximum((nc - 1 - t) * (CHUNK // HALO) - 1, 0), 0)),
                  pl.BlockSpec((HALO, B_QKV), lambda t: (0, 0)),
                  pl.BlockSpec((1, 128), lambda t: (0, 0)), pl.BlockSpec((1, 128), lambda t: (0, 0)),
                  pl.BlockSpec((MEM_LEN, 2 * X_Q), lambda t: (0, 0)),
                  pl.BlockSpec((1, B_V_HEADS, B_HD, B_HD), lambda t: (nc - 1 - t, 0, 0, 0)),
                  pl.BlockSpec((1, 2 * B_V_HEADS, CHUNK, B_HD), lambda t: (nc - 1 - t, 0, 0, 0)),
                  pl.BlockSpec((CHUNK, D), lambda t: (nc - 1 - t, 0))],
        out_specs=[pl.BlockSpec((CHUNK, IN_BP), lambda t: (nc - 1 - t, 0)),
                   pl.BlockSpec((HALO, B_QKV), lambda t: (0, 0)),
                   pl.BlockSpec((1, 128), lambda t: (0, 0)), pl.BlockSpec((1, 128), lambda t: (0, 0)),
                   pl.BlockSpec((MEM_LEN, 2 * X_Q), lambda t: (0, 0))],
        out_shape=[_SDS((s, IN_BP), _ACT), _SDS((HALO, B_QKV), F32), _SDS((1, 128), F32), _SDS((1, 128), F32),
                   _SDS((MEM_LEN, 2 * X_Q), F32)],
        scratch_shapes=[pltpu.VMEM((B_V_HEADS, B_HD, B_HD), F32), pltpu.VMEM((HALO, B_QKV), F32)],
        name=name, compiler_params=_cp("arbitrary"))(proj, proj, conv_w, par, ng, memkv, states, deltas, dmix)


def _place():
    return lax.axis_index("x"), lax.axis_index("y"), lax.axis_index("c")


def _two_level_gather(ins, outs, sems, local_sems):
    n = len(ins)
    x, y, c = _place()
    me, sibling = (x, y, c), (x, y, 1 - c)
    chips = [(1 - x, y), (x, 1 - y), (1 - x, 1 - y)]

    def rows(a, px, py, pc):
        return outs[a].at[4 * px + 2 * py + pc]

    def copy(a, k, block, to, src=None):
        return pltpu.make_async_remote_copy(
            src_ref=rows(a, *block) if src is None else src, dst_ref=rows(a, *block),
            device_id=to, device_id_type=pl.DeviceIdType.MESH, **sems(a, k))

    mine = [pltpu.make_async_copy(ins[a], rows(a, *me), local_sems.at[a]) for a in range(n)]
    for cp in mine:
        cp.start()
    first = []
    for a in range(n):
        first.append(copy(a, 0, me, sibling, src=ins[a]))
        first += [copy(a, 1 + j, me, (*chip, c), src=ins[a]) for j, chip in enumerate(chips)]
    for cp in first:
        cp.start()
    passed = []
    for j, chip in enumerate(chips):
        for a in range(n):
            copy(a, 1 + j, (*chip, c), me).wait_recv()
            fwd = copy(a, 4 + j, (*chip, c), sibling)
            fwd.start()
            passed.append(fwd)
    for a in range(n):
        copy(a, 0, sibling, me).wait_recv()
        for j, chip in enumerate(chips):
            copy(a, 4 + j, (*chip, 1 - c), me).wait_recv()
    for cp in first + passed:
        cp.wait_send()
    for cp in mine:
        cp.wait()


class _Exchange:
    def __init__(self, lands, srcs):
        self.lands, self.srcs = lands, srcs


def _seq_exchange(srcs, land_shapes, plan, name, cid):
    n, nl = len(srcs), len(land_shapes)

    def launch(*refs):
        src_refs, land_refs = refs[:n], refs[n:n + nl]
        send_sems, recv_sems, local_sems = refs[n + nl:]
        x, y, c = _place()
        my = 4 * x + 2 * y + c
        peers = [(x ^ ((k + 1) >> 2 & 1), y ^ ((k + 1) >> 1 & 1), c ^ ((k + 1) & 1)) for k in range(N_DEV - 1)]
        barrier = pltpu.get_barrier_semaphore()
        for p in peers:
            pl.semaphore_signal(barrier, inc=1, device_id=p, device_id_type=pl.DeviceIdType.MESH)
        pl.semaphore_wait(barrier, N_DEV - 1)

        def src_for(a, dest):
            return src_refs[a].at[dest] if plan[a][1] else src_refs[a]

        def slot(a, source):
            return land_refs[plan[a][0]].at[source]

        mine = [pltpu.make_async_copy(src_for(a, my), slot(a, my), local_sems.at[a]) for a in range(n)]
        for cp in mine:
            cp.start()
        sends, recvs = [], []
        for k, (px, py, pc) in enumerate(peers):
            peer = 4 * px + 2 * py + pc
            for a in range(n):
                kw = dict(send_sem=send_sems.at[a * (N_DEV - 1) + k], recv_sem=recv_sems.at[a * (N_DEV - 1) + k],
                          device_id=(px, py, pc), device_id_type=pl.DeviceIdType.MESH)
                sends.append(pltpu.make_async_remote_copy(src_ref=src_for(a, peer), dst_ref=slot(a, my), **kw))
                recvs.append(pltpu.make_async_remote_copy(src_ref=src_for(a, my), dst_ref=slot(a, peer), **kw))
        for cp in sends:
            cp.start()
        for cp in recvs:
            cp.wait_recv()
        for cp in sends:
            cp.wait_send()
        for cp in mine:
            cp.wait()

    lands = pl.kernel(
        launch, out_type=[_SDS(s, d) for s, d in land_shapes],
        mesh=plsc.ScalarSubcoreMesh(axis_name="sequencer", num_cores=1), name=name,
        scratch_types=(pltpu.SemaphoreType.DMA((n * (N_DEV - 1),)), pltpu.SemaphoreType.DMA((n * (N_DEV - 1),)),
                       pltpu.SemaphoreType.DMA((n,))),
        compiler_params=pltpu.CompilerParams(collective_id=cid))(*srcs)
    return _Exchange(list(lands), list(srcs))


def _seq_gather(srcs, name, cid):
    n = len(srcs)

    def launch(*refs):
        send_sems, recv_sems, local_sems = refs[2 * n:]
        x, y, c = _place()
        barrier = pltpu.get_barrier_semaphore()
        for k in range(1, N_DEV):
            pl.semaphore_signal(barrier, inc=1, device_id=(x ^ (k >> 2 & 1), y ^ (k >> 1 & 1), c ^ (k & 1)),
                                device_id_type=pl.DeviceIdType.MESH)
        pl.semaphore_wait(barrier, N_DEV - 1)
        at = lambda a, k: a * (N_DEV - 1) + k
        _two_level_gather(refs[:n], refs[n:2 * n], lambda a, k: dict(send_sem=send_sems.at[at(a, k)], recv_sem=recv_sems.at[at(a, k)]),
                          local_sems)

    lands = pl.kernel(
        launch, out_type=[_SDS((N_DEV,) + s.shape, s.dtype) for s in srcs],
        mesh=plsc.ScalarSubcoreMesh(axis_name="sequencer", num_cores=1), name=name,
        scratch_types=(pltpu.SemaphoreType.DMA((n * (N_DEV - 1),)), pltpu.SemaphoreType.DMA((n * (N_DEV - 1),)),
                       pltpu.SemaphoreType.DMA((n,))),
        compiler_params=pltpu.CompilerParams(collective_id=cid))(*srcs)
    return _Exchange(list(lands), list(srcs))


def _adam_update(g, w, m, v):
    c1 = 1.0 - ADAM_B1 ** ADAM_STEP
    c2 = 1.0 - ADAM_B2 ** ADAM_STEP
    mm = ADAM_B1 * m + (1.0 - ADAM_B1) * g
    vv = ADAM_B2 * v + (1.0 - ADAM_B2) * (g * g)
    delta = -ADAM_LR * ((mm / c1) / (jnp.sqrt(vv / c2) + ADAM_EPS) + ADAM_WD * w)
    return delta, mm, vv


def _sum_sources(p_ref):
    g = p_ref[0].astype(F32)
    for s in range(1, N_DEV):
        g = g + p_ref[s].astype(F32)
    return g


def _adamw(parts, w, m, v, tr, name, restore_b=False, deps=()):
    nl, r, c = w.shape
    cp = parts[0].shape[-1]

    def body(*refs):
        p_refs = refs[:nl]
        w_ref, m_ref, v_ref = refs[nl:nl + 3]
        g_ref, d_ref, nm_ref, nv_ref = refs[-4:]
        g = _sum_sources(p_refs[0])
        for l in range(1, nl):
            g = jnp.where(pl.program_id(0) == l, _sum_sources(p_refs[l]), g)
        if restore_b:
            g = jnp.concatenate([g[:, :BP_XQ], g[:, BP_GATE:BP_GATE + 2 * B_V_HEADS], g[:, BP_XQ:BP_GATE]], axis=1)
        delta, mm, vv = _adam_update(g, w_ref[...], m_ref[...], v_ref[...])
        g_ref[...] = g
        d_ref[...] = delta
        nm_ref[...] = mm
        nv_ref[...] = vv

    spec = pl.BlockSpec((None, tr, c), lambda l, i: (l, i, 0))
    part_specs = [pl.BlockSpec((N_DEV, tr, cp), functools.partial(lambda l, i, k: (0, jnp.where(l == k, i, 0), 0), k=k))
                  for k in range(nl)]
    return pl.pallas_call(
        body, grid=(nl, r // tr),
        in_specs=part_specs + [spec, spec, spec] + _dep_specs(deps),
        out_specs=[spec] * 4, out_shape=[_SDS(w.shape, F32)] * 4,
        name=name, compiler_params=_cp("arbitrary", "arbitrary"))(*parts, w, m, v, *deps)


def _pack_small(d_rel, d_cb, d_cw, d_qkv, d_mix, d_mem, d_ffn, d_final, d_sinks, d_par, d_ng, loss_row, name):
    flat = [d_rel, *d_cb, *d_cw, d_qkv, *d_mix, *d_mem, *d_ffn, d_final, d_sinks, d_par, d_ng, loss_row]
    n = len(flat)

    def body(*refs):
        ins, o_ref = refs[:n], refs[n]
        rel, cb0, cb1, cw0, cw1, qkv, mx0, mx1, me0, me1, ff0, ff1, fin, snk, par, ng, lss = ins
        o_ref[...] = jnp.zeros_like(o_ref)
        for k in range(N_BUCKETS):
            lane = SP_REL_LANE + 128 * (k % 8)
            o_ref[SP_QKV + k // 8:SP_QKV + k // 8 + 1, lane:lane + 128] = rel[k:k + 1, :]
        for l, (cb, cw) in enumerate(((cb0, cw0), (cb1, cw1))):
            o_ref[SP_CB + l:SP_CB + l + 1, :] = jnp.concatenate([cb[j] for j in range(FF_BLOCKS)], axis=1)
            full = jnp.concatenate([cw[j] for j in range(FF_BLOCKS)], axis=1)
            o_ref[SP_CW + FFN_CONV * l:SP_CW + FFN_CONV * (l + 1), :] = full[:FFN_CONV]
        o_ref[SP_QKV:SP_QKV + B_CONV, 0:B_QKV] = qkv[0:B_CONV, :]
        for base, pair in ((SP_MIX, (mx0, mx1)), (SP_MEM, (me0, me1)), (SP_FFN, (ff0, ff1))):
            for l in range(2):
                o_ref[base + l:base + l + 1, 0:D] = pair[l][...]
        o_ref[SP_FINAL:SP_FINAL + 1, 0:D] = fin[...]
        o_ref[SP_MISC:SP_MISC + 1, 0:128] = snk[...]
        o_ref[SP_MISC:SP_MISC + 1, 128:256] = par[...]
        o_ref[SP_MISC:SP_MISC + 1, 256:384] = ng[...]
        o_ref[SP_MISC:SP_MISC + 1, 384:512] = lss[...]

    vm = pl.BlockSpec(memory_space=pltpu.VMEM)
    return pl.pallas_call(body, in_specs=[vm] * n, out_specs=vm, out_shape=_SDS((SMALL_ROWS, D_FF), F32), name=name)(*flat)


_SMALL = ["rel_bias", "norm_mix_g", "norm_mem_g", "sinks_a", "a_log_b", "dt_bias_b", "out_norm_g_b", "norm_ffn_g",
          "ffn_conv_b", "final_norm_g", "conv_qkv_b", "ffn_conv_w"]


def _adamw_small(recv, rc_qkv, rc_ffn, ws, ms, vs, name, deps=()):
    n = len(_SMALL)

    def body(*refs):
        recv_ref, qkv_ref, ffn_ref = refs[:3]
        w_refs, m_refs, v_refs = refs[3:3 + n], refs[3 + n:3 + 2 * n], refs[3 + 2 * n:3 + 3 * n]
        outs, loss_ref = refs[len(refs) - 4 * n - 1:len(refs) - 1], refs[-1]
        gs = _sum_sources(recv_ref)
        loss_ref[...] = gs[SP_MISC:SP_MISC + 1, 384:512]
        grads = {
            "rel_bias": jnp.concatenate(
                [gs[SP_QKV + k // 8:SP_QKV + k // 8 + 1, SP_REL_LANE + 128 * (k % 8):SP_REL_LANE + 128 * (k % 8) + A_HEADS]
                 for k in range(N_BUCKETS)], axis=0),
            "norm_mix_g": gs[SP_MIX:SP_MIX + 2, 0:D], "norm_mem_g": gs[SP_MEM:SP_MEM + 2, 0:D],
            "sinks_a": gs[SP_MISC:SP_MISC + 1, 0:A_HEADS],
            "a_log_b": gs[SP_MISC:SP_MISC + 1, 128:128 + B_V_HEADS],
            "dt_bias_b": gs[SP_MISC:SP_MISC + 1, 128 + B_V_HEADS:128 + 2 * B_V_HEADS],
            "out_norm_g_b": gs[SP_MISC:SP_MISC + 1, 256:256 + B_HD],
            "norm_ffn_g": gs[SP_FFN:SP_FFN + 2, 0:D], "ffn_conv_b": gs[SP_CB:SP_CB + 2, :],
            "final_norm_g": gs[SP_FINAL:SP_FINAL + 1, 0:D],
            "conv_qkv_b": _sum_sources(qkv_ref), "ffn_conv_w": _sum_sources(ffn_ref),
        }
        for i, nm in enumerate(_SMALL):
            g = grads[nm]
            delta, mm, vv = _adam_update(g, w_refs[i][...], m_refs[i][...], v_refs[i][...])
            outs[i][...] = g
            outs[n + i][...] = delta
            outs[2 * n + i][...] = mm
            outs[3 * n + i][...] = vv

    vm = pl.BlockSpec(memory_space=pltpu.VMEM)
    shapes = [_SDS(w.shape, F32) for w in ws]
    return pl.pallas_call(
        body, in_specs=[vm] * (3 + 3 * n) + _dep_specs(deps), out_specs=[vm] * (4 * n + 1),
        out_shape=shapes * 4 + [_SDS((1, 128), F32)],
        name=name)(recv, rc_qkv, rc_ffn, *ws, *ms, *vs, *deps)


def _assemble(gathered, axis):
    g = jnp.moveaxis(gathered, 0, axis)
    shp = list(g.shape)
    return g.reshape(shp[:axis] + [shp[axis] * shp[axis + 1]] + shp[axis + 2:])


def _pad_rows(a, rows):
    return jnp.pad(a, ((0, rows - a.shape[0]), (0, 0)))


def _pad_lanes(a, lanes=128):
    return jnp.pad(a, ((0, 0), (0, lanes - a.shape[1])))


def _ff_blocks(a):
    return jnp.moveaxis(a.reshape(a.shape[0], FF_BLOCKS, GU_SHARD), 1, 0)


def _reorder_b(w):
    qkv_z = w[..., :B_QKV + B_V]
    gates = w[..., B_QKV + B_V:B_QKV + B_V + 2 * B_V_HEADS]
    xq = w[..., IN_B - X_Q:]
    pad = jnp.zeros(w.shape[:-1] + (IN_BP - IN_B,), w.dtype)
    return jnp.concatenate([qkv_z, xq, gates, pad], axis=-1)


def kernel(x, mem, rel_bias, norm_mix_g, norm_mem_g, w_mem_kv, w_out, w_in_a, sinks_a, w_in_b, conv_qkv_b, a_log_b, dt_bias_b, out_norm_g_b, norm_ffn_g, w_gate_up, ffn_conv_w, ffn_conv_b, w_down, final_norm_g, loss_target, m_rel_bias, m_norm_mix_g, m_norm_mem_g, m_w_mem_kv, m_w_out, m_w_in_a, m_sinks_a, m_w_in_b, m_conv_qkv_b, m_a_log_b, m_dt_bias_b, m_out_norm_g_b, m_norm_ffn_g, m_w_gate_up, m_ffn_conv_w, m_ffn_conv_b, m_w_down, m_final_norm_g, v_rel_bias, v_norm_mix_g, v_norm_mem_g, v_w_mem_kv, v_w_out, v_w_in_a, v_sinks_a, v_w_in_b, v_conv_qkv_b, v_a_log_b, v_dt_bias_b, v_out_norm_g_b, v_norm_ffn_g, v_w_gate_up, v_ffn_conv_w, v_ffn_conv_b, v_w_down, v_final_norm_g):
    local = dict(locals())
    order = ["rel_bias", "norm_mix_g", "norm_mem_g", "w_mem_kv", "w_out", "w_in_a", "sinks_a", "w_in_b", "conv_qkv_b",
             "a_log_b", "dt_bias_b", "out_norm_g_b", "norm_ffn_g", "w_gate_up", "ffn_conv_w", "ffn_conv_b", "w_down",
             "final_norm_g"]
    wts = {n: local[n] for n in order}
    moms = {n: local["m_" + n] for n in order}
    vars_ = {n: local["v_" + n] for n in order}
    h0 = x[0]
    memx = mem[0]
    tgt = loss_target[0]
    s = h0.shape[0]
    tm = _rows(s)
    tb = min(s, _TM_BIG)

    t_ = lambda a: jnp.swapaxes(a, 1, 2)
    def after(a, b):
        return a + (b[(0,) * b.ndim] * 0).astype(a.dtype)

    g_ia, = _seq_gather([t_(w_in_a).astype(_MXU)], "gather_first", 9).lands
    g_mk0, g_out0, g_cq, g_cw = _seq_gather(
        [after(a, g_ia) for a in (w_mem_kv[0:1].astype(_MXU), w_out[0:1].astype(_MXU), conv_qkv_b, ffn_conv_w)], "gather_second", 10).lands
    g_mk, g_out = [g_mk0], [g_out0]

    gu0_w = _seq_gather([after(t_(w_gate_up)[0].astype(_MXU), g_ia)], "gather_gate_up0", 1)
    dn0_w = _seq_gather([after(w_down[0].astype(_MXU), g_ia)], "gather_down0", 8)
    w_ia = g_ia.reshape(IN_A, D)
    conv_qkv = _pad_rows(_assemble(g_cq, 2)[0], HALO)
    ffn_cw_full = _assemble(g_cw, 2)
    ffn_cw = [_ff_blocks(_pad_rows(ffn_cw_full[i], HALO)) for i in range(2)]
    ffn_cb = [_ff_blocks(ffn_conv_b[i:i + 1]) for i in range(2)]
    bucket = jnp.asarray(_bucket_table())
    bias = _bias_build(rel_bias, bucket, "bias_build")
    sinks = _pad_lanes(sinks_a)
    par_b = _pad_lanes(jnp.concatenate([a_log_b, dt_bias_b], axis=1))

    row_x = pl.BlockSpec((tm, D), lambda i, j: (i, 0))
    gu_shape = (2, FF_BLOCKS, s, GU_SHARD)

    def in_proj(h, g, w, w_spec, n_cols, tn, name, deps=(), out_dtype=F32, w_t=False, tm=None):
        return _norm_matmul(h, g, w, w_spec, n_cols // tn, (h.shape[0], n_cols),
                            pl.BlockSpec((tm or _rows(h.shape[0]), tn), lambda i, j: (i, j)), name, deps=deps, out_dtype=out_dtype,
                            w_t=w_t, tm=tm)

    def ffn_fwd(i, h, g_gu, g_dn, deps=()):
        gu, hn = _norm_matmul(h, norm_ffn_g[i:i + 1], g_gu, _spec_gate_up(1), N_DEV, gu_shape,
                              _spec_gu_act(0, 1, tb), f"gate_up_{i}", deps=deps, out_dtype=_ACT, w_t=True, tm=tb)
        h_new, act, gc = _glu_down(gu, ffn_cw[i], ffn_cb[i], g_dn, h, f"glu_down_{i}")
        return h_new, gu, hn, (act, gc)

    def out_proj(i, mix, h):
        return _matmul_res(mix, row_x, g_out[i], _spec_rowsharded(0, D // N_DEV, D), 1, h, f"out_proj_{i}")

    proj_a, hn_a = in_proj(h0, norm_mix_g[0:1], w_ia, pl.BlockSpec((640, D), lambda i, j: (j, 0)), IN_A, 640, "in_proj_a",
                           deps=gu0_w.srcs + dn0_w.srcs, out_dtype=_ACT, w_t=True)
    memkv0, memn0 = in_proj(memx, norm_mem_g[0:1], g_mk[0], _spec_rowsharded(0, D // N_DEV, 2 * X_Q), 2 * X_Q, 2 * X_Q, "mem_proj_0")
    mix_a = _mix_a_fwd(proj_a, bias, sinks, memkv0, "mix_a_fwd")
    h1 = out_proj(0, mix_a, h0)
    g_gu0, g_dn0 = gu0_w.lands[0], dn0_w.lands[0]
    in_b_w = _seq_gather([after(_reorder_b(w_in_b).astype(_MXU), h1), after(w_mem_kv[1:2].astype(_MXU), h1)], "gather_in_b", 2)
    ffn1_w = _seq_gather([after(t_(w_gate_up)[1].astype(_MXU), h1), after(w_down[1].astype(_MXU), h1),
                          after(w_out[1:2].astype(_MXU), h1)], "gather_ffn1", 3)
    h2, gu0, hn_f0, act0 = ffn_fwd(0, h1, g_gu0, g_dn0, deps=in_b_w.srcs + ffn1_w.srcs)
    g_ib, g_mk1 = in_b_w.lands
    g_gu1, g_dn1, g_out1 = ffn1_w.lands
    g_mk.append(g_mk1)
    g_out.append(g_out1)
    proj_b, hn_b = in_proj(h2, norm_mix_g[1:2], g_ib, _spec_rowsharded(0, D // N_DEV, 896, col_block=1), IN_BP, 896, "in_proj_b")
    memkv1, memn1 = in_proj(memx, norm_mem_g[1:2], g_mk[1], _spec_rowsharded(0, D // N_DEV, 2 * X_Q), 2 * X_Q, 2 * X_Q, "mem_proj_1",
                            deps=[h2])
    mix_b, states, deltas = _mix_b_fwd(proj_b, conv_qkv, par_b, out_norm_g_b, memkv1, "mix_b_fwd")
    h3 = out_proj(1, mix_b, h2)
    h4, gu1, hn_f1, act1 = ffn_fwd(1, h3, g_gu1, g_dn1)
    loss_row, *dh, d_final_g = _loss_head(h4, final_norm_g[None, :], tgt, "loss_head")

    zeros_mem = jnp.zeros_like(memx)
    per_dest2 = [(0, True), (1, True)]

    def ffn_bwd(i, dh, h_in, gu, hn_f, act_gc, g_gu, g_dn, deps=()):
        act, gc = act_gc
        dgu, d_cw, d_cb = _glu_bwd(gu, gc, ffn_cw[i], dh[1], g_dn, f"glu_bwd_{i}", deps=deps)
        d_wdown = _matmul_tn(act, pl.BlockSpec((None, tb, GU_SHARD), lambda j, r: (j, r, 0)),
                             dh[1], pl.BlockSpec((tb, D), lambda j, r: (r, 0)), s, FF_BLOCKS, (GU_SHARD, D),
                             (N_DEV, DN_SHARD, D), pl.BlockSpec((2, DN_SHARD, D), lambda j, r: (j, 0, 0)), f"d_w_down_{i}",
                             tm=tb)
        *dh_new, d_g = _matmul_nt_normbwd(dgu, _spec_gu_act(0, 1, tm), g_gu, _spec_gate_up(1), N_DEV, h_in,
                                          norm_ffn_g[i:i + 1], dh[0], f"d_ffn_in_{i}", w_t=True, act_copy=True)
        d_wgu = _matmul_tn(dgu, _spec_gu_act(1, 0, tb), hn_f, pl.BlockSpec((tb, D), lambda j, r: (r, 0)), s, N_DEV,
                           (GU_SHARD, D), (N_DEV, GU_SHARD, D), pl.BlockSpec((None, GU_SHARD, D), lambda j, r: (j, 0, 0)),
                           f"d_w_gate_up_{i}", tm=tb)
        return dh_new, [d_wdown, d_wgu], d_cw, d_cb, d_g

    def out_bwd(i, dh, mix, deps):
        dmix = _matmul_nt(dh[1], g_out[i], _spec_rowsharded(0, D // N_DEV, D), 1, (s, D), row_x, f"d_mix_{i}", deps=deps, out_dtype=_ACT)
        d_wout = _matmul_tn(mix, pl.BlockSpec((tb, D), lambda j, r: (r, 0)), dh[1], pl.BlockSpec((tb, D), lambda j, r: (r, 0)),
                            s, 1, (D, D), (N_DEV, D // N_DEV, D), pl.BlockSpec((N_DEV, D // N_DEV, D), lambda j, r: (0, 0, 0)),
                            f"d_w_out_{i}", tm=tb)
        return dmix, d_wout

    def mem_bwd(i, dmemkv, memn):
        tmm = _rows(MEM_LEN)
        *_, d_g = _matmul_nt_normbwd(dmemkv, pl.BlockSpec((tmm, 2 * X_Q), lambda r, j: (r, 0)), g_mk[i],
                                     _spec_rowsharded(0, D // N_DEV, 2 * X_Q), 1, memx, norm_mem_g[i:i + 1], zeros_mem,
                                     f"d_mem_in_{i}")
        by_row = lambda j, r: (r, 0)
        d_w = _matmul_tn(memn, pl.BlockSpec((tmm, D), by_row), dmemkv, pl.BlockSpec((tmm, 2 * X_Q), by_row), MEM_LEN, 1,
                         (D, 2 * X_Q), (N_DEV, D // N_DEV, 2 * X_Q),
                         pl.BlockSpec((N_DEV, D // N_DEV, 2 * X_Q), lambda j, r: (0, 0, 0)), f"d_w_mem_kv_{i}")
        return d_w, d_g

    out_land = ((N_DEV, D // N_DEV, D), _WIRE)
    mk_land = ((N_DEV, D // N_DEV, 2 * X_Q), _WIRE)
    ffn_lands = [((N_DEV, DN_SHARD, D), _WIRE), ((N_DEV, GU_SHARD, D), _WIRE)]
    dh, d_ffn1, d_cw1, d_cb1, d_gf1 = ffn_bwd(1, dh, h3, gu1, hn_f1, act1, g_gu1, g_dn1)
    ffn1_g = _seq_exchange(d_ffn1, ffn_lands, per_dest2, "send_ffn1_grads", 5)
    dmix, d_wout1 = out_bwd(1, dh, mix_b, ffn1_g.srcs)
    dproj_b, d_convw, d_par, d_ng, dmemkv1 = _mix_b_bwd(proj_b, conv_qkv, par_b, out_norm_g_b, memkv1, states, deltas, dmix, "mix_b_bwd")
    *dh, d_gm1 = _matmul_nt_normbwd(dproj_b, pl.BlockSpec((tm, 896), lambda i, j: (i, j)), g_ib,
                                    _spec_rowsharded(0, D // N_DEV, 896, col_block=1), IN_BP // 896, h2, norm_mix_g[1:2], dh[0],
                                    "d_in_b", act_copy=True)
    d_wib = _matmul_tn(hn_b, pl.BlockSpec((tb, D), lambda j, r: (r, 0)), dproj_b, pl.BlockSpec((tb, 896), lambda j, r: (r, j)),
                       s, IN_BP // 896, (D, 896), (N_DEV, D // N_DEV, IN_BP),
                       pl.BlockSpec((N_DEV, D // N_DEV, 896), lambda j, r: (0, 0, j)), "d_w_in_b", tm=tb)
    d_wmk1, d_gmem1 = mem_bwd(1, dmemkv1, memn1)
    mix1_g = _seq_exchange([d_wout1, d_wib, d_wmk1], [out_land, ((N_DEV, D // N_DEV, IN_BP), _WIRE), mk_land],
                           [(0, True), (1, True), (2, True)], "send_mix1_grads", 6)
    dh, d_ffn0, d_cw0, d_cb0, d_gf0 = ffn_bwd(0, dh, h1, gu0, hn_f0, act0, g_gu0, g_dn0, deps=mix1_g.srcs)
    dmix, d_wout0 = out_bwd(0, dh, mix_a, d_ffn0 + ffn1_g.lands[:1])
    ffn0_g = _seq_exchange(d_ffn0 + [d_wout0], ffn_lands + [out_land], per_dest2 + [(2, True)], "send_ffn0_grads", 4)
    dproj_a, dbias, dsinks, dmemkv0 = _mix_a_bwd(proj_a, bias, sinks, memkv0, dmix, "mix_a_bwd", deps=ffn0_g.srcs)
    dx, _, d_gm0 = _matmul_nt_normbwd(dproj_a, pl.BlockSpec((tm, 640), lambda i, j: (i, j)), w_ia,
                                      pl.BlockSpec((640, D), lambda i, j: (j, 0)), IN_A // 640, h0, norm_mix_g[0:1], dh[0],
                                      "d_in_a", w_t=True)
    d_wia = _matmul_tn(dproj_a, pl.BlockSpec((tb, IN_A), lambda j, r: (r, 0)), hn_a, pl.BlockSpec((tb, D), lambda j, r: (r, 0)),
                       s, 1, (IN_A, D), (N_DEV, IA_SHARD, D), pl.BlockSpec((N_DEV, IA_SHARD, D), lambda j, r: (0, 0, 0)),
                       "d_w_in_a", tm=tb)
    d_wmk0, d_gmem0 = mem_bwd(0, dmemkv0, memn0)
    d_rel = _bias_reduce(dbias, bucket, "bias_reduce")
    small = _pack_small(d_rel, (d_cb0, d_cb1), (d_cw0, d_cw1), d_convw, (d_gm0, d_gm1), (d_gmem0, d_gmem1),
                        (d_gf0, d_gf1), d_final_g, dsinks, d_par, d_ng, loss_row, "pack_small")
    mix0_g = _seq_exchange([d_wia, d_wmk0, small],
                           [((N_DEV, IA_SHARD, D), _WIRE), mk_land, ((N_DEV, SMALL_ROWS, D_FF), F32)],
                           [(0, True), (1, True), (2, False)], "send_mix0_grads", 7)

    res = {}
    last = []

    def update(nm, parts, tr, restore=False, transposed=False):
        view = t_ if transposed else (lambda a: a)
        out = _adamw(parts, view(wts[nm]), view(moms[nm]), view(vars_[nm]), tr, "adamw_" + nm, restore_b=restore, deps=last[-1:])
        res[nm] = [view(o) for o in out]
        last.append(out[1])

    r_dn1, r_gu1 = ffn1_g.lands
    r_dn0, r_gu0, r_out0 = ffn0_g.lands
    r_out1, r_ib, r_mk1 = mix1_g.lands
    update("w_in_b", [r_ib], 32, True)
    update("w_gate_up", [r_gu0, r_gu1], 176, transposed=True)
    update("w_down", [r_dn0, r_dn1], 176)
    r_ia, r_mk0, r_small = mix0_g.lands
    update("w_mem_kv", [r_mk0, r_mk1], 128)
    update("w_out", [r_out0, r_out1], 128)
    update("w_in_a", [r_ia], IA_SHARD, transposed=True)

    my = 4 * lax.axis_index("x") + 2 * lax.axis_index("y") + lax.axis_index("c")
    cq = conv_qkv_b.shape[-1]
    cf = ffn_conv_w.shape[-1]
    rc_qkv = lax.dynamic_slice_in_dim(r_small[:, SP_QKV:SP_QKV + B_CONV, :B_QKV], my * cq, cq, axis=2)[:, None]
    rc_ffn = lax.dynamic_slice_in_dim(r_small[:, SP_CW:SP_CW + 2 * FFN_CONV, :], my * cf, cf, axis=2).reshape(N_DEV, 2, FFN_CONV, cf)
    as2d = lambda a: a[None, :] if a.ndim == 1 else a
    small_out = _adamw_small(r_small, rc_qkv, rc_ffn, [as2d(wts[n]) for n in _SMALL], [as2d(moms[n]) for n in _SMALL],
                             [as2d(vars_[n]) for n in _SMALL], "adamw_small", deps=last[-1:])
    ns = len(_SMALL)
    for i, nm in enumerate(_SMALL):
        res[nm] = [small_out[k * ns + i].reshape(wts[nm].shape) for k in range(4)]

    return (small_out[-1][0, 0], dx[None], *[res[n][0] for n in order], *[res[n][1] for n in order],
            *[res[n][2] for n in order], *[res[n][3] for n in order])
```
